```python
import math
import jax, jax.numpy as jnp
from jax import lax
import numpy as np

D_MODEL = 1024
BATCH = 8
SEQ = 8192
DEPTH = 4

N_MIXERS = 2
N_A = (DEPTH + 1) // 2
N_B = DEPTH // 2
NORM_EPS = 1e-6
D_RNN = D_MODEL
RG_HEADS = 8
RG_BW = D_RNN // RG_HEADS
RG_CONV_W = 4
RG_C = 8.0
RG_RAD_MIN = 0.9
RG_RAD_MAX = 0.999
D_S5 = D_MODEL
S5_GC = 16
S5_G = D_S5 // S5_GC
S5_P = 64
S5_DT_MIN = 0.001
S5_DT_MAX = 0.1
D_FF = 3 * D_MODEL
FFN_CONV_W = 3

kernel_name = "hybrid_rglru_s5_convffn_trunk"


def _rmsnorm(x, g):
    x32 = x.astype(jnp.float32)
    var = jnp.mean(x32 * x32, axis=-1, keepdims=True)
    return (x32 * lax.rsqrt(var + NORM_EPS) * g.astype(jnp.float32)).astype(x.dtype)


def _causal_dwconv(x, w, b):
    k_w = w.shape[0]
    s = x.shape[1]
    xp = jnp.pad(x, ((0, 0), (k_w - 1, 0), (0, 0)))
    out = b
    for k in range(k_w):
        out = out + xp[:, k:k + s, :] * w[k]
    return out


def _real_scan_combine(e1, e2):
    a1, b1 = e1
    a2, b2 = e2
    return a1 * a2, a2 * b1 + b2


def _complex_scan_combine(e1, e2):
    a1r, a1i, b1r, b1i = e1
    a2r, a2i, b2r, b2i = e2
    ar = a2r * a1r - a2i * a1i
    ai = a2r * a1i + a2i * a1r
    br = a2r * b1r - a2i * b1i + b2r
    bi = a2r * b1i + a2i * b1r + b2i
    return ar, ai, br, bi


def _rglru_mixer(h, w_in, conv_w, conv_b, w_a, b_a, w_x, b_x, lam, w_out):
    bsz, s, _ = h.shape
    xg = h @ w_in
    xr, gate = xg[..., :D_RNN], xg[..., D_RNN:]
    xr = _causal_dwconv(xr, conv_w, conv_b)
    xh = xr.reshape(bsz, s, RG_HEADS, RG_BW)
    r = jax.nn.sigmoid(jnp.einsum('bshi,hij->bshj', xh, w_a) + b_a).reshape(bsz, s, D_RNN)
    ig = jax.nn.sigmoid(jnp.einsum('bshi,hij->bshj', xh, w_x) + b_x).reshape(bsz, s, D_RNN)
    log_a = -RG_C * r.astype(jnp.float32) * jax.nn.softplus(-lam.astype(jnp.float32))
    a = jnp.exp(log_a)
    mult = jnp.sqrt(-jnp.expm1(2.0 * log_a))
    bterm = mult * (ig * xr).astype(jnp.float32)
    _, hs = lax.associative_scan(_real_scan_combine, (a, bterm), axis=1)
    y = hs.astype(h.dtype) * jax.nn.gelu(gate)
    return y @ w_out


def _s5_mixer(h, w_in, a_re, a_im, log_dt, b_re, b_im, c_re, c_im, d, w_glu, w_out):
    bsz, s, _ = h.shape
    u = h @ w_in
    ug = u.reshape(bsz, s, S5_G, S5_GC).astype(jnp.float32)
    ar = a_re.astype(jnp.float32)
    ai = a_im.astype(jnp.float32)
    dt = jnp.exp(log_dt.astype(jnp.float32))[:, None]
    mag = jnp.exp(ar * dt)
    abr = mag * jnp.cos(ai * dt)
    abi = mag * jnp.sin(ai * dt)
    ur, ui = abr - 1.0, abi
    den = ar * ar + ai * ai
    wr = (ur * ar + ui * ai) / den
    wi = (ui * ar - ur * ai) / den
    br32, bi32 = b_re.astype(jnp.float32), b_im.astype(jnp.float32)
    bbr = wr[..., None] * br32 - wi[..., None] * bi32
    bbi = wr[..., None] * bi32 + wi[..., None] * br32
    bu_r = jnp.einsum('bsgc,gpc->bsgp', ug, bbr)
    bu_i = jnp.einsum('bsgc,gpc->bsgp', ug, bbi)
    a_r = jnp.broadcast_to(abr, (1, s, S5_G, S5_P))
    a_i = jnp.broadcast_to(abi, (1, s, S5_G, S5_P))
    _, _, hr, hi = lax.associative_scan(_complex_scan_combine, (a_r, a_i, bu_r, bu_i), axis=1)
    y = (jnp.einsum('bsgp,gcp->bsgc', hr, c_re.astype(jnp.float32))
         - jnp.einsum('bsgp,gcp->bsgc', hi, c_im.astype(jnp.float32)))
    y = y.reshape(bsz, s, D_S5).astype(h.dtype) + d * u
    g = jax.nn.gelu(y)
    gl = g @ w_glu
    out = gl[..., :D_S5] * jax.nn.sigmoid(gl[..., D_S5:])
    return out @ w_out


def _conv_ffn(h, w_up, conv_w, conv_b, w_down):
    up = _causal_dwconv(h @ w_up, conv_w, conv_b)
    return (jax.nn.gelu(up[..., :D_FF]) * up[..., D_FF:]) @ w_down


def _fwd_setup_inputs(seed: int = 0) -> dict:
    key = jax.random.key(seed)
    ks = jax.random.split(key, 32)
    f32 = jnp.float32
    nrm = lambda k, shp, sc: jax.random.normal(k, shp, f32) * sc
    x = jax.random.normal(ks[0], (BATCH, SEQ, D_MODEL), f32)
    norm_mix_g = 1.0 + nrm(ks[1], (DEPTH, D_MODEL), 0.02)
    norm_ffn_g = 1.0 + nrm(ks[2], (DEPTH, D_MODEL), 0.02)
    norm_final_g = 1.0 + nrm(ks[3], (D_MODEL,), 0.02)
    rg_w_in = nrm(ks[4], (N_A, D_MODEL, 2 * D_RNN), D_MODEL ** -0.5)
    rg_conv_w = nrm(ks[5], (N_A, RG_CONV_W, D_RNN), RG_CONV_W ** -0.5)
    rg_conv_b = nrm(ks[6], (N_A, D_RNN), 0.01)
    rg_w_a = nrm(ks[7], (N_A, RG_HEADS, RG_BW, RG_BW), RG_BW ** -0.5)
    rg_b_a = nrm(ks[8], (N_A, RG_HEADS, RG_BW), 0.01)
    rg_w_x = nrm(ks[9], (N_A, RG_HEADS, RG_BW, RG_BW), RG_BW ** -0.5)
    rg_b_x = nrm(ks[10], (N_A, RG_HEADS, RG_BW), 0.01)
    a0 = jnp.sqrt(jax.random.uniform(ks[11], (N_A, D_RNN), f32,
                                     RG_RAD_MIN ** 2, RG_RAD_MAX ** 2))
    rg_lambda = jnp.log(a0) - jnp.log1p(-a0)
    rg_w_out = nrm(ks[12], (N_A, D_RNN, D_MODEL), D_RNN ** -0.5)
    s5_w_in = nrm(ks[13], (N_B, D_MODEL, D_S5), D_MODEL ** -0.5)
    s5_a_re = -0.5 + nrm(ks[14], (N_B, S5_G, S5_P), 0.01)
    s5_a_im = (math.pi * jnp.arange(S5_P, dtype=f32))[None, None, :] + nrm(ks[15], (N_B, S5_G, S5_P), 0.01)
    s5_log_dt = jax.random.uniform(ks[16], (N_B, S5_G), f32,
                                   math.log(S5_DT_MIN), math.log(S5_DT_MAX))
    s5_b_re = nrm(ks[17], (N_B, S5_G, S5_P, S5_GC), (2 * S5_GC) ** -0.5)
    s5_b_im = nrm(ks[18], (N_B, S5_G, S5_P, S5_GC), (2 * S5_GC) ** -0.5)
    s5_c_re = nrm(ks[19], (N_B, S5_G, S5_GC, S5_P), (0.5 * S5_P) ** -0.5)
    s5_c_im = nrm(ks[20], (N_B, S5_G, S5_GC, S5_P), (0.5 * S5_P) ** -0.5)
    s5_d = nrm(ks[21], (N_B, D_S5), 1.0)
    s5_w_glu = nrm(ks[22], (N_B, D_S5, 2 * D_S5), D_S5 ** -0.5)
    s5_w_out = nrm(ks[23], (N_B, D_S5, D_MODEL), D_S5 ** -0.5)
    ffn_w_up = nrm(ks[24], (DEPTH, D_MODEL, 2 * D_FF), D_MODEL ** -0.5)
    ffn_conv_w = nrm(ks[25], (DEPTH, FFN_CONV_W, 2 * D_FF), FFN_CONV_W ** -0.5)
    ffn_conv_b = nrm(ks[26], (DEPTH, 2 * D_FF), 0.01)
    ffn_w_down = nrm(ks[27], (DEPTH, D_FF, D_MODEL), D_FF ** -0.5)
    return {"x": x, "norm_mix_g": norm_mix_g, "norm_ffn_g": norm_ffn_g, "norm_final_g": norm_final_g,
            "rg_w_in": rg_w_in, "rg_conv_w": rg_conv_w, "rg_conv_b": rg_conv_b,
            "rg_w_a": rg_w_a, "rg_b_a": rg_b_a, "rg_w_x": rg_w_x, "rg_b_x": rg_b_x,
            "rg_lambda": rg_lambda, "rg_w_out": rg_w_out,
            "s5_w_in": s5_w_in, "s5_a_re": s5_a_re, "s5_a_im": s5_a_im, "s5_log_dt": s5_log_dt,
            "s5_b_re": s5_b_re, "s5_b_im": s5_b_im, "s5_c_re": s5_c_re, "s5_c_im": s5_c_im,
            "s5_d": s5_d, "s5_w_glu": s5_w_glu, "s5_w_out": s5_w_out,
            "ffn_w_up": ffn_w_up, "ffn_conv_w": ffn_conv_w, "ffn_conv_b": ffn_conv_b,
            "ffn_w_down": ffn_w_down}


def _fwd_reference(x, norm_mix_g, norm_ffn_g, norm_final_g,
              rg_w_in, rg_conv_w, rg_conv_b, rg_w_a, rg_b_a, rg_w_x, rg_b_x, rg_lambda, rg_w_out,
              s5_w_in, s5_a_re, s5_a_im, s5_log_dt, s5_b_re, s5_b_im, s5_c_re, s5_c_im,
              s5_d, s5_w_glu, s5_w_out,
              ffn_w_up, ffn_conv_w, ffn_conv_b, ffn_w_down):
    h = x
    for i in range(DEPTH):
        hn = _rmsnorm(h, norm_mix_g[i])
        j = i // N_MIXERS
        if i % N_MIXERS == 0:
            mix = _rglru_mixer(hn, rg_w_in[j], rg_conv_w[j], rg_conv_b[j], rg_w_a[j], rg_b_a[j],
                               rg_w_x[j], rg_b_x[j], rg_lambda[j], rg_w_out[j])
        else:
            mix = _s5_mixer(hn, s5_w_in[j], s5_a_re[j], s5_a_im[j], s5_log_dt[j], s5_b_re[j],
                            s5_b_im[j], s5_c_re[j], s5_c_im[j], s5_d[j], s5_w_glu[j], s5_w_out[j])
        h = h + mix.astype(h.dtype)
        hn = _rmsnorm(h, norm_ffn_g[i])
        h = h + _conv_ffn(hn, ffn_w_up[i], ffn_conv_w[i], ffn_conv_b[i], ffn_w_down[i]).astype(h.dtype)
    return _rmsnorm(h, norm_final_g)


import jax as _jax
import jax.numpy as _jnp

TWIN_FORMAT = 'train_step'
FWD_PARAMS = ['x', 'norm_mix_g', 'norm_ffn_g', 'norm_final_g', 'rg_w_in', 'rg_conv_w', 'rg_conv_b', 'rg_w_a', 'rg_b_a', 'rg_w_x', 'rg_b_x', 'rg_lambda', 'rg_w_out', 's5_w_in', 's5_a_re', 's5_a_im', 's5_log_dt', 's5_b_re', 's5_b_im', 's5_c_re', 's5_c_im', 's5_d', 's5_w_glu', 's5_w_out', 'ffn_w_up', 'ffn_conv_w', 'ffn_conv_b', 'ffn_w_down']
TWIN_WEIGHTS = ['norm_mix_g', 'norm_ffn_g', 'norm_final_g', 'rg_w_in', 'rg_conv_w', 'rg_conv_b', 'rg_w_a', 'rg_b_a', 'rg_w_x', 'rg_b_x', 'rg_lambda', 'rg_w_out', 's5_w_in', 's5_a_re', 's5_a_im', 's5_log_dt', 's5_b_re', 's5_b_im', 's5_c_re', 's5_c_im', 's5_d', 's5_w_glu', 's5_w_out', 'ffn_w_up', 'ffn_conv_w', 'ffn_conv_b', 'ffn_w_down']
TWIN_DIFF_INPUT = 'x'
TWIN_INPUTS = ['x', 'norm_mix_g', 'norm_ffn_g', 'norm_final_g', 'rg_w_in', 'rg_conv_w', 'rg_conv_b', 'rg_w_a', 'rg_b_a', 'rg_w_x', 'rg_b_x', 'rg_lambda', 'rg_w_out', 's5_w_in', 's5_a_re', 's5_a_im', 's5_log_dt', 's5_b_re', 's5_b_im', 's5_c_re', 's5_c_im', 's5_d', 's5_w_glu', 's5_w_out', 'ffn_w_up', 'ffn_conv_w', 'ffn_conv_b', 'ffn_w_down', 'loss_target', 'm_norm_mix_g', 'm_norm_ffn_g', 'm_norm_final_g', 'm_rg_w_in', 'm_rg_conv_w', 'm_rg_conv_b', 'm_rg_w_a', 'm_rg_b_a', 'm_rg_w_x', 'm_rg_b_x', 'm_rg_lambda', 'm_rg_w_out', 'm_s5_w_in', 'm_s5_a_re', 'm_s5_a_im', 'm_s5_log_dt', 'm_s5_b_re', 'm_s5_b_im', 'm_s5_c_re', 'm_s5_c_im', 'm_s5_d', 'm_s5_w_glu', 'm_s5_w_out', 'm_ffn_w_up', 'm_ffn_conv_w', 'm_ffn_conv_b', 'm_ffn_w_down', 'v_norm_mix_g', 'v_norm_ffn_g', 'v_norm_final_g', 'v_rg_w_in', 'v_rg_conv_w', 'v_rg_conv_b', 'v_rg_w_a', 'v_rg_b_a', 'v_rg_w_x', 'v_rg_b_x', 'v_rg_lambda', 'v_rg_w_out', 'v_s5_w_in', 'v_s5_a_re', 'v_s5_a_im', 'v_s5_log_dt', 'v_s5_b_re', 'v_s5_b_im', 'v_s5_c_re', 'v_s5_c_im', 'v_s5_d', 'v_s5_w_glu', 'v_s5_w_out', 'v_ffn_w_up', 'v_ffn_conv_w', 'v_ffn_conv_b', 'v_ffn_w_down']
TWIN_OUTPUTS = ['loss', 'grad_x', 'grad_norm_mix_g', 'grad_norm_ffn_g', 'grad_norm_final_g', 'grad_rg_w_in', 'grad_rg_conv_w', 'grad_rg_conv_b', 'grad_rg_w_a', 'grad_rg_b_a', 'grad_rg_w_x', 'grad_rg_b_x', 'grad_rg_lambda', 'grad_rg_w_out', 'grad_s5_w_in', 'grad_s5_a_re', 'grad_s5_a_im', 'grad_s5_log_dt', 'grad_s5_b_re', 'grad_s5_b_im', 'grad_s5_c_re', 'grad_s5_c_im', 'grad_s5_d', 'grad_s5_w_glu', 'grad_s5_w_out', 'grad_ffn_w_up', 'grad_ffn_conv_w', 'grad_ffn_conv_b', 'grad_ffn_w_down', 'delta_norm_mix_g', 'delta_norm_ffn_g', 'delta_norm_final_g', 'delta_rg_w_in', 'delta_rg_conv_w', 'delta_rg_conv_b', 'delta_rg_w_a', 'delta_rg_b_a', 'delta_rg_w_x', 'delta_rg_b_x', 'delta_rg_lambda', 'delta_rg_w_out', 'delta_s5_w_in', 'delta_s5_a_re', 'delta_s5_a_im', 'delta_s5_log_dt', 'delta_s5_b_re', 'delta_s5_b_im', 'delta_s5_c_re', 'delta_s5_c_im', 'delta_s5_d', 'delta_s5_w_glu', 'delta_s5_w_out', 'delta_ffn_w_up', 'delta_ffn_conv_w', 'delta_ffn_conv_b', 'delta_ffn_w_down', 'new_m_norm_mix_g', 'new_m_norm_ffn_g', 'new_m_norm_final_g', 'new_m_rg_w_in', 'new_m_rg_conv_w', 'new_m_rg_conv_b', 'new_m_rg_w_a', 'new_m_rg_b_a', 'new_m_rg_w_x', 'new_m_rg_b_x', 'new_m_rg_lambda', 'new_m_rg_w_out', 'new_m_s5_w_in', 'new_m_s5_a_re', 'new_m_s5_a_im', 'new_m_s5_log_dt', 'new_m_s5_b_re', 'new_m_s5_b_im', 'new_m_s5_c_re', 'new_m_s5_c_im', 'new_m_s5_d', 'new_m_s5_w_glu', 'new_m_s5_w_out', 'new_m_ffn_w_up', 'new_m_ffn_conv_w', 'new_m_ffn_conv_b', 'new_m_ffn_w_down', 'new_v_norm_mix_g', 'new_v_norm_ffn_g', 'new_v_norm_final_g', 'new_v_rg_w_in', 'new_v_rg_conv_w', 'new_v_rg_conv_b', 'new_v_rg_w_a', 'new_v_rg_b_a', 'new_v_rg_w_x', 'new_v_rg_b_x', 'new_v_rg_lambda', 'new_v_rg_w_out', 'new_v_s5_w_in', 'new_v_s5_a_re', 'new_v_s5_a_im', 'new_v_s5_log_dt', 'new_v_s5_b_re', 'new_v_s5_b_im', 'new_v_s5_c_re', 'new_v_s5_c_im', 'new_v_s5_d', 'new_v_s5_w_glu', 'new_v_s5_w_out', 'new_v_ffn_w_up', 'new_v_ffn_conv_w', 'new_v_ffn_conv_b', 'new_v_ffn_w_down']
TWIN_LEAF_KINDS = {'loss': 'loss', 'grad_x': 'grad_x', 'grad_norm_mix_g': 'grad_w', 'grad_norm_ffn_g': 'grad_w', 'grad_norm_final_g': 'grad_w', 'grad_rg_w_in': 'grad_w', 'grad_rg_conv_w': 'grad_w', 'grad_rg_conv_b': 'grad_w', 'grad_rg_w_a': 'grad_w', 'grad_rg_b_a': 'grad_w', 'grad_rg_w_x': 'grad_w', 'grad_rg_b_x': 'grad_w', 'grad_rg_lambda': 'grad_w', 'grad_rg_w_out': 'grad_w', 'grad_s5_w_in': 'grad_w', 'grad_s5_a_re': 'grad_w', 'grad_s5_a_im': 'grad_w', 'grad_s5_log_dt': 'grad_w', 'grad_s5_b_re': 'grad_w', 'grad_s5_b_im': 'grad_w', 'grad_s5_c_re': 'grad_w', 'grad_s5_c_im': 'grad_w', 'grad_s5_d': 'grad_w', 'grad_s5_w_glu': 'grad_w', 'grad_s5_w_out': 'grad_w', 'grad_ffn_w_up': 'grad_w', 'grad_ffn_conv_w': 'grad_w', 'grad_ffn_conv_b': 'grad_w', 'grad_ffn_w_down': 'grad_w', 'delta_norm_mix_g': 'delta_w', 'delta_norm_ffn_g': 'delta_w', 'delta_norm_final_g': 'delta_w', 'delta_rg_w_in': 'delta_w', 'delta_rg_conv_w': 'delta_w', 'delta_rg_conv_b': 'delta_w', 'delta_rg_w_a': 'delta_w', 'delta_rg_b_a': 'delta_w', 'delta_rg_w_x': 'delta_w', 'delta_rg_b_x': 'delta_w', 'delta_rg_lambda': 'delta_w', 'delta_rg_w_out': 'delta_w', 'delta_s5_w_in': 'delta_w', 'delta_s5_a_re': 'delta_w', 'delta_s5_a_im': 'delta_w', 'delta_s5_log_dt': 'delta_w', 'delta_s5_b_re': 'delta_w', 'delta_s5_b_im': 'delta_w', 'delta_s5_c_re': 'delta_w', 'delta_s5_c_im': 'delta_w', 'delta_s5_d': 'delta_w', 'delta_s5_w_glu': 'delta_w', 'delta_s5_w_out': 'delta_w', 'delta_ffn_w_up': 'delta_w', 'delta_ffn_conv_w': 'delta_w', 'delta_ffn_conv_b': 'delta_w', 'delta_ffn_w_down': 'delta_w', 'new_m_norm_mix_g': 'new_m', 'new_m_norm_ffn_g': 'new_m', 'new_m_norm_final_g': 'new_m', 'new_m_rg_w_in': 'new_m', 'new_m_rg_conv_w': 'new_m', 'new_m_rg_conv_b': 'new_m', 'new_m_rg_w_a': 'new_m', 'new_m_rg_b_a': 'new_m', 'new_m_rg_w_x': 'new_m', 'new_m_rg_b_x': 'new_m', 'new_m_rg_lambda': 'new_m', 'new_m_rg_w_out': 'new_m', 'new_m_s5_w_in': 'new_m', 'new_m_s5_a_re': 'new_m', 'new_m_s5_a_im': 'new_m', 'new_m_s5_log_dt': 'new_m', 'new_m_s5_b_re': 'new_m', 'new_m_s5_b_im': 'new_m', 'new_m_s5_c_re': 'new_m', 'new_m_s5_c_im': 'new_m', 'new_m_s5_d': 'new_m', 'new_m_s5_w_glu': 'new_m', 'new_m_s5_w_out': 'new_m', 'new_m_ffn_w_up': 'new_m', 'new_m_ffn_conv_w': 'new_m', 'new_m_ffn_conv_b': 'new_m', 'new_m_ffn_w_down': 'new_m', 'new_v_norm_mix_g': 'new_v', 'new_v_norm_ffn_g': 'new_v', 'new_v_norm_final_g': 'new_v', 'new_v_rg_w_in': 'new_v', 'new_v_rg_conv_w': 'new_v', 'new_v_rg_conv_b': 'new_v', 'new_v_rg_w_a': 'new_v', 'new_v_rg_b_a': 'new_v', 'new_v_rg_w_x': 'new_v', 'new_v_rg_b_x': 'new_v', 'new_v_rg_lambda': 'new_v', 'new_v_rg_w_out': 'new_v', 'new_v_s5_w_in': 'new_v', 'new_v_s5_a_re': 'new_v', 'new_v_s5_a_im': 'new_v', 'new_v_s5_log_dt': 'new_v', 'new_v_s5_b_re': 'new_v', 'new_v_s5_b_im': 'new_v', 'new_v_s5_c_re': 'new_v', 'new_v_s5_c_im': 'new_v', 'new_v_s5_d': 'new_v', 'new_v_s5_w_glu': 'new_v', 'new_v_s5_w_out': 'new_v', 'new_v_ffn_w_up': 'new_v', 'new_v_ffn_conv_w': 'new_v', 'new_v_ffn_conv_b': 'new_v', 'new_v_ffn_w_down': 'new_v'}


def _forward(args):
    return _fwd_reference(*[args[k] for k in FWD_PARAMS])


def _output_shape():
    def fwd():
        inp = _fwd_setup_inputs(0)
        return _fwd_reference(*[inp[k] for k in FWD_PARAMS])
    out = _jax.eval_shape(fwd)
    return out.shape, out.dtype

N_MICROBATCH = 1
ADAM_LR = 0.001
ADAM_B1 = 0.9
ADAM_B2 = 0.999
ADAM_EPS = 1e-08
ADAM_WD = 0.01
ADAM_STEP = 10
PER_EXAMPLE_BATCH_AXIS = {'x': 0, 'loss_target': 0}
SHARED_INPUTS = []
_WEIGHT_DTYPES = {'norm_mix_g': _jnp.float32, 'norm_ffn_g': _jnp.float32, 'norm_final_g': _jnp.float32, 'rg_w_in': _jnp.float32, 'rg_conv_w': _jnp.float32, 'rg_conv_b': _jnp.float32, 'rg_w_a': _jnp.float32, 'rg_b_a': _jnp.float32, 'rg_w_x': _jnp.float32, 'rg_b_x': _jnp.float32, 'rg_lambda': _jnp.float32, 'rg_w_out': _jnp.float32, 's5_w_in': _jnp.float32, 's5_a_re': _jnp.float32, 's5_a_im': _jnp.float32, 's5_log_dt': _jnp.float32, 's5_b_re': _jnp.float32, 's5_b_im': _jnp.float32, 's5_c_re': _jnp.float32, 's5_c_im': _jnp.float32, 's5_d': _jnp.float32, 's5_w_glu': _jnp.float32, 's5_w_out': _jnp.float32, 'ffn_w_up': _jnp.float32, 'ffn_conv_w': _jnp.float32, 'ffn_conv_b': _jnp.float32, 'ffn_w_down': _jnp.float32}
MOMENT_SCALE = {'norm_mix_g': 1.359954e-01, 'norm_ffn_g': 1.888998e-01, 'norm_final_g': 6.405413e+01, 'rg_w_in': 1.167704e-01, 'rg_conv_w': 1.208695e-01, 'rg_conv_b': 6.207006e-01, 'rg_w_a': 2.740639e-02, 'rg_b_a': 2.799869e-02, 'rg_w_x': 4.711592e-02, 'rg_b_x': 4.097976e-02, 'rg_lambda': 6.033894e-02, 'rg_w_out': 1.122435e-01, 's5_w_in': 8.190092e-02, 's5_a_re': 8.362824e-03, 's5_a_im': 8.602576e-03, 's5_log_dt': 3.936784e+00, 's5_b_re': 5.511679e-03, 's5_b_im': 5.379650e-03, 's5_c_re': 5.502761e-03, 's5_c_im': 5.560958e-03, 's5_d': 9.376679e-02, 's5_w_glu': 6.181603e-02, 's5_w_out': 8.530109e-02, 'ffn_w_up': 7.738297e-02, 'ffn_conv_w': 7.846319e-02, 'ffn_conv_b': 8.164870e-02, 'ffn_w_down': 1.320443e-01}


def _to_microbatches(a, axis):
    t = _jnp.moveaxis(a, axis, 0)
    t = t.reshape((N_MICROBATCH, t.shape[0] // N_MICROBATCH) + t.shape[1:])
    return _jnp.moveaxis(t, 1, axis + 1)


def setup_inputs(seed: int = 0) -> dict:
    inp = _fwd_setup_inputs(seed)
    key = _jax.random.fold_in(_jax.random.key(seed), 7919)
    shape, _ = _output_shape()
    out = dict(inp)
    out["loss_target"] = _jax.random.normal(_jax.random.fold_in(key, 0), shape, _jnp.float32)
    for i, name in enumerate(TWIN_WEIGHTS):
        w = inp[name].astype(_jnp.float32)
        if MOMENT_SCALE is None:
            s = _jnp.sqrt(_jnp.mean(_jnp.square(w)) + 1e-30)
        else:
            s = MOMENT_SCALE[name]
        km, kv = _jax.random.split(_jax.random.fold_in(key, i + 1))
        out[name] = w
        out["m_" + name] = s * _jax.random.normal(km, w.shape, _jnp.float32)
        out["v_" + name] = (s * s) * _jax.random.uniform(kv, w.shape, _jnp.float32, 0.5, 1.5)
    if N_MICROBATCH > 1:
        for name, axis in PER_EXAMPLE_BATCH_AXIS.items():
            out[name] = _to_microbatches(out[name], axis)
    return {'x': out['x'], 'norm_mix_g': out['norm_mix_g'], 'norm_ffn_g': out['norm_ffn_g'], 'norm_final_g': out['norm_final_g'], 'rg_w_in': out['rg_w_in'], 'rg_conv_w': out['rg_conv_w'], 'rg_conv_b': out['rg_conv_b'], 'rg_w_a': out['rg_w_a'], 'rg_b_a': out['rg_b_a'], 'rg_w_x': out['rg_w_x'], 'rg_b_x': out['rg_b_x'], 'rg_lambda': out['rg_lambda'], 'rg_w_out': out['rg_w_out'], 's5_w_in': out['s5_w_in'], 's5_a_re': out['s5_a_re'], 's5_a_im': out['s5_a_im'], 's5_log_dt': out['s5_log_dt'], 's5_b_re': out['s5_b_re'], 's5_b_im': out['s5_b_im'], 's5_c_re': out['s5_c_re'], 's5_c_im': out['s5_c_im'], 's5_d': out['s5_d'], 's5_w_glu': out['s5_w_glu'], 's5_w_out': out['s5_w_out'], 'ffn_w_up': out['ffn_w_up'], 'ffn_conv_w': out['ffn_conv_w'], 'ffn_conv_b': out['ffn_conv_b'], 'ffn_w_down': out['ffn_w_down'], 'loss_target': out['loss_target'], 'm_norm_mix_g': out['m_norm_mix_g'], 'm_norm_ffn_g': out['m_norm_ffn_g'], 'm_norm_final_g': out['m_norm_final_g'], 'm_rg_w_in': out['m_rg_w_in'], 'm_rg_conv_w': out['m_rg_conv_w'], 'm_rg_conv_b': out['m_rg_conv_b'], 'm_rg_w_a': out['m_rg_w_a'], 'm_rg_b_a': out['m_rg_b_a'], 'm_rg_w_x': out['m_rg_w_x'], 'm_rg_b_x': out['m_rg_b_x'], 'm_rg_lambda': out['m_rg_lambda'], 'm_rg_w_out': out['m_rg_w_out'], 'm_s5_w_in': out['m_s5_w_in'], 'm_s5_a_re': out['m_s5_a_re'], 'm_s5_a_im': out['m_s5_a_im'], 'm_s5_log_dt': out['m_s5_log_dt'], 'm_s5_b_re': out['m_s5_b_re'], 'm_s5_b_im': out['m_s5_b_im'], 'm_s5_c_re': out['m_s5_c_re'], 'm_s5_c_im': out['m_s5_c_im'], 'm_s5_d': out['m_s5_d'], 'm_s5_w_glu': out['m_s5_w_glu'], 'm_s5_w_out': out['m_s5_w_out'], 'm_ffn_w_up': out['m_ffn_w_up'], 'm_ffn_conv_w': out['m_ffn_conv_w'], 'm_ffn_conv_b': out['m_ffn_conv_b'], 'm_ffn_w_down': out['m_ffn_w_down'], 'v_norm_mix_g': out['v_norm_mix_g'], 'v_norm_ffn_g': out['v_norm_ffn_g'], 'v_norm_final_g': out['v_norm_final_g'], 'v_rg_w_in': out['v_rg_w_in'], 'v_rg_conv_w': out['v_rg_conv_w'], 'v_rg_conv_b': out['v_rg_conv_b'], 'v_rg_w_a': out['v_rg_w_a'], 'v_rg_b_a': out['v_rg_b_a'], 'v_rg_w_x': out['v_rg_w_x'], 'v_rg_b_x': out['v_rg_b_x'], 'v_rg_lambda': out['v_rg_lambda'], 'v_rg_w_out': out['v_rg_w_out'], 'v_s5_w_in': out['v_s5_w_in'], 'v_s5_a_re': out['v_s5_a_re'], 'v_s5_a_im': out['v_s5_a_im'], 'v_s5_log_dt': out['v_s5_log_dt'], 'v_s5_b_re': out['v_s5_b_re'], 'v_s5_b_im': out['v_s5_b_im'], 'v_s5_c_re': out['v_s5_c_re'], 'v_s5_c_im': out['v_s5_c_im'], 'v_s5_d': out['v_s5_d'], 'v_s5_w_glu': out['v_s5_w_glu'], 'v_s5_w_out': out['v_s5_w_out'], 'v_ffn_w_up': out['v_ffn_w_up'], 'v_ffn_conv_w': out['v_ffn_conv_w'], 'v_ffn_conv_b': out['v_ffn_conv_b'], 'v_ffn_w_down': out['v_ffn_w_down']}


def _loss(weights, diff, rest, loss_target):
    with _jax.named_scope("forward"):
        args = {**rest, TWIN_DIFF_INPUT: diff, **{k: w.astype(_WEIGHT_DTYPES[k]) for k, w in weights.items()}}
        y = _forward(args)
    with _jax.named_scope("loss_head"):
        err = _jnp.square(y.astype(_jnp.float32) - loss_target)
        return 0.5 * _jnp.sum(_jnp.mean(err, axis=-1)) if err.ndim else 0.5 * err


def _adamw(w, g, m, v):
    m = ADAM_B1 * m + (1.0 - ADAM_B1) * g
    v = ADAM_B2 * v + (1.0 - ADAM_B2) * _jnp.square(g)
    m_hat = m / (1.0 - ADAM_B1 ** ADAM_STEP)
    v_hat = v / (1.0 - ADAM_B2 ** ADAM_STEP)
    delta = -ADAM_LR * (m_hat / (_jnp.sqrt(v_hat) + ADAM_EPS) + ADAM_WD * w)
    return delta, m, v


def reference(x, norm_mix_g, norm_ffn_g, norm_final_g, rg_w_in, rg_conv_w, rg_conv_b, rg_w_a, rg_b_a, rg_w_x, rg_b_x, rg_lambda, rg_w_out, s5_w_in, s5_a_re, s5_a_im, s5_log_dt, s5_b_re, s5_b_im, s5_c_re, s5_c_im, s5_d, s5_w_glu, s5_w_out, ffn_w_up, ffn_conv_w, ffn_conv_b, ffn_w_down, loss_target, m_norm_mix_g, m_norm_ffn_g, m_norm_final_g, m_rg_w_in, m_rg_conv_w, m_rg_conv_b, m_rg_w_a, m_rg_b_a, m_rg_w_x, m_rg_b_x, m_rg_lambda, m_rg_w_out, m_s5_w_in, m_s5_a_re, m_s5_a_im, m_s5_log_dt, m_s5_b_re, m_s5_b_im, m_s5_c_re, m_s5_c_im, m_s5_d, m_s5_w_glu, m_s5_w_out, m_ffn_w_up, m_ffn_conv_w, m_ffn_conv_b, m_ffn_w_down, v_norm_mix_g, v_norm_ffn_g, v_norm_final_g, v_rg_w_in, v_rg_conv_w, v_rg_conv_b, v_rg_w_a, v_rg_b_a, v_rg_w_x, v_rg_b_x, v_rg_lambda, v_rg_w_out, v_s5_w_in, v_s5_a_re, v_s5_a_im, v_s5_log_dt, v_s5_b_re, v_s5_b_im, v_s5_c_re, v_s5_c_im, v_s5_d, v_s5_w_glu, v_s5_w_out, v_ffn_w_up, v_ffn_conv_w, v_ffn_conv_b, v_ffn_w_down):
    given = dict(x=x, norm_mix_g=norm_mix_g, norm_ffn_g=norm_ffn_g, norm_final_g=norm_final_g, rg_w_in=rg_w_in, rg_conv_w=rg_conv_w, rg_conv_b=rg_conv_b, rg_w_a=rg_w_a, rg_b_a=rg_b_a, rg_w_x=rg_w_x, rg_b_x=rg_b_x, rg_lambda=rg_lambda, rg_w_out=rg_w_out, s5_w_in=s5_w_in, s5_a_re=s5_a_re, s5_a_im=s5_a_im, s5_log_dt=s5_log_dt, s5_b_re=s5_b_re, s5_b_im=s5_b_im, s5_c_re=s5_c_re, s5_c_im=s5_c_im, s5_d=s5_d, s5_w_glu=s5_w_glu, s5_w_out=s5_w_out, ffn_w_up=ffn_w_up, ffn_conv_w=ffn_conv_w, ffn_conv_b=ffn_conv_b, ffn_w_down=ffn_w_down, loss_target=loss_target, m_norm_mix_g=m_norm_mix_g, m_norm_ffn_g=m_norm_ffn_g, m_norm_final_g=m_norm_final_g, m_rg_w_in=m_rg_w_in, m_rg_conv_w=m_rg_conv_w, m_rg_conv_b=m_rg_conv_b, m_rg_w_a=m_rg_w_a, m_rg_b_a=m_rg_b_a, m_rg_w_x=m_rg_w_x, m_rg_b_x=m_rg_b_x, m_rg_lambda=m_rg_lambda, m_rg_w_out=m_rg_w_out, m_s5_w_in=m_s5_w_in, m_s5_a_re=m_s5_a_re, m_s5_a_im=m_s5_a_im, m_s5_log_dt=m_s5_log_dt, m_s5_b_re=m_s5_b_re, m_s5_b_im=m_s5_b_im, m_s5_c_re=m_s5_c_re, m_s5_c_im=m_s5_c_im, m_s5_d=m_s5_d, m_s5_w_glu=m_s5_w_glu, m_s5_w_out=m_s5_w_out, m_ffn_w_up=m_ffn_w_up, m_ffn_conv_w=m_ffn_conv_w, m_ffn_conv_b=m_ffn_conv_b, m_ffn_w_down=m_ffn_w_down, v_norm_mix_g=v_norm_mix_g, v_norm_ffn_g=v_norm_ffn_g, v_norm_final_g=v_norm_final_g, v_rg_w_in=v_rg_w_in, v_rg_conv_w=v_rg_conv_w, v_rg_conv_b=v_rg_conv_b, v_rg_w_a=v_rg_w_a, v_rg_b_a=v_rg_b_a, v_rg_w_x=v_rg_w_x, v_rg_b_x=v_rg_b_x, v_rg_lambda=v_rg_lambda, v_rg_w_out=v_rg_w_out, v_s5_w_in=v_s5_w_in, v_s5_a_re=v_s5_a_re, v_s5_a_im=v_s5_a_im, v_s5_log_dt=v_s5_log_dt, v_s5_b_re=v_s5_b_re, v_s5_b_im=v_s5_b_im, v_s5_c_re=v_s5_c_re, v_s5_c_im=v_s5_c_im, v_s5_d=v_s5_d, v_s5_w_glu=v_s5_w_glu, v_s5_w_out=v_s5_w_out, v_ffn_w_up=v_ffn_w_up, v_ffn_conv_w=v_ffn_conv_w, v_ffn_conv_b=v_ffn_conv_b, v_ffn_w_down=v_ffn_w_down)
    weights = {n: given[n] for n in TWIN_WEIGHTS}
    shared = {n: given[n] for n in SHARED_INPUTS}
    per_example = {n: given[n] for n in ['x']}
    grad_fn = _jax.value_and_grad(_loss, argnums=(0, 1))

    def one_microbatch(ex, loss_target):
        ex = dict(ex)
        diff = ex.pop(TWIN_DIFF_INPUT)
        return grad_fn(weights, diff, {**shared, **ex}, loss_target)

    if N_MICROBATCH == 1:
        loss, (grad_w, grad_x) = one_microbatch(per_example, given["loss_target"])
    else:
        def body(carry, xs):
            loss_sum, grad_sum = carry
            l_k, (gw_k, gx_k) = one_microbatch(xs[0], xs[1])
            with _jax.named_scope("update"):
                return (loss_sum + l_k, _jax.tree.map(_jnp.add, grad_sum, gw_k)), gx_k

        init = (_jnp.zeros((), _jnp.float32), _jax.tree.map(_jnp.zeros_like, weights))
        (loss, grad_w), grad_x = _jax.lax.scan(body, init, (per_example, given["loss_target"]))
    with _jax.named_scope("update"):
        delta_w, new_m, new_v = {}, {}, {}
        for n in TWIN_WEIGHTS:
            delta_w[n], new_m[n], new_v[n] = _adamw(weights[n], grad_w[n], given["m_" + n], given["v_" + n])
    return (loss, grad_x, *[grad_w[n] for n in TWIN_WEIGHTS], *[delta_w[n] for n in TWIN_WEIGHTS],
            *[new_m[n] for n in TWIN_WEIGHTS], *[new_v[n] for n in TWIN_WEIGHTS])
```

```python
import functools
import math

import jax
import jax.numpy as jnp
from jax import lax
from jax.experimental import pallas as pl
from jax.experimental.pallas import tpu as pltpu

F32 = jnp.float32
BF16 = jnp.bfloat16
MESH = pl.DeviceIdType.MESH

NORM_EPS = 1e-6
RG_HEADS = 8
RG_CONV_W = 4
RG_C = 8.0
S5_GC = 16
S5_P = 64
S5_GROUPS_PER_BLOCK = 8
FFN_CONV_W = 3
N_CHIPS = 4
ADAM_LR, ADAM_B1, ADAM_B2, ADAM_EPS, ADAM_WD, ADAM_STEP = 0.001, 0.9, 0.999, 1e-08, 0.01, 10
VMEM_LIMIT_BYTES = 56 * 1024 * 1024
SUBLANES = 8
LANES = 128

PARAM_NAMES = ['norm_mix_g', 'norm_ffn_g', 'norm_final_g', 'rg_w_in', 'rg_conv_w', 'rg_conv_b', 'rg_w_a', 'rg_b_a', 'rg_w_x',
               'rg_b_x', 'rg_lambda', 'rg_w_out', 's5_w_in', 's5_a_re', 's5_a_im', 's5_log_dt', 's5_b_re', 's5_b_im', 's5_c_re',
               's5_c_im', 's5_d', 's5_w_glu', 's5_w_out', 'ffn_w_up', 'ffn_conv_w', 'ffn_conv_b', 'ffn_w_down']
SHARDED = ['rg_w_in', 'rg_conv_w', 'rg_w_out', 's5_w_in', 's5_d', 's5_w_glu', 's5_w_out', 'ffn_w_up', 'ffn_conv_w', 'ffn_w_down']
BIG = ['rg_w_in', 'rg_w_out', 's5_w_in', 's5_w_glu', 's5_w_out', 'ffn_w_up', 'ffn_w_down']
REPLICATED = [n for n in PARAM_NAMES if n not in SHARDED]


def _cparams():
    return pltpu.CompilerParams(vmem_limit_bytes=VMEM_LIMIT_BYTES)


_GELU_C = math.sqrt(2.0 / math.pi)
_GELU_K = 0.044715


def _gelu(x):
    return 0.5 * x * (1.0 + jnp.tanh(_GELU_C * (x + _GELU_K * x * x * x)))


def _gelu_and_grad(x):
    t = jnp.tanh(_GELU_C * (x + _GELU_K * x * x * x))
    g = 0.5 * x * (1.0 + t)
    dg = 0.5 * (1.0 + t) + 0.5 * x * (1.0 - t * t) * (_GELU_C * (1.0 + 3.0 * _GELU_K * x * x))
    return g, dg


def _sigmoid(x):
    return jax.nn.sigmoid(x)


def _neg_expm1(x):
    series = -(x * (1.0 + x * (0.5 + x * (1.0 / 6 + x * (1.0 / 24 + x * (1.0 / 120 + x * (1.0 / 720)))))))
    return jnp.where(x > -0.25, series, 1.0 - jnp.exp(x))


def _softplus(z):
    return jnp.maximum(z, 0.0) + jnp.log1p(jnp.exp(-jnp.abs(z)))


def _rows(shape):
    return lax.broadcasted_iota(jnp.int32, shape, 0)


def _shift_down(x, halo, k):
    ext = jnp.concatenate([halo, x], axis=0)
    return pltpu.roll(ext, k, 0)[SUBLANES:]


def _shift_up(x, halo, k):
    ext = jnp.concatenate([x, halo], axis=0)
    n = ext.shape[0]
    return pltpu.roll(ext, n - k, 0)[:x.shape[0]]


def _scan_real_fwd(a, b):
    n = a.shape[0]
    row = _rows(a.shape)
    sh = 1
    while sh < n:
        ok = row >= sh
        b = a * jnp.where(ok, pltpu.roll(b, sh, 0), 0.0) + b
        if sh * 2 < n:
            a = a * jnp.where(ok, pltpu.roll(a, sh, 0), 1.0)
        sh *= 2
    return b


def _scan_real_rev(c, d):
    n = c.shape[0]
    row = _rows(c.shape)
    sh = 1
    while sh < n:
        ok = row < n - sh
        d = c * jnp.where(ok, pltpu.roll(d, n - sh, 0), 0.0) + d
        if sh * 2 < n:
            c = c * jnp.where(ok, pltpu.roll(c, n - sh, 0), 1.0)
        sh *= 2
    return d


def _scan_cplx(br, bi, pr_ref, pi_ref, reverse):
    n = br.shape[0]
    row = _rows(br.shape)
    sh, k = 1, 0
    while sh < n:
        pr = pr_ref[k:k + 1, :]
        pi = pi_ref[k:k + 1, :]
        if reverse:
            ok = row < n - sh
            sr = jnp.where(ok, pltpu.roll(br, n - sh, 0), 0.0)
            si = jnp.where(ok, pltpu.roll(bi, n - sh, 0), 0.0)
        else:
            ok = row >= sh
            sr = jnp.where(ok, pltpu.roll(br, sh, 0), 0.0)
            si = jnp.where(ok, pltpu.roll(bi, sh, 0), 0.0)
        br, bi = br + pr * sr - pi * si, bi + pr * si + pi * sr
        sh *= 2
        k += 1
    return br, bi


class Mat:
    def __init__(self, arr, l=0, split='c'):
        assert arr.ndim == 4
        self.arr, self.l, self.split = arr, l, split
        p, _, r, c = arr.shape
        self.shape = (r, c * p) if split == 'c' else (r * p, c)

    def spec(self, tr, tc, rc):
        p, _, r, c = self.arr.shape
        l = self.l
        assert r % tr == 0 and c % tc == 0, (self.arr.shape, tr, tc)
        if self.split == 'c':
            per = c // tc
            return pl.BlockSpec((None, None, tr, tc), lambda i, j, k: (rc(i, j, k)[1] // per, l, rc(i, j, k)[0], rc(i, j, k)[1] % per))
        per = r // tr
        return pl.BlockSpec((None, None, tr, tc), lambda i, j, k: (rc(i, j, k)[0] // per, l, rc(i, j, k)[0] % per, rc(i, j, k)[1]))


def act(x, parts=1):
    s, c = x.shape
    return Mat(x.reshape(s, parts, c // parts).transpose(1, 0, 2)[:, None] if parts > 1 else x[None, None])


def _mm(name, mode, a, b, *, out_parts=1, out_split='c', out_dtype=F32, res=None, tm=512, tn=512, tk=512):
    if mode == 'nn':
        (m, kk), (kb, n) = a.shape, b.shape
    elif mode == 'nt':
        (m, kk), (n, kb) = a.shape, b.shape
    else:
        (kk, m), (kb, n) = a.shape, b.shape
    assert kk == kb, (name, a.shape, b.shape)
    tm, tn, tk = min(tm, m), min(tn, n), min(tk, kk)
    assert m % tm == 0 and n % tn == 0 and kk % tk == 0, (name, m, n, kk, tm, tn, tk)
    nk = kk // tk
    if mode == 'nn':
        a_spec = a.spec(tm, tk, lambda i, j, k: (i, k))
        b_spec = b.spec(tk, tn, lambda i, j, k: (k, j))
        dims = (((1,), (0,)), ((), ()))
    elif mode == 'nt':
        a_spec = a.spec(tm, tk, lambda i, j, k: (i, k))
        b_spec = b.spec(tn, tk, lambda i, j, k: (j, k))
        dims = (((1,), (1,)), ((), ()))
    else:
        a_spec = a.spec(tk, tm, lambda i, j, k: (k, i))
        b_spec = b.spec(tk, tn, lambda i, j, k: (k, j))
        dims = (((0,), (0,)), ((), ()))
    if out_split == 'c':
        out_arr = jax.ShapeDtypeStruct((out_parts, 1, m, n // out_parts), out_dtype)
    else:
        out_arr = jax.ShapeDtypeStruct((out_parts, 1, m // out_parts, n), out_dtype)
    out_mat = Mat(out_arr, 0, out_split)
    o_spec = out_mat.spec(tm, tn, lambda i, j, k: (i, j))
    has_res = res is not None

    def body(*refs):
        if has_res:
            a_ref, b_ref, r_ref, o_ref = refs[:4]
        else:
            a_ref, b_ref, o_ref = refs[:3]
        prod = lax.dot_general(a_ref[...].astype(BF16), b_ref[...].astype(BF16), dims, preferred_element_type=F32)

        def finish(acc):
            if has_res:
                acc = acc + r_ref[...]
            o_ref[...] = acc.astype(out_dtype)

        if nk == 1:
            finish(prod)
        else:
            acc_ref = refs[-1]
            k = pl.program_id(2)

            @pl.when(k == 0)
            def _():
                acc_ref[...] = prod

            @pl.when(k > 0)
            def _():
                acc_ref[...] += prod

            @pl.when(k == nk - 1)
            def _():
                finish(acc_ref[...])

    in_specs = [a_spec, b_spec]
    args = [a.arr, b.arr]
    if has_res:
        in_specs.append(res.spec(tm, tn, lambda i, j, k: (i, j)))
        args.append(res.arr)
    out = pl.pallas_call(
        body, name=name, grid=(m // tm, n // tn, nk), in_specs=in_specs, out_specs=o_spec, out_shape=out_arr,
        scratch_shapes=[pltpu.VMEM((tm, tn), F32)] if nk > 1 else [], compiler_params=_cparams(),
    )(*args)
    return Mat(out, 0, out_split)


def _rms_fwd(h, g, ts=512):
    s, d = h.shape
    ts = min(ts, s)

    def body(h_ref, g_ref, o_ref):
        x = h_ref[...]
        var = jnp.mean(x * x, axis=-1, keepdims=True)
        o_ref[...] = (x * lax.rsqrt(var + NORM_EPS) * g_ref[...]).astype(BF16)

    return pl.pallas_call(
        body, name="rms_fwd", grid=(s // ts,),
        in_specs=[pl.BlockSpec((ts, d), lambda i: (i, 0)), pl.BlockSpec((1, d), lambda i: (0, 0))],
        out_specs=pl.BlockSpec((ts, d), lambda i: (i, 0)), out_shape=jax.ShapeDtypeStruct((s, d), BF16),
        compiler_params=_cparams(),
    )(h, g)


def _rms_bwd(h, g, dhn, dh_in, ts=512):
    s, d = h.shape
    ts = min(ts, s)

    def body(h_ref, g_ref, dhn_ref, dhin_ref, dh_ref, dg_ref):
        i = pl.program_id(0)
        x = h_ref[...]
        rstd = lax.rsqrt(jnp.mean(x * x, axis=-1, keepdims=True) + NORM_EPS)
        xhat = x * rstd
        dhn_v = dhn_ref[...]
        dxh = dhn_v * g_ref[...]
        dh_ref[...] = dhin_ref[...] + rstd * (dxh - xhat * jnp.mean(dxh * xhat, axis=-1, keepdims=True))
        part = jnp.sum(dhn_v * xhat, axis=0, keepdims=True)

        @pl.when(i == 0)
        def _():
            dg_ref[...] = part

        @pl.when(i > 0)
        def _():
            dg_ref[...] += part

    row = pl.BlockSpec((ts, d), lambda i: (i, 0))
    vec = pl.BlockSpec((1, d), lambda i: (0, 0))
    return pl.pallas_call(
        body, name="rms_bwd", grid=(s // ts,), in_specs=[row, vec, row, row], out_specs=[row, vec],
        out_shape=[jax.ShapeDtypeStruct((s, d), F32), jax.ShapeDtypeStruct((1, d), F32)], compiler_params=_cparams(),
    )(h, g, dhn, dh_in)


def _loss_and_grad(h, g, tgt, ts=512):
    s, d = h.shape
    ts = min(ts, s)

    def body(h_ref, g_ref, t_ref, loss_ref, dh_ref, dg_ref):
        i = pl.program_id(0)
        x = h_ref[...]
        gv = g_ref[...]
        rstd = lax.rsqrt(jnp.mean(x * x, axis=-1, keepdims=True) + NORM_EPS)
        xhat = x * rstd
        err = xhat * gv - t_ref[...]
        dy = err * (1.0 / d)
        dxh = dy * gv
        dh_ref[...] = rstd * (dxh - xhat * jnp.mean(dxh * xhat, axis=-1, keepdims=True))
        part = jnp.sum(dy * xhat, axis=0, keepdims=True)
        lpart = jnp.broadcast_to(jnp.sum(jnp.sum(err * err, axis=0, keepdims=True), axis=1, keepdims=True) * (0.5 / d), (1, LANES))

        @pl.when(i == 0)
        def _():
            dg_ref[...] = part
            loss_ref[...] = lpart

        @pl.when(i > 0)
        def _():
            dg_ref[...] += part
            loss_ref[...] += lpart

    row = pl.BlockSpec((ts, d), lambda i: (i, 0))
    vec = pl.BlockSpec((1, d), lambda i: (0, 0))
    return pl.pallas_call(
        body, name="loss_and_grad", grid=(s // ts,), in_specs=[row, vec, row],
        out_specs=[pl.BlockSpec((1, LANES), lambda i: (0, 0)), row, vec],
        out_shape=[jax.ShapeDtypeStruct((1, LANES), F32), jax.ShapeDtypeStruct((s, d), F32), jax.ShapeDtypeStruct((1, d), F32)],
        compiler_params=_cparams(),
    )(h, g, tgt)


def _halo_before(ts, nrow8):
    return lambda i: jnp.maximum(i * (ts // SUBLANES) - 1, 0)


def _ffn_act(up2, conv_w2, conv_b2, ts=512, tn=512):
    _, s, f = up2.shape
    ts, tn = min(ts, s), min(tn, f)
    kw = FFN_CONV_W

    def body(up_ref, halo_ref, w_ref, b_ref, o_ref):
        i = pl.program_id(0)
        cs = []
        for h in range(2):
            x = up_ref[h]
            halo = jnp.where(i == 0, 0.0, halo_ref[h])
            c = b_ref[h] + w_ref[h, kw - 1:kw, :] * x
            for sft in range(1, kw):
                c = c + w_ref[h, kw - 1 - sft:kw - sft, :] * _shift_down(x, halo, sft)
            cs.append(c)
        o_ref[...] = (_gelu(cs[0]) * cs[1]).astype(BF16)

    hb = ts // SUBLANES
    return pl.pallas_call(
        body, name="ffn_act", grid=(s // ts, f // tn),
        in_specs=[pl.BlockSpec((2, ts, tn), lambda i, j: (0, i, j)),
                  pl.BlockSpec((2, SUBLANES, tn), lambda i, j: (0, jnp.maximum(i * hb - 1, 0), j)),
                  pl.BlockSpec((2, kw, tn), lambda i, j: (0, 0, j)),
                  pl.BlockSpec((2, 1, tn), lambda i, j: (0, 0, j))],
        out_specs=pl.BlockSpec((ts, tn), lambda i, j: (i, j)), out_shape=jax.ShapeDtypeStruct((s, f), BF16),
        compiler_params=_cparams(),
    )(up2, up2, conv_w2, conv_b2)


def _ffn_bwd(up2, dact, conv_w2, conv_b2, ts=256, tn=512):
    _, s, f = up2.shape
    ts, tn = min(ts, s), min(tn, f)
    kw = FFN_CONV_W
    nt = s // ts
    hb = ts // SUBLANES
    last8 = s // SUBLANES - 1

    def body(up_ref, hb_ref, ha_ref, da_ref, dah_ref, w_ref, b_ref, dup_ref, dw_ref, db_ref):
        i = pl.program_id(1)
        first, last = i == 0, i == nt - 1
        ce, xs = [], []
        for h in range(2):
            x = up_ref[h]
            before = jnp.where(first, 0.0, hb_ref[h])
            after = ha_ref[h]
            ext = jnp.concatenate([before, x, after], axis=0)
            c = b_ref[h] + w_ref[h, kw - 1:kw, :] * ext
            shifted = [ext]
            for sft in range(1, kw):
                sh = pltpu.roll(ext, sft, 0)
                shifted.append(sh)
                c = c + w_ref[h, kw - 1 - sft:kw - sft, :] * sh
            ce.append(c[SUBLANES:])
            xs.append([sh[SUBLANES:SUBLANES + ts] for sh in shifted])
        da = jnp.concatenate([da_ref[...], jnp.where(last, 0.0, dah_ref[...])], axis=0)
        g1, dg1 = _gelu_and_grad(ce[0])
        dcs = [da * ce[1] * dg1, da * g1]
        for h in range(2):
            dc = dcs[h]
            n = dc.shape[0]
            dup = w_ref[h, kw - 1:kw, :] * dc[:ts]
            for sft in range(1, kw):
                dup = dup + w_ref[h, kw - 1 - sft:kw - sft, :] * pltpu.roll(dc, n - sft, 0)[:ts]
            dup_ref[h] = dup.astype(BF16)
            dct = dc[:ts]
            dbp = jnp.sum(dct, axis=0, keepdims=True)
            dwp = [jnp.sum(dct * xs[h][kw - 1 - k], axis=0, keepdims=True) for k in range(kw)]

            @pl.when(first)
            def _():
                db_ref[h] = dbp
                for k in range(kw):
                    dw_ref[h, k:k + 1, :] = dwp[k]

            @pl.when(i > 0)
            def _():
                db_ref[h] += dbp
                for k in range(kw):
                    dw_ref[h, k:k + 1, :] += dwp[k]

    return pl.pallas_call(
        body, name="ffn_bwd", grid=(f // tn, nt),
        in_specs=[pl.BlockSpec((2, ts, tn), lambda j, i: (0, i, j)),
                  pl.BlockSpec((2, SUBLANES, tn), lambda j, i: (0, jnp.maximum(i * hb - 1, 0), j)),
                  pl.BlockSpec((2, SUBLANES, tn), lambda j, i: (0, jnp.minimum((i + 1) * hb, last8), j)),
                  pl.BlockSpec((ts, tn), lambda j, i: (i, j)),
                  pl.BlockSpec((SUBLANES, tn), lambda j, i: (jnp.minimum((i + 1) * hb, last8), j)),
                  pl.BlockSpec((2, kw, tn), lambda j, i: (0, 0, j)),
                  pl.BlockSpec((2, 1, tn), lambda j, i: (0, 0, j))],
        out_specs=[pl.BlockSpec((2, ts, tn), lambda j, i: (0, i, j)),
                   pl.BlockSpec((2, kw, tn), lambda j, i: (0, 0, j)),
                   pl.BlockSpec((2, 1, tn), lambda j, i: (0, 0, j))],
        out_shape=[jax.ShapeDtypeStruct((2, s, f), BF16), jax.ShapeDtypeStruct((2, kw, f), F32),
                   jax.ShapeDtypeStruct((2, 1, f), F32)],
        compiler_params=_cparams(),
    )(up2, up2, up2, dact, dact, conv_w2, conv_b2)


def _rg_gates(xr, wa_ref, ba_ref, wx_ref, bx_ref, lam_ref):
    bw = wa_ref.shape[-1]
    xb = xr.astype(BF16)
    za = jnp.concatenate([jnp.dot(xb[:, h * bw:(h + 1) * bw], wa_ref[h], preferred_element_type=F32)
                          for h in range(RG_HEADS)], axis=1) + ba_ref[...]
    zx = jnp.concatenate([jnp.dot(xb[:, h * bw:(h + 1) * bw], wx_ref[h], preferred_element_type=F32)
                          for h in range(RG_HEADS)], axis=1) + bx_ref[...]
    r, ig = _sigmoid(za), _sigmoid(zx)
    sp = _softplus(-lam_ref[...])
    la = -RG_C * r * sp
    a = jnp.exp(la)
    mult = jnp.sqrt(_neg_expm1(2.0 * la))
    return xb, r, ig, sp, a, mult


def _rg_fwd(xg2, conv_w, conv_b, w_a, b_a, w_x, b_x, lam, ts=256):
    _, s, c = xg2.shape
    ts = min(ts, s)
    kw = RG_CONV_W
    hb = ts // SUBLANES

    def body(xg_ref, halo_ref, cw_ref, cb_ref, wa_ref, ba_ref, wx_ref, bx_ref, lam_ref, xr_ref, hs_ref, y_ref, carry_ref):
        i = pl.program_id(0)

        @pl.when(i == 0)
        def _():
            carry_ref[...] = jnp.zeros_like(carry_ref)

        xp = xg_ref[0]
        halo = jnp.where(i == 0, 0.0, halo_ref[...])
        xr = cb_ref[...] + cw_ref[kw - 1:kw, :] * xp
        for sft in range(1, kw):
            xr = xr + cw_ref[kw - 1 - sft:kw - sft, :] * _shift_down(xp, halo, sft)
        _, r, ig, sp, a, mult = _rg_gates(xr, wa_ref, ba_ref, wx_ref, bx_ref, lam_ref)
        bt = mult * (ig * xr)
        row = _rows(bt.shape)
        bt = bt + jnp.where(row == 0, a * carry_ref[SUBLANES - 1:SUBLANES, :], 0.0)
        hs = _scan_real_fwd(a, bt)
        carry_ref[...] = hs[ts - SUBLANES:, :]
        xr_ref[...] = xr
        hs_ref[...] = hs
        y_ref[...] = (hs * _gelu(xg_ref[1])).astype(BF16)

    full = lambda shape: pl.BlockSpec(shape, lambda i: (0,) * len(shape))
    row_spec = pl.BlockSpec((ts, c), lambda i: (i, 0))
    return pl.pallas_call(
        body, name="rg_fwd", grid=(s // ts,),
        in_specs=[pl.BlockSpec((2, ts, c), lambda i: (0, i, 0)),
                  pl.BlockSpec((None, SUBLANES, c), lambda i: (0, jnp.maximum(i * hb - 1, 0), 0)),
                  full(conv_w.shape), full(conv_b.shape), full(w_a.shape), full(b_a.shape), full(w_x.shape), full(b_x.shape),
                  full(lam.shape)],
        out_specs=[row_spec, row_spec, row_spec],
        out_shape=[jax.ShapeDtypeStruct((s, c), F32), jax.ShapeDtypeStruct((s, c), F32), jax.ShapeDtypeStruct((s, c), BF16)],
        scratch_shapes=[pltpu.VMEM((SUBLANES, c), F32)], compiler_params=_cparams(),
    )(xg2, xg2, conv_w, conv_b, w_a, b_a, w_x, b_x, lam)


def _rg_bwd(dy, xg2, xr, hs, conv_w, w_a, b_a, w_x, b_x, lam, ts=256):
    _, s, c = xg2.shape
    ts = min(ts, s)
    nt = s // ts
    kw = RG_CONV_W
    hb = ts // SUBLANES
    bw = c // RG_HEADS
    tn_dims = (((0,), (0,)), ((), ()))
    nt_dims = (((1,), (1,)), ((), ()))

    def body(dy_ref, xg_ref, xph_ref, xr_ref, hs_ref, hsh_ref, cw_ref, wa_ref, ba_ref, wx_ref, bx_ref, lam_ref,
             dxg_ref, dcw_ref, dcb_ref, dwa_ref, dba_ref, dwx_ref, dbx_ref, dlam_ref,
             lam_carry, a_carry, dxr_carry, dsp_acc):
        i = pl.program_id(0)
        first_step = i == 0
        time_first = i == nt - 1

        @pl.when(first_step)
        def _():
            lam_carry[...] = jnp.zeros_like(lam_carry)
            a_carry[...] = jnp.ones_like(a_carry)
            dxr_carry[...] = jnp.zeros_like(dxr_carry)
            dsp_acc[...] = jnp.zeros_like(dsp_acc)
            for ref in (dcw_ref, dcb_ref, dwa_ref, dba_ref, dwx_ref, dbx_ref):
                ref[...] = jnp.zeros_like(ref)

        xr = xr_ref[...]
        hs = hs_ref[...]
        gate = xg_ref[1]
        xb, r, ig, sp, a, mult = _rg_gates(xr, wa_ref, ba_ref, wx_ref, bx_ref, lam_ref)
        dyv = dy_ref[...]
        gg, dgg = _gelu_and_grad(gate)
        dhs = dyv * gg
        dxg_ref[1] = (dyv * hs * dgg).astype(BF16)
        row = _rows(xr.shape)
        coef = jnp.where(row == ts - 1, a_carry[0:1, :], pltpu.roll(a, ts - 1, 0))
        dhs = dhs + jnp.where(row == ts - 1, coef * lam_carry[0:1, :], 0.0)
        lmb = _scan_real_rev(coef, dhs)
        lam_carry[...] = lmb[:SUBLANES]
        a_carry[...] = a[:SUBLANES]
        hs_prev = _shift_down(hs, jnp.where(time_first, 0.0, hsh_ref[...]), 1)
        d_a = lmb * hs_prev
        d_m = lmb * (ig * xr)
        d_ig = lmb * mult * xr
        d_xr = lmb * mult * ig
        d_la = a * d_a - (a * a / mult) * d_m
        dsp_acc[...] += jnp.sum(-RG_C * r * d_la, axis=0, keepdims=True)
        d_za = (-RG_C * sp) * d_la * r * (1.0 - r)
        d_zx = d_ig * ig * (1.0 - ig)
        dba_ref[...] += jnp.sum(d_za, axis=0, keepdims=True)
        dbx_ref[...] += jnp.sum(d_zx, axis=0, keepdims=True)
        dzab, dzxb = d_za.astype(BF16), d_zx.astype(BF16)
        back = []
        for h in range(RG_HEADS):
            sl = slice(h * bw, (h + 1) * bw)
            dwa_ref[h] += lax.dot_general(xb[:, sl], dzab[:, sl], tn_dims, preferred_element_type=F32)
            dwx_ref[h] += lax.dot_general(xb[:, sl], dzxb[:, sl], tn_dims, preferred_element_type=F32)
            back.append(lax.dot_general(dzab[:, sl], wa_ref[h], nt_dims, preferred_element_type=F32)
                        + lax.dot_general(dzxb[:, sl], wx_ref[h], nt_dims, preferred_element_type=F32))
        d_xr = d_xr + jnp.concatenate(back, axis=1)
        d_xp = cw_ref[kw - 1:kw, :] * d_xr
        after = dxr_carry[...]
        for sft in range(1, kw):
            d_xp = d_xp + cw_ref[kw - 1 - sft:kw - sft, :] * _shift_up(d_xr, after, sft)
        dxr_carry[...] = d_xr[:SUBLANES]
        dxg_ref[0] = d_xp.astype(BF16)
        xp = xg_ref[0]
        before = jnp.where(time_first, 0.0, xph_ref[...])
        dcb_ref[...] += jnp.sum(d_xr, axis=0, keepdims=True)
        dcw_ref[kw - 1:kw, :] += jnp.sum(d_xr * xp, axis=0, keepdims=True)
        for sft in range(1, kw):
            dcw_ref[kw - 1 - sft:kw - sft, :] += jnp.sum(d_xr * _shift_down(xp, before, sft), axis=0, keepdims=True)
        dlam_ref[...] = dsp_acc[...] * (-_sigmoid(-lam_ref[...]))

    full = lambda shape: pl.BlockSpec(shape, lambda i: (0,) * len(shape))
    rev = lambda i: nt - 1 - i
    row_spec = pl.BlockSpec((ts, c), lambda i: (rev(i), 0))
    halo_idx = lambda i: jnp.maximum(rev(i) * hb - 1, 0)
    vec = (1, c)
    return pl.pallas_call(
        body, name="rg_bwd", grid=(nt,),
        in_specs=[row_spec,
                  pl.BlockSpec((2, ts, c), lambda i: (0, rev(i), 0)),
                  pl.BlockSpec((None, SUBLANES, c), lambda i: (0, halo_idx(i), 0)),
                  row_spec, row_spec,
                  pl.BlockSpec((SUBLANES, c), lambda i: (halo_idx(i), 0)),
                  full(conv_w.shape), full(w_a.shape), full(b_a.shape), full(w_x.shape), full(b_x.shape), full(lam.shape)],
        out_specs=[pl.BlockSpec((2, ts, c), lambda i: (0, rev(i), 0)), full(conv_w.shape), full(vec), full(w_a.shape), full(vec),
                   full(w_x.shape), full(vec), full(vec)],
        out_shape=[jax.ShapeDtypeStruct((2, s, c), BF16), jax.ShapeDtypeStruct(conv_w.shape, F32), jax.ShapeDtypeStruct(vec, F32),
                   jax.ShapeDtypeStruct(w_a.shape, F32), jax.ShapeDtypeStruct(vec, F32), jax.ShapeDtypeStruct(w_x.shape, F32),
                   jax.ShapeDtypeStruct(vec, F32), jax.ShapeDtypeStruct(vec, F32)],
        scratch_shapes=[pltpu.VMEM((SUBLANES, c), F32), pltpu.VMEM((SUBLANES, c), F32), pltpu.VMEM((SUBLANES, c), F32),
                        pltpu.VMEM(vec, F32)],
        compiler_params=_cparams(),
    )(dy, xg2, xg2, xr, hs, hs, conv_w, w_a, b_a, w_x, b_x, lam)


def _s5_param_fn(a_re, a_im, log_dt, bt_re, bt_im):
    dt = jnp.exp(log_dt)
    mag = jnp.exp(a_re * dt)
    abr = mag * jnp.cos(a_im * dt)
    abi = mag * jnp.sin(a_im * dt)
    ur, ui = abr - 1.0, abi
    den = a_re * a_re + a_im * a_im
    wr = (ur * a_re + ui * a_im) / den
    wi = (ui * a_re - ur * a_im) / den
    bbr = wr[None] * bt_re - wi[None] * bt_im
    bbi = wr[None] * bt_im + wi[None] * bt_re
    return abr, abi, bbr, bbi


def _s5_params(a_re, a_im, log_dt, bt_re, bt_im, nlev):
    g, p = a_re.shape
    gc = bt_re.shape[0]

    def body(ar_ref, ai_ref, dt_ref, br_ref, bi_ref, abr_ref, abi_ref, pr_ref, pi_ref, bbr_ref, bbi_ref):
        abr, abi, bbr, bbi = _s5_param_fn(ar_ref[...], ai_ref[...], dt_ref[...], br_ref[...], bi_ref[...])
        abr_ref[...] = abr
        abi_ref[...] = abi
        bbr_ref[...] = bbr
        bbi_ref[...] = bbi
        qr, qi = abr, abi
        for k in range(nlev):
            pr_ref[k] = qr
            pi_ref[k] = qi
            qr, qi = qr * qr - qi * qi, 2.0 * qr * qi

    sd = jax.ShapeDtypeStruct
    return pl.pallas_call(
        body, name="s5_params",
        out_shape=[sd((g, p), F32), sd((g, p), F32), sd((nlev, g, p), F32), sd((nlev, g, p), F32), sd((gc, g, p), F32),
                   sd((gc, g, p), F32)],
    )(a_re, a_im, log_dt, bt_re, bt_im)


def _s5_params_bwd(a_re, a_im, log_dt, bt_re, bt_im, d_abr, d_abi, d_bbr, d_bbi):
    def body(ar_ref, ai_ref, dt_ref, br_ref, bi_ref, g0, g1, g2, g3, o0, o1, o2, o3, o4):
        _, vjp = jax.vjp(_s5_param_fn, ar_ref[...], ai_ref[...], dt_ref[...], br_ref[...], bi_ref[...])
        outs = vjp((g0[...], g1[...], g2[...], g3[...]))
        for o, v in zip((o0, o1, o2, o3, o4), outs):
            o[...] = v

    sd = jax.ShapeDtypeStruct
    return pl.pallas_call(
        body, name="s5_params_bwd",
        out_shape=[sd(a_re.shape, F32), sd(a_im.shape, F32), sd(log_dt.shape, F32), sd(bt_re.shape, F32), sd(bt_im.shape, F32)],
    )(a_re, a_im, log_dt, bt_re, bt_im, d_abr, d_abi, d_bbr, d_bbi)


def _s5_fwd(u, abr, abi, pw_r, pw_i, bp_r, bp_i, cp_r, cp_i, dvec, ts=128):
    s, c = u.shape
    n = abr.shape[1]
    nblk, cb, nb = bp_r.shape
    ts = min(ts, s)

    def body(u_ref, ar_ref, ai_ref, pr_ref, pi_ref, bpr_ref, bpi_ref, cpr_ref, cpi_ref, d_ref,
             hr_ref, hi_ref, yp_ref, gy_ref, car_r, car_i):
        i = pl.program_id(0)

        @pl.when(i == 0)
        def _():
            car_r[...] = jnp.zeros_like(car_r)
            car_i[...] = jnp.zeros_like(car_i)

        uv = u_ref[...]
        ub = uv.astype(BF16)
        br = jnp.concatenate([jnp.dot(ub[:, k * cb:(k + 1) * cb], bpr_ref[k], preferred_element_type=F32) for k in range(nblk)], axis=1)
        bi = jnp.concatenate([jnp.dot(ub[:, k * cb:(k + 1) * cb], bpi_ref[k], preferred_element_type=F32) for k in range(nblk)], axis=1)
        ar, ai = ar_ref[...], ai_ref[...]
        pr, pi_ = car_r[SUBLANES - 1:SUBLANES, :], car_i[SUBLANES - 1:SUBLANES, :]
        row = _rows(br.shape)
        br = br + jnp.where(row == 0, ar * pr - ai * pi_, 0.0)
        bi = bi + jnp.where(row == 0, ar * pi_ + ai * pr, 0.0)
        hr, hi = _scan_cplx(br, bi, pr_ref, pi_ref, reverse=False)
        car_r[...] = hr[ts - SUBLANES:]
        car_i[...] = hi[ts - SUBLANES:]
        hr_ref[...] = hr
        hi_ref[...] = hi
        hrb, hib = hr.astype(BF16), hi.astype(BF16)
        y = jnp.concatenate([jnp.dot(hrb[:, k * nb:(k + 1) * nb], cpr_ref[k], preferred_element_type=F32)
                             - jnp.dot(hib[:, k * nb:(k + 1) * nb], cpi_ref[k], preferred_element_type=F32) for k in range(nblk)], axis=1)
        yp = y + d_ref[...] * uv
        yp_ref[...] = yp
        gy_ref[...] = _gelu(yp).astype(BF16)

    full = lambda shape: pl.BlockSpec(shape, lambda i: (0,) * len(shape))
    rc = pl.BlockSpec((ts, c), lambda i: (i, 0))
    rn = pl.BlockSpec((ts, n), lambda i: (i, 0))
    sd = jax.ShapeDtypeStruct
    return pl.pallas_call(
        body, name="s5_fwd", grid=(s // ts,),
        in_specs=[rc, full(abr.shape), full(abi.shape), full(pw_r.shape), full(pw_i.shape), full(bp_r.shape), full(bp_i.shape),
                  full(cp_r.shape), full(cp_i.shape), full(dvec.shape)],
        out_specs=[rn, rn, rc, rc],
        out_shape=[sd((s, n), F32), sd((s, n), F32), sd((s, c), F32), sd((s, c), BF16)],
        scratch_shapes=[pltpu.VMEM((SUBLANES, n), F32), pltpu.VMEM((SUBLANES, n), F32)], compiler_params=_cparams(),
    )(u, abr, abi, pw_r, pw_i, bp_r, bp_i, cp_r, cp_i, dvec)


def _s5_bwd(dgy, ypre, u, hr, hi, abr, abi, pw_r, pw_i, bp_r, bp_i, cp_r, cp_i, dvec, ts=128):
    s, c = u.shape
    n = abr.shape[1]
    nblk, cb, nb = bp_r.shape
    ts = min(ts, s)
    nt = s // ts
    hb = ts // SUBLANES
    tn_dims = (((0,), (0,)), ((), ()))
    nt_dims = (((1,), (1,)), ((), ()))

    def body(dgy_ref, yp_ref, u_ref, hr_ref, hi_ref, hrh_ref, hih_ref, ar_ref, ai_ref, pr_ref, pi_ref, bpr_ref, bpi_ref,
             cpr_ref, cpi_ref, d_ref,
             du_ref, dar_ref, dai_ref, dbr_ref, dbi_ref, dcr_ref, dci_ref, dd_ref, car_r, car_i, npi_ref):
        i = pl.program_id(0)
        time_first = i == nt - 1

        @pl.when(i == 0)
        def _():
            car_r[...] = jnp.zeros_like(car_r)
            car_i[...] = jnp.zeros_like(car_i)
            npi_ref[...] = -pi_ref[...]
            for ref in (dar_ref, dai_ref, dbr_ref, dbi_ref, dcr_ref, dci_ref, dd_ref):
                ref[...] = jnp.zeros_like(ref)

        uv = u_ref[...]
        _, dgel = _gelu_and_grad(yp_ref[...])
        dyv = dgy_ref[...] * dgel
        dd_ref[...] += jnp.sum(dyv * uv, axis=0, keepdims=True)
        dyb = dyv.astype(BF16)
        hr, hi = hr_ref[...], hi_ref[...]
        hrb, hib = hr.astype(BF16), hi.astype(BF16)
        dhr, dhi = [], []
        for k in range(nblk):
            dblk = dyb[:, k * cb:(k + 1) * cb]
            dhr.append(lax.dot_general(dblk, cpr_ref[k], nt_dims, preferred_element_type=F32))
            dhi.append(-lax.dot_general(dblk, cpi_ref[k], nt_dims, preferred_element_type=F32))
            dcr_ref[k] += lax.dot_general(hrb[:, k * nb:(k + 1) * nb], dblk, tn_dims, preferred_element_type=F32)
            dci_ref[k] += lax.dot_general(hib[:, k * nb:(k + 1) * nb], dblk, tn_dims, preferred_element_type=F32)
        dhr = jnp.concatenate(dhr, axis=1)
        dhi = jnp.concatenate(dhi, axis=1)
        ar, ai = ar_ref[...], ai_ref[...]
        nr, ni = car_r[0:1, :], car_i[0:1, :]
        row = _rows(dhr.shape)
        dhr = dhr + jnp.where(row == ts - 1, ar * nr + ai * ni, 0.0)
        dhi = dhi + jnp.where(row == ts - 1, ar * ni - ai * nr, 0.0)
        lr, li = _scan_cplx(dhr, dhi, pr_ref, npi_ref, reverse=True)
        car_r[...] = lr[:SUBLANES]
        car_i[...] = li[:SUBLANES]
        hpr = _shift_down(hr, jnp.where(time_first, 0.0, hrh_ref[...]), 1)
        hpi = _shift_down(hi, jnp.where(time_first, 0.0, hih_ref[...]), 1)
        dar_ref[...] += jnp.sum(lr * hpr + li * hpi, axis=0, keepdims=True)
        dai_ref[...] += jnp.sum(li * hpr - lr * hpi, axis=0, keepdims=True)
        lrb, lib = lr.astype(BF16), li.astype(BF16)
        ub = uv.astype(BF16)
        du = []
        for k in range(nblk):
            ublk = ub[:, k * cb:(k + 1) * cb]
            lrk, lik = lrb[:, k * nb:(k + 1) * nb], lib[:, k * nb:(k + 1) * nb]
            dbr_ref[k] += lax.dot_general(ublk, lrk, tn_dims, preferred_element_type=F32)
            dbi_ref[k] += lax.dot_general(ublk, lik, tn_dims, preferred_element_type=F32)
            du.append(lax.dot_general(lrk, bpr_ref[k], nt_dims, preferred_element_type=F32)
                      + lax.dot_general(lik, bpi_ref[k], nt_dims, preferred_element_type=F32))
        du_ref[...] = (d_ref[...] * dyv + jnp.concatenate(du, axis=1)).astype(BF16)

    full = lambda shape: pl.BlockSpec(shape, lambda i: (0,) * len(shape))
    rev = lambda i: nt - 1 - i
    halo_idx = lambda i: jnp.maximum(rev(i) * hb - 1, 0)
    rc = pl.BlockSpec((ts, c), lambda i: (rev(i), 0))
    rn = pl.BlockSpec((ts, n), lambda i: (rev(i), 0))
    hn = pl.BlockSpec((SUBLANES, n), lambda i: (halo_idx(i), 0))
    sd = jax.ShapeDtypeStruct
    return pl.pallas_call(
        body, name="s5_bwd", grid=(nt,),
        in_specs=[rc, rc, rc, rn, rn, hn, hn, full(abr.shape), full(abi.shape), full(pw_r.shape), full(pw_i.shape),
                  full(bp_r.shape), full(bp_i.shape), full(cp_r.shape), full(cp_i.shape), full(dvec.shape)],
        out_specs=[rc, full(abr.shape), full(abi.shape), full(bp_r.shape), full(bp_i.shape), full(cp_r.shape), full(cp_i.shape),
                   full(dvec.shape)],
        out_shape=[sd((s, c), BF16), sd(abr.shape, F32), sd(abi.shape, F32), sd(bp_r.shape, F32), sd(bp_i.shape, F32),
                   sd(cp_r.shape, F32), sd(cp_i.shape, F32), sd(dvec.shape, F32)],
        scratch_shapes=[pltpu.VMEM((SUBLANES, n), F32), pltpu.VMEM((SUBLANES, n), F32), pltpu.VMEM(pw_i.shape, F32)],
        compiler_params=_cparams(),
    )(dgy, ypre, u, hr, hi, hr, hi, abr, abi, pw_r, pw_i, bp_r, bp_i, cp_r, cp_i, dvec)


def _glu(gl2, ts=512):
    _, s, c = gl2.shape
    ts = min(ts, s)

    def body(g_ref, o_ref):
        o_ref[...] = (g_ref[0] * _sigmoid(g_ref[1])).astype(BF16)

    return pl.pallas_call(
        body, name="glu", grid=(s // ts,), in_specs=[pl.BlockSpec((2, ts, c), lambda i: (0, i, 0))],
        out_specs=pl.BlockSpec((ts, c), lambda i: (i, 0)), out_shape=jax.ShapeDtypeStruct((s, c), BF16), compiler_params=_cparams(),
    )(gl2)


def _glu_bwd(gl2, d_o, ts=512):
    _, s, c = gl2.shape
    ts = min(ts, s)

    def body(g_ref, do_ref, o_ref):
        sg = _sigmoid(g_ref[1])
        dov = do_ref[...]
        o_ref[0] = (dov * sg).astype(BF16)
        o_ref[1] = (dov * g_ref[0] * sg * (1.0 - sg)).astype(BF16)

    blk = pl.BlockSpec((2, ts, c), lambda i: (0, i, 0))
    return pl.pallas_call(
        body, name="glu_bwd", grid=(s // ts,), in_specs=[blk, pl.BlockSpec((ts, c), lambda i: (i, 0))],
        out_specs=blk, out_shape=jax.ShapeDtypeStruct((2, s, c), BF16), compiler_params=_cparams(),
    )(gl2, d_o)


def _row_tile(rows, pref=512):
    if rows <= pref:
        return rows
    t = pref
    while rows % t:
        t //= 2
    assert t >= SUBLANES, rows
    return t


def _sum_parts(r):
    p, rows, cols = r.shape
    tr = _row_tile(rows, 128)

    def body(r_ref, o_ref):
        acc = r_ref[0].astype(F32)
        for k in range(1, p):
            acc = acc + r_ref[k].astype(F32)
        o_ref[...] = acc

    return pl.pallas_call(
        body, name="sum_parts", grid=(rows // tr,), in_specs=[pl.BlockSpec((p, tr, cols), lambda i: (0, i, 0))],
        out_specs=pl.BlockSpec((tr, cols), lambda i: (i, 0)), out_shape=jax.ShapeDtypeStruct((rows, cols), F32),
        compiler_params=_cparams(),
    )(r)


def _adamw(w, g_parts, m, v):
    rows, cols = w.shape
    tr = _row_tile(rows, 128)
    ng = len(g_parts)
    c1 = 1.0 / (1.0 - ADAM_B1 ** ADAM_STEP)
    c2 = 1.0 / (1.0 - ADAM_B2 ** ADAM_STEP)

    def body(*refs):
        w_ref, m_ref, v_ref = refs[0], refs[1 + ng], refs[2 + ng]
        g_ref, dl_ref, nm_ref, nv_ref = refs[3 + ng:]
        g = refs[1][...]
        for k in range(1, ng):
            g = g + refs[1 + k][...]
        mn = ADAM_B1 * m_ref[...] + (1.0 - ADAM_B1) * g
        vn = ADAM_B2 * v_ref[...] + (1.0 - ADAM_B2) * (g * g)
        g_ref[...] = g
        nm_ref[...] = mn
        nv_ref[...] = vn
        dl_ref[...] = -ADAM_LR * ((mn * c1) / (jnp.sqrt(vn * c2) + ADAM_EPS) + ADAM_WD * w_ref[...])

    blk = pl.BlockSpec((tr, cols), lambda i: (i, 0))
    sd = jax.ShapeDtypeStruct((rows, cols), F32)
    return pl.pallas_call(
        body, name="adamw", grid=(rows // tr,), in_specs=[blk] * (3 + ng), out_specs=[blk] * 4, out_shape=[sd] * 4,
        compiler_params=_cparams(),
    )(w, *g_parts, m, v)


def _place():
    x, y, c = lax.axis_index("x"), lax.axis_index("y"), lax.axis_index("c")
    chips = [(1 - x, y), (x, 1 - y), (1 - x, 1 - y)]
    return x, y, c, chips


def _gather_shards(shards):
    n = len(shards)

    def body(*refs):
        ins, outs = refs[:n], refs[n:2 * n]
        send, recv, lsem = refs[2 * n:]
        x, y, c, chips = _place()
        me = 2 * x + y
        local, sends = [], []
        for t in range(n):
            cp = pltpu.make_async_copy(ins[t], outs[t].at[me], lsem.at[t])
            cp.start()
            local.append(cp)
            for r, (px, py) in enumerate(chips):
                rc = pltpu.make_async_remote_copy(src_ref=ins[t], dst_ref=outs[t].at[me], send_sem=send.at[3 * t + r],
                                                  recv_sem=recv.at[3 * t + r], device_id=(px, py, c), device_id_type=MESH)
                rc.start()
                sends.append(rc)
        for t in range(n):
            for r, (px, py) in enumerate(chips):
                pltpu.make_async_remote_copy(src_ref=ins[t], dst_ref=outs[t].at[2 * px + py], send_sem=send.at[3 * t + r],
                                             recv_sem=recv.at[3 * t + r], device_id=(px, py, c), device_id_type=MESH).wait_recv()
        for rc in sends:
            rc.wait_send()
        for cp in local:
            cp.wait()

    any_spec = pl.BlockSpec(memory_space=pl.ANY)
    return pl.pallas_call(
        body, name="gather_shards", in_specs=[any_spec] * n, out_specs=[any_spec] * n,
        out_shape=[jax.ShapeDtypeStruct((N_CHIPS,) + s.shape, s.dtype) for s in shards],
        scratch_shapes=[pltpu.SemaphoreType.DMA((3 * n,)), pltpu.SemaphoreType.DMA((3 * n,)), pltpu.SemaphoreType.DMA((n,))],
    )(*shards)


def _scatter_grads(groups):
    flat = [(gi, li, a) for gi, grp in enumerate(groups) for li, a in enumerate(grp)]
    n = len(flat)
    ng = len(groups)

    def body(*refs):
        ins, outs = refs[:n], refs[n:n + ng]
        send, recv, lsem = refs[n + ng:]
        x, y, c, chips = _place()
        me = 2 * x + y
        local, sends = [], []
        for t, (gi, li, _) in enumerate(flat):
            cp = pltpu.make_async_copy(ins[t].at[me], outs[gi].at[me, li], lsem.at[t])
            cp.start()
            local.append(cp)
            for r, (px, py) in enumerate(chips):
                rc = pltpu.make_async_remote_copy(src_ref=ins[t].at[2 * px + py], dst_ref=outs[gi].at[me, li],
                                                  send_sem=send.at[3 * t + r], recv_sem=recv.at[3 * t + r],
                                                  device_id=(px, py, c), device_id_type=MESH)
                rc.start()
                sends.append(rc)
        for t, (gi, li, _) in enumerate(flat):
            for r, (px, py) in enumerate(chips):
                pltpu.make_async_remote_copy(src_ref=ins[t].at[me], dst_ref=outs[gi].at[2 * px + py, li],
                                             send_sem=send.at[3 * t + r], recv_sem=recv.at[3 * t + r],
                                             device_id=(px, py, c), device_id_type=MESH).wait_recv()
        for rc in sends:
            rc.wait_send()
        for cp in local:
            cp.wait()

    any_spec = pl.BlockSpec(memory_space=pl.ANY)
    return pl.pallas_call(
        body, name="scatter_grads", in_specs=[any_spec] * n, out_specs=[any_spec] * ng,
        out_shape=[jax.ShapeDtypeStruct((N_CHIPS, len(grp)) + grp[0].shape[1:], grp[0].dtype) for grp in groups],
        scratch_shapes=[pltpu.SemaphoreType.DMA((3 * n,)), pltpu.SemaphoreType.DMA((3 * n,)), pltpu.SemaphoreType.DMA((n,))],
    )(*[a for _, _, a in flat])


def _swap_with_sibling(arrs):
    n = len(arrs)

    def body(*refs):
        ins, outs = refs[:n], refs[n:2 * n]
        send, recv = refs[2 * n:]
        x, y, c, _ = _place()
        cps = []
        for t in range(n):
            rc = pltpu.make_async_remote_copy(src_ref=ins[t], dst_ref=outs[t], send_sem=send.at[t], recv_sem=recv.at[t],
                                              device_id=(x, y, 1 - c), device_id_type=MESH)
            rc.start()
            cps.append(rc)
        for rc in cps:
            rc.wait_recv()
        for rc in cps:
            rc.wait_send()

    any_spec = pl.BlockSpec(memory_space=pl.ANY)
    return pl.pallas_call(
        body, name="swap_with_sibling", in_specs=[any_spec] * n, out_specs=[any_spec] * n,
        out_shape=[jax.ShapeDtypeStruct(a.shape, a.dtype) for a in arrs],
        scratch_shapes=[pltpu.SemaphoreType.DMA((n,)), pltpu.SemaphoreType.DMA((n,))],
    )(*arrs)


def _allreduce_small(v):
    rows, cols = v.shape

    def body(v_ref, o_ref, sib_ref, chip_ref, send, recv):
        x, y, c, chips = _place()
        me = 2 * x + y
        d2d = pltpu.make_async_remote_copy(src_ref=v_ref, dst_ref=sib_ref, send_sem=send.at[0], recv_sem=recv.at[0],
                                           device_id=(x, y, 1 - c), device_id_type=MESH)
        d2d.start()
        d2d.wait_recv()
        chip_ref[me] = v_ref[...] + sib_ref[...]
        sends = []
        for r, (px, py) in enumerate(chips):
            rc = pltpu.make_async_remote_copy(src_ref=chip_ref.at[me], dst_ref=chip_ref.at[me], send_sem=send.at[1 + r],
                                              recv_sem=recv.at[1 + r], device_id=(px, py, c), device_id_type=MESH)
            rc.start()
            sends.append(rc)
        for r, (px, py) in enumerate(chips):
            pltpu.make_async_remote_copy(src_ref=chip_ref.at[me], dst_ref=chip_ref.at[2 * px + py], send_sem=send.at[1 + r],
                                         recv_sem=recv.at[1 + r], device_id=(px, py, c), device_id_type=MESH).wait_recv()
        o_ref[...] = (chip_ref[0] + chip_ref[1]) + (chip_ref[2] + chip_ref[3])
        d2d.wait_send()
        for rc in sends:
            rc.wait_send()

    vm = pl.BlockSpec(memory_space=pltpu.VMEM)
    return pl.pallas_call(
        body, name="allreduce_small", in_specs=[vm], out_specs=vm, out_shape=jax.ShapeDtypeStruct((rows, cols), F32),
        scratch_shapes=[pltpu.VMEM((rows, cols), F32), pltpu.VMEM((N_CHIPS, rows, cols), F32), pltpu.SemaphoreType.DMA((4,)),
                        pltpu.SemaphoreType.DMA((4,))],
        compiler_params=_cparams(),
    )(v)


def _pack(tensors):
    pieces = []
    for t in tensors:
        flat = t.reshape(-1)
        pad = (-flat.shape[0]) % (SUBLANES * LANES)
        pieces.append(jnp.pad(flat, (0, pad)).reshape(-1, LANES))
    return jnp.concatenate(pieces, axis=0)


def _unpack(buf, like):
    out, off = [], 0
    for t in like:
        size = math.prod(t.shape)
        rows = -(-size // (SUBLANES * LANES)) * SUBLANES
        out.append(buf[off:off + rows].reshape(-1)[:size].reshape(t.shape))
        off += rows
    return out


def _s5_pack_b(bb):
    gc, g, p = bb.shape
    q = S5_GROUPS_PER_BLOCK
    t = bb.reshape(gc, g // q, q, p).transpose(1, 2, 0, 3)
    eye = jnp.eye(q, dtype=bb.dtype)
    return (t[:, :, :, None, :] * eye[None, :, None, :, None]).reshape(g // q, q * gc, q * p)


def _s5_unpack_b(dbp, gc, p):
    nb = dbp.shape[0]
    q = S5_GROUPS_PER_BLOCK
    eye = jnp.eye(q, dtype=dbp.dtype)
    t = (dbp.reshape(nb, q, gc, q, p) * eye[None, :, None, :, None]).sum(axis=3)
    return t.transpose(2, 0, 1, 3).reshape(gc, nb * q, p)


def _s5_pack_c(cc):
    g, gc, p = cc.shape
    q = S5_GROUPS_PER_BLOCK
    t = cc.reshape(g // q, q, gc, p).transpose(0, 1, 3, 2)
    eye = jnp.eye(q, dtype=cc.dtype)
    return (t[:, :, :, None, :] * eye[None, :, None, :, None]).reshape(g // q, q * p, q * gc)


def _s5_unpack_c(dcp, gc, p):
    nb = dcp.shape[0]
    q = S5_GROUPS_PER_BLOCK
    eye = jnp.eye(q, dtype=dcp.dtype)
    t = (dcp.reshape(nb, q, p, q, gc) * eye[None, :, None, :, None]).sum(axis=3)
    return t.transpose(0, 1, 3, 2).reshape(nb * q, gc, p)


def _split2(m):
    return m.arr[:, 0]


def kernel(x, norm_mix_g, norm_ffn_g, norm_final_g, rg_w_in, rg_conv_w, rg_conv_b, rg_w_a, rg_b_a, rg_w_x, rg_b_x, rg_lambda, rg_w_out, s5_w_in, s5_a_re, s5_a_im, s5_log_dt, s5_b_re, s5_b_im, s5_c_re, s5_c_im, s5_d, s5_w_glu, s5_w_out, ffn_w_up, ffn_conv_w, ffn_conv_b, ffn_w_down, loss_target, m_norm_mix_g, m_norm_ffn_g, m_norm_final_g, m_rg_w_in, m_rg_conv_w, m_rg_conv_b, m_rg_w_a, m_rg_b_a, m_rg_w_x, m_rg_b_x, m_rg_lambda, m_rg_w_out, m_s5_w_in, m_s5_a_re, m_s5_a_im, m_s5_log_dt, m_s5_b_re, m_s5_b_im, m_s5_c_re, m_s5_c_im, m_s5_d, m_s5_w_glu, m_s5_w_out, m_ffn_w_up, m_ffn_conv_w, m_ffn_conv_b, m_ffn_w_down, v_norm_mix_g, v_norm_ffn_g, v_norm_final_g, v_rg_w_in, v_rg_conv_w, v_rg_conv_b, v_rg_w_a, v_rg_b_a, v_rg_w_x, v_rg_b_x, v_rg_lambda, v_rg_w_out, v_s5_w_in, v_s5_a_re, v_s5_a_im, v_s5_log_dt, v_s5_b_re, v_s5_b_im, v_s5_c_re, v_s5_c_im, v_s5_d, v_s5_w_glu, v_s5_w_out, v_ffn_w_up, v_ffn_conv_w, v_ffn_conv_b, v_ffn_w_down):
    w = dict(zip(PARAM_NAMES, (norm_mix_g, norm_ffn_g, norm_final_g, rg_w_in, rg_conv_w, rg_conv_b, rg_w_a, rg_b_a, rg_w_x, rg_b_x,
                               rg_lambda, rg_w_out, s5_w_in, s5_a_re, s5_a_im, s5_log_dt, s5_b_re, s5_b_im, s5_c_re, s5_c_im, s5_d,
                               s5_w_glu, s5_w_out, ffn_w_up, ffn_conv_w, ffn_conv_b, ffn_w_down)))
    mom = dict(zip(PARAM_NAMES, (m_norm_mix_g, m_norm_ffn_g, m_norm_final_g, m_rg_w_in, m_rg_conv_w, m_rg_conv_b, m_rg_w_a, m_rg_b_a,
                                 m_rg_w_x, m_rg_b_x, m_rg_lambda, m_rg_w_out, m_s5_w_in, m_s5_a_re, m_s5_a_im, m_s5_log_dt, m_s5_b_re,
                                 m_s5_b_im, m_s5_c_re, m_s5_c_im, m_s5_d, m_s5_w_glu, m_s5_w_out, m_ffn_w_up, m_ffn_conv_w,
                                 m_ffn_conv_b, m_ffn_w_down)))
    vel = dict(zip(PARAM_NAMES, (v_norm_mix_g, v_norm_ffn_g, v_norm_final_g, v_rg_w_in, v_rg_conv_w, v_rg_conv_b, v_rg_w_a, v_rg_b_a,
                                 v_rg_w_x, v_rg_b_x, v_rg_lambda, v_rg_w_out, v_s5_w_in, v_s5_a_re, v_s5_a_im, v_s5_log_dt, v_s5_b_re,
                                 v_s5_b_im, v_s5_c_re, v_s5_c_im, v_s5_d, v_s5_w_glu, v_s5_w_out, v_ffn_w_up, v_ffn_conv_w,
                                 v_ffn_conv_b, v_ffn_w_down)))
    _, s, d = x.shape
    depth = norm_mix_g.shape[0]
    n_grp, n_state = s5_a_re.shape[1], s5_a_re.shape[2]
    gc = s5_b_re.shape[3]
    d_ff = ffn_w_down.shape[1] * N_CHIPS
    s5_ts = min(128, s)
    nlev = max(1, int(math.log2(s5_ts)))

    gathered = dict(zip(SHARDED, _gather_shards([w[n].astype(BF16) if n in BIG else w[n] for n in SHARDED])))
    rg_cw = gathered['rg_conv_w'].transpose(1, 2, 0, 3).reshape(rg_conv_w.shape[0], RG_CONV_W, d)
    s5_dv = gathered['s5_d'].transpose(1, 0, 2).reshape(s5_d.shape[0], 1, d)
    f_cw = gathered['ffn_conv_w'].transpose(1, 2, 0, 3).reshape(depth, FFN_CONV_W, 2, d_ff).transpose(0, 2, 1, 3)
    f_cb = ffn_conv_b.reshape(depth, 2, 1, d_ff)

    h = x.reshape(s, d)
    saved = []
    for i in range(depth):
        j = i // 2
        sv = {'h_in': h}
        hn = _rms_fwd(h, norm_mix_g[i:i + 1])
        sv['hn'] = hn
        if i % 2 == 0:
            xg = _mm("rg_in", 'nn', act(hn), Mat(gathered['rg_w_in'], j, 'c'), out_parts=2, tn=512, tk=d)
            xg2 = _split2(xg)
            wa, wx = rg_w_a[j].astype(BF16), rg_w_x[j].astype(BF16)
            ba, bx = rg_b_a[j].reshape(1, d), rg_b_x[j].reshape(1, d)
            xr, hs, y = _rg_fwd(xg2, rg_cw[j], rg_conv_b[j:j + 1], wa, ba, wx, bx, rg_lambda[j:j + 1])
            sv.update(xg2=xg2, xr=xr, hs=hs, y=y, wa=wa, wx=wx, ba=ba, bx=bx)
            h = _mm("rg_out", 'nn', act(y), Mat(gathered['rg_w_out'], j, 'r'), res=act(h), tn=d, tk=256).arr[0, 0]
        else:
            u = _mm("s5_in", 'nn', act(hn), Mat(gathered['s5_w_in'], j, 'r'), tn=d, tk=256).arr[0, 0]
            bt_re, bt_im = s5_b_re[j].transpose(2, 0, 1), s5_b_im[j].transpose(2, 0, 1)
            ldt = s5_log_dt[j].reshape(n_grp, 1)
            abr, abi, pw_r, pw_i, bbr, bbi = _s5_params(s5_a_re[j], s5_a_im[j], ldt, bt_re, bt_im, nlev)
            nn_ = n_grp * n_state
            prm = dict(abr=abr.reshape(1, nn_), abi=abi.reshape(1, nn_), pw_r=pw_r.reshape(nlev, nn_), pw_i=pw_i.reshape(nlev, nn_),
                       bp_r=_s5_pack_b(bbr).astype(BF16), bp_i=_s5_pack_b(bbi).astype(BF16),
                       cp_r=_s5_pack_c(s5_c_re[j]).astype(BF16), cp_i=_s5_pack_c(s5_c_im[j]).astype(BF16), dvec=s5_dv[j])
            hr, hi, ypre, gy = _s5_fwd(u, ts=s5_ts, **prm)
            gl = _mm("s5_glu", 'nn', act(gy), Mat(gathered['s5_w_glu'], j, 'c'), out_parts=2, tn=512, tk=d)
            gl2 = _split2(gl)
            o = _glu(gl2)
            sv.update(u=u, prm=prm, hr=hr, hi=hi, ypre=ypre, gy=gy, gl2=gl2, o=o, bt_re=bt_re, bt_im=bt_im, ldt=ldt)
            h = _mm("s5_out", 'nn', act(o), Mat(gathered['s5_w_out'], j, 'r'), res=act(h), tn=d, tk=256).arr[0, 0]
        sv['h_mid'] = h
        hn2 = _rms_fwd(h, norm_ffn_g[i:i + 1])
        up = _mm("ffn_up", 'nn', act(hn2), Mat(gathered['ffn_w_up'], i, 'c'), out_parts=2, tn=512, tk=d)
        up2 = _split2(up)
        a_ffn = _ffn_act(up2, f_cw[i], f_cb[i])
        sv.update(hn2=hn2, up2=up2, act=a_ffn)
        h = _mm("ffn_down", 'nn', act(a_ffn), Mat(gathered['ffn_w_down'], i, 'r'), res=act(h), tn=d, tk=d_ff // N_CHIPS).arr[0, 0]
        saved.append(sv)

    loss_row, dh, dg_final = _loss_and_grad(h, norm_final_g.reshape(1, d), loss_target.reshape(s, d))
    loss = lax.psum(loss_row[0, 0], ("x", "y", "c"))

    gl_ = {n: [None] * w[n].shape[0] for n in PARAM_NAMES if n != 'norm_final_g'}
    for i in reversed(range(depth)):
        j = i // 2
        sv = saved[i]
        dact = _mm("ffn_down_dx", 'nt', act(dh), Mat(gathered['ffn_w_down'], i, 'r'), tn=d_ff // N_CHIPS, tk=d).arr[0, 0]
        gl_['ffn_w_down'][i] = _mm("ffn_down_dw", 'tn', act(sv['act']), act(dh), out_parts=N_CHIPS, out_split='r',
                                   tm=d_ff // N_CHIPS, tn=d).arr
        dup2, dcw2, dcb2 = _ffn_bwd(sv['up2'], dact, f_cw[i], f_cb[i])
        gl_['ffn_conv_w'][i] = dcw2.transpose(1, 0, 2).reshape(FFN_CONV_W, N_CHIPS, 2 * d_ff // N_CHIPS).transpose(1, 0, 2)
        gl_['ffn_conv_b'][i] = dcb2.reshape(2 * d_ff)
        dup = Mat(dup2[:, None], 0, 'c')
        gl_['ffn_w_up'][i] = _mm("ffn_up_dw", 'tn', act(sv['hn2']), dup, out_parts=N_CHIPS, tm=d, tn=512).arr
        dhn2 = _mm("ffn_up_dx", 'nt', dup, Mat(gathered['ffn_w_up'], i, 'c'), tn=d, tk=512).arr[0, 0]
        dh, dg = _rms_bwd(sv['h_mid'], norm_ffn_g[i:i + 1], dhn2, dh)
        gl_['norm_ffn_g'][i] = dg[0]
        if i % 2 == 0:
            dy = _mm("rg_out_dx", 'nt', act(dh), Mat(gathered['rg_w_out'], j, 'r'), tn=256, tk=d).arr[0, 0]
            gl_['rg_w_out'][j] = _mm("rg_out_dw", 'tn', act(sv['y']), act(dh), out_parts=N_CHIPS, out_split='r', tm=256, tn=d).arr
            dxg2, dcw, dcb, dwa, dba, dwx, dbx, dlam = _rg_bwd(dy, sv['xg2'], sv['xr'], sv['hs'], rg_cw[j], sv['wa'], sv['ba'],
                                                              sv['wx'], sv['bx'], rg_lambda[j:j + 1])
            gl_['rg_conv_w'][j] = dcw.reshape(RG_CONV_W, N_CHIPS, d // N_CHIPS).transpose(1, 0, 2)
            gl_['rg_conv_b'][j] = dcb[0]
            gl_['rg_w_a'][j], gl_['rg_w_x'][j] = dwa, dwx
            gl_['rg_b_a'][j], gl_['rg_b_x'][j] = dba.reshape(rg_b_a.shape[1:]), dbx.reshape(rg_b_x.shape[1:])
            gl_['rg_lambda'][j] = dlam[0]
            dxg = Mat(dxg2[:, None], 0, 'c')
            gl_['rg_w_in'][j] = _mm("rg_in_dw", 'tn', act(sv['hn']), dxg, out_parts=N_CHIPS, tm=d, tn=512).arr
            dhn = _mm("rg_in_dx", 'nt', dxg, Mat(gathered['rg_w_in'], j, 'c'), tn=d, tk=512).arr[0, 0]
        else:
            d_o = _mm("s5_out_dx", 'nt', act(dh), Mat(gathered['s5_w_out'], j, 'r'), tn=256, tk=d).arr[0, 0]
            gl_['s5_w_out'][j] = _mm("s5_out_dw", 'tn', act(sv['o']), act(dh), out_parts=N_CHIPS, out_split='r', tm=256, tn=d).arr
            dgl2 = _glu_bwd(sv['gl2'], d_o)
            dgl = Mat(dgl2[:, None], 0, 'c')
            gl_['s5_w_glu'][j] = _mm("s5_glu_dw", 'tn', act(sv['gy']), dgl, out_parts=N_CHIPS, tm=d, tn=512).arr
            dgy = _mm("s5_glu_dx", 'nt', dgl, Mat(gathered['s5_w_glu'], j, 'c'), tn=d, tk=512).arr[0, 0]
            du, dar, dai, dbpr, dbpi, dcpr, dcpi, dd = _s5_bwd(dgy, sv['ypre'], sv['u'], sv['hr'], sv['hi'], ts=s5_ts, **sv['prm'])
            gl_['s5_d'][j] = dd.reshape(N_CHIPS, d // N_CHIPS)
            gl_['s5_c_re'][j] = _s5_unpack_c(dcpr, gc, n_state)
            gl_['s5_c_im'][j] = -_s5_unpack_c(dcpi, gc, n_state)
            d_are, d_aim, d_ldt, d_btr, d_bti = _s5_params_bwd(
                s5_a_re[j], s5_a_im[j], sv['ldt'], sv['bt_re'], sv['bt_im'], dar.reshape(n_grp, n_state), dai.reshape(n_grp, n_state),
                _s5_unpack_b(dbpr, gc, n_state), _s5_unpack_b(dbpi, gc, n_state))
            gl_['s5_a_re'][j], gl_['s5_a_im'][j], gl_['s5_log_dt'][j] = d_are, d_aim, d_ldt[:, 0]
            gl_['s5_b_re'][j], gl_['s5_b_im'][j] = d_btr.transpose(1, 2, 0), d_bti.transpose(1, 2, 0)
            dum = act(du)
            gl_['s5_w_in'][j] = _mm("s5_in_dw", 'tn', act(sv['hn']), dum, out_parts=N_CHIPS, out_split='r', tm=256, tn=d).arr
            dhn = _mm("s5_in_dx", 'nt', dum, Mat(gathered['s5_w_in'], j, 'r'), tn=256, tk=d).arr[0, 0]
        dh, dg = _rms_bwd(sv['h_in'], norm_mix_g[i:i + 1], dhn, dh)
        gl_['norm_mix_g'][i] = dg[0]
    grad_x = dh.reshape(x.shape)

    def as4(n, a):
        shp = w[n].shape[1:]
        return a.reshape((N_CHIPS,) + shp)

    groups = [[as4(n, a) for a in gl_[n]] for n in SHARDED]
    recv = _scatter_grads(groups)
    chip_sums = []
    for n, r in zip(SHARDED, recv):
        cols = w[n].shape[-1]
        chip_sums.append(_sum_parts(r.reshape(N_CHIPS, -1, cols)))
    sib_sums = _swap_with_sibling(chip_sums)
    results = {}
    for n, mine, theirs in zip(SHARDED, chip_sums, sib_sums):
        cols = w[n].shape[-1]
        outs = _adamw(w[n].reshape(-1, cols), [mine, theirs], mom[n].reshape(-1, cols), vel[n].reshape(-1, cols))
        results[n] = [o.reshape(w[n].shape) for o in outs]

    rep_local = [dg_final.reshape(d) if n == 'norm_final_g' else jnp.stack(gl_[n]) for n in REPLICATED]
    rep_like = [w[n] for n in REPLICATED]
    g_pack = _allreduce_small(_pack(rep_local))
    outs = _adamw(_pack(rep_like), [g_pack], _pack([mom[n] for n in REPLICATED]), _pack([vel[n] for n in REPLICATED]))
    unpacked = [_unpack(o, rep_like) for o in outs]
    for k, n in enumerate(REPLICATED):
        results[n] = [unpacked[q][k] for q in range(4)]

    return (loss, grad_x, *[results[n][0] for n in PARAM_NAMES], *[results[n][1] for n in PARAM_NAMES],
            *[results[n][2] for n in PARAM_NAMES], *[results[n][3] for n in PARAM_NAMES])
```

```python
import functools
import math

import jax
import jax.numpy as jnp
from jax import lax
from jax.experimental import pallas as pl
from jax.experimental.pallas import tpu as pltpu

F32 = jnp.float32
BF16 = jnp.bfloat16
MESH = pl.DeviceIdType.MESH

NORM_EPS = 1e-6
RG_HEADS = 8
RG_CONV_W = 4
RG_C = 8.0
S5_GC = 16
S5_P = 64
S5_GROUPS_PER_BLOCK = 8
FFN_CONV_W = 3
N_CHIPS = 4
ADAM_LR, ADAM_B1, ADAM_B2, ADAM_EPS, ADAM_WD, ADAM_STEP = 0.001, 0.9, 0.999, 1e-08, 0.01, 10
VMEM_LIMIT_BYTES = 56 * 1024 * 1024
SUBLANES = 8
LANES = 128

PARAM_NAMES = ['norm_mix_g', 'norm_ffn_g', 'norm_final_g', 'rg_w_in', 'rg_conv_w', 'rg_conv_b', 'rg_w_a', 'rg_b_a', 'rg_w_x',
               'rg_b_x', 'rg_lambda', 'rg_w_out', 's5_w_in', 's5_a_re', 's5_a_im', 's5_log_dt', 's5_b_re', 's5_b_im', 's5_c_re',
               's5_c_im', 's5_d', 's5_w_glu', 's5_w_out', 'ffn_w_up', 'ffn_conv_w', 'ffn_conv_b', 'ffn_w_down']
SHARDED = ['rg_w_in', 'rg_conv_w', 'rg_w_out', 's5_w_in', 's5_d', 's5_w_glu', 's5_w_out', 'ffn_w_up', 'ffn_conv_w', 'ffn_w_down']
BIG = ['rg_w_in', 'rg_w_out', 's5_w_in', 's5_w_glu', 's5_w_out', 'ffn_w_up', 'ffn_w_down']
ROW_SHARDED = ['rg_w_out', 's5_w_in', 's5_w_out', 'ffn_w_down']
REPLICATED = [n for n in PARAM_NAMES if n not in SHARDED]


def _cparams():
    return pltpu.CompilerParams(vmem_limit_bytes=VMEM_LIMIT_BYTES)


_GELU_C = math.sqrt(2.0 / math.pi)
_GELU_K = 0.044715


def _gelu(x):
    return 0.5 * x * (1.0 + jnp.tanh(_GELU_C * (x + _GELU_K * x * x * x)))


def _gelu_and_grad(x):
    t = jnp.tanh(_GELU_C * (x + _GELU_K * x * x * x))
    g = 0.5 * x * (1.0 + t)
    dg = 0.5 * (1.0 + t) + 0.5 * x * (1.0 - t * t) * (_GELU_C * (1.0 + 3.0 * _GELU_K * x * x))
    return g, dg


def _sigmoid(x):
    return jax.nn.sigmoid(x)


def _neg_expm1(x):
    series = -(x * (1.0 + x * (0.5 + x * (1.0 / 6 + x * (1.0 / 24 + x * (1.0 / 120 + x * (1.0 / 720)))))))
    return jnp.where(x > -0.25, series, 1.0 - jnp.exp(x))


def _softplus(z):
    return jnp.maximum(z, 0.0) + jnp.log1p(jnp.exp(-jnp.abs(z)))


def _rows(shape):
    return lax.broadcasted_iota(jnp.int32, shape, 0)


def _shift_down(x, halo, k):
    ext = jnp.concatenate([halo, x], axis=0)
    return pltpu.roll(ext, k, 0)[SUBLANES:]


def _shift_up(x, halo, k):
    ext = jnp.concatenate([x, halo], axis=0)
    n = ext.shape[0]
    return pltpu.roll(ext, n - k, 0)[:x.shape[0]]


def _scan_real_fwd(a, b):
    n = a.shape[0]
    row = _rows(a.shape)
    sh = 1
    while sh < n:
        ok = row >= sh
        b = a * jnp.where(ok, pltpu.roll(b, sh, 0), 0.0) + b
        if sh * 2 < n:
            a = a * jnp.where(ok, pltpu.roll(a, sh, 0), 1.0)
        sh *= 2
    return b


def _scan_real_rev(c, d):
    n = c.shape[0]
    row = _rows(c.shape)
    sh = 1
    while sh < n:
        ok = row < n - sh
        d = c * jnp.where(ok, pltpu.roll(d, n - sh, 0), 0.0) + d
        if sh * 2 < n:
            c = c * jnp.where(ok, pltpu.roll(c, n - sh, 0), 1.0)
        sh *= 2
    return d


def _scan_cplx(br, bi, pr_ref, pi_ref, reverse):
    n = br.shape[0]
    row = _rows(br.shape)
    sh, k = 1, 0
    while sh < n:
        pr = pr_ref[k:k + 1, :]
        pi = pi_ref[k:k + 1, :]
        if reverse:
            ok = row < n - sh
            sr = jnp.where(ok, pltpu.roll(br, n - sh, 0), 0.0)
            si = jnp.where(ok, pltpu.roll(bi, n - sh, 0), 0.0)
        else:
            ok = row >= sh
            sr = jnp.where(ok, pltpu.roll(br, sh, 0), 0.0)
            si = jnp.where(ok, pltpu.roll(bi, sh, 0), 0.0)
        br, bi = br + pr * sr - pi * si, bi + pr * si + pi * sr
        sh *= 2
        k += 1
    return br, bi


class Mat:
    def __init__(self, arr, l=0, split='c'):
        assert arr.ndim == 4
        self.arr, self.l, self.split = arr, l, split
        p, _, r, c = arr.shape
        self.shape = (r, c * p) if split == 'c' else (r * p, c)

    def spec(self, tr, tc, rc):
        p, _, r, c = self.arr.shape
        l = self.l
        assert r % tr == 0 and c % tc == 0, (self.arr.shape, tr, tc)
        if self.split == 'c':
            per = c // tc
            return pl.BlockSpec((None, None, tr, tc), lambda i, j, k: (rc(i, j, k)[1] // per, l, rc(i, j, k)[0], rc(i, j, k)[1] % per))
        per = r // tr
        return pl.BlockSpec((None, None, tr, tc), lambda i, j, k: (rc(i, j, k)[0] // per, l, rc(i, j, k)[0] % per, rc(i, j, k)[1]))


def act(x, parts=1):
    s, c = x.shape
    return Mat(x.reshape(s, parts, c // parts).transpose(1, 0, 2)[:, None] if parts > 1 else x[None, None])


def _mm(name, mode, a, b, *, out_parts=1, out_split='c', out_dtype=F32, res=None, tm=512, tn=512, tk=512):
    if mode == 'nn':
        (m, kk), (kb, n) = a.shape, b.shape
    elif mode == 'nt':
        (m, kk), (n, kb) = a.shape, b.shape
    else:
        (kk, m), (kb, n) = a.shape, b.shape
    assert kk == kb, (name, a.shape, b.shape)
    tm, tn, tk = min(tm, m), min(tn, n), min(tk, kk)
    assert m % tm == 0 and n % tn == 0 and kk % tk == 0, (name, m, n, kk, tm, tn, tk)
    nk = kk // tk
    if mode == 'nn':
        a_spec = a.spec(tm, tk, lambda i, j, k: (i, k))
        b_spec = b.spec(tk, tn, lambda i, j, k: (k, j))
        dims = (((1,), (0,)), ((), ()))
    elif mode == 'nt':
        a_spec = a.spec(tm, tk, lambda i, j, k: (i, k))
        b_spec = b.spec(tn, tk, lambda i, j, k: (j, k))
        dims = (((1,), (1,)), ((), ()))
    else:
        a_spec = a.spec(tk, tm, lambda i, j, k: (k, i))
        b_spec = b.spec(tk, tn, lambda i, j, k: (k, j))
        dims = (((0,), (0,)), ((), ()))
    if out_split == 'c':
        out_arr = jax.ShapeDtypeStruct((out_parts, 1, m, n // out_parts), out_dtype)
    else:
        out_arr = jax.ShapeDtypeStruct((out_parts, 1, m // out_parts, n), out_dtype)
    out_mat = Mat(out_arr, 0, out_split)
    o_spec = out_mat.spec(tm, tn, lambda i, j, k: (i, j))
    has_res = res is not None

    def body(*refs):
        if has_res:
            a_ref, b_ref, r_ref, o_ref = refs[:4]
        else:
            a_ref, b_ref, o_ref = refs[:3]
        prod = lax.dot_general(a_ref[...].astype(BF16), b_ref[...].astype(BF16), dims, preferred_element_type=F32)

        def finish(acc):
            if has_res:
                acc = acc + r_ref[...]
            o_ref[...] = acc.astype(out_dtype)

        if nk == 1:
            finish(prod)
        else:
            acc_ref = refs[-1]
            k = pl.program_id(2)

            @pl.when(k == 0)
            def _():
                acc_ref[...] = prod

            @pl.when(k > 0)
            def _():
                acc_ref[...] += prod

            @pl.when(k == nk - 1)
            def _():
                finish(acc_ref[...])

    in_specs = [a_spec, b_spec]
    args = [a.arr, b.arr]
    if has_res:
        in_specs.append(res.spec(tm, tn, lambda i, j, k: (i, j)))
        args.append(res.arr)
    out = pl.pallas_call(
        body, name=name, grid=(m // tm, n // tn, nk), in_specs=in_specs, out_specs=o_spec, out_shape=out_arr,
        scratch_shapes=[pltpu.VMEM((tm, tn), F32)] if nk > 1 else [], compiler_params=_cparams(),
    )(*args)
    return Mat(out, 0, out_split)


def _rms_fwd(h, g, ts=512):
    s, d = h.shape
    ts = min(ts, s)

    def body(h_ref, g_ref, o_ref):
        x = h_ref[...]
        var = jnp.mean(x * x, axis=-1, keepdims=True)
        o_ref[...] = (x * lax.rsqrt(var + NORM_EPS) * g_ref[...]).astype(BF16)

    return pl.pallas_call(
        body, name="rms_fwd", grid=(s // ts,),
        in_specs=[pl.BlockSpec((ts, d), lambda i: (i, 0)), pl.BlockSpec((1, d), lambda i: (0, 0))],
        out_specs=pl.BlockSpec((ts, d), lambda i: (i, 0)), out_shape=jax.ShapeDtypeStruct((s, d), BF16),
        compiler_params=_cparams(),
    )(h, g)


def _rms_bwd(h, g, dhn, dh_in, ts=512):
    s, d = h.shape
    ts = min(ts, s)

    def body(h_ref, g_ref, dhn_ref, dhin_ref, dh_ref, dg_ref):
        i = pl.program_id(0)
        x = h_ref[...]
        rstd = lax.rsqrt(jnp.mean(x * x, axis=-1, keepdims=True) + NORM_EPS)
        xhat = x * rstd
        dhn_v = dhn_ref[...]
        dxh = dhn_v * g_ref[...]
        dh_ref[...] = dhin_ref[...] + rstd * (dxh - xhat * jnp.mean(dxh * xhat, axis=-1, keepdims=True))
        part = jnp.sum(dhn_v * xhat, axis=0, keepdims=True)

        @pl.when(i == 0)
        def _():
            dg_ref[...] = part

        @pl.when(i > 0)
        def _():
            dg_ref[...] += part

    row = pl.BlockSpec((ts, d), lambda i: (i, 0))
    vec = pl.BlockSpec((1, d), lambda i: (0, 0))
    return pl.pallas_call(
        body, name="rms_bwd", grid=(s // ts,), in_specs=[row, vec, row, row], out_specs=[row, vec],
        out_shape=[jax.ShapeDtypeStruct((s, d), F32), jax.ShapeDtypeStruct((1, d), F32)], compiler_params=_cparams(),
    )(h, g, dhn, dh_in)


def _loss_and_grad(h, g, tgt, ts=512):
    s, d = h.shape
    ts = min(ts, s)

    def body(h_ref, g_ref, t_ref, loss_ref, dh_ref, dg_ref):
        i = pl.program_id(0)
        x = h_ref[...]
        gv = g_ref[...]
        rstd = lax.rsqrt(jnp.mean(x * x, axis=-1, keepdims=True) + NORM_EPS)
        xhat = x * rstd
        err = xhat * gv - t_ref[...]
        dy = err * (1.0 / d)
        dxh = dy * gv
        dh_ref[...] = rstd * (dxh - xhat * jnp.mean(dxh * xhat, axis=-1, keepdims=True))
        part = jnp.sum(dy * xhat, axis=0, keepdims=True)
        lpart = jnp.broadcast_to(jnp.sum(jnp.sum(err * err, axis=0, keepdims=True), axis=1, keepdims=True) * (0.5 / d), (1, LANES))

        @pl.when(i == 0)
        def _():
            dg_ref[...] = part
            loss_ref[...] = lpart

        @pl.when(i > 0)
        def _():
            dg_ref[...] += part
            loss_ref[...] += lpart

    row = pl.BlockSpec((ts, d), lambda i: (i, 0))
    vec = pl.BlockSpec((1, d), lambda i: (0, 0))
    return pl.pallas_call(
        body, name="loss_and_grad", grid=(s // ts,), in_specs=[row, vec, row],
        out_specs=[pl.BlockSpec((1, LANES), lambda i: (0, 0)), row, vec],
        out_shape=[jax.ShapeDtypeStruct((1, LANES), F32), jax.ShapeDtypeStruct((s, d), F32), jax.ShapeDtypeStruct((1, d), F32)],
        compiler_params=_cparams(),
    )(h, g, tgt)


def _halo_before(ts, nrow8):
    return lambda i: jnp.maximum(i * (ts // SUBLANES) - 1, 0)


def _ffn_act(up2, conv_w2, conv_b2, ts=512, tn=512):
    _, s, f = up2.shape
    ts, tn = min(ts, s), min(tn, f)
    kw = FFN_CONV_W

    def body(up_ref, halo_ref, w_ref, b_ref, o_ref):
        i = pl.program_id(0)
        cs = []
        for h in range(2):
            x = up_ref[h]
            halo = jnp.where(i == 0, 0.0, halo_ref[h])
            c = b_ref[h] + w_ref[h, kw - 1:kw, :] * x
            for sft in range(1, kw):
                c = c + w_ref[h, kw - 1 - sft:kw - sft, :] * _shift_down(x, halo, sft)
            cs.append(c)
        o_ref[...] = (_gelu(cs[0]) * cs[1]).astype(BF16)

    hb = ts // SUBLANES
    return pl.pallas_call(
        body, name="ffn_act", grid=(s // ts, f // tn),
        in_specs=[pl.BlockSpec((2, ts, tn), lambda i, j: (0, i, j)),
                  pl.BlockSpec((2, SUBLANES, tn), lambda i, j: (0, jnp.maximum(i * hb - 1, 0), j)),
                  pl.BlockSpec((2, kw, tn), lambda i, j: (0, 0, j)),
                  pl.BlockSpec((2, 1, tn), lambda i, j: (0, 0, j))],
        out_specs=pl.BlockSpec((ts, tn), lambda i, j: (i, j)), out_shape=jax.ShapeDtypeStruct((s, f), BF16),
        compiler_params=_cparams(),
    )(up2, up2, conv_w2, conv_b2)


def _ffn_bwd(up2, dact, conv_w2, conv_b2, ts=256, tn=512):
    _, s, f = up2.shape
    ts, tn = min(ts, s), min(tn, f)
    kw = FFN_CONV_W
    nt = s // ts
    hb = ts // SUBLANES
    last8 = s // SUBLANES - 1

    def body(up_ref, hb_ref, ha_ref, da_ref, dah_ref, w_ref, b_ref, dup_ref, dw_ref, db_ref):
        i = pl.program_id(1)
        first, last = i == 0, i == nt - 1
        ce, xs = [], []
        for h in range(2):
            x = up_ref[h]
            before = jnp.where(first, 0.0, hb_ref[h])
            after = ha_ref[h]
            ext = jnp.concatenate([before, x, after], axis=0)
            c = b_ref[h] + w_ref[h, kw - 1:kw, :] * ext
            shifted = [ext]
            for sft in range(1, kw):
                sh = pltpu.roll(ext, sft, 0)
                shifted.append(sh)
                c = c + w_ref[h, kw - 1 - sft:kw - sft, :] * sh
            ce.append(c[SUBLANES:])
            xs.append([sh[SUBLANES:SUBLANES + ts] for sh in shifted])
        da = jnp.concatenate([da_ref[...], jnp.where(last, 0.0, dah_ref[...])], axis=0)
        g1, dg1 = _gelu_and_grad(ce[0])
        dcs = [da * ce[1] * dg1, da * g1]
        for h in range(2):
            dc = dcs[h]
            n = dc.shape[0]
            dup = w_ref[h, kw - 1:kw, :] * dc[:ts]
            for sft in range(1, kw):
                dup = dup + w_ref[h, kw - 1 - sft:kw - sft, :] * pltpu.roll(dc, n - sft, 0)[:ts]
            dup_ref[h] = dup.astype(BF16)
            dct = dc[:ts]
            dbp = jnp.sum(dct, axis=0, keepdims=True)
            dwp = [jnp.sum(dct * xs[h][kw - 1 - k], axis=0, keepdims=True) for k in range(kw)]

            @pl.when(first)
            def _():
                db_ref[h] = dbp
                for k in range(kw):
                    dw_ref[h, k:k + 1, :] = dwp[k]

            @pl.when(i > 0)
            def _():
                db_ref[h] += dbp
                for k in range(kw):
                    dw_ref[h, k:k + 1, :] += dwp[k]

    return pl.pallas_call(
        body, name="ffn_bwd", grid=(f // tn, nt),
        in_specs=[pl.BlockSpec((2, ts, tn), lambda j, i: (0, i, j)),
                  pl.BlockSpec((2, SUBLANES, tn), lambda j, i: (0, jnp.maximum(i * hb - 1, 0), j)),
                  pl.BlockSpec((2, SUBLANES, tn), lambda j, i: (0, jnp.minimum((i + 1) * hb, last8), j)),
                  pl.BlockSpec((ts, tn), lambda j, i: (i, j)),
                  pl.BlockSpec((SUBLANES, tn), lambda j, i: (jnp.minimum((i + 1) * hb, last8), j)),
                  pl.BlockSpec((2, kw, tn), lambda j, i: (0, 0, j)),
                  pl.BlockSpec((2, 1, tn), lambda j, i: (0, 0, j))],
        out_specs=[pl.BlockSpec((2, ts, tn), lambda j, i: (0, i, j)),
                   pl.BlockSpec((2, kw, tn), lambda j, i: (0, 0, j)),
                   pl.BlockSpec((2, 1, tn), lambda j, i: (0, 0, j))],
        out_shape=[jax.ShapeDtypeStruct((2, s, f), BF16), jax.ShapeDtypeStruct((2, kw, f), F32),
                   jax.ShapeDtypeStruct((2, 1, f), F32)],
        compiler_params=_cparams(),
    )(up2, up2, up2, dact, dact, conv_w2, conv_b2)


def _rg_gates(xr, wa_ref, ba_ref, wx_ref, bx_ref, lam_ref):
    bw = wa_ref.shape[-1]
    xb = xr.astype(BF16)
    za = jnp.concatenate([jnp.dot(xb[:, h * bw:(h + 1) * bw], wa_ref[h], preferred_element_type=F32)
                          for h in range(RG_HEADS)], axis=1) + ba_ref[...]
    zx = jnp.concatenate([jnp.dot(xb[:, h * bw:(h + 1) * bw], wx_ref[h], preferred_element_type=F32)
                          for h in range(RG_HEADS)], axis=1) + bx_ref[...]
    r, ig = _sigmoid(za), _sigmoid(zx)
    sp = _softplus(-lam_ref[...])
    la = -RG_C * r * sp
    a = jnp.exp(la)
    mult = jnp.sqrt(_neg_expm1(2.0 * la))
    return xb, r, ig, sp, a, mult


def _rg_fwd(xg2, conv_w, conv_b, w_a, b_a, w_x, b_x, lam, ts=256):
    _, s, c = xg2.shape
    ts = min(ts, s)
    kw = RG_CONV_W
    hb = ts // SUBLANES

    def body(xg_ref, halo_ref, cw_ref, cb_ref, wa_ref, ba_ref, wx_ref, bx_ref, lam_ref, xr_ref, hs_ref, y_ref, carry_ref):
        i = pl.program_id(0)

        @pl.when(i == 0)
        def _():
            carry_ref[...] = jnp.zeros_like(carry_ref)

        xp = xg_ref[0]
        halo = jnp.where(i == 0, 0.0, halo_ref[...])
        xr = cb_ref[...] + cw_ref[kw - 1:kw, :] * xp
        for sft in range(1, kw):
            xr = xr + cw_ref[kw - 1 - sft:kw - sft, :] * _shift_down(xp, halo, sft)
        _, r, ig, sp, a, mult = _rg_gates(xr, wa_ref, ba_ref, wx_ref, bx_ref, lam_ref)
        bt = mult * (ig * xr)
        row = _rows(bt.shape)
        bt = bt + jnp.where(row == 0, a * carry_ref[SUBLANES - 1:SUBLANES, :], 0.0)
        hs = _scan_real_fwd(a, bt)
        carry_ref[...] = hs[ts - SUBLANES:, :]
        xr_ref[...] = xr
        hs_ref[...] = hs
        y_ref[...] = (hs * _gelu(xg_ref[1])).astype(BF16)

    full = lambda shape: pl.BlockSpec(shape, lambda i: (0,) * len(shape))
    row_spec = pl.BlockSpec((ts, c), lambda i: (i, 0))
    return pl.pallas_call(
        body, name="rg_fwd", grid=(s // ts,),
        in_specs=[pl.BlockSpec((2, ts, c), lambda i: (0, i, 0)),
                  pl.BlockSpec((None, SUBLANES, c), lambda i: (0, jnp.maximum(i * hb - 1, 0), 0)),
                  full(conv_w.shape), full(conv_b.shape), full(w_a.shape), full(b_a.shape), full(w_x.shape), full(b_x.shape),
                  full(lam.shape)],
        out_specs=[row_spec, row_spec, row_spec],
        out_shape=[jax.ShapeDtypeStruct((s, c), F32), jax.ShapeDtypeStruct((s, c), F32), jax.ShapeDtypeStruct((s, c), BF16)],
        scratch_shapes=[pltpu.VMEM((SUBLANES, c), F32)], compiler_params=_cparams(),
    )(xg2, xg2, conv_w, conv_b, w_a, b_a, w_x, b_x, lam)


def _rg_bwd(dy, xg2, xr, hs, conv_w, w_a, b_a, w_x, b_x, lam, ts=256):
    _, s, c = xg2.shape
    ts = min(ts, s)
    nt = s // ts
    kw = RG_CONV_W
    hb = ts // SUBLANES
    bw = c // RG_HEADS
    tn_dims = (((0,), (0,)), ((), ()))
    nt_dims = (((1,), (1,)), ((), ()))

    def body(dy_ref, xg_ref, xph_ref, xr_ref, hs_ref, hsh_ref, cw_ref, wa_ref, ba_ref, wx_ref, bx_ref, lam_ref,
             dxg_ref, dcw_ref, dcb_ref, dwa_ref, dba_ref, dwx_ref, dbx_ref, dlam_ref,
             lam_carry, a_carry, dxr_carry, dsp_acc):
        i = pl.program_id(0)
        first_step = i == 0
        time_first = i == nt - 1

        @pl.when(first_step)
        def _():
            lam_carry[...] = jnp.zeros_like(lam_carry)
            a_carry[...] = jnp.ones_like(a_carry)
            dxr_carry[...] = jnp.zeros_like(dxr_carry)
            dsp_acc[...] = jnp.zeros_like(dsp_acc)
            for ref in (dcw_ref, dcb_ref, dwa_ref, dba_ref, dwx_ref, dbx_ref):
                ref[...] = jnp.zeros_like(ref)

        xr = xr_ref[...]
        hs = hs_ref[...]
        gate = xg_ref[1]
        xb, r, ig, sp, a, mult = _rg_gates(xr, wa_ref, ba_ref, wx_ref, bx_ref, lam_ref)
        dyv = dy_ref[...]
        gg, dgg = _gelu_and_grad(gate)
        dhs = dyv * gg
        dxg_ref[1] = (dyv * hs * dgg).astype(BF16)
        row = _rows(xr.shape)
        coef = jnp.where(row == ts - 1, a_carry[0:1, :], pltpu.roll(a, ts - 1, 0))
        dhs = dhs + jnp.where(row == ts - 1, coef * lam_carry[0:1, :], 0.0)
        lmb = _scan_real_rev(coef, dhs)
        lam_carry[...] = lmb[:SUBLANES]
        a_carry[...] = a[:SUBLANES]
        hs_prev = _shift_down(hs, jnp.where(time_first, 0.0, hsh_ref[...]), 1)
        d_a = lmb * hs_prev
        d_m = lmb * (ig * xr)
        d_ig = lmb * mult * xr
        d_xr = lmb * mult * ig
        d_la = a * d_a - (a * a / mult) * d_m
        dsp_acc[...] += jnp.sum(-RG_C * r * d_la, axis=0, keepdims=True)
        d_za = (-RG_C * sp) * d_la * r * (1.0 - r)
        d_zx = d_ig * ig * (1.0 - ig)
        dba_ref[...] += jnp.sum(d_za, axis=0, keepdims=True)
        dbx_ref[...] += jnp.sum(d_zx, axis=0, keepdims=True)
        dzab, dzxb = d_za.astype(BF16), d_zx.astype(BF16)
        back = []
        for h in range(RG_HEADS):
            sl = slice(h * bw, (h + 1) * bw)
            dwa_ref[h] += lax.dot_general(xb[:, sl], dzab[:, sl], tn_dims, preferred_element_type=F32)
            dwx_ref[h] += lax.dot_general(xb[:, sl], dzxb[:, sl], tn_dims, preferred_element_type=F32)
            back.append(lax.dot_general(dzab[:, sl], wa_ref[h], nt_dims, preferred_element_type=F32)
                        + lax.dot_general(dzxb[:, sl], wx_ref[h], nt_dims, preferred_element_type=F32))
        d_xr = d_xr + jnp.concatenate(back, axis=1)
        d_xp = cw_ref[kw - 1:kw, :] * d_xr
        after = dxr_carry[...]
        for sft in range(1, kw):
            d_xp = d_xp + cw_ref[kw - 1 - sft:kw - sft, :] * _shift_up(d_xr, after, sft)
        dxr_carry[...] = d_xr[:SUBLANES]
        dxg_ref[0] = d_xp.astype(BF16)
        xp = xg_ref[0]
        before = jnp.where(time_first, 0.0, xph_ref[...])
        dcb_ref[...] += jnp.sum(d_xr, axis=0, keepdims=True)
        dcw_ref[kw - 1:kw, :] += jnp.sum(d_xr * xp, axis=0, keepdims=True)
        for sft in range(1, kw):
            dcw_ref[kw - 1 - sft:kw - sft, :] += jnp.sum(d_xr * _shift_down(xp, before, sft), axis=0, keepdims=True)
        dlam_ref[...] = dsp_acc[...] * (-_sigmoid(-lam_ref[...]))

    full = lambda shape: pl.BlockSpec(shape, lambda i: (0,) * len(shape))
    rev = lambda i: nt - 1 - i
    row_spec = pl.BlockSpec((ts, c), lambda i: (rev(i), 0))
    halo_idx = lambda i: jnp.maximum(rev(i) * hb - 1, 0)
    vec = (1, c)
    return pl.pallas_call(
        body, name="rg_bwd", grid=(nt,),
        in_specs=[row_spec,
                  pl.BlockSpec((2, ts, c), lambda i: (0, rev(i), 0)),
                  pl.BlockSpec((None, SUBLANES, c), lambda i: (0, halo_idx(i), 0)),
                  row_spec, row_spec,
                  pl.BlockSpec((SUBLANES, c), lambda i: (halo_idx(i), 0)),
                  full(conv_w.shape), full(w_a.shape), full(b_a.shape), full(w_x.shape), full(b_x.shape), full(lam.shape)],
        out_specs=[pl.BlockSpec((2, ts, c), lambda i: (0, rev(i), 0)), full(conv_w.shape), full(vec), full(w_a.shape), full(vec),
                   full(w_x.shape), full(vec), full(vec)],
        out_shape=[jax.ShapeDtypeStruct((2, s, c), BF16), jax.ShapeDtypeStruct(conv_w.shape, F32), jax.ShapeDtypeStruct(vec, F32),
                   jax.ShapeDtypeStruct(w_a.shape, F32), jax.ShapeDtypeStruct(vec, F32), jax.ShapeDtypeStruct(w_x.shape, F32),
                   jax.ShapeDtypeStruct(vec, F32), jax.ShapeDtypeStruct(vec, F32)],
        scratch_shapes=[pltpu.VMEM((SUBLANES, c), F32), pltpu.VMEM((SUBLANES, c), F32), pltpu.VMEM((SUBLANES, c), F32),
                        pltpu.VMEM(vec, F32)],
        compiler_params=_cparams(),
    )(dy, xg2, xg2, xr, hs, hs, conv_w, w_a, b_a, w_x, b_x, lam)


def _s5_param_fn(a_re, a_im, log_dt, bt_re, bt_im):
    dt = jnp.exp(log_dt)
    mag = jnp.exp(a_re * dt)
    abr = mag * jnp.cos(a_im * dt)
    abi = mag * jnp.sin(a_im * dt)
    ur, ui = abr - 1.0, abi
    den = a_re * a_re + a_im * a_im
    wr = (ur * a_re + ui * a_im) / den
    wi = (ui * a_re - ur * a_im) / den
    bbr = wr[None] * bt_re - wi[None] * bt_im
    bbi = wr[None] * bt_im + wi[None] * bt_re
    return abr, abi, bbr, bbi


def _s5_params(a_re, a_im, log_dt, bt_re, bt_im, nlev):
    g, p = a_re.shape
    gc = bt_re.shape[0]

    def body(ar_ref, ai_ref, dt_ref, br_ref, bi_ref, abr_ref, abi_ref, pr_ref, pi_ref, bbr_ref, bbi_ref):
        abr, abi, bbr, bbi = _s5_param_fn(ar_ref[...], ai_ref[...], dt_ref[...], br_ref[...], bi_ref[...])
        abr_ref[...] = abr
        abi_ref[...] = abi
        bbr_ref[...] = bbr
        bbi_ref[...] = bbi
        qr, qi = abr, abi
        for k in range(nlev):
            pr_ref[k] = qr
            pi_ref[k] = qi
            qr, qi = qr * qr - qi * qi, 2.0 * qr * qi

    sd = jax.ShapeDtypeStruct
    return pl.pallas_call(
        body, name="s5_params",
        out_shape=[sd((g, p), F32), sd((g, p), F32), sd((nlev, g, p), F32), sd((nlev, g, p), F32), sd((gc, g, p), F32),
                   sd((gc, g, p), F32)],
    )(a_re, a_im, log_dt, bt_re, bt_im)


def _s5_params_bwd(a_re, a_im, log_dt, bt_re, bt_im, d_abr, d_abi, d_bbr, d_bbi):
    def body(ar_ref, ai_ref, dt_ref, br_ref, bi_ref, g0, g1, g2, g3, o0, o1, o2, o3, o4):
        _, vjp = jax.vjp(_s5_param_fn, ar_ref[...], ai_ref[...], dt_ref[...], br_ref[...], bi_ref[...])
        outs = vjp((g0[...], g1[...], g2[...], g3[...]))
        for o, v in zip((o0, o1, o2, o3, o4), outs):
            o[...] = v

    sd = jax.ShapeDtypeStruct
    return pl.pallas_call(
        body, name="s5_params_bwd",
        out_shape=[sd(a_re.shape, F32), sd(a_im.shape, F32), sd(log_dt.shape, F32), sd(bt_re.shape, F32), sd(bt_im.shape, F32)],
    )(a_re, a_im, log_dt, bt_re, bt_im, d_abr, d_abi, d_bbr, d_bbi)


def _s5_fwd(u, abr, abi, pw_r, pw_i, bp_r, bp_i, cp_r, cp_i, dvec, ts=128):
    s, c = u.shape
    n = abr.shape[1]
    nblk, cb, nb = bp_r.shape
    ts = min(ts, s)

    def body(u_ref, ar_ref, ai_ref, pr_ref, pi_ref, bpr_ref, bpi_ref, cpr_ref, cpi_ref, d_ref,
             hr_ref, hi_ref, yp_ref, gy_ref, car_r, car_i):
        i = pl.program_id(0)

        @pl.when(i == 0)
        def _():
            car_r[...] = jnp.zeros_like(car_r)
            car_i[...] = jnp.zeros_like(car_i)

        uv = u_ref[...]
        ub = uv.astype(BF16)
        br = jnp.concatenate([jnp.dot(ub[:, k * cb:(k + 1) * cb], bpr_ref[k], preferred_element_type=F32) for k in range(nblk)], axis=1)
        bi = jnp.concatenate([jnp.dot(ub[:, k * cb:(k + 1) * cb], bpi_ref[k], preferred_element_type=F32) for k in range(nblk)], axis=1)
        ar, ai = ar_ref[...], ai_ref[...]
        pr, pi_ = car_r[SUBLANES - 1:SUBLANES, :], car_i[SUBLANES - 1:SUBLANES, :]
        row = _rows(br.shape)
        br = br + jnp.where(row == 0, ar * pr - ai * pi_, 0.0)
        bi = bi + jnp.where(row == 0, ar * pi_ + ai * pr, 0.0)
        hr, hi = _scan_cplx(br, bi, pr_ref, pi_ref, reverse=False)
        car_r[...] = hr[ts - SUBLANES:]
        car_i[...] = hi[ts - SUBLANES:]
        hr_ref[...] = hr
        hi_ref[...] = hi
        hrb, hib = hr.astype(BF16), hi.astype(BF16)
        y = jnp.concatenate([jnp.dot(hrb[:, k * nb:(k + 1) * nb], cpr_ref[k], preferred_element_type=F32)
                             - jnp.dot(hib[:, k * nb:(k + 1) * nb], cpi_ref[k], preferred_element_type=F32) for k in range(nblk)], axis=1)
        yp = y + d_ref[...] * uv
        yp_ref[...] = yp
        gy_ref[...] = _gelu(yp).astype(BF16)

    full = lambda shape: pl.BlockSpec(shape, lambda i: (0,) * len(shape))
    rc = pl.BlockSpec((ts, c), lambda i: (i, 0))
    rn = pl.BlockSpec((ts, n), lambda i: (i, 0))
    sd = jax.ShapeDtypeStruct
    return pl.pallas_call(
        body, name="s5_fwd", grid=(s // ts,),
        in_specs=[rc, full(abr.shape), full(abi.shape), full(pw_r.shape), full(pw_i.shape), full(bp_r.shape), full(bp_i.shape),
                  full(cp_r.shape), full(cp_i.shape), full(dvec.shape)],
        out_specs=[rn, rn, rc, rc],
        out_shape=[sd((s, n), F32), sd((s, n), F32), sd((s, c), F32), sd((s, c), BF16)],
        scratch_shapes=[pltpu.VMEM((SUBLANES, n), F32), pltpu.VMEM((SUBLANES, n), F32)], compiler_params=_cparams(),
    )(u, abr, abi, pw_r, pw_i, bp_r, bp_i, cp_r, cp_i, dvec)


def _s5_bwd(dgy, ypre, u, hr, hi, abr, abi, pw_r, pw_i, bp_r, bp_i, cp_r, cp_i, dvec, ts=128):
    s, c = u.shape
    n = abr.shape[1]
    nblk, cb, nb = bp_r.shape
    ts = min(ts, s)
    nt = s // ts
    hb = ts // SUBLANES
    tn_dims = (((0,), (0,)), ((), ()))
    nt_dims = (((1,), (1,)), ((), ()))

    def body(dgy_ref, yp_ref, u_ref, hr_ref, hi_ref, hrh_ref, hih_ref, ar_ref, ai_ref, pr_ref, pi_ref, bpr_ref, bpi_ref,
             cpr_ref, cpi_ref, d_ref,
             du_ref, dar_ref, dai_ref, dbr_ref, dbi_ref, dcr_ref, dci_ref, dd_ref, car_r, car_i, npi_ref):
        i = pl.program_id(0)
        time_first = i == nt - 1

        @pl.when(i == 0)
        def _():
            car_r[...] = jnp.zeros_like(car_r)
            car_i[...] = jnp.zeros_like(car_i)
            npi_ref[...] = -pi_ref[...]
            for ref in (dar_ref, dai_ref, dbr_ref, dbi_ref, dcr_ref, dci_ref, dd_ref):
                ref[...] = jnp.zeros_like(ref)

        uv = u_ref[...]
        _, dgel = _gelu_and_grad(yp_ref[...])
        dyv = dgy_ref[...] * dgel
        dd_ref[...] += jnp.sum(dyv * uv, axis=0, keepdims=True)
        dyb = dyv.astype(BF16)
        hr, hi = hr_ref[...], hi_ref[...]
        hrb, hib = hr.astype(BF16), hi.astype(BF16)
        dhr, dhi = [], []
        for k in range(nblk):
            dblk = dyb[:, k * cb:(k + 1) * cb]
            dhr.append(lax.dot_general(dblk, cpr_ref[k], nt_dims, preferred_element_type=F32))
            dhi.append(-lax.dot_general(dblk, cpi_ref[k], nt_dims, preferred_element_type=F32))
            dcr_ref[k] += lax.dot_general(hrb[:, k * nb:(k + 1) * nb], dblk, tn_dims, preferred_element_type=F32)
            dci_ref[k] += lax.dot_general(hib[:, k * nb:(k + 1) * nb], dblk, tn_dims, preferred_element_type=F32)
        dhr = jnp.concatenate(dhr, axis=1)
        dhi = jnp.concatenate(dhi, axis=1)
        ar, ai = ar_ref[...], ai_ref[...]
        nr, ni = car_r[0:1, :], car_i[0:1, :]
        row = _rows(dhr.shape)
        dhr = dhr + jnp.where(row == ts - 1, ar * nr + ai * ni, 0.0)
        dhi = dhi + jnp.where(row == ts - 1, ar * ni - ai * nr, 0.0)
        lr, li = _scan_cplx(dhr, dhi, pr_ref, npi_ref, reverse=True)
        car_r[...] = lr[:SUBLANES]
        car_i[...] = li[:SUBLANES]
        hpr = _shift_down(hr, jnp.where(time_first, 0.0, hrh_ref[...]), 1)
        hpi = _shift_down(hi, jnp.where(time_first, 0.0, hih_ref[...]), 1)
        dar_ref[...] += jnp.sum(lr * hpr + li * hpi, axis=0, keepdims=True)
        dai_ref[...] += jnp.sum(li * hpr - lr * hpi, axis=0, keepdims=True)
        lrb, lib = lr.astype(BF16), li.astype(BF16)
        ub = uv.astype(BF16)
        du = []
        for k in range(nblk):
            ublk = ub[:, k * cb:(k + 1) * cb]
            lrk, lik = lrb[:, k * nb:(k + 1) * nb], lib[:, k * nb:(k + 1) * nb]
            dbr_ref[k] += lax.dot_general(ublk, lrk, tn_dims, preferred_element_type=F32)
            dbi_ref[k] += lax.dot_general(ublk, lik, tn_dims, preferred_element_type=F32)
            du.append(lax.dot_general(lrk, bpr_ref[k], nt_dims, preferred_element_type=F32)
                      + lax.dot_general(lik, bpi_ref[k], nt_dims, preferred_element_type=F32))
        du_ref[...] = (d_ref[...] * dyv + jnp.concatenate(du, axis=1)).astype(BF16)

    full = lambda shape: pl.BlockSpec(shape, lambda i: (0,) * len(shape))
    rev = lambda i: nt - 1 - i
    halo_idx = lambda i: jnp.maximum(rev(i) * hb - 1, 0)
    rc = pl.BlockSpec((ts, c), lambda i: (rev(i), 0))
    rn = pl.BlockSpec((ts, n), lambda i: (rev(i), 0))
    hn = pl.BlockSpec((SUBLANES, n), lambda i: (halo_idx(i), 0))
    sd = jax.ShapeDtypeStruct
    return pl.pallas_call(
        body, name="s5_bwd", grid=(nt,),
        in_specs=[rc, rc, rc, rn, rn, hn, hn, full(abr.shape), full(abi.shape), full(pw_r.shape), full(pw_i.shape),
                  full(bp_r.shape), full(bp_i.shape), full(cp_r.shape), full(cp_i.shape), full(dvec.shape)],
        out_specs=[rc, full(abr.shape), full(abi.shape), full(bp_r.shape), full(bp_i.shape), full(cp_r.shape), full(cp_i.shape),
                   full(dvec.shape)],
        out_shape=[sd((s, c), BF16), sd(abr.shape, F32), sd(abi.shape, F32), sd(bp_r.shape, F32), sd(bp_i.shape, F32),
                   sd(cp_r.shape, F32), sd(cp_i.shape, F32), sd(dvec.shape, F32)],
        scratch_shapes=[pltpu.VMEM((SUBLANES, n), F32), pltpu.VMEM((SUBLANES, n), F32), pltpu.VMEM(pw_i.shape, F32)],
        compiler_params=_cparams(),
    )(dgy, ypre, u, hr, hi, hr, hi, abr, abi, pw_r, pw_i, bp_r, bp_i, cp_r, cp_i, dvec)


def _glu(gl2, ts=512):
    _, s, c = gl2.shape
    ts = min(ts, s)

    def body(g_ref, o_ref):
        o_ref[...] = (g_ref[0] * _sigmoid(g_ref[1])).astype(BF16)

    return pl.pallas_call(
        body, name="glu", grid=(s // ts,), in_specs=[pl.BlockSpec((2, ts, c), lambda i: (0, i, 0))],
        out_specs=pl.BlockSpec((ts, c), lambda i: (i, 0)), out_shape=jax.ShapeDtypeStruct((s, c), BF16), compiler_params=_cparams(),
    )(gl2)


def _glu_bwd(gl2, d_o, ts=512):
    _, s, c = gl2.shape
    ts = min(ts, s)

    def body(g_ref, do_ref, o_ref):
        sg = _sigmoid(g_ref[1])
        dov = do_ref[...]
        o_ref[0] = (dov * sg).astype(BF16)
        o_ref[1] = (dov * g_ref[0] * sg * (1.0 - sg)).astype(BF16)

    blk = pl.BlockSpec((2, ts, c), lambda i: (0, i, 0))
    return pl.pallas_call(
        body, name="glu_bwd", grid=(s // ts,), in_specs=[blk, pl.BlockSpec((ts, c), lambda i: (i, 0))],
        out_specs=blk, out_shape=jax.ShapeDtypeStruct((2, s, c), BF16), compiler_params=_cparams(),
    )(gl2, d_o)


def _row_tile(rows, pref=512):
    if rows <= pref:
        return rows
    t = pref
    while rows % t:
        t //= 2
    assert t >= SUBLANES, rows
    return t


def _sum_parts(r):
    p, rows, cols = r.shape
    tr = _row_tile(rows, 128)

    def body(r_ref, o_ref):
        acc = r_ref[0].astype(F32)
        for k in range(1, p):
            acc = acc + r_ref[k].astype(F32)
        o_ref[...] = acc

    return pl.pallas_call(
        body, name="sum_parts", grid=(rows // tr,), in_specs=[pl.BlockSpec((p, tr, cols), lambda i: (0, i, 0))],
        out_specs=pl.BlockSpec((tr, cols), lambda i: (i, 0)), out_shape=jax.ShapeDtypeStruct((rows, cols), F32),
        compiler_params=_cparams(),
    )(r)


def _adamw(w, g_parts, m, v):
    rows, cols = w.shape
    tr = _row_tile(rows, 128)
    ng = len(g_parts)
    c1 = 1.0 / (1.0 - ADAM_B1 ** ADAM_STEP)
    c2 = 1.0 / (1.0 - ADAM_B2 ** ADAM_STEP)

    def body(*refs):
        w_ref, m_ref, v_ref = refs[0], refs[1 + ng], refs[2 + ng]
        g_ref, dl_ref, nm_ref, nv_ref = refs[3 + ng:]
        g = refs[1][...]
        for k in range(1, ng):
            g = g + refs[1 + k][...]
        mn = ADAM_B1 * m_ref[...] + (1.0 - ADAM_B1) * g
        vn = ADAM_B2 * v_ref[...] + (1.0 - ADAM_B2) * (g * g)
        g_ref[...] = g
        nm_ref[...] = mn
        nv_ref[...] = vn
        dl_ref[...] = -ADAM_LR * ((mn * c1) / (jnp.sqrt(vn * c2) + ADAM_EPS) + ADAM_WD * w_ref[...])

    blk = pl.BlockSpec((tr, cols), lambda i: (i, 0))
    sd = jax.ShapeDtypeStruct((rows, cols), F32)
    return pl.pallas_call(
        body, name="adamw", grid=(rows // tr,), in_specs=[blk] * (3 + ng), out_specs=[blk] * 4, out_shape=[sd] * 4,
        compiler_params=_cparams(),
    )(w, *g_parts, m, v)


def _place():
    x, y, c = lax.axis_index("x"), lax.axis_index("y"), lax.axis_index("c")
    chips = [(1 - x, y), (x, 1 - y), (1 - x, 1 - y)]
    return x, y, c, chips


def _gather_shards(shards, layer_major):
    n = len(shards)

    def body(*refs):
        ins, outs = refs[:n], refs[n:2 * n]
        send, recv, lsem = refs[2 * n:]
        x, y, c, chips = _place()
        me = 2 * x + y

        def slot(t, chip):
            return outs[t].at[:, chip] if layer_major[t] else outs[t].at[chip]

        local, sends = [], []
        for t in range(n):
            cp = pltpu.make_async_copy(ins[t], slot(t, me), lsem.at[t])
            cp.start()
            local.append(cp)
            for r, (px, py) in enumerate(chips):
                rc = pltpu.make_async_remote_copy(src_ref=ins[t], dst_ref=slot(t, me), send_sem=send.at[3 * t + r],
                                                  recv_sem=recv.at[3 * t + r], device_id=(px, py, c), device_id_type=MESH)
                rc.start()
                sends.append(rc)
        for t in range(n):
            for r, (px, py) in enumerate(chips):
                pltpu.make_async_remote_copy(src_ref=ins[t], dst_ref=slot(t, 2 * px + py), send_sem=send.at[3 * t + r],
                                             recv_sem=recv.at[3 * t + r], device_id=(px, py, c), device_id_type=MESH).wait_recv()
        for rc in sends:
            rc.wait_send()
        for cp in local:
            cp.wait()

    any_spec = pl.BlockSpec(memory_space=pl.ANY)
    return pl.pallas_call(
        body, name="gather_shards", in_specs=[any_spec] * n, out_specs=[any_spec] * n,
        out_shape=[jax.ShapeDtypeStruct((s.shape[0], N_CHIPS) + s.shape[1:] if lm else (N_CHIPS,) + s.shape, s.dtype)
                   for s, lm in zip(shards, layer_major)],
        scratch_shapes=[pltpu.SemaphoreType.DMA((3 * n,)), pltpu.SemaphoreType.DMA((3 * n,)), pltpu.SemaphoreType.DMA((n,))],
    )(*shards)


def _scatter_grads(groups):
    flat = [(gi, li, a) for gi, grp in enumerate(groups) for li, a in enumerate(grp)]
    n = len(flat)
    ng = len(groups)

    def body(*refs):
        ins, outs = refs[:n], refs[n:n + ng]
        send, recv, lsem = refs[n + ng:]
        x, y, c, chips = _place()
        me = 2 * x + y
        local, sends = [], []
        for t, (gi, li, _) in enumerate(flat):
            cp = pltpu.make_async_copy(ins[t].at[me], outs[gi].at[me, li], lsem.at[t])
            cp.start()
            local.append(cp)
            for r, (px, py) in enumerate(chips):
                rc = pltpu.make_async_remote_copy(src_ref=ins[t].at[2 * px + py], dst_ref=outs[gi].at[me, li],
                                                  send_sem=send.at[3 * t + r], recv_sem=recv.at[3 * t + r],
                                                  device_id=(px, py, c), device_id_type=MESH)
                rc.start()
                sends.append(rc)
        for t, (gi, li, _) in enumerate(flat):
            for r, (px, py) in enumerate(chips):
                pltpu.make_async_remote_copy(src_ref=ins[t].at[me], dst_ref=outs[gi].at[2 * px + py, li],
                                             send_sem=send.at[3 * t + r], recv_sem=recv.at[3 * t + r],
                                             device_id=(px, py, c), device_id_type=MESH).wait_recv()
        for rc in sends:
            rc.wait_send()
        for cp in local:
            cp.wait()

    any_spec = pl.BlockSpec(memory_space=pl.ANY)
    return pl.pallas_call(
        body, name="scatter_grads", in_specs=[any_spec] * n, out_specs=[any_spec] * ng,
        out_shape=[jax.ShapeDtypeStruct((N_CHIPS, len(grp)) + grp[0].shape[1:], grp[0].dtype) for grp in groups],
        scratch_shapes=[pltpu.SemaphoreType.DMA((3 * n,)), pltpu.SemaphoreType.DMA((3 * n,)), pltpu.SemaphoreType.DMA((n,))],
    )(*[a for _, _, a in flat])


def _swap_with_sibling(arrs):
    n = len(arrs)

    def body(*refs):
        ins, outs = refs[:n], refs[n:2 * n]
        send, recv = refs[2 * n:]
        x, y, c, _ = _place()
        cps = []
        for t in range(n):
            rc = pltpu.make_async_remote_copy(src_ref=ins[t], dst_ref=outs[t], send_sem=send.at[t], recv_sem=recv.at[t],
                                              device_id=(x, y, 1 - c), device_id_type=MESH)
            rc.start()
            cps.append(rc)
        for rc in cps:
            rc.wait_recv()
        for rc in cps:
            rc.wait_send()

    any_spec = pl.BlockSpec(memory_space=pl.ANY)
    return pl.pallas_call(
        body, name="swap_with_sibling", in_specs=[any_spec] * n, out_specs=[any_spec] * n,
        out_shape=[jax.ShapeDtypeStruct(a.shape, a.dtype) for a in arrs],
        scratch_shapes=[pltpu.SemaphoreType.DMA((n,)), pltpu.SemaphoreType.DMA((n,))],
    )(*arrs)


def _allreduce_small(v):
    rows, cols = v.shape

    def body(v_ref, o_ref, sib_ref, chip_ref, send, recv):
        x, y, c, chips = _place()
        me = 2 * x + y
        d2d = pltpu.make_async_remote_copy(src_ref=v_ref, dst_ref=sib_ref, send_sem=send.at[0], recv_sem=recv.at[0],
                                           device_id=(x, y, 1 - c), device_id_type=MESH)
        d2d.start()
        d2d.wait_recv()
        chip_ref[me] = v_ref[...] + sib_ref[...]
        sends = []
        for r, (px, py) in enumerate(chips):
            rc = pltpu.make_async_remote_copy(src_ref=chip_ref.at[me], dst_ref=chip_ref.at[me], send_sem=send.at[1 + r],
                                              recv_sem=recv.at[1 + r], device_id=(px, py, c), device_id_type=MESH)
            rc.start()
            sends.append(rc)
        for r, (px, py) in enumerate(chips):
            pltpu.make_async_remote_copy(src_ref=chip_ref.at[me], dst_ref=chip_ref.at[2 * px + py], send_sem=send.at[1 + r],
                                         recv_sem=recv.at[1 + r], device_id=(px, py, c), device_id_type=MESH).wait_recv()
        o_ref[...] = (chip_ref[0] + chip_ref[1]) + (chip_ref[2] + chip_ref[3])
        d2d.wait_send()
        for rc in sends:
            rc.wait_send()

    vm = pl.BlockSpec(memory_space=pltpu.VMEM)
    return pl.pallas_call(
        body, name="allreduce_small", in_specs=[vm], out_specs=vm, out_shape=jax.ShapeDtypeStruct((rows, cols), F32),
        scratch_shapes=[pltpu.VMEM((rows, cols), F32), pltpu.VMEM((N_CHIPS, rows, cols), F32), pltpu.SemaphoreType.DMA((4,)),
                        pltpu.SemaphoreType.DMA((4,))],
        compiler_params=_cparams(),
    )(v)


def _pack(tensors):
    pieces = []
    for t in tensors:
        flat = t.reshape(-1)
        pad = (-flat.shape[0]) % (SUBLANES * LANES)
        pieces.append(jnp.pad(flat, (0, pad)).reshape(-1, LANES))
    return jnp.concatenate(pieces, axis=0)


def _unpack(buf, like):
    out, off = [], 0
    for t in like:
        size = math.prod(t.shape)
        rows = -(-size // (SUBLANES * LANES)) * SUBLANES
        out.append(buf[off:off + rows].reshape(-1)[:size].reshape(t.shape))
        off += rows
    return out


def _s5_pack_b(bb):
    gc, g, p = bb.shape
    q = S5_GROUPS_PER_BLOCK
    t = bb.reshape(gc, g // q, q, p).transpose(1, 2, 0, 3)
    eye = jnp.eye(q, dtype=bb.dtype)
    return (t[:, :, :, None, :] * eye[None, :, None, :, None]).reshape(g // q, q * gc, q * p)


def _s5_unpack_b(dbp, gc, p):
    nb = dbp.shape[0]
    q = S5_GROUPS_PER_BLOCK
    eye = jnp.eye(q, dtype=dbp.dtype)
    t = (dbp.reshape(nb, q, gc, q, p) * eye[None, :, None, :, None]).sum(axis=3)
    return t.transpose(2, 0, 1, 3).reshape(gc, nb * q, p)


def _s5_pack_c(cc):
    g, gc, p = cc.shape
    q = S5_GROUPS_PER_BLOCK
    t = cc.reshape(g // q, q, gc, p).transpose(0, 1, 3, 2)
    eye = jnp.eye(q, dtype=cc.dtype)
    return (t[:, :, :, None, :] * eye[None, :, None, :, None]).reshape(g // q, q * p, q * gc)


def _s5_unpack_c(dcp, gc, p):
    nb = dcp.shape[0]
    q = S5_GROUPS_PER_BLOCK
    eye = jnp.eye(q, dtype=dcp.dtype)
    t = (dcp.reshape(nb, q, p, q, gc) * eye[None, :, None, :, None]).sum(axis=3)
    return t.transpose(0, 1, 3, 2).reshape(nb * q, gc, p)


def _split2(m):
    return m.arr[:, 0]


def kernel(x, norm_mix_g, norm_ffn_g, norm_final_g, rg_w_in, rg_conv_w, rg_conv_b, rg_w_a, rg_b_a, rg_w_x, rg_b_x, rg_lambda, rg_w_out, s5_w_in, s5_a_re, s5_a_im, s5_log_dt, s5_b_re, s5_b_im, s5_c_re, s5_c_im, s5_d, s5_w_glu, s5_w_out, ffn_w_up, ffn_conv_w, ffn_conv_b, ffn_w_down, loss_target, m_norm_mix_g, m_norm_ffn_g, m_norm_final_g, m_rg_w_in, m_rg_conv_w, m_rg_conv_b, m_rg_w_a, m_rg_b_a, m_rg_w_x, m_rg_b_x, m_rg_lambda, m_rg_w_out, m_s5_w_in, m_s5_a_re, m_s5_a_im, m_s5_log_dt, m_s5_b_re, m_s5_b_im, m_s5_c_re, m_s5_c_im, m_s5_d, m_s5_w_glu, m_s5_w_out, m_ffn_w_up, m_ffn_conv_w, m_ffn_conv_b, m_ffn_w_down, v_norm_mix_g, v_norm_ffn_g, v_norm_final_g, v_rg_w_in, v_rg_conv_w, v_rg_conv_b, v_rg_w_a, v_rg_b_a, v_rg_w_x, v_rg_b_x, v_rg_lambda, v_rg_w_out, v_s5_w_in, v_s5_a_re, v_s5_a_im, v_s5_log_dt, v_s5_b_re, v_s5_b_im, v_s5_c_re, v_s5_c_im, v_s5_d, v_s5_w_glu, v_s5_w_out, v_ffn_w_up, v_ffn_conv_w, v_ffn_conv_b, v_ffn_w_down):
    w = dict(zip(PARAM_NAMES, (norm_mix_g, norm_ffn_g, norm_final_g, rg_w_in, rg_conv_w, rg_conv_b, rg_w_a, rg_b_a, rg_w_x, rg_b_x,
                               rg_lambda, rg_w_out, s5_w_in, s5_a_re, s5_a_im, s5_log_dt, s5_b_re, s5_b_im, s5_c_re, s5_c_im, s5_d,
                               s5_w_glu, s5_w_out, ffn_w_up, ffn_conv_w, ffn_conv_b, ffn_w_down)))
    mom = dict(zip(PARAM_NAMES, (m_norm_mix_g, m_norm_ffn_g, m_norm_final_g, m_rg_w_in, m_rg_conv_w, m_rg_conv_b, m_rg_w_a, m_rg_b_a,
                                 m_rg_w_x, m_rg_b_x, m_rg_lambda, m_rg_w_out, m_s5_w_in, m_s5_a_re, m_s5_a_im, m_s5_log_dt, m_s5_b_re,
                                 m_s5_b_im, m_s5_c_re, m_s5_c_im, m_s5_d, m_s5_w_glu, m_s5_w_out, m_ffn_w_up, m_ffn_conv_w,
                                 m_ffn_conv_b, m_ffn_w_down)))
    vel = dict(zip(PARAM_NAMES, (v_norm_mix_g, v_norm_ffn_g, v_norm_final_g, v_rg_w_in, v_rg_conv_w, v_rg_conv_b, v_rg_w_a, v_rg_b_a,
                                 v_rg_w_x, v_rg_b_x, v_rg_lambda, v_rg_w_out, v_s5_w_in, v_s5_a_re, v_s5_a_im, v_s5_log_dt, v_s5_b_re,
                                 v_s5_b_im, v_s5_c_re, v_s5_c_im, v_s5_d, v_s5_w_glu, v_s5_w_out, v_ffn_w_up, v_ffn_conv_w,
                                 v_ffn_conv_b, v_ffn_w_down)))
    _, s, d = x.shape
    depth = norm_mix_g.shape[0]
    n_grp, n_state = s5_a_re.shape[1], s5_a_re.shape[2]
    gc = s5_b_re.shape[3]
    d_ff = ffn_w_down.shape[1] * N_CHIPS
    s5_ts = min(128, s)
    nlev = max(1, int(math.log2(s5_ts)))

    gathered = dict(zip(SHARDED, _gather_shards([w[n].astype(BF16) if n in BIG else w[n] for n in SHARDED],
                                                [n in ROW_SHARDED for n in SHARDED])))

    def wcol(n, l):
        return Mat(gathered[n], l, 'c')

    def wrow(n, l):
        g = gathered[n]
        return Mat(g.reshape(1, g.shape[0], N_CHIPS * g.shape[2], g.shape[3]), l, 'c')

    tm = min(1024, s)
    d_up = 2 * d_ff // N_CHIPS
    rg_cw = gathered['rg_conv_w'].transpose(1, 2, 0, 3).reshape(rg_conv_w.shape[0], RG_CONV_W, d)
    s5_dv = gathered['s5_d'].transpose(1, 0, 2).reshape(s5_d.shape[0], 1, d)
    f_cw = gathered['ffn_conv_w'].transpose(1, 2, 0, 3).reshape(depth, FFN_CONV_W, 2, d_ff).transpose(0, 2, 1, 3)
    f_cb = ffn_conv_b.reshape(depth, 2, 1, d_ff)

    h = x.reshape(s, d)
    saved = []
    for i in range(depth):
        j = i // 2
        sv = {'h_in': h}
        hn = _rms_fwd(h, norm_mix_g[i:i + 1])
        sv['hn'] = hn
        if i % 2 == 0:
            xg = _mm("rg_in", 'nn', act(hn), wcol('rg_w_in', j), out_parts=2, tm=tm, tn=512, tk=d)
            xg2 = _split2(xg)
            wa, wx = rg_w_a[j].astype(BF16), rg_w_x[j].astype(BF16)
            ba, bx = rg_b_a[j].reshape(1, d), rg_b_x[j].reshape(1, d)
            xr, hs, y = _rg_fwd(xg2, rg_cw[j], rg_conv_b[j:j + 1], wa, ba, wx, bx, rg_lambda[j:j + 1])
            sv.update(xg2=xg2, xr=xr, hs=hs, y=y, wa=wa, wx=wx, ba=ba, bx=bx)
            h = _mm("rg_out", 'nn', act(y), wrow('rg_w_out', j), res=act(h), tm=tm, tn=d, tk=d).arr[0, 0]
        else:
            u = _mm("s5_in", 'nn', act(hn), wrow('s5_w_in', j), tm=tm, tn=d, tk=d).arr[0, 0]
            bt_re, bt_im = s5_b_re[j].transpose(2, 0, 1), s5_b_im[j].transpose(2, 0, 1)
            ldt = s5_log_dt[j].reshape(n_grp, 1)
            abr, abi, pw_r, pw_i, bbr, bbi = _s5_params(s5_a_re[j], s5_a_im[j], ldt, bt_re, bt_im, nlev)
            nn_ = n_grp * n_state
            prm = dict(abr=abr.reshape(1, nn_), abi=abi.reshape(1, nn_), pw_r=pw_r.reshape(nlev, nn_), pw_i=pw_i.reshape(nlev, nn_),
                       bp_r=_s5_pack_b(bbr).astype(BF16), bp_i=_s5_pack_b(bbi).astype(BF16),
                       cp_r=_s5_pack_c(s5_c_re[j]).astype(BF16), cp_i=_s5_pack_c(s5_c_im[j]).astype(BF16), dvec=s5_dv[j])
            hr, hi, ypre, gy = _s5_fwd(u, ts=s5_ts, **prm)
            gl = _mm("s5_glu", 'nn', act(gy), wcol('s5_w_glu', j), out_parts=2, tm=tm, tn=512, tk=d)
            gl2 = _split2(gl)
            o = _glu(gl2)
            sv.update(u=u, prm=prm, hr=hr, hi=hi, ypre=ypre, gy=gy, gl2=gl2, o=o, bt_re=bt_re, bt_im=bt_im, ldt=ldt)
            h = _mm("s5_out", 'nn', act(o), wrow('s5_w_out', j), res=act(h), tm=tm, tn=d, tk=d).arr[0, 0]
        sv['h_mid'] = h
        hn2 = _rms_fwd(h, norm_ffn_g[i:i + 1])
        up = _mm("ffn_up", 'nn', act(hn2), wcol('ffn_w_up', i), out_parts=2, tm=tm, tn=d_up, tk=d)
        up2 = _split2(up)
        a_ffn = _ffn_act(up2, f_cw[i], f_cb[i])
        sv.update(hn2=hn2, up2=up2, act=a_ffn)
        h = _mm("ffn_down", 'nn', act(a_ffn), wrow('ffn_w_down', i), res=act(h), tm=tm, tn=d, tk=d_ff // 2).arr[0, 0]
        saved.append(sv)

    loss_row, dh, dg_final = _loss_and_grad(h, norm_final_g.reshape(1, d), loss_target.reshape(s, d))
    loss = lax.psum(loss_row[0, 0], ("x", "y", "c"))

    gl_ = {n: [None] * w[n].shape[0] for n in PARAM_NAMES if n != 'norm_final_g'}
    for i in reversed(range(depth)):
        j = i // 2
        sv = saved[i]
        dact = _mm("ffn_down_dx", 'nt', act(dh), wrow('ffn_w_down', i), tm=tm, tn=d_ff // 2, tk=d).arr[0, 0]
        gl_['ffn_w_down'][i] = _mm("ffn_down_dw", 'tn', act(sv['act']), act(dh), out_dtype=BF16, tm=d_ff // 2, tn=d, tk=tm).arr
        dup2, dcw2, dcb2 = _ffn_bwd(sv['up2'], dact, f_cw[i], f_cb[i])
        gl_['ffn_conv_w'][i] = dcw2.transpose(1, 0, 2).reshape(FFN_CONV_W, N_CHIPS, 2 * d_ff // N_CHIPS).transpose(1, 0, 2)
        gl_['ffn_conv_b'][i] = dcb2.reshape(2 * d_ff)
        dup = Mat(dup2[:, None], 0, 'c')
        gl_['ffn_w_up'][i] = _mm("ffn_up_dw", 'tn', act(sv['hn2']), dup, out_parts=N_CHIPS, out_dtype=BF16, tm=d, tn=d_up, tk=tm).arr
        dhn2 = _mm("ffn_up_dx", 'nt', dup, wcol('ffn_w_up', i), tm=tm, tn=d, tk=d_up).arr[0, 0]
        dh, dg = _rms_bwd(sv['h_mid'], norm_ffn_g[i:i + 1], dhn2, dh)
        gl_['norm_ffn_g'][i] = dg[0]
        if i % 2 == 0:
            dy = _mm("rg_out_dx", 'nt', act(dh), wrow('rg_w_out', j), tm=tm, tn=d, tk=d).arr[0, 0]
            gl_['rg_w_out'][j] = _mm("rg_out_dw", 'tn', act(sv['y']), act(dh), out_dtype=BF16, tm=d, tn=d, tk=tm).arr
            dxg2, dcw, dcb, dwa, dba, dwx, dbx, dlam = _rg_bwd(dy, sv['xg2'], sv['xr'], sv['hs'], rg_cw[j], sv['wa'], sv['ba'],
                                                              sv['wx'], sv['bx'], rg_lambda[j:j + 1])
            gl_['rg_conv_w'][j] = dcw.reshape(RG_CONV_W, N_CHIPS, d // N_CHIPS).transpose(1, 0, 2)
            gl_['rg_conv_b'][j] = dcb[0]
            gl_['rg_w_a'][j], gl_['rg_w_x'][j] = dwa, dwx
            gl_['rg_b_a'][j], gl_['rg_b_x'][j] = dba.reshape(rg_b_a.shape[1:]), dbx.reshape(rg_b_x.shape[1:])
            gl_['rg_lambda'][j] = dlam[0]
            dxg = Mat(dxg2[:, None], 0, 'c')
            gl_['rg_w_in'][j] = _mm("rg_in_dw", 'tn', act(sv['hn']), dxg, out_parts=N_CHIPS, out_dtype=BF16, tm=d, tn=512, tk=tm).arr
            dhn = _mm("rg_in_dx", 'nt', dxg, wcol('rg_w_in', j), tm=tm, tn=d, tk=512).arr[0, 0]
        else:
            d_o = _mm("s5_out_dx", 'nt', act(dh), wrow('s5_w_out', j), tm=tm, tn=d, tk=d).arr[0, 0]
            gl_['s5_w_out'][j] = _mm("s5_out_dw", 'tn', act(sv['o']), act(dh), out_dtype=BF16, tm=d, tn=d, tk=tm).arr
            dgl2 = _glu_bwd(sv['gl2'], d_o)
            dgl = Mat(dgl2[:, None], 0, 'c')
            gl_['s5_w_glu'][j] = _mm("s5_glu_dw", 'tn', act(sv['gy']), dgl, out_parts=N_CHIPS, out_dtype=BF16, tm=d, tn=512, tk=tm).arr
            dgy = _mm("s5_glu_dx", 'nt', dgl, wcol('s5_w_glu', j), tm=tm, tn=d, tk=512).arr[0, 0]
            du, dar, dai, dbpr, dbpi, dcpr, dcpi, dd = _s5_bwd(dgy, sv['ypre'], sv['u'], sv['hr'], sv['hi'], ts=s5_ts, **sv['prm'])
            gl_['s5_d'][j] = dd.reshape(N_CHIPS, d // N_CHIPS)
            gl_['s5_c_re'][j] = _s5_unpack_c(dcpr, gc, n_state)
            gl_['s5_c_im'][j] = -_s5_unpack_c(dcpi, gc, n_state)
            d_are, d_aim, d_ldt, d_btr, d_bti = _s5_params_bwd(
                s5_a_re[j], s5_a_im[j], sv['ldt'], sv['bt_re'], sv['bt_im'], dar.reshape(n_grp, n_state), dai.reshape(n_grp, n_state),
                _s5_unpack_b(dbpr, gc, n_state), _s5_unpack_b(dbpi, gc, n_state))
            gl_['s5_a_re'][j], gl_['s5_a_im'][j], gl_['s5_log_dt'][j] = d_are, d_aim, d_ldt[:, 0]
            gl_['s5_b_re'][j], gl_['s5_b_im'][j] = d_btr.transpose(1, 2, 0), d_bti.transpose(1, 2, 0)
            dum = act(du)
            gl_['s5_w_in'][j] = _mm("s5_in_dw", 'tn', act(sv['hn']), dum, out_dtype=BF16, tm=d, tn=d, tk=tm).arr
            dhn = _mm("s5_in_dx", 'nt', dum, wrow('s5_w_in', j), tm=tm, tn=d, tk=d).arr[0, 0]
        dh, dg = _rms_bwd(sv['h_in'], norm_mix_g[i:i + 1], dhn, dh)
        gl_['norm_mix_g'][i] = dg[0]
    grad_x = dh.reshape(x.shape)

    def as4(n, a):
        shp = w[n].shape[1:]
        return a.reshape((N_CHIPS,) + shp)

    groups = [[as4(n, a) for a in gl_[n]] for n in SHARDED]
    recv = _scatter_grads(groups)
    chip_sums = []
    for n, r in zip(SHARDED, recv):
        cols = w[n].shape[-1]
        chip_sums.append(_sum_parts(r.reshape(N_CHIPS, -1, cols)))
    sib_sums = _swap_with_sibling(chip_sums)
    results = {}
    for n, mine, theirs in zip(SHARDED, chip_sums, sib_sums):
        cols = w[n].shape[-1]
        outs = _adamw(w[n].reshape(-1, cols), [mine, theirs], mom[n].reshape(-1, cols), vel[n].reshape(-1, cols))
        results[n] = [o.reshape(w[n].shape) for o in outs]

    rep_local = [dg_final.reshape(d) if n == 'norm_final_g' else jnp.stack(gl_[n]) for n in REPLICATED]
    rep_like = [w[n] for n in REPLICATED]
    g_pack = _allreduce_small(_pack(rep_local))
    outs = _adamw(_pack(rep_like), [g_pack], _pack([mom[n] for n in REPLICATED]), _pack([vel[n] for n in REPLICATED]))
    unpacked = [_unpack(o, rep_like) for o in outs]
    for k, n in enumerate(REPLICATED):
        results[n] = [unpacked[q][k] for q in range(4)]

    return (loss, grad_x, *[results[n][0] for n in PARAM_NAMES], *[results[n][1] for n in PARAM_NAMES],
            *[results[n][2] for n in PARAM_NAMES], *[results[n][3] for n in PARAM_NAMES])
```

```python
import functools
import math

import jax
import jax.numpy as jnp
from jax import lax
from jax.experimental import pallas as pl
from jax.experimental.pallas import tpu as pltpu

F32 = jnp.float32
BF16 = jnp.bfloat16
MESH = pl.DeviceIdType.MESH

NORM_EPS = 1e-6
RG_HEADS = 8
RG_CONV_W = 4
RG_C = 8.0
S5_GC = 16
S5_P = 64
S5_GROUPS_PER_BLOCK = 8
FFN_CONV_W = 3
N_CHIPS = 4
ADAM_LR, ADAM_B1, ADAM_B2, ADAM_EPS, ADAM_WD, ADAM_STEP = 0.001, 0.9, 0.999, 1e-08, 0.01, 10
VMEM_LIMIT_BYTES = 56 * 1024 * 1024
SUBLANES = 8
LANES = 128

PARAM_NAMES = ['norm_mix_g', 'norm_ffn_g', 'norm_final_g', 'rg_w_in', 'rg_conv_w', 'rg_conv_b', 'rg_w_a', 'rg_b_a', 'rg_w_x',
               'rg_b_x', 'rg_lambda', 'rg_w_out', 's5_w_in', 's5_a_re', 's5_a_im', 's5_log_dt', 's5_b_re', 's5_b_im', 's5_c_re',
               's5_c_im', 's5_d', 's5_w_glu', 's5_w_out', 'ffn_w_up', 'ffn_conv_w', 'ffn_conv_b', 'ffn_w_down']
SHARDED = ['rg_w_in', 'rg_conv_w', 'rg_w_out', 's5_w_in', 's5_d', 's5_w_glu', 's5_w_out', 'ffn_w_up', 'ffn_conv_w', 'ffn_w_down']
BIG = ['rg_w_in', 'rg_w_out', 's5_w_in', 's5_w_glu', 's5_w_out', 'ffn_w_up', 'ffn_w_down']
ROW_SHARDED = ['rg_w_out', 's5_w_in', 's5_w_out', 'ffn_w_down']
SMALL_SHARDED = ['rg_conv_w', 's5_d', 'ffn_conv_w']
MIXER_SHARDED = [['rg_w_in', 'rg_conv_w', 'rg_w_out'], ['s5_w_in', 's5_d', 's5_w_glu', 's5_w_out']]
FFN_SHARDED = ['ffn_w_up', 'ffn_conv_w', 'ffn_w_down']
REPLICATED = [n for n in PARAM_NAMES if n not in SHARDED]


def _cparams():
    return pltpu.CompilerParams(vmem_limit_bytes=VMEM_LIMIT_BYTES)


_GELU_C = math.sqrt(2.0 / math.pi)
_GELU_K = 0.044715


def _gelu(x):
    return 0.5 * x * (1.0 + jnp.tanh(_GELU_C * (x + _GELU_K * x * x * x)))


def _gelu_and_grad(x):
    t = jnp.tanh(_GELU_C * (x + _GELU_K * x * x * x))
    g = 0.5 * x * (1.0 + t)
    dg = 0.5 * (1.0 + t) + 0.5 * x * (1.0 - t * t) * (_GELU_C * (1.0 + 3.0 * _GELU_K * x * x))
    return g, dg


def _sigmoid(x):
    return jax.nn.sigmoid(x)


def _neg_expm1(x):
    series = -(x * (1.0 + x * (0.5 + x * (1.0 / 6 + x * (1.0 / 24 + x * (1.0 / 120 + x * (1.0 / 720)))))))
    return jnp.where(x > -0.25, series, 1.0 - jnp.exp(x))


def _softplus(z):
    return jnp.maximum(z, 0.0) + jnp.log1p(jnp.exp(-jnp.abs(z)))


def _rows(shape):
    return lax.broadcasted_iota(jnp.int32, shape, 0)


def _shift_down(x, halo, k):
    ext = jnp.concatenate([halo, x], axis=0)
    return pltpu.roll(ext, k, 0)[SUBLANES:]


def _shift_up(x, halo, k):
    ext = jnp.concatenate([x, halo], axis=0)
    n = ext.shape[0]
    return pltpu.roll(ext, n - k, 0)[:x.shape[0]]


def _scan_real_fwd(a, b):
    n = a.shape[0]
    row = _rows(a.shape)
    sh = 1
    while sh < n:
        ok = row >= sh
        b = a * jnp.where(ok, pltpu.roll(b, sh, 0), 0.0) + b
        if sh * 2 < n:
            a = a * jnp.where(ok, pltpu.roll(a, sh, 0), 1.0)
        sh *= 2
    return b


def _scan_real_rev(c, d):
    n = c.shape[0]
    row = _rows(c.shape)
    sh = 1
    while sh < n:
        ok = row < n - sh
        d = c * jnp.where(ok, pltpu.roll(d, n - sh, 0), 0.0) + d
        if sh * 2 < n:
            c = c * jnp.where(ok, pltpu.roll(c, n - sh, 0), 1.0)
        sh *= 2
    return d


def _scan_cplx(br, bi, pr_ref, pi_ref, reverse):
    n = br.shape[0]
    row = _rows(br.shape)
    sh, k = 1, 0
    while sh < n:
        pr = pr_ref[k:k + 1, :]
        pi = pi_ref[k:k + 1, :]
        if reverse:
            ok = row < n - sh
            sr = jnp.where(ok, pltpu.roll(br, n - sh, 0), 0.0)
            si = jnp.where(ok, pltpu.roll(bi, n - sh, 0), 0.0)
        else:
            ok = row >= sh
            sr = jnp.where(ok, pltpu.roll(br, sh, 0), 0.0)
            si = jnp.where(ok, pltpu.roll(bi, sh, 0), 0.0)
        br, bi = br + pr * sr - pi * si, bi + pr * si + pi * sr
        sh *= 2
        k += 1
    return br, bi


class Mat:
    def __init__(self, arr, l=0, split='c'):
        assert arr.ndim == 4
        self.arr, self.l, self.split = arr, l, split
        p, _, r, c = arr.shape
        self.shape = (r, c * p) if split == 'c' else (r * p, c)

    def spec(self, tr, tc, rc):
        p, _, r, c = self.arr.shape
        l = self.l
        assert r % tr == 0 and c % tc == 0, (self.arr.shape, tr, tc)
        if self.split == 'c':
            per = c // tc
            return pl.BlockSpec((None, None, tr, tc), lambda i, j, k: (rc(i, j, k)[1] // per, l, rc(i, j, k)[0], rc(i, j, k)[1] % per))
        per = r // tr
        return pl.BlockSpec((None, None, tr, tc), lambda i, j, k: (rc(i, j, k)[0] // per, l, rc(i, j, k)[0] % per, rc(i, j, k)[1]))


def act(x, parts=1):
    s, c = x.shape
    return Mat(x.reshape(s, parts, c // parts).transpose(1, 0, 2)[:, None] if parts > 1 else x[None, None])


def _mm(name, mode, a, b, *, out_parts=1, out_split='c', out_dtype=F32, res=None, tm=512, tn=512, tk=512):
    if mode == 'nn':
        (m, kk), (kb, n) = a.shape, b.shape
    elif mode == 'nt':
        (m, kk), (n, kb) = a.shape, b.shape
    else:
        (kk, m), (kb, n) = a.shape, b.shape
    assert kk == kb, (name, a.shape, b.shape)
    tm, tn, tk = min(tm, m), min(tn, n), min(tk, kk)
    assert m % tm == 0 and n % tn == 0 and kk % tk == 0, (name, m, n, kk, tm, tn, tk)
    nk = kk // tk
    if mode == 'nn':
        a_spec = a.spec(tm, tk, lambda i, j, k: (i, k))
        b_spec = b.spec(tk, tn, lambda i, j, k: (k, j))
        dims = (((1,), (0,)), ((), ()))
    elif mode == 'nt':
        a_spec = a.spec(tm, tk, lambda i, j, k: (i, k))
        b_spec = b.spec(tn, tk, lambda i, j, k: (j, k))
        dims = (((1,), (1,)), ((), ()))
    else:
        a_spec = a.spec(tk, tm, lambda i, j, k: (k, i))
        b_spec = b.spec(tk, tn, lambda i, j, k: (k, j))
        dims = (((0,), (0,)), ((), ()))
    if out_split == 'c':
        out_arr = jax.ShapeDtypeStruct((out_parts, 1, m, n // out_parts), out_dtype)
    else:
        out_arr = jax.ShapeDtypeStruct((out_parts, 1, m // out_parts, n), out_dtype)
    out_mat = Mat(out_arr, 0, out_split)
    o_spec = out_mat.spec(tm, tn, lambda i, j, k: (i, j))
    has_res = res is not None

    def body(*refs):
        if has_res:
            a_ref, b_ref, r_ref, o_ref = refs[:4]
        else:
            a_ref, b_ref, o_ref = refs[:3]
        prod = lax.dot_general(a_ref[...].astype(BF16), b_ref[...].astype(BF16), dims, preferred_element_type=F32)

        def finish(acc):
            if has_res:
                acc = acc + r_ref[...]
            o_ref[...] = acc.astype(out_dtype)

        if nk == 1:
            finish(prod)
        else:
            acc_ref = refs[-1]
            k = pl.program_id(2)

            @pl.when(k == 0)
            def _():
                acc_ref[...] = prod

            @pl.when(k > 0)
            def _():
                acc_ref[...] += prod

            @pl.when(k == nk - 1)
            def _():
                finish(acc_ref[...])

    in_specs = [a_spec, b_spec]
    args = [a.arr, b.arr]
    if has_res:
        in_specs.append(res.spec(tm, tn, lambda i, j, k: (i, j)))
        args.append(res.arr)
    out = pl.pallas_call(
        body, name=name, grid=(m // tm, n // tn, nk), in_specs=in_specs, out_specs=o_spec, out_shape=out_arr,
        scratch_shapes=[pltpu.VMEM((tm, tn), F32)] if nk > 1 else [], compiler_params=_cparams(),
    )(*args)
    return Mat(out, 0, out_split)


def _rms_fwd(h, g, ts=512):
    s, d = h.shape
    ts = min(ts, s)

    def body(h_ref, g_ref, o_ref):
        x = h_ref[...]
        var = jnp.mean(x * x, axis=-1, keepdims=True)
        o_ref[...] = (x * lax.rsqrt(var + NORM_EPS) * g_ref[...]).astype(BF16)

    return pl.pallas_call(
        body, name="rms_fwd", grid=(s // ts,),
        in_specs=[pl.BlockSpec((ts, d), lambda i: (i, 0)), pl.BlockSpec((1, d), lambda i: (0, 0))],
        out_specs=pl.BlockSpec((ts, d), lambda i: (i, 0)), out_shape=jax.ShapeDtypeStruct((s, d), BF16),
        compiler_params=_cparams(),
    )(h, g)


def _rms_bwd(h, g, dhn, dh_in, ts=512):
    s, d = h.shape
    ts = min(ts, s)

    def body(h_ref, g_ref, dhn_ref, dhin_ref, dh_ref, dg_ref):
        i = pl.program_id(0)
        x = h_ref[...]
        rstd = lax.rsqrt(jnp.mean(x * x, axis=-1, keepdims=True) + NORM_EPS)
        xhat = x * rstd
        dhn_v = dhn_ref[...]
        dxh = dhn_v * g_ref[...]
        dh_ref[...] = dhin_ref[...] + rstd * (dxh - xhat * jnp.mean(dxh * xhat, axis=-1, keepdims=True))
        part = jnp.sum(dhn_v * xhat, axis=0, keepdims=True)

        @pl.when(i == 0)
        def _():
            dg_ref[...] = part

        @pl.when(i > 0)
        def _():
            dg_ref[...] += part

    row = pl.BlockSpec((ts, d), lambda i: (i, 0))
    vec = pl.BlockSpec((1, d), lambda i: (0, 0))
    return pl.pallas_call(
        body, name="rms_bwd", grid=(s // ts,), in_specs=[row, vec, row, row], out_specs=[row, vec],
        out_shape=[jax.ShapeDtypeStruct((s, d), F32), jax.ShapeDtypeStruct((1, d), F32)], compiler_params=_cparams(),
    )(h, g, dhn, dh_in)


def _loss_and_grad(h, g, tgt, ts=512):
    s, d = h.shape
    ts = min(ts, s)

    def body(h_ref, g_ref, t_ref, loss_ref, dh_ref, dg_ref):
        i = pl.program_id(0)
        x = h_ref[...]
        gv = g_ref[...]
        rstd = lax.rsqrt(jnp.mean(x * x, axis=-1, keepdims=True) + NORM_EPS)
        xhat = x * rstd
        err = xhat * gv - t_ref[...]
        dy = err * (1.0 / d)
        dxh = dy * gv
        dh_ref[...] = rstd * (dxh - xhat * jnp.mean(dxh * xhat, axis=-1, keepdims=True))
        part = jnp.sum(dy * xhat, axis=0, keepdims=True)
        lpart = jnp.broadcast_to(jnp.sum(jnp.sum(err * err, axis=0, keepdims=True), axis=1, keepdims=True) * (0.5 / d), (1, LANES))

        @pl.when(i == 0)
        def _():
            dg_ref[...] = part
            loss_ref[...] = lpart

        @pl.when(i > 0)
        def _():
            dg_ref[...] += part
            loss_ref[...] += lpart

    row = pl.BlockSpec((ts, d), lambda i: (i, 0))
    vec = pl.BlockSpec((1, d), lambda i: (0, 0))
    return pl.pallas_call(
        body, name="loss_and_grad", grid=(s // ts,), in_specs=[row, vec, row],
        out_specs=[pl.BlockSpec((1, LANES), lambda i: (0, 0)), row, vec],
        out_shape=[jax.ShapeDtypeStruct((1, LANES), F32), jax.ShapeDtypeStruct((s, d), F32), jax.ShapeDtypeStruct((1, d), F32)],
        compiler_params=_cparams(),
    )(h, g, tgt)


def _halo_before(ts, nrow8):
    return lambda i: jnp.maximum(i * (ts // SUBLANES) - 1, 0)


def _ffn_act(up2, conv_w2, conv_b2, ts=512, tn=512, side=None):
    _, s, f = up2.shape
    ts, tn = min(ts, s), min(tn, f)
    kw = FFN_CONV_W

    def body(up_ref, halo_ref, w_ref, b_ref, o_ref):
        i = pl.program_id(0)
        cs = []
        for h in range(2):
            x = up_ref[h]
            halo = jnp.where(i == 0, 0.0, halo_ref[h])
            c = b_ref[h] + w_ref[h, kw - 1:kw, :] * x
            for sft in range(1, kw):
                c = c + w_ref[h, kw - 1 - sft:kw - sft, :] * _shift_down(x, halo, sft)
            cs.append(c)
        o_ref[...] = (_gelu(cs[0]) * cs[1]).astype(BF16)

    hb = ts // SUBLANES
    g0, g1 = s // ts, f // tn
    outs, side_outs = _call_with_side(
        body, side, lambda: (pl.program_id(0) == 0) & (pl.program_id(1) == 0),
        lambda: (pl.program_id(0) == g0 - 1) & (pl.program_id(1) == g1 - 1),
        name="ffn_act", grid=(g0, g1),
        in_specs=[pl.BlockSpec((2, ts, tn), lambda i, j: (0, i, j)),
                  pl.BlockSpec((2, SUBLANES, tn), lambda i, j: (0, jnp.maximum(i * hb - 1, 0), j)),
                  pl.BlockSpec((2, kw, tn), lambda i, j: (0, 0, j)),
                  pl.BlockSpec((2, 1, tn), lambda i, j: (0, 0, j))],
        out_specs=[pl.BlockSpec((ts, tn), lambda i, j: (i, j))], out_shape=[jax.ShapeDtypeStruct((s, f), BF16)],
        scratch_shapes=[], args=(up2, up2, conv_w2, conv_b2))
    return outs[0], side_outs


def _ffn_bwd(up2, dact, conv_w2, conv_b2, ts=256, tn=512, side=None):
    _, s, f = up2.shape
    ts, tn = min(ts, s), min(tn, f)
    kw = FFN_CONV_W
    nt = s // ts
    hb = ts // SUBLANES
    last8 = s // SUBLANES - 1

    def body(up_ref, hb_ref, ha_ref, da_ref, dah_ref, w_ref, b_ref, dup_ref, dw_ref, db_ref):
        i = pl.program_id(1)
        first, last = i == 0, i == nt - 1
        ce, xs = [], []
        for h in range(2):
            x = up_ref[h]
            before = jnp.where(first, 0.0, hb_ref[h])
            after = ha_ref[h]
            ext = jnp.concatenate([before, x, after], axis=0)
            c = b_ref[h] + w_ref[h, kw - 1:kw, :] * ext
            shifted = [ext]
            for sft in range(1, kw):
                sh = pltpu.roll(ext, sft, 0)
                shifted.append(sh)
                c = c + w_ref[h, kw - 1 - sft:kw - sft, :] * sh
            ce.append(c[SUBLANES:])
            xs.append([sh[SUBLANES:SUBLANES + ts] for sh in shifted])
        da = jnp.concatenate([da_ref[...], jnp.where(last, 0.0, dah_ref[...])], axis=0)
        g1, dg1 = _gelu_and_grad(ce[0])
        dcs = [da * ce[1] * dg1, da * g1]
        for h in range(2):
            dc = dcs[h]
            n = dc.shape[0]
            dup = w_ref[h, kw - 1:kw, :] * dc[:ts]
            for sft in range(1, kw):
                dup = dup + w_ref[h, kw - 1 - sft:kw - sft, :] * pltpu.roll(dc, n - sft, 0)[:ts]
            dup_ref[h] = dup.astype(BF16)
            dct = dc[:ts]
            dbp = jnp.sum(dct, axis=0, keepdims=True)
            dwp = [jnp.sum(dct * xs[h][kw - 1 - k], axis=0, keepdims=True) for k in range(kw)]

            @pl.when(first)
            def _():
                db_ref[h] = dbp
                for k in range(kw):
                    dw_ref[h, k:k + 1, :] = dwp[k]

            @pl.when(i > 0)
            def _():
                db_ref[h] += dbp
                for k in range(kw):
                    dw_ref[h, k:k + 1, :] += dwp[k]

    g0 = f // tn
    return _call_with_side(
        body, side, lambda: (pl.program_id(0) == 0) & (pl.program_id(1) == 0),
        lambda: (pl.program_id(0) == g0 - 1) & (pl.program_id(1) == nt - 1),
        name="ffn_bwd", grid=(g0, nt),
        in_specs=[pl.BlockSpec((2, ts, tn), lambda j, i: (0, i, j)),
                  pl.BlockSpec((2, SUBLANES, tn), lambda j, i: (0, jnp.maximum(i * hb - 1, 0), j)),
                  pl.BlockSpec((2, SUBLANES, tn), lambda j, i: (0, jnp.minimum((i + 1) * hb, last8), j)),
                  pl.BlockSpec((ts, tn), lambda j, i: (i, j)),
                  pl.BlockSpec((SUBLANES, tn), lambda j, i: (jnp.minimum((i + 1) * hb, last8), j)),
                  pl.BlockSpec((2, kw, tn), lambda j, i: (0, 0, j)),
                  pl.BlockSpec((2, 1, tn), lambda j, i: (0, 0, j))],
        out_specs=[pl.BlockSpec((2, ts, tn), lambda j, i: (0, i, j)),
                   pl.BlockSpec((2, kw, tn), lambda j, i: (0, 0, j)),
                   pl.BlockSpec((2, 1, tn), lambda j, i: (0, 0, j))],
        out_shape=[jax.ShapeDtypeStruct((2, s, f), BF16), jax.ShapeDtypeStruct((2, kw, f), F32),
                   jax.ShapeDtypeStruct((2, 1, f), F32)],
        scratch_shapes=[], args=(up2, up2, up2, dact, dact, conv_w2, conv_b2))


def _rg_gates(xr, wa_ref, ba_ref, wx_ref, bx_ref, lam_ref):
    bw = wa_ref.shape[-1]
    xb = xr.astype(BF16)
    za = jnp.concatenate([jnp.dot(xb[:, h * bw:(h + 1) * bw], wa_ref[h], preferred_element_type=F32)
                          for h in range(RG_HEADS)], axis=1) + ba_ref[...]
    zx = jnp.concatenate([jnp.dot(xb[:, h * bw:(h + 1) * bw], wx_ref[h], preferred_element_type=F32)
                          for h in range(RG_HEADS)], axis=1) + bx_ref[...]
    r, ig = _sigmoid(za), _sigmoid(zx)
    sp = _softplus(-lam_ref[...])
    la = -RG_C * r * sp
    a = jnp.exp(la)
    mult = jnp.sqrt(_neg_expm1(2.0 * la))
    return xb, r, ig, sp, a, mult


def _rg_fwd(xg2, conv_w, conv_b, w_a, b_a, w_x, b_x, lam, ts=256, side=None):
    _, s, c = xg2.shape
    ts = min(ts, s)
    kw = RG_CONV_W
    hb = ts // SUBLANES

    def body(xg_ref, halo_ref, cw_ref, cb_ref, wa_ref, ba_ref, wx_ref, bx_ref, lam_ref, xr_ref, hs_ref, y_ref, carry_ref):
        i = pl.program_id(0)

        @pl.when(i == 0)
        def _():
            carry_ref[...] = jnp.zeros_like(carry_ref)

        xp = xg_ref[0]
        halo = jnp.where(i == 0, 0.0, halo_ref[...])
        xr = cb_ref[...] + cw_ref[kw - 1:kw, :] * xp
        for sft in range(1, kw):
            xr = xr + cw_ref[kw - 1 - sft:kw - sft, :] * _shift_down(xp, halo, sft)
        _, r, ig, sp, a, mult = _rg_gates(xr, wa_ref, ba_ref, wx_ref, bx_ref, lam_ref)
        bt = mult * (ig * xr)
        row = _rows(bt.shape)
        bt = bt + jnp.where(row == 0, a * carry_ref[SUBLANES - 1:SUBLANES, :], 0.0)
        hs = _scan_real_fwd(a, bt)
        carry_ref[...] = hs[ts - SUBLANES:, :]
        xr_ref[...] = xr
        hs_ref[...] = hs
        y_ref[...] = (hs * _gelu(xg_ref[1])).astype(BF16)

    full = lambda shape: pl.BlockSpec(shape, lambda i: (0,) * len(shape))
    row_spec = pl.BlockSpec((ts, c), lambda i: (i, 0))
    nt = s // ts
    return _call_with_side(
        body, side, lambda: pl.program_id(0) == 0, lambda: pl.program_id(0) == nt - 1,
        name="rg_fwd", grid=(nt,),
        in_specs=[pl.BlockSpec((2, ts, c), lambda i: (0, i, 0)),
                  pl.BlockSpec((None, SUBLANES, c), lambda i: (0, jnp.maximum(i * hb - 1, 0), 0)),
                  full(conv_w.shape), full(conv_b.shape), full(w_a.shape), full(b_a.shape), full(w_x.shape), full(b_x.shape),
                  full(lam.shape)],
        out_specs=[row_spec, row_spec, row_spec],
        out_shape=[jax.ShapeDtypeStruct((s, c), F32), jax.ShapeDtypeStruct((s, c), F32), jax.ShapeDtypeStruct((s, c), BF16)],
        scratch_shapes=[pltpu.VMEM((SUBLANES, c), F32)], args=(xg2, xg2, conv_w, conv_b, w_a, b_a, w_x, b_x, lam))


def _rg_bwd(dy, xg2, xr, hs, conv_w, w_a, b_a, w_x, b_x, lam, ts=256, side=None):
    _, s, c = xg2.shape
    ts = min(ts, s)
    nt = s // ts
    kw = RG_CONV_W
    hb = ts // SUBLANES
    bw = c // RG_HEADS
    tn_dims = (((0,), (0,)), ((), ()))
    nt_dims = (((1,), (1,)), ((), ()))

    def body(dy_ref, xg_ref, xph_ref, xr_ref, hs_ref, hsh_ref, cw_ref, wa_ref, ba_ref, wx_ref, bx_ref, lam_ref,
             dxg_ref, dcw_ref, dcb_ref, dwa_ref, dba_ref, dwx_ref, dbx_ref, dlam_ref,
             lam_carry, a_carry, dxr_carry, dsp_acc):
        i = pl.program_id(0)
        first_step = i == 0
        time_first = i == nt - 1

        @pl.when(first_step)
        def _():
            lam_carry[...] = jnp.zeros_like(lam_carry)
            a_carry[...] = jnp.ones_like(a_carry)
            dxr_carry[...] = jnp.zeros_like(dxr_carry)
            dsp_acc[...] = jnp.zeros_like(dsp_acc)
            for ref in (dcw_ref, dcb_ref, dwa_ref, dba_ref, dwx_ref, dbx_ref):
                ref[...] = jnp.zeros_like(ref)

        xr = xr_ref[...]
        hs = hs_ref[...]
        gate = xg_ref[1]
        xb, r, ig, sp, a, mult = _rg_gates(xr, wa_ref, ba_ref, wx_ref, bx_ref, lam_ref)
        dyv = dy_ref[...]
        gg, dgg = _gelu_and_grad(gate)
        dhs = dyv * gg
        dxg_ref[1] = (dyv * hs * dgg).astype(BF16)
        row = _rows(xr.shape)
        coef = jnp.where(row == ts - 1, a_carry[0:1, :], pltpu.roll(a, ts - 1, 0))
        dhs = dhs + jnp.where(row == ts - 1, coef * lam_carry[0:1, :], 0.0)
        lmb = _scan_real_rev(coef, dhs)
        lam_carry[...] = lmb[:SUBLANES]
        a_carry[...] = a[:SUBLANES]
        hs_prev = _shift_down(hs, jnp.where(time_first, 0.0, hsh_ref[...]), 1)
        d_a = lmb * hs_prev
        d_m = lmb * (ig * xr)
        d_ig = lmb * mult * xr
        d_xr = lmb * mult * ig
        d_la = a * d_a - (a * a / mult) * d_m
        dsp_acc[...] += jnp.sum(-RG_C * r * d_la, axis=0, keepdims=True)
        d_za = (-RG_C * sp) * d_la * r * (1.0 - r)
        d_zx = d_ig * ig * (1.0 - ig)
        dba_ref[...] += jnp.sum(d_za, axis=0, keepdims=True)
        dbx_ref[...] += jnp.sum(d_zx, axis=0, keepdims=True)
        dzab, dzxb = d_za.astype(BF16), d_zx.astype(BF16)
        back = []
        for h in range(RG_HEADS):
            sl = slice(h * bw, (h + 1) * bw)
            dwa_ref[h] += lax.dot_general(xb[:, sl], dzab[:, sl], tn_dims, preferred_element_type=F32)
            dwx_ref[h] += lax.dot_general(xb[:, sl], dzxb[:, sl], tn_dims, preferred_element_type=F32)
            back.append(lax.dot_general(dzab[:, sl], wa_ref[h], nt_dims, preferred_element_type=F32)
                        + lax.dot_general(dzxb[:, sl], wx_ref[h], nt_dims, preferred_element_type=F32))
        d_xr = d_xr + jnp.concatenate(back, axis=1)
        d_xp = cw_ref[kw - 1:kw, :] * d_xr
        after = dxr_carry[...]
        for sft in range(1, kw):
            d_xp = d_xp + cw_ref[kw - 1 - sft:kw - sft, :] * _shift_up(d_xr, after, sft)
        dxr_carry[...] = d_xr[:SUBLANES]
        dxg_ref[0] = d_xp.astype(BF16)
        xp = xg_ref[0]
        before = jnp.where(time_first, 0.0, xph_ref[...])
        dcb_ref[...] += jnp.sum(d_xr, axis=0, keepdims=True)
        dcw_ref[kw - 1:kw, :] += jnp.sum(d_xr * xp, axis=0, keepdims=True)
        for sft in range(1, kw):
            dcw_ref[kw - 1 - sft:kw - sft, :] += jnp.sum(d_xr * _shift_down(xp, before, sft), axis=0, keepdims=True)
        dlam_ref[...] = dsp_acc[...] * (-_sigmoid(-lam_ref[...]))

    full = lambda shape: pl.BlockSpec(shape, lambda i: (0,) * len(shape))
    rev = lambda i: nt - 1 - i
    row_spec = pl.BlockSpec((ts, c), lambda i: (rev(i), 0))
    halo_idx = lambda i: jnp.maximum(rev(i) * hb - 1, 0)
    vec = (1, c)
    return _call_with_side(
        body, side, lambda: pl.program_id(0) == 0, lambda: pl.program_id(0) == nt - 1,
        name="rg_bwd", grid=(nt,),
        in_specs=[row_spec,
                  pl.BlockSpec((2, ts, c), lambda i: (0, rev(i), 0)),
                  pl.BlockSpec((None, SUBLANES, c), lambda i: (0, halo_idx(i), 0)),
                  row_spec, row_spec,
                  pl.BlockSpec((SUBLANES, c), lambda i: (halo_idx(i), 0)),
                  full(conv_w.shape), full(w_a.shape), full(b_a.shape), full(w_x.shape), full(b_x.shape), full(lam.shape)],
        out_specs=[pl.BlockSpec((2, ts, c), lambda i: (0, rev(i), 0)), full(conv_w.shape), full(vec), full(w_a.shape), full(vec),
                   full(w_x.shape), full(vec), full(vec)],
        out_shape=[jax.ShapeDtypeStruct((2, s, c), BF16), jax.ShapeDtypeStruct(conv_w.shape, F32), jax.ShapeDtypeStruct(vec, F32),
                   jax.ShapeDtypeStruct(w_a.shape, F32), jax.ShapeDtypeStruct(vec, F32), jax.ShapeDtypeStruct(w_x.shape, F32),
                   jax.ShapeDtypeStruct(vec, F32), jax.ShapeDtypeStruct(vec, F32)],
        scratch_shapes=[pltpu.VMEM((SUBLANES, c), F32), pltpu.VMEM((SUBLANES, c), F32), pltpu.VMEM((SUBLANES, c), F32),
                        pltpu.VMEM(vec, F32)],
        args=(dy, xg2, xg2, xr, hs, hs, conv_w, w_a, b_a, w_x, b_x, lam))


def _s5_param_fn(a_re, a_im, log_dt, bt_re, bt_im):
    dt = jnp.exp(log_dt)
    mag = jnp.exp(a_re * dt)
    abr = mag * jnp.cos(a_im * dt)
    abi = mag * jnp.sin(a_im * dt)
    ur, ui = abr - 1.0, abi
    den = a_re * a_re + a_im * a_im
    wr = (ur * a_re + ui * a_im) / den
    wi = (ui * a_re - ur * a_im) / den
    bbr = wr[None] * bt_re - wi[None] * bt_im
    bbi = wr[None] * bt_im + wi[None] * bt_re
    return abr, abi, bbr, bbi


def _s5_params(a_re, a_im, log_dt, bt_re, bt_im, nlev):
    g, p = a_re.shape
    gc = bt_re.shape[0]

    def body(ar_ref, ai_ref, dt_ref, br_ref, bi_ref, abr_ref, abi_ref, pr_ref, pi_ref, bbr_ref, bbi_ref):
        abr, abi, bbr, bbi = _s5_param_fn(ar_ref[...], ai_ref[...], dt_ref[...], br_ref[...], bi_ref[...])
        abr_ref[...] = abr
        abi_ref[...] = abi
        bbr_ref[...] = bbr
        bbi_ref[...] = bbi
        qr, qi = abr, abi
        for k in range(nlev):
            pr_ref[k] = qr
            pi_ref[k] = qi
            qr, qi = qr * qr - qi * qi, 2.0 * qr * qi

    sd = jax.ShapeDtypeStruct
    return pl.pallas_call(
        body, name="s5_params",
        out_shape=[sd((g, p), F32), sd((g, p), F32), sd((nlev, g, p), F32), sd((nlev, g, p), F32), sd((gc, g, p), F32),
                   sd((gc, g, p), F32)],
    )(a_re, a_im, log_dt, bt_re, bt_im)


def _s5_params_bwd(a_re, a_im, log_dt, bt_re, bt_im, d_abr, d_abi, d_bbr, d_bbi):
    def body(ar_ref, ai_ref, dt_ref, br_ref, bi_ref, g0, g1, g2, g3, o0, o1, o2, o3, o4):
        _, vjp = jax.vjp(_s5_param_fn, ar_ref[...], ai_ref[...], dt_ref[...], br_ref[...], bi_ref[...])
        outs = vjp((g0[...], g1[...], g2[...], g3[...]))
        for o, v in zip((o0, o1, o2, o3, o4), outs):
            o[...] = v

    sd = jax.ShapeDtypeStruct
    return pl.pallas_call(
        body, name="s5_params_bwd",
        out_shape=[sd(a_re.shape, F32), sd(a_im.shape, F32), sd(log_dt.shape, F32), sd(bt_re.shape, F32), sd(bt_im.shape, F32)],
    )(a_re, a_im, log_dt, bt_re, bt_im, d_abr, d_abi, d_bbr, d_bbi)


def _s5_fwd(u, abr, abi, pw_r, pw_i, bp_r, bp_i, cp_r, cp_i, dvec, ts=128, side=None):
    s, c = u.shape
    n = abr.shape[1]
    nblk, cb, nb = bp_r.shape
    ts = min(ts, s)

    def body(u_ref, ar_ref, ai_ref, pr_ref, pi_ref, bpr_ref, bpi_ref, cpr_ref, cpi_ref, d_ref,
             hr_ref, hi_ref, yp_ref, gy_ref, car_r, car_i):
        i = pl.program_id(0)

        @pl.when(i == 0)
        def _():
            car_r[...] = jnp.zeros_like(car_r)
            car_i[...] = jnp.zeros_like(car_i)

        uv = u_ref[...]
        ub = uv.astype(BF16)
        br = jnp.concatenate([jnp.dot(ub[:, k * cb:(k + 1) * cb], bpr_ref[k], preferred_element_type=F32) for k in range(nblk)], axis=1)
        bi = jnp.concatenate([jnp.dot(ub[:, k * cb:(k + 1) * cb], bpi_ref[k], preferred_element_type=F32) for k in range(nblk)], axis=1)
        ar, ai = ar_ref[...], ai_ref[...]
        pr, pi_ = car_r[SUBLANES - 1:SUBLANES, :], car_i[SUBLANES - 1:SUBLANES, :]
        row = _rows(br.shape)
        br = br + jnp.where(row == 0, ar * pr - ai * pi_, 0.0)
        bi = bi + jnp.where(row == 0, ar * pi_ + ai * pr, 0.0)
        hr, hi = _scan_cplx(br, bi, pr_ref, pi_ref, reverse=False)
        car_r[...] = hr[ts - SUBLANES:]
        car_i[...] = hi[ts - SUBLANES:]
        hr_ref[...] = hr
        hi_ref[...] = hi
        hrb, hib = hr.astype(BF16), hi.astype(BF16)
        y = jnp.concatenate([jnp.dot(hrb[:, k * nb:(k + 1) * nb], cpr_ref[k], preferred_element_type=F32)
                             - jnp.dot(hib[:, k * nb:(k + 1) * nb], cpi_ref[k], preferred_element_type=F32) for k in range(nblk)], axis=1)
        yp = y + d_ref[...] * uv
        yp_ref[...] = yp
        gy_ref[...] = _gelu(yp).astype(BF16)

    full = lambda shape: pl.BlockSpec(shape, lambda i: (0,) * len(shape))
    rc = pl.BlockSpec((ts, c), lambda i: (i, 0))
    rn = pl.BlockSpec((ts, n), lambda i: (i, 0))
    sd = jax.ShapeDtypeStruct
    nt = s // ts
    return _call_with_side(
        body, side, lambda: pl.program_id(0) == 0, lambda: pl.program_id(0) == nt - 1,
        name="s5_fwd", grid=(nt,),
        in_specs=[rc, full(abr.shape), full(abi.shape), full(pw_r.shape), full(pw_i.shape), full(bp_r.shape), full(bp_i.shape),
                  full(cp_r.shape), full(cp_i.shape), full(dvec.shape)],
        out_specs=[rn, rn, rc, rc],
        out_shape=[sd((s, n), F32), sd((s, n), F32), sd((s, c), F32), sd((s, c), BF16)],
        scratch_shapes=[pltpu.VMEM((SUBLANES, n), F32), pltpu.VMEM((SUBLANES, n), F32)],
        args=(u, abr, abi, pw_r, pw_i, bp_r, bp_i, cp_r, cp_i, dvec))


def _s5_bwd(dgy, ypre, u, hr, hi, abr, abi, pw_r, pw_i, bp_r, bp_i, cp_r, cp_i, dvec, ts=128, side=None):
    s, c = u.shape
    n = abr.shape[1]
    nblk, cb, nb = bp_r.shape
    ts = min(ts, s)
    nt = s // ts
    hb = ts // SUBLANES
    tn_dims = (((0,), (0,)), ((), ()))
    nt_dims = (((1,), (1,)), ((), ()))

    def body(dgy_ref, yp_ref, u_ref, hr_ref, hi_ref, hrh_ref, hih_ref, ar_ref, ai_ref, pr_ref, pi_ref, bpr_ref, bpi_ref,
             cpr_ref, cpi_ref, d_ref,
             du_ref, dar_ref, dai_ref, dbr_ref, dbi_ref, dcr_ref, dci_ref, dd_ref, car_r, car_i, npi_ref):
        i = pl.program_id(0)
        time_first = i == nt - 1

        @pl.when(i == 0)
        def _():
            car_r[...] = jnp.zeros_like(car_r)
            car_i[...] = jnp.zeros_like(car_i)
            npi_ref[...] = -pi_ref[...]
            for ref in (dar_ref, dai_ref, dbr_ref, dbi_ref, dcr_ref, dci_ref, dd_ref):
                ref[...] = jnp.zeros_like(ref)

        uv = u_ref[...]
        _, dgel = _gelu_and_grad(yp_ref[...])
        dyv = dgy_ref[...] * dgel
        dd_ref[...] += jnp.sum(dyv * uv, axis=0, keepdims=True)
        dyb = dyv.astype(BF16)
        hr, hi = hr_ref[...], hi_ref[...]
        hrb, hib = hr.astype(BF16), hi.astype(BF16)
        dhr, dhi = [], []
        for k in range(nblk):
            dblk = dyb[:, k * cb:(k + 1) * cb]
            dhr.append(lax.dot_general(dblk, cpr_ref[k], nt_dims, preferred_element_type=F32))
            dhi.append(-lax.dot_general(dblk, cpi_ref[k], nt_dims, preferred_element_type=F32))
            dcr_ref[k] += lax.dot_general(hrb[:, k * nb:(k + 1) * nb], dblk, tn_dims, preferred_element_type=F32)
            dci_ref[k] += lax.dot_general(hib[:, k * nb:(k + 1) * nb], dblk, tn_dims, preferred_element_type=F32)
        dhr = jnp.concatenate(dhr, axis=1)
        dhi = jnp.concatenate(dhi, axis=1)
        ar, ai = ar_ref[...], ai_ref[...]
        nr, ni = car_r[0:1, :], car_i[0:1, :]
        row = _rows(dhr.shape)
        dhr = dhr + jnp.where(row == ts - 1, ar * nr + ai * ni, 0.0)
        dhi = dhi + jnp.where(row == ts - 1, ar * ni - ai * nr, 0.0)
        lr, li = _scan_cplx(dhr, dhi, pr_ref, npi_ref, reverse=True)
        car_r[...] = lr[:SUBLANES]
        car_i[...] = li[:SUBLANES]
        hpr = _shift_down(hr, jnp.where(time_first, 0.0, hrh_ref[...]), 1)
        hpi = _shift_down(hi, jnp.where(time_first, 0.0, hih_ref[...]), 1)
        dar_ref[...] += jnp.sum(lr * hpr + li * hpi, axis=0, keepdims=True)
        dai_ref[...] += jnp.sum(li * hpr - lr * hpi, axis=0, keepdims=True)
        lrb, lib = lr.astype(BF16), li.astype(BF16)
        ub = uv.astype(BF16)
        du = []
        for k in range(nblk):
            ublk = ub[:, k * cb:(k + 1) * cb]
            lrk, lik = lrb[:, k * nb:(k + 1) * nb], lib[:, k * nb:(k + 1) * nb]
            dbr_ref[k] += lax.dot_general(ublk, lrk, tn_dims, preferred_element_type=F32)
            dbi_ref[k] += lax.dot_general(ublk, lik, tn_dims, preferred_element_type=F32)
            du.append(lax.dot_general(lrk, bpr_ref[k], nt_dims, preferred_element_type=F32)
                      + lax.dot_general(lik, bpi_ref[k], nt_dims, preferred_element_type=F32))
        du_ref[...] = (d_ref[...] * dyv + jnp.concatenate(du, axis=1)).astype(BF16)

    full = lambda shape: pl.BlockSpec(shape, lambda i: (0,) * len(shape))
    rev = lambda i: nt - 1 - i
    halo_idx = lambda i: jnp.maximum(rev(i) * hb - 1, 0)
    rc = pl.BlockSpec((ts, c), lambda i: (rev(i), 0))
    rn = pl.BlockSpec((ts, n), lambda i: (rev(i), 0))
    hn = pl.BlockSpec((SUBLANES, n), lambda i: (halo_idx(i), 0))
    sd = jax.ShapeDtypeStruct
    return _call_with_side(
        body, side, lambda: pl.program_id(0) == 0, lambda: pl.program_id(0) == nt - 1,
        name="s5_bwd", grid=(nt,),
        in_specs=[rc, rc, rc, rn, rn, hn, hn, full(abr.shape), full(abi.shape), full(pw_r.shape), full(pw_i.shape),
                  full(bp_r.shape), full(bp_i.shape), full(cp_r.shape), full(cp_i.shape), full(dvec.shape)],
        out_specs=[rc, full(abr.shape), full(abi.shape), full(bp_r.shape), full(bp_i.shape), full(cp_r.shape), full(cp_i.shape),
                   full(dvec.shape)],
        out_shape=[sd((s, c), BF16), sd(abr.shape, F32), sd(abi.shape, F32), sd(bp_r.shape, F32), sd(bp_i.shape, F32),
                   sd(cp_r.shape, F32), sd(cp_i.shape, F32), sd(dvec.shape, F32)],
        scratch_shapes=[pltpu.VMEM((SUBLANES, n), F32), pltpu.VMEM((SUBLANES, n), F32), pltpu.VMEM(pw_i.shape, F32)],
        args=(dgy, ypre, u, hr, hi, hr, hi, abr, abi, pw_r, pw_i, bp_r, bp_i, cp_r, cp_i, dvec))


def _glu(gl2, ts=512):
    _, s, c = gl2.shape
    ts = min(ts, s)

    def body(g_ref, o_ref):
        o_ref[...] = (g_ref[0] * _sigmoid(g_ref[1])).astype(BF16)

    return pl.pallas_call(
        body, name="glu", grid=(s // ts,), in_specs=[pl.BlockSpec((2, ts, c), lambda i: (0, i, 0))],
        out_specs=pl.BlockSpec((ts, c), lambda i: (i, 0)), out_shape=jax.ShapeDtypeStruct((s, c), BF16), compiler_params=_cparams(),
    )(gl2)


def _glu_bwd(gl2, d_o, ts=512):
    _, s, c = gl2.shape
    ts = min(ts, s)

    def body(g_ref, do_ref, o_ref):
        sg = _sigmoid(g_ref[1])
        dov = do_ref[...]
        o_ref[0] = (dov * sg).astype(BF16)
        o_ref[1] = (dov * g_ref[0] * sg * (1.0 - sg)).astype(BF16)

    blk = pl.BlockSpec((2, ts, c), lambda i: (0, i, 0))
    return pl.pallas_call(
        body, name="glu_bwd", grid=(s // ts,), in_specs=[blk, pl.BlockSpec((ts, c), lambda i: (i, 0))],
        out_specs=blk, out_shape=jax.ShapeDtypeStruct((2, s, c), BF16), compiler_params=_cparams(),
    )(gl2, d_o)


def _row_tile(rows, pref=512):
    if rows <= pref:
        return rows
    t = pref
    while rows % t:
        t //= 2
    assert t >= SUBLANES, rows
    return t


def _sum_parts(rs):
    nl = len(rs)
    p, rows, cols = rs[0].shape
    tr = _row_tile(rows, 128)

    def body(*refs):
        o_ref = refs[nl]
        for l in range(nl):
            acc = refs[l][0].astype(F32)
            for k in range(1, p):
                acc = acc + refs[l][k].astype(F32)
            o_ref[l] = acc

    return pl.pallas_call(
        body, name="sum_parts", grid=(rows // tr,), in_specs=[pl.BlockSpec((p, tr, cols), lambda i: (0, i, 0))] * nl,
        out_specs=pl.BlockSpec((nl, tr, cols), lambda i: (0, i, 0)), out_shape=jax.ShapeDtypeStruct((nl, rows, cols), F32),
        compiler_params=_cparams(),
    )(*rs)


def _adamw(w, g_parts, m, v):
    rows, cols = w.shape
    tr = _row_tile(rows, 128)
    ng = len(g_parts)
    c1 = 1.0 / (1.0 - ADAM_B1 ** ADAM_STEP)
    c2 = 1.0 / (1.0 - ADAM_B2 ** ADAM_STEP)

    def body(*refs):
        w_ref, m_ref, v_ref = refs[0], refs[1 + ng], refs[2 + ng]
        g_ref, dl_ref, nm_ref, nv_ref = refs[3 + ng:]
        g = refs[1][...]
        for k in range(1, ng):
            g = g + refs[1 + k][...]
        mn = ADAM_B1 * m_ref[...] + (1.0 - ADAM_B1) * g
        vn = ADAM_B2 * v_ref[...] + (1.0 - ADAM_B2) * (g * g)
        g_ref[...] = g
        nm_ref[...] = mn
        nv_ref[...] = vn
        dl_ref[...] = -ADAM_LR * ((mn * c1) / (jnp.sqrt(vn * c2) + ADAM_EPS) + ADAM_WD * w_ref[...])

    blk = pl.BlockSpec((tr, cols), lambda i: (i, 0))
    sd = jax.ShapeDtypeStruct((rows, cols), F32)
    return pl.pallas_call(
        body, name="adamw", grid=(rows // tr,), in_specs=[blk] * (3 + ng), out_specs=[blk] * 4, out_shape=[sd] * 4,
        compiler_params=_cparams(),
    )(w, *g_parts, m, v)


def _place():
    x, y, c = lax.axis_index("x"), lax.axis_index("y"), lax.axis_index("c")
    chips = [(1 - x, y), (x, 1 - y), (1 - x, 1 - y)]
    return x, y, c, chips


class Side:
    def __init__(self, ins, outs, kind):
        self.ins, self.outs, self.kind = list(ins), list(outs), kind
        n = len(self.ins)
        self.sems = [pltpu.SemaphoreType.DMA((3 * n,)), pltpu.SemaphoreType.DMA((3 * n,)), pltpu.SemaphoreType.DMA((n,))]

    def _copies(self, ins, outs, send, recv, lsem):
        x, y, c, chips = _place()
        me = 2 * x + y
        local, out_going, in_coming = [], [], []
        for t in range(len(ins)):
            if self.kind == 'gather':
                src_local, srcs, dst_mine = ins[t], [ins[t]] * 3, outs[t].at[me]
            else:
                src_local, srcs, dst_mine = ins[t].at[me], [ins[t].at[2 * px + py] for px, py in chips], outs[t].at[me]
            local.append(pltpu.make_async_copy(src_local, dst_mine, lsem.at[t]))
            for r, (px, py) in enumerate(chips):
                out_going.append(pltpu.make_async_remote_copy(
                    src_ref=srcs[r], dst_ref=dst_mine, send_sem=send.at[3 * t + r], recv_sem=recv.at[3 * t + r],
                    device_id=(px, py, c), device_id_type=MESH))
                in_coming.append(pltpu.make_async_remote_copy(
                    src_ref=srcs[r], dst_ref=outs[t].at[2 * px + py], send_sem=send.at[3 * t + r], recv_sem=recv.at[3 * t + r],
                    device_id=(px, py, c), device_id_type=MESH))
        return local, out_going, in_coming

    def start(self, ins, outs, send, recv, lsem):
        local, out_going, _ = self._copies(ins, outs, send, recv, lsem)
        for cp in local + out_going:
            cp.start()

    def wait(self, ins, outs, send, recv, lsem):
        local, out_going, in_coming = self._copies(ins, outs, send, recv, lsem)
        for cp in in_coming:
            cp.wait_recv()
        for cp in out_going:
            cp.wait_send()
        for cp in local:
            cp.wait()


def _gather_side(shards):
    return Side(shards, [jax.ShapeDtypeStruct((N_CHIPS,) + s.shape, s.dtype) for s in shards], 'gather')


def _scatter_side(grads):
    return Side(grads, [jax.ShapeDtypeStruct(g.shape, g.dtype) for g in grads], 'scatter')


def _call_with_side(body, side, first, last, *, name, grid, in_specs, out_specs, out_shape, scratch_shapes, args):
    if side is None:
        outs = pl.pallas_call(body, name=name, grid=grid, in_specs=in_specs, out_specs=out_specs, out_shape=out_shape,
                              scratch_shapes=scratch_shapes, compiler_params=_cparams())(*args)
        return outs, []
    n_in, n_out, n_sc = len(in_specs), len(out_specs), len(scratch_shapes)
    ns_in, ns_out = len(side.ins), len(side.outs)

    def wrapped(*refs):
        base_in, s_in = refs[:n_in], refs[n_in:n_in + ns_in]
        o0 = n_in + ns_in
        base_out, s_out = refs[o0:o0 + n_out], refs[o0 + n_out:o0 + n_out + ns_out]
        sc0 = o0 + n_out + ns_out
        base_sc, sems = refs[sc0:sc0 + n_sc], refs[sc0 + n_sc:]

        @pl.when(first())
        def _():
            side.start(s_in, s_out, *sems)

        body(*base_in, *base_out, *base_sc)

        @pl.when(last())
        def _():
            side.wait(s_in, s_out, *sems)

    any_spec = pl.BlockSpec(memory_space=pl.ANY)
    outs = pl.pallas_call(
        wrapped, name=name, grid=grid, in_specs=list(in_specs) + [any_spec] * ns_in, out_specs=list(out_specs) + [any_spec] * ns_out,
        out_shape=list(out_shape) + side.outs, scratch_shapes=list(scratch_shapes) + side.sems, compiler_params=_cparams(),
    )(*args, *side.ins)
    return outs[:n_out], outs[n_out:]


def _run_side(name, side):
    def body(*refs):
        n = len(side.ins)
        side.start(refs[:n], refs[n:2 * n], *refs[2 * n:])
        side.wait(refs[:n], refs[n:2 * n], *refs[2 * n:])

    any_spec = pl.BlockSpec(memory_space=pl.ANY)
    return pl.pallas_call(body, name=name, in_specs=[any_spec] * len(side.ins), out_specs=[any_spec] * len(side.outs),
                          out_shape=side.outs, scratch_shapes=side.sems)(*side.ins)


def _gather_shards(shards, layer_major):
    n = len(shards)

    def body(*refs):
        ins, outs = refs[:n], refs[n:2 * n]
        send, recv, lsem = refs[2 * n:]
        x, y, c, chips = _place()
        me = 2 * x + y

        def slot(t, chip):
            return outs[t].at[:, chip] if layer_major[t] else outs[t].at[chip]

        local, sends = [], []
        for t in range(n):
            cp = pltpu.make_async_copy(ins[t], slot(t, me), lsem.at[t])
            cp.start()
            local.append(cp)
            for r, (px, py) in enumerate(chips):
                rc = pltpu.make_async_remote_copy(src_ref=ins[t], dst_ref=slot(t, me), send_sem=send.at[3 * t + r],
                                                  recv_sem=recv.at[3 * t + r], device_id=(px, py, c), device_id_type=MESH)
                rc.start()
                sends.append(rc)
        for t in range(n):
            for r, (px, py) in enumerate(chips):
                pltpu.make_async_remote_copy(src_ref=ins[t], dst_ref=slot(t, 2 * px + py), send_sem=send.at[3 * t + r],
                                             recv_sem=recv.at[3 * t + r], device_id=(px, py, c), device_id_type=MESH).wait_recv()
        for rc in sends:
            rc.wait_send()
        for cp in local:
            cp.wait()

    any_spec = pl.BlockSpec(memory_space=pl.ANY)
    return pl.pallas_call(
        body, name="gather_shards", in_specs=[any_spec] * n, out_specs=[any_spec] * n,
        out_shape=[jax.ShapeDtypeStruct((s.shape[0], N_CHIPS) + s.shape[1:] if lm else (N_CHIPS,) + s.shape, s.dtype)
                   for s, lm in zip(shards, layer_major)],
        scratch_shapes=[pltpu.SemaphoreType.DMA((3 * n,)), pltpu.SemaphoreType.DMA((3 * n,)), pltpu.SemaphoreType.DMA((n,))],
    )(*shards)


def _scatter_grads(groups):
    flat = [(gi, li, a) for gi, grp in enumerate(groups) for li, a in enumerate(grp)]
    n = len(flat)
    ng = len(groups)

    def body(*refs):
        ins, outs = refs[:n], refs[n:n + ng]
        send, recv, lsem = refs[n + ng:]
        x, y, c, chips = _place()
        me = 2 * x + y
        local, sends = [], []
        for t, (gi, li, _) in enumerate(flat):
            cp = pltpu.make_async_copy(ins[t].at[me], outs[gi].at[me, li], lsem.at[t])
            cp.start()
            local.append(cp)
            for r, (px, py) in enumerate(chips):
                rc = pltpu.make_async_remote_copy(src_ref=ins[t].at[2 * px + py], dst_ref=outs[gi].at[me, li],
                                                  send_sem=send.at[3 * t + r], recv_sem=recv.at[3 * t + r],
                                                  device_id=(px, py, c), device_id_type=MESH)
                rc.start()
                sends.append(rc)
        for t, (gi, li, _) in enumerate(flat):
            for r, (px, py) in enumerate(chips):
                pltpu.make_async_remote_copy(src_ref=ins[t].at[me], dst_ref=outs[gi].at[2 * px + py, li],
                                             send_sem=send.at[3 * t + r], recv_sem=recv.at[3 * t + r],
                                             device_id=(px, py, c), device_id_type=MESH).wait_recv()
        for rc in sends:
            rc.wait_send()
        for cp in local:
            cp.wait()

    any_spec = pl.BlockSpec(memory_space=pl.ANY)
    return pl.pallas_call(
        body, name="scatter_grads", in_specs=[any_spec] * n, out_specs=[any_spec] * ng,
        out_shape=[jax.ShapeDtypeStruct((N_CHIPS, len(grp)) + grp[0].shape[1:], grp[0].dtype) for grp in groups],
        scratch_shapes=[pltpu.SemaphoreType.DMA((3 * n,)), pltpu.SemaphoreType.DMA((3 * n,)), pltpu.SemaphoreType.DMA((n,))],
    )(*[a for _, _, a in flat])


def _swap_with_sibling(arrs):
    n = len(arrs)

    def body(*refs):
        ins, outs = refs[:n], refs[n:2 * n]
        send, recv = refs[2 * n:]
        x, y, c, _ = _place()
        cps = []
        for t in range(n):
            rc = pltpu.make_async_remote_copy(src_ref=ins[t], dst_ref=outs[t], send_sem=send.at[t], recv_sem=recv.at[t],
                                              device_id=(x, y, 1 - c), device_id_type=MESH)
            rc.start()
            cps.append(rc)
        for rc in cps:
            rc.wait_recv()
        for rc in cps:
            rc.wait_send()

    any_spec = pl.BlockSpec(memory_space=pl.ANY)
    return pl.pallas_call(
        body, name="swap_with_sibling", in_specs=[any_spec] * n, out_specs=[any_spec] * n,
        out_shape=[jax.ShapeDtypeStruct(a.shape, a.dtype) for a in arrs],
        scratch_shapes=[pltpu.SemaphoreType.DMA((n,)), pltpu.SemaphoreType.DMA((n,))],
    )(*arrs)


def _allreduce_small(v):
    rows, cols = v.shape

    def body(v_ref, o_ref, sib_ref, chip_ref, send, recv):
        x, y, c, chips = _place()
        me = 2 * x + y
        d2d = pltpu.make_async_remote_copy(src_ref=v_ref, dst_ref=sib_ref, send_sem=send.at[0], recv_sem=recv.at[0],
                                           device_id=(x, y, 1 - c), device_id_type=MESH)
        d2d.start()
        d2d.wait_recv()
        chip_ref[me] = v_ref[...] + sib_ref[...]
        sends = []
        for r, (px, py) in enumerate(chips):
            rc = pltpu.make_async_remote_copy(src_ref=chip_ref.at[me], dst_ref=chip_ref.at[me], send_sem=send.at[1 + r],
                                              recv_sem=recv.at[1 + r], device_id=(px, py, c), device_id_type=MESH)
            rc.start()
            sends.append(rc)
        for r, (px, py) in enumerate(chips):
            pltpu.make_async_remote_copy(src_ref=chip_ref.at[me], dst_ref=chip_ref.at[2 * px + py], send_sem=send.at[1 + r],
                                         recv_sem=recv.at[1 + r], device_id=(px, py, c), device_id_type=MESH).wait_recv()
        o_ref[...] = (chip_ref[0] + chip_ref[1]) + (chip_ref[2] + chip_ref[3])
        d2d.wait_send()
        for rc in sends:
            rc.wait_send()

    vm = pl.BlockSpec(memory_space=pltpu.VMEM)
    return pl.pallas_call(
        body, name="allreduce_small", in_specs=[vm], out_specs=vm, out_shape=jax.ShapeDtypeStruct((rows, cols), F32),
        scratch_shapes=[pltpu.VMEM((rows, cols), F32), pltpu.VMEM((N_CHIPS, rows, cols), F32), pltpu.SemaphoreType.DMA((4,)),
                        pltpu.SemaphoreType.DMA((4,))],
        compiler_params=_cparams(),
    )(v)


def _pack(tensors):
    pieces = []
    for t in tensors:
        flat = t.reshape(-1)
        pad = (-flat.shape[0]) % (SUBLANES * LANES)
        pieces.append(jnp.pad(flat, (0, pad)).reshape(-1, LANES))
    return jnp.concatenate(pieces, axis=0)


def _unpack(buf, like):
    out, off = [], 0
    for t in like:
        size = math.prod(t.shape)
        rows = -(-size // (SUBLANES * LANES)) * SUBLANES
        out.append(buf[off:off + rows].reshape(-1)[:size].reshape(t.shape))
        off += rows
    return out


def _s5_pack_b(bb):
    gc, g, p = bb.shape
    q = S5_GROUPS_PER_BLOCK
    t = bb.reshape(gc, g // q, q, p).transpose(1, 2, 0, 3)
    eye = jnp.eye(q, dtype=bb.dtype)
    return (t[:, :, :, None, :] * eye[None, :, None, :, None]).reshape(g // q, q * gc, q * p)


def _s5_unpack_b(dbp, gc, p):
    nb = dbp.shape[0]
    q = S5_GROUPS_PER_BLOCK
    eye = jnp.eye(q, dtype=dbp.dtype)
    t = (dbp.reshape(nb, q, gc, q, p) * eye[None, :, None, :, None]).sum(axis=3)
    return t.transpose(2, 0, 1, 3).reshape(gc, nb * q, p)


def _s5_pack_c(cc):
    g, gc, p = cc.shape
    q = S5_GROUPS_PER_BLOCK
    t = cc.reshape(g // q, q, gc, p).transpose(0, 1, 3, 2)
    eye = jnp.eye(q, dtype=cc.dtype)
    return (t[:, :, :, None, :] * eye[None, :, None, :, None]).reshape(g // q, q * p, q * gc)


def _s5_unpack_c(dcp, gc, p):
    nb = dcp.shape[0]
    q = S5_GROUPS_PER_BLOCK
    eye = jnp.eye(q, dtype=dcp.dtype)
    t = (dcp.reshape(nb, q, p, q, gc) * eye[None, :, None, :, None]).sum(axis=3)
    return t.transpose(0, 1, 3, 2).reshape(nb * q, gc, p)


def _split2(m):
    return m.arr[:, 0]


def kernel(x, norm_mix_g, norm_ffn_g, norm_final_g, rg_w_in, rg_conv_w, rg_conv_b, rg_w_a, rg_b_a, rg_w_x, rg_b_x, rg_lambda, rg_w_out, s5_w_in, s5_a_re, s5_a_im, s5_log_dt, s5_b_re, s5_b_im, s5_c_re, s5_c_im, s5_d, s5_w_glu, s5_w_out, ffn_w_up, ffn_conv_w, ffn_conv_b, ffn_w_down, loss_target, m_norm_mix_g, m_norm_ffn_g, m_norm_final_g, m_rg_w_in, m_rg_conv_w, m_rg_conv_b, m_rg_w_a, m_rg_b_a, m_rg_w_x, m_rg_b_x, m_rg_lambda, m_rg_w_out, m_s5_w_in, m_s5_a_re, m_s5_a_im, m_s5_log_dt, m_s5_b_re, m_s5_b_im, m_s5_c_re, m_s5_c_im, m_s5_d, m_s5_w_glu, m_s5_w_out, m_ffn_w_up, m_ffn_conv_w, m_ffn_conv_b, m_ffn_w_down, v_norm_mix_g, v_norm_ffn_g, v_norm_final_g, v_rg_w_in, v_rg_conv_w, v_rg_conv_b, v_rg_w_a, v_rg_b_a, v_rg_w_x, v_rg_b_x, v_rg_lambda, v_rg_w_out, v_s5_w_in, v_s5_a_re, v_s5_a_im, v_s5_log_dt, v_s5_b_re, v_s5_b_im, v_s5_c_re, v_s5_c_im, v_s5_d, v_s5_w_glu, v_s5_w_out, v_ffn_w_up, v_ffn_conv_w, v_ffn_conv_b, v_ffn_w_down):
    w = dict(zip(PARAM_NAMES, (norm_mix_g, norm_ffn_g, norm_final_g, rg_w_in, rg_conv_w, rg_conv_b, rg_w_a, rg_b_a, rg_w_x, rg_b_x,
                               rg_lambda, rg_w_out, s5_w_in, s5_a_re, s5_a_im, s5_log_dt, s5_b_re, s5_b_im, s5_c_re, s5_c_im, s5_d,
                               s5_w_glu, s5_w_out, ffn_w_up, ffn_conv_w, ffn_conv_b, ffn_w_down)))
    mom = dict(zip(PARAM_NAMES, (m_norm_mix_g, m_norm_ffn_g, m_norm_final_g, m_rg_w_in, m_rg_conv_w, m_rg_conv_b, m_rg_w_a, m_rg_b_a,
                                 m_rg_w_x, m_rg_b_x, m_rg_lambda, m_rg_w_out, m_s5_w_in, m_s5_a_re, m_s5_a_im, m_s5_log_dt, m_s5_b_re,
                                 m_s5_b_im, m_s5_c_re, m_s5_c_im, m_s5_d, m_s5_w_glu, m_s5_w_out, m_ffn_w_up, m_ffn_conv_w,
                                 m_ffn_conv_b, m_ffn_w_down)))
    vel = dict(zip(PARAM_NAMES, (v_norm_mix_g, v_norm_ffn_g, v_norm_final_g, v_rg_w_in, v_rg_conv_w, v_rg_conv_b, v_rg_w_a, v_rg_b_a,
                                 v_rg_w_x, v_rg_b_x, v_rg_lambda, v_rg_w_out, v_s5_w_in, v_s5_a_re, v_s5_a_im, v_s5_log_dt, v_s5_b_re,
                                 v_s5_b_im, v_s5_c_re, v_s5_c_im, v_s5_d, v_s5_w_glu, v_s5_w_out, v_ffn_w_up, v_ffn_conv_w,
                                 v_ffn_conv_b, v_ffn_w_down)))
    _, s, d = x.shape
    depth = norm_mix_g.shape[0]
    n_grp, n_state = s5_a_re.shape[1], s5_a_re.shape[2]
    gc = s5_b_re.shape[3]
    d_ff = ffn_w_down.shape[1] * N_CHIPS
    s5_ts = min(128, s)
    nlev = max(1, int(math.log2(s5_ts)))

    wb = {n: (w[n].astype(BF16) if n in BIG else w[n]) for n in SHARDED}
    gath = {}

    def mixer_names(i):
        return MIXER_SHARDED[i % 2]

    def gather_side(names, layer):
        return _gather_side([wb[n][layer] for n in names])

    def put(names, layer, arrs):
        for n, a in zip(names, arrs):
            gath[(n, layer)] = a

    def wcol(n, l):
        return Mat(gath[(n, l)][:, None], 0, 'c')

    def wrow(n, l):
        g = gath[(n, l)]
        return Mat(g.reshape(1, 1, N_CHIPS * g.shape[1], g.shape[2]), 0, 'c')

    def rg_cw(l):
        return gath[('rg_conv_w', l)].transpose(1, 0, 2).reshape(RG_CONV_W, d)

    def s5_dv(l):
        return gath[('s5_d', l)].reshape(1, d)

    def f_cw(l):
        return gath[('ffn_conv_w', l)].transpose(1, 0, 2).reshape(FFN_CONV_W, 2, d_ff).transpose(1, 0, 2)

    tm = min(1024, s)
    d_up = 2 * d_ff // N_CHIPS
    f_cb = ffn_conv_b.reshape(depth, 2, 1, d_ff)
    put(mixer_names(0), 0, _run_side("gather_first", gather_side(mixer_names(0), 0)))

    h = x.reshape(s, d)
    saved = []
    for i in range(depth):
        j = i // 2
        sv = {'h_in': h}
        hn = _rms_fwd(h, norm_mix_g[i:i + 1])
        sv['hn'] = hn
        if i % 2 == 0:
            xg = _mm("rg_in", 'nn', act(hn), wcol('rg_w_in', j), out_parts=2, tm=tm, tn=512, tk=d)
            xg2 = _split2(xg)
            wa, wx = rg_w_a[j].astype(BF16), rg_w_x[j].astype(BF16)
            ba, bx = rg_b_a[j].reshape(1, d), rg_b_x[j].reshape(1, d)
            (xr, hs, y), got = _rg_fwd(xg2, rg_cw(j), rg_conv_b[j:j + 1], wa, ba, wx, bx, rg_lambda[j:j + 1],
                                       side=gather_side(FFN_SHARDED, i))
            put(FFN_SHARDED, i, got)
            sv.update(xg2=xg2, xr=xr, hs=hs, y=y, wa=wa, wx=wx, ba=ba, bx=bx)
            h = _mm("rg_out", 'nn', act(y), wrow('rg_w_out', j), res=act(h), tm=tm, tn=d, tk=d).arr[0, 0]
        else:
            u = _mm("s5_in", 'nn', act(hn), wrow('s5_w_in', j), tm=tm, tn=d, tk=d).arr[0, 0]
            bt_re, bt_im = s5_b_re[j].transpose(2, 0, 1), s5_b_im[j].transpose(2, 0, 1)
            ldt = s5_log_dt[j].reshape(n_grp, 1)
            abr, abi, pw_r, pw_i, bbr, bbi = _s5_params(s5_a_re[j], s5_a_im[j], ldt, bt_re, bt_im, nlev)
            nn_ = n_grp * n_state
            prm = dict(abr=abr.reshape(1, nn_), abi=abi.reshape(1, nn_), pw_r=pw_r.reshape(nlev, nn_), pw_i=pw_i.reshape(nlev, nn_),
                       bp_r=_s5_pack_b(bbr).astype(BF16), bp_i=_s5_pack_b(bbi).astype(BF16),
                       cp_r=_s5_pack_c(s5_c_re[j]).astype(BF16), cp_i=_s5_pack_c(s5_c_im[j]).astype(BF16), dvec=s5_dv(j))
            (hr, hi, ypre, gy), got = _s5_fwd(u, ts=s5_ts, side=gather_side(FFN_SHARDED, i), **prm)
            put(FFN_SHARDED, i, got)
            gl = _mm("s5_glu", 'nn', act(gy), wcol('s5_w_glu', j), out_parts=2, tm=tm, tn=512, tk=d)
            gl2 = _split2(gl)
            o = _glu(gl2)
            sv.update(u=u, prm=prm, hr=hr, hi=hi, ypre=ypre, gy=gy, gl2=gl2, o=o, bt_re=bt_re, bt_im=bt_im, ldt=ldt)
            h = _mm("s5_out", 'nn', act(o), wrow('s5_w_out', j), res=act(h), tm=tm, tn=d, tk=d).arr[0, 0]
        sv['h_mid'] = h
        hn2 = _rms_fwd(h, norm_ffn_g[i:i + 1])
        up = _mm("ffn_up", 'nn', act(hn2), wcol('ffn_w_up', i), out_parts=2, tm=tm, tn=d_up, tk=d)
        up2 = _split2(up)
        nxt = i + 1
        a_ffn, got = _ffn_act(up2, f_cw(i), f_cb[i], side=gather_side(mixer_names(nxt), nxt // 2) if nxt < depth else None)
        if nxt < depth:
            put(mixer_names(nxt), nxt // 2, got)
        sv.update(hn2=hn2, up2=up2, act=a_ffn)
        h = _mm("ffn_down", 'nn', act(a_ffn), wrow('ffn_w_down', i), res=act(h), tm=tm, tn=d, tk=d_ff // 2).arr[0, 0]
        saved.append(sv)

    loss_row, dh, dg_final = _loss_and_grad(h, norm_final_g.reshape(1, d), loss_target.reshape(s, d))
    loss = lax.psum(loss_row[0, 0], ("x", "y", "c"))

    gl_ = {n: [None] * w[n].shape[0] for n in PARAM_NAMES if n != 'norm_final_g'}
    recvd = {}

    def as4(n, a):
        return a.reshape((N_CHIPS,) + w[n].shape[1:])

    def scatter_side(keys):
        return _scatter_side([as4(n, gl_[n][l]) for n, l in keys])

    def record(keys, arrs):
        for k, a in zip(keys, arrs):
            recvd[k] = a

    pending = None
    for i in reversed(range(depth)):
        j = i // 2
        sv = saved[i]
        dact = _mm("ffn_down_dx", 'nt', act(dh), wrow('ffn_w_down', i), tm=tm, tn=d_ff // 2, tk=d).arr[0, 0]
        gl_['ffn_w_down'][i] = _mm("ffn_down_dw", 'tn', act(sv['act']), act(dh), out_dtype=BF16, tm=d_ff // 2, tn=d, tk=tm).arr
        (dup2, dcw2, dcb2), got = _ffn_bwd(sv['up2'], dact, f_cw(i), f_cb[i], side=scatter_side(pending) if pending else None)
        if pending:
            record(pending, got)
        gl_['ffn_conv_w'][i] = dcw2.transpose(1, 0, 2).reshape(FFN_CONV_W, 2 * d_ff)
        gl_['ffn_conv_b'][i] = dcb2.reshape(2 * d_ff)
        dup = Mat(dup2[:, None], 0, 'c')
        gl_['ffn_w_up'][i] = _mm("ffn_up_dw", 'tn', act(sv['hn2']), dup, out_parts=N_CHIPS, out_dtype=BF16, tm=d, tn=d_up, tk=tm).arr
        dhn2 = _mm("ffn_up_dx", 'nt', dup, wcol('ffn_w_up', i), tm=tm, tn=d, tk=d_up).arr[0, 0]
        dh, dg = _rms_bwd(sv['h_mid'], norm_ffn_g[i:i + 1], dhn2, dh)
        gl_['norm_ffn_g'][i] = dg[0]
        ffn_keys = [('ffn_w_up', i), ('ffn_w_down', i)]
        if i % 2 == 0:
            dy = _mm("rg_out_dx", 'nt', act(dh), wrow('rg_w_out', j), tm=tm, tn=d, tk=d).arr[0, 0]
            gl_['rg_w_out'][j] = _mm("rg_out_dw", 'tn', act(sv['y']), act(dh), out_dtype=BF16, tm=d, tn=d, tk=tm).arr
            (dxg2, dcw, dcb, dwa, dba, dwx, dbx, dlam), got = _rg_bwd(
                dy, sv['xg2'], sv['xr'], sv['hs'], rg_cw(j), sv['wa'], sv['ba'], sv['wx'], sv['bx'], rg_lambda[j:j + 1],
                side=scatter_side(ffn_keys))
            record(ffn_keys, got)
            gl_['rg_conv_w'][j] = dcw
            gl_['rg_conv_b'][j] = dcb[0]
            gl_['rg_w_a'][j], gl_['rg_w_x'][j] = dwa, dwx
            gl_['rg_b_a'][j], gl_['rg_b_x'][j] = dba.reshape(rg_b_a.shape[1:]), dbx.reshape(rg_b_x.shape[1:])
            gl_['rg_lambda'][j] = dlam[0]
            dxg = Mat(dxg2[:, None], 0, 'c')
            gl_['rg_w_in'][j] = _mm("rg_in_dw", 'tn', act(sv['hn']), dxg, out_parts=N_CHIPS, out_dtype=BF16, tm=d, tn=512, tk=tm).arr
            dhn = _mm("rg_in_dx", 'nt', dxg, wcol('rg_w_in', j), tm=tm, tn=d, tk=512).arr[0, 0]
            pending = [('rg_w_in', j), ('rg_w_out', j)]
        else:
            d_o = _mm("s5_out_dx", 'nt', act(dh), wrow('s5_w_out', j), tm=tm, tn=d, tk=d).arr[0, 0]
            gl_['s5_w_out'][j] = _mm("s5_out_dw", 'tn', act(sv['o']), act(dh), out_dtype=BF16, tm=d, tn=d, tk=tm).arr
            dgl2 = _glu_bwd(sv['gl2'], d_o)
            dgl = Mat(dgl2[:, None], 0, 'c')
            gl_['s5_w_glu'][j] = _mm("s5_glu_dw", 'tn', act(sv['gy']), dgl, out_parts=N_CHIPS, out_dtype=BF16, tm=d, tn=512, tk=tm).arr
            dgy = _mm("s5_glu_dx", 'nt', dgl, wcol('s5_w_glu', j), tm=tm, tn=d, tk=512).arr[0, 0]
            (du, dar, dai, dbpr, dbpi, dcpr, dcpi, dd), got = _s5_bwd(dgy, sv['ypre'], sv['u'], sv['hr'], sv['hi'], ts=s5_ts,
                                                                      side=scatter_side(ffn_keys), **sv['prm'])
            record(ffn_keys, got)
            gl_['s5_d'][j] = dd[0]
            gl_['s5_c_re'][j] = _s5_unpack_c(dcpr, gc, n_state)
            gl_['s5_c_im'][j] = -_s5_unpack_c(dcpi, gc, n_state)
            d_are, d_aim, d_ldt, d_btr, d_bti = _s5_params_bwd(
                s5_a_re[j], s5_a_im[j], sv['ldt'], sv['bt_re'], sv['bt_im'], dar.reshape(n_grp, n_state), dai.reshape(n_grp, n_state),
                _s5_unpack_b(dbpr, gc, n_state), _s5_unpack_b(dbpi, gc, n_state))
            gl_['s5_a_re'][j], gl_['s5_a_im'][j], gl_['s5_log_dt'][j] = d_are, d_aim, d_ldt[:, 0]
            gl_['s5_b_re'][j], gl_['s5_b_im'][j] = d_btr.transpose(1, 2, 0), d_bti.transpose(1, 2, 0)
            dum = act(du)
            gl_['s5_w_in'][j] = _mm("s5_in_dw", 'tn', act(sv['hn']), dum, out_dtype=BF16, tm=d, tn=d, tk=tm).arr
            dhn = _mm("s5_in_dx", 'nt', dum, wrow('s5_w_in', j), tm=tm, tn=d, tk=d).arr[0, 0]
            pending = [('s5_w_in', j), ('s5_w_glu', j), ('s5_w_out', j)]
        dh, dg = _rms_bwd(sv['h_in'], norm_mix_g[i:i + 1], dhn, dh)
        gl_['norm_mix_g'][i] = dg[0]
    grad_x = dh.reshape(x.shape)
    record(pending, _run_side("scatter_last", scatter_side(pending)))

    chip_sums = []
    for n in BIG:
        cols = w[n].shape[-1]
        chip_sums.append(_sum_parts([recvd[(n, l)].reshape(N_CHIPS, -1, cols) for l in range(w[n].shape[0])]))
    sib_sums = _swap_with_sibling(chip_sums)
    results = {}
    for n, mine, theirs in zip(BIG, chip_sums, sib_sums):
        cols = w[n].shape[-1]
        outs = _adamw(w[n].reshape(-1, cols), [mine.reshape(-1, cols), theirs.reshape(-1, cols)], mom[n].reshape(-1, cols),
                      vel[n].reshape(-1, cols))
        results[n] = [o.reshape(w[n].shape) for o in outs]

    small = REPLICATED + SMALL_SHARDED
    local = [dg_final.reshape(d) if n == 'norm_final_g' else jnp.stack(gl_[n]) for n in small]
    summed = _unpack(_allreduce_small(_pack(local)), local)
    me = 2 * lax.axis_index("x") + lax.axis_index("y")
    grads = [lax.dynamic_slice_in_dim(g, me * w[n].shape[-1], w[n].shape[-1], axis=g.ndim - 1) if n in SMALL_SHARDED else g
             for n, g in zip(small, summed)]
    like = [w[n] for n in small]
    outs = _adamw(_pack(like), [_pack(grads)], _pack([mom[n] for n in small]), _pack([vel[n] for n in small]))
    unpacked = [_unpack(o, like) for o in outs]
    for k, n in enumerate(small):
        results[n] = [unpacked[q][k] for q in range(4)]

    return (loss, grad_x, *[results[n][0] for n in PARAM_NAMES], *[results[n][1] for n in PARAM_NAMES],
            *[results[n][2] for n in PARAM_NAMES], *[results[n][3] for n in PARAM_NAMES])
```

```python
import functools
import math

import jax
import jax.numpy as jnp
from jax import lax
from jax.experimental import pallas as pl
from jax.experimental.pallas import tpu as pltpu

F32 = jnp.float32
BF16 = jnp.bfloat16
MESH = pl.DeviceIdType.MESH

NORM_EPS = 1e-6
RG_HEADS = 8
RG_CONV_W = 4
RG_C = 8.0
S5_GC = 16
S5_P = 64
S5_GROUPS_PER_BLOCK = 8
FFN_CONV_W = 3
N_CHIPS = 4
ADAM_LR, ADAM_B1, ADAM_B2, ADAM_EPS, ADAM_WD, ADAM_STEP = 0.001, 0.9, 0.999, 1e-08, 0.01, 10
VMEM_LIMIT_BYTES = 56 * 1024 * 1024
SUBLANES = 8
LANES = 128

PARAM_NAMES = ['norm_mix_g', 'norm_ffn_g', 'norm_final_g', 'rg_w_in', 'rg_conv_w', 'rg_conv_b', 'rg_w_a', 'rg_b_a', 'rg_w_x',
               'rg_b_x', 'rg_lambda', 'rg_w_out', 's5_w_in', 's5_a_re', 's5_a_im', 's5_log_dt', 's5_b_re', 's5_b_im', 's5_c_re',
               's5_c_im', 's5_d', 's5_w_glu', 's5_w_out', 'ffn_w_up', 'ffn_conv_w', 'ffn_conv_b', 'ffn_w_down']
SHARDED = ['rg_w_in', 'rg_conv_w', 'rg_w_out', 's5_w_in', 's5_d', 's5_w_glu', 's5_w_out', 'ffn_w_up', 'ffn_conv_w', 'ffn_w_down']
BIG = ['rg_w_in', 'rg_w_out', 's5_w_in', 's5_w_glu', 's5_w_out', 'ffn_w_up', 'ffn_w_down']
ROW_SHARDED = ['rg_w_out', 's5_w_in', 's5_w_out', 'ffn_w_down']
SMALL_SHARDED = ['rg_conv_w', 's5_d', 'ffn_conv_w']
MIXER_SHARDED = [['rg_w_in', 'rg_conv_w', 'rg_w_out'], ['s5_w_in', 's5_d', 's5_w_glu', 's5_w_out']]
FFN_SHARDED = ['ffn_w_up', 'ffn_conv_w', 'ffn_w_down']
REPLICATED = [n for n in PARAM_NAMES if n not in SHARDED]


def _cparams():
    return pltpu.CompilerParams(vmem_limit_bytes=VMEM_LIMIT_BYTES)


_GELU_C = math.sqrt(2.0 / math.pi)
_GELU_K = 0.044715


def _gelu(x):
    return 0.5 * x * (1.0 + jnp.tanh(_GELU_C * (x + _GELU_K * x * x * x)))


def _gelu_and_grad(x):
    t = jnp.tanh(_GELU_C * (x + _GELU_K * x * x * x))
    g = 0.5 * x * (1.0 + t)
    dg = 0.5 * (1.0 + t) + 0.5 * x * (1.0 - t * t) * (_GELU_C * (1.0 + 3.0 * _GELU_K * x * x))
    return g, dg


def _sigmoid(x):
    return jax.nn.sigmoid(x)


def _neg_expm1(x):
    series = -(x * (1.0 + x * (0.5 + x * (1.0 / 6 + x * (1.0 / 24 + x * (1.0 / 120 + x * (1.0 / 720)))))))
    return jnp.where(x > -0.25, series, 1.0 - jnp.exp(x))


def _softplus(z):
    return jnp.maximum(z, 0.0) + jnp.log1p(jnp.exp(-jnp.abs(z)))


def _rows(shape):
    return lax.broadcasted_iota(jnp.int32, shape, 0)


def _shift_down(x, halo, k):
    ext = jnp.concatenate([halo, x], axis=0)
    return pltpu.roll(ext, k, 0)[SUBLANES:]


def _shift_up(x, halo, k):
    ext = jnp.concatenate([x, halo], axis=0)
    n = ext.shape[0]
    return pltpu.roll(ext, n - k, 0)[:x.shape[0]]


def _scan_real_fwd(a, b):
    n = a.shape[0]
    row = _rows(a.shape)
    sh = 1
    while sh < n:
        ok = row >= sh
        b = a * jnp.where(ok, pltpu.roll(b, sh, 0), 0.0) + b
        if sh * 2 < n:
            a = a * jnp.where(ok, pltpu.roll(a, sh, 0), 1.0)
        sh *= 2
    return b


def _scan_real_rev(c, d):
    n = c.shape[0]
    row = _rows(c.shape)
    sh = 1
    while sh < n:
        ok = row < n - sh
        d = c * jnp.where(ok, pltpu.roll(d, n - sh, 0), 0.0) + d
        if sh * 2 < n:
            c = c * jnp.where(ok, pltpu.roll(c, n - sh, 0), 1.0)
        sh *= 2
    return d


def _scan_cplx(br, bi, pr_ref, pi_ref, reverse):
    n = br.shape[0]
    row = _rows(br.shape)
    sh, k = 1, 0
    while sh < n:
        pr = pr_ref[k:k + 1, :]
        pi = pi_ref[k:k + 1, :]
        if reverse:
            ok = row < n - sh
            sr = jnp.where(ok, pltpu.roll(br, n - sh, 0), 0.0)
            si = jnp.where(ok, pltpu.roll(bi, n - sh, 0), 0.0)
        else:
            ok = row >= sh
            sr = jnp.where(ok, pltpu.roll(br, sh, 0), 0.0)
            si = jnp.where(ok, pltpu.roll(bi, sh, 0), 0.0)
        br, bi = br + pr * sr - pi * si, bi + pr * si + pi * sr
        sh *= 2
        k += 1
    return br, bi


class Mat:
    def __init__(self, arr, l=0, split='c'):
        assert arr.ndim == 4
        self.arr, self.l, self.split = arr, l, split
        p, _, r, c = arr.shape
        self.shape = (r, c * p) if split == 'c' else (r * p, c)

    def spec(self, tr, tc, rc):
        p, _, r, c = self.arr.shape
        l = self.l
        assert r % tr == 0 and c % tc == 0, (self.arr.shape, tr, tc)
        if self.split == 'c':
            per = c // tc
            return pl.BlockSpec((None, None, tr, tc), lambda i, j, k: (rc(i, j, k)[1] // per, l, rc(i, j, k)[0], rc(i, j, k)[1] % per))
        per = r // tr
        return pl.BlockSpec((None, None, tr, tc), lambda i, j, k: (rc(i, j, k)[0] // per, l, rc(i, j, k)[0] % per, rc(i, j, k)[1]))


def act(x, parts=1):
    s, c = x.shape
    return Mat(x.reshape(s, parts, c // parts).transpose(1, 0, 2)[:, None] if parts > 1 else x[None, None])


def _mm(name, mode, a, b, *, out_parts=1, out_split='c', out_dtype=F32, res=None, tm=512, tn=512, tk=512):
    if mode == 'nn':
        (m, kk), (kb, n) = a.shape, b.shape
    elif mode == 'nt':
        (m, kk), (n, kb) = a.shape, b.shape
    else:
        (kk, m), (kb, n) = a.shape, b.shape
    assert kk == kb, (name, a.shape, b.shape)
    tm, tn, tk = min(tm, m), min(tn, n), min(tk, kk)
    assert m % tm == 0 and n % tn == 0 and kk % tk == 0, (name, m, n, kk, tm, tn, tk)
    nk = kk // tk
    if mode == 'nn':
        a_spec = a.spec(tm, tk, lambda i, j, k: (i, k))
        b_spec = b.spec(tk, tn, lambda i, j, k: (k, j))
        dims = (((1,), (0,)), ((), ()))
    elif mode == 'nt':
        a_spec = a.spec(tm, tk, lambda i, j, k: (i, k))
        b_spec = b.spec(tn, tk, lambda i, j, k: (j, k))
        dims = (((1,), (1,)), ((), ()))
    else:
        a_spec = a.spec(tk, tm, lambda i, j, k: (k, i))
        b_spec = b.spec(tk, tn, lambda i, j, k: (k, j))
        dims = (((0,), (0,)), ((), ()))
    if out_split == 'c':
        out_arr = jax.ShapeDtypeStruct((out_parts, 1, m, n // out_parts), out_dtype)
    else:
        out_arr = jax.ShapeDtypeStruct((out_parts, 1, m // out_parts, n), out_dtype)
    out_mat = Mat(out_arr, 0, out_split)
    o_spec = out_mat.spec(tm, tn, lambda i, j, k: (i, j))
    has_res = res is not None

    def body(*refs):
        if has_res:
            a_ref, b_ref, r_ref, o_ref = refs[:4]
        else:
            a_ref, b_ref, o_ref = refs[:3]
        prod = lax.dot_general(a_ref[...].astype(BF16), b_ref[...].astype(BF16), dims, preferred_element_type=F32)

        def finish(acc):
            if has_res:
                acc = acc + r_ref[...]
            o_ref[...] = acc.astype(out_dtype)

        if nk == 1:
            finish(prod)
        else:
            acc_ref = refs[-1]
            k = pl.program_id(2)

            @pl.when(k == 0)
            def _():
                acc_ref[...] = prod

            @pl.when(k > 0)
            def _():
                acc_ref[...] += prod

            @pl.when(k == nk - 1)
            def _():
                finish(acc_ref[...])

    in_specs = [a_spec, b_spec]
    args = [a.arr, b.arr]
    if has_res:
        in_specs.append(res.spec(tm, tn, lambda i, j, k: (i, j)))
        args.append(res.arr)
    out = pl.pallas_call(
        body, name=name, grid=(m // tm, n // tn, nk), in_specs=in_specs, out_specs=o_spec, out_shape=out_arr,
        scratch_shapes=[pltpu.VMEM((tm, tn), F32)] if nk > 1 else [], compiler_params=_cparams(),
    )(*args)
    return Mat(out, 0, out_split)


def _rms_fwd(h, g, ts=512):
    s, d = h.shape
    ts = min(ts, s)

    def body(h_ref, g_ref, o_ref):
        x = h_ref[...]
        var = jnp.mean(x * x, axis=-1, keepdims=True)
        o_ref[...] = (x * lax.rsqrt(var + NORM_EPS) * g_ref[...]).astype(BF16)

    return pl.pallas_call(
        body, name="rms_fwd", grid=(s // ts,),
        in_specs=[pl.BlockSpec((ts, d), lambda i: (i, 0)), pl.BlockSpec((1, d), lambda i: (0, 0))],
        out_specs=pl.BlockSpec((ts, d), lambda i: (i, 0)), out_shape=jax.ShapeDtypeStruct((s, d), BF16),
        compiler_params=_cparams(),
    )(h, g)


def _rms_bwd(h, g, dhn, dh_in, ts=512):
    s, d = h.shape
    ts = min(ts, s)

    def body(h_ref, g_ref, dhn_ref, dhin_ref, dh_ref, dg_ref):
        i = pl.program_id(0)
        x = h_ref[...]
        rstd = lax.rsqrt(jnp.mean(x * x, axis=-1, keepdims=True) + NORM_EPS)
        xhat = x * rstd
        dhn_v = dhn_ref[...]
        dxh = dhn_v * g_ref[...]
        dh_ref[...] = dhin_ref[...] + rstd * (dxh - xhat * jnp.mean(dxh * xhat, axis=-1, keepdims=True))
        part = jnp.sum(dhn_v * xhat, axis=0, keepdims=True)

        @pl.when(i == 0)
        def _():
            dg_ref[...] = part

        @pl.when(i > 0)
        def _():
            dg_ref[...] += part

    row = pl.BlockSpec((ts, d), lambda i: (i, 0))
    vec = pl.BlockSpec((1, d), lambda i: (0, 0))
    return pl.pallas_call(
        body, name="rms_bwd", grid=(s // ts,), in_specs=[row, vec, row, row], out_specs=[row, vec],
        out_shape=[jax.ShapeDtypeStruct((s, d), F32), jax.ShapeDtypeStruct((1, d), F32)], compiler_params=_cparams(),
    )(h, g, dhn, dh_in)


def _loss_and_grad(h, g, tgt, ts=512):
    s, d = h.shape
    ts = min(ts, s)

    def body(h_ref, g_ref, t_ref, loss_ref, dh_ref, dg_ref):
        i = pl.program_id(0)
        x = h_ref[...]
        gv = g_ref[...]
        rstd = lax.rsqrt(jnp.mean(x * x, axis=-1, keepdims=True) + NORM_EPS)
        xhat = x * rstd
        err = xhat * gv - t_ref[...]
        dy = err * (1.0 / d)
        dxh = dy * gv
        dh_ref[...] = rstd * (dxh - xhat * jnp.mean(dxh * xhat, axis=-1, keepdims=True))
        part = jnp.sum(dy * xhat, axis=0, keepdims=True)
        lpart = jnp.broadcast_to(jnp.sum(jnp.sum(err * err, axis=0, keepdims=True), axis=1, keepdims=True) * (0.5 / d), (1, LANES))

        @pl.when(i == 0)
        def _():
            dg_ref[...] = part
            loss_ref[...] = lpart

        @pl.when(i > 0)
        def _():
            dg_ref[...] += part
            loss_ref[...] += lpart

    row = pl.BlockSpec((ts, d), lambda i: (i, 0))
    vec = pl.BlockSpec((1, d), lambda i: (0, 0))
    return pl.pallas_call(
        body, name="loss_and_grad", grid=(s // ts,), in_specs=[row, vec, row],
        out_specs=[pl.BlockSpec((1, LANES), lambda i: (0, 0)), row, vec],
        out_shape=[jax.ShapeDtypeStruct((1, LANES), F32), jax.ShapeDtypeStruct((s, d), F32), jax.ShapeDtypeStruct((1, d), F32)],
        compiler_params=_cparams(),
    )(h, g, tgt)


def _halo_before(ts, nrow8):
    return lambda i: jnp.maximum(i * (ts // SUBLANES) - 1, 0)


def _ffn_act(up2, conv_w2, conv_b2, ts=512, tn=512, side=None):
    _, s, f = up2.shape
    ts, tn = min(ts, s), min(tn, f)
    kw = FFN_CONV_W

    def body(up_ref, halo_ref, w_ref, b_ref, o_ref):
        i = pl.program_id(0)
        cs = []
        for h in range(2):
            x = up_ref[h]
            halo = jnp.where(i == 0, 0.0, halo_ref[h])
            c = b_ref[h] + w_ref[h, kw - 1:kw, :] * x
            for sft in range(1, kw):
                c = c + w_ref[h, kw - 1 - sft:kw - sft, :] * _shift_down(x, halo, sft)
            cs.append(c)
        o_ref[...] = (_gelu(cs[0]) * cs[1]).astype(BF16)

    hb = ts // SUBLANES
    g0, g1 = s // ts, f // tn
    outs, side_outs = _call_with_side(
        body, side, lambda: (pl.program_id(0) == 0) & (pl.program_id(1) == 0),
        lambda: (pl.program_id(0) == g0 - 1) & (pl.program_id(1) == g1 - 1),
        name="ffn_act", grid=(g0, g1),
        in_specs=[pl.BlockSpec((2, ts, tn), lambda i, j: (0, i, j)),
                  pl.BlockSpec((2, SUBLANES, tn), lambda i, j: (0, jnp.maximum(i * hb - 1, 0), j)),
                  pl.BlockSpec((2, kw, tn), lambda i, j: (0, 0, j)),
                  pl.BlockSpec((2, 1, tn), lambda i, j: (0, 0, j))],
        out_specs=[pl.BlockSpec((ts, tn), lambda i, j: (i, j))], out_shape=[jax.ShapeDtypeStruct((s, f), BF16)],
        scratch_shapes=[], args=(up2, up2, conv_w2, conv_b2))
    return outs[0], side_outs


def _ffn_bwd(up2, dact, conv_w2, conv_b2, ts=256, tn=512, side=None):
    _, s, f = up2.shape
    ts, tn = min(ts, s), min(tn, f)
    kw = FFN_CONV_W
    nt = s // ts
    hb = ts // SUBLANES
    last8 = s // SUBLANES - 1

    def body(up_ref, hb_ref, ha_ref, da_ref, dah_ref, w_ref, b_ref, dup_ref, dw_ref, db_ref):
        i = pl.program_id(1)
        first, last = i == 0, i == nt - 1
        ce, xs = [], []
        for h in range(2):
            x = up_ref[h]
            before = jnp.where(first, 0.0, hb_ref[h])
            after = ha_ref[h]
            ext = jnp.concatenate([before, x, after], axis=0)
            c = b_ref[h] + w_ref[h, kw - 1:kw, :] * ext
            shifted = [ext]
            for sft in range(1, kw):
                sh = pltpu.roll(ext, sft, 0)
                shifted.append(sh)
                c = c + w_ref[h, kw - 1 - sft:kw - sft, :] * sh
            ce.append(c[SUBLANES:])
            xs.append([sh[SUBLANES:SUBLANES + ts] for sh in shifted])
        da = jnp.concatenate([da_ref[...], jnp.where(last, 0.0, dah_ref[...])], axis=0)
        g1, dg1 = _gelu_and_grad(ce[0])
        dcs = [da * ce[1] * dg1, da * g1]
        for h in range(2):
            dc = dcs[h]
            n = dc.shape[0]
            dup = w_ref[h, kw - 1:kw, :] * dc[:ts]
            for sft in range(1, kw):
                dup = dup + w_ref[h, kw - 1 - sft:kw - sft, :] * pltpu.roll(dc, n - sft, 0)[:ts]
            dup_ref[h] = dup.astype(BF16)
            dct = dc[:ts]
            dbp = jnp.sum(dct, axis=0, keepdims=True)
            dwp = [jnp.sum(dct * xs[h][kw - 1 - k], axis=0, keepdims=True) for k in range(kw)]

            @pl.when(first)
            def _():
                db_ref[h] = dbp
                for k in range(kw):
                    dw_ref[h, k:k + 1, :] = dwp[k]

            @pl.when(i > 0)
            def _():
                db_ref[h] += dbp
                for k in range(kw):
                    dw_ref[h, k:k + 1, :] += dwp[k]

    g0 = f // tn
    return _call_with_side(
        body, side, lambda: (pl.program_id(0) == 0) & (pl.program_id(1) == 0),
        lambda: (pl.program_id(0) == g0 - 1) & (pl.program_id(1) == nt - 1),
        name="ffn_bwd", grid=(g0, nt),
        in_specs=[pl.BlockSpec((2, ts, tn), lambda j, i: (0, i, j)),
                  pl.BlockSpec((2, SUBLANES, tn), lambda j, i: (0, jnp.maximum(i * hb - 1, 0), j)),
                  pl.BlockSpec((2, SUBLANES, tn), lambda j, i: (0, jnp.minimum((i + 1) * hb, last8), j)),
                  pl.BlockSpec((ts, tn), lambda j, i: (i, j)),
                  pl.BlockSpec((SUBLANES, tn), lambda j, i: (jnp.minimum((i + 1) * hb, last8), j)),
                  pl.BlockSpec((2, kw, tn), lambda j, i: (0, 0, j)),
                  pl.BlockSpec((2, 1, tn), lambda j, i: (0, 0, j))],
        out_specs=[pl.BlockSpec((2, ts, tn), lambda j, i: (0, i, j)),
                   pl.BlockSpec((2, kw, tn), lambda j, i: (0, 0, j)),
                   pl.BlockSpec((2, 1, tn), lambda j, i: (0, 0, j))],
        out_shape=[jax.ShapeDtypeStruct((2, s, f), BF16), jax.ShapeDtypeStruct((2, kw, f), F32),
                   jax.ShapeDtypeStruct((2, 1, f), F32)],
        scratch_shapes=[], args=(up2, up2, up2, dact, dact, conv_w2, conv_b2))


def _rg_gates(xr, wa_ref, ba_ref, wx_ref, bx_ref, lam_ref):
    bw = wa_ref.shape[-1]
    xb = xr.astype(BF16)
    za = jnp.concatenate([jnp.dot(xb[:, h * bw:(h + 1) * bw], wa_ref[h], preferred_element_type=F32)
                          for h in range(RG_HEADS)], axis=1) + ba_ref[...]
    zx = jnp.concatenate([jnp.dot(xb[:, h * bw:(h + 1) * bw], wx_ref[h], preferred_element_type=F32)
                          for h in range(RG_HEADS)], axis=1) + bx_ref[...]
    r, ig = _sigmoid(za), _sigmoid(zx)
    sp = _softplus(-lam_ref[...])
    la = -RG_C * r * sp
    a = jnp.exp(la)
    mult = jnp.sqrt(_neg_expm1(2.0 * la))
    return xb, r, ig, sp, a, mult


def _rg_fwd(xg2, conv_w, conv_b, w_a, b_a, w_x, b_x, lam, ts=256, side=None):
    _, s, c = xg2.shape
    ts = min(ts, s)
    kw = RG_CONV_W
    hb = ts // SUBLANES

    def body(xg_ref, halo_ref, cw_ref, cb_ref, wa_ref, ba_ref, wx_ref, bx_ref, lam_ref, xr_ref, hs_ref, y_ref, carry_ref):
        i = pl.program_id(0)

        @pl.when(i == 0)
        def _():
            carry_ref[...] = jnp.zeros_like(carry_ref)

        xp = xg_ref[0]
        halo = jnp.where(i == 0, 0.0, halo_ref[...])
        xr = cb_ref[...] + cw_ref[kw - 1:kw, :] * xp
        for sft in range(1, kw):
            xr = xr + cw_ref[kw - 1 - sft:kw - sft, :] * _shift_down(xp, halo, sft)
        _, r, ig, sp, a, mult = _rg_gates(xr, wa_ref, ba_ref, wx_ref, bx_ref, lam_ref)
        bt = mult * (ig * xr)
        row = _rows(bt.shape)
        bt = bt + jnp.where(row == 0, a * carry_ref[SUBLANES - 1:SUBLANES, :], 0.0)
        hs = _scan_real_fwd(a, bt)
        carry_ref[...] = hs[ts - SUBLANES:, :]
        xr_ref[...] = xr
        hs_ref[...] = hs
        y_ref[...] = (hs * _gelu(xg_ref[1])).astype(BF16)

    full = lambda shape: pl.BlockSpec(shape, lambda i: (0,) * len(shape))
    row_spec = pl.BlockSpec((ts, c), lambda i: (i, 0))
    nt = s // ts
    return _call_with_side(
        body, side, lambda: pl.program_id(0) == 0, lambda: pl.program_id(0) == nt - 1,
        name="rg_fwd", grid=(nt,),
        in_specs=[pl.BlockSpec((2, ts, c), lambda i: (0, i, 0)),
                  pl.BlockSpec((None, SUBLANES, c), lambda i: (0, jnp.maximum(i * hb - 1, 0), 0)),
                  full(conv_w.shape), full(conv_b.shape), full(w_a.shape), full(b_a.shape), full(w_x.shape), full(b_x.shape),
                  full(lam.shape)],
        out_specs=[row_spec, row_spec, row_spec],
        out_shape=[jax.ShapeDtypeStruct((s, c), F32), jax.ShapeDtypeStruct((s, c), F32), jax.ShapeDtypeStruct((s, c), BF16)],
        scratch_shapes=[pltpu.VMEM((SUBLANES, c), F32)], args=(xg2, xg2, conv_w, conv_b, w_a, b_a, w_x, b_x, lam))


def _rg_bwd(dy, xg2, xr, hs, conv_w, w_a, b_a, w_x, b_x, lam, ts=256, side=None):
    _, s, c = xg2.shape
    ts = min(ts, s)
    nt = s // ts
    kw = RG_CONV_W
    hb = ts // SUBLANES
    bw = c // RG_HEADS
    tn_dims = (((0,), (0,)), ((), ()))
    nt_dims = (((1,), (1,)), ((), ()))

    def body(dy_ref, xg_ref, xph_ref, xr_ref, hs_ref, hsh_ref, cw_ref, wa_ref, ba_ref, wx_ref, bx_ref, lam_ref,
             dxg_ref, dcw_ref, dcb_ref, dwa_ref, dba_ref, dwx_ref, dbx_ref, dlam_ref,
             lam_carry, a_carry, dxr_carry, dsp_acc):
        i = pl.program_id(0)
        first_step = i == 0
        time_first = i == nt - 1

        @pl.when(first_step)
        def _():
            lam_carry[...] = jnp.zeros_like(lam_carry)
            a_carry[...] = jnp.ones_like(a_carry)
            dxr_carry[...] = jnp.zeros_like(dxr_carry)
            dsp_acc[...] = jnp.zeros_like(dsp_acc)
            for ref in (dcw_ref, dcb_ref, dwa_ref, dba_ref, dwx_ref, dbx_ref):
                ref[...] = jnp.zeros_like(ref)

        xr = xr_ref[...]
        hs = hs_ref[...]
        gate = xg_ref[1]
        xb, r, ig, sp, a, mult = _rg_gates(xr, wa_ref, ba_ref, wx_ref, bx_ref, lam_ref)
        dyv = dy_ref[...]
        gg, dgg = _gelu_and_grad(gate)
        dhs = dyv * gg
        dxg_ref[1] = (dyv * hs * dgg).astype(BF16)
        row = _rows(xr.shape)
        coef = jnp.where(row == ts - 1, a_carry[0:1, :], pltpu.roll(a, ts - 1, 0))
        dhs = dhs + jnp.where(row == ts - 1, coef * lam_carry[0:1, :], 0.0)
        lmb = _scan_real_rev(coef, dhs)
        lam_carry[...] = lmb[:SUBLANES]
        a_carry[...] = a[:SUBLANES]
        hs_prev = _shift_down(hs, jnp.where(time_first, 0.0, hsh_ref[...]), 1)
        d_a = lmb * hs_prev
        d_m = lmb * (ig * xr)
        d_ig = lmb * mult * xr
        d_xr = lmb * mult * ig
        d_la = a * d_a - (a * a / mult) * d_m
        dsp_acc[...] += jnp.sum(-RG_C * r * d_la, axis=0, keepdims=True)
        d_za = (-RG_C * sp) * d_la * r * (1.0 - r)
        d_zx = d_ig * ig * (1.0 - ig)
        dba_ref[...] += jnp.sum(d_za, axis=0, keepdims=True)
        dbx_ref[...] += jnp.sum(d_zx, axis=0, keepdims=True)
        dzab, dzxb = d_za.astype(BF16), d_zx.astype(BF16)
        back = []
        for h in range(RG_HEADS):
            sl = slice(h * bw, (h + 1) * bw)
            dwa_ref[h] += lax.dot_general(xb[:, sl], dzab[:, sl], tn_dims, preferred_element_type=F32)
            dwx_ref[h] += lax.dot_general(xb[:, sl], dzxb[:, sl], tn_dims, preferred_element_type=F32)
            back.append(lax.dot_general(dzab[:, sl], wa_ref[h], nt_dims, preferred_element_type=F32)
                        + lax.dot_general(dzxb[:, sl], wx_ref[h], nt_dims, preferred_element_type=F32))
        d_xr = d_xr + jnp.concatenate(back, axis=1)
        d_xp = cw_ref[kw - 1:kw, :] * d_xr
        after = dxr_carry[...]
        for sft in range(1, kw):
            d_xp = d_xp + cw_ref[kw - 1 - sft:kw - sft, :] * _shift_up(d_xr, after, sft)
        dxr_carry[...] = d_xr[:SUBLANES]
        dxg_ref[0] = d_xp.astype(BF16)
        xp = xg_ref[0]
        before = jnp.where(time_first, 0.0, xph_ref[...])
        dcb_ref[...] += jnp.sum(d_xr, axis=0, keepdims=True)
        dcw_ref[kw - 1:kw, :] += jnp.sum(d_xr * xp, axis=0, keepdims=True)
        for sft in range(1, kw):
            dcw_ref[kw - 1 - sft:kw - sft, :] += jnp.sum(d_xr * _shift_down(xp, before, sft), axis=0, keepdims=True)
        dlam_ref[...] = dsp_acc[...] * (-_sigmoid(-lam_ref[...]))

    full = lambda shape: pl.BlockSpec(shape, lambda i: (0,) * len(shape))
    rev = lambda i: nt - 1 - i
    row_spec = pl.BlockSpec((ts, c), lambda i: (rev(i), 0))
    halo_idx = lambda i: jnp.maximum(rev(i) * hb - 1, 0)
    vec = (1, c)
    return _call_with_side(
        body, side, lambda: pl.program_id(0) == 0, lambda: pl.program_id(0) == nt - 1,
        name="rg_bwd", grid=(nt,),
        in_specs=[row_spec,
                  pl.BlockSpec((2, ts, c), lambda i: (0, rev(i), 0)),
                  pl.BlockSpec((None, SUBLANES, c), lambda i: (0, halo_idx(i), 0)),
                  row_spec, row_spec,
                  pl.BlockSpec((SUBLANES, c), lambda i: (halo_idx(i), 0)),
                  full(conv_w.shape), full(w_a.shape), full(b_a.shape), full(w_x.shape), full(b_x.shape), full(lam.shape)],
        out_specs=[pl.BlockSpec((2, ts, c), lambda i: (0, rev(i), 0)), full(conv_w.shape), full(vec), full(w_a.shape), full(vec),
                   full(w_x.shape), full(vec), full(vec)],
        out_shape=[jax.ShapeDtypeStruct((2, s, c), BF16), jax.ShapeDtypeStruct(conv_w.shape, F32), jax.ShapeDtypeStruct(vec, F32),
                   jax.ShapeDtypeStruct(w_a.shape, F32), jax.ShapeDtypeStruct(vec, F32), jax.ShapeDtypeStruct(w_x.shape, F32),
                   jax.ShapeDtypeStruct(vec, F32), jax.ShapeDtypeStruct(vec, F32)],
        scratch_shapes=[pltpu.VMEM((SUBLANES, c), F32), pltpu.VMEM((SUBLANES, c), F32), pltpu.VMEM((SUBLANES, c), F32),
                        pltpu.VMEM(vec, F32)],
        args=(dy, xg2, xg2, xr, hs, hs, conv_w, w_a, b_a, w_x, b_x, lam))


def _s5_param_fn(a_re, a_im, log_dt, bt_re, bt_im):
    dt = jnp.exp(log_dt)
    mag = jnp.exp(a_re * dt)
    abr = mag * jnp.cos(a_im * dt)
    abi = mag * jnp.sin(a_im * dt)
    ur, ui = abr - 1.0, abi
    den = a_re * a_re + a_im * a_im
    wr = (ur * a_re + ui * a_im) / den
    wi = (ui * a_re - ur * a_im) / den
    bbr = wr[None] * bt_re - wi[None] * bt_im
    bbi = wr[None] * bt_im + wi[None] * bt_re
    return abr, abi, bbr, bbi


def _s5_params(a_re, a_im, log_dt, bt_re, bt_im, nlev):
    g, p = a_re.shape
    gc = bt_re.shape[0]

    def body(ar_ref, ai_ref, dt_ref, br_ref, bi_ref, abr_ref, abi_ref, pr_ref, pi_ref, bbr_ref, bbi_ref):
        abr, abi, bbr, bbi = _s5_param_fn(ar_ref[...], ai_ref[...], dt_ref[...], br_ref[...], bi_ref[...])
        abr_ref[...] = abr
        abi_ref[...] = abi
        bbr_ref[...] = bbr
        bbi_ref[...] = bbi
        qr, qi = abr, abi
        for k in range(nlev):
            pr_ref[k] = qr
            pi_ref[k] = qi
            qr, qi = qr * qr - qi * qi, 2.0 * qr * qi

    sd = jax.ShapeDtypeStruct
    return pl.pallas_call(
        body, name="s5_params",
        out_shape=[sd((g, p), F32), sd((g, p), F32), sd((nlev, g, p), F32), sd((nlev, g, p), F32), sd((gc, g, p), F32),
                   sd((gc, g, p), F32)],
    )(a_re, a_im, log_dt, bt_re, bt_im)


def _s5_params_bwd(a_re, a_im, log_dt, bt_re, bt_im, d_abr, d_abi, d_bbr, d_bbi):
    def body(ar_ref, ai_ref, dt_ref, br_ref, bi_ref, g0, g1, g2, g3, o0, o1, o2, o3, o4):
        _, vjp = jax.vjp(_s5_param_fn, ar_ref[...], ai_ref[...], dt_ref[...], br_ref[...], bi_ref[...])
        outs = vjp((g0[...], g1[...], g2[...], g3[...]))
        for o, v in zip((o0, o1, o2, o3, o4), outs):
            o[...] = v

    sd = jax.ShapeDtypeStruct
    return pl.pallas_call(
        body, name="s5_params_bwd",
        out_shape=[sd(a_re.shape, F32), sd(a_im.shape, F32), sd(log_dt.shape, F32), sd(bt_re.shape, F32), sd(bt_im.shape, F32)],
    )(a_re, a_im, log_dt, bt_re, bt_im, d_abr, d_abi, d_bbr, d_bbi)


def _s5_fwd(u, abr, abi, pw_r, pw_i, bp_r, bp_i, cp_r, cp_i, dvec, ts=128, side=None):
    s, c = u.shape
    n = abr.shape[1]
    nblk, cb, nb = bp_r.shape
    ts = min(ts, s)

    def body(u_ref, ar_ref, ai_ref, pr_ref, pi_ref, bpr_ref, bpi_ref, cpr_ref, cpi_ref, d_ref,
             hr_ref, hi_ref, yp_ref, gy_ref, car_r, car_i):
        i = pl.program_id(0)

        @pl.when(i == 0)
        def _():
            car_r[...] = jnp.zeros_like(car_r)
            car_i[...] = jnp.zeros_like(car_i)

        uv = u_ref[...]
        ub = uv.astype(BF16)
        br = jnp.concatenate([jnp.dot(ub[:, k * cb:(k + 1) * cb], bpr_ref[k], preferred_element_type=F32) for k in range(nblk)], axis=1)
        bi = jnp.concatenate([jnp.dot(ub[:, k * cb:(k + 1) * cb], bpi_ref[k], preferred_element_type=F32) for k in range(nblk)], axis=1)
        ar, ai = ar_ref[...], ai_ref[...]
        pr, pi_ = car_r[SUBLANES - 1:SUBLANES, :], car_i[SUBLANES - 1:SUBLANES, :]
        row = _rows(br.shape)
        br = br + jnp.where(row == 0, ar * pr - ai * pi_, 0.0)
        bi = bi + jnp.where(row == 0, ar * pi_ + ai * pr, 0.0)
        hr, hi = _scan_cplx(br, bi, pr_ref, pi_ref, reverse=False)
        car_r[...] = hr[ts - SUBLANES:]
        car_i[...] = hi[ts - SUBLANES:]
        hr_ref[...] = hr
        hi_ref[...] = hi
        hrb, hib = hr.astype(BF16), hi.astype(BF16)
        y = jnp.concatenate([jnp.dot(hrb[:, k * nb:(k + 1) * nb], cpr_ref[k], preferred_element_type=F32)
                             - jnp.dot(hib[:, k * nb:(k + 1) * nb], cpi_ref[k], preferred_element_type=F32) for k in range(nblk)], axis=1)
        yp = y + d_ref[...] * uv
        yp_ref[...] = yp
        gy_ref[...] = _gelu(yp).astype(BF16)

    full = lambda shape: pl.BlockSpec(shape, lambda i: (0,) * len(shape))
    rc = pl.BlockSpec((ts, c), lambda i: (i, 0))
    rn = pl.BlockSpec((ts, n), lambda i: (i, 0))
    sd = jax.ShapeDtypeStruct
    nt = s // ts
    return _call_with_side(
        body, side, lambda: pl.program_id(0) == 0, lambda: pl.program_id(0) == nt - 1,
        name="s5_fwd", grid=(nt,),
        in_specs=[rc, full(abr.shape), full(abi.shape), full(pw_r.shape), full(pw_i.shape), full(bp_r.shape), full(bp_i.shape),
                  full(cp_r.shape), full(cp_i.shape), full(dvec.shape)],
        out_specs=[rn, rn, rc, rc],
        out_shape=[sd((s, n), F32), sd((s, n), F32), sd((s, c), F32), sd((s, c), BF16)],
        scratch_shapes=[pltpu.VMEM((SUBLANES, n), F32), pltpu.VMEM((SUBLANES, n), F32)],
        args=(u, abr, abi, pw_r, pw_i, bp_r, bp_i, cp_r, cp_i, dvec))


def _s5_bwd(dgy, ypre, u, hr, hi, abr, abi, pw_r, pw_i, bp_r, bp_i, cp_r, cp_i, dvec, ts=128, side=None):
    s, c = u.shape
    n = abr.shape[1]
    nblk, cb, nb = bp_r.shape
    ts = min(ts, s)
    nt = s // ts
    hb = ts // SUBLANES
    tn_dims = (((0,), (0,)), ((), ()))
    nt_dims = (((1,), (1,)), ((), ()))

    def body(dgy_ref, yp_ref, u_ref, hr_ref, hi_ref, hrh_ref, hih_ref, ar_ref, ai_ref, pr_ref, pi_ref, bpr_ref, bpi_ref,
             cpr_ref, cpi_ref, d_ref,
             du_ref, dar_ref, dai_ref, dbr_ref, dbi_ref, dcr_ref, dci_ref, dd_ref, car_r, car_i, npi_ref):
        i = pl.program_id(0)
        time_first = i == nt - 1

        @pl.when(i == 0)
        def _():
            car_r[...] = jnp.zeros_like(car_r)
            car_i[...] = jnp.zeros_like(car_i)
            npi_ref[...] = -pi_ref[...]
            for ref in (dar_ref, dai_ref, dbr_ref, dbi_ref, dcr_ref, dci_ref, dd_ref):
                ref[...] = jnp.zeros_like(ref)

        uv = u_ref[...]
        _, dgel = _gelu_and_grad(yp_ref[...])
        dyv = dgy_ref[...] * dgel
        dd_ref[...] += jnp.sum(dyv * uv, axis=0, keepdims=True)
        dyb = dyv.astype(BF16)
        hr, hi = hr_ref[...], hi_ref[...]
        hrb, hib = hr.astype(BF16), hi.astype(BF16)
        dhr, dhi = [], []
        for k in range(nblk):
            dblk = dyb[:, k * cb:(k + 1) * cb]
            dhr.append(lax.dot_general(dblk, cpr_ref[k], nt_dims, preferred_element_type=F32))
            dhi.append(-lax.dot_general(dblk, cpi_ref[k], nt_dims, preferred_element_type=F32))
            dcr_ref[k] += lax.dot_general(hrb[:, k * nb:(k + 1) * nb], dblk, tn_dims, preferred_element_type=F32)
            dci_ref[k] += lax.dot_general(hib[:, k * nb:(k + 1) * nb], dblk, tn_dims, preferred_element_type=F32)
        dhr = jnp.concatenate(dhr, axis=1)
        dhi = jnp.concatenate(dhi, axis=1)
        ar, ai = ar_ref[...], ai_ref[...]
        nr, ni = car_r[0:1, :], car_i[0:1, :]
        row = _rows(dhr.shape)
        dhr = dhr + jnp.where(row == ts - 1, ar * nr + ai * ni, 0.0)
        dhi = dhi + jnp.where(row == ts - 1, ar * ni - ai * nr, 0.0)
        lr, li = _scan_cplx(dhr, dhi, pr_ref, npi_ref, reverse=True)
        car_r[...] = lr[:SUBLANES]
        car_i[...] = li[:SUBLANES]
        hpr = _shift_down(hr, jnp.where(time_first, 0.0, hrh_ref[...]), 1)
        hpi = _shift_down(hi, jnp.where(time_first, 0.0, hih_ref[...]), 1)
        dar_ref[...] += jnp.sum(lr * hpr + li * hpi, axis=0, keepdims=True)
        dai_ref[...] += jnp.sum(li * hpr - lr * hpi, axis=0, keepdims=True)
        lrb, lib = lr.astype(BF16), li.astype(BF16)
        ub = uv.astype(BF16)
        du = []
        for k in range(nblk):
            ublk = ub[:, k * cb:(k + 1) * cb]
            lrk, lik = lrb[:, k * nb:(k + 1) * nb], lib[:, k * nb:(k + 1) * nb]
            dbr_ref[k] += lax.dot_general(ublk, lrk, tn_dims, preferred_element_type=F32)
            dbi_ref[k] += lax.dot_general(ublk, lik, tn_dims, preferred_element_type=F32)
            du.append(lax.dot_general(lrk, bpr_ref[k], nt_dims, preferred_element_type=F32)
                      + lax.dot_general(lik, bpi_ref[k], nt_dims, preferred_element_type=F32))
        du_ref[...] = (d_ref[...] * dyv + jnp.concatenate(du, axis=1)).astype(BF16)

    full = lambda shape: pl.BlockSpec(shape, lambda i: (0,) * len(shape))
    rev = lambda i: nt - 1 - i
    halo_idx = lambda i: jnp.maximum(rev(i) * hb - 1, 0)
    rc = pl.BlockSpec((ts, c), lambda i: (rev(i), 0))
    rn = pl.BlockSpec((ts, n), lambda i: (rev(i), 0))
    hn = pl.BlockSpec((SUBLANES, n), lambda i: (halo_idx(i), 0))
    sd = jax.ShapeDtypeStruct
    return _call_with_side(
        body, side, lambda: pl.program_id(0) == 0, lambda: pl.program_id(0) == nt - 1,
        name="s5_bwd", grid=(nt,),
        in_specs=[rc, rc, rc, rn, rn, hn, hn, full(abr.shape), full(abi.shape), full(pw_r.shape), full(pw_i.shape),
                  full(bp_r.shape), full(bp_i.shape), full(cp_r.shape), full(cp_i.shape), full(dvec.shape)],
        out_specs=[rc, full(abr.shape), full(abi.shape), full(bp_r.shape), full(bp_i.shape), full(cp_r.shape), full(cp_i.shape),
                   full(dvec.shape)],
        out_shape=[sd((s, c), BF16), sd(abr.shape, F32), sd(abi.shape, F32), sd(bp_r.shape, F32), sd(bp_i.shape, F32),
                   sd(cp_r.shape, F32), sd(cp_i.shape, F32), sd(dvec.shape, F32)],
        scratch_shapes=[pltpu.VMEM((SUBLANES, n), F32), pltpu.VMEM((SUBLANES, n), F32), pltpu.VMEM(pw_i.shape, F32)],
        args=(dgy, ypre, u, hr, hi, hr, hi, abr, abi, pw_r, pw_i, bp_r, bp_i, cp_r, cp_i, dvec))


S5_LANE_CHUNK = 256


def _s5_tables(a_re, a_im, log_dt, bt_re, bt_im):
    g, p = a_re.shape
    gc = bt_re.shape[0]

    def body(ar_ref, ai_ref, dt_ref, br_ref, bi_ref, abr_ref, abi_ref, tr_ref, ti_ref, bbr_ref, bbi_ref):
        abr, abi, bbr, bbi = _s5_param_fn(ar_ref[...], ai_ref[...], dt_ref[...], br_ref[...], bi_ref[...])
        abr_ref[...] = abr
        abi_ref[...] = abi
        bbr_ref[...] = bbr
        bbi_ref[...] = bbi
        pows = [(abr, abi)]
        for _ in range(1, SUBLANES):
            qr, qi = pows[-1]
            pows.append((qr * abr - qi * abi, qr * abi + qi * abr))
        zero = jnp.zeros_like(abr)
        for r in range(SUBLANES):
            for k in range(3):
                sh = 1 << k
                tr_ref[k, r] = pows[sh - 1][0] if r >= sh else zero
                ti_ref[k, r] = pows[sh - 1][1] if r >= sh else zero
            tr_ref[3, r] = pows[r][0]
            ti_ref[3, r] = pows[r][1]

    sd = jax.ShapeDtypeStruct
    return pl.pallas_call(
        body, name="s5_tables",
        out_shape=[sd((g, p), F32), sd((g, p), F32), sd((4, SUBLANES, g, p), F32), sd((4, SUBLANES, g, p), F32),
                   sd((gc, g, p), F32), sd((gc, g, p), F32)],
    )(a_re, a_im, log_dt, bt_re, bt_im)


def _cmul_add(br, bi, tr, ti, sr, si):
    return br + tr * sr - ti * si, bi + tr * si + ti * sr


def _s5_fwd2(u, tab_r, tab_i, bp_r, bp_i, cp_r, cp_i, dvec, ts=256, side=None):
    s, c = u.shape
    n = tab_r.shape[2]
    nblk, cb, nb = bp_r.shape
    ts = min(ts, s)
    nsl = ts // SUBLANES
    lc = min(S5_LANE_CHUNK, n)

    def body(u_ref, tr_ref, ti_ref, bpr_ref, bpi_ref, cpr_ref, cpi_ref, d_ref, hr_ref, hi_ref, yp_ref, gy_ref,
             bur_ref, bui_ref, car_r, car_i):
        i = pl.program_id(0)

        @pl.when(i == 0)
        def _():
            car_r[...] = jnp.zeros_like(car_r)
            car_i[...] = jnp.zeros_like(car_i)

        uv = u_ref[...]
        ub = uv.astype(BF16)
        for k in range(nblk):
            bur_ref[:, k * nb:(k + 1) * nb] = jnp.dot(ub[:, k * cb:(k + 1) * cb], bpr_ref[k], preferred_element_type=F32)
            bui_ref[:, k * nb:(k + 1) * nb] = jnp.dot(ub[:, k * cb:(k + 1) * cb], bpi_ref[k], preferred_element_type=F32)
        for q in range(n // lc):
            sl = slice(q * lc, (q + 1) * lc)
            tabs = [(tr_ref[k, :, sl], ti_ref[k, :, sl]) for k in range(4)]

            def slab(j, carry, sl=sl, tabs=tabs):
                cr, ci = carry
                r0 = pl.multiple_of(j * SUBLANES, SUBLANES)
                br, bi = bur_ref[pl.ds(r0, SUBLANES), sl], bui_ref[pl.ds(r0, SUBLANES), sl]
                for k in range(3):
                    sh = 1 << k
                    br, bi = _cmul_add(br, bi, tabs[k][0], tabs[k][1], pltpu.roll(br, sh, 0), pltpu.roll(bi, sh, 0))
                hr, hi = _cmul_add(br, bi, tabs[3][0], tabs[3][1], jnp.broadcast_to(cr, br.shape), jnp.broadcast_to(ci, bi.shape))
                hr_ref[pl.ds(r0, SUBLANES), sl] = hr
                hi_ref[pl.ds(r0, SUBLANES), sl] = hi
                return hr[SUBLANES - 1:, :], hi[SUBLANES - 1:, :]

            cr, ci = lax.fori_loop(0, nsl, slab, (car_r[:, sl], car_i[:, sl]), unroll=2)
            car_r[:, sl] = cr
            car_i[:, sl] = ci
        hrb, hib = hr_ref[...].astype(BF16), hi_ref[...].astype(BF16)
        y = jnp.concatenate([jnp.dot(hrb[:, k * nb:(k + 1) * nb], cpr_ref[k], preferred_element_type=F32)
                             - jnp.dot(hib[:, k * nb:(k + 1) * nb], cpi_ref[k], preferred_element_type=F32) for k in range(nblk)], axis=1)
        yp = y + d_ref[...] * uv
        yp_ref[...] = yp
        gy_ref[...] = _gelu(yp).astype(BF16)

    full = lambda shape: pl.BlockSpec(shape, lambda i: (0,) * len(shape))
    rc = pl.BlockSpec((ts, c), lambda i: (i, 0))
    rn = pl.BlockSpec((ts, n), lambda i: (i, 0))
    sd = jax.ShapeDtypeStruct
    nt = s // ts
    return _call_with_side(
        body, side, lambda: pl.program_id(0) == 0, lambda: pl.program_id(0) == nt - 1,
        name="s5_fwd", grid=(nt,),
        in_specs=[rc, full(tab_r.shape), full(tab_i.shape), full(bp_r.shape), full(bp_i.shape), full(cp_r.shape), full(cp_i.shape),
                  full(dvec.shape)],
        out_specs=[rn, rn, rc, rc],
        out_shape=[sd((s, n), F32), sd((s, n), F32), sd((s, c), F32), sd((s, c), BF16)],
        scratch_shapes=[pltpu.VMEM((ts, n), F32), pltpu.VMEM((ts, n), F32), pltpu.VMEM((1, n), F32), pltpu.VMEM((1, n), F32)],
        args=(u, tab_r, tab_i, bp_r, bp_i, cp_r, cp_i, dvec))


def _s5_bwd2(dgy, ypre, u, hr, hi, rtab_r, rtab_i, bp_r, bp_i, cp_r, cp_i, dvec, ts=256, side=None):
    s, c = u.shape
    n = rtab_r.shape[2]
    nblk, cb, nb = bp_r.shape
    ts = min(ts, s)
    nt = s // ts
    hb = ts // SUBLANES
    nsl = ts // SUBLANES
    lc = min(S5_LANE_CHUNK, n)
    tn_dims = (((0,), (0,)), ((), ()))
    nt_dims = (((1,), (1,)), ((), ()))

    def body(dgy_ref, yp_ref, u_ref, hr_ref, hi_ref, hrh_ref, hih_ref, tr_ref, ti_ref, bpr_ref, bpi_ref, cpr_ref, cpi_ref, d_ref,
             du_ref, dar_ref, dai_ref, dbr_ref, dbi_ref, dcr_ref, dci_ref, dd_ref, lr_ref, li_ref, car_r, car_i):
        i = pl.program_id(0)
        time_first = i == nt - 1

        @pl.when(i == 0)
        def _():
            car_r[...] = jnp.zeros_like(car_r)
            car_i[...] = jnp.zeros_like(car_i)
            for ref in (dar_ref, dai_ref, dbr_ref, dbi_ref, dcr_ref, dci_ref, dd_ref):
                ref[...] = jnp.zeros_like(ref)

        uv = u_ref[...]
        _, dgel = _gelu_and_grad(yp_ref[...])
        dyv = dgy_ref[...] * dgel
        dd_ref[...] += jnp.sum(dyv * uv, axis=0, keepdims=True)
        dyb = dyv.astype(BF16)
        hrb, hib = hr_ref[...].astype(BF16), hi_ref[...].astype(BF16)
        for k in range(nblk):
            dblk = dyb[:, k * cb:(k + 1) * cb]
            lr_ref[:, k * nb:(k + 1) * nb] = lax.dot_general(dblk, cpr_ref[k], nt_dims, preferred_element_type=F32)
            li_ref[:, k * nb:(k + 1) * nb] = -lax.dot_general(dblk, cpi_ref[k], nt_dims, preferred_element_type=F32)
            dcr_ref[k] += lax.dot_general(hrb[:, k * nb:(k + 1) * nb], dblk, tn_dims, preferred_element_type=F32)
            dci_ref[k] += lax.dot_general(hib[:, k * nb:(k + 1) * nb], dblk, tn_dims, preferred_element_type=F32)
        row8 = _rows((SUBLANES, lc))
        for q in range(n // lc):
            sl = slice(q * lc, (q + 1) * lc)
            tabs = [(tr_ref[k, :, sl], ti_ref[k, :, sl]) for k in range(4)]
            halo_r = jnp.where(time_first, 0.0, hrh_ref[SUBLANES - 1:, sl])
            halo_i = jnp.where(time_first, 0.0, hih_ref[SUBLANES - 1:, sl])

            def slab(jj, carry, sl=sl, tabs=tabs, halo_r=halo_r, halo_i=halo_i):
                nr, ni, acc_r, acc_i = carry
                j = nsl - 1 - jj
                r0 = pl.multiple_of(j * SUBLANES, SUBLANES)
                br, bi = lr_ref[pl.ds(r0, SUBLANES), sl], li_ref[pl.ds(r0, SUBLANES), sl]
                for k in range(3):
                    sh = 1 << k
                    br, bi = _cmul_add(br, bi, tabs[k][0], tabs[k][1], pltpu.roll(br, SUBLANES - sh, 0),
                                       pltpu.roll(bi, SUBLANES - sh, 0))
                lr, li = _cmul_add(br, bi, tabs[3][0], tabs[3][1], jnp.broadcast_to(nr, br.shape), jnp.broadcast_to(ni, bi.shape))
                lr_ref[pl.ds(r0, SUBLANES), sl] = lr
                li_ref[pl.ds(r0, SUBLANES), sl] = li
                p0 = pl.multiple_of(jnp.maximum(j - 1, 0) * SUBLANES, SUBLANES)
                prev_r = jnp.where(j == 0, halo_r, hr_ref[pl.ds(p0, SUBLANES), sl][SUBLANES - 1:, :])
                prev_i = jnp.where(j == 0, halo_i, hi_ref[pl.ds(p0, SUBLANES), sl][SUBLANES - 1:, :])
                hpr = jnp.where(row8 == 0, jnp.broadcast_to(prev_r, br.shape), pltpu.roll(hr_ref[pl.ds(r0, SUBLANES), sl], 1, 0))
                hpi = jnp.where(row8 == 0, jnp.broadcast_to(prev_i, bi.shape), pltpu.roll(hi_ref[pl.ds(r0, SUBLANES), sl], 1, 0))
                return lr[:1, :], li[:1, :], acc_r + (lr * hpr + li * hpi), acc_i + (li * hpr - lr * hpi)

            zero = jnp.zeros((SUBLANES, lc), F32)
            nr, ni, acc_r, acc_i = lax.fori_loop(0, nsl, slab, (car_r[:, sl], car_i[:, sl], zero, zero), unroll=2)
            car_r[:, sl] = nr
            car_i[:, sl] = ni
            dar_ref[:, sl] += jnp.sum(acc_r, axis=0, keepdims=True)
            dai_ref[:, sl] += jnp.sum(acc_i, axis=0, keepdims=True)
        lrb, lib = lr_ref[...].astype(BF16), li_ref[...].astype(BF16)
        ub = uv.astype(BF16)
        du = []
        for k in range(nblk):
            ublk = ub[:, k * cb:(k + 1) * cb]
            lrk, lik = lrb[:, k * nb:(k + 1) * nb], lib[:, k * nb:(k + 1) * nb]
            dbr_ref[k] += lax.dot_general(ublk, lrk, tn_dims, preferred_element_type=F32)
            dbi_ref[k] += lax.dot_general(ublk, lik, tn_dims, preferred_element_type=F32)
            du.append(lax.dot_general(lrk, bpr_ref[k], nt_dims, preferred_element_type=F32)
                      + lax.dot_general(lik, bpi_ref[k], nt_dims, preferred_element_type=F32))
        du_ref[...] = (d_ref[...] * dyv + jnp.concatenate(du, axis=1)).astype(BF16)

    full = lambda shape: pl.BlockSpec(shape, lambda i: (0,) * len(shape))
    rev = lambda i: nt - 1 - i
    halo_idx = lambda i: jnp.maximum(rev(i) * hb - 1, 0)
    rc = pl.BlockSpec((ts, c), lambda i: (rev(i), 0))
    rn = pl.BlockSpec((ts, n), lambda i: (rev(i), 0))
    hn = pl.BlockSpec((SUBLANES, n), lambda i: (halo_idx(i), 0))
    sd = jax.ShapeDtypeStruct
    vec_n = (1, n)
    return _call_with_side(
        body, side, lambda: pl.program_id(0) == 0, lambda: pl.program_id(0) == nt - 1,
        name="s5_bwd", grid=(nt,),
        in_specs=[rc, rc, rc, rn, rn, hn, hn, full(rtab_r.shape), full(rtab_i.shape),
                  full(bp_r.shape), full(bp_i.shape), full(cp_r.shape), full(cp_i.shape), full(dvec.shape)],
        out_specs=[rc, full(vec_n), full(vec_n), full(bp_r.shape), full(bp_i.shape), full(cp_r.shape), full(cp_i.shape),
                   full(dvec.shape)],
        out_shape=[sd((s, c), BF16), sd(vec_n, F32), sd(vec_n, F32), sd(bp_r.shape, F32), sd(bp_i.shape, F32),
                   sd(cp_r.shape, F32), sd(cp_i.shape, F32), sd(dvec.shape, F32)],
        scratch_shapes=[pltpu.VMEM((ts, n), F32), pltpu.VMEM((ts, n), F32), pltpu.VMEM((1, n), F32), pltpu.VMEM((1, n), F32)],
        args=(dgy, ypre, u, hr, hi, hr, hi, rtab_r, rtab_i, bp_r, bp_i, cp_r, cp_i, dvec))


def _glu(gl2, ts=512):
    _, s, c = gl2.shape
    ts = min(ts, s)

    def body(g_ref, o_ref):
        o_ref[...] = (g_ref[0] * _sigmoid(g_ref[1])).astype(BF16)

    return pl.pallas_call(
        body, name="glu", grid=(s // ts,), in_specs=[pl.BlockSpec((2, ts, c), lambda i: (0, i, 0))],
        out_specs=pl.BlockSpec((ts, c), lambda i: (i, 0)), out_shape=jax.ShapeDtypeStruct((s, c), BF16), compiler_params=_cparams(),
    )(gl2)


def _glu_bwd(gl2, d_o, ts=512):
    _, s, c = gl2.shape
    ts = min(ts, s)

    def body(g_ref, do_ref, o_ref):
        sg = _sigmoid(g_ref[1])
        dov = do_ref[...]
        o_ref[0] = (dov * sg).astype(BF16)
        o_ref[1] = (dov * g_ref[0] * sg * (1.0 - sg)).astype(BF16)

    blk = pl.BlockSpec((2, ts, c), lambda i: (0, i, 0))
    return pl.pallas_call(
        body, name="glu_bwd", grid=(s // ts,), in_specs=[blk, pl.BlockSpec((ts, c), lambda i: (i, 0))],
        out_specs=blk, out_shape=jax.ShapeDtypeStruct((2, s, c), BF16), compiler_params=_cparams(),
    )(gl2, d_o)


PACK_ROW_MULTIPLE = 1024
ELEMENTWISE_BLOCK_ELEMS = 256 * 1024


def _row_tile(rows, cols):
    pref = max(SUBLANES, 1 << int(math.log2(max(1, ELEMENTWISE_BLOCK_ELEMS // cols))))
    if rows <= pref:
        return rows
    t = pref
    while rows % t:
        t //= 2
    assert t >= SUBLANES, rows
    return t


def _sum_parts(rs):
    nl = len(rs)
    p, rows, cols = rs[0].shape
    tr = _row_tile(rows, cols)

    def body(*refs):
        o_ref = refs[nl]
        for l in range(nl):
            acc = refs[l][0].astype(F32)
            for k in range(1, p):
                acc = acc + refs[l][k].astype(F32)
            o_ref[l] = acc

    return pl.pallas_call(
        body, name="sum_parts", grid=(rows // tr,), in_specs=[pl.BlockSpec((p, tr, cols), lambda i: (0, i, 0))] * nl,
        out_specs=pl.BlockSpec((nl, tr, cols), lambda i: (0, i, 0)), out_shape=jax.ShapeDtypeStruct((nl, rows, cols), F32),
        compiler_params=_cparams(),
    )(*rs)


def _adamw(w, g_parts, m, v):
    rows, cols = w.shape
    tr = _row_tile(rows, cols)
    ng = len(g_parts)
    c1 = 1.0 / (1.0 - ADAM_B1 ** ADAM_STEP)
    c2 = 1.0 / (1.0 - ADAM_B2 ** ADAM_STEP)

    def body(*refs):
        w_ref, m_ref, v_ref = refs[0], refs[1 + ng], refs[2 + ng]
        g_ref, dl_ref, nm_ref, nv_ref = refs[3 + ng:]
        g = refs[1][...]
        for k in range(1, ng):
            g = g + refs[1 + k][...]
        mn = ADAM_B1 * m_ref[...] + (1.0 - ADAM_B1) * g
        vn = ADAM_B2 * v_ref[...] + (1.0 - ADAM_B2) * (g * g)
        g_ref[...] = g
        nm_ref[...] = mn
        nv_ref[...] = vn
        dl_ref[...] = -ADAM_LR * ((mn * c1) / (jnp.sqrt(vn * c2) + ADAM_EPS) + ADAM_WD * w_ref[...])

    blk = pl.BlockSpec((tr, cols), lambda i: (i, 0))
    sd = jax.ShapeDtypeStruct((rows, cols), F32)
    return pl.pallas_call(
        body, name="adamw", grid=(rows // tr,), in_specs=[blk] * (3 + ng), out_specs=[blk] * 4, out_shape=[sd] * 4,
        compiler_params=_cparams(),
    )(w, *g_parts, m, v)


def _place():
    x, y, c = lax.axis_index("x"), lax.axis_index("y"), lax.axis_index("c")
    chips = [(1 - x, y), (x, 1 - y), (1 - x, 1 - y)]
    return x, y, c, chips


class Side:
    def __init__(self, ins, outs, kind):
        self.ins, self.outs, self.kind = list(ins), list(outs), kind
        n = len(self.ins)
        self.sems = [pltpu.SemaphoreType.DMA((3 * n,)), pltpu.SemaphoreType.DMA((3 * n,)), pltpu.SemaphoreType.DMA((n,))]

    def _copies(self, ins, outs, send, recv, lsem):
        x, y, c, chips = _place()
        me = 2 * x + y
        local, out_going, in_coming = [], [], []
        for t in range(len(ins)):
            if self.kind == 'gather':
                src_local, srcs, dst_mine = ins[t], [ins[t]] * 3, outs[t].at[me]
            else:
                src_local, srcs, dst_mine = ins[t].at[me], [ins[t].at[2 * px + py] for px, py in chips], outs[t].at[me]
            local.append(pltpu.make_async_copy(src_local, dst_mine, lsem.at[t]))
            for r, (px, py) in enumerate(chips):
                out_going.append(pltpu.make_async_remote_copy(
                    src_ref=srcs[r], dst_ref=dst_mine, send_sem=send.at[3 * t + r], recv_sem=recv.at[3 * t + r],
                    device_id=(px, py, c), device_id_type=MESH))
                in_coming.append(pltpu.make_async_remote_copy(
                    src_ref=srcs[r], dst_ref=outs[t].at[2 * px + py], send_sem=send.at[3 * t + r], recv_sem=recv.at[3 * t + r],
                    device_id=(px, py, c), device_id_type=MESH))
        return local, out_going, in_coming

    def start(self, ins, outs, send, recv, lsem):
        local, out_going, _ = self._copies(ins, outs, send, recv, lsem)
        for cp in local + out_going:
            cp.start()

    def wait(self, ins, outs, send, recv, lsem):
        local, out_going, in_coming = self._copies(ins, outs, send, recv, lsem)
        for cp in in_coming:
            cp.wait_recv()
        for cp in out_going:
            cp.wait_send()
        for cp in local:
            cp.wait()


def _gather_side(shards):
    return Side(shards, [jax.ShapeDtypeStruct((N_CHIPS,) + s.shape, s.dtype) for s in shards], 'gather')


def _scatter_side(grads):
    return Side(grads, [jax.ShapeDtypeStruct(g.shape, g.dtype) for g in grads], 'scatter')


def _call_with_side(body, side, first, last, *, name, grid, in_specs, out_specs, out_shape, scratch_shapes, args):
    if side is None:
        outs = pl.pallas_call(body, name=name, grid=grid, in_specs=in_specs, out_specs=out_specs, out_shape=out_shape,
                              scratch_shapes=scratch_shapes, compiler_params=_cparams())(*args)
        return outs, []
    n_in, n_out, n_sc = len(in_specs), len(out_specs), len(scratch_shapes)
    ns_in, ns_out = len(side.ins), len(side.outs)

    def wrapped(*refs):
        base_in, s_in = refs[:n_in], refs[n_in:n_in + ns_in]
        o0 = n_in + ns_in
        base_out, s_out = refs[o0:o0 + n_out], refs[o0 + n_out:o0 + n_out + ns_out]
        sc0 = o0 + n_out + ns_out
        base_sc, sems = refs[sc0:sc0 + n_sc], refs[sc0 + n_sc:]

        @pl.when(first())
        def _():
            side.start(s_in, s_out, *sems)

        body(*base_in, *base_out, *base_sc)

        @pl.when(last())
        def _():
            side.wait(s_in, s_out, *sems)

    any_spec = pl.BlockSpec(memory_space=pl.ANY)
    outs = pl.pallas_call(
        wrapped, name=name, grid=grid, in_specs=list(in_specs) + [any_spec] * ns_in, out_specs=list(out_specs) + [any_spec] * ns_out,
        out_shape=list(out_shape) + side.outs, scratch_shapes=list(scratch_shapes) + side.sems, compiler_params=_cparams(),
    )(*args, *side.ins)
    return outs[:n_out], outs[n_out:]


def _run_side(name, side):
    def body(*refs):
        n = len(side.ins)
        side.start(refs[:n], refs[n:2 * n], *refs[2 * n:])
        side.wait(refs[:n], refs[n:2 * n], *refs[2 * n:])

    any_spec = pl.BlockSpec(memory_space=pl.ANY)
    return pl.pallas_call(body, name=name, in_specs=[any_spec] * len(side.ins), out_specs=[any_spec] * len(side.outs),
                          out_shape=side.outs, scratch_shapes=side.sems)(*side.ins)


def _gather_shards(shards, layer_major):
    n = len(shards)

    def body(*refs):
        ins, outs = refs[:n], refs[n:2 * n]
        send, recv, lsem = refs[2 * n:]
        x, y, c, chips = _place()
        me = 2 * x + y

        def slot(t, chip):
            return outs[t].at[:, chip] if layer_major[t] else outs[t].at[chip]

        local, sends = [], []
        for t in range(n):
            cp = pltpu.make_async_copy(ins[t], slot(t, me), lsem.at[t])
            cp.start()
            local.append(cp)
            for r, (px, py) in enumerate(chips):
                rc = pltpu.make_async_remote_copy(src_ref=ins[t], dst_ref=slot(t, me), send_sem=send.at[3 * t + r],
                                                  recv_sem=recv.at[3 * t + r], device_id=(px, py, c), device_id_type=MESH)
                rc.start()
                sends.append(rc)
        for t in range(n):
            for r, (px, py) in enumerate(chips):
                pltpu.make_async_remote_copy(src_ref=ins[t], dst_ref=slot(t, 2 * px + py), send_sem=send.at[3 * t + r],
                                             recv_sem=recv.at[3 * t + r], device_id=(px, py, c), device_id_type=MESH).wait_recv()
        for rc in sends:
            rc.wait_send()
        for cp in local:
            cp.wait()

    any_spec = pl.BlockSpec(memory_space=pl.ANY)
    return pl.pallas_call(
        body, name="gather_shards", in_specs=[any_spec] * n, out_specs=[any_spec] * n,
        out_shape=[jax.ShapeDtypeStruct((s.shape[0], N_CHIPS) + s.shape[1:] if lm else (N_CHIPS,) + s.shape, s.dtype)
                   for s, lm in zip(shards, layer_major)],
        scratch_shapes=[pltpu.SemaphoreType.DMA((3 * n,)), pltpu.SemaphoreType.DMA((3 * n,)), pltpu.SemaphoreType.DMA((n,))],
    )(*shards)


def _scatter_grads(groups):
    flat = [(gi, li, a) for gi, grp in enumerate(groups) for li, a in enumerate(grp)]
    n = len(flat)
    ng = len(groups)

    def body(*refs):
        ins, outs = refs[:n], refs[n:n + ng]
        send, recv, lsem = refs[n + ng:]
        x, y, c, chips = _place()
        me = 2 * x + y
        local, sends = [], []
        for t, (gi, li, _) in enumerate(flat):
            cp = pltpu.make_async_copy(ins[t].at[me], outs[gi].at[me, li], lsem.at[t])
            cp.start()
            local.append(cp)
            for r, (px, py) in enumerate(chips):
                rc = pltpu.make_async_remote_copy(src_ref=ins[t].at[2 * px + py], dst_ref=outs[gi].at[me, li],
                                                  send_sem=send.at[3 * t + r], recv_sem=recv.at[3 * t + r],
                                                  device_id=(px, py, c), device_id_type=MESH)
                rc.start()
                sends.append(rc)
        for t, (gi, li, _) in enumerate(flat):
            for r, (px, py) in enumerate(chips):
                pltpu.make_async_remote_copy(src_ref=ins[t].at[me], dst_ref=outs[gi].at[2 * px + py, li],
                                             send_sem=send.at[3 * t + r], recv_sem=recv.at[3 * t + r],
                                             device_id=(px, py, c), device_id_type=MESH).wait_recv()
        for rc in sends:
            rc.wait_send()
        for cp in local:
            cp.wait()

    any_spec = pl.BlockSpec(memory_space=pl.ANY)
    return pl.pallas_call(
        body, name="scatter_grads", in_specs=[any_spec] * n, out_specs=[any_spec] * ng,
        out_shape=[jax.ShapeDtypeStruct((N_CHIPS, len(grp)) + grp[0].shape[1:], grp[0].dtype) for grp in groups],
        scratch_shapes=[pltpu.SemaphoreType.DMA((3 * n,)), pltpu.SemaphoreType.DMA((3 * n,)), pltpu.SemaphoreType.DMA((n,))],
    )(*[a for _, _, a in flat])


def _swap_with_sibling(arrs):
    n = len(arrs)

    def body(*refs):
        ins, outs = refs[:n], refs[n:2 * n]
        send, recv = refs[2 * n:]
        x, y, c, _ = _place()
        cps = []
        for t in range(n):
            rc = pltpu.make_async_remote_copy(src_ref=ins[t], dst_ref=outs[t], send_sem=send.at[t], recv_sem=recv.at[t],
                                              device_id=(x, y, 1 - c), device_id_type=MESH)
            rc.start()
            cps.append(rc)
        for rc in cps:
            rc.wait_recv()
        for rc in cps:
            rc.wait_send()

    any_spec = pl.BlockSpec(memory_space=pl.ANY)
    return pl.pallas_call(
        body, name="swap_with_sibling", in_specs=[any_spec] * n, out_specs=[any_spec] * n,
        out_shape=[jax.ShapeDtypeStruct(a.shape, a.dtype) for a in arrs],
        scratch_shapes=[pltpu.SemaphoreType.DMA((n,)), pltpu.SemaphoreType.DMA((n,))],
    )(*arrs)


def _allreduce_small(v):
    rows, cols = v.shape

    def body(v_ref, o_ref, sib_ref, chip_ref, send, recv):
        x, y, c, chips = _place()
        me = 2 * x + y
        d2d = pltpu.make_async_remote_copy(src_ref=v_ref, dst_ref=sib_ref, send_sem=send.at[0], recv_sem=recv.at[0],
                                           device_id=(x, y, 1 - c), device_id_type=MESH)
        d2d.start()
        d2d.wait_recv()
        chip_ref[me] = v_ref[...] + sib_ref[...]
        sends = []
        for r, (px, py) in enumerate(chips):
            rc = pltpu.make_async_remote_copy(src_ref=chip_ref.at[me], dst_ref=chip_ref.at[me], send_sem=send.at[1 + r],
                                              recv_sem=recv.at[1 + r], device_id=(px, py, c), device_id_type=MESH)
            rc.start()
            sends.append(rc)
        for r, (px, py) in enumerate(chips):
            pltpu.make_async_remote_copy(src_ref=chip_ref.at[me], dst_ref=chip_ref.at[2 * px + py], send_sem=send.at[1 + r],
                                         recv_sem=recv.at[1 + r], device_id=(px, py, c), device_id_type=MESH).wait_recv()
        o_ref[...] = (chip_ref[0] + chip_ref[1]) + (chip_ref[2] + chip_ref[3])
        d2d.wait_send()
        for rc in sends:
            rc.wait_send()

    vm = pl.BlockSpec(memory_space=pltpu.VMEM)
    return pl.pallas_call(
        body, name="allreduce_small", in_specs=[vm], out_specs=vm, out_shape=jax.ShapeDtypeStruct((rows, cols), F32),
        scratch_shapes=[pltpu.VMEM((rows, cols), F32), pltpu.VMEM((N_CHIPS, rows, cols), F32), pltpu.SemaphoreType.DMA((4,)),
                        pltpu.SemaphoreType.DMA((4,))],
        compiler_params=_cparams(),
    )(v)


def _pack(tensors):
    pieces = []
    for t in tensors:
        flat = t.reshape(-1)
        pad = (-flat.shape[0]) % (SUBLANES * LANES)
        pieces.append(jnp.pad(flat, (0, pad)).reshape(-1, LANES))
    rows = sum(p.shape[0] for p in pieces)
    pieces.append(jnp.zeros(((-rows) % PACK_ROW_MULTIPLE, LANES), tensors[0].dtype))
    return jnp.concatenate(pieces, axis=0)


def _unpack(buf, like):
    out, off = [], 0
    for t in like:
        size = math.prod(t.shape)
        rows = -(-size // (SUBLANES * LANES)) * SUBLANES
        out.append(buf[off:off + rows].reshape(-1)[:size].reshape(t.shape))
        off += rows
    return out


def _s5_pack_b(bb):
    gc, g, p = bb.shape
    q = S5_GROUPS_PER_BLOCK
    t = bb.reshape(gc, g // q, q, p).transpose(1, 2, 0, 3)
    eye = jnp.eye(q, dtype=bb.dtype)
    return (t[:, :, :, None, :] * eye[None, :, None, :, None]).reshape(g // q, q * gc, q * p)


def _s5_unpack_b(dbp, gc, p):
    nb = dbp.shape[0]
    q = S5_GROUPS_PER_BLOCK
    eye = jnp.eye(q, dtype=dbp.dtype)
    t = (dbp.reshape(nb, q, gc, q, p) * eye[None, :, None, :, None]).sum(axis=3)
    return t.transpose(2, 0, 1, 3).reshape(gc, nb * q, p)


def _s5_pack_c(cc):
    g, gc, p = cc.shape
    q = S5_GROUPS_PER_BLOCK
    t = cc.reshape(g // q, q, gc, p).transpose(0, 1, 3, 2)
    eye = jnp.eye(q, dtype=cc.dtype)
    return (t[:, :, :, None, :] * eye[None, :, None, :, None]).reshape(g // q, q * p, q * gc)


def _s5_unpack_c(dcp, gc, p):
    nb = dcp.shape[0]
    q = S5_GROUPS_PER_BLOCK
    eye = jnp.eye(q, dtype=dcp.dtype)
    t = (dcp.reshape(nb, q, p, q, gc) * eye[None, :, None, :, None]).sum(axis=3)
    return t.transpose(0, 1, 3, 2).reshape(nb * q, gc, p)


def _split2(m):
    return m.arr[:, 0]


def kernel(x, norm_mix_g, norm_ffn_g, norm_final_g, rg_w_in, rg_conv_w, rg_conv_b, rg_w_a, rg_b_a, rg_w_x, rg_b_x, rg_lambda, rg_w_out, s5_w_in, s5_a_re, s5_a_im, s5_log_dt, s5_b_re, s5_b_im, s5_c_re, s5_c_im, s5_d, s5_w_glu, s5_w_out, ffn_w_up, ffn_conv_w, ffn_conv_b, ffn_w_down, loss_target, m_norm_mix_g, m_norm_ffn_g, m_norm_final_g, m_rg_w_in, m_rg_conv_w, m_rg_conv_b, m_rg_w_a, m_rg_b_a, m_rg_w_x, m_rg_b_x, m_rg_lambda, m_rg_w_out, m_s5_w_in, m_s5_a_re, m_s5_a_im, m_s5_log_dt, m_s5_b_re, m_s5_b_im, m_s5_c_re, m_s5_c_im, m_s5_d, m_s5_w_glu, m_s5_w_out, m_ffn_w_up, m_ffn_conv_w, m_ffn_conv_b, m_ffn_w_down, v_norm_mix_g, v_norm_ffn_g, v_norm_final_g, v_rg_w_in, v_rg_conv_w, v_rg_conv_b, v_rg_w_a, v_rg_b_a, v_rg_w_x, v_rg_b_x, v_rg_lambda, v_rg_w_out, v_s5_w_in, v_s5_a_re, v_s5_a_im, v_s5_log_dt, v_s5_b_re, v_s5_b_im, v_s5_c_re, v_s5_c_im, v_s5_d, v_s5_w_glu, v_s5_w_out, v_ffn_w_up, v_ffn_conv_w, v_ffn_conv_b, v_ffn_w_down):
    w = dict(zip(PARAM_NAMES, (norm_mix_g, norm_ffn_g, norm_final_g, rg_w_in, rg_conv_w, rg_conv_b, rg_w_a, rg_b_a, rg_w_x, rg_b_x,
                               rg_lambda, rg_w_out, s5_w_in, s5_a_re, s5_a_im, s5_log_dt, s5_b_re, s5_b_im, s5_c_re, s5_c_im, s5_d,
                               s5_w_glu, s5_w_out, ffn_w_up, ffn_conv_w, ffn_conv_b, ffn_w_down)))
    mom = dict(zip(PARAM_NAMES, (m_norm_mix_g, m_norm_ffn_g, m_norm_final_g, m_rg_w_in, m_rg_conv_w, m_rg_conv_b, m_rg_w_a, m_rg_b_a,
                                 m_rg_w_x, m_rg_b_x, m_rg_lambda, m_rg_w_out, m_s5_w_in, m_s5_a_re, m_s5_a_im, m_s5_log_dt, m_s5_b_re,
                                 m_s5_b_im, m_s5_c_re, m_s5_c_im, m_s5_d, m_s5_w_glu, m_s5_w_out, m_ffn_w_up, m_ffn_conv_w,
                                 m_ffn_conv_b, m_ffn_w_down)))
    vel = dict(zip(PARAM_NAMES, (v_norm_mix_g, v_norm_ffn_g, v_norm_final_g, v_rg_w_in, v_rg_conv_w, v_rg_conv_b, v_rg_w_a, v_rg_b_a,
                                 v_rg_w_x, v_rg_b_x, v_rg_lambda, v_rg_w_out, v_s5_w_in, v_s5_a_re, v_s5_a_im, v_s5_log_dt, v_s5_b_re,
                                 v_s5_b_im, v_s5_c_re, v_s5_c_im, v_s5_d, v_s5_w_glu, v_s5_w_out, v_ffn_w_up, v_ffn_conv_w,
                                 v_ffn_conv_b, v_ffn_w_down)))
    _, s, d = x.shape
    depth = norm_mix_g.shape[0]
    n_grp, n_state = s5_a_re.shape[1], s5_a_re.shape[2]
    gc = s5_b_re.shape[3]
    d_ff = ffn_w_down.shape[1] * N_CHIPS
    s5_ts = min(256, s)

    wb = {n: (w[n].astype(BF16) if n in BIG else w[n]) for n in SHARDED}
    gath = {}

    def mixer_names(i):
        return MIXER_SHARDED[i % 2]

    def gather_side(names, layer):
        return _gather_side([wb[n][layer] for n in names])

    def put(names, layer, arrs):
        for n, a in zip(names, arrs):
            gath[(n, layer)] = a

    def wcol(n, l):
        return Mat(gath[(n, l)][:, None], 0, 'c')

    def wrow(n, l):
        g = gath[(n, l)]
        return Mat(g.reshape(1, 1, N_CHIPS * g.shape[1], g.shape[2]), 0, 'c')

    def rg_cw(l):
        return gath[('rg_conv_w', l)].transpose(1, 0, 2).reshape(RG_CONV_W, d)

    def s5_dv(l):
        return gath[('s5_d', l)].reshape(1, d)

    def f_cw(l):
        return gath[('ffn_conv_w', l)].transpose(1, 0, 2).reshape(FFN_CONV_W, 2, d_ff).transpose(1, 0, 2)

    tm = min(1024, s)
    d_up = 2 * d_ff // N_CHIPS
    f_cb = ffn_conv_b.reshape(depth, 2, 1, d_ff)
    put(mixer_names(0), 0, _run_side("gather_first", gather_side(mixer_names(0), 0)))

    h = x.reshape(s, d)
    saved = []
    for i in range(depth):
        j = i // 2
        sv = {'h_in': h}
        hn = _rms_fwd(h, norm_mix_g[i:i + 1])
        sv['hn'] = hn
        if i % 2 == 0:
            xg = _mm("rg_in", 'nn', act(hn), wcol('rg_w_in', j), out_parts=2, tm=tm, tn=512, tk=d)
            xg2 = _split2(xg)
            wa, wx = rg_w_a[j].astype(BF16), rg_w_x[j].astype(BF16)
            ba, bx = rg_b_a[j].reshape(1, d), rg_b_x[j].reshape(1, d)
            (xr, hs, y), got = _rg_fwd(xg2, rg_cw(j), rg_conv_b[j:j + 1], wa, ba, wx, bx, rg_lambda[j:j + 1],
                                       side=gather_side(FFN_SHARDED, i))
            put(FFN_SHARDED, i, got)
            sv.update(xg2=xg2, xr=xr, hs=hs, y=y, wa=wa, wx=wx, ba=ba, bx=bx)
            h = _mm("rg_out", 'nn', act(y), wrow('rg_w_out', j), res=act(h), tm=tm, tn=d, tk=d).arr[0, 0]
        else:
            u = _mm("s5_in", 'nn', act(hn), wrow('s5_w_in', j), tm=tm, tn=d, tk=d).arr[0, 0]
            bt_re, bt_im = s5_b_re[j].transpose(2, 0, 1), s5_b_im[j].transpose(2, 0, 1)
            ldt = s5_log_dt[j].reshape(n_grp, 1)
            _, _, tab_r, tab_i, bbr, bbi = _s5_tables(s5_a_re[j], s5_a_im[j], ldt, bt_re, bt_im)
            nn_ = n_grp * n_state
            tab_r, tab_i = tab_r.reshape(4, SUBLANES, nn_), tab_i.reshape(4, SUBLANES, nn_)
            prm = dict(bp_r=_s5_pack_b(bbr).astype(BF16), bp_i=_s5_pack_b(bbi).astype(BF16),
                       cp_r=_s5_pack_c(s5_c_re[j]).astype(BF16), cp_i=_s5_pack_c(s5_c_im[j]).astype(BF16), dvec=s5_dv(j))
            (hr, hi, ypre, gy), got = _s5_fwd2(u, tab_r, tab_i, ts=s5_ts, side=gather_side(FFN_SHARDED, i), **prm)
            sv.update(rtab_r=tab_r[:, ::-1], rtab_i=-tab_i[:, ::-1])
            put(FFN_SHARDED, i, got)
            gl = _mm("s5_glu", 'nn', act(gy), wcol('s5_w_glu', j), out_parts=2, tm=tm, tn=512, tk=d)
            gl2 = _split2(gl)
            o = _glu(gl2)
            sv.update(u=u, prm=prm, hr=hr, hi=hi, ypre=ypre, gy=gy, gl2=gl2, o=o, bt_re=bt_re, bt_im=bt_im, ldt=ldt)
            h = _mm("s5_out", 'nn', act(o), wrow('s5_w_out', j), res=act(h), tm=tm, tn=d, tk=d).arr[0, 0]
        sv['h_mid'] = h
        hn2 = _rms_fwd(h, norm_ffn_g[i:i + 1])
        up = _mm("ffn_up", 'nn', act(hn2), wcol('ffn_w_up', i), out_parts=2, tm=tm, tn=d_up, tk=d)
        up2 = _split2(up)
        nxt = i + 1
        a_ffn, got = _ffn_act(up2, f_cw(i), f_cb[i], side=gather_side(mixer_names(nxt), nxt // 2) if nxt < depth else None)
        if nxt < depth:
            put(mixer_names(nxt), nxt // 2, got)
        sv.update(hn2=hn2, up2=up2, act=a_ffn)
        h = _mm("ffn_down", 'nn', act(a_ffn), wrow('ffn_w_down', i), res=act(h), tm=tm, tn=d, tk=d_ff // 2).arr[0, 0]
        saved.append(sv)

    loss_row, dh, dg_final = _loss_and_grad(h, norm_final_g.reshape(1, d), loss_target.reshape(s, d))
    loss = lax.psum(loss_row[0, 0], ("x", "y", "c"))

    gl_ = {n: [None] * w[n].shape[0] for n in PARAM_NAMES if n != 'norm_final_g'}
    recvd = {}

    def as4(n, a):
        return a.reshape((N_CHIPS,) + w[n].shape[1:])

    def scatter_side(keys):
        return _scatter_side([as4(n, gl_[n][l]) for n, l in keys])

    def record(keys, arrs):
        for k, a in zip(keys, arrs):
            recvd[k] = a

    pending = None
    for i in reversed(range(depth)):
        j = i // 2
        sv = saved[i]
        dact = _mm("ffn_down_dx", 'nt', act(dh), wrow('ffn_w_down', i), tm=tm, tn=d_ff // 2, tk=d).arr[0, 0]
        gl_['ffn_w_down'][i] = _mm("ffn_down_dw", 'tn', act(sv['act']), act(dh), out_dtype=BF16, tm=d_ff // 2, tn=d, tk=tm).arr
        (dup2, dcw2, dcb2), got = _ffn_bwd(sv['up2'], dact, f_cw(i), f_cb[i], side=scatter_side(pending) if pending else None)
        if pending:
            record(pending, got)
        gl_['ffn_conv_w'][i] = dcw2.transpose(1, 0, 2).reshape(FFN_CONV_W, 2 * d_ff)
        gl_['ffn_conv_b'][i] = dcb2.reshape(2 * d_ff)
        dup = Mat(dup2[:, None], 0, 'c')
        gl_['ffn_w_up'][i] = _mm("ffn_up_dw", 'tn', act(sv['hn2']), dup, out_parts=N_CHIPS, out_dtype=BF16, tm=d, tn=d_up, tk=tm).arr
        dhn2 = _mm("ffn_up_dx", 'nt', dup, wcol('ffn_w_up', i), tm=tm, tn=d, tk=d_up).arr[0, 0]
        dh, dg = _rms_bwd(sv['h_mid'], norm_ffn_g[i:i + 1], dhn2, dh)
        gl_['norm_ffn_g'][i] = dg[0]
        ffn_keys = [('ffn_w_up', i), ('ffn_w_down', i)]
        if i % 2 == 0:
            dy = _mm("rg_out_dx", 'nt', act(dh), wrow('rg_w_out', j), tm=tm, tn=d, tk=d).arr[0, 0]
            gl_['rg_w_out'][j] = _mm("rg_out_dw", 'tn', act(sv['y']), act(dh), out_dtype=BF16, tm=d, tn=d, tk=tm).arr
            (dxg2, dcw, dcb, dwa, dba, dwx, dbx, dlam), got = _rg_bwd(
                dy, sv['xg2'], sv['xr'], sv['hs'], rg_cw(j), sv['wa'], sv['ba'], sv['wx'], sv['bx'], rg_lambda[j:j + 1],
                side=scatter_side(ffn_keys))
            record(ffn_keys, got)
            gl_['rg_conv_w'][j] = dcw
            gl_['rg_conv_b'][j] = dcb[0]
            gl_['rg_w_a'][j], gl_['rg_w_x'][j] = dwa, dwx
            gl_['rg_b_a'][j], gl_['rg_b_x'][j] = dba.reshape(rg_b_a.shape[1:]), dbx.reshape(rg_b_x.shape[1:])
            gl_['rg_lambda'][j] = dlam[0]
            dxg = Mat(dxg2[:, None], 0, 'c')
            gl_['rg_w_in'][j] = _mm("rg_in_dw", 'tn', act(sv['hn']), dxg, out_parts=N_CHIPS, out_dtype=BF16, tm=d, tn=512, tk=tm).arr
            dhn = _mm("rg_in_dx", 'nt', dxg, wcol('rg_w_in', j), tm=tm, tn=d, tk=512).arr[0, 0]
            pending = [('rg_w_in', j), ('rg_w_out', j)]
        else:
            d_o = _mm("s5_out_dx", 'nt', act(dh), wrow('s5_w_out', j), tm=tm, tn=d, tk=d).arr[0, 0]
            gl_['s5_w_out'][j] = _mm("s5_out_dw", 'tn', act(sv['o']), act(dh), out_dtype=BF16, tm=d, tn=d, tk=tm).arr
            dgl2 = _glu_bwd(sv['gl2'], d_o)
            dgl = Mat(dgl2[:, None], 0, 'c')
            gl_['s5_w_glu'][j] = _mm("s5_glu_dw", 'tn', act(sv['gy']), dgl, out_parts=N_CHIPS, out_dtype=BF16, tm=d, tn=512, tk=tm).arr
            dgy = _mm("s5_glu_dx", 'nt', dgl, wcol('s5_w_glu', j), tm=tm, tn=d, tk=512).arr[0, 0]
            (du, dar, dai, dbpr, dbpi, dcpr, dcpi, dd), got = _s5_bwd2(
                dgy, sv['ypre'], sv['u'], sv['hr'], sv['hi'], sv['rtab_r'], sv['rtab_i'], ts=s5_ts, side=scatter_side(ffn_keys),
                **sv['prm'])
            record(ffn_keys, got)
            gl_['s5_d'][j] = dd[0]
            gl_['s5_c_re'][j] = _s5_unpack_c(dcpr, gc, n_state)
            gl_['s5_c_im'][j] = -_s5_unpack_c(dcpi, gc, n_state)
            d_are, d_aim, d_ldt, d_btr, d_bti = _s5_params_bwd(
                s5_a_re[j], s5_a_im[j], sv['ldt'], sv['bt_re'], sv['bt_im'], dar.reshape(n_grp, n_state), dai.reshape(n_grp, n_state),
                _s5_unpack_b(dbpr, gc, n_state), _s5_unpack_b(dbpi, gc, n_state))
            gl_['s5_a_re'][j], gl_['s5_a_im'][j], gl_['s5_log_dt'][j] = d_are, d_aim, d_ldt[:, 0]
            gl_['s5_b_re'][j], gl_['s5_b_im'][j] = d_btr.transpose(1, 2, 0), d_bti.transpose(1, 2, 0)
            dum = act(du)
            gl_['s5_w_in'][j] = _mm("s5_in_dw", 'tn', act(sv['hn']), dum, out_dtype=BF16, tm=d, tn=d, tk=tm).arr
            dhn = _mm("s5_in_dx", 'nt', dum, wrow('s5_w_in', j), tm=tm, tn=d, tk=d).arr[0, 0]
            pending = [('s5_w_in', j), ('s5_w_glu', j), ('s5_w_out', j)]
        dh, dg = _rms_bwd(sv['h_in'], norm_mix_g[i:i + 1], dhn, dh)
        gl_['norm_mix_g'][i] = dg[0]
    grad_x = dh.reshape(x.shape)
    record(pending, _run_side("scatter_last", scatter_side(pending)))

    chip_sums = []
    for n in BIG:
        cols = w[n].shape[-1]
        chip_sums.append(_sum_parts([recvd[(n, l)].reshape(N_CHIPS, -1, cols) for l in range(w[n].shape[0])]))
    sib_sums = _swap_with_sibling(chip_sums)
    results = {}
    for n, mine, theirs in zip(BIG, chip_sums, sib_sums):
        cols = w[n].shape[-1]
        outs = _adamw(w[n].reshape(-1, cols), [mine.reshape(-1, cols), theirs.reshape(-1, cols)], mom[n].reshape(-1, cols),
                      vel[n].reshape(-1, cols))
        results[n] = [o.reshape(w[n].shape) for o in outs]

    small = REPLICATED + SMALL_SHARDED
    local = [dg_final.reshape(d) if n == 'norm_final_g' else jnp.stack(gl_[n]) for n in small]
    summed = _unpack(_allreduce_small(_pack(local)), local)
    me = 2 * lax.axis_index("x") + lax.axis_index("y")
    grads = [lax.dynamic_slice_in_dim(g, me * w[n].shape[-1], w[n].shape[-1], axis=g.ndim - 1) if n in SMALL_SHARDED else g
             for n, g in zip(small, summed)]
    like = [w[n] for n in small]
    outs = _adamw(_pack(like), [_pack(grads)], _pack([mom[n] for n in small]), _pack([vel[n] for n in small]))
    unpacked = [_unpack(o, like) for o in outs]
    for k, n in enumerate(small):
        results[n] = [unpacked[q][k] for q in range(4)]

    return (loss, grad_x, *[results[n][0] for n in PARAM_NAMES], *[results[n][1] for n in PARAM_NAMES],
            *[results[n][2] for n in PARAM_NAMES], *[results[n][3] for n in PARAM_NAMES])
```

```python
import functools
import math

import jax
import jax.numpy as jnp
from jax import lax
from jax.experimental import pallas as pl
from jax.experimental.pallas import tpu as pltpu

F32 = jnp.float32
BF16 = jnp.bfloat16
MESH = pl.DeviceIdType.MESH

NORM_EPS = 1e-6
RG_HEADS = 8
RG_CONV_W = 4
RG_C = 8.0
S5_GC = 16
S5_P = 64
S5_GROUPS_PER_BLOCK = 8
FFN_CONV_W = 3
N_CHIPS = 4
ADAM_LR, ADAM_B1, ADAM_B2, ADAM_EPS, ADAM_WD, ADAM_STEP = 0.001, 0.9, 0.999, 1e-08, 0.01, 10
VMEM_LIMIT_BYTES = 56 * 1024 * 1024
SUBLANES = 8
LANES = 128

PARAM_NAMES = ['norm_mix_g', 'norm_ffn_g', 'norm_final_g', 'rg_w_in', 'rg_conv_w', 'rg_conv_b', 'rg_w_a', 'rg_b_a', 'rg_w_x',
               'rg_b_x', 'rg_lambda', 'rg_w_out', 's5_w_in', 's5_a_re', 's5_a_im', 's5_log_dt', 's5_b_re', 's5_b_im', 's5_c_re',
               's5_c_im', 's5_d', 's5_w_glu', 's5_w_out', 'ffn_w_up', 'ffn_conv_w', 'ffn_conv_b', 'ffn_w_down']
SHARDED = ['rg_w_in', 'rg_conv_w', 'rg_w_out', 's5_w_in', 's5_d', 's5_w_glu', 's5_w_out', 'ffn_w_up', 'ffn_conv_w', 'ffn_w_down']
BIG = ['rg_w_in', 'rg_w_out', 's5_w_in', 's5_w_glu', 's5_w_out', 'ffn_w_up', 'ffn_w_down']
ROW_SHARDED = ['rg_w_out', 's5_w_in', 's5_w_out', 'ffn_w_down']
SMALL_SHARDED = ['rg_conv_w', 's5_d', 'ffn_conv_w']
MIXER_SHARDED = [['rg_w_in', 'rg_conv_w', 'rg_w_out'], ['s5_w_in', 's5_d', 's5_w_glu', 's5_w_out']]
FFN_SHARDED = ['ffn_w_up', 'ffn_conv_w', 'ffn_w_down']
REPLICATED = [n for n in PARAM_NAMES if n not in SHARDED]


def _cparams():
    return pltpu.CompilerParams(vmem_limit_bytes=VMEM_LIMIT_BYTES)


_GELU_C = math.sqrt(2.0 / math.pi)
_GELU_K = 0.044715


def _gelu(x):
    return 0.5 * x * (1.0 + jnp.tanh(_GELU_C * (x + _GELU_K * x * x * x)))


def _gelu_and_grad(x):
    t = jnp.tanh(_GELU_C * (x + _GELU_K * x * x * x))
    g = 0.5 * x * (1.0 + t)
    dg = 0.5 * (1.0 + t) + 0.5 * x * (1.0 - t * t) * (_GELU_C * (1.0 + 3.0 * _GELU_K * x * x))
    return g, dg


def _sigmoid(x):
    return jax.nn.sigmoid(x)


def _neg_expm1(x):
    series = -(x * (1.0 + x * (0.5 + x * (1.0 / 6 + x * (1.0 / 24 + x * (1.0 / 120 + x * (1.0 / 720)))))))
    return jnp.where(x > -0.25, series, 1.0 - jnp.exp(x))


def _softplus(z):
    return jnp.maximum(z, 0.0) + jnp.log1p(jnp.exp(-jnp.abs(z)))


def _rows(shape):
    return lax.broadcasted_iota(jnp.int32, shape, 0)


def _shift_down(x, halo, k):
    ext = jnp.concatenate([halo, x], axis=0)
    return pltpu.roll(ext, k, 0)[SUBLANES:]


def _shift_up(x, halo, k):
    ext = jnp.concatenate([x, halo], axis=0)
    n = ext.shape[0]
    return pltpu.roll(ext, n - k, 0)[:x.shape[0]]


def _scan_real_fwd(a, b):
    n = a.shape[0]
    row = _rows(a.shape)
    sh = 1
    while sh < n:
        ok = row >= sh
        b = a * jnp.where(ok, pltpu.roll(b, sh, 0), 0.0) + b
        if sh * 2 < n:
            a = a * jnp.where(ok, pltpu.roll(a, sh, 0), 1.0)
        sh *= 2
    return b


def _scan_real_rev(c, d):
    n = c.shape[0]
    row = _rows(c.shape)
    sh = 1
    while sh < n:
        ok = row < n - sh
        d = c * jnp.where(ok, pltpu.roll(d, n - sh, 0), 0.0) + d
        if sh * 2 < n:
            c = c * jnp.where(ok, pltpu.roll(c, n - sh, 0), 1.0)
        sh *= 2
    return d


def _scan_cplx(br, bi, pr_ref, pi_ref, reverse):
    n = br.shape[0]
    row = _rows(br.shape)
    sh, k = 1, 0
    while sh < n:
        pr = pr_ref[k:k + 1, :]
        pi = pi_ref[k:k + 1, :]
        if reverse:
            ok = row < n - sh
            sr = jnp.where(ok, pltpu.roll(br, n - sh, 0), 0.0)
            si = jnp.where(ok, pltpu.roll(bi, n - sh, 0), 0.0)
        else:
            ok = row >= sh
            sr = jnp.where(ok, pltpu.roll(br, sh, 0), 0.0)
            si = jnp.where(ok, pltpu.roll(bi, sh, 0), 0.0)
        br, bi = br + pr * sr - pi * si, bi + pr * si + pi * sr
        sh *= 2
        k += 1
    return br, bi


class Mat:
    def __init__(self, arr, l=0, split='c'):
        assert arr.ndim == 4
        self.arr, self.l, self.split = arr, l, split
        p, _, r, c = arr.shape
        self.shape = (r, c * p) if split == 'c' else (r * p, c)

    def spec(self, tr, tc, rc):
        p, _, r, c = self.arr.shape
        l = self.l
        assert r % tr == 0 and c % tc == 0, (self.arr.shape, tr, tc)
        if self.split == 'c':
            per = c // tc
            return pl.BlockSpec((None, None, tr, tc), lambda i, j, k: (rc(i, j, k)[1] // per, l, rc(i, j, k)[0], rc(i, j, k)[1] % per))
        per = r // tr
        return pl.BlockSpec((None, None, tr, tc), lambda i, j, k: (rc(i, j, k)[0] // per, l, rc(i, j, k)[0] % per, rc(i, j, k)[1]))


def act(x, parts=1):
    s, c = x.shape
    return Mat(x.reshape(s, parts, c // parts).transpose(1, 0, 2)[:, None] if parts > 1 else x[None, None])


def _mm(name, mode, a, b, *, out_parts=1, out_split='c', out_dtype=F32, res=None, tm=512, tn=512, tk=512):
    if mode == 'nn':
        (m, kk), (kb, n) = a.shape, b.shape
    elif mode == 'nt':
        (m, kk), (n, kb) = a.shape, b.shape
    else:
        (kk, m), (kb, n) = a.shape, b.shape
    assert kk == kb, (name, a.shape, b.shape)
    tm, tn, tk = min(tm, m), min(tn, n), min(tk, kk)
    assert m % tm == 0 and n % tn == 0 and kk % tk == 0, (name, m, n, kk, tm, tn, tk)
    nk = kk // tk
    if mode == 'nn':
        a_spec = a.spec(tm, tk, lambda i, j, k: (i, k))
        b_spec = b.spec(tk, tn, lambda i, j, k: (k, j))
        dims = (((1,), (0,)), ((), ()))
    elif mode == 'nt':
        a_spec = a.spec(tm, tk, lambda i, j, k: (i, k))
        b_spec = b.spec(tn, tk, lambda i, j, k: (j, k))
        dims = (((1,), (1,)), ((), ()))
    else:
        a_spec = a.spec(tk, tm, lambda i, j, k: (k, i))
        b_spec = b.spec(tk, tn, lambda i, j, k: (k, j))
        dims = (((0,), (0,)), ((), ()))
    if out_split == 'c':
        out_arr = jax.ShapeDtypeStruct((out_parts, 1, m, n // out_parts), out_dtype)
    else:
        out_arr = jax.ShapeDtypeStruct((out_parts, 1, m // out_parts, n), out_dtype)
    out_mat = Mat(out_arr, 0, out_split)
    o_spec = out_mat.spec(tm, tn, lambda i, j, k: (i, j))
    has_res = res is not None

    def body(*refs):
        if has_res:
            a_ref, b_ref, r_ref, o_ref = refs[:4]
        else:
            a_ref, b_ref, o_ref = refs[:3]
        prod = lax.dot_general(a_ref[...].astype(BF16), b_ref[...].astype(BF16), dims, preferred_element_type=F32)

        def finish(acc):
            if has_res:
                acc = acc + r_ref[...]
            o_ref[...] = acc.astype(out_dtype)

        if nk == 1:
            finish(prod)
        else:
            acc_ref = refs[-1]
            k = pl.program_id(2)

            @pl.when(k == 0)
            def _():
                acc_ref[...] = prod

            @pl.when(k > 0)
            def _():
                acc_ref[...] += prod

            @pl.when(k == nk - 1)
            def _():
                finish(acc_ref[...])

    in_specs = [a_spec, b_spec]
    args = [a.arr, b.arr]
    if has_res:
        in_specs.append(res.spec(tm, tn, lambda i, j, k: (i, j)))
        args.append(res.arr)
    out = pl.pallas_call(
        body, name=name, grid=(m // tm, n // tn, nk), in_specs=in_specs, out_specs=o_spec, out_shape=out_arr,
        scratch_shapes=[pltpu.VMEM((tm, tn), F32)] if nk > 1 else [], compiler_params=_cparams(),
    )(*args)
    return Mat(out, 0, out_split)


def _rms_fwd(h, g, ts=512):
    s, d = h.shape
    ts = min(ts, s)

    def body(h_ref, g_ref, o_ref):
        x = h_ref[...]
        var = jnp.mean(x * x, axis=-1, keepdims=True)
        o_ref[...] = (x * lax.rsqrt(var + NORM_EPS) * g_ref[...]).astype(BF16)

    return pl.pallas_call(
        body, name="rms_fwd", grid=(s // ts,),
        in_specs=[pl.BlockSpec((ts, d), lambda i: (i, 0)), pl.BlockSpec((1, d), lambda i: (0, 0))],
        out_specs=pl.BlockSpec((ts, d), lambda i: (i, 0)), out_shape=jax.ShapeDtypeStruct((s, d), BF16),
        compiler_params=_cparams(),
    )(h, g)


def _rms_bwd(h, g, dhn, dh_in, ts=512):
    s, d = h.shape
    ts = min(ts, s)

    def body(h_ref, g_ref, dhn_ref, dhin_ref, dh_ref, dg_ref):
        i = pl.program_id(0)
        x = h_ref[...]
        rstd = lax.rsqrt(jnp.mean(x * x, axis=-1, keepdims=True) + NORM_EPS)
        xhat = x * rstd
        dhn_v = dhn_ref[...]
        dxh = dhn_v * g_ref[...]
        dh_ref[...] = dhin_ref[...] + rstd * (dxh - xhat * jnp.mean(dxh * xhat, axis=-1, keepdims=True))
        part = jnp.sum(dhn_v * xhat, axis=0, keepdims=True)

        @pl.when(i == 0)
        def _():
            dg_ref[...] = part

        @pl.when(i > 0)
        def _():
            dg_ref[...] += part

    row = pl.BlockSpec((ts, d), lambda i: (i, 0))
    vec = pl.BlockSpec((1, d), lambda i: (0, 0))
    return pl.pallas_call(
        body, name="rms_bwd", grid=(s // ts,), in_specs=[row, vec, row, row], out_specs=[row, vec],
        out_shape=[jax.ShapeDtypeStruct((s, d), F32), jax.ShapeDtypeStruct((1, d), F32)], compiler_params=_cparams(),
    )(h, g, dhn, dh_in)


def _loss_and_grad(h, g, tgt, ts=512):
    s, d = h.shape
    ts = min(ts, s)

    def body(h_ref, g_ref, t_ref, loss_ref, dh_ref, dg_ref):
        i = pl.program_id(0)
        x = h_ref[...]
        gv = g_ref[...]
        rstd = lax.rsqrt(jnp.mean(x * x, axis=-1, keepdims=True) + NORM_EPS)
        xhat = x * rstd
        err = xhat * gv - t_ref[...]
        dy = err * (1.0 / d)
        dxh = dy * gv
        dh_ref[...] = rstd * (dxh - xhat * jnp.mean(dxh * xhat, axis=-1, keepdims=True))
        part = jnp.sum(dy * xhat, axis=0, keepdims=True)
        lpart = jnp.broadcast_to(jnp.sum(jnp.sum(err * err, axis=0, keepdims=True), axis=1, keepdims=True) * (0.5 / d), (1, LANES))

        @pl.when(i == 0)
        def _():
            dg_ref[...] = part
            loss_ref[...] = lpart

        @pl.when(i > 0)
        def _():
            dg_ref[...] += part
            loss_ref[...] += lpart

    row = pl.BlockSpec((ts, d), lambda i: (i, 0))
    vec = pl.BlockSpec((1, d), lambda i: (0, 0))
    return pl.pallas_call(
        body, name="loss_and_grad", grid=(s // ts,), in_specs=[row, vec, row],
        out_specs=[pl.BlockSpec((1, LANES), lambda i: (0, 0)), row, vec],
        out_shape=[jax.ShapeDtypeStruct((1, LANES), F32), jax.ShapeDtypeStruct((s, d), F32), jax.ShapeDtypeStruct((1, d), F32)],
        compiler_params=_cparams(),
    )(h, g, tgt)


def _halo_before(ts, nrow8):
    return lambda i: jnp.maximum(i * (ts // SUBLANES) - 1, 0)


def _ffn_act(up2, conv_w2, conv_b2, ts=512, tn=512, side=None):
    _, s, f = up2.shape
    ts, tn = min(ts, s), min(tn, f)
    kw = FFN_CONV_W

    def body(up_ref, halo_ref, w_ref, b_ref, o_ref):
        i = pl.program_id(0)
        cs = []
        for h in range(2):
            x = up_ref[h]
            halo = jnp.where(i == 0, 0.0, halo_ref[h])
            c = b_ref[h] + w_ref[h, kw - 1:kw, :] * x
            for sft in range(1, kw):
                c = c + w_ref[h, kw - 1 - sft:kw - sft, :] * _shift_down(x, halo, sft)
            cs.append(c)
        o_ref[...] = (_gelu(cs[0]) * cs[1]).astype(BF16)

    hb = ts // SUBLANES
    g0, g1 = s // ts, f // tn
    outs, side_outs = _call_with_side(
        body, side, lambda: (pl.program_id(0) == 0) & (pl.program_id(1) == 0),
        lambda: (pl.program_id(0) == g0 - 1) & (pl.program_id(1) == g1 - 1),
        name="ffn_act", grid=(g0, g1),
        in_specs=[pl.BlockSpec((2, ts, tn), lambda i, j: (0, i, j)),
                  pl.BlockSpec((2, SUBLANES, tn), lambda i, j: (0, jnp.maximum(i * hb - 1, 0), j)),
                  pl.BlockSpec((2, kw, tn), lambda i, j: (0, 0, j)),
                  pl.BlockSpec((2, 1, tn), lambda i, j: (0, 0, j))],
        out_specs=[pl.BlockSpec((ts, tn), lambda i, j: (i, j))], out_shape=[jax.ShapeDtypeStruct((s, f), BF16)],
        scratch_shapes=[], args=(up2, up2, conv_w2, conv_b2))
    return outs[0], side_outs


def _ffn_bwd(up2, dact, conv_w2, conv_b2, ts=256, tn=512, side=None):
    _, s, f = up2.shape
    ts, tn = min(ts, s), min(tn, f)
    kw = FFN_CONV_W
    nt = s // ts
    hb = ts // SUBLANES
    last8 = s // SUBLANES - 1

    def body(up_ref, hb_ref, ha_ref, da_ref, dah_ref, w_ref, b_ref, dup_ref, dw_ref, db_ref):
        i = pl.program_id(1)
        first, last = i == 0, i == nt - 1
        ce, xs = [], []
        for h in range(2):
            x = up_ref[h]
            before = jnp.where(first, 0.0, hb_ref[h])
            after = ha_ref[h]
            ext = jnp.concatenate([before, x, after], axis=0)
            c = b_ref[h] + w_ref[h, kw - 1:kw, :] * ext
            shifted = [ext]
            for sft in range(1, kw):
                sh = pltpu.roll(ext, sft, 0)
                shifted.append(sh)
                c = c + w_ref[h, kw - 1 - sft:kw - sft, :] * sh
            ce.append(c[SUBLANES:])
            xs.append([sh[SUBLANES:SUBLANES + ts] for sh in shifted])
        da = jnp.concatenate([da_ref[...], jnp.where(last, 0.0, dah_ref[...])], axis=0)
        g1, dg1 = _gelu_and_grad(ce[0])
        dcs = [da * ce[1] * dg1, da * g1]
        for h in range(2):
            dc = dcs[h]
            n = dc.shape[0]
            dup = w_ref[h, kw - 1:kw, :] * dc[:ts]
            for sft in range(1, kw):
                dup = dup + w_ref[h, kw - 1 - sft:kw - sft, :] * pltpu.roll(dc, n - sft, 0)[:ts]
            dup_ref[h] = dup.astype(BF16)
            dct = dc[:ts]
            dbp = jnp.sum(dct, axis=0, keepdims=True)
            dwp = [jnp.sum(dct * xs[h][kw - 1 - k], axis=0, keepdims=True) for k in range(kw)]

            @pl.when(first)
            def _():
                db_ref[h] = dbp
                for k in range(kw):
                    dw_ref[h, k:k + 1, :] = dwp[k]

            @pl.when(i > 0)
            def _():
                db_ref[h] += dbp
                for k in range(kw):
                    dw_ref[h, k:k + 1, :] += dwp[k]

    g0 = f // tn
    return _call_with_side(
        body, side, lambda: (pl.program_id(0) == 0) & (pl.program_id(1) == 0),
        lambda: (pl.program_id(0) == g0 - 1) & (pl.program_id(1) == nt - 1),
        name="ffn_bwd", grid=(g0, nt),
        in_specs=[pl.BlockSpec((2, ts, tn), lambda j, i: (0, i, j)),
                  pl.BlockSpec((2, SUBLANES, tn), lambda j, i: (0, jnp.maximum(i * hb - 1, 0), j)),
                  pl.BlockSpec((2, SUBLANES, tn), lambda j, i: (0, jnp.minimum((i + 1) * hb, last8), j)),
                  pl.BlockSpec((ts, tn), lambda j, i: (i, j)),
                  pl.BlockSpec((SUBLANES, tn), lambda j, i: (jnp.minimum((i + 1) * hb, last8), j)),
                  pl.BlockSpec((2, kw, tn), lambda j, i: (0, 0, j)),
                  pl.BlockSpec((2, 1, tn), lambda j, i: (0, 0, j))],
        out_specs=[pl.BlockSpec((2, ts, tn), lambda j, i: (0, i, j)),
                   pl.BlockSpec((2, kw, tn), lambda j, i: (0, 0, j)),
                   pl.BlockSpec((2, 1, tn), lambda j, i: (0, 0, j))],
        out_shape=[jax.ShapeDtypeStruct((2, s, f), BF16), jax.ShapeDtypeStruct((2, kw, f), F32),
                   jax.ShapeDtypeStruct((2, 1, f), F32)],
        scratch_shapes=[], args=(up2, up2, up2, dact, dact, conv_w2, conv_b2))


def _mm_rms_bwd(name, a, b, h, g, dh_in, *, tm, tk):
    (m, kk), (n, kb) = a.shape, b.shape
    assert kk == kb and h.shape == (m, n), (name, a.shape, b.shape, h.shape)
    tm, tk = min(tm, m), min(tk, kk)
    nk = kk // tk
    dims = (((1,), (1,)), ((), ()))

    def body(a_ref, b_ref, h_ref, g_ref, dhin_ref, dh_ref, dg_ref, *acc):
        i, k = pl.program_id(0), pl.program_id(2)
        prod = lax.dot_general(a_ref[...].astype(BF16), b_ref[...].astype(BF16), dims, preferred_element_type=F32)

        def finish(dhn):
            x = h_ref[...]
            rstd = lax.rsqrt(jnp.mean(x * x, axis=-1, keepdims=True) + NORM_EPS)
            xhat = x * rstd
            dxh = dhn * g_ref[...]
            dh_ref[...] = dhin_ref[...] + rstd * (dxh - xhat * jnp.mean(dxh * xhat, axis=-1, keepdims=True))
            part = jnp.sum(dhn * xhat, axis=0, keepdims=True)

            @pl.when(i == 0)
            def _():
                dg_ref[...] = part

            @pl.when(i > 0)
            def _():
                dg_ref[...] += part

        if nk == 1:
            finish(prod)
        else:
            acc_ref = acc[0]

            @pl.when(k == 0)
            def _():
                acc_ref[...] = prod

            @pl.when(k > 0)
            def _():
                acc_ref[...] += prod

            @pl.when(k == nk - 1)
            def _():
                finish(acc_ref[...])

    row = pl.BlockSpec((tm, n), lambda i, j, k: (i, 0))
    vec = pl.BlockSpec((1, n), lambda i, j, k: (0, 0))
    return pl.pallas_call(
        body, name=name, grid=(m // tm, 1, nk),
        in_specs=[a.spec(tm, tk, lambda i, j, k: (i, k)), b.spec(n, tk, lambda i, j, k: (0, k)), row, vec, row],
        out_specs=[row, vec], out_shape=[jax.ShapeDtypeStruct((m, n), F32), jax.ShapeDtypeStruct((1, n), F32)],
        scratch_shapes=[pltpu.VMEM((tm, n), F32)] if nk > 1 else [], compiler_params=_cparams(),
    )(a.arr, b.arr, h, g, dh_in)


def _ffn_up_act(hn2, w_up4, conv_w2, conv_b2, ts=1024, tn=512, sub=256, side=None):
    s, d = hn2.shape
    p, _, wc = w_up4.shape
    f = p * wc // 2
    ts, tn = min(ts, s), min(tn, wc)
    sub = min(sub, ts)
    per = wc // tn
    kw = FFN_CONV_W
    g0, g1 = f // tn, s // ts

    def body(hn_ref, w1_ref, w2_ref, cw_ref, cb_ref, up_ref, act_ref, carry_ref):
        @pl.when(pl.program_id(1) == 0)
        def _():
            carry_ref[...] = jnp.zeros_like(carry_ref)

        for q in range(ts // sub):
            rows = slice(q * sub, (q + 1) * sub)
            hn = hn_ref[rows, :]
            cs = []
            for h, w_ref in enumerate((w1_ref, w2_ref)):
                x = jnp.dot(hn, w_ref[...], preferred_element_type=F32)
                up_ref[h, rows, :] = x
                halo = carry_ref[h]
                c = cb_ref[h] + cw_ref[h, kw - 1:kw, :] * x
                for sft in range(1, kw):
                    c = c + cw_ref[h, kw - 1 - sft:kw - sft, :] * _shift_down(x, halo, sft)
                carry_ref[h] = x[sub - SUBLANES:, :]
                cs.append(c)
            act_ref[rows, :] = (_gelu(cs[0]) * cs[1]).astype(BF16)

    outs, side_outs = _call_with_side(
        body, side, lambda: (pl.program_id(0) == 0) & (pl.program_id(1) == 0),
        lambda: (pl.program_id(0) == g0 - 1) & (pl.program_id(1) == g1 - 1),
        name="ffn_up_act", grid=(g0, g1),
        in_specs=[pl.BlockSpec((ts, d), lambda j, i: (i, 0)),
                  pl.BlockSpec((None, d, tn), lambda j, i: (j // per, 0, j % per)),
                  pl.BlockSpec((None, d, tn), lambda j, i: (p // 2 + j // per, 0, j % per)),
                  pl.BlockSpec((2, kw, tn), lambda j, i: (0, 0, j)),
                  pl.BlockSpec((2, 1, tn), lambda j, i: (0, 0, j))],
        out_specs=[pl.BlockSpec((2, ts, tn), lambda j, i: (0, i, j)), pl.BlockSpec((ts, tn), lambda j, i: (i, j))],
        out_shape=[jax.ShapeDtypeStruct((2, s, f), F32), jax.ShapeDtypeStruct((s, f), BF16)],
        scratch_shapes=[pltpu.VMEM((2, SUBLANES, tn), F32)], args=(hn2, w_up4, w_up4, conv_w2, conv_b2))
    return outs, side_outs


def _ffn_bwd_fused(dh, w_down, up2, conv_w2, conv_b2, ts=256, tn=512, side=None):
    s, d = dh.shape
    _, _, f = up2.shape
    ts, tn = min(ts, s), min(tn, f)
    kw = FFN_CONV_W
    nt = s // ts
    hb = ts // SUBLANES
    g0 = f // tn
    nt_dims = (((1,), (1,)), ((), ()))

    def body(dh_ref, wd_ref, up_ref, hb_ref, w_ref, b_ref, dup_ref, dw_ref, db_ref, carry_ref):
        i = pl.program_id(1)
        first_step, time_first = i == 0, i == nt - 1

        @pl.when(first_step)
        def _():
            carry_ref[...] = jnp.zeros_like(carry_ref)

        da = lax.dot_general(dh_ref[...].astype(BF16), wd_ref[...], nt_dims, preferred_element_type=F32)
        cs, xs = [], []
        for h in range(2):
            x = up_ref[h]
            ext = jnp.concatenate([jnp.where(time_first, 0.0, hb_ref[h]), x], axis=0)
            c = b_ref[h] + w_ref[h, kw - 1:kw, :] * ext
            shifted = [x]
            for sft in range(1, kw):
                sh = pltpu.roll(ext, sft, 0)
                shifted.append(sh[SUBLANES:])
                c = c + w_ref[h, kw - 1 - sft:kw - sft, :] * sh
            cs.append(c[SUBLANES:])
            xs.append(shifted)
        g1, dg1 = _gelu_and_grad(cs[0])
        dcs = [da * cs[1] * dg1, da * g1]
        for h in range(2):
            dc = dcs[h]
            after = carry_ref[h]
            dup = w_ref[h, kw - 1:kw, :] * dc
            for sft in range(1, kw):
                dup = dup + w_ref[h, kw - 1 - sft:kw - sft, :] * _shift_up(dc, after, sft)
            carry_ref[h] = dc[:SUBLANES]
            dup_ref[h] = dup.astype(BF16)
            dbp = jnp.sum(dc, axis=0, keepdims=True)
            dwp = [jnp.sum(dc * xs[h][kw - 1 - k], axis=0, keepdims=True) for k in range(kw)]

            @pl.when(first_step)
            def _():
                db_ref[h] = dbp
                for k in range(kw):
                    dw_ref[h, k:k + 1, :] = dwp[k]

            @pl.when(i > 0)
            def _():
                db_ref[h] += dbp
                for k in range(kw):
                    dw_ref[h, k:k + 1, :] += dwp[k]

    rev = lambda i: nt - 1 - i
    return _call_with_side(
        body, side, lambda: (pl.program_id(0) == 0) & (pl.program_id(1) == 0),
        lambda: (pl.program_id(0) == g0 - 1) & (pl.program_id(1) == nt - 1),
        name="ffn_bwd", grid=(g0, nt),
        in_specs=[pl.BlockSpec((ts, d), lambda j, i: (rev(i), 0)),
                  pl.BlockSpec((tn, d), lambda j, i: (j, 0)),
                  pl.BlockSpec((2, ts, tn), lambda j, i: (0, rev(i), j)),
                  pl.BlockSpec((2, SUBLANES, tn), lambda j, i: (0, jnp.maximum(rev(i) * hb - 1, 0), j)),
                  pl.BlockSpec((2, kw, tn), lambda j, i: (0, 0, j)),
                  pl.BlockSpec((2, 1, tn), lambda j, i: (0, 0, j))],
        out_specs=[pl.BlockSpec((2, ts, tn), lambda j, i: (0, rev(i), j)),
                   pl.BlockSpec((2, kw, tn), lambda j, i: (0, 0, j)),
                   pl.BlockSpec((2, 1, tn), lambda j, i: (0, 0, j))],
        out_shape=[jax.ShapeDtypeStruct((2, s, f), BF16), jax.ShapeDtypeStruct((2, kw, f), F32),
                   jax.ShapeDtypeStruct((2, 1, f), F32)],
        scratch_shapes=[pltpu.VMEM((2, SUBLANES, tn), F32)], args=(dh, w_down, up2, up2, conv_w2, conv_b2))


def _rg_gates(xr, wa_ref, ba_ref, wx_ref, bx_ref, lam_ref):
    bw = wa_ref.shape[-1]
    xb = xr.astype(BF16)
    za = jnp.concatenate([jnp.dot(xb[:, h * bw:(h + 1) * bw], wa_ref[h], preferred_element_type=F32)
                          for h in range(RG_HEADS)], axis=1) + ba_ref[...]
    zx = jnp.concatenate([jnp.dot(xb[:, h * bw:(h + 1) * bw], wx_ref[h], preferred_element_type=F32)
                          for h in range(RG_HEADS)], axis=1) + bx_ref[...]
    r, ig = _sigmoid(za), _sigmoid(zx)
    sp = _softplus(-lam_ref[...])
    la = -RG_C * r * sp
    a = jnp.exp(la)
    mult = jnp.sqrt(_neg_expm1(2.0 * la))
    return xb, r, ig, sp, a, mult


def _rg_fwd(xg2, conv_w, conv_b, w_a, b_a, w_x, b_x, lam, ts=256, side=None):
    _, s, c = xg2.shape
    ts = min(ts, s)
    kw = RG_CONV_W
    hb = ts // SUBLANES

    def body(xg_ref, halo_ref, cw_ref, cb_ref, wa_ref, ba_ref, wx_ref, bx_ref, lam_ref, xr_ref, hs_ref, y_ref, carry_ref):
        i = pl.program_id(0)

        @pl.when(i == 0)
        def _():
            carry_ref[...] = jnp.zeros_like(carry_ref)

        xp = xg_ref[0]
        halo = jnp.where(i == 0, 0.0, halo_ref[...])
        xr = cb_ref[...] + cw_ref[kw - 1:kw, :] * xp
        for sft in range(1, kw):
            xr = xr + cw_ref[kw - 1 - sft:kw - sft, :] * _shift_down(xp, halo, sft)
        _, r, ig, sp, a, mult = _rg_gates(xr, wa_ref, ba_ref, wx_ref, bx_ref, lam_ref)
        bt = mult * (ig * xr)
        row = _rows(bt.shape)
        bt = bt + jnp.where(row == 0, a * carry_ref[SUBLANES - 1:SUBLANES, :], 0.0)
        hs = _scan_real_fwd(a, bt)
        carry_ref[...] = hs[ts - SUBLANES:, :]
        xr_ref[...] = xr
        hs_ref[...] = hs
        y_ref[...] = (hs * _gelu(xg_ref[1])).astype(BF16)

    full = lambda shape: pl.BlockSpec(shape, lambda i: (0,) * len(shape))
    row_spec = pl.BlockSpec((ts, c), lambda i: (i, 0))
    nt = s // ts
    return _call_with_side(
        body, side, lambda: pl.program_id(0) == 0, lambda: pl.program_id(0) == nt - 1,
        name="rg_fwd", grid=(nt,),
        in_specs=[pl.BlockSpec((2, ts, c), lambda i: (0, i, 0)),
                  pl.BlockSpec((None, SUBLANES, c), lambda i: (0, jnp.maximum(i * hb - 1, 0), 0)),
                  full(conv_w.shape), full(conv_b.shape), full(w_a.shape), full(b_a.shape), full(w_x.shape), full(b_x.shape),
                  full(lam.shape)],
        out_specs=[row_spec, row_spec, row_spec],
        out_shape=[jax.ShapeDtypeStruct((s, c), F32), jax.ShapeDtypeStruct((s, c), F32), jax.ShapeDtypeStruct((s, c), BF16)],
        scratch_shapes=[pltpu.VMEM((SUBLANES, c), F32)], args=(xg2, xg2, conv_w, conv_b, w_a, b_a, w_x, b_x, lam))


def _rg_bwd(dy, xg2, xr, hs, conv_w, w_a, b_a, w_x, b_x, lam, ts=256, side=None):
    _, s, c = xg2.shape
    ts = min(ts, s)
    nt = s // ts
    kw = RG_CONV_W
    hb = ts // SUBLANES
    bw = c // RG_HEADS
    tn_dims = (((0,), (0,)), ((), ()))
    nt_dims = (((1,), (1,)), ((), ()))

    def body(dy_ref, xg_ref, xph_ref, xr_ref, hs_ref, hsh_ref, cw_ref, wa_ref, ba_ref, wx_ref, bx_ref, lam_ref,
             dxg_ref, dcw_ref, dcb_ref, dwa_ref, dba_ref, dwx_ref, dbx_ref, dlam_ref,
             lam_carry, a_carry, dxr_carry, dsp_acc):
        i = pl.program_id(0)
        first_step = i == 0
        time_first = i == nt - 1

        @pl.when(first_step)
        def _():
            lam_carry[...] = jnp.zeros_like(lam_carry)
            a_carry[...] = jnp.ones_like(a_carry)
            dxr_carry[...] = jnp.zeros_like(dxr_carry)
            dsp_acc[...] = jnp.zeros_like(dsp_acc)
            for ref in (dcw_ref, dcb_ref, dwa_ref, dba_ref, dwx_ref, dbx_ref):
                ref[...] = jnp.zeros_like(ref)

        xr = xr_ref[...]
        hs = hs_ref[...]
        gate = xg_ref[1]
        xb, r, ig, sp, a, mult = _rg_gates(xr, wa_ref, ba_ref, wx_ref, bx_ref, lam_ref)
        dyv = dy_ref[...]
        gg, dgg = _gelu_and_grad(gate)
        dhs = dyv * gg
        dxg_ref[1] = (dyv * hs * dgg).astype(BF16)
        row = _rows(xr.shape)
        coef = jnp.where(row == ts - 1, a_carry[0:1, :], pltpu.roll(a, ts - 1, 0))
        dhs = dhs + jnp.where(row == ts - 1, coef * lam_carry[0:1, :], 0.0)
        lmb = _scan_real_rev(coef, dhs)
        lam_carry[...] = lmb[:SUBLANES]
        a_carry[...] = a[:SUBLANES]
        hs_prev = _shift_down(hs, jnp.where(time_first, 0.0, hsh_ref[...]), 1)
        d_a = lmb * hs_prev
        d_m = lmb * (ig * xr)
        d_ig = lmb * mult * xr
        d_xr = lmb * mult * ig
        d_la = a * d_a - (a * a / mult) * d_m
        dsp_acc[...] += jnp.sum(-RG_C * r * d_la, axis=0, keepdims=True)
        d_za = (-RG_C * sp) * d_la * r * (1.0 - r)
        d_zx = d_ig * ig * (1.0 - ig)
        dba_ref[...] += jnp.sum(d_za, axis=0, keepdims=True)
        dbx_ref[...] += jnp.sum(d_zx, axis=0, keepdims=True)
        dzab, dzxb = d_za.astype(BF16), d_zx.astype(BF16)
        back = []
        for h in range(RG_HEADS):
            sl = slice(h * bw, (h + 1) * bw)
            dwa_ref[h] += lax.dot_general(xb[:, sl], dzab[:, sl], tn_dims, preferred_element_type=F32)
            dwx_ref[h] += lax.dot_general(xb[:, sl], dzxb[:, sl], tn_dims, preferred_element_type=F32)
            back.append(lax.dot_general(dzab[:, sl], wa_ref[h], nt_dims, preferred_element_type=F32)
                        + lax.dot_general(dzxb[:, sl], wx_ref[h], nt_dims, preferred_element_type=F32))
        d_xr = d_xr + jnp.concatenate(back, axis=1)
        d_xp = cw_ref[kw - 1:kw, :] * d_xr
        after = dxr_carry[...]
        for sft in range(1, kw):
            d_xp = d_xp + cw_ref[kw - 1 - sft:kw - sft, :] * _shift_up(d_xr, after, sft)
        dxr_carry[...] = d_xr[:SUBLANES]
        dxg_ref[0] = d_xp.astype(BF16)
        xp = xg_ref[0]
        before = jnp.where(time_first, 0.0, xph_ref[...])
        dcb_ref[...] += jnp.sum(d_xr, axis=0, keepdims=True)
        dcw_ref[kw - 1:kw, :] += jnp.sum(d_xr * xp, axis=0, keepdims=True)
        for sft in range(1, kw):
            dcw_ref[kw - 1 - sft:kw - sft, :] += jnp.sum(d_xr * _shift_down(xp, before, sft), axis=0, keepdims=True)
        dlam_ref[...] = dsp_acc[...] * (-_sigmoid(-lam_ref[...]))

    full = lambda shape: pl.BlockSpec(shape, lambda i: (0,) * len(shape))
    rev = lambda i: nt - 1 - i
    row_spec = pl.BlockSpec((ts, c), lambda i: (rev(i), 0))
    halo_idx = lambda i: jnp.maximum(rev(i) * hb - 1, 0)
    vec = (1, c)
    return _call_with_side(
        body, side, lambda: pl.program_id(0) == 0, lambda: pl.program_id(0) == nt - 1,
        name="rg_bwd", grid=(nt,),
        in_specs=[row_spec,
                  pl.BlockSpec((2, ts, c), lambda i: (0, rev(i), 0)),
                  pl.BlockSpec((None, SUBLANES, c), lambda i: (0, halo_idx(i), 0)),
                  row_spec, row_spec,
                  pl.BlockSpec((SUBLANES, c), lambda i: (halo_idx(i), 0)),
                  full(conv_w.shape), full(w_a.shape), full(b_a.shape), full(w_x.shape), full(b_x.shape), full(lam.shape)],
        out_specs=[pl.BlockSpec((2, ts, c), lambda i: (0, rev(i), 0)), full(conv_w.shape), full(vec), full(w_a.shape), full(vec),
                   full(w_x.shape), full(vec), full(vec)],
        out_shape=[jax.ShapeDtypeStruct((2, s, c), BF16), jax.ShapeDtypeStruct(conv_w.shape, F32), jax.ShapeDtypeStruct(vec, F32),
                   jax.ShapeDtypeStruct(w_a.shape, F32), jax.ShapeDtypeStruct(vec, F32), jax.ShapeDtypeStruct(w_x.shape, F32),
                   jax.ShapeDtypeStruct(vec, F32), jax.ShapeDtypeStruct(vec, F32)],
        scratch_shapes=[pltpu.VMEM((SUBLANES, c), F32), pltpu.VMEM((SUBLANES, c), F32), pltpu.VMEM((SUBLANES, c), F32),
                        pltpu.VMEM(vec, F32)],
        args=(dy, xg2, xg2, xr, hs, hs, conv_w, w_a, b_a, w_x, b_x, lam))


def _s5_param_fn(a_re, a_im, log_dt, bt_re, bt_im):
    dt = jnp.exp(log_dt)
    mag = jnp.exp(a_re * dt)
    abr = mag * jnp.cos(a_im * dt)
    abi = mag * jnp.sin(a_im * dt)
    ur, ui = abr - 1.0, abi
    den = a_re * a_re + a_im * a_im
    wr = (ur * a_re + ui * a_im) / den
    wi = (ui * a_re - ur * a_im) / den
    bbr = wr[None] * bt_re - wi[None] * bt_im
    bbi = wr[None] * bt_im + wi[None] * bt_re
    return abr, abi, bbr, bbi


def _s5_params(a_re, a_im, log_dt, bt_re, bt_im, nlev):
    g, p = a_re.shape
    gc = bt_re.shape[0]

    def body(ar_ref, ai_ref, dt_ref, br_ref, bi_ref, abr_ref, abi_ref, pr_ref, pi_ref, bbr_ref, bbi_ref):
        abr, abi, bbr, bbi = _s5_param_fn(ar_ref[...], ai_ref[...], dt_ref[...], br_ref[...], bi_ref[...])
        abr_ref[...] = abr
        abi_ref[...] = abi
        bbr_ref[...] = bbr
        bbi_ref[...] = bbi
        qr, qi = abr, abi
        for k in range(nlev):
            pr_ref[k] = qr
            pi_ref[k] = qi
            qr, qi = qr * qr - qi * qi, 2.0 * qr * qi

    sd = jax.ShapeDtypeStruct
    return pl.pallas_call(
        body, name="s5_params",
        out_shape=[sd((g, p), F32), sd((g, p), F32), sd((nlev, g, p), F32), sd((nlev, g, p), F32), sd((gc, g, p), F32),
                   sd((gc, g, p), F32)],
    )(a_re, a_im, log_dt, bt_re, bt_im)


def _s5_params_bwd(a_re, a_im, log_dt, bt_re, bt_im, d_abr, d_abi, d_bbr, d_bbi):
    def body(ar_ref, ai_ref, dt_ref, br_ref, bi_ref, g0, g1, g2, g3, o0, o1, o2, o3, o4):
        _, vjp = jax.vjp(_s5_param_fn, ar_ref[...], ai_ref[...], dt_ref[...], br_ref[...], bi_ref[...])
        outs = vjp((g0[...], g1[...], g2[...], g3[...]))
        for o, v in zip((o0, o1, o2, o3, o4), outs):
            o[...] = v

    sd = jax.ShapeDtypeStruct
    return pl.pallas_call(
        body, name="s5_params_bwd",
        out_shape=[sd(a_re.shape, F32), sd(a_im.shape, F32), sd(log_dt.shape, F32), sd(bt_re.shape, F32), sd(bt_im.shape, F32)],
    )(a_re, a_im, log_dt, bt_re, bt_im, d_abr, d_abi, d_bbr, d_bbi)


def _s5_fwd(u, abr, abi, pw_r, pw_i, bp_r, bp_i, cp_r, cp_i, dvec, ts=128, side=None):
    s, c = u.shape
    n = abr.shape[1]
    nblk, cb, nb = bp_r.shape
    ts = min(ts, s)

    def body(u_ref, ar_ref, ai_ref, pr_ref, pi_ref, bpr_ref, bpi_ref, cpr_ref, cpi_ref, d_ref,
             hr_ref, hi_ref, yp_ref, gy_ref, car_r, car_i):
        i = pl.program_id(0)

        @pl.when(i == 0)
        def _():
            car_r[...] = jnp.zeros_like(car_r)
            car_i[...] = jnp.zeros_like(car_i)

        uv = u_ref[...]
        ub = uv.astype(BF16)
        br = jnp.concatenate([jnp.dot(ub[:, k * cb:(k + 1) * cb], bpr_ref[k], preferred_element_type=F32) for k in range(nblk)], axis=1)
        bi = jnp.concatenate([jnp.dot(ub[:, k * cb:(k + 1) * cb], bpi_ref[k], preferred_element_type=F32) for k in range(nblk)], axis=1)
        ar, ai = ar_ref[...], ai_ref[...]
        pr, pi_ = car_r[SUBLANES - 1:SUBLANES, :], car_i[SUBLANES - 1:SUBLANES, :]
        row = _rows(br.shape)
        br = br + jnp.where(row == 0, ar * pr - ai * pi_, 0.0)
        bi = bi + jnp.where(row == 0, ar * pi_ + ai * pr, 0.0)
        hr, hi = _scan_cplx(br, bi, pr_ref, pi_ref, reverse=False)
        car_r[...] = hr[ts - SUBLANES:]
        car_i[...] = hi[ts - SUBLANES:]
        hr_ref[...] = hr
        hi_ref[...] = hi
        hrb, hib = hr.astype(BF16), hi.astype(BF16)
        y = jnp.concatenate([jnp.dot(hrb[:, k * nb:(k + 1) * nb], cpr_ref[k], preferred_element_type=F32)
                             - jnp.dot(hib[:, k * nb:(k + 1) * nb], cpi_ref[k], preferred_element_type=F32) for k in range(nblk)], axis=1)
        yp = y + d_ref[...] * uv
        yp_ref[...] = yp
        gy_ref[...] = _gelu(yp).astype(BF16)

    full = lambda shape: pl.BlockSpec(shape, lambda i: (0,) * len(shape))
    rc = pl.BlockSpec((ts, c), lambda i: (i, 0))
    rn = pl.BlockSpec((ts, n), lambda i: (i, 0))
    sd = jax.ShapeDtypeStruct
    nt = s // ts
    return _call_with_side(
        body, side, lambda: pl.program_id(0) == 0, lambda: pl.program_id(0) == nt - 1,
        name="s5_fwd", grid=(nt,),
        in_specs=[rc, full(abr.shape), full(abi.shape), full(pw_r.shape), full(pw_i.shape), full(bp_r.shape), full(bp_i.shape),
                  full(cp_r.shape), full(cp_i.shape), full(dvec.shape)],
        out_specs=[rn, rn, rc, rc],
        out_shape=[sd((s, n), F32), sd((s, n), F32), sd((s, c), F32), sd((s, c), BF16)],
        scratch_shapes=[pltpu.VMEM((SUBLANES, n), F32), pltpu.VMEM((SUBLANES, n), F32)],
        args=(u, abr, abi, pw_r, pw_i, bp_r, bp_i, cp_r, cp_i, dvec))


def _s5_bwd(dgy, ypre, u, hr, hi, abr, abi, pw_r, pw_i, bp_r, bp_i, cp_r, cp_i, dvec, ts=128, side=None):
    s, c = u.shape
    n = abr.shape[1]
    nblk, cb, nb = bp_r.shape
    ts = min(ts, s)
    nt = s // ts
    hb = ts // SUBLANES
    tn_dims = (((0,), (0,)), ((), ()))
    nt_dims = (((1,), (1,)), ((), ()))

    def body(dgy_ref, yp_ref, u_ref, hr_ref, hi_ref, hrh_ref, hih_ref, ar_ref, ai_ref, pr_ref, pi_ref, bpr_ref, bpi_ref,
             cpr_ref, cpi_ref, d_ref,
             du_ref, dar_ref, dai_ref, dbr_ref, dbi_ref, dcr_ref, dci_ref, dd_ref, car_r, car_i, npi_ref):
        i = pl.program_id(0)
        time_first = i == nt - 1

        @pl.when(i == 0)
        def _():
            car_r[...] = jnp.zeros_like(car_r)
            car_i[...] = jnp.zeros_like(car_i)
            npi_ref[...] = -pi_ref[...]
            for ref in (dar_ref, dai_ref, dbr_ref, dbi_ref, dcr_ref, dci_ref, dd_ref):
                ref[...] = jnp.zeros_like(ref)

        uv = u_ref[...]
        _, dgel = _gelu_and_grad(yp_ref[...])
        dyv = dgy_ref[...] * dgel
        dd_ref[...] += jnp.sum(dyv * uv, axis=0, keepdims=True)
        dyb = dyv.astype(BF16)
        hr, hi = hr_ref[...], hi_ref[...]
        hrb, hib = hr.astype(BF16), hi.astype(BF16)
        dhr, dhi = [], []
        for k in range(nblk):
            dblk = dyb[:, k * cb:(k + 1) * cb]
            dhr.append(lax.dot_general(dblk, cpr_ref[k], nt_dims, preferred_element_type=F32))
            dhi.append(-lax.dot_general(dblk, cpi_ref[k], nt_dims, preferred_element_type=F32))
            dcr_ref[k] += lax.dot_general(hrb[:, k * nb:(k + 1) * nb], dblk, tn_dims, preferred_element_type=F32)
            dci_ref[k] += lax.dot_general(hib[:, k * nb:(k + 1) * nb], dblk, tn_dims, preferred_element_type=F32)
        dhr = jnp.concatenate(dhr, axis=1)
        dhi = jnp.concatenate(dhi, axis=1)
        ar, ai = ar_ref[...], ai_ref[...]
        nr, ni = car_r[0:1, :], car_i[0:1, :]
        row = _rows(dhr.shape)
        dhr = dhr + jnp.where(row == ts - 1, ar * nr + ai * ni, 0.0)
        dhi = dhi + jnp.where(row == ts - 1, ar * ni - ai * nr, 0.0)
        lr, li = _scan_cplx(dhr, dhi, pr_ref, npi_ref, reverse=True)
        car_r[...] = lr[:SUBLANES]
        car_i[...] = li[:SUBLANES]
        hpr = _shift_down(hr, jnp.where(time_first, 0.0, hrh_ref[...]), 1)
        hpi = _shift_down(hi, jnp.where(time_first, 0.0, hih_ref[...]), 1)
        dar_ref[...] += jnp.sum(lr * hpr + li * hpi, axis=0, keepdims=True)
        dai_ref[...] += jnp.sum(li * hpr - lr * hpi, axis=0, keepdims=True)
        lrb, lib = lr.astype(BF16), li.astype(BF16)
        ub = uv.astype(BF16)
        du = []
        for k in range(nblk):
            ublk = ub[:, k * cb:(k + 1) * cb]
            lrk, lik = lrb[:, k * nb:(k + 1) * nb], lib[:, k * nb:(k + 1) * nb]
            dbr_ref[k] += lax.dot_general(ublk, lrk, tn_dims, preferred_element_type=F32)
            dbi_ref[k] += lax.dot_general(ublk, lik, tn_dims, preferred_element_type=F32)
            du.append(lax.dot_general(lrk, bpr_ref[k], nt_dims, preferred_element_type=F32)
                      + lax.dot_general(lik, bpi_ref[k], nt_dims, preferred_element_type=F32))
        du_ref[...] = (d_ref[...] * dyv + jnp.concatenate(du, axis=1)).astype(BF16)

    full = lambda shape: pl.BlockSpec(shape, lambda i: (0,) * len(shape))
    rev = lambda i: nt - 1 - i
    halo_idx = lambda i: jnp.maximum(rev(i) * hb - 1, 0)
    rc = pl.BlockSpec((ts, c), lambda i: (rev(i), 0))
    rn = pl.BlockSpec((ts, n), lambda i: (rev(i), 0))
    hn = pl.BlockSpec((SUBLANES, n), lambda i: (halo_idx(i), 0))
    sd = jax.ShapeDtypeStruct
    return _call_with_side(
        body, side, lambda: pl.program_id(0) == 0, lambda: pl.program_id(0) == nt - 1,
        name="s5_bwd", grid=(nt,),
        in_specs=[rc, rc, rc, rn, rn, hn, hn, full(abr.shape), full(abi.shape), full(pw_r.shape), full(pw_i.shape),
                  full(bp_r.shape), full(bp_i.shape), full(cp_r.shape), full(cp_i.shape), full(dvec.shape)],
        out_specs=[rc, full(abr.shape), full(abi.shape), full(bp_r.shape), full(bp_i.shape), full(cp_r.shape), full(cp_i.shape),
                   full(dvec.shape)],
        out_shape=[sd((s, c), BF16), sd(abr.shape, F32), sd(abi.shape, F32), sd(bp_r.shape, F32), sd(bp_i.shape, F32),
                   sd(cp_r.shape, F32), sd(cp_i.shape, F32), sd(dvec.shape, F32)],
        scratch_shapes=[pltpu.VMEM((SUBLANES, n), F32), pltpu.VMEM((SUBLANES, n), F32), pltpu.VMEM(pw_i.shape, F32)],
        args=(dgy, ypre, u, hr, hi, hr, hi, abr, abi, pw_r, pw_i, bp_r, bp_i, cp_r, cp_i, dvec))


S5_LANE_CHUNK = 512


def _s5_tables(a_re, a_im, log_dt, bt_re, bt_im):
    g, p = a_re.shape
    gc = bt_re.shape[0]

    def body(ar_ref, ai_ref, dt_ref, br_ref, bi_ref, abr_ref, abi_ref, tr_ref, ti_ref, bbr_ref, bbi_ref):
        abr, abi, bbr, bbi = _s5_param_fn(ar_ref[...], ai_ref[...], dt_ref[...], br_ref[...], bi_ref[...])
        abr_ref[...] = abr
        abi_ref[...] = abi
        bbr_ref[...] = bbr
        bbi_ref[...] = bbi
        pows = [(abr, abi)]
        for _ in range(1, SUBLANES):
            qr, qi = pows[-1]
            pows.append((qr * abr - qi * abi, qr * abi + qi * abr))
        zero = jnp.zeros_like(abr)
        for r in range(SUBLANES):
            for k in range(3):
                sh = 1 << k
                tr_ref[k, r] = pows[sh - 1][0] if r >= sh else zero
                ti_ref[k, r] = pows[sh - 1][1] if r >= sh else zero
            tr_ref[3, r] = pows[r][0]
            ti_ref[3, r] = pows[r][1]

    sd = jax.ShapeDtypeStruct
    return pl.pallas_call(
        body, name="s5_tables",
        out_shape=[sd((g, p), F32), sd((g, p), F32), sd((4, SUBLANES, g, p), F32), sd((4, SUBLANES, g, p), F32),
                   sd((gc, g, p), F32), sd((gc, g, p), F32)],
    )(a_re, a_im, log_dt, bt_re, bt_im)


def _cmul_add(br, bi, tr, ti, sr, si):
    return br + tr * sr - ti * si, bi + tr * si + ti * sr


def _s5_fwd2(u, tab_r, tab_i, bp_r, bp_i, cp_r, cp_i, dvec, ts=256, side=None):
    s, c = u.shape
    n = tab_r.shape[2]
    nblk, cb, nb = bp_r.shape
    ts = min(ts, s)
    nsl = ts // SUBLANES
    lc = min(S5_LANE_CHUNK, n)

    def body(u_ref, tr_ref, ti_ref, bpr_ref, bpi_ref, cpr_ref, cpi_ref, d_ref, hr_ref, hi_ref, yp_ref, gy_ref,
             bur_ref, bui_ref, car_r, car_i):
        i = pl.program_id(0)

        @pl.when(i == 0)
        def _():
            car_r[...] = jnp.zeros_like(car_r)
            car_i[...] = jnp.zeros_like(car_i)

        uv = u_ref[...]
        ub = uv.astype(BF16)
        for k in range(nblk):
            bur_ref[:, k * nb:(k + 1) * nb] = jnp.dot(ub[:, k * cb:(k + 1) * cb], bpr_ref[k], preferred_element_type=F32)
            bui_ref[:, k * nb:(k + 1) * nb] = jnp.dot(ub[:, k * cb:(k + 1) * cb], bpi_ref[k], preferred_element_type=F32)
        for q in range(n // lc):
            sl = slice(q * lc, (q + 1) * lc)
            tabs = [(tr_ref[k, :, sl], ti_ref[k, :, sl]) for k in range(4)]

            def slab(j, carry, sl=sl, tabs=tabs):
                cr, ci = carry
                r0 = pl.multiple_of(j * SUBLANES, SUBLANES)
                br, bi = bur_ref[pl.ds(r0, SUBLANES), sl], bui_ref[pl.ds(r0, SUBLANES), sl]
                for k in range(3):
                    sh = 1 << k
                    br, bi = _cmul_add(br, bi, tabs[k][0], tabs[k][1], pltpu.roll(br, sh, 0), pltpu.roll(bi, sh, 0))
                hr, hi = _cmul_add(br, bi, tabs[3][0], tabs[3][1], jnp.broadcast_to(cr, br.shape), jnp.broadcast_to(ci, bi.shape))
                hr_ref[pl.ds(r0, SUBLANES), sl] = hr
                hi_ref[pl.ds(r0, SUBLANES), sl] = hi
                return hr[SUBLANES - 1:, :], hi[SUBLANES - 1:, :]

            cr, ci = lax.fori_loop(0, nsl, slab, (car_r[:, sl], car_i[:, sl]), unroll=2)
            car_r[:, sl] = cr
            car_i[:, sl] = ci
        hrb, hib = hr_ref[...].astype(BF16), hi_ref[...].astype(BF16)
        y = jnp.concatenate([jnp.dot(hrb[:, k * nb:(k + 1) * nb], cpr_ref[k], preferred_element_type=F32)
                             - jnp.dot(hib[:, k * nb:(k + 1) * nb], cpi_ref[k], preferred_element_type=F32) for k in range(nblk)], axis=1)
        yp = y + d_ref[...] * uv
        yp_ref[...] = yp
        gy_ref[...] = _gelu(yp).astype(BF16)

    full = lambda shape: pl.BlockSpec(shape, lambda i: (0,) * len(shape))
    rc = pl.BlockSpec((ts, c), lambda i: (i, 0))
    rn = pl.BlockSpec((ts, n), lambda i: (i, 0))
    sd = jax.ShapeDtypeStruct
    nt = s // ts
    return _call_with_side(
        body, side, lambda: pl.program_id(0) == 0, lambda: pl.program_id(0) == nt - 1,
        name="s5_fwd", grid=(nt,),
        in_specs=[rc, full(tab_r.shape), full(tab_i.shape), full(bp_r.shape), full(bp_i.shape), full(cp_r.shape), full(cp_i.shape),
                  full(dvec.shape)],
        out_specs=[rn, rn, rc, rc],
        out_shape=[sd((s, n), F32), sd((s, n), F32), sd((s, c), F32), sd((s, c), BF16)],
        scratch_shapes=[pltpu.VMEM((ts, n), F32), pltpu.VMEM((ts, n), F32), pltpu.VMEM((1, n), F32), pltpu.VMEM((1, n), F32)],
        args=(u, tab_r, tab_i, bp_r, bp_i, cp_r, cp_i, dvec))


def _s5_bwd2(dgy, ypre, u, hr, hi, rtab_r, rtab_i, bp_r, bp_i, cp_r, cp_i, dvec, ts=256, side=None):
    s, c = u.shape
    n = rtab_r.shape[2]
    nblk, cb, nb = bp_r.shape
    ts = min(ts, s)
    nt = s // ts
    hb = ts // SUBLANES
    nsl = ts // SUBLANES
    lc = min(S5_LANE_CHUNK, n)
    tn_dims = (((0,), (0,)), ((), ()))
    nt_dims = (((1,), (1,)), ((), ()))

    def body(dgy_ref, yp_ref, u_ref, hr_ref, hi_ref, hrh_ref, hih_ref, tr_ref, ti_ref, bpr_ref, bpi_ref, cpr_ref, cpi_ref, d_ref,
             du_ref, dar_ref, dai_ref, dbr_ref, dbi_ref, dcr_ref, dci_ref, dd_ref, lr_ref, li_ref, car_r, car_i):
        i = pl.program_id(0)
        time_first = i == nt - 1

        @pl.when(i == 0)
        def _():
            car_r[...] = jnp.zeros_like(car_r)
            car_i[...] = jnp.zeros_like(car_i)
            for ref in (dar_ref, dai_ref, dbr_ref, dbi_ref, dcr_ref, dci_ref, dd_ref):
                ref[...] = jnp.zeros_like(ref)

        uv = u_ref[...]
        _, dgel = _gelu_and_grad(yp_ref[...])
        dyv = dgy_ref[...] * dgel
        dd_ref[...] += jnp.sum(dyv * uv, axis=0, keepdims=True)
        dyb = dyv.astype(BF16)
        hrb, hib = hr_ref[...].astype(BF16), hi_ref[...].astype(BF16)
        for k in range(nblk):
            dblk = dyb[:, k * cb:(k + 1) * cb]
            lr_ref[:, k * nb:(k + 1) * nb] = lax.dot_general(dblk, cpr_ref[k], nt_dims, preferred_element_type=F32)
            li_ref[:, k * nb:(k + 1) * nb] = -lax.dot_general(dblk, cpi_ref[k], nt_dims, preferred_element_type=F32)
            dcr_ref[k] += lax.dot_general(hrb[:, k * nb:(k + 1) * nb], dblk, tn_dims, preferred_element_type=F32)
            dci_ref[k] += lax.dot_general(hib[:, k * nb:(k + 1) * nb], dblk, tn_dims, preferred_element_type=F32)
        row8 = _rows((SUBLANES, lc))
        for q in range(n // lc):
            sl = slice(q * lc, (q + 1) * lc)
            tabs = [(tr_ref[k, :, sl], ti_ref[k, :, sl]) for k in range(4)]
            halo_r = jnp.where(time_first, 0.0, hrh_ref[SUBLANES - 1:, sl])
            halo_i = jnp.where(time_first, 0.0, hih_ref[SUBLANES - 1:, sl])

            def slab(jj, carry, sl=sl, tabs=tabs, halo_r=halo_r, halo_i=halo_i):
                nr, ni, acc_r, acc_i = carry
                j = nsl - 1 - jj
                r0 = pl.multiple_of(j * SUBLANES, SUBLANES)
                br, bi = lr_ref[pl.ds(r0, SUBLANES), sl], li_ref[pl.ds(r0, SUBLANES), sl]
                for k in range(3):
                    sh = 1 << k
                    br, bi = _cmul_add(br, bi, tabs[k][0], tabs[k][1], pltpu.roll(br, SUBLANES - sh, 0),
                                       pltpu.roll(bi, SUBLANES - sh, 0))
                lr, li = _cmul_add(br, bi, tabs[3][0], tabs[3][1], jnp.broadcast_to(nr, br.shape), jnp.broadcast_to(ni, bi.shape))
                lr_ref[pl.ds(r0, SUBLANES), sl] = lr
                li_ref[pl.ds(r0, SUBLANES), sl] = li
                p0 = pl.multiple_of(jnp.maximum(j - 1, 0) * SUBLANES, SUBLANES)
                prev_r = jnp.where(j == 0, halo_r, hr_ref[pl.ds(p0, SUBLANES), sl][SUBLANES - 1:, :])
                prev_i = jnp.where(j == 0, halo_i, hi_ref[pl.ds(p0, SUBLANES), sl][SUBLANES - 1:, :])
                hpr = jnp.where(row8 == 0, jnp.broadcast_to(prev_r, br.shape), pltpu.roll(hr_ref[pl.ds(r0, SUBLANES), sl], 1, 0))
                hpi = jnp.where(row8 == 0, jnp.broadcast_to(prev_i, bi.shape), pltpu.roll(hi_ref[pl.ds(r0, SUBLANES), sl], 1, 0))
                return lr[:1, :], li[:1, :], acc_r + (lr * hpr + li * hpi), acc_i + (li * hpr - lr * hpi)

            zero = jnp.zeros((SUBLANES, lc), F32)
            nr, ni, acc_r, acc_i = lax.fori_loop(0, nsl, slab, (car_r[:, sl], car_i[:, sl], zero, zero), unroll=2)
            car_r[:, sl] = nr
            car_i[:, sl] = ni
            dar_ref[:, sl] += jnp.sum(acc_r, axis=0, keepdims=True)
            dai_ref[:, sl] += jnp.sum(acc_i, axis=0, keepdims=True)
        lrb, lib = lr_ref[...].astype(BF16), li_ref[...].astype(BF16)
        ub = uv.astype(BF16)
        du = []
        for k in range(nblk):
            ublk = ub[:, k * cb:(k + 1) * cb]
            lrk, lik = lrb[:, k * nb:(k + 1) * nb], lib[:, k * nb:(k + 1) * nb]
            dbr_ref[k] += lax.dot_general(ublk, lrk, tn_dims, preferred_element_type=F32)
            dbi_ref[k] += lax.dot_general(ublk, lik, tn_dims, preferred_element_type=F32)
            du.append(lax.dot_general(lrk, bpr_ref[k], nt_dims, preferred_element_type=F32)
                      + lax.dot_general(lik, bpi_ref[k], nt_dims, preferred_element_type=F32))
        du_ref[...] = (d_ref[...] * dyv + jnp.concatenate(du, axis=1)).astype(BF16)

    full = lambda shape: pl.BlockSpec(shape, lambda i: (0,) * len(shape))
    rev = lambda i: nt - 1 - i
    halo_idx = lambda i: jnp.maximum(rev(i) * hb - 1, 0)
    rc = pl.BlockSpec((ts, c), lambda i: (rev(i), 0))
    rn = pl.BlockSpec((ts, n), lambda i: (rev(i), 0))
    hn = pl.BlockSpec((SUBLANES, n), lambda i: (halo_idx(i), 0))
    sd = jax.ShapeDtypeStruct
    vec_n = (1, n)
    return _call_with_side(
        body, side, lambda: pl.program_id(0) == 0, lambda: pl.program_id(0) == nt - 1,
        name="s5_bwd", grid=(nt,),
        in_specs=[rc, rc, rc, rn, rn, hn, hn, full(rtab_r.shape), full(rtab_i.shape),
                  full(bp_r.shape), full(bp_i.shape), full(cp_r.shape), full(cp_i.shape), full(dvec.shape)],
        out_specs=[rc, full(vec_n), full(vec_n), full(bp_r.shape), full(bp_i.shape), full(cp_r.shape), full(cp_i.shape),
                   full(dvec.shape)],
        out_shape=[sd((s, c), BF16), sd(vec_n, F32), sd(vec_n, F32), sd(bp_r.shape, F32), sd(bp_i.shape, F32),
                   sd(cp_r.shape, F32), sd(cp_i.shape, F32), sd(dvec.shape, F32)],
        scratch_shapes=[pltpu.VMEM((ts, n), F32), pltpu.VMEM((ts, n), F32), pltpu.VMEM((1, n), F32), pltpu.VMEM((1, n), F32)],
        args=(dgy, ypre, u, hr, hi, hr, hi, rtab_r, rtab_i, bp_r, bp_i, cp_r, cp_i, dvec))


def _glu(gl2, ts=512):
    _, s, c = gl2.shape
    ts = min(ts, s)

    def body(g_ref, o_ref):
        o_ref[...] = (g_ref[0] * _sigmoid(g_ref[1])).astype(BF16)

    return pl.pallas_call(
        body, name="glu", grid=(s // ts,), in_specs=[pl.BlockSpec((2, ts, c), lambda i: (0, i, 0))],
        out_specs=pl.BlockSpec((ts, c), lambda i: (i, 0)), out_shape=jax.ShapeDtypeStruct((s, c), BF16), compiler_params=_cparams(),
    )(gl2)


def _glu_bwd(gl2, d_o, ts=512):
    _, s, c = gl2.shape
    ts = min(ts, s)

    def body(g_ref, do_ref, o_ref):
        sg = _sigmoid(g_ref[1])
        dov = do_ref[...]
        o_ref[0] = (dov * sg).astype(BF16)
        o_ref[1] = (dov * g_ref[0] * sg * (1.0 - sg)).astype(BF16)

    blk = pl.BlockSpec((2, ts, c), lambda i: (0, i, 0))
    return pl.pallas_call(
        body, name="glu_bwd", grid=(s // ts,), in_specs=[blk, pl.BlockSpec((ts, c), lambda i: (i, 0))],
        out_specs=blk, out_shape=jax.ShapeDtypeStruct((2, s, c), BF16), compiler_params=_cparams(),
    )(gl2, d_o)


PACK_ROW_MULTIPLE = 1024
ELEMENTWISE_BLOCK_ELEMS = 256 * 1024


def _row_tile(rows, cols):
    pref = max(SUBLANES, 1 << int(math.log2(max(1, ELEMENTWISE_BLOCK_ELEMS // cols))))
    if rows <= pref:
        return rows
    t = pref
    while rows % t:
        t //= 2
    assert t >= SUBLANES, rows
    return t


def _sum_parts(rs):
    nl = len(rs)
    p, rows, cols = rs[0].shape
    tr = _row_tile(rows, cols)

    def body(*refs):
        o_ref = refs[nl]
        for l in range(nl):
            acc = refs[l][0].astype(F32)
            for k in range(1, p):
                acc = acc + refs[l][k].astype(F32)
            o_ref[l] = acc

    return pl.pallas_call(
        body, name="sum_parts", grid=(rows // tr,), in_specs=[pl.BlockSpec((p, tr, cols), lambda i: (0, i, 0))] * nl,
        out_specs=pl.BlockSpec((nl, tr, cols), lambda i: (0, i, 0)), out_shape=jax.ShapeDtypeStruct((nl, rows, cols), F32),
        compiler_params=_cparams(),
    )(*rs)


def _adamw(w, g_parts, m, v):
    rows, cols = w.shape
    tr = _row_tile(rows, cols)
    ng = len(g_parts)
    c1 = 1.0 / (1.0 - ADAM_B1 ** ADAM_STEP)
    c2 = 1.0 / (1.0 - ADAM_B2 ** ADAM_STEP)

    def body(*refs):
        w_ref, m_ref, v_ref = refs[0], refs[1 + ng], refs[2 + ng]
        g_ref, dl_ref, nm_ref, nv_ref = refs[3 + ng:]
        g = refs[1][...]
        for k in range(1, ng):
            g = g + refs[1 + k][...]
        mn = ADAM_B1 * m_ref[...] + (1.0 - ADAM_B1) * g
        vn = ADAM_B2 * v_ref[...] + (1.0 - ADAM_B2) * (g * g)
        g_ref[...] = g
        nm_ref[...] = mn
        nv_ref[...] = vn
        dl_ref[...] = -ADAM_LR * ((mn * c1) / (jnp.sqrt(vn * c2) + ADAM_EPS) + ADAM_WD * w_ref[...])

    blk = pl.BlockSpec((tr, cols), lambda i: (i, 0))
    sd = jax.ShapeDtypeStruct((rows, cols), F32)
    return pl.pallas_call(
        body, name="adamw", grid=(rows // tr,), in_specs=[blk] * (3 + ng), out_specs=[blk] * 4, out_shape=[sd] * 4,
        compiler_params=_cparams(),
    )(w, *g_parts, m, v)


def _place():
    x, y, c = lax.axis_index("x"), lax.axis_index("y"), lax.axis_index("c")
    chips = [(1 - x, y), (x, 1 - y), (1 - x, 1 - y)]
    return x, y, c, chips


class Side:
    def __init__(self, ins, outs, kind):
        self.ins, self.outs, self.kind = list(ins), list(outs), kind
        n = len(self.ins)
        self.sems = [pltpu.SemaphoreType.DMA((3 * n,)), pltpu.SemaphoreType.DMA((3 * n,)), pltpu.SemaphoreType.DMA((n,))]

    def _copies(self, ins, outs, send, recv, lsem):
        x, y, c, chips = _place()
        me = 2 * x + y
        local, out_going, in_coming = [], [], []
        for t in range(len(ins)):
            if self.kind == 'gather':
                src_local, srcs, dst_mine = ins[t], [ins[t]] * 3, outs[t].at[me]
            else:
                src_local, srcs, dst_mine = ins[t].at[me], [ins[t].at[2 * px + py] for px, py in chips], outs[t].at[me]
            local.append(pltpu.make_async_copy(src_local, dst_mine, lsem.at[t]))
            for r, (px, py) in enumerate(chips):
                out_going.append(pltpu.make_async_remote_copy(
                    src_ref=srcs[r], dst_ref=dst_mine, send_sem=send.at[3 * t + r], recv_sem=recv.at[3 * t + r],
                    device_id=(px, py, c), device_id_type=MESH))
                in_coming.append(pltpu.make_async_remote_copy(
                    src_ref=srcs[r], dst_ref=outs[t].at[2 * px + py], send_sem=send.at[3 * t + r], recv_sem=recv.at[3 * t + r],
                    device_id=(px, py, c), device_id_type=MESH))
        return local, out_going, in_coming

    def start(self, ins, outs, send, recv, lsem):
        local, out_going, _ = self._copies(ins, outs, send, recv, lsem)
        for cp in local + out_going:
            cp.start()

    def wait(self, ins, outs, send, recv, lsem):
        local, out_going, in_coming = self._copies(ins, outs, send, recv, lsem)
        for cp in in_coming:
            cp.wait_recv()
        for cp in out_going:
            cp.wait_send()
        for cp in local:
            cp.wait()


def _gather_side(shards):
    return Side(shards, [jax.ShapeDtypeStruct((N_CHIPS,) + s.shape, s.dtype) for s in shards], 'gather')


def _scatter_side(grads):
    return Side(grads, [jax.ShapeDtypeStruct(g.shape, g.dtype) for g in grads], 'scatter')


def _call_with_side(body, side, first, last, *, name, grid, in_specs, out_specs, out_shape, scratch_shapes, args):
    if side is None:
        outs = pl.pallas_call(body, name=name, grid=grid, in_specs=in_specs, out_specs=out_specs, out_shape=out_shape,
                              scratch_shapes=scratch_shapes, compiler_params=_cparams())(*args)
        return outs, []
    n_in, n_out, n_sc = len(in_specs), len(out_specs), len(scratch_shapes)
    ns_in, ns_out = len(side.ins), len(side.outs)

    def wrapped(*refs):
        base_in, s_in = refs[:n_in], refs[n_in:n_in + ns_in]
        o0 = n_in + ns_in
        base_out, s_out = refs[o0:o0 + n_out], refs[o0 + n_out:o0 + n_out + ns_out]
        sc0 = o0 + n_out + ns_out
        base_sc, sems = refs[sc0:sc0 + n_sc], refs[sc0 + n_sc:]

        @pl.when(first())
        def _():
            side.start(s_in, s_out, *sems)

        body(*base_in, *base_out, *base_sc)

        @pl.when(last())
        def _():
            side.wait(s_in, s_out, *sems)

    any_spec = pl.BlockSpec(memory_space=pl.ANY)
    outs = pl.pallas_call(
        wrapped, name=name, grid=grid, in_specs=list(in_specs) + [any_spec] * ns_in, out_specs=list(out_specs) + [any_spec] * ns_out,
        out_shape=list(out_shape) + side.outs, scratch_shapes=list(scratch_shapes) + side.sems, compiler_params=_cparams(),
    )(*args, *side.ins)
    return outs[:n_out], outs[n_out:]


def _run_side(name, side):
    def body(*refs):
        n = len(side.ins)
        side.start(refs[:n], refs[n:2 * n], *refs[2 * n:])
        side.wait(refs[:n], refs[n:2 * n], *refs[2 * n:])

    any_spec = pl.BlockSpec(memory_space=pl.ANY)
    return pl.pallas_call(body, name=name, in_specs=[any_spec] * len(side.ins), out_specs=[any_spec] * len(side.outs),
                          out_shape=side.outs, scratch_shapes=side.sems)(*side.ins)


def _gather_shards(shards, layer_major):
    n = len(shards)

    def body(*refs):
        ins, outs = refs[:n], refs[n:2 * n]
        send, recv, lsem = refs[2 * n:]
        x, y, c, chips = _place()
        me = 2 * x + y

        def slot(t, chip):
            return outs[t].at[:, chip] if layer_major[t] else outs[t].at[chip]

        local, sends = [], []
        for t in range(n):
            cp = pltpu.make_async_copy(ins[t], slot(t, me), lsem.at[t])
            cp.start()
            local.append(cp)
            for r, (px, py) in enumerate(chips):
                rc = pltpu.make_async_remote_copy(src_ref=ins[t], dst_ref=slot(t, me), send_sem=send.at[3 * t + r],
                                                  recv_sem=recv.at[3 * t + r], device_id=(px, py, c), device_id_type=MESH)
                rc.start()
                sends.append(rc)
        for t in range(n):
            for r, (px, py) in enumerate(chips):
                pltpu.make_async_remote_copy(src_ref=ins[t], dst_ref=slot(t, 2 * px + py), send_sem=send.at[3 * t + r],
                                             recv_sem=recv.at[3 * t + r], device_id=(px, py, c), device_id_type=MESH).wait_recv()
        for rc in sends:
            rc.wait_send()
        for cp in local:
            cp.wait()

    any_spec = pl.BlockSpec(memory_space=pl.ANY)
    return pl.pallas_call(
        body, name="gather_shards", in_specs=[any_spec] * n, out_specs=[any_spec] * n,
        out_shape=[jax.ShapeDtypeStruct((s.shape[0], N_CHIPS) + s.shape[1:] if lm else (N_CHIPS,) + s.shape, s.dtype)
                   for s, lm in zip(shards, layer_major)],
        scratch_shapes=[pltpu.SemaphoreType.DMA((3 * n,)), pltpu.SemaphoreType.DMA((3 * n,)), pltpu.SemaphoreType.DMA((n,))],
    )(*shards)


def _scatter_grads(groups):
    flat = [(gi, li, a) for gi, grp in enumerate(groups) for li, a in enumerate(grp)]
    n = len(flat)
    ng = len(groups)

    def body(*refs):
        ins, outs = refs[:n], refs[n:n + ng]
        send, recv, lsem = refs[n + ng:]
        x, y, c, chips = _place()
        me = 2 * x + y
        local, sends = [], []
        for t, (gi, li, _) in enumerate(flat):
            cp = pltpu.make_async_copy(ins[t].at[me], outs[gi].at[me, li], lsem.at[t])
            cp.start()
            local.append(cp)
            for r, (px, py) in enumerate(chips):
                rc = pltpu.make_async_remote_copy(src_ref=ins[t].at[2 * px + py], dst_ref=outs[gi].at[me, li],
                                                  send_sem=send.at[3 * t + r], recv_sem=recv.at[3 * t + r],
                                                  device_id=(px, py, c), device_id_type=MESH)
                rc.start()
                sends.append(rc)
        for t, (gi, li, _) in enumerate(flat):
            for r, (px, py) in enumerate(chips):
                pltpu.make_async_remote_copy(src_ref=ins[t].at[me], dst_ref=outs[gi].at[2 * px + py, li],
                                             send_sem=send.at[3 * t + r], recv_sem=recv.at[3 * t + r],
                                             device_id=(px, py, c), device_id_type=MESH).wait_recv()
        for rc in sends:
            rc.wait_send()
        for cp in local:
            cp.wait()

    any_spec = pl.BlockSpec(memory_space=pl.ANY)
    return pl.pallas_call(
        body, name="scatter_grads", in_specs=[any_spec] * n, out_specs=[any_spec] * ng,
        out_shape=[jax.ShapeDtypeStruct((N_CHIPS, len(grp)) + grp[0].shape[1:], grp[0].dtype) for grp in groups],
        scratch_shapes=[pltpu.SemaphoreType.DMA((3 * n,)), pltpu.SemaphoreType.DMA((3 * n,)), pltpu.SemaphoreType.DMA((n,))],
    )(*[a for _, _, a in flat])


def _swap_with_sibling(arrs):
    n = len(arrs)

    def body(*refs):
        ins, outs = refs[:n], refs[n:2 * n]
        send, recv = refs[2 * n:]
        x, y, c, _ = _place()
        cps = []
        for t in range(n):
            rc = pltpu.make_async_remote_copy(src_ref=ins[t], dst_ref=outs[t], send_sem=send.at[t], recv_sem=recv.at[t],
                                              device_id=(x, y, 1 - c), device_id_type=MESH)
            rc.start()
            cps.append(rc)
        for rc in cps:
            rc.wait_recv()
        for rc in cps:
            rc.wait_send()

    any_spec = pl.BlockSpec(memory_space=pl.ANY)
    return pl.pallas_call(
        body, name="swap_with_sibling", in_specs=[any_spec] * n, out_specs=[any_spec] * n,
        out_shape=[jax.ShapeDtypeStruct(a.shape, a.dtype) for a in arrs],
        scratch_shapes=[pltpu.SemaphoreType.DMA((n,)), pltpu.SemaphoreType.DMA((n,))],
    )(*arrs)


def _allreduce_small(v):
    rows, cols = v.shape

    def body(v_ref, o_ref, sib_ref, chip_ref, send, recv):
        x, y, c, chips = _place()
        me = 2 * x + y
        d2d = pltpu.make_async_remote_copy(src_ref=v_ref, dst_ref=sib_ref, send_sem=send.at[0], recv_sem=recv.at[0],
                                           device_id=(x, y, 1 - c), device_id_type=MESH)
        d2d.start()
        d2d.wait_recv()
        chip_ref[me] = v_ref[...] + sib_ref[...]
        sends = []
        for r, (px, py) in enumerate(chips):
            rc = pltpu.make_async_remote_copy(src_ref=chip_ref.at[me], dst_ref=chip_ref.at[me], send_sem=send.at[1 + r],
                                              recv_sem=recv.at[1 + r], device_id=(px, py, c), device_id_type=MESH)
            rc.start()
            sends.append(rc)
        for r, (px, py) in enumerate(chips):
            pltpu.make_async_remote_copy(src_ref=chip_ref.at[me], dst_ref=chip_ref.at[2 * px + py], send_sem=send.at[1 + r],
                                         recv_sem=recv.at[1 + r], device_id=(px, py, c), device_id_type=MESH).wait_recv()
        o_ref[...] = (chip_ref[0] + chip_ref[1]) + (chip_ref[2] + chip_ref[3])
        d2d.wait_send()
        for rc in sends:
            rc.wait_send()

    vm = pl.BlockSpec(memory_space=pltpu.VMEM)
    return pl.pallas_call(
        body, name="allreduce_small", in_specs=[vm], out_specs=vm, out_shape=jax.ShapeDtypeStruct((rows, cols), F32),
        scratch_shapes=[pltpu.VMEM((rows, cols), F32), pltpu.VMEM((N_CHIPS, rows, cols), F32), pltpu.SemaphoreType.DMA((4,)),
                        pltpu.SemaphoreType.DMA((4,))],
        compiler_params=_cparams(),
    )(v)


def _pack(tensors):
    pieces = []
    for t in tensors:
        flat = t.reshape(-1)
        pad = (-flat.shape[0]) % (SUBLANES * LANES)
        pieces.append(jnp.pad(flat, (0, pad)).reshape(-1, LANES))
    rows = sum(p.shape[0] for p in pieces)
    pieces.append(jnp.zeros(((-rows) % PACK_ROW_MULTIPLE, LANES), tensors[0].dtype))
    return jnp.concatenate(pieces, axis=0)


def _unpack(buf, like):
    out, off = [], 0
    for t in like:
        size = math.prod(t.shape)
        rows = -(-size // (SUBLANES * LANES)) * SUBLANES
        out.append(buf[off:off + rows].reshape(-1)[:size].reshape(t.shape))
        off += rows
    return out


def _s5_pack_b(bb):
    gc, g, p = bb.shape
    q = S5_GROUPS_PER_BLOCK
    t = bb.reshape(gc, g // q, q, p).transpose(1, 2, 0, 3)
    eye = jnp.eye(q, dtype=bb.dtype)
    return (t[:, :, :, None, :] * eye[None, :, None, :, None]).reshape(g // q, q * gc, q * p)


def _s5_unpack_b(dbp, gc, p):
    nb = dbp.shape[0]
    q = S5_GROUPS_PER_BLOCK
    eye = jnp.eye(q, dtype=dbp.dtype)
    t = (dbp.reshape(nb, q, gc, q, p) * eye[None, :, None, :, None]).sum(axis=3)
    return t.transpose(2, 0, 1, 3).reshape(gc, nb * q, p)


def _s5_pack_c(cc):
    g, gc, p = cc.shape
    q = S5_GROUPS_PER_BLOCK
    t = cc.reshape(g // q, q, gc, p).transpose(0, 1, 3, 2)
    eye = jnp.eye(q, dtype=cc.dtype)
    return (t[:, :, :, None, :] * eye[None, :, None, :, None]).reshape(g // q, q * p, q * gc)


def _s5_unpack_c(dcp, gc, p):
    nb = dcp.shape[0]
    q = S5_GROUPS_PER_BLOCK
    eye = jnp.eye(q, dtype=dcp.dtype)
    t = (dcp.reshape(nb, q, p, q, gc) * eye[None, :, None, :, None]).sum(axis=3)
    return t.transpose(0, 1, 3, 2).reshape(nb * q, gc, p)


def _split2(m):
    return m.arr[:, 0]


def kernel(x, norm_mix_g, norm_ffn_g, norm_final_g, rg_w_in, rg_conv_w, rg_conv_b, rg_w_a, rg_b_a, rg_w_x, rg_b_x, rg_lambda, rg_w_out, s5_w_in, s5_a_re, s5_a_im, s5_log_dt, s5_b_re, s5_b_im, s5_c_re, s5_c_im, s5_d, s5_w_glu, s5_w_out, ffn_w_up, ffn_conv_w, ffn_conv_b, ffn_w_down, loss_target, m_norm_mix_g, m_norm_ffn_g, m_norm_final_g, m_rg_w_in, m_rg_conv_w, m_rg_conv_b, m_rg_w_a, m_rg_b_a, m_rg_w_x, m_rg_b_x, m_rg_lambda, m_rg_w_out, m_s5_w_in, m_s5_a_re, m_s5_a_im, m_s5_log_dt, m_s5_b_re, m_s5_b_im, m_s5_c_re, m_s5_c_im, m_s5_d, m_s5_w_glu, m_s5_w_out, m_ffn_w_up, m_ffn_conv_w, m_ffn_conv_b, m_ffn_w_down, v_norm_mix_g, v_norm_ffn_g, v_norm_final_g, v_rg_w_in, v_rg_conv_w, v_rg_conv_b, v_rg_w_a, v_rg_b_a, v_rg_w_x, v_rg_b_x, v_rg_lambda, v_rg_w_out, v_s5_w_in, v_s5_a_re, v_s5_a_im, v_s5_log_dt, v_s5_b_re, v_s5_b_im, v_s5_c_re, v_s5_c_im, v_s5_d, v_s5_w_glu, v_s5_w_out, v_ffn_w_up, v_ffn_conv_w, v_ffn_conv_b, v_ffn_w_down):
    w = dict(zip(PARAM_NAMES, (norm_mix_g, norm_ffn_g, norm_final_g, rg_w_in, rg_conv_w, rg_conv_b, rg_w_a, rg_b_a, rg_w_x, rg_b_x,
                               rg_lambda, rg_w_out, s5_w_in, s5_a_re, s5_a_im, s5_log_dt, s5_b_re, s5_b_im, s5_c_re, s5_c_im, s5_d,
                               s5_w_glu, s5_w_out, ffn_w_up, ffn_conv_w, ffn_conv_b, ffn_w_down)))
    mom = dict(zip(PARAM_NAMES, (m_norm_mix_g, m_norm_ffn_g, m_norm_final_g, m_rg_w_in, m_rg_conv_w, m_rg_conv_b, m_rg_w_a, m_rg_b_a,
                                 m_rg_w_x, m_rg_b_x, m_rg_lambda, m_rg_w_out, m_s5_w_in, m_s5_a_re, m_s5_a_im, m_s5_log_dt, m_s5_b_re,
                                 m_s5_b_im, m_s5_c_re, m_s5_c_im, m_s5_d, m_s5_w_glu, m_s5_w_out, m_ffn_w_up, m_ffn_conv_w,
                                 m_ffn_conv_b, m_ffn_w_down)))
    vel = dict(zip(PARAM_NAMES, (v_norm_mix_g, v_norm_ffn_g, v_norm_final_g, v_rg_w_in, v_rg_conv_w, v_rg_conv_b, v_rg_w_a, v_rg_b_a,
                                 v_rg_w_x, v_rg_b_x, v_rg_lambda, v_rg_w_out, v_s5_w_in, v_s5_a_re, v_s5_a_im, v_s5_log_dt, v_s5_b_re,
                                 v_s5_b_im, v_s5_c_re, v_s5_c_im, v_s5_d, v_s5_w_glu, v_s5_w_out, v_ffn_w_up, v_ffn_conv_w,
                                 v_ffn_conv_b, v_ffn_w_down)))
    _, s, d = x.shape
    depth = norm_mix_g.shape[0]
    n_grp, n_state = s5_a_re.shape[1], s5_a_re.shape[2]
    gc = s5_b_re.shape[3]
    d_ff = ffn_w_down.shape[1] * N_CHIPS
    s5_ts = min(256, s)

    wb = {n: (w[n].astype(BF16) if n in BIG else w[n]) for n in SHARDED}
    gath = {}

    def mixer_keys(i):
        return [(n, i // 2) for n in MIXER_SHARDED[i % 2]] if i < depth else []

    def gather_side(keys):
        return _gather_side([wb[n][l] for n, l in keys])

    def put(keys, arrs):
        for k, a in zip(keys, arrs):
            gath[k] = a

    def wcol(n, l):
        return Mat(gath[(n, l)][:, None], 0, 'c')

    def wrow(n, l):
        g = gath[(n, l)]
        return Mat(g.reshape(1, 1, N_CHIPS * g.shape[1], g.shape[2]), 0, 'c')

    def rg_cw(l):
        return gath[('rg_conv_w', l)].transpose(1, 0, 2).reshape(RG_CONV_W, d)

    def s5_dv(l):
        return gath[('s5_d', l)].reshape(1, d)

    def f_cw(l):
        return gath[('ffn_conv_w', l)].transpose(1, 0, 2).reshape(FFN_CONV_W, 2, d_ff).transpose(1, 0, 2)

    tm = min(1024, s)
    d_up = 2 * d_ff // N_CHIPS
    f_cb = ffn_conv_b.reshape(depth, 2, 1, d_ff)
    put(mixer_keys(0), _run_side("gather_first", gather_side(mixer_keys(0))))

    h = x.reshape(s, d)
    saved = []
    for i in range(depth):
        j = i // 2
        sv = {'h_in': h}
        hn = _rms_fwd(h, norm_mix_g[i:i + 1])
        sv['hn'] = hn
        up_keys = [('ffn_w_up', i), ('ffn_conv_w', i)]
        if i % 2 == 0:
            xg = _mm("rg_in", 'nn', act(hn), wcol('rg_w_in', j), out_parts=2, tm=tm, tn=512, tk=d)
            xg2 = _split2(xg)
            wa, wx = rg_w_a[j].astype(BF16), rg_w_x[j].astype(BF16)
            ba, bx = rg_b_a[j].reshape(1, d), rg_b_x[j].reshape(1, d)
            (xr, hs, y), got = _rg_fwd(xg2, rg_cw(j), rg_conv_b[j:j + 1], wa, ba, wx, bx, rg_lambda[j:j + 1],
                                       side=gather_side(up_keys))
            put(up_keys, got)
            sv.update(xg2=xg2, xr=xr, hs=hs, y=y, wa=wa, wx=wx, ba=ba, bx=bx)
            h = _mm("rg_out", 'nn', act(y), wrow('rg_w_out', j), res=act(h), tm=tm, tn=d, tk=d).arr[0, 0]
        else:
            u = _mm("s5_in", 'nn', act(hn), wrow('s5_w_in', j), tm=tm, tn=d, tk=d).arr[0, 0]
            bt_re, bt_im = s5_b_re[j].transpose(2, 0, 1), s5_b_im[j].transpose(2, 0, 1)
            ldt = s5_log_dt[j].reshape(n_grp, 1)
            _, _, tab_r, tab_i, bbr, bbi = _s5_tables(s5_a_re[j], s5_a_im[j], ldt, bt_re, bt_im)
            nn_ = n_grp * n_state
            tab_r, tab_i = tab_r.reshape(4, SUBLANES, nn_), tab_i.reshape(4, SUBLANES, nn_)
            prm = dict(bp_r=_s5_pack_b(bbr).astype(BF16), bp_i=_s5_pack_b(bbi).astype(BF16),
                       cp_r=_s5_pack_c(s5_c_re[j]).astype(BF16), cp_i=_s5_pack_c(s5_c_im[j]).astype(BF16), dvec=s5_dv(j))
            (hr, hi, ypre, gy), got = _s5_fwd2(u, tab_r, tab_i, ts=s5_ts, side=gather_side(up_keys), **prm)
            sv.update(rtab_r=tab_r[:, ::-1], rtab_i=-tab_i[:, ::-1])
            put(up_keys, got)
            gl = _mm("s5_glu", 'nn', act(gy), wcol('s5_w_glu', j), out_parts=2, tm=tm, tn=512, tk=d)
            gl2 = _split2(gl)
            o = _glu(gl2)
            sv.update(u=u, prm=prm, hr=hr, hi=hi, ypre=ypre, gy=gy, gl2=gl2, o=o, bt_re=bt_re, bt_im=bt_im, ldt=ldt)
            h = _mm("s5_out", 'nn', act(o), wrow('s5_w_out', j), res=act(h), tm=tm, tn=d, tk=d).arr[0, 0]
        sv['h_mid'] = h
        hn2 = _rms_fwd(h, norm_ffn_g[i:i + 1])
        next_keys = [('ffn_w_down', i)] + mixer_keys(i + 1)
        (up2, a_ffn), got = _ffn_up_act(hn2, gath[('ffn_w_up', i)], f_cw(i), f_cb[i], side=gather_side(next_keys))
        put(next_keys, got)
        sv.update(hn2=hn2, up2=up2, act=a_ffn)
        h = _mm("ffn_down", 'nn', act(a_ffn), wrow('ffn_w_down', i), res=act(h), tm=tm, tn=d, tk=d_ff // 2).arr[0, 0]
        saved.append(sv)

    loss_row, dh, dg_final = _loss_and_grad(h, norm_final_g.reshape(1, d), loss_target.reshape(s, d))
    loss = lax.psum(loss_row[0, 0], ("x", "y", "c"))

    gl_ = {n: [None] * w[n].shape[0] for n in PARAM_NAMES if n != 'norm_final_g'}
    recvd = {}

    def as4(n, a):
        return a.reshape((N_CHIPS,) + w[n].shape[1:])

    def scatter_side(keys):
        return _scatter_side([as4(n, gl_[n][l]) for n, l in keys])

    def record(keys, arrs):
        for k, a in zip(keys, arrs):
            recvd[k] = a

    pending = None
    for i in reversed(range(depth)):
        j = i // 2
        sv = saved[i]
        gl_['ffn_w_down'][i] = _mm("ffn_down_dw", 'tn', act(sv['act']), act(dh), out_dtype=BF16, tm=d_ff // 2, tn=d, tk=tm).arr
        (dup2, dcw2, dcb2), got = _ffn_bwd_fused(dh, gath[('ffn_w_down', i)].reshape(d_ff, d), sv['up2'], f_cw(i), f_cb[i],
                                                 side=scatter_side(pending) if pending else None)
        if pending:
            record(pending, got)
        gl_['ffn_conv_w'][i] = dcw2.transpose(1, 0, 2).reshape(FFN_CONV_W, 2 * d_ff)
        gl_['ffn_conv_b'][i] = dcb2.reshape(2 * d_ff)
        dup = Mat(dup2[:, None], 0, 'c')
        gl_['ffn_w_up'][i] = _mm("ffn_up_dw", 'tn', act(sv['hn2']), dup, out_parts=N_CHIPS, out_dtype=BF16, tm=d, tn=d_up, tk=tm).arr
        dh, dg = _mm_rms_bwd("ffn_up_dx", dup, wcol('ffn_w_up', i), sv['h_mid'], norm_ffn_g[i:i + 1], dh, tm=tm, tk=d_up)
        gl_['norm_ffn_g'][i] = dg[0]
        ffn_keys = [('ffn_w_up', i), ('ffn_w_down', i)]
        if i % 2 == 0:
            dy = _mm("rg_out_dx", 'nt', act(dh), wrow('rg_w_out', j), tm=tm, tn=d, tk=d).arr[0, 0]
            gl_['rg_w_out'][j] = _mm("rg_out_dw", 'tn', act(sv['y']), act(dh), out_dtype=BF16, tm=d, tn=d, tk=tm).arr
            (dxg2, dcw, dcb, dwa, dba, dwx, dbx, dlam), got = _rg_bwd(
                dy, sv['xg2'], sv['xr'], sv['hs'], rg_cw(j), sv['wa'], sv['ba'], sv['wx'], sv['bx'], rg_lambda[j:j + 1],
                side=scatter_side(ffn_keys))
            record(ffn_keys, got)
            gl_['rg_conv_w'][j] = dcw
            gl_['rg_conv_b'][j] = dcb[0]
            gl_['rg_w_a'][j], gl_['rg_w_x'][j] = dwa, dwx
            gl_['rg_b_a'][j], gl_['rg_b_x'][j] = dba.reshape(rg_b_a.shape[1:]), dbx.reshape(rg_b_x.shape[1:])
            gl_['rg_lambda'][j] = dlam[0]
            dxg = Mat(dxg2[:, None], 0, 'c')
            gl_['rg_w_in'][j] = _mm("rg_in_dw", 'tn', act(sv['hn']), dxg, out_parts=N_CHIPS, out_dtype=BF16, tm=d, tn=512, tk=tm).arr
            mix_dx = ("rg_in_dx", dxg, wcol('rg_w_in', j), 512)
            pending = [('rg_w_in', j), ('rg_w_out', j)]
        else:
            d_o = _mm("s5_out_dx", 'nt', act(dh), wrow('s5_w_out', j), tm=tm, tn=d, tk=d).arr[0, 0]
            gl_['s5_w_out'][j] = _mm("s5_out_dw", 'tn', act(sv['o']), act(dh), out_dtype=BF16, tm=d, tn=d, tk=tm).arr
            dgl2 = _glu_bwd(sv['gl2'], d_o)
            dgl = Mat(dgl2[:, None], 0, 'c')
            gl_['s5_w_glu'][j] = _mm("s5_glu_dw", 'tn', act(sv['gy']), dgl, out_parts=N_CHIPS, out_dtype=BF16, tm=d, tn=512, tk=tm).arr
            dgy = _mm("s5_glu_dx", 'nt', dgl, wcol('s5_w_glu', j), tm=tm, tn=d, tk=512).arr[0, 0]
            (du, dar, dai, dbpr, dbpi, dcpr, dcpi, dd), got = _s5_bwd2(
                dgy, sv['ypre'], sv['u'], sv['hr'], sv['hi'], sv['rtab_r'], sv['rtab_i'], ts=s5_ts, side=scatter_side(ffn_keys),
                **sv['prm'])
            record(ffn_keys, got)
            gl_['s5_d'][j] = dd[0]
            gl_['s5_c_re'][j] = _s5_unpack_c(dcpr, gc, n_state)
            gl_['s5_c_im'][j] = -_s5_unpack_c(dcpi, gc, n_state)
            d_are, d_aim, d_ldt, d_btr, d_bti = _s5_params_bwd(
                s5_a_re[j], s5_a_im[j], sv['ldt'], sv['bt_re'], sv['bt_im'], dar.reshape(n_grp, n_state), dai.reshape(n_grp, n_state),
                _s5_unpack_b(dbpr, gc, n_state), _s5_unpack_b(dbpi, gc, n_state))
            gl_['s5_a_re'][j], gl_['s5_a_im'][j], gl_['s5_log_dt'][j] = d_are, d_aim, d_ldt[:, 0]
            gl_['s5_b_re'][j], gl_['s5_b_im'][j] = d_btr.transpose(1, 2, 0), d_bti.transpose(1, 2, 0)
            dum = act(du)
            gl_['s5_w_in'][j] = _mm("s5_in_dw", 'tn', act(sv['hn']), dum, out_dtype=BF16, tm=d, tn=d, tk=tm).arr
            mix_dx = ("s5_in_dx", dum, wrow('s5_w_in', j), d)
            pending = [('s5_w_in', j), ('s5_w_glu', j), ('s5_w_out', j)]
        dh, dg = _mm_rms_bwd(mix_dx[0], mix_dx[1], mix_dx[2], sv['h_in'], norm_mix_g[i:i + 1], dh, tm=tm, tk=mix_dx[3])
        gl_['norm_mix_g'][i] = dg[0]
    grad_x = dh.reshape(x.shape)
    record(pending, _run_side("scatter_last", scatter_side(pending)))

    chip_sums = []
    for n in BIG:
        cols = w[n].shape[-1]
        chip_sums.append(_sum_parts([recvd[(n, l)].reshape(N_CHIPS, -1, cols) for l in range(w[n].shape[0])]))
    sib_sums = _swap_with_sibling(chip_sums)
    results = {}
    for n, mine, theirs in zip(BIG, chip_sums, sib_sums):
        cols = w[n].shape[-1]
        outs = _adamw(w[n].reshape(-1, cols), [mine.reshape(-1, cols), theirs.reshape(-1, cols)], mom[n].reshape(-1, cols),
                      vel[n].reshape(-1, cols))
        results[n] = [o.reshape(w[n].shape) for o in outs]

    small = REPLICATED + SMALL_SHARDED
    local = [dg_final.reshape(d) if n == 'norm_final_g' else jnp.stack(gl_[n]) for n in small]
    summed = _unpack(_allreduce_small(_pack(local)), local)
    me = 2 * lax.axis_index("x") + lax.axis_index("y")
    grads = [lax.dynamic_slice_in_dim(g, me * w[n].shape[-1], w[n].shape[-1], axis=g.ndim - 1) if n in SMALL_SHARDED else g
             for n, g in zip(small, summed)]
    like = [w[n] for n in small]
    outs = _adamw(_pack(like), [_pack(grads)], _pack([mom[n] for n in small]), _pack([vel[n] for n in small]))
    unpacked = [_unpack(o, like) for o in outs]
    for k, n in enumerate(small):
        results[n] = [unpacked[q][k] for q in range(4)]

    return (loss, grad_x, *[results[n][0] for n in PARAM_NAMES], *[results[n][1] for n in PARAM_NAMES],
            *[results[n][2] for n in PARAM_NAMES], *[results[n][3] for n in PARAM_NAMES])
```

```python
import functools
import math

import jax
import jax.numpy as jnp
from jax import lax
from jax.experimental import pallas as pl
from jax.experimental.pallas import tpu as pltpu

F32 = jnp.float32
BF16 = jnp.bfloat16
MESH = pl.DeviceIdType.MESH

NORM_EPS = 1e-6
RG_HEADS = 8
RG_CONV_W = 4
RG_C = 8.0
S5_GC = 16
S5_P = 64
S5_GROUPS_PER_BLOCK = 8
FFN_CONV_W = 3
N_CHIPS = 4
ADAM_LR, ADAM_B1, ADAM_B2, ADAM_EPS, ADAM_WD, ADAM_STEP = 0.001, 0.9, 0.999, 1e-08, 0.01, 10
VMEM_LIMIT_BYTES = 56 * 1024 * 1024
SUBLANES = 8
LANES = 128

PARAM_NAMES = ['norm_mix_g', 'norm_ffn_g', 'norm_final_g', 'rg_w_in', 'rg_conv_w', 'rg_conv_b', 'rg_w_a', 'rg_b_a', 'rg_w_x',
               'rg_b_x', 'rg_lambda', 'rg_w_out', 's5_w_in', 's5_a_re', 's5_a_im', 's5_log_dt', 's5_b_re', 's5_b_im', 's5_c_re',
               's5_c_im', 's5_d', 's5_w_glu', 's5_w_out', 'ffn_w_up', 'ffn_conv_w', 'ffn_conv_b', 'ffn_w_down']
SHARDED = ['rg_w_in', 'rg_conv_w', 'rg_w_out', 's5_w_in', 's5_d', 's5_w_glu', 's5_w_out', 'ffn_w_up', 'ffn_conv_w', 'ffn_w_down']
BIG = ['rg_w_in', 'rg_w_out', 's5_w_in', 's5_w_glu', 's5_w_out', 'ffn_w_up', 'ffn_w_down']
ROW_SHARDED = ['rg_w_out', 's5_w_in', 's5_w_out', 'ffn_w_down']
SMALL_SHARDED = ['rg_conv_w', 's5_d', 'ffn_conv_w']
MIXER_SHARDED = [['rg_w_in', 'rg_conv_w', 'rg_w_out'], ['s5_w_in', 's5_d', 's5_w_glu', 's5_w_out']]
FFN_SHARDED = ['ffn_w_up', 'ffn_conv_w', 'ffn_w_down']
REPLICATED = [n for n in PARAM_NAMES if n not in SHARDED]


def _cparams():
    return pltpu.CompilerParams(vmem_limit_bytes=VMEM_LIMIT_BYTES)


_GELU_C = math.sqrt(2.0 / math.pi)
_GELU_K = 0.044715


def _gelu(x):
    return 0.5 * x * (1.0 + jnp.tanh(_GELU_C * (x + _GELU_K * x * x * x)))


def _gelu_and_grad(x):
    t = jnp.tanh(_GELU_C * (x + _GELU_K * x * x * x))
    g = 0.5 * x * (1.0 + t)
    dg = 0.5 * (1.0 + t) + 0.5 * x * (1.0 - t * t) * (_GELU_C * (1.0 + 3.0 * _GELU_K * x * x))
    return g, dg


def _sigmoid(x):
    return jax.nn.sigmoid(x)


def _neg_expm1(x):
    series = -(x * (1.0 + x * (0.5 + x * (1.0 / 6 + x * (1.0 / 24 + x * (1.0 / 120 + x * (1.0 / 720)))))))
    return jnp.where(x > -0.25, series, 1.0 - jnp.exp(x))


def _softplus(z):
    return jnp.maximum(z, 0.0) + jnp.log1p(jnp.exp(-jnp.abs(z)))


def _rows(shape):
    return lax.broadcasted_iota(jnp.int32, shape, 0)


def _shift_down(x, halo, k):
    ext = jnp.concatenate([halo, x], axis=0)
    return pltpu.roll(ext, k, 0)[SUBLANES:]


def _shift_up(x, halo, k):
    ext = jnp.concatenate([x, halo], axis=0)
    n = ext.shape[0]
    return pltpu.roll(ext, n - k, 0)[:x.shape[0]]


def _scan_real_fwd(a, b):
    n = a.shape[0]
    row = _rows(a.shape)
    sh = 1
    while sh < n:
        ok = row >= sh
        b = a * jnp.where(ok, pltpu.roll(b, sh, 0), 0.0) + b
        if sh * 2 < n:
            a = a * jnp.where(ok, pltpu.roll(a, sh, 0), 1.0)
        sh *= 2
    return b


def _scan_real_rev(c, d):
    n = c.shape[0]
    row = _rows(c.shape)
    sh = 1
    while sh < n:
        ok = row < n - sh
        d = c * jnp.where(ok, pltpu.roll(d, n - sh, 0), 0.0) + d
        if sh * 2 < n:
            c = c * jnp.where(ok, pltpu.roll(c, n - sh, 0), 1.0)
        sh *= 2
    return d


def _scan_cplx(br, bi, pr_ref, pi_ref, reverse):
    n = br.shape[0]
    row = _rows(br.shape)
    sh, k = 1, 0
    while sh < n:
        pr = pr_ref[k:k + 1, :]
        pi = pi_ref[k:k + 1, :]
        if reverse:
            ok = row < n - sh
            sr = jnp.where(ok, pltpu.roll(br, n - sh, 0), 0.0)
            si = jnp.where(ok, pltpu.roll(bi, n - sh, 0), 0.0)
        else:
            ok = row >= sh
            sr = jnp.where(ok, pltpu.roll(br, sh, 0), 0.0)
            si = jnp.where(ok, pltpu.roll(bi, sh, 0), 0.0)
        br, bi = br + pr * sr - pi * si, bi + pr * si + pi * sr
        sh *= 2
        k += 1
    return br, bi


RG_LANE_CHUNK = 512


def _real_slab_scan(a_ref, b_ref, out_ref, carry_ref, reverse):
    t, c = a_ref.shape
    nsl = t // SUBLANES
    lc = min(RG_LANE_CHUNK, c)
    row8 = _rows((SUBLANES, lc))
    for q in range(c // lc):
        sl = slice(q * lc, (q + 1) * lc)

        def slab(jj, carry, sl=sl):
            j = nsl - 1 - jj if reverse else jj
            r0 = pl.multiple_of(j * SUBLANES, SUBLANES)
            a, b = a_ref[pl.ds(r0, SUBLANES), sl], b_ref[pl.ds(r0, SUBLANES), sl]
            for k in range(3):
                sh = 1 << k
                keep = row8 < SUBLANES - sh if reverse else row8 >= sh
                amount = SUBLANES - sh if reverse else sh
                b = a * jnp.where(keep, pltpu.roll(b, amount, 0), 0.0) + b
                a = a * jnp.where(keep, pltpu.roll(a, amount, 0), 1.0)
            x = b + a * jnp.broadcast_to(carry, b.shape)
            out_ref[pl.ds(r0, SUBLANES), sl] = x
            return x[:1, :] if reverse else x[SUBLANES - 1:, :]

        carry_ref[:, sl] = lax.fori_loop(0, nsl, slab, carry_ref[:, sl], unroll=2)


class Mat:
    def __init__(self, arr, l=0, split='c'):
        assert arr.ndim == 4
        self.arr, self.l, self.split = arr, l, split
        p, _, r, c = arr.shape
        self.shape = (r, c * p) if split == 'c' else (r * p, c)

    def spec(self, tr, tc, rc):
        p, _, r, c = self.arr.shape
        l = self.l
        assert r % tr == 0 and c % tc == 0, (self.arr.shape, tr, tc)
        if self.split == 'c':
            per = c // tc
            return pl.BlockSpec((None, None, tr, tc), lambda i, j, k: (rc(i, j, k)[1] // per, l, rc(i, j, k)[0], rc(i, j, k)[1] % per))
        per = r // tr
        return pl.BlockSpec((None, None, tr, tc), lambda i, j, k: (rc(i, j, k)[0] // per, l, rc(i, j, k)[0] % per, rc(i, j, k)[1]))


def act(x, parts=1):
    s, c = x.shape
    return Mat(x.reshape(s, parts, c // parts).transpose(1, 0, 2)[:, None] if parts > 1 else x[None, None])


def _mm(name, mode, a, b, *, out_parts=1, out_split='c', out_dtype=F32, res=None, tm=512, tn=512, tk=512):
    if mode == 'nn':
        (m, kk), (kb, n) = a.shape, b.shape
    elif mode == 'nt':
        (m, kk), (n, kb) = a.shape, b.shape
    else:
        (kk, m), (kb, n) = a.shape, b.shape
    assert kk == kb, (name, a.shape, b.shape)
    tm, tn, tk = min(tm, m), min(tn, n), min(tk, kk)
    assert m % tm == 0 and n % tn == 0 and kk % tk == 0, (name, m, n, kk, tm, tn, tk)
    nk = kk // tk
    if mode == 'nn':
        a_spec = a.spec(tm, tk, lambda i, j, k: (i, k))
        b_spec = b.spec(tk, tn, lambda i, j, k: (k, j))
        dims = (((1,), (0,)), ((), ()))
    elif mode == 'nt':
        a_spec = a.spec(tm, tk, lambda i, j, k: (i, k))
        b_spec = b.spec(tn, tk, lambda i, j, k: (j, k))
        dims = (((1,), (1,)), ((), ()))
    else:
        a_spec = a.spec(tk, tm, lambda i, j, k: (k, i))
        b_spec = b.spec(tk, tn, lambda i, j, k: (k, j))
        dims = (((0,), (0,)), ((), ()))
    if out_split == 'c':
        out_arr = jax.ShapeDtypeStruct((out_parts, 1, m, n // out_parts), out_dtype)
    else:
        out_arr = jax.ShapeDtypeStruct((out_parts, 1, m // out_parts, n), out_dtype)
    out_mat = Mat(out_arr, 0, out_split)
    o_spec = out_mat.spec(tm, tn, lambda i, j, k: (i, j))
    has_res = res is not None

    def body(*refs):
        if has_res:
            a_ref, b_ref, r_ref, o_ref = refs[:4]
        else:
            a_ref, b_ref, o_ref = refs[:3]
        prod = lax.dot_general(a_ref[...].astype(BF16), b_ref[...].astype(BF16), dims, preferred_element_type=F32)

        def finish(acc):
            if has_res:
                acc = acc + r_ref[...]
            o_ref[...] = acc.astype(out_dtype)

        if nk == 1:
            finish(prod)
        else:
            acc_ref = refs[-1]
            k = pl.program_id(2)

            @pl.when(k == 0)
            def _():
                acc_ref[...] = prod

            @pl.when(k > 0)
            def _():
                acc_ref[...] += prod

            @pl.when(k == nk - 1)
            def _():
                finish(acc_ref[...])

    in_specs = [a_spec, b_spec]
    args = [a.arr, b.arr]
    if has_res:
        in_specs.append(res.spec(tm, tn, lambda i, j, k: (i, j)))
        args.append(res.arr)
    out = pl.pallas_call(
        body, name=name, grid=(m // tm, n // tn, nk), in_specs=in_specs, out_specs=o_spec, out_shape=out_arr,
        scratch_shapes=[pltpu.VMEM((tm, tn), F32)] if nk > 1 else [], compiler_params=_cparams(),
    )(*args)
    return Mat(out, 0, out_split)


def _rms_fwd(h, g, ts=512):
    s, d = h.shape
    ts = min(ts, s)

    def body(h_ref, g_ref, o_ref):
        x = h_ref[...]
        var = jnp.mean(x * x, axis=-1, keepdims=True)
        o_ref[...] = (x * lax.rsqrt(var + NORM_EPS) * g_ref[...]).astype(BF16)

    return pl.pallas_call(
        body, name="rms_fwd", grid=(s // ts,),
        in_specs=[pl.BlockSpec((ts, d), lambda i: (i, 0)), pl.BlockSpec((1, d), lambda i: (0, 0))],
        out_specs=pl.BlockSpec((ts, d), lambda i: (i, 0)), out_shape=jax.ShapeDtypeStruct((s, d), BF16),
        compiler_params=_cparams(),
    )(h, g)


def _rms_bwd(h, g, dhn, dh_in, ts=512):
    s, d = h.shape
    ts = min(ts, s)

    def body(h_ref, g_ref, dhn_ref, dhin_ref, dh_ref, dg_ref):
        i = pl.program_id(0)
        x = h_ref[...]
        rstd = lax.rsqrt(jnp.mean(x * x, axis=-1, keepdims=True) + NORM_EPS)
        xhat = x * rstd
        dhn_v = dhn_ref[...]
        dxh = dhn_v * g_ref[...]
        dh_ref[...] = dhin_ref[...] + rstd * (dxh - xhat * jnp.mean(dxh * xhat, axis=-1, keepdims=True))
        part = jnp.sum(dhn_v * xhat, axis=0, keepdims=True)

        @pl.when(i == 0)
        def _():
            dg_ref[...] = part

        @pl.when(i > 0)
        def _():
            dg_ref[...] += part

    row = pl.BlockSpec((ts, d), lambda i: (i, 0))
    vec = pl.BlockSpec((1, d), lambda i: (0, 0))
    return pl.pallas_call(
        body, name="rms_bwd", grid=(s // ts,), in_specs=[row, vec, row, row], out_specs=[row, vec],
        out_shape=[jax.ShapeDtypeStruct((s, d), F32), jax.ShapeDtypeStruct((1, d), F32)], compiler_params=_cparams(),
    )(h, g, dhn, dh_in)


def _loss_and_grad(h, g, tgt, ts=512):
    s, d = h.shape
    ts = min(ts, s)

    def body(h_ref, g_ref, t_ref, loss_ref, dh_ref, dg_ref):
        i = pl.program_id(0)
        x = h_ref[...]
        gv = g_ref[...]
        rstd = lax.rsqrt(jnp.mean(x * x, axis=-1, keepdims=True) + NORM_EPS)
        xhat = x * rstd
        err = xhat * gv - t_ref[...]
        dy = err * (1.0 / d)
        dxh = dy * gv
        dh_ref[...] = rstd * (dxh - xhat * jnp.mean(dxh * xhat, axis=-1, keepdims=True))
        part = jnp.sum(dy * xhat, axis=0, keepdims=True)
        lpart = jnp.broadcast_to(jnp.sum(jnp.sum(err * err, axis=0, keepdims=True), axis=1, keepdims=True) * (0.5 / d), (1, LANES))

        @pl.when(i == 0)
        def _():
            dg_ref[...] = part
            loss_ref[...] = lpart

        @pl.when(i > 0)
        def _():
            dg_ref[...] += part
            loss_ref[...] += lpart

    row = pl.BlockSpec((ts, d), lambda i: (i, 0))
    vec = pl.BlockSpec((1, d), lambda i: (0, 0))
    return pl.pallas_call(
        body, name="loss_and_grad", grid=(s // ts,), in_specs=[row, vec, row],
        out_specs=[pl.BlockSpec((1, LANES), lambda i: (0, 0)), row, vec],
        out_shape=[jax.ShapeDtypeStruct((1, LANES), F32), jax.ShapeDtypeStruct((s, d), F32), jax.ShapeDtypeStruct((1, d), F32)],
        compiler_params=_cparams(),
    )(h, g, tgt)


def _halo_before(ts, nrow8):
    return lambda i: jnp.maximum(i * (ts // SUBLANES) - 1, 0)


def _ffn_act(up2, conv_w2, conv_b2, ts=512, tn=512, side=None):
    _, s, f = up2.shape
    ts, tn = min(ts, s), min(tn, f)
    kw = FFN_CONV_W

    def body(up_ref, halo_ref, w_ref, b_ref, o_ref):
        i = pl.program_id(0)
        cs = []
        for h in range(2):
            x = up_ref[h]
            halo = jnp.where(i == 0, 0.0, halo_ref[h])
            c = b_ref[h] + w_ref[h, kw - 1:kw, :] * x
            for sft in range(1, kw):
                c = c + w_ref[h, kw - 1 - sft:kw - sft, :] * _shift_down(x, halo, sft)
            cs.append(c)
        o_ref[...] = (_gelu(cs[0]) * cs[1]).astype(BF16)

    hb = ts // SUBLANES
    g0, g1 = s // ts, f // tn
    outs, side_outs = _call_with_side(
        body, side, lambda: (pl.program_id(0) == 0) & (pl.program_id(1) == 0),
        lambda: (pl.program_id(0) == g0 - 1) & (pl.program_id(1) == g1 - 1),
        name="ffn_act", grid=(g0, g1),
        in_specs=[pl.BlockSpec((2, ts, tn), lambda i, j: (0, i, j)),
                  pl.BlockSpec((2, SUBLANES, tn), lambda i, j: (0, jnp.maximum(i * hb - 1, 0), j)),
                  pl.BlockSpec((2, kw, tn), lambda i, j: (0, 0, j)),
                  pl.BlockSpec((2, 1, tn), lambda i, j: (0, 0, j))],
        out_specs=[pl.BlockSpec((ts, tn), lambda i, j: (i, j))], out_shape=[jax.ShapeDtypeStruct((s, f), BF16)],
        scratch_shapes=[], args=(up2, up2, conv_w2, conv_b2))
    return outs[0], side_outs


def _ffn_bwd(up2, dact, conv_w2, conv_b2, ts=256, tn=512, side=None):
    _, s, f = up2.shape
    ts, tn = min(ts, s), min(tn, f)
    kw = FFN_CONV_W
    nt = s // ts
    hb = ts // SUBLANES
    last8 = s // SUBLANES - 1

    def body(up_ref, hb_ref, ha_ref, da_ref, dah_ref, w_ref, b_ref, dup_ref, dw_ref, db_ref):
        i = pl.program_id(1)
        first, last = i == 0, i == nt - 1
        ce, xs = [], []
        for h in range(2):
            x = up_ref[h]
            before = jnp.where(first, 0.0, hb_ref[h])
            after = ha_ref[h]
            ext = jnp.concatenate([before, x, after], axis=0)
            c = b_ref[h] + w_ref[h, kw - 1:kw, :] * ext
            shifted = [ext]
            for sft in range(1, kw):
                sh = pltpu.roll(ext, sft, 0)
                shifted.append(sh)
                c = c + w_ref[h, kw - 1 - sft:kw - sft, :] * sh
            ce.append(c[SUBLANES:])
            xs.append([sh[SUBLANES:SUBLANES + ts] for sh in shifted])
        da = jnp.concatenate([da_ref[...], jnp.where(last, 0.0, dah_ref[...])], axis=0)
        g1, dg1 = _gelu_and_grad(ce[0])
        dcs = [da * ce[1] * dg1, da * g1]
        for h in range(2):
            dc = dcs[h]
            n = dc.shape[0]
            dup = w_ref[h, kw - 1:kw, :] * dc[:ts]
            for sft in range(1, kw):
                dup = dup + w_ref[h, kw - 1 - sft:kw - sft, :] * pltpu.roll(dc, n - sft, 0)[:ts]
            dup_ref[h] = dup.astype(BF16)
            dct = dc[:ts]
            dbp = jnp.sum(dct, axis=0, keepdims=True)
            dwp = [jnp.sum(dct * xs[h][kw - 1 - k], axis=0, keepdims=True) for k in range(kw)]

            @pl.when(first)
            def _():
                db_ref[h] = dbp
                for k in range(kw):
                    dw_ref[h, k:k + 1, :] = dwp[k]

            @pl.when(i > 0)
            def _():
                db_ref[h] += dbp
                for k in range(kw):
                    dw_ref[h, k:k + 1, :] += dwp[k]

    g0 = f // tn
    return _call_with_side(
        body, side, lambda: (pl.program_id(0) == 0) & (pl.program_id(1) == 0),
        lambda: (pl.program_id(0) == g0 - 1) & (pl.program_id(1) == nt - 1),
        name="ffn_bwd", grid=(g0, nt),
        in_specs=[pl.BlockSpec((2, ts, tn), lambda j, i: (0, i, j)),
                  pl.BlockSpec((2, SUBLANES, tn), lambda j, i: (0, jnp.maximum(i * hb - 1, 0), j)),
                  pl.BlockSpec((2, SUBLANES, tn), lambda j, i: (0, jnp.minimum((i + 1) * hb, last8), j)),
                  pl.BlockSpec((ts, tn), lambda j, i: (i, j)),
                  pl.BlockSpec((SUBLANES, tn), lambda j, i: (jnp.minimum((i + 1) * hb, last8), j)),
                  pl.BlockSpec((2, kw, tn), lambda j, i: (0, 0, j)),
                  pl.BlockSpec((2, 1, tn), lambda j, i: (0, 0, j))],
        out_specs=[pl.BlockSpec((2, ts, tn), lambda j, i: (0, i, j)),
                   pl.BlockSpec((2, kw, tn), lambda j, i: (0, 0, j)),
                   pl.BlockSpec((2, 1, tn), lambda j, i: (0, 0, j))],
        out_shape=[jax.ShapeDtypeStruct((2, s, f), BF16), jax.ShapeDtypeStruct((2, kw, f), F32),
                   jax.ShapeDtypeStruct((2, 1, f), F32)],
        scratch_shapes=[], args=(up2, up2, up2, dact, dact, conv_w2, conv_b2))


def _mm_rms_bwd(name, a, b, h, g, dh_in, *, tm, tk):
    (m, kk), (n, kb) = a.shape, b.shape
    assert kk == kb and h.shape == (m, n), (name, a.shape, b.shape, h.shape)
    tm, tk = min(tm, m), min(tk, kk)
    nk = kk // tk
    dims = (((1,), (1,)), ((), ()))

    def body(a_ref, b_ref, h_ref, g_ref, dhin_ref, dh_ref, dg_ref, *acc):
        i, k = pl.program_id(0), pl.program_id(2)
        prod = lax.dot_general(a_ref[...].astype(BF16), b_ref[...].astype(BF16), dims, preferred_element_type=F32)

        def finish(dhn):
            x = h_ref[...]
            rstd = lax.rsqrt(jnp.mean(x * x, axis=-1, keepdims=True) + NORM_EPS)
            xhat = x * rstd
            dxh = dhn * g_ref[...]
            dh_ref[...] = dhin_ref[...] + rstd * (dxh - xhat * jnp.mean(dxh * xhat, axis=-1, keepdims=True))
            part = jnp.sum(dhn * xhat, axis=0, keepdims=True)

            @pl.when(i == 0)
            def _():
                dg_ref[...] = part

            @pl.when(i > 0)
            def _():
                dg_ref[...] += part

        if nk == 1:
            finish(prod)
        else:
            acc_ref = acc[0]

            @pl.when(k == 0)
            def _():
                acc_ref[...] = prod

            @pl.when(k > 0)
            def _():
                acc_ref[...] += prod

            @pl.when(k == nk - 1)
            def _():
                finish(acc_ref[...])

    row = pl.BlockSpec((tm, n), lambda i, j, k: (i, 0))
    vec = pl.BlockSpec((1, n), lambda i, j, k: (0, 0))
    return pl.pallas_call(
        body, name=name, grid=(m // tm, 1, nk),
        in_specs=[a.spec(tm, tk, lambda i, j, k: (i, k)), b.spec(n, tk, lambda i, j, k: (0, k)), row, vec, row],
        out_specs=[row, vec], out_shape=[jax.ShapeDtypeStruct((m, n), F32), jax.ShapeDtypeStruct((1, n), F32)],
        scratch_shapes=[pltpu.VMEM((tm, n), F32)] if nk > 1 else [], compiler_params=_cparams(),
    )(a.arr, b.arr, h, g, dh_in)


def _ffn_up_act(hn2, w_up4, conv_w2, conv_b2, ts=1024, tn=512, sub=256, side=None):
    s, d = hn2.shape
    p, _, wc = w_up4.shape
    f = p * wc // 2
    ts, tn = min(ts, s), min(tn, wc)
    sub = min(sub, ts)
    per = wc // tn
    kw = FFN_CONV_W
    g0, g1 = f // tn, s // ts

    def body(hn_ref, w1_ref, w2_ref, cw_ref, cb_ref, up_ref, c_ref, act_ref, carry_ref):
        @pl.when(pl.program_id(1) == 0)
        def _():
            carry_ref[...] = jnp.zeros_like(carry_ref)

        for q in range(ts // sub):
            rows = slice(q * sub, (q + 1) * sub)
            hn = hn_ref[rows, :]
            cs = []
            for h, w_ref in enumerate((w1_ref, w2_ref)):
                x = jnp.dot(hn, w_ref[...], preferred_element_type=F32)
                up_ref[h, rows, :] = x
                halo = carry_ref[h]
                c = cb_ref[h] + cw_ref[h, kw - 1:kw, :] * x
                for sft in range(1, kw):
                    c = c + cw_ref[h, kw - 1 - sft:kw - sft, :] * _shift_down(x, halo, sft)
                carry_ref[h] = x[sub - SUBLANES:, :]
                c_ref[h, rows, :] = c
                cs.append(c)
            act_ref[rows, :] = (_gelu(cs[0]) * cs[1]).astype(BF16)

    outs, side_outs = _call_with_side(
        body, side, lambda: (pl.program_id(0) == 0) & (pl.program_id(1) == 0),
        lambda: (pl.program_id(0) == g0 - 1) & (pl.program_id(1) == g1 - 1),
        name="ffn_up_act", grid=(g0, g1),
        in_specs=[pl.BlockSpec((ts, d), lambda j, i: (i, 0)),
                  pl.BlockSpec((None, d, tn), lambda j, i: (j // per, 0, j % per)),
                  pl.BlockSpec((None, d, tn), lambda j, i: (p // 2 + j // per, 0, j % per)),
                  pl.BlockSpec((2, kw, tn), lambda j, i: (0, 0, j)),
                  pl.BlockSpec((2, 1, tn), lambda j, i: (0, 0, j))],
        out_specs=[pl.BlockSpec((2, ts, tn), lambda j, i: (0, i, j)), pl.BlockSpec((2, ts, tn), lambda j, i: (0, i, j)),
                   pl.BlockSpec((ts, tn), lambda j, i: (i, j))],
        out_shape=[jax.ShapeDtypeStruct((2, s, f), F32), jax.ShapeDtypeStruct((2, s, f), F32), jax.ShapeDtypeStruct((s, f), BF16)],
        scratch_shapes=[pltpu.VMEM((2, SUBLANES, tn), F32)], args=(hn2, w_up4, w_up4, conv_w2, conv_b2))
    return outs, side_outs


def _ffn_bwd_fused(dh, w_down, up2, c2, conv_w2, ts=256, tn=512, side=None):
    s, d = dh.shape
    _, _, f = up2.shape
    ts, tn = min(ts, s), min(tn, f)
    kw = FFN_CONV_W
    nt = s // ts
    hb = ts // SUBLANES
    g0 = f // tn
    nt_dims = (((1,), (1,)), ((), ()))

    def body(dh_ref, wd_ref, up_ref, c_ref, w_ref, dup_ref, dw_ref, db_ref, carry_ref):
        i = pl.program_id(1)
        first_step = i == 0

        @pl.when(first_step)
        def _():
            carry_ref[...] = jnp.zeros_like(carry_ref)

        da = lax.dot_general(dh_ref[...].astype(BF16), wd_ref[...], nt_dims, preferred_element_type=F32)
        g1, dg1 = _gelu_and_grad(c_ref[0])
        dcs = [da * c_ref[1] * dg1, da * g1]
        for h in range(2):
            dc = dcs[h]
            after = carry_ref[h]
            ups = [dc] + [_shift_up(dc, after, sft) for sft in range(1, kw)]
            dup = w_ref[h, kw - 1:kw, :] * dc
            for sft in range(1, kw):
                dup = dup + w_ref[h, kw - 1 - sft:kw - sft, :] * ups[sft]
            carry_ref[h] = dc[:SUBLANES]
            dup_ref[h] = dup.astype(BF16)
            dbp = jnp.sum(dc, axis=0, keepdims=True)
            x = up_ref[h]
            dwp = [jnp.sum(ups[kw - 1 - k] * x, axis=0, keepdims=True) for k in range(kw)]

            @pl.when(first_step)
            def _():
                db_ref[h] = dbp
                for k in range(kw):
                    dw_ref[h, k:k + 1, :] = dwp[k]

            @pl.when(i > 0)
            def _():
                db_ref[h] += dbp
                for k in range(kw):
                    dw_ref[h, k:k + 1, :] += dwp[k]

    rev = lambda i: nt - 1 - i
    return _call_with_side(
        body, side, lambda: (pl.program_id(0) == 0) & (pl.program_id(1) == 0),
        lambda: (pl.program_id(0) == g0 - 1) & (pl.program_id(1) == nt - 1),
        name="ffn_bwd", grid=(g0, nt),
        in_specs=[pl.BlockSpec((ts, d), lambda j, i: (rev(i), 0)),
                  pl.BlockSpec((tn, d), lambda j, i: (j, 0)),
                  pl.BlockSpec((2, ts, tn), lambda j, i: (0, rev(i), j)),
                  pl.BlockSpec((2, ts, tn), lambda j, i: (0, rev(i), j)),
                  pl.BlockSpec((2, kw, tn), lambda j, i: (0, 0, j))],
        out_specs=[pl.BlockSpec((2, ts, tn), lambda j, i: (0, rev(i), j)),
                   pl.BlockSpec((2, kw, tn), lambda j, i: (0, 0, j)),
                   pl.BlockSpec((2, 1, tn), lambda j, i: (0, 0, j))],
        out_shape=[jax.ShapeDtypeStruct((2, s, f), BF16), jax.ShapeDtypeStruct((2, kw, f), F32),
                   jax.ShapeDtypeStruct((2, 1, f), F32)],
        scratch_shapes=[pltpu.VMEM((2, SUBLANES, tn), F32)], args=(dh, w_down, up2, c2, conv_w2))


def _rg_gates(xr, wa_ref, ba_ref, wx_ref, bx_ref, lam_ref):
    bw = wa_ref.shape[-1]
    xb = xr.astype(BF16)
    za = jnp.concatenate([jnp.dot(xb[:, h * bw:(h + 1) * bw], wa_ref[h], preferred_element_type=F32)
                          for h in range(RG_HEADS)], axis=1) + ba_ref[...]
    zx = jnp.concatenate([jnp.dot(xb[:, h * bw:(h + 1) * bw], wx_ref[h], preferred_element_type=F32)
                          for h in range(RG_HEADS)], axis=1) + bx_ref[...]
    r, ig = _sigmoid(za), _sigmoid(zx)
    sp = _softplus(-lam_ref[...])
    la = -RG_C * r * sp
    a = jnp.exp(la)
    mult = jnp.sqrt(_neg_expm1(2.0 * la))
    return xb, r, ig, sp, a, mult


def _rg_fwd(xg2, conv_w, conv_b, w_a, b_a, w_x, b_x, lam, ts=256, side=None):
    _, s, c = xg2.shape
    ts = min(ts, s)
    kw = RG_CONV_W
    hb = ts // SUBLANES

    def body(xg_ref, halo_ref, cw_ref, cb_ref, wa_ref, ba_ref, wx_ref, bx_ref, lam_ref, xr_ref, hs_ref, y_ref, carry_ref,
             a_scr, b_scr):
        i = pl.program_id(0)

        @pl.when(i == 0)
        def _():
            carry_ref[...] = jnp.zeros_like(carry_ref)

        xp = xg_ref[0]
        halo = jnp.where(i == 0, 0.0, halo_ref[...])
        xr = cb_ref[...] + cw_ref[kw - 1:kw, :] * xp
        for sft in range(1, kw):
            xr = xr + cw_ref[kw - 1 - sft:kw - sft, :] * _shift_down(xp, halo, sft)
        _, r, ig, sp, a, mult = _rg_gates(xr, wa_ref, ba_ref, wx_ref, bx_ref, lam_ref)
        a_scr[...] = a
        b_scr[...] = mult * (ig * xr)
        _real_slab_scan(a_scr, b_scr, hs_ref, carry_ref, reverse=False)
        xr_ref[...] = xr
        y_ref[...] = (hs_ref[...] * _gelu(xg_ref[1])).astype(BF16)

    full = lambda shape: pl.BlockSpec(shape, lambda i: (0,) * len(shape))
    row_spec = pl.BlockSpec((ts, c), lambda i: (i, 0))
    nt = s // ts
    return _call_with_side(
        body, side, lambda: pl.program_id(0) == 0, lambda: pl.program_id(0) == nt - 1,
        name="rg_fwd", grid=(nt,),
        in_specs=[pl.BlockSpec((2, ts, c), lambda i: (0, i, 0)),
                  pl.BlockSpec((None, SUBLANES, c), lambda i: (0, jnp.maximum(i * hb - 1, 0), 0)),
                  full(conv_w.shape), full(conv_b.shape), full(w_a.shape), full(b_a.shape), full(w_x.shape), full(b_x.shape),
                  full(lam.shape)],
        out_specs=[row_spec, row_spec, row_spec],
        out_shape=[jax.ShapeDtypeStruct((s, c), F32), jax.ShapeDtypeStruct((s, c), F32), jax.ShapeDtypeStruct((s, c), BF16)],
        scratch_shapes=[pltpu.VMEM((1, c), F32), pltpu.VMEM((ts, c), F32), pltpu.VMEM((ts, c), F32)],
        args=(xg2, xg2, conv_w, conv_b, w_a, b_a, w_x, b_x, lam))


def _rg_bwd(dy, xg2, xr, hs, conv_w, w_a, b_a, w_x, b_x, lam, ts=256, side=None):
    _, s, c = xg2.shape
    ts = min(ts, s)
    nt = s // ts
    kw = RG_CONV_W
    hb = ts // SUBLANES
    bw = c // RG_HEADS
    tn_dims = (((0,), (0,)), ((), ()))
    nt_dims = (((1,), (1,)), ((), ()))

    def body(dy_ref, xg_ref, xph_ref, xr_ref, hs_ref, hsh_ref, cw_ref, wa_ref, ba_ref, wx_ref, bx_ref, lam_ref,
             dxg_ref, dcw_ref, dcb_ref, dwa_ref, dba_ref, dwx_ref, dbx_ref, dlam_ref,
             lam_carry, a_carry, dxr_carry, dsp_acc, a_scr, b_scr):
        i = pl.program_id(0)
        first_step = i == 0
        time_first = i == nt - 1

        @pl.when(first_step)
        def _():
            lam_carry[...] = jnp.zeros_like(lam_carry)
            a_carry[...] = jnp.ones_like(a_carry)
            dxr_carry[...] = jnp.zeros_like(dxr_carry)
            dsp_acc[...] = jnp.zeros_like(dsp_acc)
            for ref in (dcw_ref, dcb_ref, dwa_ref, dba_ref, dwx_ref, dbx_ref):
                ref[...] = jnp.zeros_like(ref)

        xr = xr_ref[...]
        hs = hs_ref[...]
        gate = xg_ref[1]
        xb, r, ig, sp, a, mult = _rg_gates(xr, wa_ref, ba_ref, wx_ref, bx_ref, lam_ref)
        dyv = dy_ref[...]
        gg, dgg = _gelu_and_grad(gate)
        dhs = dyv * gg
        dxg_ref[1] = (dyv * hs * dgg).astype(BF16)
        row = _rows(xr.shape)
        a_scr[...] = jnp.where(row == ts - 1, a_carry[0:1, :], pltpu.roll(a, ts - 1, 0))
        b_scr[...] = dhs
        _real_slab_scan(a_scr, b_scr, b_scr, lam_carry, reverse=True)
        lmb = b_scr[...]
        a_carry[...] = a[:SUBLANES]
        hs_prev = _shift_down(hs, jnp.where(time_first, 0.0, hsh_ref[...]), 1)
        d_a = lmb * hs_prev
        d_m = lmb * (ig * xr)
        d_ig = lmb * mult * xr
        d_xr = lmb * mult * ig
        d_la = a * d_a - (a * a / mult) * d_m
        dsp_acc[...] += jnp.sum(-RG_C * r * d_la, axis=0, keepdims=True)
        d_za = (-RG_C * sp) * d_la * r * (1.0 - r)
        d_zx = d_ig * ig * (1.0 - ig)
        dba_ref[...] += jnp.sum(d_za, axis=0, keepdims=True)
        dbx_ref[...] += jnp.sum(d_zx, axis=0, keepdims=True)
        dzab, dzxb = d_za.astype(BF16), d_zx.astype(BF16)
        back = []
        for h in range(RG_HEADS):
            sl = slice(h * bw, (h + 1) * bw)
            dwa_ref[h] += lax.dot_general(xb[:, sl], dzab[:, sl], tn_dims, preferred_element_type=F32)
            dwx_ref[h] += lax.dot_general(xb[:, sl], dzxb[:, sl], tn_dims, preferred_element_type=F32)
            back.append(lax.dot_general(dzab[:, sl], wa_ref[h], nt_dims, preferred_element_type=F32)
                        + lax.dot_general(dzxb[:, sl], wx_ref[h], nt_dims, preferred_element_type=F32))
        d_xr = d_xr + jnp.concatenate(back, axis=1)
        d_xp = cw_ref[kw - 1:kw, :] * d_xr
        after = dxr_carry[...]
        for sft in range(1, kw):
            d_xp = d_xp + cw_ref[kw - 1 - sft:kw - sft, :] * _shift_up(d_xr, after, sft)
        dxr_carry[...] = d_xr[:SUBLANES]
        dxg_ref[0] = d_xp.astype(BF16)
        xp = xg_ref[0]
        before = jnp.where(time_first, 0.0, xph_ref[...])
        dcb_ref[...] += jnp.sum(d_xr, axis=0, keepdims=True)
        dcw_ref[kw - 1:kw, :] += jnp.sum(d_xr * xp, axis=0, keepdims=True)
        for sft in range(1, kw):
            dcw_ref[kw - 1 - sft:kw - sft, :] += jnp.sum(d_xr * _shift_down(xp, before, sft), axis=0, keepdims=True)
        dlam_ref[...] = dsp_acc[...] * (-_sigmoid(-lam_ref[...]))

    full = lambda shape: pl.BlockSpec(shape, lambda i: (0,) * len(shape))
    rev = lambda i: nt - 1 - i
    row_spec = pl.BlockSpec((ts, c), lambda i: (rev(i), 0))
    halo_idx = lambda i: jnp.maximum(rev(i) * hb - 1, 0)
    vec = (1, c)
    return _call_with_side(
        body, side, lambda: pl.program_id(0) == 0, lambda: pl.program_id(0) == nt - 1,
        name="rg_bwd", grid=(nt,),
        in_specs=[row_spec,
                  pl.BlockSpec((2, ts, c), lambda i: (0, rev(i), 0)),
                  pl.BlockSpec((None, SUBLANES, c), lambda i: (0, halo_idx(i), 0)),
                  row_spec, row_spec,
                  pl.BlockSpec((SUBLANES, c), lambda i: (halo_idx(i), 0)),
                  full(conv_w.shape), full(w_a.shape), full(b_a.shape), full(w_x.shape), full(b_x.shape), full(lam.shape)],
        out_specs=[pl.BlockSpec((2, ts, c), lambda i: (0, rev(i), 0)), full(conv_w.shape), full(vec), full(w_a.shape), full(vec),
                   full(w_x.shape), full(vec), full(vec)],
        out_shape=[jax.ShapeDtypeStruct((2, s, c), BF16), jax.ShapeDtypeStruct(conv_w.shape, F32), jax.ShapeDtypeStruct(vec, F32),
                   jax.ShapeDtypeStruct(w_a.shape, F32), jax.ShapeDtypeStruct(vec, F32), jax.ShapeDtypeStruct(w_x.shape, F32),
                   jax.ShapeDtypeStruct(vec, F32), jax.ShapeDtypeStruct(vec, F32)],
        scratch_shapes=[pltpu.VMEM(vec, F32), pltpu.VMEM((SUBLANES, c), F32), pltpu.VMEM((SUBLANES, c), F32),
                        pltpu.VMEM(vec, F32), pltpu.VMEM((ts, c), F32), pltpu.VMEM((ts, c), F32)],
        args=(dy, xg2, xg2, xr, hs, hs, conv_w, w_a, b_a, w_x, b_x, lam))


def _s5_param_fn(a_re, a_im, log_dt, bt_re, bt_im):
    dt = jnp.exp(log_dt)
    mag = jnp.exp(a_re * dt)
    abr = mag * jnp.cos(a_im * dt)
    abi = mag * jnp.sin(a_im * dt)
    ur, ui = abr - 1.0, abi
    den = a_re * a_re + a_im * a_im
    wr = (ur * a_re + ui * a_im) / den
    wi = (ui * a_re - ur * a_im) / den
    bbr = wr[None] * bt_re - wi[None] * bt_im
    bbi = wr[None] * bt_im + wi[None] * bt_re
    return abr, abi, bbr, bbi


def _s5_params(a_re, a_im, log_dt, bt_re, bt_im, nlev):
    g, p = a_re.shape
    gc = bt_re.shape[0]

    def body(ar_ref, ai_ref, dt_ref, br_ref, bi_ref, abr_ref, abi_ref, pr_ref, pi_ref, bbr_ref, bbi_ref):
        abr, abi, bbr, bbi = _s5_param_fn(ar_ref[...], ai_ref[...], dt_ref[...], br_ref[...], bi_ref[...])
        abr_ref[...] = abr
        abi_ref[...] = abi
        bbr_ref[...] = bbr
        bbi_ref[...] = bbi
        qr, qi = abr, abi
        for k in range(nlev):
            pr_ref[k] = qr
            pi_ref[k] = qi
            qr, qi = qr * qr - qi * qi, 2.0 * qr * qi

    sd = jax.ShapeDtypeStruct
    return pl.pallas_call(
        body, name="s5_params",
        out_shape=[sd((g, p), F32), sd((g, p), F32), sd((nlev, g, p), F32), sd((nlev, g, p), F32), sd((gc, g, p), F32),
                   sd((gc, g, p), F32)],
    )(a_re, a_im, log_dt, bt_re, bt_im)


def _s5_params_bwd(a_re, a_im, log_dt, bt_re, bt_im, d_abr, d_abi, d_bbr, d_bbi):
    def body(ar_ref, ai_ref, dt_ref, br_ref, bi_ref, g0, g1, g2, g3, o0, o1, o2, o3, o4):
        _, vjp = jax.vjp(_s5_param_fn, ar_ref[...], ai_ref[...], dt_ref[...], br_ref[...], bi_ref[...])
        outs = vjp((g0[...], g1[...], g2[...], g3[...]))
        for o, v in zip((o0, o1, o2, o3, o4), outs):
            o[...] = v

    sd = jax.ShapeDtypeStruct
    return pl.pallas_call(
        body, name="s5_params_bwd",
        out_shape=[sd(a_re.shape, F32), sd(a_im.shape, F32), sd(log_dt.shape, F32), sd(bt_re.shape, F32), sd(bt_im.shape, F32)],
    )(a_re, a_im, log_dt, bt_re, bt_im, d_abr, d_abi, d_bbr, d_bbi)


def _s5_fwd(u, abr, abi, pw_r, pw_i, bp_r, bp_i, cp_r, cp_i, dvec, ts=128, side=None):
    s, c = u.shape
    n = abr.shape[1]
    nblk, cb, nb = bp_r.shape
    ts = min(ts, s)

    def body(u_ref, ar_ref, ai_ref, pr_ref, pi_ref, bpr_ref, bpi_ref, cpr_ref, cpi_ref, d_ref,
             hr_ref, hi_ref, yp_ref, gy_ref, car_r, car_i):
        i = pl.program_id(0)

        @pl.when(i == 0)
        def _():
            car_r[...] = jnp.zeros_like(car_r)
            car_i[...] = jnp.zeros_like(car_i)

        uv = u_ref[...]
        ub = uv.astype(BF16)
        br = jnp.concatenate([jnp.dot(ub[:, k * cb:(k + 1) * cb], bpr_ref[k], preferred_element_type=F32) for k in range(nblk)], axis=1)
        bi = jnp.concatenate([jnp.dot(ub[:, k * cb:(k + 1) * cb], bpi_ref[k], preferred_element_type=F32) for k in range(nblk)], axis=1)
        ar, ai = ar_ref[...], ai_ref[...]
        pr, pi_ = car_r[SUBLANES - 1:SUBLANES, :], car_i[SUBLANES - 1:SUBLANES, :]
        row = _rows(br.shape)
        br = br + jnp.where(row == 0, ar * pr - ai * pi_, 0.0)
        bi = bi + jnp.where(row == 0, ar * pi_ + ai * pr, 0.0)
        hr, hi = _scan_cplx(br, bi, pr_ref, pi_ref, reverse=False)
        car_r[...] = hr[ts - SUBLANES:]
        car_i[...] = hi[ts - SUBLANES:]
        hr_ref[...] = hr
        hi_ref[...] = hi
        hrb, hib = hr.astype(BF16), hi.astype(BF16)
        y = jnp.concatenate([jnp.dot(hrb[:, k * nb:(k + 1) * nb], cpr_ref[k], preferred_element_type=F32)
                             - jnp.dot(hib[:, k * nb:(k + 1) * nb], cpi_ref[k], preferred_element_type=F32) for k in range(nblk)], axis=1)
        yp = y + d_ref[...] * uv
        yp_ref[...] = yp
        gy_ref[...] = _gelu(yp).astype(BF16)

    full = lambda shape: pl.BlockSpec(shape, lambda i: (0,) * len(shape))
    rc = pl.BlockSpec((ts, c), lambda i: (i, 0))
    rn = pl.BlockSpec((ts, n), lambda i: (i, 0))
    sd = jax.ShapeDtypeStruct
    nt = s // ts
    return _call_with_side(
        body, side, lambda: pl.program_id(0) == 0, lambda: pl.program_id(0) == nt - 1,
        name="s5_fwd", grid=(nt,),
        in_specs=[rc, full(abr.shape), full(abi.shape), full(pw_r.shape), full(pw_i.shape), full(bp_r.shape), full(bp_i.shape),
                  full(cp_r.shape), full(cp_i.shape), full(dvec.shape)],
        out_specs=[rn, rn, rc, rc],
        out_shape=[sd((s, n), F32), sd((s, n), F32), sd((s, c), F32), sd((s, c), BF16)],
        scratch_shapes=[pltpu.VMEM((SUBLANES, n), F32), pltpu.VMEM((SUBLANES, n), F32)],
        args=(u, abr, abi, pw_r, pw_i, bp_r, bp_i, cp_r, cp_i, dvec))


def _s5_bwd(dgy, ypre, u, hr, hi, abr, abi, pw_r, pw_i, bp_r, bp_i, cp_r, cp_i, dvec, ts=128, side=None):
    s, c = u.shape
    n = abr.shape[1]
    nblk, cb, nb = bp_r.shape
    ts = min(ts, s)
    nt = s // ts
    hb = ts // SUBLANES
    tn_dims = (((0,), (0,)), ((), ()))
    nt_dims = (((1,), (1,)), ((), ()))

    def body(dgy_ref, yp_ref, u_ref, hr_ref, hi_ref, hrh_ref, hih_ref, ar_ref, ai_ref, pr_ref, pi_ref, bpr_ref, bpi_ref,
             cpr_ref, cpi_ref, d_ref,
             du_ref, dar_ref, dai_ref, dbr_ref, dbi_ref, dcr_ref, dci_ref, dd_ref, car_r, car_i, npi_ref):
        i = pl.program_id(0)
        time_first = i == nt - 1

        @pl.when(i == 0)
        def _():
            car_r[...] = jnp.zeros_like(car_r)
            car_i[...] = jnp.zeros_like(car_i)
            npi_ref[...] = -pi_ref[...]
            for ref in (dar_ref, dai_ref, dbr_ref, dbi_ref, dcr_ref, dci_ref, dd_ref):
                ref[...] = jnp.zeros_like(ref)

        uv = u_ref[...]
        _, dgel = _gelu_and_grad(yp_ref[...])
        dyv = dgy_ref[...] * dgel
        dd_ref[...] += jnp.sum(dyv * uv, axis=0, keepdims=True)
        dyb = dyv.astype(BF16)
        hr, hi = hr_ref[...], hi_ref[...]
        hrb, hib = hr.astype(BF16), hi.astype(BF16)
        dhr, dhi = [], []
        for k in range(nblk):
            dblk = dyb[:, k * cb:(k + 1) * cb]
            dhr.append(lax.dot_general(dblk, cpr_ref[k], nt_dims, preferred_element_type=F32))
            dhi.append(-lax.dot_general(dblk, cpi_ref[k], nt_dims, preferred_element_type=F32))
            dcr_ref[k] += lax.dot_general(hrb[:, k * nb:(k + 1) * nb], dblk, tn_dims, preferred_element_type=F32)
            dci_ref[k] += lax.dot_general(hib[:, k * nb:(k + 1) * nb], dblk, tn_dims, preferred_element_type=F32)
        dhr = jnp.concatenate(dhr, axis=1)
        dhi = jnp.concatenate(dhi, axis=1)
        ar, ai = ar_ref[...], ai_ref[...]
        nr, ni = car_r[0:1, :], car_i[0:1, :]
        row = _rows(dhr.shape)
        dhr = dhr + jnp.where(row == ts - 1, ar * nr + ai * ni, 0.0)
        dhi = dhi + jnp.where(row == ts - 1, ar * ni - ai * nr, 0.0)
        lr, li = _scan_cplx(dhr, dhi, pr_ref, npi_ref, reverse=True)
        car_r[...] = lr[:SUBLANES]
        car_i[...] = li[:SUBLANES]
        hpr = _shift_down(hr, jnp.where(time_first, 0.0, hrh_ref[...]), 1)
        hpi = _shift_down(hi, jnp.where(time_first, 0.0, hih_ref[...]), 1)
        dar_ref[...] += jnp.sum(lr * hpr + li * hpi, axis=0, keepdims=True)
        dai_ref[...] += jnp.sum(li * hpr - lr * hpi, axis=0, keepdims=True)
        lrb, lib = lr.astype(BF16), li.astype(BF16)
        ub = uv.astype(BF16)
        du = []
        for k in range(nblk):
            ublk = ub[:, k * cb:(k + 1) * cb]
            lrk, lik = lrb[:, k * nb:(k + 1) * nb], lib[:, k * nb:(k + 1) * nb]
            dbr_ref[k] += lax.dot_general(ublk, lrk, tn_dims, preferred_element_type=F32)
            dbi_ref[k] += lax.dot_general(ublk, lik, tn_dims, preferred_element_type=F32)
            du.append(lax.dot_general(lrk, bpr_ref[k], nt_dims, preferred_element_type=F32)
                      + lax.dot_general(lik, bpi_ref[k], nt_dims, preferred_element_type=F32))
        du_ref[...] = (d_ref[...] * dyv + jnp.concatenate(du, axis=1)).astype(BF16)

    full = lambda shape: pl.BlockSpec(shape, lambda i: (0,) * len(shape))
    rev = lambda i: nt - 1 - i
    halo_idx = lambda i: jnp.maximum(rev(i) * hb - 1, 0)
    rc = pl.BlockSpec((ts, c), lambda i: (rev(i), 0))
    rn = pl.BlockSpec((ts, n), lambda i: (rev(i), 0))
    hn = pl.BlockSpec((SUBLANES, n), lambda i: (halo_idx(i), 0))
    sd = jax.ShapeDtypeStruct
    return _call_with_side(
        body, side, lambda: pl.program_id(0) == 0, lambda: pl.program_id(0) == nt - 1,
        name="s5_bwd", grid=(nt,),
        in_specs=[rc, rc, rc, rn, rn, hn, hn, full(abr.shape), full(abi.shape), full(pw_r.shape), full(pw_i.shape),
                  full(bp_r.shape), full(bp_i.shape), full(cp_r.shape), full(cp_i.shape), full(dvec.shape)],
        out_specs=[rc, full(abr.shape), full(abi.shape), full(bp_r.shape), full(bp_i.shape), full(cp_r.shape), full(cp_i.shape),
                   full(dvec.shape)],
        out_shape=[sd((s, c), BF16), sd(abr.shape, F32), sd(abi.shape, F32), sd(bp_r.shape, F32), sd(bp_i.shape, F32),
                   sd(cp_r.shape, F32), sd(cp_i.shape, F32), sd(dvec.shape, F32)],
        scratch_shapes=[pltpu.VMEM((SUBLANES, n), F32), pltpu.VMEM((SUBLANES, n), F32), pltpu.VMEM(pw_i.shape, F32)],
        args=(dgy, ypre, u, hr, hi, hr, hi, abr, abi, pw_r, pw_i, bp_r, bp_i, cp_r, cp_i, dvec))


S5_LANE_CHUNK = 512


def _s5_tables(a_re, a_im, log_dt, bt_re, bt_im):
    g, p = a_re.shape
    gc = bt_re.shape[0]

    def body(ar_ref, ai_ref, dt_ref, br_ref, bi_ref, abr_ref, abi_ref, tr_ref, ti_ref, bbr_ref, bbi_ref):
        abr, abi, bbr, bbi = _s5_param_fn(ar_ref[...], ai_ref[...], dt_ref[...], br_ref[...], bi_ref[...])
        abr_ref[...] = abr
        abi_ref[...] = abi
        bbr_ref[...] = bbr
        bbi_ref[...] = bbi
        pows = [(abr, abi)]
        for _ in range(1, SUBLANES):
            qr, qi = pows[-1]
            pows.append((qr * abr - qi * abi, qr * abi + qi * abr))
        zero = jnp.zeros_like(abr)
        for r in range(SUBLANES):
            for k in range(3):
                sh = 1 << k
                tr_ref[k, r] = pows[sh - 1][0] if r >= sh else zero
                ti_ref[k, r] = pows[sh - 1][1] if r >= sh else zero
            tr_ref[3, r] = pows[r][0]
            ti_ref[3, r] = pows[r][1]

    sd = jax.ShapeDtypeStruct
    return pl.pallas_call(
        body, name="s5_tables",
        out_shape=[sd((g, p), F32), sd((g, p), F32), sd((4, SUBLANES, g, p), F32), sd((4, SUBLANES, g, p), F32),
                   sd((gc, g, p), F32), sd((gc, g, p), F32)],
    )(a_re, a_im, log_dt, bt_re, bt_im)


def _cmul_add(br, bi, tr, ti, sr, si):
    return br + tr * sr - ti * si, bi + tr * si + ti * sr


def _s5_fwd2(u, tab_r, tab_i, bp_r, bp_i, cp_r, cp_i, dvec, ts=256, side=None):
    s, c = u.shape
    n = tab_r.shape[2]
    nblk, cb, nb = bp_r.shape
    ts = min(ts, s)
    nsl = ts // SUBLANES
    lc = min(S5_LANE_CHUNK, n)

    def body(u_ref, tr_ref, ti_ref, bpr_ref, bpi_ref, cpr_ref, cpi_ref, d_ref, hr_ref, hi_ref, yp_ref, gy_ref,
             bur_ref, bui_ref, car_r, car_i):
        i = pl.program_id(0)

        @pl.when(i == 0)
        def _():
            car_r[...] = jnp.zeros_like(car_r)
            car_i[...] = jnp.zeros_like(car_i)

        uv = u_ref[...]
        ub = uv.astype(BF16)
        for k in range(nblk):
            bur_ref[:, k * nb:(k + 1) * nb] = jnp.dot(ub[:, k * cb:(k + 1) * cb], bpr_ref[k], preferred_element_type=F32)
            bui_ref[:, k * nb:(k + 1) * nb] = jnp.dot(ub[:, k * cb:(k + 1) * cb], bpi_ref[k], preferred_element_type=F32)
        for q in range(n // lc):
            sl = slice(q * lc, (q + 1) * lc)
            tabs = [(tr_ref[k, :, sl], ti_ref[k, :, sl]) for k in range(4)]

            def slab(j, carry, sl=sl, tabs=tabs):
                cr, ci = carry
                r0 = pl.multiple_of(j * SUBLANES, SUBLANES)
                br, bi = bur_ref[pl.ds(r0, SUBLANES), sl], bui_ref[pl.ds(r0, SUBLANES), sl]
                for k in range(3):
                    sh = 1 << k
                    br, bi = _cmul_add(br, bi, tabs[k][0], tabs[k][1], pltpu.roll(br, sh, 0), pltpu.roll(bi, sh, 0))
                hr, hi = _cmul_add(br, bi, tabs[3][0], tabs[3][1], jnp.broadcast_to(cr, br.shape), jnp.broadcast_to(ci, bi.shape))
                hr_ref[pl.ds(r0, SUBLANES), sl] = hr
                hi_ref[pl.ds(r0, SUBLANES), sl] = hi
                return hr[SUBLANES - 1:, :], hi[SUBLANES - 1:, :]

            cr, ci = lax.fori_loop(0, nsl, slab, (car_r[:, sl], car_i[:, sl]), unroll=2)
            car_r[:, sl] = cr
            car_i[:, sl] = ci
        hrb, hib = hr_ref[...].astype(BF16), hi_ref[...].astype(BF16)
        y = jnp.concatenate([jnp.dot(hrb[:, k * nb:(k + 1) * nb], cpr_ref[k], preferred_element_type=F32)
                             - jnp.dot(hib[:, k * nb:(k + 1) * nb], cpi_ref[k], preferred_element_type=F32) for k in range(nblk)], axis=1)
        yp = y + d_ref[...] * uv
        yp_ref[...] = yp
        gy_ref[...] = _gelu(yp).astype(BF16)

    full = lambda shape: pl.BlockSpec(shape, lambda i: (0,) * len(shape))
    rc = pl.BlockSpec((ts, c), lambda i: (i, 0))
    rn = pl.BlockSpec((ts, n), lambda i: (i, 0))
    sd = jax.ShapeDtypeStruct
    nt = s // ts
    return _call_with_side(
        body, side, lambda: pl.program_id(0) == 0, lambda: pl.program_id(0) == nt - 1,
        name="s5_fwd", grid=(nt,),
        in_specs=[rc, full(tab_r.shape), full(tab_i.shape), full(bp_r.shape), full(bp_i.shape), full(cp_r.shape), full(cp_i.shape),
                  full(dvec.shape)],
        out_specs=[rn, rn, rc, rc],
        out_shape=[sd((s, n), F32), sd((s, n), F32), sd((s, c), F32), sd((s, c), BF16)],
        scratch_shapes=[pltpu.VMEM((ts, n), F32), pltpu.VMEM((ts, n), F32), pltpu.VMEM((1, n), F32), pltpu.VMEM((1, n), F32)],
        args=(u, tab_r, tab_i, bp_r, bp_i, cp_r, cp_i, dvec))


def _s5_bwd2(dgy, ypre, u, hr, hi, rtab_r, rtab_i, bp_r, bp_i, cp_r, cp_i, dvec, ts=256, side=None):
    s, c = u.shape
    n = rtab_r.shape[2]
    nblk, cb, nb = bp_r.shape
    ts = min(ts, s)
    nt = s // ts
    hb = ts // SUBLANES
    nsl = ts // SUBLANES
    lc = min(S5_LANE_CHUNK, n)
    tn_dims = (((0,), (0,)), ((), ()))
    nt_dims = (((1,), (1,)), ((), ()))

    def body(dgy_ref, yp_ref, u_ref, hr_ref, hi_ref, hrh_ref, hih_ref, tr_ref, ti_ref, bpr_ref, bpi_ref, cpr_ref, cpi_ref, d_ref,
             du_ref, dar_ref, dai_ref, dbr_ref, dbi_ref, dcr_ref, dci_ref, dd_ref, lr_ref, li_ref, car_r, car_i):
        i = pl.program_id(0)
        time_first = i == nt - 1

        @pl.when(i == 0)
        def _():
            car_r[...] = jnp.zeros_like(car_r)
            car_i[...] = jnp.zeros_like(car_i)
            for ref in (dar_ref, dai_ref, dbr_ref, dbi_ref, dcr_ref, dci_ref, dd_ref):
                ref[...] = jnp.zeros_like(ref)

        uv = u_ref[...]
        _, dgel = _gelu_and_grad(yp_ref[...])
        dyv = dgy_ref[...] * dgel
        dd_ref[...] += jnp.sum(dyv * uv, axis=0, keepdims=True)
        dyb = dyv.astype(BF16)
        hrb, hib = hr_ref[...].astype(BF16), hi_ref[...].astype(BF16)
        for k in range(nblk):
            dblk = dyb[:, k * cb:(k + 1) * cb]
            lr_ref[:, k * nb:(k + 1) * nb] = lax.dot_general(dblk, cpr_ref[k], nt_dims, preferred_element_type=F32)
            li_ref[:, k * nb:(k + 1) * nb] = -lax.dot_general(dblk, cpi_ref[k], nt_dims, preferred_element_type=F32)
            dcr_ref[k] += lax.dot_general(hrb[:, k * nb:(k + 1) * nb], dblk, tn_dims, preferred_element_type=F32)
            dci_ref[k] += lax.dot_general(hib[:, k * nb:(k + 1) * nb], dblk, tn_dims, preferred_element_type=F32)
        row8 = _rows((SUBLANES, lc))
        for q in range(n // lc):
            sl = slice(q * lc, (q + 1) * lc)
            tabs = [(tr_ref[k, :, sl], ti_ref[k, :, sl]) for k in range(4)]
            halo_r = jnp.where(time_first, 0.0, hrh_ref[SUBLANES - 1:, sl])
            halo_i = jnp.where(time_first, 0.0, hih_ref[SUBLANES - 1:, sl])

            def slab(jj, carry, sl=sl, tabs=tabs, halo_r=halo_r, halo_i=halo_i):
                nr, ni, acc_r, acc_i = carry
                j = nsl - 1 - jj
                r0 = pl.multiple_of(j * SUBLANES, SUBLANES)
                br, bi = lr_ref[pl.ds(r0, SUBLANES), sl], li_ref[pl.ds(r0, SUBLANES), sl]
                for k in range(3):
                    sh = 1 << k
                    br, bi = _cmul_add(br, bi, tabs[k][0], tabs[k][1], pltpu.roll(br, SUBLANES - sh, 0),
                                       pltpu.roll(bi, SUBLANES - sh, 0))
                lr, li = _cmul_add(br, bi, tabs[3][0], tabs[3][1], jnp.broadcast_to(nr, br.shape), jnp.broadcast_to(ni, bi.shape))
                lr_ref[pl.ds(r0, SUBLANES), sl] = lr
                li_ref[pl.ds(r0, SUBLANES), sl] = li
                p0 = pl.multiple_of(jnp.maximum(j - 1, 0) * SUBLANES, SUBLANES)
                prev_r = jnp.where(j == 0, halo_r, hr_ref[pl.ds(p0, SUBLANES), sl][SUBLANES - 1:, :])
                prev_i = jnp.where(j == 0, halo_i, hi_ref[pl.ds(p0, SUBLANES), sl][SUBLANES - 1:, :])
                hpr = jnp.where(row8 == 0, jnp.broadcast_to(prev_r, br.shape), pltpu.roll(hr_ref[pl.ds(r0, SUBLANES), sl], 1, 0))
                hpi = jnp.where(row8 == 0, jnp.broadcast_to(prev_i, bi.shape), pltpu.roll(hi_ref[pl.ds(r0, SUBLANES), sl], 1, 0))
                return lr[:1, :], li[:1, :], acc_r + (lr * hpr + li * hpi), acc_i + (li * hpr - lr * hpi)

            zero = jnp.zeros((SUBLANES, lc), F32)
            nr, ni, acc_r, acc_i = lax.fori_loop(0, nsl, slab, (car_r[:, sl], car_i[:, sl], zero, zero), unroll=2)
            car_r[:, sl] = nr
            car_i[:, sl] = ni
            dar_ref[:, sl] += jnp.sum(acc_r, axis=0, keepdims=True)
            dai_ref[:, sl] += jnp.sum(acc_i, axis=0, keepdims=True)
        lrb, lib = lr_ref[...].astype(BF16), li_ref[...].astype(BF16)
        ub = uv.astype(BF16)
        du = []
        for k in range(nblk):
            ublk = ub[:, k * cb:(k + 1) * cb]
            lrk, lik = lrb[:, k * nb:(k + 1) * nb], lib[:, k * nb:(k + 1) * nb]
            dbr_ref[k] += lax.dot_general(ublk, lrk, tn_dims, preferred_element_type=F32)
            dbi_ref[k] += lax.dot_general(ublk, lik, tn_dims, preferred_element_type=F32)
            du.append(lax.dot_general(lrk, bpr_ref[k], nt_dims, preferred_element_type=F32)
                      + lax.dot_general(lik, bpi_ref[k], nt_dims, preferred_element_type=F32))
        du_ref[...] = (d_ref[...] * dyv + jnp.concatenate(du, axis=1)).astype(BF16)

    full = lambda shape: pl.BlockSpec(shape, lambda i: (0,) * len(shape))
    rev = lambda i: nt - 1 - i
    halo_idx = lambda i: jnp.maximum(rev(i) * hb - 1, 0)
    rc = pl.BlockSpec((ts, c), lambda i: (rev(i), 0))
    rn = pl.BlockSpec((ts, n), lambda i: (rev(i), 0))
    hn = pl.BlockSpec((SUBLANES, n), lambda i: (halo_idx(i), 0))
    sd = jax.ShapeDtypeStruct
    vec_n = (1, n)
    return _call_with_side(
        body, side, lambda: pl.program_id(0) == 0, lambda: pl.program_id(0) == nt - 1,
        name="s5_bwd", grid=(nt,),
        in_specs=[rc, rc, rc, rn, rn, hn, hn, full(rtab_r.shape), full(rtab_i.shape),
                  full(bp_r.shape), full(bp_i.shape), full(cp_r.shape), full(cp_i.shape), full(dvec.shape)],
        out_specs=[rc, full(vec_n), full(vec_n), full(bp_r.shape), full(bp_i.shape), full(cp_r.shape), full(cp_i.shape),
                   full(dvec.shape)],
        out_shape=[sd((s, c), BF16), sd(vec_n, F32), sd(vec_n, F32), sd(bp_r.shape, F32), sd(bp_i.shape, F32),
                   sd(cp_r.shape, F32), sd(cp_i.shape, F32), sd(dvec.shape, F32)],
        scratch_shapes=[pltpu.VMEM((ts, n), F32), pltpu.VMEM((ts, n), F32), pltpu.VMEM((1, n), F32), pltpu.VMEM((1, n), F32)],
        args=(dgy, ypre, u, hr, hi, hr, hi, rtab_r, rtab_i, bp_r, bp_i, cp_r, cp_i, dvec))


def _glu(gl2, ts=512):
    _, s, c = gl2.shape
    ts = min(ts, s)

    def body(g_ref, o_ref):
        o_ref[...] = (g_ref[0] * _sigmoid(g_ref[1])).astype(BF16)

    return pl.pallas_call(
        body, name="glu", grid=(s // ts,), in_specs=[pl.BlockSpec((2, ts, c), lambda i: (0, i, 0))],
        out_specs=pl.BlockSpec((ts, c), lambda i: (i, 0)), out_shape=jax.ShapeDtypeStruct((s, c), BF16), compiler_params=_cparams(),
    )(gl2)


def _glu_bwd(gl2, d_o, ts=512):
    _, s, c = gl2.shape
    ts = min(ts, s)

    def body(g_ref, do_ref, o_ref):
        sg = _sigmoid(g_ref[1])
        dov = do_ref[...]
        o_ref[0] = (dov * sg).astype(BF16)
        o_ref[1] = (dov * g_ref[0] * sg * (1.0 - sg)).astype(BF16)

    blk = pl.BlockSpec((2, ts, c), lambda i: (0, i, 0))
    return pl.pallas_call(
        body, name="glu_bwd", grid=(s // ts,), in_specs=[blk, pl.BlockSpec((ts, c), lambda i: (i, 0))],
        out_specs=blk, out_shape=jax.ShapeDtypeStruct((2, s, c), BF16), compiler_params=_cparams(),
    )(gl2, d_o)


PACK_ROW_MULTIPLE = 1024
ELEMENTWISE_BLOCK_ELEMS = 256 * 1024


def _row_tile(rows, cols):
    pref = max(SUBLANES, 1 << int(math.log2(max(1, ELEMENTWISE_BLOCK_ELEMS // cols))))
    if rows <= pref:
        return rows
    t = pref
    while rows % t:
        t //= 2
    assert t >= SUBLANES, rows
    return t


def _sum_parts(rs):
    nl = len(rs)
    p, rows, cols = rs[0].shape
    tr = _row_tile(rows, cols)

    def body(*refs):
        o_ref = refs[nl]
        for l in range(nl):
            acc = refs[l][0].astype(F32)
            for k in range(1, p):
                acc = acc + refs[l][k].astype(F32)
            o_ref[l] = acc

    return pl.pallas_call(
        body, name="sum_parts", grid=(rows // tr,), in_specs=[pl.BlockSpec((p, tr, cols), lambda i: (0, i, 0))] * nl,
        out_specs=pl.BlockSpec((nl, tr, cols), lambda i: (0, i, 0)), out_shape=jax.ShapeDtypeStruct((nl, rows, cols), F32),
        compiler_params=_cparams(),
    )(*rs)


def _adamw(w, g_parts, m, v):
    rows, cols = w.shape
    tr = _row_tile(rows, cols)
    ng = len(g_parts)
    c1 = 1.0 / (1.0 - ADAM_B1 ** ADAM_STEP)
    c2 = 1.0 / (1.0 - ADAM_B2 ** ADAM_STEP)

    def body(*refs):
        w_ref, m_ref, v_ref = refs[0], refs[1 + ng], refs[2 + ng]
        g_ref, dl_ref, nm_ref, nv_ref = refs[3 + ng:]
        g = refs[1][...]
        for k in range(1, ng):
            g = g + refs[1 + k][...]
        mn = ADAM_B1 * m_ref[...] + (1.0 - ADAM_B1) * g
        vn = ADAM_B2 * v_ref[...] + (1.0 - ADAM_B2) * (g * g)
        g_ref[...] = g
        nm_ref[...] = mn
        nv_ref[...] = vn
        dl_ref[...] = -ADAM_LR * ((mn * c1) / (jnp.sqrt(vn * c2) + ADAM_EPS) + ADAM_WD * w_ref[...])

    blk = pl.BlockSpec((tr, cols), lambda i: (i, 0))
    sd = jax.ShapeDtypeStruct((rows, cols), F32)
    return pl.pallas_call(
        body, name="adamw", grid=(rows // tr,), in_specs=[blk] * (3 + ng), out_specs=[blk] * 4, out_shape=[sd] * 4,
        compiler_params=_cparams(),
    )(w, *g_parts, m, v)


def _place():
    x, y, c = lax.axis_index("x"), lax.axis_index("y"), lax.axis_index("c")
    chips = [(1 - x, y), (x, 1 - y), (1 - x, 1 - y)]
    return x, y, c, chips


class Side:
    def __init__(self, ins, outs, kind):
        self.ins, self.outs, self.kind = list(ins), list(outs), kind
        n = len(self.ins)
        self.sems = [pltpu.SemaphoreType.DMA((3 * n,)), pltpu.SemaphoreType.DMA((3 * n,)), pltpu.SemaphoreType.DMA((n,))]

    def _copies(self, ins, outs, send, recv, lsem):
        x, y, c, chips = _place()
        me = 2 * x + y
        local, out_going, in_coming = [], [], []
        for t in range(len(ins)):
            if self.kind == 'gather':
                src_local, srcs, dst_mine = ins[t], [ins[t]] * 3, outs[t].at[me]
            else:
                src_local, srcs, dst_mine = ins[t].at[me], [ins[t].at[2 * px + py] for px, py in chips], outs[t].at[me]
            local.append(pltpu.make_async_copy(src_local, dst_mine, lsem.at[t]))
            for r, (px, py) in enumerate(chips):
                out_going.append(pltpu.make_async_remote_copy(
                    src_ref=srcs[r], dst_ref=dst_mine, send_sem=send.at[3 * t + r], recv_sem=recv.at[3 * t + r],
                    device_id=(px, py, c), device_id_type=MESH))
                in_coming.append(pltpu.make_async_remote_copy(
                    src_ref=srcs[r], dst_ref=outs[t].at[2 * px + py], send_sem=send.at[3 * t + r], recv_sem=recv.at[3 * t + r],
                    device_id=(px, py, c), device_id_type=MESH))
        return local, out_going, in_coming

    def start(self, ins, outs, send, recv, lsem):
        local, out_going, _ = self._copies(ins, outs, send, recv, lsem)
        for cp in local + out_going:
            cp.start()

    def wait(self, ins, outs, send, recv, lsem):
        local, out_going, in_coming = self._copies(ins, outs, send, recv, lsem)
        for cp in in_coming:
            cp.wait_recv()
        for cp in out_going:
            cp.wait_send()
        for cp in local:
            cp.wait()


def _gather_side(shards):
    return Side(shards, [jax.ShapeDtypeStruct((N_CHIPS,) + s.shape, s.dtype) for s in shards], 'gather')


def _scatter_side(grads):
    return Side(grads, [jax.ShapeDtypeStruct(g.shape, g.dtype) for g in grads], 'scatter')


def _call_with_side(body, side, first, last, *, name, grid, in_specs, out_specs, out_shape, scratch_shapes, args):
    if side is None:
        outs = pl.pallas_call(body, name=name, grid=grid, in_specs=in_specs, out_specs=out_specs, out_shape=out_shape,
                              scratch_shapes=scratch_shapes, compiler_params=_cparams())(*args)
        return outs, []
    n_in, n_out, n_sc = len(in_specs), len(out_specs), len(scratch_shapes)
    ns_in, ns_out = len(side.ins), len(side.outs)

    def wrapped(*refs):
        base_in, s_in = refs[:n_in], refs[n_in:n_in + ns_in]
        o0 = n_in + ns_in
        base_out, s_out = refs[o0:o0 + n_out], refs[o0 + n_out:o0 + n_out + ns_out]
        sc0 = o0 + n_out + ns_out
        base_sc, sems = refs[sc0:sc0 + n_sc], refs[sc0 + n_sc:]

        @pl.when(first())
        def _():
            side.start(s_in, s_out, *sems)

        body(*base_in, *base_out, *base_sc)

        @pl.when(last())
        def _():
            side.wait(s_in, s_out, *sems)

    any_spec = pl.BlockSpec(memory_space=pl.ANY)
    outs = pl.pallas_call(
        wrapped, name=name, grid=grid, in_specs=list(in_specs) + [any_spec] * ns_in, out_specs=list(out_specs) + [any_spec] * ns_out,
        out_shape=list(out_shape) + side.outs, scratch_shapes=list(scratch_shapes) + side.sems, compiler_params=_cparams(),
    )(*args, *side.ins)
    return outs[:n_out], outs[n_out:]


def _run_side(name, side):
    def body(*refs):
        n = len(side.ins)
        side.start(refs[:n], refs[n:2 * n], *refs[2 * n:])
        side.wait(refs[:n], refs[n:2 * n], *refs[2 * n:])

    any_spec = pl.BlockSpec(memory_space=pl.ANY)
    return pl.pallas_call(body, name=name, in_specs=[any_spec] * len(side.ins), out_specs=[any_spec] * len(side.outs),
                          out_shape=side.outs, scratch_shapes=side.sems)(*side.ins)


def _gather_shards(shards, layer_major):
    n = len(shards)

    def body(*refs):
        ins, outs = refs[:n], refs[n:2 * n]
        send, recv, lsem = refs[2 * n:]
        x, y, c, chips = _place()
        me = 2 * x + y

        def slot(t, chip):
            return outs[t].at[:, chip] if layer_major[t] else outs[t].at[chip]

        local, sends = [], []
        for t in range(n):
            cp = pltpu.make_async_copy(ins[t], slot(t, me), lsem.at[t])
            cp.start()
            local.append(cp)
            for r, (px, py) in enumerate(chips):
                rc = pltpu.make_async_remote_copy(src_ref=ins[t], dst_ref=slot(t, me), send_sem=send.at[3 * t + r],
                                                  recv_sem=recv.at[3 * t + r], device_id=(px, py, c), device_id_type=MESH)
                rc.start()
                sends.append(rc)
        for t in range(n):
            for r, (px, py) in enumerate(chips):
                pltpu.make_async_remote_copy(src_ref=ins[t], dst_ref=slot(t, 2 * px + py), send_sem=send.at[3 * t + r],
                                             recv_sem=recv.at[3 * t + r], device_id=(px, py, c), device_id_type=MESH).wait_recv()
        for rc in sends:
            rc.wait_send()
        for cp in local:
            cp.wait()

    any_spec = pl.BlockSpec(memory_space=pl.ANY)
    return pl.pallas_call(
        body, name="gather_shards", in_specs=[any_spec] * n, out_specs=[any_spec] * n,
        out_shape=[jax.ShapeDtypeStruct((s.shape[0], N_CHIPS) + s.shape[1:] if lm else (N_CHIPS,) + s.shape, s.dtype)
                   for s, lm in zip(shards, layer_major)],
        scratch_shapes=[pltpu.SemaphoreType.DMA((3 * n,)), pltpu.SemaphoreType.DMA((3 * n,)), pltpu.SemaphoreType.DMA((n,))],
    )(*shards)


def _scatter_grads(groups):
    flat = [(gi, li, a) for gi, grp in enumerate(groups) for li, a in enumerate(grp)]
    n = len(flat)
    ng = len(groups)

    def body(*refs):
        ins, outs = refs[:n], refs[n:n + ng]
        send, recv, lsem = refs[n + ng:]
        x, y, c, chips = _place()
        me = 2 * x + y
        local, sends = [], []
        for t, (gi, li, _) in enumerate(flat):
            cp = pltpu.make_async_copy(ins[t].at[me], outs[gi].at[me, li], lsem.at[t])
            cp.start()
            local.append(cp)
            for r, (px, py) in enumerate(chips):
                rc = pltpu.make_async_remote_copy(src_ref=ins[t].at[2 * px + py], dst_ref=outs[gi].at[me, li],
                                                  send_sem=send.at[3 * t + r], recv_sem=recv.at[3 * t + r],
                                                  device_id=(px, py, c), device_id_type=MESH)
                rc.start()
                sends.append(rc)
        for t, (gi, li, _) in enumerate(flat):
            for r, (px, py) in enumerate(chips):
                pltpu.make_async_remote_copy(src_ref=ins[t].at[me], dst_ref=outs[gi].at[2 * px + py, li],
                                             send_sem=send.at[3 * t + r], recv_sem=recv.at[3 * t + r],
                                             device_id=(px, py, c), device_id_type=MESH).wait_recv()
        for rc in sends:
            rc.wait_send()
        for cp in local:
            cp.wait()

    any_spec = pl.BlockSpec(memory_space=pl.ANY)
    return pl.pallas_call(
        body, name="scatter_grads", in_specs=[any_spec] * n, out_specs=[any_spec] * ng,
        out_shape=[jax.ShapeDtypeStruct((N_CHIPS, len(grp)) + grp[0].shape[1:], grp[0].dtype) for grp in groups],
        scratch_shapes=[pltpu.SemaphoreType.DMA((3 * n,)), pltpu.SemaphoreType.DMA((3 * n,)), pltpu.SemaphoreType.DMA((n,))],
    )(*[a for _, _, a in flat])


def _swap_with_sibling(arrs):
    n = len(arrs)

    def body(*refs):
        ins, outs = refs[:n], refs[n:2 * n]
        send, recv = refs[2 * n:]
        x, y, c, _ = _place()
        cps = []
        for t in range(n):
            rc = pltpu.make_async_remote_copy(src_ref=ins[t], dst_ref=outs[t], send_sem=send.at[t], recv_sem=recv.at[t],
                                              device_id=(x, y, 1 - c), device_id_type=MESH)
            rc.start()
            cps.append(rc)
        for rc in cps:
            rc.wait_recv()
        for rc in cps:
            rc.wait_send()

    any_spec = pl.BlockSpec(memory_space=pl.ANY)
    return pl.pallas_call(
        body, name="swap_with_sibling", in_specs=[any_spec] * n, out_specs=[any_spec] * n,
        out_shape=[jax.ShapeDtypeStruct(a.shape, a.dtype) for a in arrs],
        scratch_shapes=[pltpu.SemaphoreType.DMA((n,)), pltpu.SemaphoreType.DMA((n,))],
    )(*arrs)


def _allreduce_small(v):
    rows, cols = v.shape

    def body(v_ref, o_ref, sib_ref, chip_ref, send, recv):
        x, y, c, chips = _place()
        me = 2 * x + y
        d2d = pltpu.make_async_remote_copy(src_ref=v_ref, dst_ref=sib_ref, send_sem=send.at[0], recv_sem=recv.at[0],
                                           device_id=(x, y, 1 - c), device_id_type=MESH)
        d2d.start()
        d2d.wait_recv()
        chip_ref[me] = v_ref[...] + sib_ref[...]
        sends = []
        for r, (px, py) in enumerate(chips):
            rc = pltpu.make_async_remote_copy(src_ref=chip_ref.at[me], dst_ref=chip_ref.at[me], send_sem=send.at[1 + r],
                                              recv_sem=recv.at[1 + r], device_id=(px, py, c), device_id_type=MESH)
            rc.start()
            sends.append(rc)
        for r, (px, py) in enumerate(chips):
            pltpu.make_async_remote_copy(src_ref=chip_ref.at[me], dst_ref=chip_ref.at[2 * px + py], send_sem=send.at[1 + r],
                                         recv_sem=recv.at[1 + r], device_id=(px, py, c), device_id_type=MESH).wait_recv()
        o_ref[...] = (chip_ref[0] + chip_ref[1]) + (chip_ref[2] + chip_ref[3])
        d2d.wait_send()
        for rc in sends:
            rc.wait_send()

    vm = pl.BlockSpec(memory_space=pltpu.VMEM)
    return pl.pallas_call(
        body, name="allreduce_small", in_specs=[vm], out_specs=vm, out_shape=jax.ShapeDtypeStruct((rows, cols), F32),
        scratch_shapes=[pltpu.VMEM((rows, cols), F32), pltpu.VMEM((N_CHIPS, rows, cols), F32), pltpu.SemaphoreType.DMA((4,)),
                        pltpu.SemaphoreType.DMA((4,))],
        compiler_params=_cparams(),
    )(v)


def _pack(tensors):
    pieces = []
    for t in tensors:
        flat = t.reshape(-1)
        pad = (-flat.shape[0]) % (SUBLANES * LANES)
        pieces.append(jnp.pad(flat, (0, pad)).reshape(-1, LANES))
    rows = sum(p.shape[0] for p in pieces)
    pieces.append(jnp.zeros(((-rows) % PACK_ROW_MULTIPLE, LANES), tensors[0].dtype))
    return jnp.concatenate(pieces, axis=0)


def _unpack(buf, like):
    out, off = [], 0
    for t in like:
        size = math.prod(t.shape)
        rows = -(-size // (SUBLANES * LANES)) * SUBLANES
        out.append(buf[off:off + rows].reshape(-1)[:size].reshape(t.shape))
        off += rows
    return out


def _s5_pack_b(bb):
    gc, g, p = bb.shape
    q = S5_GROUPS_PER_BLOCK
    t = bb.reshape(gc, g // q, q, p).transpose(1, 2, 0, 3)
    eye = jnp.eye(q, dtype=bb.dtype)
    return (t[:, :, :, None, :] * eye[None, :, None, :, None]).reshape(g // q, q * gc, q * p)


def _s5_unpack_b(dbp, gc, p):
    nb = dbp.shape[0]
    q = S5_GROUPS_PER_BLOCK
    eye = jnp.eye(q, dtype=dbp.dtype)
    t = (dbp.reshape(nb, q, gc, q, p) * eye[None, :, None, :, None]).sum(axis=3)
    return t.transpose(2, 0, 1, 3).reshape(gc, nb * q, p)


def _s5_pack_c(cc):
    g, gc, p = cc.shape
    q = S5_GROUPS_PER_BLOCK
    t = cc.reshape(g // q, q, gc, p).transpose(0, 1, 3, 2)
    eye = jnp.eye(q, dtype=cc.dtype)
    return (t[:, :, :, None, :] * eye[None, :, None, :, None]).reshape(g // q, q * p, q * gc)


def _s5_unpack_c(dcp, gc, p):
    nb = dcp.shape[0]
    q = S5_GROUPS_PER_BLOCK
    eye = jnp.eye(q, dtype=dcp.dtype)
    t = (dcp.reshape(nb, q, p, q, gc) * eye[None, :, None, :, None]).sum(axis=3)
    return t.transpose(0, 1, 3, 2).reshape(nb * q, gc, p)


def _split2(m):
    return m.arr[:, 0]


def kernel(x, norm_mix_g, norm_ffn_g, norm_final_g, rg_w_in, rg_conv_w, rg_conv_b, rg_w_a, rg_b_a, rg_w_x, rg_b_x, rg_lambda, rg_w_out, s5_w_in, s5_a_re, s5_a_im, s5_log_dt, s5_b_re, s5_b_im, s5_c_re, s5_c_im, s5_d, s5_w_glu, s5_w_out, ffn_w_up, ffn_conv_w, ffn_conv_b, ffn_w_down, loss_target, m_norm_mix_g, m_norm_ffn_g, m_norm_final_g, m_rg_w_in, m_rg_conv_w, m_rg_conv_b, m_rg_w_a, m_rg_b_a, m_rg_w_x, m_rg_b_x, m_rg_lambda, m_rg_w_out, m_s5_w_in, m_s5_a_re, m_s5_a_im, m_s5_log_dt, m_s5_b_re, m_s5_b_im, m_s5_c_re, m_s5_c_im, m_s5_d, m_s5_w_glu, m_s5_w_out, m_ffn_w_up, m_ffn_conv_w, m_ffn_conv_b, m_ffn_w_down, v_norm_mix_g, v_norm_ffn_g, v_norm_final_g, v_rg_w_in, v_rg_conv_w, v_rg_conv_b, v_rg_w_a, v_rg_b_a, v_rg_w_x, v_rg_b_x, v_rg_lambda, v_rg_w_out, v_s5_w_in, v_s5_a_re, v_s5_a_im, v_s5_log_dt, v_s5_b_re, v_s5_b_im, v_s5_c_re, v_s5_c_im, v_s5_d, v_s5_w_glu, v_s5_w_out, v_ffn_w_up, v_ffn_conv_w, v_ffn_conv_b, v_ffn_w_down):
    w = dict(zip(PARAM_NAMES, (norm_mix_g, norm_ffn_g, norm_final_g, rg_w_in, rg_conv_w, rg_conv_b, rg_w_a, rg_b_a, rg_w_x, rg_b_x,
                               rg_lambda, rg_w_out, s5_w_in, s5_a_re, s5_a_im, s5_log_dt, s5_b_re, s5_b_im, s5_c_re, s5_c_im, s5_d,
                               s5_w_glu, s5_w_out, ffn_w_up, ffn_conv_w, ffn_conv_b, ffn_w_down)))
    mom = dict(zip(PARAM_NAMES, (m_norm_mix_g, m_norm_ffn_g, m_norm_final_g, m_rg_w_in, m_rg_conv_w, m_rg_conv_b, m_rg_w_a, m_rg_b_a,
                                 m_rg_w_x, m_rg_b_x, m_rg_lambda, m_rg_w_out, m_s5_w_in, m_s5_a_re, m_s5_a_im, m_s5_log_dt, m_s5_b_re,
                                 m_s5_b_im, m_s5_c_re, m_s5_c_im, m_s5_d, m_s5_w_glu, m_s5_w_out, m_ffn_w_up, m_ffn_conv_w,
                                 m_ffn_conv_b, m_ffn_w_down)))
    vel = dict(zip(PARAM_NAMES, (v_norm_mix_g, v_norm_ffn_g, v_norm_final_g, v_rg_w_in, v_rg_conv_w, v_rg_conv_b, v_rg_w_a, v_rg_b_a,
                                 v_rg_w_x, v_rg_b_x, v_rg_lambda, v_rg_w_out, v_s5_w_in, v_s5_a_re, v_s5_a_im, v_s5_log_dt, v_s5_b_re,
                                 v_s5_b_im, v_s5_c_re, v_s5_c_im, v_s5_d, v_s5_w_glu, v_s5_w_out, v_ffn_w_up, v_ffn_conv_w,
                                 v_ffn_conv_b, v_ffn_w_down)))
    _, s, d = x.shape
    depth = norm_mix_g.shape[0]
    n_grp, n_state = s5_a_re.shape[1], s5_a_re.shape[2]
    gc = s5_b_re.shape[3]
    d_ff = ffn_w_down.shape[1] * N_CHIPS
    s5_ts = min(256, s)

    wb = {n: (w[n].astype(BF16) if n in BIG else w[n]) for n in SHARDED}
    gath = {}

    def mixer_keys(i):
        return [(n, i // 2) for n in MIXER_SHARDED[i % 2]] if i < depth else []

    def gather_side(keys):
        return _gather_side([wb[n][l] for n, l in keys])

    def put(keys, arrs):
        for k, a in zip(keys, arrs):
            gath[k] = a

    def wcol(n, l):
        return Mat(gath[(n, l)][:, None], 0, 'c')

    def wrow(n, l):
        g = gath[(n, l)]
        return Mat(g.reshape(1, 1, N_CHIPS * g.shape[1], g.shape[2]), 0, 'c')

    def rg_cw(l):
        return gath[('rg_conv_w', l)].transpose(1, 0, 2).reshape(RG_CONV_W, d)

    def s5_dv(l):
        return gath[('s5_d', l)].reshape(1, d)

    def f_cw(l):
        return gath[('ffn_conv_w', l)].transpose(1, 0, 2).reshape(FFN_CONV_W, 2, d_ff).transpose(1, 0, 2)

    tm = min(1024, s)
    d_up = 2 * d_ff // N_CHIPS
    f_cb = ffn_conv_b.reshape(depth, 2, 1, d_ff)
    put(mixer_keys(0), _run_side("gather_first", gather_side(mixer_keys(0))))

    h = x.reshape(s, d)
    saved = []
    for i in range(depth):
        j = i // 2
        sv = {'h_in': h}
        hn = _rms_fwd(h, norm_mix_g[i:i + 1])
        sv['hn'] = hn
        up_keys = [('ffn_w_up', i), ('ffn_conv_w', i)]
        if i % 2 == 0:
            xg = _mm("rg_in", 'nn', act(hn), wcol('rg_w_in', j), out_parts=2, tm=tm, tn=512, tk=d)
            xg2 = _split2(xg)
            wa, wx = rg_w_a[j].astype(BF16), rg_w_x[j].astype(BF16)
            ba, bx = rg_b_a[j].reshape(1, d), rg_b_x[j].reshape(1, d)
            (xr, hs, y), got = _rg_fwd(xg2, rg_cw(j), rg_conv_b[j:j + 1], wa, ba, wx, bx, rg_lambda[j:j + 1],
                                       side=gather_side(up_keys))
            put(up_keys, got)
            sv.update(xg2=xg2, xr=xr, hs=hs, y=y, wa=wa, wx=wx, ba=ba, bx=bx)
            h = _mm("rg_out", 'nn', act(y), wrow('rg_w_out', j), res=act(h), tm=tm, tn=d, tk=d).arr[0, 0]
        else:
            u = _mm("s5_in", 'nn', act(hn), wrow('s5_w_in', j), tm=tm, tn=d, tk=d).arr[0, 0]
            bt_re, bt_im = s5_b_re[j].transpose(2, 0, 1), s5_b_im[j].transpose(2, 0, 1)
            ldt = s5_log_dt[j].reshape(n_grp, 1)
            _, _, tab_r, tab_i, bbr, bbi = _s5_tables(s5_a_re[j], s5_a_im[j], ldt, bt_re, bt_im)
            nn_ = n_grp * n_state
            tab_r, tab_i = tab_r.reshape(4, SUBLANES, nn_), tab_i.reshape(4, SUBLANES, nn_)
            prm = dict(bp_r=_s5_pack_b(bbr).astype(BF16), bp_i=_s5_pack_b(bbi).astype(BF16),
                       cp_r=_s5_pack_c(s5_c_re[j]).astype(BF16), cp_i=_s5_pack_c(s5_c_im[j]).astype(BF16), dvec=s5_dv(j))
            (hr, hi, ypre, gy), got = _s5_fwd2(u, tab_r, tab_i, ts=s5_ts, side=gather_side(up_keys), **prm)
            sv.update(rtab_r=tab_r[:, ::-1], rtab_i=-tab_i[:, ::-1])
            put(up_keys, got)
            gl = _mm("s5_glu", 'nn', act(gy), wcol('s5_w_glu', j), out_parts=2, tm=tm, tn=512, tk=d)
            gl2 = _split2(gl)
            o = _glu(gl2)
            sv.update(u=u, prm=prm, hr=hr, hi=hi, ypre=ypre, gy=gy, gl2=gl2, o=o, bt_re=bt_re, bt_im=bt_im, ldt=ldt)
            h = _mm("s5_out", 'nn', act(o), wrow('s5_w_out', j), res=act(h), tm=tm, tn=d, tk=d).arr[0, 0]
        sv['h_mid'] = h
        hn2 = _rms_fwd(h, norm_ffn_g[i:i + 1])
        next_keys = [('ffn_w_down', i)] + mixer_keys(i + 1)
        (up2, c2, a_ffn), got = _ffn_up_act(hn2, gath[('ffn_w_up', i)], f_cw(i), f_cb[i], side=gather_side(next_keys))
        put(next_keys, got)
        sv.update(hn2=hn2, up2=up2, c2=c2, act=a_ffn)
        h = _mm("ffn_down", 'nn', act(a_ffn), wrow('ffn_w_down', i), res=act(h), tm=tm, tn=d, tk=d_ff // 2).arr[0, 0]
        saved.append(sv)

    loss_row, dh, dg_final = _loss_and_grad(h, norm_final_g.reshape(1, d), loss_target.reshape(s, d))
    loss = lax.psum(loss_row[0, 0], ("x", "y", "c"))

    gl_ = {n: [None] * w[n].shape[0] for n in PARAM_NAMES if n != 'norm_final_g'}
    recvd = {}

    def as4(n, a):
        return a.reshape((N_CHIPS,) + w[n].shape[1:])

    def scatter_side(keys):
        return _scatter_side([as4(n, gl_[n][l]) for n, l in keys])

    def record(keys, arrs):
        for k, a in zip(keys, arrs):
            recvd[k] = a

    pending = None
    for i in reversed(range(depth)):
        j = i // 2
        sv = saved[i]
        gl_['ffn_w_down'][i] = _mm("ffn_down_dw", 'tn', act(sv['act']), act(dh), out_dtype=BF16, tm=d_ff // 2, tn=d, tk=tm).arr
        (dup2, dcw2, dcb2), got = _ffn_bwd_fused(dh, gath[('ffn_w_down', i)].reshape(d_ff, d), sv['up2'], sv['c2'], f_cw(i),
                                                 side=scatter_side(pending) if pending else None)
        if pending:
            record(pending, got)
        gl_['ffn_conv_w'][i] = dcw2.transpose(1, 0, 2).reshape(FFN_CONV_W, 2 * d_ff)
        gl_['ffn_conv_b'][i] = dcb2.reshape(2 * d_ff)
        dup = Mat(dup2[:, None], 0, 'c')
        gl_['ffn_w_up'][i] = _mm("ffn_up_dw", 'tn', act(sv['hn2']), dup, out_parts=N_CHIPS, out_dtype=BF16, tm=d, tn=d_up, tk=tm).arr
        dh, dg = _mm_rms_bwd("ffn_up_dx", dup, wcol('ffn_w_up', i), sv['h_mid'], norm_ffn_g[i:i + 1], dh, tm=tm, tk=d_up)
        gl_['norm_ffn_g'][i] = dg[0]
        ffn_keys = [('ffn_w_up', i), ('ffn_w_down', i)]
        if i % 2 == 0:
            dy = _mm("rg_out_dx", 'nt', act(dh), wrow('rg_w_out', j), tm=tm, tn=d, tk=d).arr[0, 0]
            gl_['rg_w_out'][j] = _mm("rg_out_dw", 'tn', act(sv['y']), act(dh), out_dtype=BF16, tm=d, tn=d, tk=tm).arr
            (dxg2, dcw, dcb, dwa, dba, dwx, dbx, dlam), got = _rg_bwd(
                dy, sv['xg2'], sv['xr'], sv['hs'], rg_cw(j), sv['wa'], sv['ba'], sv['wx'], sv['bx'], rg_lambda[j:j + 1],
                side=scatter_side(ffn_keys))
            record(ffn_keys, got)
            gl_['rg_conv_w'][j] = dcw
            gl_['rg_conv_b'][j] = dcb[0]
            gl_['rg_w_a'][j], gl_['rg_w_x'][j] = dwa, dwx
            gl_['rg_b_a'][j], gl_['rg_b_x'][j] = dba.reshape(rg_b_a.shape[1:]), dbx.reshape(rg_b_x.shape[1:])
            gl_['rg_lambda'][j] = dlam[0]
            dxg = Mat(dxg2[:, None], 0, 'c')
            gl_['rg_w_in'][j] = _mm("rg_in_dw", 'tn', act(sv['hn']), dxg, out_parts=N_CHIPS, out_dtype=BF16, tm=d, tn=512, tk=tm).arr
            mix_dx = ("rg_in_dx", dxg, wcol('rg_w_in', j), 512)
            pending = [('rg_w_in', j), ('rg_w_out', j)]
        else:
            d_o = _mm("s5_out_dx", 'nt', act(dh), wrow('s5_w_out', j), tm=tm, tn=d, tk=d).arr[0, 0]
            gl_['s5_w_out'][j] = _mm("s5_out_dw", 'tn', act(sv['o']), act(dh), out_dtype=BF16, tm=d, tn=d, tk=tm).arr
            dgl2 = _glu_bwd(sv['gl2'], d_o)
            dgl = Mat(dgl2[:, None], 0, 'c')
            gl_['s5_w_glu'][j] = _mm("s5_glu_dw", 'tn', act(sv['gy']), dgl, out_parts=N_CHIPS, out_dtype=BF16, tm=d, tn=512, tk=tm).arr
            dgy = _mm("s5_glu_dx", 'nt', dgl, wcol('s5_w_glu', j), tm=tm, tn=d, tk=512).arr[0, 0]
            (du, dar, dai, dbpr, dbpi, dcpr, dcpi, dd), got = _s5_bwd2(
                dgy, sv['ypre'], sv['u'], sv['hr'], sv['hi'], sv['rtab_r'], sv['rtab_i'], ts=s5_ts, side=scatter_side(ffn_keys),
                **sv['prm'])
            record(ffn_keys, got)
            gl_['s5_d'][j] = dd[0]
            gl_['s5_c_re'][j] = _s5_unpack_c(dcpr, gc, n_state)
            gl_['s5_c_im'][j] = -_s5_unpack_c(dcpi, gc, n_state)
            d_are, d_aim, d_ldt, d_btr, d_bti = _s5_params_bwd(
                s5_a_re[j], s5_a_im[j], sv['ldt'], sv['bt_re'], sv['bt_im'], dar.reshape(n_grp, n_state), dai.reshape(n_grp, n_state),
                _s5_unpack_b(dbpr, gc, n_state), _s5_unpack_b(dbpi, gc, n_state))
            gl_['s5_a_re'][j], gl_['s5_a_im'][j], gl_['s5_log_dt'][j] = d_are, d_aim, d_ldt[:, 0]
            gl_['s5_b_re'][j], gl_['s5_b_im'][j] = d_btr.transpose(1, 2, 0), d_bti.transpose(1, 2, 0)
            dum = act(du)
            gl_['s5_w_in'][j] = _mm("s5_in_dw", 'tn', act(sv['hn']), dum, out_dtype=BF16, tm=d, tn=d, tk=tm).arr
            mix_dx = ("s5_in_dx", dum, wrow('s5_w_in', j), d)
            pending = [('s5_w_in', j), ('s5_w_glu', j), ('s5_w_out', j)]
        dh, dg = _mm_rms_bwd(mix_dx[0], mix_dx[1], mix_dx[2], sv['h_in'], norm_mix_g[i:i + 1], dh, tm=tm, tk=mix_dx[3])
        gl_['norm_mix_g'][i] = dg[0]
    grad_x = dh.reshape(x.shape)
    record(pending, _run_side("scatter_last", scatter_side(pending)))

    chip_sums = []
    for n in BIG:
        cols = w[n].shape[-1]
        chip_sums.append(_sum_parts([recvd[(n, l)].reshape(N_CHIPS, -1, cols) for l in range(w[n].shape[0])]))
    sib_sums = _swap_with_sibling(chip_sums)
    results = {}
    for n, mine, theirs in zip(BIG, chip_sums, sib_sums):
        cols = w[n].shape[-1]
        outs = _adamw(w[n].reshape(-1, cols), [mine.reshape(-1, cols), theirs.reshape(-1, cols)], mom[n].reshape(-1, cols),
                      vel[n].reshape(-1, cols))
        results[n] = [o.reshape(w[n].shape) for o in outs]

    small = REPLICATED + SMALL_SHARDED
    local = [dg_final.reshape(d) if n == 'norm_final_g' else jnp.stack(gl_[n]) for n in small]
    summed = _unpack(_allreduce_small(_pack(local)), local)
    me = 2 * lax.axis_index("x") + lax.axis_index("y")
    grads = [lax.dynamic_slice_in_dim(g, me * w[n].shape[-1], w[n].shape[-1], axis=g.ndim - 1) if n in SMALL_SHARDED else g
             for n, g in zip(small, summed)]
    like = [w[n] for n in small]
    outs = _adamw(_pack(like), [_pack(grads)], _pack([mom[n] for n in small]), _pack([vel[n] for n in small]))
    unpacked = [_unpack(o, like) for o in outs]
    for k, n in enumerate(small):
        results[n] = [unpacked[q][k] for q in range(4)]

    return (loss, grad_x, *[results[n][0] for n in PARAM_NAMES], *[results[n][1] for n in PARAM_NAMES],
            *[results[n][2] for n in PARAM_NAMES], *[results[n][3] for n in PARAM_NAMES])
```

```python
import functools
import math

import jax
import jax.numpy as jnp
from jax import lax
from jax.experimental import pallas as pl
from jax.experimental.pallas import tpu as pltpu

F32 = jnp.float32
BF16 = jnp.bfloat16
MESH = pl.DeviceIdType.MESH

NORM_EPS = 1e-6
RG_HEADS = 8
RG_CONV_W = 4
RG_C = 8.0
S5_GC = 16
S5_P = 64
S5_GROUPS_PER_BLOCK = 8
FFN_CONV_W = 3
N_CHIPS = 4
ADAM_LR, ADAM_B1, ADAM_B2, ADAM_EPS, ADAM_WD, ADAM_STEP = 0.001, 0.9, 0.999, 1e-08, 0.01, 10
VMEM_LIMIT_BYTES = 56 * 1024 * 1024
SUBLANES = 8
LANES = 128

PARAM_NAMES = ['norm_mix_g', 'norm_ffn_g', 'norm_final_g', 'rg_w_in', 'rg_conv_w', 'rg_conv_b', 'rg_w_a', 'rg_b_a', 'rg_w_x',
               'rg_b_x', 'rg_lambda', 'rg_w_out', 's5_w_in', 's5_a_re', 's5_a_im', 's5_log_dt', 's5_b_re', 's5_b_im', 's5_c_re',
               's5_c_im', 's5_d', 's5_w_glu', 's5_w_out', 'ffn_w_up', 'ffn_conv_w', 'ffn_conv_b', 'ffn_w_down']
SHARDED = ['rg_w_in', 'rg_conv_w', 'rg_w_out', 's5_w_in', 's5_d', 's5_w_glu', 's5_w_out', 'ffn_w_up', 'ffn_conv_w', 'ffn_w_down']
BIG = ['rg_w_in', 'rg_w_out', 's5_w_in', 's5_w_glu', 's5_w_out', 'ffn_w_up', 'ffn_w_down']
ROW_SHARDED = ['rg_w_out', 's5_w_in', 's5_w_out', 'ffn_w_down']
SMALL_SHARDED = ['rg_conv_w', 's5_d', 'ffn_conv_w']
MIXER_SHARDED = [['rg_w_in', 'rg_conv_w', 'rg_w_out'], ['s5_w_in', 's5_d', 's5_w_glu', 's5_w_out']]
FFN_SHARDED = ['ffn_w_up', 'ffn_conv_w', 'ffn_w_down']
REPLICATED = [n for n in PARAM_NAMES if n not in SHARDED]


def _cparams():
    return pltpu.CompilerParams(vmem_limit_bytes=VMEM_LIMIT_BYTES)


_GELU_C = math.sqrt(2.0 / math.pi)
_GELU_K = 0.044715


def _gelu(x):
    return 0.5 * x * (1.0 + jnp.tanh(_GELU_C * (x + _GELU_K * x * x * x)))


def _gelu_and_grad(x):
    t = jnp.tanh(_GELU_C * (x + _GELU_K * x * x * x))
    g = 0.5 * x * (1.0 + t)
    dg = 0.5 * (1.0 + t) + 0.5 * x * (1.0 - t * t) * (_GELU_C * (1.0 + 3.0 * _GELU_K * x * x))
    return g, dg


def _sigmoid(x):
    return jax.nn.sigmoid(x)


def _neg_expm1(x):
    series = -(x * (1.0 + x * (0.5 + x * (1.0 / 6 + x * (1.0 / 24 + x * (1.0 / 120 + x * (1.0 / 720)))))))
    return jnp.where(x > -0.25, series, 1.0 - jnp.exp(x))


def _softplus(z):
    return jnp.maximum(z, 0.0) + jnp.log1p(jnp.exp(-jnp.abs(z)))


def _rows(shape):
    return lax.broadcasted_iota(jnp.int32, shape, 0)


def _shift_down(x, halo, k):
    ext = jnp.concatenate([halo, x], axis=0)
    return pltpu.roll(ext, k, 0)[SUBLANES:]


def _shift_up(x, halo, k):
    ext = jnp.concatenate([x, halo], axis=0)
    n = ext.shape[0]
    return pltpu.roll(ext, n - k, 0)[:x.shape[0]]


def _scan_real_fwd(a, b):
    n = a.shape[0]
    row = _rows(a.shape)
    sh = 1
    while sh < n:
        ok = row >= sh
        b = a * jnp.where(ok, pltpu.roll(b, sh, 0), 0.0) + b
        if sh * 2 < n:
            a = a * jnp.where(ok, pltpu.roll(a, sh, 0), 1.0)
        sh *= 2
    return b


def _scan_real_rev(c, d):
    n = c.shape[0]
    row = _rows(c.shape)
    sh = 1
    while sh < n:
        ok = row < n - sh
        d = c * jnp.where(ok, pltpu.roll(d, n - sh, 0), 0.0) + d
        if sh * 2 < n:
            c = c * jnp.where(ok, pltpu.roll(c, n - sh, 0), 1.0)
        sh *= 2
    return d


def _scan_cplx(br, bi, pr_ref, pi_ref, reverse):
    n = br.shape[0]
    row = _rows(br.shape)
    sh, k = 1, 0
    while sh < n:
        pr = pr_ref[k:k + 1, :]
        pi = pi_ref[k:k + 1, :]
        if reverse:
            ok = row < n - sh
            sr = jnp.where(ok, pltpu.roll(br, n - sh, 0), 0.0)
            si = jnp.where(ok, pltpu.roll(bi, n - sh, 0), 0.0)
        else:
            ok = row >= sh
            sr = jnp.where(ok, pltpu.roll(br, sh, 0), 0.0)
            si = jnp.where(ok, pltpu.roll(bi, sh, 0), 0.0)
        br, bi = br + pr * sr - pi * si, bi + pr * si + pi * sr
        sh *= 2
        k += 1
    return br, bi


RG_LANE_CHUNK = 512


def _real_slab_scan(a_ref, b_ref, out_ref, carry_ref, reverse):
    t, c = a_ref.shape
    nsl = t // SUBLANES
    lc = min(RG_LANE_CHUNK, c)
    row8 = _rows((SUBLANES, lc))
    for q in range(c // lc):
        sl = slice(q * lc, (q + 1) * lc)

        def slab(jj, carry, sl=sl):
            j = nsl - 1 - jj if reverse else jj
            r0 = pl.multiple_of(j * SUBLANES, SUBLANES)
            a, b = a_ref[pl.ds(r0, SUBLANES), sl], b_ref[pl.ds(r0, SUBLANES), sl]
            for k in range(3):
                sh = 1 << k
                keep = row8 < SUBLANES - sh if reverse else row8 >= sh
                amount = SUBLANES - sh if reverse else sh
                b = a * jnp.where(keep, pltpu.roll(b, amount, 0), 0.0) + b
                a = a * jnp.where(keep, pltpu.roll(a, amount, 0), 1.0)
            x = b + a * jnp.broadcast_to(carry, b.shape)
            out_ref[pl.ds(r0, SUBLANES), sl] = x
            return x[:1, :] if reverse else x[SUBLANES - 1:, :]

        carry_ref[:, sl] = lax.fori_loop(0, nsl, slab, carry_ref[:, sl], unroll=2)


class Mat:
    def __init__(self, arr, l=0, split='c'):
        assert arr.ndim == 4
        self.arr, self.l, self.split = arr, l, split
        p, _, r, c = arr.shape
        self.shape = (r, c * p) if split == 'c' else (r * p, c)

    def spec(self, tr, tc, rc):
        p, _, r, c = self.arr.shape
        l = self.l
        assert r % tr == 0 and c % tc == 0, (self.arr.shape, tr, tc)
        if self.split == 'c':
            per = c // tc
            return pl.BlockSpec((None, None, tr, tc), lambda i, j, k: (rc(i, j, k)[1] // per, l, rc(i, j, k)[0], rc(i, j, k)[1] % per))
        per = r // tr
        return pl.BlockSpec((None, None, tr, tc), lambda i, j, k: (rc(i, j, k)[0] // per, l, rc(i, j, k)[0] % per, rc(i, j, k)[1]))


def act(x, parts=1):
    s, c = x.shape
    return Mat(x.reshape(s, parts, c // parts).transpose(1, 0, 2)[:, None] if parts > 1 else x[None, None])


def _mm(name, mode, a, b, *, out_parts=1, out_split='c', out_dtype=F32, res=None, tm=512, tn=512, tk=512):
    if mode == 'nn':
        (m, kk), (kb, n) = a.shape, b.shape
    elif mode == 'nt':
        (m, kk), (n, kb) = a.shape, b.shape
    else:
        (kk, m), (kb, n) = a.shape, b.shape
    assert kk == kb, (name, a.shape, b.shape)
    tm, tn, tk = min(tm, m), min(tn, n), min(tk, kk)
    assert m % tm == 0 and n % tn == 0 and kk % tk == 0, (name, m, n, kk, tm, tn, tk)
    nk = kk // tk
    if mode == 'nn':
        a_spec = a.spec(tm, tk, lambda i, j, k: (i, k))
        b_spec = b.spec(tk, tn, lambda i, j, k: (k, j))
        dims = (((1,), (0,)), ((), ()))
    elif mode == 'nt':
        a_spec = a.spec(tm, tk, lambda i, j, k: (i, k))
        b_spec = b.spec(tn, tk, lambda i, j, k: (j, k))
        dims = (((1,), (1,)), ((), ()))
    else:
        a_spec = a.spec(tk, tm, lambda i, j, k: (k, i))
        b_spec = b.spec(tk, tn, lambda i, j, k: (k, j))
        dims = (((0,), (0,)), ((), ()))
    if out_split == 'c':
        out_arr = jax.ShapeDtypeStruct((out_parts, 1, m, n // out_parts), out_dtype)
    else:
        out_arr = jax.ShapeDtypeStruct((out_parts, 1, m // out_parts, n), out_dtype)
    out_mat = Mat(out_arr, 0, out_split)
    o_spec = out_mat.spec(tm, tn, lambda i, j, k: (i, j))
    has_res = res is not None

    def body(*refs):
        if has_res:
            a_ref, b_ref, r_ref, o_ref = refs[:4]
        else:
            a_ref, b_ref, o_ref = refs[:3]
        prod = lax.dot_general(a_ref[...].astype(BF16), b_ref[...].astype(BF16), dims, preferred_element_type=F32)

        def finish(acc):
            if has_res:
                acc = acc + r_ref[...]
            o_ref[...] = acc.astype(out_dtype)

        if nk == 1:
            finish(prod)
        else:
            acc_ref = refs[-1]
            k = pl.program_id(2)

            @pl.when(k == 0)
            def _():
                acc_ref[...] = prod

            @pl.when(k > 0)
            def _():
                acc_ref[...] += prod

            @pl.when(k == nk - 1)
            def _():
                finish(acc_ref[...])

    in_specs = [a_spec, b_spec]
    args = [a.arr, b.arr]
    if has_res:
        in_specs.append(res.spec(tm, tn, lambda i, j, k: (i, j)))
        args.append(res.arr)
    out = pl.pallas_call(
        body, name=name, grid=(m // tm, n // tn, nk), in_specs=in_specs, out_specs=o_spec, out_shape=out_arr,
        scratch_shapes=[pltpu.VMEM((tm, tn), F32)] if nk > 1 else [], compiler_params=_cparams(),
    )(*args)
    return Mat(out, 0, out_split)


def _rms_fwd(h, g, ts=512):
    s, d = h.shape
    ts = min(ts, s)

    def body(h_ref, g_ref, o_ref):
        x = h_ref[...]
        var = jnp.mean(x * x, axis=-1, keepdims=True)
        o_ref[...] = (x * lax.rsqrt(var + NORM_EPS) * g_ref[...]).astype(BF16)

    return pl.pallas_call(
        body, name="rms_fwd", grid=(s // ts,),
        in_specs=[pl.BlockSpec((ts, d), lambda i: (i, 0)), pl.BlockSpec((1, d), lambda i: (0, 0))],
        out_specs=pl.BlockSpec((ts, d), lambda i: (i, 0)), out_shape=jax.ShapeDtypeStruct((s, d), BF16),
        compiler_params=_cparams(),
    )(h, g)


def _rms_bwd(h, g, dhn, dh_in, ts=512):
    s, d = h.shape
    ts = min(ts, s)

    def body(h_ref, g_ref, dhn_ref, dhin_ref, dh_ref, dg_ref):
        i = pl.program_id(0)
        x = h_ref[...]
        rstd = lax.rsqrt(jnp.mean(x * x, axis=-1, keepdims=True) + NORM_EPS)
        xhat = x * rstd
        dhn_v = dhn_ref[...]
        dxh = dhn_v * g_ref[...]
        dh_ref[...] = dhin_ref[...] + rstd * (dxh - xhat * jnp.mean(dxh * xhat, axis=-1, keepdims=True))
        part = jnp.sum(dhn_v * xhat, axis=0, keepdims=True)

        @pl.when(i == 0)
        def _():
            dg_ref[...] = part

        @pl.when(i > 0)
        def _():
            dg_ref[...] += part

    row = pl.BlockSpec((ts, d), lambda i: (i, 0))
    vec = pl.BlockSpec((1, d), lambda i: (0, 0))
    return pl.pallas_call(
        body, name="rms_bwd", grid=(s // ts,), in_specs=[row, vec, row, row], out_specs=[row, vec],
        out_shape=[jax.ShapeDtypeStruct((s, d), F32), jax.ShapeDtypeStruct((1, d), F32)], compiler_params=_cparams(),
    )(h, g, dhn, dh_in)


def _loss_and_grad(h, g, tgt, ts=512):
    s, d = h.shape
    ts = min(ts, s)

    def body(h_ref, g_ref, t_ref, loss_ref, dh_ref, dg_ref):
        i = pl.program_id(0)
        x = h_ref[...]
        gv = g_ref[...]
        rstd = lax.rsqrt(jnp.mean(x * x, axis=-1, keepdims=True) + NORM_EPS)
        xhat = x * rstd
        err = xhat * gv - t_ref[...]
        dy = err * (1.0 / d)
        dxh = dy * gv
        dh_ref[...] = rstd * (dxh - xhat * jnp.mean(dxh * xhat, axis=-1, keepdims=True))
        part = jnp.sum(dy * xhat, axis=0, keepdims=True)
        lpart = jnp.broadcast_to(jnp.sum(jnp.sum(err * err, axis=0, keepdims=True), axis=1, keepdims=True) * (0.5 / d), (1, LANES))

        @pl.when(i == 0)
        def _():
            dg_ref[...] = part
            loss_ref[...] = lpart

        @pl.when(i > 0)
        def _():
            dg_ref[...] += part
            loss_ref[...] += lpart

    row = pl.BlockSpec((ts, d), lambda i: (i, 0))
    vec = pl.BlockSpec((1, d), lambda i: (0, 0))
    return pl.pallas_call(
        body, name="loss_and_grad", grid=(s // ts,), in_specs=[row, vec, row],
        out_specs=[pl.BlockSpec((1, LANES), lambda i: (0, 0)), row, vec],
        out_shape=[jax.ShapeDtypeStruct((1, LANES), F32), jax.ShapeDtypeStruct((s, d), F32), jax.ShapeDtypeStruct((1, d), F32)],
        compiler_params=_cparams(),
    )(h, g, tgt)


def _halo_before(ts, nrow8):
    return lambda i: jnp.maximum(i * (ts // SUBLANES) - 1, 0)


def _ffn_act(up2, conv_w2, conv_b2, ts=512, tn=512, side=None):
    _, s, f = up2.shape
    ts, tn = min(ts, s), min(tn, f)
    kw = FFN_CONV_W

    def body(up_ref, halo_ref, w_ref, b_ref, o_ref):
        i = pl.program_id(0)
        cs = []
        for h in range(2):
            x = up_ref[h]
            halo = jnp.where(i == 0, 0.0, halo_ref[h])
            c = b_ref[h] + w_ref[h, kw - 1:kw, :] * x
            for sft in range(1, kw):
                c = c + w_ref[h, kw - 1 - sft:kw - sft, :] * _shift_down(x, halo, sft)
            cs.append(c)
        o_ref[...] = (_gelu(cs[0]) * cs[1]).astype(BF16)

    hb = ts // SUBLANES
    g0, g1 = s // ts, f // tn
    outs, side_outs = _call_with_side(
        body, side, lambda: (pl.program_id(0) == 0) & (pl.program_id(1) == 0),
        lambda: (pl.program_id(0) == g0 - 1) & (pl.program_id(1) == g1 - 1),
        name="ffn_act", grid=(g0, g1),
        in_specs=[pl.BlockSpec((2, ts, tn), lambda i, j: (0, i, j)),
                  pl.BlockSpec((2, SUBLANES, tn), lambda i, j: (0, jnp.maximum(i * hb - 1, 0), j)),
                  pl.BlockSpec((2, kw, tn), lambda i, j: (0, 0, j)),
                  pl.BlockSpec((2, 1, tn), lambda i, j: (0, 0, j))],
        out_specs=[pl.BlockSpec((ts, tn), lambda i, j: (i, j))], out_shape=[jax.ShapeDtypeStruct((s, f), BF16)],
        scratch_shapes=[], args=(up2, up2, conv_w2, conv_b2))
    return outs[0], side_outs


def _ffn_bwd(up2, dact, conv_w2, conv_b2, ts=256, tn=512, side=None):
    _, s, f = up2.shape
    ts, tn = min(ts, s), min(tn, f)
    kw = FFN_CONV_W
    nt = s // ts
    hb = ts // SUBLANES
    last8 = s // SUBLANES - 1

    def body(up_ref, hb_ref, ha_ref, da_ref, dah_ref, w_ref, b_ref, dup_ref, dw_ref, db_ref):
        i = pl.program_id(1)
        first, last = i == 0, i == nt - 1
        ce, xs = [], []
        for h in range(2):
            x = up_ref[h]
            before = jnp.where(first, 0.0, hb_ref[h])
            after = ha_ref[h]
            ext = jnp.concatenate([before, x, after], axis=0)
            c = b_ref[h] + w_ref[h, kw - 1:kw, :] * ext
            shifted = [ext]
            for sft in range(1, kw):
                sh = pltpu.roll(ext, sft, 0)
                shifted.append(sh)
                c = c + w_ref[h, kw - 1 - sft:kw - sft, :] * sh
            ce.append(c[SUBLANES:])
            xs.append([sh[SUBLANES:SUBLANES + ts] for sh in shifted])
        da = jnp.concatenate([da_ref[...], jnp.where(last, 0.0, dah_ref[...])], axis=0)
        g1, dg1 = _gelu_and_grad(ce[0])
        dcs = [da * ce[1] * dg1, da * g1]
        for h in range(2):
            dc = dcs[h]
            n = dc.shape[0]
            dup = w_ref[h, kw - 1:kw, :] * dc[:ts]
            for sft in range(1, kw):
                dup = dup + w_ref[h, kw - 1 - sft:kw - sft, :] * pltpu.roll(dc, n - sft, 0)[:ts]
            dup_ref[h] = dup.astype(BF16)
            dct = dc[:ts]
            dbp = jnp.sum(dct, axis=0, keepdims=True)
            dwp = [jnp.sum(dct * xs[h][kw - 1 - k], axis=0, keepdims=True) for k in range(kw)]

            @pl.when(first)
            def _():
                db_ref[h] = dbp
                for k in range(kw):
                    dw_ref[h, k:k + 1, :] = dwp[k]

            @pl.when(i > 0)
            def _():
                db_ref[h] += dbp
                for k in range(kw):
                    dw_ref[h, k:k + 1, :] += dwp[k]

    g0 = f // tn
    return _call_with_side(
        body, side, lambda: (pl.program_id(0) == 0) & (pl.program_id(1) == 0),
        lambda: (pl.program_id(0) == g0 - 1) & (pl.program_id(1) == nt - 1),
        name="ffn_bwd", grid=(g0, nt),
        in_specs=[pl.BlockSpec((2, ts, tn), lambda j, i: (0, i, j)),
                  pl.BlockSpec((2, SUBLANES, tn), lambda j, i: (0, jnp.maximum(i * hb - 1, 0), j)),
                  pl.BlockSpec((2, SUBLANES, tn), lambda j, i: (0, jnp.minimum((i + 1) * hb, last8), j)),
                  pl.BlockSpec((ts, tn), lambda j, i: (i, j)),
                  pl.BlockSpec((SUBLANES, tn), lambda j, i: (jnp.minimum((i + 1) * hb, last8), j)),
                  pl.BlockSpec((2, kw, tn), lambda j, i: (0, 0, j)),
                  pl.BlockSpec((2, 1, tn), lambda j, i: (0, 0, j))],
        out_specs=[pl.BlockSpec((2, ts, tn), lambda j, i: (0, i, j)),
                   pl.BlockSpec((2, kw, tn), lambda j, i: (0, 0, j)),
                   pl.BlockSpec((2, 1, tn), lambda j, i: (0, 0, j))],
        out_shape=[jax.ShapeDtypeStruct((2, s, f), BF16), jax.ShapeDtypeStruct((2, kw, f), F32),
                   jax.ShapeDtypeStruct((2, 1, f), F32)],
        scratch_shapes=[], args=(up2, up2, up2, dact, dact, conv_w2, conv_b2))


def _mm_rms_bwd(name, a, b, h, g, dh_in, *, tm, tk):
    (m, kk), (n, kb) = a.shape, b.shape
    assert kk == kb and h.shape == (m, n), (name, a.shape, b.shape, h.shape)
    tm, tk = min(tm, m), min(tk, kk)
    nk = kk // tk
    dims = (((1,), (1,)), ((), ()))

    def body(a_ref, b_ref, h_ref, g_ref, dhin_ref, dh_ref, dg_ref, *acc):
        i, k = pl.program_id(0), pl.program_id(2)
        prod = lax.dot_general(a_ref[...].astype(BF16), b_ref[...].astype(BF16), dims, preferred_element_type=F32)

        def finish(dhn):
            x = h_ref[...]
            rstd = lax.rsqrt(jnp.mean(x * x, axis=-1, keepdims=True) + NORM_EPS)
            xhat = x * rstd
            dxh = dhn * g_ref[...]
            dh_ref[...] = dhin_ref[...] + rstd * (dxh - xhat * jnp.mean(dxh * xhat, axis=-1, keepdims=True))
            part = jnp.sum(dhn * xhat, axis=0, keepdims=True)

            @pl.when(i == 0)
            def _():
                dg_ref[...] = part

            @pl.when(i > 0)
            def _():
                dg_ref[...] += part

        if nk == 1:
            finish(prod)
        else:
            acc_ref = acc[0]

            @pl.when(k == 0)
            def _():
                acc_ref[...] = prod

            @pl.when(k > 0)
            def _():
                acc_ref[...] += prod

            @pl.when(k == nk - 1)
            def _():
                finish(acc_ref[...])

    row = pl.BlockSpec((tm, n), lambda i, j, k: (i, 0))
    vec = pl.BlockSpec((1, n), lambda i, j, k: (0, 0))
    return pl.pallas_call(
        body, name=name, grid=(m // tm, 1, nk),
        in_specs=[a.spec(tm, tk, lambda i, j, k: (i, k)), b.spec(n, tk, lambda i, j, k: (0, k)), row, vec, row],
        out_specs=[row, vec], out_shape=[jax.ShapeDtypeStruct((m, n), F32), jax.ShapeDtypeStruct((1, n), F32)],
        scratch_shapes=[pltpu.VMEM((tm, n), F32)] if nk > 1 else [], compiler_params=_cparams(),
    )(a.arr, b.arr, h, g, dh_in)


def _ffn_up_act(hn2, w_up4, conv_w2, conv_b2, ts=1024, tn=512, sub=256, side=None):
    s, d = hn2.shape
    p, _, wc = w_up4.shape
    f = p * wc // 2
    ts, tn = min(ts, s), min(tn, wc)
    sub = min(sub, ts)
    per = wc // tn
    kw = FFN_CONV_W
    g0, g1 = f // tn, s // ts

    def body(hn_ref, w1_ref, w2_ref, cw_ref, cb_ref, up_ref, c_ref, act_ref, carry_ref):
        @pl.when(pl.program_id(1) == 0)
        def _():
            carry_ref[...] = jnp.zeros_like(carry_ref)

        for q in range(ts // sub):
            rows = slice(q * sub, (q + 1) * sub)
            hn = hn_ref[rows, :]
            cs = []
            for h, w_ref in enumerate((w1_ref, w2_ref)):
                x = jnp.dot(hn, w_ref[...], preferred_element_type=F32)
                up_ref[h, rows, :] = x
                halo = carry_ref[h]
                c = cb_ref[h] + cw_ref[h, kw - 1:kw, :] * x
                for sft in range(1, kw):
                    c = c + cw_ref[h, kw - 1 - sft:kw - sft, :] * _shift_down(x, halo, sft)
                carry_ref[h] = x[sub - SUBLANES:, :]
                c_ref[h, rows, :] = c
                cs.append(c)
            act_ref[rows, :] = (_gelu(cs[0]) * cs[1]).astype(BF16)

    outs, side_outs = _call_with_side(
        body, side, lambda: (pl.program_id(0) == 0) & (pl.program_id(1) == 0),
        lambda: (pl.program_id(0) == g0 - 1) & (pl.program_id(1) == g1 - 1),
        name="ffn_up_act", grid=(g0, g1),
        in_specs=[pl.BlockSpec((ts, d), lambda j, i: (i, 0)),
                  pl.BlockSpec((None, d, tn), lambda j, i: (j // per, 0, j % per)),
                  pl.BlockSpec((None, d, tn), lambda j, i: (p // 2 + j // per, 0, j % per)),
                  pl.BlockSpec((2, kw, tn), lambda j, i: (0, 0, j)),
                  pl.BlockSpec((2, 1, tn), lambda j, i: (0, 0, j))],
        out_specs=[pl.BlockSpec((2, ts, tn), lambda j, i: (0, i, j)), pl.BlockSpec((2, ts, tn), lambda j, i: (0, i, j)),
                   pl.BlockSpec((ts, tn), lambda j, i: (i, j))],
        out_shape=[jax.ShapeDtypeStruct((2, s, f), F32), jax.ShapeDtypeStruct((2, s, f), F32), jax.ShapeDtypeStruct((s, f), BF16)],
        scratch_shapes=[pltpu.VMEM((2, SUBLANES, tn), F32)], args=(hn2, w_up4, w_up4, conv_w2, conv_b2))
    return outs, side_outs


def _ffn_bwd_fused(dh, w_down, up2, c2, conv_w2, ts=256, tn=512, side=None):
    s, d = dh.shape
    _, _, f = up2.shape
    ts, tn = min(ts, s), min(tn, f)
    kw = FFN_CONV_W
    nt = s // ts
    hb = ts // SUBLANES
    g0 = f // tn
    nt_dims = (((1,), (1,)), ((), ()))

    def body(dh_ref, wd_ref, up_ref, c_ref, w_ref, dup_ref, dw_ref, db_ref, carry_ref):
        i = pl.program_id(1)
        first_step = i == 0

        @pl.when(first_step)
        def _():
            carry_ref[...] = jnp.zeros_like(carry_ref)

        da = lax.dot_general(dh_ref[...].astype(BF16), wd_ref[...], nt_dims, preferred_element_type=F32)
        g1, dg1 = _gelu_and_grad(c_ref[0])
        dcs = [da * c_ref[1] * dg1, da * g1]
        for h in range(2):
            dc = dcs[h]
            after = carry_ref[h]
            ups = [dc] + [_shift_up(dc, after, sft) for sft in range(1, kw)]
            dup = w_ref[h, kw - 1:kw, :] * dc
            for sft in range(1, kw):
                dup = dup + w_ref[h, kw - 1 - sft:kw - sft, :] * ups[sft]
            carry_ref[h] = dc[:SUBLANES]
            dup_ref[h] = dup.astype(BF16)
            dbp = jnp.sum(dc, axis=0, keepdims=True)
            x = up_ref[h]
            dwp = [jnp.sum(ups[kw - 1 - k] * x, axis=0, keepdims=True) for k in range(kw)]

            @pl.when(first_step)
            def _():
                db_ref[h] = dbp
                for k in range(kw):
                    dw_ref[h, k:k + 1, :] = dwp[k]

            @pl.when(i > 0)
            def _():
                db_ref[h] += dbp
                for k in range(kw):
                    dw_ref[h, k:k + 1, :] += dwp[k]

    rev = lambda i: nt - 1 - i
    return _call_with_side(
        body, side, lambda: (pl.program_id(0) == 0) & (pl.program_id(1) == 0),
        lambda: (pl.program_id(0) == g0 - 1) & (pl.program_id(1) == nt - 1),
        name="ffn_bwd", grid=(g0, nt),
        in_specs=[pl.BlockSpec((ts, d), lambda j, i: (rev(i), 0)),
                  pl.BlockSpec((tn, d), lambda j, i: (j, 0)),
                  pl.BlockSpec((2, ts, tn), lambda j, i: (0, rev(i), j)),
                  pl.BlockSpec((2, ts, tn), lambda j, i: (0, rev(i), j)),
                  pl.BlockSpec((2, kw, tn), lambda j, i: (0, 0, j))],
        out_specs=[pl.BlockSpec((2, ts, tn), lambda j, i: (0, rev(i), j)),
                   pl.BlockSpec((2, kw, tn), lambda j, i: (0, 0, j)),
                   pl.BlockSpec((2, 1, tn), lambda j, i: (0, 0, j))],
        out_shape=[jax.ShapeDtypeStruct((2, s, f), BF16), jax.ShapeDtypeStruct((2, kw, f), F32),
                   jax.ShapeDtypeStruct((2, 1, f), F32)],
        scratch_shapes=[pltpu.VMEM((2, SUBLANES, tn), F32)], args=(dh, w_down, up2, c2, conv_w2))


def _ffn_fwd(h, g, w_up4, w_down, conv_w2, conv_b2, ts=512, tn=512, sub=256, side=None):
    s, d = h.shape
    p, _, wc = w_up4.shape
    f = p * wc // 2
    ts, tn = min(ts, s), min(tn, wc)
    sub = min(sub, ts)
    per = wc // tn
    kw = FFN_CONV_W
    g0, g1 = s // ts, f // tn

    def body(h_ref, g_ref, w1_ref, w2_ref, wd_ref, cw_ref, cb_ref, ho_ref, hn_ref, up_ref, c_ref, act_ref, carry_ref):
        i, j = pl.program_id(0), pl.program_id(1)

        @pl.when(j == 0)
        def _():
            x = h_ref[...]
            var = jnp.mean(x * x, axis=-1, keepdims=True)
            hn_ref[...] = (x * lax.rsqrt(var + NORM_EPS) * g_ref[...]).astype(BF16)
            ho_ref[...] = x

        @pl.when(i == 0)
        def _():
            carry_ref[j] = jnp.zeros(carry_ref.shape[1:], F32)

        for q in range(ts // sub):
            rows = slice(q * sub, (q + 1) * sub)
            hn = hn_ref[rows, :]
            cs = []
            for hf, w_ref in enumerate((w1_ref, w2_ref)):
                x = jnp.dot(hn, w_ref[...], preferred_element_type=F32)
                up_ref[hf, rows, :] = x
                halo = carry_ref[j, hf]
                c = cb_ref[hf] + cw_ref[hf, kw - 1:kw, :] * x
                for sft in range(1, kw):
                    c = c + cw_ref[hf, kw - 1 - sft:kw - sft, :] * _shift_down(x, halo, sft)
                carry_ref[j, hf] = x[sub - SUBLANES:, :]
                c_ref[hf, rows, :] = c
                cs.append(c)
            a = (_gelu(cs[0]) * cs[1]).astype(BF16)
            act_ref[rows, :] = a
            ho_ref[rows, :] += jnp.dot(a, wd_ref[...], preferred_element_type=F32)

    row = pl.BlockSpec((ts, d), lambda i, j: (i, 0))
    col2 = pl.BlockSpec((2, ts, tn), lambda i, j: (0, i, j))
    return _call_with_side(
        body, side, lambda: (pl.program_id(0) == 0) & (pl.program_id(1) == 0),
        lambda: (pl.program_id(0) == g0 - 1) & (pl.program_id(1) == g1 - 1),
        name="ffn_fwd", grid=(g0, g1),
        in_specs=[row, pl.BlockSpec((1, d), lambda i, j: (0, 0)),
                  pl.BlockSpec((None, d, tn), lambda i, j: (j // per, 0, j % per)),
                  pl.BlockSpec((None, d, tn), lambda i, j: (p // 2 + j // per, 0, j % per)),
                  pl.BlockSpec((tn, d), lambda i, j: (j, 0)),
                  pl.BlockSpec((2, kw, tn), lambda i, j: (0, 0, j)),
                  pl.BlockSpec((2, 1, tn), lambda i, j: (0, 0, j))],
        out_specs=[row, row, col2, col2, pl.BlockSpec((ts, tn), lambda i, j: (i, j))],
        out_shape=[jax.ShapeDtypeStruct((s, d), F32), jax.ShapeDtypeStruct((s, d), BF16), jax.ShapeDtypeStruct((2, s, f), F32),
                   jax.ShapeDtypeStruct((2, s, f), F32), jax.ShapeDtypeStruct((s, f), BF16)],
        scratch_shapes=[pltpu.VMEM((g1, 2, SUBLANES, tn), F32)],
        args=(h, g, w_up4, w_up4, w_down, conv_w2, conv_b2))


def _ffn_bwd_all(dh, w_down, up2, c2, hn, act_, conv_w2, ts=256, tn=512, sub=128, side=None):
    s, d = dh.shape
    _, _, f = up2.shape
    ts, tn = min(ts, s), min(tn, f)
    sub = min(sub, ts)
    kw = FFN_CONV_W
    nt = s // ts
    g0 = f // tn
    nt_dims = (((1,), (1,)), ((), ()))
    tn_dims = (((0,), (0,)), ((), ()))

    def body(dh_ref, wd_ref, up_ref, c_ref, hn_ref, act_ref, w_ref, dup_ref, dw_ref, db_ref, dwu_ref, dwd_ref,
             carry_ref, dwu_acc, dwd_acc):
        i = pl.program_id(1)
        first_step = i == 0

        @pl.when(first_step)
        def _():
            carry_ref[...] = jnp.zeros_like(carry_ref)
            dwu_acc[...] = jnp.zeros_like(dwu_acc)
            dwd_acc[...] = jnp.zeros_like(dwd_acc)
            dw_ref[...] = jnp.zeros_like(dw_ref)
            db_ref[...] = jnp.zeros_like(db_ref)

        dhb = dh_ref[...].astype(BF16)
        da_all = lax.dot_general(dhb, wd_ref[...], nt_dims, preferred_element_type=F32)
        for q in reversed(range(ts // sub)):
            rows = slice(q * sub, (q + 1) * sub)
            da = da_all[rows, :]
            g1, dg1 = _gelu_and_grad(c_ref[0, rows, :])
            dcs = [da * c_ref[1, rows, :] * dg1, da * g1]
            hnq = hn_ref[rows, :]
            for hf in range(2):
                dc = dcs[hf]
                after = carry_ref[hf]
                ups = [dc] + [_shift_up(dc, after, sft) for sft in range(1, kw)]
                dup = w_ref[hf, kw - 1:kw, :] * dc
                for sft in range(1, kw):
                    dup = dup + w_ref[hf, kw - 1 - sft:kw - sft, :] * ups[sft]
                carry_ref[hf] = dc[:SUBLANES]
                dupb = dup.astype(BF16)
                dup_ref[hf, rows, :] = dupb
                dwu_acc[hf] += lax.dot_general(hnq, dupb, tn_dims, preferred_element_type=F32)
                db_ref[hf] += jnp.sum(dc, axis=0, keepdims=True)
                x = up_ref[hf, rows, :]
                for k in range(kw):
                    dw_ref[hf, k:k + 1, :] += jnp.sum(ups[kw - 1 - k] * x, axis=0, keepdims=True)
            dwd_acc[...] += lax.dot_general(act_ref[rows, :], dhb[rows, :], tn_dims, preferred_element_type=F32)

        @pl.when(i == nt - 1)
        def _():
            dwu_ref[...] = dwu_acc[...].astype(BF16)
            dwd_ref[...] = dwd_acc[...].astype(BF16)

    rev = lambda i: nt - 1 - i
    col2 = pl.BlockSpec((2, ts, tn), lambda j, i: (0, rev(i), j))
    return _call_with_side(
        body, side, lambda: (pl.program_id(0) == 0) & (pl.program_id(1) == 0),
        lambda: (pl.program_id(0) == g0 - 1) & (pl.program_id(1) == nt - 1),
        name="ffn_bwd", grid=(g0, nt),
        in_specs=[pl.BlockSpec((ts, d), lambda j, i: (rev(i), 0)),
                  pl.BlockSpec((tn, d), lambda j, i: (j, 0)),
                  col2, col2,
                  pl.BlockSpec((ts, d), lambda j, i: (rev(i), 0)),
                  pl.BlockSpec((ts, tn), lambda j, i: (rev(i), j)),
                  pl.BlockSpec((2, kw, tn), lambda j, i: (0, 0, j))],
        out_specs=[col2,
                   pl.BlockSpec((2, kw, tn), lambda j, i: (0, 0, j)),
                   pl.BlockSpec((2, 1, tn), lambda j, i: (0, 0, j)),
                   pl.BlockSpec((2, d, tn), lambda j, i: (0, 0, j)),
                   pl.BlockSpec((tn, d), lambda j, i: (j, 0))],
        out_shape=[jax.ShapeDtypeStruct((2, s, f), BF16), jax.ShapeDtypeStruct((2, kw, f), F32),
                   jax.ShapeDtypeStruct((2, 1, f), F32), jax.ShapeDtypeStruct((2, d, f), BF16), jax.ShapeDtypeStruct((f, d), BF16)],
        scratch_shapes=[pltpu.VMEM((2, SUBLANES, tn), F32), pltpu.VMEM((2, d, tn), F32), pltpu.VMEM((tn, d), F32)],
        args=(dh, w_down, up2, c2, hn, act_, conv_w2))


def _rg_gates(xr, wa_ref, ba_ref, wx_ref, bx_ref, lam_ref):
    bw = wa_ref.shape[-1]
    xb = xr.astype(BF16)
    za = jnp.concatenate([jnp.dot(xb[:, h * bw:(h + 1) * bw], wa_ref[h], preferred_element_type=F32)
                          for h in range(RG_HEADS)], axis=1) + ba_ref[...]
    zx = jnp.concatenate([jnp.dot(xb[:, h * bw:(h + 1) * bw], wx_ref[h], preferred_element_type=F32)
                          for h in range(RG_HEADS)], axis=1) + bx_ref[...]
    r, ig = _sigmoid(za), _sigmoid(zx)
    sp = _softplus(-lam_ref[...])
    la = -RG_C * r * sp
    a = jnp.exp(la)
    mult = jnp.sqrt(_neg_expm1(2.0 * la))
    return xb, r, ig, sp, a, mult


def _rg_fwd(xg2, conv_w, conv_b, w_a, b_a, w_x, b_x, lam, ts=256, side=None):
    _, s, c = xg2.shape
    ts = min(ts, s)
    kw = RG_CONV_W
    hb = ts // SUBLANES

    def body(xg_ref, halo_ref, cw_ref, cb_ref, wa_ref, ba_ref, wx_ref, bx_ref, lam_ref, xr_ref, hs_ref, y_ref, carry_ref,
             a_scr, b_scr):
        i = pl.program_id(0)

        @pl.when(i == 0)
        def _():
            carry_ref[...] = jnp.zeros_like(carry_ref)

        xp = xg_ref[0]
        halo = jnp.where(i == 0, 0.0, halo_ref[...])
        xr = cb_ref[...] + cw_ref[kw - 1:kw, :] * xp
        for sft in range(1, kw):
            xr = xr + cw_ref[kw - 1 - sft:kw - sft, :] * _shift_down(xp, halo, sft)
        _, r, ig, sp, a, mult = _rg_gates(xr, wa_ref, ba_ref, wx_ref, bx_ref, lam_ref)
        a_scr[...] = a
        b_scr[...] = mult * (ig * xr)
        _real_slab_scan(a_scr, b_scr, hs_ref, carry_ref, reverse=False)
        xr_ref[...] = xr
        y_ref[...] = (hs_ref[...] * _gelu(xg_ref[1])).astype(BF16)

    full = lambda shape: pl.BlockSpec(shape, lambda i: (0,) * len(shape))
    row_spec = pl.BlockSpec((ts, c), lambda i: (i, 0))
    nt = s // ts
    return _call_with_side(
        body, side, lambda: pl.program_id(0) == 0, lambda: pl.program_id(0) == nt - 1,
        name="rg_fwd", grid=(nt,),
        in_specs=[pl.BlockSpec((2, ts, c), lambda i: (0, i, 0)),
                  pl.BlockSpec((None, SUBLANES, c), lambda i: (0, jnp.maximum(i * hb - 1, 0), 0)),
                  full(conv_w.shape), full(conv_b.shape), full(w_a.shape), full(b_a.shape), full(w_x.shape), full(b_x.shape),
                  full(lam.shape)],
        out_specs=[row_spec, row_spec, row_spec],
        out_shape=[jax.ShapeDtypeStruct((s, c), F32), jax.ShapeDtypeStruct((s, c), F32), jax.ShapeDtypeStruct((s, c), BF16)],
        scratch_shapes=[pltpu.VMEM((1, c), F32), pltpu.VMEM((ts, c), F32), pltpu.VMEM((ts, c), F32)],
        args=(xg2, xg2, conv_w, conv_b, w_a, b_a, w_x, b_x, lam))


def _rg_bwd(dy, xg2, xr, hs, conv_w, w_a, b_a, w_x, b_x, lam, ts=256, side=None):
    _, s, c = xg2.shape
    ts = min(ts, s)
    nt = s // ts
    kw = RG_CONV_W
    hb = ts // SUBLANES
    bw = c // RG_HEADS
    tn_dims = (((0,), (0,)), ((), ()))
    nt_dims = (((1,), (1,)), ((), ()))

    def body(dy_ref, xg_ref, xph_ref, xr_ref, hs_ref, hsh_ref, cw_ref, wa_ref, ba_ref, wx_ref, bx_ref, lam_ref,
             dxg_ref, dcw_ref, dcb_ref, dwa_ref, dba_ref, dwx_ref, dbx_ref, dlam_ref,
             lam_carry, a_carry, dxr_carry, dsp_acc, a_scr, b_scr):
        i = pl.program_id(0)
        first_step = i == 0
        time_first = i == nt - 1

        @pl.when(first_step)
        def _():
            lam_carry[...] = jnp.zeros_like(lam_carry)
            a_carry[...] = jnp.ones_like(a_carry)
            dxr_carry[...] = jnp.zeros_like(dxr_carry)
            dsp_acc[...] = jnp.zeros_like(dsp_acc)
            for ref in (dcw_ref, dcb_ref, dwa_ref, dba_ref, dwx_ref, dbx_ref):
                ref[...] = jnp.zeros_like(ref)

        xr = xr_ref[...]
        hs = hs_ref[...]
        gate = xg_ref[1]
        xb, r, ig, sp, a, mult = _rg_gates(xr, wa_ref, ba_ref, wx_ref, bx_ref, lam_ref)
        dyv = dy_ref[...]
        gg, dgg = _gelu_and_grad(gate)
        dhs = dyv * gg
        dxg_ref[1] = (dyv * hs * dgg).astype(BF16)
        row = _rows(xr.shape)
        a_scr[...] = jnp.where(row == ts - 1, a_carry[0:1, :], pltpu.roll(a, ts - 1, 0))
        b_scr[...] = dhs
        _real_slab_scan(a_scr, b_scr, b_scr, lam_carry, reverse=True)
        lmb = b_scr[...]
        a_carry[...] = a[:SUBLANES]
        hs_prev = _shift_down(hs, jnp.where(time_first, 0.0, hsh_ref[...]), 1)
        d_a = lmb * hs_prev
        d_m = lmb * (ig * xr)
        d_ig = lmb * mult * xr
        d_xr = lmb * mult * ig
        d_la = a * d_a - (a * a / mult) * d_m
        dsp_acc[...] += jnp.sum(-RG_C * r * d_la, axis=0, keepdims=True)
        d_za = (-RG_C * sp) * d_la * r * (1.0 - r)
        d_zx = d_ig * ig * (1.0 - ig)
        dba_ref[...] += jnp.sum(d_za, axis=0, keepdims=True)
        dbx_ref[...] += jnp.sum(d_zx, axis=0, keepdims=True)
        dzab, dzxb = d_za.astype(BF16), d_zx.astype(BF16)
        back = []
        for h in range(RG_HEADS):
            sl = slice(h * bw, (h + 1) * bw)
            dwa_ref[h] += lax.dot_general(xb[:, sl], dzab[:, sl], tn_dims, preferred_element_type=F32)
            dwx_ref[h] += lax.dot_general(xb[:, sl], dzxb[:, sl], tn_dims, preferred_element_type=F32)
            back.append(lax.dot_general(dzab[:, sl], wa_ref[h], nt_dims, preferred_element_type=F32)
                        + lax.dot_general(dzxb[:, sl], wx_ref[h], nt_dims, preferred_element_type=F32))
        d_xr = d_xr + jnp.concatenate(back, axis=1)
        d_xp = cw_ref[kw - 1:kw, :] * d_xr
        after = dxr_carry[...]
        for sft in range(1, kw):
            d_xp = d_xp + cw_ref[kw - 1 - sft:kw - sft, :] * _shift_up(d_xr, after, sft)
        dxr_carry[...] = d_xr[:SUBLANES]
        dxg_ref[0] = d_xp.astype(BF16)
        xp = xg_ref[0]
        before = jnp.where(time_first, 0.0, xph_ref[...])
        dcb_ref[...] += jnp.sum(d_xr, axis=0, keepdims=True)
        dcw_ref[kw - 1:kw, :] += jnp.sum(d_xr * xp, axis=0, keepdims=True)
        for sft in range(1, kw):
            dcw_ref[kw - 1 - sft:kw - sft, :] += jnp.sum(d_xr * _shift_down(xp, before, sft), axis=0, keepdims=True)
        dlam_ref[...] = dsp_acc[...] * (-_sigmoid(-lam_ref[...]))

    full = lambda shape: pl.BlockSpec(shape, lambda i: (0,) * len(shape))
    rev = lambda i: nt - 1 - i
    row_spec = pl.BlockSpec((ts, c), lambda i: (rev(i), 0))
    halo_idx = lambda i: jnp.maximum(rev(i) * hb - 1, 0)
    vec = (1, c)
    return _call_with_side(
        body, side, lambda: pl.program_id(0) == 0, lambda: pl.program_id(0) == nt - 1,
        name="rg_bwd", grid=(nt,),
        in_specs=[row_spec,
                  pl.BlockSpec((2, ts, c), lambda i: (0, rev(i), 0)),
                  pl.BlockSpec((None, SUBLANES, c), lambda i: (0, halo_idx(i), 0)),
                  row_spec, row_spec,
                  pl.BlockSpec((SUBLANES, c), lambda i: (halo_idx(i), 0)),
                  full(conv_w.shape), full(w_a.shape), full(b_a.shape), full(w_x.shape), full(b_x.shape), full(lam.shape)],
        out_specs=[pl.BlockSpec((2, ts, c), lambda i: (0, rev(i), 0)), full(conv_w.shape), full(vec), full(w_a.shape), full(vec),
                   full(w_x.shape), full(vec), full(vec)],
        out_shape=[jax.ShapeDtypeStruct((2, s, c), BF16), jax.ShapeDtypeStruct(conv_w.shape, F32), jax.ShapeDtypeStruct(vec, F32),
                   jax.ShapeDtypeStruct(w_a.shape, F32), jax.ShapeDtypeStruct(vec, F32), jax.ShapeDtypeStruct(w_x.shape, F32),
                   jax.ShapeDtypeStruct(vec, F32), jax.ShapeDtypeStruct(vec, F32)],
        scratch_shapes=[pltpu.VMEM(vec, F32), pltpu.VMEM((SUBLANES, c), F32), pltpu.VMEM((SUBLANES, c), F32),
                        pltpu.VMEM(vec, F32), pltpu.VMEM((ts, c), F32), pltpu.VMEM((ts, c), F32)],
        args=(dy, xg2, xg2, xr, hs, hs, conv_w, w_a, b_a, w_x, b_x, lam))


def _s5_param_fn(a_re, a_im, log_dt, bt_re, bt_im):
    dt = jnp.exp(log_dt)
    mag = jnp.exp(a_re * dt)
    abr = mag * jnp.cos(a_im * dt)
    abi = mag * jnp.sin(a_im * dt)
    ur, ui = abr - 1.0, abi
    den = a_re * a_re + a_im * a_im
    wr = (ur * a_re + ui * a_im) / den
    wi = (ui * a_re - ur * a_im) / den
    bbr = wr[None] * bt_re - wi[None] * bt_im
    bbi = wr[None] * bt_im + wi[None] * bt_re
    return abr, abi, bbr, bbi


def _s5_params(a_re, a_im, log_dt, bt_re, bt_im, nlev):
    g, p = a_re.shape
    gc = bt_re.shape[0]

    def body(ar_ref, ai_ref, dt_ref, br_ref, bi_ref, abr_ref, abi_ref, pr_ref, pi_ref, bbr_ref, bbi_ref):
        abr, abi, bbr, bbi = _s5_param_fn(ar_ref[...], ai_ref[...], dt_ref[...], br_ref[...], bi_ref[...])
        abr_ref[...] = abr
        abi_ref[...] = abi
        bbr_ref[...] = bbr
        bbi_ref[...] = bbi
        qr, qi = abr, abi
        for k in range(nlev):
            pr_ref[k] = qr
            pi_ref[k] = qi
            qr, qi = qr * qr - qi * qi, 2.0 * qr * qi

    sd = jax.ShapeDtypeStruct
    return pl.pallas_call(
        body, name="s5_params",
        out_shape=[sd((g, p), F32), sd((g, p), F32), sd((nlev, g, p), F32), sd((nlev, g, p), F32), sd((gc, g, p), F32),
                   sd((gc, g, p), F32)],
    )(a_re, a_im, log_dt, bt_re, bt_im)


def _s5_params_bwd(a_re, a_im, log_dt, bt_re, bt_im, d_abr, d_abi, d_bbr, d_bbi):
    def body(ar_ref, ai_ref, dt_ref, br_ref, bi_ref, g0, g1, g2, g3, o0, o1, o2, o3, o4):
        _, vjp = jax.vjp(_s5_param_fn, ar_ref[...], ai_ref[...], dt_ref[...], br_ref[...], bi_ref[...])
        outs = vjp((g0[...], g1[...], g2[...], g3[...]))
        for o, v in zip((o0, o1, o2, o3, o4), outs):
            o[...] = v

    sd = jax.ShapeDtypeStruct
    return pl.pallas_call(
        body, name="s5_params_bwd",
        out_shape=[sd(a_re.shape, F32), sd(a_im.shape, F32), sd(log_dt.shape, F32), sd(bt_re.shape, F32), sd(bt_im.shape, F32)],
    )(a_re, a_im, log_dt, bt_re, bt_im, d_abr, d_abi, d_bbr, d_bbi)


def _s5_fwd(u, abr, abi, pw_r, pw_i, bp_r, bp_i, cp_r, cp_i, dvec, ts=128, side=None):
    s, c = u.shape
    n = abr.shape[1]
    nblk, cb, nb = bp_r.shape
    ts = min(ts, s)

    def body(u_ref, ar_ref, ai_ref, pr_ref, pi_ref, bpr_ref, bpi_ref, cpr_ref, cpi_ref, d_ref,
             hr_ref, hi_ref, yp_ref, gy_ref, car_r, car_i):
        i = pl.program_id(0)

        @pl.when(i == 0)
        def _():
            car_r[...] = jnp.zeros_like(car_r)
            car_i[...] = jnp.zeros_like(car_i)

        uv = u_ref[...]
        ub = uv.astype(BF16)
        br = jnp.concatenate([jnp.dot(ub[:, k * cb:(k + 1) * cb], bpr_ref[k], preferred_element_type=F32) for k in range(nblk)], axis=1)
        bi = jnp.concatenate([jnp.dot(ub[:, k * cb:(k + 1) * cb], bpi_ref[k], preferred_element_type=F32) for k in range(nblk)], axis=1)
        ar, ai = ar_ref[...], ai_ref[...]
        pr, pi_ = car_r[SUBLANES - 1:SUBLANES, :], car_i[SUBLANES - 1:SUBLANES, :]
        row = _rows(br.shape)
        br = br + jnp.where(row == 0, ar * pr - ai * pi_, 0.0)
        bi = bi + jnp.where(row == 0, ar * pi_ + ai * pr, 0.0)
        hr, hi = _scan_cplx(br, bi, pr_ref, pi_ref, reverse=False)
        car_r[...] = hr[ts - SUBLANES:]
        car_i[...] = hi[ts - SUBLANES:]
        hr_ref[...] = hr
        hi_ref[...] = hi
        hrb, hib = hr.astype(BF16), hi.astype(BF16)
        y = jnp.concatenate([jnp.dot(hrb[:, k * nb:(k + 1) * nb], cpr_ref[k], preferred_element_type=F32)
                             - jnp.dot(hib[:, k * nb:(k + 1) * nb], cpi_ref[k], preferred_element_type=F32) for k in range(nblk)], axis=1)
        yp = y + d_ref[...] * uv
        yp_ref[...] = yp
        gy_ref[...] = _gelu(yp).astype(BF16)

    full = lambda shape: pl.BlockSpec(shape, lambda i: (0,) * len(shape))
    rc = pl.BlockSpec((ts, c), lambda i: (i, 0))
    rn = pl.BlockSpec((ts, n), lambda i: (i, 0))
    sd = jax.ShapeDtypeStruct
    nt = s // ts
    return _call_with_side(
        body, side, lambda: pl.program_id(0) == 0, lambda: pl.program_id(0) == nt - 1,
        name="s5_fwd", grid=(nt,),
        in_specs=[rc, full(abr.shape), full(abi.shape), full(pw_r.shape), full(pw_i.shape), full(bp_r.shape), full(bp_i.shape),
                  full(cp_r.shape), full(cp_i.shape), full(dvec.shape)],
        out_specs=[rn, rn, rc, rc],
        out_shape=[sd((s, n), F32), sd((s, n), F32), sd((s, c), F32), sd((s, c), BF16)],
        scratch_shapes=[pltpu.VMEM((SUBLANES, n), F32), pltpu.VMEM((SUBLANES, n), F32)],
        args=(u, abr, abi, pw_r, pw_i, bp_r, bp_i, cp_r, cp_i, dvec))


def _s5_bwd(dgy, ypre, u, hr, hi, abr, abi, pw_r, pw_i, bp_r, bp_i, cp_r, cp_i, dvec, ts=128, side=None):
    s, c = u.shape
    n = abr.shape[1]
    nblk, cb, nb = bp_r.shape
    ts = min(ts, s)
    nt = s // ts
    hb = ts // SUBLANES
    tn_dims = (((0,), (0,)), ((), ()))
    nt_dims = (((1,), (1,)), ((), ()))

    def body(dgy_ref, yp_ref, u_ref, hr_ref, hi_ref, hrh_ref, hih_ref, ar_ref, ai_ref, pr_ref, pi_ref, bpr_ref, bpi_ref,
             cpr_ref, cpi_ref, d_ref,
             du_ref, dar_ref, dai_ref, dbr_ref, dbi_ref, dcr_ref, dci_ref, dd_ref, car_r, car_i, npi_ref):
        i = pl.program_id(0)
        time_first = i == nt - 1

        @pl.when(i == 0)
        def _():
            car_r[...] = jnp.zeros_like(car_r)
            car_i[...] = jnp.zeros_like(car_i)
            npi_ref[...] = -pi_ref[...]
            for ref in (dar_ref, dai_ref, dbr_ref, dbi_ref, dcr_ref, dci_ref, dd_ref):
                ref[...] = jnp.zeros_like(ref)

        uv = u_ref[...]
        _, dgel = _gelu_and_grad(yp_ref[...])
        dyv = dgy_ref[...] * dgel
        dd_ref[...] += jnp.sum(dyv * uv, axis=0, keepdims=True)
        dyb = dyv.astype(BF16)
        hr, hi = hr_ref[...], hi_ref[...]
        hrb, hib = hr.astype(BF16), hi.astype(BF16)
        dhr, dhi = [], []
        for k in range(nblk):
            dblk = dyb[:, k * cb:(k + 1) * cb]
            dhr.append(lax.dot_general(dblk, cpr_ref[k], nt_dims, preferred_element_type=F32))
            dhi.append(-lax.dot_general(dblk, cpi_ref[k], nt_dims, preferred_element_type=F32))
            dcr_ref[k] += lax.dot_general(hrb[:, k * nb:(k + 1) * nb], dblk, tn_dims, preferred_element_type=F32)
            dci_ref[k] += lax.dot_general(hib[:, k * nb:(k + 1) * nb], dblk, tn_dims, preferred_element_type=F32)
        dhr = jnp.concatenate(dhr, axis=1)
        dhi = jnp.concatenate(dhi, axis=1)
        ar, ai = ar_ref[...], ai_ref[...]
        nr, ni = car_r[0:1, :], car_i[0:1, :]
        row = _rows(dhr.shape)
        dhr = dhr + jnp.where(row == ts - 1, ar * nr + ai * ni, 0.0)
        dhi = dhi + jnp.where(row == ts - 1, ar * ni - ai * nr, 0.0)
        lr, li = _scan_cplx(dhr, dhi, pr_ref, npi_ref, reverse=True)
        car_r[...] = lr[:SUBLANES]
        car_i[...] = li[:SUBLANES]
        hpr = _shift_down(hr, jnp.where(time_first, 0.0, hrh_ref[...]), 1)
        hpi = _shift_down(hi, jnp.where(time_first, 0.0, hih_ref[...]), 1)
        dar_ref[...] += jnp.sum(lr * hpr + li * hpi, axis=0, keepdims=True)
        dai_ref[...] += jnp.sum(li * hpr - lr * hpi, axis=0, keepdims=True)
        lrb, lib = lr.astype(BF16), li.astype(BF16)
        ub = uv.astype(BF16)
        du = []
        for k in range(nblk):
            ublk = ub[:, k * cb:(k + 1) * cb]
            lrk, lik = lrb[:, k * nb:(k + 1) * nb], lib[:, k * nb:(k + 1) * nb]
            dbr_ref[k] += lax.dot_general(ublk, lrk, tn_dims, preferred_element_type=F32)
            dbi_ref[k] += lax.dot_general(ublk, lik, tn_dims, preferred_element_type=F32)
            du.append(lax.dot_general(lrk, bpr_ref[k], nt_dims, preferred_element_type=F32)
                      + lax.dot_general(lik, bpi_ref[k], nt_dims, preferred_element_type=F32))
        du_ref[...] = (d_ref[...] * dyv + jnp.concatenate(du, axis=1)).astype(BF16)

    full = lambda shape: pl.BlockSpec(shape, lambda i: (0,) * len(shape))
    rev = lambda i: nt - 1 - i
    halo_idx = lambda i: jnp.maximum(rev(i) * hb - 1, 0)
    rc = pl.BlockSpec((ts, c), lambda i: (rev(i), 0))
    rn = pl.BlockSpec((ts, n), lambda i: (rev(i), 0))
    hn = pl.BlockSpec((SUBLANES, n), lambda i: (halo_idx(i), 0))
    sd = jax.ShapeDtypeStruct
    return _call_with_side(
        body, side, lambda: pl.program_id(0) == 0, lambda: pl.program_id(0) == nt - 1,
        name="s5_bwd", grid=(nt,),
        in_specs=[rc, rc, rc, rn, rn, hn, hn, full(abr.shape), full(abi.shape), full(pw_r.shape), full(pw_i.shape),
                  full(bp_r.shape), full(bp_i.shape), full(cp_r.shape), full(cp_i.shape), full(dvec.shape)],
        out_specs=[rc, full(abr.shape), full(abi.shape), full(bp_r.shape), full(bp_i.shape), full(cp_r.shape), full(cp_i.shape),
                   full(dvec.shape)],
        out_shape=[sd((s, c), BF16), sd(abr.shape, F32), sd(abi.shape, F32), sd(bp_r.shape, F32), sd(bp_i.shape, F32),
                   sd(cp_r.shape, F32), sd(cp_i.shape, F32), sd(dvec.shape, F32)],
        scratch_shapes=[pltpu.VMEM((SUBLANES, n), F32), pltpu.VMEM((SUBLANES, n), F32), pltpu.VMEM(pw_i.shape, F32)],
        args=(dgy, ypre, u, hr, hi, hr, hi, abr, abi, pw_r, pw_i, bp_r, bp_i, cp_r, cp_i, dvec))


S5_LANE_CHUNK = 512


def _s5_tables(a_re, a_im, log_dt, bt_re, bt_im):
    g, p = a_re.shape
    gc = bt_re.shape[0]

    def body(ar_ref, ai_ref, dt_ref, br_ref, bi_ref, abr_ref, abi_ref, tr_ref, ti_ref, bbr_ref, bbi_ref):
        abr, abi, bbr, bbi = _s5_param_fn(ar_ref[...], ai_ref[...], dt_ref[...], br_ref[...], bi_ref[...])
        abr_ref[...] = abr
        abi_ref[...] = abi
        bbr_ref[...] = bbr
        bbi_ref[...] = bbi
        pows = [(abr, abi)]
        for _ in range(1, SUBLANES):
            qr, qi = pows[-1]
            pows.append((qr * abr - qi * abi, qr * abi + qi * abr))
        zero = jnp.zeros_like(abr)
        for r in range(SUBLANES):
            for k in range(3):
                sh = 1 << k
                tr_ref[k, r] = pows[sh - 1][0] if r >= sh else zero
                ti_ref[k, r] = pows[sh - 1][1] if r >= sh else zero
            tr_ref[3, r] = pows[r][0]
            ti_ref[3, r] = pows[r][1]

    sd = jax.ShapeDtypeStruct
    return pl.pallas_call(
        body, name="s5_tables",
        out_shape=[sd((g, p), F32), sd((g, p), F32), sd((4, SUBLANES, g, p), F32), sd((4, SUBLANES, g, p), F32),
                   sd((gc, g, p), F32), sd((gc, g, p), F32)],
    )(a_re, a_im, log_dt, bt_re, bt_im)


def _cmul_add(br, bi, tr, ti, sr, si):
    return br + tr * sr - ti * si, bi + tr * si + ti * sr


def _s5_fwd2(u, tab_r, tab_i, bp_r, bp_i, cp_r, cp_i, dvec, ts=256, side=None):
    s, c = u.shape
    n = tab_r.shape[2]
    nblk, cb, nb = bp_r.shape
    ts = min(ts, s)
    nsl = ts // SUBLANES
    lc = min(S5_LANE_CHUNK, n)

    def body(u_ref, tr_ref, ti_ref, bpr_ref, bpi_ref, cpr_ref, cpi_ref, d_ref, hr_ref, hi_ref, yp_ref, gy_ref,
             bur_ref, bui_ref, car_r, car_i):
        i = pl.program_id(0)

        @pl.when(i == 0)
        def _():
            car_r[...] = jnp.zeros_like(car_r)
            car_i[...] = jnp.zeros_like(car_i)

        uv = u_ref[...]
        ub = uv.astype(BF16)
        for k in range(nblk):
            bur_ref[:, k * nb:(k + 1) * nb] = jnp.dot(ub[:, k * cb:(k + 1) * cb], bpr_ref[k], preferred_element_type=F32)
            bui_ref[:, k * nb:(k + 1) * nb] = jnp.dot(ub[:, k * cb:(k + 1) * cb], bpi_ref[k], preferred_element_type=F32)
        for q in range(n // lc):
            sl = slice(q * lc, (q + 1) * lc)
            tabs = [(tr_ref[k, :, sl], ti_ref[k, :, sl]) for k in range(4)]

            def slab(j, carry, sl=sl, tabs=tabs):
                cr, ci = carry
                r0 = pl.multiple_of(j * SUBLANES, SUBLANES)
                br, bi = bur_ref[pl.ds(r0, SUBLANES), sl], bui_ref[pl.ds(r0, SUBLANES), sl]
                for k in range(3):
                    sh = 1 << k
                    br, bi = _cmul_add(br, bi, tabs[k][0], tabs[k][1], pltpu.roll(br, sh, 0), pltpu.roll(bi, sh, 0))
                hr, hi = _cmul_add(br, bi, tabs[3][0], tabs[3][1], jnp.broadcast_to(cr, br.shape), jnp.broadcast_to(ci, bi.shape))
                hr_ref[pl.ds(r0, SUBLANES), sl] = hr
                hi_ref[pl.ds(r0, SUBLANES), sl] = hi
                return hr[SUBLANES - 1:, :], hi[SUBLANES - 1:, :]

            cr, ci = lax.fori_loop(0, nsl, slab, (car_r[:, sl], car_i[:, sl]), unroll=2)
            car_r[:, sl] = cr
            car_i[:, sl] = ci
        hrb, hib = hr_ref[...].astype(BF16), hi_ref[...].astype(BF16)
        y = jnp.concatenate([jnp.dot(hrb[:, k * nb:(k + 1) * nb], cpr_ref[k], preferred_element_type=F32)
                             - jnp.dot(hib[:, k * nb:(k + 1) * nb], cpi_ref[k], preferred_element_type=F32) for k in range(nblk)], axis=1)
        yp = y + d_ref[...] * uv
        yp_ref[...] = yp
        gy_ref[...] = _gelu(yp).astype(BF16)

    full = lambda shape: pl.BlockSpec(shape, lambda i: (0,) * len(shape))
    rc = pl.BlockSpec((ts, c), lambda i: (i, 0))
    rn = pl.BlockSpec((ts, n), lambda i: (i, 0))
    sd = jax.ShapeDtypeStruct
    nt = s // ts
    return _call_with_side(
        body, side, lambda: pl.program_id(0) == 0, lambda: pl.program_id(0) == nt - 1,
        name="s5_fwd", grid=(nt,),
        in_specs=[rc, full(tab_r.shape), full(tab_i.shape), full(bp_r.shape), full(bp_i.shape), full(cp_r.shape), full(cp_i.shape),
                  full(dvec.shape)],
        out_specs=[rn, rn, rc, rc],
        out_shape=[sd((s, n), F32), sd((s, n), F32), sd((s, c), F32), sd((s, c), BF16)],
        scratch_shapes=[pltpu.VMEM((ts, n), F32), pltpu.VMEM((ts, n), F32), pltpu.VMEM((1, n), F32), pltpu.VMEM((1, n), F32)],
        args=(u, tab_r, tab_i, bp_r, bp_i, cp_r, cp_i, dvec))


def _s5_bwd2(dgy, ypre, u, hr, hi, rtab_r, rtab_i, bp_r, bp_i, cp_r, cp_i, dvec, ts=256, side=None):
    s, c = u.shape
    n = rtab_r.shape[2]
    nblk, cb, nb = bp_r.shape
    ts = min(ts, s)
    nt = s // ts
    hb = ts // SUBLANES
    nsl = ts // SUBLANES
    lc = min(S5_LANE_CHUNK, n)
    tn_dims = (((0,), (0,)), ((), ()))
    nt_dims = (((1,), (1,)), ((), ()))

    def body(dgy_ref, yp_ref, u_ref, hr_ref, hi_ref, hrh_ref, hih_ref, tr_ref, ti_ref, bpr_ref, bpi_ref, cpr_ref, cpi_ref, d_ref,
             du_ref, dar_ref, dai_ref, dbr_ref, dbi_ref, dcr_ref, dci_ref, dd_ref, lr_ref, li_ref, car_r, car_i):
        i = pl.program_id(0)
        time_first = i == nt - 1

        @pl.when(i == 0)
        def _():
            car_r[...] = jnp.zeros_like(car_r)
            car_i[...] = jnp.zeros_like(car_i)
            for ref in (dar_ref, dai_ref, dbr_ref, dbi_ref, dcr_ref, dci_ref, dd_ref):
                ref[...] = jnp.zeros_like(ref)

        uv = u_ref[...]
        _, dgel = _gelu_and_grad(yp_ref[...])
        dyv = dgy_ref[...] * dgel
        dd_ref[...] += jnp.sum(dyv * uv, axis=0, keepdims=True)
        dyb = dyv.astype(BF16)
        hrb, hib = hr_ref[...].astype(BF16), hi_ref[...].astype(BF16)
        for k in range(nblk):
            dblk = dyb[:, k * cb:(k + 1) * cb]
            lr_ref[:, k * nb:(k + 1) * nb] = lax.dot_general(dblk, cpr_ref[k], nt_dims, preferred_element_type=F32)
            li_ref[:, k * nb:(k + 1) * nb] = -lax.dot_general(dblk, cpi_ref[k], nt_dims, preferred_element_type=F32)
            dcr_ref[k] += lax.dot_general(hrb[:, k * nb:(k + 1) * nb], dblk, tn_dims, preferred_element_type=F32)
            dci_ref[k] += lax.dot_general(hib[:, k * nb:(k + 1) * nb], dblk, tn_dims, preferred_element_type=F32)
        row8 = _rows((SUBLANES, lc))
        for q in range(n // lc):
            sl = slice(q * lc, (q + 1) * lc)
            tabs = [(tr_ref[k, :, sl], ti_ref[k, :, sl]) for k in range(4)]
            halo_r = jnp.where(time_first, 0.0, hrh_ref[SUBLANES - 1:, sl])
            halo_i = jnp.where(time_first, 0.0, hih_ref[SUBLANES - 1:, sl])

            def slab(jj, carry, sl=sl, tabs=tabs, halo_r=halo_r, halo_i=halo_i):
                nr, ni, acc_r, acc_i = carry
                j = nsl - 1 - jj
                r0 = pl.multiple_of(j * SUBLANES, SUBLANES)
                br, bi = lr_ref[pl.ds(r0, SUBLANES), sl], li_ref[pl.ds(r0, SUBLANES), sl]
                for k in range(3):
                    sh = 1 << k
                    br, bi = _cmul_add(br, bi, tabs[k][0], tabs[k][1], pltpu.roll(br, SUBLANES - sh, 0),
                                       pltpu.roll(bi, SUBLANES - sh, 0))
                lr, li = _cmul_add(br, bi, tabs[3][0], tabs[3][1], jnp.broadcast_to(nr, br.shape), jnp.broadcast_to(ni, bi.shape))
                lr_ref[pl.ds(r0, SUBLANES), sl] = lr
                li_ref[pl.ds(r0, SUBLANES), sl] = li
                p0 = pl.multiple_of(jnp.maximum(j - 1, 0) * SUBLANES, SUBLANES)
                prev_r = jnp.where(j == 0, halo_r, hr_ref[pl.ds(p0, SUBLANES), sl][SUBLANES - 1:, :])
                prev_i = jnp.where(j == 0, halo_i, hi_ref[pl.ds(p0, SUBLANES), sl][SUBLANES - 1:, :])
                hpr = jnp.where(row8 == 0, jnp.broadcast_to(prev_r, br.shape), pltpu.roll(hr_ref[pl.ds(r0, SUBLANES), sl], 1, 0))
                hpi = jnp.where(row8 == 0, jnp.broadcast_to(prev_i, bi.shape), pltpu.roll(hi_ref[pl.ds(r0, SUBLANES), sl], 1, 0))
                return lr[:1, :], li[:1, :], acc_r + (lr * hpr + li * hpi), acc_i + (li * hpr - lr * hpi)

            zero = jnp.zeros((SUBLANES, lc), F32)
            nr, ni, acc_r, acc_i = lax.fori_loop(0, nsl, slab, (car_r[:, sl], car_i[:, sl], zero, zero), unroll=2)
            car_r[:, sl] = nr
            car_i[:, sl] = ni
            dar_ref[:, sl] += jnp.sum(acc_r, axis=0, keepdims=True)
            dai_ref[:, sl] += jnp.sum(acc_i, axis=0, keepdims=True)
        lrb, lib = lr_ref[...].astype(BF16), li_ref[...].astype(BF16)
        ub = uv.astype(BF16)
        du = []
        for k in range(nblk):
            ublk = ub[:, k * cb:(k + 1) * cb]
            lrk, lik = lrb[:, k * nb:(k + 1) * nb], lib[:, k * nb:(k + 1) * nb]
            dbr_ref[k] += lax.dot_general(ublk, lrk, tn_dims, preferred_element_type=F32)
            dbi_ref[k] += lax.dot_general(ublk, lik, tn_dims, preferred_element_type=F32)
            du.append(lax.dot_general(lrk, bpr_ref[k], nt_dims, preferred_element_type=F32)
                      + lax.dot_general(lik, bpi_ref[k], nt_dims, preferred_element_type=F32))
        du_ref[...] = (d_ref[...] * dyv + jnp.concatenate(du, axis=1)).astype(BF16)

    full = lambda shape: pl.BlockSpec(shape, lambda i: (0,) * len(shape))
    rev = lambda i: nt - 1 - i
    halo_idx = lambda i: jnp.maximum(rev(i) * hb - 1, 0)
    rc = pl.BlockSpec((ts, c), lambda i: (rev(i), 0))
    rn = pl.BlockSpec((ts, n), lambda i: (rev(i), 0))
    hn = pl.BlockSpec((SUBLANES, n), lambda i: (halo_idx(i), 0))
    sd = jax.ShapeDtypeStruct
    vec_n = (1, n)
    return _call_with_side(
        body, side, lambda: pl.program_id(0) == 0, lambda: pl.program_id(0) == nt - 1,
        name="s5_bwd", grid=(nt,),
        in_specs=[rc, rc, rc, rn, rn, hn, hn, full(rtab_r.shape), full(rtab_i.shape),
                  full(bp_r.shape), full(bp_i.shape), full(cp_r.shape), full(cp_i.shape), full(dvec.shape)],
        out_specs=[rc, full(vec_n), full(vec_n), full(bp_r.shape), full(bp_i.shape), full(cp_r.shape), full(cp_i.shape),
                   full(dvec.shape)],
        out_shape=[sd((s, c), BF16), sd(vec_n, F32), sd(vec_n, F32), sd(bp_r.shape, F32), sd(bp_i.shape, F32),
                   sd(cp_r.shape, F32), sd(cp_i.shape, F32), sd(dvec.shape, F32)],
        scratch_shapes=[pltpu.VMEM((ts, n), F32), pltpu.VMEM((ts, n), F32), pltpu.VMEM((1, n), F32), pltpu.VMEM((1, n), F32)],
        args=(dgy, ypre, u, hr, hi, hr, hi, rtab_r, rtab_i, bp_r, bp_i, cp_r, cp_i, dvec))


def _glu(gl2, ts=512):
    _, s, c = gl2.shape
    ts = min(ts, s)

    def body(g_ref, o_ref):
        o_ref[...] = (g_ref[0] * _sigmoid(g_ref[1])).astype(BF16)

    return pl.pallas_call(
        body, name="glu", grid=(s // ts,), in_specs=[pl.BlockSpec((2, ts, c), lambda i: (0, i, 0))],
        out_specs=pl.BlockSpec((ts, c), lambda i: (i, 0)), out_shape=jax.ShapeDtypeStruct((s, c), BF16), compiler_params=_cparams(),
    )(gl2)


def _glu_bwd(gl2, d_o, ts=512):
    _, s, c = gl2.shape
    ts = min(ts, s)

    def body(g_ref, do_ref, o_ref):
        sg = _sigmoid(g_ref[1])
        dov = do_ref[...]
        o_ref[0] = (dov * sg).astype(BF16)
        o_ref[1] = (dov * g_ref[0] * sg * (1.0 - sg)).astype(BF16)

    blk = pl.BlockSpec((2, ts, c), lambda i: (0, i, 0))
    return pl.pallas_call(
        body, name="glu_bwd", grid=(s // ts,), in_specs=[blk, pl.BlockSpec((ts, c), lambda i: (i, 0))],
        out_specs=blk, out_shape=jax.ShapeDtypeStruct((2, s, c), BF16), compiler_params=_cparams(),
    )(gl2, d_o)


PACK_ROW_MULTIPLE = 1024
ELEMENTWISE_BLOCK_ELEMS = 256 * 1024


def _row_tile(rows, cols):
    pref = max(SUBLANES, 1 << int(math.log2(max(1, ELEMENTWISE_BLOCK_ELEMS // cols))))
    if rows <= pref:
        return rows
    t = pref
    while rows % t:
        t //= 2
    assert t >= SUBLANES, rows
    return t


def _sum_parts(rs):
    nl = len(rs)
    p, rows, cols = rs[0].shape
    tr = _row_tile(rows, cols)

    def body(*refs):
        o_ref = refs[nl]
        for l in range(nl):
            acc = refs[l][0].astype(F32)
            for k in range(1, p):
                acc = acc + refs[l][k].astype(F32)
            o_ref[l] = acc

    return pl.pallas_call(
        body, name="sum_parts", grid=(rows // tr,), in_specs=[pl.BlockSpec((p, tr, cols), lambda i: (0, i, 0))] * nl,
        out_specs=pl.BlockSpec((nl, tr, cols), lambda i: (0, i, 0)), out_shape=jax.ShapeDtypeStruct((nl, rows, cols), F32),
        compiler_params=_cparams(),
    )(*rs)


def _adamw(w, g_parts, m, v):
    rows, cols = w.shape
    tr = _row_tile(rows, cols)
    ng = len(g_parts)
    c1 = 1.0 / (1.0 - ADAM_B1 ** ADAM_STEP)
    c2 = 1.0 / (1.0 - ADAM_B2 ** ADAM_STEP)

    def body(*refs):
        w_ref, m_ref, v_ref = refs[0], refs[1 + ng], refs[2 + ng]
        g_ref, dl_ref, nm_ref, nv_ref = refs[3 + ng:]
        g = refs[1][...]
        for k in range(1, ng):
            g = g + refs[1 + k][...]
        mn = ADAM_B1 * m_ref[...] + (1.0 - ADAM_B1) * g
        vn = ADAM_B2 * v_ref[...] + (1.0 - ADAM_B2) * (g * g)
        g_ref[...] = g
        nm_ref[...] = mn
        nv_ref[...] = vn
        dl_ref[...] = -ADAM_LR * ((mn * c1) / (jnp.sqrt(vn * c2) + ADAM_EPS) + ADAM_WD * w_ref[...])

    blk = pl.BlockSpec((tr, cols), lambda i: (i, 0))
    sd = jax.ShapeDtypeStruct((rows, cols), F32)
    return pl.pallas_call(
        body, name="adamw", grid=(rows // tr,), in_specs=[blk] * (3 + ng), out_specs=[blk] * 4, out_shape=[sd] * 4,
        compiler_params=_cparams(),
    )(w, *g_parts, m, v)


def _place():
    x, y, c = lax.axis_index("x"), lax.axis_index("y"), lax.axis_index("c")
    chips = [(1 - x, y), (x, 1 - y), (1 - x, 1 - y)]
    return x, y, c, chips


class Side:
    def __init__(self, ins, outs, kind, views=None):
        self.ins, self.outs, self.kind = list(ins), list(outs), kind
        n = len(self.ins)
        self.views = views or [None] * n
        self.sems = [pltpu.SemaphoreType.DMA((3 * n,)), pltpu.SemaphoreType.DMA((3 * n,)), pltpu.SemaphoreType.DMA((n,))]

    def _copies(self, ins, outs, send, recv, lsem):
        x, y, c, chips = _place()
        me = 2 * x + y
        local, out_going, in_coming = [], [], []
        for t in range(len(ins)):
            if self.kind == 'gather':
                src_local, srcs, dst_mine = ins[t], [ins[t]] * 3, outs[t].at[me]
            else:
                part = (lambda p, t=t: self.views[t](ins[t], p)) if self.views[t] else (lambda p, t=t: ins[t].at[p])
                src_local, srcs, dst_mine = part(me), [part(2 * px + py) for px, py in chips], outs[t].at[me]
            local.append(pltpu.make_async_copy(src_local, dst_mine, lsem.at[t]))
            for r, (px, py) in enumerate(chips):
                out_going.append(pltpu.make_async_remote_copy(
                    src_ref=srcs[r], dst_ref=dst_mine, send_sem=send.at[3 * t + r], recv_sem=recv.at[3 * t + r],
                    device_id=(px, py, c), device_id_type=MESH))
                in_coming.append(pltpu.make_async_remote_copy(
                    src_ref=srcs[r], dst_ref=outs[t].at[2 * px + py], send_sem=send.at[3 * t + r], recv_sem=recv.at[3 * t + r],
                    device_id=(px, py, c), device_id_type=MESH))
        return local, out_going, in_coming

    def start(self, ins, outs, send, recv, lsem):
        local, out_going, _ = self._copies(ins, outs, send, recv, lsem)
        for cp in local + out_going:
            cp.start()

    def wait(self, ins, outs, send, recv, lsem):
        local, out_going, in_coming = self._copies(ins, outs, send, recv, lsem)
        for cp in in_coming:
            cp.wait_recv()
        for cp in out_going:
            cp.wait_send()
        for cp in local:
            cp.wait()


def _gather_side(shards):
    return Side(shards, [jax.ShapeDtypeStruct((N_CHIPS,) + s.shape, s.dtype) for s in shards], 'gather')


def _scatter_side(grads, shapes, views):
    return Side(grads, [jax.ShapeDtypeStruct(s, g.dtype) for g, s in zip(grads, shapes)], 'scatter', views)


def _halves_view(ref, p):
    half = ref.shape[2] // 2
    return ref.at[p // 2, :, pl.ds((p % 2) * half, half)]


def _call_with_side(body, side, first, last, *, name, grid, in_specs, out_specs, out_shape, scratch_shapes, args):
    if side is None:
        outs = pl.pallas_call(body, name=name, grid=grid, in_specs=in_specs, out_specs=out_specs, out_shape=out_shape,
                              scratch_shapes=scratch_shapes, compiler_params=_cparams())(*args)
        return outs, []
    n_in, n_out, n_sc = len(in_specs), len(out_specs), len(scratch_shapes)
    ns_in, ns_out = len(side.ins), len(side.outs)

    def wrapped(*refs):
        base_in, s_in = refs[:n_in], refs[n_in:n_in + ns_in]
        o0 = n_in + ns_in
        base_out, s_out = refs[o0:o0 + n_out], refs[o0 + n_out:o0 + n_out + ns_out]
        sc0 = o0 + n_out + ns_out
        base_sc, sems = refs[sc0:sc0 + n_sc], refs[sc0 + n_sc:]

        @pl.when(first())
        def _():
            side.start(s_in, s_out, *sems)

        body(*base_in, *base_out, *base_sc)

        @pl.when(last())
        def _():
            side.wait(s_in, s_out, *sems)

    any_spec = pl.BlockSpec(memory_space=pl.ANY)
    outs = pl.pallas_call(
        wrapped, name=name, grid=grid, in_specs=list(in_specs) + [any_spec] * ns_in, out_specs=list(out_specs) + [any_spec] * ns_out,
        out_shape=list(out_shape) + side.outs, scratch_shapes=list(scratch_shapes) + side.sems, compiler_params=_cparams(),
    )(*args, *side.ins)
    return outs[:n_out], outs[n_out:]


def _run_side(name, side):
    def body(*refs):
        n = len(side.ins)
        side.start(refs[:n], refs[n:2 * n], *refs[2 * n:])
        side.wait(refs[:n], refs[n:2 * n], *refs[2 * n:])

    any_spec = pl.BlockSpec(memory_space=pl.ANY)
    return pl.pallas_call(body, name=name, in_specs=[any_spec] * len(side.ins), out_specs=[any_spec] * len(side.outs),
                          out_shape=side.outs, scratch_shapes=side.sems)(*side.ins)


def _gather_shards(shards, layer_major):
    n = len(shards)

    def body(*refs):
        ins, outs = refs[:n], refs[n:2 * n]
        send, recv, lsem = refs[2 * n:]
        x, y, c, chips = _place()
        me = 2 * x + y

        def slot(t, chip):
            return outs[t].at[:, chip] if layer_major[t] else outs[t].at[chip]

        local, sends = [], []
        for t in range(n):
            cp = pltpu.make_async_copy(ins[t], slot(t, me), lsem.at[t])
            cp.start()
            local.append(cp)
            for r, (px, py) in enumerate(chips):
                rc = pltpu.make_async_remote_copy(src_ref=ins[t], dst_ref=slot(t, me), send_sem=send.at[3 * t + r],
                                                  recv_sem=recv.at[3 * t + r], device_id=(px, py, c), device_id_type=MESH)
                rc.start()
                sends.append(rc)
        for t in range(n):
            for r, (px, py) in enumerate(chips):
                pltpu.make_async_remote_copy(src_ref=ins[t], dst_ref=slot(t, 2 * px + py), send_sem=send.at[3 * t + r],
                                             recv_sem=recv.at[3 * t + r], device_id=(px, py, c), device_id_type=MESH).wait_recv()
        for rc in sends:
            rc.wait_send()
        for cp in local:
            cp.wait()

    any_spec = pl.BlockSpec(memory_space=pl.ANY)
    return pl.pallas_call(
        body, name="gather_shards", in_specs=[any_spec] * n, out_specs=[any_spec] * n,
        out_shape=[jax.ShapeDtypeStruct((s.shape[0], N_CHIPS) + s.shape[1:] if lm else (N_CHIPS,) + s.shape, s.dtype)
                   for s, lm in zip(shards, layer_major)],
        scratch_shapes=[pltpu.SemaphoreType.DMA((3 * n,)), pltpu.SemaphoreType.DMA((3 * n,)), pltpu.SemaphoreType.DMA((n,))],
    )(*shards)


def _scatter_grads(groups):
    flat = [(gi, li, a) for gi, grp in enumerate(groups) for li, a in enumerate(grp)]
    n = len(flat)
    ng = len(groups)

    def body(*refs):
        ins, outs = refs[:n], refs[n:n + ng]
        send, recv, lsem = refs[n + ng:]
        x, y, c, chips = _place()
        me = 2 * x + y
        local, sends = [], []
        for t, (gi, li, _) in enumerate(flat):
            cp = pltpu.make_async_copy(ins[t].at[me], outs[gi].at[me, li], lsem.at[t])
            cp.start()
            local.append(cp)
            for r, (px, py) in enumerate(chips):
                rc = pltpu.make_async_remote_copy(src_ref=ins[t].at[2 * px + py], dst_ref=outs[gi].at[me, li],
                                                  send_sem=send.at[3 * t + r], recv_sem=recv.at[3 * t + r],
                                                  device_id=(px, py, c), device_id_type=MESH)
                rc.start()
                sends.append(rc)
        for t, (gi, li, _) in enumerate(flat):
            for r, (px, py) in enumerate(chips):
                pltpu.make_async_remote_copy(src_ref=ins[t].at[me], dst_ref=outs[gi].at[2 * px + py, li],
                                             send_sem=send.at[3 * t + r], recv_sem=recv.at[3 * t + r],
                                             device_id=(px, py, c), device_id_type=MESH).wait_recv()
        for rc in sends:
            rc.wait_send()
        for cp in local:
            cp.wait()

    any_spec = pl.BlockSpec(memory_space=pl.ANY)
    return pl.pallas_call(
        body, name="scatter_grads", in_specs=[any_spec] * n, out_specs=[any_spec] * ng,
        out_shape=[jax.ShapeDtypeStruct((N_CHIPS, len(grp)) + grp[0].shape[1:], grp[0].dtype) for grp in groups],
        scratch_shapes=[pltpu.SemaphoreType.DMA((3 * n,)), pltpu.SemaphoreType.DMA((3 * n,)), pltpu.SemaphoreType.DMA((n,))],
    )(*[a for _, _, a in flat])


def _swap_with_sibling(arrs):
    n = len(arrs)

    def body(*refs):
        ins, outs = refs[:n], refs[n:2 * n]
        send, recv = refs[2 * n:]
        x, y, c, _ = _place()
        cps = []
        for t in range(n):
            rc = pltpu.make_async_remote_copy(src_ref=ins[t], dst_ref=outs[t], send_sem=send.at[t], recv_sem=recv.at[t],
                                              device_id=(x, y, 1 - c), device_id_type=MESH)
            rc.start()
            cps.append(rc)
        for rc in cps:
            rc.wait_recv()
        for rc in cps:
            rc.wait_send()

    any_spec = pl.BlockSpec(memory_space=pl.ANY)
    return pl.pallas_call(
        body, name="swap_with_sibling", in_specs=[any_spec] * n, out_specs=[any_spec] * n,
        out_shape=[jax.ShapeDtypeStruct(a.shape, a.dtype) for a in arrs],
        scratch_shapes=[pltpu.SemaphoreType.DMA((n,)), pltpu.SemaphoreType.DMA((n,))],
    )(*arrs)


def _allreduce_small(v):
    rows, cols = v.shape

    def body(v_ref, o_ref, sib_ref, chip_ref, send, recv):
        x, y, c, chips = _place()
        me = 2 * x + y
        d2d = pltpu.make_async_remote_copy(src_ref=v_ref, dst_ref=sib_ref, send_sem=send.at[0], recv_sem=recv.at[0],
                                           device_id=(x, y, 1 - c), device_id_type=MESH)
        d2d.start()
        d2d.wait_recv()
        chip_ref[me] = v_ref[...] + sib_ref[...]
        sends = []
        for r, (px, py) in enumerate(chips):
            rc = pltpu.make_async_remote_copy(src_ref=chip_ref.at[me], dst_ref=chip_ref.at[me], send_sem=send.at[1 + r],
                                              recv_sem=recv.at[1 + r], device_id=(px, py, c), device_id_type=MESH)
            rc.start()
            sends.append(rc)
        for r, (px, py) in enumerate(chips):
            pltpu.make_async_remote_copy(src_ref=chip_ref.at[me], dst_ref=chip_ref.at[2 * px + py], send_sem=send.at[1 + r],
                                         recv_sem=recv.at[1 + r], device_id=(px, py, c), device_id_type=MESH).wait_recv()
        o_ref[...] = (chip_ref[0] + chip_ref[1]) + (chip_ref[2] + chip_ref[3])
        d2d.wait_send()
        for rc in sends:
            rc.wait_send()

    vm = pl.BlockSpec(memory_space=pltpu.VMEM)
    return pl.pallas_call(
        body, name="allreduce_small", in_specs=[vm], out_specs=vm, out_shape=jax.ShapeDtypeStruct((rows, cols), F32),
        scratch_shapes=[pltpu.VMEM((rows, cols), F32), pltpu.VMEM((N_CHIPS, rows, cols), F32), pltpu.SemaphoreType.DMA((4,)),
                        pltpu.SemaphoreType.DMA((4,))],
        compiler_params=_cparams(),
    )(v)


def _pack(tensors):
    pieces = []
    for t in tensors:
        flat = t.reshape(-1)
        pad = (-flat.shape[0]) % (SUBLANES * LANES)
        pieces.append(jnp.pad(flat, (0, pad)).reshape(-1, LANES))
    rows = sum(p.shape[0] for p in pieces)
    pieces.append(jnp.zeros(((-rows) % PACK_ROW_MULTIPLE, LANES), tensors[0].dtype))
    return jnp.concatenate(pieces, axis=0)


def _unpack(buf, like):
    out, off = [], 0
    for t in like:
        size = math.prod(t.shape)
        rows = -(-size // (SUBLANES * LANES)) * SUBLANES
        out.append(buf[off:off + rows].reshape(-1)[:size].reshape(t.shape))
        off += rows
    return out


def _s5_pack_b(bb):
    gc, g, p = bb.shape
    q = S5_GROUPS_PER_BLOCK
    t = bb.reshape(gc, g // q, q, p).transpose(1, 2, 0, 3)
    eye = jnp.eye(q, dtype=bb.dtype)
    return (t[:, :, :, None, :] * eye[None, :, None, :, None]).reshape(g // q, q * gc, q * p)


def _s5_unpack_b(dbp, gc, p):
    nb = dbp.shape[0]
    q = S5_GROUPS_PER_BLOCK
    eye = jnp.eye(q, dtype=dbp.dtype)
    t = (dbp.reshape(nb, q, gc, q, p) * eye[None, :, None, :, None]).sum(axis=3)
    return t.transpose(2, 0, 1, 3).reshape(gc, nb * q, p)


def _s5_pack_c(cc):
    g, gc, p = cc.shape
    q = S5_GROUPS_PER_BLOCK
    t = cc.reshape(g // q, q, gc, p).transpose(0, 1, 3, 2)
    eye = jnp.eye(q, dtype=cc.dtype)
    return (t[:, :, :, None, :] * eye[None, :, None, :, None]).reshape(g // q, q * p, q * gc)


def _s5_unpack_c(dcp, gc, p):
    nb = dcp.shape[0]
    q = S5_GROUPS_PER_BLOCK
    eye = jnp.eye(q, dtype=dcp.dtype)
    t = (dcp.reshape(nb, q, p, q, gc) * eye[None, :, None, :, None]).sum(axis=3)
    return t.transpose(0, 1, 3, 2).reshape(nb * q, gc, p)


def _split2(m):
    return m.arr[:, 0]


def kernel(x, norm_mix_g, norm_ffn_g, norm_final_g, rg_w_in, rg_conv_w, rg_conv_b, rg_w_a, rg_b_a, rg_w_x, rg_b_x, rg_lambda, rg_w_out, s5_w_in, s5_a_re, s5_a_im, s5_log_dt, s5_b_re, s5_b_im, s5_c_re, s5_c_im, s5_d, s5_w_glu, s5_w_out, ffn_w_up, ffn_conv_w, ffn_conv_b, ffn_w_down, loss_target, m_norm_mix_g, m_norm_ffn_g, m_norm_final_g, m_rg_w_in, m_rg_conv_w, m_rg_conv_b, m_rg_w_a, m_rg_b_a, m_rg_w_x, m_rg_b_x, m_rg_lambda, m_rg_w_out, m_s5_w_in, m_s5_a_re, m_s5_a_im, m_s5_log_dt, m_s5_b_re, m_s5_b_im, m_s5_c_re, m_s5_c_im, m_s5_d, m_s5_w_glu, m_s5_w_out, m_ffn_w_up, m_ffn_conv_w, m_ffn_conv_b, m_ffn_w_down, v_norm_mix_g, v_norm_ffn_g, v_norm_final_g, v_rg_w_in, v_rg_conv_w, v_rg_conv_b, v_rg_w_a, v_rg_b_a, v_rg_w_x, v_rg_b_x, v_rg_lambda, v_rg_w_out, v_s5_w_in, v_s5_a_re, v_s5_a_im, v_s5_log_dt, v_s5_b_re, v_s5_b_im, v_s5_c_re, v_s5_c_im, v_s5_d, v_s5_w_glu, v_s5_w_out, v_ffn_w_up, v_ffn_conv_w, v_ffn_conv_b, v_ffn_w_down):
    w = dict(zip(PARAM_NAMES, (norm_mix_g, norm_ffn_g, norm_final_g, rg_w_in, rg_conv_w, rg_conv_b, rg_w_a, rg_b_a, rg_w_x, rg_b_x,
                               rg_lambda, rg_w_out, s5_w_in, s5_a_re, s5_a_im, s5_log_dt, s5_b_re, s5_b_im, s5_c_re, s5_c_im, s5_d,
                               s5_w_glu, s5_w_out, ffn_w_up, ffn_conv_w, ffn_conv_b, ffn_w_down)))
    mom = dict(zip(PARAM_NAMES, (m_norm_mix_g, m_norm_ffn_g, m_norm_final_g, m_rg_w_in, m_rg_conv_w, m_rg_conv_b, m_rg_w_a, m_rg_b_a,
                                 m_rg_w_x, m_rg_b_x, m_rg_lambda, m_rg_w_out, m_s5_w_in, m_s5_a_re, m_s5_a_im, m_s5_log_dt, m_s5_b_re,
                                 m_s5_b_im, m_s5_c_re, m_s5_c_im, m_s5_d, m_s5_w_glu, m_s5_w_out, m_ffn_w_up, m_ffn_conv_w,
                                 m_ffn_conv_b, m_ffn_w_down)))
    vel = dict(zip(PARAM_NAMES, (v_norm_mix_g, v_norm_ffn_g, v_norm_final_g, v_rg_w_in, v_rg_conv_w, v_rg_conv_b, v_rg_w_a, v_rg_b_a,
                                 v_rg_w_x, v_rg_b_x, v_rg_lambda, v_rg_w_out, v_s5_w_in, v_s5_a_re, v_s5_a_im, v_s5_log_dt, v_s5_b_re,
                                 v_s5_b_im, v_s5_c_re, v_s5_c_im, v_s5_d, v_s5_w_glu, v_s5_w_out, v_ffn_w_up, v_ffn_conv_w,
                                 v_ffn_conv_b, v_ffn_w_down)))
    _, s, d = x.shape
    depth = norm_mix_g.shape[0]
    n_grp, n_state = s5_a_re.shape[1], s5_a_re.shape[2]
    gc = s5_b_re.shape[3]
    d_ff = ffn_w_down.shape[1] * N_CHIPS
    s5_ts = min(256, s)

    wb = {n: (w[n].astype(BF16) if n in BIG else w[n]) for n in SHARDED}
    gath = {}

    def mixer_keys(i):
        return [(n, i // 2) for n in MIXER_SHARDED[i % 2]] if i < depth else []

    def gather_side(keys):
        return _gather_side([wb[n][l] for n, l in keys])

    def put(keys, arrs):
        for k, a in zip(keys, arrs):
            gath[k] = a

    def wcol(n, l):
        return Mat(gath[(n, l)][:, None], 0, 'c')

    def wrow(n, l):
        g = gath[(n, l)]
        return Mat(g.reshape(1, 1, N_CHIPS * g.shape[1], g.shape[2]), 0, 'c')

    def rg_cw(l):
        return gath[('rg_conv_w', l)].transpose(1, 0, 2).reshape(RG_CONV_W, d)

    def s5_dv(l):
        return gath[('s5_d', l)].reshape(1, d)

    def f_cw(l):
        return gath[('ffn_conv_w', l)].transpose(1, 0, 2).reshape(FFN_CONV_W, 2, d_ff).transpose(1, 0, 2)

    tm = min(1024, s)
    d_up = 2 * d_ff // N_CHIPS
    f_cb = ffn_conv_b.reshape(depth, 2, 1, d_ff)
    put(mixer_keys(0), _run_side("gather_first", gather_side(mixer_keys(0))))

    h = x.reshape(s, d)
    saved = []
    for i in range(depth):
        j = i // 2
        sv = {'h_in': h}
        hn = _rms_fwd(h, norm_mix_g[i:i + 1])
        sv['hn'] = hn
        up_keys = [('ffn_w_up', i), ('ffn_conv_w', i)] + ([('ffn_w_down', i)] if i == 0 else [])
        if i % 2 == 0:
            xg = _mm("rg_in", 'nn', act(hn), wcol('rg_w_in', j), out_parts=2, tm=tm, tn=512, tk=d)
            xg2 = _split2(xg)
            wa, wx = rg_w_a[j].astype(BF16), rg_w_x[j].astype(BF16)
            ba, bx = rg_b_a[j].reshape(1, d), rg_b_x[j].reshape(1, d)
            (xr, hs, y), got = _rg_fwd(xg2, rg_cw(j), rg_conv_b[j:j + 1], wa, ba, wx, bx, rg_lambda[j:j + 1],
                                       side=gather_side(up_keys))
            put(up_keys, got)
            sv.update(xg2=xg2, xr=xr, hs=hs, y=y, wa=wa, wx=wx, ba=ba, bx=bx)
            h = _mm("rg_out", 'nn', act(y), wrow('rg_w_out', j), res=act(h), tm=tm, tn=d, tk=d).arr[0, 0]
        else:
            u = _mm("s5_in", 'nn', act(hn), wrow('s5_w_in', j), tm=tm, tn=d, tk=d).arr[0, 0]
            bt_re, bt_im = s5_b_re[j].transpose(2, 0, 1), s5_b_im[j].transpose(2, 0, 1)
            ldt = s5_log_dt[j].reshape(n_grp, 1)
            _, _, tab_r, tab_i, bbr, bbi = _s5_tables(s5_a_re[j], s5_a_im[j], ldt, bt_re, bt_im)
            nn_ = n_grp * n_state
            tab_r, tab_i = tab_r.reshape(4, SUBLANES, nn_), tab_i.reshape(4, SUBLANES, nn_)
            prm = dict(bp_r=_s5_pack_b(bbr).astype(BF16), bp_i=_s5_pack_b(bbi).astype(BF16),
                       cp_r=_s5_pack_c(s5_c_re[j]).astype(BF16), cp_i=_s5_pack_c(s5_c_im[j]).astype(BF16), dvec=s5_dv(j))
            (hr, hi, ypre, gy), got = _s5_fwd2(u, tab_r, tab_i, ts=s5_ts, side=gather_side(up_keys), **prm)
            sv.update(rtab_r=tab_r[:, ::-1], rtab_i=-tab_i[:, ::-1])
            put(up_keys, got)
            gl = _mm("s5_glu", 'nn', act(gy), wcol('s5_w_glu', j), out_parts=2, tm=tm, tn=512, tk=d)
            gl2 = _split2(gl)
            o = _glu(gl2)
            sv.update(u=u, prm=prm, hr=hr, hi=hi, ypre=ypre, gy=gy, gl2=gl2, o=o, bt_re=bt_re, bt_im=bt_im, ldt=ldt)
            h = _mm("s5_out", 'nn', act(o), wrow('s5_w_out', j), res=act(h), tm=tm, tn=d, tk=d).arr[0, 0]
        sv['h_mid'] = h
        next_keys = mixer_keys(i + 1) + ([('ffn_w_down', i + 1)] if i + 1 < depth else [])
        (h, hn2, up2, c2, a_ffn), got = _ffn_fwd(h, norm_ffn_g[i:i + 1], gath[('ffn_w_up', i)],
                                                 gath[('ffn_w_down', i)].reshape(d_ff, d), f_cw(i), f_cb[i],
                                                 side=gather_side(next_keys) if next_keys else None)
        put(next_keys, got)
        sv.update(hn2=hn2, up2=up2, c2=c2, act=a_ffn)
        saved.append(sv)

    loss_row, dh, dg_final = _loss_and_grad(h, norm_final_g.reshape(1, d), loss_target.reshape(s, d))
    loss = lax.psum(loss_row[0, 0], ("x", "y", "c"))

    gl_ = {n: [None] * w[n].shape[0] for n in PARAM_NAMES if n != 'norm_final_g'}
    recvd = {}

    def as4(n, a):
        return a.reshape((N_CHIPS,) + w[n].shape[1:])

    def scatter_side(keys):
        arrs, shapes, views = [], [], []
        for n, l in keys:
            shape = (N_CHIPS,) + w[n].shape[1:]
            halves = n == 'ffn_w_up'
            arrs.append(gl_[n][l] if halves else gl_[n][l].reshape(shape))
            views.append(_halves_view if halves else None)
            shapes.append(shape)
        return _scatter_side(arrs, shapes, views)

    def record(keys, arrs):
        for k, a in zip(keys, arrs):
            recvd[k] = a

    pending = None
    for i in reversed(range(depth)):
        j = i // 2
        sv = saved[i]
        (dup2, dcw2, dcb2, dwu, dwd), got = _ffn_bwd_all(dh, gath[('ffn_w_down', i)].reshape(d_ff, d), sv['up2'], sv['c2'], sv['hn2'],
                                                         sv['act'], f_cw(i), side=scatter_side(pending) if pending else None)
        if pending:
            record(pending, got)
        gl_['ffn_w_up'][i], gl_['ffn_w_down'][i] = dwu, dwd
        gl_['ffn_conv_w'][i] = dcw2.transpose(1, 0, 2).reshape(FFN_CONV_W, 2 * d_ff)
        gl_['ffn_conv_b'][i] = dcb2.reshape(2 * d_ff)
        dup = Mat(dup2[:, None], 0, 'c')
        dh, dg = _mm_rms_bwd("ffn_up_dx", dup, wcol('ffn_w_up', i), sv['h_mid'], norm_ffn_g[i:i + 1], dh, tm=tm, tk=d_up)
        gl_['norm_ffn_g'][i] = dg[0]
        ffn_keys = [('ffn_w_up', i), ('ffn_w_down', i)]
        if i % 2 == 0:
            dy = _mm("rg_out_dx", 'nt', act(dh), wrow('rg_w_out', j), tm=tm, tn=d, tk=d).arr[0, 0]
            gl_['rg_w_out'][j] = _mm("rg_out_dw", 'tn', act(sv['y']), act(dh), out_dtype=BF16, tm=d, tn=d, tk=tm).arr
            (dxg2, dcw, dcb, dwa, dba, dwx, dbx, dlam), got = _rg_bwd(
                dy, sv['xg2'], sv['xr'], sv['hs'], rg_cw(j), sv['wa'], sv['ba'], sv['wx'], sv['bx'], rg_lambda[j:j + 1],
                side=scatter_side(ffn_keys))
            record(ffn_keys, got)
            gl_['rg_conv_w'][j] = dcw
            gl_['rg_conv_b'][j] = dcb[0]
            gl_['rg_w_a'][j], gl_['rg_w_x'][j] = dwa, dwx
            gl_['rg_b_a'][j], gl_['rg_b_x'][j] = dba.reshape(rg_b_a.shape[1:]), dbx.reshape(rg_b_x.shape[1:])
            gl_['rg_lambda'][j] = dlam[0]
            dxg = Mat(dxg2[:, None], 0, 'c')
            gl_['rg_w_in'][j] = _mm("rg_in_dw", 'tn', act(sv['hn']), dxg, out_parts=N_CHIPS, out_dtype=BF16, tm=d, tn=512, tk=tm).arr
            mix_dx = ("rg_in_dx", dxg, wcol('rg_w_in', j), 512)
            pending = [('rg_w_in', j), ('rg_w_out', j)]
        else:
            d_o = _mm("s5_out_dx", 'nt', act(dh), wrow('s5_w_out', j), tm=tm, tn=d, tk=d).arr[0, 0]
            gl_['s5_w_out'][j] = _mm("s5_out_dw", 'tn', act(sv['o']), act(dh), out_dtype=BF16, tm=d, tn=d, tk=tm).arr
            dgl2 = _glu_bwd(sv['gl2'], d_o)
            dgl = Mat(dgl2[:, None], 0, 'c')
            gl_['s5_w_glu'][j] = _mm("s5_glu_dw", 'tn', act(sv['gy']), dgl, out_parts=N_CHIPS, out_dtype=BF16, tm=d, tn=512, tk=tm).arr
            dgy = _mm("s5_glu_dx", 'nt', dgl, wcol('s5_w_glu', j), tm=tm, tn=d, tk=512).arr[0, 0]
            (du, dar, dai, dbpr, dbpi, dcpr, dcpi, dd), got = _s5_bwd2(
                dgy, sv['ypre'], sv['u'], sv['hr'], sv['hi'], sv['rtab_r'], sv['rtab_i'], ts=s5_ts, side=scatter_side(ffn_keys),
                **sv['prm'])
            record(ffn_keys, got)
            gl_['s5_d'][j] = dd[0]
            gl_['s5_c_re'][j] = _s5_unpack_c(dcpr, gc, n_state)
            gl_['s5_c_im'][j] = -_s5_unpack_c(dcpi, gc, n_state)
            d_are, d_aim, d_ldt, d_btr, d_bti = _s5_params_bwd(
                s5_a_re[j], s5_a_im[j], sv['ldt'], sv['bt_re'], sv['bt_im'], dar.reshape(n_grp, n_state), dai.reshape(n_grp, n_state),
                _s5_unpack_b(dbpr, gc, n_state), _s5_unpack_b(dbpi, gc, n_state))
            gl_['s5_a_re'][j], gl_['s5_a_im'][j], gl_['s5_log_dt'][j] = d_are, d_aim, d_ldt[:, 0]
            gl_['s5_b_re'][j], gl_['s5_b_im'][j] = d_btr.transpose(1, 2, 0), d_bti.transpose(1, 2, 0)
            dum = act(du)
            gl_['s5_w_in'][j] = _mm("s5_in_dw", 'tn', act(sv['hn']), dum, out_dtype=BF16, tm=d, tn=d, tk=tm).arr
            mix_dx = ("s5_in_dx", dum, wrow('s5_w_in', j), d)
            pending = [('s5_w_in', j), ('s5_w_glu', j), ('s5_w_out', j)]
        dh, dg = _mm_rms_bwd(mix_dx[0], mix_dx[1], mix_dx[2], sv['h_in'], norm_mix_g[i:i + 1], dh, tm=tm, tk=mix_dx[3])
        gl_['norm_mix_g'][i] = dg[0]
    grad_x = dh.reshape(x.shape)
    record(pending, _run_side("scatter_last", scatter_side(pending)))

    chip_sums = []
    for n in BIG:
        cols = w[n].shape[-1]
        chip_sums.append(_sum_parts([recvd[(n, l)].reshape(N_CHIPS, -1, cols) for l in range(w[n].shape[0])]))
    sib_sums = _swap_with_sibling(chip_sums)
    results = {}
    for n, mine, theirs in zip(BIG, chip_sums, sib_sums):
        cols = w[n].shape[-1]
        outs = _adamw(w[n].reshape(-1, cols), [mine.reshape(-1, cols), theirs.reshape(-1, cols)], mom[n].reshape(-1, cols),
                      vel[n].reshape(-1, cols))
        results[n] = [o.reshape(w[n].shape) for o in outs]

    small = REPLICATED + SMALL_SHARDED
    local = [dg_final.reshape(d) if n == 'norm_final_g' else jnp.stack(gl_[n]) for n in small]
    summed = _unpack(_allreduce_small(_pack(local)), local)
    me = 2 * lax.axis_index("x") + lax.axis_index("y")
    grads = [lax.dynamic_slice_in_dim(g, me * w[n].shape[-1], w[n].shape[-1], axis=g.ndim - 1) if n in SMALL_SHARDED else g
             for n, g in zip(small, summed)]
    like = [w[n] for n in small]
    outs = _adamw(_pack(like), [_pack(grads)], _pack([mom[n] for n in small]), _pack([vel[n] for n in small]))
    unpacked = [_unpack(o, like) for o in outs]
    for k, n in enumerate(small):
        results[n] = [unpacked[q][k] for q in range(4)]

    return (loss, grad_x, *[results[n][0] for n in PARAM_NAMES], *[results[n][1] for n in PARAM_NAMES],
            *[results[n][2] for n in PARAM_NAMES], *[results[n][3] for n in PARAM_NAMES])
```

```python
import functools
import math

import jax
import jax.numpy as jnp
from jax import lax
from jax.experimental import pallas as pl
from jax.experimental.pallas import tpu as pltpu

F32 = jnp.float32
BF16 = jnp.bfloat16
MESH = pl.DeviceIdType.MESH

NORM_EPS = 1e-6
RG_HEADS = 8
RG_CONV_W = 4
RG_C = 8.0
S5_GC = 16
S5_P = 64
S5_GROUPS_PER_BLOCK = 8
FFN_CONV_W = 3
N_CHIPS = 4
ADAM_LR, ADAM_B1, ADAM_B2, ADAM_EPS, ADAM_WD, ADAM_STEP = 0.001, 0.9, 0.999, 1e-08, 0.01, 10
VMEM_LIMIT_BYTES = 56 * 1024 * 1024
SUBLANES = 8
LANES = 128

PARAM_NAMES = ['norm_mix_g', 'norm_ffn_g', 'norm_final_g', 'rg_w_in', 'rg_conv_w', 'rg_conv_b', 'rg_w_a', 'rg_b_a', 'rg_w_x',
               'rg_b_x', 'rg_lambda', 'rg_w_out', 's5_w_in', 's5_a_re', 's5_a_im', 's5_log_dt', 's5_b_re', 's5_b_im', 's5_c_re',
               's5_c_im', 's5_d', 's5_w_glu', 's5_w_out', 'ffn_w_up', 'ffn_conv_w', 'ffn_conv_b', 'ffn_w_down']
SHARDED = ['rg_w_in', 'rg_conv_w', 'rg_w_out', 's5_w_in', 's5_d', 's5_w_glu', 's5_w_out', 'ffn_w_up', 'ffn_conv_w', 'ffn_w_down']
BIG = ['rg_w_in', 'rg_w_out', 's5_w_in', 's5_w_glu', 's5_w_out', 'ffn_w_up', 'ffn_w_down']
ROW_SHARDED = ['rg_w_out', 's5_w_in', 's5_w_out', 'ffn_w_down']
SMALL_SHARDED = ['rg_conv_w', 's5_d', 'ffn_conv_w']
MIXER_SHARDED = [['rg_w_in', 'rg_conv_w', 'rg_w_out'], ['s5_w_in', 's5_d', 's5_w_glu', 's5_w_out']]
FFN_SHARDED = ['ffn_w_up', 'ffn_conv_w', 'ffn_w_down']
REPLICATED = [n for n in PARAM_NAMES if n not in SHARDED]


def _cparams():
    return pltpu.CompilerParams(vmem_limit_bytes=VMEM_LIMIT_BYTES)


_GELU_C = math.sqrt(2.0 / math.pi)
_GELU_K = 0.044715


def _gelu(x):
    return 0.5 * x * (1.0 + jnp.tanh(_GELU_C * (x + _GELU_K * x * x * x)))


def _gelu_and_grad(x):
    t = jnp.tanh(_GELU_C * (x + _GELU_K * x * x * x))
    g = 0.5 * x * (1.0 + t)
    dg = 0.5 * (1.0 + t) + 0.5 * x * (1.0 - t * t) * (_GELU_C * (1.0 + 3.0 * _GELU_K * x * x))
    return g, dg


def _sigmoid(x):
    return jax.nn.sigmoid(x)


def _neg_expm1(x):
    series = -(x * (1.0 + x * (0.5 + x * (1.0 / 6 + x * (1.0 / 24 + x * (1.0 / 120 + x * (1.0 / 720)))))))
    return jnp.where(x > -0.25, series, 1.0 - jnp.exp(x))


def _softplus(z):
    return jnp.maximum(z, 0.0) + jnp.log1p(jnp.exp(-jnp.abs(z)))


def _rows(shape):
    return lax.broadcasted_iota(jnp.int32, shape, 0)


def _shift_down(x, halo, k):
    ext = jnp.concatenate([halo, x], axis=0)
    return pltpu.roll(ext, k, 0)[SUBLANES:]


def _shift_up(x, halo, k):
    ext = jnp.concatenate([x, halo], axis=0)
    n = ext.shape[0]
    return pltpu.roll(ext, n - k, 0)[:x.shape[0]]


def _scan_real_fwd(a, b):
    n = a.shape[0]
    row = _rows(a.shape)
    sh = 1
    while sh < n:
        ok = row >= sh
        b = a * jnp.where(ok, pltpu.roll(b, sh, 0), 0.0) + b
        if sh * 2 < n:
            a = a * jnp.where(ok, pltpu.roll(a, sh, 0), 1.0)
        sh *= 2
    return b


def _scan_real_rev(c, d):
    n = c.shape[0]
    row = _rows(c.shape)
    sh = 1
    while sh < n:
        ok = row < n - sh
        d = c * jnp.where(ok, pltpu.roll(d, n - sh, 0), 0.0) + d
        if sh * 2 < n:
            c = c * jnp.where(ok, pltpu.roll(c, n - sh, 0), 1.0)
        sh *= 2
    return d


def _scan_cplx(br, bi, pr_ref, pi_ref, reverse):
    n = br.shape[0]
    row = _rows(br.shape)
    sh, k = 1, 0
    while sh < n:
        pr = pr_ref[k:k + 1, :]
        pi = pi_ref[k:k + 1, :]
        if reverse:
            ok = row < n - sh
            sr = jnp.where(ok, pltpu.roll(br, n - sh, 0), 0.0)
            si = jnp.where(ok, pltpu.roll(bi, n - sh, 0), 0.0)
        else:
            ok = row >= sh
            sr = jnp.where(ok, pltpu.roll(br, sh, 0), 0.0)
            si = jnp.where(ok, pltpu.roll(bi, sh, 0), 0.0)
        br, bi = br + pr * sr - pi * si, bi + pr * si + pi * sr
        sh *= 2
        k += 1
    return br, bi


RG_LANE_CHUNK = 512


def _real_slab_scan(a_ref, b_ref, out_ref, carry_ref, reverse):
    t, c = a_ref.shape
    nsl = t // SUBLANES
    lc = min(RG_LANE_CHUNK, c)
    row8 = _rows((SUBLANES, lc))
    for q in range(c // lc):
        sl = slice(q * lc, (q + 1) * lc)

        def slab(jj, carry, sl=sl):
            j = nsl - 1 - jj if reverse else jj
            r0 = pl.multiple_of(j * SUBLANES, SUBLANES)
            a, b = a_ref[pl.ds(r0, SUBLANES), sl], b_ref[pl.ds(r0, SUBLANES), sl]
            for k in range(3):
                sh = 1 << k
                keep = row8 < SUBLANES - sh if reverse else row8 >= sh
                amount = SUBLANES - sh if reverse else sh
                b = a * jnp.where(keep, pltpu.roll(b, amount, 0), 0.0) + b
                a = a * jnp.where(keep, pltpu.roll(a, amount, 0), 1.0)
            x = b + a * jnp.broadcast_to(carry, b.shape)
            out_ref[pl.ds(r0, SUBLANES), sl] = x
            return x[:1, :] if reverse else x[SUBLANES - 1:, :]

        carry_ref[:, sl] = lax.fori_loop(0, nsl, slab, carry_ref[:, sl], unroll=2)


class Mat:
    def __init__(self, arr, l=0, split='c'):
        assert arr.ndim == 4
        self.arr, self.l, self.split = arr, l, split
        p, _, r, c = arr.shape
        self.shape = (r, c * p) if split == 'c' else (r * p, c)

    def spec(self, tr, tc, rc):
        p, _, r, c = self.arr.shape
        l = self.l
        assert r % tr == 0 and c % tc == 0, (self.arr.shape, tr, tc)
        if self.split == 'c':
            per = c // tc
            return pl.BlockSpec((None, None, tr, tc), lambda i, j, k: (rc(i, j, k)[1] // per, l, rc(i, j, k)[0], rc(i, j, k)[1] % per))
        per = r // tr
        return pl.BlockSpec((None, None, tr, tc), lambda i, j, k: (rc(i, j, k)[0] // per, l, rc(i, j, k)[0] % per, rc(i, j, k)[1]))


def act(x, parts=1):
    s, c = x.shape
    return Mat(x.reshape(s, parts, c // parts).transpose(1, 0, 2)[:, None] if parts > 1 else x[None, None])


def _mm(name, mode, a, b, *, out_parts=1, out_split='c', out_dtype=F32, res=None, tm=512, tn=512, tk=512):
    if mode == 'nn':
        (m, kk), (kb, n) = a.shape, b.shape
    elif mode == 'nt':
        (m, kk), (n, kb) = a.shape, b.shape
    else:
        (kk, m), (kb, n) = a.shape, b.shape
    assert kk == kb, (name, a.shape, b.shape)
    tm, tn, tk = min(tm, m), min(tn, n), min(tk, kk)
    assert m % tm == 0 and n % tn == 0 and kk % tk == 0, (name, m, n, kk, tm, tn, tk)
    nk = kk // tk
    if mode == 'nn':
        a_spec = a.spec(tm, tk, lambda i, j, k: (i, k))
        b_spec = b.spec(tk, tn, lambda i, j, k: (k, j))
        dims = (((1,), (0,)), ((), ()))
    elif mode == 'nt':
        a_spec = a.spec(tm, tk, lambda i, j, k: (i, k))
        b_spec = b.spec(tn, tk, lambda i, j, k: (j, k))
        dims = (((1,), (1,)), ((), ()))
    else:
        a_spec = a.spec(tk, tm, lambda i, j, k: (k, i))
        b_spec = b.spec(tk, tn, lambda i, j, k: (k, j))
        dims = (((0,), (0,)), ((), ()))
    if out_split == 'c':
        out_arr = jax.ShapeDtypeStruct((out_parts, 1, m, n // out_parts), out_dtype)
    else:
        out_arr = jax.ShapeDtypeStruct((out_parts, 1, m // out_parts, n), out_dtype)
    out_mat = Mat(out_arr, 0, out_split)
    o_spec = out_mat.spec(tm, tn, lambda i, j, k: (i, j))
    has_res = res is not None

    def body(*refs):
        if has_res:
            a_ref, b_ref, r_ref, o_ref = refs[:4]
        else:
            a_ref, b_ref, o_ref = refs[:3]
        prod = lax.dot_general(a_ref[...].astype(BF16), b_ref[...].astype(BF16), dims, preferred_element_type=F32)

        def finish(acc):
            if has_res:
                acc = acc + r_ref[...]
            o_ref[...] = acc.astype(out_dtype)

        if nk == 1:
            finish(prod)
        else:
            acc_ref = refs[-1]
            k = pl.program_id(2)

            @pl.when(k == 0)
            def _():
                acc_ref[...] = prod

            @pl.when(k > 0)
            def _():
                acc_ref[...] += prod

            @pl.when(k == nk - 1)
            def _():
                finish(acc_ref[...])

    in_specs = [a_spec, b_spec]
    args = [a.arr, b.arr]
    if has_res:
        in_specs.append(res.spec(tm, tn, lambda i, j, k: (i, j)))
        args.append(res.arr)
    out = pl.pallas_call(
        body, name=name, grid=(m // tm, n // tn, nk), in_specs=in_specs, out_specs=o_spec, out_shape=out_arr,
        scratch_shapes=[pltpu.VMEM((tm, tn), F32)] if nk > 1 else [], compiler_params=_cparams(),
    )(*args)
    return Mat(out, 0, out_split)


def _rms_fwd(h, g, ts=512):
    s, d = h.shape
    ts = min(ts, s)

    def body(h_ref, g_ref, o_ref):
        x = h_ref[...]
        var = jnp.mean(x * x, axis=-1, keepdims=True)
        o_ref[...] = (x * lax.rsqrt(var + NORM_EPS) * g_ref[...]).astype(BF16)

    return pl.pallas_call(
        body, name="rms_fwd", grid=(s // ts,),
        in_specs=[pl.BlockSpec((ts, d), lambda i: (i, 0)), pl.BlockSpec((1, d), lambda i: (0, 0))],
        out_specs=pl.BlockSpec((ts, d), lambda i: (i, 0)), out_shape=jax.ShapeDtypeStruct((s, d), BF16),
        compiler_params=_cparams(),
    )(h, g)


def _rms_bwd(h, g, dhn, dh_in, ts=512):
    s, d = h.shape
    ts = min(ts, s)

    def body(h_ref, g_ref, dhn_ref, dhin_ref, dh_ref, dg_ref):
        i = pl.program_id(0)
        x = h_ref[...]
        rstd = lax.rsqrt(jnp.mean(x * x, axis=-1, keepdims=True) + NORM_EPS)
        xhat = x * rstd
        dhn_v = dhn_ref[...]
        dxh = dhn_v * g_ref[...]
        dh_ref[...] = dhin_ref[...] + rstd * (dxh - xhat * jnp.mean(dxh * xhat, axis=-1, keepdims=True))
        part = jnp.sum(dhn_v * xhat, axis=0, keepdims=True)

        @pl.when(i == 0)
        def _():
            dg_ref[...] = part

        @pl.when(i > 0)
        def _():
            dg_ref[...] += part

    row = pl.BlockSpec((ts, d), lambda i: (i, 0))
    vec = pl.BlockSpec((1, d), lambda i: (0, 0))
    return pl.pallas_call(
        body, name="rms_bwd", grid=(s // ts,), in_specs=[row, vec, row, row], out_specs=[row, vec],
        out_shape=[jax.ShapeDtypeStruct((s, d), F32), jax.ShapeDtypeStruct((1, d), F32)], compiler_params=_cparams(),
    )(h, g, dhn, dh_in)


def _loss_and_grad(h, g, tgt, ts=512):
    s, d = h.shape
    ts = min(ts, s)

    def body(h_ref, g_ref, t_ref, loss_ref, dh_ref, dg_ref):
        i = pl.program_id(0)
        x = h_ref[...]
        gv = g_ref[...]
        rstd = lax.rsqrt(jnp.mean(x * x, axis=-1, keepdims=True) + NORM_EPS)
        xhat = x * rstd
        err = xhat * gv - t_ref[...]
        dy = err * (1.0 / d)
        dxh = dy * gv
        dh_ref[...] = rstd * (dxh - xhat * jnp.mean(dxh * xhat, axis=-1, keepdims=True))
        part = jnp.sum(dy * xhat, axis=0, keepdims=True)
        lpart = jnp.broadcast_to(jnp.sum(jnp.sum(err * err, axis=0, keepdims=True), axis=1, keepdims=True) * (0.5 / d), (1, LANES))

        @pl.when(i == 0)
        def _():
            dg_ref[...] = part
            loss_ref[...] = lpart

        @pl.when(i > 0)
        def _():
            dg_ref[...] += part
            loss_ref[...] += lpart

    row = pl.BlockSpec((ts, d), lambda i: (i, 0))
    vec = pl.BlockSpec((1, d), lambda i: (0, 0))
    return pl.pallas_call(
        body, name="loss_and_grad", grid=(s // ts,), in_specs=[row, vec, row],
        out_specs=[pl.BlockSpec((1, LANES), lambda i: (0, 0)), row, vec],
        out_shape=[jax.ShapeDtypeStruct((1, LANES), F32), jax.ShapeDtypeStruct((s, d), F32), jax.ShapeDtypeStruct((1, d), F32)],
        compiler_params=_cparams(),
    )(h, g, tgt)


def _halo_before(ts, nrow8):
    return lambda i: jnp.maximum(i * (ts // SUBLANES) - 1, 0)


def _ffn_act(up2, conv_w2, conv_b2, ts=512, tn=512, side=None):
    _, s, f = up2.shape
    ts, tn = min(ts, s), min(tn, f)
    kw = FFN_CONV_W

    def body(up_ref, halo_ref, w_ref, b_ref, o_ref):
        i = pl.program_id(0)
        cs = []
        for h in range(2):
            x = up_ref[h]
            halo = jnp.where(i == 0, 0.0, halo_ref[h])
            c = b_ref[h] + w_ref[h, kw - 1:kw, :] * x
            for sft in range(1, kw):
                c = c + w_ref[h, kw - 1 - sft:kw - sft, :] * _shift_down(x, halo, sft)
            cs.append(c)
        o_ref[...] = (_gelu(cs[0]) * cs[1]).astype(BF16)

    hb = ts // SUBLANES
    g0, g1 = s // ts, f // tn
    outs, side_outs = _call_with_side(
        body, side, lambda: (pl.program_id(0) == 0) & (pl.program_id(1) == 0),
        lambda: (pl.program_id(0) == g0 - 1) & (pl.program_id(1) == g1 - 1),
        name="ffn_act", grid=(g0, g1),
        in_specs=[pl.BlockSpec((2, ts, tn), lambda i, j: (0, i, j)),
                  pl.BlockSpec((2, SUBLANES, tn), lambda i, j: (0, jnp.maximum(i * hb - 1, 0), j)),
                  pl.BlockSpec((2, kw, tn), lambda i, j: (0, 0, j)),
                  pl.BlockSpec((2, 1, tn), lambda i, j: (0, 0, j))],
        out_specs=[pl.BlockSpec((ts, tn), lambda i, j: (i, j))], out_shape=[jax.ShapeDtypeStruct((s, f), BF16)],
        scratch_shapes=[], args=(up2, up2, conv_w2, conv_b2))
    return outs[0], side_outs


def _ffn_bwd(up2, dact, conv_w2, conv_b2, ts=256, tn=512, side=None):
    _, s, f = up2.shape
    ts, tn = min(ts, s), min(tn, f)
    kw = FFN_CONV_W
    nt = s // ts
    hb = ts // SUBLANES
    last8 = s // SUBLANES - 1

    def body(up_ref, hb_ref, ha_ref, da_ref, dah_ref, w_ref, b_ref, dup_ref, dw_ref, db_ref):
        i = pl.program_id(1)
        first, last = i == 0, i == nt - 1
        ce, xs = [], []
        for h in range(2):
            x = up_ref[h]
            before = jnp.where(first, 0.0, hb_ref[h])
            after = ha_ref[h]
            ext = jnp.concatenate([before, x, after], axis=0)
            c = b_ref[h] + w_ref[h, kw - 1:kw, :] * ext
            shifted = [ext]
            for sft in range(1, kw):
                sh = pltpu.roll(ext, sft, 0)
                shifted.append(sh)
                c = c + w_ref[h, kw - 1 - sft:kw - sft, :] * sh
            ce.append(c[SUBLANES:])
            xs.append([sh[SUBLANES:SUBLANES + ts] for sh in shifted])
        da = jnp.concatenate([da_ref[...], jnp.where(last, 0.0, dah_ref[...])], axis=0)
        g1, dg1 = _gelu_and_grad(ce[0])
        dcs = [da * ce[1] * dg1, da * g1]
        for h in range(2):
            dc = dcs[h]
            n = dc.shape[0]
            dup = w_ref[h, kw - 1:kw, :] * dc[:ts]
            for sft in range(1, kw):
                dup = dup + w_ref[h, kw - 1 - sft:kw - sft, :] * pltpu.roll(dc, n - sft, 0)[:ts]
            dup_ref[h] = dup.astype(BF16)
            dct = dc[:ts]
            dbp = jnp.sum(dct, axis=0, keepdims=True)
            dwp = [jnp.sum(dct * xs[h][kw - 1 - k], axis=0, keepdims=True) for k in range(kw)]

            @pl.when(first)
            def _():
                db_ref[h] = dbp
                for k in range(kw):
                    dw_ref[h, k:k + 1, :] = dwp[k]

            @pl.when(i > 0)
            def _():
                db_ref[h] += dbp
                for k in range(kw):
                    dw_ref[h, k:k + 1, :] += dwp[k]

    g0 = f // tn
    return _call_with_side(
        body, side, lambda: (pl.program_id(0) == 0) & (pl.program_id(1) == 0),
        lambda: (pl.program_id(0) == g0 - 1) & (pl.program_id(1) == nt - 1),
        name="ffn_bwd", grid=(g0, nt),
        in_specs=[pl.BlockSpec((2, ts, tn), lambda j, i: (0, i, j)),
                  pl.BlockSpec((2, SUBLANES, tn), lambda j, i: (0, jnp.maximum(i * hb - 1, 0), j)),
                  pl.BlockSpec((2, SUBLANES, tn), lambda j, i: (0, jnp.minimum((i + 1) * hb, last8), j)),
                  pl.BlockSpec((ts, tn), lambda j, i: (i, j)),
                  pl.BlockSpec((SUBLANES, tn), lambda j, i: (jnp.minimum((i + 1) * hb, last8), j)),
                  pl.BlockSpec((2, kw, tn), lambda j, i: (0, 0, j)),
                  pl.BlockSpec((2, 1, tn), lambda j, i: (0, 0, j))],
        out_specs=[pl.BlockSpec((2, ts, tn), lambda j, i: (0, i, j)),
                   pl.BlockSpec((2, kw, tn), lambda j, i: (0, 0, j)),
                   pl.BlockSpec((2, 1, tn), lambda j, i: (0, 0, j))],
        out_shape=[jax.ShapeDtypeStruct((2, s, f), BF16), jax.ShapeDtypeStruct((2, kw, f), F32),
                   jax.ShapeDtypeStruct((2, 1, f), F32)],
        scratch_shapes=[], args=(up2, up2, up2, dact, dact, conv_w2, conv_b2))


def _mm_rms_bwd(name, a, b, h, g, dh_in, *, tm, tk):
    (m, kk), (n, kb) = a.shape, b.shape
    assert kk == kb and h.shape == (m, n), (name, a.shape, b.shape, h.shape)
    tm, tk = min(tm, m), min(tk, kk)
    nk = kk // tk
    dims = (((1,), (1,)), ((), ()))

    def body(a_ref, b_ref, h_ref, g_ref, dhin_ref, dh_ref, dg_ref, *acc):
        i, k = pl.program_id(0), pl.program_id(2)
        prod = lax.dot_general(a_ref[...].astype(BF16), b_ref[...].astype(BF16), dims, preferred_element_type=F32)

        def finish(dhn):
            x = h_ref[...]
            rstd = lax.rsqrt(jnp.mean(x * x, axis=-1, keepdims=True) + NORM_EPS)
            xhat = x * rstd
            dxh = dhn * g_ref[...]
            dh_ref[...] = dhin_ref[...] + rstd * (dxh - xhat * jnp.mean(dxh * xhat, axis=-1, keepdims=True))
            part = jnp.sum(dhn * xhat, axis=0, keepdims=True)

            @pl.when(i == 0)
            def _():
                dg_ref[...] = part

            @pl.when(i > 0)
            def _():
                dg_ref[...] += part

        if nk == 1:
            finish(prod)
        else:
            acc_ref = acc[0]

            @pl.when(k == 0)
            def _():
                acc_ref[...] = prod

            @pl.when(k > 0)
            def _():
                acc_ref[...] += prod

            @pl.when(k == nk - 1)
            def _():
                finish(acc_ref[...])

    row = pl.BlockSpec((tm, n), lambda i, j, k: (i, 0))
    vec = pl.BlockSpec((1, n), lambda i, j, k: (0, 0))
    return pl.pallas_call(
        body, name=name, grid=(m // tm, 1, nk),
        in_specs=[a.spec(tm, tk, lambda i, j, k: (i, k)), b.spec(n, tk, lambda i, j, k: (0, k)), row, vec, row],
        out_specs=[row, vec], out_shape=[jax.ShapeDtypeStruct((m, n), F32), jax.ShapeDtypeStruct((1, n), F32)],
        scratch_shapes=[pltpu.VMEM((tm, n), F32)] if nk > 1 else [], compiler_params=_cparams(),
    )(a.arr, b.arr, h, g, dh_in)


def _ffn_up_act(hn2, w_up4, conv_w2, conv_b2, ts=1024, tn=512, sub=256, side=None):
    s, d = hn2.shape
    p, _, wc = w_up4.shape
    f = p * wc // 2
    ts, tn = min(ts, s), min(tn, wc)
    sub = min(sub, ts)
    per = wc // tn
    kw = FFN_CONV_W
    g0, g1 = f // tn, s // ts

    def body(hn_ref, w1_ref, w2_ref, cw_ref, cb_ref, up_ref, c_ref, act_ref, carry_ref):
        @pl.when(pl.program_id(1) == 0)
        def _():
            carry_ref[...] = jnp.zeros_like(carry_ref)

        for q in range(ts // sub):
            rows = slice(q * sub, (q + 1) * sub)
            hn = hn_ref[rows, :]
            cs = []
            for h, w_ref in enumerate((w1_ref, w2_ref)):
                x = jnp.dot(hn, w_ref[...], preferred_element_type=F32)
                up_ref[h, rows, :] = x
                halo = carry_ref[h]
                c = cb_ref[h] + cw_ref[h, kw - 1:kw, :] * x
                for sft in range(1, kw):
                    c = c + cw_ref[h, kw - 1 - sft:kw - sft, :] * _shift_down(x, halo, sft)
                carry_ref[h] = x[sub - SUBLANES:, :]
                c_ref[h, rows, :] = c
                cs.append(c)
            act_ref[rows, :] = (_gelu(cs[0]) * cs[1]).astype(BF16)

    outs, side_outs = _call_with_side(
        body, side, lambda: (pl.program_id(0) == 0) & (pl.program_id(1) == 0),
        lambda: (pl.program_id(0) == g0 - 1) & (pl.program_id(1) == g1 - 1),
        name="ffn_up_act", grid=(g0, g1),
        in_specs=[pl.BlockSpec((ts, d), lambda j, i: (i, 0)),
                  pl.BlockSpec((None, d, tn), lambda j, i: (j // per, 0, j % per)),
                  pl.BlockSpec((None, d, tn), lambda j, i: (p // 2 + j // per, 0, j % per)),
                  pl.BlockSpec((2, kw, tn), lambda j, i: (0, 0, j)),
                  pl.BlockSpec((2, 1, tn), lambda j, i: (0, 0, j))],
        out_specs=[pl.BlockSpec((2, ts, tn), lambda j, i: (0, i, j)), pl.BlockSpec((2, ts, tn), lambda j, i: (0, i, j)),
                   pl.BlockSpec((ts, tn), lambda j, i: (i, j))],
        out_shape=[jax.ShapeDtypeStruct((2, s, f), F32), jax.ShapeDtypeStruct((2, s, f), F32), jax.ShapeDtypeStruct((s, f), BF16)],
        scratch_shapes=[pltpu.VMEM((2, SUBLANES, tn), F32)], args=(hn2, w_up4, w_up4, conv_w2, conv_b2))
    return outs, side_outs


def _ffn_bwd_fused(dh, w_down, up2, c2, conv_w2, ts=256, tn=512, side=None):
    s, d = dh.shape
    _, _, f = up2.shape
    ts, tn = min(ts, s), min(tn, f)
    kw = FFN_CONV_W
    nt = s // ts
    hb = ts // SUBLANES
    g0 = f // tn
    nt_dims = (((1,), (1,)), ((), ()))

    def body(dh_ref, wd_ref, up_ref, c_ref, w_ref, dup_ref, dw_ref, db_ref, carry_ref):
        i = pl.program_id(1)
        first_step = i == 0

        @pl.when(first_step)
        def _():
            carry_ref[...] = jnp.zeros_like(carry_ref)

        da = lax.dot_general(dh_ref[...].astype(BF16), wd_ref[...], nt_dims, preferred_element_type=F32)
        g1, dg1 = _gelu_and_grad(c_ref[0])
        dcs = [da * c_ref[1] * dg1, da * g1]
        for h in range(2):
            dc = dcs[h]
            after = carry_ref[h]
            ups = [dc] + [_shift_up(dc, after, sft) for sft in range(1, kw)]
            dup = w_ref[h, kw - 1:kw, :] * dc
            for sft in range(1, kw):
                dup = dup + w_ref[h, kw - 1 - sft:kw - sft, :] * ups[sft]
            carry_ref[h] = dc[:SUBLANES]
            dup_ref[h] = dup.astype(BF16)
            dbp = jnp.sum(dc, axis=0, keepdims=True)
            x = up_ref[h]
            dwp = [jnp.sum(ups[kw - 1 - k] * x, axis=0, keepdims=True) for k in range(kw)]

            @pl.when(first_step)
            def _():
                db_ref[h] = dbp
                for k in range(kw):
                    dw_ref[h, k:k + 1, :] = dwp[k]

            @pl.when(i > 0)
            def _():
                db_ref[h] += dbp
                for k in range(kw):
                    dw_ref[h, k:k + 1, :] += dwp[k]

    rev = lambda i: nt - 1 - i
    return _call_with_side(
        body, side, lambda: (pl.program_id(0) == 0) & (pl.program_id(1) == 0),
        lambda: (pl.program_id(0) == g0 - 1) & (pl.program_id(1) == nt - 1),
        name="ffn_bwd", grid=(g0, nt),
        in_specs=[pl.BlockSpec((ts, d), lambda j, i: (rev(i), 0)),
                  pl.BlockSpec((tn, d), lambda j, i: (j, 0)),
                  pl.BlockSpec((2, ts, tn), lambda j, i: (0, rev(i), j)),
                  pl.BlockSpec((2, ts, tn), lambda j, i: (0, rev(i), j)),
                  pl.BlockSpec((2, kw, tn), lambda j, i: (0, 0, j))],
        out_specs=[pl.BlockSpec((2, ts, tn), lambda j, i: (0, rev(i), j)),
                   pl.BlockSpec((2, kw, tn), lambda j, i: (0, 0, j)),
                   pl.BlockSpec((2, 1, tn), lambda j, i: (0, 0, j))],
        out_shape=[jax.ShapeDtypeStruct((2, s, f), BF16), jax.ShapeDtypeStruct((2, kw, f), F32),
                   jax.ShapeDtypeStruct((2, 1, f), F32)],
        scratch_shapes=[pltpu.VMEM((2, SUBLANES, tn), F32)], args=(dh, w_down, up2, c2, conv_w2))


def _ffn_fwd(h, g, w_up4, w_down, conv_w2, conv_b2, ts=512, tn=512, sub=256, side=None):
    s, d = h.shape
    p, _, wc = w_up4.shape
    f = p * wc // 2
    ts, tn = min(ts, s), min(tn, wc)
    sub = min(sub, ts)
    per = wc // tn
    kw = FFN_CONV_W
    g0, g1 = s // ts, f // tn

    def body(h_ref, g_ref, w1_ref, w2_ref, wd_ref, cw_ref, cb_ref, ho_ref, hn_ref, up_ref, c_ref, act_ref, carry_ref):
        i, j = pl.program_id(0), pl.program_id(1)

        @pl.when(j == 0)
        def _():
            x = h_ref[...]
            var = jnp.mean(x * x, axis=-1, keepdims=True)
            hn_ref[...] = (x * lax.rsqrt(var + NORM_EPS) * g_ref[...]).astype(BF16)
            ho_ref[...] = x

        @pl.when(i == 0)
        def _():
            carry_ref[j] = jnp.zeros(carry_ref.shape[1:], F32)

        for q in range(ts // sub):
            rows = slice(q * sub, (q + 1) * sub)
            hn = hn_ref[rows, :]
            cs = []
            for hf, w_ref in enumerate((w1_ref, w2_ref)):
                x = jnp.dot(hn, w_ref[...], preferred_element_type=F32)
                up_ref[hf, rows, :] = x
                halo = carry_ref[j, hf]
                c = cb_ref[hf] + cw_ref[hf, kw - 1:kw, :] * x
                for sft in range(1, kw):
                    c = c + cw_ref[hf, kw - 1 - sft:kw - sft, :] * _shift_down(x, halo, sft)
                carry_ref[j, hf] = x[sub - SUBLANES:, :]
                c_ref[hf, rows, :] = c
                cs.append(c)
            a = (_gelu(cs[0]) * cs[1]).astype(BF16)
            act_ref[rows, :] = a
            ho_ref[rows, :] += jnp.dot(a, wd_ref[...], preferred_element_type=F32)

    row = pl.BlockSpec((ts, d), lambda i, j: (i, 0))
    col2 = pl.BlockSpec((2, ts, tn), lambda i, j: (0, i, j))
    return _call_with_side(
        body, side, lambda: (pl.program_id(0) == 0) & (pl.program_id(1) == 0),
        lambda: (pl.program_id(0) == g0 - 1) & (pl.program_id(1) == g1 - 1),
        name="ffn_fwd", grid=(g0, g1),
        in_specs=[row, pl.BlockSpec((1, d), lambda i, j: (0, 0)),
                  pl.BlockSpec((None, d, tn), lambda i, j: (j // per, 0, j % per)),
                  pl.BlockSpec((None, d, tn), lambda i, j: (p // 2 + j // per, 0, j % per)),
                  pl.BlockSpec((tn, d), lambda i, j: (j, 0)),
                  pl.BlockSpec((2, kw, tn), lambda i, j: (0, 0, j)),
                  pl.BlockSpec((2, 1, tn), lambda i, j: (0, 0, j))],
        out_specs=[row, row, col2, col2, pl.BlockSpec((ts, tn), lambda i, j: (i, j))],
        out_shape=[jax.ShapeDtypeStruct((s, d), F32), jax.ShapeDtypeStruct((s, d), BF16), jax.ShapeDtypeStruct((2, s, f), F32),
                   jax.ShapeDtypeStruct((2, s, f), F32), jax.ShapeDtypeStruct((s, f), BF16)],
        scratch_shapes=[pltpu.VMEM((g1, 2, SUBLANES, tn), F32)],
        args=(h, g, w_up4, w_up4, w_down, conv_w2, conv_b2))


def _ffn_bwd_all(dh, w_down, up2, c2, hn, act_, conv_w2, ts=256, tn=512, sub=128, side=None):
    s, d = dh.shape
    _, _, f = up2.shape
    ts, tn = min(ts, s), min(tn, f)
    sub = min(sub, ts)
    kw = FFN_CONV_W
    nt = s // ts
    g0 = f // tn
    nt_dims = (((1,), (1,)), ((), ()))
    tn_dims = (((0,), (0,)), ((), ()))

    def body(dh_ref, wd_ref, up_ref, c_ref, hn_ref, act_ref, w_ref, dup_ref, dw_ref, db_ref, dwu_ref, dwd_ref,
             carry_ref, dwu_acc, dwd_acc):
        i = pl.program_id(1)
        first_step = i == 0

        @pl.when(first_step)
        def _():
            carry_ref[...] = jnp.zeros_like(carry_ref)
            dwu_acc[...] = jnp.zeros_like(dwu_acc)
            dwd_acc[...] = jnp.zeros_like(dwd_acc)
            dw_ref[...] = jnp.zeros_like(dw_ref)
            db_ref[...] = jnp.zeros_like(db_ref)

        dhb = dh_ref[...].astype(BF16)
        da_all = lax.dot_general(dhb, wd_ref[...], nt_dims, preferred_element_type=F32)
        for q in reversed(range(ts // sub)):
            rows = slice(q * sub, (q + 1) * sub)
            da = da_all[rows, :]
            g1, dg1 = _gelu_and_grad(c_ref[0, rows, :])
            dcs = [da * c_ref[1, rows, :] * dg1, da * g1]
            hnq = hn_ref[rows, :]
            for hf in range(2):
                dc = dcs[hf]
                after = carry_ref[hf]
                ups = [dc] + [_shift_up(dc, after, sft) for sft in range(1, kw)]
                dup = w_ref[hf, kw - 1:kw, :] * dc
                for sft in range(1, kw):
                    dup = dup + w_ref[hf, kw - 1 - sft:kw - sft, :] * ups[sft]
                carry_ref[hf] = dc[:SUBLANES]
                dupb = dup.astype(BF16)
                dup_ref[hf, rows, :] = dupb
                dwu_acc[hf] += lax.dot_general(hnq, dupb, tn_dims, preferred_element_type=F32)
                db_ref[hf] += jnp.sum(dc, axis=0, keepdims=True)
                x = up_ref[hf, rows, :]
                for k in range(kw):
                    dw_ref[hf, k:k + 1, :] += jnp.sum(ups[kw - 1 - k] * x, axis=0, keepdims=True)
            dwd_acc[...] += lax.dot_general(act_ref[rows, :], dhb[rows, :], tn_dims, preferred_element_type=F32)

        @pl.when(i == nt - 1)
        def _():
            dwu_ref[...] = dwu_acc[...].astype(BF16)
            dwd_ref[...] = dwd_acc[...].astype(BF16)

    rev = lambda i: nt - 1 - i
    col2 = pl.BlockSpec((2, ts, tn), lambda j, i: (0, rev(i), j))
    return _call_with_side(
        body, side, lambda: (pl.program_id(0) == 0) & (pl.program_id(1) == 0),
        lambda: (pl.program_id(0) == g0 - 1) & (pl.program_id(1) == nt - 1),
        name="ffn_bwd", grid=(g0, nt),
        in_specs=[pl.BlockSpec((ts, d), lambda j, i: (rev(i), 0)),
                  pl.BlockSpec((tn, d), lambda j, i: (j, 0)),
                  col2, col2,
                  pl.BlockSpec((ts, d), lambda j, i: (rev(i), 0)),
                  pl.BlockSpec((ts, tn), lambda j, i: (rev(i), j)),
                  pl.BlockSpec((2, kw, tn), lambda j, i: (0, 0, j))],
        out_specs=[col2,
                   pl.BlockSpec((2, kw, tn), lambda j, i: (0, 0, j)),
                   pl.BlockSpec((2, 1, tn), lambda j, i: (0, 0, j)),
                   pl.BlockSpec((2, d, tn), lambda j, i: (0, 0, j)),
                   pl.BlockSpec((tn, d), lambda j, i: (j, 0))],
        out_shape=[jax.ShapeDtypeStruct((2, s, f), BF16), jax.ShapeDtypeStruct((2, kw, f), F32),
                   jax.ShapeDtypeStruct((2, 1, f), F32), jax.ShapeDtypeStruct((2, d, f), BF16), jax.ShapeDtypeStruct((f, d), BF16)],
        scratch_shapes=[pltpu.VMEM((2, SUBLANES, tn), F32), pltpu.VMEM((2, d, tn), F32), pltpu.VMEM((tn, d), F32)],
        args=(dh, w_down, up2, c2, hn, act_, conv_w2))


def _rg_gates(xr, wa_ref, ba_ref, wx_ref, bx_ref, lam_ref):
    bw = wa_ref.shape[-1]
    xb = xr.astype(BF16)
    za = jnp.concatenate([jnp.dot(xb[:, h * bw:(h + 1) * bw], wa_ref[h], preferred_element_type=F32)
                          for h in range(RG_HEADS)], axis=1) + ba_ref[...]
    zx = jnp.concatenate([jnp.dot(xb[:, h * bw:(h + 1) * bw], wx_ref[h], preferred_element_type=F32)
                          for h in range(RG_HEADS)], axis=1) + bx_ref[...]
    r, ig = _sigmoid(za), _sigmoid(zx)
    sp = _softplus(-lam_ref[...])
    la = -RG_C * r * sp
    a = jnp.exp(la)
    mult = jnp.sqrt(_neg_expm1(2.0 * la))
    return xb, r, ig, sp, a, mult


def _rg_fwd(xg2, conv_w, conv_b, w_a, b_a, w_x, b_x, lam, ts=256, side=None):
    _, s, c = xg2.shape
    ts = min(ts, s)
    kw = RG_CONV_W
    hb = ts // SUBLANES

    def body(xg_ref, halo_ref, cw_ref, cb_ref, wa_ref, ba_ref, wx_ref, bx_ref, lam_ref, xr_ref, hs_ref, y_ref, carry_ref,
             a_scr, b_scr):
        i = pl.program_id(0)

        @pl.when(i == 0)
        def _():
            carry_ref[...] = jnp.zeros_like(carry_ref)

        xp = xg_ref[0]
        halo = jnp.where(i == 0, 0.0, halo_ref[...])
        xr = cb_ref[...] + cw_ref[kw - 1:kw, :] * xp
        for sft in range(1, kw):
            xr = xr + cw_ref[kw - 1 - sft:kw - sft, :] * _shift_down(xp, halo, sft)
        _, r, ig, sp, a, mult = _rg_gates(xr, wa_ref, ba_ref, wx_ref, bx_ref, lam_ref)
        a_scr[...] = a
        b_scr[...] = mult * (ig * xr)
        _real_slab_scan(a_scr, b_scr, hs_ref, carry_ref, reverse=False)
        xr_ref[...] = xr
        y_ref[...] = (hs_ref[...] * _gelu(xg_ref[1])).astype(BF16)

    full = lambda shape: pl.BlockSpec(shape, lambda i: (0,) * len(shape))
    row_spec = pl.BlockSpec((ts, c), lambda i: (i, 0))
    nt = s // ts
    return _call_with_side(
        body, side, lambda: pl.program_id(0) == 0, lambda: pl.program_id(0) == nt - 1,
        name="rg_fwd", grid=(nt,),
        in_specs=[pl.BlockSpec((2, ts, c), lambda i: (0, i, 0)),
                  pl.BlockSpec((None, SUBLANES, c), lambda i: (0, jnp.maximum(i * hb - 1, 0), 0)),
                  full(conv_w.shape), full(conv_b.shape), full(w_a.shape), full(b_a.shape), full(w_x.shape), full(b_x.shape),
                  full(lam.shape)],
        out_specs=[row_spec, row_spec, row_spec],
        out_shape=[jax.ShapeDtypeStruct((s, c), F32), jax.ShapeDtypeStruct((s, c), F32), jax.ShapeDtypeStruct((s, c), BF16)],
        scratch_shapes=[pltpu.VMEM((1, c), F32), pltpu.VMEM((ts, c), F32), pltpu.VMEM((ts, c), F32)],
        args=(xg2, xg2, conv_w, conv_b, w_a, b_a, w_x, b_x, lam))


def _rg_bwd(dy, xg2, xr, hs, conv_w, w_a, b_a, w_x, b_x, lam, ts=256, side=None):
    _, s, c = xg2.shape
    ts = min(ts, s)
    nt = s // ts
    kw = RG_CONV_W
    hb = ts // SUBLANES
    bw = c // RG_HEADS
    tn_dims = (((0,), (0,)), ((), ()))
    nt_dims = (((1,), (1,)), ((), ()))

    def body(dy_ref, xg_ref, xph_ref, xr_ref, hs_ref, hsh_ref, cw_ref, wa_ref, ba_ref, wx_ref, bx_ref, lam_ref,
             dxg_ref, dcw_ref, dcb_ref, dwa_ref, dba_ref, dwx_ref, dbx_ref, dlam_ref,
             lam_carry, a_carry, dxr_carry, dsp_acc, a_scr, b_scr):
        i = pl.program_id(0)
        first_step = i == 0
        time_first = i == nt - 1

        @pl.when(first_step)
        def _():
            lam_carry[...] = jnp.zeros_like(lam_carry)
            a_carry[...] = jnp.ones_like(a_carry)
            dxr_carry[...] = jnp.zeros_like(dxr_carry)
            dsp_acc[...] = jnp.zeros_like(dsp_acc)
            for ref in (dcw_ref, dcb_ref, dwa_ref, dba_ref, dwx_ref, dbx_ref):
                ref[...] = jnp.zeros_like(ref)

        xr = xr_ref[...]
        hs = hs_ref[...]
        gate = xg_ref[1]
        xb, r, ig, sp, a, mult = _rg_gates(xr, wa_ref, ba_ref, wx_ref, bx_ref, lam_ref)
        dyv = dy_ref[...]
        gg, dgg = _gelu_and_grad(gate)
        dhs = dyv * gg
        dxg_ref[1] = (dyv * hs * dgg).astype(BF16)
        row = _rows(xr.shape)
        a_scr[...] = jnp.where(row == ts - 1, a_carry[0:1, :], pltpu.roll(a, ts - 1, 0))
        b_scr[...] = dhs
        _real_slab_scan(a_scr, b_scr, b_scr, lam_carry, reverse=True)
        lmb = b_scr[...]
        a_carry[...] = a[:SUBLANES]
        hs_prev = _shift_down(hs, jnp.where(time_first, 0.0, hsh_ref[...]), 1)
        d_a = lmb * hs_prev
        d_m = lmb * (ig * xr)
        d_ig = lmb * mult * xr
        d_xr = lmb * mult * ig
        d_la = a * d_a - (a * a / mult) * d_m
        dsp_acc[...] += jnp.sum(-RG_C * r * d_la, axis=0, keepdims=True)
        d_za = (-RG_C * sp) * d_la * r * (1.0 - r)
        d_zx = d_ig * ig * (1.0 - ig)
        dba_ref[...] += jnp.sum(d_za, axis=0, keepdims=True)
        dbx_ref[...] += jnp.sum(d_zx, axis=0, keepdims=True)
        dzab, dzxb = d_za.astype(BF16), d_zx.astype(BF16)
        back = []
        for h in range(RG_HEADS):
            sl = slice(h * bw, (h + 1) * bw)
            dwa_ref[h] += lax.dot_general(xb[:, sl], dzab[:, sl], tn_dims, preferred_element_type=F32)
            dwx_ref[h] += lax.dot_general(xb[:, sl], dzxb[:, sl], tn_dims, preferred_element_type=F32)
            back.append(lax.dot_general(dzab[:, sl], wa_ref[h], nt_dims, preferred_element_type=F32)
                        + lax.dot_general(dzxb[:, sl], wx_ref[h], nt_dims, preferred_element_type=F32))
        d_xr = d_xr + jnp.concatenate(back, axis=1)
        d_xp = cw_ref[kw - 1:kw, :] * d_xr
        after = dxr_carry[...]
        for sft in range(1, kw):
            d_xp = d_xp + cw_ref[kw - 1 - sft:kw - sft, :] * _shift_up(d_xr, after, sft)
        dxr_carry[...] = d_xr[:SUBLANES]
        dxg_ref[0] = d_xp.astype(BF16)
        xp = xg_ref[0]
        before = jnp.where(time_first, 0.0, xph_ref[...])
        dcb_ref[...] += jnp.sum(d_xr, axis=0, keepdims=True)
        dcw_ref[kw - 1:kw, :] += jnp.sum(d_xr * xp, axis=0, keepdims=True)
        for sft in range(1, kw):
            dcw_ref[kw - 1 - sft:kw - sft, :] += jnp.sum(d_xr * _shift_down(xp, before, sft), axis=0, keepdims=True)
        dlam_ref[...] = dsp_acc[...] * (-_sigmoid(-lam_ref[...]))

    full = lambda shape: pl.BlockSpec(shape, lambda i: (0,) * len(shape))
    rev = lambda i: nt - 1 - i
    row_spec = pl.BlockSpec((ts, c), lambda i: (rev(i), 0))
    halo_idx = lambda i: jnp.maximum(rev(i) * hb - 1, 0)
    vec = (1, c)
    return _call_with_side(
        body, side, lambda: pl.program_id(0) == 0, lambda: pl.program_id(0) == nt - 1,
        name="rg_bwd", grid=(nt,),
        in_specs=[row_spec,
                  pl.BlockSpec((2, ts, c), lambda i: (0, rev(i), 0)),
                  pl.BlockSpec((None, SUBLANES, c), lambda i: (0, halo_idx(i), 0)),
                  row_spec, row_spec,
                  pl.BlockSpec((SUBLANES, c), lambda i: (halo_idx(i), 0)),
                  full(conv_w.shape), full(w_a.shape), full(b_a.shape), full(w_x.shape), full(b_x.shape), full(lam.shape)],
        out_specs=[pl.BlockSpec((2, ts, c), lambda i: (0, rev(i), 0)), full(conv_w.shape), full(vec), full(w_a.shape), full(vec),
                   full(w_x.shape), full(vec), full(vec)],
        out_shape=[jax.ShapeDtypeStruct((2, s, c), BF16), jax.ShapeDtypeStruct(conv_w.shape, F32), jax.ShapeDtypeStruct(vec, F32),
                   jax.ShapeDtypeStruct(w_a.shape, F32), jax.ShapeDtypeStruct(vec, F32), jax.ShapeDtypeStruct(w_x.shape, F32),
                   jax.ShapeDtypeStruct(vec, F32), jax.ShapeDtypeStruct(vec, F32)],
        scratch_shapes=[pltpu.VMEM(vec, F32), pltpu.VMEM((SUBLANES, c), F32), pltpu.VMEM((SUBLANES, c), F32),
                        pltpu.VMEM(vec, F32), pltpu.VMEM((ts, c), F32), pltpu.VMEM((ts, c), F32)],
        args=(dy, xg2, xg2, xr, hs, hs, conv_w, w_a, b_a, w_x, b_x, lam))


def _s5_param_fn(a_re, a_im, log_dt, bt_re, bt_im):
    dt = jnp.exp(log_dt)
    mag = jnp.exp(a_re * dt)
    abr = mag * jnp.cos(a_im * dt)
    abi = mag * jnp.sin(a_im * dt)
    ur, ui = abr - 1.0, abi
    den = a_re * a_re + a_im * a_im
    wr = (ur * a_re + ui * a_im) / den
    wi = (ui * a_re - ur * a_im) / den
    bbr = wr[None] * bt_re - wi[None] * bt_im
    bbi = wr[None] * bt_im + wi[None] * bt_re
    return abr, abi, bbr, bbi


def _s5_params(a_re, a_im, log_dt, bt_re, bt_im, nlev):
    g, p = a_re.shape
    gc = bt_re.shape[0]

    def body(ar_ref, ai_ref, dt_ref, br_ref, bi_ref, abr_ref, abi_ref, pr_ref, pi_ref, bbr_ref, bbi_ref):
        abr, abi, bbr, bbi = _s5_param_fn(ar_ref[...], ai_ref[...], dt_ref[...], br_ref[...], bi_ref[...])
        abr_ref[...] = abr
        abi_ref[...] = abi
        bbr_ref[...] = bbr
        bbi_ref[...] = bbi
        qr, qi = abr, abi
        for k in range(nlev):
            pr_ref[k] = qr
            pi_ref[k] = qi
            qr, qi = qr * qr - qi * qi, 2.0 * qr * qi

    sd = jax.ShapeDtypeStruct
    return pl.pallas_call(
        body, name="s5_params",
        out_shape=[sd((g, p), F32), sd((g, p), F32), sd((nlev, g, p), F32), sd((nlev, g, p), F32), sd((gc, g, p), F32),
                   sd((gc, g, p), F32)],
    )(a_re, a_im, log_dt, bt_re, bt_im)


def _s5_params_bwd(a_re, a_im, log_dt, bt_re, bt_im, d_abr, d_abi, d_bbr, d_bbi):
    def body(ar_ref, ai_ref, dt_ref, br_ref, bi_ref, g0, g1, g2, g3, o0, o1, o2, o3, o4):
        _, vjp = jax.vjp(_s5_param_fn, ar_ref[...], ai_ref[...], dt_ref[...], br_ref[...], bi_ref[...])
        outs = vjp((g0[...], g1[...], g2[...], g3[...]))
        for o, v in zip((o0, o1, o2, o3, o4), outs):
            o[...] = v

    sd = jax.ShapeDtypeStruct
    return pl.pallas_call(
        body, name="s5_params_bwd",
        out_shape=[sd(a_re.shape, F32), sd(a_im.shape, F32), sd(log_dt.shape, F32), sd(bt_re.shape, F32), sd(bt_im.shape, F32)],
    )(a_re, a_im, log_dt, bt_re, bt_im, d_abr, d_abi, d_bbr, d_bbi)


def _s5_fwd(u, abr, abi, pw_r, pw_i, bp_r, bp_i, cp_r, cp_i, dvec, ts=128, side=None):
    s, c = u.shape
    n = abr.shape[1]
    nblk, cb, nb = bp_r.shape
    ts = min(ts, s)

    def body(u_ref, ar_ref, ai_ref, pr_ref, pi_ref, bpr_ref, bpi_ref, cpr_ref, cpi_ref, d_ref,
             hr_ref, hi_ref, yp_ref, gy_ref, car_r, car_i):
        i = pl.program_id(0)

        @pl.when(i == 0)
        def _():
            car_r[...] = jnp.zeros_like(car_r)
            car_i[...] = jnp.zeros_like(car_i)

        uv = u_ref[...]
        ub = uv.astype(BF16)
        br = jnp.concatenate([jnp.dot(ub[:, k * cb:(k + 1) * cb], bpr_ref[k], preferred_element_type=F32) for k in range(nblk)], axis=1)
        bi = jnp.concatenate([jnp.dot(ub[:, k * cb:(k + 1) * cb], bpi_ref[k], preferred_element_type=F32) for k in range(nblk)], axis=1)
        ar, ai = ar_ref[...], ai_ref[...]
        pr, pi_ = car_r[SUBLANES - 1:SUBLANES, :], car_i[SUBLANES - 1:SUBLANES, :]
        row = _rows(br.shape)
        br = br + jnp.where(row == 0, ar * pr - ai * pi_, 0.0)
        bi = bi + jnp.where(row == 0, ar * pi_ + ai * pr, 0.0)
        hr, hi = _scan_cplx(br, bi, pr_ref, pi_ref, reverse=False)
        car_r[...] = hr[ts - SUBLANES:]
        car_i[...] = hi[ts - SUBLANES:]
        hr_ref[...] = hr
        hi_ref[...] = hi
        hrb, hib = hr.astype(BF16), hi.astype(BF16)
        y = jnp.concatenate([jnp.dot(hrb[:, k * nb:(k + 1) * nb], cpr_ref[k], preferred_element_type=F32)
                             - jnp.dot(hib[:, k * nb:(k + 1) * nb], cpi_ref[k], preferred_element_type=F32) for k in range(nblk)], axis=1)
        yp = y + d_ref[...] * uv
        yp_ref[...] = yp
        gy_ref[...] = _gelu(yp).astype(BF16)

    full = lambda shape: pl.BlockSpec(shape, lambda i: (0,) * len(shape))
    rc = pl.BlockSpec((ts, c), lambda i: (i, 0))
    rn = pl.BlockSpec((ts, n), lambda i: (i, 0))
    sd = jax.ShapeDtypeStruct
    nt = s // ts
    return _call_with_side(
        body, side, lambda: pl.program_id(0) == 0, lambda: pl.program_id(0) == nt - 1,
        name="s5_fwd", grid=(nt,),
        in_specs=[rc, full(abr.shape), full(abi.shape), full(pw_r.shape), full(pw_i.shape), full(bp_r.shape), full(bp_i.shape),
                  full(cp_r.shape), full(cp_i.shape), full(dvec.shape)],
        out_specs=[rn, rn, rc, rc],
        out_shape=[sd((s, n), F32), sd((s, n), F32), sd((s, c), F32), sd((s, c), BF16)],
        scratch_shapes=[pltpu.VMEM((SUBLANES, n), F32), pltpu.VMEM((SUBLANES, n), F32)],
        args=(u, abr, abi, pw_r, pw_i, bp_r, bp_i, cp_r, cp_i, dvec))


def _s5_bwd(dgy, ypre, u, hr, hi, abr, abi, pw_r, pw_i, bp_r, bp_i, cp_r, cp_i, dvec, ts=128, side=None):
    s, c = u.shape
    n = abr.shape[1]
    nblk, cb, nb = bp_r.shape
    ts = min(ts, s)
    nt = s // ts
    hb = ts // SUBLANES
    tn_dims = (((0,), (0,)), ((), ()))
    nt_dims = (((1,), (1,)), ((), ()))

    def body(dgy_ref, yp_ref, u_ref, hr_ref, hi_ref, hrh_ref, hih_ref, ar_ref, ai_ref, pr_ref, pi_ref, bpr_ref, bpi_ref,
             cpr_ref, cpi_ref, d_ref,
             du_ref, dar_ref, dai_ref, dbr_ref, dbi_ref, dcr_ref, dci_ref, dd_ref, car_r, car_i, npi_ref):
        i = pl.program_id(0)
        time_first = i == nt - 1

        @pl.when(i == 0)
        def _():
            car_r[...] = jnp.zeros_like(car_r)
            car_i[...] = jnp.zeros_like(car_i)
            npi_ref[...] = -pi_ref[...]
            for ref in (dar_ref, dai_ref, dbr_ref, dbi_ref, dcr_ref, dci_ref, dd_ref):
                ref[...] = jnp.zeros_like(ref)

        uv = u_ref[...]
        _, dgel = _gelu_and_grad(yp_ref[...])
        dyv = dgy_ref[...] * dgel
        dd_ref[...] += jnp.sum(dyv * uv, axis=0, keepdims=True)
        dyb = dyv.astype(BF16)
        hr, hi = hr_ref[...], hi_ref[...]
        hrb, hib = hr.astype(BF16), hi.astype(BF16)
        dhr, dhi = [], []
        for k in range(nblk):
            dblk = dyb[:, k * cb:(k + 1) * cb]
            dhr.append(lax.dot_general(dblk, cpr_ref[k], nt_dims, preferred_element_type=F32))
            dhi.append(-lax.dot_general(dblk, cpi_ref[k], nt_dims, preferred_element_type=F32))
            dcr_ref[k] += lax.dot_general(hrb[:, k * nb:(k + 1) * nb], dblk, tn_dims, preferred_element_type=F32)
            dci_ref[k] += lax.dot_general(hib[:, k * nb:(k + 1) * nb], dblk, tn_dims, preferred_element_type=F32)
        dhr = jnp.concatenate(dhr, axis=1)
        dhi = jnp.concatenate(dhi, axis=1)
        ar, ai = ar_ref[...], ai_ref[...]
        nr, ni = car_r[0:1, :], car_i[0:1, :]
        row = _rows(dhr.shape)
        dhr = dhr + jnp.where(row == ts - 1, ar * nr + ai * ni, 0.0)
        dhi = dhi + jnp.where(row == ts - 1, ar * ni - ai * nr, 0.0)
        lr, li = _scan_cplx(dhr, dhi, pr_ref, npi_ref, reverse=True)
        car_r[...] = lr[:SUBLANES]
        car_i[...] = li[:SUBLANES]
        hpr = _shift_down(hr, jnp.where(time_first, 0.0, hrh_ref[...]), 1)
        hpi = _shift_down(hi, jnp.where(time_first, 0.0, hih_ref[...]), 1)
        dar_ref[...] += jnp.sum(lr * hpr + li * hpi, axis=0, keepdims=True)
        dai_ref[...] += jnp.sum(li * hpr - lr * hpi, axis=0, keepdims=True)
        lrb, lib = lr.astype(BF16), li.astype(BF16)
        ub = uv.astype(BF16)
        du = []
        for k in range(nblk):
            ublk = ub[:, k * cb:(k + 1) * cb]
            lrk, lik = lrb[:, k * nb:(k + 1) * nb], lib[:, k * nb:(k + 1) * nb]
            dbr_ref[k] += lax.dot_general(ublk, lrk, tn_dims, preferred_element_type=F32)
            dbi_ref[k] += lax.dot_general(ublk, lik, tn_dims, preferred_element_type=F32)
            du.append(lax.dot_general(lrk, bpr_ref[k], nt_dims, preferred_element_type=F32)
                      + lax.dot_general(lik, bpi_ref[k], nt_dims, preferred_element_type=F32))
        du_ref[...] = (d_ref[...] * dyv + jnp.concatenate(du, axis=1)).astype(BF16)

    full = lambda shape: pl.BlockSpec(shape, lambda i: (0,) * len(shape))
    rev = lambda i: nt - 1 - i
    halo_idx = lambda i: jnp.maximum(rev(i) * hb - 1, 0)
    rc = pl.BlockSpec((ts, c), lambda i: (rev(i), 0))
    rn = pl.BlockSpec((ts, n), lambda i: (rev(i), 0))
    hn = pl.BlockSpec((SUBLANES, n), lambda i: (halo_idx(i), 0))
    sd = jax.ShapeDtypeStruct
    return _call_with_side(
        body, side, lambda: pl.program_id(0) == 0, lambda: pl.program_id(0) == nt - 1,
        name="s5_bwd", grid=(nt,),
        in_specs=[rc, rc, rc, rn, rn, hn, hn, full(abr.shape), full(abi.shape), full(pw_r.shape), full(pw_i.shape),
                  full(bp_r.shape), full(bp_i.shape), full(cp_r.shape), full(cp_i.shape), full(dvec.shape)],
        out_specs=[rc, full(abr.shape), full(abi.shape), full(bp_r.shape), full(bp_i.shape), full(cp_r.shape), full(cp_i.shape),
                   full(dvec.shape)],
        out_shape=[sd((s, c), BF16), sd(abr.shape, F32), sd(abi.shape, F32), sd(bp_r.shape, F32), sd(bp_i.shape, F32),
                   sd(cp_r.shape, F32), sd(cp_i.shape, F32), sd(dvec.shape, F32)],
        scratch_shapes=[pltpu.VMEM((SUBLANES, n), F32), pltpu.VMEM((SUBLANES, n), F32), pltpu.VMEM(pw_i.shape, F32)],
        args=(dgy, ypre, u, hr, hi, hr, hi, abr, abi, pw_r, pw_i, bp_r, bp_i, cp_r, cp_i, dvec))


S5_LANE_CHUNK = 512


def _s5_tables(a_re, a_im, log_dt, bt_re, bt_im):
    g, p = a_re.shape
    gc = bt_re.shape[0]

    def body(ar_ref, ai_ref, dt_ref, br_ref, bi_ref, abr_ref, abi_ref, tr_ref, ti_ref, bbr_ref, bbi_ref):
        abr, abi, bbr, bbi = _s5_param_fn(ar_ref[...], ai_ref[...], dt_ref[...], br_ref[...], bi_ref[...])
        abr_ref[...] = abr
        abi_ref[...] = abi
        bbr_ref[...] = bbr
        bbi_ref[...] = bbi
        pows = [(abr, abi)]
        for _ in range(1, SUBLANES):
            qr, qi = pows[-1]
            pows.append((qr * abr - qi * abi, qr * abi + qi * abr))
        zero = jnp.zeros_like(abr)
        for r in range(SUBLANES):
            for k in range(3):
                sh = 1 << k
                tr_ref[k, r] = pows[sh - 1][0] if r >= sh else zero
                ti_ref[k, r] = pows[sh - 1][1] if r >= sh else zero
            tr_ref[3, r] = pows[r][0]
            ti_ref[3, r] = pows[r][1]

    sd = jax.ShapeDtypeStruct
    return pl.pallas_call(
        body, name="s5_tables",
        out_shape=[sd((g, p), F32), sd((g, p), F32), sd((4, SUBLANES, g, p), F32), sd((4, SUBLANES, g, p), F32),
                   sd((gc, g, p), F32), sd((gc, g, p), F32)],
    )(a_re, a_im, log_dt, bt_re, bt_im)


def _cmul_add(br, bi, tr, ti, sr, si):
    return br + tr * sr - ti * si, bi + tr * si + ti * sr


def _s5_fwd2(u, tab_r, tab_i, bp_r, bp_i, cp_r, cp_i, dvec, ts=256, side=None):
    s, c = u.shape
    n = tab_r.shape[2]
    nblk, cb, nb = bp_r.shape
    ts = min(ts, s)
    nsl = ts // SUBLANES
    lc = min(S5_LANE_CHUNK, n)

    def body(u_ref, tr_ref, ti_ref, bpr_ref, bpi_ref, cpr_ref, cpi_ref, d_ref, hr_ref, hi_ref, yp_ref, gy_ref,
             bur_ref, bui_ref, car_r, car_i):
        i = pl.program_id(0)

        @pl.when(i == 0)
        def _():
            car_r[...] = jnp.zeros_like(car_r)
            car_i[...] = jnp.zeros_like(car_i)

        uv = u_ref[...]
        ub = uv.astype(BF16)
        for k in range(nblk):
            bur_ref[:, k * nb:(k + 1) * nb] = jnp.dot(ub[:, k * cb:(k + 1) * cb], bpr_ref[k], preferred_element_type=F32)
            bui_ref[:, k * nb:(k + 1) * nb] = jnp.dot(ub[:, k * cb:(k + 1) * cb], bpi_ref[k], preferred_element_type=F32)
        for q in range(n // lc):
            sl = slice(q * lc, (q + 1) * lc)
            tabs = [(tr_ref[k, :, sl], ti_ref[k, :, sl]) for k in range(4)]

            def slab(j, carry, sl=sl, tabs=tabs):
                cr, ci = carry
                r0 = pl.multiple_of(j * SUBLANES, SUBLANES)
                br, bi = bur_ref[pl.ds(r0, SUBLANES), sl], bui_ref[pl.ds(r0, SUBLANES), sl]
                for k in range(3):
                    sh = 1 << k
                    br, bi = _cmul_add(br, bi, tabs[k][0], tabs[k][1], pltpu.roll(br, sh, 0), pltpu.roll(bi, sh, 0))
                hr, hi = _cmul_add(br, bi, tabs[3][0], tabs[3][1], jnp.broadcast_to(cr, br.shape), jnp.broadcast_to(ci, bi.shape))
                hr_ref[pl.ds(r0, SUBLANES), sl] = hr
                hi_ref[pl.ds(r0, SUBLANES), sl] = hi
                return hr[SUBLANES - 1:, :], hi[SUBLANES - 1:, :]

            cr, ci = lax.fori_loop(0, nsl, slab, (car_r[:, sl], car_i[:, sl]), unroll=2)
            car_r[:, sl] = cr
            car_i[:, sl] = ci
        hrb, hib = hr_ref[...].astype(BF16), hi_ref[...].astype(BF16)
        y = jnp.concatenate([jnp.dot(hrb[:, k * nb:(k + 1) * nb], cpr_ref[k], preferred_element_type=F32)
                             - jnp.dot(hib[:, k * nb:(k + 1) * nb], cpi_ref[k], preferred_element_type=F32) for k in range(nblk)], axis=1)
        yp = y + d_ref[...] * uv
        yp_ref[...] = yp
        gy_ref[...] = _gelu(yp).astype(BF16)

    full = lambda shape: pl.BlockSpec(shape, lambda i: (0,) * len(shape))
    rc = pl.BlockSpec((ts, c), lambda i: (i, 0))
    rn = pl.BlockSpec((ts, n), lambda i: (i, 0))
    sd = jax.ShapeDtypeStruct
    nt = s // ts
    return _call_with_side(
        body, side, lambda: pl.program_id(0) == 0, lambda: pl.program_id(0) == nt - 1,
        name="s5_fwd", grid=(nt,),
        in_specs=[rc, full(tab_r.shape), full(tab_i.shape), full(bp_r.shape), full(bp_i.shape), full(cp_r.shape), full(cp_i.shape),
                  full(dvec.shape)],
        out_specs=[rn, rn, rc, rc],
        out_shape=[sd((s, n), F32), sd((s, n), F32), sd((s, c), F32), sd((s, c), BF16)],
        scratch_shapes=[pltpu.VMEM((ts, n), F32), pltpu.VMEM((ts, n), F32), pltpu.VMEM((1, n), F32), pltpu.VMEM((1, n), F32)],
        args=(u, tab_r, tab_i, bp_r, bp_i, cp_r, cp_i, dvec))


def _s5_bwd2(dgy, ypre, u, hr, hi, rtab_r, rtab_i, bp_r, bp_i, cp_r, cp_i, dvec, ts=256, side=None):
    s, c = u.shape
    n = rtab_r.shape[2]
    nblk, cb, nb = bp_r.shape
    ts = min(ts, s)
    nt = s // ts
    hb = ts // SUBLANES
    nsl = ts // SUBLANES
    lc = min(S5_LANE_CHUNK, n)
    tn_dims = (((0,), (0,)), ((), ()))
    nt_dims = (((1,), (1,)), ((), ()))

    def body(dgy_ref, yp_ref, u_ref, hr_ref, hi_ref, hrh_ref, hih_ref, tr_ref, ti_ref, bpr_ref, bpi_ref, cpr_ref, cpi_ref, d_ref,
             du_ref, dar_ref, dai_ref, dbr_ref, dbi_ref, dcr_ref, dci_ref, dd_ref, lr_ref, li_ref, car_r, car_i):
        i = pl.program_id(0)
        time_first = i == nt - 1

        @pl.when(i == 0)
        def _():
            car_r[...] = jnp.zeros_like(car_r)
            car_i[...] = jnp.zeros_like(car_i)
            for ref in (dar_ref, dai_ref, dbr_ref, dbi_ref, dcr_ref, dci_ref, dd_ref):
                ref[...] = jnp.zeros_like(ref)

        uv = u_ref[...]
        _, dgel = _gelu_and_grad(yp_ref[...])
        dyv = dgy_ref[...] * dgel
        dd_ref[...] += jnp.sum(dyv * uv, axis=0, keepdims=True)
        dyb = dyv.astype(BF16)
        hrb, hib = hr_ref[...].astype(BF16), hi_ref[...].astype(BF16)
        for k in range(nblk):
            dblk = dyb[:, k * cb:(k + 1) * cb]
            lr_ref[:, k * nb:(k + 1) * nb] = lax.dot_general(dblk, cpr_ref[k], nt_dims, preferred_element_type=F32)
            li_ref[:, k * nb:(k + 1) * nb] = -lax.dot_general(dblk, cpi_ref[k], nt_dims, preferred_element_type=F32)
            dcr_ref[k] += lax.dot_general(hrb[:, k * nb:(k + 1) * nb], dblk, tn_dims, preferred_element_type=F32)
            dci_ref[k] += lax.dot_general(hib[:, k * nb:(k + 1) * nb], dblk, tn_dims, preferred_element_type=F32)
        row8 = _rows((SUBLANES, lc))
        for q in range(n // lc):
            sl = slice(q * lc, (q + 1) * lc)
            tabs = [(tr_ref[k, :, sl], ti_ref[k, :, sl]) for k in range(4)]
            halo_r = jnp.where(time_first, 0.0, hrh_ref[SUBLANES - 1:, sl])
            halo_i = jnp.where(time_first, 0.0, hih_ref[SUBLANES - 1:, sl])

            def slab(jj, carry, sl=sl, tabs=tabs, halo_r=halo_r, halo_i=halo_i):
                nr, ni, acc_r, acc_i = carry
                j = nsl - 1 - jj
                r0 = pl.multiple_of(j * SUBLANES, SUBLANES)
                br, bi = lr_ref[pl.ds(r0, SUBLANES), sl], li_ref[pl.ds(r0, SUBLANES), sl]
                for k in range(3):
                    sh = 1 << k
                    br, bi = _cmul_add(br, bi, tabs[k][0], tabs[k][1], pltpu.roll(br, SUBLANES - sh, 0),
                                       pltpu.roll(bi, SUBLANES - sh, 0))
                lr, li = _cmul_add(br, bi, tabs[3][0], tabs[3][1], jnp.broadcast_to(nr, br.shape), jnp.broadcast_to(ni, bi.shape))
                lr_ref[pl.ds(r0, SUBLANES), sl] = lr
                li_ref[pl.ds(r0, SUBLANES), sl] = li
                p0 = pl.multiple_of(jnp.maximum(j - 1, 0) * SUBLANES, SUBLANES)
                prev_r = jnp.where(j == 0, halo_r, hr_ref[pl.ds(p0, SUBLANES), sl][SUBLANES - 1:, :])
                prev_i = jnp.where(j == 0, halo_i, hi_ref[pl.ds(p0, SUBLANES), sl][SUBLANES - 1:, :])
                hpr = jnp.where(row8 == 0, jnp.broadcast_to(prev_r, br.shape), pltpu.roll(hr_ref[pl.ds(r0, SUBLANES), sl], 1, 0))
                hpi = jnp.where(row8 == 0, jnp.broadcast_to(prev_i, bi.shape), pltpu.roll(hi_ref[pl.ds(r0, SUBLANES), sl], 1, 0))
                return lr[:1, :], li[:1, :], acc_r + (lr * hpr + li * hpi), acc_i + (li * hpr - lr * hpi)

            zero = jnp.zeros((SUBLANES, lc), F32)
            nr, ni, acc_r, acc_i = lax.fori_loop(0, nsl, slab, (car_r[:, sl], car_i[:, sl], zero, zero), unroll=2)
            car_r[:, sl] = nr
            car_i[:, sl] = ni
            dar_ref[:, sl] += jnp.sum(acc_r, axis=0, keepdims=True)
            dai_ref[:, sl] += jnp.sum(acc_i, axis=0, keepdims=True)
        lrb, lib = lr_ref[...].astype(BF16), li_ref[...].astype(BF16)
        ub = uv.astype(BF16)
        du = []
        for k in range(nblk):
            ublk = ub[:, k * cb:(k + 1) * cb]
            lrk, lik = lrb[:, k * nb:(k + 1) * nb], lib[:, k * nb:(k + 1) * nb]
            dbr_ref[k] += lax.dot_general(ublk, lrk, tn_dims, preferred_element_type=F32)
            dbi_ref[k] += lax.dot_general(ublk, lik, tn_dims, preferred_element_type=F32)
            du.append(lax.dot_general(lrk, bpr_ref[k], nt_dims, preferred_element_type=F32)
                      + lax.dot_general(lik, bpi_ref[k], nt_dims, preferred_element_type=F32))
        du_ref[...] = (d_ref[...] * dyv + jnp.concatenate(du, axis=1)).astype(BF16)

    full = lambda shape: pl.BlockSpec(shape, lambda i: (0,) * len(shape))
    rev = lambda i: nt - 1 - i
    halo_idx = lambda i: jnp.maximum(rev(i) * hb - 1, 0)
    rc = pl.BlockSpec((ts, c), lambda i: (rev(i), 0))
    rn = pl.BlockSpec((ts, n), lambda i: (rev(i), 0))
    hn = pl.BlockSpec((SUBLANES, n), lambda i: (halo_idx(i), 0))
    sd = jax.ShapeDtypeStruct
    vec_n = (1, n)
    return _call_with_side(
        body, side, lambda: pl.program_id(0) == 0, lambda: pl.program_id(0) == nt - 1,
        name="s5_bwd", grid=(nt,),
        in_specs=[rc, rc, rc, rn, rn, hn, hn, full(rtab_r.shape), full(rtab_i.shape),
                  full(bp_r.shape), full(bp_i.shape), full(cp_r.shape), full(cp_i.shape), full(dvec.shape)],
        out_specs=[rc, full(vec_n), full(vec_n), full(bp_r.shape), full(bp_i.shape), full(cp_r.shape), full(cp_i.shape),
                   full(dvec.shape)],
        out_shape=[sd((s, c), BF16), sd(vec_n, F32), sd(vec_n, F32), sd(bp_r.shape, F32), sd(bp_i.shape, F32),
                   sd(cp_r.shape, F32), sd(cp_i.shape, F32), sd(dvec.shape, F32)],
        scratch_shapes=[pltpu.VMEM((ts, n), F32), pltpu.VMEM((ts, n), F32), pltpu.VMEM((1, n), F32), pltpu.VMEM((1, n), F32)],
        args=(dgy, ypre, u, hr, hi, hr, hi, rtab_r, rtab_i, bp_r, bp_i, cp_r, cp_i, dvec))


def _s5_tables3(a_re, a_im, log_dt, bt_re, bt_im, seg):
    g, p = a_re.shape
    gc = bt_re.shape[0]
    nsq = int(math.log2(seg))
    assert 1 << nsq == seg

    def body(ar_ref, ai_ref, dt_ref, br_ref, bi_ref, tr_ref, ti_ref, bbr_ref, bbi_ref):
        abr, abi, bbr, bbi = _s5_param_fn(ar_ref[...], ai_ref[...], dt_ref[...], br_ref[...], bi_ref[...])
        bbr_ref[...] = bbr
        bbi_ref[...] = bbi
        qr, qi = abr, abi
        for _ in range(nsq):
            qr, qi = qr * qr - qi * qi, 2.0 * qr * qi
        pows = [(qr, qi)]
        for _ in range(1, SUBLANES):
            cr, ci = pows[-1]
            pows.append((cr * qr - ci * qi, cr * qi + ci * qr))
        zero = jnp.zeros_like(abr)
        for r in range(SUBLANES):
            for k in range(3):
                sh = 1 << k
                tr_ref[k, r] = pows[sh - 1][0] if r >= sh else zero
                ti_ref[k, r] = pows[sh - 1][1] if r >= sh else zero
            tr_ref[3, r] = pows[r][0]
            ti_ref[3, r] = pows[r][1]
            tr_ref[4, r] = abr
            ti_ref[4, r] = abi

    sd = jax.ShapeDtypeStruct
    return pl.pallas_call(
        body, name="s5_tables",
        out_shape=[sd((5, SUBLANES, g, p), F32), sd((5, SUBLANES, g, p), F32), sd((gc, g, p), F32), sd((gc, g, p), F32)],
    )(a_re, a_im, log_dt, bt_re, bt_im)


def _segment_perm(ts):
    seg = ts // SUBLANES
    rho = jnp.arange(ts)
    src = (rho % SUBLANES) * seg + rho // SUBLANES
    return (src[:, None] == jnp.arange(ts)[None, :]).astype(BF16)


def _exact_rows(perm_t, x):
    hi = x.astype(BF16)
    r1 = x - hi.astype(F32)
    mid = r1.astype(BF16)
    lo = (r1 - mid.astype(F32)).astype(BF16)
    dot = lambda v: jnp.dot(perm_t, v, preferred_element_type=F32)
    return (dot(hi) + dot(mid)) + dot(lo)


def _s5_fwd3(u, perm, perm_t, tab_r, tab_i, bp_r, bp_i, cp_r, cp_i, dvec, ts=256, side=None):
    s, c = u.shape
    n = tab_r.shape[2]
    nblk, cb, nb = bp_r.shape
    ts = min(ts, s)
    seg = ts // SUBLANES
    lc = min(S5_LANE_CHUNK, n)

    def body(u_ref, p_ref, pt_ref, tr_ref, ti_ref, bpr_ref, bpi_ref, cpr_ref, cpi_ref, d_ref, hr_ref, hi_ref, yp_ref, gy_ref,
             bur_ref, bui_ref, car_r, car_i):
        i = pl.program_id(0)

        @pl.when(i == 0)
        def _():
            car_r[...] = jnp.zeros_like(car_r)
            car_i[...] = jnp.zeros_like(car_i)

        uv = u_ref[...]
        ubp = jnp.dot(p_ref[...], uv.astype(BF16), preferred_element_type=F32).astype(BF16)
        for k in range(nblk):
            bur_ref[:, k * nb:(k + 1) * nb] = jnp.dot(ubp[:, k * cb:(k + 1) * cb], bpr_ref[k], preferred_element_type=F32)
            bui_ref[:, k * nb:(k + 1) * nb] = jnp.dot(ubp[:, k * cb:(k + 1) * cb], bpi_ref[k], preferred_element_type=F32)
        row8 = _rows((SUBLANES, lc))
        for q in range(n // lc):
            sl = slice(q * lc, (q + 1) * lc)
            tabs = [(tr_ref[k, :, sl], ti_ref[k, :, sl]) for k in range(5)]
            a_r, a_i = tabs[4]

            def local(r, carry, sl=sl, a_r=a_r, a_i=a_i):
                r0 = pl.multiple_of(r * SUBLANES, SUBLANES)
                hr, hi = _cmul_add(bur_ref[pl.ds(r0, SUBLANES), sl], bui_ref[pl.ds(r0, SUBLANES), sl], a_r, a_i, carry[0], carry[1])
                hr_ref[pl.ds(r0, SUBLANES), sl] = hr
                hi_ref[pl.ds(r0, SUBLANES), sl] = hi
                return hr, hi

            zero = jnp.zeros((SUBLANES, lc), F32)
            er, ei = lax.fori_loop(0, seg, local, (zero, zero), unroll=4)
            for k in range(3):
                sh = 1 << k
                er, ei = _cmul_add(er, ei, tabs[k][0], tabs[k][1], pltpu.roll(er, sh, 0), pltpu.roll(ei, sh, 0))
            cin_r, cin_i = jnp.broadcast_to(car_r[:, sl], er.shape), jnp.broadcast_to(car_i[:, sl], ei.shape)
            er, ei = _cmul_add(er, ei, tabs[3][0], tabs[3][1], cin_r, cin_i)
            car_r[:, sl] = er[SUBLANES - 1:, :]
            car_i[:, sl] = ei[SUBLANES - 1:, :]
            c_r = jnp.where(row8 == 0, cin_r, pltpu.roll(er, 1, 0))
            c_i = jnp.where(row8 == 0, cin_i, pltpu.roll(ei, 1, 0))

            def fix(r, carry, sl=sl, a_r=a_r, a_i=a_i, c_r=c_r, c_i=c_i):
                pr, pi = carry
                r0 = pl.multiple_of(r * SUBLANES, SUBLANES)
                hr, hi = _cmul_add(hr_ref[pl.ds(r0, SUBLANES), sl], hi_ref[pl.ds(r0, SUBLANES), sl], pr, pi, c_r, c_i)
                hr_ref[pl.ds(r0, SUBLANES), sl] = hr
                hi_ref[pl.ds(r0, SUBLANES), sl] = hi
                return pr * a_r - pi * a_i, pr * a_i + pi * a_r

            lax.fori_loop(0, seg, fix, (a_r, a_i), unroll=4)
        hrb, hib = hr_ref[...].astype(BF16), hi_ref[...].astype(BF16)
        y = jnp.concatenate([jnp.dot(hrb[:, k * nb:(k + 1) * nb], cpr_ref[k], preferred_element_type=F32)
                             - jnp.dot(hib[:, k * nb:(k + 1) * nb], cpi_ref[k], preferred_element_type=F32) for k in range(nblk)], axis=1)
        yp = _exact_rows(pt_ref[...], y) + d_ref[...] * uv
        yp_ref[...] = yp
        gy_ref[...] = _gelu(yp).astype(BF16)

    full = lambda shape: pl.BlockSpec(shape, lambda i: (0,) * len(shape))
    rc = pl.BlockSpec((ts, c), lambda i: (i, 0))
    rn = pl.BlockSpec((ts, n), lambda i: (i, 0))
    sd = jax.ShapeDtypeStruct
    nt = s // ts
    return _call_with_side(
        body, side, lambda: pl.program_id(0) == 0, lambda: pl.program_id(0) == nt - 1,
        name="s5_fwd", grid=(nt,),
        in_specs=[rc, full(perm.shape), full(perm_t.shape), full(tab_r.shape), full(tab_i.shape), full(bp_r.shape), full(bp_i.shape),
                  full(cp_r.shape), full(cp_i.shape), full(dvec.shape)],
        out_specs=[rn, rn, rc, rc],
        out_shape=[sd((s, n), F32), sd((s, n), F32), sd((s, c), F32), sd((s, c), BF16)],
        scratch_shapes=[pltpu.VMEM((ts, n), F32), pltpu.VMEM((ts, n), F32), pltpu.VMEM((1, n), F32), pltpu.VMEM((1, n), F32)],
        args=(u, perm, perm_t, tab_r, tab_i, bp_r, bp_i, cp_r, cp_i, dvec))


def _s5_bwd3(dgy, ypre, u, hr, hi, perm, perm_t, rtab_r, rtab_i, bp_r, bp_i, cp_r, cp_i, dvec, ts=256, side=None):
    s, c = u.shape
    n = rtab_r.shape[2]
    nblk, cb, nb = bp_r.shape
    ts = min(ts, s)
    nt = s // ts
    hb = ts // SUBLANES
    seg = ts // SUBLANES
    lc = min(S5_LANE_CHUNK, n)
    tn_dims = (((0,), (0,)), ((), ()))
    nt_dims = (((1,), (1,)), ((), ()))

    def body(dgy_ref, yp_ref, u_ref, hr_ref, hi_ref, hrh_ref, hih_ref, p_ref, pt_ref, tr_ref, ti_ref, bpr_ref, bpi_ref,
             cpr_ref, cpi_ref, d_ref, du_ref, dar_ref, dai_ref, dbr_ref, dbi_ref, dcr_ref, dci_ref, dd_ref, lr_ref, li_ref,
             car_r, car_i):
        i = pl.program_id(0)
        time_first = i == nt - 1

        @pl.when(i == 0)
        def _():
            car_r[...] = jnp.zeros_like(car_r)
            car_i[...] = jnp.zeros_like(car_i)
            for ref in (dar_ref, dai_ref, dbr_ref, dbi_ref, dcr_ref, dci_ref, dd_ref):
                ref[...] = jnp.zeros_like(ref)

        uv = u_ref[...]
        _, dgel = _gelu_and_grad(yp_ref[...])
        dyv = dgy_ref[...] * dgel
        dd_ref[...] += jnp.sum(dyv * uv, axis=0, keepdims=True)
        perm_m = p_ref[...]
        dyb = jnp.dot(perm_m, dyv.astype(BF16), preferred_element_type=F32).astype(BF16)
        ub = jnp.dot(perm_m, uv.astype(BF16), preferred_element_type=F32).astype(BF16)
        hrb, hib = hr_ref[...].astype(BF16), hi_ref[...].astype(BF16)
        for k in range(nblk):
            dblk = dyb[:, k * cb:(k + 1) * cb]
            lr_ref[:, k * nb:(k + 1) * nb] = lax.dot_general(dblk, cpr_ref[k], nt_dims, preferred_element_type=F32)
            li_ref[:, k * nb:(k + 1) * nb] = -lax.dot_general(dblk, cpi_ref[k], nt_dims, preferred_element_type=F32)
            dcr_ref[k] += lax.dot_general(hrb[:, k * nb:(k + 1) * nb], dblk, tn_dims, preferred_element_type=F32)
            dci_ref[k] += lax.dot_general(hib[:, k * nb:(k + 1) * nb], dblk, tn_dims, preferred_element_type=F32)
        row8 = _rows((SUBLANES, lc))
        last0 = (seg - 1) * SUBLANES
        for q in range(n // lc):
            sl = slice(q * lc, (q + 1) * lc)
            tabs = [(tr_ref[k, :, sl], ti_ref[k, :, sl]) for k in range(5)]
            a_r, a_i = tabs[4]

            def local(rr, carry, sl=sl, a_r=a_r, a_i=a_i):
                r0 = pl.multiple_of((seg - 1 - rr) * SUBLANES, SUBLANES)
                lr, li = _cmul_add(lr_ref[pl.ds(r0, SUBLANES), sl], li_ref[pl.ds(r0, SUBLANES), sl], a_r, a_i, carry[0], carry[1])
                lr_ref[pl.ds(r0, SUBLANES), sl] = lr
                li_ref[pl.ds(r0, SUBLANES), sl] = li
                return lr, li

            zero = jnp.zeros((SUBLANES, lc), F32)
            er, ei = lax.fori_loop(0, seg, local, (zero, zero), unroll=4)
            for k in range(3):
                sh = 1 << k
                er, ei = _cmul_add(er, ei, tabs[k][0], tabs[k][1], pltpu.roll(er, SUBLANES - sh, 0), pltpu.roll(ei, SUBLANES - sh, 0))
            cin_r, cin_i = jnp.broadcast_to(car_r[:, sl], er.shape), jnp.broadcast_to(car_i[:, sl], ei.shape)
            er, ei = _cmul_add(er, ei, tabs[3][0], tabs[3][1], cin_r, cin_i)
            car_r[:, sl] = er[:1, :]
            car_i[:, sl] = ei[:1, :]
            c_r = jnp.where(row8 == SUBLANES - 1, cin_r, pltpu.roll(er, SUBLANES - 1, 0))
            c_i = jnp.where(row8 == SUBLANES - 1, cin_i, pltpu.roll(ei, SUBLANES - 1, 0))
            halo_r = jnp.where(time_first, 0.0, hrh_ref[SUBLANES - 1:, sl])
            halo_i = jnp.where(time_first, 0.0, hih_ref[SUBLANES - 1:, sl])
            hp0_r = jnp.where(row8 == 0, jnp.broadcast_to(halo_r, zero.shape), pltpu.roll(hr_ref[pl.ds(last0, SUBLANES), sl], 1, 0))
            hp0_i = jnp.where(row8 == 0, jnp.broadcast_to(halo_i, zero.shape), pltpu.roll(hi_ref[pl.ds(last0, SUBLANES), sl], 1, 0))

            def fix(rr, carry, sl=sl, a_r=a_r, a_i=a_i, c_r=c_r, c_i=c_i, hp0_r=hp0_r, hp0_i=hp0_i):
                pr, pi, acc_r, acc_i = carry
                r = seg - 1 - rr
                r0 = pl.multiple_of(r * SUBLANES, SUBLANES)
                lr, li = _cmul_add(lr_ref[pl.ds(r0, SUBLANES), sl], li_ref[pl.ds(r0, SUBLANES), sl], pr, pi, c_r, c_i)
                lr_ref[pl.ds(r0, SUBLANES), sl] = lr
                li_ref[pl.ds(r0, SUBLANES), sl] = li
                p0 = pl.multiple_of(jnp.maximum(r - 1, 0) * SUBLANES, SUBLANES)
                hpr = jnp.where(r == 0, hp0_r, hr_ref[pl.ds(p0, SUBLANES), sl])
                hpi = jnp.where(r == 0, hp0_i, hi_ref[pl.ds(p0, SUBLANES), sl])
                return (pr * a_r - pi * a_i, pr * a_i + pi * a_r, acc_r + (lr * hpr + li * hpi), acc_i + (li * hpr - lr * hpi))

            _, _, acc_r, acc_i = lax.fori_loop(0, seg, fix, (a_r, a_i, zero, zero), unroll=4)
            dar_ref[:, sl] += jnp.sum(acc_r, axis=0, keepdims=True)
            dai_ref[:, sl] += jnp.sum(acc_i, axis=0, keepdims=True)
        lrb, lib = lr_ref[...].astype(BF16), li_ref[...].astype(BF16)
        du = []
        for k in range(nblk):
            ublk = ub[:, k * cb:(k + 1) * cb]
            lrk, lik = lrb[:, k * nb:(k + 1) * nb], lib[:, k * nb:(k + 1) * nb]
            dbr_ref[k] += lax.dot_general(ublk, lrk, tn_dims, preferred_element_type=F32)
            dbi_ref[k] += lax.dot_general(ublk, lik, tn_dims, preferred_element_type=F32)
            du.append(lax.dot_general(lrk, bpr_ref[k], nt_dims, preferred_element_type=F32)
                      + lax.dot_general(lik, bpi_ref[k], nt_dims, preferred_element_type=F32))
        du_ref[...] = (d_ref[...] * dyv + _exact_rows(pt_ref[...], jnp.concatenate(du, axis=1))).astype(BF16)

    full = lambda shape: pl.BlockSpec(shape, lambda i: (0,) * len(shape))
    rev = lambda i: nt - 1 - i
    halo_idx = lambda i: jnp.maximum(rev(i) * hb - 1, 0)
    rc = pl.BlockSpec((ts, c), lambda i: (rev(i), 0))
    rn = pl.BlockSpec((ts, n), lambda i: (rev(i), 0))
    hn = pl.BlockSpec((SUBLANES, n), lambda i: (halo_idx(i), 0))
    sd = jax.ShapeDtypeStruct
    vec_n = (1, n)
    return _call_with_side(
        body, side, lambda: pl.program_id(0) == 0, lambda: pl.program_id(0) == nt - 1,
        name="s5_bwd", grid=(nt,),
        in_specs=[rc, rc, rc, rn, rn, hn, hn, full(perm.shape), full(perm_t.shape), full(rtab_r.shape), full(rtab_i.shape),
                  full(bp_r.shape), full(bp_i.shape), full(cp_r.shape), full(cp_i.shape), full(dvec.shape)],
        out_specs=[rc, full(vec_n), full(vec_n), full(bp_r.shape), full(bp_i.shape), full(cp_r.shape), full(cp_i.shape),
                   full(dvec.shape)],
        out_shape=[sd((s, c), BF16), sd(vec_n, F32), sd(vec_n, F32), sd(bp_r.shape, F32), sd(bp_i.shape, F32),
                   sd(cp_r.shape, F32), sd(cp_i.shape, F32), sd(dvec.shape, F32)],
        scratch_shapes=[pltpu.VMEM((ts, n), F32), pltpu.VMEM((ts, n), F32), pltpu.VMEM((1, n), F32), pltpu.VMEM((1, n), F32)],
        args=(dgy, ypre, u, hr, hi, hr, hi, perm, perm_t, rtab_r, rtab_i, bp_r, bp_i, cp_r, cp_i, dvec))


def _glu(gl2, ts=512):
    _, s, c = gl2.shape
    ts = min(ts, s)

    def body(g_ref, o_ref):
        o_ref[...] = (g_ref[0] * _sigmoid(g_ref[1])).astype(BF16)

    return pl.pallas_call(
        body, name="glu", grid=(s // ts,), in_specs=[pl.BlockSpec((2, ts, c), lambda i: (0, i, 0))],
        out_specs=pl.BlockSpec((ts, c), lambda i: (i, 0)), out_shape=jax.ShapeDtypeStruct((s, c), BF16), compiler_params=_cparams(),
    )(gl2)


def _glu_bwd(gl2, d_o, ts=512):
    _, s, c = gl2.shape
    ts = min(ts, s)

    def body(g_ref, do_ref, o_ref):
        sg = _sigmoid(g_ref[1])
        dov = do_ref[...]
        o_ref[0] = (dov * sg).astype(BF16)
        o_ref[1] = (dov * g_ref[0] * sg * (1.0 - sg)).astype(BF16)

    blk = pl.BlockSpec((2, ts, c), lambda i: (0, i, 0))
    return pl.pallas_call(
        body, name="glu_bwd", grid=(s // ts,), in_specs=[blk, pl.BlockSpec((ts, c), lambda i: (i, 0))],
        out_specs=blk, out_shape=jax.ShapeDtypeStruct((2, s, c), BF16), compiler_params=_cparams(),
    )(gl2, d_o)


PACK_ROW_MULTIPLE = 1024
ELEMENTWISE_BLOCK_ELEMS = 256 * 1024


def _row_tile(rows, cols):
    pref = max(SUBLANES, 1 << int(math.log2(max(1, ELEMENTWISE_BLOCK_ELEMS // cols))))
    if rows <= pref:
        return rows
    t = pref
    while rows % t:
        t //= 2
    assert t >= SUBLANES, rows
    return t


def _sum_parts(rs):
    nl = len(rs)
    p, rows, cols = rs[0].shape
    tr = _row_tile(rows, cols)

    def body(*refs):
        o_ref = refs[nl]
        for l in range(nl):
            acc = refs[l][0].astype(F32)
            for k in range(1, p):
                acc = acc + refs[l][k].astype(F32)
            o_ref[l] = acc

    return pl.pallas_call(
        body, name="sum_parts", grid=(rows // tr,), in_specs=[pl.BlockSpec((p, tr, cols), lambda i: (0, i, 0))] * nl,
        out_specs=pl.BlockSpec((nl, tr, cols), lambda i: (0, i, 0)), out_shape=jax.ShapeDtypeStruct((nl, rows, cols), F32),
        compiler_params=_cparams(),
    )(*rs)


def _adamw(w, g_parts, m, v):
    rows, cols = w.shape
    tr = _row_tile(rows, cols)
    ng = len(g_parts)
    c1 = 1.0 / (1.0 - ADAM_B1 ** ADAM_STEP)
    c2 = 1.0 / (1.0 - ADAM_B2 ** ADAM_STEP)

    def body(*refs):
        w_ref, m_ref, v_ref = refs[0], refs[1 + ng], refs[2 + ng]
        g_ref, dl_ref, nm_ref, nv_ref = refs[3 + ng:]
        g = refs[1][...]
        for k in range(1, ng):
            g = g + refs[1 + k][...]
        mn = ADAM_B1 * m_ref[...] + (1.0 - ADAM_B1) * g
        vn = ADAM_B2 * v_ref[...] + (1.0 - ADAM_B2) * (g * g)
        g_ref[...] = g
        nm_ref[...] = mn
        nv_ref[...] = vn
        dl_ref[...] = -ADAM_LR * ((mn * c1) / (jnp.sqrt(vn * c2) + ADAM_EPS) + ADAM_WD * w_ref[...])

    blk = pl.BlockSpec((tr, cols), lambda i: (i, 0))
    sd = jax.ShapeDtypeStruct((rows, cols), F32)
    return pl.pallas_call(
        body, name="adamw", grid=(rows // tr,), in_specs=[blk] * (3 + ng), out_specs=[blk] * 4, out_shape=[sd] * 4,
        compiler_params=_cparams(),
    )(w, *g_parts, m, v)


def _place():
    x, y, c = lax.axis_index("x"), lax.axis_index("y"), lax.axis_index("c")
    chips = [(1 - x, y), (x, 1 - y), (1 - x, 1 - y)]
    return x, y, c, chips


class Side:
    def __init__(self, ins, outs, kind, views=None):
        self.ins, self.outs, self.kind = list(ins), list(outs), kind
        n = len(self.ins)
        self.views = views or [None] * n
        self.sems = [pltpu.SemaphoreType.DMA((3 * n,)), pltpu.SemaphoreType.DMA((3 * n,)), pltpu.SemaphoreType.DMA((n,))]

    def _copies(self, ins, outs, send, recv, lsem):
        x, y, c, chips = _place()
        me = 2 * x + y
        local, out_going, in_coming = [], [], []
        for t in range(len(ins)):
            if self.kind == 'gather':
                src_local, srcs, dst_mine = ins[t], [ins[t]] * 3, outs[t].at[me]
            else:
                part = (lambda p, t=t: self.views[t](ins[t], p)) if self.views[t] else (lambda p, t=t: ins[t].at[p])
                src_local, srcs, dst_mine = part(me), [part(2 * px + py) for px, py in chips], outs[t].at[me]
            local.append(pltpu.make_async_copy(src_local, dst_mine, lsem.at[t]))
            for r, (px, py) in enumerate(chips):
                out_going.append(pltpu.make_async_remote_copy(
                    src_ref=srcs[r], dst_ref=dst_mine, send_sem=send.at[3 * t + r], recv_sem=recv.at[3 * t + r],
                    device_id=(px, py, c), device_id_type=MESH))
                in_coming.append(pltpu.make_async_remote_copy(
                    src_ref=srcs[r], dst_ref=outs[t].at[2 * px + py], send_sem=send.at[3 * t + r], recv_sem=recv.at[3 * t + r],
                    device_id=(px, py, c), device_id_type=MESH))
        return local, out_going, in_coming

    def start(self, ins, outs, send, recv, lsem):
        local, out_going, _ = self._copies(ins, outs, send, recv, lsem)
        for cp in local + out_going:
            cp.start()

    def wait(self, ins, outs, send, recv, lsem):
        local, out_going, in_coming = self._copies(ins, outs, send, recv, lsem)
        for cp in in_coming:
            cp.wait_recv()
        for cp in out_going:
            cp.wait_send()
        for cp in local:
            cp.wait()


def _gather_side(shards):
    return Side(shards, [jax.ShapeDtypeStruct((N_CHIPS,) + s.shape, s.dtype) for s in shards], 'gather')


def _scatter_side(grads, shapes, views):
    return Side(grads, [jax.ShapeDtypeStruct(s, g.dtype) for g, s in zip(grads, shapes)], 'scatter', views)


def _halves_view(ref, p):
    half = ref.shape[2] // 2
    return ref.at[p // 2, :, pl.ds((p % 2) * half, half)]


def _call_with_side(body, side, first, last, *, name, grid, in_specs, out_specs, out_shape, scratch_shapes, args):
    if side is None:
        outs = pl.pallas_call(body, name=name, grid=grid, in_specs=in_specs, out_specs=out_specs, out_shape=out_shape,
                              scratch_shapes=scratch_shapes, compiler_params=_cparams())(*args)
        return outs, []
    n_in, n_out, n_sc = len(in_specs), len(out_specs), len(scratch_shapes)
    ns_in, ns_out = len(side.ins), len(side.outs)

    def wrapped(*refs):
        base_in, s_in = refs[:n_in], refs[n_in:n_in + ns_in]
        o0 = n_in + ns_in
        base_out, s_out = refs[o0:o0 + n_out], refs[o0 + n_out:o0 + n_out + ns_out]
        sc0 = o0 + n_out + ns_out
        base_sc, sems = refs[sc0:sc0 + n_sc], refs[sc0 + n_sc:]

        @pl.when(first())
        def _():
            side.start(s_in, s_out, *sems)

        body(*base_in, *base_out, *base_sc)

        @pl.when(last())
        def _():
            side.wait(s_in, s_out, *sems)

    any_spec = pl.BlockSpec(memory_space=pl.ANY)
    outs = pl.pallas_call(
        wrapped, name=name, grid=grid, in_specs=list(in_specs) + [any_spec] * ns_in, out_specs=list(out_specs) + [any_spec] * ns_out,
        out_shape=list(out_shape) + side.outs, scratch_shapes=list(scratch_shapes) + side.sems, compiler_params=_cparams(),
    )(*args, *side.ins)
    return outs[:n_out], outs[n_out:]


def _run_side(name, side):
    def body(*refs):
        n = len(side.ins)
        side.start(refs[:n], refs[n:2 * n], *refs[2 * n:])
        side.wait(refs[:n], refs[n:2 * n], *refs[2 * n:])

    any_spec = pl.BlockSpec(memory_space=pl.ANY)
    return pl.pallas_call(body, name=name, in_specs=[any_spec] * len(side.ins), out_specs=[any_spec] * len(side.outs),
                          out_shape=side.outs, scratch_shapes=side.sems)(*side.ins)


def _gather_shards(shards, layer_major):
    n = len(shards)

    def body(*refs):
        ins, outs = refs[:n], refs[n:2 * n]
        send, recv, lsem = refs[2 * n:]
        x, y, c, chips = _place()
        me = 2 * x + y

        def slot(t, chip):
            return outs[t].at[:, chip] if layer_major[t] else outs[t].at[chip]

        local, sends = [], []
        for t in range(n):
            cp = pltpu.make_async_copy(ins[t], slot(t, me), lsem.at[t])
            cp.start()
            local.append(cp)
            for r, (px, py) in enumerate(chips):
                rc = pltpu.make_async_remote_copy(src_ref=ins[t], dst_ref=slot(t, me), send_sem=send.at[3 * t + r],
                                                  recv_sem=recv.at[3 * t + r], device_id=(px, py, c), device_id_type=MESH)
                rc.start()
                sends.append(rc)
        for t in range(n):
            for r, (px, py) in enumerate(chips):
                pltpu.make_async_remote_copy(src_ref=ins[t], dst_ref=slot(t, 2 * px + py), send_sem=send.at[3 * t + r],
                                             recv_sem=recv.at[3 * t + r], device_id=(px, py, c), device_id_type=MESH).wait_recv()
        for rc in sends:
            rc.wait_send()
        for cp in local:
            cp.wait()

    any_spec = pl.BlockSpec(memory_space=pl.ANY)
    return pl.pallas_call(
        body, name="gather_shards", in_specs=[any_spec] * n, out_specs=[any_spec] * n,
        out_shape=[jax.ShapeDtypeStruct((s.shape[0], N_CHIPS) + s.shape[1:] if lm else (N_CHIPS,) + s.shape, s.dtype)
                   for s, lm in zip(shards, layer_major)],
        scratch_shapes=[pltpu.SemaphoreType.DMA((3 * n,)), pltpu.SemaphoreType.DMA((3 * n,)), pltpu.SemaphoreType.DMA((n,))],
    )(*shards)


def _scatter_grads(groups):
    flat = [(gi, li, a) for gi, grp in enumerate(groups) for li, a in enumerate(grp)]
    n = len(flat)
    ng = len(groups)

    def body(*refs):
        ins, outs = refs[:n], refs[n:n + ng]
        send, recv, lsem = refs[n + ng:]
        x, y, c, chips = _place()
        me = 2 * x + y
        local, sends = [], []
        for t, (gi, li, _) in enumerate(flat):
            cp = pltpu.make_async_copy(ins[t].at[me], outs[gi].at[me, li], lsem.at[t])
            cp.start()
            local.append(cp)
            for r, (px, py) in enumerate(chips):
                rc = pltpu.make_async_remote_copy(src_ref=ins[t].at[2 * px + py], dst_ref=outs[gi].at[me, li],
                                                  send_sem=send.at[3 * t + r], recv_sem=recv.at[3 * t + r],
                                                  device_id=(px, py, c), device_id_type=MESH)
                rc.start()
                sends.append(rc)
        for t, (gi, li, _) in enumerate(flat):
            for r, (px, py) in enumerate(chips):
                pltpu.make_async_remote_copy(src_ref=ins[t].at[me], dst_ref=outs[gi].at[2 * px + py, li],
                                             send_sem=send.at[3 * t + r], recv_sem=recv.at[3 * t + r],
                                             device_id=(px, py, c), device_id_type=MESH).wait_recv()
        for rc in sends:
            rc.wait_send()
        for cp in local:
            cp.wait()

    any_spec = pl.BlockSpec(memory_space=pl.ANY)
    return pl.pallas_call(
        body, name="scatter_grads", in_specs=[any_spec] * n, out_specs=[any_spec] * ng,
        out_shape=[jax.ShapeDtypeStruct((N_CHIPS, len(grp)) + grp[0].shape[1:], grp[0].dtype) for grp in groups],
        scratch_shapes=[pltpu.SemaphoreType.DMA((3 * n,)), pltpu.SemaphoreType.DMA((3 * n,)), pltpu.SemaphoreType.DMA((n,))],
    )(*[a for _, _, a in flat])


def _swap_with_sibling(arrs):
    n = len(arrs)

    def body(*refs):
        ins, outs = refs[:n], refs[n:2 * n]
        send, recv = refs[2 * n:]
        x, y, c, _ = _place()
        cps = []
        for t in range(n):
            rc = pltpu.make_async_remote_copy(src_ref=ins[t], dst_ref=outs[t], send_sem=send.at[t], recv_sem=recv.at[t],
                                              device_id=(x, y, 1 - c), device_id_type=MESH)
            rc.start()
            cps.append(rc)
        for rc in cps:
            rc.wait_recv()
        for rc in cps:
            rc.wait_send()

    any_spec = pl.BlockSpec(memory_space=pl.ANY)
    return pl.pallas_call(
        body, name="swap_with_sibling", in_specs=[any_spec] * n, out_specs=[any_spec] * n,
        out_shape=[jax.ShapeDtypeStruct(a.shape, a.dtype) for a in arrs],
        scratch_shapes=[pltpu.SemaphoreType.DMA((n,)), pltpu.SemaphoreType.DMA((n,))],
    )(*arrs)


def _allreduce_small(v):
    rows, cols = v.shape

    def body(v_ref, o_ref, sib_ref, chip_ref, send, recv):
        x, y, c, chips = _place()
        me = 2 * x + y
        d2d = pltpu.make_async_remote_copy(src_ref=v_ref, dst_ref=sib_ref, send_sem=send.at[0], recv_sem=recv.at[0],
                                           device_id=(x, y, 1 - c), device_id_type=MESH)
        d2d.start()
        d2d.wait_recv()
        chip_ref[me] = v_ref[...] + sib_ref[...]
        sends = []
        for r, (px, py) in enumerate(chips):
            rc = pltpu.make_async_remote_copy(src_ref=chip_ref.at[me], dst_ref=chip_ref.at[me], send_sem=send.at[1 + r],
                                              recv_sem=recv.at[1 + r], device_id=(px, py, c), device_id_type=MESH)
            rc.start()
            sends.append(rc)
        for r, (px, py) in enumerate(chips):
            pltpu.make_async_remote_copy(src_ref=chip_ref.at[me], dst_ref=chip_ref.at[2 * px + py], send_sem=send.at[1 + r],
                                         recv_sem=recv.at[1 + r], device_id=(px, py, c), device_id_type=MESH).wait_recv()
        o_ref[...] = (chip_ref[0] + chip_ref[1]) + (chip_ref[2] + chip_ref[3])
        d2d.wait_send()
        for rc in sends:
            rc.wait_send()

    vm = pl.BlockSpec(memory_space=pltpu.VMEM)
    return pl.pallas_call(
        body, name="allreduce_small", in_specs=[vm], out_specs=vm, out_shape=jax.ShapeDtypeStruct((rows, cols), F32),
        scratch_shapes=[pltpu.VMEM((rows, cols), F32), pltpu.VMEM((N_CHIPS, rows, cols), F32), pltpu.SemaphoreType.DMA((4,)),
                        pltpu.SemaphoreType.DMA((4,))],
        compiler_params=_cparams(),
    )(v)


def _pack(tensors):
    pieces = []
    for t in tensors:
        flat = t.reshape(-1)
        pad = (-flat.shape[0]) % (SUBLANES * LANES)
        pieces.append(jnp.pad(flat, (0, pad)).reshape(-1, LANES))
    rows = sum(p.shape[0] for p in pieces)
    pieces.append(jnp.zeros(((-rows) % PACK_ROW_MULTIPLE, LANES), tensors[0].dtype))
    return jnp.concatenate(pieces, axis=0)


def _unpack(buf, like):
    out, off = [], 0
    for t in like:
        size = math.prod(t.shape)
        rows = -(-size // (SUBLANES * LANES)) * SUBLANES
        out.append(buf[off:off + rows].reshape(-1)[:size].reshape(t.shape))
        off += rows
    return out


def _s5_pack_b(bb):
    gc, g, p = bb.shape
    q = S5_GROUPS_PER_BLOCK
    t = bb.reshape(gc, g // q, q, p).transpose(1, 2, 0, 3)
    eye = jnp.eye(q, dtype=bb.dtype)
    return (t[:, :, :, None, :] * eye[None, :, None, :, None]).reshape(g // q, q * gc, q * p)


def _s5_unpack_b(dbp, gc, p):
    nb = dbp.shape[0]
    q = S5_GROUPS_PER_BLOCK
    eye = jnp.eye(q, dtype=dbp.dtype)
    t = (dbp.reshape(nb, q, gc, q, p) * eye[None, :, None, :, None]).sum(axis=3)
    return t.transpose(2, 0, 1, 3).reshape(gc, nb * q, p)


def _s5_pack_c(cc):
    g, gc, p = cc.shape
    q = S5_GROUPS_PER_BLOCK
    t = cc.reshape(g // q, q, gc, p).transpose(0, 1, 3, 2)
    eye = jnp.eye(q, dtype=cc.dtype)
    return (t[:, :, :, None, :] * eye[None, :, None, :, None]).reshape(g // q, q * p, q * gc)


def _s5_unpack_c(dcp, gc, p):
    nb = dcp.shape[0]
    q = S5_GROUPS_PER_BLOCK
    eye = jnp.eye(q, dtype=dcp.dtype)
    t = (dcp.reshape(nb, q, p, q, gc) * eye[None, :, None, :, None]).sum(axis=3)
    return t.transpose(0, 1, 3, 2).reshape(nb * q, gc, p)


def _split2(m):
    return m.arr[:, 0]


def kernel(x, norm_mix_g, norm_ffn_g, norm_final_g, rg_w_in, rg_conv_w, rg_conv_b, rg_w_a, rg_b_a, rg_w_x, rg_b_x, rg_lambda, rg_w_out, s5_w_in, s5_a_re, s5_a_im, s5_log_dt, s5_b_re, s5_b_im, s5_c_re, s5_c_im, s5_d, s5_w_glu, s5_w_out, ffn_w_up, ffn_conv_w, ffn_conv_b, ffn_w_down, loss_target, m_norm_mix_g, m_norm_ffn_g, m_norm_final_g, m_rg_w_in, m_rg_conv_w, m_rg_conv_b, m_rg_w_a, m_rg_b_a, m_rg_w_x, m_rg_b_x, m_rg_lambda, m_rg_w_out, m_s5_w_in, m_s5_a_re, m_s5_a_im, m_s5_log_dt, m_s5_b_re, m_s5_b_im, m_s5_c_re, m_s5_c_im, m_s5_d, m_s5_w_glu, m_s5_w_out, m_ffn_w_up, m_ffn_conv_w, m_ffn_conv_b, m_ffn_w_down, v_norm_mix_g, v_norm_ffn_g, v_norm_final_g, v_rg_w_in, v_rg_conv_w, v_rg_conv_b, v_rg_w_a, v_rg_b_a, v_rg_w_x, v_rg_b_x, v_rg_lambda, v_rg_w_out, v_s5_w_in, v_s5_a_re, v_s5_a_im, v_s5_log_dt, v_s5_b_re, v_s5_b_im, v_s5_c_re, v_s5_c_im, v_s5_d, v_s5_w_glu, v_s5_w_out, v_ffn_w_up, v_ffn_conv_w, v_ffn_conv_b, v_ffn_w_down):
    w = dict(zip(PARAM_NAMES, (norm_mix_g, norm_ffn_g, norm_final_g, rg_w_in, rg_conv_w, rg_conv_b, rg_w_a, rg_b_a, rg_w_x, rg_b_x,
                               rg_lambda, rg_w_out, s5_w_in, s5_a_re, s5_a_im, s5_log_dt, s5_b_re, s5_b_im, s5_c_re, s5_c_im, s5_d,
                               s5_w_glu, s5_w_out, ffn_w_up, ffn_conv_w, ffn_conv_b, ffn_w_down)))
    mom = dict(zip(PARAM_NAMES, (m_norm_mix_g, m_norm_ffn_g, m_norm_final_g, m_rg_w_in, m_rg_conv_w, m_rg_conv_b, m_rg_w_a, m_rg_b_a,
                                 m_rg_w_x, m_rg_b_x, m_rg_lambda, m_rg_w_out, m_s5_w_in, m_s5_a_re, m_s5_a_im, m_s5_log_dt, m_s5_b_re,
                                 m_s5_b_im, m_s5_c_re, m_s5_c_im, m_s5_d, m_s5_w_glu, m_s5_w_out, m_ffn_w_up, m_ffn_conv_w,
                                 m_ffn_conv_b, m_ffn_w_down)))
    vel = dict(zip(PARAM_NAMES, (v_norm_mix_g, v_norm_ffn_g, v_norm_final_g, v_rg_w_in, v_rg_conv_w, v_rg_conv_b, v_rg_w_a, v_rg_b_a,
                                 v_rg_w_x, v_rg_b_x, v_rg_lambda, v_rg_w_out, v_s5_w_in, v_s5_a_re, v_s5_a_im, v_s5_log_dt, v_s5_b_re,
                                 v_s5_b_im, v_s5_c_re, v_s5_c_im, v_s5_d, v_s5_w_glu, v_s5_w_out, v_ffn_w_up, v_ffn_conv_w,
                                 v_ffn_conv_b, v_ffn_w_down)))
    _, s, d = x.shape
    depth = norm_mix_g.shape[0]
    n_grp, n_state = s5_a_re.shape[1], s5_a_re.shape[2]
    gc = s5_b_re.shape[3]
    d_ff = ffn_w_down.shape[1] * N_CHIPS
    s5_ts = min(256, s)
    s5_perm = _segment_perm(s5_ts)

    wb = {n: (w[n].astype(BF16) if n in BIG else w[n]) for n in SHARDED}
    gath = {}

    def mixer_keys(i):
        return [(n, i // 2) for n in MIXER_SHARDED[i % 2]] if i < depth else []

    def gather_side(keys):
        return _gather_side([wb[n][l] for n, l in keys])

    def put(keys, arrs):
        for k, a in zip(keys, arrs):
            gath[k] = a

    def wcol(n, l):
        return Mat(gath[(n, l)][:, None], 0, 'c')

    def wrow(n, l):
        g = gath[(n, l)]
        return Mat(g.reshape(1, 1, N_CHIPS * g.shape[1], g.shape[2]), 0, 'c')

    def rg_cw(l):
        return gath[('rg_conv_w', l)].transpose(1, 0, 2).reshape(RG_CONV_W, d)

    def s5_dv(l):
        return gath[('s5_d', l)].reshape(1, d)

    def f_cw(l):
        return gath[('ffn_conv_w', l)].transpose(1, 0, 2).reshape(FFN_CONV_W, 2, d_ff).transpose(1, 0, 2)

    tm = min(1024, s)
    d_up = 2 * d_ff // N_CHIPS
    f_cb = ffn_conv_b.reshape(depth, 2, 1, d_ff)
    put(mixer_keys(0), _run_side("gather_first", gather_side(mixer_keys(0))))

    h = x.reshape(s, d)
    saved = []
    for i in range(depth):
        j = i // 2
        sv = {'h_in': h}
        hn = _rms_fwd(h, norm_mix_g[i:i + 1])
        sv['hn'] = hn
        up_keys = [('ffn_w_up', i), ('ffn_conv_w', i)]
        if i % 2 == 0:
            xg = _mm("rg_in", 'nn', act(hn), wcol('rg_w_in', j), out_parts=2, tm=tm, tn=512, tk=d)
            xg2 = _split2(xg)
            wa, wx = rg_w_a[j].astype(BF16), rg_w_x[j].astype(BF16)
            ba, bx = rg_b_a[j].reshape(1, d), rg_b_x[j].reshape(1, d)
            (xr, hs, y), got = _rg_fwd(xg2, rg_cw(j), rg_conv_b[j:j + 1], wa, ba, wx, bx, rg_lambda[j:j + 1],
                                       side=gather_side(up_keys))
            put(up_keys, got)
            sv.update(xg2=xg2, xr=xr, hs=hs, y=y, wa=wa, wx=wx, ba=ba, bx=bx)
            h = _mm("rg_out", 'nn', act(y), wrow('rg_w_out', j), res=act(h), tm=tm, tn=d, tk=d).arr[0, 0]
        else:
            u = _mm("s5_in", 'nn', act(hn), wrow('s5_w_in', j), tm=tm, tn=d, tk=d).arr[0, 0]
            bt_re, bt_im = s5_b_re[j].transpose(2, 0, 1), s5_b_im[j].transpose(2, 0, 1)
            ldt = s5_log_dt[j].reshape(n_grp, 1)
            tab_r, tab_i, bbr, bbi = _s5_tables3(s5_a_re[j], s5_a_im[j], ldt, bt_re, bt_im, seg=s5_ts // SUBLANES)
            nn_ = n_grp * n_state
            tab_r, tab_i = tab_r.reshape(5, SUBLANES, nn_), tab_i.reshape(5, SUBLANES, nn_)
            prm = dict(bp_r=_s5_pack_b(bbr).astype(BF16), bp_i=_s5_pack_b(bbi).astype(BF16),
                       cp_r=_s5_pack_c(s5_c_re[j]).astype(BF16), cp_i=_s5_pack_c(s5_c_im[j]).astype(BF16), dvec=s5_dv(j))
            (hr, hi, ypre, gy), got = _s5_fwd3(u, s5_perm, s5_perm.T, tab_r, tab_i, ts=s5_ts, side=gather_side(up_keys), **prm)
            sv.update(rtab_r=tab_r[:, ::-1], rtab_i=-tab_i[:, ::-1])
            put(up_keys, got)
            gl = _mm("s5_glu", 'nn', act(gy), wcol('s5_w_glu', j), out_parts=2, tm=tm, tn=512, tk=d)
            gl2 = _split2(gl)
            o = _glu(gl2)
            sv.update(u=u, prm=prm, hr=hr, hi=hi, ypre=ypre, gy=gy, gl2=gl2, o=o, bt_re=bt_re, bt_im=bt_im, ldt=ldt)
            h = _mm("s5_out", 'nn', act(o), wrow('s5_w_out', j), res=act(h), tm=tm, tn=d, tk=d).arr[0, 0]
        sv['h_mid'] = h
        hn2 = _rms_fwd(h, norm_ffn_g[i:i + 1])
        next_keys = [('ffn_w_down', i)] + mixer_keys(i + 1)
        (up2, c2, a_ffn), got = _ffn_up_act(hn2, gath[('ffn_w_up', i)], f_cw(i), f_cb[i], side=gather_side(next_keys))
        put(next_keys, got)
        sv.update(hn2=hn2, up2=up2, c2=c2, act=a_ffn)
        h = _mm("ffn_down", 'nn', act(a_ffn), wrow('ffn_w_down', i), res=act(h), tm=tm, tn=d, tk=d_ff // 2).arr[0, 0]
        saved.append(sv)

    loss_row, dh, dg_final = _loss_and_grad(h, norm_final_g.reshape(1, d), loss_target.reshape(s, d))
    loss = lax.psum(loss_row[0, 0], ("x", "y", "c"))

    gl_ = {n: [None] * w[n].shape[0] for n in PARAM_NAMES if n != 'norm_final_g'}
    recvd = {}

    def as4(n, a):
        return a.reshape((N_CHIPS,) + w[n].shape[1:])

    def scatter_side(keys):
        arrs, shapes, views = [], [], []
        for n, l in keys:
            shape = (N_CHIPS,) + w[n].shape[1:]
            halves = False
            arrs.append(gl_[n][l] if halves else gl_[n][l].reshape(shape))
            views.append(_halves_view if halves else None)
            shapes.append(shape)
        return _scatter_side(arrs, shapes, views)

    def record(keys, arrs):
        for k, a in zip(keys, arrs):
            recvd[k] = a

    pending = None
    for i in reversed(range(depth)):
        j = i // 2
        sv = saved[i]
        gl_['ffn_w_down'][i] = _mm("ffn_down_dw", 'tn', act(sv['act']), act(dh), out_dtype=BF16, tm=d_ff // 2, tn=d, tk=tm).arr
        (dup2, dcw2, dcb2), got = _ffn_bwd_fused(dh, gath[('ffn_w_down', i)].reshape(d_ff, d), sv['up2'], sv['c2'], f_cw(i),
                                                 side=scatter_side(pending) if pending else None)
        if pending:
            record(pending, got)
        gl_['ffn_conv_w'][i] = dcw2.transpose(1, 0, 2).reshape(FFN_CONV_W, 2 * d_ff)
        gl_['ffn_conv_b'][i] = dcb2.reshape(2 * d_ff)
        dup = Mat(dup2[:, None], 0, 'c')
        gl_['ffn_w_up'][i] = _mm("ffn_up_dw", 'tn', act(sv['hn2']), dup, out_parts=N_CHIPS, out_dtype=BF16, tm=d, tn=d_up, tk=tm).arr
        dh, dg = _mm_rms_bwd("ffn_up_dx", dup, wcol('ffn_w_up', i), sv['h_mid'], norm_ffn_g[i:i + 1], dh, tm=tm, tk=d_up)
        gl_['norm_ffn_g'][i] = dg[0]
        ffn_keys = [('ffn_w_up', i), ('ffn_w_down', i)]
        if i % 2 == 0:
            dy = _mm("rg_out_dx", 'nt', act(dh), wrow('rg_w_out', j), tm=tm, tn=d, tk=d).arr[0, 0]
            gl_['rg_w_out'][j] = _mm("rg_out_dw", 'tn', act(sv['y']), act(dh), out_dtype=BF16, tm=d, tn=d, tk=tm).arr
            (dxg2, dcw, dcb, dwa, dba, dwx, dbx, dlam), got = _rg_bwd(
                dy, sv['xg2'], sv['xr'], sv['hs'], rg_cw(j), sv['wa'], sv['ba'], sv['wx'], sv['bx'], rg_lambda[j:j + 1],
                side=scatter_side(ffn_keys))
            record(ffn_keys, got)
            gl_['rg_conv_w'][j] = dcw
            gl_['rg_conv_b'][j] = dcb[0]
            gl_['rg_w_a'][j], gl_['rg_w_x'][j] = dwa, dwx
            gl_['rg_b_a'][j], gl_['rg_b_x'][j] = dba.reshape(rg_b_a.shape[1:]), dbx.reshape(rg_b_x.shape[1:])
            gl_['rg_lambda'][j] = dlam[0]
            dxg = Mat(dxg2[:, None], 0, 'c')
            gl_['rg_w_in'][j] = _mm("rg_in_dw", 'tn', act(sv['hn']), dxg, out_parts=N_CHIPS, out_dtype=BF16, tm=d, tn=512, tk=tm).arr
            mix_dx = ("rg_in_dx", dxg, wcol('rg_w_in', j), 512)
            pending = [('rg_w_in', j), ('rg_w_out', j)]
        else:
            d_o = _mm("s5_out_dx", 'nt', act(dh), wrow('s5_w_out', j), tm=tm, tn=d, tk=d).arr[0, 0]
            gl_['s5_w_out'][j] = _mm("s5_out_dw", 'tn', act(sv['o']), act(dh), out_dtype=BF16, tm=d, tn=d, tk=tm).arr
            dgl2 = _glu_bwd(sv['gl2'], d_o)
            dgl = Mat(dgl2[:, None], 0, 'c')
            gl_['s5_w_glu'][j] = _mm("s5_glu_dw", 'tn', act(sv['gy']), dgl, out_parts=N_CHIPS, out_dtype=BF16, tm=d, tn=512, tk=tm).arr
            dgy = _mm("s5_glu_dx", 'nt', dgl, wcol('s5_w_glu', j), tm=tm, tn=d, tk=512).arr[0, 0]
            (du, dar, dai, dbpr, dbpi, dcpr, dcpi, dd), got = _s5_bwd3(
                dgy, sv['ypre'], sv['u'], sv['hr'], sv['hi'], s5_perm, s5_perm.T, sv['rtab_r'], sv['rtab_i'], ts=s5_ts,
                side=scatter_side(ffn_keys), **sv['prm'])
            record(ffn_keys, got)
            gl_['s5_d'][j] = dd[0]
            gl_['s5_c_re'][j] = _s5_unpack_c(dcpr, gc, n_state)
            gl_['s5_c_im'][j] = -_s5_unpack_c(dcpi, gc, n_state)
            d_are, d_aim, d_ldt, d_btr, d_bti = _s5_params_bwd(
                s5_a_re[j], s5_a_im[j], sv['ldt'], sv['bt_re'], sv['bt_im'], dar.reshape(n_grp, n_state), dai.reshape(n_grp, n_state),
                _s5_unpack_b(dbpr, gc, n_state), _s5_unpack_b(dbpi, gc, n_state))
            gl_['s5_a_re'][j], gl_['s5_a_im'][j], gl_['s5_log_dt'][j] = d_are, d_aim, d_ldt[:, 0]
            gl_['s5_b_re'][j], gl_['s5_b_im'][j] = d_btr.transpose(1, 2, 0), d_bti.transpose(1, 2, 0)
            dum = act(du)
            gl_['s5_w_in'][j] = _mm("s5_in_dw", 'tn', act(sv['hn']), dum, out_dtype=BF16, tm=d, tn=d, tk=tm).arr
            mix_dx = ("s5_in_dx", dum, wrow('s5_w_in', j), d)
            pending = [('s5_w_in', j), ('s5_w_glu', j), ('s5_w_out', j)]
        dh, dg = _mm_rms_bwd(mix_dx[0], mix_dx[1], mix_dx[2], sv['h_in'], norm_mix_g[i:i + 1], dh, tm=tm, tk=mix_dx[3])
        gl_['norm_mix_g'][i] = dg[0]
    grad_x = dh.reshape(x.shape)
    record(pending, _run_side("scatter_last", scatter_side(pending)))

    chip_sums = []
    for n in BIG:
        cols = w[n].shape[-1]
        chip_sums.append(_sum_parts([recvd[(n, l)].reshape(N_CHIPS, -1, cols) for l in range(w[n].shape[0])]))
    sib_sums = _swap_with_sibling(chip_sums)
    results = {}
    for n, mine, theirs in zip(BIG, chip_sums, sib_sums):
        cols = w[n].shape[-1]
        outs = _adamw(w[n].reshape(-1, cols), [mine.reshape(-1, cols), theirs.reshape(-1, cols)], mom[n].reshape(-1, cols),
                      vel[n].reshape(-1, cols))
        results[n] = [o.reshape(w[n].shape) for o in outs]

    small = REPLICATED + SMALL_SHARDED
    local = [dg_final.reshape(d) if n == 'norm_final_g' else jnp.stack(gl_[n]) for n in small]
    summed = _unpack(_allreduce_small(_pack(local)), local)
    me = 2 * lax.axis_index("x") + lax.axis_index("y")
    grads = [lax.dynamic_slice_in_dim(g, me * w[n].shape[-1], w[n].shape[-1], axis=g.ndim - 1) if n in SMALL_SHARDED else g
             for n, g in zip(small, summed)]
    like = [w[n] for n in small]
    outs = _adamw(_pack(like), [_pack(grads)], _pack([mom[n] for n in small]), _pack([vel[n] for n in small]))
    unpacked = [_unpack(o, like) for o in outs]
    for k, n in enumerate(small):
        results[n] = [unpacked[q][k] for q in range(4)]

    return (loss, grad_x, *[results[n][0] for n in PARAM_NAMES], *[results[n][1] for n in PARAM_NAMES],
            *[results[n][2] for n in PARAM_NAMES], *[results[n][3] for n in PARAM_NAMES])
```

```python
import functools
import math

import jax
import jax.numpy as jnp
from jax import lax
from jax.experimental import pallas as pl
from jax.experimental.pallas import tpu as pltpu

F32 = jnp.float32
BF16 = jnp.bfloat16
MESH = pl.DeviceIdType.MESH

NORM_EPS = 1e-6
RG_HEADS = 8
RG_CONV_W = 4
RG_C = 8.0
S5_GC = 16
S5_P = 64
S5_GROUPS_PER_BLOCK = 8
FFN_CONV_W = 3
N_CHIPS = 4
ADAM_LR, ADAM_B1, ADAM_B2, ADAM_EPS, ADAM_WD, ADAM_STEP = 0.001, 0.9, 0.999, 1e-08, 0.01, 10
VMEM_LIMIT_BYTES = 56 * 1024 * 1024
SUBLANES = 8
LANES = 128

PARAM_NAMES = ['norm_mix_g', 'norm_ffn_g', 'norm_final_g', 'rg_w_in', 'rg_conv_w', 'rg_conv_b', 'rg_w_a', 'rg_b_a', 'rg_w_x',
               'rg_b_x', 'rg_lambda', 'rg_w_out', 's5_w_in', 's5_a_re', 's5_a_im', 's5_log_dt', 's5_b_re', 's5_b_im', 's5_c_re',
               's5_c_im', 's5_d', 's5_w_glu', 's5_w_out', 'ffn_w_up', 'ffn_conv_w', 'ffn_conv_b', 'ffn_w_down']
SHARDED = ['rg_w_in', 'rg_conv_w', 'rg_w_out', 's5_w_in', 's5_d', 's5_w_glu', 's5_w_out', 'ffn_w_up', 'ffn_conv_w', 'ffn_w_down']
BIG = ['rg_w_in', 'rg_w_out', 's5_w_in', 's5_w_glu', 's5_w_out', 'ffn_w_up', 'ffn_w_down']
ROW_SHARDED = ['rg_w_out', 's5_w_in', 's5_w_out', 'ffn_w_down']
SMALL_SHARDED = ['rg_conv_w', 's5_d', 'ffn_conv_w']
MIXER_SHARDED = [['rg_w_in', 'rg_conv_w', 'rg_w_out'], ['s5_w_in', 's5_d', 's5_w_glu', 's5_w_out']]
FFN_SHARDED = ['ffn_w_up', 'ffn_conv_w', 'ffn_w_down']
REPLICATED = [n for n in PARAM_NAMES if n not in SHARDED]


def _cparams():
    return pltpu.CompilerParams(vmem_limit_bytes=VMEM_LIMIT_BYTES)


_GELU_C = math.sqrt(2.0 / math.pi)
_GELU_K = 0.044715


def _gelu(x):
    return 0.5 * x * (1.0 + jnp.tanh(_GELU_C * (x + _GELU_K * x * x * x)))


def _gelu_and_grad(x):
    t = jnp.tanh(_GELU_C * (x + _GELU_K * x * x * x))
    g = 0.5 * x * (1.0 + t)
    dg = 0.5 * (1.0 + t) + 0.5 * x * (1.0 - t * t) * (_GELU_C * (1.0 + 3.0 * _GELU_K * x * x))
    return g, dg


def _sigmoid(x):
    return jax.nn.sigmoid(x)


def _neg_expm1(x):
    series = -(x * (1.0 + x * (0.5 + x * (1.0 / 6 + x * (1.0 / 24 + x * (1.0 / 120 + x * (1.0 / 720)))))))
    return jnp.where(x > -0.25, series, 1.0 - jnp.exp(x))


def _softplus(z):
    return jnp.maximum(z, 0.0) + jnp.log1p(jnp.exp(-jnp.abs(z)))


def _rows(shape):
    return lax.broadcasted_iota(jnp.int32, shape, 0)


def _shift_down(x, halo, k):
    ext = jnp.concatenate([halo, x], axis=0)
    return pltpu.roll(ext, k, 0)[SUBLANES:]


def _shift_up(x, halo, k):
    ext = jnp.concatenate([x, halo], axis=0)
    n = ext.shape[0]
    return pltpu.roll(ext, n - k, 0)[:x.shape[0]]


def _scan_real_fwd(a, b):
    n = a.shape[0]
    row = _rows(a.shape)
    sh = 1
    while sh < n:
        ok = row >= sh
        b = a * jnp.where(ok, pltpu.roll(b, sh, 0), 0.0) + b
        if sh * 2 < n:
            a = a * jnp.where(ok, pltpu.roll(a, sh, 0), 1.0)
        sh *= 2
    return b


def _scan_real_rev(c, d):
    n = c.shape[0]
    row = _rows(c.shape)
    sh = 1
    while sh < n:
        ok = row < n - sh
        d = c * jnp.where(ok, pltpu.roll(d, n - sh, 0), 0.0) + d
        if sh * 2 < n:
            c = c * jnp.where(ok, pltpu.roll(c, n - sh, 0), 1.0)
        sh *= 2
    return d


def _scan_cplx(br, bi, pr_ref, pi_ref, reverse):
    n = br.shape[0]
    row = _rows(br.shape)
    sh, k = 1, 0
    while sh < n:
        pr = pr_ref[k:k + 1, :]
        pi = pi_ref[k:k + 1, :]
        if reverse:
            ok = row < n - sh
            sr = jnp.where(ok, pltpu.roll(br, n - sh, 0), 0.0)
            si = jnp.where(ok, pltpu.roll(bi, n - sh, 0), 0.0)
        else:
            ok = row >= sh
            sr = jnp.where(ok, pltpu.roll(br, sh, 0), 0.0)
            si = jnp.where(ok, pltpu.roll(bi, sh, 0), 0.0)
        br, bi = br + pr * sr - pi * si, bi + pr * si + pi * sr
        sh *= 2
        k += 1
    return br, bi


RG_LANE_CHUNK = 512


def _real_slab_scan(a_ref, b_ref, out_ref, carry_ref, reverse):
    t, c = a_ref.shape
    nsl = t // SUBLANES
    lc = min(RG_LANE_CHUNK, c)
    row8 = _rows((SUBLANES, lc))
    for q in range(c // lc):
        sl = slice(q * lc, (q + 1) * lc)

        def slab(jj, carry, sl=sl):
            j = nsl - 1 - jj if reverse else jj
            r0 = pl.multiple_of(j * SUBLANES, SUBLANES)
            a, b = a_ref[pl.ds(r0, SUBLANES), sl], b_ref[pl.ds(r0, SUBLANES), sl]
            for k in range(3):
                sh = 1 << k
                keep = row8 < SUBLANES - sh if reverse else row8 >= sh
                amount = SUBLANES - sh if reverse else sh
                b = a * jnp.where(keep, pltpu.roll(b, amount, 0), 0.0) + b
                a = a * jnp.where(keep, pltpu.roll(a, amount, 0), 1.0)
            x = b + a * jnp.broadcast_to(carry, b.shape)
            out_ref[pl.ds(r0, SUBLANES), sl] = x
            return x[:1, :] if reverse else x[SUBLANES - 1:, :]

        carry_ref[:, sl] = lax.fori_loop(0, nsl, slab, carry_ref[:, sl], unroll=2)


class Mat:
    def __init__(self, arr, l=0, split='c'):
        assert arr.ndim == 4
        self.arr, self.l, self.split = arr, l, split
        p, _, r, c = arr.shape
        self.shape = (r, c * p) if split == 'c' else (r * p, c)

    def spec(self, tr, tc, rc):
        p, _, r, c = self.arr.shape
        l = self.l
        assert r % tr == 0 and c % tc == 0, (self.arr.shape, tr, tc)
        if self.split == 'c':
            per = c // tc
            return pl.BlockSpec((None, None, tr, tc), lambda i, j, k: (rc(i, j, k)[1] // per, l, rc(i, j, k)[0], rc(i, j, k)[1] % per))
        per = r // tr
        return pl.BlockSpec((None, None, tr, tc), lambda i, j, k: (rc(i, j, k)[0] // per, l, rc(i, j, k)[0] % per, rc(i, j, k)[1]))


def act(x, parts=1):
    s, c = x.shape
    return Mat(x.reshape(s, parts, c // parts).transpose(1, 0, 2)[:, None] if parts > 1 else x[None, None])


def _mm(name, mode, a, b, *, out_parts=1, out_split='c', out_dtype=F32, res=None, tm=512, tn=512, tk=512):
    if mode == 'nn':
        (m, kk), (kb, n) = a.shape, b.shape
    elif mode == 'nt':
        (m, kk), (n, kb) = a.shape, b.shape
    else:
        (kk, m), (kb, n) = a.shape, b.shape
    assert kk == kb, (name, a.shape, b.shape)
    tm, tn, tk = min(tm, m), min(tn, n), min(tk, kk)
    assert m % tm == 0 and n % tn == 0 and kk % tk == 0, (name, m, n, kk, tm, tn, tk)
    nk = kk // tk
    if mode == 'nn':
        a_spec = a.spec(tm, tk, lambda i, j, k: (i, k))
        b_spec = b.spec(tk, tn, lambda i, j, k: (k, j))
        dims = (((1,), (0,)), ((), ()))
    elif mode == 'nt':
        a_spec = a.spec(tm, tk, lambda i, j, k: (i, k))
        b_spec = b.spec(tn, tk, lambda i, j, k: (j, k))
        dims = (((1,), (1,)), ((), ()))
    else:
        a_spec = a.spec(tk, tm, lambda i, j, k: (k, i))
        b_spec = b.spec(tk, tn, lambda i, j, k: (k, j))
        dims = (((0,), (0,)), ((), ()))
    if out_split == 'c':
        out_arr = jax.ShapeDtypeStruct((out_parts, 1, m, n // out_parts), out_dtype)
    else:
        out_arr = jax.ShapeDtypeStruct((out_parts, 1, m // out_parts, n), out_dtype)
    out_mat = Mat(out_arr, 0, out_split)
    o_spec = out_mat.spec(tm, tn, lambda i, j, k: (i, j))
    has_res = res is not None

    def body(*refs):
        if has_res:
            a_ref, b_ref, r_ref, o_ref = refs[:4]
        else:
            a_ref, b_ref, o_ref = refs[:3]
        prod = lax.dot_general(a_ref[...].astype(BF16), b_ref[...].astype(BF16), dims, preferred_element_type=F32)

        def finish(acc):
            if has_res:
                acc = acc + r_ref[...]
            o_ref[...] = acc.astype(out_dtype)

        if nk == 1:
            finish(prod)
        else:
            acc_ref = refs[-1]
            k = pl.program_id(2)

            @pl.when(k == 0)
            def _():
                acc_ref[...] = prod

            @pl.when(k > 0)
            def _():
                acc_ref[...] += prod

            @pl.when(k == nk - 1)
            def _():
                finish(acc_ref[...])

    in_specs = [a_spec, b_spec]
    args = [a.arr, b.arr]
    if has_res:
        in_specs.append(res.spec(tm, tn, lambda i, j, k: (i, j)))
        args.append(res.arr)
    out = pl.pallas_call(
        body, name=name, grid=(m // tm, n // tn, nk), in_specs=in_specs, out_specs=o_spec, out_shape=out_arr,
        scratch_shapes=[pltpu.VMEM((tm, tn), F32)] if nk > 1 else [], compiler_params=_cparams(),
    )(*args)
    return Mat(out, 0, out_split)


def _rms_fwd(h, g, ts=512):
    s, d = h.shape
    ts = min(ts, s)

    def body(h_ref, g_ref, o_ref):
        x = h_ref[...]
        var = jnp.mean(x * x, axis=-1, keepdims=True)
        o_ref[...] = (x * lax.rsqrt(var + NORM_EPS) * g_ref[...]).astype(BF16)

    return pl.pallas_call(
        body, name="rms_fwd", grid=(s // ts,),
        in_specs=[pl.BlockSpec((ts, d), lambda i: (i, 0)), pl.BlockSpec((1, d), lambda i: (0, 0))],
        out_specs=pl.BlockSpec((ts, d), lambda i: (i, 0)), out_shape=jax.ShapeDtypeStruct((s, d), BF16),
        compiler_params=_cparams(),
    )(h, g)


def _rms_bwd(h, g, dhn, dh_in, ts=512):
    s, d = h.shape
    ts = min(ts, s)

    def body(h_ref, g_ref, dhn_ref, dhin_ref, dh_ref, dg_ref):
        i = pl.program_id(0)
        x = h_ref[...]
        rstd = lax.rsqrt(jnp.mean(x * x, axis=-1, keepdims=True) + NORM_EPS)
        xhat = x * rstd
        dhn_v = dhn_ref[...]
        dxh = dhn_v * g_ref[...]
        dh_ref[...] = dhin_ref[...] + rstd * (dxh - xhat * jnp.mean(dxh * xhat, axis=-1, keepdims=True))
        part = jnp.sum(dhn_v * xhat, axis=0, keepdims=True)

        @pl.when(i == 0)
        def _():
            dg_ref[...] = part

        @pl.when(i > 0)
        def _():
            dg_ref[...] += part

    row = pl.BlockSpec((ts, d), lambda i: (i, 0))
    vec = pl.BlockSpec((1, d), lambda i: (0, 0))
    return pl.pallas_call(
        body, name="rms_bwd", grid=(s // ts,), in_specs=[row, vec, row, row], out_specs=[row, vec],
        out_shape=[jax.ShapeDtypeStruct((s, d), F32), jax.ShapeDtypeStruct((1, d), F32)], compiler_params=_cparams(),
    )(h, g, dhn, dh_in)


def _loss_and_grad(h, g, tgt, ts=512):
    s, d = h.shape
    ts = min(ts, s)

    def body(h_ref, g_ref, t_ref, loss_ref, dh_ref, dg_ref):
        i = pl.program_id(0)
        x = h_ref[...]
        gv = g_ref[...]
        rstd = lax.rsqrt(jnp.mean(x * x, axis=-1, keepdims=True) + NORM_EPS)
        xhat = x * rstd
        err = xhat * gv - t_ref[...]
        dy = err * (1.0 / d)
        dxh = dy * gv
        dh_ref[...] = rstd * (dxh - xhat * jnp.mean(dxh * xhat, axis=-1, keepdims=True))
        part = jnp.sum(dy * xhat, axis=0, keepdims=True)
        lpart = jnp.broadcast_to(jnp.sum(jnp.sum(err * err, axis=0, keepdims=True), axis=1, keepdims=True) * (0.5 / d), (1, LANES))

        @pl.when(i == 0)
        def _():
            dg_ref[...] = part
            loss_ref[...] = lpart

        @pl.when(i > 0)
        def _():
            dg_ref[...] += part
            loss_ref[...] += lpart

    row = pl.BlockSpec((ts, d), lambda i: (i, 0))
    vec = pl.BlockSpec((1, d), lambda i: (0, 0))
    return pl.pallas_call(
        body, name="loss_and_grad", grid=(s // ts,), in_specs=[row, vec, row],
        out_specs=[pl.BlockSpec((1, LANES), lambda i: (0, 0)), row, vec],
        out_shape=[jax.ShapeDtypeStruct((1, LANES), F32), jax.ShapeDtypeStruct((s, d), F32), jax.ShapeDtypeStruct((1, d), F32)],
        compiler_params=_cparams(),
    )(h, g, tgt)


def _halo_before(ts, nrow8):
    return lambda i: jnp.maximum(i * (ts // SUBLANES) - 1, 0)


def _ffn_act(up2, conv_w2, conv_b2, ts=512, tn=512, side=None):
    _, s, f = up2.shape
    ts, tn = min(ts, s), min(tn, f)
    kw = FFN_CONV_W

    def body(up_ref, halo_ref, w_ref, b_ref, o_ref):
        i = pl.program_id(0)
        cs = []
        for h in range(2):
            x = up_ref[h]
            halo = jnp.where(i == 0, 0.0, halo_ref[h])
            c = b_ref[h] + w_ref[h, kw - 1:kw, :] * x
            for sft in range(1, kw):
                c = c + w_ref[h, kw - 1 - sft:kw - sft, :] * _shift_down(x, halo, sft)
            cs.append(c)
        o_ref[...] = (_gelu(cs[0]) * cs[1]).astype(BF16)

    hb = ts // SUBLANES
    g0, g1 = s // ts, f // tn
    outs, side_outs = _call_with_side(
        body, side, lambda: (pl.program_id(0) == 0) & (pl.program_id(1) == 0),
        lambda: (pl.program_id(0) == g0 - 1) & (pl.program_id(1) == g1 - 1),
        name="ffn_act", grid=(g0, g1),
        in_specs=[pl.BlockSpec((2, ts, tn), lambda i, j: (0, i, j)),
                  pl.BlockSpec((2, SUBLANES, tn), lambda i, j: (0, jnp.maximum(i * hb - 1, 0), j)),
                  pl.BlockSpec((2, kw, tn), lambda i, j: (0, 0, j)),
                  pl.BlockSpec((2, 1, tn), lambda i, j: (0, 0, j))],
        out_specs=[pl.BlockSpec((ts, tn), lambda i, j: (i, j))], out_shape=[jax.ShapeDtypeStruct((s, f), BF16)],
        scratch_shapes=[], args=(up2, up2, conv_w2, conv_b2))
    return outs[0], side_outs


def _ffn_bwd(up2, dact, conv_w2, conv_b2, ts=256, tn=512, side=None):
    _, s, f = up2.shape
    ts, tn = min(ts, s), min(tn, f)
    kw = FFN_CONV_W
    nt = s // ts
    hb = ts // SUBLANES
    last8 = s // SUBLANES - 1

    def body(up_ref, hb_ref, ha_ref, da_ref, dah_ref, w_ref, b_ref, dup_ref, dw_ref, db_ref):
        i = pl.program_id(1)
        first, last = i == 0, i == nt - 1
        ce, xs = [], []
        for h in range(2):
            x = up_ref[h]
            before = jnp.where(first, 0.0, hb_ref[h])
            after = ha_ref[h]
            ext = jnp.concatenate([before, x, after], axis=0)
            c = b_ref[h] + w_ref[h, kw - 1:kw, :] * ext
            shifted = [ext]
            for sft in range(1, kw):
                sh = pltpu.roll(ext, sft, 0)
                shifted.append(sh)
                c = c + w_ref[h, kw - 1 - sft:kw - sft, :] * sh
            ce.append(c[SUBLANES:])
            xs.append([sh[SUBLANES:SUBLANES + ts] for sh in shifted])
        da = jnp.concatenate([da_ref[...], jnp.where(last, 0.0, dah_ref[...])], axis=0)
        g1, dg1 = _gelu_and_grad(ce[0])
        dcs = [da * ce[1] * dg1, da * g1]
        for h in range(2):
            dc = dcs[h]
            n = dc.shape[0]
            dup = w_ref[h, kw - 1:kw, :] * dc[:ts]
            for sft in range(1, kw):
                dup = dup + w_ref[h, kw - 1 - sft:kw - sft, :] * pltpu.roll(dc, n - sft, 0)[:ts]
            dup_ref[h] = dup.astype(BF16)
            dct = dc[:ts]
            dbp = jnp.sum(dct, axis=0, keepdims=True)
            dwp = [jnp.sum(dct * xs[h][kw - 1 - k], axis=0, keepdims=True) for k in range(kw)]

            @pl.when(first)
            def _():
                db_ref[h] = dbp
                for k in range(kw):
                    dw_ref[h, k:k + 1, :] = dwp[k]

            @pl.when(i > 0)
            def _():
                db_ref[h] += dbp
                for k in range(kw):
                    dw_ref[h, k:k + 1, :] += dwp[k]

    g0 = f // tn
    return _call_with_side(
        body, side, lambda: (pl.program_id(0) == 0) & (pl.program_id(1) == 0),
        lambda: (pl.program_id(0) == g0 - 1) & (pl.program_id(1) == nt - 1),
        name="ffn_bwd", grid=(g0, nt),
        in_specs=[pl.BlockSpec((2, ts, tn), lambda j, i: (0, i, j)),
                  pl.BlockSpec((2, SUBLANES, tn), lambda j, i: (0, jnp.maximum(i * hb - 1, 0), j)),
                  pl.BlockSpec((2, SUBLANES, tn), lambda j, i: (0, jnp.minimum((i + 1) * hb, last8), j)),
                  pl.BlockSpec((ts, tn), lambda j, i: (i, j)),
                  pl.BlockSpec((SUBLANES, tn), lambda j, i: (jnp.minimum((i + 1) * hb, last8), j)),
                  pl.BlockSpec((2, kw, tn), lambda j, i: (0, 0, j)),
                  pl.BlockSpec((2, 1, tn), lambda j, i: (0, 0, j))],
        out_specs=[pl.BlockSpec((2, ts, tn), lambda j, i: (0, i, j)),
                   pl.BlockSpec((2, kw, tn), lambda j, i: (0, 0, j)),
                   pl.BlockSpec((2, 1, tn), lambda j, i: (0, 0, j))],
        out_shape=[jax.ShapeDtypeStruct((2, s, f), BF16), jax.ShapeDtypeStruct((2, kw, f), F32),
                   jax.ShapeDtypeStruct((2, 1, f), F32)],
        scratch_shapes=[], args=(up2, up2, up2, dact, dact, conv_w2, conv_b2))


def _mm_rms_bwd(name, a, b, h, g, dh_in, *, tm, tk):
    (m, kk), (n, kb) = a.shape, b.shape
    assert kk == kb and h.shape == (m, n), (name, a.shape, b.shape, h.shape)
    tm, tk = min(tm, m), min(tk, kk)
    nk = kk // tk
    dims = (((1,), (1,)), ((), ()))

    def body(a_ref, b_ref, h_ref, g_ref, dhin_ref, dh_ref, dg_ref, *acc):
        i, k = pl.program_id(0), pl.program_id(2)
        prod = lax.dot_general(a_ref[...].astype(BF16), b_ref[...].astype(BF16), dims, preferred_element_type=F32)

        def finish(dhn):
            x = h_ref[...]
            rstd = lax.rsqrt(jnp.mean(x * x, axis=-1, keepdims=True) + NORM_EPS)
            xhat = x * rstd
            dxh = dhn * g_ref[...]
            dh_ref[...] = dhin_ref[...] + rstd * (dxh - xhat * jnp.mean(dxh * xhat, axis=-1, keepdims=True))
            part = jnp.sum(dhn * xhat, axis=0, keepdims=True)

            @pl.when(i == 0)
            def _():
                dg_ref[...] = part

            @pl.when(i > 0)
            def _():
                dg_ref[...] += part

        if nk == 1:
            finish(prod)
        else:
            acc_ref = acc[0]

            @pl.when(k == 0)
            def _():
                acc_ref[...] = prod

            @pl.when(k > 0)
            def _():
                acc_ref[...] += prod

            @pl.when(k == nk - 1)
            def _():
                finish(acc_ref[...])

    row = pl.BlockSpec((tm, n), lambda i, j, k: (i, 0))
    vec = pl.BlockSpec((1, n), lambda i, j, k: (0, 0))
    return pl.pallas_call(
        body, name=name, grid=(m // tm, 1, nk),
        in_specs=[a.spec(tm, tk, lambda i, j, k: (i, k)), b.spec(n, tk, lambda i, j, k: (0, k)), row, vec, row],
        out_specs=[row, vec], out_shape=[jax.ShapeDtypeStruct((m, n), F32), jax.ShapeDtypeStruct((1, n), F32)],
        scratch_shapes=[pltpu.VMEM((tm, n), F32)] if nk > 1 else [], compiler_params=_cparams(),
    )(a.arr, b.arr, h, g, dh_in)


def _ffn_up_act(hn2, w_up4, conv_w2, conv_b2, ts=1024, tn=512, sub=1024, side=None):
    s, d = hn2.shape
    p, _, wc = w_up4.shape
    f = p * wc // 2
    ts, tn = min(ts, s), min(tn, wc)
    sub = min(sub, ts)
    per = wc // tn
    kw = FFN_CONV_W
    g0, g1 = f // tn, s // ts

    def body(hn_ref, w1_ref, w2_ref, cw_ref, cb_ref, up_ref, c_ref, act_ref, carry_ref):
        @pl.when(pl.program_id(1) == 0)
        def _():
            carry_ref[...] = jnp.zeros_like(carry_ref)

        for q in range(ts // sub):
            rows = slice(q * sub, (q + 1) * sub)
            hn = hn_ref[rows, :]
            cs = []
            for h, w_ref in enumerate((w1_ref, w2_ref)):
                x = jnp.dot(hn, w_ref[...], preferred_element_type=F32)
                up_ref[h, rows, :] = x
                halo = carry_ref[h]
                c = cb_ref[h] + cw_ref[h, kw - 1:kw, :] * x
                for sft in range(1, kw):
                    c = c + cw_ref[h, kw - 1 - sft:kw - sft, :] * _shift_down(x, halo, sft)
                carry_ref[h] = x[sub - SUBLANES:, :]
                c_ref[h, rows, :] = c
                cs.append(c)
            act_ref[rows, :] = (_gelu(cs[0]) * cs[1]).astype(BF16)

    outs, side_outs = _call_with_side(
        body, side, lambda: (pl.program_id(0) == 0) & (pl.program_id(1) == 0),
        lambda: (pl.program_id(0) == g0 - 1) & (pl.program_id(1) == g1 - 1),
        name="ffn_up_act", grid=(g0, g1),
        in_specs=[pl.BlockSpec((ts, d), lambda j, i: (i, 0)),
                  pl.BlockSpec((None, d, tn), lambda j, i: (j // per, 0, j % per)),
                  pl.BlockSpec((None, d, tn), lambda j, i: (p // 2 + j // per, 0, j % per)),
                  pl.BlockSpec((2, kw, tn), lambda j, i: (0, 0, j)),
                  pl.BlockSpec((2, 1, tn), lambda j, i: (0, 0, j))],
        out_specs=[pl.BlockSpec((2, ts, tn), lambda j, i: (0, i, j)), pl.BlockSpec((2, ts, tn), lambda j, i: (0, i, j)),
                   pl.BlockSpec((ts, tn), lambda j, i: (i, j))],
        out_shape=[jax.ShapeDtypeStruct((2, s, f), F32), jax.ShapeDtypeStruct((2, s, f), F32), jax.ShapeDtypeStruct((s, f), BF16)],
        scratch_shapes=[pltpu.VMEM((2, SUBLANES, tn), F32)], args=(hn2, w_up4, w_up4, conv_w2, conv_b2))
    return outs, side_outs


def _ffn_bwd_fused(dh, w_down, up2, c2, conv_w2, ts=512, tn=512, side=None):
    s, d = dh.shape
    _, _, f = up2.shape
    ts, tn = min(ts, s), min(tn, f)
    kw = FFN_CONV_W
    nt = s // ts
    hb = ts // SUBLANES
    g0 = f // tn
    nt_dims = (((1,), (1,)), ((), ()))

    def body(dh_ref, wd_ref, up_ref, c_ref, w_ref, dup_ref, dw_ref, db_ref, carry_ref):
        i = pl.program_id(1)
        first_step = i == 0

        @pl.when(first_step)
        def _():
            carry_ref[...] = jnp.zeros_like(carry_ref)

        da = lax.dot_general(dh_ref[...].astype(BF16), wd_ref[...], nt_dims, preferred_element_type=F32)
        g1, dg1 = _gelu_and_grad(c_ref[0])
        dcs = [da * c_ref[1] * dg1, da * g1]
        for h in range(2):
            dc = dcs[h]
            after = carry_ref[h]
            ups = [dc] + [_shift_up(dc, after, sft) for sft in range(1, kw)]
            dup = w_ref[h, kw - 1:kw, :] * dc
            for sft in range(1, kw):
                dup = dup + w_ref[h, kw - 1 - sft:kw - sft, :] * ups[sft]
            carry_ref[h] = dc[:SUBLANES]
            dup_ref[h] = dup.astype(BF16)
            dbp = jnp.sum(dc, axis=0, keepdims=True)
            x = up_ref[h]
            dwp = [jnp.sum(ups[kw - 1 - k] * x, axis=0, keepdims=True) for k in range(kw)]

            @pl.when(first_step)
            def _():
                db_ref[h] = dbp
                for k in range(kw):
                    dw_ref[h, k:k + 1, :] = dwp[k]

            @pl.when(i > 0)
            def _():
                db_ref[h] += dbp
                for k in range(kw):
                    dw_ref[h, k:k + 1, :] += dwp[k]

    rev = lambda i: nt - 1 - i
    return _call_with_side(
        body, side, lambda: (pl.program_id(0) == 0) & (pl.program_id(1) == 0),
        lambda: (pl.program_id(0) == g0 - 1) & (pl.program_id(1) == nt - 1),
        name="ffn_bwd", grid=(g0, nt),
        in_specs=[pl.BlockSpec((ts, d), lambda j, i: (rev(i), 0)),
                  pl.BlockSpec((tn, d), lambda j, i: (j, 0)),
                  pl.BlockSpec((2, ts, tn), lambda j, i: (0, rev(i), j)),
                  pl.BlockSpec((2, ts, tn), lambda j, i: (0, rev(i), j)),
                  pl.BlockSpec((2, kw, tn), lambda j, i: (0, 0, j))],
        out_specs=[pl.BlockSpec((2, ts, tn), lambda j, i: (0, rev(i), j)),
                   pl.BlockSpec((2, kw, tn), lambda j, i: (0, 0, j)),
                   pl.BlockSpec((2, 1, tn), lambda j, i: (0, 0, j))],
        out_shape=[jax.ShapeDtypeStruct((2, s, f), BF16), jax.ShapeDtypeStruct((2, kw, f), F32),
                   jax.ShapeDtypeStruct((2, 1, f), F32)],
        scratch_shapes=[pltpu.VMEM((2, SUBLANES, tn), F32)], args=(dh, w_down, up2, c2, conv_w2))


def _ffn_fwd(h, g, w_up4, w_down, conv_w2, conv_b2, ts=512, tn=512, sub=256, side=None):
    s, d = h.shape
    p, _, wc = w_up4.shape
    f = p * wc // 2
    ts, tn = min(ts, s), min(tn, wc)
    sub = min(sub, ts)
    per = wc // tn
    kw = FFN_CONV_W
    g0, g1 = s // ts, f // tn

    def body(h_ref, g_ref, w1_ref, w2_ref, wd_ref, cw_ref, cb_ref, ho_ref, hn_ref, up_ref, c_ref, act_ref, carry_ref):
        i, j = pl.program_id(0), pl.program_id(1)

        @pl.when(j == 0)
        def _():
            x = h_ref[...]
            var = jnp.mean(x * x, axis=-1, keepdims=True)
            hn_ref[...] = (x * lax.rsqrt(var + NORM_EPS) * g_ref[...]).astype(BF16)
            ho_ref[...] = x

        @pl.when(i == 0)
        def _():
            carry_ref[j] = jnp.zeros(carry_ref.shape[1:], F32)

        for q in range(ts // sub):
            rows = slice(q * sub, (q + 1) * sub)
            hn = hn_ref[rows, :]
            cs = []
            for hf, w_ref in enumerate((w1_ref, w2_ref)):
                x = jnp.dot(hn, w_ref[...], preferred_element_type=F32)
                up_ref[hf, rows, :] = x
                halo = carry_ref[j, hf]
                c = cb_ref[hf] + cw_ref[hf, kw - 1:kw, :] * x
                for sft in range(1, kw):
                    c = c + cw_ref[hf, kw - 1 - sft:kw - sft, :] * _shift_down(x, halo, sft)
                carry_ref[j, hf] = x[sub - SUBLANES:, :]
                c_ref[hf, rows, :] = c
                cs.append(c)
            a = (_gelu(cs[0]) * cs[1]).astype(BF16)
            act_ref[rows, :] = a
            ho_ref[rows, :] += jnp.dot(a, wd_ref[...], preferred_element_type=F32)

    row = pl.BlockSpec((ts, d), lambda i, j: (i, 0))
    col2 = pl.BlockSpec((2, ts, tn), lambda i, j: (0, i, j))
    return _call_with_side(
        body, side, lambda: (pl.program_id(0) == 0) & (pl.program_id(1) == 0),
        lambda: (pl.program_id(0) == g0 - 1) & (pl.program_id(1) == g1 - 1),
        name="ffn_fwd", grid=(g0, g1),
        in_specs=[row, pl.BlockSpec((1, d), lambda i, j: (0, 0)),
                  pl.BlockSpec((None, d, tn), lambda i, j: (j // per, 0, j % per)),
                  pl.BlockSpec((None, d, tn), lambda i, j: (p // 2 + j // per, 0, j % per)),
                  pl.BlockSpec((tn, d), lambda i, j: (j, 0)),
                  pl.BlockSpec((2, kw, tn), lambda i, j: (0, 0, j)),
                  pl.BlockSpec((2, 1, tn), lambda i, j: (0, 0, j))],
        out_specs=[row, row, col2, col2, pl.BlockSpec((ts, tn), lambda i, j: (i, j))],
        out_shape=[jax.ShapeDtypeStruct((s, d), F32), jax.ShapeDtypeStruct((s, d), BF16), jax.ShapeDtypeStruct((2, s, f), F32),
                   jax.ShapeDtypeStruct((2, s, f), F32), jax.ShapeDtypeStruct((s, f), BF16)],
        scratch_shapes=[pltpu.VMEM((g1, 2, SUBLANES, tn), F32)],
        args=(h, g, w_up4, w_up4, w_down, conv_w2, conv_b2))


def _ffn_bwd_all(dh, w_down, up2, c2, hn, act_, conv_w2, ts=256, tn=512, sub=128, side=None):
    s, d = dh.shape
    _, _, f = up2.shape
    ts, tn = min(ts, s), min(tn, f)
    sub = min(sub, ts)
    kw = FFN_CONV_W
    nt = s // ts
    g0 = f // tn
    nt_dims = (((1,), (1,)), ((), ()))
    tn_dims = (((0,), (0,)), ((), ()))

    def body(dh_ref, wd_ref, up_ref, c_ref, hn_ref, act_ref, w_ref, dup_ref, dw_ref, db_ref, dwu_ref, dwd_ref,
             carry_ref, dwu_acc, dwd_acc):
        i = pl.program_id(1)
        first_step = i == 0

        @pl.when(first_step)
        def _():
            carry_ref[...] = jnp.zeros_like(carry_ref)
            dwu_acc[...] = jnp.zeros_like(dwu_acc)
            dwd_acc[...] = jnp.zeros_like(dwd_acc)
            dw_ref[...] = jnp.zeros_like(dw_ref)
            db_ref[...] = jnp.zeros_like(db_ref)

        dhb = dh_ref[...].astype(BF16)
        da_all = lax.dot_general(dhb, wd_ref[...], nt_dims, preferred_element_type=F32)
        for q in reversed(range(ts // sub)):
            rows = slice(q * sub, (q + 1) * sub)
            da = da_all[rows, :]
            g1, dg1 = _gelu_and_grad(c_ref[0, rows, :])
            dcs = [da * c_ref[1, rows, :] * dg1, da * g1]
            hnq = hn_ref[rows, :]
            for hf in range(2):
                dc = dcs[hf]
                after = carry_ref[hf]
                ups = [dc] + [_shift_up(dc, after, sft) for sft in range(1, kw)]
                dup = w_ref[hf, kw - 1:kw, :] * dc
                for sft in range(1, kw):
                    dup = dup + w_ref[hf, kw - 1 - sft:kw - sft, :] * ups[sft]
                carry_ref[hf] = dc[:SUBLANES]
                dupb = dup.astype(BF16)
                dup_ref[hf, rows, :] = dupb
                dwu_acc[hf] += lax.dot_general(hnq, dupb, tn_dims, preferred_element_type=F32)
                db_ref[hf] += jnp.sum(dc, axis=0, keepdims=True)
                x = up_ref[hf, rows, :]
                for k in range(kw):
                    dw_ref[hf, k:k + 1, :] += jnp.sum(ups[kw - 1 - k] * x, axis=0, keepdims=True)
            dwd_acc[...] += lax.dot_general(act_ref[rows, :], dhb[rows, :], tn_dims, preferred_element_type=F32)

        @pl.when(i == nt - 1)
        def _():
            dwu_ref[...] = dwu_acc[...].astype(BF16)
            dwd_ref[...] = dwd_acc[...].astype(BF16)

    rev = lambda i: nt - 1 - i
    col2 = pl.BlockSpec((2, ts, tn), lambda j, i: (0, rev(i), j))
    return _call_with_side(
        body, side, lambda: (pl.program_id(0) == 0) & (pl.program_id(1) == 0),
        lambda: (pl.program_id(0) == g0 - 1) & (pl.program_id(1) == nt - 1),
        name="ffn_bwd", grid=(g0, nt),
        in_specs=[pl.BlockSpec((ts, d), lambda j, i: (rev(i), 0)),
                  pl.BlockSpec((tn, d), lambda j, i: (j, 0)),
                  col2, col2,
                  pl.BlockSpec((ts, d), lambda j, i: (rev(i), 0)),
                  pl.BlockSpec((ts, tn), lambda j, i: (rev(i), j)),
                  pl.BlockSpec((2, kw, tn), lambda j, i: (0, 0, j))],
        out_specs=[col2,
                   pl.BlockSpec((2, kw, tn), lambda j, i: (0, 0, j)),
                   pl.BlockSpec((2, 1, tn), lambda j, i: (0, 0, j)),
                   pl.BlockSpec((2, d, tn), lambda j, i: (0, 0, j)),
                   pl.BlockSpec((tn, d), lambda j, i: (j, 0))],
        out_shape=[jax.ShapeDtypeStruct((2, s, f), BF16), jax.ShapeDtypeStruct((2, kw, f), F32),
                   jax.ShapeDtypeStruct((2, 1, f), F32), jax.ShapeDtypeStruct((2, d, f), BF16), jax.ShapeDtypeStruct((f, d), BF16)],
        scratch_shapes=[pltpu.VMEM((2, SUBLANES, tn), F32), pltpu.VMEM((2, d, tn), F32), pltpu.VMEM((tn, d), F32)],
        args=(dh, w_down, up2, c2, hn, act_, conv_w2))


def _rg_gates(xr, wa_ref, ba_ref, wx_ref, bx_ref, lam_ref):
    bw = wa_ref.shape[-1]
    xb = xr.astype(BF16)
    za = jnp.concatenate([jnp.dot(xb[:, h * bw:(h + 1) * bw], wa_ref[h], preferred_element_type=F32)
                          for h in range(RG_HEADS)], axis=1) + ba_ref[...]
    zx = jnp.concatenate([jnp.dot(xb[:, h * bw:(h + 1) * bw], wx_ref[h], preferred_element_type=F32)
                          for h in range(RG_HEADS)], axis=1) + bx_ref[...]
    r, ig = _sigmoid(za), _sigmoid(zx)
    sp = _softplus(-lam_ref[...])
    la = -RG_C * r * sp
    a = jnp.exp(la)
    mult = jnp.sqrt(_neg_expm1(2.0 * la))
    return xb, r, ig, sp, a, mult


def _rg_fwd(xg2, conv_w, conv_b, w_a, b_a, w_x, b_x, lam, ts=256, side=None):
    _, s, c = xg2.shape
    ts = min(ts, s)
    kw = RG_CONV_W
    hb = ts // SUBLANES

    def body(xg_ref, halo_ref, cw_ref, cb_ref, wa_ref, ba_ref, wx_ref, bx_ref, lam_ref, xr_ref, hs_ref, y_ref, carry_ref,
             a_scr, b_scr):
        i = pl.program_id(0)

        @pl.when(i == 0)
        def _():
            carry_ref[...] = jnp.zeros_like(carry_ref)

        xp = xg_ref[0]
        halo = jnp.where(i == 0, 0.0, halo_ref[...])
        xr = cb_ref[...] + cw_ref[kw - 1:kw, :] * xp
        for sft in range(1, kw):
            xr = xr + cw_ref[kw - 1 - sft:kw - sft, :] * _shift_down(xp, halo, sft)
        _, r, ig, sp, a, mult = _rg_gates(xr, wa_ref, ba_ref, wx_ref, bx_ref, lam_ref)
        a_scr[...] = a
        b_scr[...] = mult * (ig * xr)
        _real_slab_scan(a_scr, b_scr, hs_ref, carry_ref, reverse=False)
        xr_ref[...] = xr
        y_ref[...] = (hs_ref[...] * _gelu(xg_ref[1])).astype(BF16)

    full = lambda shape: pl.BlockSpec(shape, lambda i: (0,) * len(shape))
    row_spec = pl.BlockSpec((ts, c), lambda i: (i, 0))
    nt = s // ts
    return _call_with_side(
        body, side, lambda: pl.program_id(0) == 0, lambda: pl.program_id(0) == nt - 1,
        name="rg_fwd", grid=(nt,),
        in_specs=[pl.BlockSpec((2, ts, c), lambda i: (0, i, 0)),
                  pl.BlockSpec((None, SUBLANES, c), lambda i: (0, jnp.maximum(i * hb - 1, 0), 0)),
                  full(conv_w.shape), full(conv_b.shape), full(w_a.shape), full(b_a.shape), full(w_x.shape), full(b_x.shape),
                  full(lam.shape)],
        out_specs=[row_spec, row_spec, row_spec],
        out_shape=[jax.ShapeDtypeStruct((s, c), F32), jax.ShapeDtypeStruct((s, c), F32), jax.ShapeDtypeStruct((s, c), BF16)],
        scratch_shapes=[pltpu.VMEM((1, c), F32), pltpu.VMEM((ts, c), F32), pltpu.VMEM((ts, c), F32)],
        args=(xg2, xg2, conv_w, conv_b, w_a, b_a, w_x, b_x, lam))


def _rg_bwd(dy, xg2, xr, hs, conv_w, w_a, b_a, w_x, b_x, lam, ts=256, side=None):
    _, s, c = xg2.shape
    ts = min(ts, s)
    nt = s // ts
    kw = RG_CONV_W
    hb = ts // SUBLANES
    bw = c // RG_HEADS
    tn_dims = (((0,), (0,)), ((), ()))
    nt_dims = (((1,), (1,)), ((), ()))

    def body(dy_ref, xg_ref, xph_ref, xr_ref, hs_ref, hsh_ref, cw_ref, wa_ref, ba_ref, wx_ref, bx_ref, lam_ref,
             dxg_ref, dcw_ref, dcb_ref, dwa_ref, dba_ref, dwx_ref, dbx_ref, dlam_ref,
             lam_carry, a_carry, dxr_carry, dsp_acc, a_scr, b_scr):
        i = pl.program_id(0)
        first_step = i == 0
        time_first = i == nt - 1

        @pl.when(first_step)
        def _():
            lam_carry[...] = jnp.zeros_like(lam_carry)
            a_carry[...] = jnp.ones_like(a_carry)
            dxr_carry[...] = jnp.zeros_like(dxr_carry)
            dsp_acc[...] = jnp.zeros_like(dsp_acc)
            for ref in (dcw_ref, dcb_ref, dwa_ref, dba_ref, dwx_ref, dbx_ref):
                ref[...] = jnp.zeros_like(ref)

        xr = xr_ref[...]
        hs = hs_ref[...]
        gate = xg_ref[1]
        xb, r, ig, sp, a, mult = _rg_gates(xr, wa_ref, ba_ref, wx_ref, bx_ref, lam_ref)
        dyv = dy_ref[...]
        gg, dgg = _gelu_and_grad(gate)
        dhs = dyv * gg
        dxg_ref[1] = (dyv * hs * dgg).astype(BF16)
        row = _rows(xr.shape)
        a_scr[...] = jnp.where(row == ts - 1, a_carry[0:1, :], pltpu.roll(a, ts - 1, 0))
        b_scr[...] = dhs
        _real_slab_scan(a_scr, b_scr, b_scr, lam_carry, reverse=True)
        lmb = b_scr[...]
        a_carry[...] = a[:SUBLANES]
        hs_prev = _shift_down(hs, jnp.where(time_first, 0.0, hsh_ref[...]), 1)
        d_a = lmb * hs_prev
        d_m = lmb * (ig * xr)
        d_ig = lmb * mult * xr
        d_xr = lmb * mult * ig
        d_la = a * d_a - (a * a / mult) * d_m
        dsp_acc[...] += jnp.sum(-RG_C * r * d_la, axis=0, keepdims=True)
        d_za = (-RG_C * sp) * d_la * r * (1.0 - r)
        d_zx = d_ig * ig * (1.0 - ig)
        dba_ref[...] += jnp.sum(d_za, axis=0, keepdims=True)
        dbx_ref[...] += jnp.sum(d_zx, axis=0, keepdims=True)
        dzab, dzxb = d_za.astype(BF16), d_zx.astype(BF16)
        back = []
        for h in range(RG_HEADS):
            sl = slice(h * bw, (h + 1) * bw)
            dwa_ref[h] += lax.dot_general(xb[:, sl], dzab[:, sl], tn_dims, preferred_element_type=F32)
            dwx_ref[h] += lax.dot_general(xb[:, sl], dzxb[:, sl], tn_dims, preferred_element_type=F32)
            back.append(lax.dot_general(dzab[:, sl], wa_ref[h], nt_dims, preferred_element_type=F32)
                        + lax.dot_general(dzxb[:, sl], wx_ref[h], nt_dims, preferred_element_type=F32))
        d_xr = d_xr + jnp.concatenate(back, axis=1)
        d_xp = cw_ref[kw - 1:kw, :] * d_xr
        after = dxr_carry[...]
        for sft in range(1, kw):
            d_xp = d_xp + cw_ref[kw - 1 - sft:kw - sft, :] * _shift_up(d_xr, after, sft)
        dxr_carry[...] = d_xr[:SUBLANES]
        dxg_ref[0] = d_xp.astype(BF16)
        xp = xg_ref[0]
        before = jnp.where(time_first, 0.0, xph_ref[...])
        dcb_ref[...] += jnp.sum(d_xr, axis=0, keepdims=True)
        dcw_ref[kw - 1:kw, :] += jnp.sum(d_xr * xp, axis=0, keepdims=True)
        for sft in range(1, kw):
            dcw_ref[kw - 1 - sft:kw - sft, :] += jnp.sum(d_xr * _shift_down(xp, before, sft), axis=0, keepdims=True)
        dlam_ref[...] = dsp_acc[...] * (-_sigmoid(-lam_ref[...]))

    full = lambda shape: pl.BlockSpec(shape, lambda i: (0,) * len(shape))
    rev = lambda i: nt - 1 - i
    row_spec = pl.BlockSpec((ts, c), lambda i: (rev(i), 0))
    halo_idx = lambda i: jnp.maximum(rev(i) * hb - 1, 0)
    vec = (1, c)
    return _call_with_side(
        body, side, lambda: pl.program_id(0) == 0, lambda: pl.program_id(0) == nt - 1,
        name="rg_bwd", grid=(nt,),
        in_specs=[row_spec,
                  pl.BlockSpec((2, ts, c), lambda i: (0, rev(i), 0)),
                  pl.BlockSpec((None, SUBLANES, c), lambda i: (0, halo_idx(i), 0)),
                  row_spec, row_spec,
                  pl.BlockSpec((SUBLANES, c), lambda i: (halo_idx(i), 0)),
                  full(conv_w.shape), full(w_a.shape), full(b_a.shape), full(w_x.shape), full(b_x.shape), full(lam.shape)],
        out_specs=[pl.BlockSpec((2, ts, c), lambda i: (0, rev(i), 0)), full(conv_w.shape), full(vec), full(w_a.shape), full(vec),
                   full(w_x.shape), full(vec), full(vec)],
        out_shape=[jax.ShapeDtypeStruct((2, s, c), BF16), jax.ShapeDtypeStruct(conv_w.shape, F32), jax.ShapeDtypeStruct(vec, F32),
                   jax.ShapeDtypeStruct(w_a.shape, F32), jax.ShapeDtypeStruct(vec, F32), jax.ShapeDtypeStruct(w_x.shape, F32),
                   jax.ShapeDtypeStruct(vec, F32), jax.ShapeDtypeStruct(vec, F32)],
        scratch_shapes=[pltpu.VMEM(vec, F32), pltpu.VMEM((SUBLANES, c), F32), pltpu.VMEM((SUBLANES, c), F32),
                        pltpu.VMEM(vec, F32), pltpu.VMEM((ts, c), F32), pltpu.VMEM((ts, c), F32)],
        args=(dy, xg2, xg2, xr, hs, hs, conv_w, w_a, b_a, w_x, b_x, lam))


def _s5_param_fn(a_re, a_im, log_dt, bt_re, bt_im):
    dt = jnp.exp(log_dt)
    mag = jnp.exp(a_re * dt)
    abr = mag * jnp.cos(a_im * dt)
    abi = mag * jnp.sin(a_im * dt)
    ur, ui = abr - 1.0, abi
    den = a_re * a_re + a_im * a_im
    wr = (ur * a_re + ui * a_im) / den
    wi = (ui * a_re - ur * a_im) / den
    bbr = wr[None] * bt_re - wi[None] * bt_im
    bbi = wr[None] * bt_im + wi[None] * bt_re
    return abr, abi, bbr, bbi


def _s5_params(a_re, a_im, log_dt, bt_re, bt_im, nlev):
    g, p = a_re.shape
    gc = bt_re.shape[0]

    def body(ar_ref, ai_ref, dt_ref, br_ref, bi_ref, abr_ref, abi_ref, pr_ref, pi_ref, bbr_ref, bbi_ref):
        abr, abi, bbr, bbi = _s5_param_fn(ar_ref[...], ai_ref[...], dt_ref[...], br_ref[...], bi_ref[...])
        abr_ref[...] = abr
        abi_ref[...] = abi
        bbr_ref[...] = bbr
        bbi_ref[...] = bbi
        qr, qi = abr, abi
        for k in range(nlev):
            pr_ref[k] = qr
            pi_ref[k] = qi
            qr, qi = qr * qr - qi * qi, 2.0 * qr * qi

    sd = jax.ShapeDtypeStruct
    return pl.pallas_call(
        body, name="s5_params",
        out_shape=[sd((g, p), F32), sd((g, p), F32), sd((nlev, g, p), F32), sd((nlev, g, p), F32), sd((gc, g, p), F32),
                   sd((gc, g, p), F32)],
    )(a_re, a_im, log_dt, bt_re, bt_im)


def _s5_params_bwd(a_re, a_im, log_dt, bt_re, bt_im, d_abr, d_abi, d_bbr, d_bbi):
    def body(ar_ref, ai_ref, dt_ref, br_ref, bi_ref, g0, g1, g2, g3, o0, o1, o2, o3, o4):
        _, vjp = jax.vjp(_s5_param_fn, ar_ref[...], ai_ref[...], dt_ref[...], br_ref[...], bi_ref[...])
        outs = vjp((g0[...], g1[...], g2[...], g3[...]))
        for o, v in zip((o0, o1, o2, o3, o4), outs):
            o[...] = v

    sd = jax.ShapeDtypeStruct
    return pl.pallas_call(
        body, name="s5_params_bwd",
        out_shape=[sd(a_re.shape, F32), sd(a_im.shape, F32), sd(log_dt.shape, F32), sd(bt_re.shape, F32), sd(bt_im.shape, F32)],
    )(a_re, a_im, log_dt, bt_re, bt_im, d_abr, d_abi, d_bbr, d_bbi)


def _s5_fwd(u, abr, abi, pw_r, pw_i, bp_r, bp_i, cp_r, cp_i, dvec, ts=128, side=None):
    s, c = u.shape
    n = abr.shape[1]
    nblk, cb, nb = bp_r.shape
    ts = min(ts, s)

    def body(u_ref, ar_ref, ai_ref, pr_ref, pi_ref, bpr_ref, bpi_ref, cpr_ref, cpi_ref, d_ref,
             hr_ref, hi_ref, yp_ref, gy_ref, car_r, car_i):
        i = pl.program_id(0)

        @pl.when(i == 0)
        def _():
            car_r[...] = jnp.zeros_like(car_r)
            car_i[...] = jnp.zeros_like(car_i)

        uv = u_ref[...]
        ub = uv.astype(BF16)
        br = jnp.concatenate([jnp.dot(ub[:, k * cb:(k + 1) * cb], bpr_ref[k], preferred_element_type=F32) for k in range(nblk)], axis=1)
        bi = jnp.concatenate([jnp.dot(ub[:, k * cb:(k + 1) * cb], bpi_ref[k], preferred_element_type=F32) for k in range(nblk)], axis=1)
        ar, ai = ar_ref[...], ai_ref[...]
        pr, pi_ = car_r[SUBLANES - 1:SUBLANES, :], car_i[SUBLANES - 1:SUBLANES, :]
        row = _rows(br.shape)
        br = br + jnp.where(row == 0, ar * pr - ai * pi_, 0.0)
        bi = bi + jnp.where(row == 0, ar * pi_ + ai * pr, 0.0)
        hr, hi = _scan_cplx(br, bi, pr_ref, pi_ref, reverse=False)
        car_r[...] = hr[ts - SUBLANES:]
        car_i[...] = hi[ts - SUBLANES:]
        hr_ref[...] = hr
        hi_ref[...] = hi
        hrb, hib = hr.astype(BF16), hi.astype(BF16)
        y = jnp.concatenate([jnp.dot(hrb[:, k * nb:(k + 1) * nb], cpr_ref[k], preferred_element_type=F32)
                             - jnp.dot(hib[:, k * nb:(k + 1) * nb], cpi_ref[k], preferred_element_type=F32) for k in range(nblk)], axis=1)
        yp = y + d_ref[...] * uv
        yp_ref[...] = yp
        gy_ref[...] = _gelu(yp).astype(BF16)

    full = lambda shape: pl.BlockSpec(shape, lambda i: (0,) * len(shape))
    rc = pl.BlockSpec((ts, c), lambda i: (i, 0))
    rn = pl.BlockSpec((ts, n), lambda i: (i, 0))
    sd = jax.ShapeDtypeStruct
    nt = s // ts
    return _call_with_side(
        body, side, lambda: pl.program_id(0) == 0, lambda: pl.program_id(0) == nt - 1,
        name="s5_fwd", grid=(nt,),
        in_specs=[rc, full(abr.shape), full(abi.shape), full(pw_r.shape), full(pw_i.shape), full(bp_r.shape), full(bp_i.shape),
                  full(cp_r.shape), full(cp_i.shape), full(dvec.shape)],
        out_specs=[rn, rn, rc, rc],
        out_shape=[sd((s, n), F32), sd((s, n), F32), sd((s, c), F32), sd((s, c), BF16)],
        scratch_shapes=[pltpu.VMEM((SUBLANES, n), F32), pltpu.VMEM((SUBLANES, n), F32)],
        args=(u, abr, abi, pw_r, pw_i, bp_r, bp_i, cp_r, cp_i, dvec))


def _s5_bwd(dgy, ypre, u, hr, hi, abr, abi, pw_r, pw_i, bp_r, bp_i, cp_r, cp_i, dvec, ts=128, side=None):
    s, c = u.shape
    n = abr.shape[1]
    nblk, cb, nb = bp_r.shape
    ts = min(ts, s)
    nt = s // ts
    hb = ts // SUBLANES
    tn_dims = (((0,), (0,)), ((), ()))
    nt_dims = (((1,), (1,)), ((), ()))

    def body(dgy_ref, yp_ref, u_ref, hr_ref, hi_ref, hrh_ref, hih_ref, ar_ref, ai_ref, pr_ref, pi_ref, bpr_ref, bpi_ref,
             cpr_ref, cpi_ref, d_ref,
             du_ref, dar_ref, dai_ref, dbr_ref, dbi_ref, dcr_ref, dci_ref, dd_ref, car_r, car_i, npi_ref):
        i = pl.program_id(0)
        time_first = i == nt - 1

        @pl.when(i == 0)
        def _():
            car_r[...] = jnp.zeros_like(car_r)
            car_i[...] = jnp.zeros_like(car_i)
            npi_ref[...] = -pi_ref[...]
            for ref in (dar_ref, dai_ref, dbr_ref, dbi_ref, dcr_ref, dci_ref, dd_ref):
                ref[...] = jnp.zeros_like(ref)

        uv = u_ref[...]
        _, dgel = _gelu_and_grad(yp_ref[...])
        dyv = dgy_ref[...] * dgel
        dd_ref[...] += jnp.sum(dyv * uv, axis=0, keepdims=True)
        dyb = dyv.astype(BF16)
        hr, hi = hr_ref[...], hi_ref[...]
        hrb, hib = hr.astype(BF16), hi.astype(BF16)
        dhr, dhi = [], []
        for k in range(nblk):
            dblk = dyb[:, k * cb:(k + 1) * cb]
            dhr.append(lax.dot_general(dblk, cpr_ref[k], nt_dims, preferred_element_type=F32))
            dhi.append(-lax.dot_general(dblk, cpi_ref[k], nt_dims, preferred_element_type=F32))
            dcr_ref[k] += lax.dot_general(hrb[:, k * nb:(k + 1) * nb], dblk, tn_dims, preferred_element_type=F32)
            dci_ref[k] += lax.dot_general(hib[:, k * nb:(k + 1) * nb], dblk, tn_dims, preferred_element_type=F32)
        dhr = jnp.concatenate(dhr, axis=1)
        dhi = jnp.concatenate(dhi, axis=1)
        ar, ai = ar_ref[...], ai_ref[...]
        nr, ni = car_r[0:1, :], car_i[0:1, :]
        row = _rows(dhr.shape)
        dhr = dhr + jnp.where(row == ts - 1, ar * nr + ai * ni, 0.0)
        dhi = dhi + jnp.where(row == ts - 1, ar * ni - ai * nr, 0.0)
        lr, li = _scan_cplx(dhr, dhi, pr_ref, npi_ref, reverse=True)
        car_r[...] = lr[:SUBLANES]
        car_i[...] = li[:SUBLANES]
        hpr = _shift_down(hr, jnp.where(time_first, 0.0, hrh_ref[...]), 1)
        hpi = _shift_down(hi, jnp.where(time_first, 0.0, hih_ref[...]), 1)
        dar_ref[...] += jnp.sum(lr * hpr + li * hpi, axis=0, keepdims=True)
        dai_ref[...] += jnp.sum(li * hpr - lr * hpi, axis=0, keepdims=True)
        lrb, lib = lr.astype(BF16), li.astype(BF16)
        ub = uv.astype(BF16)
        du = []
        for k in range(nblk):
            ublk = ub[:, k * cb:(k + 1) * cb]
            lrk, lik = lrb[:, k * nb:(k + 1) * nb], lib[:, k * nb:(k + 1) * nb]
            dbr_ref[k] += lax.dot_general(ublk, lrk, tn_dims, preferred_element_type=F32)
            dbi_ref[k] += lax.dot_general(ublk, lik, tn_dims, preferred_element_type=F32)
            du.append(lax.dot_general(lrk, bpr_ref[k], nt_dims, preferred_element_type=F32)
                      + lax.dot_general(lik, bpi_ref[k], nt_dims, preferred_element_type=F32))
        du_ref[...] = (d_ref[...] * dyv + jnp.concatenate(du, axis=1)).astype(BF16)

    full = lambda shape: pl.BlockSpec(shape, lambda i: (0,) * len(shape))
    rev = lambda i: nt - 1 - i
    halo_idx = lambda i: jnp.maximum(rev(i) * hb - 1, 0)
    rc = pl.BlockSpec((ts, c), lambda i: (rev(i), 0))
    rn = pl.BlockSpec((ts, n), lambda i: (rev(i), 0))
    hn = pl.BlockSpec((SUBLANES, n), lambda i: (halo_idx(i), 0))
    sd = jax.ShapeDtypeStruct
    return _call_with_side(
        body, side, lambda: pl.program_id(0) == 0, lambda: pl.program_id(0) == nt - 1,
        name="s5_bwd", grid=(nt,),
        in_specs=[rc, rc, rc, rn, rn, hn, hn, full(abr.shape), full(abi.shape), full(pw_r.shape), full(pw_i.shape),
                  full(bp_r.shape), full(bp_i.shape), full(cp_r.shape), full(cp_i.shape), full(dvec.shape)],
        out_specs=[rc, full(abr.shape), full(abi.shape), full(bp_r.shape), full(bp_i.shape), full(cp_r.shape), full(cp_i.shape),
                   full(dvec.shape)],
        out_shape=[sd((s, c), BF16), sd(abr.shape, F32), sd(abi.shape, F32), sd(bp_r.shape, F32), sd(bp_i.shape, F32),
                   sd(cp_r.shape, F32), sd(cp_i.shape, F32), sd(dvec.shape, F32)],
        scratch_shapes=[pltpu.VMEM((SUBLANES, n), F32), pltpu.VMEM((SUBLANES, n), F32), pltpu.VMEM(pw_i.shape, F32)],
        args=(dgy, ypre, u, hr, hi, hr, hi, abr, abi, pw_r, pw_i, bp_r, bp_i, cp_r, cp_i, dvec))


S5_LANE_CHUNK = 512


def _s5_tables(a_re, a_im, log_dt, bt_re, bt_im):
    g, p = a_re.shape
    gc = bt_re.shape[0]

    def body(ar_ref, ai_ref, dt_ref, br_ref, bi_ref, abr_ref, abi_ref, tr_ref, ti_ref, bbr_ref, bbi_ref):
        abr, abi, bbr, bbi = _s5_param_fn(ar_ref[...], ai_ref[...], dt_ref[...], br_ref[...], bi_ref[...])
        abr_ref[...] = abr
        abi_ref[...] = abi
        bbr_ref[...] = bbr
        bbi_ref[...] = bbi
        pows = [(abr, abi)]
        for _ in range(1, SUBLANES):
            qr, qi = pows[-1]
            pows.append((qr * abr - qi * abi, qr * abi + qi * abr))
        zero = jnp.zeros_like(abr)
        for r in range(SUBLANES):
            for k in range(3):
                sh = 1 << k
                tr_ref[k, r] = pows[sh - 1][0] if r >= sh else zero
                ti_ref[k, r] = pows[sh - 1][1] if r >= sh else zero
            tr_ref[3, r] = pows[r][0]
            ti_ref[3, r] = pows[r][1]

    sd = jax.ShapeDtypeStruct
    return pl.pallas_call(
        body, name="s5_tables",
        out_shape=[sd((g, p), F32), sd((g, p), F32), sd((4, SUBLANES, g, p), F32), sd((4, SUBLANES, g, p), F32),
                   sd((gc, g, p), F32), sd((gc, g, p), F32)],
    )(a_re, a_im, log_dt, bt_re, bt_im)


def _cmul_add(br, bi, tr, ti, sr, si):
    return br + tr * sr - ti * si, bi + tr * si + ti * sr


def _s5_fwd2(u, tab_r, tab_i, bp_r, bp_i, cp_r, cp_i, dvec, ts=256, side=None):
    s, c = u.shape
    n = tab_r.shape[2]
    nblk, cb, nb = bp_r.shape
    ts = min(ts, s)
    nsl = ts // SUBLANES
    lc = min(S5_LANE_CHUNK, n)

    def body(u_ref, tr_ref, ti_ref, bpr_ref, bpi_ref, cpr_ref, cpi_ref, d_ref, hr_ref, hi_ref, yp_ref, gy_ref,
             bur_ref, bui_ref, car_r, car_i):
        i = pl.program_id(0)

        @pl.when(i == 0)
        def _():
            car_r[...] = jnp.zeros_like(car_r)
            car_i[...] = jnp.zeros_like(car_i)

        uv = u_ref[...]
        ub = uv.astype(BF16)
        for k in range(nblk):
            bur_ref[:, k * nb:(k + 1) * nb] = jnp.dot(ub[:, k * cb:(k + 1) * cb], bpr_ref[k], preferred_element_type=F32)
            bui_ref[:, k * nb:(k + 1) * nb] = jnp.dot(ub[:, k * cb:(k + 1) * cb], bpi_ref[k], preferred_element_type=F32)
        for q in range(n // lc):
            sl = slice(q * lc, (q + 1) * lc)
            tabs = [(tr_ref[k, :, sl], ti_ref[k, :, sl]) for k in range(4)]

            def slab(j, carry, sl=sl, tabs=tabs):
                cr, ci = carry
                r0 = pl.multiple_of(j * SUBLANES, SUBLANES)
                br, bi = bur_ref[pl.ds(r0, SUBLANES), sl], bui_ref[pl.ds(r0, SUBLANES), sl]
                for k in range(3):
                    sh = 1 << k
                    br, bi = _cmul_add(br, bi, tabs[k][0], tabs[k][1], pltpu.roll(br, sh, 0), pltpu.roll(bi, sh, 0))
                hr, hi = _cmul_add(br, bi, tabs[3][0], tabs[3][1], jnp.broadcast_to(cr, br.shape), jnp.broadcast_to(ci, bi.shape))
                hr_ref[pl.ds(r0, SUBLANES), sl] = hr
                hi_ref[pl.ds(r0, SUBLANES), sl] = hi
                return hr[SUBLANES - 1:, :], hi[SUBLANES - 1:, :]

            cr, ci = lax.fori_loop(0, nsl, slab, (car_r[:, sl], car_i[:, sl]), unroll=2)
            car_r[:, sl] = cr
            car_i[:, sl] = ci
        hrb, hib = hr_ref[...].astype(BF16), hi_ref[...].astype(BF16)
        y = jnp.concatenate([jnp.dot(hrb[:, k * nb:(k + 1) * nb], cpr_ref[k], preferred_element_type=F32)
                             - jnp.dot(hib[:, k * nb:(k + 1) * nb], cpi_ref[k], preferred_element_type=F32) for k in range(nblk)], axis=1)
        yp = y + d_ref[...] * uv
        yp_ref[...] = yp
        gy_ref[...] = _gelu(yp).astype(BF16)

    full = lambda shape: pl.BlockSpec(shape, lambda i: (0,) * len(shape))
    rc = pl.BlockSpec((ts, c), lambda i: (i, 0))
    rn = pl.BlockSpec((ts, n), lambda i: (i, 0))
    sd = jax.ShapeDtypeStruct
    nt = s // ts
    return _call_with_side(
        body, side, lambda: pl.program_id(0) == 0, lambda: pl.program_id(0) == nt - 1,
        name="s5_fwd", grid=(nt,),
        in_specs=[rc, full(tab_r.shape), full(tab_i.shape), full(bp_r.shape), full(bp_i.shape), full(cp_r.shape), full(cp_i.shape),
                  full(dvec.shape)],
        out_specs=[rn, rn, rc, rc],
        out_shape=[sd((s, n), F32), sd((s, n), F32), sd((s, c), F32), sd((s, c), BF16)],
        scratch_shapes=[pltpu.VMEM((ts, n), F32), pltpu.VMEM((ts, n), F32), pltpu.VMEM((1, n), F32), pltpu.VMEM((1, n), F32)],
        args=(u, tab_r, tab_i, bp_r, bp_i, cp_r, cp_i, dvec))


def _s5_bwd2(dgy, ypre, u, hr, hi, rtab_r, rtab_i, bp_r, bp_i, cp_r, cp_i, dvec, ts=256, side=None):
    s, c = u.shape
    n = rtab_r.shape[2]
    nblk, cb, nb = bp_r.shape
    ts = min(ts, s)
    nt = s // ts
    hb = ts // SUBLANES
    nsl = ts // SUBLANES
    lc = min(S5_LANE_CHUNK, n)
    tn_dims = (((0,), (0,)), ((), ()))
    nt_dims = (((1,), (1,)), ((), ()))

    def body(dgy_ref, yp_ref, u_ref, hr_ref, hi_ref, hrh_ref, hih_ref, tr_ref, ti_ref, bpr_ref, bpi_ref, cpr_ref, cpi_ref, d_ref,
             du_ref, dar_ref, dai_ref, dbr_ref, dbi_ref, dcr_ref, dci_ref, dd_ref, lr_ref, li_ref, car_r, car_i):
        i = pl.program_id(0)
        time_first = i == nt - 1

        @pl.when(i == 0)
        def _():
            car_r[...] = jnp.zeros_like(car_r)
            car_i[...] = jnp.zeros_like(car_i)
            for ref in (dar_ref, dai_ref, dbr_ref, dbi_ref, dcr_ref, dci_ref, dd_ref):
                ref[...] = jnp.zeros_like(ref)

        uv = u_ref[...]
        _, dgel = _gelu_and_grad(yp_ref[...])
        dyv = dgy_ref[...] * dgel
        dd_ref[...] += jnp.sum(dyv * uv, axis=0, keepdims=True)
        dyb = dyv.astype(BF16)
        hrb, hib = hr_ref[...].astype(BF16), hi_ref[...].astype(BF16)
        for k in range(nblk):
            dblk = dyb[:, k * cb:(k + 1) * cb]
            lr_ref[:, k * nb:(k + 1) * nb] = lax.dot_general(dblk, cpr_ref[k], nt_dims, preferred_element_type=F32)
            li_ref[:, k * nb:(k + 1) * nb] = -lax.dot_general(dblk, cpi_ref[k], nt_dims, preferred_element_type=F32)
            dcr_ref[k] += lax.dot_general(hrb[:, k * nb:(k + 1) * nb], dblk, tn_dims, preferred_element_type=F32)
            dci_ref[k] += lax.dot_general(hib[:, k * nb:(k + 1) * nb], dblk, tn_dims, preferred_element_type=F32)
        row8 = _rows((SUBLANES, lc))
        for q in range(n // lc):
            sl = slice(q * lc, (q + 1) * lc)
            tabs = [(tr_ref[k, :, sl], ti_ref[k, :, sl]) for k in range(4)]
            halo_r = jnp.where(time_first, 0.0, hrh_ref[SUBLANES - 1:, sl])
            halo_i = jnp.where(time_first, 0.0, hih_ref[SUBLANES - 1:, sl])

            def slab(jj, carry, sl=sl, tabs=tabs, halo_r=halo_r, halo_i=halo_i):
                nr, ni, acc_r, acc_i = carry
                j = nsl - 1 - jj
                r0 = pl.multiple_of(j * SUBLANES, SUBLANES)
                br, bi = lr_ref[pl.ds(r0, SUBLANES), sl], li_ref[pl.ds(r0, SUBLANES), sl]
                for k in range(3):
                    sh = 1 << k
                    br, bi = _cmul_add(br, bi, tabs[k][0], tabs[k][1], pltpu.roll(br, SUBLANES - sh, 0),
                                       pltpu.roll(bi, SUBLANES - sh, 0))
                lr, li = _cmul_add(br, bi, tabs[3][0], tabs[3][1], jnp.broadcast_to(nr, br.shape), jnp.broadcast_to(ni, bi.shape))
                lr_ref[pl.ds(r0, SUBLANES), sl] = lr
                li_ref[pl.ds(r0, SUBLANES), sl] = li
                p0 = pl.multiple_of(jnp.maximum(j - 1, 0) * SUBLANES, SUBLANES)
                prev_r = jnp.where(j == 0, halo_r, hr_ref[pl.ds(p0, SUBLANES), sl][SUBLANES - 1:, :])
                prev_i = jnp.where(j == 0, halo_i, hi_ref[pl.ds(p0, SUBLANES), sl][SUBLANES - 1:, :])
                hpr = jnp.where(row8 == 0, jnp.broadcast_to(prev_r, br.shape), pltpu.roll(hr_ref[pl.ds(r0, SUBLANES), sl], 1, 0))
                hpi = jnp.where(row8 == 0, jnp.broadcast_to(prev_i, bi.shape), pltpu.roll(hi_ref[pl.ds(r0, SUBLANES), sl], 1, 0))
                return lr[:1, :], li[:1, :], acc_r + (lr * hpr + li * hpi), acc_i + (li * hpr - lr * hpi)

            zero = jnp.zeros((SUBLANES, lc), F32)
            nr, ni, acc_r, acc_i = lax.fori_loop(0, nsl, slab, (car_r[:, sl], car_i[:, sl], zero, zero), unroll=2)
            car_r[:, sl] = nr
            car_i[:, sl] = ni
            dar_ref[:, sl] += jnp.sum(acc_r, axis=0, keepdims=True)
            dai_ref[:, sl] += jnp.sum(acc_i, axis=0, keepdims=True)
        lrb, lib = lr_ref[...].astype(BF16), li_ref[...].astype(BF16)
        ub = uv.astype(BF16)
        du = []
        for k in range(nblk):
            ublk = ub[:, k * cb:(k + 1) * cb]
            lrk, lik = lrb[:, k * nb:(k + 1) * nb], lib[:, k * nb:(k + 1) * nb]
            dbr_ref[k] += lax.dot_general(ublk, lrk, tn_dims, preferred_element_type=F32)
            dbi_ref[k] += lax.dot_general(ublk, lik, tn_dims, preferred_element_type=F32)
            du.append(lax.dot_general(lrk, bpr_ref[k], nt_dims, preferred_element_type=F32)
                      + lax.dot_general(lik, bpi_ref[k], nt_dims, preferred_element_type=F32))
        du_ref[...] = (d_ref[...] * dyv + jnp.concatenate(du, axis=1)).astype(BF16)

    full = lambda shape: pl.BlockSpec(shape, lambda i: (0,) * len(shape))
    rev = lambda i: nt - 1 - i
    halo_idx = lambda i: jnp.maximum(rev(i) * hb - 1, 0)
    rc = pl.BlockSpec((ts, c), lambda i: (rev(i), 0))
    rn = pl.BlockSpec((ts, n), lambda i: (rev(i), 0))
    hn = pl.BlockSpec((SUBLANES, n), lambda i: (halo_idx(i), 0))
    sd = jax.ShapeDtypeStruct
    vec_n = (1, n)
    return _call_with_side(
        body, side, lambda: pl.program_id(0) == 0, lambda: pl.program_id(0) == nt - 1,
        name="s5_bwd", grid=(nt,),
        in_specs=[rc, rc, rc, rn, rn, hn, hn, full(rtab_r.shape), full(rtab_i.shape),
                  full(bp_r.shape), full(bp_i.shape), full(cp_r.shape), full(cp_i.shape), full(dvec.shape)],
        out_specs=[rc, full(vec_n), full(vec_n), full(bp_r.shape), full(bp_i.shape), full(cp_r.shape), full(cp_i.shape),
                   full(dvec.shape)],
        out_shape=[sd((s, c), BF16), sd(vec_n, F32), sd(vec_n, F32), sd(bp_r.shape, F32), sd(bp_i.shape, F32),
                   sd(cp_r.shape, F32), sd(cp_i.shape, F32), sd(dvec.shape, F32)],
        scratch_shapes=[pltpu.VMEM((ts, n), F32), pltpu.VMEM((ts, n), F32), pltpu.VMEM((1, n), F32), pltpu.VMEM((1, n), F32)],
        args=(dgy, ypre, u, hr, hi, hr, hi, rtab_r, rtab_i, bp_r, bp_i, cp_r, cp_i, dvec))


def _s5_tables3(a_re, a_im, log_dt, bt_re, bt_im, seg):
    g, p = a_re.shape
    gc = bt_re.shape[0]
    nsq = int(math.log2(seg))
    assert 1 << nsq == seg

    def body(ar_ref, ai_ref, dt_ref, br_ref, bi_ref, tr_ref, ti_ref, rtr_ref, rti_ref, bbr_ref, bbi_ref):
        abr, abi, bbr, bbi = _s5_param_fn(ar_ref[...], ai_ref[...], dt_ref[...], br_ref[...], bi_ref[...])
        bbr_ref[...] = bbr
        bbi_ref[...] = bbi
        qr, qi = abr, abi
        for _ in range(nsq):
            qr, qi = qr * qr - qi * qi, 2.0 * qr * qi
        pows = [(qr, qi)]
        for _ in range(1, SUBLANES):
            cr, ci = pows[-1]
            pows.append((cr * qr - ci * qi, cr * qi + ci * qr))
        zero = jnp.zeros_like(abr)
        for r in range(SUBLANES):
            rows = [(pows[(1 << k) - 1] if r >= (1 << k) else (zero, zero)) for k in range(3)] + [pows[r], (abr, abi)]
            for k, (vr, vi) in enumerate(rows):
                tr_ref[k, r] = vr
                ti_ref[k, r] = vi
                rtr_ref[k, SUBLANES - 1 - r] = vr
                rti_ref[k, SUBLANES - 1 - r] = -vi

    sd = jax.ShapeDtypeStruct
    tab = sd((5, SUBLANES, g, p), F32)
    return pl.pallas_call(
        body, name="s5_tables", out_shape=[tab, tab, tab, tab, sd((gc, g, p), F32), sd((gc, g, p), F32)],
    )(a_re, a_im, log_dt, bt_re, bt_im)


def _segment_perm(ts):
    seg = ts // SUBLANES
    rho = jnp.arange(ts)
    src = (rho % SUBLANES) * seg + rho // SUBLANES
    return (src[:, None] == jnp.arange(ts)[None, :]).astype(BF16)


def _exact_rows(perm_t, x):
    hi = x.astype(BF16)
    r1 = x - hi.astype(F32)
    mid = r1.astype(BF16)
    lo = (r1 - mid.astype(F32)).astype(BF16)
    dot = lambda v: jnp.dot(perm_t, v, preferred_element_type=F32)
    return (dot(hi) + dot(mid)) + dot(lo)


def _s5_fwd3(u, perm, perm_t, tab_r, tab_i, bp_r, bp_i, cp_r, cp_i, dvec, ts=256, side=None):
    s, c = u.shape
    n = tab_r.shape[2]
    nblk, cb, nb = bp_r.shape
    ts = min(ts, s)
    seg = ts // SUBLANES
    lc = min(S5_LANE_CHUNK, n)

    def body(u_ref, p_ref, pt_ref, tr_ref, ti_ref, bpr_ref, bpi_ref, cpr_ref, cpi_ref, d_ref, hr_ref, hi_ref, yp_ref, gy_ref,
             bur_ref, bui_ref, car_r, car_i):
        i = pl.program_id(0)

        @pl.when(i == 0)
        def _():
            car_r[...] = jnp.zeros_like(car_r)
            car_i[...] = jnp.zeros_like(car_i)

        uv = u_ref[...]
        ubp = jnp.dot(p_ref[...], uv.astype(BF16), preferred_element_type=F32).astype(BF16)
        for k in range(nblk):
            bur_ref[:, k * nb:(k + 1) * nb] = jnp.dot(ubp[:, k * cb:(k + 1) * cb], bpr_ref[k], preferred_element_type=F32)
            bui_ref[:, k * nb:(k + 1) * nb] = jnp.dot(ubp[:, k * cb:(k + 1) * cb], bpi_ref[k], preferred_element_type=F32)
        row8 = _rows((SUBLANES, lc))
        for q in range(n // lc):
            sl = slice(q * lc, (q + 1) * lc)
            tabs = [(tr_ref[k, :, sl], ti_ref[k, :, sl]) for k in range(5)]
            a_r, a_i = tabs[4]

            def local(r, carry, sl=sl, a_r=a_r, a_i=a_i):
                r0 = pl.multiple_of(r * SUBLANES, SUBLANES)
                hr, hi = _cmul_add(bur_ref[pl.ds(r0, SUBLANES), sl], bui_ref[pl.ds(r0, SUBLANES), sl], a_r, a_i, carry[0], carry[1])
                hr_ref[pl.ds(r0, SUBLANES), sl] = hr
                hi_ref[pl.ds(r0, SUBLANES), sl] = hi
                return hr, hi

            zero = jnp.zeros((SUBLANES, lc), F32)
            er, ei = lax.fori_loop(0, seg, local, (zero, zero), unroll=4)
            for k in range(3):
                sh = 1 << k
                er, ei = _cmul_add(er, ei, tabs[k][0], tabs[k][1], pltpu.roll(er, sh, 0), pltpu.roll(ei, sh, 0))
            cin_r, cin_i = jnp.broadcast_to(car_r[:, sl], er.shape), jnp.broadcast_to(car_i[:, sl], ei.shape)
            er, ei = _cmul_add(er, ei, tabs[3][0], tabs[3][1], cin_r, cin_i)
            car_r[:, sl] = er[SUBLANES - 1:, :]
            car_i[:, sl] = ei[SUBLANES - 1:, :]
            c_r = jnp.where(row8 == 0, cin_r, pltpu.roll(er, 1, 0))
            c_i = jnp.where(row8 == 0, cin_i, pltpu.roll(ei, 1, 0))

            def fix(r, carry, sl=sl, a_r=a_r, a_i=a_i, c_r=c_r, c_i=c_i):
                pr, pi = carry
                r0 = pl.multiple_of(r * SUBLANES, SUBLANES)
                hr, hi = _cmul_add(hr_ref[pl.ds(r0, SUBLANES), sl], hi_ref[pl.ds(r0, SUBLANES), sl], pr, pi, c_r, c_i)
                hr_ref[pl.ds(r0, SUBLANES), sl] = hr
                hi_ref[pl.ds(r0, SUBLANES), sl] = hi
                return pr * a_r - pi * a_i, pr * a_i + pi * a_r

            lax.fori_loop(0, seg, fix, (a_r, a_i), unroll=4)
        hrb, hib = hr_ref[...].astype(BF16), hi_ref[...].astype(BF16)
        y = jnp.concatenate([jnp.dot(hrb[:, k * nb:(k + 1) * nb], cpr_ref[k], preferred_element_type=F32)
                             - jnp.dot(hib[:, k * nb:(k + 1) * nb], cpi_ref[k], preferred_element_type=F32) for k in range(nblk)], axis=1)
        yp = _exact_rows(pt_ref[...], y) + d_ref[...] * uv
        yp_ref[...] = yp
        gy_ref[...] = _gelu(yp).astype(BF16)

    full = lambda shape: pl.BlockSpec(shape, lambda i: (0,) * len(shape))
    rc = pl.BlockSpec((ts, c), lambda i: (i, 0))
    rn = pl.BlockSpec((ts, n), lambda i: (i, 0))
    sd = jax.ShapeDtypeStruct
    nt = s // ts
    return _call_with_side(
        body, side, lambda: pl.program_id(0) == 0, lambda: pl.program_id(0) == nt - 1,
        name="s5_fwd", grid=(nt,),
        in_specs=[rc, full(perm.shape), full(perm_t.shape), full(tab_r.shape), full(tab_i.shape), full(bp_r.shape), full(bp_i.shape),
                  full(cp_r.shape), full(cp_i.shape), full(dvec.shape)],
        out_specs=[rn, rn, rc, rc],
        out_shape=[sd((s, n), F32), sd((s, n), F32), sd((s, c), F32), sd((s, c), BF16)],
        scratch_shapes=[pltpu.VMEM((ts, n), F32), pltpu.VMEM((ts, n), F32), pltpu.VMEM((1, n), F32), pltpu.VMEM((1, n), F32)],
        args=(u, perm, perm_t, tab_r, tab_i, bp_r, bp_i, cp_r, cp_i, dvec))


def _s5_bwd3(dgy, ypre, u, hr, hi, perm, perm_t, rtab_r, rtab_i, bp_r, bp_i, cp_r, cp_i, dvec, ts=256, side=None):
    s, c = u.shape
    n = rtab_r.shape[2]
    nblk, cb, nb = bp_r.shape
    ts = min(ts, s)
    nt = s // ts
    hb = ts // SUBLANES
    seg = ts // SUBLANES
    lc = min(S5_LANE_CHUNK, n)
    tn_dims = (((0,), (0,)), ((), ()))
    nt_dims = (((1,), (1,)), ((), ()))

    def body(dgy_ref, yp_ref, u_ref, hr_ref, hi_ref, hrh_ref, hih_ref, p_ref, pt_ref, tr_ref, ti_ref, bpr_ref, bpi_ref,
             cpr_ref, cpi_ref, d_ref, du_ref, dar_ref, dai_ref, dbr_ref, dbi_ref, dcr_ref, dci_ref, dd_ref, lr_ref, li_ref,
             car_r, car_i):
        i = pl.program_id(0)
        time_first = i == nt - 1

        @pl.when(i == 0)
        def _():
            car_r[...] = jnp.zeros_like(car_r)
            car_i[...] = jnp.zeros_like(car_i)
            for ref in (dar_ref, dai_ref, dbr_ref, dbi_ref, dcr_ref, dci_ref, dd_ref):
                ref[...] = jnp.zeros_like(ref)

        uv = u_ref[...]
        _, dgel = _gelu_and_grad(yp_ref[...])
        dyv = dgy_ref[...] * dgel
        dd_ref[...] += jnp.sum(dyv * uv, axis=0, keepdims=True)
        perm_m = p_ref[...]
        dyb = jnp.dot(perm_m, dyv.astype(BF16), preferred_element_type=F32).astype(BF16)
        ub = jnp.dot(perm_m, uv.astype(BF16), preferred_element_type=F32).astype(BF16)
        hrb, hib = hr_ref[...].astype(BF16), hi_ref[...].astype(BF16)
        for k in range(nblk):
            dblk = dyb[:, k * cb:(k + 1) * cb]
            lr_ref[:, k * nb:(k + 1) * nb] = lax.dot_general(dblk, cpr_ref[k], nt_dims, preferred_element_type=F32)
            li_ref[:, k * nb:(k + 1) * nb] = -lax.dot_general(dblk, cpi_ref[k], nt_dims, preferred_element_type=F32)
            dcr_ref[k] += lax.dot_general(hrb[:, k * nb:(k + 1) * nb], dblk, tn_dims, preferred_element_type=F32)
            dci_ref[k] += lax.dot_general(hib[:, k * nb:(k + 1) * nb], dblk, tn_dims, preferred_element_type=F32)
        row8 = _rows((SUBLANES, lc))
        last0 = (seg - 1) * SUBLANES
        for q in range(n // lc):
            sl = slice(q * lc, (q + 1) * lc)
            tabs = [(tr_ref[k, :, sl], ti_ref[k, :, sl]) for k in range(5)]
            a_r, a_i = tabs[4]

            def local(rr, carry, sl=sl, a_r=a_r, a_i=a_i):
                r0 = pl.multiple_of((seg - 1 - rr) * SUBLANES, SUBLANES)
                lr, li = _cmul_add(lr_ref[pl.ds(r0, SUBLANES), sl], li_ref[pl.ds(r0, SUBLANES), sl], a_r, a_i, carry[0], carry[1])
                lr_ref[pl.ds(r0, SUBLANES), sl] = lr
                li_ref[pl.ds(r0, SUBLANES), sl] = li
                return lr, li

            zero = jnp.zeros((SUBLANES, lc), F32)
            er, ei = lax.fori_loop(0, seg, local, (zero, zero), unroll=4)
            for k in range(3):
                sh = 1 << k
                er, ei = _cmul_add(er, ei, tabs[k][0], tabs[k][1], pltpu.roll(er, SUBLANES - sh, 0), pltpu.roll(ei, SUBLANES - sh, 0))
            cin_r, cin_i = jnp.broadcast_to(car_r[:, sl], er.shape), jnp.broadcast_to(car_i[:, sl], ei.shape)
            er, ei = _cmul_add(er, ei, tabs[3][0], tabs[3][1], cin_r, cin_i)
            car_r[:, sl] = er[:1, :]
            car_i[:, sl] = ei[:1, :]
            c_r = jnp.where(row8 == SUBLANES - 1, cin_r, pltpu.roll(er, SUBLANES - 1, 0))
            c_i = jnp.where(row8 == SUBLANES - 1, cin_i, pltpu.roll(ei, SUBLANES - 1, 0))
            halo_r = jnp.where(time_first, 0.0, hrh_ref[SUBLANES - 1:, sl])
            halo_i = jnp.where(time_first, 0.0, hih_ref[SUBLANES - 1:, sl])
            hp0_r = jnp.where(row8 == 0, jnp.broadcast_to(halo_r, zero.shape), pltpu.roll(hr_ref[pl.ds(last0, SUBLANES), sl], 1, 0))
            hp0_i = jnp.where(row8 == 0, jnp.broadcast_to(halo_i, zero.shape), pltpu.roll(hi_ref[pl.ds(last0, SUBLANES), sl], 1, 0))

            def fix(rr, carry, sl=sl, a_r=a_r, a_i=a_i, c_r=c_r, c_i=c_i, hp0_r=hp0_r, hp0_i=hp0_i):
                pr, pi, acc_r, acc_i = carry
                r = seg - 1 - rr
                r0 = pl.multiple_of(r * SUBLANES, SUBLANES)
                lr, li = _cmul_add(lr_ref[pl.ds(r0, SUBLANES), sl], li_ref[pl.ds(r0, SUBLANES), sl], pr, pi, c_r, c_i)
                lr_ref[pl.ds(r0, SUBLANES), sl] = lr
                li_ref[pl.ds(r0, SUBLANES), sl] = li
                p0 = pl.multiple_of(jnp.maximum(r - 1, 0) * SUBLANES, SUBLANES)
                hpr = jnp.where(r == 0, hp0_r, hr_ref[pl.ds(p0, SUBLANES), sl])
                hpi = jnp.where(r == 0, hp0_i, hi_ref[pl.ds(p0, SUBLANES), sl])
                return (pr * a_r - pi * a_i, pr * a_i + pi * a_r, acc_r + (lr * hpr + li * hpi), acc_i + (li * hpr - lr * hpi))

            _, _, acc_r, acc_i = lax.fori_loop(0, seg, fix, (a_r, a_i, zero, zero), unroll=4)
            dar_ref[:, sl] += jnp.sum(acc_r, axis=0, keepdims=True)
            dai_ref[:, sl] += jnp.sum(acc_i, axis=0, keepdims=True)
        lrb, lib = lr_ref[...].astype(BF16), li_ref[...].astype(BF16)
        du = []
        for k in range(nblk):
            ublk = ub[:, k * cb:(k + 1) * cb]
            lrk, lik = lrb[:, k * nb:(k + 1) * nb], lib[:, k * nb:(k + 1) * nb]
            dbr_ref[k] += lax.dot_general(ublk, lrk, tn_dims, preferred_element_type=F32)
            dbi_ref[k] += lax.dot_general(ublk, lik, tn_dims, preferred_element_type=F32)
            du.append(lax.dot_general(lrk, bpr_ref[k], nt_dims, preferred_element_type=F32)
                      + lax.dot_general(lik, bpi_ref[k], nt_dims, preferred_element_type=F32))
        du_ref[...] = (d_ref[...] * dyv + _exact_rows(pt_ref[...], jnp.concatenate(du, axis=1))).astype(BF16)

    full = lambda shape: pl.BlockSpec(shape, lambda i: (0,) * len(shape))
    rev = lambda i: nt - 1 - i
    halo_idx = lambda i: jnp.maximum(rev(i) * hb - 1, 0)
    rc = pl.BlockSpec((ts, c), lambda i: (rev(i), 0))
    rn = pl.BlockSpec((ts, n), lambda i: (rev(i), 0))
    hn = pl.BlockSpec((SUBLANES, n), lambda i: (halo_idx(i), 0))
    sd = jax.ShapeDtypeStruct
    vec_n = (1, n)
    return _call_with_side(
        body, side, lambda: pl.program_id(0) == 0, lambda: pl.program_id(0) == nt - 1,
        name="s5_bwd", grid=(nt,),
        in_specs=[rc, rc, rc, rn, rn, hn, hn, full(perm.shape), full(perm_t.shape), full(rtab_r.shape), full(rtab_i.shape),
                  full(bp_r.shape), full(bp_i.shape), full(cp_r.shape), full(cp_i.shape), full(dvec.shape)],
        out_specs=[rc, full(vec_n), full(vec_n), full(bp_r.shape), full(bp_i.shape), full(cp_r.shape), full(cp_i.shape),
                   full(dvec.shape)],
        out_shape=[sd((s, c), BF16), sd(vec_n, F32), sd(vec_n, F32), sd(bp_r.shape, F32), sd(bp_i.shape, F32),
                   sd(cp_r.shape, F32), sd(cp_i.shape, F32), sd(dvec.shape, F32)],
        scratch_shapes=[pltpu.VMEM((ts, n), F32), pltpu.VMEM((ts, n), F32), pltpu.VMEM((1, n), F32), pltpu.VMEM((1, n), F32)],
        args=(dgy, ypre, u, hr, hi, hr, hi, perm, perm_t, rtab_r, rtab_i, bp_r, bp_i, cp_r, cp_i, dvec))


def _glu(gl2, ts=512):
    _, s, c = gl2.shape
    ts = min(ts, s)

    def body(g_ref, o_ref):
        o_ref[...] = (g_ref[0] * _sigmoid(g_ref[1])).astype(BF16)

    return pl.pallas_call(
        body, name="glu", grid=(s // ts,), in_specs=[pl.BlockSpec((2, ts, c), lambda i: (0, i, 0))],
        out_specs=pl.BlockSpec((ts, c), lambda i: (i, 0)), out_shape=jax.ShapeDtypeStruct((s, c), BF16), compiler_params=_cparams(),
    )(gl2)


def _glu_bwd(gl2, d_o, ts=512):
    _, s, c = gl2.shape
    ts = min(ts, s)

    def body(g_ref, do_ref, o_ref):
        sg = _sigmoid(g_ref[1])
        dov = do_ref[...]
        o_ref[0] = (dov * sg).astype(BF16)
        o_ref[1] = (dov * g_ref[0] * sg * (1.0 - sg)).astype(BF16)

    blk = pl.BlockSpec((2, ts, c), lambda i: (0, i, 0))
    return pl.pallas_call(
        body, name="glu_bwd", grid=(s // ts,), in_specs=[blk, pl.BlockSpec((ts, c), lambda i: (i, 0))],
        out_specs=blk, out_shape=jax.ShapeDtypeStruct((2, s, c), BF16), compiler_params=_cparams(),
    )(gl2, d_o)


PACK_ROW_MULTIPLE = 1024
ELEMENTWISE_BLOCK_ELEMS = 256 * 1024


def _row_tile(rows, cols):
    pref = max(SUBLANES, 1 << int(math.log2(max(1, ELEMENTWISE_BLOCK_ELEMS // cols))))
    if rows <= pref:
        return rows
    t = pref
    while rows % t:
        t //= 2
    assert t >= SUBLANES, rows
    return t


def _sum_parts(rs):
    nl = len(rs)
    p, rows, cols = rs[0].shape
    tr = _row_tile(rows, cols)

    def body(*refs):
        o_ref = refs[nl]
        for l in range(nl):
            acc = refs[l][0].astype(F32)
            for k in range(1, p):
                acc = acc + refs[l][k].astype(F32)
            o_ref[l] = acc

    return pl.pallas_call(
        body, name="sum_parts", grid=(rows // tr,), in_specs=[pl.BlockSpec((p, tr, cols), lambda i: (0, i, 0))] * nl,
        out_specs=pl.BlockSpec((nl, tr, cols), lambda i: (0, i, 0)), out_shape=jax.ShapeDtypeStruct((nl, rows, cols), F32),
        compiler_params=_cparams(),
    )(*rs)


def _adamw(w, g_parts, m, v):
    rows, cols = w.shape
    tr = _row_tile(rows, cols)
    ng = len(g_parts)
    c1 = 1.0 / (1.0 - ADAM_B1 ** ADAM_STEP)
    c2 = 1.0 / (1.0 - ADAM_B2 ** ADAM_STEP)

    def body(*refs):
        w_ref, m_ref, v_ref = refs[0], refs[1 + ng], refs[2 + ng]
        g_ref, dl_ref, nm_ref, nv_ref = refs[3 + ng:]
        g = refs[1][...]
        for k in range(1, ng):
            g = g + refs[1 + k][...]
        mn = ADAM_B1 * m_ref[...] + (1.0 - ADAM_B1) * g
        vn = ADAM_B2 * v_ref[...] + (1.0 - ADAM_B2) * (g * g)
        g_ref[...] = g
        nm_ref[...] = mn
        nv_ref[...] = vn
        dl_ref[...] = -ADAM_LR * ((mn * c1) / (jnp.sqrt(vn * c2) + ADAM_EPS) + ADAM_WD * w_ref[...])

    blk = pl.BlockSpec((tr, cols), lambda i: (i, 0))
    sd = jax.ShapeDtypeStruct((rows, cols), F32)
    return pl.pallas_call(
        body, name="adamw", grid=(rows // tr,), in_specs=[blk] * (3 + ng), out_specs=[blk] * 4, out_shape=[sd] * 4,
        compiler_params=_cparams(),
    )(w, *g_parts, m, v)


def _place():
    x, y, c = lax.axis_index("x"), lax.axis_index("y"), lax.axis_index("c")
    chips = [(1 - x, y), (x, 1 - y), (1 - x, 1 - y)]
    return x, y, c, chips


class Side:
    def __init__(self, ins, outs, kind, views=None):
        self.ins, self.outs, self.kind = list(ins), list(outs), kind
        n = len(self.ins)
        self.views = views or [None] * n
        self.sems = [pltpu.SemaphoreType.DMA((3 * n,)), pltpu.SemaphoreType.DMA((3 * n,)), pltpu.SemaphoreType.DMA((n,))]

    def _copies(self, ins, outs, send, recv, lsem):
        x, y, c, chips = _place()
        me = 2 * x + y
        local, out_going, in_coming = [], [], []
        for t in range(len(ins)):
            if self.kind == 'gather':
                src_local, srcs, dst_mine = ins[t], [ins[t]] * 3, outs[t].at[me]
            else:
                part = (lambda p, t=t: self.views[t](ins[t], p)) if self.views[t] else (lambda p, t=t: ins[t].at[p])
                src_local, srcs, dst_mine = part(me), [part(2 * px + py) for px, py in chips], outs[t].at[me]
            local.append(pltpu.make_async_copy(src_local, dst_mine, lsem.at[t]))
            for r, (px, py) in enumerate(chips):
                out_going.append(pltpu.make_async_remote_copy(
                    src_ref=srcs[r], dst_ref=dst_mine, send_sem=send.at[3 * t + r], recv_sem=recv.at[3 * t + r],
                    device_id=(px, py, c), device_id_type=MESH))
                in_coming.append(pltpu.make_async_remote_copy(
                    src_ref=srcs[r], dst_ref=outs[t].at[2 * px + py], send_sem=send.at[3 * t + r], recv_sem=recv.at[3 * t + r],
                    device_id=(px, py, c), device_id_type=MESH))
        return local, out_going, in_coming

    def start(self, ins, outs, send, recv, lsem):
        local, out_going, _ = self._copies(ins, outs, send, recv, lsem)
        for cp in local + out_going:
            cp.start()

    def wait(self, ins, outs, send, recv, lsem):
        local, out_going, in_coming = self._copies(ins, outs, send, recv, lsem)
        for cp in in_coming:
            cp.wait_recv()
        for cp in out_going:
            cp.wait_send()
        for cp in local:
            cp.wait()


def _gather_side(shards):
    return Side(shards, [jax.ShapeDtypeStruct((N_CHIPS,) + s.shape, s.dtype) for s in shards], 'gather')


def _scatter_side(grads, shapes, views):
    return Side(grads, [jax.ShapeDtypeStruct(s, g.dtype) for g, s in zip(grads, shapes)], 'scatter', views)


def _halves_view(ref, p):
    half = ref.shape[2] // 2
    return ref.at[p // 2, :, pl.ds((p % 2) * half, half)]


def _call_with_side(body, side, first, last, *, name, grid, in_specs, out_specs, out_shape, scratch_shapes, args):
    if side is None:
        outs = pl.pallas_call(body, name=name, grid=grid, in_specs=in_specs, out_specs=out_specs, out_shape=out_shape,
                              scratch_shapes=scratch_shapes, compiler_params=_cparams())(*args)
        return outs, []
    n_in, n_out, n_sc = len(in_specs), len(out_specs), len(scratch_shapes)
    ns_in, ns_out = len(side.ins), len(side.outs)

    def wrapped(*refs):
        base_in, s_in = refs[:n_in], refs[n_in:n_in + ns_in]
        o0 = n_in + ns_in
        base_out, s_out = refs[o0:o0 + n_out], refs[o0 + n_out:o0 + n_out + ns_out]
        sc0 = o0 + n_out + ns_out
        base_sc, sems = refs[sc0:sc0 + n_sc], refs[sc0 + n_sc:]

        @pl.when(first())
        def _():
            side.start(s_in, s_out, *sems)

        body(*base_in, *base_out, *base_sc)

        @pl.when(last())
        def _():
            side.wait(s_in, s_out, *sems)

    any_spec = pl.BlockSpec(memory_space=pl.ANY)
    outs = pl.pallas_call(
        wrapped, name=name, grid=grid, in_specs=list(in_specs) + [any_spec] * ns_in, out_specs=list(out_specs) + [any_spec] * ns_out,
        out_shape=list(out_shape) + side.outs, scratch_shapes=list(scratch_shapes) + side.sems, compiler_params=_cparams(),
    )(*args, *side.ins)
    return outs[:n_out], outs[n_out:]


def _run_side(name, side):
    def body(*refs):
        n = len(side.ins)
        side.start(refs[:n], refs[n:2 * n], *refs[2 * n:])
        side.wait(refs[:n], refs[n:2 * n], *refs[2 * n:])

    any_spec = pl.BlockSpec(memory_space=pl.ANY)
    return pl.pallas_call(body, name=name, in_specs=[any_spec] * len(side.ins), out_specs=[any_spec] * len(side.outs),
                          out_shape=side.outs, scratch_shapes=side.sems)(*side.ins)


def _gather_shards(shards, layer_major):
    n = len(shards)

    def body(*refs):
        ins, outs = refs[:n], refs[n:2 * n]
        send, recv, lsem = refs[2 * n:]
        x, y, c, chips = _place()
        me = 2 * x + y

        def slot(t, chip):
            return outs[t].at[:, chip] if layer_major[t] else outs[t].at[chip]

        local, sends = [], []
        for t in range(n):
            cp = pltpu.make_async_copy(ins[t], slot(t, me), lsem.at[t])
            cp.start()
            local.append(cp)
            for r, (px, py) in enumerate(chips):
                rc = pltpu.make_async_remote_copy(src_ref=ins[t], dst_ref=slot(t, me), send_sem=send.at[3 * t + r],
                                                  recv_sem=recv.at[3 * t + r], device_id=(px, py, c), device_id_type=MESH)
                rc.start()
                sends.append(rc)
        for t in range(n):
            for r, (px, py) in enumerate(chips):
                pltpu.make_async_remote_copy(src_ref=ins[t], dst_ref=slot(t, 2 * px + py), send_sem=send.at[3 * t + r],
                                             recv_sem=recv.at[3 * t + r], device_id=(px, py, c), device_id_type=MESH).wait_recv()
        for rc in sends:
            rc.wait_send()
        for cp in local:
            cp.wait()

    any_spec = pl.BlockSpec(memory_space=pl.ANY)
    return pl.pallas_call(
        body, name="gather_shards", in_specs=[any_spec] * n, out_specs=[any_spec] * n,
        out_shape=[jax.ShapeDtypeStruct((s.shape[0], N_CHIPS) + s.shape[1:] if lm else (N_CHIPS,) + s.shape, s.dtype)
                   for s, lm in zip(shards, layer_major)],
        scratch_shapes=[pltpu.SemaphoreType.DMA((3 * n,)), pltpu.SemaphoreType.DMA((3 * n,)), pltpu.SemaphoreType.DMA((n,))],
    )(*shards)


def _scatter_grads(groups):
    flat = [(gi, li, a) for gi, grp in enumerate(groups) for li, a in enumerate(grp)]
    n = len(flat)
    ng = len(groups)

    def body(*refs):
        ins, outs = refs[:n], refs[n:n + ng]
        send, recv, lsem = refs[n + ng:]
        x, y, c, chips = _place()
        me = 2 * x + y
        local, sends = [], []
        for t, (gi, li, _) in enumerate(flat):
            cp = pltpu.make_async_copy(ins[t].at[me], outs[gi].at[me, li], lsem.at[t])
            cp.start()
            local.append(cp)
            for r, (px, py) in enumerate(chips):
                rc = pltpu.make_async_remote_copy(src_ref=ins[t].at[2 * px + py], dst_ref=outs[gi].at[me, li],
                                                  send_sem=send.at[3 * t + r], recv_sem=recv.at[3 * t + r],
                                                  device_id=(px, py, c), device_id_type=MESH)
                rc.start()
                sends.append(rc)
        for t, (gi, li, _) in enumerate(flat):
            for r, (px, py) in enumerate(chips):
                pltpu.make_async_remote_copy(src_ref=ins[t].at[me], dst_ref=outs[gi].at[2 * px + py, li],
                                             send_sem=send.at[3 * t + r], recv_sem=recv.at[3 * t + r],
                                             device_id=(px, py, c), device_id_type=MESH).wait_recv()
        for rc in sends:
            rc.wait_send()
        for cp in local:
            cp.wait()

    any_spec = pl.BlockSpec(memory_space=pl.ANY)
    return pl.pallas_call(
        body, name="scatter_grads", in_specs=[any_spec] * n, out_specs=[any_spec] * ng,
        out_shape=[jax.ShapeDtypeStruct((N_CHIPS, len(grp)) + grp[0].shape[1:], grp[0].dtype) for grp in groups],
        scratch_shapes=[pltpu.SemaphoreType.DMA((3 * n,)), pltpu.SemaphoreType.DMA((3 * n,)), pltpu.SemaphoreType.DMA((n,))],
    )(*[a for _, _, a in flat])


def _swap_with_sibling(arrs):
    n = len(arrs)

    def body(*refs):
        ins, outs = refs[:n], refs[n:2 * n]
        send, recv = refs[2 * n:]
        x, y, c, _ = _place()
        cps = []
        for t in range(n):
            rc = pltpu.make_async_remote_copy(src_ref=ins[t], dst_ref=outs[t], send_sem=send.at[t], recv_sem=recv.at[t],
                                              device_id=(x, y, 1 - c), device_id_type=MESH)
            rc.start()
            cps.append(rc)
        for rc in cps:
            rc.wait_recv()
        for rc in cps:
            rc.wait_send()

    any_spec = pl.BlockSpec(memory_space=pl.ANY)
    return pl.pallas_call(
        body, name="swap_with_sibling", in_specs=[any_spec] * n, out_specs=[any_spec] * n,
        out_shape=[jax.ShapeDtypeStruct(a.shape, a.dtype) for a in arrs],
        scratch_shapes=[pltpu.SemaphoreType.DMA((n,)), pltpu.SemaphoreType.DMA((n,))],
    )(*arrs)


def _allreduce_small(v):
    rows, cols = v.shape
    r8 = rows // (2 * N_CHIPS)
    assert r8 * 2 * N_CHIPS == rows and r8 % SUBLANES == 0, rows

    def body(v_ref, o_ref, sib_ref, cs_ref, slot_ref, send, recv):
        x, y, c, chips = _place()
        me = 2 * x + y
        sibling = (x, y, 1 - c)

        def eighth(ref, chip, core):
            return ref.at[pl.ds(pl.multiple_of((2 * chip + core) * r8, SUBLANES), r8)]

        def copy(src, dst, k, to):
            return pltpu.make_async_remote_copy(src_ref=src, dst_ref=dst, send_sem=send.at[k], recv_sem=recv.at[k],
                                                device_id=to, device_id_type=MESH)

        d2d = copy(v_ref, sib_ref, 0, sibling)
        d2d.start()
        d2d.wait_recv()
        cs_ref[...] = v_ref[...] + sib_ref[...]
        reduce_out = [copy(eighth(cs_ref, 2 * px + py, c), slot_ref.at[me], 1 + r, (px, py, c)) for r, (px, py) in enumerate(chips)]
        for cp in reduce_out:
            cp.start()
        slot_ref[me] = cs_ref[pl.ds(pl.multiple_of((2 * me + c) * r8, SUBLANES), r8), :]
        for r, (px, py) in enumerate(chips):
            copy(eighth(cs_ref, me, c), slot_ref.at[2 * px + py], 1 + r, (px, py, c)).wait_recv()
        o_ref[pl.ds(pl.multiple_of((2 * me + c) * r8, SUBLANES), r8), :] = (slot_ref[0] + slot_ref[1]) + (slot_ref[2] + slot_ref[3])
        mine = eighth(o_ref, me, c)
        hand_out = [copy(mine, mine, 4, sibling)] + [copy(mine, mine, 5 + r, (px, py, c)) for r, (px, py) in enumerate(chips)]
        for cp in hand_out:
            cp.start()
        passed_on = []
        for r, (px, py) in enumerate(chips):
            theirs = eighth(o_ref, 2 * px + py, c)
            copy(theirs, theirs, 5 + r, (px, py, c)).wait_recv()
            fw = copy(theirs, theirs, 8 + r, sibling)
            fw.start()
            passed_on.append(fw)
        sib_own = eighth(o_ref, me, 1 - c)
        copy(sib_own, sib_own, 4, sibling).wait_recv()
        for r, (px, py) in enumerate(chips):
            got = eighth(o_ref, 2 * px + py, 1 - c)
            copy(got, got, 8 + r, sibling).wait_recv()
        for cp in [d2d] + reduce_out + hand_out + passed_on:
            cp.wait_send()

    vm = pl.BlockSpec(memory_space=pltpu.VMEM)
    return pl.pallas_call(
        body, name="allreduce_small", in_specs=[vm], out_specs=vm, out_shape=jax.ShapeDtypeStruct((rows, cols), F32),
        scratch_shapes=[pltpu.VMEM((rows, cols), F32), pltpu.VMEM((rows, cols), F32), pltpu.VMEM((N_CHIPS, r8, cols), F32),
                        pltpu.SemaphoreType.DMA((11,)), pltpu.SemaphoreType.DMA((11,))],
        compiler_params=_cparams(),
    )(v)


def _pack(tensors):
    pieces = []
    for t in tensors:
        flat = t.reshape(-1)
        pad = (-flat.shape[0]) % (SUBLANES * LANES)
        pieces.append(jnp.pad(flat, (0, pad)).reshape(-1, LANES))
    rows = sum(p.shape[0] for p in pieces)
    pieces.append(jnp.zeros(((-rows) % PACK_ROW_MULTIPLE, LANES), tensors[0].dtype))
    return jnp.concatenate(pieces, axis=0)


def _unpack(buf, like):
    out, off = [], 0
    for t in like:
        size = math.prod(t.shape)
        rows = -(-size // (SUBLANES * LANES)) * SUBLANES
        out.append(buf[off:off + rows].reshape(-1)[:size].reshape(t.shape))
        off += rows
    return out


def _s5_pack_b(bb):
    gc, g, p = bb.shape
    q = S5_GROUPS_PER_BLOCK
    t = bb.reshape(gc, g // q, q, p).transpose(1, 2, 0, 3)
    eye = jnp.eye(q, dtype=bb.dtype)
    return (t[:, :, :, None, :] * eye[None, :, None, :, None]).reshape(g // q, q * gc, q * p)


def _s5_unpack_b(dbp, gc, p):
    nb = dbp.shape[0]
    q = S5_GROUPS_PER_BLOCK
    eye = jnp.eye(q, dtype=dbp.dtype)
    t = (dbp.reshape(nb, q, gc, q, p) * eye[None, :, None, :, None]).sum(axis=3)
    return t.transpose(2, 0, 1, 3).reshape(gc, nb * q, p)


def _s5_pack_c(cc):
    g, gc, p = cc.shape
    q = S5_GROUPS_PER_BLOCK
    t = cc.reshape(g // q, q, gc, p).transpose(0, 1, 3, 2)
    eye = jnp.eye(q, dtype=cc.dtype)
    return (t[:, :, :, None, :] * eye[None, :, None, :, None]).reshape(g // q, q * p, q * gc)


def _s5_unpack_c(dcp, gc, p):
    nb = dcp.shape[0]
    q = S5_GROUPS_PER_BLOCK
    eye = jnp.eye(q, dtype=dcp.dtype)
    t = (dcp.reshape(nb, q, p, q, gc) * eye[None, :, None, :, None]).sum(axis=3)
    return t.transpose(0, 1, 3, 2).reshape(nb * q, gc, p)


def _split2(m):
    return m.arr[:, 0]


def kernel(x, norm_mix_g, norm_ffn_g, norm_final_g, rg_w_in, rg_conv_w, rg_conv_b, rg_w_a, rg_b_a, rg_w_x, rg_b_x, rg_lambda, rg_w_out, s5_w_in, s5_a_re, s5_a_im, s5_log_dt, s5_b_re, s5_b_im, s5_c_re, s5_c_im, s5_d, s5_w_glu, s5_w_out, ffn_w_up, ffn_conv_w, ffn_conv_b, ffn_w_down, loss_target, m_norm_mix_g, m_norm_ffn_g, m_norm_final_g, m_rg_w_in, m_rg_conv_w, m_rg_conv_b, m_rg_w_a, m_rg_b_a, m_rg_w_x, m_rg_b_x, m_rg_lambda, m_rg_w_out, m_s5_w_in, m_s5_a_re, m_s5_a_im, m_s5_log_dt, m_s5_b_re, m_s5_b_im, m_s5_c_re, m_s5_c_im, m_s5_d, m_s5_w_glu, m_s5_w_out, m_ffn_w_up, m_ffn_conv_w, m_ffn_conv_b, m_ffn_w_down, v_norm_mix_g, v_norm_ffn_g, v_norm_final_g, v_rg_w_in, v_rg_conv_w, v_rg_conv_b, v_rg_w_a, v_rg_b_a, v_rg_w_x, v_rg_b_x, v_rg_lambda, v_rg_w_out, v_s5_w_in, v_s5_a_re, v_s5_a_im, v_s5_log_dt, v_s5_b_re, v_s5_b_im, v_s5_c_re, v_s5_c_im, v_s5_d, v_s5_w_glu, v_s5_w_out, v_ffn_w_up, v_ffn_conv_w, v_ffn_conv_b, v_ffn_w_down):
    w = dict(zip(PARAM_NAMES, (norm_mix_g, norm_ffn_g, norm_final_g, rg_w_in, rg_conv_w, rg_conv_b, rg_w_a, rg_b_a, rg_w_x, rg_b_x,
                               rg_lambda, rg_w_out, s5_w_in, s5_a_re, s5_a_im, s5_log_dt, s5_b_re, s5_b_im, s5_c_re, s5_c_im, s5_d,
                               s5_w_glu, s5_w_out, ffn_w_up, ffn_conv_w, ffn_conv_b, ffn_w_down)))
    mom = dict(zip(PARAM_NAMES, (m_norm_mix_g, m_norm_ffn_g, m_norm_final_g, m_rg_w_in, m_rg_conv_w, m_rg_conv_b, m_rg_w_a, m_rg_b_a,
                                 m_rg_w_x, m_rg_b_x, m_rg_lambda, m_rg_w_out, m_s5_w_in, m_s5_a_re, m_s5_a_im, m_s5_log_dt, m_s5_b_re,
                                 m_s5_b_im, m_s5_c_re, m_s5_c_im, m_s5_d, m_s5_w_glu, m_s5_w_out, m_ffn_w_up, m_ffn_conv_w,
                                 m_ffn_conv_b, m_ffn_w_down)))
    vel = dict(zip(PARAM_NAMES, (v_norm_mix_g, v_norm_ffn_g, v_norm_final_g, v_rg_w_in, v_rg_conv_w, v_rg_conv_b, v_rg_w_a, v_rg_b_a,
                                 v_rg_w_x, v_rg_b_x, v_rg_lambda, v_rg_w_out, v_s5_w_in, v_s5_a_re, v_s5_a_im, v_s5_log_dt, v_s5_b_re,
                                 v_s5_b_im, v_s5_c_re, v_s5_c_im, v_s5_d, v_s5_w_glu, v_s5_w_out, v_ffn_w_up, v_ffn_conv_w,
                                 v_ffn_conv_b, v_ffn_w_down)))
    _, s, d = x.shape
    depth = norm_mix_g.shape[0]
    n_grp, n_state = s5_a_re.shape[1], s5_a_re.shape[2]
    gc = s5_b_re.shape[3]
    d_ff = ffn_w_down.shape[1] * N_CHIPS
    s5_ts = min(256, s)
    s5_perm = _segment_perm(s5_ts)

    wb = {n: (w[n].astype(BF16) if n in BIG else w[n]) for n in SHARDED}
    gath = {}

    def mixer_keys(i):
        return [(n, i // 2) for n in MIXER_SHARDED[i % 2]] if i < depth else []

    def gather_side(keys):
        return _gather_side([wb[n][l] for n, l in keys])

    def put(keys, arrs):
        for k, a in zip(keys, arrs):
            gath[k] = a

    def wcol(n, l):
        return Mat(gath[(n, l)][:, None], 0, 'c')

    def wrow(n, l):
        g = gath[(n, l)]
        return Mat(g.reshape(1, 1, N_CHIPS * g.shape[1], g.shape[2]), 0, 'c')

    def rg_cw(l):
        return gath[('rg_conv_w', l)].transpose(1, 0, 2).reshape(RG_CONV_W, d)

    def s5_dv(l):
        return gath[('s5_d', l)].reshape(1, d)

    def f_cw(l):
        return gath[('ffn_conv_w', l)].transpose(1, 0, 2).reshape(FFN_CONV_W, 2, d_ff).transpose(1, 0, 2)

    tm = min(1024, s)
    d_up = 2 * d_ff // N_CHIPS
    f_cb = ffn_conv_b.reshape(depth, 2, 1, d_ff)
    put(mixer_keys(0), _run_side("gather_first", gather_side(mixer_keys(0))))

    h = x.reshape(s, d)
    saved = []
    for i in range(depth):
        j = i // 2
        sv = {'h_in': h}
        hn = _rms_fwd(h, norm_mix_g[i:i + 1])
        sv['hn'] = hn
        up_keys = [('ffn_w_up', i), ('ffn_conv_w', i)]
        if i % 2 == 0:
            xg = _mm("rg_in", 'nn', act(hn), wcol('rg_w_in', j), out_parts=2, tm=tm, tn=512, tk=d)
            xg2 = _split2(xg)
            wa, wx = rg_w_a[j].astype(BF16), rg_w_x[j].astype(BF16)
            ba, bx = rg_b_a[j].reshape(1, d), rg_b_x[j].reshape(1, d)
            (xr, hs, y), got = _rg_fwd(xg2, rg_cw(j), rg_conv_b[j:j + 1], wa, ba, wx, bx, rg_lambda[j:j + 1],
                                       side=gather_side(up_keys))
            put(up_keys, got)
            sv.update(xg2=xg2, xr=xr, hs=hs, y=y, wa=wa, wx=wx, ba=ba, bx=bx)
            h = _mm("rg_out", 'nn', act(y), wrow('rg_w_out', j), res=act(h), tm=tm, tn=d, tk=d).arr[0, 0]
        else:
            u = _mm("s5_in", 'nn', act(hn), wrow('s5_w_in', j), tm=tm, tn=d, tk=d).arr[0, 0]
            bt_re, bt_im = s5_b_re[j].transpose(2, 0, 1), s5_b_im[j].transpose(2, 0, 1)
            ldt = s5_log_dt[j].reshape(n_grp, 1)
            tab_r, tab_i, rtab_r, rtab_i, bbr, bbi = _s5_tables3(s5_a_re[j], s5_a_im[j], ldt, bt_re, bt_im, seg=s5_ts // SUBLANES)
            nn_ = n_grp * n_state
            tab_r, tab_i, rtab_r, rtab_i = (t.reshape(5, SUBLANES, nn_) for t in (tab_r, tab_i, rtab_r, rtab_i))
            prm = dict(bp_r=_s5_pack_b(bbr).astype(BF16), bp_i=_s5_pack_b(bbi).astype(BF16),
                       cp_r=_s5_pack_c(s5_c_re[j]).astype(BF16), cp_i=_s5_pack_c(s5_c_im[j]).astype(BF16), dvec=s5_dv(j))
            (hr, hi, ypre, gy), got = _s5_fwd3(u, s5_perm, s5_perm.T, tab_r, tab_i, ts=s5_ts, side=gather_side(up_keys), **prm)
            sv.update(rtab_r=rtab_r, rtab_i=rtab_i)
            put(up_keys, got)
            gl = _mm("s5_glu", 'nn', act(gy), wcol('s5_w_glu', j), out_parts=2, tm=tm, tn=512, tk=d)
            gl2 = _split2(gl)
            o = _glu(gl2)
            sv.update(u=u, prm=prm, hr=hr, hi=hi, ypre=ypre, gy=gy, gl2=gl2, o=o, bt_re=bt_re, bt_im=bt_im, ldt=ldt)
            h = _mm("s5_out", 'nn', act(o), wrow('s5_w_out', j), res=act(h), tm=tm, tn=d, tk=d).arr[0, 0]
        sv['h_mid'] = h
        hn2 = _rms_fwd(h, norm_ffn_g[i:i + 1])
        next_keys = [('ffn_w_down', i)] + mixer_keys(i + 1)
        (up2, c2, a_ffn), got = _ffn_up_act(hn2, gath[('ffn_w_up', i)], f_cw(i), f_cb[i], side=gather_side(next_keys))
        put(next_keys, got)
        sv.update(hn2=hn2, up2=up2, c2=c2, act=a_ffn)
        h = _mm("ffn_down", 'nn', act(a_ffn), wrow('ffn_w_down', i), res=act(h), tm=tm, tn=d, tk=d_ff // 2).arr[0, 0]
        saved.append(sv)

    loss_row, dh, dg_final = _loss_and_grad(h, norm_final_g.reshape(1, d), loss_target.reshape(s, d))
    loss = lax.psum(loss_row[0, 0], ("x", "y", "c"))

    gl_ = {n: [None] * w[n].shape[0] for n in PARAM_NAMES if n != 'norm_final_g'}
    recvd = {}

    def as4(n, a):
        return a.reshape((N_CHIPS,) + w[n].shape[1:])

    def scatter_side(keys):
        arrs, shapes, views = [], [], []
        for n, l in keys:
            shape = (N_CHIPS,) + w[n].shape[1:]
            halves = False
            arrs.append(gl_[n][l] if halves else gl_[n][l].reshape(shape))
            views.append(_halves_view if halves else None)
            shapes.append(shape)
        return _scatter_side(arrs, shapes, views)

    def record(keys, arrs):
        for k, a in zip(keys, arrs):
            recvd[k] = a

    pending = None
    for i in reversed(range(depth)):
        j = i // 2
        sv = saved[i]
        gl_['ffn_w_down'][i] = _mm("ffn_down_dw", 'tn', act(sv['act']), act(dh), out_dtype=BF16, tm=d_ff // 2, tn=d, tk=tm).arr
        (dup2, dcw2, dcb2), got = _ffn_bwd_fused(dh, gath[('ffn_w_down', i)].reshape(d_ff, d), sv['up2'], sv['c2'], f_cw(i),
                                                 side=scatter_side(pending) if pending else None)
        if pending:
            record(pending, got)
        gl_['ffn_conv_w'][i] = dcw2.transpose(1, 0, 2).reshape(FFN_CONV_W, 2 * d_ff)
        gl_['ffn_conv_b'][i] = dcb2.reshape(2 * d_ff)
        dup = Mat(dup2[:, None], 0, 'c')
        gl_['ffn_w_up'][i] = _mm("ffn_up_dw", 'tn', act(sv['hn2']), dup, out_parts=N_CHIPS, out_dtype=BF16, tm=d, tn=d_up, tk=tm).arr
        dh, dg = _mm_rms_bwd("ffn_up_dx", dup, wcol('ffn_w_up', i), sv['h_mid'], norm_ffn_g[i:i + 1], dh, tm=tm, tk=d_up)
        gl_['norm_ffn_g'][i] = dg[0]
        ffn_keys = [('ffn_w_up', i), ('ffn_w_down', i)]
        if i % 2 == 0:
            dy = _mm("rg_out_dx", 'nt', act(dh), wrow('rg_w_out', j), tm=tm, tn=d, tk=d).arr[0, 0]
            gl_['rg_w_out'][j] = _mm("rg_out_dw", 'tn', act(sv['y']), act(dh), out_dtype=BF16, tm=d, tn=d, tk=tm).arr
            (dxg2, dcw, dcb, dwa, dba, dwx, dbx, dlam), got = _rg_bwd(
                dy, sv['xg2'], sv['xr'], sv['hs'], rg_cw(j), sv['wa'], sv['ba'], sv['wx'], sv['bx'], rg_lambda[j:j + 1],
                side=scatter_side(ffn_keys))
            record(ffn_keys, got)
            gl_['rg_conv_w'][j] = dcw
            gl_['rg_conv_b'][j] = dcb[0]
            gl_['rg_w_a'][j], gl_['rg_w_x'][j] = dwa, dwx
            gl_['rg_b_a'][j], gl_['rg_b_x'][j] = dba.reshape(rg_b_a.shape[1:]), dbx.reshape(rg_b_x.shape[1:])
            gl_['rg_lambda'][j] = dlam[0]
            dxg = Mat(dxg2[:, None], 0, 'c')
            gl_['rg_w_in'][j] = _mm("rg_in_dw", 'tn', act(sv['hn']), dxg, out_parts=N_CHIPS, out_dtype=BF16, tm=d, tn=512, tk=tm).arr
            mix_dx = ("rg_in_dx", dxg, wcol('rg_w_in', j), 512)
            pending = [('rg_w_in', j), ('rg_w_out', j)]
        else:
            d_o = _mm("s5_out_dx", 'nt', act(dh), wrow('s5_w_out', j), tm=tm, tn=d, tk=d).arr[0, 0]
            gl_['s5_w_out'][j] = _mm("s5_out_dw", 'tn', act(sv['o']), act(dh), out_dtype=BF16, tm=d, tn=d, tk=tm).arr
            dgl2 = _glu_bwd(sv['gl2'], d_o)
            dgl = Mat(dgl2[:, None], 0, 'c')
            gl_['s5_w_glu'][j] = _mm("s5_glu_dw", 'tn', act(sv['gy']), dgl, out_parts=N_CHIPS, out_dtype=BF16, tm=d, tn=512, tk=tm).arr
            dgy = _mm("s5_glu_dx", 'nt', dgl, wcol('s5_w_glu', j), tm=tm, tn=d, tk=512).arr[0, 0]
            (du, dar, dai, dbpr, dbpi, dcpr, dcpi, dd), got = _s5_bwd3(
                dgy, sv['ypre'], sv['u'], sv['hr'], sv['hi'], s5_perm, s5_perm.T, sv['rtab_r'], sv['rtab_i'], ts=s5_ts,
                side=scatter_side(ffn_keys), **sv['prm'])
            record(ffn_keys, got)
            gl_['s5_d'][j] = dd[0]
            gl_['s5_c_re'][j] = _s5_unpack_c(dcpr, gc, n_state)
            gl_['s5_c_im'][j] = -_s5_unpack_c(dcpi, gc, n_state)
            d_are, d_aim, d_ldt, d_btr, d_bti = _s5_params_bwd(
                s5_a_re[j], s5_a_im[j], sv['ldt'], sv['bt_re'], sv['bt_im'], dar.reshape(n_grp, n_state), dai.reshape(n_grp, n_state),
                _s5_unpack_b(dbpr, gc, n_state), _s5_unpack_b(dbpi, gc, n_state))
            gl_['s5_a_re'][j], gl_['s5_a_im'][j], gl_['s5_log_dt'][j] = d_are, d_aim, d_ldt[:, 0]
            gl_['s5_b_re'][j], gl_['s5_b_im'][j] = d_btr.transpose(1, 2, 0), d_bti.transpose(1, 2, 0)
            dum = act(du)
            gl_['s5_w_in'][j] = _mm("s5_in_dw", 'tn', act(sv['hn']), dum, out_dtype=BF16, tm=d, tn=d, tk=tm).arr
            mix_dx = ("s5_in_dx", dum, wrow('s5_w_in', j), d)
            pending = [('s5_w_in', j), ('s5_w_glu', j), ('s5_w_out', j)]
        dh, dg = _mm_rms_bwd(mix_dx[0], mix_dx[1], mix_dx[2], sv['h_in'], norm_mix_g[i:i + 1], dh, tm=tm, tk=mix_dx[3])
        gl_['norm_mix_g'][i] = dg[0]
    grad_x = dh.reshape(x.shape)
    record(pending, _run_side("scatter_last", scatter_side(pending)))

    chip_sums = []
    for n in BIG:
        cols = w[n].shape[-1]
        chip_sums.append(_sum_parts([recvd[(n, l)].reshape(N_CHIPS, -1, cols) for l in range(w[n].shape[0])]))
    sib_sums = _swap_with_sibling(chip_sums)
    results = {}
    for n, mine, theirs in zip(BIG, chip_sums, sib_sums):
        cols = w[n].shape[-1]
        outs = _adamw(w[n].reshape(-1, cols), [mine.reshape(-1, cols), theirs.reshape(-1, cols)], mom[n].reshape(-1, cols),
                      vel[n].reshape(-1, cols))
        results[n] = [o.reshape(w[n].shape) for o in outs]

    small = REPLICATED + SMALL_SHARDED
    local = [dg_final.reshape(d) if n == 'norm_final_g' else jnp.stack(gl_[n]) for n in small]
    summed = _unpack(_allreduce_small(_pack(local)), local)
    me = 2 * lax.axis_index("x") + lax.axis_index("y")
    grads = [lax.dynamic_slice_in_dim(g, me * w[n].shape[-1], w[n].shape[-1], axis=g.ndim - 1) if n in SMALL_SHARDED else g
             for n, g in zip(small, summed)]
    like = [w[n] for n in small]
    outs = _adamw(_pack(like), [_pack(grads)], _pack([mom[n] for n in small]), _pack([vel[n] for n in small]))
    unpacked = [_unpack(o, like) for o in outs]
    for k, n in enumerate(small):
        results[n] = [unpacked[q][k] for q in range(4)]

    return (loss, grad_x, *[results[n][0] for n in PARAM_NAMES], *[results[n][1] for n in PARAM_NAMES],
            *[results[n][2] for n in PARAM_NAMES], *[results[n][3] for n in PARAM_NAMES])
```

```python
import functools
import math

import jax
import jax.numpy as jnp
from jax import lax
from jax.experimental import pallas as pl
from jax.experimental.pallas import tpu as pltpu

F32 = jnp.float32
BF16 = jnp.bfloat16
MESH = pl.DeviceIdType.MESH

NORM_EPS = 1e-6
RG_HEADS = 8
RG_CONV_W = 4
RG_C = 8.0
S5_GC = 16
S5_P = 64
S5_GROUPS_PER_BLOCK = 8
FFN_CONV_W = 3
N_CHIPS = 4
ADAM_LR, ADAM_B1, ADAM_B2, ADAM_EPS, ADAM_WD, ADAM_STEP = 0.001, 0.9, 0.999, 1e-08, 0.01, 10
VMEM_LIMIT_BYTES = 56 * 1024 * 1024
SUBLANES = 8
LANES = 128

PARAM_NAMES = ['norm_mix_g', 'norm_ffn_g', 'norm_final_g', 'rg_w_in', 'rg_conv_w', 'rg_conv_b', 'rg_w_a', 'rg_b_a', 'rg_w_x',
               'rg_b_x', 'rg_lambda', 'rg_w_out', 's5_w_in', 's5_a_re', 's5_a_im', 's5_log_dt', 's5_b_re', 's5_b_im', 's5_c_re',
               's5_c_im', 's5_d', 's5_w_glu', 's5_w_out', 'ffn_w_up', 'ffn_conv_w', 'ffn_conv_b', 'ffn_w_down']
SHARDED = ['rg_w_in', 'rg_conv_w', 'rg_w_out', 's5_w_in', 's5_d', 's5_w_glu', 's5_w_out', 'ffn_w_up', 'ffn_conv_w', 'ffn_w_down']
BIG = ['rg_w_in', 'rg_w_out', 's5_w_in', 's5_w_glu', 's5_w_out', 'ffn_w_up', 'ffn_w_down']
ROW_SHARDED = ['rg_w_out', 's5_w_in', 's5_w_out', 'ffn_w_down']
SMALL_SHARDED = ['rg_conv_w', 's5_d', 'ffn_conv_w']
MIXER_SHARDED = [['rg_w_in', 'rg_conv_w', 'rg_w_out'], ['s5_w_in', 's5_d', 's5_w_glu', 's5_w_out']]
FFN_SHARDED = ['ffn_w_up', 'ffn_conv_w', 'ffn_w_down']
REPLICATED = [n for n in PARAM_NAMES if n not in SHARDED]


def _cparams():
    return pltpu.CompilerParams(vmem_limit_bytes=VMEM_LIMIT_BYTES)


_GELU_C = math.sqrt(2.0 / math.pi)
_GELU_K = 0.044715


def _gelu(x):
    return 0.5 * x * (1.0 + jnp.tanh(_GELU_C * (x + _GELU_K * x * x * x)))


def _gelu_and_grad(x):
    t = jnp.tanh(_GELU_C * (x + _GELU_K * x * x * x))
    g = 0.5 * x * (1.0 + t)
    dg = 0.5 * (1.0 + t) + 0.5 * x * (1.0 - t * t) * (_GELU_C * (1.0 + 3.0 * _GELU_K * x * x))
    return g, dg


def _sigmoid(x):
    return jax.nn.sigmoid(x)


def _neg_expm1(x):
    series = -(x * (1.0 + x * (0.5 + x * (1.0 / 6 + x * (1.0 / 24 + x * (1.0 / 120 + x * (1.0 / 720)))))))
    return jnp.where(x > -0.25, series, 1.0 - jnp.exp(x))


def _softplus(z):
    return jnp.maximum(z, 0.0) + jnp.log1p(jnp.exp(-jnp.abs(z)))


def _rows(shape):
    return lax.broadcasted_iota(jnp.int32, shape, 0)


def _shift_down(x, halo, k):
    ext = jnp.concatenate([halo, x], axis=0)
    return pltpu.roll(ext, k, 0)[SUBLANES:]


def _shift_up(x, halo, k):
    ext = jnp.concatenate([x, halo], axis=0)
    n = ext.shape[0]
    return pltpu.roll(ext, n - k, 0)[:x.shape[0]]


def _scan_real_fwd(a, b):
    n = a.shape[0]
    row = _rows(a.shape)
    sh = 1
    while sh < n:
        ok = row >= sh
        b = a * jnp.where(ok, pltpu.roll(b, sh, 0), 0.0) + b
        if sh * 2 < n:
            a = a * jnp.where(ok, pltpu.roll(a, sh, 0), 1.0)
        sh *= 2
    return b


def _scan_real_rev(c, d):
    n = c.shape[0]
    row = _rows(c.shape)
    sh = 1
    while sh < n:
        ok = row < n - sh
        d = c * jnp.where(ok, pltpu.roll(d, n - sh, 0), 0.0) + d
        if sh * 2 < n:
            c = c * jnp.where(ok, pltpu.roll(c, n - sh, 0), 1.0)
        sh *= 2
    return d


def _scan_cplx(br, bi, pr_ref, pi_ref, reverse):
    n = br.shape[0]
    row = _rows(br.shape)
    sh, k = 1, 0
    while sh < n:
        pr = pr_ref[k:k + 1, :]
        pi = pi_ref[k:k + 1, :]
        if reverse:
            ok = row < n - sh
            sr = jnp.where(ok, pltpu.roll(br, n - sh, 0), 0.0)
            si = jnp.where(ok, pltpu.roll(bi, n - sh, 0), 0.0)
        else:
            ok = row >= sh
            sr = jnp.where(ok, pltpu.roll(br, sh, 0), 0.0)
            si = jnp.where(ok, pltpu.roll(bi, sh, 0), 0.0)
        br, bi = br + pr * sr - pi * si, bi + pr * si + pi * sr
        sh *= 2
        k += 1
    return br, bi


RG_LANE_CHUNK = 512


def _real_slab_scan(a_ref, b_ref, out_ref, carry_ref, reverse):
    t, c = a_ref.shape
    nsl = t // SUBLANES
    lc = min(RG_LANE_CHUNK, c)
    row8 = _rows((SUBLANES, lc))
    for q in range(c // lc):
        sl = slice(q * lc, (q + 1) * lc)

        def slab(jj, carry, sl=sl):
            j = nsl - 1 - jj if reverse else jj
            r0 = pl.multiple_of(j * SUBLANES, SUBLANES)
            a, b = a_ref[pl.ds(r0, SUBLANES), sl], b_ref[pl.ds(r0, SUBLANES), sl]
            for k in range(3):
                sh = 1 << k
                keep = row8 < SUBLANES - sh if reverse else row8 >= sh
                amount = SUBLANES - sh if reverse else sh
                b = a * jnp.where(keep, pltpu.roll(b, amount, 0), 0.0) + b
                a = a * jnp.where(keep, pltpu.roll(a, amount, 0), 1.0)
            x = b + a * jnp.broadcast_to(carry, b.shape)
            out_ref[pl.ds(r0, SUBLANES), sl] = x
            return x[:1, :] if reverse else x[SUBLANES - 1:, :]

        carry_ref[:, sl] = lax.fori_loop(0, nsl, slab, carry_ref[:, sl], unroll=2)


class Mat:
    def __init__(self, arr, l=0, split='c'):
        assert arr.ndim == 4
        self.arr, self.l, self.split = arr, l, split
        p, _, r, c = arr.shape
        self.shape = (r, c * p) if split == 'c' else (r * p, c)

    def spec(self, tr, tc, rc):
        p, _, r, c = self.arr.shape
        l = self.l
        assert r % tr == 0 and c % tc == 0, (self.arr.shape, tr, tc)
        if self.split == 'c':
            per = c // tc
            return pl.BlockSpec((None, None, tr, tc), lambda i, j, k: (rc(i, j, k)[1] // per, l, rc(i, j, k)[0], rc(i, j, k)[1] % per))
        per = r // tr
        return pl.BlockSpec((None, None, tr, tc), lambda i, j, k: (rc(i, j, k)[0] // per, l, rc(i, j, k)[0] % per, rc(i, j, k)[1]))


def act(x, parts=1):
    s, c = x.shape
    return Mat(x.reshape(s, parts, c // parts).transpose(1, 0, 2)[:, None] if parts > 1 else x[None, None])


def _mm(name, mode, a, b, *, out_parts=1, out_split='c', out_dtype=F32, res=None, tm=512, tn=512, tk=512):
    if mode == 'nn':
        (m, kk), (kb, n) = a.shape, b.shape
    elif mode == 'nt':
        (m, kk), (n, kb) = a.shape, b.shape
    else:
        (kk, m), (kb, n) = a.shape, b.shape
    assert kk == kb, (name, a.shape, b.shape)
    tm, tn, tk = min(tm, m), min(tn, n), min(tk, kk)
    assert m % tm == 0 and n % tn == 0 and kk % tk == 0, (name, m, n, kk, tm, tn, tk)
    nk = kk // tk
    if mode == 'nn':
        a_spec = a.spec(tm, tk, lambda i, j, k: (i, k))
        b_spec = b.spec(tk, tn, lambda i, j, k: (k, j))
        dims = (((1,), (0,)), ((), ()))
    elif mode == 'nt':
        a_spec = a.spec(tm, tk, lambda i, j, k: (i, k))
        b_spec = b.spec(tn, tk, lambda i, j, k: (j, k))
        dims = (((1,), (1,)), ((), ()))
    else:
        a_spec = a.spec(tk, tm, lambda i, j, k: (k, i))
        b_spec = b.spec(tk, tn, lambda i, j, k: (k, j))
        dims = (((0,), (0,)), ((), ()))
    if out_split == 'c':
        out_arr = jax.ShapeDtypeStruct((out_parts, 1, m, n // out_parts), out_dtype)
    else:
        out_arr = jax.ShapeDtypeStruct((out_parts, 1, m // out_parts, n), out_dtype)
    out_mat = Mat(out_arr, 0, out_split)
    o_spec = out_mat.spec(tm, tn, lambda i, j, k: (i, j))
    has_res = res is not None

    def body(*refs):
        if has_res:
            a_ref, b_ref, r_ref, o_ref = refs[:4]
        else:
            a_ref, b_ref, o_ref = refs[:3]
        prod = lax.dot_general(a_ref[...].astype(BF16), b_ref[...].astype(BF16), dims, preferred_element_type=F32)

        def finish(acc):
            if has_res:
                acc = acc + r_ref[...]
            o_ref[...] = acc.astype(out_dtype)

        if nk == 1:
            finish(prod)
        else:
            acc_ref = refs[-1]
            k = pl.program_id(2)

            @pl.when(k == 0)
            def _():
                acc_ref[...] = prod

            @pl.when(k > 0)
            def _():
                acc_ref[...] += prod

            @pl.when(k == nk - 1)
            def _():
                finish(acc_ref[...])

    in_specs = [a_spec, b_spec]
    args = [a.arr, b.arr]
    if has_res:
        in_specs.append(res.spec(tm, tn, lambda i, j, k: (i, j)))
        args.append(res.arr)
    out = pl.pallas_call(
        body, name=name, grid=(m // tm, n // tn, nk), in_specs=in_specs, out_specs=o_spec, out_shape=out_arr,
        scratch_shapes=[pltpu.VMEM((tm, tn), F32)] if nk > 1 else [], compiler_params=_cparams(),
    )(*args)
    return Mat(out, 0, out_split)


def _rms_fwd(h, g, ts=512):
    s, d = h.shape
    ts = min(ts, s)

    def body(h_ref, g_ref, o_ref):
        x = h_ref[...]
        var = jnp.mean(x * x, axis=-1, keepdims=True)
        o_ref[...] = (x * lax.rsqrt(var + NORM_EPS) * g_ref[...]).astype(BF16)

    return pl.pallas_call(
        body, name="rms_fwd", grid=(s // ts,),
        in_specs=[pl.BlockSpec((ts, d), lambda i: (i, 0)), pl.BlockSpec((1, d), lambda i: (0, 0))],
        out_specs=pl.BlockSpec((ts, d), lambda i: (i, 0)), out_shape=jax.ShapeDtypeStruct((s, d), BF16),
        compiler_params=_cparams(),
    )(h, g)


def _rms_bwd(h, g, dhn, dh_in, ts=512):
    s, d = h.shape
    ts = min(ts, s)

    def body(h_ref, g_ref, dhn_ref, dhin_ref, dh_ref, dg_ref):
        i = pl.program_id(0)
        x = h_ref[...]
        rstd = lax.rsqrt(jnp.mean(x * x, axis=-1, keepdims=True) + NORM_EPS)
        xhat = x * rstd
        dhn_v = dhn_ref[...]
        dxh = dhn_v * g_ref[...]
        dh_ref[...] = dhin_ref[...] + rstd * (dxh - xhat * jnp.mean(dxh * xhat, axis=-1, keepdims=True))
        part = jnp.sum(dhn_v * xhat, axis=0, keepdims=True)

        @pl.when(i == 0)
        def _():
            dg_ref[...] = part

        @pl.when(i > 0)
        def _():
            dg_ref[...] += part

    row = pl.BlockSpec((ts, d), lambda i: (i, 0))
    vec = pl.BlockSpec((1, d), lambda i: (0, 0))
    return pl.pallas_call(
        body, name="rms_bwd", grid=(s // ts,), in_specs=[row, vec, row, row], out_specs=[row, vec],
        out_shape=[jax.ShapeDtypeStruct((s, d), F32), jax.ShapeDtypeStruct((1, d), F32)], compiler_params=_cparams(),
    )(h, g, dhn, dh_in)


def _loss_and_grad(h, g, tgt, ts=512):
    s, d = h.shape
    ts = min(ts, s)

    def body(h_ref, g_ref, t_ref, loss_ref, dh_ref, dg_ref):
        i = pl.program_id(0)
        x = h_ref[...]
        gv = g_ref[...]
        rstd = lax.rsqrt(jnp.mean(x * x, axis=-1, keepdims=True) + NORM_EPS)
        xhat = x * rstd
        err = xhat * gv - t_ref[...]
        dy = err * (1.0 / d)
        dxh = dy * gv
        dh_ref[...] = rstd * (dxh - xhat * jnp.mean(dxh * xhat, axis=-1, keepdims=True))
        part = jnp.sum(dy * xhat, axis=0, keepdims=True)
        lpart = jnp.broadcast_to(jnp.sum(jnp.sum(err * err, axis=0, keepdims=True), axis=1, keepdims=True) * (0.5 / d), (1, LANES))

        @pl.when(i == 0)
        def _():
            dg_ref[...] = part
            loss_ref[...] = lpart

        @pl.when(i > 0)
        def _():
            dg_ref[...] += part
            loss_ref[...] += lpart

    row = pl.BlockSpec((ts, d), lambda i: (i, 0))
    vec = pl.BlockSpec((1, d), lambda i: (0, 0))
    return pl.pallas_call(
        body, name="loss_and_grad", grid=(s // ts,), in_specs=[row, vec, row],
        out_specs=[pl.BlockSpec((1, LANES), lambda i: (0, 0)), row, vec],
        out_shape=[jax.ShapeDtypeStruct((1, LANES), F32), jax.ShapeDtypeStruct((s, d), F32), jax.ShapeDtypeStruct((1, d), F32)],
        compiler_params=_cparams(),
    )(h, g, tgt)


def _halo_before(ts, nrow8):
    return lambda i: jnp.maximum(i * (ts // SUBLANES) - 1, 0)


def _ffn_act(up2, conv_w2, conv_b2, ts=512, tn=512, side=None):
    _, s, f = up2.shape
    ts, tn = min(ts, s), min(tn, f)
    kw = FFN_CONV_W

    def body(up_ref, halo_ref, w_ref, b_ref, o_ref):
        i = pl.program_id(0)
        cs = []
        for h in range(2):
            x = up_ref[h]
            halo = jnp.where(i == 0, 0.0, halo_ref[h])
            c = b_ref[h] + w_ref[h, kw - 1:kw, :] * x
            for sft in range(1, kw):
                c = c + w_ref[h, kw - 1 - sft:kw - sft, :] * _shift_down(x, halo, sft)
            cs.append(c)
        o_ref[...] = (_gelu(cs[0]) * cs[1]).astype(BF16)

    hb = ts // SUBLANES
    g0, g1 = s // ts, f // tn
    outs, side_outs = _call_with_side(
        body, side, lambda: (pl.program_id(0) == 0) & (pl.program_id(1) == 0),
        lambda: (pl.program_id(0) == g0 - 1) & (pl.program_id(1) == g1 - 1),
        name="ffn_act", grid=(g0, g1),
        in_specs=[pl.BlockSpec((2, ts, tn), lambda i, j: (0, i, j)),
                  pl.BlockSpec((2, SUBLANES, tn), lambda i, j: (0, jnp.maximum(i * hb - 1, 0), j)),
                  pl.BlockSpec((2, kw, tn), lambda i, j: (0, 0, j)),
                  pl.BlockSpec((2, 1, tn), lambda i, j: (0, 0, j))],
        out_specs=[pl.BlockSpec((ts, tn), lambda i, j: (i, j))], out_shape=[jax.ShapeDtypeStruct((s, f), BF16)],
        scratch_shapes=[], args=(up2, up2, conv_w2, conv_b2))
    return outs[0], side_outs


def _ffn_bwd(up2, dact, conv_w2, conv_b2, ts=256, tn=512, side=None):
    _, s, f = up2.shape
    ts, tn = min(ts, s), min(tn, f)
    kw = FFN_CONV_W
    nt = s // ts
    hb = ts // SUBLANES
    last8 = s // SUBLANES - 1

    def body(up_ref, hb_ref, ha_ref, da_ref, dah_ref, w_ref, b_ref, dup_ref, dw_ref, db_ref):
        i = pl.program_id(1)
        first, last = i == 0, i == nt - 1
        ce, xs = [], []
        for h in range(2):
            x = up_ref[h]
            before = jnp.where(first, 0.0, hb_ref[h])
            after = ha_ref[h]
            ext = jnp.concatenate([before, x, after], axis=0)
            c = b_ref[h] + w_ref[h, kw - 1:kw, :] * ext
            shifted = [ext]
            for sft in range(1, kw):
                sh = pltpu.roll(ext, sft, 0)
                shifted.append(sh)
                c = c + w_ref[h, kw - 1 - sft:kw - sft, :] * sh
            ce.append(c[SUBLANES:])
            xs.append([sh[SUBLANES:SUBLANES + ts] for sh in shifted])
        da = jnp.concatenate([da_ref[...], jnp.where(last, 0.0, dah_ref[...])], axis=0)
        g1, dg1 = _gelu_and_grad(ce[0])
        dcs = [da * ce[1] * dg1, da * g1]
        for h in range(2):
            dc = dcs[h]
            n = dc.shape[0]
            dup = w_ref[h, kw - 1:kw, :] * dc[:ts]
            for sft in range(1, kw):
                dup = dup + w_ref[h, kw - 1 - sft:kw - sft, :] * pltpu.roll(dc, n - sft, 0)[:ts]
            dup_ref[h] = dup.astype(BF16)
            dct = dc[:ts]
            dbp = jnp.sum(dct, axis=0, keepdims=True)
            dwp = [jnp.sum(dct * xs[h][kw - 1 - k], axis=0, keepdims=True) for k in range(kw)]

            @pl.when(first)
            def _():
                db_ref[h] = dbp
                for k in range(kw):
                    dw_ref[h, k:k + 1, :] = dwp[k]

            @pl.when(i > 0)
            def _():
                db_ref[h] += dbp
                for k in range(kw):
                    dw_ref[h, k:k + 1, :] += dwp[k]

    g0 = f // tn
    return _call_with_side(
        body, side, lambda: (pl.program_id(0) == 0) & (pl.program_id(1) == 0),
        lambda: (pl.program_id(0) == g0 - 1) & (pl.program_id(1) == nt - 1),
        name="ffn_bwd", grid=(g0, nt),
        in_specs=[pl.BlockSpec((2, ts, tn), lambda j, i: (0, i, j)),
                  pl.BlockSpec((2, SUBLANES, tn), lambda j, i: (0, jnp.maximum(i * hb - 1, 0), j)),
                  pl.BlockSpec((2, SUBLANES, tn), lambda j, i: (0, jnp.minimum((i + 1) * hb, last8), j)),
                  pl.BlockSpec((ts, tn), lambda j, i: (i, j)),
                  pl.BlockSpec((SUBLANES, tn), lambda j, i: (jnp.minimum((i + 1) * hb, last8), j)),
                  pl.BlockSpec((2, kw, tn), lambda j, i: (0, 0, j)),
                  pl.BlockSpec((2, 1, tn), lambda j, i: (0, 0, j))],
        out_specs=[pl.BlockSpec((2, ts, tn), lambda j, i: (0, i, j)),
                   pl.BlockSpec((2, kw, tn), lambda j, i: (0, 0, j)),
                   pl.BlockSpec((2, 1, tn), lambda j, i: (0, 0, j))],
        out_shape=[jax.ShapeDtypeStruct((2, s, f), BF16), jax.ShapeDtypeStruct((2, kw, f), F32),
                   jax.ShapeDtypeStruct((2, 1, f), F32)],
        scratch_shapes=[], args=(up2, up2, up2, dact, dact, conv_w2, conv_b2))


def _mm_rms_bwd(name, a, b, h, g, dh_in, *, tm, tk):
    (m, kk), (n, kb) = a.shape, b.shape
    assert kk == kb and h.shape == (m, n), (name, a.shape, b.shape, h.shape)
    tm, tk = min(tm, m), min(tk, kk)
    nk = kk // tk
    dims = (((1,), (1,)), ((), ()))

    def body(a_ref, b_ref, h_ref, g_ref, dhin_ref, dh_ref, dg_ref, *acc):
        i, k = pl.program_id(0), pl.program_id(2)
        prod = lax.dot_general(a_ref[...].astype(BF16), b_ref[...].astype(BF16), dims, preferred_element_type=F32)

        def finish(dhn):
            x = h_ref[...]
            rstd = lax.rsqrt(jnp.mean(x * x, axis=-1, keepdims=True) + NORM_EPS)
            xhat = x * rstd
            dxh = dhn * g_ref[...]
            dh_ref[...] = dhin_ref[...] + rstd * (dxh - xhat * jnp.mean(dxh * xhat, axis=-1, keepdims=True))
            part = jnp.sum(dhn * xhat, axis=0, keepdims=True)

            @pl.when(i == 0)
            def _():
                dg_ref[...] = part

            @pl.when(i > 0)
            def _():
                dg_ref[...] += part

        if nk == 1:
            finish(prod)
        else:
            acc_ref = acc[0]

            @pl.when(k == 0)
            def _():
                acc_ref[...] = prod

            @pl.when(k > 0)
            def _():
                acc_ref[...] += prod

            @pl.when(k == nk - 1)
            def _():
                finish(acc_ref[...])

    row = pl.BlockSpec((tm, n), lambda i, j, k: (i, 0))
    vec = pl.BlockSpec((1, n), lambda i, j, k: (0, 0))
    return pl.pallas_call(
        body, name=name, grid=(m // tm, 1, nk),
        in_specs=[a.spec(tm, tk, lambda i, j, k: (i, k)), b.spec(n, tk, lambda i, j, k: (0, k)), row, vec, row],
        out_specs=[row, vec], out_shape=[jax.ShapeDtypeStruct((m, n), F32), jax.ShapeDtypeStruct((1, n), F32)],
        scratch_shapes=[pltpu.VMEM((tm, n), F32)] if nk > 1 else [], compiler_params=_cparams(),
    )(a.arr, b.arr, h, g, dh_in)


def _ffn_up_act(hn2, w_up4, conv_w2, conv_b2, ts=1024, tn=512, sub=1024, side=None):
    s, d = hn2.shape
    p, _, wc = w_up4.shape
    f = p * wc // 2
    ts, tn = min(ts, s), min(tn, wc)
    sub = min(sub, ts)
    per = wc // tn
    kw = FFN_CONV_W
    g0, g1 = f // tn, s // ts

    def body(hn_ref, w1_ref, w2_ref, cw_ref, cb_ref, up_ref, c_ref, act_ref, carry_ref):
        @pl.when(pl.program_id(1) == 0)
        def _():
            carry_ref[...] = jnp.zeros_like(carry_ref)

        for q in range(ts // sub):
            rows = slice(q * sub, (q + 1) * sub)
            hn = hn_ref[rows, :]
            cs = []
            for h, w_ref in enumerate((w1_ref, w2_ref)):
                x = jnp.dot(hn, w_ref[...], preferred_element_type=F32)
                up_ref[h, rows, :] = x
                halo = carry_ref[h]
                c = cb_ref[h] + cw_ref[h, kw - 1:kw, :] * x
                for sft in range(1, kw):
                    c = c + cw_ref[h, kw - 1 - sft:kw - sft, :] * _shift_down(x, halo, sft)
                carry_ref[h] = x[sub - SUBLANES:, :]
                c_ref[h, rows, :] = c
                cs.append(c)
            act_ref[rows, :] = (_gelu(cs[0]) * cs[1]).astype(BF16)

    outs, side_outs = _call_with_side(
        body, side, lambda: (pl.program_id(0) == 0) & (pl.program_id(1) == 0),
        lambda: (pl.program_id(0) == g0 - 1) & (pl.program_id(1) == g1 - 1),
        name="ffn_up_act", grid=(g0, g1),
        in_specs=[pl.BlockSpec((ts, d), lambda j, i: (i, 0)),
                  pl.BlockSpec((None, d, tn), lambda j, i: (j // per, 0, j % per)),
                  pl.BlockSpec((None, d, tn), lambda j, i: (p // 2 + j // per, 0, j % per)),
                  pl.BlockSpec((2, kw, tn), lambda j, i: (0, 0, j)),
                  pl.BlockSpec((2, 1, tn), lambda j, i: (0, 0, j))],
        out_specs=[pl.BlockSpec((2, ts, tn), lambda j, i: (0, i, j)), pl.BlockSpec((2, ts, tn), lambda j, i: (0, i, j)),
                   pl.BlockSpec((ts, tn), lambda j, i: (i, j))],
        out_shape=[jax.ShapeDtypeStruct((2, s, f), F32), jax.ShapeDtypeStruct((2, s, f), F32), jax.ShapeDtypeStruct((s, f), BF16)],
        scratch_shapes=[pltpu.VMEM((2, SUBLANES, tn), F32)], args=(hn2, w_up4, w_up4, conv_w2, conv_b2))
    return outs, side_outs


def _ffn_bwd_fused(dh, w_down, up2, c2, conv_w2, ts=1024, tn=512, side=None):
    s, d = dh.shape
    _, _, f = up2.shape
    ts, tn = min(ts, s), min(tn, f)
    kw = FFN_CONV_W
    nt = s // ts
    hb = ts // SUBLANES
    g0 = f // tn
    nt_dims = (((1,), (1,)), ((), ()))

    def body(dh_ref, wd_ref, up_ref, c_ref, w_ref, dup_ref, dw_ref, db_ref, carry_ref):
        i = pl.program_id(1)
        first_step = i == 0

        @pl.when(first_step)
        def _():
            carry_ref[...] = jnp.zeros_like(carry_ref)

        da = lax.dot_general(dh_ref[...].astype(BF16), wd_ref[...], nt_dims, preferred_element_type=F32)
        g1, dg1 = _gelu_and_grad(c_ref[0])
        dcs = [da * c_ref[1] * dg1, da * g1]
        for h in range(2):
            dc = dcs[h]
            after = carry_ref[h]
            ups = [dc] + [_shift_up(dc, after, sft) for sft in range(1, kw)]
            dup = w_ref[h, kw - 1:kw, :] * dc
            for sft in range(1, kw):
                dup = dup + w_ref[h, kw - 1 - sft:kw - sft, :] * ups[sft]
            carry_ref[h] = dc[:SUBLANES]
            dup_ref[h] = dup.astype(BF16)
            dbp = jnp.sum(dc, axis=0, keepdims=True)
            x = up_ref[h]
            dwp = [jnp.sum(ups[kw - 1 - k] * x, axis=0, keepdims=True) for k in range(kw)]

            @pl.when(first_step)
            def _():
                db_ref[h] = dbp
                for k in range(kw):
                    dw_ref[h, k:k + 1, :] = dwp[k]

            @pl.when(i > 0)
            def _():
                db_ref[h] += dbp
                for k in range(kw):
                    dw_ref[h, k:k + 1, :] += dwp[k]

    rev = lambda i: nt - 1 - i
    return _call_with_side(
        body, side, lambda: (pl.program_id(0) == 0) & (pl.program_id(1) == 0),
        lambda: (pl.program_id(0) == g0 - 1) & (pl.program_id(1) == nt - 1),
        name="ffn_bwd", grid=(g0, nt),
        in_specs=[pl.BlockSpec((ts, d), lambda j, i: (rev(i), 0)),
                  pl.BlockSpec((tn, d), lambda j, i: (j, 0)),
                  pl.BlockSpec((2, ts, tn), lambda j, i: (0, rev(i), j)),
                  pl.BlockSpec((2, ts, tn), lambda j, i: (0, rev(i), j)),
                  pl.BlockSpec((2, kw, tn), lambda j, i: (0, 0, j))],
        out_specs=[pl.BlockSpec((2, ts, tn), lambda j, i: (0, rev(i), j)),
                   pl.BlockSpec((2, kw, tn), lambda j, i: (0, 0, j)),
                   pl.BlockSpec((2, 1, tn), lambda j, i: (0, 0, j))],
        out_shape=[jax.ShapeDtypeStruct((2, s, f), BF16), jax.ShapeDtypeStruct((2, kw, f), F32),
                   jax.ShapeDtypeStruct((2, 1, f), F32)],
        scratch_shapes=[pltpu.VMEM((2, SUBLANES, tn), F32)], args=(dh, w_down, up2, c2, conv_w2))


def _ffn_fwd(h, g, w_up4, w_down, conv_w2, conv_b2, ts=512, tn=512, sub=256, side=None):
    s, d = h.shape
    p, _, wc = w_up4.shape
    f = p * wc // 2
    ts, tn = min(ts, s), min(tn, wc)
    sub = min(sub, ts)
    per = wc // tn
    kw = FFN_CONV_W
    g0, g1 = s // ts, f // tn

    def body(h_ref, g_ref, w1_ref, w2_ref, wd_ref, cw_ref, cb_ref, ho_ref, hn_ref, up_ref, c_ref, act_ref, carry_ref):
        i, j = pl.program_id(0), pl.program_id(1)

        @pl.when(j == 0)
        def _():
            x = h_ref[...]
            var = jnp.mean(x * x, axis=-1, keepdims=True)
            hn_ref[...] = (x * lax.rsqrt(var + NORM_EPS) * g_ref[...]).astype(BF16)
            ho_ref[...] = x

        @pl.when(i == 0)
        def _():
            carry_ref[j] = jnp.zeros(carry_ref.shape[1:], F32)

        for q in range(ts // sub):
            rows = slice(q * sub, (q + 1) * sub)
            hn = hn_ref[rows, :]
            cs = []
            for hf, w_ref in enumerate((w1_ref, w2_ref)):
                x = jnp.dot(hn, w_ref[...], preferred_element_type=F32)
                up_ref[hf, rows, :] = x
                halo = carry_ref[j, hf]
                c = cb_ref[hf] + cw_ref[hf, kw - 1:kw, :] * x
                for sft in range(1, kw):
                    c = c + cw_ref[hf, kw - 1 - sft:kw - sft, :] * _shift_down(x, halo, sft)
                carry_ref[j, hf] = x[sub - SUBLANES:, :]
                c_ref[hf, rows, :] = c
                cs.append(c)
            a = (_gelu(cs[0]) * cs[1]).astype(BF16)
            act_ref[rows, :] = a
            ho_ref[rows, :] += jnp.dot(a, wd_ref[...], preferred_element_type=F32)

    row = pl.BlockSpec((ts, d), lambda i, j: (i, 0))
    col2 = pl.BlockSpec((2, ts, tn), lambda i, j: (0, i, j))
    return _call_with_side(
        body, side, lambda: (pl.program_id(0) == 0) & (pl.program_id(1) == 0),
        lambda: (pl.program_id(0) == g0 - 1) & (pl.program_id(1) == g1 - 1),
        name="ffn_fwd", grid=(g0, g1),
        in_specs=[row, pl.BlockSpec((1, d), lambda i, j: (0, 0)),
                  pl.BlockSpec((None, d, tn), lambda i, j: (j // per, 0, j % per)),
                  pl.BlockSpec((None, d, tn), lambda i, j: (p // 2 + j // per, 0, j % per)),
                  pl.BlockSpec((tn, d), lambda i, j: (j, 0)),
                  pl.BlockSpec((2, kw, tn), lambda i, j: (0, 0, j)),
                  pl.BlockSpec((2, 1, tn), lambda i, j: (0, 0, j))],
        out_specs=[row, row, col2, col2, pl.BlockSpec((ts, tn), lambda i, j: (i, j))],
        out_shape=[jax.ShapeDtypeStruct((s, d), F32), jax.ShapeDtypeStruct((s, d), BF16), jax.ShapeDtypeStruct((2, s, f), F32),
                   jax.ShapeDtypeStruct((2, s, f), F32), jax.ShapeDtypeStruct((s, f), BF16)],
        scratch_shapes=[pltpu.VMEM((g1, 2, SUBLANES, tn), F32)],
        args=(h, g, w_up4, w_up4, w_down, conv_w2, conv_b2))


def _ffn_bwd_all(dh, w_down, up2, c2, hn, act_, conv_w2, ts=256, tn=512, sub=128, side=None):
    s, d = dh.shape
    _, _, f = up2.shape
    ts, tn = min(ts, s), min(tn, f)
    sub = min(sub, ts)
    kw = FFN_CONV_W
    nt = s // ts
    g0 = f // tn
    nt_dims = (((1,), (1,)), ((), ()))
    tn_dims = (((0,), (0,)), ((), ()))

    def body(dh_ref, wd_ref, up_ref, c_ref, hn_ref, act_ref, w_ref, dup_ref, dw_ref, db_ref, dwu_ref, dwd_ref,
             carry_ref, dwu_acc, dwd_acc):
        i = pl.program_id(1)
        first_step = i == 0

        @pl.when(first_step)
        def _():
            carry_ref[...] = jnp.zeros_like(carry_ref)
            dwu_acc[...] = jnp.zeros_like(dwu_acc)
            dwd_acc[...] = jnp.zeros_like(dwd_acc)
            dw_ref[...] = jnp.zeros_like(dw_ref)
            db_ref[...] = jnp.zeros_like(db_ref)

        dhb = dh_ref[...].astype(BF16)
        da_all = lax.dot_general(dhb, wd_ref[...], nt_dims, preferred_element_type=F32)
        for q in reversed(range(ts // sub)):
            rows = slice(q * sub, (q + 1) * sub)
            da = da_all[rows, :]
            g1, dg1 = _gelu_and_grad(c_ref[0, rows, :])
            dcs = [da * c_ref[1, rows, :] * dg1, da * g1]
            hnq = hn_ref[rows, :]
            for hf in range(2):
                dc = dcs[hf]
                after = carry_ref[hf]
                ups = [dc] + [_shift_up(dc, after, sft) for sft in range(1, kw)]
                dup = w_ref[hf, kw - 1:kw, :] * dc
                for sft in range(1, kw):
                    dup = dup + w_ref[hf, kw - 1 - sft:kw - sft, :] * ups[sft]
                carry_ref[hf] = dc[:SUBLANES]
                dupb = dup.astype(BF16)
                dup_ref[hf, rows, :] = dupb
                dwu_acc[hf] += lax.dot_general(hnq, dupb, tn_dims, preferred_element_type=F32)
                db_ref[hf] += jnp.sum(dc, axis=0, keepdims=True)
                x = up_ref[hf, rows, :]
                for k in range(kw):
                    dw_ref[hf, k:k + 1, :] += jnp.sum(ups[kw - 1 - k] * x, axis=0, keepdims=True)
            dwd_acc[...] += lax.dot_general(act_ref[rows, :], dhb[rows, :], tn_dims, preferred_element_type=F32)

        @pl.when(i == nt - 1)
        def _():
            dwu_ref[...] = dwu_acc[...].astype(BF16)
            dwd_ref[...] = dwd_acc[...].astype(BF16)

    rev = lambda i: nt - 1 - i
    col2 = pl.BlockSpec((2, ts, tn), lambda j, i: (0, rev(i), j))
    return _call_with_side(
        body, side, lambda: (pl.program_id(0) == 0) & (pl.program_id(1) == 0),
        lambda: (pl.program_id(0) == g0 - 1) & (pl.program_id(1) == nt - 1),
        name="ffn_bwd", grid=(g0, nt),
        in_specs=[pl.BlockSpec((ts, d), lambda j, i: (rev(i), 0)),
                  pl.BlockSpec((tn, d), lambda j, i: (j, 0)),
                  col2, col2,
                  pl.BlockSpec((ts, d), lambda j, i: (rev(i), 0)),
                  pl.BlockSpec((ts, tn), lambda j, i: (rev(i), j)),
                  pl.BlockSpec((2, kw, tn), lambda j, i: (0, 0, j))],
        out_specs=[col2,
                   pl.BlockSpec((2, kw, tn), lambda j, i: (0, 0, j)),
                   pl.BlockSpec((2, 1, tn), lambda j, i: (0, 0, j)),
                   pl.BlockSpec((2, d, tn), lambda j, i: (0, 0, j)),
                   pl.BlockSpec((tn, d), lambda j, i: (j, 0))],
        out_shape=[jax.ShapeDtypeStruct((2, s, f), BF16), jax.ShapeDtypeStruct((2, kw, f), F32),
                   jax.ShapeDtypeStruct((2, 1, f), F32), jax.ShapeDtypeStruct((2, d, f), BF16), jax.ShapeDtypeStruct((f, d), BF16)],
        scratch_shapes=[pltpu.VMEM((2, SUBLANES, tn), F32), pltpu.VMEM((2, d, tn), F32), pltpu.VMEM((tn, d), F32)],
        args=(dh, w_down, up2, c2, hn, act_, conv_w2))


def _rg_gates(xr, wa_ref, ba_ref, wx_ref, bx_ref, lam_ref):
    bw = wa_ref.shape[-1]
    xb = xr.astype(BF16)
    za = jnp.concatenate([jnp.dot(xb[:, h * bw:(h + 1) * bw], wa_ref[h], preferred_element_type=F32)
                          for h in range(RG_HEADS)], axis=1) + ba_ref[...]
    zx = jnp.concatenate([jnp.dot(xb[:, h * bw:(h + 1) * bw], wx_ref[h], preferred_element_type=F32)
                          for h in range(RG_HEADS)], axis=1) + bx_ref[...]
    r, ig = _sigmoid(za), _sigmoid(zx)
    sp = _softplus(-lam_ref[...])
    la = -RG_C * r * sp
    a = jnp.exp(la)
    mult = jnp.sqrt(_neg_expm1(2.0 * la))
    return xb, r, ig, sp, a, mult


def _rg_fwd(xg2, conv_w, conv_b, w_a, b_a, w_x, b_x, lam, ts=256, side=None):
    _, s, c = xg2.shape
    ts = min(ts, s)
    kw = RG_CONV_W
    hb = ts // SUBLANES

    def body(xg_ref, halo_ref, cw_ref, cb_ref, wa_ref, ba_ref, wx_ref, bx_ref, lam_ref, xr_ref, hs_ref, y_ref, carry_ref,
             a_scr, b_scr):
        i = pl.program_id(0)

        @pl.when(i == 0)
        def _():
            carry_ref[...] = jnp.zeros_like(carry_ref)

        xp = xg_ref[0]
        halo = jnp.where(i == 0, 0.0, halo_ref[...])
        xr = cb_ref[...] + cw_ref[kw - 1:kw, :] * xp
        for sft in range(1, kw):
            xr = xr + cw_ref[kw - 1 - sft:kw - sft, :] * _shift_down(xp, halo, sft)
        _, r, ig, sp, a, mult = _rg_gates(xr, wa_ref, ba_ref, wx_ref, bx_ref, lam_ref)
        a_scr[...] = a
        b_scr[...] = mult * (ig * xr)
        _real_slab_scan(a_scr, b_scr, hs_ref, carry_ref, reverse=False)
        xr_ref[...] = xr
        y_ref[...] = (hs_ref[...] * _gelu(xg_ref[1])).astype(BF16)

    full = lambda shape: pl.BlockSpec(shape, lambda i: (0,) * len(shape))
    row_spec = pl.BlockSpec((ts, c), lambda i: (i, 0))
    nt = s // ts
    return _call_with_side(
        body, side, lambda: pl.program_id(0) == 0, lambda: pl.program_id(0) == nt - 1,
        name="rg_fwd", grid=(nt,),
        in_specs=[pl.BlockSpec((2, ts, c), lambda i: (0, i, 0)),
                  pl.BlockSpec((None, SUBLANES, c), lambda i: (0, jnp.maximum(i * hb - 1, 0), 0)),
                  full(conv_w.shape), full(conv_b.shape), full(w_a.shape), full(b_a.shape), full(w_x.shape), full(b_x.shape),
                  full(lam.shape)],
        out_specs=[row_spec, row_spec, row_spec],
        out_shape=[jax.ShapeDtypeStruct((s, c), F32), jax.ShapeDtypeStruct((s, c), F32), jax.ShapeDtypeStruct((s, c), BF16)],
        scratch_shapes=[pltpu.VMEM((1, c), F32), pltpu.VMEM((ts, c), F32), pltpu.VMEM((ts, c), F32)],
        args=(xg2, xg2, conv_w, conv_b, w_a, b_a, w_x, b_x, lam))


def _rg_bwd(dy, xg2, xr, hs, conv_w, w_a, b_a, w_x, b_x, lam, ts=256, side=None):
    _, s, c = xg2.shape
    ts = min(ts, s)
    nt = s // ts
    kw = RG_CONV_W
    hb = ts // SUBLANES
    bw = c // RG_HEADS
    tn_dims = (((0,), (0,)), ((), ()))
    nt_dims = (((1,), (1,)), ((), ()))

    def body(dy_ref, xg_ref, xph_ref, xr_ref, hs_ref, hsh_ref, cw_ref, wa_ref, ba_ref, wx_ref, bx_ref, lam_ref,
             dxg_ref, dcw_ref, dcb_ref, dwa_ref, dba_ref, dwx_ref, dbx_ref, dlam_ref,
             lam_carry, a_carry, dxr_carry, dsp_acc, a_scr, b_scr):
        i = pl.program_id(0)
        first_step = i == 0
        time_first = i == nt - 1

        @pl.when(first_step)
        def _():
            lam_carry[...] = jnp.zeros_like(lam_carry)
            a_carry[...] = jnp.ones_like(a_carry)
            dxr_carry[...] = jnp.zeros_like(dxr_carry)
            dsp_acc[...] = jnp.zeros_like(dsp_acc)
            for ref in (dcw_ref, dcb_ref, dwa_ref, dba_ref, dwx_ref, dbx_ref):
                ref[...] = jnp.zeros_like(ref)

        xr = xr_ref[...]
        hs = hs_ref[...]
        gate = xg_ref[1]
        xb, r, ig, sp, a, mult = _rg_gates(xr, wa_ref, ba_ref, wx_ref, bx_ref, lam_ref)
        dyv = dy_ref[...]
        gg, dgg = _gelu_and_grad(gate)
        dhs = dyv * gg
        dxg_ref[1] = (dyv * hs * dgg).astype(BF16)
        row = _rows(xr.shape)
        a_scr[...] = jnp.where(row == ts - 1, a_carry[0:1, :], pltpu.roll(a, ts - 1, 0))
        b_scr[...] = dhs
        _real_slab_scan(a_scr, b_scr, b_scr, lam_carry, reverse=True)
        lmb = b_scr[...]
        a_carry[...] = a[:SUBLANES]
        hs_prev = _shift_down(hs, jnp.where(time_first, 0.0, hsh_ref[...]), 1)
        d_a = lmb * hs_prev
        d_m = lmb * (ig * xr)
        d_ig = lmb * mult * xr
        d_xr = lmb * mult * ig
        d_la = a * d_a - (a * a / mult) * d_m
        dsp_acc[...] += jnp.sum(-RG_C * r * d_la, axis=0, keepdims=True)
        d_za = (-RG_C * sp) * d_la * r * (1.0 - r)
        d_zx = d_ig * ig * (1.0 - ig)
        dba_ref[...] += jnp.sum(d_za, axis=0, keepdims=True)
        dbx_ref[...] += jnp.sum(d_zx, axis=0, keepdims=True)
        dzab, dzxb = d_za.astype(BF16), d_zx.astype(BF16)
        back = []
        for h in range(RG_HEADS):
            sl = slice(h * bw, (h + 1) * bw)
            dwa_ref[h] += lax.dot_general(xb[:, sl], dzab[:, sl], tn_dims, preferred_element_type=F32)
            dwx_ref[h] += lax.dot_general(xb[:, sl], dzxb[:, sl], tn_dims, preferred_element_type=F32)
            back.append(lax.dot_general(dzab[:, sl], wa_ref[h], nt_dims, preferred_element_type=F32)
                        + lax.dot_general(dzxb[:, sl], wx_ref[h], nt_dims, preferred_element_type=F32))
        d_xr = d_xr + jnp.concatenate(back, axis=1)
        d_xp = cw_ref[kw - 1:kw, :] * d_xr
        after = dxr_carry[...]
        for sft in range(1, kw):
            d_xp = d_xp + cw_ref[kw - 1 - sft:kw - sft, :] * _shift_up(d_xr, after, sft)
        dxr_carry[...] = d_xr[:SUBLANES]
        dxg_ref[0] = d_xp.astype(BF16)
        xp = xg_ref[0]
        before = jnp.where(time_first, 0.0, xph_ref[...])
        dcb_ref[...] += jnp.sum(d_xr, axis=0, keepdims=True)
        dcw_ref[kw - 1:kw, :] += jnp.sum(d_xr * xp, axis=0, keepdims=True)
        for sft in range(1, kw):
            dcw_ref[kw - 1 - sft:kw - sft, :] += jnp.sum(d_xr * _shift_down(xp, before, sft), axis=0, keepdims=True)
        dlam_ref[...] = dsp_acc[...] * (-_sigmoid(-lam_ref[...]))

    full = lambda shape: pl.BlockSpec(shape, lambda i: (0,) * len(shape))
    rev = lambda i: nt - 1 - i
    row_spec = pl.BlockSpec((ts, c), lambda i: (rev(i), 0))
    halo_idx = lambda i: jnp.maximum(rev(i) * hb - 1, 0)
    vec = (1, c)
    return _call_with_side(
        body, side, lambda: pl.program_id(0) == 0, lambda: pl.program_id(0) == nt - 1,
        name="rg_bwd", grid=(nt,),
        in_specs=[row_spec,
                  pl.BlockSpec((2, ts, c), lambda i: (0, rev(i), 0)),
                  pl.BlockSpec((None, SUBLANES, c), lambda i: (0, halo_idx(i), 0)),
                  row_spec, row_spec,
                  pl.BlockSpec((SUBLANES, c), lambda i: (halo_idx(i), 0)),
                  full(conv_w.shape), full(w_a.shape), full(b_a.shape), full(w_x.shape), full(b_x.shape), full(lam.shape)],
        out_specs=[pl.BlockSpec((2, ts, c), lambda i: (0, rev(i), 0)), full(conv_w.shape), full(vec), full(w_a.shape), full(vec),
                   full(w_x.shape), full(vec), full(vec)],
        out_shape=[jax.ShapeDtypeStruct((2, s, c), BF16), jax.ShapeDtypeStruct(conv_w.shape, F32), jax.ShapeDtypeStruct(vec, F32),
                   jax.ShapeDtypeStruct(w_a.shape, F32), jax.ShapeDtypeStruct(vec, F32), jax.ShapeDtypeStruct(w_x.shape, F32),
                   jax.ShapeDtypeStruct(vec, F32), jax.ShapeDtypeStruct(vec, F32)],
        scratch_shapes=[pltpu.VMEM(vec, F32), pltpu.VMEM((SUBLANES, c), F32), pltpu.VMEM((SUBLANES, c), F32),
                        pltpu.VMEM(vec, F32), pltpu.VMEM((ts, c), F32), pltpu.VMEM((ts, c), F32)],
        args=(dy, xg2, xg2, xr, hs, hs, conv_w, w_a, b_a, w_x, b_x, lam))


def _s5_param_fn(a_re, a_im, log_dt, bt_re, bt_im):
    dt = jnp.exp(log_dt)
    mag = jnp.exp(a_re * dt)
    abr = mag * jnp.cos(a_im * dt)
    abi = mag * jnp.sin(a_im * dt)
    ur, ui = abr - 1.0, abi
    den = a_re * a_re + a_im * a_im
    wr = (ur * a_re + ui * a_im) / den
    wi = (ui * a_re - ur * a_im) / den
    bbr = wr[None] * bt_re - wi[None] * bt_im
    bbi = wr[None] * bt_im + wi[None] * bt_re
    return abr, abi, bbr, bbi


def _s5_params(a_re, a_im, log_dt, bt_re, bt_im, nlev):
    g, p = a_re.shape
    gc = bt_re.shape[0]

    def body(ar_ref, ai_ref, dt_ref, br_ref, bi_ref, abr_ref, abi_ref, pr_ref, pi_ref, bbr_ref, bbi_ref):
        abr, abi, bbr, bbi = _s5_param_fn(ar_ref[...], ai_ref[...], dt_ref[...], br_ref[...], bi_ref[...])
        abr_ref[...] = abr
        abi_ref[...] = abi
        bbr_ref[...] = bbr
        bbi_ref[...] = bbi
        qr, qi = abr, abi
        for k in range(nlev):
            pr_ref[k] = qr
            pi_ref[k] = qi
            qr, qi = qr * qr - qi * qi, 2.0 * qr * qi

    sd = jax.ShapeDtypeStruct
    return pl.pallas_call(
        body, name="s5_params",
        out_shape=[sd((g, p), F32), sd((g, p), F32), sd((nlev, g, p), F32), sd((nlev, g, p), F32), sd((gc, g, p), F32),
                   sd((gc, g, p), F32)],
    )(a_re, a_im, log_dt, bt_re, bt_im)


def _s5_params_bwd(a_re, a_im, log_dt, bt_re, bt_im, d_abr, d_abi, d_bbr, d_bbi):
    def body(ar_ref, ai_ref, dt_ref, br_ref, bi_ref, g0, g1, g2, g3, o0, o1, o2, o3, o4):
        _, vjp = jax.vjp(_s5_param_fn, ar_ref[...], ai_ref[...], dt_ref[...], br_ref[...], bi_ref[...])
        outs = vjp((g0[...], g1[...], g2[...], g3[...]))
        for o, v in zip((o0, o1, o2, o3, o4), outs):
            o[...] = v

    sd = jax.ShapeDtypeStruct
    return pl.pallas_call(
        body, name="s5_params_bwd",
        out_shape=[sd(a_re.shape, F32), sd(a_im.shape, F32), sd(log_dt.shape, F32), sd(bt_re.shape, F32), sd(bt_im.shape, F32)],
    )(a_re, a_im, log_dt, bt_re, bt_im, d_abr, d_abi, d_bbr, d_bbi)


def _s5_fwd(u, abr, abi, pw_r, pw_i, bp_r, bp_i, cp_r, cp_i, dvec, ts=128, side=None):
    s, c = u.shape
    n = abr.shape[1]
    nblk, cb, nb = bp_r.shape
    ts = min(ts, s)

    def body(u_ref, ar_ref, ai_ref, pr_ref, pi_ref, bpr_ref, bpi_ref, cpr_ref, cpi_ref, d_ref,
             hr_ref, hi_ref, yp_ref, gy_ref, car_r, car_i):
        i = pl.program_id(0)

        @pl.when(i == 0)
        def _():
            car_r[...] = jnp.zeros_like(car_r)
            car_i[...] = jnp.zeros_like(car_i)

        uv = u_ref[...]
        ub = uv.astype(BF16)
        br = jnp.concatenate([jnp.dot(ub[:, k * cb:(k + 1) * cb], bpr_ref[k], preferred_element_type=F32) for k in range(nblk)], axis=1)
        bi = jnp.concatenate([jnp.dot(ub[:, k * cb:(k + 1) * cb], bpi_ref[k], preferred_element_type=F32) for k in range(nblk)], axis=1)
        ar, ai = ar_ref[...], ai_ref[...]
        pr, pi_ = car_r[SUBLANES - 1:SUBLANES, :], car_i[SUBLANES - 1:SUBLANES, :]
        row = _rows(br.shape)
        br = br + jnp.where(row == 0, ar * pr - ai * pi_, 0.0)
        bi = bi + jnp.where(row == 0, ar * pi_ + ai * pr, 0.0)
        hr, hi = _scan_cplx(br, bi, pr_ref, pi_ref, reverse=False)
        car_r[...] = hr[ts - SUBLANES:]
        car_i[...] = hi[ts - SUBLANES:]
        hr_ref[...] = hr
        hi_ref[...] = hi
        hrb, hib = hr.astype(BF16), hi.astype(BF16)
        y = jnp.concatenate([jnp.dot(hrb[:, k * nb:(k + 1) * nb], cpr_ref[k], preferred_element_type=F32)
                             - jnp.dot(hib[:, k * nb:(k + 1) * nb], cpi_ref[k], preferred_element_type=F32) for k in range(nblk)], axis=1)
        yp = y + d_ref[...] * uv
        yp_ref[...] = yp
        gy_ref[...] = _gelu(yp).astype(BF16)

    full = lambda shape: pl.BlockSpec(shape, lambda i: (0,) * len(shape))
    rc = pl.BlockSpec((ts, c), lambda i: (i, 0))
    rn = pl.BlockSpec((ts, n), lambda i: (i, 0))
    sd = jax.ShapeDtypeStruct
    nt = s // ts
    return _call_with_side(
        body, side, lambda: pl.program_id(0) == 0, lambda: pl.program_id(0) == nt - 1,
        name="s5_fwd", grid=(nt,),
        in_specs=[rc, full(abr.shape), full(abi.shape), full(pw_r.shape), full(pw_i.shape), full(bp_r.shape), full(bp_i.shape),
                  full(cp_r.shape), full(cp_i.shape), full(dvec.shape)],
        out_specs=[rn, rn, rc, rc],
        out_shape=[sd((s, n), F32), sd((s, n), F32), sd((s, c), F32), sd((s, c), BF16)],
        scratch_shapes=[pltpu.VMEM((SUBLANES, n), F32), pltpu.VMEM((SUBLANES, n), F32)],
        args=(u, abr, abi, pw_r, pw_i, bp_r, bp_i, cp_r, cp_i, dvec))


def _s5_bwd(dgy, ypre, u, hr, hi, abr, abi, pw_r, pw_i, bp_r, bp_i, cp_r, cp_i, dvec, ts=128, side=None):
    s, c = u.shape
    n = abr.shape[1]
    nblk, cb, nb = bp_r.shape
    ts = min(ts, s)
    nt = s // ts
    hb = ts // SUBLANES
    tn_dims = (((0,), (0,)), ((), ()))
    nt_dims = (((1,), (1,)), ((), ()))

    def body(dgy_ref, yp_ref, u_ref, hr_ref, hi_ref, hrh_ref, hih_ref, ar_ref, ai_ref, pr_ref, pi_ref, bpr_ref, bpi_ref,
             cpr_ref, cpi_ref, d_ref,
             du_ref, dar_ref, dai_ref, dbr_ref, dbi_ref, dcr_ref, dci_ref, dd_ref, car_r, car_i, npi_ref):
        i = pl.program_id(0)
        time_first = i == nt - 1

        @pl.when(i == 0)
        def _():
            car_r[...] = jnp.zeros_like(car_r)
            car_i[...] = jnp.zeros_like(car_i)
            npi_ref[...] = -pi_ref[...]
            for ref in (dar_ref, dai_ref, dbr_ref, dbi_ref, dcr_ref, dci_ref, dd_ref):
                ref[...] = jnp.zeros_like(ref)

        uv = u_ref[...]
        _, dgel = _gelu_and_grad(yp_ref[...])
        dyv = dgy_ref[...] * dgel
        dd_ref[...] += jnp.sum(dyv * uv, axis=0, keepdims=True)
        dyb = dyv.astype(BF16)
        hr, hi = hr_ref[...], hi_ref[...]
        hrb, hib = hr.astype(BF16), hi.astype(BF16)
        dhr, dhi = [], []
        for k in range(nblk):
            dblk = dyb[:, k * cb:(k + 1) * cb]
            dhr.append(lax.dot_general(dblk, cpr_ref[k], nt_dims, preferred_element_type=F32))
            dhi.append(-lax.dot_general(dblk, cpi_ref[k], nt_dims, preferred_element_type=F32))
            dcr_ref[k] += lax.dot_general(hrb[:, k * nb:(k + 1) * nb], dblk, tn_dims, preferred_element_type=F32)
            dci_ref[k] += lax.dot_general(hib[:, k * nb:(k + 1) * nb], dblk, tn_dims, preferred_element_type=F32)
        dhr = jnp.concatenate(dhr, axis=1)
        dhi = jnp.concatenate(dhi, axis=1)
        ar, ai = ar_ref[...], ai_ref[...]
        nr, ni = car_r[0:1, :], car_i[0:1, :]
        row = _rows(dhr.shape)
        dhr = dhr + jnp.where(row == ts - 1, ar * nr + ai * ni, 0.0)
        dhi = dhi + jnp.where(row == ts - 1, ar * ni - ai * nr, 0.0)
        lr, li = _scan_cplx(dhr, dhi, pr_ref, npi_ref, reverse=True)
        car_r[...] = lr[:SUBLANES]
        car_i[...] = li[:SUBLANES]
        hpr = _shift_down(hr, jnp.where(time_first, 0.0, hrh_ref[...]), 1)
        hpi = _shift_down(hi, jnp.where(time_first, 0.0, hih_ref[...]), 1)
        dar_ref[...] += jnp.sum(lr * hpr + li * hpi, axis=0, keepdims=True)
        dai_ref[...] += jnp.sum(li * hpr - lr * hpi, axis=0, keepdims=True)
        lrb, lib = lr.astype(BF16), li.astype(BF16)
        ub = uv.astype(BF16)
        du = []
        for k in range(nblk):
            ublk = ub[:, k * cb:(k + 1) * cb]
            lrk, lik = lrb[:, k * nb:(k + 1) * nb], lib[:, k * nb:(k + 1) * nb]
            dbr_ref[k] += lax.dot_general(ublk, lrk, tn_dims, preferred_element_type=F32)
            dbi_ref[k] += lax.dot_general(ublk, lik, tn_dims, preferred_element_type=F32)
            du.append(lax.dot_general(lrk, bpr_ref[k], nt_dims, preferred_element_type=F32)
                      + lax.dot_general(lik, bpi_ref[k], nt_dims, preferred_element_type=F32))
        du_ref[...] = (d_ref[...] * dyv + jnp.concatenate(du, axis=1)).astype(BF16)

    full = lambda shape: pl.BlockSpec(shape, lambda i: (0,) * len(shape))
    rev = lambda i: nt - 1 - i
    halo_idx = lambda i: jnp.maximum(rev(i) * hb - 1, 0)
    rc = pl.BlockSpec((ts, c), lambda i: (rev(i), 0))
    rn = pl.BlockSpec((ts, n), lambda i: (rev(i), 0))
    hn = pl.BlockSpec((SUBLANES, n), lambda i: (halo_idx(i), 0))
    sd = jax.ShapeDtypeStruct
    return _call_with_side(
        body, side, lambda: pl.program_id(0) == 0, lambda: pl.program_id(0) == nt - 1,
        name="s5_bwd", grid=(nt,),
        in_specs=[rc, rc, rc, rn, rn, hn, hn, full(abr.shape), full(abi.shape), full(pw_r.shape), full(pw_i.shape),
                  full(bp_r.shape), full(bp_i.shape), full(cp_r.shape), full(cp_i.shape), full(dvec.shape)],
        out_specs=[rc, full(abr.shape), full(abi.shape), full(bp_r.shape), full(bp_i.shape), full(cp_r.shape), full(cp_i.shape),
                   full(dvec.shape)],
        out_shape=[sd((s, c), BF16), sd(abr.shape, F32), sd(abi.shape, F32), sd(bp_r.shape, F32), sd(bp_i.shape, F32),
                   sd(cp_r.shape, F32), sd(cp_i.shape, F32), sd(dvec.shape, F32)],
        scratch_shapes=[pltpu.VMEM((SUBLANES, n), F32), pltpu.VMEM((SUBLANES, n), F32), pltpu.VMEM(pw_i.shape, F32)],
        args=(dgy, ypre, u, hr, hi, hr, hi, abr, abi, pw_r, pw_i, bp_r, bp_i, cp_r, cp_i, dvec))


S5_LANE_CHUNK = 512


def _s5_tables(a_re, a_im, log_dt, bt_re, bt_im):
    g, p = a_re.shape
    gc = bt_re.shape[0]

    def body(ar_ref, ai_ref, dt_ref, br_ref, bi_ref, abr_ref, abi_ref, tr_ref, ti_ref, bbr_ref, bbi_ref):
        abr, abi, bbr, bbi = _s5_param_fn(ar_ref[...], ai_ref[...], dt_ref[...], br_ref[...], bi_ref[...])
        abr_ref[...] = abr
        abi_ref[...] = abi
        bbr_ref[...] = bbr
        bbi_ref[...] = bbi
        pows = [(abr, abi)]
        for _ in range(1, SUBLANES):
            qr, qi = pows[-1]
            pows.append((qr * abr - qi * abi, qr * abi + qi * abr))
        zero = jnp.zeros_like(abr)
        for r in range(SUBLANES):
            for k in range(3):
                sh = 1 << k
                tr_ref[k, r] = pows[sh - 1][0] if r >= sh else zero
                ti_ref[k, r] = pows[sh - 1][1] if r >= sh else zero
            tr_ref[3, r] = pows[r][0]
            ti_ref[3, r] = pows[r][1]

    sd = jax.ShapeDtypeStruct
    return pl.pallas_call(
        body, name="s5_tables",
        out_shape=[sd((g, p), F32), sd((g, p), F32), sd((4, SUBLANES, g, p), F32), sd((4, SUBLANES, g, p), F32),
                   sd((gc, g, p), F32), sd((gc, g, p), F32)],
    )(a_re, a_im, log_dt, bt_re, bt_im)


def _cmul_add(br, bi, tr, ti, sr, si):
    return br + tr * sr - ti * si, bi + tr * si + ti * sr


def _s5_fwd2(u, tab_r, tab_i, bp_r, bp_i, cp_r, cp_i, dvec, ts=256, side=None):
    s, c = u.shape
    n = tab_r.shape[2]
    nblk, cb, nb = bp_r.shape
    ts = min(ts, s)
    nsl = ts // SUBLANES
    lc = min(S5_LANE_CHUNK, n)

    def body(u_ref, tr_ref, ti_ref, bpr_ref, bpi_ref, cpr_ref, cpi_ref, d_ref, hr_ref, hi_ref, yp_ref, gy_ref,
             bur_ref, bui_ref, car_r, car_i):
        i = pl.program_id(0)

        @pl.when(i == 0)
        def _():
            car_r[...] = jnp.zeros_like(car_r)
            car_i[...] = jnp.zeros_like(car_i)

        uv = u_ref[...]
        ub = uv.astype(BF16)
        for k in range(nblk):
            bur_ref[:, k * nb:(k + 1) * nb] = jnp.dot(ub[:, k * cb:(k + 1) * cb], bpr_ref[k], preferred_element_type=F32)
            bui_ref[:, k * nb:(k + 1) * nb] = jnp.dot(ub[:, k * cb:(k + 1) * cb], bpi_ref[k], preferred_element_type=F32)
        for q in range(n // lc):
            sl = slice(q * lc, (q + 1) * lc)
            tabs = [(tr_ref[k, :, sl], ti_ref[k, :, sl]) for k in range(4)]

            def slab(j, carry, sl=sl, tabs=tabs):
                cr, ci = carry
                r0 = pl.multiple_of(j * SUBLANES, SUBLANES)
                br, bi = bur_ref[pl.ds(r0, SUBLANES), sl], bui_ref[pl.ds(r0, SUBLANES), sl]
                for k in range(3):
                    sh = 1 << k
                    br, bi = _cmul_add(br, bi, tabs[k][0], tabs[k][1], pltpu.roll(br, sh, 0), pltpu.roll(bi, sh, 0))
                hr, hi = _cmul_add(br, bi, tabs[3][0], tabs[3][1], jnp.broadcast_to(cr, br.shape), jnp.broadcast_to(ci, bi.shape))
                hr_ref[pl.ds(r0, SUBLANES), sl] = hr
                hi_ref[pl.ds(r0, SUBLANES), sl] = hi
                return hr[SUBLANES - 1:, :], hi[SUBLANES - 1:, :]

            cr, ci = lax.fori_loop(0, nsl, slab, (car_r[:, sl], car_i[:, sl]), unroll=2)
            car_r[:, sl] = cr
            car_i[:, sl] = ci
        hrb, hib = hr_ref[...].astype(BF16), hi_ref[...].astype(BF16)
        y = jnp.concatenate([jnp.dot(hrb[:, k * nb:(k + 1) * nb], cpr_ref[k], preferred_element_type=F32)
                             - jnp.dot(hib[:, k * nb:(k + 1) * nb], cpi_ref[k], preferred_element_type=F32) for k in range(nblk)], axis=1)
        yp = y + d_ref[...] * uv
        yp_ref[...] = yp
        gy_ref[...] = _gelu(yp).astype(BF16)

    full = lambda shape: pl.BlockSpec(shape, lambda i: (0,) * len(shape))
    rc = pl.BlockSpec((ts, c), lambda i: (i, 0))
    rn = pl.BlockSpec((ts, n), lambda i: (i, 0))
    sd = jax.ShapeDtypeStruct
    nt = s // ts
    return _call_with_side(
        body, side, lambda: pl.program_id(0) == 0, lambda: pl.program_id(0) == nt - 1,
        name="s5_fwd", grid=(nt,),
        in_specs=[rc, full(tab_r.shape), full(tab_i.shape), full(bp_r.shape), full(bp_i.shape), full(cp_r.shape), full(cp_i.shape),
                  full(dvec.shape)],
        out_specs=[rn, rn, rc, rc],
        out_shape=[sd((s, n), F32), sd((s, n), F32), sd((s, c), F32), sd((s, c), BF16)],
        scratch_shapes=[pltpu.VMEM((ts, n), F32), pltpu.VMEM((ts, n), F32), pltpu.VMEM((1, n), F32), pltpu.VMEM((1, n), F32)],
        args=(u, tab_r, tab_i, bp_r, bp_i, cp_r, cp_i, dvec))


def _s5_bwd2(dgy, ypre, u, hr, hi, rtab_r, rtab_i, bp_r, bp_i, cp_r, cp_i, dvec, ts=256, side=None):
    s, c = u.shape
    n = rtab_r.shape[2]
    nblk, cb, nb = bp_r.shape
    ts = min(ts, s)
    nt = s // ts
    hb = ts // SUBLANES
    nsl = ts // SUBLANES
    lc = min(S5_LANE_CHUNK, n)
    tn_dims = (((0,), (0,)), ((), ()))
    nt_dims = (((1,), (1,)), ((), ()))

    def body(dgy_ref, yp_ref, u_ref, hr_ref, hi_ref, hrh_ref, hih_ref, tr_ref, ti_ref, bpr_ref, bpi_ref, cpr_ref, cpi_ref, d_ref,
             du_ref, dar_ref, dai_ref, dbr_ref, dbi_ref, dcr_ref, dci_ref, dd_ref, lr_ref, li_ref, car_r, car_i):
        i = pl.program_id(0)
        time_first = i == nt - 1

        @pl.when(i == 0)
        def _():
            car_r[...] = jnp.zeros_like(car_r)
            car_i[...] = jnp.zeros_like(car_i)
            for ref in (dar_ref, dai_ref, dbr_ref, dbi_ref, dcr_ref, dci_ref, dd_ref):
                ref[...] = jnp.zeros_like(ref)

        uv = u_ref[...]
        _, dgel = _gelu_and_grad(yp_ref[...])
        dyv = dgy_ref[...] * dgel
        dd_ref[...] += jnp.sum(dyv * uv, axis=0, keepdims=True)
        dyb = dyv.astype(BF16)
        hrb, hib = hr_ref[...].astype(BF16), hi_ref[...].astype(BF16)
        for k in range(nblk):
            dblk = dyb[:, k * cb:(k + 1) * cb]
            lr_ref[:, k * nb:(k + 1) * nb] = lax.dot_general(dblk, cpr_ref[k], nt_dims, preferred_element_type=F32)
            li_ref[:, k * nb:(k + 1) * nb] = -lax.dot_general(dblk, cpi_ref[k], nt_dims, preferred_element_type=F32)
            dcr_ref[k] += lax.dot_general(hrb[:, k * nb:(k + 1) * nb], dblk, tn_dims, preferred_element_type=F32)
            dci_ref[k] += lax.dot_general(hib[:, k * nb:(k + 1) * nb], dblk, tn_dims, preferred_element_type=F32)
        row8 = _rows((SUBLANES, lc))
        for q in range(n // lc):
            sl = slice(q * lc, (q + 1) * lc)
            tabs = [(tr_ref[k, :, sl], ti_ref[k, :, sl]) for k in range(4)]
            halo_r = jnp.where(time_first, 0.0, hrh_ref[SUBLANES - 1:, sl])
            halo_i = jnp.where(time_first, 0.0, hih_ref[SUBLANES - 1:, sl])

            def slab(jj, carry, sl=sl, tabs=tabs, halo_r=halo_r, halo_i=halo_i):
                nr, ni, acc_r, acc_i = carry
                j = nsl - 1 - jj
                r0 = pl.multiple_of(j * SUBLANES, SUBLANES)
                br, bi = lr_ref[pl.ds(r0, SUBLANES), sl], li_ref[pl.ds(r0, SUBLANES), sl]
                for k in range(3):
                    sh = 1 << k
                    br, bi = _cmul_add(br, bi, tabs[k][0], tabs[k][1], pltpu.roll(br, SUBLANES - sh, 0),
                                       pltpu.roll(bi, SUBLANES - sh, 0))
                lr, li = _cmul_add(br, bi, tabs[3][0], tabs[3][1], jnp.broadcast_to(nr, br.shape), jnp.broadcast_to(ni, bi.shape))
                lr_ref[pl.ds(r0, SUBLANES), sl] = lr
                li_ref[pl.ds(r0, SUBLANES), sl] = li
                p0 = pl.multiple_of(jnp.maximum(j - 1, 0) * SUBLANES, SUBLANES)
                prev_r = jnp.where(j == 0, halo_r, hr_ref[pl.ds(p0, SUBLANES), sl][SUBLANES - 1:, :])
                prev_i = jnp.where(j == 0, halo_i, hi_ref[pl.ds(p0, SUBLANES), sl][SUBLANES - 1:, :])
                hpr = jnp.where(row8 == 0, jnp.broadcast_to(prev_r, br.shape), pltpu.roll(hr_ref[pl.ds(r0, SUBLANES), sl], 1, 0))
                hpi = jnp.where(row8 == 0, jnp.broadcast_to(prev_i, bi.shape), pltpu.roll(hi_ref[pl.ds(r0, SUBLANES), sl], 1, 0))
                return lr[:1, :], li[:1, :], acc_r + (lr * hpr + li * hpi), acc_i + (li * hpr - lr * hpi)

            zero = jnp.zeros((SUBLANES, lc), F32)
            nr, ni, acc_r, acc_i = lax.fori_loop(0, nsl, slab, (car_r[:, sl], car_i[:, sl], zero, zero), unroll=2)
            car_r[:, sl] = nr
            car_i[:, sl] = ni
            dar_ref[:, sl] += jnp.sum(acc_r, axis=0, keepdims=True)
            dai_ref[:, sl] += jnp.sum(acc_i, axis=0, keepdims=True)
        lrb, lib = lr_ref[...].astype(BF16), li_ref[...].astype(BF16)
        ub = uv.astype(BF16)
        du = []
        for k in range(nblk):
            ublk = ub[:, k * cb:(k + 1) * cb]
            lrk, lik = lrb[:, k * nb:(k + 1) * nb], lib[:, k * nb:(k + 1) * nb]
            dbr_ref[k] += lax.dot_general(ublk, lrk, tn_dims, preferred_element_type=F32)
            dbi_ref[k] += lax.dot_general(ublk, lik, tn_dims, preferred_element_type=F32)
            du.append(lax.dot_general(lrk, bpr_ref[k], nt_dims, preferred_element_type=F32)
                      + lax.dot_general(lik, bpi_ref[k], nt_dims, preferred_element_type=F32))
        du_ref[...] = (d_ref[...] * dyv + jnp.concatenate(du, axis=1)).astype(BF16)

    full = lambda shape: pl.BlockSpec(shape, lambda i: (0,) * len(shape))
    rev = lambda i: nt - 1 - i
    halo_idx = lambda i: jnp.maximum(rev(i) * hb - 1, 0)
    rc = pl.BlockSpec((ts, c), lambda i: (rev(i), 0))
    rn = pl.BlockSpec((ts, n), lambda i: (rev(i), 0))
    hn = pl.BlockSpec((SUBLANES, n), lambda i: (halo_idx(i), 0))
    sd = jax.ShapeDtypeStruct
    vec_n = (1, n)
    return _call_with_side(
        body, side, lambda: pl.program_id(0) == 0, lambda: pl.program_id(0) == nt - 1,
        name="s5_bwd", grid=(nt,),
        in_specs=[rc, rc, rc, rn, rn, hn, hn, full(rtab_r.shape), full(rtab_i.shape),
                  full(bp_r.shape), full(bp_i.shape), full(cp_r.shape), full(cp_i.shape), full(dvec.shape)],
        out_specs=[rc, full(vec_n), full(vec_n), full(bp_r.shape), full(bp_i.shape), full(cp_r.shape), full(cp_i.shape),
                   full(dvec.shape)],
        out_shape=[sd((s, c), BF16), sd(vec_n, F32), sd(vec_n, F32), sd(bp_r.shape, F32), sd(bp_i.shape, F32),
                   sd(cp_r.shape, F32), sd(cp_i.shape, F32), sd(dvec.shape, F32)],
        scratch_shapes=[pltpu.VMEM((ts, n), F32), pltpu.VMEM((ts, n), F32), pltpu.VMEM((1, n), F32), pltpu.VMEM((1, n), F32)],
        args=(dgy, ypre, u, hr, hi, hr, hi, rtab_r, rtab_i, bp_r, bp_i, cp_r, cp_i, dvec))


def _s5_tables3(a_re, a_im, log_dt, bt_re, bt_im, seg):
    g, p = a_re.shape
    gc = bt_re.shape[0]
    nsq = int(math.log2(seg))
    assert 1 << nsq == seg

    def body(ar_ref, ai_ref, dt_ref, br_ref, bi_ref, tr_ref, ti_ref, rtr_ref, rti_ref, bbr_ref, bbi_ref):
        abr, abi, bbr, bbi = _s5_param_fn(ar_ref[...], ai_ref[...], dt_ref[...], br_ref[...], bi_ref[...])
        bbr_ref[...] = bbr
        bbi_ref[...] = bbi
        qr, qi = abr, abi
        for _ in range(nsq):
            qr, qi = qr * qr - qi * qi, 2.0 * qr * qi
        pows = [(qr, qi)]
        for _ in range(1, SUBLANES):
            cr, ci = pows[-1]
            pows.append((cr * qr - ci * qi, cr * qi + ci * qr))
        zero = jnp.zeros_like(abr)
        for r in range(SUBLANES):
            rows = [(pows[(1 << k) - 1] if r >= (1 << k) else (zero, zero)) for k in range(3)] + [pows[r], (abr, abi)]
            for k, (vr, vi) in enumerate(rows):
                tr_ref[k, r] = vr
                ti_ref[k, r] = vi
                rtr_ref[k, SUBLANES - 1 - r] = vr
                rti_ref[k, SUBLANES - 1 - r] = -vi

    sd = jax.ShapeDtypeStruct
    tab = sd((5, SUBLANES, g, p), F32)
    return pl.pallas_call(
        body, name="s5_tables", out_shape=[tab, tab, tab, tab, sd((gc, g, p), F32), sd((gc, g, p), F32)],
    )(a_re, a_im, log_dt, bt_re, bt_im)


def _segment_perm(ts):
    seg = ts // SUBLANES
    rho = jnp.arange(ts)
    src = (rho % SUBLANES) * seg + rho // SUBLANES
    return (src[:, None] == jnp.arange(ts)[None, :]).astype(BF16)


def _exact_rows(perm_t, x):
    hi = x.astype(BF16)
    r1 = x - hi.astype(F32)
    mid = r1.astype(BF16)
    lo = (r1 - mid.astype(F32)).astype(BF16)
    dot = lambda v: jnp.dot(perm_t, v, preferred_element_type=F32)
    return (dot(hi) + dot(mid)) + dot(lo)


def _s5_fwd3(u, perm, perm_t, tab_r, tab_i, bp_r, bp_i, cp_r, cp_i, dvec, ts=256, side=None):
    s, c = u.shape
    n = tab_r.shape[2]
    nblk, cb, nb = bp_r.shape
    ts = min(ts, s)
    seg = ts // SUBLANES
    lc = min(S5_LANE_CHUNK, n)

    def body(u_ref, p_ref, pt_ref, tr_ref, ti_ref, bpr_ref, bpi_ref, cpr_ref, cpi_ref, d_ref, hr_ref, hi_ref, yp_ref, gy_ref,
             bur_ref, bui_ref, car_r, car_i):
        i = pl.program_id(0)

        @pl.when(i == 0)
        def _():
            car_r[...] = jnp.zeros_like(car_r)
            car_i[...] = jnp.zeros_like(car_i)

        uv = u_ref[...]
        ubp = jnp.dot(p_ref[...], uv.astype(BF16), preferred_element_type=F32).astype(BF16)
        for k in range(nblk):
            bur_ref[:, k * nb:(k + 1) * nb] = jnp.dot(ubp[:, k * cb:(k + 1) * cb], bpr_ref[k], preferred_element_type=F32)
            bui_ref[:, k * nb:(k + 1) * nb] = jnp.dot(ubp[:, k * cb:(k + 1) * cb], bpi_ref[k], preferred_element_type=F32)
        row8 = _rows((SUBLANES, lc))
        for q in range(n // lc):
            sl = slice(q * lc, (q + 1) * lc)
            tabs = [(tr_ref[k, :, sl], ti_ref[k, :, sl]) for k in range(5)]
            a_r, a_i = tabs[4]

            def local(r, carry, sl=sl, a_r=a_r, a_i=a_i):
                r0 = pl.multiple_of(r * SUBLANES, SUBLANES)
                hr, hi = _cmul_add(bur_ref[pl.ds(r0, SUBLANES), sl], bui_ref[pl.ds(r0, SUBLANES), sl], a_r, a_i, carry[0], carry[1])
                hr_ref[pl.ds(r0, SUBLANES), sl] = hr
                hi_ref[pl.ds(r0, SUBLANES), sl] = hi
                return hr, hi

            zero = jnp.zeros((SUBLANES, lc), F32)
            er, ei = lax.fori_loop(0, seg, local, (zero, zero), unroll=4)
            for k in range(3):
                sh = 1 << k
                er, ei = _cmul_add(er, ei, tabs[k][0], tabs[k][1], pltpu.roll(er, sh, 0), pltpu.roll(ei, sh, 0))
            cin_r, cin_i = jnp.broadcast_to(car_r[:, sl], er.shape), jnp.broadcast_to(car_i[:, sl], ei.shape)
            er, ei = _cmul_add(er, ei, tabs[3][0], tabs[3][1], cin_r, cin_i)
            car_r[:, sl] = er[SUBLANES - 1:, :]
            car_i[:, sl] = ei[SUBLANES - 1:, :]
            c_r = jnp.where(row8 == 0, cin_r, pltpu.roll(er, 1, 0))
            c_i = jnp.where(row8 == 0, cin_i, pltpu.roll(ei, 1, 0))

            def fix(r, carry, sl=sl, a_r=a_r, a_i=a_i, c_r=c_r, c_i=c_i):
                pr, pi = carry
                r0 = pl.multiple_of(r * SUBLANES, SUBLANES)
                hr, hi = _cmul_add(hr_ref[pl.ds(r0, SUBLANES), sl], hi_ref[pl.ds(r0, SUBLANES), sl], pr, pi, c_r, c_i)
                hr_ref[pl.ds(r0, SUBLANES), sl] = hr
                hi_ref[pl.ds(r0, SUBLANES), sl] = hi
                return pr * a_r - pi * a_i, pr * a_i + pi * a_r

            lax.fori_loop(0, seg, fix, (a_r, a_i), unroll=4)
        hrb, hib = hr_ref[...].astype(BF16), hi_ref[...].astype(BF16)
        y = jnp.concatenate([jnp.dot(hrb[:, k * nb:(k + 1) * nb], cpr_ref[k], preferred_element_type=F32)
                             - jnp.dot(hib[:, k * nb:(k + 1) * nb], cpi_ref[k], preferred_element_type=F32) for k in range(nblk)], axis=1)
        yp = _exact_rows(pt_ref[...], y) + d_ref[...] * uv
        yp_ref[...] = yp
        gy_ref[...] = _gelu(yp).astype(BF16)

    full = lambda shape: pl.BlockSpec(shape, lambda i: (0,) * len(shape))
    rc = pl.BlockSpec((ts, c), lambda i: (i, 0))
    rn = pl.BlockSpec((ts, n), lambda i: (i, 0))
    sd = jax.ShapeDtypeStruct
    nt = s // ts
    return _call_with_side(
        body, side, lambda: pl.program_id(0) == 0, lambda: pl.program_id(0) == nt - 1,
        name="s5_fwd", grid=(nt,),
        in_specs=[rc, full(perm.shape), full(perm_t.shape), full(tab_r.shape), full(tab_i.shape), full(bp_r.shape), full(bp_i.shape),
                  full(cp_r.shape), full(cp_i.shape), full(dvec.shape)],
        out_specs=[rn, rn, rc, rc],
        out_shape=[sd((s, n), F32), sd((s, n), F32), sd((s, c), F32), sd((s, c), BF16)],
        scratch_shapes=[pltpu.VMEM((ts, n), F32), pltpu.VMEM((ts, n), F32), pltpu.VMEM((1, n), F32), pltpu.VMEM((1, n), F32)],
        args=(u, perm, perm_t, tab_r, tab_i, bp_r, bp_i, cp_r, cp_i, dvec))


def _s5_bwd3(dgy, ypre, u, hr, hi, perm, perm_t, rtab_r, rtab_i, bp_r, bp_i, cp_r, cp_i, dvec, ts=256, side=None):
    s, c = u.shape
    n = rtab_r.shape[2]
    nblk, cb, nb = bp_r.shape
    ts = min(ts, s)
    nt = s // ts
    hb = ts // SUBLANES
    seg = ts // SUBLANES
    lc = min(S5_LANE_CHUNK, n)
    tn_dims = (((0,), (0,)), ((), ()))
    nt_dims = (((1,), (1,)), ((), ()))

    def body(dgy_ref, yp_ref, u_ref, hr_ref, hi_ref, hrh_ref, hih_ref, p_ref, pt_ref, tr_ref, ti_ref, bpr_ref, bpi_ref,
             cpr_ref, cpi_ref, d_ref, du_ref, dar_ref, dai_ref, dbr_ref, dbi_ref, dcr_ref, dci_ref, dd_ref, lr_ref, li_ref,
             car_r, car_i):
        i = pl.program_id(0)
        time_first = i == nt - 1

        @pl.when(i == 0)
        def _():
            car_r[...] = jnp.zeros_like(car_r)
            car_i[...] = jnp.zeros_like(car_i)
            for ref in (dar_ref, dai_ref, dbr_ref, dbi_ref, dcr_ref, dci_ref, dd_ref):
                ref[...] = jnp.zeros_like(ref)

        uv = u_ref[...]
        _, dgel = _gelu_and_grad(yp_ref[...])
        dyv = dgy_ref[...] * dgel
        dd_ref[...] += jnp.sum(dyv * uv, axis=0, keepdims=True)
        perm_m = p_ref[...]
        dyb = jnp.dot(perm_m, dyv.astype(BF16), preferred_element_type=F32).astype(BF16)
        ub = jnp.dot(perm_m, uv.astype(BF16), preferred_element_type=F32).astype(BF16)
        hrb, hib = hr_ref[...].astype(BF16), hi_ref[...].astype(BF16)
        for k in range(nblk):
            dblk = dyb[:, k * cb:(k + 1) * cb]
            lr_ref[:, k * nb:(k + 1) * nb] = lax.dot_general(dblk, cpr_ref[k], nt_dims, preferred_element_type=F32)
            li_ref[:, k * nb:(k + 1) * nb] = -lax.dot_general(dblk, cpi_ref[k], nt_dims, preferred_element_type=F32)
            dcr_ref[k] += lax.dot_general(hrb[:, k * nb:(k + 1) * nb], dblk, tn_dims, preferred_element_type=F32)
            dci_ref[k] += lax.dot_general(hib[:, k * nb:(k + 1) * nb], dblk, tn_dims, preferred_element_type=F32)
        row8 = _rows((SUBLANES, lc))
        last0 = (seg - 1) * SUBLANES
        for q in range(n // lc):
            sl = slice(q * lc, (q + 1) * lc)
            tabs = [(tr_ref[k, :, sl], ti_ref[k, :, sl]) for k in range(5)]
            a_r, a_i = tabs[4]

            def local(rr, carry, sl=sl, a_r=a_r, a_i=a_i):
                r0 = pl.multiple_of((seg - 1 - rr) * SUBLANES, SUBLANES)
                lr, li = _cmul_add(lr_ref[pl.ds(r0, SUBLANES), sl], li_ref[pl.ds(r0, SUBLANES), sl], a_r, a_i, carry[0], carry[1])
                lr_ref[pl.ds(r0, SUBLANES), sl] = lr
                li_ref[pl.ds(r0, SUBLANES), sl] = li
                return lr, li

            zero = jnp.zeros((SUBLANES, lc), F32)
            er, ei = lax.fori_loop(0, seg, local, (zero, zero), unroll=4)
            for k in range(3):
                sh = 1 << k
                er, ei = _cmul_add(er, ei, tabs[k][0], tabs[k][1], pltpu.roll(er, SUBLANES - sh, 0), pltpu.roll(ei, SUBLANES - sh, 0))
            cin_r, cin_i = jnp.broadcast_to(car_r[:, sl], er.shape), jnp.broadcast_to(car_i[:, sl], ei.shape)
            er, ei = _cmul_add(er, ei, tabs[3][0], tabs[3][1], cin_r, cin_i)
            car_r[:, sl] = er[:1, :]
            car_i[:, sl] = ei[:1, :]
            c_r = jnp.where(row8 == SUBLANES - 1, cin_r, pltpu.roll(er, SUBLANES - 1, 0))
            c_i = jnp.where(row8 == SUBLANES - 1, cin_i, pltpu.roll(ei, SUBLANES - 1, 0))
            halo_r = jnp.where(time_first, 0.0, hrh_ref[SUBLANES - 1:, sl])
            halo_i = jnp.where(time_first, 0.0, hih_ref[SUBLANES - 1:, sl])
            hp0_r = jnp.where(row8 == 0, jnp.broadcast_to(halo_r, zero.shape), pltpu.roll(hr_ref[pl.ds(last0, SUBLANES), sl], 1, 0))
            hp0_i = jnp.where(row8 == 0, jnp.broadcast_to(halo_i, zero.shape), pltpu.roll(hi_ref[pl.ds(last0, SUBLANES), sl], 1, 0))

            def fix(rr, carry, sl=sl, a_r=a_r, a_i=a_i, c_r=c_r, c_i=c_i, hp0_r=hp0_r, hp0_i=hp0_i):
                pr, pi, acc_r, acc_i = carry
                r = seg - 1 - rr
                r0 = pl.multiple_of(r * SUBLANES, SUBLANES)
                lr, li = _cmul_add(lr_ref[pl.ds(r0, SUBLANES), sl], li_ref[pl.ds(r0, SUBLANES), sl], pr, pi, c_r, c_i)
                lr_ref[pl.ds(r0, SUBLANES), sl] = lr
                li_ref[pl.ds(r0, SUBLANES), sl] = li
                p0 = pl.multiple_of(jnp.maximum(r - 1, 0) * SUBLANES, SUBLANES)
                hpr = jnp.where(r == 0, hp0_r, hr_ref[pl.ds(p0, SUBLANES), sl])
                hpi = jnp.where(r == 0, hp0_i, hi_ref[pl.ds(p0, SUBLANES), sl])
                return (pr * a_r - pi * a_i, pr * a_i + pi * a_r, acc_r + (lr * hpr + li * hpi), acc_i + (li * hpr - lr * hpi))

            _, _, acc_r, acc_i = lax.fori_loop(0, seg, fix, (a_r, a_i, zero, zero), unroll=4)
            dar_ref[:, sl] += jnp.sum(acc_r, axis=0, keepdims=True)
            dai_ref[:, sl] += jnp.sum(acc_i, axis=0, keepdims=True)
        lrb, lib = lr_ref[...].astype(BF16), li_ref[...].astype(BF16)
        du = []
        for k in range(nblk):
            ublk = ub[:, k * cb:(k + 1) * cb]
            lrk, lik = lrb[:, k * nb:(k + 1) * nb], lib[:, k * nb:(k + 1) * nb]
            dbr_ref[k] += lax.dot_general(ublk, lrk, tn_dims, preferred_element_type=F32)
            dbi_ref[k] += lax.dot_general(ublk, lik, tn_dims, preferred_element_type=F32)
            du.append(lax.dot_general(lrk, bpr_ref[k], nt_dims, preferred_element_type=F32)
                      + lax.dot_general(lik, bpi_ref[k], nt_dims, preferred_element_type=F32))
        du_ref[...] = (d_ref[...] * dyv + _exact_rows(pt_ref[...], jnp.concatenate(du, axis=1))).astype(BF16)

    full = lambda shape: pl.BlockSpec(shape, lambda i: (0,) * len(shape))
    rev = lambda i: nt - 1 - i
    halo_idx = lambda i: jnp.maximum(rev(i) * hb - 1, 0)
    rc = pl.BlockSpec((ts, c), lambda i: (rev(i), 0))
    rn = pl.BlockSpec((ts, n), lambda i: (rev(i), 0))
    hn = pl.BlockSpec((SUBLANES, n), lambda i: (halo_idx(i), 0))
    sd = jax.ShapeDtypeStruct
    vec_n = (1, n)
    return _call_with_side(
        body, side, lambda: pl.program_id(0) == 0, lambda: pl.program_id(0) == nt - 1,
        name="s5_bwd", grid=(nt,),
        in_specs=[rc, rc, rc, rn, rn, hn, hn, full(perm.shape), full(perm_t.shape), full(rtab_r.shape), full(rtab_i.shape),
                  full(bp_r.shape), full(bp_i.shape), full(cp_r.shape), full(cp_i.shape), full(dvec.shape)],
        out_specs=[rc, full(vec_n), full(vec_n), full(bp_r.shape), full(bp_i.shape), full(cp_r.shape), full(cp_i.shape),
                   full(dvec.shape)],
        out_shape=[sd((s, c), BF16), sd(vec_n, F32), sd(vec_n, F32), sd(bp_r.shape, F32), sd(bp_i.shape, F32),
                   sd(cp_r.shape, F32), sd(cp_i.shape, F32), sd(dvec.shape, F32)],
        scratch_shapes=[pltpu.VMEM((ts, n), F32), pltpu.VMEM((ts, n), F32), pltpu.VMEM((1, n), F32), pltpu.VMEM((1, n), F32)],
        args=(dgy, ypre, u, hr, hi, hr, hi, perm, perm_t, rtab_r, rtab_i, bp_r, bp_i, cp_r, cp_i, dvec))


def _glu(gl2, ts=512):
    _, s, c = gl2.shape
    ts = min(ts, s)

    def body(g_ref, o_ref):
        o_ref[...] = (g_ref[0] * _sigmoid(g_ref[1])).astype(BF16)

    return pl.pallas_call(
        body, name="glu", grid=(s // ts,), in_specs=[pl.BlockSpec((2, ts, c), lambda i: (0, i, 0))],
        out_specs=pl.BlockSpec((ts, c), lambda i: (i, 0)), out_shape=jax.ShapeDtypeStruct((s, c), BF16), compiler_params=_cparams(),
    )(gl2)


def _glu_bwd(gl2, d_o, ts=512):
    _, s, c = gl2.shape
    ts = min(ts, s)

    def body(g_ref, do_ref, o_ref):
        sg = _sigmoid(g_ref[1])
        dov = do_ref[...]
        o_ref[0] = (dov * sg).astype(BF16)
        o_ref[1] = (dov * g_ref[0] * sg * (1.0 - sg)).astype(BF16)

    blk = pl.BlockSpec((2, ts, c), lambda i: (0, i, 0))
    return pl.pallas_call(
        body, name="glu_bwd", grid=(s // ts,), in_specs=[blk, pl.BlockSpec((ts, c), lambda i: (i, 0))],
        out_specs=blk, out_shape=jax.ShapeDtypeStruct((2, s, c), BF16), compiler_params=_cparams(),
    )(gl2, d_o)


PACK_ROW_MULTIPLE = 1024
ELEMENTWISE_BLOCK_ELEMS = 256 * 1024


def _row_tile(rows, cols):
    pref = max(SUBLANES, 1 << int(math.log2(max(1, ELEMENTWISE_BLOCK_ELEMS // cols))))
    if rows <= pref:
        return rows
    t = pref
    while rows % t:
        t //= 2
    assert t >= SUBLANES, rows
    return t


def _sum_parts(rs):
    nl = len(rs)
    p, rows, cols = rs[0].shape
    tr = _row_tile(rows, cols)

    def body(*refs):
        o_ref = refs[nl]
        for l in range(nl):
            acc = refs[l][0].astype(F32)
            for k in range(1, p):
                acc = acc + refs[l][k].astype(F32)
            o_ref[l] = acc

    return pl.pallas_call(
        body, name="sum_parts", grid=(rows // tr,), in_specs=[pl.BlockSpec((p, tr, cols), lambda i: (0, i, 0))] * nl,
        out_specs=pl.BlockSpec((nl, tr, cols), lambda i: (0, i, 0)), out_shape=jax.ShapeDtypeStruct((nl, rows, cols), F32),
        compiler_params=_cparams(),
    )(*rs)


def _adamw(w, g_parts, m, v, side=None):
    rows, cols = w.shape
    tr = _row_tile(rows, max(cols, LANES))
    ng = len(g_parts)
    emit_grad = ng > 1
    c1 = 1.0 / (1.0 - ADAM_B1 ** ADAM_STEP)
    c2 = 1.0 / (1.0 - ADAM_B2 ** ADAM_STEP)

    def body(*refs):
        w_ref, m_ref, v_ref = refs[0], refs[1 + ng], refs[2 + ng]
        dl_ref, nm_ref, nv_ref = refs[3 + ng:6 + ng]
        g = refs[1][...]
        for k in range(1, ng):
            g = g + refs[1 + k][...]
        mn = ADAM_B1 * m_ref[...] + (1.0 - ADAM_B1) * g
        vn = ADAM_B2 * v_ref[...] + (1.0 - ADAM_B2) * (g * g)
        if emit_grad:
            refs[6 + ng][...] = g
        nm_ref[...] = mn
        nv_ref[...] = vn
        dl_ref[...] = -ADAM_LR * ((mn * c1) / (jnp.sqrt(vn * c2) + ADAM_EPS) + ADAM_WD * w_ref[...])

    blk = pl.BlockSpec((tr, cols), lambda i: (i, 0))
    sd = jax.ShapeDtypeStruct((rows, cols), F32)
    nout = 4 if emit_grad else 3
    nt = rows // tr
    return _call_with_side(
        body, side, lambda: pl.program_id(0) == 0, lambda: pl.program_id(0) == nt - 1,
        name="adamw", grid=(nt,), in_specs=[blk] * (3 + ng), out_specs=[blk] * nout, out_shape=[sd] * nout,
        scratch_shapes=[], args=(w, *g_parts, m, v))


def _place():
    x, y, c = lax.axis_index("x"), lax.axis_index("y"), lax.axis_index("c")
    chips = [(1 - x, y), (x, 1 - y), (1 - x, 1 - y)]
    return x, y, c, chips


class Side:
    def __init__(self, ins, outs, kind, views=None):
        self.ins, self.outs, self.kind = list(ins), list(outs), kind
        n = len(self.ins)
        self.views = views or [None] * n
        self.sems = [pltpu.SemaphoreType.DMA((3 * n,)), pltpu.SemaphoreType.DMA((3 * n,)), pltpu.SemaphoreType.DMA((n,))]

    def _copies(self, ins, outs, send, recv, lsem):
        x, y, c, chips = _place()
        me = 2 * x + y
        local, out_going, in_coming = [], [], []
        for t in range(len(ins)):
            if self.kind == 'sibling':
                cp = pltpu.make_async_remote_copy(src_ref=ins[t], dst_ref=outs[t], send_sem=send.at[t], recv_sem=recv.at[t],
                                                  device_id=(x, y, 1 - c), device_id_type=MESH)
                out_going.append(cp)
                in_coming.append(cp)
                continue
            if self.kind == 'gather':
                src_local, srcs, dst_mine = ins[t], [ins[t]] * 3, outs[t].at[me]
            else:
                part = (lambda p, t=t: self.views[t](ins[t], p)) if self.views[t] else (lambda p, t=t: ins[t].at[p])
                src_local, srcs, dst_mine = part(me), [part(2 * px + py) for px, py in chips], outs[t].at[me]
            local.append(pltpu.make_async_copy(src_local, dst_mine, lsem.at[t]))
            for r, (px, py) in enumerate(chips):
                out_going.append(pltpu.make_async_remote_copy(
                    src_ref=srcs[r], dst_ref=dst_mine, send_sem=send.at[3 * t + r], recv_sem=recv.at[3 * t + r],
                    device_id=(px, py, c), device_id_type=MESH))
                in_coming.append(pltpu.make_async_remote_copy(
                    src_ref=srcs[r], dst_ref=outs[t].at[2 * px + py], send_sem=send.at[3 * t + r], recv_sem=recv.at[3 * t + r],
                    device_id=(px, py, c), device_id_type=MESH))
        return local, out_going, in_coming

    def start(self, ins, outs, send, recv, lsem):
        local, out_going, _ = self._copies(ins, outs, send, recv, lsem)
        for cp in local + out_going:
            cp.start()

    def wait(self, ins, outs, send, recv, lsem):
        local, out_going, in_coming = self._copies(ins, outs, send, recv, lsem)
        for cp in in_coming:
            cp.wait_recv()
        for cp in out_going:
            cp.wait_send()
        for cp in local:
            cp.wait()


def _gather_side(shards):
    return Side(shards, [jax.ShapeDtypeStruct((N_CHIPS,) + s.shape, s.dtype) for s in shards], 'gather')


def _scatter_side(grads, shapes, views):
    return Side(grads, [jax.ShapeDtypeStruct(s, g.dtype) for g, s in zip(grads, shapes)], 'scatter', views)


def _sibling_side(arrs):
    return Side(arrs, [jax.ShapeDtypeStruct(a.shape, a.dtype) for a in arrs], 'sibling')


def _halves_view(ref, p):
    half = ref.shape[2] // 2
    return ref.at[p // 2, :, pl.ds((p % 2) * half, half)]


def _call_with_side(body, side, first, last, *, name, grid, in_specs, out_specs, out_shape, scratch_shapes, args):
    if side is None:
        outs = pl.pallas_call(body, name=name, grid=grid, in_specs=in_specs, out_specs=out_specs, out_shape=out_shape,
                              scratch_shapes=scratch_shapes, compiler_params=_cparams())(*args)
        return outs, []
    n_in, n_out, n_sc = len(in_specs), len(out_specs), len(scratch_shapes)
    ns_in, ns_out = len(side.ins), len(side.outs)

    def wrapped(*refs):
        base_in, s_in = refs[:n_in], refs[n_in:n_in + ns_in]
        o0 = n_in + ns_in
        base_out, s_out = refs[o0:o0 + n_out], refs[o0 + n_out:o0 + n_out + ns_out]
        sc0 = o0 + n_out + ns_out
        base_sc, sems = refs[sc0:sc0 + n_sc], refs[sc0 + n_sc:]

        @pl.when(first())
        def _():
            side.start(s_in, s_out, *sems)

        body(*base_in, *base_out, *base_sc)

        @pl.when(last())
        def _():
            side.wait(s_in, s_out, *sems)

    any_spec = pl.BlockSpec(memory_space=pl.ANY)
    outs = pl.pallas_call(
        wrapped, name=name, grid=grid, in_specs=list(in_specs) + [any_spec] * ns_in, out_specs=list(out_specs) + [any_spec] * ns_out,
        out_shape=list(out_shape) + side.outs, scratch_shapes=list(scratch_shapes) + side.sems, compiler_params=_cparams(),
    )(*args, *side.ins)
    return outs[:n_out], outs[n_out:]


def _run_side(name, side):
    def body(*refs):
        n = len(side.ins)
        side.start(refs[:n], refs[n:2 * n], *refs[2 * n:])
        side.wait(refs[:n], refs[n:2 * n], *refs[2 * n:])

    any_spec = pl.BlockSpec(memory_space=pl.ANY)
    return pl.pallas_call(body, name=name, in_specs=[any_spec] * len(side.ins), out_specs=[any_spec] * len(side.outs),
                          out_shape=side.outs, scratch_shapes=side.sems)(*side.ins)


def _gather_shards(shards, layer_major):
    n = len(shards)

    def body(*refs):
        ins, outs = refs[:n], refs[n:2 * n]
        send, recv, lsem = refs[2 * n:]
        x, y, c, chips = _place()
        me = 2 * x + y

        def slot(t, chip):
            return outs[t].at[:, chip] if layer_major[t] else outs[t].at[chip]

        local, sends = [], []
        for t in range(n):
            cp = pltpu.make_async_copy(ins[t], slot(t, me), lsem.at[t])
            cp.start()
            local.append(cp)
            for r, (px, py) in enumerate(chips):
                rc = pltpu.make_async_remote_copy(src_ref=ins[t], dst_ref=slot(t, me), send_sem=send.at[3 * t + r],
                                                  recv_sem=recv.at[3 * t + r], device_id=(px, py, c), device_id_type=MESH)
                rc.start()
                sends.append(rc)
        for t in range(n):
            for r, (px, py) in enumerate(chips):
                pltpu.make_async_remote_copy(src_ref=ins[t], dst_ref=slot(t, 2 * px + py), send_sem=send.at[3 * t + r],
                                             recv_sem=recv.at[3 * t + r], device_id=(px, py, c), device_id_type=MESH).wait_recv()
        for rc in sends:
            rc.wait_send()
        for cp in local:
            cp.wait()

    any_spec = pl.BlockSpec(memory_space=pl.ANY)
    return pl.pallas_call(
        body, name="gather_shards", in_specs=[any_spec] * n, out_specs=[any_spec] * n,
        out_shape=[jax.ShapeDtypeStruct((s.shape[0], N_CHIPS) + s.shape[1:] if lm else (N_CHIPS,) + s.shape, s.dtype)
                   for s, lm in zip(shards, layer_major)],
        scratch_shapes=[pltpu.SemaphoreType.DMA((3 * n,)), pltpu.SemaphoreType.DMA((3 * n,)), pltpu.SemaphoreType.DMA((n,))],
    )(*shards)


def _scatter_grads(groups):
    flat = [(gi, li, a) for gi, grp in enumerate(groups) for li, a in enumerate(grp)]
    n = len(flat)
    ng = len(groups)

    def body(*refs):
        ins, outs = refs[:n], refs[n:n + ng]
        send, recv, lsem = refs[n + ng:]
        x, y, c, chips = _place()
        me = 2 * x + y
        local, sends = [], []
        for t, (gi, li, _) in enumerate(flat):
            cp = pltpu.make_async_copy(ins[t].at[me], outs[gi].at[me, li], lsem.at[t])
            cp.start()
            local.append(cp)
            for r, (px, py) in enumerate(chips):
                rc = pltpu.make_async_remote_copy(src_ref=ins[t].at[2 * px + py], dst_ref=outs[gi].at[me, li],
                                                  send_sem=send.at[3 * t + r], recv_sem=recv.at[3 * t + r],
                                                  device_id=(px, py, c), device_id_type=MESH)
                rc.start()
                sends.append(rc)
        for t, (gi, li, _) in enumerate(flat):
            for r, (px, py) in enumerate(chips):
                pltpu.make_async_remote_copy(src_ref=ins[t].at[me], dst_ref=outs[gi].at[2 * px + py, li],
                                             send_sem=send.at[3 * t + r], recv_sem=recv.at[3 * t + r],
                                             device_id=(px, py, c), device_id_type=MESH).wait_recv()
        for rc in sends:
            rc.wait_send()
        for cp in local:
            cp.wait()

    any_spec = pl.BlockSpec(memory_space=pl.ANY)
    return pl.pallas_call(
        body, name="scatter_grads", in_specs=[any_spec] * n, out_specs=[any_spec] * ng,
        out_shape=[jax.ShapeDtypeStruct((N_CHIPS, len(grp)) + grp[0].shape[1:], grp[0].dtype) for grp in groups],
        scratch_shapes=[pltpu.SemaphoreType.DMA((3 * n,)), pltpu.SemaphoreType.DMA((3 * n,)), pltpu.SemaphoreType.DMA((n,))],
    )(*[a for _, _, a in flat])


def _swap_with_sibling(arrs):
    n = len(arrs)

    def body(*refs):
        ins, outs = refs[:n], refs[n:2 * n]
        send, recv = refs[2 * n:]
        x, y, c, _ = _place()
        cps = []
        for t in range(n):
            rc = pltpu.make_async_remote_copy(src_ref=ins[t], dst_ref=outs[t], send_sem=send.at[t], recv_sem=recv.at[t],
                                              device_id=(x, y, 1 - c), device_id_type=MESH)
            rc.start()
            cps.append(rc)
        for rc in cps:
            rc.wait_recv()
        for rc in cps:
            rc.wait_send()

    any_spec = pl.BlockSpec(memory_space=pl.ANY)
    return pl.pallas_call(
        body, name="swap_with_sibling", in_specs=[any_spec] * n, out_specs=[any_spec] * n,
        out_shape=[jax.ShapeDtypeStruct(a.shape, a.dtype) for a in arrs],
        scratch_shapes=[pltpu.SemaphoreType.DMA((n,)), pltpu.SemaphoreType.DMA((n,))],
    )(*arrs)


def _allreduce_small(v):
    rows, cols = v.shape
    r8 = rows // (2 * N_CHIPS)
    assert r8 * 2 * N_CHIPS == rows and r8 % SUBLANES == 0, rows

    def body(v_ref, o_ref, sib_ref, cs_ref, slot_ref, send, recv):
        x, y, c, chips = _place()
        me = 2 * x + y
        sibling = (x, y, 1 - c)

        def eighth(ref, chip, core):
            return ref.at[pl.ds(pl.multiple_of((2 * chip + core) * r8, SUBLANES), r8)]

        def copy(src, dst, k, to):
            return pltpu.make_async_remote_copy(src_ref=src, dst_ref=dst, send_sem=send.at[k], recv_sem=recv.at[k],
                                                device_id=to, device_id_type=MESH)

        d2d = copy(v_ref, sib_ref, 0, sibling)
        d2d.start()
        d2d.wait_recv()
        cs_ref[...] = v_ref[...] + sib_ref[...]
        reduce_out = [copy(eighth(cs_ref, 2 * px + py, c), slot_ref.at[me], 1 + r, (px, py, c)) for r, (px, py) in enumerate(chips)]
        for cp in reduce_out:
            cp.start()
        slot_ref[me] = cs_ref[pl.ds(pl.multiple_of((2 * me + c) * r8, SUBLANES), r8), :]
        for r, (px, py) in enumerate(chips):
            copy(eighth(cs_ref, me, c), slot_ref.at[2 * px + py], 1 + r, (px, py, c)).wait_recv()
        o_ref[pl.ds(pl.multiple_of((2 * me + c) * r8, SUBLANES), r8), :] = (slot_ref[0] + slot_ref[1]) + (slot_ref[2] + slot_ref[3])
        mine = eighth(o_ref, me, c)
        hand_out = [copy(mine, mine, 4, sibling)] + [copy(mine, mine, 5 + r, (px, py, c)) for r, (px, py) in enumerate(chips)]
        for cp in hand_out:
            cp.start()
        passed_on = []
        for r, (px, py) in enumerate(chips):
            theirs = eighth(o_ref, 2 * px + py, c)
            copy(theirs, theirs, 5 + r, (px, py, c)).wait_recv()
            fw = copy(theirs, theirs, 8 + r, sibling)
            fw.start()
            passed_on.append(fw)
        sib_own = eighth(o_ref, me, 1 - c)
        copy(sib_own, sib_own, 4, sibling).wait_recv()
        for r, (px, py) in enumerate(chips):
            got = eighth(o_ref, 2 * px + py, 1 - c)
            copy(got, got, 8 + r, sibling).wait_recv()
        for cp in [d2d] + reduce_out + hand_out + passed_on:
            cp.wait_send()

    vm = pl.BlockSpec(memory_space=pltpu.VMEM)
    return pl.pallas_call(
        body, name="allreduce_small", in_specs=[vm], out_specs=vm, out_shape=jax.ShapeDtypeStruct((rows, cols), F32),
        scratch_shapes=[pltpu.VMEM((rows, cols), F32), pltpu.VMEM((rows, cols), F32), pltpu.VMEM((N_CHIPS, r8, cols), F32),
                        pltpu.SemaphoreType.DMA((11,)), pltpu.SemaphoreType.DMA((11,))],
        compiler_params=_cparams(),
    )(v)


def _pack(tensors):
    pieces = []
    for t in tensors:
        flat = t.reshape(-1)
        pad = (-flat.shape[0]) % (SUBLANES * LANES)
        pieces.append(jnp.pad(flat, (0, pad)).reshape(-1, LANES))
    rows = sum(p.shape[0] for p in pieces)
    pieces.append(jnp.zeros(((-rows) % PACK_ROW_MULTIPLE, LANES), tensors[0].dtype))
    return jnp.concatenate(pieces, axis=0)


def _unpack(buf, like):
    out, off = [], 0
    for t in like:
        size = math.prod(t.shape)
        rows = -(-size // (SUBLANES * LANES)) * SUBLANES
        out.append(buf[off:off + rows].reshape(-1)[:size].reshape(t.shape))
        off += rows
    return out


def _s5_pack_b(bb):
    gc, g, p = bb.shape
    q = S5_GROUPS_PER_BLOCK
    t = bb.reshape(gc, g // q, q, p).transpose(1, 2, 0, 3)
    eye = jnp.eye(q, dtype=bb.dtype)
    return (t[:, :, :, None, :] * eye[None, :, None, :, None]).reshape(g // q, q * gc, q * p)


def _s5_unpack_b(dbp, gc, p):
    nb = dbp.shape[0]
    q = S5_GROUPS_PER_BLOCK
    eye = jnp.eye(q, dtype=dbp.dtype)
    t = (dbp.reshape(nb, q, gc, q, p) * eye[None, :, None, :, None]).sum(axis=3)
    return t.transpose(2, 0, 1, 3).reshape(gc, nb * q, p)


def _s5_pack_c(cc):
    g, gc, p = cc.shape
    q = S5_GROUPS_PER_BLOCK
    t = cc.reshape(g // q, q, gc, p).transpose(0, 1, 3, 2)
    eye = jnp.eye(q, dtype=cc.dtype)
    return (t[:, :, :, None, :] * eye[None, :, None, :, None]).reshape(g // q, q * p, q * gc)


def _s5_unpack_c(dcp, gc, p):
    nb = dcp.shape[0]
    q = S5_GROUPS_PER_BLOCK
    eye = jnp.eye(q, dtype=dcp.dtype)
    t = (dcp.reshape(nb, q, p, q, gc) * eye[None, :, None, :, None]).sum(axis=3)
    return t.transpose(0, 1, 3, 2).reshape(nb * q, gc, p)


def _split2(m):
    return m.arr[:, 0]


def kernel(x, norm_mix_g, norm_ffn_g, norm_final_g, rg_w_in, rg_conv_w, rg_conv_b, rg_w_a, rg_b_a, rg_w_x, rg_b_x, rg_lambda, rg_w_out, s5_w_in, s5_a_re, s5_a_im, s5_log_dt, s5_b_re, s5_b_im, s5_c_re, s5_c_im, s5_d, s5_w_glu, s5_w_out, ffn_w_up, ffn_conv_w, ffn_conv_b, ffn_w_down, loss_target, m_norm_mix_g, m_norm_ffn_g, m_norm_final_g, m_rg_w_in, m_rg_conv_w, m_rg_conv_b, m_rg_w_a, m_rg_b_a, m_rg_w_x, m_rg_b_x, m_rg_lambda, m_rg_w_out, m_s5_w_in, m_s5_a_re, m_s5_a_im, m_s5_log_dt, m_s5_b_re, m_s5_b_im, m_s5_c_re, m_s5_c_im, m_s5_d, m_s5_w_glu, m_s5_w_out, m_ffn_w_up, m_ffn_conv_w, m_ffn_conv_b, m_ffn_w_down, v_norm_mix_g, v_norm_ffn_g, v_norm_final_g, v_rg_w_in, v_rg_conv_w, v_rg_conv_b, v_rg_w_a, v_rg_b_a, v_rg_w_x, v_rg_b_x, v_rg_lambda, v_rg_w_out, v_s5_w_in, v_s5_a_re, v_s5_a_im, v_s5_log_dt, v_s5_b_re, v_s5_b_im, v_s5_c_re, v_s5_c_im, v_s5_d, v_s5_w_glu, v_s5_w_out, v_ffn_w_up, v_ffn_conv_w, v_ffn_conv_b, v_ffn_w_down):
    w = dict(zip(PARAM_NAMES, (norm_mix_g, norm_ffn_g, norm_final_g, rg_w_in, rg_conv_w, rg_conv_b, rg_w_a, rg_b_a, rg_w_x, rg_b_x,
                               rg_lambda, rg_w_out, s5_w_in, s5_a_re, s5_a_im, s5_log_dt, s5_b_re, s5_b_im, s5_c_re, s5_c_im, s5_d,
                               s5_w_glu, s5_w_out, ffn_w_up, ffn_conv_w, ffn_conv_b, ffn_w_down)))
    mom = dict(zip(PARAM_NAMES, (m_norm_mix_g, m_norm_ffn_g, m_norm_final_g, m_rg_w_in, m_rg_conv_w, m_rg_conv_b, m_rg_w_a, m_rg_b_a,
                                 m_rg_w_x, m_rg_b_x, m_rg_lambda, m_rg_w_out, m_s5_w_in, m_s5_a_re, m_s5_a_im, m_s5_log_dt, m_s5_b_re,
                                 m_s5_b_im, m_s5_c_re, m_s5_c_im, m_s5_d, m_s5_w_glu, m_s5_w_out, m_ffn_w_up, m_ffn_conv_w,
                                 m_ffn_conv_b, m_ffn_w_down)))
    vel = dict(zip(PARAM_NAMES, (v_norm_mix_g, v_norm_ffn_g, v_norm_final_g, v_rg_w_in, v_rg_conv_w, v_rg_conv_b, v_rg_w_a, v_rg_b_a,
                                 v_rg_w_x, v_rg_b_x, v_rg_lambda, v_rg_w_out, v_s5_w_in, v_s5_a_re, v_s5_a_im, v_s5_log_dt, v_s5_b_re,
                                 v_s5_b_im, v_s5_c_re, v_s5_c_im, v_s5_d, v_s5_w_glu, v_s5_w_out, v_ffn_w_up, v_ffn_conv_w,
                                 v_ffn_conv_b, v_ffn_w_down)))
    _, s, d = x.shape
    depth = norm_mix_g.shape[0]
    n_grp, n_state = s5_a_re.shape[1], s5_a_re.shape[2]
    gc = s5_b_re.shape[3]
    d_ff = ffn_w_down.shape[1] * N_CHIPS
    s5_ts = min(256, s)
    s5_perm = _segment_perm(s5_ts)

    wb = {n: (w[n].astype(BF16) if n in BIG else w[n]) for n in SHARDED}
    gath = {}

    def mixer_keys(i):
        return [(n, i // 2) for n in MIXER_SHARDED[i % 2]] if i < depth else []

    def gather_side(keys):
        return _gather_side([wb[n][l] for n, l in keys])

    def put(keys, arrs):
        for k, a in zip(keys, arrs):
            gath[k] = a

    def wcol(n, l):
        return Mat(gath[(n, l)][:, None], 0, 'c')

    def wrow(n, l):
        g = gath[(n, l)]
        return Mat(g.reshape(1, 1, N_CHIPS * g.shape[1], g.shape[2]), 0, 'c')

    def rg_cw(l):
        return gath[('rg_conv_w', l)].transpose(1, 0, 2).reshape(RG_CONV_W, d)

    def s5_dv(l):
        return gath[('s5_d', l)].reshape(1, d)

    def f_cw(l):
        return gath[('ffn_conv_w', l)].transpose(1, 0, 2).reshape(FFN_CONV_W, 2, d_ff).transpose(1, 0, 2)

    tm = min(1024, s)
    d_up = 2 * d_ff // N_CHIPS
    f_cb = ffn_conv_b.reshape(depth, 2, 1, d_ff)
    put(mixer_keys(0), _run_side("gather_first", gather_side(mixer_keys(0))))

    h = x.reshape(s, d)
    saved = []
    for i in range(depth):
        j = i // 2
        sv = {'h_in': h}
        hn = _rms_fwd(h, norm_mix_g[i:i + 1])
        sv['hn'] = hn
        up_keys = [('ffn_w_up', i), ('ffn_conv_w', i)]
        if i % 2 == 0:
            xg = _mm("rg_in", 'nn', act(hn), wcol('rg_w_in', j), out_parts=2, tm=tm, tn=512, tk=d)
            xg2 = _split2(xg)
            wa, wx = rg_w_a[j].astype(BF16), rg_w_x[j].astype(BF16)
            ba, bx = rg_b_a[j].reshape(1, d), rg_b_x[j].reshape(1, d)
            (xr, hs, y), got = _rg_fwd(xg2, rg_cw(j), rg_conv_b[j:j + 1], wa, ba, wx, bx, rg_lambda[j:j + 1],
                                       side=gather_side(up_keys))
            put(up_keys, got)
            sv.update(xg2=xg2, xr=xr, hs=hs, y=y, wa=wa, wx=wx, ba=ba, bx=bx)
            h = _mm("rg_out", 'nn', act(y), wrow('rg_w_out', j), res=act(h), tm=tm, tn=d, tk=d).arr[0, 0]
        else:
            u = _mm("s5_in", 'nn', act(hn), wrow('s5_w_in', j), tm=tm, tn=d, tk=d).arr[0, 0]
            bt_re, bt_im = s5_b_re[j].transpose(2, 0, 1), s5_b_im[j].transpose(2, 0, 1)
            ldt = s5_log_dt[j].reshape(n_grp, 1)
            tab_r, tab_i, rtab_r, rtab_i, bbr, bbi = _s5_tables3(s5_a_re[j], s5_a_im[j], ldt, bt_re, bt_im, seg=s5_ts // SUBLANES)
            nn_ = n_grp * n_state
            tab_r, tab_i, rtab_r, rtab_i = (t.reshape(5, SUBLANES, nn_) for t in (tab_r, tab_i, rtab_r, rtab_i))
            prm = dict(bp_r=_s5_pack_b(bbr).astype(BF16), bp_i=_s5_pack_b(bbi).astype(BF16),
                       cp_r=_s5_pack_c(s5_c_re[j]).astype(BF16), cp_i=_s5_pack_c(s5_c_im[j]).astype(BF16), dvec=s5_dv(j))
            (hr, hi, ypre, gy), got = _s5_fwd3(u, s5_perm, s5_perm.T, tab_r, tab_i, ts=s5_ts, side=gather_side(up_keys), **prm)
            sv.update(rtab_r=rtab_r, rtab_i=rtab_i)
            put(up_keys, got)
            gl = _mm("s5_glu", 'nn', act(gy), wcol('s5_w_glu', j), out_parts=2, tm=tm, tn=512, tk=d)
            gl2 = _split2(gl)
            o = _glu(gl2)
            sv.update(u=u, prm=prm, hr=hr, hi=hi, ypre=ypre, gy=gy, gl2=gl2, o=o, bt_re=bt_re, bt_im=bt_im, ldt=ldt)
            h = _mm("s5_out", 'nn', act(o), wrow('s5_w_out', j), res=act(h), tm=tm, tn=d, tk=d).arr[0, 0]
        sv['h_mid'] = h
        hn2 = _rms_fwd(h, norm_ffn_g[i:i + 1])
        next_keys = [('ffn_w_down', i)] + mixer_keys(i + 1)
        (up2, c2, a_ffn), got = _ffn_up_act(hn2, gath[('ffn_w_up', i)], f_cw(i), f_cb[i], side=gather_side(next_keys))
        put(next_keys, got)
        sv.update(hn2=hn2, up2=up2, c2=c2, act=a_ffn)
        h = _mm("ffn_down", 'nn', act(a_ffn), wrow('ffn_w_down', i), res=act(h), tm=tm, tn=d, tk=d_ff // 2).arr[0, 0]
        saved.append(sv)

    loss_row, dh, dg_final = _loss_and_grad(h, norm_final_g.reshape(1, d), loss_target.reshape(s, d))
    loss = lax.psum(loss_row[0, 0], ("x", "y", "c"))

    gl_ = {n: [None] * w[n].shape[0] for n in PARAM_NAMES if n != 'norm_final_g'}
    recvd = {}

    def as4(n, a):
        return a.reshape((N_CHIPS,) + w[n].shape[1:])

    def scatter_side(keys):
        arrs, shapes, views = [], [], []
        for n, l in keys:
            shape = (N_CHIPS,) + w[n].shape[1:]
            halves = False
            arrs.append(gl_[n][l] if halves else gl_[n][l].reshape(shape))
            views.append(_halves_view if halves else None)
            shapes.append(shape)
        return _scatter_side(arrs, shapes, views)

    def record(keys, arrs):
        for k, a in zip(keys, arrs):
            recvd[k] = a

    pending = None
    for i in reversed(range(depth)):
        j = i // 2
        sv = saved[i]
        gl_['ffn_w_down'][i] = _mm("ffn_down_dw", 'tn', act(sv['act']), act(dh), out_dtype=BF16, tm=d_ff // 2, tn=d, tk=tm).arr
        (dup2, dcw2, dcb2), got = _ffn_bwd_fused(dh, gath[('ffn_w_down', i)].reshape(d_ff, d), sv['up2'], sv['c2'], f_cw(i),
                                                 side=scatter_side(pending) if pending else None)
        if pending:
            record(pending, got)
        gl_['ffn_conv_w'][i] = dcw2.transpose(1, 0, 2).reshape(FFN_CONV_W, 2 * d_ff)
        gl_['ffn_conv_b'][i] = dcb2.reshape(2 * d_ff)
        dup = Mat(dup2[:, None], 0, 'c')
        gl_['ffn_w_up'][i] = _mm("ffn_up_dw", 'tn', act(sv['hn2']), dup, out_parts=N_CHIPS, out_dtype=BF16, tm=d, tn=d_up, tk=tm).arr
        dh, dg = _mm_rms_bwd("ffn_up_dx", dup, wcol('ffn_w_up', i), sv['h_mid'], norm_ffn_g[i:i + 1], dh, tm=tm, tk=d_up)
        gl_['norm_ffn_g'][i] = dg[0]
        ffn_keys = [('ffn_w_up', i), ('ffn_w_down', i)]
        if i % 2 == 0:
            dy = _mm("rg_out_dx", 'nt', act(dh), wrow('rg_w_out', j), tm=tm, tn=d, tk=d).arr[0, 0]
            gl_['rg_w_out'][j] = _mm("rg_out_dw", 'tn', act(sv['y']), act(dh), out_dtype=BF16, tm=d, tn=d, tk=tm).arr
            (dxg2, dcw, dcb, dwa, dba, dwx, dbx, dlam), got = _rg_bwd(
                dy, sv['xg2'], sv['xr'], sv['hs'], rg_cw(j), sv['wa'], sv['ba'], sv['wx'], sv['bx'], rg_lambda[j:j + 1],
                side=scatter_side(ffn_keys))
            record(ffn_keys, got)
            gl_['rg_conv_w'][j] = dcw
            gl_['rg_conv_b'][j] = dcb[0]
            gl_['rg_w_a'][j], gl_['rg_w_x'][j] = dwa, dwx
            gl_['rg_b_a'][j], gl_['rg_b_x'][j] = dba.reshape(rg_b_a.shape[1:]), dbx.reshape(rg_b_x.shape[1:])
            gl_['rg_lambda'][j] = dlam[0]
            dxg = Mat(dxg2[:, None], 0, 'c')
            gl_['rg_w_in'][j] = _mm("rg_in_dw", 'tn', act(sv['hn']), dxg, out_parts=N_CHIPS, out_dtype=BF16, tm=d, tn=512, tk=tm).arr
            mix_dx = ("rg_in_dx", dxg, wcol('rg_w_in', j), 512)
            pending = [('rg_w_in', j), ('rg_w_out', j)]
        else:
            d_o = _mm("s5_out_dx", 'nt', act(dh), wrow('s5_w_out', j), tm=tm, tn=d, tk=d).arr[0, 0]
            gl_['s5_w_out'][j] = _mm("s5_out_dw", 'tn', act(sv['o']), act(dh), out_dtype=BF16, tm=d, tn=d, tk=tm).arr
            dgl2 = _glu_bwd(sv['gl2'], d_o)
            dgl = Mat(dgl2[:, None], 0, 'c')
            gl_['s5_w_glu'][j] = _mm("s5_glu_dw", 'tn', act(sv['gy']), dgl, out_parts=N_CHIPS, out_dtype=BF16, tm=d, tn=512, tk=tm).arr
            dgy = _mm("s5_glu_dx", 'nt', dgl, wcol('s5_w_glu', j), tm=tm, tn=d, tk=512).arr[0, 0]
            (du, dar, dai, dbpr, dbpi, dcpr, dcpi, dd), got = _s5_bwd3(
                dgy, sv['ypre'], sv['u'], sv['hr'], sv['hi'], s5_perm, s5_perm.T, sv['rtab_r'], sv['rtab_i'], ts=s5_ts,
                side=scatter_side(ffn_keys), **sv['prm'])
            record(ffn_keys, got)
            gl_['s5_d'][j] = dd[0]
            gl_['s5_c_re'][j] = _s5_unpack_c(dcpr, gc, n_state)
            gl_['s5_c_im'][j] = -_s5_unpack_c(dcpi, gc, n_state)
            d_are, d_aim, d_ldt, d_btr, d_bti = _s5_params_bwd(
                s5_a_re[j], s5_a_im[j], sv['ldt'], sv['bt_re'], sv['bt_im'], dar.reshape(n_grp, n_state), dai.reshape(n_grp, n_state),
                _s5_unpack_b(dbpr, gc, n_state), _s5_unpack_b(dbpi, gc, n_state))
            gl_['s5_a_re'][j], gl_['s5_a_im'][j], gl_['s5_log_dt'][j] = d_are, d_aim, d_ldt[:, 0]
            gl_['s5_b_re'][j], gl_['s5_b_im'][j] = d_btr.transpose(1, 2, 0), d_bti.transpose(1, 2, 0)
            dum = act(du)
            gl_['s5_w_in'][j] = _mm("s5_in_dw", 'tn', act(sv['hn']), dum, out_dtype=BF16, tm=d, tn=d, tk=tm).arr
            mix_dx = ("s5_in_dx", dum, wrow('s5_w_in', j), d)
            pending = [('s5_w_in', j), ('s5_w_glu', j), ('s5_w_out', j)]
        dh, dg = _mm_rms_bwd(mix_dx[0], mix_dx[1], mix_dx[2], sv['h_in'], norm_mix_g[i:i + 1], dh, tm=tm, tk=mix_dx[3])
        gl_['norm_mix_g'][i] = dg[0]
    grad_x = dh.reshape(x.shape)
    record(pending, _run_side("scatter_last", scatter_side(pending)))

    order = sorted(BIG, key=lambda n: math.prod(w[n].shape))
    chip_sums = {}
    for n in order:
        cols = w[n].shape[-1]
        chip_sums[n] = _sum_parts([recvd[(n, l)].reshape(N_CHIPS, -1, cols) for l in range(w[n].shape[0])]).reshape(-1, cols)
    results = {}
    theirs = _run_side("swap_first", _sibling_side([chip_sums[order[0]]]))[0]
    for k, n in enumerate(order):
        cols = w[n].shape[-1]
        nxt = order[k + 1] if k + 1 < len(order) else None
        (delta, new_m, new_v, grad), got = _adamw(w[n].reshape(-1, cols), [chip_sums[n], theirs], mom[n].reshape(-1, cols),
                                                  vel[n].reshape(-1, cols), side=_sibling_side([chip_sums[nxt]]) if nxt else None)
        results[n] = [o.reshape(w[n].shape) for o in (grad, delta, new_m, new_v)]
        theirs = got[0] if nxt else None

    small = REPLICATED + SMALL_SHARDED
    local = [dg_final.reshape(d) if n == 'norm_final_g' else jnp.stack(gl_[n]) for n in small]
    summed = _unpack(_allreduce_small(_pack(local)), local)
    me = 2 * lax.axis_index("x") + lax.axis_index("y")
    for n, g in zip(small, summed):
        if n in SMALL_SHARDED:
            g = lax.dynamic_slice_in_dim(g, me * w[n].shape[-1], w[n].shape[-1], axis=g.ndim - 1)
        view = (-1, w[n].shape[-1])
        (delta, new_m, new_v), _ = _adamw(w[n].reshape(view), [g.reshape(view)], mom[n].reshape(view), vel[n].reshape(view))
        results[n] = [g] + [o.reshape(w[n].shape) for o in (delta, new_m, new_v)]

    return (loss, grad_x, *[results[n][0] for n in PARAM_NAMES], *[results[n][1] for n in PARAM_NAMES],
            *[results[n][2] for n in PARAM_NAMES], *[results[n][3] for n in PARAM_NAMES])
```

```python
import functools
import math

import jax
import jax.numpy as jnp
from jax import lax
from jax.experimental import pallas as pl
from jax.experimental.pallas import tpu as pltpu

F32 = jnp.float32
BF16 = jnp.bfloat16
MESH = pl.DeviceIdType.MESH

NORM_EPS = 1e-6
RG_HEADS = 8
RG_CONV_W = 4
RG_C = 8.0
S5_GC = 16
S5_P = 64
S5_GROUPS_PER_BLOCK = 8
FFN_CONV_W = 3
N_CHIPS = 4
ADAM_LR, ADAM_B1, ADAM_B2, ADAM_EPS, ADAM_WD, ADAM_STEP = 0.001, 0.9, 0.999, 1e-08, 0.01, 10
VMEM_LIMIT_BYTES = 56 * 1024 * 1024
SUBLANES = 8
LANES = 128

PARAM_NAMES = ['norm_mix_g', 'norm_ffn_g', 'norm_final_g', 'rg_w_in', 'rg_conv_w', 'rg_conv_b', 'rg_w_a', 'rg_b_a', 'rg_w_x',
               'rg_b_x', 'rg_lambda', 'rg_w_out', 's5_w_in', 's5_a_re', 's5_a_im', 's5_log_dt', 's5_b_re', 's5_b_im', 's5_c_re',
               's5_c_im', 's5_d', 's5_w_glu', 's5_w_out', 'ffn_w_up', 'ffn_conv_w', 'ffn_conv_b', 'ffn_w_down']
SHARDED = ['rg_w_in', 'rg_conv_w', 'rg_w_out', 's5_w_in', 's5_d', 's5_w_glu', 's5_w_out', 'ffn_w_up', 'ffn_conv_w', 'ffn_w_down']
BIG = ['rg_w_in', 'rg_w_out', 's5_w_in', 's5_w_glu', 's5_w_out', 'ffn_w_up', 'ffn_w_down']
ROW_SHARDED = ['rg_w_out', 's5_w_in', 's5_w_out', 'ffn_w_down']
SMALL_SHARDED = ['rg_conv_w', 's5_d', 'ffn_conv_w']
MIXER_SHARDED = [['rg_w_in', 'rg_conv_w', 'rg_w_out'], ['s5_w_in', 's5_d', 's5_w_glu', 's5_w_out']]
FFN_SHARDED = ['ffn_w_up', 'ffn_conv_w', 'ffn_w_down']
REPLICATED = [n for n in PARAM_NAMES if n not in SHARDED]


def _cparams():
    return pltpu.CompilerParams(vmem_limit_bytes=VMEM_LIMIT_BYTES)


_GELU_C = math.sqrt(2.0 / math.pi)
_GELU_K = 0.044715


def _gelu(x):
    return 0.5 * x * (1.0 + jnp.tanh(_GELU_C * (x + _GELU_K * x * x * x)))


def _gelu_and_grad(x):
    t = jnp.tanh(_GELU_C * (x + _GELU_K * x * x * x))
    g = 0.5 * x * (1.0 + t)
    dg = 0.5 * (1.0 + t) + 0.5 * x * (1.0 - t * t) * (_GELU_C * (1.0 + 3.0 * _GELU_K * x * x))
    return g, dg


def _sigmoid(x):
    return jax.nn.sigmoid(x)


def _neg_expm1(x):
    series = -(x * (1.0 + x * (0.5 + x * (1.0 / 6 + x * (1.0 / 24 + x * (1.0 / 120 + x * (1.0 / 720)))))))
    return jnp.where(x > -0.25, series, 1.0 - jnp.exp(x))


def _softplus(z):
    return jnp.maximum(z, 0.0) + jnp.log1p(jnp.exp(-jnp.abs(z)))


def _rows(shape):
    return lax.broadcasted_iota(jnp.int32, shape, 0)


def _shift_down(x, halo, k):
    ext = jnp.concatenate([halo, x], axis=0)
    return pltpu.roll(ext, k, 0)[SUBLANES:]


def _shift_up(x, halo, k):
    ext = jnp.concatenate([x, halo], axis=0)
    n = ext.shape[0]
    return pltpu.roll(ext, n - k, 0)[:x.shape[0]]


def _scan_real_fwd(a, b):
    n = a.shape[0]
    row = _rows(a.shape)
    sh = 1
    while sh < n:
        ok = row >= sh
        b = a * jnp.where(ok, pltpu.roll(b, sh, 0), 0.0) + b
        if sh * 2 < n:
            a = a * jnp.where(ok, pltpu.roll(a, sh, 0), 1.0)
        sh *= 2
    return b


def _scan_real_rev(c, d):
    n = c.shape[0]
    row = _rows(c.shape)
    sh = 1
    while sh < n:
        ok = row < n - sh
        d = c * jnp.where(ok, pltpu.roll(d, n - sh, 0), 0.0) + d
        if sh * 2 < n:
            c = c * jnp.where(ok, pltpu.roll(c, n - sh, 0), 1.0)
        sh *= 2
    return d


def _scan_cplx(br, bi, pr_ref, pi_ref, reverse):
    n = br.shape[0]
    row = _rows(br.shape)
    sh, k = 1, 0
    while sh < n:
        pr = pr_ref[k:k + 1, :]
        pi = pi_ref[k:k + 1, :]
        if reverse:
            ok = row < n - sh
            sr = jnp.where(ok, pltpu.roll(br, n - sh, 0), 0.0)
            si = jnp.where(ok, pltpu.roll(bi, n - sh, 0), 0.0)
        else:
            ok = row >= sh
            sr = jnp.where(ok, pltpu.roll(br, sh, 0), 0.0)
            si = jnp.where(ok, pltpu.roll(bi, sh, 0), 0.0)
        br, bi = br + pr * sr - pi * si, bi + pr * si + pi * sr
        sh *= 2
        k += 1
    return br, bi


RG_LANE_CHUNK = 512


def _real_slab_scan(a_ref, b_ref, out_ref, carry_ref, reverse):
    t, c = a_ref.shape
    nsl = t // SUBLANES
    lc = min(RG_LANE_CHUNK, c)
    row8 = _rows((SUBLANES, lc))
    for q in range(c // lc):
        sl = slice(q * lc, (q + 1) * lc)

        def slab(jj, carry, sl=sl):
            j = nsl - 1 - jj if reverse else jj
            r0 = pl.multiple_of(j * SUBLANES, SUBLANES)
            a, b = a_ref[pl.ds(r0, SUBLANES), sl], b_ref[pl.ds(r0, SUBLANES), sl]
            for k in range(3):
                sh = 1 << k
                keep = row8 < SUBLANES - sh if reverse else row8 >= sh
                amount = SUBLANES - sh if reverse else sh
                b = a * jnp.where(keep, pltpu.roll(b, amount, 0), 0.0) + b
                a = a * jnp.where(keep, pltpu.roll(a, amount, 0), 1.0)
            x = b + a * jnp.broadcast_to(carry, b.shape)
            out_ref[pl.ds(r0, SUBLANES), sl] = x
            return x[:1, :] if reverse else x[SUBLANES - 1:, :]

        carry_ref[:, sl] = lax.fori_loop(0, nsl, slab, carry_ref[:, sl], unroll=2)


class Mat:
    def __init__(self, arr, l=0, split='c'):
        assert arr.ndim == 4
        self.arr, self.l, self.split = arr, l, split
        p, _, r, c = arr.shape
        self.shape = (r, c * p) if split == 'c' else (r * p, c)

    def spec(self, tr, tc, rc):
        p, _, r, c = self.arr.shape
        l = self.l
        assert r % tr == 0 and c % tc == 0, (self.arr.shape, tr, tc)
        if self.split == 'c':
            per = c // tc
            return pl.BlockSpec((None, None, tr, tc), lambda i, j, k: (rc(i, j, k)[1] // per, l, rc(i, j, k)[0], rc(i, j, k)[1] % per))
        per = r // tr
        return pl.BlockSpec((None, None, tr, tc), lambda i, j, k: (rc(i, j, k)[0] // per, l, rc(i, j, k)[0] % per, rc(i, j, k)[1]))


def act(x, parts=1):
    s, c = x.shape
    return Mat(x.reshape(s, parts, c // parts).transpose(1, 0, 2)[:, None] if parts > 1 else x[None, None])


def _mm(name, mode, a, b, *, out_parts=1, out_split='c', out_dtype=F32, res=None, tm=512, tn=512, tk=512):
    if mode == 'nn':
        (m, kk), (kb, n) = a.shape, b.shape
    elif mode == 'nt':
        (m, kk), (n, kb) = a.shape, b.shape
    else:
        (kk, m), (kb, n) = a.shape, b.shape
    assert kk == kb, (name, a.shape, b.shape)
    tm, tn, tk = min(tm, m), min(tn, n), min(tk, kk)
    assert m % tm == 0 and n % tn == 0 and kk % tk == 0, (name, m, n, kk, tm, tn, tk)
    nk = kk // tk
    if mode == 'nn':
        a_spec = a.spec(tm, tk, lambda i, j, k: (i, k))
        b_spec = b.spec(tk, tn, lambda i, j, k: (k, j))
        dims = (((1,), (0,)), ((), ()))
    elif mode == 'nt':
        a_spec = a.spec(tm, tk, lambda i, j, k: (i, k))
        b_spec = b.spec(tn, tk, lambda i, j, k: (j, k))
        dims = (((1,), (1,)), ((), ()))
    else:
        a_spec = a.spec(tk, tm, lambda i, j, k: (k, i))
        b_spec = b.spec(tk, tn, lambda i, j, k: (k, j))
        dims = (((0,), (0,)), ((), ()))
    if out_split == 'c':
        out_arr = jax.ShapeDtypeStruct((out_parts, 1, m, n // out_parts), out_dtype)
    else:
        out_arr = jax.ShapeDtypeStruct((out_parts, 1, m // out_parts, n), out_dtype)
    out_mat = Mat(out_arr, 0, out_split)
    o_spec = out_mat.spec(tm, tn, lambda i, j, k: (i, j))
    has_res = res is not None

    def body(*refs):
        if has_res:
            a_ref, b_ref, r_ref, o_ref = refs[:4]
        else:
            a_ref, b_ref, o_ref = refs[:3]
        prod = lax.dot_general(a_ref[...].astype(BF16), b_ref[...].astype(BF16), dims, preferred_element_type=F32)

        def finish(acc):
            if has_res:
                acc = acc + r_ref[...]
            o_ref[...] = acc.astype(out_dtype)

        if nk == 1:
            finish(prod)
        else:
            acc_ref = refs[-1]
            k = pl.program_id(2)

            @pl.when(k == 0)
            def _():
                acc_ref[...] = prod

            @pl.when(k > 0)
            def _():
                acc_ref[...] += prod

            @pl.when(k == nk - 1)
            def _():
                finish(acc_ref[...])

    in_specs = [a_spec, b_spec]
    args = [a.arr, b.arr]
    if has_res:
        in_specs.append(res.spec(tm, tn, lambda i, j, k: (i, j)))
        args.append(res.arr)
    out = pl.pallas_call(
        body, name=name, grid=(m // tm, n // tn, nk), in_specs=in_specs, out_specs=o_spec, out_shape=out_arr,
        scratch_shapes=[pltpu.VMEM((tm, tn), F32)] if nk > 1 else [], compiler_params=_cparams(),
    )(*args)
    return Mat(out, 0, out_split)


def _rms_fwd(h, g, ts=512):
    s, d = h.shape
    ts = min(ts, s)

    def body(h_ref, g_ref, o_ref):
        x = h_ref[...]
        var = jnp.mean(x * x, axis=-1, keepdims=True)
        o_ref[...] = (x * lax.rsqrt(var + NORM_EPS) * g_ref[...]).astype(BF16)

    return pl.pallas_call(
        body, name="rms_fwd", grid=(s // ts,),
        in_specs=[pl.BlockSpec((ts, d), lambda i: (i, 0)), pl.BlockSpec((1, d), lambda i: (0, 0))],
        out_specs=pl.BlockSpec((ts, d), lambda i: (i, 0)), out_shape=jax.ShapeDtypeStruct((s, d), BF16),
        compiler_params=_cparams(),
    )(h, g)


def _rms_bwd(h, g, dhn, dh_in, ts=512):
    s, d = h.shape
    ts = min(ts, s)

    def body(h_ref, g_ref, dhn_ref, dhin_ref, dh_ref, dg_ref):
        i = pl.program_id(0)
        x = h_ref[...]
        rstd = lax.rsqrt(jnp.mean(x * x, axis=-1, keepdims=True) + NORM_EPS)
        xhat = x * rstd
        dhn_v = dhn_ref[...]
        dxh = dhn_v * g_ref[...]
        dh_ref[...] = dhin_ref[...] + rstd * (dxh - xhat * jnp.mean(dxh * xhat, axis=-1, keepdims=True))
        part = jnp.sum(dhn_v * xhat, axis=0, keepdims=True)

        @pl.when(i == 0)
        def _():
            dg_ref[...] = part

        @pl.when(i > 0)
        def _():
            dg_ref[...] += part

    row = pl.BlockSpec((ts, d), lambda i: (i, 0))
    vec = pl.BlockSpec((1, d), lambda i: (0, 0))
    return pl.pallas_call(
        body, name="rms_bwd", grid=(s // ts,), in_specs=[row, vec, row, row], out_specs=[row, vec],
        out_shape=[jax.ShapeDtypeStruct((s, d), F32), jax.ShapeDtypeStruct((1, d), F32)], compiler_params=_cparams(),
    )(h, g, dhn, dh_in)


def _loss_and_grad(h, g, tgt, ts=512):
    s, d = h.shape
    ts = min(ts, s)

    def body(h_ref, g_ref, t_ref, loss_ref, dh_ref, dg_ref):
        i = pl.program_id(0)
        x = h_ref[...]
        gv = g_ref[...]
        rstd = lax.rsqrt(jnp.mean(x * x, axis=-1, keepdims=True) + NORM_EPS)
        xhat = x * rstd
        err = xhat * gv - t_ref[...]
        dy = err * (1.0 / d)
        dxh = dy * gv
        dh_ref[...] = rstd * (dxh - xhat * jnp.mean(dxh * xhat, axis=-1, keepdims=True))
        part = jnp.sum(dy * xhat, axis=0, keepdims=True)
        lpart = jnp.broadcast_to(jnp.sum(jnp.sum(err * err, axis=0, keepdims=True), axis=1, keepdims=True) * (0.5 / d), (1, LANES))

        @pl.when(i == 0)
        def _():
            dg_ref[...] = part
            loss_ref[...] = lpart

        @pl.when(i > 0)
        def _():
            dg_ref[...] += part
            loss_ref[...] += lpart

    row = pl.BlockSpec((ts, d), lambda i: (i, 0))
    vec = pl.BlockSpec((1, d), lambda i: (0, 0))
    return pl.pallas_call(
        body, name="loss_and_grad", grid=(s // ts,), in_specs=[row, vec, row],
        out_specs=[pl.BlockSpec((1, LANES), lambda i: (0, 0)), row, vec],
        out_shape=[jax.ShapeDtypeStruct((1, LANES), F32), jax.ShapeDtypeStruct((s, d), F32), jax.ShapeDtypeStruct((1, d), F32)],
        compiler_params=_cparams(),
    )(h, g, tgt)


def _halo_before(ts, nrow8):
    return lambda i: jnp.maximum(i * (ts // SUBLANES) - 1, 0)


def _ffn_act(up2, conv_w2, conv_b2, ts=512, tn=512, side=None):
    _, s, f = up2.shape
    ts, tn = min(ts, s), min(tn, f)
    kw = FFN_CONV_W

    def body(up_ref, halo_ref, w_ref, b_ref, o_ref):
        i = pl.program_id(0)
        cs = []
        for h in range(2):
            x = up_ref[h]
            halo = jnp.where(i == 0, 0.0, halo_ref[h])
            c = b_ref[h] + w_ref[h, kw - 1:kw, :] * x
            for sft in range(1, kw):
                c = c + w_ref[h, kw - 1 - sft:kw - sft, :] * _shift_down(x, halo, sft)
            cs.append(c)
        o_ref[...] = (_gelu(cs[0]) * cs[1]).astype(BF16)

    hb = ts // SUBLANES
    g0, g1 = s // ts, f // tn
    outs, side_outs = _call_with_side(
        body, side, lambda: (pl.program_id(0) == 0) & (pl.program_id(1) == 0),
        lambda: (pl.program_id(0) == g0 - 1) & (pl.program_id(1) == g1 - 1),
        name="ffn_act", grid=(g0, g1),
        in_specs=[pl.BlockSpec((2, ts, tn), lambda i, j: (0, i, j)),
                  pl.BlockSpec((2, SUBLANES, tn), lambda i, j: (0, jnp.maximum(i * hb - 1, 0), j)),
                  pl.BlockSpec((2, kw, tn), lambda i, j: (0, 0, j)),
                  pl.BlockSpec((2, 1, tn), lambda i, j: (0, 0, j))],
        out_specs=[pl.BlockSpec((ts, tn), lambda i, j: (i, j))], out_shape=[jax.ShapeDtypeStruct((s, f), BF16)],
        scratch_shapes=[], args=(up2, up2, conv_w2, conv_b2))
    return outs[0], side_outs


def _ffn_bwd(up2, dact, conv_w2, conv_b2, ts=256, tn=512, side=None):
    _, s, f = up2.shape
    ts, tn = min(ts, s), min(tn, f)
    kw = FFN_CONV_W
    nt = s // ts
    hb = ts // SUBLANES
    last8 = s // SUBLANES - 1

    def body(up_ref, hb_ref, ha_ref, da_ref, dah_ref, w_ref, b_ref, dup_ref, dw_ref, db_ref):
        i = pl.program_id(1)
        first, last = i == 0, i == nt - 1
        ce, xs = [], []
        for h in range(2):
            x = up_ref[h]
            before = jnp.where(first, 0.0, hb_ref[h])
            after = ha_ref[h]
            ext = jnp.concatenate([before, x, after], axis=0)
            c = b_ref[h] + w_ref[h, kw - 1:kw, :] * ext
            shifted = [ext]
            for sft in range(1, kw):
                sh = pltpu.roll(ext, sft, 0)
                shifted.append(sh)
                c = c + w_ref[h, kw - 1 - sft:kw - sft, :] * sh
            ce.append(c[SUBLANES:])
            xs.append([sh[SUBLANES:SUBLANES + ts] for sh in shifted])
        da = jnp.concatenate([da_ref[...], jnp.where(last, 0.0, dah_ref[...])], axis=0)
        g1, dg1 = _gelu_and_grad(ce[0])
        dcs = [da * ce[1] * dg1, da * g1]
        for h in range(2):
            dc = dcs[h]
            n = dc.shape[0]
            dup = w_ref[h, kw - 1:kw, :] * dc[:ts]
            for sft in range(1, kw):
                dup = dup + w_ref[h, kw - 1 - sft:kw - sft, :] * pltpu.roll(dc, n - sft, 0)[:ts]
            dup_ref[h] = dup.astype(BF16)
            dct = dc[:ts]
            dbp = jnp.sum(dct, axis=0, keepdims=True)
            dwp = [jnp.sum(dct * xs[h][kw - 1 - k], axis=0, keepdims=True) for k in range(kw)]

            @pl.when(first)
            def _():
                db_ref[h] = dbp
                for k in range(kw):
                    dw_ref[h, k:k + 1, :] = dwp[k]

            @pl.when(i > 0)
            def _():
                db_ref[h] += dbp
                for k in range(kw):
                    dw_ref[h, k:k + 1, :] += dwp[k]

    g0 = f // tn
    return _call_with_side(
        body, side, lambda: (pl.program_id(0) == 0) & (pl.program_id(1) == 0),
        lambda: (pl.program_id(0) == g0 - 1) & (pl.program_id(1) == nt - 1),
        name="ffn_bwd", grid=(g0, nt),
        in_specs=[pl.BlockSpec((2, ts, tn), lambda j, i: (0, i, j)),
                  pl.BlockSpec((2, SUBLANES, tn), lambda j, i: (0, jnp.maximum(i * hb - 1, 0), j)),
                  pl.BlockSpec((2, SUBLANES, tn), lambda j, i: (0, jnp.minimum((i + 1) * hb, last8), j)),
                  pl.BlockSpec((ts, tn), lambda j, i: (i, j)),
                  pl.BlockSpec((SUBLANES, tn), lambda j, i: (jnp.minimum((i + 1) * hb, last8), j)),
                  pl.BlockSpec((2, kw, tn), lambda j, i: (0, 0, j)),
                  pl.BlockSpec((2, 1, tn), lambda j, i: (0, 0, j))],
        out_specs=[pl.BlockSpec((2, ts, tn), lambda j, i: (0, i, j)),
                   pl.BlockSpec((2, kw, tn), lambda j, i: (0, 0, j)),
                   pl.BlockSpec((2, 1, tn), lambda j, i: (0, 0, j))],
        out_shape=[jax.ShapeDtypeStruct((2, s, f), BF16), jax.ShapeDtypeStruct((2, kw, f), F32),
                   jax.ShapeDtypeStruct((2, 1, f), F32)],
        scratch_shapes=[], args=(up2, up2, up2, dact, dact, conv_w2, conv_b2))


def _mm_rms_bwd(name, a, b, h, g, dh_in, *, tm, tk):
    (m, kk), (n, kb) = a.shape, b.shape
    assert kk == kb and h.shape == (m, n), (name, a.shape, b.shape, h.shape)
    tm, tk = min(tm, m), min(tk, kk)
    nk = kk // tk
    dims = (((1,), (1,)), ((), ()))

    def body(a_ref, b_ref, h_ref, g_ref, dhin_ref, dh_ref, dg_ref, *acc):
        i, k = pl.program_id(0), pl.program_id(2)
        prod = lax.dot_general(a_ref[...].astype(BF16), b_ref[...].astype(BF16), dims, preferred_element_type=F32)

        def finish(dhn):
            x = h_ref[...]
            rstd = lax.rsqrt(jnp.mean(x * x, axis=-1, keepdims=True) + NORM_EPS)
            xhat = x * rstd
            dxh = dhn * g_ref[...]
            dh_ref[...] = dhin_ref[...] + rstd * (dxh - xhat * jnp.mean(dxh * xhat, axis=-1, keepdims=True))
            part = jnp.sum(dhn * xhat, axis=0, keepdims=True)

            @pl.when(i == 0)
            def _():
                dg_ref[...] = part

            @pl.when(i > 0)
            def _():
                dg_ref[...] += part

        if nk == 1:
            finish(prod)
        else:
            acc_ref = acc[0]

            @pl.when(k == 0)
            def _():
                acc_ref[...] = prod

            @pl.when(k > 0)
            def _():
                acc_ref[...] += prod

            @pl.when(k == nk - 1)
            def _():
                finish(acc_ref[...])

    row = pl.BlockSpec((tm, n), lambda i, j, k: (i, 0))
    vec = pl.BlockSpec((1, n), lambda i, j, k: (0, 0))
    return pl.pallas_call(
        body, name=name, grid=(m // tm, 1, nk),
        in_specs=[a.spec(tm, tk, lambda i, j, k: (i, k)), b.spec(n, tk, lambda i, j, k: (0, k)), row, vec, row],
        out_specs=[row, vec], out_shape=[jax.ShapeDtypeStruct((m, n), F32), jax.ShapeDtypeStruct((1, n), F32)],
        scratch_shapes=[pltpu.VMEM((tm, n), F32)] if nk > 1 else [], compiler_params=_cparams(),
    )(a.arr, b.arr, h, g, dh_in)


def _ffn_up_act(hn2, w_up4, conv_w2, conv_b2, ts=1024, tn=512, sub=1024, side=None):
    s, d = hn2.shape
    p, _, wc = w_up4.shape
    f = p * wc // 2
    ts, tn = min(ts, s), min(tn, wc)
    sub = min(sub, ts)
    per = wc // tn
    kw = FFN_CONV_W
    g0, g1 = f // tn, s // ts

    def body(hn_ref, w1_ref, w2_ref, cw_ref, cb_ref, up_ref, c_ref, act_ref, carry_ref):
        @pl.when(pl.program_id(1) == 0)
        def _():
            carry_ref[...] = jnp.zeros_like(carry_ref)

        for q in range(ts // sub):
            rows = slice(q * sub, (q + 1) * sub)
            hn = hn_ref[rows, :]
            cs = []
            for h, w_ref in enumerate((w1_ref, w2_ref)):
                x = jnp.dot(hn, w_ref[...], preferred_element_type=F32)
                up_ref[h, rows, :] = x
                halo = carry_ref[h]
                c = cb_ref[h] + cw_ref[h, kw - 1:kw, :] * x
                for sft in range(1, kw):
                    c = c + cw_ref[h, kw - 1 - sft:kw - sft, :] * _shift_down(x, halo, sft)
                carry_ref[h] = x[sub - SUBLANES:, :]
                c_ref[h, rows, :] = c
                cs.append(c)
            act_ref[rows, :] = (_gelu(cs[0]) * cs[1]).astype(BF16)

    outs, side_outs = _call_with_side(
        body, side, lambda: (pl.program_id(0) == 0) & (pl.program_id(1) == 0),
        lambda: (pl.program_id(0) == g0 - 1) & (pl.program_id(1) == g1 - 1),
        name="ffn_up_act", grid=(g0, g1),
        in_specs=[pl.BlockSpec((ts, d), lambda j, i: (i, 0)),
                  pl.BlockSpec((None, d, tn), lambda j, i: (j // per, 0, j % per)),
                  pl.BlockSpec((None, d, tn), lambda j, i: (p // 2 + j // per, 0, j % per)),
                  pl.BlockSpec((2, kw, tn), lambda j, i: (0, 0, j)),
                  pl.BlockSpec((2, 1, tn), lambda j, i: (0, 0, j))],
        out_specs=[pl.BlockSpec((2, ts, tn), lambda j, i: (0, i, j)), pl.BlockSpec((2, ts, tn), lambda j, i: (0, i, j)),
                   pl.BlockSpec((ts, tn), lambda j, i: (i, j))],
        out_shape=[jax.ShapeDtypeStruct((2, s, f), F32), jax.ShapeDtypeStruct((2, s, f), F32), jax.ShapeDtypeStruct((s, f), BF16)],
        scratch_shapes=[pltpu.VMEM((2, SUBLANES, tn), F32)], args=(hn2, w_up4, w_up4, conv_w2, conv_b2))
    return outs, side_outs


def _ffn_bwd_fused(dh, w_down, up2, c2, conv_w2, ts=1024, tn=512, side=None):
    s, d = dh.shape
    _, _, f = up2.shape
    ts, tn = min(ts, s), min(tn, f)
    kw = FFN_CONV_W
    nt = s // ts
    hb = ts // SUBLANES
    g0 = f // tn
    nt_dims = (((1,), (1,)), ((), ()))

    def body(dh_ref, wd_ref, up_ref, c_ref, w_ref, dup_ref, dw_ref, db_ref, carry_ref):
        i = pl.program_id(1)
        first_step = i == 0

        @pl.when(first_step)
        def _():
            carry_ref[...] = jnp.zeros_like(carry_ref)

        da = lax.dot_general(dh_ref[...].astype(BF16), wd_ref[...], nt_dims, preferred_element_type=F32)
        g1, dg1 = _gelu_and_grad(c_ref[0])
        dcs = [da * c_ref[1] * dg1, da * g1]
        for h in range(2):
            dc = dcs[h]
            after = carry_ref[h]
            ups = [dc] + [_shift_up(dc, after, sft) for sft in range(1, kw)]
            dup = w_ref[h, kw - 1:kw, :] * dc
            for sft in range(1, kw):
                dup = dup + w_ref[h, kw - 1 - sft:kw - sft, :] * ups[sft]
            carry_ref[h] = dc[:SUBLANES]
            dup_ref[h] = dup.astype(BF16)
            dbp = jnp.sum(dc, axis=0, keepdims=True)
            x = up_ref[h]
            dwp = [jnp.sum(ups[kw - 1 - k] * x, axis=0, keepdims=True) for k in range(kw)]

            @pl.when(first_step)
            def _():
                db_ref[h] = dbp
                for k in range(kw):
                    dw_ref[h, k:k + 1, :] = dwp[k]

            @pl.when(i > 0)
            def _():
                db_ref[h] += dbp
                for k in range(kw):
                    dw_ref[h, k:k + 1, :] += dwp[k]

    rev = lambda i: nt - 1 - i
    return _call_with_side(
        body, side, lambda: (pl.program_id(0) == 0) & (pl.program_id(1) == 0),
        lambda: (pl.program_id(0) == g0 - 1) & (pl.program_id(1) == nt - 1),
        name="ffn_bwd", grid=(g0, nt),
        in_specs=[pl.BlockSpec((ts, d), lambda j, i: (rev(i), 0)),
                  pl.BlockSpec((tn, d), lambda j, i: (j, 0)),
                  pl.BlockSpec((2, ts, tn), lambda j, i: (0, rev(i), j)),
                  pl.BlockSpec((2, ts, tn), lambda j, i: (0, rev(i), j)),
                  pl.BlockSpec((2, kw, tn), lambda j, i: (0, 0, j))],
        out_specs=[pl.BlockSpec((2, ts, tn), lambda j, i: (0, rev(i), j)),
                   pl.BlockSpec((2, kw, tn), lambda j, i: (0, 0, j)),
                   pl.BlockSpec((2, 1, tn), lambda j, i: (0, 0, j))],
        out_shape=[jax.ShapeDtypeStruct((2, s, f), BF16), jax.ShapeDtypeStruct((2, kw, f), F32),
                   jax.ShapeDtypeStruct((2, 1, f), F32)],
        scratch_shapes=[pltpu.VMEM((2, SUBLANES, tn), F32)], args=(dh, w_down, up2, c2, conv_w2))


def _ffn_fwd(h, g, w_up4, w_down, conv_w2, conv_b2, ts=512, tn=512, sub=256, side=None):
    s, d = h.shape
    p, _, wc = w_up4.shape
    f = p * wc // 2
    ts, tn = min(ts, s), min(tn, wc)
    sub = min(sub, ts)
    per = wc // tn
    kw = FFN_CONV_W
    g0, g1 = s // ts, f // tn

    def body(h_ref, g_ref, w1_ref, w2_ref, wd_ref, cw_ref, cb_ref, ho_ref, hn_ref, up_ref, c_ref, act_ref, carry_ref):
        i, j = pl.program_id(0), pl.program_id(1)

        @pl.when(j == 0)
        def _():
            x = h_ref[...]
            var = jnp.mean(x * x, axis=-1, keepdims=True)
            hn_ref[...] = (x * lax.rsqrt(var + NORM_EPS) * g_ref[...]).astype(BF16)
            ho_ref[...] = x

        @pl.when(i == 0)
        def _():
            carry_ref[j] = jnp.zeros(carry_ref.shape[1:], F32)

        for q in range(ts // sub):
            rows = slice(q * sub, (q + 1) * sub)
            hn = hn_ref[rows, :]
            cs = []
            for hf, w_ref in enumerate((w1_ref, w2_ref)):
                x = jnp.dot(hn, w_ref[...], preferred_element_type=F32)
                up_ref[hf, rows, :] = x
                halo = carry_ref[j, hf]
                c = cb_ref[hf] + cw_ref[hf, kw - 1:kw, :] * x
                for sft in range(1, kw):
                    c = c + cw_ref[hf, kw - 1 - sft:kw - sft, :] * _shift_down(x, halo, sft)
                carry_ref[j, hf] = x[sub - SUBLANES:, :]
                c_ref[hf, rows, :] = c
                cs.append(c)
            a = (_gelu(cs[0]) * cs[1]).astype(BF16)
            act_ref[rows, :] = a
            ho_ref[rows, :] += jnp.dot(a, wd_ref[...], preferred_element_type=F32)

    row = pl.BlockSpec((ts, d), lambda i, j: (i, 0))
    col2 = pl.BlockSpec((2, ts, tn), lambda i, j: (0, i, j))
    return _call_with_side(
        body, side, lambda: (pl.program_id(0) == 0) & (pl.program_id(1) == 0),
        lambda: (pl.program_id(0) == g0 - 1) & (pl.program_id(1) == g1 - 1),
        name="ffn_fwd", grid=(g0, g1),
        in_specs=[row, pl.BlockSpec((1, d), lambda i, j: (0, 0)),
                  pl.BlockSpec((None, d, tn), lambda i, j: (j // per, 0, j % per)),
                  pl.BlockSpec((None, d, tn), lambda i, j: (p // 2 + j // per, 0, j % per)),
                  pl.BlockSpec((tn, d), lambda i, j: (j, 0)),
                  pl.BlockSpec((2, kw, tn), lambda i, j: (0, 0, j)),
                  pl.BlockSpec((2, 1, tn), lambda i, j: (0, 0, j))],
        out_specs=[row, row, col2, col2, pl.BlockSpec((ts, tn), lambda i, j: (i, j))],
        out_shape=[jax.ShapeDtypeStruct((s, d), F32), jax.ShapeDtypeStruct((s, d), BF16), jax.ShapeDtypeStruct((2, s, f), F32),
                   jax.ShapeDtypeStruct((2, s, f), F32), jax.ShapeDtypeStruct((s, f), BF16)],
        scratch_shapes=[pltpu.VMEM((g1, 2, SUBLANES, tn), F32)],
        args=(h, g, w_up4, w_up4, w_down, conv_w2, conv_b2))


def _ffn_bwd_all(dh, w_down, up2, c2, hn, act_, conv_w2, ts=256, tn=512, sub=128, side=None):
    s, d = dh.shape
    _, _, f = up2.shape
    ts, tn = min(ts, s), min(tn, f)
    sub = min(sub, ts)
    kw = FFN_CONV_W
    nt = s // ts
    g0 = f // tn
    nt_dims = (((1,), (1,)), ((), ()))
    tn_dims = (((0,), (0,)), ((), ()))

    def body(dh_ref, wd_ref, up_ref, c_ref, hn_ref, act_ref, w_ref, dup_ref, dw_ref, db_ref, dwu_ref, dwd_ref,
             carry_ref, dwu_acc, dwd_acc):
        i = pl.program_id(1)
        first_step = i == 0

        @pl.when(first_step)
        def _():
            carry_ref[...] = jnp.zeros_like(carry_ref)
            dwu_acc[...] = jnp.zeros_like(dwu_acc)
            dwd_acc[...] = jnp.zeros_like(dwd_acc)
            dw_ref[...] = jnp.zeros_like(dw_ref)
            db_ref[...] = jnp.zeros_like(db_ref)

        dhb = dh_ref[...].astype(BF16)
        da_all = lax.dot_general(dhb, wd_ref[...], nt_dims, preferred_element_type=F32)
        for q in reversed(range(ts // sub)):
            rows = slice(q * sub, (q + 1) * sub)
            da = da_all[rows, :]
            g1, dg1 = _gelu_and_grad(c_ref[0, rows, :])
            dcs = [da * c_ref[1, rows, :] * dg1, da * g1]
            hnq = hn_ref[rows, :]
            for hf in range(2):
                dc = dcs[hf]
                after = carry_ref[hf]
                ups = [dc] + [_shift_up(dc, after, sft) for sft in range(1, kw)]
                dup = w_ref[hf, kw - 1:kw, :] * dc
                for sft in range(1, kw):
                    dup = dup + w_ref[hf, kw - 1 - sft:kw - sft, :] * ups[sft]
                carry_ref[hf] = dc[:SUBLANES]
                dupb = dup.astype(BF16)
                dup_ref[hf, rows, :] = dupb
                dwu_acc[hf] += lax.dot_general(hnq, dupb, tn_dims, preferred_element_type=F32)
                db_ref[hf] += jnp.sum(dc, axis=0, keepdims=True)
                x = up_ref[hf, rows, :]
                for k in range(kw):
                    dw_ref[hf, k:k + 1, :] += jnp.sum(ups[kw - 1 - k] * x, axis=0, keepdims=True)
            dwd_acc[...] += lax.dot_general(act_ref[rows, :], dhb[rows, :], tn_dims, preferred_element_type=F32)

        @pl.when(i == nt - 1)
        def _():
            dwu_ref[...] = dwu_acc[...].astype(BF16)
            dwd_ref[...] = dwd_acc[...].astype(BF16)

    rev = lambda i: nt - 1 - i
    col2 = pl.BlockSpec((2, ts, tn), lambda j, i: (0, rev(i), j))
    return _call_with_side(
        body, side, lambda: (pl.program_id(0) == 0) & (pl.program_id(1) == 0),
        lambda: (pl.program_id(0) == g0 - 1) & (pl.program_id(1) == nt - 1),
        name="ffn_bwd", grid=(g0, nt),
        in_specs=[pl.BlockSpec((ts, d), lambda j, i: (rev(i), 0)),
                  pl.BlockSpec((tn, d), lambda j, i: (j, 0)),
                  col2, col2,
                  pl.BlockSpec((ts, d), lambda j, i: (rev(i), 0)),
                  pl.BlockSpec((ts, tn), lambda j, i: (rev(i), j)),
                  pl.BlockSpec((2, kw, tn), lambda j, i: (0, 0, j))],
        out_specs=[col2,
                   pl.BlockSpec((2, kw, tn), lambda j, i: (0, 0, j)),
                   pl.BlockSpec((2, 1, tn), lambda j, i: (0, 0, j)),
                   pl.BlockSpec((2, d, tn), lambda j, i: (0, 0, j)),
                   pl.BlockSpec((tn, d), lambda j, i: (j, 0))],
        out_shape=[jax.ShapeDtypeStruct((2, s, f), BF16), jax.ShapeDtypeStruct((2, kw, f), F32),
                   jax.ShapeDtypeStruct((2, 1, f), F32), jax.ShapeDtypeStruct((2, d, f), BF16), jax.ShapeDtypeStruct((f, d), BF16)],
        scratch_shapes=[pltpu.VMEM((2, SUBLANES, tn), F32), pltpu.VMEM((2, d, tn), F32), pltpu.VMEM((tn, d), F32)],
        args=(dh, w_down, up2, c2, hn, act_, conv_w2))


def _rg_gates(xr, wa_ref, ba_ref, wx_ref, bx_ref, lam_ref):
    bw = wa_ref.shape[-1]
    xb = xr.astype(BF16)
    za = jnp.concatenate([jnp.dot(xb[:, h * bw:(h + 1) * bw], wa_ref[h], preferred_element_type=F32)
                          for h in range(RG_HEADS)], axis=1) + ba_ref[...]
    zx = jnp.concatenate([jnp.dot(xb[:, h * bw:(h + 1) * bw], wx_ref[h], preferred_element_type=F32)
                          for h in range(RG_HEADS)], axis=1) + bx_ref[...]
    r, ig = _sigmoid(za), _sigmoid(zx)
    sp = _softplus(-lam_ref[...])
    la = -RG_C * r * sp
    a = jnp.exp(la)
    mult = jnp.sqrt(_neg_expm1(2.0 * la))
    return xb, r, ig, sp, a, mult


def _rg_fwd(xg2, conv_w, conv_b, w_a, b_a, w_x, b_x, lam, ts=256, side=None):
    _, s, c = xg2.shape
    ts = min(ts, s)
    kw = RG_CONV_W
    hb = ts // SUBLANES

    def body(xg_ref, halo_ref, cw_ref, cb_ref, wa_ref, ba_ref, wx_ref, bx_ref, lam_ref, xr_ref, hs_ref, y_ref, carry_ref,
             a_scr, b_scr):
        i = pl.program_id(0)

        @pl.when(i == 0)
        def _():
            carry_ref[...] = jnp.zeros_like(carry_ref)

        xp = xg_ref[0]
        halo = jnp.where(i == 0, 0.0, halo_ref[...])
        xr = cb_ref[...] + cw_ref[kw - 1:kw, :] * xp
        for sft in range(1, kw):
            xr = xr + cw_ref[kw - 1 - sft:kw - sft, :] * _shift_down(xp, halo, sft)
        _, r, ig, sp, a, mult = _rg_gates(xr, wa_ref, ba_ref, wx_ref, bx_ref, lam_ref)
        a_scr[...] = a
        b_scr[...] = mult * (ig * xr)
        _real_slab_scan(a_scr, b_scr, hs_ref, carry_ref, reverse=False)
        xr_ref[...] = xr
        y_ref[...] = (hs_ref[...] * _gelu(xg_ref[1])).astype(BF16)

    full = lambda shape: pl.BlockSpec(shape, lambda i: (0,) * len(shape))
    row_spec = pl.BlockSpec((ts, c), lambda i: (i, 0))
    nt = s // ts
    return _call_with_side(
        body, side, lambda: pl.program_id(0) == 0, lambda: pl.program_id(0) == nt - 1,
        name="rg_fwd", grid=(nt,),
        in_specs=[pl.BlockSpec((2, ts, c), lambda i: (0, i, 0)),
                  pl.BlockSpec((None, SUBLANES, c), lambda i: (0, jnp.maximum(i * hb - 1, 0), 0)),
                  full(conv_w.shape), full(conv_b.shape), full(w_a.shape), full(b_a.shape), full(w_x.shape), full(b_x.shape),
                  full(lam.shape)],
        out_specs=[row_spec, row_spec, row_spec],
        out_shape=[jax.ShapeDtypeStruct((s, c), F32), jax.ShapeDtypeStruct((s, c), F32), jax.ShapeDtypeStruct((s, c), BF16)],
        scratch_shapes=[pltpu.VMEM((1, c), F32), pltpu.VMEM((ts, c), F32), pltpu.VMEM((ts, c), F32)],
        args=(xg2, xg2, conv_w, conv_b, w_a, b_a, w_x, b_x, lam))


def _rg_bwd(dy, xg2, xr, hs, conv_w, w_a, b_a, w_x, b_x, lam, ts=256, side=None):
    _, s, c = xg2.shape
    ts = min(ts, s)
    nt = s // ts
    kw = RG_CONV_W
    hb = ts // SUBLANES
    bw = c // RG_HEADS
    tn_dims = (((0,), (0,)), ((), ()))
    nt_dims = (((1,), (1,)), ((), ()))

    def body(dy_ref, xg_ref, xph_ref, xr_ref, hs_ref, hsh_ref, cw_ref, wa_ref, ba_ref, wx_ref, bx_ref, lam_ref,
             dxg_ref, dcw_ref, dcb_ref, dwa_ref, dba_ref, dwx_ref, dbx_ref, dlam_ref,
             lam_carry, a_carry, dxr_carry, dsp_acc, a_scr, b_scr):
        i = pl.program_id(0)
        first_step = i == 0
        time_first = i == nt - 1

        @pl.when(first_step)
        def _():
            lam_carry[...] = jnp.zeros_like(lam_carry)
            a_carry[...] = jnp.ones_like(a_carry)
            dxr_carry[...] = jnp.zeros_like(dxr_carry)
            dsp_acc[...] = jnp.zeros_like(dsp_acc)
            for ref in (dcw_ref, dcb_ref, dwa_ref, dba_ref, dwx_ref, dbx_ref):
                ref[...] = jnp.zeros_like(ref)

        xr = xr_ref[...]
        hs = hs_ref[...]
        gate = xg_ref[1]
        xb, r, ig, sp, a, mult = _rg_gates(xr, wa_ref, ba_ref, wx_ref, bx_ref, lam_ref)
        dyv = dy_ref[...]
        gg, dgg = _gelu_and_grad(gate)
        dhs = dyv * gg
        dxg_ref[1] = (dyv * hs * dgg).astype(BF16)
        row = _rows(xr.shape)
        a_scr[...] = jnp.where(row == ts - 1, a_carry[0:1, :], pltpu.roll(a, ts - 1, 0))
        b_scr[...] = dhs
        _real_slab_scan(a_scr, b_scr, b_scr, lam_carry, reverse=True)
        lmb = b_scr[...]
        a_carry[...] = a[:SUBLANES]
        hs_prev = _shift_down(hs, jnp.where(time_first, 0.0, hsh_ref[...]), 1)
        d_a = lmb * hs_prev
        d_m = lmb * (ig * xr)
        d_ig = lmb * mult * xr
        d_xr = lmb * mult * ig
        d_la = a * d_a - (a * a / mult) * d_m
        dsp_acc[...] += jnp.sum(-RG_C * r * d_la, axis=0, keepdims=True)
        d_za = (-RG_C * sp) * d_la * r * (1.0 - r)
        d_zx = d_ig * ig * (1.0 - ig)
        dba_ref[...] += jnp.sum(d_za, axis=0, keepdims=True)
        dbx_ref[...] += jnp.sum(d_zx, axis=0, keepdims=True)
        dzab, dzxb = d_za.astype(BF16), d_zx.astype(BF16)
        back = []
        for h in range(RG_HEADS):
            sl = slice(h * bw, (h + 1) * bw)
            dwa_ref[h] += lax.dot_general(xb[:, sl], dzab[:, sl], tn_dims, preferred_element_type=F32)
            dwx_ref[h] += lax.dot_general(xb[:, sl], dzxb[:, sl], tn_dims, preferred_element_type=F32)
            back.append(lax.dot_general(dzab[:, sl], wa_ref[h], nt_dims, preferred_element_type=F32)
                        + lax.dot_general(dzxb[:, sl], wx_ref[h], nt_dims, preferred_element_type=F32))
        d_xr = d_xr + jnp.concatenate(back, axis=1)
        d_xp = cw_ref[kw - 1:kw, :] * d_xr
        after = dxr_carry[...]
        for sft in range(1, kw):
            d_xp = d_xp + cw_ref[kw - 1 - sft:kw - sft, :] * _shift_up(d_xr, after, sft)
        dxr_carry[...] = d_xr[:SUBLANES]
        dxg_ref[0] = d_xp.astype(BF16)
        xp = xg_ref[0]
        before = jnp.where(time_first, 0.0, xph_ref[...])
        dcb_ref[...] += jnp.sum(d_xr, axis=0, keepdims=True)
        dcw_ref[kw - 1:kw, :] += jnp.sum(d_xr * xp, axis=0, keepdims=True)
        for sft in range(1, kw):
            dcw_ref[kw - 1 - sft:kw - sft, :] += jnp.sum(d_xr * _shift_down(xp, before, sft), axis=0, keepdims=True)
        dlam_ref[...] = dsp_acc[...] * (-_sigmoid(-lam_ref[...]))

    full = lambda shape: pl.BlockSpec(shape, lambda i: (0,) * len(shape))
    rev = lambda i: nt - 1 - i
    row_spec = pl.BlockSpec((ts, c), lambda i: (rev(i), 0))
    halo_idx = lambda i: jnp.maximum(rev(i) * hb - 1, 0)
    vec = (1, c)
    return _call_with_side(
        body, side, lambda: pl.program_id(0) == 0, lambda: pl.program_id(0) == nt - 1,
        name="rg_bwd", grid=(nt,),
        in_specs=[row_spec,
                  pl.BlockSpec((2, ts, c), lambda i: (0, rev(i), 0)),
                  pl.BlockSpec((None, SUBLANES, c), lambda i: (0, halo_idx(i), 0)),
                  row_spec, row_spec,
                  pl.BlockSpec((SUBLANES, c), lambda i: (halo_idx(i), 0)),
                  full(conv_w.shape), full(w_a.shape), full(b_a.shape), full(w_x.shape), full(b_x.shape), full(lam.shape)],
        out_specs=[pl.BlockSpec((2, ts, c), lambda i: (0, rev(i), 0)), full(conv_w.shape), full(vec), full(w_a.shape), full(vec),
                   full(w_x.shape), full(vec), full(vec)],
        out_shape=[jax.ShapeDtypeStruct((2, s, c), BF16), jax.ShapeDtypeStruct(conv_w.shape, F32), jax.ShapeDtypeStruct(vec, F32),
                   jax.ShapeDtypeStruct(w_a.shape, F32), jax.ShapeDtypeStruct(vec, F32), jax.ShapeDtypeStruct(w_x.shape, F32),
                   jax.ShapeDtypeStruct(vec, F32), jax.ShapeDtypeStruct(vec, F32)],
        scratch_shapes=[pltpu.VMEM(vec, F32), pltpu.VMEM((SUBLANES, c), F32), pltpu.VMEM((SUBLANES, c), F32),
                        pltpu.VMEM(vec, F32), pltpu.VMEM((ts, c), F32), pltpu.VMEM((ts, c), F32)],
        args=(dy, xg2, xg2, xr, hs, hs, conv_w, w_a, b_a, w_x, b_x, lam))


def _s5_param_fn(a_re, a_im, log_dt, bt_re, bt_im):
    dt = jnp.exp(log_dt)
    mag = jnp.exp(a_re * dt)
    abr = mag * jnp.cos(a_im * dt)
    abi = mag * jnp.sin(a_im * dt)
    ur, ui = abr - 1.0, abi
    den = a_re * a_re + a_im * a_im
    wr = (ur * a_re + ui * a_im) / den
    wi = (ui * a_re - ur * a_im) / den
    bbr = wr[None] * bt_re - wi[None] * bt_im
    bbi = wr[None] * bt_im + wi[None] * bt_re
    return abr, abi, bbr, bbi


def _s5_params(a_re, a_im, log_dt, bt_re, bt_im, nlev):
    g, p = a_re.shape
    gc = bt_re.shape[0]

    def body(ar_ref, ai_ref, dt_ref, br_ref, bi_ref, abr_ref, abi_ref, pr_ref, pi_ref, bbr_ref, bbi_ref):
        abr, abi, bbr, bbi = _s5_param_fn(ar_ref[...], ai_ref[...], dt_ref[...], br_ref[...], bi_ref[...])
        abr_ref[...] = abr
        abi_ref[...] = abi
        bbr_ref[...] = bbr
        bbi_ref[...] = bbi
        qr, qi = abr, abi
        for k in range(nlev):
            pr_ref[k] = qr
            pi_ref[k] = qi
            qr, qi = qr * qr - qi * qi, 2.0 * qr * qi

    sd = jax.ShapeDtypeStruct
    return pl.pallas_call(
        body, name="s5_params",
        out_shape=[sd((g, p), F32), sd((g, p), F32), sd((nlev, g, p), F32), sd((nlev, g, p), F32), sd((gc, g, p), F32),
                   sd((gc, g, p), F32)],
    )(a_re, a_im, log_dt, bt_re, bt_im)


def _s5_params_bwd(a_re, a_im, log_dt, bt_re, bt_im, d_abr, d_abi, d_bbr, d_bbi):
    def body(ar_ref, ai_ref, dt_ref, br_ref, bi_ref, g0, g1, g2, g3, o0, o1, o2, o3, o4):
        _, vjp = jax.vjp(_s5_param_fn, ar_ref[...], ai_ref[...], dt_ref[...], br_ref[...], bi_ref[...])
        outs = vjp((g0[...], g1[...], g2[...], g3[...]))
        for o, v in zip((o0, o1, o2, o3, o4), outs):
            o[...] = v

    sd = jax.ShapeDtypeStruct
    return pl.pallas_call(
        body, name="s5_params_bwd",
        out_shape=[sd(a_re.shape, F32), sd(a_im.shape, F32), sd(log_dt.shape, F32), sd(bt_re.shape, F32), sd(bt_im.shape, F32)],
    )(a_re, a_im, log_dt, bt_re, bt_im, d_abr, d_abi, d_bbr, d_bbi)


def _s5_fwd(u, abr, abi, pw_r, pw_i, bp_r, bp_i, cp_r, cp_i, dvec, ts=128, side=None):
    s, c = u.shape
    n = abr.shape[1]
    nblk, cb, nb = bp_r.shape
    ts = min(ts, s)

    def body(u_ref, ar_ref, ai_ref, pr_ref, pi_ref, bpr_ref, bpi_ref, cpr_ref, cpi_ref, d_ref,
             hr_ref, hi_ref, yp_ref, gy_ref, car_r, car_i):
        i = pl.program_id(0)

        @pl.when(i == 0)
        def _():
            car_r[...] = jnp.zeros_like(car_r)
            car_i[...] = jnp.zeros_like(car_i)

        uv = u_ref[...]
        ub = uv.astype(BF16)
        br = jnp.concatenate([jnp.dot(ub[:, k * cb:(k + 1) * cb], bpr_ref[k], preferred_element_type=F32) for k in range(nblk)], axis=1)
        bi = jnp.concatenate([jnp.dot(ub[:, k * cb:(k + 1) * cb], bpi_ref[k], preferred_element_type=F32) for k in range(nblk)], axis=1)
        ar, ai = ar_ref[...], ai_ref[...]
        pr, pi_ = car_r[SUBLANES - 1:SUBLANES, :], car_i[SUBLANES - 1:SUBLANES, :]
        row = _rows(br.shape)
        br = br + jnp.where(row == 0, ar * pr - ai * pi_, 0.0)
        bi = bi + jnp.where(row == 0, ar * pi_ + ai * pr, 0.0)
        hr, hi = _scan_cplx(br, bi, pr_ref, pi_ref, reverse=False)
        car_r[...] = hr[ts - SUBLANES:]
        car_i[...] = hi[ts - SUBLANES:]
        hr_ref[...] = hr
        hi_ref[...] = hi
        hrb, hib = hr.astype(BF16), hi.astype(BF16)
        y = jnp.concatenate([jnp.dot(hrb[:, k * nb:(k + 1) * nb], cpr_ref[k], preferred_element_type=F32)
                             - jnp.dot(hib[:, k * nb:(k + 1) * nb], cpi_ref[k], preferred_element_type=F32) for k in range(nblk)], axis=1)
        yp = y + d_ref[...] * uv
        yp_ref[...] = yp
        gy_ref[...] = _gelu(yp).astype(BF16)

    full = lambda shape: pl.BlockSpec(shape, lambda i: (0,) * len(shape))
    rc = pl.BlockSpec((ts, c), lambda i: (i, 0))
    rn = pl.BlockSpec((ts, n), lambda i: (i, 0))
    sd = jax.ShapeDtypeStruct
    nt = s // ts
    return _call_with_side(
        body, side, lambda: pl.program_id(0) == 0, lambda: pl.program_id(0) == nt - 1,
        name="s5_fwd", grid=(nt,),
        in_specs=[rc, full(abr.shape), full(abi.shape), full(pw_r.shape), full(pw_i.shape), full(bp_r.shape), full(bp_i.shape),
                  full(cp_r.shape), full(cp_i.shape), full(dvec.shape)],
        out_specs=[rn, rn, rc, rc],
        out_shape=[sd((s, n), F32), sd((s, n), F32), sd((s, c), F32), sd((s, c), BF16)],
        scratch_shapes=[pltpu.VMEM((SUBLANES, n), F32), pltpu.VMEM((SUBLANES, n), F32)],
        args=(u, abr, abi, pw_r, pw_i, bp_r, bp_i, cp_r, cp_i, dvec))


def _s5_bwd(dgy, ypre, u, hr, hi, abr, abi, pw_r, pw_i, bp_r, bp_i, cp_r, cp_i, dvec, ts=128, side=None):
    s, c = u.shape
    n = abr.shape[1]
    nblk, cb, nb = bp_r.shape
    ts = min(ts, s)
    nt = s // ts
    hb = ts // SUBLANES
    tn_dims = (((0,), (0,)), ((), ()))
    nt_dims = (((1,), (1,)), ((), ()))

    def body(dgy_ref, yp_ref, u_ref, hr_ref, hi_ref, hrh_ref, hih_ref, ar_ref, ai_ref, pr_ref, pi_ref, bpr_ref, bpi_ref,
             cpr_ref, cpi_ref, d_ref,
             du_ref, dar_ref, dai_ref, dbr_ref, dbi_ref, dcr_ref, dci_ref, dd_ref, car_r, car_i, npi_ref):
        i = pl.program_id(0)
        time_first = i == nt - 1

        @pl.when(i == 0)
        def _():
            car_r[...] = jnp.zeros_like(car_r)
            car_i[...] = jnp.zeros_like(car_i)
            npi_ref[...] = -pi_ref[...]
            for ref in (dar_ref, dai_ref, dbr_ref, dbi_ref, dcr_ref, dci_ref, dd_ref):
                ref[...] = jnp.zeros_like(ref)

        uv = u_ref[...]
        _, dgel = _gelu_and_grad(yp_ref[...])
        dyv = dgy_ref[...] * dgel
        dd_ref[...] += jnp.sum(dyv * uv, axis=0, keepdims=True)
        dyb = dyv.astype(BF16)
        hr, hi = hr_ref[...], hi_ref[...]
        hrb, hib = hr.astype(BF16), hi.astype(BF16)
        dhr, dhi = [], []
        for k in range(nblk):
            dblk = dyb[:, k * cb:(k + 1) * cb]
            dhr.append(lax.dot_general(dblk, cpr_ref[k], nt_dims, preferred_element_type=F32))
            dhi.append(-lax.dot_general(dblk, cpi_ref[k], nt_dims, preferred_element_type=F32))
            dcr_ref[k] += lax.dot_general(hrb[:, k * nb:(k + 1) * nb], dblk, tn_dims, preferred_element_type=F32)
            dci_ref[k] += lax.dot_general(hib[:, k * nb:(k + 1) * nb], dblk, tn_dims, preferred_element_type=F32)
        dhr = jnp.concatenate(dhr, axis=1)
        dhi = jnp.concatenate(dhi, axis=1)
        ar, ai = ar_ref[...], ai_ref[...]
        nr, ni = car_r[0:1, :], car_i[0:1, :]
        row = _rows(dhr.shape)
        dhr = dhr + jnp.where(row == ts - 1, ar * nr + ai * ni, 0.0)
        dhi = dhi + jnp.where(row == ts - 1, ar * ni - ai * nr, 0.0)
        lr, li = _scan_cplx(dhr, dhi, pr_ref, npi_ref, reverse=True)
        car_r[...] = lr[:SUBLANES]
        car_i[...] = li[:SUBLANES]
        hpr = _shift_down(hr, jnp.where(time_first, 0.0, hrh_ref[...]), 1)
        hpi = _shift_down(hi, jnp.where(time_first, 0.0, hih_ref[...]), 1)
        dar_ref[...] += jnp.sum(lr * hpr + li * hpi, axis=0, keepdims=True)
        dai_ref[...] += jnp.sum(li * hpr - lr * hpi, axis=0, keepdims=True)
        lrb, lib = lr.astype(BF16), li.astype(BF16)
        ub = uv.astype(BF16)
        du = []
        for k in range(nblk):
            ublk = ub[:, k * cb:(k + 1) * cb]
            lrk, lik = lrb[:, k * nb:(k + 1) * nb], lib[:, k * nb:(k + 1) * nb]
            dbr_ref[k] += lax.dot_general(ublk, lrk, tn_dims, preferred_element_type=F32)
            dbi_ref[k] += lax.dot_general(ublk, lik, tn_dims, preferred_element_type=F32)
            du.append(lax.dot_general(lrk, bpr_ref[k], nt_dims, preferred_element_type=F32)
                      + lax.dot_general(lik, bpi_ref[k], nt_dims, preferred_element_type=F32))
        du_ref[...] = (d_ref[...] * dyv + jnp.concatenate(du, axis=1)).astype(BF16)

    full = lambda shape: pl.BlockSpec(shape, lambda i: (0,) * len(shape))
    rev = lambda i: nt - 1 - i
    halo_idx = lambda i: jnp.maximum(rev(i) * hb - 1, 0)
    rc = pl.BlockSpec((ts, c), lambda i: (rev(i), 0))
    rn = pl.BlockSpec((ts, n), lambda i: (rev(i), 0))
    hn = pl.BlockSpec((SUBLANES, n), lambda i: (halo_idx(i), 0))
    sd = jax.ShapeDtypeStruct
    return _call_with_side(
        body, side, lambda: pl.program_id(0) == 0, lambda: pl.program_id(0) == nt - 1,
        name="s5_bwd", grid=(nt,),
        in_specs=[rc, rc, rc, rn, rn, hn, hn, full(abr.shape), full(abi.shape), full(pw_r.shape), full(pw_i.shape),
                  full(bp_r.shape), full(bp_i.shape), full(cp_r.shape), full(cp_i.shape), full(dvec.shape)],
        out_specs=[rc, full(abr.shape), full(abi.shape), full(bp_r.shape), full(bp_i.shape), full(cp_r.shape), full(cp_i.shape),
                   full(dvec.shape)],
        out_shape=[sd((s, c), BF16), sd(abr.shape, F32), sd(abi.shape, F32), sd(bp_r.shape, F32), sd(bp_i.shape, F32),
                   sd(cp_r.shape, F32), sd(cp_i.shape, F32), sd(dvec.shape, F32)],
        scratch_shapes=[pltpu.VMEM((SUBLANES, n), F32), pltpu.VMEM((SUBLANES, n), F32), pltpu.VMEM(pw_i.shape, F32)],
        args=(dgy, ypre, u, hr, hi, hr, hi, abr, abi, pw_r, pw_i, bp_r, bp_i, cp_r, cp_i, dvec))


S5_LANE_CHUNK = 512


def _s5_tables(a_re, a_im, log_dt, bt_re, bt_im):
    g, p = a_re.shape
    gc = bt_re.shape[0]

    def body(ar_ref, ai_ref, dt_ref, br_ref, bi_ref, abr_ref, abi_ref, tr_ref, ti_ref, bbr_ref, bbi_ref):
        abr, abi, bbr, bbi = _s5_param_fn(ar_ref[...], ai_ref[...], dt_ref[...], br_ref[...], bi_ref[...])
        abr_ref[...] = abr
        abi_ref[...] = abi
        bbr_ref[...] = bbr
        bbi_ref[...] = bbi
        pows = [(abr, abi)]
        for _ in range(1, SUBLANES):
            qr, qi = pows[-1]
            pows.append((qr * abr - qi * abi, qr * abi + qi * abr))
        zero = jnp.zeros_like(abr)
        for r in range(SUBLANES):
            for k in range(3):
                sh = 1 << k
                tr_ref[k, r] = pows[sh - 1][0] if r >= sh else zero
                ti_ref[k, r] = pows[sh - 1][1] if r >= sh else zero
            tr_ref[3, r] = pows[r][0]
            ti_ref[3, r] = pows[r][1]

    sd = jax.ShapeDtypeStruct
    return pl.pallas_call(
        body, name="s5_tables",
        out_shape=[sd((g, p), F32), sd((g, p), F32), sd((4, SUBLANES, g, p), F32), sd((4, SUBLANES, g, p), F32),
                   sd((gc, g, p), F32), sd((gc, g, p), F32)],
    )(a_re, a_im, log_dt, bt_re, bt_im)


def _cmul_add(br, bi, tr, ti, sr, si):
    return br + tr * sr - ti * si, bi + tr * si + ti * sr


def _s5_fwd2(u, tab_r, tab_i, bp_r, bp_i, cp_r, cp_i, dvec, ts=256, side=None):
    s, c = u.shape
    n = tab_r.shape[2]
    nblk, cb, nb = bp_r.shape
    ts = min(ts, s)
    nsl = ts // SUBLANES
    lc = min(S5_LANE_CHUNK, n)

    def body(u_ref, tr_ref, ti_ref, bpr_ref, bpi_ref, cpr_ref, cpi_ref, d_ref, hr_ref, hi_ref, yp_ref, gy_ref,
             bur_ref, bui_ref, car_r, car_i):
        i = pl.program_id(0)

        @pl.when(i == 0)
        def _():
            car_r[...] = jnp.zeros_like(car_r)
            car_i[...] = jnp.zeros_like(car_i)

        uv = u_ref[...]
        ub = uv.astype(BF16)
        for k in range(nblk):
            bur_ref[:, k * nb:(k + 1) * nb] = jnp.dot(ub[:, k * cb:(k + 1) * cb], bpr_ref[k], preferred_element_type=F32)
            bui_ref[:, k * nb:(k + 1) * nb] = jnp.dot(ub[:, k * cb:(k + 1) * cb], bpi_ref[k], preferred_element_type=F32)
        for q in range(n // lc):
            sl = slice(q * lc, (q + 1) * lc)
            tabs = [(tr_ref[k, :, sl], ti_ref[k, :, sl]) for k in range(4)]

            def slab(j, carry, sl=sl, tabs=tabs):
                cr, ci = carry
                r0 = pl.multiple_of(j * SUBLANES, SUBLANES)
                br, bi = bur_ref[pl.ds(r0, SUBLANES), sl], bui_ref[pl.ds(r0, SUBLANES), sl]
                for k in range(3):
                    sh = 1 << k
                    br, bi = _cmul_add(br, bi, tabs[k][0], tabs[k][1], pltpu.roll(br, sh, 0), pltpu.roll(bi, sh, 0))
                hr, hi = _cmul_add(br, bi, tabs[3][0], tabs[3][1], jnp.broadcast_to(cr, br.shape), jnp.broadcast_to(ci, bi.shape))
                hr_ref[pl.ds(r0, SUBLANES), sl] = hr
                hi_ref[pl.ds(r0, SUBLANES), sl] = hi
                return hr[SUBLANES - 1:, :], hi[SUBLANES - 1:, :]

            cr, ci = lax.fori_loop(0, nsl, slab, (car_r[:, sl], car_i[:, sl]), unroll=2)
            car_r[:, sl] = cr
            car_i[:, sl] = ci
        hrb, hib = hr_ref[...].astype(BF16), hi_ref[...].astype(BF16)
        y = jnp.concatenate([jnp.dot(hrb[:, k * nb:(k + 1) * nb], cpr_ref[k], preferred_element_type=F32)
                             - jnp.dot(hib[:, k * nb:(k + 1) * nb], cpi_ref[k], preferred_element_type=F32) for k in range(nblk)], axis=1)
        yp = y + d_ref[...] * uv
        yp_ref[...] = yp
        gy_ref[...] = _gelu(yp).astype(BF16)

    full = lambda shape: pl.BlockSpec(shape, lambda i: (0,) * len(shape))
    rc = pl.BlockSpec((ts, c), lambda i: (i, 0))
    rn = pl.BlockSpec((ts, n), lambda i: (i, 0))
    sd = jax.ShapeDtypeStruct
    nt = s // ts
    return _call_with_side(
        body, side, lambda: pl.program_id(0) == 0, lambda: pl.program_id(0) == nt - 1,
        name="s5_fwd", grid=(nt,),
        in_specs=[rc, full(tab_r.shape), full(tab_i.shape), full(bp_r.shape), full(bp_i.shape), full(cp_r.shape), full(cp_i.shape),
                  full(dvec.shape)],
        out_specs=[rn, rn, rc, rc],
        out_shape=[sd((s, n), F32), sd((s, n), F32), sd((s, c), F32), sd((s, c), BF16)],
        scratch_shapes=[pltpu.VMEM((ts, n), F32), pltpu.VMEM((ts, n), F32), pltpu.VMEM((1, n), F32), pltpu.VMEM((1, n), F32)],
        args=(u, tab_r, tab_i, bp_r, bp_i, cp_r, cp_i, dvec))


def _s5_bwd2(dgy, ypre, u, hr, hi, rtab_r, rtab_i, bp_r, bp_i, cp_r, cp_i, dvec, ts=256, side=None):
    s, c = u.shape
    n = rtab_r.shape[2]
    nblk, cb, nb = bp_r.shape
    ts = min(ts, s)
    nt = s // ts
    hb = ts // SUBLANES
    nsl = ts // SUBLANES
    lc = min(S5_LANE_CHUNK, n)
    tn_dims = (((0,), (0,)), ((), ()))
    nt_dims = (((1,), (1,)), ((), ()))

    def body(dgy_ref, yp_ref, u_ref, hr_ref, hi_ref, hrh_ref, hih_ref, tr_ref, ti_ref, bpr_ref, bpi_ref, cpr_ref, cpi_ref, d_ref,
             du_ref, dar_ref, dai_ref, dbr_ref, dbi_ref, dcr_ref, dci_ref, dd_ref, lr_ref, li_ref, car_r, car_i):
        i = pl.program_id(0)
        time_first = i == nt - 1

        @pl.when(i == 0)
        def _():
            car_r[...] = jnp.zeros_like(car_r)
            car_i[...] = jnp.zeros_like(car_i)
            for ref in (dar_ref, dai_ref, dbr_ref, dbi_ref, dcr_ref, dci_ref, dd_ref):
                ref[...] = jnp.zeros_like(ref)

        uv = u_ref[...]
        _, dgel = _gelu_and_grad(yp_ref[...])
        dyv = dgy_ref[...] * dgel
        dd_ref[...] += jnp.sum(dyv * uv, axis=0, keepdims=True)
        dyb = dyv.astype(BF16)
        hrb, hib = hr_ref[...].astype(BF16), hi_ref[...].astype(BF16)
        for k in range(nblk):
            dblk = dyb[:, k * cb:(k + 1) * cb]
            lr_ref[:, k * nb:(k + 1) * nb] = lax.dot_general(dblk, cpr_ref[k], nt_dims, preferred_element_type=F32)
            li_ref[:, k * nb:(k + 1) * nb] = -lax.dot_general(dblk, cpi_ref[k], nt_dims, preferred_element_type=F32)
            dcr_ref[k] += lax.dot_general(hrb[:, k * nb:(k + 1) * nb], dblk, tn_dims, preferred_element_type=F32)
            dci_ref[k] += lax.dot_general(hib[:, k * nb:(k + 1) * nb], dblk, tn_dims, preferred_element_type=F32)
        row8 = _rows((SUBLANES, lc))
        for q in range(n // lc):
            sl = slice(q * lc, (q + 1) * lc)
            tabs = [(tr_ref[k, :, sl], ti_ref[k, :, sl]) for k in range(4)]
            halo_r = jnp.where(time_first, 0.0, hrh_ref[SUBLANES - 1:, sl])
            halo_i = jnp.where(time_first, 0.0, hih_ref[SUBLANES - 1:, sl])

            def slab(jj, carry, sl=sl, tabs=tabs, halo_r=halo_r, halo_i=halo_i):
                nr, ni, acc_r, acc_i = carry
                j = nsl - 1 - jj
                r0 = pl.multiple_of(j * SUBLANES, SUBLANES)
                br, bi = lr_ref[pl.ds(r0, SUBLANES), sl], li_ref[pl.ds(r0, SUBLANES), sl]
                for k in range(3):
                    sh = 1 << k
                    br, bi = _cmul_add(br, bi, tabs[k][0], tabs[k][1], pltpu.roll(br, SUBLANES - sh, 0),
                                       pltpu.roll(bi, SUBLANES - sh, 0))
                lr, li = _cmul_add(br, bi, tabs[3][0], tabs[3][1], jnp.broadcast_to(nr, br.shape), jnp.broadcast_to(ni, bi.shape))
                lr_ref[pl.ds(r0, SUBLANES), sl] = lr
                li_ref[pl.ds(r0, SUBLANES), sl] = li
                p0 = pl.multiple_of(jnp.maximum(j - 1, 0) * SUBLANES, SUBLANES)
                prev_r = jnp.where(j == 0, halo_r, hr_ref[pl.ds(p0, SUBLANES), sl][SUBLANES - 1:, :])
                prev_i = jnp.where(j == 0, halo_i, hi_ref[pl.ds(p0, SUBLANES), sl][SUBLANES - 1:, :])
                hpr = jnp.where(row8 == 0, jnp.broadcast_to(prev_r, br.shape), pltpu.roll(hr_ref[pl.ds(r0, SUBLANES), sl], 1, 0))
                hpi = jnp.where(row8 == 0, jnp.broadcast_to(prev_i, bi.shape), pltpu.roll(hi_ref[pl.ds(r0, SUBLANES), sl], 1, 0))
                return lr[:1, :], li[:1, :], acc_r + (lr * hpr + li * hpi), acc_i + (li * hpr - lr * hpi)

            zero = jnp.zeros((SUBLANES, lc), F32)
            nr, ni, acc_r, acc_i = lax.fori_loop(0, nsl, slab, (car_r[:, sl], car_i[:, sl], zero, zero), unroll=2)
            car_r[:, sl] = nr
            car_i[:, sl] = ni
            dar_ref[:, sl] += jnp.sum(acc_r, axis=0, keepdims=True)
            dai_ref[:, sl] += jnp.sum(acc_i, axis=0, keepdims=True)
        lrb, lib = lr_ref[...].astype(BF16), li_ref[...].astype(BF16)
        ub = uv.astype(BF16)
        du = []
        for k in range(nblk):
            ublk = ub[:, k * cb:(k + 1) * cb]
            lrk, lik = lrb[:, k * nb:(k + 1) * nb], lib[:, k * nb:(k + 1) * nb]
            dbr_ref[k] += lax.dot_general(ublk, lrk, tn_dims, preferred_element_type=F32)
            dbi_ref[k] += lax.dot_general(ublk, lik, tn_dims, preferred_element_type=F32)
            du.append(lax.dot_general(lrk, bpr_ref[k], nt_dims, preferred_element_type=F32)
                      + lax.dot_general(lik, bpi_ref[k], nt_dims, preferred_element_type=F32))
        du_ref[...] = (d_ref[...] * dyv + jnp.concatenate(du, axis=1)).astype(BF16)

    full = lambda shape: pl.BlockSpec(shape, lambda i: (0,) * len(shape))
    rev = lambda i: nt - 1 - i
    halo_idx = lambda i: jnp.maximum(rev(i) * hb - 1, 0)
    rc = pl.BlockSpec((ts, c), lambda i: (rev(i), 0))
    rn = pl.BlockSpec((ts, n), lambda i: (rev(i), 0))
    hn = pl.BlockSpec((SUBLANES, n), lambda i: (halo_idx(i), 0))
    sd = jax.ShapeDtypeStruct
    vec_n = (1, n)
    return _call_with_side(
        body, side, lambda: pl.program_id(0) == 0, lambda: pl.program_id(0) == nt - 1,
        name="s5_bwd", grid=(nt,),
        in_specs=[rc, rc, rc, rn, rn, hn, hn, full(rtab_r.shape), full(rtab_i.shape),
                  full(bp_r.shape), full(bp_i.shape), full(cp_r.shape), full(cp_i.shape), full(dvec.shape)],
        out_specs=[rc, full(vec_n), full(vec_n), full(bp_r.shape), full(bp_i.shape), full(cp_r.shape), full(cp_i.shape),
                   full(dvec.shape)],
        out_shape=[sd((s, c), BF16), sd(vec_n, F32), sd(vec_n, F32), sd(bp_r.shape, F32), sd(bp_i.shape, F32),
                   sd(cp_r.shape, F32), sd(cp_i.shape, F32), sd(dvec.shape, F32)],
        scratch_shapes=[pltpu.VMEM((ts, n), F32), pltpu.VMEM((ts, n), F32), pltpu.VMEM((1, n), F32), pltpu.VMEM((1, n), F32)],
        args=(dgy, ypre, u, hr, hi, hr, hi, rtab_r, rtab_i, bp_r, bp_i, cp_r, cp_i, dvec))


def _s5_tables3(a_re, a_im, log_dt, bt_re, bt_im, seg):
    g, p = a_re.shape
    gc = bt_re.shape[0]
    nsq = int(math.log2(seg))
    assert 1 << nsq == seg

    def body(ar_ref, ai_ref, dt_ref, br_ref, bi_ref, tr_ref, ti_ref, rtr_ref, rti_ref, bbr_ref, bbi_ref):
        abr, abi, bbr, bbi = _s5_param_fn(ar_ref[...], ai_ref[...], dt_ref[...], br_ref[...], bi_ref[...])
        bbr_ref[...] = bbr
        bbi_ref[...] = bbi
        qr, qi = abr, abi
        for _ in range(nsq):
            qr, qi = qr * qr - qi * qi, 2.0 * qr * qi
        pows = [(qr, qi)]
        for _ in range(1, SUBLANES):
            cr, ci = pows[-1]
            pows.append((cr * qr - ci * qi, cr * qi + ci * qr))
        zero = jnp.zeros_like(abr)
        for r in range(SUBLANES):
            rows = [(pows[(1 << k) - 1] if r >= (1 << k) else (zero, zero)) for k in range(3)] + [pows[r], (abr, abi)]
            for k, (vr, vi) in enumerate(rows):
                tr_ref[k, r] = vr
                ti_ref[k, r] = vi
                rtr_ref[k, SUBLANES - 1 - r] = vr
                rti_ref[k, SUBLANES - 1 - r] = -vi

    sd = jax.ShapeDtypeStruct
    tab = sd((5, SUBLANES, g, p), F32)
    return pl.pallas_call(
        body, name="s5_tables", out_shape=[tab, tab, tab, tab, sd((gc, g, p), F32), sd((gc, g, p), F32)],
    )(a_re, a_im, log_dt, bt_re, bt_im)


def _segment_perm(ts):
    seg = ts // SUBLANES
    rho = jnp.arange(ts)
    src = (rho % SUBLANES) * seg + rho // SUBLANES
    return (src[:, None] == jnp.arange(ts)[None, :]).astype(BF16)


def _exact_rows(perm_t, x):
    hi = x.astype(BF16)
    r1 = x - hi.astype(F32)
    mid = r1.astype(BF16)
    lo = (r1 - mid.astype(F32)).astype(BF16)
    dot = lambda v: jnp.dot(perm_t, v, preferred_element_type=F32)
    return (dot(hi) + dot(mid)) + dot(lo)


def _s5_fwd3(u, perm, perm_t, tab_r, tab_i, bp_r, bp_i, cp_r, cp_i, dvec, ts=256, side=None):
    s, c = u.shape
    n = tab_r.shape[2]
    nblk, cb, nb = bp_r.shape
    ts = min(ts, s)
    seg = ts // SUBLANES
    lc = min(S5_LANE_CHUNK, n)

    def body(u_ref, p_ref, pt_ref, tr_ref, ti_ref, bpr_ref, bpi_ref, cpr_ref, cpi_ref, d_ref, hr_ref, hi_ref, yp_ref, gy_ref,
             bur_ref, bui_ref, car_r, car_i):
        i = pl.program_id(0)

        @pl.when(i == 0)
        def _():
            car_r[...] = jnp.zeros_like(car_r)
            car_i[...] = jnp.zeros_like(car_i)

        uv = u_ref[...]
        ubp = jnp.dot(p_ref[...], uv.astype(BF16), preferred_element_type=F32).astype(BF16)
        for k in range(nblk):
            bur_ref[:, k * nb:(k + 1) * nb] = jnp.dot(ubp[:, k * cb:(k + 1) * cb], bpr_ref[k], preferred_element_type=F32)
            bui_ref[:, k * nb:(k + 1) * nb] = jnp.dot(ubp[:, k * cb:(k + 1) * cb], bpi_ref[k], preferred_element_type=F32)
        row8 = _rows((SUBLANES, lc))
        for q in range(n // lc):
            sl = slice(q * lc, (q + 1) * lc)
            tabs = [(tr_ref[k, :, sl], ti_ref[k, :, sl]) for k in range(5)]
            a_r, a_i = tabs[4]

            def local(r, carry, sl=sl, a_r=a_r, a_i=a_i):
                r0 = pl.multiple_of(r * SUBLANES, SUBLANES)
                hr, hi = _cmul_add(bur_ref[pl.ds(r0, SUBLANES), sl], bui_ref[pl.ds(r0, SUBLANES), sl], a_r, a_i, carry[0], carry[1])
                hr_ref[pl.ds(r0, SUBLANES), sl] = hr
                hi_ref[pl.ds(r0, SUBLANES), sl] = hi
                return hr, hi

            zero = jnp.zeros((SUBLANES, lc), F32)
            er, ei = lax.fori_loop(0, seg, local, (zero, zero), unroll=4)
            for k in range(3):
                sh = 1 << k
                er, ei = _cmul_add(er, ei, tabs[k][0], tabs[k][1], pltpu.roll(er, sh, 0), pltpu.roll(ei, sh, 0))
            cin_r, cin_i = jnp.broadcast_to(car_r[:, sl], er.shape), jnp.broadcast_to(car_i[:, sl], ei.shape)
            er, ei = _cmul_add(er, ei, tabs[3][0], tabs[3][1], cin_r, cin_i)
            car_r[:, sl] = er[SUBLANES - 1:, :]
            car_i[:, sl] = ei[SUBLANES - 1:, :]
            c_r = jnp.where(row8 == 0, cin_r, pltpu.roll(er, 1, 0))
            c_i = jnp.where(row8 == 0, cin_i, pltpu.roll(ei, 1, 0))

            def fix(r, carry, sl=sl, a_r=a_r, a_i=a_i, c_r=c_r, c_i=c_i):
                pr, pi = carry
                r0 = pl.multiple_of(r * SUBLANES, SUBLANES)
                hr, hi = _cmul_add(hr_ref[pl.ds(r0, SUBLANES), sl], hi_ref[pl.ds(r0, SUBLANES), sl], pr, pi, c_r, c_i)
                hr_ref[pl.ds(r0, SUBLANES), sl] = hr
                hi_ref[pl.ds(r0, SUBLANES), sl] = hi
                return pr * a_r - pi * a_i, pr * a_i + pi * a_r

            lax.fori_loop(0, seg, fix, (a_r, a_i), unroll=4)
        hrb, hib = hr_ref[...].astype(BF16), hi_ref[...].astype(BF16)
        y = jnp.concatenate([jnp.dot(hrb[:, k * nb:(k + 1) * nb], cpr_ref[k], preferred_element_type=F32)
                             - jnp.dot(hib[:, k * nb:(k + 1) * nb], cpi_ref[k], preferred_element_type=F32) for k in range(nblk)], axis=1)
        yp = _exact_rows(pt_ref[...], y) + d_ref[...] * uv
        yp_ref[...] = yp
        gy_ref[...] = _gelu(yp).astype(BF16)

    full = lambda shape: pl.BlockSpec(shape, lambda i: (0,) * len(shape))
    rc = pl.BlockSpec((ts, c), lambda i: (i, 0))
    rn = pl.BlockSpec((ts, n), lambda i: (i, 0))
    sd = jax.ShapeDtypeStruct
    nt = s // ts
    return _call_with_side(
        body, side, lambda: pl.program_id(0) == 0, lambda: pl.program_id(0) == nt - 1,
        name="s5_fwd", grid=(nt,),
        in_specs=[rc, full(perm.shape), full(perm_t.shape), full(tab_r.shape), full(tab_i.shape), full(bp_r.shape), full(bp_i.shape),
                  full(cp_r.shape), full(cp_i.shape), full(dvec.shape)],
        out_specs=[rn, rn, rc, rc],
        out_shape=[sd((s, n), F32), sd((s, n), F32), sd((s, c), F32), sd((s, c), BF16)],
        scratch_shapes=[pltpu.VMEM((ts, n), F32), pltpu.VMEM((ts, n), F32), pltpu.VMEM((1, n), F32), pltpu.VMEM((1, n), F32)],
        args=(u, perm, perm_t, tab_r, tab_i, bp_r, bp_i, cp_r, cp_i, dvec))


def _s5_bwd3(dgy, ypre, u, hr, hi, perm, perm_t, rtab_r, rtab_i, bp_r, bp_i, cp_r, cp_i, dvec, ts=256, side=None):
    s, c = u.shape
    n = rtab_r.shape[2]
    nblk, cb, nb = bp_r.shape
    ts = min(ts, s)
    nt = s // ts
    hb = ts // SUBLANES
    seg = ts // SUBLANES
    lc = min(S5_LANE_CHUNK, n)
    tn_dims = (((0,), (0,)), ((), ()))
    nt_dims = (((1,), (1,)), ((), ()))

    def body(dgy_ref, yp_ref, u_ref, hr_ref, hi_ref, hrh_ref, hih_ref, p_ref, pt_ref, tr_ref, ti_ref, bpr_ref, bpi_ref,
             cpr_ref, cpi_ref, d_ref, du_ref, dar_ref, dai_ref, dbr_ref, dbi_ref, dcr_ref, dci_ref, dd_ref, lr_ref, li_ref,
             car_r, car_i):
        i = pl.program_id(0)
        time_first = i == nt - 1

        @pl.when(i == 0)
        def _():
            car_r[...] = jnp.zeros_like(car_r)
            car_i[...] = jnp.zeros_like(car_i)
            for ref in (dar_ref, dai_ref, dbr_ref, dbi_ref, dcr_ref, dci_ref, dd_ref):
                ref[...] = jnp.zeros_like(ref)

        uv = u_ref[...]
        _, dgel = _gelu_and_grad(yp_ref[...])
        dyv = dgy_ref[...] * dgel
        dd_ref[...] += jnp.sum(dyv * uv, axis=0, keepdims=True)
        perm_m = p_ref[...]
        dyb = jnp.dot(perm_m, dyv.astype(BF16), preferred_element_type=F32).astype(BF16)
        ub = jnp.dot(perm_m, uv.astype(BF16), preferred_element_type=F32).astype(BF16)
        hrb, hib = hr_ref[...].astype(BF16), hi_ref[...].astype(BF16)
        for k in range(nblk):
            dblk = dyb[:, k * cb:(k + 1) * cb]
            lr_ref[:, k * nb:(k + 1) * nb] = lax.dot_general(dblk, cpr_ref[k], nt_dims, preferred_element_type=F32)
            li_ref[:, k * nb:(k + 1) * nb] = -lax.dot_general(dblk, cpi_ref[k], nt_dims, preferred_element_type=F32)
            dcr_ref[k] += lax.dot_general(hrb[:, k * nb:(k + 1) * nb], dblk, tn_dims, preferred_element_type=F32)
            dci_ref[k] += lax.dot_general(hib[:, k * nb:(k + 1) * nb], dblk, tn_dims, preferred_element_type=F32)
        row8 = _rows((SUBLANES, lc))
        last0 = (seg - 1) * SUBLANES
        for q in range(n // lc):
            sl = slice(q * lc, (q + 1) * lc)
            tabs = [(tr_ref[k, :, sl], ti_ref[k, :, sl]) for k in range(5)]
            a_r, a_i = tabs[4]

            def local(rr, carry, sl=sl, a_r=a_r, a_i=a_i):
                r0 = pl.multiple_of((seg - 1 - rr) * SUBLANES, SUBLANES)
                lr, li = _cmul_add(lr_ref[pl.ds(r0, SUBLANES), sl], li_ref[pl.ds(r0, SUBLANES), sl], a_r, a_i, carry[0], carry[1])
                lr_ref[pl.ds(r0, SUBLANES), sl] = lr
                li_ref[pl.ds(r0, SUBLANES), sl] = li
                return lr, li

            zero = jnp.zeros((SUBLANES, lc), F32)
            er, ei = lax.fori_loop(0, seg, local, (zero, zero), unroll=4)
            for k in range(3):
                sh = 1 << k
                er, ei = _cmul_add(er, ei, tabs[k][0], tabs[k][1], pltpu.roll(er, SUBLANES - sh, 0), pltpu.roll(ei, SUBLANES - sh, 0))
            cin_r, cin_i = jnp.broadcast_to(car_r[:, sl], er.shape), jnp.broadcast_to(car_i[:, sl], ei.shape)
            er, ei = _cmul_add(er, ei, tabs[3][0], tabs[3][1], cin_r, cin_i)
            car_r[:, sl] = er[:1, :]
            car_i[:, sl] = ei[:1, :]
            c_r = jnp.where(row8 == SUBLANES - 1, cin_r, pltpu.roll(er, SUBLANES - 1, 0))
            c_i = jnp.where(row8 == SUBLANES - 1, cin_i, pltpu.roll(ei, SUBLANES - 1, 0))
            halo_r = jnp.where(time_first, 0.0, hrh_ref[SUBLANES - 1:, sl])
            halo_i = jnp.where(time_first, 0.0, hih_ref[SUBLANES - 1:, sl])
            hp0_r = jnp.where(row8 == 0, jnp.broadcast_to(halo_r, zero.shape), pltpu.roll(hr_ref[pl.ds(last0, SUBLANES), sl], 1, 0))
            hp0_i = jnp.where(row8 == 0, jnp.broadcast_to(halo_i, zero.shape), pltpu.roll(hi_ref[pl.ds(last0, SUBLANES), sl], 1, 0))

            def fix(rr, carry, sl=sl, a_r=a_r, a_i=a_i, c_r=c_r, c_i=c_i, hp0_r=hp0_r, hp0_i=hp0_i):
                pr, pi, acc_r, acc_i = carry
                r = seg - 1 - rr
                r0 = pl.multiple_of(r * SUBLANES, SUBLANES)
                lr, li = _cmul_add(lr_ref[pl.ds(r0, SUBLANES), sl], li_ref[pl.ds(r0, SUBLANES), sl], pr, pi, c_r, c_i)
                lr_ref[pl.ds(r0, SUBLANES), sl] = lr
                li_ref[pl.ds(r0, SUBLANES), sl] = li
                p0 = pl.multiple_of(jnp.maximum(r - 1, 0) * SUBLANES, SUBLANES)
                hpr = jnp.where(r == 0, hp0_r, hr_ref[pl.ds(p0, SUBLANES), sl])
                hpi = jnp.where(r == 0, hp0_i, hi_ref[pl.ds(p0, SUBLANES), sl])
                return (pr * a_r - pi * a_i, pr * a_i + pi * a_r, acc_r + (lr * hpr + li * hpi), acc_i + (li * hpr - lr * hpi))

            _, _, acc_r, acc_i = lax.fori_loop(0, seg, fix, (a_r, a_i, zero, zero), unroll=4)
            dar_ref[:, sl] += jnp.sum(acc_r, axis=0, keepdims=True)
            dai_ref[:, sl] += jnp.sum(acc_i, axis=0, keepdims=True)
        lrb, lib = lr_ref[...].astype(BF16), li_ref[...].astype(BF16)
        du = []
        for k in range(nblk):
            ublk = ub[:, k * cb:(k + 1) * cb]
            lrk, lik = lrb[:, k * nb:(k + 1) * nb], lib[:, k * nb:(k + 1) * nb]
            dbr_ref[k] += lax.dot_general(ublk, lrk, tn_dims, preferred_element_type=F32)
            dbi_ref[k] += lax.dot_general(ublk, lik, tn_dims, preferred_element_type=F32)
            du.append(lax.dot_general(lrk, bpr_ref[k], nt_dims, preferred_element_type=F32)
                      + lax.dot_general(lik, bpi_ref[k], nt_dims, preferred_element_type=F32))
        du_ref[...] = (d_ref[...] * dyv + _exact_rows(pt_ref[...], jnp.concatenate(du, axis=1))).astype(BF16)

    full = lambda shape: pl.BlockSpec(shape, lambda i: (0,) * len(shape))
    rev = lambda i: nt - 1 - i
    halo_idx = lambda i: jnp.maximum(rev(i) * hb - 1, 0)
    rc = pl.BlockSpec((ts, c), lambda i: (rev(i), 0))
    rn = pl.BlockSpec((ts, n), lambda i: (rev(i), 0))
    hn = pl.BlockSpec((SUBLANES, n), lambda i: (halo_idx(i), 0))
    sd = jax.ShapeDtypeStruct
    vec_n = (1, n)
    return _call_with_side(
        body, side, lambda: pl.program_id(0) == 0, lambda: pl.program_id(0) == nt - 1,
        name="s5_bwd", grid=(nt,),
        in_specs=[rc, rc, rc, rn, rn, hn, hn, full(perm.shape), full(perm_t.shape), full(rtab_r.shape), full(rtab_i.shape),
                  full(bp_r.shape), full(bp_i.shape), full(cp_r.shape), full(cp_i.shape), full(dvec.shape)],
        out_specs=[rc, full(vec_n), full(vec_n), full(bp_r.shape), full(bp_i.shape), full(cp_r.shape), full(cp_i.shape),
                   full(dvec.shape)],
        out_shape=[sd((s, c), BF16), sd(vec_n, F32), sd(vec_n, F32), sd(bp_r.shape, F32), sd(bp_i.shape, F32),
                   sd(cp_r.shape, F32), sd(cp_i.shape, F32), sd(dvec.shape, F32)],
        scratch_shapes=[pltpu.VMEM((ts, n), F32), pltpu.VMEM((ts, n), F32), pltpu.VMEM((1, n), F32), pltpu.VMEM((1, n), F32)],
        args=(dgy, ypre, u, hr, hi, hr, hi, perm, perm_t, rtab_r, rtab_i, bp_r, bp_i, cp_r, cp_i, dvec))


def _glu(gl2, ts=512):
    _, s, c = gl2.shape
    ts = min(ts, s)

    def body(g_ref, o_ref):
        o_ref[...] = (g_ref[0] * _sigmoid(g_ref[1])).astype(BF16)

    return pl.pallas_call(
        body, name="glu", grid=(s // ts,), in_specs=[pl.BlockSpec((2, ts, c), lambda i: (0, i, 0))],
        out_specs=pl.BlockSpec((ts, c), lambda i: (i, 0)), out_shape=jax.ShapeDtypeStruct((s, c), BF16), compiler_params=_cparams(),
    )(gl2)


def _glu_bwd(gl2, d_o, ts=512):
    _, s, c = gl2.shape
    ts = min(ts, s)

    def body(g_ref, do_ref, o_ref):
        sg = _sigmoid(g_ref[1])
        dov = do_ref[...]
        o_ref[0] = (dov * sg).astype(BF16)
        o_ref[1] = (dov * g_ref[0] * sg * (1.0 - sg)).astype(BF16)

    blk = pl.BlockSpec((2, ts, c), lambda i: (0, i, 0))
    return pl.pallas_call(
        body, name="glu_bwd", grid=(s // ts,), in_specs=[blk, pl.BlockSpec((ts, c), lambda i: (i, 0))],
        out_specs=blk, out_shape=jax.ShapeDtypeStruct((2, s, c), BF16), compiler_params=_cparams(),
    )(gl2, d_o)


PACK_ROW_MULTIPLE = 1024
ELEMENTWISE_BLOCK_ELEMS = 256 * 1024


def _row_tile(rows, cols):
    pref = max(SUBLANES, 1 << int(math.log2(max(1, ELEMENTWISE_BLOCK_ELEMS // cols))))
    if rows <= pref:
        return rows
    t = pref
    while rows % t:
        t //= 2
    assert t >= SUBLANES, rows
    return t


def _sum_parts(rs, side=None):
    nl = len(rs)
    p, rows, cols = rs[0].shape
    tr = _row_tile(rows, cols)
    nt = rows // tr

    def body(*refs):
        o_ref = refs[nl]
        for l in range(nl):
            acc = refs[l][0].astype(F32)
            for k in range(1, p):
                acc = acc + refs[l][k].astype(F32)
            o_ref[l] = acc

    outs, got = _call_with_side(
        body, side, lambda: pl.program_id(0) == 0, lambda: pl.program_id(0) == nt - 1,
        name="sum_parts", grid=(nt,), in_specs=[pl.BlockSpec((p, tr, cols), lambda i: (0, i, 0))] * nl,
        out_specs=[pl.BlockSpec((nl, tr, cols), lambda i: (0, i, 0))], out_shape=[jax.ShapeDtypeStruct((nl, rows, cols), F32)],
        scratch_shapes=[], args=tuple(rs))
    return outs[0], got


def _adamw(w, g_parts, m, v, side=None):
    rows, cols = w.shape
    tr = _row_tile(rows, max(cols, LANES))
    ng = len(g_parts)
    emit_grad = ng > 1
    c1 = 1.0 / (1.0 - ADAM_B1 ** ADAM_STEP)
    c2 = 1.0 / (1.0 - ADAM_B2 ** ADAM_STEP)

    def body(*refs):
        w_ref, m_ref, v_ref = refs[0], refs[1 + ng], refs[2 + ng]
        dl_ref, nm_ref, nv_ref = refs[3 + ng:6 + ng]
        g = refs[1][...]
        for k in range(1, ng):
            g = g + refs[1 + k][...]
        mn = ADAM_B1 * m_ref[...] + (1.0 - ADAM_B1) * g
        vn = ADAM_B2 * v_ref[...] + (1.0 - ADAM_B2) * (g * g)
        if emit_grad:
            refs[6 + ng][...] = g
        nm_ref[...] = mn
        nv_ref[...] = vn
        dl_ref[...] = -ADAM_LR * ((mn * c1) / (jnp.sqrt(vn * c2) + ADAM_EPS) + ADAM_WD * w_ref[...])

    blk = pl.BlockSpec((tr, cols), lambda i: (i, 0))
    sd = jax.ShapeDtypeStruct((rows, cols), F32)
    nout = 4 if emit_grad else 3
    nt = rows // tr
    return _call_with_side(
        body, side, lambda: pl.program_id(0) == 0, lambda: pl.program_id(0) == nt - 1,
        name="adamw", grid=(nt,), in_specs=[blk] * (3 + ng), out_specs=[blk] * nout, out_shape=[sd] * nout,
        scratch_shapes=[], args=(w, *g_parts, m, v))


def _place():
    x, y, c = lax.axis_index("x"), lax.axis_index("y"), lax.axis_index("c")
    chips = [(1 - x, y), (x, 1 - y), (1 - x, 1 - y)]
    return x, y, c, chips


class Side:
    def __init__(self, ins, outs, kind, views=None):
        self.ins, self.outs, self.kind = list(ins), list(outs), kind
        n = len(self.ins)
        self.views = views or [None] * n
        self.sems = [pltpu.SemaphoreType.DMA((3 * n,)), pltpu.SemaphoreType.DMA((3 * n,)), pltpu.SemaphoreType.DMA((n,))]

    def _copies(self, ins, outs, send, recv, lsem):
        x, y, c, chips = _place()
        me = 2 * x + y
        local, out_going, in_coming = [], [], []
        for t in range(len(ins)):
            if self.kind == 'sibling':
                cp = pltpu.make_async_remote_copy(src_ref=ins[t], dst_ref=outs[t], send_sem=send.at[t], recv_sem=recv.at[t],
                                                  device_id=(x, y, 1 - c), device_id_type=MESH)
                out_going.append(cp)
                in_coming.append(cp)
                continue
            if self.kind == 'gather':
                src_local, srcs, dst_mine = ins[t], [ins[t]] * 3, outs[t].at[me]
            else:
                part = (lambda p, t=t: self.views[t](ins[t], p)) if self.views[t] else (lambda p, t=t: ins[t].at[p])
                src_local, srcs, dst_mine = part(me), [part(2 * px + py) for px, py in chips], outs[t].at[me]
            local.append(pltpu.make_async_copy(src_local, dst_mine, lsem.at[t]))
            for r, (px, py) in enumerate(chips):
                out_going.append(pltpu.make_async_remote_copy(
                    src_ref=srcs[r], dst_ref=dst_mine, send_sem=send.at[3 * t + r], recv_sem=recv.at[3 * t + r],
                    device_id=(px, py, c), device_id_type=MESH))
                in_coming.append(pltpu.make_async_remote_copy(
                    src_ref=srcs[r], dst_ref=outs[t].at[2 * px + py], send_sem=send.at[3 * t + r], recv_sem=recv.at[3 * t + r],
                    device_id=(px, py, c), device_id_type=MESH))
        return local, out_going, in_coming

    def start(self, ins, outs, send, recv, lsem):
        local, out_going, _ = self._copies(ins, outs, send, recv, lsem)
        for cp in local + out_going:
            cp.start()

    def wait(self, ins, outs, send, recv, lsem):
        local, out_going, in_coming = self._copies(ins, outs, send, recv, lsem)
        for cp in in_coming:
            cp.wait_recv()
        for cp in out_going:
            cp.wait_send()
        for cp in local:
            cp.wait()


def _gather_side(shards):
    return Side(shards, [jax.ShapeDtypeStruct((N_CHIPS,) + s.shape, s.dtype) for s in shards], 'gather')


def _scatter_side(grads, shapes, views):
    return Side(grads, [jax.ShapeDtypeStruct(s, g.dtype) for g, s in zip(grads, shapes)], 'scatter', views)


def _sibling_side(arrs):
    return Side(arrs, [jax.ShapeDtypeStruct(a.shape, a.dtype) for a in arrs], 'sibling')


def _halves_view(ref, p):
    half = ref.shape[2] // 2
    return ref.at[p // 2, :, pl.ds((p % 2) * half, half)]


def _call_with_side(body, side, first, last, *, name, grid, in_specs, out_specs, out_shape, scratch_shapes, args):
    if side is None:
        outs = pl.pallas_call(body, name=name, grid=grid, in_specs=in_specs, out_specs=out_specs, out_shape=out_shape,
                              scratch_shapes=scratch_shapes, compiler_params=_cparams())(*args)
        return outs, []
    n_in, n_out, n_sc = len(in_specs), len(out_specs), len(scratch_shapes)
    ns_in, ns_out = len(side.ins), len(side.outs)

    def wrapped(*refs):
        base_in, s_in = refs[:n_in], refs[n_in:n_in + ns_in]
        o0 = n_in + ns_in
        base_out, s_out = refs[o0:o0 + n_out], refs[o0 + n_out:o0 + n_out + ns_out]
        sc0 = o0 + n_out + ns_out
        base_sc, sems = refs[sc0:sc0 + n_sc], refs[sc0 + n_sc:]

        @pl.when(first())
        def _():
            side.start(s_in, s_out, *sems)

        body(*base_in, *base_out, *base_sc)

        @pl.when(last())
        def _():
            side.wait(s_in, s_out, *sems)

    any_spec = pl.BlockSpec(memory_space=pl.ANY)
    outs = pl.pallas_call(
        wrapped, name=name, grid=grid, in_specs=list(in_specs) + [any_spec] * ns_in, out_specs=list(out_specs) + [any_spec] * ns_out,
        out_shape=list(out_shape) + side.outs, scratch_shapes=list(scratch_shapes) + side.sems, compiler_params=_cparams(),
    )(*args, *side.ins)
    return outs[:n_out], outs[n_out:]


def _run_side(name, side):
    def body(*refs):
        n = len(side.ins)
        side.start(refs[:n], refs[n:2 * n], *refs[2 * n:])
        side.wait(refs[:n], refs[n:2 * n], *refs[2 * n:])

    any_spec = pl.BlockSpec(memory_space=pl.ANY)
    return pl.pallas_call(body, name=name, in_specs=[any_spec] * len(side.ins), out_specs=[any_spec] * len(side.outs),
                          out_shape=side.outs, scratch_shapes=side.sems)(*side.ins)


def _gather_shards(shards, layer_major):
    n = len(shards)

    def body(*refs):
        ins, outs = refs[:n], refs[n:2 * n]
        send, recv, lsem = refs[2 * n:]
        x, y, c, chips = _place()
        me = 2 * x + y

        def slot(t, chip):
            return outs[t].at[:, chip] if layer_major[t] else outs[t].at[chip]

        local, sends = [], []
        for t in range(n):
            cp = pltpu.make_async_copy(ins[t], slot(t, me), lsem.at[t])
            cp.start()
            local.append(cp)
            for r, (px, py) in enumerate(chips):
                rc = pltpu.make_async_remote_copy(src_ref=ins[t], dst_ref=slot(t, me), send_sem=send.at[3 * t + r],
                                                  recv_sem=recv.at[3 * t + r], device_id=(px, py, c), device_id_type=MESH)
                rc.start()
                sends.append(rc)
        for t in range(n):
            for r, (px, py) in enumerate(chips):
                pltpu.make_async_remote_copy(src_ref=ins[t], dst_ref=slot(t, 2 * px + py), send_sem=send.at[3 * t + r],
                                             recv_sem=recv.at[3 * t + r], device_id=(px, py, c), device_id_type=MESH).wait_recv()
        for rc in sends:
            rc.wait_send()
        for cp in local:
            cp.wait()

    any_spec = pl.BlockSpec(memory_space=pl.ANY)
    return pl.pallas_call(
        body, name="gather_shards", in_specs=[any_spec] * n, out_specs=[any_spec] * n,
        out_shape=[jax.ShapeDtypeStruct((s.shape[0], N_CHIPS) + s.shape[1:] if lm else (N_CHIPS,) + s.shape, s.dtype)
                   for s, lm in zip(shards, layer_major)],
        scratch_shapes=[pltpu.SemaphoreType.DMA((3 * n,)), pltpu.SemaphoreType.DMA((3 * n,)), pltpu.SemaphoreType.DMA((n,))],
    )(*shards)


def _scatter_grads(groups):
    flat = [(gi, li, a) for gi, grp in enumerate(groups) for li, a in enumerate(grp)]
    n = len(flat)
    ng = len(groups)

    def body(*refs):
        ins, outs = refs[:n], refs[n:n + ng]
        send, recv, lsem = refs[n + ng:]
        x, y, c, chips = _place()
        me = 2 * x + y
        local, sends = [], []
        for t, (gi, li, _) in enumerate(flat):
            cp = pltpu.make_async_copy(ins[t].at[me], outs[gi].at[me, li], lsem.at[t])
            cp.start()
            local.append(cp)
            for r, (px, py) in enumerate(chips):
                rc = pltpu.make_async_remote_copy(src_ref=ins[t].at[2 * px + py], dst_ref=outs[gi].at[me, li],
                                                  send_sem=send.at[3 * t + r], recv_sem=recv.at[3 * t + r],
                                                  device_id=(px, py, c), device_id_type=MESH)
                rc.start()
                sends.append(rc)
        for t, (gi, li, _) in enumerate(flat):
            for r, (px, py) in enumerate(chips):
                pltpu.make_async_remote_copy(src_ref=ins[t].at[me], dst_ref=outs[gi].at[2 * px + py, li],
                                             send_sem=send.at[3 * t + r], recv_sem=recv.at[3 * t + r],
                                             device_id=(px, py, c), device_id_type=MESH).wait_recv()
        for rc in sends:
            rc.wait_send()
        for cp in local:
            cp.wait()

    any_spec = pl.BlockSpec(memory_space=pl.ANY)
    return pl.pallas_call(
        body, name="scatter_grads", in_specs=[any_spec] * n, out_specs=[any_spec] * ng,
        out_shape=[jax.ShapeDtypeStruct((N_CHIPS, len(grp)) + grp[0].shape[1:], grp[0].dtype) for grp in groups],
        scratch_shapes=[pltpu.SemaphoreType.DMA((3 * n,)), pltpu.SemaphoreType.DMA((3 * n,)), pltpu.SemaphoreType.DMA((n,))],
    )(*[a for _, _, a in flat])


def _swap_with_sibling(arrs):
    n = len(arrs)

    def body(*refs):
        ins, outs = refs[:n], refs[n:2 * n]
        send, recv = refs[2 * n:]
        x, y, c, _ = _place()
        cps = []
        for t in range(n):
            rc = pltpu.make_async_remote_copy(src_ref=ins[t], dst_ref=outs[t], send_sem=send.at[t], recv_sem=recv.at[t],
                                              device_id=(x, y, 1 - c), device_id_type=MESH)
            rc.start()
            cps.append(rc)
        for rc in cps:
            rc.wait_recv()
        for rc in cps:
            rc.wait_send()

    any_spec = pl.BlockSpec(memory_space=pl.ANY)
    return pl.pallas_call(
        body, name="swap_with_sibling", in_specs=[any_spec] * n, out_specs=[any_spec] * n,
        out_shape=[jax.ShapeDtypeStruct(a.shape, a.dtype) for a in arrs],
        scratch_shapes=[pltpu.SemaphoreType.DMA((n,)), pltpu.SemaphoreType.DMA((n,))],
    )(*arrs)


def _allreduce_small(v):
    rows, cols = v.shape
    r8 = rows // (2 * N_CHIPS)
    assert r8 * 2 * N_CHIPS == rows and r8 % SUBLANES == 0, rows

    def body(v_ref, o_ref, sib_ref, cs_ref, slot_ref, send, recv):
        x, y, c, chips = _place()
        me = 2 * x + y
        sibling = (x, y, 1 - c)

        def eighth(ref, chip, core):
            return ref.at[pl.ds(pl.multiple_of((2 * chip + core) * r8, SUBLANES), r8)]

        def copy(src, dst, k, to):
            return pltpu.make_async_remote_copy(src_ref=src, dst_ref=dst, send_sem=send.at[k], recv_sem=recv.at[k],
                                                device_id=to, device_id_type=MESH)

        d2d = copy(v_ref, sib_ref, 0, sibling)
        d2d.start()
        d2d.wait_recv()
        cs_ref[...] = v_ref[...] + sib_ref[...]
        reduce_out = [copy(eighth(cs_ref, 2 * px + py, c), slot_ref.at[me], 1 + r, (px, py, c)) for r, (px, py) in enumerate(chips)]
        for cp in reduce_out:
            cp.start()
        slot_ref[me] = cs_ref[pl.ds(pl.multiple_of((2 * me + c) * r8, SUBLANES), r8), :]
        for r, (px, py) in enumerate(chips):
            copy(eighth(cs_ref, me, c), slot_ref.at[2 * px + py], 1 + r, (px, py, c)).wait_recv()
        o_ref[pl.ds(pl.multiple_of((2 * me + c) * r8, SUBLANES), r8), :] = (slot_ref[0] + slot_ref[1]) + (slot_ref[2] + slot_ref[3])
        mine = eighth(o_ref, me, c)
        hand_out = [copy(mine, mine, 4, sibling)] + [copy(mine, mine, 5 + r, (px, py, c)) for r, (px, py) in enumerate(chips)]
        for cp in hand_out:
            cp.start()
        passed_on = []
        for r, (px, py) in enumerate(chips):
            theirs = eighth(o_ref, 2 * px + py, c)
            copy(theirs, theirs, 5 + r, (px, py, c)).wait_recv()
            fw = copy(theirs, theirs, 8 + r, sibling)
            fw.start()
            passed_on.append(fw)
        sib_own = eighth(o_ref, me, 1 - c)
        copy(sib_own, sib_own, 4, sibling).wait_recv()
        for r, (px, py) in enumerate(chips):
            got = eighth(o_ref, 2 * px + py, 1 - c)
            copy(got, got, 8 + r, sibling).wait_recv()
        for cp in [d2d] + reduce_out + hand_out + passed_on:
            cp.wait_send()

    vm = pl.BlockSpec(memory_space=pltpu.VMEM)
    return pl.pallas_call(
        body, name="allreduce_small", in_specs=[vm], out_specs=vm, out_shape=jax.ShapeDtypeStruct((rows, cols), F32),
        scratch_shapes=[pltpu.VMEM((rows, cols), F32), pltpu.VMEM((rows, cols), F32), pltpu.VMEM((N_CHIPS, r8, cols), F32),
                        pltpu.SemaphoreType.DMA((11,)), pltpu.SemaphoreType.DMA((11,))],
        compiler_params=_cparams(),
    )(v)


def _pack(tensors):
    pieces = []
    for t in tensors:
        flat = t.reshape(-1)
        pad = (-flat.shape[0]) % (SUBLANES * LANES)
        pieces.append(jnp.pad(flat, (0, pad)).reshape(-1, LANES))
    rows = sum(p.shape[0] for p in pieces)
    pieces.append(jnp.zeros(((-rows) % PACK_ROW_MULTIPLE, LANES), tensors[0].dtype))
    return jnp.concatenate(pieces, axis=0)


def _unpack(buf, like):
    out, off = [], 0
    for t in like:
        size = math.prod(t.shape)
        rows = -(-size // (SUBLANES * LANES)) * SUBLANES
        out.append(buf[off:off + rows].reshape(-1)[:size].reshape(t.shape))
        off += rows
    return out


def _s5_pack_b(bb):
    gc, g, p = bb.shape
    q = S5_GROUPS_PER_BLOCK
    t = bb.reshape(gc, g // q, q, p).transpose(1, 2, 0, 3)
    eye = jnp.eye(q, dtype=bb.dtype)
    return (t[:, :, :, None, :] * eye[None, :, None, :, None]).reshape(g // q, q * gc, q * p)


def _s5_unpack_b(dbp, gc, p):
    nb = dbp.shape[0]
    q = S5_GROUPS_PER_BLOCK
    eye = jnp.eye(q, dtype=dbp.dtype)
    t = (dbp.reshape(nb, q, gc, q, p) * eye[None, :, None, :, None]).sum(axis=3)
    return t.transpose(2, 0, 1, 3).reshape(gc, nb * q, p)


def _s5_pack_c(cc):
    g, gc, p = cc.shape
    q = S5_GROUPS_PER_BLOCK
    t = cc.reshape(g // q, q, gc, p).transpose(0, 1, 3, 2)
    eye = jnp.eye(q, dtype=cc.dtype)
    return (t[:, :, :, None, :] * eye[None, :, None, :, None]).reshape(g // q, q * p, q * gc)


def _s5_unpack_c(dcp, gc, p):
    nb = dcp.shape[0]
    q = S5_GROUPS_PER_BLOCK
    eye = jnp.eye(q, dtype=dcp.dtype)
    t = (dcp.reshape(nb, q, p, q, gc) * eye[None, :, None, :, None]).sum(axis=3)
    return t.transpose(0, 1, 3, 2).reshape(nb * q, gc, p)


def _split2(m):
    return m.arr[:, 0]


def kernel(x, norm_mix_g, norm_ffn_g, norm_final_g, rg_w_in, rg_conv_w, rg_conv_b, rg_w_a, rg_b_a, rg_w_x, rg_b_x, rg_lambda, rg_w_out, s5_w_in, s5_a_re, s5_a_im, s5_log_dt, s5_b_re, s5_b_im, s5_c_re, s5_c_im, s5_d, s5_w_glu, s5_w_out, ffn_w_up, ffn_conv_w, ffn_conv_b, ffn_w_down, loss_target, m_norm_mix_g, m_norm_ffn_g, m_norm_final_g, m_rg_w_in, m_rg_conv_w, m_rg_conv_b, m_rg_w_a, m_rg_b_a, m_rg_w_x, m_rg_b_x, m_rg_lambda, m_rg_w_out, m_s5_w_in, m_s5_a_re, m_s5_a_im, m_s5_log_dt, m_s5_b_re, m_s5_b_im, m_s5_c_re, m_s5_c_im, m_s5_d, m_s5_w_glu, m_s5_w_out, m_ffn_w_up, m_ffn_conv_w, m_ffn_conv_b, m_ffn_w_down, v_norm_mix_g, v_norm_ffn_g, v_norm_final_g, v_rg_w_in, v_rg_conv_w, v_rg_conv_b, v_rg_w_a, v_rg_b_a, v_rg_w_x, v_rg_b_x, v_rg_lambda, v_rg_w_out, v_s5_w_in, v_s5_a_re, v_s5_a_im, v_s5_log_dt, v_s5_b_re, v_s5_b_im, v_s5_c_re, v_s5_c_im, v_s5_d, v_s5_w_glu, v_s5_w_out, v_ffn_w_up, v_ffn_conv_w, v_ffn_conv_b, v_ffn_w_down):
    w = dict(zip(PARAM_NAMES, (norm_mix_g, norm_ffn_g, norm_final_g, rg_w_in, rg_conv_w, rg_conv_b, rg_w_a, rg_b_a, rg_w_x, rg_b_x,
                               rg_lambda, rg_w_out, s5_w_in, s5_a_re, s5_a_im, s5_log_dt, s5_b_re, s5_b_im, s5_c_re, s5_c_im, s5_d,
                               s5_w_glu, s5_w_out, ffn_w_up, ffn_conv_w, ffn_conv_b, ffn_w_down)))
    mom = dict(zip(PARAM_NAMES, (m_norm_mix_g, m_norm_ffn_g, m_norm_final_g, m_rg_w_in, m_rg_conv_w, m_rg_conv_b, m_rg_w_a, m_rg_b_a,
                                 m_rg_w_x, m_rg_b_x, m_rg_lambda, m_rg_w_out, m_s5_w_in, m_s5_a_re, m_s5_a_im, m_s5_log_dt, m_s5_b_re,
                                 m_s5_b_im, m_s5_c_re, m_s5_c_im, m_s5_d, m_s5_w_glu, m_s5_w_out, m_ffn_w_up, m_ffn_conv_w,
                                 m_ffn_conv_b, m_ffn_w_down)))
    vel = dict(zip(PARAM_NAMES, (v_norm_mix_g, v_norm_ffn_g, v_norm_final_g, v_rg_w_in, v_rg_conv_w, v_rg_conv_b, v_rg_w_a, v_rg_b_a,
                                 v_rg_w_x, v_rg_b_x, v_rg_lambda, v_rg_w_out, v_s5_w_in, v_s5_a_re, v_s5_a_im, v_s5_log_dt, v_s5_b_re,
                                 v_s5_b_im, v_s5_c_re, v_s5_c_im, v_s5_d, v_s5_w_glu, v_s5_w_out, v_ffn_w_up, v_ffn_conv_w,
                                 v_ffn_conv_b, v_ffn_w_down)))
    _, s, d = x.shape
    depth = norm_mix_g.shape[0]
    n_grp, n_state = s5_a_re.shape[1], s5_a_re.shape[2]
    gc = s5_b_re.shape[3]
    d_ff = ffn_w_down.shape[1] * N_CHIPS
    s5_ts = min(256, s)
    s5_perm = _segment_perm(s5_ts)

    wb = {n: (w[n].astype(BF16) if n in BIG else w[n]) for n in SHARDED}
    gath = {}

    def mixer_keys(i):
        return [(n, i // 2) for n in MIXER_SHARDED[i % 2]] if i < depth else []

    def gather_side(keys):
        return _gather_side([wb[n][l] for n, l in keys])

    def put(keys, arrs):
        for k, a in zip(keys, arrs):
            gath[k] = a

    def wcol(n, l):
        return Mat(gath[(n, l)][:, None], 0, 'c')

    def wrow(n, l):
        g = gath[(n, l)]
        return Mat(g.reshape(1, 1, N_CHIPS * g.shape[1], g.shape[2]), 0, 'c')

    def rg_cw(l):
        return gath[('rg_conv_w', l)].transpose(1, 0, 2).reshape(RG_CONV_W, d)

    def s5_dv(l):
        return gath[('s5_d', l)].reshape(1, d)

    def f_cw(l):
        return gath[('ffn_conv_w', l)].transpose(1, 0, 2).reshape(FFN_CONV_W, 2, d_ff).transpose(1, 0, 2)

    tm = min(1024, s)
    d_up = 2 * d_ff // N_CHIPS
    f_cb = ffn_conv_b.reshape(depth, 2, 1, d_ff)
    put(mixer_keys(0), _run_side("gather_first", gather_side(mixer_keys(0))))

    h = x.reshape(s, d)
    saved = []
    for i in range(depth):
        j = i // 2
        sv = {'h_in': h}
        hn = _rms_fwd(h, norm_mix_g[i:i + 1])
        sv['hn'] = hn
        up_keys = [('ffn_w_up', i), ('ffn_conv_w', i)]
        if i % 2 == 0:
            xg = _mm("rg_in", 'nn', act(hn), wcol('rg_w_in', j), out_parts=2, tm=tm, tn=512, tk=d)
            xg2 = _split2(xg)
            wa, wx = rg_w_a[j].astype(BF16), rg_w_x[j].astype(BF16)
            ba, bx = rg_b_a[j].reshape(1, d), rg_b_x[j].reshape(1, d)
            (xr, hs, y), got = _rg_fwd(xg2, rg_cw(j), rg_conv_b[j:j + 1], wa, ba, wx, bx, rg_lambda[j:j + 1],
                                       side=gather_side(up_keys))
            put(up_keys, got)
            sv.update(xg2=xg2, xr=xr, hs=hs, y=y, wa=wa, wx=wx, ba=ba, bx=bx)
            h = _mm("rg_out", 'nn', act(y), wrow('rg_w_out', j), res=act(h), tm=tm, tn=d, tk=d).arr[0, 0]
        else:
            u = _mm("s5_in", 'nn', act(hn), wrow('s5_w_in', j), tm=tm, tn=d, tk=d).arr[0, 0]
            bt_re, bt_im = s5_b_re[j].transpose(2, 0, 1), s5_b_im[j].transpose(2, 0, 1)
            ldt = s5_log_dt[j].reshape(n_grp, 1)
            tab_r, tab_i, rtab_r, rtab_i, bbr, bbi = _s5_tables3(s5_a_re[j], s5_a_im[j], ldt, bt_re, bt_im, seg=s5_ts // SUBLANES)
            nn_ = n_grp * n_state
            tab_r, tab_i, rtab_r, rtab_i = (t.reshape(5, SUBLANES, nn_) for t in (tab_r, tab_i, rtab_r, rtab_i))
            prm = dict(bp_r=_s5_pack_b(bbr).astype(BF16), bp_i=_s5_pack_b(bbi).astype(BF16),
                       cp_r=_s5_pack_c(s5_c_re[j]).astype(BF16), cp_i=_s5_pack_c(s5_c_im[j]).astype(BF16), dvec=s5_dv(j))
            (hr, hi, ypre, gy), got = _s5_fwd3(u, s5_perm, s5_perm.T, tab_r, tab_i, ts=s5_ts, side=gather_side(up_keys), **prm)
            sv.update(rtab_r=rtab_r, rtab_i=rtab_i)
            put(up_keys, got)
            gl = _mm("s5_glu", 'nn', act(gy), wcol('s5_w_glu', j), out_parts=2, tm=tm, tn=512, tk=d)
            gl2 = _split2(gl)
            o = _glu(gl2)
            sv.update(u=u, prm=prm, hr=hr, hi=hi, ypre=ypre, gy=gy, gl2=gl2, o=o, bt_re=bt_re, bt_im=bt_im, ldt=ldt)
            h = _mm("s5_out", 'nn', act(o), wrow('s5_w_out', j), res=act(h), tm=tm, tn=d, tk=d).arr[0, 0]
        sv['h_mid'] = h
        hn2 = _rms_fwd(h, norm_ffn_g[i:i + 1])
        next_keys = [('ffn_w_down', i)] + mixer_keys(i + 1)
        (up2, c2, a_ffn), got = _ffn_up_act(hn2, gath[('ffn_w_up', i)], f_cw(i), f_cb[i], side=gather_side(next_keys))
        put(next_keys, got)
        sv.update(hn2=hn2, up2=up2, c2=c2, act=a_ffn)
        h = _mm("ffn_down", 'nn', act(a_ffn), wrow('ffn_w_down', i), res=act(h), tm=tm, tn=d, tk=d_ff // 2).arr[0, 0]
        saved.append(sv)

    loss_row, dh, dg_final = _loss_and_grad(h, norm_final_g.reshape(1, d), loss_target.reshape(s, d))
    loss = lax.psum(loss_row[0, 0], ("x", "y", "c"))

    gl_ = {n: [None] * w[n].shape[0] for n in PARAM_NAMES if n != 'norm_final_g'}
    recvd = {}

    def as4(n, a):
        return a.reshape((N_CHIPS,) + w[n].shape[1:])

    def scatter_side(keys):
        arrs, shapes, views = [], [], []
        for n, l in keys:
            shape = (N_CHIPS,) + w[n].shape[1:]
            halves = False
            arrs.append(gl_[n][l] if halves else gl_[n][l].reshape(shape))
            views.append(_halves_view if halves else None)
            shapes.append(shape)
        return _scatter_side(arrs, shapes, views)

    def record(keys, arrs):
        for k, a in zip(keys, arrs):
            recvd[k] = a

    pending = None
    for i in reversed(range(depth)):
        j = i // 2
        sv = saved[i]
        gl_['ffn_w_down'][i] = _mm("ffn_down_dw", 'tn', act(sv['act']), act(dh), out_dtype=BF16, tm=d_ff // 2, tn=d, tk=tm).arr
        (dup2, dcw2, dcb2), got = _ffn_bwd_fused(dh, gath[('ffn_w_down', i)].reshape(d_ff, d), sv['up2'], sv['c2'], f_cw(i),
                                                 side=scatter_side(pending) if pending else None)
        if pending:
            record(pending, got)
        gl_['ffn_conv_w'][i] = dcw2.transpose(1, 0, 2).reshape(FFN_CONV_W, 2 * d_ff)
        gl_['ffn_conv_b'][i] = dcb2.reshape(2 * d_ff)
        dup = Mat(dup2[:, None], 0, 'c')
        gl_['ffn_w_up'][i] = _mm("ffn_up_dw", 'tn', act(sv['hn2']), dup, out_parts=N_CHIPS, out_dtype=BF16, tm=d, tn=d_up, tk=tm).arr
        dh, dg = _mm_rms_bwd("ffn_up_dx", dup, wcol('ffn_w_up', i), sv['h_mid'], norm_ffn_g[i:i + 1], dh, tm=tm, tk=d_up)
        gl_['norm_ffn_g'][i] = dg[0]
        ffn_keys = [('ffn_w_up', i), ('ffn_w_down', i)]
        if i % 2 == 0:
            dy = _mm("rg_out_dx", 'nt', act(dh), wrow('rg_w_out', j), tm=tm, tn=d, tk=d).arr[0, 0]
            gl_['rg_w_out'][j] = _mm("rg_out_dw", 'tn', act(sv['y']), act(dh), out_dtype=BF16, tm=d, tn=d, tk=tm).arr
            (dxg2, dcw, dcb, dwa, dba, dwx, dbx, dlam), got = _rg_bwd(
                dy, sv['xg2'], sv['xr'], sv['hs'], rg_cw(j), sv['wa'], sv['ba'], sv['wx'], sv['bx'], rg_lambda[j:j + 1],
                side=scatter_side(ffn_keys))
            record(ffn_keys, got)
            gl_['rg_conv_w'][j] = dcw
            gl_['rg_conv_b'][j] = dcb[0]
            gl_['rg_w_a'][j], gl_['rg_w_x'][j] = dwa, dwx
            gl_['rg_b_a'][j], gl_['rg_b_x'][j] = dba.reshape(rg_b_a.shape[1:]), dbx.reshape(rg_b_x.shape[1:])
            gl_['rg_lambda'][j] = dlam[0]
            dxg = Mat(dxg2[:, None], 0, 'c')
            gl_['rg_w_in'][j] = _mm("rg_in_dw", 'tn', act(sv['hn']), dxg, out_parts=N_CHIPS, out_dtype=BF16, tm=d, tn=512, tk=tm).arr
            mix_dx = ("rg_in_dx", dxg, wcol('rg_w_in', j), 512)
            pending = [('rg_w_in', j), ('rg_w_out', j)]
        else:
            d_o = _mm("s5_out_dx", 'nt', act(dh), wrow('s5_w_out', j), tm=tm, tn=d, tk=d).arr[0, 0]
            gl_['s5_w_out'][j] = _mm("s5_out_dw", 'tn', act(sv['o']), act(dh), out_dtype=BF16, tm=d, tn=d, tk=tm).arr
            dgl2 = _glu_bwd(sv['gl2'], d_o)
            dgl = Mat(dgl2[:, None], 0, 'c')
            gl_['s5_w_glu'][j] = _mm("s5_glu_dw", 'tn', act(sv['gy']), dgl, out_parts=N_CHIPS, out_dtype=BF16, tm=d, tn=512, tk=tm).arr
            dgy = _mm("s5_glu_dx", 'nt', dgl, wcol('s5_w_glu', j), tm=tm, tn=d, tk=512).arr[0, 0]
            (du, dar, dai, dbpr, dbpi, dcpr, dcpi, dd), got = _s5_bwd3(
                dgy, sv['ypre'], sv['u'], sv['hr'], sv['hi'], s5_perm, s5_perm.T, sv['rtab_r'], sv['rtab_i'], ts=s5_ts,
                side=scatter_side(ffn_keys), **sv['prm'])
            record(ffn_keys, got)
            gl_['s5_d'][j] = dd[0]
            gl_['s5_c_re'][j] = _s5_unpack_c(dcpr, gc, n_state)
            gl_['s5_c_im'][j] = -_s5_unpack_c(dcpi, gc, n_state)
            d_are, d_aim, d_ldt, d_btr, d_bti = _s5_params_bwd(
                s5_a_re[j], s5_a_im[j], sv['ldt'], sv['bt_re'], sv['bt_im'], dar.reshape(n_grp, n_state), dai.reshape(n_grp, n_state),
                _s5_unpack_b(dbpr, gc, n_state), _s5_unpack_b(dbpi, gc, n_state))
            gl_['s5_a_re'][j], gl_['s5_a_im'][j], gl_['s5_log_dt'][j] = d_are, d_aim, d_ldt[:, 0]
            gl_['s5_b_re'][j], gl_['s5_b_im'][j] = d_btr.transpose(1, 2, 0), d_bti.transpose(1, 2, 0)
            dum = act(du)
            gl_['s5_w_in'][j] = _mm("s5_in_dw", 'tn', act(sv['hn']), dum, out_dtype=BF16, tm=d, tn=d, tk=tm).arr
            mix_dx = ("s5_in_dx", dum, wrow('s5_w_in', j), d)
            pending = [('s5_w_in', j), ('s5_w_glu', j), ('s5_w_out', j)]
        dh, dg = _mm_rms_bwd(mix_dx[0], mix_dx[1], mix_dx[2], sv['h_in'], norm_mix_g[i:i + 1], dh, tm=tm, tk=mix_dx[3])
        gl_['norm_mix_g'][i] = dg[0]
    grad_x = dh.reshape(x.shape)
    record(pending, _run_side("scatter_last", scatter_side(pending)))

    order = sorted(BIG, key=lambda n: -math.prod(w[n].shape))
    chip_sums, theirs, prev = {}, {}, None
    for n in order:
        cols = w[n].shape[-1]
        cs, got = _sum_parts([recvd[(n, l)].reshape(N_CHIPS, -1, cols) for l in range(w[n].shape[0])],
                             side=_sibling_side([chip_sums[prev]]) if prev else None)
        chip_sums[n] = cs.reshape(-1, cols)
        if prev:
            theirs[prev] = got[0]
        prev = n
    theirs[prev] = _run_side("swap_last", _sibling_side([chip_sums[prev]]))[0]
    results = {}
    for n in BIG:
        cols = w[n].shape[-1]
        (delta, new_m, new_v, grad), _ = _adamw(w[n].reshape(-1, cols), [chip_sums[n], theirs[n]], mom[n].reshape(-1, cols),
                                                vel[n].reshape(-1, cols))
        results[n] = [o.reshape(w[n].shape) for o in (grad, delta, new_m, new_v)]

    small = REPLICATED + SMALL_SHARDED
    local = [dg_final.reshape(d) if n == 'norm_final_g' else jnp.stack(gl_[n]) for n in small]
    summed = _unpack(_allreduce_small(_pack(local)), local)
    me = 2 * lax.axis_index("x") + lax.axis_index("y")
    for n, g in zip(small, summed):
        if n in SMALL_SHARDED:
            g = lax.dynamic_slice_in_dim(g, me * w[n].shape[-1], w[n].shape[-1], axis=g.ndim - 1)
        view = (-1, w[n].shape[-1])
        (delta, new_m, new_v), _ = _adamw(w[n].reshape(view), [g.reshape(view)], mom[n].reshape(view), vel[n].reshape(view))
        results[n] = [g] + [o.reshape(w[n].shape) for o in (delta, new_m, new_v)]

    return (loss, grad_x, *[results[n][0] for n in PARAM_NAMES], *[results[n][1] for n in PARAM_NAMES],
            *[results[n][2] for n in PARAM_NAMES], *[results[n][3] for n in PARAM_NAMES])
```

```python
import functools
import math

import jax
import jax.numpy as jnp
from jax import lax
from jax.experimental import pallas as pl
from jax.experimental.pallas import tpu as pltpu

F32 = jnp.float32
BF16 = jnp.bfloat16
MESH = pl.DeviceIdType.MESH

NORM_EPS = 1e-6
RG_HEADS = 8
RG_CONV_W = 4
RG_C = 8.0
S5_GC = 16
S5_P = 64
S5_GROUPS_PER_BLOCK = 8
FFN_CONV_W = 3
N_CHIPS = 4
ADAM_LR, ADAM_B1, ADAM_B2, ADAM_EPS, ADAM_WD, ADAM_STEP = 0.001, 0.9, 0.999, 1e-08, 0.01, 10
VMEM_LIMIT_BYTES = 56 * 1024 * 1024
SUBLANES = 8
LANES = 128

PARAM_NAMES = ['norm_mix_g', 'norm_ffn_g', 'norm_final_g', 'rg_w_in', 'rg_conv_w', 'rg_conv_b', 'rg_w_a', 'rg_b_a', 'rg_w_x',
               'rg_b_x', 'rg_lambda', 'rg_w_out', 's5_w_in', 's5_a_re', 's5_a_im', 's5_log_dt', 's5_b_re', 's5_b_im', 's5_c_re',
               's5_c_im', 's5_d', 's5_w_glu', 's5_w_out', 'ffn_w_up', 'ffn_conv_w', 'ffn_conv_b', 'ffn_w_down']
SHARDED = ['rg_w_in', 'rg_conv_w', 'rg_w_out', 's5_w_in', 's5_d', 's5_w_glu', 's5_w_out', 'ffn_w_up', 'ffn_conv_w', 'ffn_w_down']
BIG = ['rg_w_in', 'rg_w_out', 's5_w_in', 's5_w_glu', 's5_w_out', 'ffn_w_up', 'ffn_w_down']
ROW_SHARDED = ['rg_w_out', 's5_w_in', 's5_w_out', 'ffn_w_down']
SMALL_SHARDED = ['rg_conv_w', 's5_d', 'ffn_conv_w']
MIXER_SHARDED = [['rg_w_in', 'rg_conv_w', 'rg_w_out'], ['s5_w_in', 's5_d', 's5_w_glu', 's5_w_out']]
FFN_SHARDED = ['ffn_w_up', 'ffn_conv_w', 'ffn_w_down']
REPLICATED = [n for n in PARAM_NAMES if n not in SHARDED]


def _cparams():
    return pltpu.CompilerParams(vmem_limit_bytes=VMEM_LIMIT_BYTES)


_GELU_C = math.sqrt(2.0 / math.pi)
_GELU_K = 0.044715


def _gelu(x):
    return 0.5 * x * (1.0 + jnp.tanh(_GELU_C * (x + _GELU_K * x * x * x)))


def _gelu_and_grad(x):
    t = jnp.tanh(_GELU_C * (x + _GELU_K * x * x * x))
    g = 0.5 * x * (1.0 + t)
    dg = 0.5 * (1.0 + t) + 0.5 * x * (1.0 - t * t) * (_GELU_C * (1.0 + 3.0 * _GELU_K * x * x))
    return g, dg


def _sigmoid(x):
    return jax.nn.sigmoid(x)


def _neg_expm1(x):
    series = -(x * (1.0 + x * (0.5 + x * (1.0 / 6 + x * (1.0 / 24 + x * (1.0 / 120 + x * (1.0 / 720)))))))
    return jnp.where(x > -0.25, series, 1.0 - jnp.exp(x))


def _softplus(z):
    return jnp.maximum(z, 0.0) + jnp.log1p(jnp.exp(-jnp.abs(z)))


def _rows(shape):
    return lax.broadcasted_iota(jnp.int32, shape, 0)


def _shift_down(x, halo, k):
    ext = jnp.concatenate([halo, x], axis=0)
    return pltpu.roll(ext, k, 0)[SUBLANES:]


def _shift_up(x, halo, k):
    ext = jnp.concatenate([x, halo], axis=0)
    n = ext.shape[0]
    return pltpu.roll(ext, n - k, 0)[:x.shape[0]]


def _scan_real_fwd(a, b):
    n = a.shape[0]
    row = _rows(a.shape)
    sh = 1
    while sh < n:
        ok = row >= sh
        b = a * jnp.where(ok, pltpu.roll(b, sh, 0), 0.0) + b
        if sh * 2 < n:
            a = a * jnp.where(ok, pltpu.roll(a, sh, 0), 1.0)
        sh *= 2
    return b


def _scan_real_rev(c, d):
    n = c.shape[0]
    row = _rows(c.shape)
    sh = 1
    while sh < n:
        ok = row < n - sh
        d = c * jnp.where(ok, pltpu.roll(d, n - sh, 0), 0.0) + d
        if sh * 2 < n:
            c = c * jnp.where(ok, pltpu.roll(c, n - sh, 0), 1.0)
        sh *= 2
    return d


def _scan_cplx(br, bi, pr_ref, pi_ref, reverse):
    n = br.shape[0]
    row = _rows(br.shape)
    sh, k = 1, 0
    while sh < n:
        pr = pr_ref[k:k + 1, :]
        pi = pi_ref[k:k + 1, :]
        if reverse:
            ok = row < n - sh
            sr = jnp.where(ok, pltpu.roll(br, n - sh, 0), 0.0)
            si = jnp.where(ok, pltpu.roll(bi, n - sh, 0), 0.0)
        else:
            ok = row >= sh
            sr = jnp.where(ok, pltpu.roll(br, sh, 0), 0.0)
            si = jnp.where(ok, pltpu.roll(bi, sh, 0), 0.0)
        br, bi = br + pr * sr - pi * si, bi + pr * si + pi * sr
        sh *= 2
        k += 1
    return br, bi


RG_LANE_CHUNK = 512


def _real_slab_scan(a_ref, b_ref, out_ref, carry_ref, reverse):
    t, c = a_ref.shape
    nsl = t // SUBLANES
    lc = min(RG_LANE_CHUNK, c)
    row8 = _rows((SUBLANES, lc))
    for q in range(c // lc):
        sl = slice(q * lc, (q + 1) * lc)

        def slab(jj, carry, sl=sl):
            j = nsl - 1 - jj if reverse else jj
            r0 = pl.multiple_of(j * SUBLANES, SUBLANES)
            a, b = a_ref[pl.ds(r0, SUBLANES), sl], b_ref[pl.ds(r0, SUBLANES), sl]
            for k in range(3):
                sh = 1 << k
                keep = row8 < SUBLANES - sh if reverse else row8 >= sh
                amount = SUBLANES - sh if reverse else sh
                b = a * jnp.where(keep, pltpu.roll(b, amount, 0), 0.0) + b
                a = a * jnp.where(keep, pltpu.roll(a, amount, 0), 1.0)
            x = b + a * jnp.broadcast_to(carry, b.shape)
            out_ref[pl.ds(r0, SUBLANES), sl] = x
            return x[:1, :] if reverse else x[SUBLANES - 1:, :]

        carry_ref[:, sl] = lax.fori_loop(0, nsl, slab, carry_ref[:, sl], unroll=2)


class Mat:
    def __init__(self, arr, l=0, split='c'):
        assert arr.ndim == 4
        self.arr, self.l, self.split = arr, l, split
        p, _, r, c = arr.shape
        self.shape = (r, c * p) if split == 'c' else (r * p, c)

    def spec(self, tr, tc, rc):
        p, _, r, c = self.arr.shape
        l = self.l
        assert r % tr == 0 and c % tc == 0, (self.arr.shape, tr, tc)
        if self.split == 'c':
            per = c // tc
            return pl.BlockSpec((None, None, tr, tc), lambda i, j, k: (rc(i, j, k)[1] // per, l, rc(i, j, k)[0], rc(i, j, k)[1] % per))
        per = r // tr
        return pl.BlockSpec((None, None, tr, tc), lambda i, j, k: (rc(i, j, k)[0] // per, l, rc(i, j, k)[0] % per, rc(i, j, k)[1]))


def act(x, parts=1):
    s, c = x.shape
    return Mat(x.reshape(s, parts, c // parts).transpose(1, 0, 2)[:, None] if parts > 1 else x[None, None])


def _mm(name, mode, a, b, *, out_parts=1, out_split='c', out_dtype=F32, res=None, tm=512, tn=512, tk=512):
    if mode == 'nn':
        (m, kk), (kb, n) = a.shape, b.shape
    elif mode == 'nt':
        (m, kk), (n, kb) = a.shape, b.shape
    else:
        (kk, m), (kb, n) = a.shape, b.shape
    assert kk == kb, (name, a.shape, b.shape)
    tm, tn, tk = min(tm, m), min(tn, n), min(tk, kk)
    assert m % tm == 0 and n % tn == 0 and kk % tk == 0, (name, m, n, kk, tm, tn, tk)
    nk = kk // tk
    if mode == 'nn':
        a_spec = a.spec(tm, tk, lambda i, j, k: (i, k))
        b_spec = b.spec(tk, tn, lambda i, j, k: (k, j))
        dims = (((1,), (0,)), ((), ()))
    elif mode == 'nt':
        a_spec = a.spec(tm, tk, lambda i, j, k: (i, k))
        b_spec = b.spec(tn, tk, lambda i, j, k: (j, k))
        dims = (((1,), (1,)), ((), ()))
    else:
        a_spec = a.spec(tk, tm, lambda i, j, k: (k, i))
        b_spec = b.spec(tk, tn, lambda i, j, k: (k, j))
        dims = (((0,), (0,)), ((), ()))
    if out_split == 'c':
        out_arr = jax.ShapeDtypeStruct((out_parts, 1, m, n // out_parts), out_dtype)
    else:
        out_arr = jax.ShapeDtypeStruct((out_parts, 1, m // out_parts, n), out_dtype)
    out_mat = Mat(out_arr, 0, out_split)
    o_spec = out_mat.spec(tm, tn, lambda i, j, k: (i, j))
    has_res = res is not None

    def body(*refs):
        if has_res:
            a_ref, b_ref, r_ref, o_ref = refs[:4]
        else:
            a_ref, b_ref, o_ref = refs[:3]
        prod = lax.dot_general(a_ref[...].astype(BF16), b_ref[...].astype(BF16), dims, preferred_element_type=F32)

        def finish(acc):
            if has_res:
                acc = acc + r_ref[...]
            o_ref[...] = acc.astype(out_dtype)

        if nk == 1:
            finish(prod)
        else:
            acc_ref = refs[-1]
            k = pl.program_id(2)

            @pl.when(k == 0)
            def _():
                acc_ref[...] = prod

            @pl.when(k > 0)
            def _():
                acc_ref[...] += prod

            @pl.when(k == nk - 1)
            def _():
                finish(acc_ref[...])

    in_specs = [a_spec, b_spec]
    args = [a.arr, b.arr]
    if has_res:
        in_specs.append(res.spec(tm, tn, lambda i, j, k: (i, j)))
        args.append(res.arr)
    out = pl.pallas_call(
        body, name=name, grid=(m // tm, n // tn, nk), in_specs=in_specs, out_specs=o_spec, out_shape=out_arr,
        scratch_shapes=[pltpu.VMEM((tm, tn), F32)] if nk > 1 else [], compiler_params=_cparams(),
    )(*args)
    return Mat(out, 0, out_split)


def _rms_fwd(h, g, ts=512):
    s, d = h.shape
    ts = min(ts, s)

    def body(h_ref, g_ref, o_ref):
        x = h_ref[...]
        var = jnp.mean(x * x, axis=-1, keepdims=True)
        o_ref[...] = (x * lax.rsqrt(var + NORM_EPS) * g_ref[...]).astype(BF16)

    return pl.pallas_call(
        body, name="rms_fwd", grid=(s // ts,),
        in_specs=[pl.BlockSpec((ts, d), lambda i: (i, 0)), pl.BlockSpec((1, d), lambda i: (0, 0))],
        out_specs=pl.BlockSpec((ts, d), lambda i: (i, 0)), out_shape=jax.ShapeDtypeStruct((s, d), BF16),
        compiler_params=_cparams(),
    )(h, g)


def _rms_bwd(h, g, dhn, dh_in, ts=512):
    s, d = h.shape
    ts = min(ts, s)

    def body(h_ref, g_ref, dhn_ref, dhin_ref, dh_ref, dg_ref):
        i = pl.program_id(0)
        x = h_ref[...]
        rstd = lax.rsqrt(jnp.mean(x * x, axis=-1, keepdims=True) + NORM_EPS)
        xhat = x * rstd
        dhn_v = dhn_ref[...]
        dxh = dhn_v * g_ref[...]
        dh_ref[...] = dhin_ref[...] + rstd * (dxh - xhat * jnp.mean(dxh * xhat, axis=-1, keepdims=True))
        part = jnp.sum(dhn_v * xhat, axis=0, keepdims=True)

        @pl.when(i == 0)
        def _():
            dg_ref[...] = part

        @pl.when(i > 0)
        def _():
            dg_ref[...] += part

    row = pl.BlockSpec((ts, d), lambda i: (i, 0))
    vec = pl.BlockSpec((1, d), lambda i: (0, 0))
    return pl.pallas_call(
        body, name="rms_bwd", grid=(s // ts,), in_specs=[row, vec, row, row], out_specs=[row, vec],
        out_shape=[jax.ShapeDtypeStruct((s, d), F32), jax.ShapeDtypeStruct((1, d), F32)], compiler_params=_cparams(),
    )(h, g, dhn, dh_in)


def _loss_and_grad(h, g, tgt, ts=512):
    s, d = h.shape
    ts = min(ts, s)

    def body(h_ref, g_ref, t_ref, loss_ref, dh_ref, dg_ref):
        i = pl.program_id(0)
        x = h_ref[...]
        gv = g_ref[...]
        rstd = lax.rsqrt(jnp.mean(x * x, axis=-1, keepdims=True) + NORM_EPS)
        xhat = x * rstd
        err = xhat * gv - t_ref[...]
        dy = err * (1.0 / d)
        dxh = dy * gv
        dh_ref[...] = rstd * (dxh - xhat * jnp.mean(dxh * xhat, axis=-1, keepdims=True))
        part = jnp.sum(dy * xhat, axis=0, keepdims=True)
        lpart = jnp.broadcast_to(jnp.sum(jnp.sum(err * err, axis=0, keepdims=True), axis=1, keepdims=True) * (0.5 / d), (1, LANES))

        @pl.when(i == 0)
        def _():
            dg_ref[...] = part
            loss_ref[...] = lpart

        @pl.when(i > 0)
        def _():
            dg_ref[...] += part
            loss_ref[...] += lpart

    row = pl.BlockSpec((ts, d), lambda i: (i, 0))
    vec = pl.BlockSpec((1, d), lambda i: (0, 0))
    return pl.pallas_call(
        body, name="loss_and_grad", grid=(s // ts,), in_specs=[row, vec, row],
        out_specs=[pl.BlockSpec((1, LANES), lambda i: (0, 0)), row, vec],
        out_shape=[jax.ShapeDtypeStruct((1, LANES), F32), jax.ShapeDtypeStruct((s, d), F32), jax.ShapeDtypeStruct((1, d), F32)],
        compiler_params=_cparams(),
    )(h, g, tgt)


def _halo_before(ts, nrow8):
    return lambda i: jnp.maximum(i * (ts // SUBLANES) - 1, 0)


def _ffn_act(up2, conv_w2, conv_b2, ts=512, tn=512, side=None):
    _, s, f = up2.shape
    ts, tn = min(ts, s), min(tn, f)
    kw = FFN_CONV_W

    def body(up_ref, halo_ref, w_ref, b_ref, o_ref):
        i = pl.program_id(0)
        cs = []
        for h in range(2):
            x = up_ref[h]
            halo = jnp.where(i == 0, 0.0, halo_ref[h])
            c = b_ref[h] + w_ref[h, kw - 1:kw, :] * x
            for sft in range(1, kw):
                c = c + w_ref[h, kw - 1 - sft:kw - sft, :] * _shift_down(x, halo, sft)
            cs.append(c)
        o_ref[...] = (_gelu(cs[0]) * cs[1]).astype(BF16)

    hb = ts // SUBLANES
    g0, g1 = s // ts, f // tn
    outs, side_outs = _call_with_side(
        body, side, lambda: (pl.program_id(0) == 0) & (pl.program_id(1) == 0),
        lambda: (pl.program_id(0) == g0 - 1) & (pl.program_id(1) == g1 - 1),
        name="ffn_act", grid=(g0, g1),
        in_specs=[pl.BlockSpec((2, ts, tn), lambda i, j: (0, i, j)),
                  pl.BlockSpec((2, SUBLANES, tn), lambda i, j: (0, jnp.maximum(i * hb - 1, 0), j)),
                  pl.BlockSpec((2, kw, tn), lambda i, j: (0, 0, j)),
                  pl.BlockSpec((2, 1, tn), lambda i, j: (0, 0, j))],
        out_specs=[pl.BlockSpec((ts, tn), lambda i, j: (i, j))], out_shape=[jax.ShapeDtypeStruct((s, f), BF16)],
        scratch_shapes=[], args=(up2, up2, conv_w2, conv_b2))
    return outs[0], side_outs


def _ffn_bwd(up2, dact, conv_w2, conv_b2, ts=256, tn=512, side=None):
    _, s, f = up2.shape
    ts, tn = min(ts, s), min(tn, f)
    kw = FFN_CONV_W
    nt = s // ts
    hb = ts // SUBLANES
    last8 = s // SUBLANES - 1

    def body(up_ref, hb_ref, ha_ref, da_ref, dah_ref, w_ref, b_ref, dup_ref, dw_ref, db_ref):
        i = pl.program_id(1)
        first, last = i == 0, i == nt - 1
        ce, xs = [], []
        for h in range(2):
            x = up_ref[h]
            before = jnp.where(first, 0.0, hb_ref[h])
            after = ha_ref[h]
            ext = jnp.concatenate([before, x, after], axis=0)
            c = b_ref[h] + w_ref[h, kw - 1:kw, :] * ext
            shifted = [ext]
            for sft in range(1, kw):
                sh = pltpu.roll(ext, sft, 0)
                shifted.append(sh)
                c = c + w_ref[h, kw - 1 - sft:kw - sft, :] * sh
            ce.append(c[SUBLANES:])
            xs.append([sh[SUBLANES:SUBLANES + ts] for sh in shifted])
        da = jnp.concatenate([da_ref[...], jnp.where(last, 0.0, dah_ref[...])], axis=0)
        g1, dg1 = _gelu_and_grad(ce[0])
        dcs = [da * ce[1] * dg1, da * g1]
        for h in range(2):
            dc = dcs[h]
            n = dc.shape[0]
            dup = w_ref[h, kw - 1:kw, :] * dc[:ts]
            for sft in range(1, kw):
                dup = dup + w_ref[h, kw - 1 - sft:kw - sft, :] * pltpu.roll(dc, n - sft, 0)[:ts]
            dup_ref[h] = dup.astype(BF16)
            dct = dc[:ts]
            dbp = jnp.sum(dct, axis=0, keepdims=True)
            dwp = [jnp.sum(dct * xs[h][kw - 1 - k], axis=0, keepdims=True) for k in range(kw)]

            @pl.when(first)
            def _():
                db_ref[h] = dbp
                for k in range(kw):
                    dw_ref[h, k:k + 1, :] = dwp[k]

            @pl.when(i > 0)
            def _():
                db_ref[h] += dbp
                for k in range(kw):
                    dw_ref[h, k:k + 1, :] += dwp[k]

    g0 = f // tn
    return _call_with_side(
        body, side, lambda: (pl.program_id(0) == 0) & (pl.program_id(1) == 0),
        lambda: (pl.program_id(0) == g0 - 1) & (pl.program_id(1) == nt - 1),
        name="ffn_bwd", grid=(g0, nt),
        in_specs=[pl.BlockSpec((2, ts, tn), lambda j, i: (0, i, j)),
                  pl.BlockSpec((2, SUBLANES, tn), lambda j, i: (0, jnp.maximum(i * hb - 1, 0), j)),
                  pl.BlockSpec((2, SUBLANES, tn), lambda j, i: (0, jnp.minimum((i + 1) * hb, last8), j)),
                  pl.BlockSpec((ts, tn), lambda j, i: (i, j)),
                  pl.BlockSpec((SUBLANES, tn), lambda j, i: (jnp.minimum((i + 1) * hb, last8), j)),
                  pl.BlockSpec((2, kw, tn), lambda j, i: (0, 0, j)),
                  pl.BlockSpec((2, 1, tn), lambda j, i: (0, 0, j))],
        out_specs=[pl.BlockSpec((2, ts, tn), lambda j, i: (0, i, j)),
                   pl.BlockSpec((2, kw, tn), lambda j, i: (0, 0, j)),
                   pl.BlockSpec((2, 1, tn), lambda j, i: (0, 0, j))],
        out_shape=[jax.ShapeDtypeStruct((2, s, f), BF16), jax.ShapeDtypeStruct((2, kw, f), F32),
                   jax.ShapeDtypeStruct((2, 1, f), F32)],
        scratch_shapes=[], args=(up2, up2, up2, dact, dact, conv_w2, conv_b2))


def _mm_rms_bwd(name, a, b, h, g, dh_in, *, tm, tk):
    (m, kk), (n, kb) = a.shape, b.shape
    assert kk == kb and h.shape == (m, n), (name, a.shape, b.shape, h.shape)
    tm, tk = min(tm, m), min(tk, kk)
    nk = kk // tk
    dims = (((1,), (1,)), ((), ()))

    def body(a_ref, b_ref, h_ref, g_ref, dhin_ref, dh_ref, dg_ref, *acc):
        i, k = pl.program_id(0), pl.program_id(2)
        prod = lax.dot_general(a_ref[...].astype(BF16), b_ref[...].astype(BF16), dims, preferred_element_type=F32)

        def finish(dhn):
            x = h_ref[...]
            rstd = lax.rsqrt(jnp.mean(x * x, axis=-1, keepdims=True) + NORM_EPS)
            xhat = x * rstd
            dxh = dhn * g_ref[...]
            dh_ref[...] = dhin_ref[...] + rstd * (dxh - xhat * jnp.mean(dxh * xhat, axis=-1, keepdims=True))
            part = jnp.sum(dhn * xhat, axis=0, keepdims=True)

            @pl.when(i == 0)
            def _():
                dg_ref[...] = part

            @pl.when(i > 0)
            def _():
                dg_ref[...] += part

        if nk == 1:
            finish(prod)
        else:
            acc_ref = acc[0]

            @pl.when(k == 0)
            def _():
                acc_ref[...] = prod

            @pl.when(k > 0)
            def _():
                acc_ref[...] += prod

            @pl.when(k == nk - 1)
            def _():
                finish(acc_ref[...])

    row = pl.BlockSpec((tm, n), lambda i, j, k: (i, 0))
    vec = pl.BlockSpec((1, n), lambda i, j, k: (0, 0))
    return pl.pallas_call(
        body, name=name, grid=(m // tm, 1, nk),
        in_specs=[a.spec(tm, tk, lambda i, j, k: (i, k)), b.spec(n, tk, lambda i, j, k: (0, k)), row, vec, row],
        out_specs=[row, vec], out_shape=[jax.ShapeDtypeStruct((m, n), F32), jax.ShapeDtypeStruct((1, n), F32)],
        scratch_shapes=[pltpu.VMEM((tm, n), F32)] if nk > 1 else [], compiler_params=_cparams(),
    )(a.arr, b.arr, h, g, dh_in)


def _ffn_up_act(hn2, w_up4, conv_w2, conv_b2, ts=1024, tn=512, sub=1024, side=None):
    s, d = hn2.shape
    p, _, wc = w_up4.shape
    f = p * wc // 2
    ts, tn = min(ts, s), min(tn, wc)
    sub = min(sub, ts)
    per = wc // tn
    kw = FFN_CONV_W
    g0, g1 = f // tn, s // ts

    def body(hn_ref, w1_ref, w2_ref, cw_ref, cb_ref, up_ref, c_ref, act_ref, carry_ref):
        @pl.when(pl.program_id(1) == 0)
        def _():
            carry_ref[...] = jnp.zeros_like(carry_ref)

        for q in range(ts // sub):
            rows = slice(q * sub, (q + 1) * sub)
            hn = hn_ref[rows, :]
            cs = []
            for h, w_ref in enumerate((w1_ref, w2_ref)):
                x = jnp.dot(hn, w_ref[...], preferred_element_type=F32)
                up_ref[h, rows, :] = x
                halo = carry_ref[h]
                c = cb_ref[h] + cw_ref[h, kw - 1:kw, :] * x
                for sft in range(1, kw):
                    c = c + cw_ref[h, kw - 1 - sft:kw - sft, :] * _shift_down(x, halo, sft)
                carry_ref[h] = x[sub - SUBLANES:, :]
                c_ref[h, rows, :] = c
                cs.append(c)
            act_ref[rows, :] = (_gelu(cs[0]) * cs[1]).astype(BF16)

    outs, side_outs = _call_with_side(
        body, side, lambda: (pl.program_id(0) == 0) & (pl.program_id(1) == 0),
        lambda: (pl.program_id(0) == g0 - 1) & (pl.program_id(1) == g1 - 1),
        name="ffn_up_act", grid=(g0, g1),
        in_specs=[pl.BlockSpec((ts, d), lambda j, i: (i, 0)),
                  pl.BlockSpec((None, d, tn), lambda j, i: (j // per, 0, j % per)),
                  pl.BlockSpec((None, d, tn), lambda j, i: (p // 2 + j // per, 0, j % per)),
                  pl.BlockSpec((2, kw, tn), lambda j, i: (0, 0, j)),
                  pl.BlockSpec((2, 1, tn), lambda j, i: (0, 0, j))],
        out_specs=[pl.BlockSpec((2, ts, tn), lambda j, i: (0, i, j)), pl.BlockSpec((2, ts, tn), lambda j, i: (0, i, j)),
                   pl.BlockSpec((ts, tn), lambda j, i: (i, j))],
        out_shape=[jax.ShapeDtypeStruct((2, s, f), F32), jax.ShapeDtypeStruct((2, s, f), F32), jax.ShapeDtypeStruct((s, f), BF16)],
        scratch_shapes=[pltpu.VMEM((2, SUBLANES, tn), F32)], args=(hn2, w_up4, w_up4, conv_w2, conv_b2))
    return outs, side_outs


def _ffn_bwd_fused(dh, w_down, up2, c2, conv_w2, ts=1024, tn=512, side=None):
    s, d = dh.shape
    _, _, f = up2.shape
    ts, tn = min(ts, s), min(tn, f)
    kw = FFN_CONV_W
    nt = s // ts
    hb = ts // SUBLANES
    g0 = f // tn
    nt_dims = (((1,), (1,)), ((), ()))

    def body(dh_ref, wd_ref, up_ref, c_ref, w_ref, dup_ref, dw_ref, db_ref, carry_ref):
        i = pl.program_id(1)
        first_step = i == 0

        @pl.when(first_step)
        def _():
            carry_ref[...] = jnp.zeros_like(carry_ref)

        da = lax.dot_general(dh_ref[...].astype(BF16), wd_ref[...], nt_dims, preferred_element_type=F32)
        g1, dg1 = _gelu_and_grad(c_ref[0])
        dcs = [da * c_ref[1] * dg1, da * g1]
        for h in range(2):
            dc = dcs[h]
            after = carry_ref[h]
            ups = [dc] + [_shift_up(dc, after, sft) for sft in range(1, kw)]
            dup = w_ref[h, kw - 1:kw, :] * dc
            for sft in range(1, kw):
                dup = dup + w_ref[h, kw - 1 - sft:kw - sft, :] * ups[sft]
            carry_ref[h] = dc[:SUBLANES]
            dup_ref[h] = dup.astype(BF16)
            dbp = jnp.sum(dc, axis=0, keepdims=True)
            x = up_ref[h]
            dwp = [jnp.sum(ups[kw - 1 - k] * x, axis=0, keepdims=True) for k in range(kw)]

            @pl.when(first_step)
            def _():
                db_ref[h] = dbp
                for k in range(kw):
                    dw_ref[h, k:k + 1, :] = dwp[k]

            @pl.when(i > 0)
            def _():
                db_ref[h] += dbp
                for k in range(kw):
                    dw_ref[h, k:k + 1, :] += dwp[k]

    rev = lambda i: nt - 1 - i
    return _call_with_side(
        body, side, lambda: (pl.program_id(0) == 0) & (pl.program_id(1) == 0),
        lambda: (pl.program_id(0) == g0 - 1) & (pl.program_id(1) == nt - 1),
        name="ffn_bwd", grid=(g0, nt),
        in_specs=[pl.BlockSpec((ts, d), lambda j, i: (rev(i), 0)),
                  pl.BlockSpec((tn, d), lambda j, i: (j, 0)),
                  pl.BlockSpec((2, ts, tn), lambda j, i: (0, rev(i), j)),
                  pl.BlockSpec((2, ts, tn), lambda j, i: (0, rev(i), j)),
                  pl.BlockSpec((2, kw, tn), lambda j, i: (0, 0, j))],
        out_specs=[pl.BlockSpec((2, ts, tn), lambda j, i: (0, rev(i), j)),
                   pl.BlockSpec((2, kw, tn), lambda j, i: (0, 0, j)),
                   pl.BlockSpec((2, 1, tn), lambda j, i: (0, 0, j))],
        out_shape=[jax.ShapeDtypeStruct((2, s, f), BF16), jax.ShapeDtypeStruct((2, kw, f), F32),
                   jax.ShapeDtypeStruct((2, 1, f), F32)],
        scratch_shapes=[pltpu.VMEM((2, SUBLANES, tn), F32)], args=(dh, w_down, up2, c2, conv_w2))


def _ffn_fwd(h, g, w_up4, w_down, conv_w2, conv_b2, ts=512, tn=512, sub=256, side=None):
    s, d = h.shape
    p, _, wc = w_up4.shape
    f = p * wc // 2
    ts, tn = min(ts, s), min(tn, wc)
    sub = min(sub, ts)
    per = wc // tn
    kw = FFN_CONV_W
    g0, g1 = s // ts, f // tn

    def body(h_ref, g_ref, w1_ref, w2_ref, wd_ref, cw_ref, cb_ref, ho_ref, hn_ref, up_ref, c_ref, act_ref, carry_ref):
        i, j = pl.program_id(0), pl.program_id(1)

        @pl.when(j == 0)
        def _():
            x = h_ref[...]
            var = jnp.mean(x * x, axis=-1, keepdims=True)
            hn_ref[...] = (x * lax.rsqrt(var + NORM_EPS) * g_ref[...]).astype(BF16)
            ho_ref[...] = x

        @pl.when(i == 0)
        def _():
            carry_ref[j] = jnp.zeros(carry_ref.shape[1:], F32)

        for q in range(ts // sub):
            rows = slice(q * sub, (q + 1) * sub)
            hn = hn_ref[rows, :]
            cs = []
            for hf, w_ref in enumerate((w1_ref, w2_ref)):
                x = jnp.dot(hn, w_ref[...], preferred_element_type=F32)
                up_ref[hf, rows, :] = x
                halo = carry_ref[j, hf]
                c = cb_ref[hf] + cw_ref[hf, kw - 1:kw, :] * x
                for sft in range(1, kw):
                    c = c + cw_ref[hf, kw - 1 - sft:kw - sft, :] * _shift_down(x, halo, sft)
                carry_ref[j, hf] = x[sub - SUBLANES:, :]
                c_ref[hf, rows, :] = c
                cs.append(c)
            a = (_gelu(cs[0]) * cs[1]).astype(BF16)
            act_ref[rows, :] = a
            ho_ref[rows, :] += jnp.dot(a, wd_ref[...], preferred_element_type=F32)

    row = pl.BlockSpec((ts, d), lambda i, j: (i, 0))
    col2 = pl.BlockSpec((2, ts, tn), lambda i, j: (0, i, j))
    return _call_with_side(
        body, side, lambda: (pl.program_id(0) == 0) & (pl.program_id(1) == 0),
        lambda: (pl.program_id(0) == g0 - 1) & (pl.program_id(1) == g1 - 1),
        name="ffn_fwd", grid=(g0, g1),
        in_specs=[row, pl.BlockSpec((1, d), lambda i, j: (0, 0)),
                  pl.BlockSpec((None, d, tn), lambda i, j: (j // per, 0, j % per)),
                  pl.BlockSpec((None, d, tn), lambda i, j: (p // 2 + j // per, 0, j % per)),
                  pl.BlockSpec((tn, d), lambda i, j: (j, 0)),
                  pl.BlockSpec((2, kw, tn), lambda i, j: (0, 0, j)),
                  pl.BlockSpec((2, 1, tn), lambda i, j: (0, 0, j))],
        out_specs=[row, row, col2, col2, pl.BlockSpec((ts, tn), lambda i, j: (i, j))],
        out_shape=[jax.ShapeDtypeStruct((s, d), F32), jax.ShapeDtypeStruct((s, d), BF16), jax.ShapeDtypeStruct((2, s, f), F32),
                   jax.ShapeDtypeStruct((2, s, f), F32), jax.ShapeDtypeStruct((s, f), BF16)],
        scratch_shapes=[pltpu.VMEM((g1, 2, SUBLANES, tn), F32)],
        args=(h, g, w_up4, w_up4, w_down, conv_w2, conv_b2))


def _ffn_bwd_all(dh, w_down, up2, c2, hn, act_, conv_w2, ts=256, tn=512, sub=128, side=None):
    s, d = dh.shape
    _, _, f = up2.shape
    ts, tn = min(ts, s), min(tn, f)
    sub = min(sub, ts)
    kw = FFN_CONV_W
    nt = s // ts
    g0 = f // tn
    nt_dims = (((1,), (1,)), ((), ()))
    tn_dims = (((0,), (0,)), ((), ()))

    def body(dh_ref, wd_ref, up_ref, c_ref, hn_ref, act_ref, w_ref, dup_ref, dw_ref, db_ref, dwu_ref, dwd_ref,
             carry_ref, dwu_acc, dwd_acc):
        i = pl.program_id(1)
        first_step = i == 0

        @pl.when(first_step)
        def _():
            carry_ref[...] = jnp.zeros_like(carry_ref)
            dwu_acc[...] = jnp.zeros_like(dwu_acc)
            dwd_acc[...] = jnp.zeros_like(dwd_acc)
            dw_ref[...] = jnp.zeros_like(dw_ref)
            db_ref[...] = jnp.zeros_like(db_ref)

        dhb = dh_ref[...].astype(BF16)
        da_all = lax.dot_general(dhb, wd_ref[...], nt_dims, preferred_element_type=F32)
        for q in reversed(range(ts // sub)):
            rows = slice(q * sub, (q + 1) * sub)
            da = da_all[rows, :]
            g1, dg1 = _gelu_and_grad(c_ref[0, rows, :])
            dcs = [da * c_ref[1, rows, :] * dg1, da * g1]
            hnq = hn_ref[rows, :]
            for hf in range(2):
                dc = dcs[hf]
                after = carry_ref[hf]
                ups = [dc] + [_shift_up(dc, after, sft) for sft in range(1, kw)]
                dup = w_ref[hf, kw - 1:kw, :] * dc
                for sft in range(1, kw):
                    dup = dup + w_ref[hf, kw - 1 - sft:kw - sft, :] * ups[sft]
                carry_ref[hf] = dc[:SUBLANES]
                dupb = dup.astype(BF16)
                dup_ref[hf, rows, :] = dupb
                dwu_acc[hf] += lax.dot_general(hnq, dupb, tn_dims, preferred_element_type=F32)
                db_ref[hf] += jnp.sum(dc, axis=0, keepdims=True)
                x = up_ref[hf, rows, :]
                for k in range(kw):
                    dw_ref[hf, k:k + 1, :] += jnp.sum(ups[kw - 1 - k] * x, axis=0, keepdims=True)
            dwd_acc[...] += lax.dot_general(act_ref[rows, :], dhb[rows, :], tn_dims, preferred_element_type=F32)

        @pl.when(i == nt - 1)
        def _():
            dwu_ref[...] = dwu_acc[...].astype(BF16)
            dwd_ref[...] = dwd_acc[...].astype(BF16)

    rev = lambda i: nt - 1 - i
    col2 = pl.BlockSpec((2, ts, tn), lambda j, i: (0, rev(i), j))
    return _call_with_side(
        body, side, lambda: (pl.program_id(0) == 0) & (pl.program_id(1) == 0),
        lambda: (pl.program_id(0) == g0 - 1) & (pl.program_id(1) == nt - 1),
        name="ffn_bwd", grid=(g0, nt),
        in_specs=[pl.BlockSpec((ts, d), lambda j, i: (rev(i), 0)),
                  pl.BlockSpec((tn, d), lambda j, i: (j, 0)),
                  col2, col2,
                  pl.BlockSpec((ts, d), lambda j, i: (rev(i), 0)),
                  pl.BlockSpec((ts, tn), lambda j, i: (rev(i), j)),
                  pl.BlockSpec((2, kw, tn), lambda j, i: (0, 0, j))],
        out_specs=[col2,
                   pl.BlockSpec((2, kw, tn), lambda j, i: (0, 0, j)),
                   pl.BlockSpec((2, 1, tn), lambda j, i: (0, 0, j)),
                   pl.BlockSpec((2, d, tn), lambda j, i: (0, 0, j)),
                   pl.BlockSpec((tn, d), lambda j, i: (j, 0))],
        out_shape=[jax.ShapeDtypeStruct((2, s, f), BF16), jax.ShapeDtypeStruct((2, kw, f), F32),
                   jax.ShapeDtypeStruct((2, 1, f), F32), jax.ShapeDtypeStruct((2, d, f), BF16), jax.ShapeDtypeStruct((f, d), BF16)],
        scratch_shapes=[pltpu.VMEM((2, SUBLANES, tn), F32), pltpu.VMEM((2, d, tn), F32), pltpu.VMEM((tn, d), F32)],
        args=(dh, w_down, up2, c2, hn, act_, conv_w2))


def _rg_gates(xr, wa_ref, ba_ref, wx_ref, bx_ref, lam_ref):
    bw = wa_ref.shape[-1]
    xb = xr.astype(BF16)
    za = jnp.concatenate([jnp.dot(xb[:, h * bw:(h + 1) * bw], wa_ref[h], preferred_element_type=F32)
                          for h in range(RG_HEADS)], axis=1) + ba_ref[...]
    zx = jnp.concatenate([jnp.dot(xb[:, h * bw:(h + 1) * bw], wx_ref[h], preferred_element_type=F32)
                          for h in range(RG_HEADS)], axis=1) + bx_ref[...]
    r, ig = _sigmoid(za), _sigmoid(zx)
    sp = _softplus(-lam_ref[...])
    la = -RG_C * r * sp
    a = jnp.exp(la)
    mult = jnp.sqrt(_neg_expm1(2.0 * la))
    return xb, r, ig, sp, a, mult


def _rg_fwd(xg2, conv_w, conv_b, w_a, b_a, w_x, b_x, lam, ts=256, side=None):
    _, s, c = xg2.shape
    ts = min(ts, s)
    kw = RG_CONV_W
    hb = ts // SUBLANES

    def body(xg_ref, halo_ref, cw_ref, cb_ref, wa_ref, ba_ref, wx_ref, bx_ref, lam_ref, xr_ref, hs_ref, y_ref, carry_ref,
             a_scr, b_scr):
        i = pl.program_id(0)

        @pl.when(i == 0)
        def _():
            carry_ref[...] = jnp.zeros_like(carry_ref)

        xp = xg_ref[0]
        halo = jnp.where(i == 0, 0.0, halo_ref[...])
        xr = cb_ref[...] + cw_ref[kw - 1:kw, :] * xp
        for sft in range(1, kw):
            xr = xr + cw_ref[kw - 1 - sft:kw - sft, :] * _shift_down(xp, halo, sft)
        _, r, ig, sp, a, mult = _rg_gates(xr, wa_ref, ba_ref, wx_ref, bx_ref, lam_ref)
        a_scr[...] = a
        b_scr[...] = mult * (ig * xr)
        _real_slab_scan(a_scr, b_scr, hs_ref, carry_ref, reverse=False)
        xr_ref[...] = xr
        y_ref[...] = (hs_ref[...] * _gelu(xg_ref[1])).astype(BF16)

    full = lambda shape: pl.BlockSpec(shape, lambda i: (0,) * len(shape))
    row_spec = pl.BlockSpec((ts, c), lambda i: (i, 0))
    nt = s // ts
    return _call_with_side(
        body, side, lambda: pl.program_id(0) == 0, lambda: pl.program_id(0) == nt - 1,
        name="rg_fwd", grid=(nt,),
        in_specs=[pl.BlockSpec((2, ts, c), lambda i: (0, i, 0)),
                  pl.BlockSpec((None, SUBLANES, c), lambda i: (0, jnp.maximum(i * hb - 1, 0), 0)),
                  full(conv_w.shape), full(conv_b.shape), full(w_a.shape), full(b_a.shape), full(w_x.shape), full(b_x.shape),
                  full(lam.shape)],
        out_specs=[row_spec, row_spec, row_spec],
        out_shape=[jax.ShapeDtypeStruct((s, c), F32), jax.ShapeDtypeStruct((s, c), F32), jax.ShapeDtypeStruct((s, c), BF16)],
        scratch_shapes=[pltpu.VMEM((1, c), F32), pltpu.VMEM((ts, c), F32), pltpu.VMEM((ts, c), F32)],
        args=(xg2, xg2, conv_w, conv_b, w_a, b_a, w_x, b_x, lam))


def _rg_bwd(dy, xg2, xr, hs, conv_w, w_a, b_a, w_x, b_x, lam, ts=256, side=None):
    _, s, c = xg2.shape
    ts = min(ts, s)
    nt = s // ts
    kw = RG_CONV_W
    hb = ts // SUBLANES
    bw = c // RG_HEADS
    tn_dims = (((0,), (0,)), ((), ()))
    nt_dims = (((1,), (1,)), ((), ()))

    def body(dy_ref, xg_ref, xph_ref, xr_ref, hs_ref, hsh_ref, cw_ref, wa_ref, ba_ref, wx_ref, bx_ref, lam_ref,
             dxg_ref, dcw_ref, dcb_ref, dwa_ref, dba_ref, dwx_ref, dbx_ref, dlam_ref,
             lam_carry, a_carry, dxr_carry, dsp_acc, a_scr, b_scr):
        i = pl.program_id(0)
        first_step = i == 0
        time_first = i == nt - 1

        @pl.when(first_step)
        def _():
            lam_carry[...] = jnp.zeros_like(lam_carry)
            a_carry[...] = jnp.ones_like(a_carry)
            dxr_carry[...] = jnp.zeros_like(dxr_carry)
            dsp_acc[...] = jnp.zeros_like(dsp_acc)
            for ref in (dcw_ref, dcb_ref, dwa_ref, dba_ref, dwx_ref, dbx_ref):
                ref[...] = jnp.zeros_like(ref)

        xr = xr_ref[...]
        hs = hs_ref[...]
        gate = xg_ref[1]
        xb, r, ig, sp, a, mult = _rg_gates(xr, wa_ref, ba_ref, wx_ref, bx_ref, lam_ref)
        dyv = dy_ref[...]
        gg, dgg = _gelu_and_grad(gate)
        dhs = dyv * gg
        dxg_ref[1] = (dyv * hs * dgg).astype(BF16)
        row = _rows(xr.shape)
        a_scr[...] = jnp.where(row == ts - 1, a_carry[0:1, :], pltpu.roll(a, ts - 1, 0))
        b_scr[...] = dhs
        _real_slab_scan(a_scr, b_scr, b_scr, lam_carry, reverse=True)
        lmb = b_scr[...]
        a_carry[...] = a[:SUBLANES]
        hs_prev = _shift_down(hs, jnp.where(time_first, 0.0, hsh_ref[...]), 1)
        d_a = lmb * hs_prev
        d_m = lmb * (ig * xr)
        d_ig = lmb * mult * xr
        d_xr = lmb * mult * ig
        d_la = a * d_a - (a * a / mult) * d_m
        dsp_acc[...] += jnp.sum(-RG_C * r * d_la, axis=0, keepdims=True)
        d_za = (-RG_C * sp) * d_la * r * (1.0 - r)
        d_zx = d_ig * ig * (1.0 - ig)
        dba_ref[...] += jnp.sum(d_za, axis=0, keepdims=True)
        dbx_ref[...] += jnp.sum(d_zx, axis=0, keepdims=True)
        dzab, dzxb = d_za.astype(BF16), d_zx.astype(BF16)
        back = []
        for h in range(RG_HEADS):
            sl = slice(h * bw, (h + 1) * bw)
            dwa_ref[h] += lax.dot_general(xb[:, sl], dzab[:, sl], tn_dims, preferred_element_type=F32)
            dwx_ref[h] += lax.dot_general(xb[:, sl], dzxb[:, sl], tn_dims, preferred_element_type=F32)
            back.append(lax.dot_general(dzab[:, sl], wa_ref[h], nt_dims, preferred_element_type=F32)
                        + lax.dot_general(dzxb[:, sl], wx_ref[h], nt_dims, preferred_element_type=F32))
        d_xr = d_xr + jnp.concatenate(back, axis=1)
        d_xp = cw_ref[kw - 1:kw, :] * d_xr
        after = dxr_carry[...]
        for sft in range(1, kw):
            d_xp = d_xp + cw_ref[kw - 1 - sft:kw - sft, :] * _shift_up(d_xr, after, sft)
        dxr_carry[...] = d_xr[:SUBLANES]
        dxg_ref[0] = d_xp.astype(BF16)
        xp = xg_ref[0]
        before = jnp.where(time_first, 0.0, xph_ref[...])
        dcb_ref[...] += jnp.sum(d_xr, axis=0, keepdims=True)
        dcw_ref[kw - 1:kw, :] += jnp.sum(d_xr * xp, axis=0, keepdims=True)
        for sft in range(1, kw):
            dcw_ref[kw - 1 - sft:kw - sft, :] += jnp.sum(d_xr * _shift_down(xp, before, sft), axis=0, keepdims=True)
        dlam_ref[...] = dsp_acc[...] * (-_sigmoid(-lam_ref[...]))

    full = lambda shape: pl.BlockSpec(shape, lambda i: (0,) * len(shape))
    rev = lambda i: nt - 1 - i
    row_spec = pl.BlockSpec((ts, c), lambda i: (rev(i), 0))
    halo_idx = lambda i: jnp.maximum(rev(i) * hb - 1, 0)
    vec = (1, c)
    return _call_with_side(
        body, side, lambda: pl.program_id(0) == 0, lambda: pl.program_id(0) == nt - 1,
        name="rg_bwd", grid=(nt,),
        in_specs=[row_spec,
                  pl.BlockSpec((2, ts, c), lambda i: (0, rev(i), 0)),
                  pl.BlockSpec((None, SUBLANES, c), lambda i: (0, halo_idx(i), 0)),
                  row_spec, row_spec,
                  pl.BlockSpec((SUBLANES, c), lambda i: (halo_idx(i), 0)),
                  full(conv_w.shape), full(w_a.shape), full(b_a.shape), full(w_x.shape), full(b_x.shape), full(lam.shape)],
        out_specs=[pl.BlockSpec((2, ts, c), lambda i: (0, rev(i), 0)), full(conv_w.shape), full(vec), full(w_a.shape), full(vec),
                   full(w_x.shape), full(vec), full(vec)],
        out_shape=[jax.ShapeDtypeStruct((2, s, c), BF16), jax.ShapeDtypeStruct(conv_w.shape, F32), jax.ShapeDtypeStruct(vec, F32),
                   jax.ShapeDtypeStruct(w_a.shape, F32), jax.ShapeDtypeStruct(vec, F32), jax.ShapeDtypeStruct(w_x.shape, F32),
                   jax.ShapeDtypeStruct(vec, F32), jax.ShapeDtypeStruct(vec, F32)],
        scratch_shapes=[pltpu.VMEM(vec, F32), pltpu.VMEM((SUBLANES, c), F32), pltpu.VMEM((SUBLANES, c), F32),
                        pltpu.VMEM(vec, F32), pltpu.VMEM((ts, c), F32), pltpu.VMEM((ts, c), F32)],
        args=(dy, xg2, xg2, xr, hs, hs, conv_w, w_a, b_a, w_x, b_x, lam))


def _s5_param_fn(a_re, a_im, log_dt, bt_re, bt_im):
    dt = jnp.exp(log_dt)
    mag = jnp.exp(a_re * dt)
    abr = mag * jnp.cos(a_im * dt)
    abi = mag * jnp.sin(a_im * dt)
    ur, ui = abr - 1.0, abi
    den = a_re * a_re + a_im * a_im
    wr = (ur * a_re + ui * a_im) / den
    wi = (ui * a_re - ur * a_im) / den
    bbr = wr[None] * bt_re - wi[None] * bt_im
    bbi = wr[None] * bt_im + wi[None] * bt_re
    return abr, abi, bbr, bbi


def _s5_params(a_re, a_im, log_dt, bt_re, bt_im, nlev):
    g, p = a_re.shape
    gc = bt_re.shape[0]

    def body(ar_ref, ai_ref, dt_ref, br_ref, bi_ref, abr_ref, abi_ref, pr_ref, pi_ref, bbr_ref, bbi_ref):
        abr, abi, bbr, bbi = _s5_param_fn(ar_ref[...], ai_ref[...], dt_ref[...], br_ref[...], bi_ref[...])
        abr_ref[...] = abr
        abi_ref[...] = abi
        bbr_ref[...] = bbr
        bbi_ref[...] = bbi
        qr, qi = abr, abi
        for k in range(nlev):
            pr_ref[k] = qr
            pi_ref[k] = qi
            qr, qi = qr * qr - qi * qi, 2.0 * qr * qi

    sd = jax.ShapeDtypeStruct
    return pl.pallas_call(
        body, name="s5_params",
        out_shape=[sd((g, p), F32), sd((g, p), F32), sd((nlev, g, p), F32), sd((nlev, g, p), F32), sd((gc, g, p), F32),
                   sd((gc, g, p), F32)],
    )(a_re, a_im, log_dt, bt_re, bt_im)


def _s5_params_bwd(a_re, a_im, log_dt, bt_re, bt_im, d_abr, d_abi, d_bbr, d_bbi):
    def body(ar_ref, ai_ref, dt_ref, br_ref, bi_ref, g0, g1, g2, g3, o0, o1, o2, o3, o4):
        _, vjp = jax.vjp(_s5_param_fn, ar_ref[...], ai_ref[...], dt_ref[...], br_ref[...], bi_ref[...])
        outs = vjp((g0[...], g1[...], g2[...], g3[...]))
        for o, v in zip((o0, o1, o2, o3, o4), outs):
            o[...] = v

    sd = jax.ShapeDtypeStruct
    return pl.pallas_call(
        body, name="s5_params_bwd",
        out_shape=[sd(a_re.shape, F32), sd(a_im.shape, F32), sd(log_dt.shape, F32), sd(bt_re.shape, F32), sd(bt_im.shape, F32)],
    )(a_re, a_im, log_dt, bt_re, bt_im, d_abr, d_abi, d_bbr, d_bbi)


def _s5_fwd(u, abr, abi, pw_r, pw_i, bp_r, bp_i, cp_r, cp_i, dvec, ts=128, side=None):
    s, c = u.shape
    n = abr.shape[1]
    nblk, cb, nb = bp_r.shape
    ts = min(ts, s)

    def body(u_ref, ar_ref, ai_ref, pr_ref, pi_ref, bpr_ref, bpi_ref, cpr_ref, cpi_ref, d_ref,
             hr_ref, hi_ref, yp_ref, gy_ref, car_r, car_i):
        i = pl.program_id(0)

        @pl.when(i == 0)
        def _():
            car_r[...] = jnp.zeros_like(car_r)
            car_i[...] = jnp.zeros_like(car_i)

        uv = u_ref[...]
        ub = uv.astype(BF16)
        br = jnp.concatenate([jnp.dot(ub[:, k * cb:(k + 1) * cb], bpr_ref[k], preferred_element_type=F32) for k in range(nblk)], axis=1)
        bi = jnp.concatenate([jnp.dot(ub[:, k * cb:(k + 1) * cb], bpi_ref[k], preferred_element_type=F32) for k in range(nblk)], axis=1)
        ar, ai = ar_ref[...], ai_ref[...]
        pr, pi_ = car_r[SUBLANES - 1:SUBLANES, :], car_i[SUBLANES - 1:SUBLANES, :]
        row = _rows(br.shape)
        br = br + jnp.where(row == 0, ar * pr - ai * pi_, 0.0)
        bi = bi + jnp.where(row == 0, ar * pi_ + ai * pr, 0.0)
        hr, hi = _scan_cplx(br, bi, pr_ref, pi_ref, reverse=False)
        car_r[...] = hr[ts - SUBLANES:]
        car_i[...] = hi[ts - SUBLANES:]
        hr_ref[...] = hr
        hi_ref[...] = hi
        hrb, hib = hr.astype(BF16), hi.astype(BF16)
        y = jnp.concatenate([jnp.dot(hrb[:, k * nb:(k + 1) * nb], cpr_ref[k], preferred_element_type=F32)
                             - jnp.dot(hib[:, k * nb:(k + 1) * nb], cpi_ref[k], preferred_element_type=F32) for k in range(nblk)], axis=1)
        yp = y + d_ref[...] * uv
        yp_ref[...] = yp
        gy_ref[...] = _gelu(yp).astype(BF16)

    full = lambda shape: pl.BlockSpec(shape, lambda i: (0,) * len(shape))
    rc = pl.BlockSpec((ts, c), lambda i: (i, 0))
    rn = pl.BlockSpec((ts, n), lambda i: (i, 0))
    sd = jax.ShapeDtypeStruct
    nt = s // ts
    return _call_with_side(
        body, side, lambda: pl.program_id(0) == 0, lambda: pl.program_id(0) == nt - 1,
        name="s5_fwd", grid=(nt,),
        in_specs=[rc, full(abr.shape), full(abi.shape), full(pw_r.shape), full(pw_i.shape), full(bp_r.shape), full(bp_i.shape),
                  full(cp_r.shape), full(cp_i.shape), full(dvec.shape)],
        out_specs=[rn, rn, rc, rc],
        out_shape=[sd((s, n), F32), sd((s, n), F32), sd((s, c), F32), sd((s, c), BF16)],
        scratch_shapes=[pltpu.VMEM((SUBLANES, n), F32), pltpu.VMEM((SUBLANES, n), F32)],
        args=(u, abr, abi, pw_r, pw_i, bp_r, bp_i, cp_r, cp_i, dvec))


def _s5_bwd(dgy, ypre, u, hr, hi, abr, abi, pw_r, pw_i, bp_r, bp_i, cp_r, cp_i, dvec, ts=128, side=None):
    s, c = u.shape
    n = abr.shape[1]
    nblk, cb, nb = bp_r.shape
    ts = min(ts, s)
    nt = s // ts
    hb = ts // SUBLANES
    tn_dims = (((0,), (0,)), ((), ()))
    nt_dims = (((1,), (1,)), ((), ()))

    def body(dgy_ref, yp_ref, u_ref, hr_ref, hi_ref, hrh_ref, hih_ref, ar_ref, ai_ref, pr_ref, pi_ref, bpr_ref, bpi_ref,
             cpr_ref, cpi_ref, d_ref,
             du_ref, dar_ref, dai_ref, dbr_ref, dbi_ref, dcr_ref, dci_ref, dd_ref, car_r, car_i, npi_ref):
        i = pl.program_id(0)
        time_first = i == nt - 1

        @pl.when(i == 0)
        def _():
            car_r[...] = jnp.zeros_like(car_r)
            car_i[...] = jnp.zeros_like(car_i)
            npi_ref[...] = -pi_ref[...]
            for ref in (dar_ref, dai_ref, dbr_ref, dbi_ref, dcr_ref, dci_ref, dd_ref):
                ref[...] = jnp.zeros_like(ref)

        uv = u_ref[...]
        _, dgel = _gelu_and_grad(yp_ref[...])
        dyv = dgy_ref[...] * dgel
        dd_ref[...] += jnp.sum(dyv * uv, axis=0, keepdims=True)
        dyb = dyv.astype(BF16)
        hr, hi = hr_ref[...], hi_ref[...]
        hrb, hib = hr.astype(BF16), hi.astype(BF16)
        dhr, dhi = [], []
        for k in range(nblk):
            dblk = dyb[:, k * cb:(k + 1) * cb]
            dhr.append(lax.dot_general(dblk, cpr_ref[k], nt_dims, preferred_element_type=F32))
            dhi.append(-lax.dot_general(dblk, cpi_ref[k], nt_dims, preferred_element_type=F32))
            dcr_ref[k] += lax.dot_general(hrb[:, k * nb:(k + 1) * nb], dblk, tn_dims, preferred_element_type=F32)
            dci_ref[k] += lax.dot_general(hib[:, k * nb:(k + 1) * nb], dblk, tn_dims, preferred_element_type=F32)
        dhr = jnp.concatenate(dhr, axis=1)
        dhi = jnp.concatenate(dhi, axis=1)
        ar, ai = ar_ref[...], ai_ref[...]
        nr, ni = car_r[0:1, :], car_i[0:1, :]
        row = _rows(dhr.shape)
        dhr = dhr + jnp.where(row == ts - 1, ar * nr + ai * ni, 0.0)
        dhi = dhi + jnp.where(row == ts - 1, ar * ni - ai * nr, 0.0)
        lr, li = _scan_cplx(dhr, dhi, pr_ref, npi_ref, reverse=True)
        car_r[...] = lr[:SUBLANES]
        car_i[...] = li[:SUBLANES]
        hpr = _shift_down(hr, jnp.where(time_first, 0.0, hrh_ref[...]), 1)
        hpi = _shift_down(hi, jnp.where(time_first, 0.0, hih_ref[...]), 1)
        dar_ref[...] += jnp.sum(lr * hpr + li * hpi, axis=0, keepdims=True)
        dai_ref[...] += jnp.sum(li * hpr - lr * hpi, axis=0, keepdims=True)
        lrb, lib = lr.astype(BF16), li.astype(BF16)
        ub = uv.astype(BF16)
        du = []
        for k in range(nblk):
            ublk = ub[:, k * cb:(k + 1) * cb]
            lrk, lik = lrb[:, k * nb:(k + 1) * nb], lib[:, k * nb:(k + 1) * nb]
            dbr_ref[k] += lax.dot_general(ublk, lrk, tn_dims, preferred_element_type=F32)
            dbi_ref[k] += lax.dot_general(ublk, lik, tn_dims, preferred_element_type=F32)
            du.append(lax.dot_general(lrk, bpr_ref[k], nt_dims, preferred_element_type=F32)
                      + lax.dot_general(lik, bpi_ref[k], nt_dims, preferred_element_type=F32))
        du_ref[...] = (d_ref[...] * dyv + jnp.concatenate(du, axis=1)).astype(BF16)

    full = lambda shape: pl.BlockSpec(shape, lambda i: (0,) * len(shape))
    rev = lambda i: nt - 1 - i
    halo_idx = lambda i: jnp.maximum(rev(i) * hb - 1, 0)
    rc = pl.BlockSpec((ts, c), lambda i: (rev(i), 0))
    rn = pl.BlockSpec((ts, n), lambda i: (rev(i), 0))
    hn = pl.BlockSpec((SUBLANES, n), lambda i: (halo_idx(i), 0))
    sd = jax.ShapeDtypeStruct
    return _call_with_side(
        body, side, lambda: pl.program_id(0) == 0, lambda: pl.program_id(0) == nt - 1,
        name="s5_bwd", grid=(nt,),
        in_specs=[rc, rc, rc, rn, rn, hn, hn, full(abr.shape), full(abi.shape), full(pw_r.shape), full(pw_i.shape),
                  full(bp_r.shape), full(bp_i.shape), full(cp_r.shape), full(cp_i.shape), full(dvec.shape)],
        out_specs=[rc, full(abr.shape), full(abi.shape), full(bp_r.shape), full(bp_i.shape), full(cp_r.shape), full(cp_i.shape),
                   full(dvec.shape)],
        out_shape=[sd((s, c), BF16), sd(abr.shape, F32), sd(abi.shape, F32), sd(bp_r.shape, F32), sd(bp_i.shape, F32),
                   sd(cp_r.shape, F32), sd(cp_i.shape, F32), sd(dvec.shape, F32)],
        scratch_shapes=[pltpu.VMEM((SUBLANES, n), F32), pltpu.VMEM((SUBLANES, n), F32), pltpu.VMEM(pw_i.shape, F32)],
        args=(dgy, ypre, u, hr, hi, hr, hi, abr, abi, pw_r, pw_i, bp_r, bp_i, cp_r, cp_i, dvec))


S5_LANE_CHUNK = 512


def _s5_tables(a_re, a_im, log_dt, bt_re, bt_im):
    g, p = a_re.shape
    gc = bt_re.shape[0]

    def body(ar_ref, ai_ref, dt_ref, br_ref, bi_ref, abr_ref, abi_ref, tr_ref, ti_ref, bbr_ref, bbi_ref):
        abr, abi, bbr, bbi = _s5_param_fn(ar_ref[...], ai_ref[...], dt_ref[...], br_ref[...], bi_ref[...])
        abr_ref[...] = abr
        abi_ref[...] = abi
        bbr_ref[...] = bbr
        bbi_ref[...] = bbi
        pows = [(abr, abi)]
        for _ in range(1, SUBLANES):
            qr, qi = pows[-1]
            pows.append((qr * abr - qi * abi, qr * abi + qi * abr))
        zero = jnp.zeros_like(abr)
        for r in range(SUBLANES):
            for k in range(3):
                sh = 1 << k
                tr_ref[k, r] = pows[sh - 1][0] if r >= sh else zero
                ti_ref[k, r] = pows[sh - 1][1] if r >= sh else zero
            tr_ref[3, r] = pows[r][0]
            ti_ref[3, r] = pows[r][1]

    sd = jax.ShapeDtypeStruct
    return pl.pallas_call(
        body, name="s5_tables",
        out_shape=[sd((g, p), F32), sd((g, p), F32), sd((4, SUBLANES, g, p), F32), sd((4, SUBLANES, g, p), F32),
                   sd((gc, g, p), F32), sd((gc, g, p), F32)],
    )(a_re, a_im, log_dt, bt_re, bt_im)


def _cmul_add(br, bi, tr, ti, sr, si):
    return br + tr * sr - ti * si, bi + tr * si + ti * sr


def _s5_fwd2(u, tab_r, tab_i, bp_r, bp_i, cp_r, cp_i, dvec, ts=256, side=None):
    s, c = u.shape
    n = tab_r.shape[2]
    nblk, cb, nb = bp_r.shape
    ts = min(ts, s)
    nsl = ts // SUBLANES
    lc = min(S5_LANE_CHUNK, n)

    def body(u_ref, tr_ref, ti_ref, bpr_ref, bpi_ref, cpr_ref, cpi_ref, d_ref, hr_ref, hi_ref, yp_ref, gy_ref,
             bur_ref, bui_ref, car_r, car_i):
        i = pl.program_id(0)

        @pl.when(i == 0)
        def _():
            car_r[...] = jnp.zeros_like(car_r)
            car_i[...] = jnp.zeros_like(car_i)

        uv = u_ref[...]
        ub = uv.astype(BF16)
        for k in range(nblk):
            bur_ref[:, k * nb:(k + 1) * nb] = jnp.dot(ub[:, k * cb:(k + 1) * cb], bpr_ref[k], preferred_element_type=F32)
            bui_ref[:, k * nb:(k + 1) * nb] = jnp.dot(ub[:, k * cb:(k + 1) * cb], bpi_ref[k], preferred_element_type=F32)
        for q in range(n // lc):
            sl = slice(q * lc, (q + 1) * lc)
            tabs = [(tr_ref[k, :, sl], ti_ref[k, :, sl]) for k in range(4)]

            def slab(j, carry, sl=sl, tabs=tabs):
                cr, ci = carry
                r0 = pl.multiple_of(j * SUBLANES, SUBLANES)
                br, bi = bur_ref[pl.ds(r0, SUBLANES), sl], bui_ref[pl.ds(r0, SUBLANES), sl]
                for k in range(3):
                    sh = 1 << k
                    br, bi = _cmul_add(br, bi, tabs[k][0], tabs[k][1], pltpu.roll(br, sh, 0), pltpu.roll(bi, sh, 0))
                hr, hi = _cmul_add(br, bi, tabs[3][0], tabs[3][1], jnp.broadcast_to(cr, br.shape), jnp.broadcast_to(ci, bi.shape))
                hr_ref[pl.ds(r0, SUBLANES), sl] = hr
                hi_ref[pl.ds(r0, SUBLANES), sl] = hi
                return hr[SUBLANES - 1:, :], hi[SUBLANES - 1:, :]

            cr, ci = lax.fori_loop(0, nsl, slab, (car_r[:, sl], car_i[:, sl]), unroll=2)
            car_r[:, sl] = cr
            car_i[:, sl] = ci
        hrb, hib = hr_ref[...].astype(BF16), hi_ref[...].astype(BF16)
        y = jnp.concatenate([jnp.dot(hrb[:, k * nb:(k + 1) * nb], cpr_ref[k], preferred_element_type=F32)
                             - jnp.dot(hib[:, k * nb:(k + 1) * nb], cpi_ref[k], preferred_element_type=F32) for k in range(nblk)], axis=1)
        yp = y + d_ref[...] * uv
        yp_ref[...] = yp
        gy_ref[...] = _gelu(yp).astype(BF16)

    full = lambda shape: pl.BlockSpec(shape, lambda i: (0,) * len(shape))
    rc = pl.BlockSpec((ts, c), lambda i: (i, 0))
    rn = pl.BlockSpec((ts, n), lambda i: (i, 0))
    sd = jax.ShapeDtypeStruct
    nt = s // ts
    return _call_with_side(
        body, side, lambda: pl.program_id(0) == 0, lambda: pl.program_id(0) == nt - 1,
        name="s5_fwd", grid=(nt,),
        in_specs=[rc, full(tab_r.shape), full(tab_i.shape), full(bp_r.shape), full(bp_i.shape), full(cp_r.shape), full(cp_i.shape),
                  full(dvec.shape)],
        out_specs=[rn, rn, rc, rc],
        out_shape=[sd((s, n), F32), sd((s, n), F32), sd((s, c), F32), sd((s, c), BF16)],
        scratch_shapes=[pltpu.VMEM((ts, n), F32), pltpu.VMEM((ts, n), F32), pltpu.VMEM((1, n), F32), pltpu.VMEM((1, n), F32)],
        args=(u, tab_r, tab_i, bp_r, bp_i, cp_r, cp_i, dvec))


def _s5_bwd2(dgy, ypre, u, hr, hi, rtab_r, rtab_i, bp_r, bp_i, cp_r, cp_i, dvec, ts=256, side=None):
    s, c = u.shape
    n = rtab_r.shape[2]
    nblk, cb, nb = bp_r.shape
    ts = min(ts, s)
    nt = s // ts
    hb = ts // SUBLANES
    nsl = ts // SUBLANES
    lc = min(S5_LANE_CHUNK, n)
    tn_dims = (((0,), (0,)), ((), ()))
    nt_dims = (((1,), (1,)), ((), ()))

    def body(dgy_ref, yp_ref, u_ref, hr_ref, hi_ref, hrh_ref, hih_ref, tr_ref, ti_ref, bpr_ref, bpi_ref, cpr_ref, cpi_ref, d_ref,
             du_ref, dar_ref, dai_ref, dbr_ref, dbi_ref, dcr_ref, dci_ref, dd_ref, lr_ref, li_ref, car_r, car_i):
        i = pl.program_id(0)
        time_first = i == nt - 1

        @pl.when(i == 0)
        def _():
            car_r[...] = jnp.zeros_like(car_r)
            car_i[...] = jnp.zeros_like(car_i)
            for ref in (dar_ref, dai_ref, dbr_ref, dbi_ref, dcr_ref, dci_ref, dd_ref):
                ref[...] = jnp.zeros_like(ref)

        uv = u_ref[...]
        _, dgel = _gelu_and_grad(yp_ref[...])
        dyv = dgy_ref[...] * dgel
        dd_ref[...] += jnp.sum(dyv * uv, axis=0, keepdims=True)
        dyb = dyv.astype(BF16)
        hrb, hib = hr_ref[...].astype(BF16), hi_ref[...].astype(BF16)
        for k in range(nblk):
            dblk = dyb[:, k * cb:(k + 1) * cb]
            lr_ref[:, k * nb:(k + 1) * nb] = lax.dot_general(dblk, cpr_ref[k], nt_dims, preferred_element_type=F32)
            li_ref[:, k * nb:(k + 1) * nb] = -lax.dot_general(dblk, cpi_ref[k], nt_dims, preferred_element_type=F32)
            dcr_ref[k] += lax.dot_general(hrb[:, k * nb:(k + 1) * nb], dblk, tn_dims, preferred_element_type=F32)
            dci_ref[k] += lax.dot_general(hib[:, k * nb:(k + 1) * nb], dblk, tn_dims, preferred_element_type=F32)
        row8 = _rows((SUBLANES, lc))
        for q in range(n // lc):
            sl = slice(q * lc, (q + 1) * lc)
            tabs = [(tr_ref[k, :, sl], ti_ref[k, :, sl]) for k in range(4)]
            halo_r = jnp.where(time_first, 0.0, hrh_ref[SUBLANES - 1:, sl])
            halo_i = jnp.where(time_first, 0.0, hih_ref[SUBLANES - 1:, sl])

            def slab(jj, carry, sl=sl, tabs=tabs, halo_r=halo_r, halo_i=halo_i):
                nr, ni, acc_r, acc_i = carry
                j = nsl - 1 - jj
                r0 = pl.multiple_of(j * SUBLANES, SUBLANES)
                br, bi = lr_ref[pl.ds(r0, SUBLANES), sl], li_ref[pl.ds(r0, SUBLANES), sl]
                for k in range(3):
                    sh = 1 << k
                    br, bi = _cmul_add(br, bi, tabs[k][0], tabs[k][1], pltpu.roll(br, SUBLANES - sh, 0),
                                       pltpu.roll(bi, SUBLANES - sh, 0))
                lr, li = _cmul_add(br, bi, tabs[3][0], tabs[3][1], jnp.broadcast_to(nr, br.shape), jnp.broadcast_to(ni, bi.shape))
                lr_ref[pl.ds(r0, SUBLANES), sl] = lr
                li_ref[pl.ds(r0, SUBLANES), sl] = li
                p0 = pl.multiple_of(jnp.maximum(j - 1, 0) * SUBLANES, SUBLANES)
                prev_r = jnp.where(j == 0, halo_r, hr_ref[pl.ds(p0, SUBLANES), sl][SUBLANES - 1:, :])
                prev_i = jnp.where(j == 0, halo_i, hi_ref[pl.ds(p0, SUBLANES), sl][SUBLANES - 1:, :])
                hpr = jnp.where(row8 == 0, jnp.broadcast_to(prev_r, br.shape), pltpu.roll(hr_ref[pl.ds(r0, SUBLANES), sl], 1, 0))
                hpi = jnp.where(row8 == 0, jnp.broadcast_to(prev_i, bi.shape), pltpu.roll(hi_ref[pl.ds(r0, SUBLANES), sl], 1, 0))
                return lr[:1, :], li[:1, :], acc_r + (lr * hpr + li * hpi), acc_i + (li * hpr - lr * hpi)

            zero = jnp.zeros((SUBLANES, lc), F32)
            nr, ni, acc_r, acc_i = lax.fori_loop(0, nsl, slab, (car_r[:, sl], car_i[:, sl], zero, zero), unroll=2)
            car_r[:, sl] = nr
            car_i[:, sl] = ni
            dar_ref[:, sl] += jnp.sum(acc_r, axis=0, keepdims=True)
            dai_ref[:, sl] += jnp.sum(acc_i, axis=0, keepdims=True)
        lrb, lib = lr_ref[...].astype(BF16), li_ref[...].astype(BF16)
        ub = uv.astype(BF16)
        du = []
        for k in range(nblk):
            ublk = ub[:, k * cb:(k + 1) * cb]
            lrk, lik = lrb[:, k * nb:(k + 1) * nb], lib[:, k * nb:(k + 1) * nb]
            dbr_ref[k] += lax.dot_general(ublk, lrk, tn_dims, preferred_element_type=F32)
            dbi_ref[k] += lax.dot_general(ublk, lik, tn_dims, preferred_element_type=F32)
            du.append(lax.dot_general(lrk, bpr_ref[k], nt_dims, preferred_element_type=F32)
                      + lax.dot_general(lik, bpi_ref[k], nt_dims, preferred_element_type=F32))
        du_ref[...] = (d_ref[...] * dyv + jnp.concatenate(du, axis=1)).astype(BF16)

    full = lambda shape: pl.BlockSpec(shape, lambda i: (0,) * len(shape))
    rev = lambda i: nt - 1 - i
    halo_idx = lambda i: jnp.maximum(rev(i) * hb - 1, 0)
    rc = pl.BlockSpec((ts, c), lambda i: (rev(i), 0))
    rn = pl.BlockSpec((ts, n), lambda i: (rev(i), 0))
    hn = pl.BlockSpec((SUBLANES, n), lambda i: (halo_idx(i), 0))
    sd = jax.ShapeDtypeStruct
    vec_n = (1, n)
    return _call_with_side(
        body, side, lambda: pl.program_id(0) == 0, lambda: pl.program_id(0) == nt - 1,
        name="s5_bwd", grid=(nt,),
        in_specs=[rc, rc, rc, rn, rn, hn, hn, full(rtab_r.shape), full(rtab_i.shape),
                  full(bp_r.shape), full(bp_i.shape), full(cp_r.shape), full(cp_i.shape), full(dvec.shape)],
        out_specs=[rc, full(vec_n), full(vec_n), full(bp_r.shape), full(bp_i.shape), full(cp_r.shape), full(cp_i.shape),
                   full(dvec.shape)],
        out_shape=[sd((s, c), BF16), sd(vec_n, F32), sd(vec_n, F32), sd(bp_r.shape, F32), sd(bp_i.shape, F32),
                   sd(cp_r.shape, F32), sd(cp_i.shape, F32), sd(dvec.shape, F32)],
        scratch_shapes=[pltpu.VMEM((ts, n), F32), pltpu.VMEM((ts, n), F32), pltpu.VMEM((1, n), F32), pltpu.VMEM((1, n), F32)],
        args=(dgy, ypre, u, hr, hi, hr, hi, rtab_r, rtab_i, bp_r, bp_i, cp_r, cp_i, dvec))


def _s5_tables3(a_re, a_im, log_dt, bt_re, bt_im, seg):
    g, p = a_re.shape
    gc = bt_re.shape[0]
    nsq = int(math.log2(seg))
    assert 1 << nsq == seg

    def body(ar_ref, ai_ref, dt_ref, br_ref, bi_ref, tr_ref, ti_ref, rtr_ref, rti_ref, bbr_ref, bbi_ref):
        abr, abi, bbr, bbi = _s5_param_fn(ar_ref[...], ai_ref[...], dt_ref[...], br_ref[...], bi_ref[...])
        bbr_ref[...] = bbr
        bbi_ref[...] = bbi
        qr, qi = abr, abi
        for _ in range(nsq):
            qr, qi = qr * qr - qi * qi, 2.0 * qr * qi
        pows = [(qr, qi)]
        for _ in range(1, SUBLANES):
            cr, ci = pows[-1]
            pows.append((cr * qr - ci * qi, cr * qi + ci * qr))
        zero = jnp.zeros_like(abr)
        for r in range(SUBLANES):
            rows = [(pows[(1 << k) - 1] if r >= (1 << k) else (zero, zero)) for k in range(3)] + [pows[r], (abr, abi)]
            for k, (vr, vi) in enumerate(rows):
                tr_ref[k, r] = vr
                ti_ref[k, r] = vi
                rtr_ref[k, SUBLANES - 1 - r] = vr
                rti_ref[k, SUBLANES - 1 - r] = -vi

    sd = jax.ShapeDtypeStruct
    tab = sd((5, SUBLANES, g, p), F32)
    return pl.pallas_call(
        body, name="s5_tables", out_shape=[tab, tab, tab, tab, sd((gc, g, p), F32), sd((gc, g, p), F32)],
    )(a_re, a_im, log_dt, bt_re, bt_im)


def _segment_perm(ts):
    seg = ts // SUBLANES
    rho = jnp.arange(ts)
    src = (rho % SUBLANES) * seg + rho // SUBLANES
    return (src[:, None] == jnp.arange(ts)[None, :]).astype(BF16)


def _exact_rows(perm_t, x):
    hi = x.astype(BF16)
    r1 = x - hi.astype(F32)
    mid = r1.astype(BF16)
    lo = (r1 - mid.astype(F32)).astype(BF16)
    dot = lambda v: jnp.dot(perm_t, v, preferred_element_type=F32)
    return (dot(hi) + dot(mid)) + dot(lo)


def _s5_fwd3(u, perm, perm_t, tab_r, tab_i, bp_r, bp_i, cp_r, cp_i, dvec, ts=256, side=None):
    s, c = u.shape
    n = tab_r.shape[2]
    nblk, cb, nb = bp_r.shape
    ts = min(ts, s)
    seg = ts // SUBLANES
    lc = min(S5_LANE_CHUNK, n)

    def body(u_ref, p_ref, pt_ref, tr_ref, ti_ref, bpr_ref, bpi_ref, cpr_ref, cpi_ref, d_ref, hr_ref, hi_ref, yp_ref, gy_ref,
             bur_ref, bui_ref, car_r, car_i):
        i = pl.program_id(0)

        @pl.when(i == 0)
        def _():
            car_r[...] = jnp.zeros_like(car_r)
            car_i[...] = jnp.zeros_like(car_i)

        uv = u_ref[...]
        ubp = jnp.dot(p_ref[...], uv.astype(BF16), preferred_element_type=F32).astype(BF16)
        for k in range(nblk):
            bur_ref[:, k * nb:(k + 1) * nb] = jnp.dot(ubp[:, k * cb:(k + 1) * cb], bpr_ref[k], preferred_element_type=F32)
            bui_ref[:, k * nb:(k + 1) * nb] = jnp.dot(ubp[:, k * cb:(k + 1) * cb], bpi_ref[k], preferred_element_type=F32)
        row8 = _rows((SUBLANES, lc))
        for q in range(n // lc):
            sl = slice(q * lc, (q + 1) * lc)
            tabs = [(tr_ref[k, :, sl], ti_ref[k, :, sl]) for k in range(5)]
            a_r, a_i = tabs[4]

            def local(r, carry, sl=sl, a_r=a_r, a_i=a_i):
                r0 = pl.multiple_of(r * SUBLANES, SUBLANES)
                hr, hi = _cmul_add(bur_ref[pl.ds(r0, SUBLANES), sl], bui_ref[pl.ds(r0, SUBLANES), sl], a_r, a_i, carry[0], carry[1])
                hr_ref[pl.ds(r0, SUBLANES), sl] = hr
                hi_ref[pl.ds(r0, SUBLANES), sl] = hi
                return hr, hi

            zero = jnp.zeros((SUBLANES, lc), F32)
            er, ei = lax.fori_loop(0, seg, local, (zero, zero), unroll=4)
            for k in range(3):
                sh = 1 << k
                er, ei = _cmul_add(er, ei, tabs[k][0], tabs[k][1], pltpu.roll(er, sh, 0), pltpu.roll(ei, sh, 0))
            cin_r, cin_i = jnp.broadcast_to(car_r[:, sl], er.shape), jnp.broadcast_to(car_i[:, sl], ei.shape)
            er, ei = _cmul_add(er, ei, tabs[3][0], tabs[3][1], cin_r, cin_i)
            car_r[:, sl] = er[SUBLANES - 1:, :]
            car_i[:, sl] = ei[SUBLANES - 1:, :]
            c_r = jnp.where(row8 == 0, cin_r, pltpu.roll(er, 1, 0))
            c_i = jnp.where(row8 == 0, cin_i, pltpu.roll(ei, 1, 0))

            def fix(r, carry, sl=sl, a_r=a_r, a_i=a_i, c_r=c_r, c_i=c_i):
                pr, pi = carry
                r0 = pl.multiple_of(r * SUBLANES, SUBLANES)
                hr, hi = _cmul_add(hr_ref[pl.ds(r0, SUBLANES), sl], hi_ref[pl.ds(r0, SUBLANES), sl], pr, pi, c_r, c_i)
                hr_ref[pl.ds(r0, SUBLANES), sl] = hr
                hi_ref[pl.ds(r0, SUBLANES), sl] = hi
                return pr * a_r - pi * a_i, pr * a_i + pi * a_r

            lax.fori_loop(0, seg, fix, (a_r, a_i), unroll=4)
        hrb, hib = hr_ref[...].astype(BF16), hi_ref[...].astype(BF16)
        y = jnp.concatenate([jnp.dot(hrb[:, k * nb:(k + 1) * nb], cpr_ref[k], preferred_element_type=F32)
                             - jnp.dot(hib[:, k * nb:(k + 1) * nb], cpi_ref[k], preferred_element_type=F32) for k in range(nblk)], axis=1)
        yp = _exact_rows(pt_ref[...], y) + d_ref[...] * uv
        yp_ref[...] = yp
        gy_ref[...] = _gelu(yp).astype(BF16)

    full = lambda shape: pl.BlockSpec(shape, lambda i: (0,) * len(shape))
    rc = pl.BlockSpec((ts, c), lambda i: (i, 0))
    rn = pl.BlockSpec((ts, n), lambda i: (i, 0))
    sd = jax.ShapeDtypeStruct
    nt = s // ts
    return _call_with_side(
        body, side, lambda: pl.program_id(0) == 0, lambda: pl.program_id(0) == nt - 1,
        name="s5_fwd", grid=(nt,),
        in_specs=[rc, full(perm.shape), full(perm_t.shape), full(tab_r.shape), full(tab_i.shape), full(bp_r.shape), full(bp_i.shape),
                  full(cp_r.shape), full(cp_i.shape), full(dvec.shape)],
        out_specs=[rn, rn, rc, rc],
        out_shape=[sd((s, n), F32), sd((s, n), F32), sd((s, c), F32), sd((s, c), BF16)],
        scratch_shapes=[pltpu.VMEM((ts, n), F32), pltpu.VMEM((ts, n), F32), pltpu.VMEM((1, n), F32), pltpu.VMEM((1, n), F32)],
        args=(u, perm, perm_t, tab_r, tab_i, bp_r, bp_i, cp_r, cp_i, dvec))


def _s5_bwd3(dgy, ypre, u, hr, hi, perm, perm_t, rtab_r, rtab_i, bp_r, bp_i, cp_r, cp_i, dvec, ts=256, side=None):
    s, c = u.shape
    n = rtab_r.shape[2]
    nblk, cb, nb = bp_r.shape
    ts = min(ts, s)
    nt = s // ts
    hb = ts // SUBLANES
    seg = ts // SUBLANES
    lc = min(S5_LANE_CHUNK, n)
    tn_dims = (((0,), (0,)), ((), ()))
    nt_dims = (((1,), (1,)), ((), ()))

    def body(dgy_ref, yp_ref, u_ref, hr_ref, hi_ref, hrh_ref, hih_ref, p_ref, pt_ref, tr_ref, ti_ref, bpr_ref, bpi_ref,
             cpr_ref, cpi_ref, d_ref, du_ref, dar_ref, dai_ref, dbr_ref, dbi_ref, dcr_ref, dci_ref, dd_ref, lr_ref, li_ref,
             car_r, car_i):
        i = pl.program_id(0)
        time_first = i == nt - 1

        @pl.when(i == 0)
        def _():
            car_r[...] = jnp.zeros_like(car_r)
            car_i[...] = jnp.zeros_like(car_i)
            for ref in (dar_ref, dai_ref, dbr_ref, dbi_ref, dcr_ref, dci_ref, dd_ref):
                ref[...] = jnp.zeros_like(ref)

        uv = u_ref[...]
        _, dgel = _gelu_and_grad(yp_ref[...])
        dyv = dgy_ref[...] * dgel
        dd_ref[...] += jnp.sum(dyv * uv, axis=0, keepdims=True)
        perm_m = p_ref[...]
        dyb = jnp.dot(perm_m, dyv.astype(BF16), preferred_element_type=F32).astype(BF16)
        ub = jnp.dot(perm_m, uv.astype(BF16), preferred_element_type=F32).astype(BF16)
        hrb, hib = hr_ref[...].astype(BF16), hi_ref[...].astype(BF16)
        for k in range(nblk):
            dblk = dyb[:, k * cb:(k + 1) * cb]
            lr_ref[:, k * nb:(k + 1) * nb] = lax.dot_general(dblk, cpr_ref[k], nt_dims, preferred_element_type=F32)
            li_ref[:, k * nb:(k + 1) * nb] = -lax.dot_general(dblk, cpi_ref[k], nt_dims, preferred_element_type=F32)
            dcr_ref[k] += lax.dot_general(hrb[:, k * nb:(k + 1) * nb], dblk, tn_dims, preferred_element_type=F32)
            dci_ref[k] += lax.dot_general(hib[:, k * nb:(k + 1) * nb], dblk, tn_dims, preferred_element_type=F32)
        row8 = _rows((SUBLANES, lc))
        last0 = (seg - 1) * SUBLANES
        for q in range(n // lc):
            sl = slice(q * lc, (q + 1) * lc)
            tabs = [(tr_ref[k, :, sl], ti_ref[k, :, sl]) for k in range(5)]
            a_r, a_i = tabs[4]

            def local(rr, carry, sl=sl, a_r=a_r, a_i=a_i):
                r0 = pl.multiple_of((seg - 1 - rr) * SUBLANES, SUBLANES)
                lr, li = _cmul_add(lr_ref[pl.ds(r0, SUBLANES), sl], li_ref[pl.ds(r0, SUBLANES), sl], a_r, a_i, carry[0], carry[1])
                lr_ref[pl.ds(r0, SUBLANES), sl] = lr
                li_ref[pl.ds(r0, SUBLANES), sl] = li
                return lr, li

            zero = jnp.zeros((SUBLANES, lc), F32)
            er, ei = lax.fori_loop(0, seg, local, (zero, zero), unroll=4)
            for k in range(3):
                sh = 1 << k
                er, ei = _cmul_add(er, ei, tabs[k][0], tabs[k][1], pltpu.roll(er, SUBLANES - sh, 0), pltpu.roll(ei, SUBLANES - sh, 0))
            cin_r, cin_i = jnp.broadcast_to(car_r[:, sl], er.shape), jnp.broadcast_to(car_i[:, sl], ei.shape)
            er, ei = _cmul_add(er, ei, tabs[3][0], tabs[3][1], cin_r, cin_i)
            car_r[:, sl] = er[:1, :]
            car_i[:, sl] = ei[:1, :]
            c_r = jnp.where(row8 == SUBLANES - 1, cin_r, pltpu.roll(er, SUBLANES - 1, 0))
            c_i = jnp.where(row8 == SUBLANES - 1, cin_i, pltpu.roll(ei, SUBLANES - 1, 0))
            halo_r = jnp.where(time_first, 0.0, hrh_ref[SUBLANES - 1:, sl])
            halo_i = jnp.where(time_first, 0.0, hih_ref[SUBLANES - 1:, sl])
            hp0_r = jnp.where(row8 == 0, jnp.broadcast_to(halo_r, zero.shape), pltpu.roll(hr_ref[pl.ds(last0, SUBLANES), sl], 1, 0))
            hp0_i = jnp.where(row8 == 0, jnp.broadcast_to(halo_i, zero.shape), pltpu.roll(hi_ref[pl.ds(last0, SUBLANES), sl], 1, 0))

            def fix(rr, carry, sl=sl, a_r=a_r, a_i=a_i, c_r=c_r, c_i=c_i, hp0_r=hp0_r, hp0_i=hp0_i):
                pr, pi, acc_r, acc_i = carry
                r = seg - 1 - rr
                r0 = pl.multiple_of(r * SUBLANES, SUBLANES)
                lr, li = _cmul_add(lr_ref[pl.ds(r0, SUBLANES), sl], li_ref[pl.ds(r0, SUBLANES), sl], pr, pi, c_r, c_i)
                lr_ref[pl.ds(r0, SUBLANES), sl] = lr
                li_ref[pl.ds(r0, SUBLANES), sl] = li
                p0 = pl.multiple_of(jnp.maximum(r - 1, 0) * SUBLANES, SUBLANES)
                hpr = jnp.where(r == 0, hp0_r, hr_ref[pl.ds(p0, SUBLANES), sl])
                hpi = jnp.where(r == 0, hp0_i, hi_ref[pl.ds(p0, SUBLANES), sl])
                return (pr * a_r - pi * a_i, pr * a_i + pi * a_r, acc_r + (lr * hpr + li * hpi), acc_i + (li * hpr - lr * hpi))

            _, _, acc_r, acc_i = lax.fori_loop(0, seg, fix, (a_r, a_i, zero, zero), unroll=4)
            dar_ref[:, sl] += jnp.sum(acc_r, axis=0, keepdims=True)
            dai_ref[:, sl] += jnp.sum(acc_i, axis=0, keepdims=True)
        lrb, lib = lr_ref[...].astype(BF16), li_ref[...].astype(BF16)
        du = []
        for k in range(nblk):
            ublk = ub[:, k * cb:(k + 1) * cb]
            lrk, lik = lrb[:, k * nb:(k + 1) * nb], lib[:, k * nb:(k + 1) * nb]
            dbr_ref[k] += lax.dot_general(ublk, lrk, tn_dims, preferred_element_type=F32)
            dbi_ref[k] += lax.dot_general(ublk, lik, tn_dims, preferred_element_type=F32)
            du.append(lax.dot_general(lrk, bpr_ref[k], nt_dims, preferred_element_type=F32)
                      + lax.dot_general(lik, bpi_ref[k], nt_dims, preferred_element_type=F32))
        du_ref[...] = (d_ref[...] * dyv + _exact_rows(pt_ref[...], jnp.concatenate(du, axis=1))).astype(BF16)

    full = lambda shape: pl.BlockSpec(shape, lambda i: (0,) * len(shape))
    rev = lambda i: nt - 1 - i
    halo_idx = lambda i: jnp.maximum(rev(i) * hb - 1, 0)
    rc = pl.BlockSpec((ts, c), lambda i: (rev(i), 0))
    rn = pl.BlockSpec((ts, n), lambda i: (rev(i), 0))
    hn = pl.BlockSpec((SUBLANES, n), lambda i: (halo_idx(i), 0))
    sd = jax.ShapeDtypeStruct
    vec_n = (1, n)
    return _call_with_side(
        body, side, lambda: pl.program_id(0) == 0, lambda: pl.program_id(0) == nt - 1,
        name="s5_bwd", grid=(nt,),
        in_specs=[rc, rc, rc, rn, rn, hn, hn, full(perm.shape), full(perm_t.shape), full(rtab_r.shape), full(rtab_i.shape),
                  full(bp_r.shape), full(bp_i.shape), full(cp_r.shape), full(cp_i.shape), full(dvec.shape)],
        out_specs=[rc, full(vec_n), full(vec_n), full(bp_r.shape), full(bp_i.shape), full(cp_r.shape), full(cp_i.shape),
                   full(dvec.shape)],
        out_shape=[sd((s, c), BF16), sd(vec_n, F32), sd(vec_n, F32), sd(bp_r.shape, F32), sd(bp_i.shape, F32),
                   sd(cp_r.shape, F32), sd(cp_i.shape, F32), sd(dvec.shape, F32)],
        scratch_shapes=[pltpu.VMEM((ts, n), F32), pltpu.VMEM((ts, n), F32), pltpu.VMEM((1, n), F32), pltpu.VMEM((1, n), F32)],
        args=(dgy, ypre, u, hr, hi, hr, hi, perm, perm_t, rtab_r, rtab_i, bp_r, bp_i, cp_r, cp_i, dvec))


def _glu(gl2, ts=512):
    _, s, c = gl2.shape
    ts = min(ts, s)

    def body(g_ref, o_ref):
        o_ref[...] = (g_ref[0] * _sigmoid(g_ref[1])).astype(BF16)

    return pl.pallas_call(
        body, name="glu", grid=(s // ts,), in_specs=[pl.BlockSpec((2, ts, c), lambda i: (0, i, 0))],
        out_specs=pl.BlockSpec((ts, c), lambda i: (i, 0)), out_shape=jax.ShapeDtypeStruct((s, c), BF16), compiler_params=_cparams(),
    )(gl2)


def _glu_bwd(gl2, d_o, ts=512):
    _, s, c = gl2.shape
    ts = min(ts, s)

    def body(g_ref, do_ref, o_ref):
        sg = _sigmoid(g_ref[1])
        dov = do_ref[...]
        o_ref[0] = (dov * sg).astype(BF16)
        o_ref[1] = (dov * g_ref[0] * sg * (1.0 - sg)).astype(BF16)

    blk = pl.BlockSpec((2, ts, c), lambda i: (0, i, 0))
    return pl.pallas_call(
        body, name="glu_bwd", grid=(s // ts,), in_specs=[blk, pl.BlockSpec((ts, c), lambda i: (i, 0))],
        out_specs=blk, out_shape=jax.ShapeDtypeStruct((2, s, c), BF16), compiler_params=_cparams(),
    )(gl2, d_o)


PACK_ROW_MULTIPLE = 1024
ELEMENTWISE_BLOCK_ELEMS = 256 * 1024


def _row_tile(rows, cols):
    pref = max(SUBLANES, 1 << int(math.log2(max(1, ELEMENTWISE_BLOCK_ELEMS // cols))))
    if rows <= pref:
        return rows
    t = pref
    while rows % t:
        t //= 2
    assert t >= SUBLANES, rows
    return t


def _sum_parts(rs, side=None):
    nl = len(rs)
    p, rows, cols = rs[0].shape
    tr = _row_tile(rows, cols)
    nt = rows // tr

    def body(*refs):
        o_ref = refs[nl]
        for l in range(nl):
            acc = refs[l][0].astype(F32)
            for k in range(1, p):
                acc = acc + refs[l][k].astype(F32)
            o_ref[l] = acc

    outs, got = _call_with_side(
        body, side, lambda: pl.program_id(0) == 0, lambda: pl.program_id(0) == nt - 1,
        name="sum_parts", grid=(nt,), in_specs=[pl.BlockSpec((p, tr, cols), lambda i: (0, i, 0))] * nl,
        out_specs=[pl.BlockSpec((nl, tr, cols), lambda i: (0, i, 0))], out_shape=[jax.ShapeDtypeStruct((nl, rows, cols), F32)],
        scratch_shapes=[], args=tuple(rs))
    return outs[0], got


def _adamw(w, g_parts, m, v, side=None):
    rows, cols = w.shape
    tr = _row_tile(rows, max(cols, LANES))
    ng = len(g_parts)
    emit_grad = ng > 1
    c1 = 1.0 / (1.0 - ADAM_B1 ** ADAM_STEP)
    c2 = 1.0 / (1.0 - ADAM_B2 ** ADAM_STEP)

    def body(*refs):
        w_ref, m_ref, v_ref = refs[0], refs[1 + ng], refs[2 + ng]
        dl_ref, nm_ref, nv_ref = refs[3 + ng:6 + ng]
        g = refs[1][...]
        for k in range(1, ng):
            g = g + refs[1 + k][...]
        mn = ADAM_B1 * m_ref[...] + (1.0 - ADAM_B1) * g
        vn = ADAM_B2 * v_ref[...] + (1.0 - ADAM_B2) * (g * g)
        if emit_grad:
            refs[6 + ng][...] = g
        nm_ref[...] = mn
        nv_ref[...] = vn
        dl_ref[...] = -ADAM_LR * ((mn * c1) / (jnp.sqrt(vn * c2) + ADAM_EPS) + ADAM_WD * w_ref[...])

    blk = pl.BlockSpec((tr, cols), lambda i: (i, 0))
    sd = jax.ShapeDtypeStruct((rows, cols), F32)
    nout = 4 if emit_grad else 3
    nt = rows // tr
    return _call_with_side(
        body, side, lambda: pl.program_id(0) == 0, lambda: pl.program_id(0) == nt - 1,
        name="adamw", grid=(nt,), in_specs=[blk] * (3 + ng), out_specs=[blk] * nout, out_shape=[sd] * nout,
        scratch_shapes=[], args=(w, *g_parts, m, v))


def _place():
    x, y, c = lax.axis_index("x"), lax.axis_index("y"), lax.axis_index("c")
    chips = [(1 - x, y), (x, 1 - y), (1 - x, 1 - y)]
    return x, y, c, chips


class Side:
    def __init__(self, ins, outs, kind, views=None):
        self.ins, self.outs, self.kind = list(ins), list(outs), kind
        n = len(self.ins)
        self.views = views or [None] * n
        self.sems = [pltpu.SemaphoreType.DMA((3 * n,)), pltpu.SemaphoreType.DMA((3 * n,)), pltpu.SemaphoreType.DMA((n,))]

    def _copies(self, ins, outs, send, recv, lsem):
        x, y, c, chips = _place()
        me = 2 * x + y
        local, out_going, in_coming = [], [], []
        for t in range(len(ins)):
            if self.kind == 'sibling':
                cp = pltpu.make_async_remote_copy(src_ref=ins[t], dst_ref=outs[t], send_sem=send.at[t], recv_sem=recv.at[t],
                                                  device_id=(x, y, 1 - c), device_id_type=MESH)
                out_going.append(cp)
                in_coming.append(cp)
                continue
            if self.kind == 'gather':
                src_local, srcs, dst_mine = ins[t], [ins[t]] * 3, outs[t].at[me]
            else:
                part = (lambda p, t=t: self.views[t](ins[t], p)) if self.views[t] else (lambda p, t=t: ins[t].at[p])
                src_local, srcs, dst_mine = part(me), [part(2 * px + py) for px, py in chips], outs[t].at[me]
            local.append(pltpu.make_async_copy(src_local, dst_mine, lsem.at[t]))
            for r, (px, py) in enumerate(chips):
                out_going.append(pltpu.make_async_remote_copy(
                    src_ref=srcs[r], dst_ref=dst_mine, send_sem=send.at[3 * t + r], recv_sem=recv.at[3 * t + r],
                    device_id=(px, py, c), device_id_type=MESH))
                in_coming.append(pltpu.make_async_remote_copy(
                    src_ref=srcs[r], dst_ref=outs[t].at[2 * px + py], send_sem=send.at[3 * t + r], recv_sem=recv.at[3 * t + r],
                    device_id=(px, py, c), device_id_type=MESH))
        return local, out_going, in_coming

    def start(self, ins, outs, send, recv, lsem):
        local, out_going, _ = self._copies(ins, outs, send, recv, lsem)
        for cp in local + out_going:
            cp.start()

    def wait(self, ins, outs, send, recv, lsem):
        local, out_going, in_coming = self._copies(ins, outs, send, recv, lsem)
        for cp in in_coming:
            cp.wait_recv()
        for cp in out_going:
            cp.wait_send()
        for cp in local:
            cp.wait()


def _gather_side(shards):
    return Side(shards, [jax.ShapeDtypeStruct((N_CHIPS,) + s.shape, s.dtype) for s in shards], 'gather')


def _scatter_side(grads, shapes, views):
    return Side(grads, [jax.ShapeDtypeStruct(s, g.dtype) for g, s in zip(grads, shapes)], 'scatter', views)


def _sibling_side(arrs):
    return Side(arrs, [jax.ShapeDtypeStruct(a.shape, a.dtype) for a in arrs], 'sibling')


def _halves_view(ref, p):
    half = ref.shape[2] // 2
    return ref.at[p // 2, :, pl.ds((p % 2) * half, half)]


def _call_with_side(body, side, first, last, *, name, grid, in_specs, out_specs, out_shape, scratch_shapes, args):
    if side is None:
        outs = pl.pallas_call(body, name=name, grid=grid, in_specs=in_specs, out_specs=out_specs, out_shape=out_shape,
                              scratch_shapes=scratch_shapes, compiler_params=_cparams())(*args)
        return outs, []
    n_in, n_out, n_sc = len(in_specs), len(out_specs), len(scratch_shapes)
    ns_in, ns_out = len(side.ins), len(side.outs)

    def wrapped(*refs):
        base_in, s_in = refs[:n_in], refs[n_in:n_in + ns_in]
        o0 = n_in + ns_in
        base_out, s_out = refs[o0:o0 + n_out], refs[o0 + n_out:o0 + n_out + ns_out]
        sc0 = o0 + n_out + ns_out
        base_sc, sems = refs[sc0:sc0 + n_sc], refs[sc0 + n_sc:]

        @pl.when(first())
        def _():
            side.start(s_in, s_out, *sems)

        body(*base_in, *base_out, *base_sc)

        @pl.when(last())
        def _():
            side.wait(s_in, s_out, *sems)

    any_spec = pl.BlockSpec(memory_space=pl.ANY)
    outs = pl.pallas_call(
        wrapped, name=name, grid=grid, in_specs=list(in_specs) + [any_spec] * ns_in, out_specs=list(out_specs) + [any_spec] * ns_out,
        out_shape=list(out_shape) + side.outs, scratch_shapes=list(scratch_shapes) + side.sems, compiler_params=_cparams(),
    )(*args, *side.ins)
    return outs[:n_out], outs[n_out:]


def _run_side(name, side):
    def body(*refs):
        n = len(side.ins)
        side.start(refs[:n], refs[n:2 * n], *refs[2 * n:])
        side.wait(refs[:n], refs[n:2 * n], *refs[2 * n:])

    any_spec = pl.BlockSpec(memory_space=pl.ANY)
    return pl.pallas_call(body, name=name, in_specs=[any_spec] * len(side.ins), out_specs=[any_spec] * len(side.outs),
                          out_shape=side.outs, scratch_shapes=side.sems)(*side.ins)


def _gather_shards(shards, layer_major):
    n = len(shards)

    def body(*refs):
        ins, outs = refs[:n], refs[n:2 * n]
        send, recv, lsem = refs[2 * n:]
        x, y, c, chips = _place()
        me = 2 * x + y

        def slot(t, chip):
            return outs[t].at[:, chip] if layer_major[t] else outs[t].at[chip]

        local, sends = [], []
        for t in range(n):
            cp = pltpu.make_async_copy(ins[t], slot(t, me), lsem.at[t])
            cp.start()
            local.append(cp)
            for r, (px, py) in enumerate(chips):
                rc = pltpu.make_async_remote_copy(src_ref=ins[t], dst_ref=slot(t, me), send_sem=send.at[3 * t + r],
                                                  recv_sem=recv.at[3 * t + r], device_id=(px, py, c), device_id_type=MESH)
                rc.start()
                sends.append(rc)
        for t in range(n):
            for r, (px, py) in enumerate(chips):
                pltpu.make_async_remote_copy(src_ref=ins[t], dst_ref=slot(t, 2 * px + py), send_sem=send.at[3 * t + r],
                                             recv_sem=recv.at[3 * t + r], device_id=(px, py, c), device_id_type=MESH).wait_recv()
        for rc in sends:
            rc.wait_send()
        for cp in local:
            cp.wait()

    any_spec = pl.BlockSpec(memory_space=pl.ANY)
    return pl.pallas_call(
        body, name="gather_shards", in_specs=[any_spec] * n, out_specs=[any_spec] * n,
        out_shape=[jax.ShapeDtypeStruct((s.shape[0], N_CHIPS) + s.shape[1:] if lm else (N_CHIPS,) + s.shape, s.dtype)
                   for s, lm in zip(shards, layer_major)],
        scratch_shapes=[pltpu.SemaphoreType.DMA((3 * n,)), pltpu.SemaphoreType.DMA((3 * n,)), pltpu.SemaphoreType.DMA((n,))],
    )(*shards)


def _scatter_grads(groups):
    flat = [(gi, li, a) for gi, grp in enumerate(groups) for li, a in enumerate(grp)]
    n = len(flat)
    ng = len(groups)

    def body(*refs):
        ins, outs = refs[:n], refs[n:n + ng]
        send, recv, lsem = refs[n + ng:]
        x, y, c, chips = _place()
        me = 2 * x + y
        local, sends = [], []
        for t, (gi, li, _) in enumerate(flat):
            cp = pltpu.make_async_copy(ins[t].at[me], outs[gi].at[me, li], lsem.at[t])
            cp.start()
            local.append(cp)
            for r, (px, py) in enumerate(chips):
                rc = pltpu.make_async_remote_copy(src_ref=ins[t].at[2 * px + py], dst_ref=outs[gi].at[me, li],
                                                  send_sem=send.at[3 * t + r], recv_sem=recv.at[3 * t + r],
                                                  device_id=(px, py, c), device_id_type=MESH)
                rc.start()
                sends.append(rc)
        for t, (gi, li, _) in enumerate(flat):
            for r, (px, py) in enumerate(chips):
                pltpu.make_async_remote_copy(src_ref=ins[t].at[me], dst_ref=outs[gi].at[2 * px + py, li],
                                             send_sem=send.at[3 * t + r], recv_sem=recv.at[3 * t + r],
                                             device_id=(px, py, c), device_id_type=MESH).wait_recv()
        for rc in sends:
            rc.wait_send()
        for cp in local:
            cp.wait()

    any_spec = pl.BlockSpec(memory_space=pl.ANY)
    return pl.pallas_call(
        body, name="scatter_grads", in_specs=[any_spec] * n, out_specs=[any_spec] * ng,
        out_shape=[jax.ShapeDtypeStruct((N_CHIPS, len(grp)) + grp[0].shape[1:], grp[0].dtype) for grp in groups],
        scratch_shapes=[pltpu.SemaphoreType.DMA((3 * n,)), pltpu.SemaphoreType.DMA((3 * n,)), pltpu.SemaphoreType.DMA((n,))],
    )(*[a for _, _, a in flat])


def _swap_with_sibling(arrs):
    n = len(arrs)

    def body(*refs):
        ins, outs = refs[:n], refs[n:2 * n]
        send, recv = refs[2 * n:]
        x, y, c, _ = _place()
        cps = []
        for t in range(n):
            rc = pltpu.make_async_remote_copy(src_ref=ins[t], dst_ref=outs[t], send_sem=send.at[t], recv_sem=recv.at[t],
                                              device_id=(x, y, 1 - c), device_id_type=MESH)
            rc.start()
            cps.append(rc)
        for rc in cps:
            rc.wait_recv()
        for rc in cps:
            rc.wait_send()

    any_spec = pl.BlockSpec(memory_space=pl.ANY)
    return pl.pallas_call(
        body, name="swap_with_sibling", in_specs=[any_spec] * n, out_specs=[any_spec] * n,
        out_shape=[jax.ShapeDtypeStruct(a.shape, a.dtype) for a in arrs],
        scratch_shapes=[pltpu.SemaphoreType.DMA((n,)), pltpu.SemaphoreType.DMA((n,))],
    )(*arrs)


def _allreduce_small(v):
    rows, cols = v.shape
    r8 = rows // (2 * N_CHIPS)
    assert r8 * 2 * N_CHIPS == rows and r8 % SUBLANES == 0, rows

    def body(v_ref, o_ref, sib_ref, cs_ref, slot_ref, send, recv):
        x, y, c, chips = _place()
        me = 2 * x + y
        sibling = (x, y, 1 - c)

        def eighth(ref, chip, core):
            return ref.at[pl.ds(pl.multiple_of((2 * chip + core) * r8, SUBLANES), r8)]

        def copy(src, dst, k, to):
            return pltpu.make_async_remote_copy(src_ref=src, dst_ref=dst, send_sem=send.at[k], recv_sem=recv.at[k],
                                                device_id=to, device_id_type=MESH)

        d2d = copy(v_ref, sib_ref, 0, sibling)
        d2d.start()
        d2d.wait_recv()
        cs_ref[...] = v_ref[...] + sib_ref[...]
        reduce_out = [copy(eighth(cs_ref, 2 * px + py, c), slot_ref.at[me], 1 + r, (px, py, c)) for r, (px, py) in enumerate(chips)]
        for cp in reduce_out:
            cp.start()
        slot_ref[me] = cs_ref[pl.ds(pl.multiple_of((2 * me + c) * r8, SUBLANES), r8), :]
        for r, (px, py) in enumerate(chips):
            copy(eighth(cs_ref, me, c), slot_ref.at[2 * px + py], 1 + r, (px, py, c)).wait_recv()
        o_ref[pl.ds(pl.multiple_of((2 * me + c) * r8, SUBLANES), r8), :] = (slot_ref[0] + slot_ref[1]) + (slot_ref[2] + slot_ref[3])
        mine = eighth(o_ref, me, c)
        hand_out = [copy(mine, mine, 4, sibling)] + [copy(mine, mine, 5 + r, (px, py, c)) for r, (px, py) in enumerate(chips)]
        for cp in hand_out:
            cp.start()
        passed_on = []
        for r, (px, py) in enumerate(chips):
            theirs = eighth(o_ref, 2 * px + py, c)
            copy(theirs, theirs, 5 + r, (px, py, c)).wait_recv()
            fw = copy(theirs, theirs, 8 + r, sibling)
            fw.start()
            passed_on.append(fw)
        sib_own = eighth(o_ref, me, 1 - c)
        copy(sib_own, sib_own, 4, sibling).wait_recv()
        for r, (px, py) in enumerate(chips):
            got = eighth(o_ref, 2 * px + py, 1 - c)
            copy(got, got, 8 + r, sibling).wait_recv()
        for cp in [d2d] + reduce_out + hand_out + passed_on:
            cp.wait_send()

    vm = pl.BlockSpec(memory_space=pltpu.VMEM)
    return pl.pallas_call(
        body, name="allreduce_small", in_specs=[vm], out_specs=vm, out_shape=jax.ShapeDtypeStruct((rows, cols), F32),
        scratch_shapes=[pltpu.VMEM((rows, cols), F32), pltpu.VMEM((rows, cols), F32), pltpu.VMEM((N_CHIPS, r8, cols), F32),
                        pltpu.SemaphoreType.DMA((11,)), pltpu.SemaphoreType.DMA((11,))],
        compiler_params=_cparams(),
    )(v)


def _pack(tensors):
    pieces = []
    for t in tensors:
        flat = t.reshape(-1)
        pad = (-flat.shape[0]) % (SUBLANES * LANES)
        pieces.append(jnp.pad(flat, (0, pad)).reshape(-1, LANES))
    rows = sum(p.shape[0] for p in pieces)
    pieces.append(jnp.zeros(((-rows) % PACK_ROW_MULTIPLE, LANES), tensors[0].dtype))
    return jnp.concatenate(pieces, axis=0)


def _unpack(buf, like):
    out, off = [], 0
    for t in like:
        size = math.prod(t.shape)
        rows = -(-size // (SUBLANES * LANES)) * SUBLANES
        out.append(buf[off:off + rows].reshape(-1)[:size].reshape(t.shape))
        off += rows
    return out


def _s5_pack_b(bb):
    gc, g, p = bb.shape
    q = S5_GROUPS_PER_BLOCK
    t = bb.reshape(gc, g // q, q, p).transpose(1, 2, 0, 3)
    eye = jnp.eye(q, dtype=bb.dtype)
    return (t[:, :, :, None, :] * eye[None, :, None, :, None]).reshape(g // q, q * gc, q * p)


def _s5_unpack_b(dbp, gc, p):
    nb = dbp.shape[0]
    q = S5_GROUPS_PER_BLOCK
    eye = jnp.eye(q, dtype=dbp.dtype)
    t = (dbp.reshape(nb, q, gc, q, p) * eye[None, :, None, :, None]).sum(axis=3)
    return t.transpose(2, 0, 1, 3).reshape(gc, nb * q, p)


def _s5_pack_c(cc):
    g, gc, p = cc.shape
    q = S5_GROUPS_PER_BLOCK
    t = cc.reshape(g // q, q, gc, p).transpose(0, 1, 3, 2)
    eye = jnp.eye(q, dtype=cc.dtype)
    return (t[:, :, :, None, :] * eye[None, :, None, :, None]).reshape(g // q, q * p, q * gc)


def _s5_unpack_c(dcp, gc, p):
    nb = dcp.shape[0]
    q = S5_GROUPS_PER_BLOCK
    eye = jnp.eye(q, dtype=dcp.dtype)
    t = (dcp.reshape(nb, q, p, q, gc) * eye[None, :, None, :, None]).sum(axis=3)
    return t.transpose(0, 1, 3, 2).reshape(nb * q, gc, p)


def _split2(m):
    return m.arr[:, 0]


def kernel(x, norm_mix_g, norm_ffn_g, norm_final_g, rg_w_in, rg_conv_w, rg_conv_b, rg_w_a, rg_b_a, rg_w_x, rg_b_x, rg_lambda, rg_w_out, s5_w_in, s5_a_re, s5_a_im, s5_log_dt, s5_b_re, s5_b_im, s5_c_re, s5_c_im, s5_d, s5_w_glu, s5_w_out, ffn_w_up, ffn_conv_w, ffn_conv_b, ffn_w_down, loss_target, m_norm_mix_g, m_norm_ffn_g, m_norm_final_g, m_rg_w_in, m_rg_conv_w, m_rg_conv_b, m_rg_w_a, m_rg_b_a, m_rg_w_x, m_rg_b_x, m_rg_lambda, m_rg_w_out, m_s5_w_in, m_s5_a_re, m_s5_a_im, m_s5_log_dt, m_s5_b_re, m_s5_b_im, m_s5_c_re, m_s5_c_im, m_s5_d, m_s5_w_glu, m_s5_w_out, m_ffn_w_up, m_ffn_conv_w, m_ffn_conv_b, m_ffn_w_down, v_norm_mix_g, v_norm_ffn_g, v_norm_final_g, v_rg_w_in, v_rg_conv_w, v_rg_conv_b, v_rg_w_a, v_rg_b_a, v_rg_w_x, v_rg_b_x, v_rg_lambda, v_rg_w_out, v_s5_w_in, v_s5_a_re, v_s5_a_im, v_s5_log_dt, v_s5_b_re, v_s5_b_im, v_s5_c_re, v_s5_c_im, v_s5_d, v_s5_w_glu, v_s5_w_out, v_ffn_w_up, v_ffn_conv_w, v_ffn_conv_b, v_ffn_w_down):
    w = dict(zip(PARAM_NAMES, (norm_mix_g, norm_ffn_g, norm_final_g, rg_w_in, rg_conv_w, rg_conv_b, rg_w_a, rg_b_a, rg_w_x, rg_b_x,
                               rg_lambda, rg_w_out, s5_w_in, s5_a_re, s5_a_im, s5_log_dt, s5_b_re, s5_b_im, s5_c_re, s5_c_im, s5_d,
                               s5_w_glu, s5_w_out, ffn_w_up, ffn_conv_w, ffn_conv_b, ffn_w_down)))
    mom = dict(zip(PARAM_NAMES, (m_norm_mix_g, m_norm_ffn_g, m_norm_final_g, m_rg_w_in, m_rg_conv_w, m_rg_conv_b, m_rg_w_a, m_rg_b_a,
                                 m_rg_w_x, m_rg_b_x, m_rg_lambda, m_rg_w_out, m_s5_w_in, m_s5_a_re, m_s5_a_im, m_s5_log_dt, m_s5_b_re,
                                 m_s5_b_im, m_s5_c_re, m_s5_c_im, m_s5_d, m_s5_w_glu, m_s5_w_out, m_ffn_w_up, m_ffn_conv_w,
                                 m_ffn_conv_b, m_ffn_w_down)))
    vel = dict(zip(PARAM_NAMES, (v_norm_mix_g, v_norm_ffn_g, v_norm_final_g, v_rg_w_in, v_rg_conv_w, v_rg_conv_b, v_rg_w_a, v_rg_b_a,
                                 v_rg_w_x, v_rg_b_x, v_rg_lambda, v_rg_w_out, v_s5_w_in, v_s5_a_re, v_s5_a_im, v_s5_log_dt, v_s5_b_re,
                                 v_s5_b_im, v_s5_c_re, v_s5_c_im, v_s5_d, v_s5_w_glu, v_s5_w_out, v_ffn_w_up, v_ffn_conv_w,
                                 v_ffn_conv_b, v_ffn_w_down)))
    _, s, d = x.shape
    depth = norm_mix_g.shape[0]
    n_grp, n_state = s5_a_re.shape[1], s5_a_re.shape[2]
    gc = s5_b_re.shape[3]
    d_ff = ffn_w_down.shape[1] * N_CHIPS
    s5_ts = min(256, s)
    s5_perm = _segment_perm(s5_ts)

    wb = {n: (w[n].astype(BF16) if n in BIG else w[n]) for n in SHARDED}
    gath = {}

    def mixer_keys(i):
        return [(n, i // 2) for n in MIXER_SHARDED[i % 2]] if i < depth else []

    def gather_side(keys):
        return _gather_side([wb[n][l] for n, l in keys])

    def put(keys, arrs):
        for k, a in zip(keys, arrs):
            gath[k] = a

    def wcol(n, l):
        return Mat(gath[(n, l)][:, None], 0, 'c')

    def wrow(n, l):
        g = gath[(n, l)]
        return Mat(g.reshape(1, 1, N_CHIPS * g.shape[1], g.shape[2]), 0, 'c')

    def rg_cw(l):
        return gath[('rg_conv_w', l)].transpose(1, 0, 2).reshape(RG_CONV_W, d)

    def s5_dv(l):
        return gath[('s5_d', l)].reshape(1, d)

    def f_cw(l):
        return gath[('ffn_conv_w', l)].transpose(1, 0, 2).reshape(FFN_CONV_W, 2, d_ff).transpose(1, 0, 2)

    tm = min(1024, s)
    tkw = min(2048, s)
    d_up = 2 * d_ff // N_CHIPS
    f_cb = ffn_conv_b.reshape(depth, 2, 1, d_ff)
    put(mixer_keys(0), _run_side("gather_first", gather_side(mixer_keys(0))))

    h = x.reshape(s, d)
    saved = []
    for i in range(depth):
        j = i // 2
        sv = {'h_in': h}
        hn = _rms_fwd(h, norm_mix_g[i:i + 1])
        sv['hn'] = hn
        up_keys = [('ffn_w_up', i), ('ffn_conv_w', i)]
        if i % 2 == 0:
            xg = _mm("rg_in", 'nn', act(hn), wcol('rg_w_in', j), out_parts=2, tm=tm, tn=512, tk=d)
            xg2 = _split2(xg)
            wa, wx = rg_w_a[j].astype(BF16), rg_w_x[j].astype(BF16)
            ba, bx = rg_b_a[j].reshape(1, d), rg_b_x[j].reshape(1, d)
            (xr, hs, y), got = _rg_fwd(xg2, rg_cw(j), rg_conv_b[j:j + 1], wa, ba, wx, bx, rg_lambda[j:j + 1],
                                       side=gather_side(up_keys))
            put(up_keys, got)
            sv.update(xg2=xg2, xr=xr, hs=hs, y=y, wa=wa, wx=wx, ba=ba, bx=bx)
            h = _mm("rg_out", 'nn', act(y), wrow('rg_w_out', j), res=act(h), tm=tm, tn=d, tk=d).arr[0, 0]
        else:
            u = _mm("s5_in", 'nn', act(hn), wrow('s5_w_in', j), tm=tm, tn=d, tk=d).arr[0, 0]
            bt_re, bt_im = s5_b_re[j].transpose(2, 0, 1), s5_b_im[j].transpose(2, 0, 1)
            ldt = s5_log_dt[j].reshape(n_grp, 1)
            tab_r, tab_i, rtab_r, rtab_i, bbr, bbi = _s5_tables3(s5_a_re[j], s5_a_im[j], ldt, bt_re, bt_im, seg=s5_ts // SUBLANES)
            nn_ = n_grp * n_state
            tab_r, tab_i, rtab_r, rtab_i = (t.reshape(5, SUBLANES, nn_) for t in (tab_r, tab_i, rtab_r, rtab_i))
            prm = dict(bp_r=_s5_pack_b(bbr).astype(BF16), bp_i=_s5_pack_b(bbi).astype(BF16),
                       cp_r=_s5_pack_c(s5_c_re[j]).astype(BF16), cp_i=_s5_pack_c(s5_c_im[j]).astype(BF16), dvec=s5_dv(j))
            (hr, hi, ypre, gy), got = _s5_fwd3(u, s5_perm, s5_perm.T, tab_r, tab_i, ts=s5_ts, side=gather_side(up_keys), **prm)
            sv.update(rtab_r=rtab_r, rtab_i=rtab_i)
            put(up_keys, got)
            gl = _mm("s5_glu", 'nn', act(gy), wcol('s5_w_glu', j), out_parts=2, tm=tm, tn=512, tk=d)
            gl2 = _split2(gl)
            o = _glu(gl2)
            sv.update(u=u, prm=prm, hr=hr, hi=hi, ypre=ypre, gy=gy, gl2=gl2, o=o, bt_re=bt_re, bt_im=bt_im, ldt=ldt)
            h = _mm("s5_out", 'nn', act(o), wrow('s5_w_out', j), res=act(h), tm=tm, tn=d, tk=d).arr[0, 0]
        sv['h_mid'] = h
        hn2 = _rms_fwd(h, norm_ffn_g[i:i + 1])
        next_keys = [('ffn_w_down', i)] + mixer_keys(i + 1)
        (up2, c2, a_ffn), got = _ffn_up_act(hn2, gath[('ffn_w_up', i)], f_cw(i), f_cb[i], side=gather_side(next_keys))
        put(next_keys, got)
        sv.update(hn2=hn2, up2=up2, c2=c2, act=a_ffn)
        h = _mm("ffn_down", 'nn', act(a_ffn), wrow('ffn_w_down', i), res=act(h), tm=tm, tn=d, tk=d_ff // 2).arr[0, 0]
        saved.append(sv)

    loss_row, dh, dg_final = _loss_and_grad(h, norm_final_g.reshape(1, d), loss_target.reshape(s, d))
    loss = lax.psum(loss_row[0, 0], ("x", "y", "c"))

    gl_ = {n: [None] * w[n].shape[0] for n in PARAM_NAMES if n != 'norm_final_g'}
    recvd = {}

    def as4(n, a):
        return a.reshape((N_CHIPS,) + w[n].shape[1:])

    def scatter_side(keys):
        arrs, shapes, views = [], [], []
        for n, l in keys:
            shape = (N_CHIPS,) + w[n].shape[1:]
            halves = False
            arrs.append(gl_[n][l] if halves else gl_[n][l].reshape(shape))
            views.append(_halves_view if halves else None)
            shapes.append(shape)
        return _scatter_side(arrs, shapes, views)

    def record(keys, arrs):
        for k, a in zip(keys, arrs):
            recvd[k] = a

    pending = None
    for i in reversed(range(depth)):
        j = i // 2
        sv = saved[i]
        gl_['ffn_w_down'][i] = _mm("ffn_down_dw", 'tn', act(sv['act']), act(dh), out_dtype=BF16, tm=d_ff // N_CHIPS, tn=d, tk=tkw).arr
        (dup2, dcw2, dcb2), got = _ffn_bwd_fused(dh, gath[('ffn_w_down', i)].reshape(d_ff, d), sv['up2'], sv['c2'], f_cw(i),
                                                 side=scatter_side(pending) if pending else None)
        if pending:
            record(pending, got)
        gl_['ffn_conv_w'][i] = dcw2.transpose(1, 0, 2).reshape(FFN_CONV_W, 2 * d_ff)
        gl_['ffn_conv_b'][i] = dcb2.reshape(2 * d_ff)
        dup = Mat(dup2[:, None], 0, 'c')
        gl_['ffn_w_up'][i] = _mm("ffn_up_dw", 'tn', act(sv['hn2']), dup, out_parts=N_CHIPS, out_dtype=BF16, tm=d, tn=d_up, tk=tkw).arr
        dh, dg = _mm_rms_bwd("ffn_up_dx", dup, wcol('ffn_w_up', i), sv['h_mid'], norm_ffn_g[i:i + 1], dh, tm=tm, tk=d_up)
        gl_['norm_ffn_g'][i] = dg[0]
        ffn_keys = [('ffn_w_up', i), ('ffn_w_down', i)]
        if i % 2 == 0:
            dy = _mm("rg_out_dx", 'nt', act(dh), wrow('rg_w_out', j), tm=tm, tn=d, tk=d).arr[0, 0]
            gl_['rg_w_out'][j] = _mm("rg_out_dw", 'tn', act(sv['y']), act(dh), out_dtype=BF16, tm=d, tn=d, tk=tkw).arr
            (dxg2, dcw, dcb, dwa, dba, dwx, dbx, dlam), got = _rg_bwd(
                dy, sv['xg2'], sv['xr'], sv['hs'], rg_cw(j), sv['wa'], sv['ba'], sv['wx'], sv['bx'], rg_lambda[j:j + 1],
                side=scatter_side(ffn_keys))
            record(ffn_keys, got)
            gl_['rg_conv_w'][j] = dcw
            gl_['rg_conv_b'][j] = dcb[0]
            gl_['rg_w_a'][j], gl_['rg_w_x'][j] = dwa, dwx
            gl_['rg_b_a'][j], gl_['rg_b_x'][j] = dba.reshape(rg_b_a.shape[1:]), dbx.reshape(rg_b_x.shape[1:])
            gl_['rg_lambda'][j] = dlam[0]
            dxg = Mat(dxg2[:, None], 0, 'c')
            gl_['rg_w_in'][j] = _mm("rg_in_dw", 'tn', act(sv['hn']), dxg, out_parts=N_CHIPS, out_dtype=BF16, tm=d, tn=512, tk=tkw).arr
            mix_dx = ("rg_in_dx", dxg, wcol('rg_w_in', j), 512)
            pending = [('rg_w_in', j), ('rg_w_out', j)]
        else:
            d_o = _mm("s5_out_dx", 'nt', act(dh), wrow('s5_w_out', j), tm=tm, tn=d, tk=d).arr[0, 0]
            gl_['s5_w_out'][j] = _mm("s5_out_dw", 'tn', act(sv['o']), act(dh), out_dtype=BF16, tm=d, tn=d, tk=tkw).arr
            dgl2 = _glu_bwd(sv['gl2'], d_o)
            dgl = Mat(dgl2[:, None], 0, 'c')
            gl_['s5_w_glu'][j] = _mm("s5_glu_dw", 'tn', act(sv['gy']), dgl, out_parts=N_CHIPS, out_dtype=BF16, tm=d, tn=512, tk=tkw).arr
            dgy = _mm("s5_glu_dx", 'nt', dgl, wcol('s5_w_glu', j), tm=tm, tn=d, tk=512).arr[0, 0]
            (du, dar, dai, dbpr, dbpi, dcpr, dcpi, dd), got = _s5_bwd3(
                dgy, sv['ypre'], sv['u'], sv['hr'], sv['hi'], s5_perm, s5_perm.T, sv['rtab_r'], sv['rtab_i'], ts=s5_ts,
                side=scatter_side(ffn_keys), **sv['prm'])
            record(ffn_keys, got)
            gl_['s5_d'][j] = dd[0]
            gl_['s5_c_re'][j] = _s5_unpack_c(dcpr, gc, n_state)
            gl_['s5_c_im'][j] = -_s5_unpack_c(dcpi, gc, n_state)
            d_are, d_aim, d_ldt, d_btr, d_bti = _s5_params_bwd(
                s5_a_re[j], s5_a_im[j], sv['ldt'], sv['bt_re'], sv['bt_im'], dar.reshape(n_grp, n_state), dai.reshape(n_grp, n_state),
                _s5_unpack_b(dbpr, gc, n_state), _s5_unpack_b(dbpi, gc, n_state))
            gl_['s5_a_re'][j], gl_['s5_a_im'][j], gl_['s5_log_dt'][j] = d_are, d_aim, d_ldt[:, 0]
            gl_['s5_b_re'][j], gl_['s5_b_im'][j] = d_btr.transpose(1, 2, 0), d_bti.transpose(1, 2, 0)
            dum = act(du)
            gl_['s5_w_in'][j] = _mm("s5_in_dw", 'tn', act(sv['hn']), dum, out_dtype=BF16, tm=d, tn=d, tk=tkw).arr
            mix_dx = ("s5_in_dx", dum, wrow('s5_w_in', j), d)
            pending = [('s5_w_in', j), ('s5_w_glu', j), ('s5_w_out', j)]
        dh, dg = _mm_rms_bwd(mix_dx[0], mix_dx[1], mix_dx[2], sv['h_in'], norm_mix_g[i:i + 1], dh, tm=tm, tk=mix_dx[3])
        gl_['norm_mix_g'][i] = dg[0]
    grad_x = dh.reshape(x.shape)
    record(pending, _run_side("scatter_last", scatter_side(pending)))

    order = sorted(BIG, key=lambda n: -math.prod(w[n].shape))
    chip_sums, theirs, prev = {}, {}, None
    for n in order:
        cols = w[n].shape[-1]
        cs, got = _sum_parts([recvd[(n, l)].reshape(N_CHIPS, -1, cols) for l in range(w[n].shape[0])],
                             side=_sibling_side([chip_sums[prev]]) if prev else None)
        chip_sums[n] = cs.reshape(-1, cols)
        if prev:
            theirs[prev] = got[0]
        prev = n
    theirs[prev] = _run_side("swap_last", _sibling_side([chip_sums[prev]]))[0]
    results = {}
    for n in BIG:
        cols = w[n].shape[-1]
        (delta, new_m, new_v, grad), _ = _adamw(w[n].reshape(-1, cols), [chip_sums[n], theirs[n]], mom[n].reshape(-1, cols),
                                                vel[n].reshape(-1, cols))
        results[n] = [o.reshape(w[n].shape) for o in (grad, delta, new_m, new_v)]

    small = REPLICATED + SMALL_SHARDED
    local = [dg_final.reshape(d) if n == 'norm_final_g' else jnp.stack(gl_[n]) for n in small]
    summed = _unpack(_allreduce_small(_pack(local)), local)
    me = 2 * lax.axis_index("x") + lax.axis_index("y")
    for n, g in zip(small, summed):
        if n in SMALL_SHARDED:
            g = lax.dynamic_slice_in_dim(g, me * w[n].shape[-1], w[n].shape[-1], axis=g.ndim - 1)
        view = (-1, w[n].shape[-1])
        (delta, new_m, new_v), _ = _adamw(w[n].reshape(view), [g.reshape(view)], mom[n].reshape(view), vel[n].reshape(view))
        results[n] = [g] + [o.reshape(w[n].shape) for o in (delta, new_m, new_v)]

    return (loss, grad_x, *[results[n][0] for n in PARAM_NAMES], *[results[n][1] for n in PARAM_NAMES],
            *[results[n][2] for n in PARAM_NAMES], *[results[n][3] for n in PARAM_NAMES])
```

```python
import math

import jax
import jax.numpy as jnp
from jax import lax
from jax.experimental import pallas as pl
from jax.experimental.pallas import tpu as pltpu

F32 = jnp.float32
BF16 = jnp.bfloat16
MESH = pl.DeviceIdType.MESH

NORM_EPS = 1e-6
RG_HEADS = 8
RG_CONV_W = 4
RG_C = 8.0
S5_GC = 16
S5_P = 64
S5_GROUPS_PER_BLOCK = 8
FFN_CONV_W = 3
N_CHIPS = 4
ADAM_LR, ADAM_B1, ADAM_B2, ADAM_EPS, ADAM_WD, ADAM_STEP = 0.001, 0.9, 0.999, 1e-08, 0.01, 10
VMEM_LIMIT_BYTES = 56 * 1024 * 1024
SUBLANES = 8
LANES = 128

PARAM_NAMES = ['norm_mix_g', 'norm_ffn_g', 'norm_final_g', 'rg_w_in', 'rg_conv_w', 'rg_conv_b', 'rg_w_a', 'rg_b_a', 'rg_w_x',
               'rg_b_x', 'rg_lambda', 'rg_w_out', 's5_w_in', 's5_a_re', 's5_a_im', 's5_log_dt', 's5_b_re', 's5_b_im', 's5_c_re',
               's5_c_im', 's5_d', 's5_w_glu', 's5_w_out', 'ffn_w_up', 'ffn_conv_w', 'ffn_conv_b', 'ffn_w_down']
SHARDED = ['rg_w_in', 'rg_conv_w', 'rg_w_out', 's5_w_in', 's5_d', 's5_w_glu', 's5_w_out', 'ffn_w_up', 'ffn_conv_w', 'ffn_w_down']
BIG = ['rg_w_in', 'rg_w_out', 's5_w_in', 's5_w_glu', 's5_w_out', 'ffn_w_up', 'ffn_w_down']
SMALL_SHARDED = ['rg_conv_w', 's5_d', 'ffn_conv_w']
MIXER_SHARDED = [['rg_w_in', 'rg_conv_w', 'rg_w_out'], ['s5_w_in', 's5_d', 's5_w_glu', 's5_w_out']]
REPLICATED = [n for n in PARAM_NAMES if n not in SHARDED]


def _cparams():
    return pltpu.CompilerParams(vmem_limit_bytes=VMEM_LIMIT_BYTES)


_GELU_C = math.sqrt(2.0 / math.pi)
_GELU_K = 0.044715


def _gelu(x):
    return 0.5 * x * (1.0 + jnp.tanh(_GELU_C * (x + _GELU_K * x * x * x)))


def _gelu_and_grad(x):
    t = jnp.tanh(_GELU_C * (x + _GELU_K * x * x * x))
    g = 0.5 * x * (1.0 + t)
    dg = 0.5 * (1.0 + t) + 0.5 * x * (1.0 - t * t) * (_GELU_C * (1.0 + 3.0 * _GELU_K * x * x))
    return g, dg


def _sigmoid(x):
    return jax.nn.sigmoid(x)


def _neg_expm1(x):
    series = -(x * (1.0 + x * (0.5 + x * (1.0 / 6 + x * (1.0 / 24 + x * (1.0 / 120 + x * (1.0 / 720)))))))
    return jnp.where(x > -0.25, series, 1.0 - jnp.exp(x))


def _softplus(z):
    return jnp.maximum(z, 0.0) + jnp.log1p(jnp.exp(-jnp.abs(z)))


def _rows(shape):
    return lax.broadcasted_iota(jnp.int32, shape, 0)


def _shift_down(x, halo, k):
    ext = jnp.concatenate([halo, x], axis=0)
    return pltpu.roll(ext, k, 0)[SUBLANES:]


def _shift_up(x, halo, k):
    ext = jnp.concatenate([x, halo], axis=0)
    n = ext.shape[0]
    return pltpu.roll(ext, n - k, 0)[:x.shape[0]]


RG_LANE_CHUNK = 512


def _real_slab_scan(a_ref, b_ref, out_ref, carry_ref, reverse):
    t, c = a_ref.shape
    nsl = t // SUBLANES
    lc = min(RG_LANE_CHUNK, c)
    row8 = _rows((SUBLANES, lc))
    for q in range(c // lc):
        sl = slice(q * lc, (q + 1) * lc)

        def slab(jj, carry, sl=sl):
            j = nsl - 1 - jj if reverse else jj
            r0 = pl.multiple_of(j * SUBLANES, SUBLANES)
            a, b = a_ref[pl.ds(r0, SUBLANES), sl], b_ref[pl.ds(r0, SUBLANES), sl]
            for k in range(3):
                sh = 1 << k
                keep = row8 < SUBLANES - sh if reverse else row8 >= sh
                amount = SUBLANES - sh if reverse else sh
                b = a * jnp.where(keep, pltpu.roll(b, amount, 0), 0.0) + b
                a = a * jnp.where(keep, pltpu.roll(a, amount, 0), 1.0)
            x = b + a * jnp.broadcast_to(carry, b.shape)
            out_ref[pl.ds(r0, SUBLANES), sl] = x
            return x[:1, :] if reverse else x[SUBLANES - 1:, :]

        carry_ref[:, sl] = lax.fori_loop(0, nsl, slab, carry_ref[:, sl], unroll=2)


class Mat:
    def __init__(self, arr, l=0, split='c'):
        assert arr.ndim == 4
        self.arr, self.l, self.split = arr, l, split
        p, _, r, c = arr.shape
        self.shape = (r, c * p) if split == 'c' else (r * p, c)

    def spec(self, tr, tc, rc):
        p, _, r, c = self.arr.shape
        l = self.l
        assert r % tr == 0 and c % tc == 0, (self.arr.shape, tr, tc)
        if self.split == 'c':
            per = c // tc
            return pl.BlockSpec((None, None, tr, tc), lambda i, j, k: (rc(i, j, k)[1] // per, l, rc(i, j, k)[0], rc(i, j, k)[1] % per))
        per = r // tr
        return pl.BlockSpec((None, None, tr, tc), lambda i, j, k: (rc(i, j, k)[0] // per, l, rc(i, j, k)[0] % per, rc(i, j, k)[1]))


def act(x, parts=1):
    s, c = x.shape
    return Mat(x.reshape(s, parts, c // parts).transpose(1, 0, 2)[:, None] if parts > 1 else x[None, None])


def _mm(name, mode, a, b, *, out_parts=1, out_split='c', out_dtype=F32, res=None, tm=512, tn=512, tk=512):
    if mode == 'nn':
        (m, kk), (kb, n) = a.shape, b.shape
    elif mode == 'nt':
        (m, kk), (n, kb) = a.shape, b.shape
    else:
        (kk, m), (kb, n) = a.shape, b.shape
    assert kk == kb, (name, a.shape, b.shape)
    tm, tn, tk = min(tm, m), min(tn, n), min(tk, kk)
    assert m % tm == 0 and n % tn == 0 and kk % tk == 0, (name, m, n, kk, tm, tn, tk)
    nk = kk // tk
    if mode == 'nn':
        a_spec = a.spec(tm, tk, lambda i, j, k: (i, k))
        b_spec = b.spec(tk, tn, lambda i, j, k: (k, j))
        dims = (((1,), (0,)), ((), ()))
    elif mode == 'nt':
        a_spec = a.spec(tm, tk, lambda i, j, k: (i, k))
        b_spec = b.spec(tn, tk, lambda i, j, k: (j, k))
        dims = (((1,), (1,)), ((), ()))
    else:
        a_spec = a.spec(tk, tm, lambda i, j, k: (k, i))
        b_spec = b.spec(tk, tn, lambda i, j, k: (k, j))
        dims = (((0,), (0,)), ((), ()))
    if out_split == 'c':
        out_arr = jax.ShapeDtypeStruct((out_parts, 1, m, n // out_parts), out_dtype)
    else:
        out_arr = jax.ShapeDtypeStruct((out_parts, 1, m // out_parts, n), out_dtype)
    out_mat = Mat(out_arr, 0, out_split)
    o_spec = out_mat.spec(tm, tn, lambda i, j, k: (i, j))
    has_res = res is not None

    def body(*refs):
        if has_res:
            a_ref, b_ref, r_ref, o_ref = refs[:4]
        else:
            a_ref, b_ref, o_ref = refs[:3]
        prod = lax.dot_general(a_ref[...].astype(BF16), b_ref[...].astype(BF16), dims, preferred_element_type=F32)

        def finish(acc):
            if has_res:
                acc = acc + r_ref[...]
            o_ref[...] = acc.astype(out_dtype)

        if nk == 1:
            finish(prod)
        else:
            acc_ref = refs[-1]
            k = pl.program_id(2)

            @pl.when(k == 0)
            def _():
                acc_ref[...] = prod

            @pl.when(k > 0)
            def _():
                acc_ref[...] += prod

            @pl.when(k == nk - 1)
            def _():
                finish(acc_ref[...])

    in_specs = [a_spec, b_spec]
    args = [a.arr, b.arr]
    if has_res:
        in_specs.append(res.spec(tm, tn, lambda i, j, k: (i, j)))
        args.append(res.arr)
    out = pl.pallas_call(
        body, name=name, grid=(m // tm, n // tn, nk), in_specs=in_specs, out_specs=o_spec, out_shape=out_arr,
        scratch_shapes=[pltpu.VMEM((tm, tn), F32)] if nk > 1 else [], compiler_params=_cparams(),
    )(*args)
    return Mat(out, 0, out_split)


def _rms_fwd(h, g, ts=512, side=None):
    s, d = h.shape
    ts = min(ts, s)
    nt = s // ts

    def body(h_ref, g_ref, o_ref):
        x = h_ref[...]
        var = jnp.mean(x * x, axis=-1, keepdims=True)
        o_ref[...] = (x * lax.rsqrt(var + NORM_EPS) * g_ref[...]).astype(BF16)

    outs, got = _call_with_side(
        body, side, lambda: pl.program_id(0) == 0, lambda: pl.program_id(0) == nt - 1,
        name="rms_fwd", grid=(nt,),
        in_specs=[pl.BlockSpec((ts, d), lambda i: (i, 0)), pl.BlockSpec((1, d), lambda i: (0, 0))],
        out_specs=[pl.BlockSpec((ts, d), lambda i: (i, 0))], out_shape=[jax.ShapeDtypeStruct((s, d), BF16)],
        scratch_shapes=[], args=(h, g))
    return outs[0], got


def _loss_and_grad(h, g, tgt, ts=512):
    s, d = h.shape
    ts = min(ts, s)

    def body(h_ref, g_ref, t_ref, loss_ref, dh_ref, dg_ref):
        i = pl.program_id(0)
        x = h_ref[...]
        gv = g_ref[...]
        rstd = lax.rsqrt(jnp.mean(x * x, axis=-1, keepdims=True) + NORM_EPS)
        xhat = x * rstd
        err = xhat * gv - t_ref[...]
        dy = err * (1.0 / d)
        dxh = dy * gv
        dh_ref[...] = rstd * (dxh - xhat * jnp.mean(dxh * xhat, axis=-1, keepdims=True))
        part = jnp.sum(dy * xhat, axis=0, keepdims=True)
        lpart = jnp.broadcast_to(jnp.sum(jnp.sum(err * err, axis=0, keepdims=True), axis=1, keepdims=True) * (0.5 / d), (1, LANES))

        @pl.when(i == 0)
        def _():
            dg_ref[...] = part
            loss_ref[...] = lpart

        @pl.when(i > 0)
        def _():
            dg_ref[...] += part
            loss_ref[...] += lpart

    row = pl.BlockSpec((ts, d), lambda i: (i, 0))
    vec = pl.BlockSpec((1, d), lambda i: (0, 0))
    return pl.pallas_call(
        body, name="loss_and_grad", grid=(s // ts,), in_specs=[row, vec, row],
        out_specs=[pl.BlockSpec((1, LANES), lambda i: (0, 0)), row, vec],
        out_shape=[jax.ShapeDtypeStruct((1, LANES), F32), jax.ShapeDtypeStruct((s, d), F32), jax.ShapeDtypeStruct((1, d), F32)],
        compiler_params=_cparams(),
    )(h, g, tgt)


def _mm_rms_bwd(name, a, b, h, g, dh_in, *, tm, tk, side=None):
    (m, kk), (n, kb) = a.shape, b.shape
    assert kk == kb and h.shape == (m, n), (name, a.shape, b.shape, h.shape)
    tm, tk = min(tm, m), min(tk, kk)
    nk = kk // tk
    dims = (((1,), (1,)), ((), ()))

    def body(a_ref, b_ref, h_ref, g_ref, dhin_ref, dh_ref, dg_ref, *acc):
        i, k = pl.program_id(0), pl.program_id(2)
        prod = lax.dot_general(a_ref[...].astype(BF16), b_ref[...].astype(BF16), dims, preferred_element_type=F32)

        def finish(dhn):
            x = h_ref[...]
            rstd = lax.rsqrt(jnp.mean(x * x, axis=-1, keepdims=True) + NORM_EPS)
            xhat = x * rstd
            dxh = dhn * g_ref[...]
            dh_ref[...] = dhin_ref[...] + rstd * (dxh - xhat * jnp.mean(dxh * xhat, axis=-1, keepdims=True))
            part = jnp.sum(dhn * xhat, axis=0, keepdims=True)

            @pl.when(i == 0)
            def _():
                dg_ref[...] = part

            @pl.when(i > 0)
            def _():
                dg_ref[...] += part

        if nk == 1:
            finish(prod)
        else:
            acc_ref = acc[0]

            @pl.when(k == 0)
            def _():
                acc_ref[...] = prod

            @pl.when(k > 0)
            def _():
                acc_ref[...] += prod

            @pl.when(k == nk - 1)
            def _():
                finish(acc_ref[...])

    row = pl.BlockSpec((tm, n), lambda i, j, k: (i, 0))
    vec = pl.BlockSpec((1, n), lambda i, j, k: (0, 0))
    ni = m // tm
    return _call_with_side(
        body, side, lambda: (pl.program_id(0) == 0) & (pl.program_id(2) == 0),
        lambda: (pl.program_id(0) == ni - 1) & (pl.program_id(2) == nk - 1),
        name=name, grid=(ni, 1, nk),
        in_specs=[a.spec(tm, tk, lambda i, j, k: (i, k)), b.spec(n, tk, lambda i, j, k: (0, k)), row, vec, row],
        out_specs=[row, vec], out_shape=[jax.ShapeDtypeStruct((m, n), F32), jax.ShapeDtypeStruct((1, n), F32)],
        scratch_shapes=[pltpu.VMEM((tm, n), F32)] if nk > 1 else [], args=(a.arr, b.arr, h, g, dh_in))


def _ffn_up_act(hn2, w_up4, conv_w2, conv_b2, ts=1024, tn=512, sub=1024, side=None):
    s, d = hn2.shape
    p, _, wc = w_up4.shape
    f = p * wc // 2
    ts, tn = min(ts, s), min(tn, wc)
    sub = min(sub, ts)
    per = wc // tn
    kw = FFN_CONV_W
    g0, g1 = f // tn, s // ts

    def body(hn_ref, w1_ref, w2_ref, cw_ref, cb_ref, up_ref, c_ref, act_ref, carry_ref):
        @pl.when(pl.program_id(1) == 0)
        def _():
            carry_ref[...] = jnp.zeros_like(carry_ref)

        for q in range(ts // sub):
            rows = slice(q * sub, (q + 1) * sub)
            hn = hn_ref[rows, :]
            cs = []
            for h, w_ref in enumerate((w1_ref, w2_ref)):
                x = jnp.dot(hn, w_ref[...], preferred_element_type=F32)
                up_ref[h, rows, :] = x
                halo = carry_ref[h]
                c = cb_ref[h] + cw_ref[h, kw - 1:kw, :] * x
                for sft in range(1, kw):
                    c = c + cw_ref[h, kw - 1 - sft:kw - sft, :] * _shift_down(x, halo, sft)
                carry_ref[h] = x[sub - SUBLANES:, :]
                c_ref[h, rows, :] = c
                cs.append(c)
            act_ref[rows, :] = (_gelu(cs[0]) * cs[1]).astype(BF16)

    outs, side_outs = _call_with_side(
        body, side, lambda: (pl.program_id(0) == 0) & (pl.program_id(1) == 0),
        lambda: (pl.program_id(0) == g0 - 1) & (pl.program_id(1) == g1 - 1),
        name="ffn_up_act", grid=(g0, g1),
        in_specs=[pl.BlockSpec((ts, d), lambda j, i: (i, 0)),
                  pl.BlockSpec((None, d, tn), lambda j, i: (j // per, 0, j % per)),
                  pl.BlockSpec((None, d, tn), lambda j, i: (p // 2 + j // per, 0, j % per)),
                  pl.BlockSpec((2, kw, tn), lambda j, i: (0, 0, j)),
                  pl.BlockSpec((2, 1, tn), lambda j, i: (0, 0, j))],
        out_specs=[pl.BlockSpec((2, ts, tn), lambda j, i: (0, i, j)), pl.BlockSpec((2, ts, tn), lambda j, i: (0, i, j)),
                   pl.BlockSpec((ts, tn), lambda j, i: (i, j))],
        out_shape=[jax.ShapeDtypeStruct((2, s, f), F32), jax.ShapeDtypeStruct((2, s, f), F32), jax.ShapeDtypeStruct((s, f), BF16)],
        scratch_shapes=[pltpu.VMEM((2, SUBLANES, tn), F32)], args=(hn2, w_up4, w_up4, conv_w2, conv_b2))
    return outs, side_outs


def _ffn_bwd_fused(dh, w_down, up2, c2, conv_w2, ts=1024, tn=512, side=None):
    s, d = dh.shape
    _, _, f = up2.shape
    ts, tn = min(ts, s), min(tn, f)
    kw = FFN_CONV_W
    nt = s // ts
    hb = ts // SUBLANES
    g0 = f // tn
    nt_dims = (((1,), (1,)), ((), ()))

    def body(dh_ref, wd_ref, up_ref, c_ref, w_ref, dup_ref, dw_ref, db_ref, carry_ref):
        i = pl.program_id(1)
        first_step = i == 0

        @pl.when(first_step)
        def _():
            carry_ref[...] = jnp.zeros_like(carry_ref)

        da = lax.dot_general(dh_ref[...].astype(BF16), wd_ref[...], nt_dims, preferred_element_type=F32)
        g1, dg1 = _gelu_and_grad(c_ref[0])
        dcs = [da * c_ref[1] * dg1, da * g1]
        for h in range(2):
            dc = dcs[h]
            after = carry_ref[h]
            ups = [dc] + [_shift_up(dc, after, sft) for sft in range(1, kw)]
            dup = w_ref[h, kw - 1:kw, :] * dc
            for sft in range(1, kw):
                dup = dup + w_ref[h, kw - 1 - sft:kw - sft, :] * ups[sft]
            carry_ref[h] = dc[:SUBLANES]
            dup_ref[h] = dup.astype(BF16)
            dbp = jnp.sum(dc, axis=0, keepdims=True)
            x = up_ref[h]
            dwp = [jnp.sum(ups[kw - 1 - k] * x, axis=0, keepdims=True) for k in range(kw)]

            @pl.when(first_step)
            def _():
                db_ref[h] = dbp
                for k in range(kw):
                    dw_ref[h, k:k + 1, :] = dwp[k]

            @pl.when(i > 0)
            def _():
                db_ref[h] += dbp
                for k in range(kw):
                    dw_ref[h, k:k + 1, :] += dwp[k]

    rev = lambda i: nt - 1 - i
    return _call_with_side(
        body, side, lambda: (pl.program_id(0) == 0) & (pl.program_id(1) == 0),
        lambda: (pl.program_id(0) == g0 - 1) & (pl.program_id(1) == nt - 1),
        name="ffn_bwd", grid=(g0, nt),
        in_specs=[pl.BlockSpec((ts, d), lambda j, i: (rev(i), 0)),
                  pl.BlockSpec((tn, d), lambda j, i: (j, 0)),
                  pl.BlockSpec((2, ts, tn), lambda j, i: (0, rev(i), j)),
                  pl.BlockSpec((2, ts, tn), lambda j, i: (0, rev(i), j)),
                  pl.BlockSpec((2, kw, tn), lambda j, i: (0, 0, j))],
        out_specs=[pl.BlockSpec((2, ts, tn), lambda j, i: (0, rev(i), j)),
                   pl.BlockSpec((2, kw, tn), lambda j, i: (0, 0, j)),
                   pl.BlockSpec((2, 1, tn), lambda j, i: (0, 0, j))],
        out_shape=[jax.ShapeDtypeStruct((2, s, f), BF16), jax.ShapeDtypeStruct((2, kw, f), F32),
                   jax.ShapeDtypeStruct((2, 1, f), F32)],
        scratch_shapes=[pltpu.VMEM((2, SUBLANES, tn), F32)], args=(dh, w_down, up2, c2, conv_w2))


def _rg_gates(xr, wa_ref, ba_ref, wx_ref, bx_ref, lam_ref):
    bw = wa_ref.shape[-1]
    xb = xr.astype(BF16)
    za = jnp.concatenate([jnp.dot(xb[:, h * bw:(h + 1) * bw], wa_ref[h], preferred_element_type=F32)
                          for h in range(RG_HEADS)], axis=1) + ba_ref[...]
    zx = jnp.concatenate([jnp.dot(xb[:, h * bw:(h + 1) * bw], wx_ref[h], preferred_element_type=F32)
                          for h in range(RG_HEADS)], axis=1) + bx_ref[...]
    r, ig = _sigmoid(za), _sigmoid(zx)
    sp = _softplus(-lam_ref[...])
    la = -RG_C * r * sp
    a = jnp.exp(la)
    mult = jnp.sqrt(_neg_expm1(2.0 * la))
    return xb, r, ig, sp, a, mult


def _rg_fwd(xg2, conv_w, conv_b, w_a, b_a, w_x, b_x, lam, ts=256, side=None):
    _, s, c = xg2.shape
    ts = min(ts, s)
    kw = RG_CONV_W
    hb = ts // SUBLANES

    def body(xg_ref, halo_ref, cw_ref, cb_ref, wa_ref, ba_ref, wx_ref, bx_ref, lam_ref, xr_ref, hs_ref, y_ref, carry_ref,
             a_scr, b_scr):
        i = pl.program_id(0)

        @pl.when(i == 0)
        def _():
            carry_ref[...] = jnp.zeros_like(carry_ref)

        xp = xg_ref[0]
        halo = jnp.where(i == 0, 0.0, halo_ref[...])
        xr = cb_ref[...] + cw_ref[kw - 1:kw, :] * xp
        for sft in range(1, kw):
            xr = xr + cw_ref[kw - 1 - sft:kw - sft, :] * _shift_down(xp, halo, sft)
        _, r, ig, sp, a, mult = _rg_gates(xr, wa_ref, ba_ref, wx_ref, bx_ref, lam_ref)
        a_scr[...] = a
        b_scr[...] = mult * (ig * xr)
        _real_slab_scan(a_scr, b_scr, hs_ref, carry_ref, reverse=False)
        xr_ref[...] = xr
        y_ref[...] = (hs_ref[...] * _gelu(xg_ref[1])).astype(BF16)

    full = lambda shape: pl.BlockSpec(shape, lambda i: (0,) * len(shape))
    row_spec = pl.BlockSpec((ts, c), lambda i: (i, 0))
    nt = s // ts
    return _call_with_side(
        body, side, lambda: pl.program_id(0) == 0, lambda: pl.program_id(0) == nt - 1,
        name="rg_fwd", grid=(nt,),
        in_specs=[pl.BlockSpec((2, ts, c), lambda i: (0, i, 0)),
                  pl.BlockSpec((None, SUBLANES, c), lambda i: (0, jnp.maximum(i * hb - 1, 0), 0)),
                  full(conv_w.shape), full(conv_b.shape), full(w_a.shape), full(b_a.shape), full(w_x.shape), full(b_x.shape),
                  full(lam.shape)],
        out_specs=[row_spec, row_spec, row_spec],
        out_shape=[jax.ShapeDtypeStruct((s, c), F32), jax.ShapeDtypeStruct((s, c), F32), jax.ShapeDtypeStruct((s, c), BF16)],
        scratch_shapes=[pltpu.VMEM((1, c), F32), pltpu.VMEM((ts, c), F32), pltpu.VMEM((ts, c), F32)],
        args=(xg2, xg2, conv_w, conv_b, w_a, b_a, w_x, b_x, lam))


def _rg_bwd(dy, xg2, xr, hs, conv_w, w_a, b_a, w_x, b_x, lam, ts=256, side=None):
    _, s, c = xg2.shape
    ts = min(ts, s)
    nt = s // ts
    kw = RG_CONV_W
    hb = ts // SUBLANES
    bw = c // RG_HEADS
    tn_dims = (((0,), (0,)), ((), ()))
    nt_dims = (((1,), (1,)), ((), ()))

    def body(dy_ref, xg_ref, xph_ref, xr_ref, hs_ref, hsh_ref, cw_ref, wa_ref, ba_ref, wx_ref, bx_ref, lam_ref,
             dxg_ref, dcw_ref, dcb_ref, dwa_ref, dba_ref, dwx_ref, dbx_ref, dlam_ref,
             lam_carry, a_carry, dxr_carry, dsp_acc, a_scr, b_scr):
        i = pl.program_id(0)
        first_step = i == 0
        time_first = i == nt - 1

        @pl.when(first_step)
        def _():
            lam_carry[...] = jnp.zeros_like(lam_carry)
            a_carry[...] = jnp.ones_like(a_carry)
            dxr_carry[...] = jnp.zeros_like(dxr_carry)
            dsp_acc[...] = jnp.zeros_like(dsp_acc)
            for ref in (dcw_ref, dcb_ref, dwa_ref, dba_ref, dwx_ref, dbx_ref):
                ref[...] = jnp.zeros_like(ref)

        xr = xr_ref[...]
        hs = hs_ref[...]
        gate = xg_ref[1]
        xb, r, ig, sp, a, mult = _rg_gates(xr, wa_ref, ba_ref, wx_ref, bx_ref, lam_ref)
        dyv = dy_ref[...]
        gg, dgg = _gelu_and_grad(gate)
        dhs = dyv * gg
        dxg_ref[1] = (dyv * hs * dgg).astype(BF16)
        row = _rows(xr.shape)
        a_scr[...] = jnp.where(row == ts - 1, a_carry[0:1, :], pltpu.roll(a, ts - 1, 0))
        b_scr[...] = dhs
        _real_slab_scan(a_scr, b_scr, b_scr, lam_carry, reverse=True)
        lmb = b_scr[...]
        a_carry[...] = a[:SUBLANES]
        hs_prev = _shift_down(hs, jnp.where(time_first, 0.0, hsh_ref[...]), 1)
        d_a = lmb * hs_prev
        d_m = lmb * (ig * xr)
        d_ig = lmb * mult * xr
        d_xr = lmb * mult * ig
        d_la = a * d_a - (a * a / mult) * d_m
        dsp_acc[...] += jnp.sum(-RG_C * r * d_la, axis=0, keepdims=True)
        d_za = (-RG_C * sp) * d_la * r * (1.0 - r)
        d_zx = d_ig * ig * (1.0 - ig)
        dba_ref[...] += jnp.sum(d_za, axis=0, keepdims=True)
        dbx_ref[...] += jnp.sum(d_zx, axis=0, keepdims=True)
        dzab, dzxb = d_za.astype(BF16), d_zx.astype(BF16)
        back = []
        for h in range(RG_HEADS):
            sl = slice(h * bw, (h + 1) * bw)
            dwa_ref[h] += lax.dot_general(xb[:, sl], dzab[:, sl], tn_dims, preferred_element_type=F32)
            dwx_ref[h] += lax.dot_general(xb[:, sl], dzxb[:, sl], tn_dims, preferred_element_type=F32)
            back.append(lax.dot_general(dzab[:, sl], wa_ref[h], nt_dims, preferred_element_type=F32)
                        + lax.dot_general(dzxb[:, sl], wx_ref[h], nt_dims, preferred_element_type=F32))
        d_xr = d_xr + jnp.concatenate(back, axis=1)
        d_xp = cw_ref[kw - 1:kw, :] * d_xr
        after = dxr_carry[...]
        for sft in range(1, kw):
            d_xp = d_xp + cw_ref[kw - 1 - sft:kw - sft, :] * _shift_up(d_xr, after, sft)
        dxr_carry[...] = d_xr[:SUBLANES]
        dxg_ref[0] = d_xp.astype(BF16)
        xp = xg_ref[0]
        before = jnp.where(time_first, 0.0, xph_ref[...])
        dcb_ref[...] += jnp.sum(d_xr, axis=0, keepdims=True)
        dcw_ref[kw - 1:kw, :] += jnp.sum(d_xr * xp, axis=0, keepdims=True)
        for sft in range(1, kw):
            dcw_ref[kw - 1 - sft:kw - sft, :] += jnp.sum(d_xr * _shift_down(xp, before, sft), axis=0, keepdims=True)
        dlam_ref[...] = dsp_acc[...] * (-_sigmoid(-lam_ref[...]))

    full = lambda shape: pl.BlockSpec(shape, lambda i: (0,) * len(shape))
    rev = lambda i: nt - 1 - i
    row_spec = pl.BlockSpec((ts, c), lambda i: (rev(i), 0))
    halo_idx = lambda i: jnp.maximum(rev(i) * hb - 1, 0)
    vec = (1, c)
    return _call_with_side(
        body, side, lambda: pl.program_id(0) == 0, lambda: pl.program_id(0) == nt - 1,
        name="rg_bwd", grid=(nt,),
        in_specs=[row_spec,
                  pl.BlockSpec((2, ts, c), lambda i: (0, rev(i), 0)),
                  pl.BlockSpec((None, SUBLANES, c), lambda i: (0, halo_idx(i), 0)),
                  row_spec, row_spec,
                  pl.BlockSpec((SUBLANES, c), lambda i: (halo_idx(i), 0)),
                  full(conv_w.shape), full(w_a.shape), full(b_a.shape), full(w_x.shape), full(b_x.shape), full(lam.shape)],
        out_specs=[pl.BlockSpec((2, ts, c), lambda i: (0, rev(i), 0)), full(conv_w.shape), full(vec), full(w_a.shape), full(vec),
                   full(w_x.shape), full(vec), full(vec)],
        out_shape=[jax.ShapeDtypeStruct((2, s, c), BF16), jax.ShapeDtypeStruct(conv_w.shape, F32), jax.ShapeDtypeStruct(vec, F32),
                   jax.ShapeDtypeStruct(w_a.shape, F32), jax.ShapeDtypeStruct(vec, F32), jax.ShapeDtypeStruct(w_x.shape, F32),
                   jax.ShapeDtypeStruct(vec, F32), jax.ShapeDtypeStruct(vec, F32)],
        scratch_shapes=[pltpu.VMEM(vec, F32), pltpu.VMEM((SUBLANES, c), F32), pltpu.VMEM((SUBLANES, c), F32),
                        pltpu.VMEM(vec, F32), pltpu.VMEM((ts, c), F32), pltpu.VMEM((ts, c), F32)],
        args=(dy, xg2, xg2, xr, hs, hs, conv_w, w_a, b_a, w_x, b_x, lam))


def _s5_param_fn(a_re, a_im, log_dt, bt_re, bt_im):
    dt = jnp.exp(log_dt)
    mag = jnp.exp(a_re * dt)
    abr = mag * jnp.cos(a_im * dt)
    abi = mag * jnp.sin(a_im * dt)
    ur, ui = abr - 1.0, abi
    den = a_re * a_re + a_im * a_im
    wr = (ur * a_re + ui * a_im) / den
    wi = (ui * a_re - ur * a_im) / den
    bbr = wr[None] * bt_re - wi[None] * bt_im
    bbi = wr[None] * bt_im + wi[None] * bt_re
    return abr, abi, bbr, bbi


def _s5_params_bwd(a_re, a_im, log_dt, bt_re, bt_im, d_abr, d_abi, d_bbr, d_bbi):
    def body(ar_ref, ai_ref, dt_ref, br_ref, bi_ref, g0, g1, g2, g3, o0, o1, o2, o3, o4):
        _, vjp = jax.vjp(_s5_param_fn, ar_ref[...], ai_ref[...], dt_ref[...], br_ref[...], bi_ref[...])
        outs = vjp((g0[...], g1[...], g2[...], g3[...]))
        for o, v in zip((o0, o1, o2, o3, o4), outs):
            o[...] = v

    sd = jax.ShapeDtypeStruct
    return pl.pallas_call(
        body, name="s5_params_bwd",
        out_shape=[sd(a_re.shape, F32), sd(a_im.shape, F32), sd(log_dt.shape, F32), sd(bt_re.shape, F32), sd(bt_im.shape, F32)],
    )(a_re, a_im, log_dt, bt_re, bt_im, d_abr, d_abi, d_bbr, d_bbi)


S5_LANE_CHUNK = 512


def _cmul_add(br, bi, tr, ti, sr, si):
    return br + tr * sr - ti * si, bi + tr * si + ti * sr


def _s5_tables3(a_re, a_im, log_dt, bt_re, bt_im, seg):
    g, p = a_re.shape
    gc = bt_re.shape[0]
    nsq = int(math.log2(seg))
    assert 1 << nsq == seg

    def body(ar_ref, ai_ref, dt_ref, br_ref, bi_ref, tr_ref, ti_ref, rtr_ref, rti_ref, bbr_ref, bbi_ref):
        abr, abi, bbr, bbi = _s5_param_fn(ar_ref[...], ai_ref[...], dt_ref[...], br_ref[...], bi_ref[...])
        bbr_ref[...] = bbr
        bbi_ref[...] = bbi
        qr, qi = abr, abi
        for _ in range(nsq):
            qr, qi = qr * qr - qi * qi, 2.0 * qr * qi
        pows = [(qr, qi)]
        for _ in range(1, SUBLANES):
            cr, ci = pows[-1]
            pows.append((cr * qr - ci * qi, cr * qi + ci * qr))
        zero = jnp.zeros_like(abr)
        for r in range(SUBLANES):
            rows = [(pows[(1 << k) - 1] if r >= (1 << k) else (zero, zero)) for k in range(3)] + [pows[r], (abr, abi)]
            for k, (vr, vi) in enumerate(rows):
                tr_ref[k, r] = vr
                ti_ref[k, r] = vi
                rtr_ref[k, SUBLANES - 1 - r] = vr
                rti_ref[k, SUBLANES - 1 - r] = -vi

    sd = jax.ShapeDtypeStruct
    tab = sd((5, SUBLANES, g, p), F32)
    return pl.pallas_call(
        body, name="s5_tables", out_shape=[tab, tab, tab, tab, sd((gc, g, p), F32), sd((gc, g, p), F32)],
    )(a_re, a_im, log_dt, bt_re, bt_im)


def _segment_perm(ts):
    seg = ts // SUBLANES
    rho = jnp.arange(ts)
    src = (rho % SUBLANES) * seg + rho // SUBLANES
    return (src[:, None] == jnp.arange(ts)[None, :]).astype(BF16)


def _exact_rows(perm_t, x):
    hi = x.astype(BF16)
    r1 = x - hi.astype(F32)
    mid = r1.astype(BF16)
    lo = (r1 - mid.astype(F32)).astype(BF16)
    dot = lambda v: jnp.dot(perm_t, v, preferred_element_type=F32)
    return (dot(hi) + dot(mid)) + dot(lo)


def _s5_fwd3(u, perm, perm_t, tab_r, tab_i, bp_r, bp_i, cp_r, cp_i, dvec, ts=256, side=None):
    s, c = u.shape
    n = tab_r.shape[2]
    nblk, cb, nb = bp_r.shape
    ts = min(ts, s)
    seg = ts // SUBLANES
    lc = min(S5_LANE_CHUNK, n)

    def body(u_ref, p_ref, pt_ref, tr_ref, ti_ref, bpr_ref, bpi_ref, cpr_ref, cpi_ref, d_ref, hr_ref, hi_ref, yp_ref, gy_ref,
             bur_ref, bui_ref, car_r, car_i):
        i = pl.program_id(0)

        @pl.when(i == 0)
        def _():
            car_r[...] = jnp.zeros_like(car_r)
            car_i[...] = jnp.zeros_like(car_i)

        uv = u_ref[...]
        ubp = jnp.dot(p_ref[...], uv.astype(BF16), preferred_element_type=F32).astype(BF16)
        for k in range(nblk):
            bur_ref[:, k * nb:(k + 1) * nb] = jnp.dot(ubp[:, k * cb:(k + 1) * cb], bpr_ref[k], preferred_element_type=F32)
            bui_ref[:, k * nb:(k + 1) * nb] = jnp.dot(ubp[:, k * cb:(k + 1) * cb], bpi_ref[k], preferred_element_type=F32)
        row8 = _rows((SUBLANES, lc))
        for q in range(n // lc):
            sl = slice(q * lc, (q + 1) * lc)
            tabs = [(tr_ref[k, :, sl], ti_ref[k, :, sl]) for k in range(5)]
            a_r, a_i = tabs[4]

            def local(r, carry, sl=sl, a_r=a_r, a_i=a_i):
                r0 = pl.multiple_of(r * SUBLANES, SUBLANES)
                hr, hi = _cmul_add(bur_ref[pl.ds(r0, SUBLANES), sl], bui_ref[pl.ds(r0, SUBLANES), sl], a_r, a_i, carry[0], carry[1])
                hr_ref[pl.ds(r0, SUBLANES), sl] = hr
                hi_ref[pl.ds(r0, SUBLANES), sl] = hi
                return hr, hi

            zero = jnp.zeros((SUBLANES, lc), F32)
            er, ei = lax.fori_loop(0, seg, local, (zero, zero), unroll=4)
            for k in range(3):
                sh = 1 << k
                er, ei = _cmul_add(er, ei, tabs[k][0], tabs[k][1], pltpu.roll(er, sh, 0), pltpu.roll(ei, sh, 0))
            cin_r, cin_i = jnp.broadcast_to(car_r[:, sl], er.shape), jnp.broadcast_to(car_i[:, sl], ei.shape)
            er, ei = _cmul_add(er, ei, tabs[3][0], tabs[3][1], cin_r, cin_i)
            car_r[:, sl] = er[SUBLANES - 1:, :]
            car_i[:, sl] = ei[SUBLANES - 1:, :]
            c_r = jnp.where(row8 == 0, cin_r, pltpu.roll(er, 1, 0))
            c_i = jnp.where(row8 == 0, cin_i, pltpu.roll(ei, 1, 0))

            def fix(r, carry, sl=sl, a_r=a_r, a_i=a_i, c_r=c_r, c_i=c_i):
                pr, pi = carry
                r0 = pl.multiple_of(r * SUBLANES, SUBLANES)
                hr, hi = _cmul_add(hr_ref[pl.ds(r0, SUBLANES), sl], hi_ref[pl.ds(r0, SUBLANES), sl], pr, pi, c_r, c_i)
                hr_ref[pl.ds(r0, SUBLANES), sl] = hr
                hi_ref[pl.ds(r0, SUBLANES), sl] = hi
                return pr * a_r - pi * a_i, pr * a_i + pi * a_r

            lax.fori_loop(0, seg, fix, (a_r, a_i), unroll=4)
        hrb, hib = hr_ref[...].astype(BF16), hi_ref[...].astype(BF16)
        y = jnp.concatenate([jnp.dot(hrb[:, k * nb:(k + 1) * nb], cpr_ref[k], preferred_element_type=F32)
                             - jnp.dot(hib[:, k * nb:(k + 1) * nb], cpi_ref[k], preferred_element_type=F32) for k in range(nblk)], axis=1)
        yp = _exact_rows(pt_ref[...], y) + d_ref[...] * uv
        yp_ref[...] = yp
        gy_ref[...] = _gelu(yp).astype(BF16)

    full = lambda shape: pl.BlockSpec(shape, lambda i: (0,) * len(shape))
    rc = pl.BlockSpec((ts, c), lambda i: (i, 0))
    rn = pl.BlockSpec((ts, n), lambda i: (i, 0))
    sd = jax.ShapeDtypeStruct
    nt = s // ts
    return _call_with_side(
        body, side, lambda: pl.program_id(0) == 0, lambda: pl.program_id(0) == nt - 1,
        name="s5_fwd", grid=(nt,),
        in_specs=[rc, full(perm.shape), full(perm_t.shape), full(tab_r.shape), full(tab_i.shape), full(bp_r.shape), full(bp_i.shape),
                  full(cp_r.shape), full(cp_i.shape), full(dvec.shape)],
        out_specs=[rn, rn, rc, rc],
        out_shape=[sd((s, n), F32), sd((s, n), F32), sd((s, c), F32), sd((s, c), BF16)],
        scratch_shapes=[pltpu.VMEM((ts, n), F32), pltpu.VMEM((ts, n), F32), pltpu.VMEM((1, n), F32), pltpu.VMEM((1, n), F32)],
        args=(u, perm, perm_t, tab_r, tab_i, bp_r, bp_i, cp_r, cp_i, dvec))


def _s5_bwd3(dgy, ypre, u, hr, hi, perm, perm_t, rtab_r, rtab_i, bp_r, bp_i, cp_r, cp_i, dvec, ts=256, side=None):
    s, c = u.shape
    n = rtab_r.shape[2]
    nblk, cb, nb = bp_r.shape
    ts = min(ts, s)
    nt = s // ts
    hb = ts // SUBLANES
    seg = ts // SUBLANES
    lc = min(S5_LANE_CHUNK, n)
    tn_dims = (((0,), (0,)), ((), ()))
    nt_dims = (((1,), (1,)), ((), ()))

    def body(dgy_ref, yp_ref, u_ref, hr_ref, hi_ref, hrh_ref, hih_ref, p_ref, pt_ref, tr_ref, ti_ref, bpr_ref, bpi_ref,
             cpr_ref, cpi_ref, d_ref, du_ref, dar_ref, dai_ref, dbr_ref, dbi_ref, dcr_ref, dci_ref, dd_ref, lr_ref, li_ref,
             car_r, car_i):
        i = pl.program_id(0)
        time_first = i == nt - 1

        @pl.when(i == 0)
        def _():
            car_r[...] = jnp.zeros_like(car_r)
            car_i[...] = jnp.zeros_like(car_i)
            for ref in (dar_ref, dai_ref, dbr_ref, dbi_ref, dcr_ref, dci_ref, dd_ref):
                ref[...] = jnp.zeros_like(ref)

        uv = u_ref[...]
        _, dgel = _gelu_and_grad(yp_ref[...])
        dyv = dgy_ref[...] * dgel
        dd_ref[...] += jnp.sum(dyv * uv, axis=0, keepdims=True)
        perm_m = p_ref[...]
        dyb = jnp.dot(perm_m, dyv.astype(BF16), preferred_element_type=F32).astype(BF16)
        ub = jnp.dot(perm_m, uv.astype(BF16), preferred_element_type=F32).astype(BF16)
        hrb, hib = hr_ref[...].astype(BF16), hi_ref[...].astype(BF16)
        for k in range(nblk):
            dblk = dyb[:, k * cb:(k + 1) * cb]
            lr_ref[:, k * nb:(k + 1) * nb] = lax.dot_general(dblk, cpr_ref[k], nt_dims, preferred_element_type=F32)
            li_ref[:, k * nb:(k + 1) * nb] = -lax.dot_general(dblk, cpi_ref[k], nt_dims, preferred_element_type=F32)
            dcr_ref[k] += lax.dot_general(hrb[:, k * nb:(k + 1) * nb], dblk, tn_dims, preferred_element_type=F32)
            dci_ref[k] += lax.dot_general(hib[:, k * nb:(k + 1) * nb], dblk, tn_dims, preferred_element_type=F32)
        row8 = _rows((SUBLANES, lc))
        last0 = (seg - 1) * SUBLANES
        for q in range(n // lc):
            sl = slice(q * lc, (q + 1) * lc)
            tabs = [(tr_ref[k, :, sl], ti_ref[k, :, sl]) for k in range(5)]
            a_r, a_i = tabs[4]

            def local(rr, carry, sl=sl, a_r=a_r, a_i=a_i):
                r0 = pl.multiple_of((seg - 1 - rr) * SUBLANES, SUBLANES)
                lr, li = _cmul_add(lr_ref[pl.ds(r0, SUBLANES), sl], li_ref[pl.ds(r0, SUBLANES), sl], a_r, a_i, carry[0], carry[1])
                lr_ref[pl.ds(r0, SUBLANES), sl] = lr
                li_ref[pl.ds(r0, SUBLANES), sl] = li
                return lr, li

            zero = jnp.zeros((SUBLANES, lc), F32)
            er, ei = lax.fori_loop(0, seg, local, (zero, zero), unroll=4)
            for k in range(3):
                sh = 1 << k
                er, ei = _cmul_add(er, ei, tabs[k][0], tabs[k][1], pltpu.roll(er, SUBLANES - sh, 0), pltpu.roll(ei, SUBLANES - sh, 0))
            cin_r, cin_i = jnp.broadcast_to(car_r[:, sl], er.shape), jnp.broadcast_to(car_i[:, sl], ei.shape)
            er, ei = _cmul_add(er, ei, tabs[3][0], tabs[3][1], cin_r, cin_i)
            car_r[:, sl] = er[:1, :]
            car_i[:, sl] = ei[:1, :]
            c_r = jnp.where(row8 == SUBLANES - 1, cin_r, pltpu.roll(er, SUBLANES - 1, 0))
            c_i = jnp.where(row8 == SUBLANES - 1, cin_i, pltpu.roll(ei, SUBLANES - 1, 0))
            halo_r = jnp.where(time_first, 0.0, hrh_ref[SUBLANES - 1:, sl])
            halo_i = jnp.where(time_first, 0.0, hih_ref[SUBLANES - 1:, sl])
            hp0_r = jnp.where(row8 == 0, jnp.broadcast_to(halo_r, zero.shape), pltpu.roll(hr_ref[pl.ds(last0, SUBLANES), sl], 1, 0))
            hp0_i = jnp.where(row8 == 0, jnp.broadcast_to(halo_i, zero.shape), pltpu.roll(hi_ref[pl.ds(last0, SUBLANES), sl], 1, 0))

            def fix(rr, carry, sl=sl, a_r=a_r, a_i=a_i, c_r=c_r, c_i=c_i, hp0_r=hp0_r, hp0_i=hp0_i):
                pr, pi, acc_r, acc_i = carry
                r = seg - 1 - rr
                r0 = pl.multiple_of(r * SUBLANES, SUBLANES)
                lr, li = _cmul_add(lr_ref[pl.ds(r0, SUBLANES), sl], li_ref[pl.ds(r0, SUBLANES), sl], pr, pi, c_r, c_i)
                lr_ref[pl.ds(r0, SUBLANES), sl] = lr
                li_ref[pl.ds(r0, SUBLANES), sl] = li
                p0 = pl.multiple_of(jnp.maximum(r - 1, 0) * SUBLANES, SUBLANES)
                hpr = jnp.where(r == 0, hp0_r, hr_ref[pl.ds(p0, SUBLANES), sl])
                hpi = jnp.where(r == 0, hp0_i, hi_ref[pl.ds(p0, SUBLANES), sl])
                return (pr * a_r - pi * a_i, pr * a_i + pi * a_r, acc_r + (lr * hpr + li * hpi), acc_i + (li * hpr - lr * hpi))

            _, _, acc_r, acc_i = lax.fori_loop(0, seg, fix, (a_r, a_i, zero, zero), unroll=4)
            dar_ref[:, sl] += jnp.sum(acc_r, axis=0, keepdims=True)
            dai_ref[:, sl] += jnp.sum(acc_i, axis=0, keepdims=True)
        lrb, lib = lr_ref[...].astype(BF16), li_ref[...].astype(BF16)
        du = []
        for k in range(nblk):
            ublk = ub[:, k * cb:(k + 1) * cb]
            lrk, lik = lrb[:, k * nb:(k + 1) * nb], lib[:, k * nb:(k + 1) * nb]
            dbr_ref[k] += lax.dot_general(ublk, lrk, tn_dims, preferred_element_type=F32)
            dbi_ref[k] += lax.dot_general(ublk, lik, tn_dims, preferred_element_type=F32)
            du.append(lax.dot_general(lrk, bpr_ref[k], nt_dims, preferred_element_type=F32)
                      + lax.dot_general(lik, bpi_ref[k], nt_dims, preferred_element_type=F32))
        du_ref[...] = (d_ref[...] * dyv + _exact_rows(pt_ref[...], jnp.concatenate(du, axis=1))).astype(BF16)

    full = lambda shape: pl.BlockSpec(shape, lambda i: (0,) * len(shape))
    rev = lambda i: nt - 1 - i
    halo_idx = lambda i: jnp.maximum(rev(i) * hb - 1, 0)
    rc = pl.BlockSpec((ts, c), lambda i: (rev(i), 0))
    rn = pl.BlockSpec((ts, n), lambda i: (rev(i), 0))
    hn = pl.BlockSpec((SUBLANES, n), lambda i: (halo_idx(i), 0))
    sd = jax.ShapeDtypeStruct
    vec_n = (1, n)
    return _call_with_side(
        body, side, lambda: pl.program_id(0) == 0, lambda: pl.program_id(0) == nt - 1,
        name="s5_bwd", grid=(nt,),
        in_specs=[rc, rc, rc, rn, rn, hn, hn, full(perm.shape), full(perm_t.shape), full(rtab_r.shape), full(rtab_i.shape),
                  full(bp_r.shape), full(bp_i.shape), full(cp_r.shape), full(cp_i.shape), full(dvec.shape)],
        out_specs=[rc, full(vec_n), full(vec_n), full(bp_r.shape), full(bp_i.shape), full(cp_r.shape), full(cp_i.shape),
                   full(dvec.shape)],
        out_shape=[sd((s, c), BF16), sd(vec_n, F32), sd(vec_n, F32), sd(bp_r.shape, F32), sd(bp_i.shape, F32),
                   sd(cp_r.shape, F32), sd(cp_i.shape, F32), sd(dvec.shape, F32)],
        scratch_shapes=[pltpu.VMEM((ts, n), F32), pltpu.VMEM((ts, n), F32), pltpu.VMEM((1, n), F32), pltpu.VMEM((1, n), F32)],
        args=(dgy, ypre, u, hr, hi, hr, hi, perm, perm_t, rtab_r, rtab_i, bp_r, bp_i, cp_r, cp_i, dvec))


def _glu(gl2, ts=512):
    _, s, c = gl2.shape
    ts = min(ts, s)

    def body(g_ref, o_ref):
        o_ref[...] = (g_ref[0] * _sigmoid(g_ref[1])).astype(BF16)

    return pl.pallas_call(
        body, name="glu", grid=(s // ts,), in_specs=[pl.BlockSpec((2, ts, c), lambda i: (0, i, 0))],
        out_specs=pl.BlockSpec((ts, c), lambda i: (i, 0)), out_shape=jax.ShapeDtypeStruct((s, c), BF16), compiler_params=_cparams(),
    )(gl2)


def _glu_bwd(gl2, d_o, ts=512):
    _, s, c = gl2.shape
    ts = min(ts, s)

    def body(g_ref, do_ref, o_ref):
        sg = _sigmoid(g_ref[1])
        dov = do_ref[...]
        o_ref[0] = (dov * sg).astype(BF16)
        o_ref[1] = (dov * g_ref[0] * sg * (1.0 - sg)).astype(BF16)

    blk = pl.BlockSpec((2, ts, c), lambda i: (0, i, 0))
    return pl.pallas_call(
        body, name="glu_bwd", grid=(s // ts,), in_specs=[blk, pl.BlockSpec((ts, c), lambda i: (i, 0))],
        out_specs=blk, out_shape=jax.ShapeDtypeStruct((2, s, c), BF16), compiler_params=_cparams(),
    )(gl2, d_o)


PACK_ROW_MULTIPLE = 1024
ELEMENTWISE_BLOCK_ELEMS = 256 * 1024


def _row_tile(rows, cols):
    pref = max(SUBLANES, 1 << int(math.log2(max(1, ELEMENTWISE_BLOCK_ELEMS // cols))))
    if rows <= pref:
        return rows
    t = pref
    while rows % t:
        t //= 2
    assert t >= SUBLANES, rows
    return t


def _sum_parts(rs, side=None):
    nl = len(rs)
    p, rows, cols = rs[0].shape
    tr = _row_tile(rows, cols)
    nt = rows // tr

    def body(*refs):
        o_ref = refs[nl]
        for l in range(nl):
            acc = refs[l][0].astype(F32)
            for k in range(1, p):
                acc = acc + refs[l][k].astype(F32)
            o_ref[l] = acc

    outs, got = _call_with_side(
        body, side, lambda: pl.program_id(0) == 0, lambda: pl.program_id(0) == nt - 1,
        name="sum_parts", grid=(nt,), in_specs=[pl.BlockSpec((p, tr, cols), lambda i: (0, i, 0))] * nl,
        out_specs=[pl.BlockSpec((nl, tr, cols), lambda i: (0, i, 0))], out_shape=[jax.ShapeDtypeStruct((nl, rows, cols), F32)],
        scratch_shapes=[], args=tuple(rs))
    return outs[0], got


def _adamw(w, g_parts, m, v, side=None):
    rows, cols = w.shape
    tr = _row_tile(rows, max(cols, LANES))
    ng = len(g_parts)
    emit_grad = ng > 1
    c1 = 1.0 / (1.0 - ADAM_B1 ** ADAM_STEP)
    c2 = 1.0 / (1.0 - ADAM_B2 ** ADAM_STEP)

    def body(*refs):
        w_ref, m_ref, v_ref = refs[0], refs[1 + ng], refs[2 + ng]
        dl_ref, nm_ref, nv_ref = refs[3 + ng:6 + ng]
        g = refs[1][...]
        for k in range(1, ng):
            g = g + refs[1 + k][...]
        mn = ADAM_B1 * m_ref[...] + (1.0 - ADAM_B1) * g
        vn = ADAM_B2 * v_ref[...] + (1.0 - ADAM_B2) * (g * g)
        if emit_grad:
            refs[6 + ng][...] = g
        nm_ref[...] = mn
        nv_ref[...] = vn
        dl_ref[...] = -ADAM_LR * ((mn * c1) / (jnp.sqrt(vn * c2) + ADAM_EPS) + ADAM_WD * w_ref[...])

    blk = pl.BlockSpec((tr, cols), lambda i: (i, 0))
    sd = jax.ShapeDtypeStruct((rows, cols), F32)
    nout = 4 if emit_grad else 3
    nt = rows // tr
    return _call_with_side(
        body, side, lambda: pl.program_id(0) == 0, lambda: pl.program_id(0) == nt - 1,
        name="adamw", grid=(nt,), in_specs=[blk] * (3 + ng), out_specs=[blk] * nout, out_shape=[sd] * nout,
        scratch_shapes=[], args=(w, *g_parts, m, v))


def _place():
    x, y, c = lax.axis_index("x"), lax.axis_index("y"), lax.axis_index("c")
    chips = [(1 - x, y), (x, 1 - y), (1 - x, 1 - y)]
    return x, y, c, chips


class Side:
    def __init__(self, ins, outs, kind):
        self.ins, self.outs, self.kind = list(ins), list(outs), kind
        n = len(self.ins)
        self.sems = [pltpu.SemaphoreType.DMA((3 * n,)), pltpu.SemaphoreType.DMA((3 * n,)), pltpu.SemaphoreType.DMA((n,))]

    def _copies(self, ins, outs, send, recv, lsem):
        x, y, c, chips = _place()
        me = 2 * x + y
        local, out_going, in_coming = [], [], []
        for t in range(len(ins)):
            if self.kind == 'sibling':
                cp = pltpu.make_async_remote_copy(src_ref=ins[t], dst_ref=outs[t], send_sem=send.at[t], recv_sem=recv.at[t],
                                                  device_id=(x, y, 1 - c), device_id_type=MESH)
                out_going.append(cp)
                in_coming.append(cp)
                continue
            if self.kind == 'gather':
                src_local, srcs, dst_mine = ins[t], [ins[t]] * 3, outs[t].at[me]
            else:
                src_local, srcs, dst_mine = ins[t].at[me], [ins[t].at[2 * px + py] for px, py in chips], outs[t].at[me]
            local.append(pltpu.make_async_copy(src_local, dst_mine, lsem.at[t]))
            for r, (px, py) in enumerate(chips):
                out_going.append(pltpu.make_async_remote_copy(
                    src_ref=srcs[r], dst_ref=dst_mine, send_sem=send.at[3 * t + r], recv_sem=recv.at[3 * t + r],
                    device_id=(px, py, c), device_id_type=MESH))
                in_coming.append(pltpu.make_async_remote_copy(
                    src_ref=srcs[r], dst_ref=outs[t].at[2 * px + py], send_sem=send.at[3 * t + r], recv_sem=recv.at[3 * t + r],
                    device_id=(px, py, c), device_id_type=MESH))
        return local, out_going, in_coming

    def start(self, ins, outs, send, recv, lsem):
        local, out_going, _ = self._copies(ins, outs, send, recv, lsem)
        for cp in local + out_going:
            cp.start()

    def wait(self, ins, outs, send, recv, lsem):
        local, out_going, in_coming = self._copies(ins, outs, send, recv, lsem)
        for cp in in_coming:
            cp.wait_recv()
        for cp in out_going:
            cp.wait_send()
        for cp in local:
            cp.wait()


def _gather_side(shards):
    return Side(shards, [jax.ShapeDtypeStruct((N_CHIPS,) + s.shape, s.dtype) for s in shards], 'gather')


def _scatter_side(grads):
    return Side(grads, [jax.ShapeDtypeStruct(g.shape, g.dtype) for g in grads], 'scatter')


def _sibling_side(arrs):
    return Side(arrs, [jax.ShapeDtypeStruct(a.shape, a.dtype) for a in arrs], 'sibling')


def _call_with_side(body, side, first, last, *, name, grid, in_specs, out_specs, out_shape, scratch_shapes, args):
    if side is None:
        outs = pl.pallas_call(body, name=name, grid=grid, in_specs=in_specs, out_specs=out_specs, out_shape=out_shape,
                              scratch_shapes=scratch_shapes, compiler_params=_cparams())(*args)
        return outs, []
    n_in, n_out, n_sc = len(in_specs), len(out_specs), len(scratch_shapes)
    ns_in, ns_out = len(side.ins), len(side.outs)

    def wrapped(*refs):
        base_in, s_in = refs[:n_in], refs[n_in:n_in + ns_in]
        o0 = n_in + ns_in
        base_out, s_out = refs[o0:o0 + n_out], refs[o0 + n_out:o0 + n_out + ns_out]
        sc0 = o0 + n_out + ns_out
        base_sc, sems = refs[sc0:sc0 + n_sc], refs[sc0 + n_sc:]

        @pl.when(first())
        def _():
            side.start(s_in, s_out, *sems)

        body(*base_in, *base_out, *base_sc)

        @pl.when(last())
        def _():
            side.wait(s_in, s_out, *sems)

    any_spec = pl.BlockSpec(memory_space=pl.ANY)
    outs = pl.pallas_call(
        wrapped, name=name, grid=grid, in_specs=list(in_specs) + [any_spec] * ns_in, out_specs=list(out_specs) + [any_spec] * ns_out,
        out_shape=list(out_shape) + side.outs, scratch_shapes=list(scratch_shapes) + side.sems, compiler_params=_cparams(),
    )(*args, *side.ins)
    return outs[:n_out], outs[n_out:]


def _run_side(name, side):
    def body(*refs):
        n = len(side.ins)
        side.start(refs[:n], refs[n:2 * n], *refs[2 * n:])
        side.wait(refs[:n], refs[n:2 * n], *refs[2 * n:])

    any_spec = pl.BlockSpec(memory_space=pl.ANY)
    return pl.pallas_call(body, name=name, in_specs=[any_spec] * len(side.ins), out_specs=[any_spec] * len(side.outs),
                          out_shape=side.outs, scratch_shapes=side.sems)(*side.ins)


def _allreduce_small(v):
    rows, cols = v.shape
    r8 = rows // (2 * N_CHIPS)
    assert r8 * 2 * N_CHIPS == rows and r8 % SUBLANES == 0, rows

    def body(v_ref, o_ref, sib_ref, cs_ref, slot_ref, send, recv):
        x, y, c, chips = _place()
        me = 2 * x + y
        sibling = (x, y, 1 - c)

        def eighth(ref, chip, core):
            return ref.at[pl.ds(pl.multiple_of((2 * chip + core) * r8, SUBLANES), r8)]

        def copy(src, dst, k, to):
            return pltpu.make_async_remote_copy(src_ref=src, dst_ref=dst, send_sem=send.at[k], recv_sem=recv.at[k],
                                                device_id=to, device_id_type=MESH)

        d2d = copy(v_ref, sib_ref, 0, sibling)
        d2d.start()
        d2d.wait_recv()
        cs_ref[...] = v_ref[...] + sib_ref[...]
        reduce_out = [copy(eighth(cs_ref, 2 * px + py, c), slot_ref.at[me], 1 + r, (px, py, c)) for r, (px, py) in enumerate(chips)]
        for cp in reduce_out:
            cp.start()
        slot_ref[me] = cs_ref[pl.ds(pl.multiple_of((2 * me + c) * r8, SUBLANES), r8), :]
        for r, (px, py) in enumerate(chips):
            copy(eighth(cs_ref, me, c), slot_ref.at[2 * px + py], 1 + r, (px, py, c)).wait_recv()
        o_ref[pl.ds(pl.multiple_of((2 * me + c) * r8, SUBLANES), r8), :] = (slot_ref[0] + slot_ref[1]) + (slot_ref[2] + slot_ref[3])
        mine = eighth(o_ref, me, c)
        hand_out = [copy(mine, mine, 4, sibling)] + [copy(mine, mine, 5 + r, (px, py, c)) for r, (px, py) in enumerate(chips)]
        for cp in hand_out:
            cp.start()
        passed_on = []
        for r, (px, py) in enumerate(chips):
            theirs = eighth(o_ref, 2 * px + py, c)
            copy(theirs, theirs, 5 + r, (px, py, c)).wait_recv()
            fw = copy(theirs, theirs, 8 + r, sibling)
            fw.start()
            passed_on.append(fw)
        sib_own = eighth(o_ref, me, 1 - c)
        copy(sib_own, sib_own, 4, sibling).wait_recv()
        for r, (px, py) in enumerate(chips):
            got = eighth(o_ref, 2 * px + py, 1 - c)
            copy(got, got, 8 + r, sibling).wait_recv()
        for cp in [d2d] + reduce_out + hand_out + passed_on:
            cp.wait_send()

    vm = pl.BlockSpec(memory_space=pltpu.VMEM)
    return pl.pallas_call(
        body, name="allreduce_small", in_specs=[vm], out_specs=vm, out_shape=jax.ShapeDtypeStruct((rows, cols), F32),
        scratch_shapes=[pltpu.VMEM((rows, cols), F32), pltpu.VMEM((rows, cols), F32), pltpu.VMEM((N_CHIPS, r8, cols), F32),
                        pltpu.SemaphoreType.DMA((11,)), pltpu.SemaphoreType.DMA((11,))],
        compiler_params=_cparams(),
    )(v)


def _pack(tensors):
    pieces = []
    for t in tensors:
        flat = t.reshape(-1)
        pad = (-flat.shape[0]) % (SUBLANES * LANES)
        pieces.append(jnp.pad(flat, (0, pad)).reshape(-1, LANES))
    rows = sum(p.shape[0] for p in pieces)
    pieces.append(jnp.zeros(((-rows) % PACK_ROW_MULTIPLE, LANES), tensors[0].dtype))
    return jnp.concatenate(pieces, axis=0)


def _unpack(buf, like):
    out, off = [], 0
    for t in like:
        size = math.prod(t.shape)
        rows = -(-size // (SUBLANES * LANES)) * SUBLANES
        out.append(buf[off:off + rows].reshape(-1)[:size].reshape(t.shape))
        off += rows
    return out


def _s5_pack_b(bb):
    gc, g, p = bb.shape
    q = S5_GROUPS_PER_BLOCK
    t = bb.reshape(gc, g // q, q, p).transpose(1, 2, 0, 3)
    eye = jnp.eye(q, dtype=bb.dtype)
    return (t[:, :, :, None, :] * eye[None, :, None, :, None]).reshape(g // q, q * gc, q * p)


def _s5_unpack_b(dbp, gc, p):
    nb = dbp.shape[0]
    q = S5_GROUPS_PER_BLOCK
    eye = jnp.eye(q, dtype=dbp.dtype)
    t = (dbp.reshape(nb, q, gc, q, p) * eye[None, :, None, :, None]).sum(axis=3)
    return t.transpose(2, 0, 1, 3).reshape(gc, nb * q, p)


def _s5_pack_c(cc):
    g, gc, p = cc.shape
    q = S5_GROUPS_PER_BLOCK
    t = cc.reshape(g // q, q, gc, p).transpose(0, 1, 3, 2)
    eye = jnp.eye(q, dtype=cc.dtype)
    return (t[:, :, :, None, :] * eye[None, :, None, :, None]).reshape(g // q, q * p, q * gc)


def _s5_unpack_c(dcp, gc, p):
    nb = dcp.shape[0]
    q = S5_GROUPS_PER_BLOCK
    eye = jnp.eye(q, dtype=dcp.dtype)
    t = (dcp.reshape(nb, q, p, q, gc) * eye[None, :, None, :, None]).sum(axis=3)
    return t.transpose(0, 1, 3, 2).reshape(nb * q, gc, p)


def _split2(m):
    return m.arr[:, 0]


def kernel(x, norm_mix_g, norm_ffn_g, norm_final_g, rg_w_in, rg_conv_w, rg_conv_b, rg_w_a, rg_b_a, rg_w_x, rg_b_x, rg_lambda, rg_w_out, s5_w_in, s5_a_re, s5_a_im, s5_log_dt, s5_b_re, s5_b_im, s5_c_re, s5_c_im, s5_d, s5_w_glu, s5_w_out, ffn_w_up, ffn_conv_w, ffn_conv_b, ffn_w_down, loss_target, m_norm_mix_g, m_norm_ffn_g, m_norm_final_g, m_rg_w_in, m_rg_conv_w, m_rg_conv_b, m_rg_w_a, m_rg_b_a, m_rg_w_x, m_rg_b_x, m_rg_lambda, m_rg_w_out, m_s5_w_in, m_s5_a_re, m_s5_a_im, m_s5_log_dt, m_s5_b_re, m_s5_b_im, m_s5_c_re, m_s5_c_im, m_s5_d, m_s5_w_glu, m_s5_w_out, m_ffn_w_up, m_ffn_conv_w, m_ffn_conv_b, m_ffn_w_down, v_norm_mix_g, v_norm_ffn_g, v_norm_final_g, v_rg_w_in, v_rg_conv_w, v_rg_conv_b, v_rg_w_a, v_rg_b_a, v_rg_w_x, v_rg_b_x, v_rg_lambda, v_rg_w_out, v_s5_w_in, v_s5_a_re, v_s5_a_im, v_s5_log_dt, v_s5_b_re, v_s5_b_im, v_s5_c_re, v_s5_c_im, v_s5_d, v_s5_w_glu, v_s5_w_out, v_ffn_w_up, v_ffn_conv_w, v_ffn_conv_b, v_ffn_w_down):
    w = dict(zip(PARAM_NAMES, (norm_mix_g, norm_ffn_g, norm_final_g, rg_w_in, rg_conv_w, rg_conv_b, rg_w_a, rg_b_a, rg_w_x, rg_b_x,
                               rg_lambda, rg_w_out, s5_w_in, s5_a_re, s5_a_im, s5_log_dt, s5_b_re, s5_b_im, s5_c_re, s5_c_im, s5_d,
                               s5_w_glu, s5_w_out, ffn_w_up, ffn_conv_w, ffn_conv_b, ffn_w_down)))
    mom = dict(zip(PARAM_NAMES, (m_norm_mix_g, m_norm_ffn_g, m_norm_final_g, m_rg_w_in, m_rg_conv_w, m_rg_conv_b, m_rg_w_a, m_rg_b_a,
                                 m_rg_w_x, m_rg_b_x, m_rg_lambda, m_rg_w_out, m_s5_w_in, m_s5_a_re, m_s5_a_im, m_s5_log_dt, m_s5_b_re,
                                 m_s5_b_im, m_s5_c_re, m_s5_c_im, m_s5_d, m_s5_w_glu, m_s5_w_out, m_ffn_w_up, m_ffn_conv_w,
                                 m_ffn_conv_b, m_ffn_w_down)))
    vel = dict(zip(PARAM_NAMES, (v_norm_mix_g, v_norm_ffn_g, v_norm_final_g, v_rg_w_in, v_rg_conv_w, v_rg_conv_b, v_rg_w_a, v_rg_b_a,
                                 v_rg_w_x, v_rg_b_x, v_rg_lambda, v_rg_w_out, v_s5_w_in, v_s5_a_re, v_s5_a_im, v_s5_log_dt, v_s5_b_re,
                                 v_s5_b_im, v_s5_c_re, v_s5_c_im, v_s5_d, v_s5_w_glu, v_s5_w_out, v_ffn_w_up, v_ffn_conv_w,
                                 v_ffn_conv_b, v_ffn_w_down)))
    _, s, d = x.shape
    depth = norm_mix_g.shape[0]
    n_grp, n_state = s5_a_re.shape[1], s5_a_re.shape[2]
    gc = s5_b_re.shape[3]
    d_ff = ffn_w_down.shape[1] * N_CHIPS
    s5_ts = min(256, s)
    s5_perm = _segment_perm(s5_ts)

    wb = {n: (w[n].astype(BF16) if n in BIG else w[n]) for n in SHARDED}
    gath = {}

    def mixer_keys(i):
        return [(n, i // 2) for n in MIXER_SHARDED[i % 2]] if i < depth else []

    def gather_side(keys):
        return _gather_side([wb[n][l] for n, l in keys])

    def put(keys, arrs):
        for k, a in zip(keys, arrs):
            gath[k] = a

    def wcol(n, l):
        return Mat(gath[(n, l)][:, None], 0, 'c')

    def wrow(n, l):
        g = gath[(n, l)]
        return Mat(g.reshape(1, 1, N_CHIPS * g.shape[1], g.shape[2]), 0, 'c')

    def rg_cw(l):
        return gath[('rg_conv_w', l)].transpose(1, 0, 2).reshape(RG_CONV_W, d)

    def s5_dv(l):
        return gath[('s5_d', l)].reshape(1, d)

    def f_cw(l):
        return gath[('ffn_conv_w', l)].transpose(1, 0, 2).reshape(FFN_CONV_W, 2, d_ff).transpose(1, 0, 2)

    tm = min(1024, s)
    tkw = min(2048, s)
    d_up = 2 * d_ff // N_CHIPS
    f_cb = ffn_conv_b.reshape(depth, 2, 1, d_ff)

    h = x.reshape(s, d)
    saved = []
    for i in range(depth):
        j = i // 2
        sv = {'h_in': h}
        hn, got = _rms_fwd(h, norm_mix_g[i:i + 1], side=gather_side(mixer_keys(0)) if i == 0 else None)
        if i == 0:
            put(mixer_keys(0), got)
        sv['hn'] = hn
        up_keys = [('ffn_w_up', i), ('ffn_conv_w', i)]
        if i % 2 == 0:
            xg = _mm("rg_in", 'nn', act(hn), wcol('rg_w_in', j), out_parts=2, tm=tm, tn=512, tk=d)
            xg2 = _split2(xg)
            wa, wx = rg_w_a[j].astype(BF16), rg_w_x[j].astype(BF16)
            ba, bx = rg_b_a[j].reshape(1, d), rg_b_x[j].reshape(1, d)
            (xr, hs, y), got = _rg_fwd(xg2, rg_cw(j), rg_conv_b[j:j + 1], wa, ba, wx, bx, rg_lambda[j:j + 1],
                                       side=gather_side(up_keys))
            put(up_keys, got)
            sv.update(xg2=xg2, xr=xr, hs=hs, y=y, wa=wa, wx=wx, ba=ba, bx=bx)
            h = _mm("rg_out", 'nn', act(y), wrow('rg_w_out', j), res=act(h), tm=tm, tn=d, tk=d).arr[0, 0]
        else:
            u = _mm("s5_in", 'nn', act(hn), wrow('s5_w_in', j), tm=tm, tn=d, tk=d).arr[0, 0]
            bt_re, bt_im = s5_b_re[j].transpose(2, 0, 1), s5_b_im[j].transpose(2, 0, 1)
            ldt = s5_log_dt[j].reshape(n_grp, 1)
            tab_r, tab_i, rtab_r, rtab_i, bbr, bbi = _s5_tables3(s5_a_re[j], s5_a_im[j], ldt, bt_re, bt_im, seg=s5_ts // SUBLANES)
            nn_ = n_grp * n_state
            tab_r, tab_i, rtab_r, rtab_i = (t.reshape(5, SUBLANES, nn_) for t in (tab_r, tab_i, rtab_r, rtab_i))
            prm = dict(bp_r=_s5_pack_b(bbr).astype(BF16), bp_i=_s5_pack_b(bbi).astype(BF16),
                       cp_r=_s5_pack_c(s5_c_re[j]).astype(BF16), cp_i=_s5_pack_c(s5_c_im[j]).astype(BF16), dvec=s5_dv(j))
            (hr, hi, ypre, gy), got = _s5_fwd3(u, s5_perm, s5_perm.T, tab_r, tab_i, ts=s5_ts, side=gather_side(up_keys), **prm)
            sv.update(rtab_r=rtab_r, rtab_i=rtab_i)
            put(up_keys, got)
            gl = _mm("s5_glu", 'nn', act(gy), wcol('s5_w_glu', j), out_parts=2, tm=tm, tn=512, tk=d)
            gl2 = _split2(gl)
            o = _glu(gl2)
            sv.update(u=u, prm=prm, hr=hr, hi=hi, ypre=ypre, gy=gy, gl2=gl2, o=o, bt_re=bt_re, bt_im=bt_im, ldt=ldt)
            h = _mm("s5_out", 'nn', act(o), wrow('s5_w_out', j), res=act(h), tm=tm, tn=d, tk=d).arr[0, 0]
        sv['h_mid'] = h
        hn2, _ = _rms_fwd(h, norm_ffn_g[i:i + 1])
        next_keys = [('ffn_w_down', i)] + mixer_keys(i + 1)
        (up2, c2, a_ffn), got = _ffn_up_act(hn2, gath[('ffn_w_up', i)], f_cw(i), f_cb[i], side=gather_side(next_keys))
        put(next_keys, got)
        sv.update(hn2=hn2, up2=up2, c2=c2, act=a_ffn)
        h = _mm("ffn_down", 'nn', act(a_ffn), wrow('ffn_w_down', i), res=act(h), tm=tm, tn=d, tk=d_ff // 2).arr[0, 0]
        saved.append(sv)

    loss_row, dh, dg_final = _loss_and_grad(h, norm_final_g.reshape(1, d), loss_target.reshape(s, d))
    loss = lax.psum(loss_row[0, 0], ("x", "y", "c"))

    gl_ = {n: [None] * w[n].shape[0] for n in PARAM_NAMES if n != 'norm_final_g'}
    recvd = {}

    def scatter_side(keys):
        return _scatter_side([gl_[n][l].reshape((N_CHIPS,) + w[n].shape[1:]) for n, l in keys])

    def record(keys, arrs):
        for k, a in zip(keys, arrs):
            recvd[k] = a

    pending = None
    for i in reversed(range(depth)):
        j = i // 2
        sv = saved[i]
        gl_['ffn_w_down'][i] = _mm("ffn_down_dw", 'tn', act(sv['act']), act(dh), out_dtype=BF16, tm=d_ff // N_CHIPS, tn=d, tk=tkw).arr
        (dup2, dcw2, dcb2), got = _ffn_bwd_fused(dh, gath[('ffn_w_down', i)].reshape(d_ff, d), sv['up2'], sv['c2'], f_cw(i),
                                                 side=scatter_side(pending) if pending else None)
        if pending:
            record(pending, got)
        gl_['ffn_conv_w'][i] = dcw2.transpose(1, 0, 2).reshape(FFN_CONV_W, 2 * d_ff)
        gl_['ffn_conv_b'][i] = dcb2.reshape(2 * d_ff)
        dup = Mat(dup2[:, None], 0, 'c')
        gl_['ffn_w_up'][i] = _mm("ffn_up_dw", 'tn', act(sv['hn2']), dup, out_parts=N_CHIPS, out_dtype=BF16, tm=d, tn=d_up, tk=tkw).arr
        (dh, dg), _ = _mm_rms_bwd("ffn_up_dx", dup, wcol('ffn_w_up', i), sv['h_mid'], norm_ffn_g[i:i + 1], dh, tm=tm, tk=d_up)
        gl_['norm_ffn_g'][i] = dg[0]
        ffn_keys = [('ffn_w_up', i), ('ffn_w_down', i)]
        if i % 2 == 0:
            dy = _mm("rg_out_dx", 'nt', act(dh), wrow('rg_w_out', j), tm=tm, tn=d, tk=d).arr[0, 0]
            gl_['rg_w_out'][j] = _mm("rg_out_dw", 'tn', act(sv['y']), act(dh), out_dtype=BF16, tm=d, tn=d, tk=tkw).arr
            (dxg2, dcw, dcb, dwa, dba, dwx, dbx, dlam), got = _rg_bwd(
                dy, sv['xg2'], sv['xr'], sv['hs'], rg_cw(j), sv['wa'], sv['ba'], sv['wx'], sv['bx'], rg_lambda[j:j + 1],
                side=scatter_side(ffn_keys))
            record(ffn_keys, got)
            gl_['rg_conv_w'][j] = dcw
            gl_['rg_conv_b'][j] = dcb[0]
            gl_['rg_w_a'][j], gl_['rg_w_x'][j] = dwa, dwx
            gl_['rg_b_a'][j], gl_['rg_b_x'][j] = dba.reshape(rg_b_a.shape[1:]), dbx.reshape(rg_b_x.shape[1:])
            gl_['rg_lambda'][j] = dlam[0]
            dxg = Mat(dxg2[:, None], 0, 'c')
            gl_['rg_w_in'][j] = _mm("rg_in_dw", 'tn', act(sv['hn']), dxg, out_parts=N_CHIPS, out_dtype=BF16, tm=d, tn=512, tk=tkw).arr
            mix_dx = ("rg_in_dx", dxg, wcol('rg_w_in', j), 512)
            pending = [('rg_w_in', j), ('rg_w_out', j)]
        else:
            d_o = _mm("s5_out_dx", 'nt', act(dh), wrow('s5_w_out', j), tm=tm, tn=d, tk=d).arr[0, 0]
            gl_['s5_w_out'][j] = _mm("s5_out_dw", 'tn', act(sv['o']), act(dh), out_dtype=BF16, tm=d, tn=d, tk=tkw).arr
            dgl2 = _glu_bwd(sv['gl2'], d_o)
            dgl = Mat(dgl2[:, None], 0, 'c')
            gl_['s5_w_glu'][j] = _mm("s5_glu_dw", 'tn', act(sv['gy']), dgl, out_parts=N_CHIPS, out_dtype=BF16, tm=d, tn=512, tk=tkw).arr
            dgy = _mm("s5_glu_dx", 'nt', dgl, wcol('s5_w_glu', j), tm=tm, tn=d, tk=512).arr[0, 0]
            (du, dar, dai, dbpr, dbpi, dcpr, dcpi, dd), got = _s5_bwd3(
                dgy, sv['ypre'], sv['u'], sv['hr'], sv['hi'], s5_perm, s5_perm.T, sv['rtab_r'], sv['rtab_i'], ts=s5_ts,
                side=scatter_side(ffn_keys), **sv['prm'])
            record(ffn_keys, got)
            gl_['s5_d'][j] = dd[0]
            gl_['s5_c_re'][j] = _s5_unpack_c(dcpr, gc, n_state)
            gl_['s5_c_im'][j] = -_s5_unpack_c(dcpi, gc, n_state)
            d_are, d_aim, d_ldt, d_btr, d_bti = _s5_params_bwd(
                s5_a_re[j], s5_a_im[j], sv['ldt'], sv['bt_re'], sv['bt_im'], dar.reshape(n_grp, n_state), dai.reshape(n_grp, n_state),
                _s5_unpack_b(dbpr, gc, n_state), _s5_unpack_b(dbpi, gc, n_state))
            gl_['s5_a_re'][j], gl_['s5_a_im'][j], gl_['s5_log_dt'][j] = d_are, d_aim, d_ldt[:, 0]
            gl_['s5_b_re'][j], gl_['s5_b_im'][j] = d_btr.transpose(1, 2, 0), d_bti.transpose(1, 2, 0)
            dum = act(du)
            gl_['s5_w_in'][j] = _mm("s5_in_dw", 'tn', act(sv['hn']), dum, out_dtype=BF16, tm=d, tn=d, tk=tkw).arr
            mix_dx = ("s5_in_dx", dum, wrow('s5_w_in', j), d)
            pending = [('s5_w_in', j), ('s5_w_glu', j), ('s5_w_out', j)]
        (dh, dg), got = _mm_rms_bwd(mix_dx[0], mix_dx[1], mix_dx[2], sv['h_in'], norm_mix_g[i:i + 1], dh, tm=tm, tk=mix_dx[3],
                                    side=scatter_side(pending) if i == 0 else None)
        if i == 0:
            record(pending, got)
        gl_['norm_mix_g'][i] = dg[0]
    grad_x = dh.reshape(x.shape)

    order = sorted(BIG, key=lambda n: -math.prod(w[n].shape))
    chip_sums, theirs, prev = {}, {}, None
    for n in order:
        cols = w[n].shape[-1]
        cs, got = _sum_parts([recvd[(n, l)].reshape(N_CHIPS, -1, cols) for l in range(w[n].shape[0])],
                             side=_sibling_side([chip_sums[prev]]) if prev else None)
        chip_sums[n] = cs.reshape(-1, cols)
        if prev:
            theirs[prev] = got[0]
        prev = n
    theirs[prev] = _run_side("swap_last", _sibling_side([chip_sums[prev]]))[0]
    results = {}
    for n in BIG:
        cols = w[n].shape[-1]
        (delta, new_m, new_v, grad), _ = _adamw(w[n].reshape(-1, cols), [chip_sums[n], theirs[n]], mom[n].reshape(-1, cols),
                                                vel[n].reshape(-1, cols))
        results[n] = [o.reshape(w[n].shape) for o in (grad, delta, new_m, new_v)]

    small = REPLICATED + SMALL_SHARDED
    local = [dg_final.reshape(d) if n == 'norm_final_g' else jnp.stack(gl_[n]) for n in small]
    summed = _unpack(_allreduce_small(_pack(local)), local)
    me = 2 * lax.axis_index("x") + lax.axis_index("y")
    for n, g in zip(small, summed):
        if n in SMALL_SHARDED:
            g = lax.dynamic_slice_in_dim(g, me * w[n].shape[-1], w[n].shape[-1], axis=g.ndim - 1)
        view = (-1, w[n].shape[-1])
        (delta, new_m, new_v), _ = _adamw(w[n].reshape(view), [g.reshape(view)], mom[n].reshape(view), vel[n].reshape(view))
        results[n] = [g] + [o.reshape(w[n].shape) for o in (delta, new_m, new_v)]

    return (loss, grad_x, *[results[n][0] for n in PARAM_NAMES], *[results[n][1] for n in PARAM_NAMES],
            *[results[n][2] for n in PARAM_NAMES], *[results[n][3] for n in PARAM_NAMES])
```

```python
import math

import jax
import jax.numpy as jnp
from jax import lax
from jax.experimental import pallas as pl
from jax.experimental.pallas import tpu as pltpu

F32 = jnp.float32
BF16 = jnp.bfloat16
MESH = pl.DeviceIdType.MESH

NORM_EPS = 1e-6
RG_HEADS = 8
RG_CONV_W = 4
RG_C = 8.0
S5_GC = 16
S5_P = 64
S5_GROUPS_PER_BLOCK = 8
FFN_CONV_W = 3
N_CHIPS = 4
ADAM_LR, ADAM_B1, ADAM_B2, ADAM_EPS, ADAM_WD, ADAM_STEP = 0.001, 0.9, 0.999, 1e-08, 0.01, 10
VMEM_LIMIT_BYTES = 56 * 1024 * 1024
SUBLANES = 8
LANES = 128

PARAM_NAMES = ['norm_mix_g', 'norm_ffn_g', 'norm_final_g', 'rg_w_in', 'rg_conv_w', 'rg_conv_b', 'rg_w_a', 'rg_b_a', 'rg_w_x',
               'rg_b_x', 'rg_lambda', 'rg_w_out', 's5_w_in', 's5_a_re', 's5_a_im', 's5_log_dt', 's5_b_re', 's5_b_im', 's5_c_re',
               's5_c_im', 's5_d', 's5_w_glu', 's5_w_out', 'ffn_w_up', 'ffn_conv_w', 'ffn_conv_b', 'ffn_w_down']
SHARDED = ['rg_w_in', 'rg_conv_w', 'rg_w_out', 's5_w_in', 's5_d', 's5_w_glu', 's5_w_out', 'ffn_w_up', 'ffn_conv_w', 'ffn_w_down']
BIG = ['rg_w_in', 'rg_w_out', 's5_w_in', 's5_w_glu', 's5_w_out', 'ffn_w_up', 'ffn_w_down']
SMALL_SHARDED = ['rg_conv_w', 's5_d', 'ffn_conv_w']
MIXER_SHARDED = [['rg_w_in', 'rg_conv_w', 'rg_w_out'], ['s5_w_in', 's5_d', 's5_w_glu', 's5_w_out']]
REPLICATED = [n for n in PARAM_NAMES if n not in SHARDED]


def _cparams():
    return pltpu.CompilerParams(vmem_limit_bytes=VMEM_LIMIT_BYTES)


_GELU_C = math.sqrt(2.0 / math.pi)
_GELU_K = 0.044715


def _gelu(x):
    return 0.5 * x * (1.0 + jnp.tanh(_GELU_C * (x + _GELU_K * x * x * x)))


def _gelu_and_grad(x):
    t = jnp.tanh(_GELU_C * (x + _GELU_K * x * x * x))
    g = 0.5 * x * (1.0 + t)
    dg = 0.5 * (1.0 + t) + 0.5 * x * (1.0 - t * t) * (_GELU_C * (1.0 + 3.0 * _GELU_K * x * x))
    return g, dg


def _sigmoid(x):
    return jax.nn.sigmoid(x)


def _neg_expm1(x):
    series = -(x * (1.0 + x * (0.5 + x * (1.0 / 6 + x * (1.0 / 24 + x * (1.0 / 120 + x * (1.0 / 720)))))))
    return jnp.where(x > -0.25, series, 1.0 - jnp.exp(x))


def _softplus(z):
    return jnp.maximum(z, 0.0) + jnp.log1p(jnp.exp(-jnp.abs(z)))


def _rows(shape):
    return lax.broadcasted_iota(jnp.int32, shape, 0)


def _shift_down(x, halo, k):
    ext = jnp.concatenate([halo, x], axis=0)
    return pltpu.roll(ext, k, 0)[SUBLANES:]


def _shift_up(x, halo, k):
    ext = jnp.concatenate([x, halo], axis=0)
    n = ext.shape[0]
    return pltpu.roll(ext, n - k, 0)[:x.shape[0]]


RG_LANE_CHUNK = 512


def _real_slab_scan(a_ref, b_ref, out_ref, carry_ref, reverse):
    t, c = a_ref.shape
    nsl = t // SUBLANES
    lc = min(RG_LANE_CHUNK, c)
    row8 = _rows((SUBLANES, lc))
    for q in range(c // lc):
        sl = slice(q * lc, (q + 1) * lc)

        def slab(jj, carry, sl=sl):
            j = nsl - 1 - jj if reverse else jj
            r0 = pl.multiple_of(j * SUBLANES, SUBLANES)
            a, b = a_ref[pl.ds(r0, SUBLANES), sl], b_ref[pl.ds(r0, SUBLANES), sl]
            for k in range(3):
                sh = 1 << k
                keep = row8 < SUBLANES - sh if reverse else row8 >= sh
                amount = SUBLANES - sh if reverse else sh
                b = a * jnp.where(keep, pltpu.roll(b, amount, 0), 0.0) + b
                a = a * jnp.where(keep, pltpu.roll(a, amount, 0), 1.0)
            x = b + a * jnp.broadcast_to(carry, b.shape)
            out_ref[pl.ds(r0, SUBLANES), sl] = x
            return x[:1, :] if reverse else x[SUBLANES - 1:, :]

        carry_ref[:, sl] = lax.fori_loop(0, nsl, slab, carry_ref[:, sl], unroll=2)


class Mat:
    def __init__(self, arr, l=0, split='c'):
        assert arr.ndim == 4
        self.arr, self.l, self.split = arr, l, split
        p, _, r, c = arr.shape
        self.shape = (r, c * p) if split == 'c' else (r * p, c)

    def spec(self, tr, tc, rc):
        p, _, r, c = self.arr.shape
        l = self.l
        assert r % tr == 0 and c % tc == 0, (self.arr.shape, tr, tc)
        if self.split == 'c':
            per = c // tc
            return pl.BlockSpec((None, None, tr, tc), lambda i, j, k: (rc(i, j, k)[1] // per, l, rc(i, j, k)[0], rc(i, j, k)[1] % per))
        per = r // tr
        return pl.BlockSpec((None, None, tr, tc), lambda i, j, k: (rc(i, j, k)[0] // per, l, rc(i, j, k)[0] % per, rc(i, j, k)[1]))


def act(x, parts=1):
    s, c = x.shape
    return Mat(x.reshape(s, parts, c // parts).transpose(1, 0, 2)[:, None] if parts > 1 else x[None, None])


def _mm(name, mode, a, b, *, out_parts=1, out_split='c', out_dtype=F32, res=None, tm=512, tn=512, tk=512):
    if mode == 'nn':
        (m, kk), (kb, n) = a.shape, b.shape
    elif mode == 'nt':
        (m, kk), (n, kb) = a.shape, b.shape
    else:
        (kk, m), (kb, n) = a.shape, b.shape
    assert kk == kb, (name, a.shape, b.shape)
    tm, tn, tk = min(tm, m), min(tn, n), min(tk, kk)
    assert m % tm == 0 and n % tn == 0 and kk % tk == 0, (name, m, n, kk, tm, tn, tk)
    nk = kk // tk
    if mode == 'nn':
        a_spec = a.spec(tm, tk, lambda i, j, k: (i, k))
        b_spec = b.spec(tk, tn, lambda i, j, k: (k, j))
        dims = (((1,), (0,)), ((), ()))
    elif mode == 'nt':
        a_spec = a.spec(tm, tk, lambda i, j, k: (i, k))
        b_spec = b.spec(tn, tk, lambda i, j, k: (j, k))
        dims = (((1,), (1,)), ((), ()))
    else:
        a_spec = a.spec(tk, tm, lambda i, j, k: (k, i))
        b_spec = b.spec(tk, tn, lambda i, j, k: (k, j))
        dims = (((0,), (0,)), ((), ()))
    if out_split == 'c':
        out_arr = jax.ShapeDtypeStruct((out_parts, 1, m, n // out_parts), out_dtype)
    else:
        out_arr = jax.ShapeDtypeStruct((out_parts, 1, m // out_parts, n), out_dtype)
    out_mat = Mat(out_arr, 0, out_split)
    o_spec = out_mat.spec(tm, tn, lambda i, j, k: (i, j))
    has_res = res is not None

    def body(*refs):
        if has_res:
            a_ref, b_ref, r_ref, o_ref = refs[:4]
        else:
            a_ref, b_ref, o_ref = refs[:3]
        prod = lax.dot_general(a_ref[...].astype(BF16), b_ref[...].astype(BF16), dims, preferred_element_type=F32)

        def finish(acc):
            if has_res:
                acc = acc + r_ref[...]
            o_ref[...] = acc.astype(out_dtype)

        if nk == 1:
            finish(prod)
        else:
            acc_ref = refs[-1]
            k = pl.program_id(2)

            @pl.when(k == 0)
            def _():
                acc_ref[...] = prod

            @pl.when(k > 0)
            def _():
                acc_ref[...] += prod

            @pl.when(k == nk - 1)
            def _():
                finish(acc_ref[...])

    in_specs = [a_spec, b_spec]
    args = [a.arr, b.arr]
    if has_res:
        in_specs.append(res.spec(tm, tn, lambda i, j, k: (i, j)))
        args.append(res.arr)
    out = pl.pallas_call(
        body, name=name, grid=(m // tm, n // tn, nk), in_specs=in_specs, out_specs=o_spec, out_shape=out_arr,
        scratch_shapes=[pltpu.VMEM((tm, tn), F32)] if nk > 1 else [], compiler_params=_cparams(),
    )(*args)
    return Mat(out, 0, out_split)


def _rms_fwd(h, g, ts=512, side=None):
    s, d = h.shape
    ts = min(ts, s)
    nt = s // ts

    def body(h_ref, g_ref, o_ref):
        x = h_ref[...]
        var = jnp.mean(x * x, axis=-1, keepdims=True)
        o_ref[...] = (x * lax.rsqrt(var + NORM_EPS) * g_ref[...]).astype(BF16)

    outs, got = _call_with_side(
        body, side, lambda: pl.program_id(0) == 0, lambda: pl.program_id(0) == nt - 1,
        name="rms_fwd", grid=(nt,),
        in_specs=[pl.BlockSpec((ts, d), lambda i: (i, 0)), pl.BlockSpec((1, d), lambda i: (0, 0))],
        out_specs=[pl.BlockSpec((ts, d), lambda i: (i, 0))], out_shape=[jax.ShapeDtypeStruct((s, d), BF16)],
        scratch_shapes=[], args=(h, g))
    return outs[0], got


def _loss_and_grad(h, g, tgt, ts=512):
    s, d = h.shape
    ts = min(ts, s)

    def body(h_ref, g_ref, t_ref, loss_ref, dh_ref, dg_ref):
        i = pl.program_id(0)
        x = h_ref[...]
        gv = g_ref[...]
        rstd = lax.rsqrt(jnp.mean(x * x, axis=-1, keepdims=True) + NORM_EPS)
        xhat = x * rstd
        err = xhat * gv - t_ref[...]
        dy = err * (1.0 / d)
        dxh = dy * gv
        dh_ref[...] = rstd * (dxh - xhat * jnp.mean(dxh * xhat, axis=-1, keepdims=True))
        part = jnp.sum(dy * xhat, axis=0, keepdims=True)
        lpart = jnp.broadcast_to(jnp.sum(jnp.sum(err * err, axis=0, keepdims=True), axis=1, keepdims=True) * (0.5 / d), (1, LANES))

        @pl.when(i == 0)
        def _():
            dg_ref[...] = part
            loss_ref[...] = lpart

        @pl.when(i > 0)
        def _():
            dg_ref[...] += part
            loss_ref[...] += lpart

    row = pl.BlockSpec((ts, d), lambda i: (i, 0))
    vec = pl.BlockSpec((1, d), lambda i: (0, 0))
    return pl.pallas_call(
        body, name="loss_and_grad", grid=(s // ts,), in_specs=[row, vec, row],
        out_specs=[pl.BlockSpec((1, LANES), lambda i: (0, 0)), row, vec],
        out_shape=[jax.ShapeDtypeStruct((1, LANES), F32), jax.ShapeDtypeStruct((s, d), F32), jax.ShapeDtypeStruct((1, d), F32)],
        compiler_params=_cparams(),
    )(h, g, tgt)


def _mm_rms_bwd(name, a, b, h, g, dh_in, *, tm, tk, side=None):
    (m, kk), (n, kb) = a.shape, b.shape
    assert kk == kb and h.shape == (m, n), (name, a.shape, b.shape, h.shape)
    tm, tk = min(tm, m), min(tk, kk)
    nk = kk // tk
    dims = (((1,), (1,)), ((), ()))

    def body(a_ref, b_ref, h_ref, g_ref, dhin_ref, dh_ref, dg_ref, *acc):
        i, k = pl.program_id(0), pl.program_id(2)
        prod = lax.dot_general(a_ref[...].astype(BF16), b_ref[...].astype(BF16), dims, preferred_element_type=F32)

        def finish(dhn):
            x = h_ref[...]
            rstd = lax.rsqrt(jnp.mean(x * x, axis=-1, keepdims=True) + NORM_EPS)
            xhat = x * rstd
            dxh = dhn * g_ref[...]
            dh_ref[...] = dhin_ref[...] + rstd * (dxh - xhat * jnp.mean(dxh * xhat, axis=-1, keepdims=True))
            part = jnp.sum(dhn * xhat, axis=0, keepdims=True)

            @pl.when(i == 0)
            def _():
                dg_ref[...] = part

            @pl.when(i > 0)
            def _():
                dg_ref[...] += part

        if nk == 1:
            finish(prod)
        else:
            acc_ref = acc[0]

            @pl.when(k == 0)
            def _():
                acc_ref[...] = prod

            @pl.when(k > 0)
            def _():
                acc_ref[...] += prod

            @pl.when(k == nk - 1)
            def _():
                finish(acc_ref[...])

    row = pl.BlockSpec((tm, n), lambda i, j, k: (i, 0))
    vec = pl.BlockSpec((1, n), lambda i, j, k: (0, 0))
    ni = m // tm
    return _call_with_side(
        body, side, lambda: (pl.program_id(0) == 0) & (pl.program_id(2) == 0),
        lambda: (pl.program_id(0) == ni - 1) & (pl.program_id(2) == nk - 1),
        name=name, grid=(ni, 1, nk),
        in_specs=[a.spec(tm, tk, lambda i, j, k: (i, k)), b.spec(n, tk, lambda i, j, k: (0, k)), row, vec, row],
        out_specs=[row, vec], out_shape=[jax.ShapeDtypeStruct((m, n), F32), jax.ShapeDtypeStruct((1, n), F32)],
        scratch_shapes=[pltpu.VMEM((tm, n), F32)] if nk > 1 else [], args=(a.arr, b.arr, h, g, dh_in))


def _ffn_up_act(hn2, w_up4, conv_w2, conv_b2, ts=1024, tn=512, sub=1024, side=None):
    s, d = hn2.shape
    p, _, wc = w_up4.shape
    f = p * wc // 2
    ts, tn = min(ts, s), min(tn, wc)
    sub = min(sub, ts)
    per = wc // tn
    kw = FFN_CONV_W
    g0, g1 = f // tn, s // ts

    def body(hn_ref, w1_ref, w2_ref, cw_ref, cb_ref, up_ref, c_ref, act_ref, carry_ref):
        @pl.when(pl.program_id(1) == 0)
        def _():
            carry_ref[...] = jnp.zeros_like(carry_ref)

        for q in range(ts // sub):
            rows = slice(q * sub, (q + 1) * sub)
            hn = hn_ref[rows, :]
            cs = []
            for h, w_ref in enumerate((w1_ref, w2_ref)):
                x = jnp.dot(hn, w_ref[...], preferred_element_type=F32)
                up_ref[h, rows, :] = x
                halo = carry_ref[h]
                c = cb_ref[h] + cw_ref[h, kw - 1:kw, :] * x
                for sft in range(1, kw):
                    c = c + cw_ref[h, kw - 1 - sft:kw - sft, :] * _shift_down(x, halo, sft)
                carry_ref[h] = x[sub - SUBLANES:, :]
                c_ref[h, rows, :] = c
                cs.append(c)
            act_ref[rows, :] = (_gelu(cs[0]) * cs[1]).astype(BF16)

    outs, side_outs = _call_with_side(
        body, side, lambda: (pl.program_id(0) == 0) & (pl.program_id(1) == 0),
        lambda: (pl.program_id(0) == g0 - 1) & (pl.program_id(1) == g1 - 1),
        name="ffn_up_act", grid=(g0, g1),
        in_specs=[pl.BlockSpec((ts, d), lambda j, i: (i, 0)),
                  pl.BlockSpec((None, d, tn), lambda j, i: (j // per, 0, j % per)),
                  pl.BlockSpec((None, d, tn), lambda j, i: (p // 2 + j // per, 0, j % per)),
                  pl.BlockSpec((2, kw, tn), lambda j, i: (0, 0, j)),
                  pl.BlockSpec((2, 1, tn), lambda j, i: (0, 0, j))],
        out_specs=[pl.BlockSpec((2, ts, tn), lambda j, i: (0, i, j)), pl.BlockSpec((2, ts, tn), lambda j, i: (0, i, j)),
                   pl.BlockSpec((ts, tn), lambda j, i: (i, j))],
        out_shape=[jax.ShapeDtypeStruct((2, s, f), F32), jax.ShapeDtypeStruct((2, s, f), F32), jax.ShapeDtypeStruct((s, f), BF16)],
        scratch_shapes=[pltpu.VMEM((2, SUBLANES, tn), F32)], args=(hn2, w_up4, w_up4, conv_w2, conv_b2))
    return outs, side_outs


def _ffn_bwd_fused(dh, w_down, up2, c2, conv_w2, ts=1024, tn=512, side=None):
    s, d = dh.shape
    _, _, f = up2.shape
    ts, tn = min(ts, s), min(tn, f)
    kw = FFN_CONV_W
    nt = s // ts
    hb = ts // SUBLANES
    g0 = f // tn
    nt_dims = (((1,), (1,)), ((), ()))

    def body(dh_ref, wd_ref, up_ref, c_ref, w_ref, dup_ref, dw_ref, db_ref, carry_ref):
        i = pl.program_id(1)
        first_step = i == 0

        @pl.when(first_step)
        def _():
            carry_ref[...] = jnp.zeros_like(carry_ref)

        da = lax.dot_general(dh_ref[...].astype(BF16), wd_ref[...], nt_dims, preferred_element_type=F32)
        g1, dg1 = _gelu_and_grad(c_ref[0])
        dcs = [da * c_ref[1] * dg1, da * g1]
        for h in range(2):
            dc = dcs[h]
            after = carry_ref[h]
            ups = [dc] + [_shift_up(dc, after, sft) for sft in range(1, kw)]
            dup = w_ref[h, kw - 1:kw, :] * dc
            for sft in range(1, kw):
                dup = dup + w_ref[h, kw - 1 - sft:kw - sft, :] * ups[sft]
            carry_ref[h] = dc[:SUBLANES]
            dup_ref[h] = dup.astype(BF16)
            dbp = jnp.sum(dc, axis=0, keepdims=True)
            x = up_ref[h]
            dwp = [jnp.sum(ups[kw - 1 - k] * x, axis=0, keepdims=True) for k in range(kw)]

            @pl.when(first_step)
            def _():
                db_ref[h] = dbp
                for k in range(kw):
                    dw_ref[h, k:k + 1, :] = dwp[k]

            @pl.when(i > 0)
            def _():
                db_ref[h] += dbp
                for k in range(kw):
                    dw_ref[h, k:k + 1, :] += dwp[k]

    rev = lambda i: nt - 1 - i
    return _call_with_side(
        body, side, lambda: (pl.program_id(0) == 0) & (pl.program_id(1) == 0),
        lambda: (pl.program_id(0) == g0 - 1) & (pl.program_id(1) == nt - 1),
        name="ffn_bwd", grid=(g0, nt),
        in_specs=[pl.BlockSpec((ts, d), lambda j, i: (rev(i), 0)),
                  pl.BlockSpec((tn, d), lambda j, i: (j, 0)),
                  pl.BlockSpec((2, ts, tn), lambda j, i: (0, rev(i), j)),
                  pl.BlockSpec((2, ts, tn), lambda j, i: (0, rev(i), j)),
                  pl.BlockSpec((2, kw, tn), lambda j, i: (0, 0, j))],
        out_specs=[pl.BlockSpec((2, ts, tn), lambda j, i: (0, rev(i), j)),
                   pl.BlockSpec((2, kw, tn), lambda j, i: (0, 0, j)),
                   pl.BlockSpec((2, 1, tn), lambda j, i: (0, 0, j))],
        out_shape=[jax.ShapeDtypeStruct((2, s, f), BF16), jax.ShapeDtypeStruct((2, kw, f), F32),
                   jax.ShapeDtypeStruct((2, 1, f), F32)],
        scratch_shapes=[pltpu.VMEM((2, SUBLANES, tn), F32)], args=(dh, w_down, up2, c2, conv_w2))


def _rg_gates(xr, wa_ref, ba_ref, wx_ref, bx_ref, lam_ref):
    bw = wa_ref.shape[-1]
    xb = xr.astype(BF16)
    za = jnp.concatenate([jnp.dot(xb[:, h * bw:(h + 1) * bw], wa_ref[h], preferred_element_type=F32)
                          for h in range(RG_HEADS)], axis=1) + ba_ref[...]
    zx = jnp.concatenate([jnp.dot(xb[:, h * bw:(h + 1) * bw], wx_ref[h], preferred_element_type=F32)
                          for h in range(RG_HEADS)], axis=1) + bx_ref[...]
    r, ig = _sigmoid(za), _sigmoid(zx)
    sp = _softplus(-lam_ref[...])
    la = -RG_C * r * sp
    a = jnp.exp(la)
    mult = jnp.sqrt(_neg_expm1(2.0 * la))
    return xb, r, ig, sp, a, mult


def _rg_fwd(xg2, conv_w, conv_b, w_a, b_a, w_x, b_x, lam, ts=256, side=None):
    _, s, c = xg2.shape
    ts = min(ts, s)
    kw = RG_CONV_W
    hb = ts // SUBLANES

    def body(xg_ref, halo_ref, cw_ref, cb_ref, wa_ref, ba_ref, wx_ref, bx_ref, lam_ref, xr_ref, hs_ref, y_ref, carry_ref,
             a_scr, b_scr):
        i = pl.program_id(0)

        @pl.when(i == 0)
        def _():
            carry_ref[...] = jnp.zeros_like(carry_ref)

        xp = xg_ref[0]
        halo = jnp.where(i == 0, 0.0, halo_ref[...])
        xr = cb_ref[...] + cw_ref[kw - 1:kw, :] * xp
        for sft in range(1, kw):
            xr = xr + cw_ref[kw - 1 - sft:kw - sft, :] * _shift_down(xp, halo, sft)
        _, r, ig, sp, a, mult = _rg_gates(xr, wa_ref, ba_ref, wx_ref, bx_ref, lam_ref)
        a_scr[...] = a
        b_scr[...] = mult * (ig * xr)
        _real_slab_scan(a_scr, b_scr, hs_ref, carry_ref, reverse=False)
        xr_ref[...] = xr
        y_ref[...] = (hs_ref[...] * _gelu(xg_ref[1])).astype(BF16)

    full = lambda shape: pl.BlockSpec(shape, lambda i: (0,) * len(shape))
    row_spec = pl.BlockSpec((ts, c), lambda i: (i, 0))
    nt = s // ts
    return _call_with_side(
        body, side, lambda: pl.program_id(0) == 0, lambda: pl.program_id(0) == nt - 1,
        name="rg_fwd", grid=(nt,),
        in_specs=[pl.BlockSpec((2, ts, c), lambda i: (0, i, 0)),
                  pl.BlockSpec((None, SUBLANES, c), lambda i: (0, jnp.maximum(i * hb - 1, 0), 0)),
                  full(conv_w.shape), full(conv_b.shape), full(w_a.shape), full(b_a.shape), full(w_x.shape), full(b_x.shape),
                  full(lam.shape)],
        out_specs=[row_spec, row_spec, row_spec],
        out_shape=[jax.ShapeDtypeStruct((s, c), F32), jax.ShapeDtypeStruct((s, c), F32), jax.ShapeDtypeStruct((s, c), BF16)],
        scratch_shapes=[pltpu.VMEM((1, c), F32), pltpu.VMEM((ts, c), F32), pltpu.VMEM((ts, c), F32)],
        args=(xg2, xg2, conv_w, conv_b, w_a, b_a, w_x, b_x, lam))


def _rg_bwd(dy, xg2, xr, hs, conv_w, w_a, b_a, w_x, b_x, lam, ts=256, side=None):
    _, s, c = xg2.shape
    ts = min(ts, s)
    nt = s // ts
    kw = RG_CONV_W
    hb = ts // SUBLANES
    bw = c // RG_HEADS
    tn_dims = (((0,), (0,)), ((), ()))
    nt_dims = (((1,), (1,)), ((), ()))

    def body(dy_ref, xg_ref, xph_ref, xr_ref, hs_ref, hsh_ref, cw_ref, wa_ref, ba_ref, wx_ref, bx_ref, lam_ref,
             dxg_ref, dcw_ref, dcb_ref, dwa_ref, dba_ref, dwx_ref, dbx_ref, dlam_ref,
             lam_carry, a_carry, dxr_carry, dsp_acc, a_scr, b_scr):
        i = pl.program_id(0)
        first_step = i == 0
        time_first = i == nt - 1

        @pl.when(first_step)
        def _():
            lam_carry[...] = jnp.zeros_like(lam_carry)
            a_carry[...] = jnp.ones_like(a_carry)
            dxr_carry[...] = jnp.zeros_like(dxr_carry)
            dsp_acc[...] = jnp.zeros_like(dsp_acc)
            for ref in (dcw_ref, dcb_ref, dwa_ref, dba_ref, dwx_ref, dbx_ref):
                ref[...] = jnp.zeros_like(ref)

        xr = xr_ref[...]
        hs = hs_ref[...]
        gate = xg_ref[1]
        xb, r, ig, sp, a, mult = _rg_gates(xr, wa_ref, ba_ref, wx_ref, bx_ref, lam_ref)
        dyv = dy_ref[...]
        gg, dgg = _gelu_and_grad(gate)
        dhs = dyv * gg
        dxg_ref[1] = (dyv * hs * dgg).astype(BF16)
        row = _rows(xr.shape)
        a_scr[...] = jnp.where(row == ts - 1, a_carry[0:1, :], pltpu.roll(a, ts - 1, 0))
        b_scr[...] = dhs
        _real_slab_scan(a_scr, b_scr, b_scr, lam_carry, reverse=True)
        lmb = b_scr[...]
        a_carry[...] = a[:SUBLANES]
        hs_prev = _shift_down(hs, jnp.where(time_first, 0.0, hsh_ref[...]), 1)
        d_a = lmb * hs_prev
        d_m = lmb * (ig * xr)
        d_ig = lmb * mult * xr
        d_xr = lmb * mult * ig
        d_la = a * d_a - (a * a / mult) * d_m
        dsp_acc[...] += jnp.sum(-RG_C * r * d_la, axis=0, keepdims=True)
        d_za = (-RG_C * sp) * d_la * r * (1.0 - r)
        d_zx = d_ig * ig * (1.0 - ig)
        dba_ref[...] += jnp.sum(d_za, axis=0, keepdims=True)
        dbx_ref[...] += jnp.sum(d_zx, axis=0, keepdims=True)
        dzab, dzxb = d_za.astype(BF16), d_zx.astype(BF16)
        back = []
        for h in range(RG_HEADS):
            sl = slice(h * bw, (h + 1) * bw)
            dwa_ref[h] += lax.dot_general(xb[:, sl], dzab[:, sl], tn_dims, preferred_element_type=F32)
            dwx_ref[h] += lax.dot_general(xb[:, sl], dzxb[:, sl], tn_dims, preferred_element_type=F32)
            back.append(lax.dot_general(dzab[:, sl], wa_ref[h], nt_dims, preferred_element_type=F32)
                        + lax.dot_general(dzxb[:, sl], wx_ref[h], nt_dims, preferred_element_type=F32))
        d_xr = d_xr + jnp.concatenate(back, axis=1)
        d_xp = cw_ref[kw - 1:kw, :] * d_xr
        after = dxr_carry[...]
        for sft in range(1, kw):
            d_xp = d_xp + cw_ref[kw - 1 - sft:kw - sft, :] * _shift_up(d_xr, after, sft)
        dxr_carry[...] = d_xr[:SUBLANES]
        dxg_ref[0] = d_xp.astype(BF16)
        xp = xg_ref[0]
        before = jnp.where(time_first, 0.0, xph_ref[...])
        dcb_ref[...] += jnp.sum(d_xr, axis=0, keepdims=True)
        dcw_ref[kw - 1:kw, :] += jnp.sum(d_xr * xp, axis=0, keepdims=True)
        for sft in range(1, kw):
            dcw_ref[kw - 1 - sft:kw - sft, :] += jnp.sum(d_xr * _shift_down(xp, before, sft), axis=0, keepdims=True)
        dlam_ref[...] = dsp_acc[...] * (-_sigmoid(-lam_ref[...]))

    full = lambda shape: pl.BlockSpec(shape, lambda i: (0,) * len(shape))
    rev = lambda i: nt - 1 - i
    row_spec = pl.BlockSpec((ts, c), lambda i: (rev(i), 0))
    halo_idx = lambda i: jnp.maximum(rev(i) * hb - 1, 0)
    vec = (1, c)
    return _call_with_side(
        body, side, lambda: pl.program_id(0) == 0, lambda: pl.program_id(0) == nt - 1,
        name="rg_bwd", grid=(nt,),
        in_specs=[row_spec,
                  pl.BlockSpec((2, ts, c), lambda i: (0, rev(i), 0)),
                  pl.BlockSpec((None, SUBLANES, c), lambda i: (0, halo_idx(i), 0)),
                  row_spec, row_spec,
                  pl.BlockSpec((SUBLANES, c), lambda i: (halo_idx(i), 0)),
                  full(conv_w.shape), full(w_a.shape), full(b_a.shape), full(w_x.shape), full(b_x.shape), full(lam.shape)],
        out_specs=[pl.BlockSpec((2, ts, c), lambda i: (0, rev(i), 0)), full(conv_w.shape), full(vec), full(w_a.shape), full(vec),
                   full(w_x.shape), full(vec), full(vec)],
        out_shape=[jax.ShapeDtypeStruct((2, s, c), BF16), jax.ShapeDtypeStruct(conv_w.shape, F32), jax.ShapeDtypeStruct(vec, F32),
                   jax.ShapeDtypeStruct(w_a.shape, F32), jax.ShapeDtypeStruct(vec, F32), jax.ShapeDtypeStruct(w_x.shape, F32),
                   jax.ShapeDtypeStruct(vec, F32), jax.ShapeDtypeStruct(vec, F32)],
        scratch_shapes=[pltpu.VMEM(vec, F32), pltpu.VMEM((SUBLANES, c), F32), pltpu.VMEM((SUBLANES, c), F32),
                        pltpu.VMEM(vec, F32), pltpu.VMEM((ts, c), F32), pltpu.VMEM((ts, c), F32)],
        args=(dy, xg2, xg2, xr, hs, hs, conv_w, w_a, b_a, w_x, b_x, lam))


def _s5_param_fn(a_re, a_im, log_dt, bt_re, bt_im):
    dt = jnp.exp(log_dt)
    mag = jnp.exp(a_re * dt)
    abr = mag * jnp.cos(a_im * dt)
    abi = mag * jnp.sin(a_im * dt)
    ur, ui = abr - 1.0, abi
    den = a_re * a_re + a_im * a_im
    wr = (ur * a_re + ui * a_im) / den
    wi = (ui * a_re - ur * a_im) / den
    bbr = wr[None] * bt_re - wi[None] * bt_im
    bbi = wr[None] * bt_im + wi[None] * bt_re
    return abr, abi, bbr, bbi


def _s5_params_bwd(a_re, a_im, log_dt, bt_re, bt_im, d_abr, d_abi, d_bbr, d_bbi):
    def body(ar_ref, ai_ref, dt_ref, br_ref, bi_ref, g0, g1, g2, g3, o0, o1, o2, o3, o4):
        _, vjp = jax.vjp(_s5_param_fn, ar_ref[...], ai_ref[...], dt_ref[...], br_ref[...], bi_ref[...])
        outs = vjp((g0[...], g1[...], g2[...], g3[...]))
        for o, v in zip((o0, o1, o2, o3, o4), outs):
            o[...] = v

    sd = jax.ShapeDtypeStruct
    return pl.pallas_call(
        body, name="s5_params_bwd",
        out_shape=[sd(a_re.shape, F32), sd(a_im.shape, F32), sd(log_dt.shape, F32), sd(bt_re.shape, F32), sd(bt_im.shape, F32)],
    )(a_re, a_im, log_dt, bt_re, bt_im, d_abr, d_abi, d_bbr, d_bbi)


S5_LANE_CHUNK = 512


def _cmul_add(br, bi, tr, ti, sr, si):
    return br + tr * sr - ti * si, bi + tr * si + ti * sr


def _s5_tables3(a_re, a_im, log_dt, bt_re, bt_im, seg):
    g, p = a_re.shape
    gc = bt_re.shape[0]
    nsq = int(math.log2(seg))
    assert 1 << nsq == seg

    def body(ar_ref, ai_ref, dt_ref, br_ref, bi_ref, tr_ref, ti_ref, rtr_ref, rti_ref, bbr_ref, bbi_ref):
        abr, abi, bbr, bbi = _s5_param_fn(ar_ref[...], ai_ref[...], dt_ref[...], br_ref[...], bi_ref[...])
        bbr_ref[...] = bbr
        bbi_ref[...] = bbi
        qr, qi = abr, abi
        for _ in range(nsq):
            qr, qi = qr * qr - qi * qi, 2.0 * qr * qi
        pows = [(qr, qi)]
        for _ in range(1, SUBLANES):
            cr, ci = pows[-1]
            pows.append((cr * qr - ci * qi, cr * qi + ci * qr))
        zero = jnp.zeros_like(abr)
        for r in range(SUBLANES):
            rows = [(pows[(1 << k) - 1] if r >= (1 << k) else (zero, zero)) for k in range(3)] + [pows[r], (abr, abi)]
            for k, (vr, vi) in enumerate(rows):
                tr_ref[k, r] = vr
                ti_ref[k, r] = vi
                rtr_ref[k, SUBLANES - 1 - r] = vr
                rti_ref[k, SUBLANES - 1 - r] = -vi

    sd = jax.ShapeDtypeStruct
    tab = sd((5, SUBLANES, g, p), F32)
    return pl.pallas_call(
        body, name="s5_tables", out_shape=[tab, tab, tab, tab, sd((gc, g, p), F32), sd((gc, g, p), F32)],
    )(a_re, a_im, log_dt, bt_re, bt_im)


def _segment_perm(ts):
    seg = ts // SUBLANES
    rho = jnp.arange(ts)
    src = (rho % SUBLANES) * seg + rho // SUBLANES
    return (src[:, None] == jnp.arange(ts)[None, :]).astype(BF16)


def _exact_rows(perm_t, x):
    hi = x.astype(BF16)
    r1 = x - hi.astype(F32)
    mid = r1.astype(BF16)
    lo = (r1 - mid.astype(F32)).astype(BF16)
    dot = lambda v: jnp.dot(perm_t, v, preferred_element_type=F32)
    return (dot(hi) + dot(mid)) + dot(lo)


def _s5_fwd3(u, perm, perm_t, tab_r, tab_i, bp_r, bp_i, cp_r, cp_i, dvec, ts=256, side=None):
    s, c = u.shape
    n = tab_r.shape[2]
    nblk, cb, nb = bp_r.shape
    ts = min(ts, s)
    seg = ts // SUBLANES
    lc = min(S5_LANE_CHUNK, n)

    def body(u_ref, p_ref, pt_ref, tr_ref, ti_ref, bpr_ref, bpi_ref, cpr_ref, cpi_ref, d_ref, hr_ref, hi_ref, yp_ref, gy_ref,
             bur_ref, bui_ref, car_r, car_i):
        i = pl.program_id(0)

        @pl.when(i == 0)
        def _():
            car_r[...] = jnp.zeros_like(car_r)
            car_i[...] = jnp.zeros_like(car_i)

        uv = u_ref[...]
        ubp = jnp.dot(p_ref[...], uv.astype(BF16), preferred_element_type=F32).astype(BF16)
        for k in range(nblk):
            bur_ref[:, k * nb:(k + 1) * nb] = jnp.dot(ubp[:, k * cb:(k + 1) * cb], bpr_ref[k], preferred_element_type=F32)
            bui_ref[:, k * nb:(k + 1) * nb] = jnp.dot(ubp[:, k * cb:(k + 1) * cb], bpi_ref[k], preferred_element_type=F32)
        row8 = _rows((SUBLANES, lc))
        for q in range(n // lc):
            sl = slice(q * lc, (q + 1) * lc)
            tabs = [(tr_ref[k, :, sl], ti_ref[k, :, sl]) for k in range(5)]
            a_r, a_i = tabs[4]

            def local(r, carry, sl=sl, a_r=a_r, a_i=a_i):
                r0 = pl.multiple_of(r * SUBLANES, SUBLANES)
                hr, hi = _cmul_add(bur_ref[pl.ds(r0, SUBLANES), sl], bui_ref[pl.ds(r0, SUBLANES), sl], a_r, a_i, carry[0], carry[1])
                hr_ref[pl.ds(r0, SUBLANES), sl] = hr
                hi_ref[pl.ds(r0, SUBLANES), sl] = hi
                return hr, hi

            zero = jnp.zeros((SUBLANES, lc), F32)
            er, ei = lax.fori_loop(0, seg, local, (zero, zero), unroll=4)
            for k in range(3):
                sh = 1 << k
                er, ei = _cmul_add(er, ei, tabs[k][0], tabs[k][1], pltpu.roll(er, sh, 0), pltpu.roll(ei, sh, 0))
            cin_r, cin_i = jnp.broadcast_to(car_r[:, sl], er.shape), jnp.broadcast_to(car_i[:, sl], ei.shape)
            er, ei = _cmul_add(er, ei, tabs[3][0], tabs[3][1], cin_r, cin_i)
            car_r[:, sl] = er[SUBLANES - 1:, :]
            car_i[:, sl] = ei[SUBLANES - 1:, :]
            c_r = jnp.where(row8 == 0, cin_r, pltpu.roll(er, 1, 0))
            c_i = jnp.where(row8 == 0, cin_i, pltpu.roll(ei, 1, 0))

            def fix(r, carry, sl=sl, a_r=a_r, a_i=a_i, c_r=c_r, c_i=c_i):
                pr, pi = carry
                r0 = pl.multiple_of(r * SUBLANES, SUBLANES)
                hr, hi = _cmul_add(hr_ref[pl.ds(r0, SUBLANES), sl], hi_ref[pl.ds(r0, SUBLANES), sl], pr, pi, c_r, c_i)
                hr_ref[pl.ds(r0, SUBLANES), sl] = hr
                hi_ref[pl.ds(r0, SUBLANES), sl] = hi
                return pr * a_r - pi * a_i, pr * a_i + pi * a_r

            lax.fori_loop(0, seg, fix, (a_r, a_i), unroll=4)
        hrb, hib = hr_ref[...].astype(BF16), hi_ref[...].astype(BF16)
        y = jnp.concatenate([jnp.dot(hrb[:, k * nb:(k + 1) * nb], cpr_ref[k], preferred_element_type=F32)
                             - jnp.dot(hib[:, k * nb:(k + 1) * nb], cpi_ref[k], preferred_element_type=F32) for k in range(nblk)], axis=1)
        yp = _exact_rows(pt_ref[...], y) + d_ref[...] * uv
        yp_ref[...] = yp
        gy_ref[...] = _gelu(yp).astype(BF16)

    full = lambda shape: pl.BlockSpec(shape, lambda i: (0,) * len(shape))
    rc = pl.BlockSpec((ts, c), lambda i: (i, 0))
    rn = pl.BlockSpec((ts, n), lambda i: (i, 0))
    sd = jax.ShapeDtypeStruct
    nt = s // ts
    return _call_with_side(
        body, side, lambda: pl.program_id(0) == 0, lambda: pl.program_id(0) == nt - 1,
        name="s5_fwd", grid=(nt,),
        in_specs=[rc, full(perm.shape), full(perm_t.shape), full(tab_r.shape), full(tab_i.shape), full(bp_r.shape), full(bp_i.shape),
                  full(cp_r.shape), full(cp_i.shape), full(dvec.shape)],
        out_specs=[rn, rn, rc, rc],
        out_shape=[sd((s, n), F32), sd((s, n), F32), sd((s, c), F32), sd((s, c), BF16)],
        scratch_shapes=[pltpu.VMEM((ts, n), F32), pltpu.VMEM((ts, n), F32), pltpu.VMEM((1, n), F32), pltpu.VMEM((1, n), F32)],
        args=(u, perm, perm_t, tab_r, tab_i, bp_r, bp_i, cp_r, cp_i, dvec))


def _s5_bwd3(dgy, ypre, u, hr, hi, perm, perm_t, rtab_r, rtab_i, bp_r, bp_i, cp_r, cp_i, dvec, ts=256, side=None):
    s, c = u.shape
    n = rtab_r.shape[2]
    nblk, cb, nb = bp_r.shape
    ts = min(ts, s)
    nt = s // ts
    hb = ts // SUBLANES
    seg = ts // SUBLANES
    lc = min(S5_LANE_CHUNK, n)
    tn_dims = (((0,), (0,)), ((), ()))
    nt_dims = (((1,), (1,)), ((), ()))

    def body(dgy_ref, yp_ref, u_ref, hr_ref, hi_ref, hrh_ref, hih_ref, p_ref, pt_ref, tr_ref, ti_ref, bpr_ref, bpi_ref,
             cpr_ref, cpi_ref, d_ref, du_ref, dar_ref, dai_ref, dbr_ref, dbi_ref, dcr_ref, dci_ref, dd_ref, lr_ref, li_ref,
             car_r, car_i):
        i = pl.program_id(0)
        time_first = i == nt - 1

        @pl.when(i == 0)
        def _():
            car_r[...] = jnp.zeros_like(car_r)
            car_i[...] = jnp.zeros_like(car_i)
            for ref in (dar_ref, dai_ref, dbr_ref, dbi_ref, dcr_ref, dci_ref, dd_ref):
                ref[...] = jnp.zeros_like(ref)

        uv = u_ref[...]
        _, dgel = _gelu_and_grad(yp_ref[...])
        dyv = dgy_ref[...] * dgel
        dd_ref[...] += jnp.sum(dyv * uv, axis=0, keepdims=True)
        perm_m = p_ref[...]
        dyb = jnp.dot(perm_m, dyv.astype(BF16), preferred_element_type=F32).astype(BF16)
        ub = jnp.dot(perm_m, uv.astype(BF16), preferred_element_type=F32).astype(BF16)
        hrb, hib = hr_ref[...].astype(BF16), hi_ref[...].astype(BF16)
        for k in range(nblk):
            dblk = dyb[:, k * cb:(k + 1) * cb]
            lr_ref[:, k * nb:(k + 1) * nb] = lax.dot_general(dblk, cpr_ref[k], nt_dims, preferred_element_type=F32)
            li_ref[:, k * nb:(k + 1) * nb] = -lax.dot_general(dblk, cpi_ref[k], nt_dims, preferred_element_type=F32)
            dcr_ref[k] += lax.dot_general(hrb[:, k * nb:(k + 1) * nb], dblk, tn_dims, preferred_element_type=F32)
            dci_ref[k] += lax.dot_general(hib[:, k * nb:(k + 1) * nb], dblk, tn_dims, preferred_element_type=F32)
        row8 = _rows((SUBLANES, lc))
        last0 = (seg - 1) * SUBLANES
        for q in range(n // lc):
            sl = slice(q * lc, (q + 1) * lc)
            tabs = [(tr_ref[k, :, sl], ti_ref[k, :, sl]) for k in range(5)]
            a_r, a_i = tabs[4]

            def local(rr, carry, sl=sl, a_r=a_r, a_i=a_i):
                r0 = pl.multiple_of((seg - 1 - rr) * SUBLANES, SUBLANES)
                lr, li = _cmul_add(lr_ref[pl.ds(r0, SUBLANES), sl], li_ref[pl.ds(r0, SUBLANES), sl], a_r, a_i, carry[0], carry[1])
                lr_ref[pl.ds(r0, SUBLANES), sl] = lr
                li_ref[pl.ds(r0, SUBLANES), sl] = li
                return lr, li

            zero = jnp.zeros((SUBLANES, lc), F32)
            er, ei = lax.fori_loop(0, seg, local, (zero, zero), unroll=4)
            for k in range(3):
                sh = 1 << k
                er, ei = _cmul_add(er, ei, tabs[k][0], tabs[k][1], pltpu.roll(er, SUBLANES - sh, 0), pltpu.roll(ei, SUBLANES - sh, 0))
            cin_r, cin_i = jnp.broadcast_to(car_r[:, sl], er.shape), jnp.broadcast_to(car_i[:, sl], ei.shape)
            er, ei = _cmul_add(er, ei, tabs[3][0], tabs[3][1], cin_r, cin_i)
            car_r[:, sl] = er[:1, :]
            car_i[:, sl] = ei[:1, :]
            c_r = jnp.where(row8 == SUBLANES - 1, cin_r, pltpu.roll(er, SUBLANES - 1, 0))
            c_i = jnp.where(row8 == SUBLANES - 1, cin_i, pltpu.roll(ei, SUBLANES - 1, 0))
            halo_r = jnp.where(time_first, 0.0, hrh_ref[SUBLANES - 1:, sl])
            halo_i = jnp.where(time_first, 0.0, hih_ref[SUBLANES - 1:, sl])
            hp0_r = jnp.where(row8 == 0, jnp.broadcast_to(halo_r, zero.shape), pltpu.roll(hr_ref[pl.ds(last0, SUBLANES), sl], 1, 0))
            hp0_i = jnp.where(row8 == 0, jnp.broadcast_to(halo_i, zero.shape), pltpu.roll(hi_ref[pl.ds(last0, SUBLANES), sl], 1, 0))

            def fix(rr, carry, sl=sl, a_r=a_r, a_i=a_i, c_r=c_r, c_i=c_i, hp0_r=hp0_r, hp0_i=hp0_i):
                pr, pi, acc_r, acc_i = carry
                r = seg - 1 - rr
                r0 = pl.multiple_of(r * SUBLANES, SUBLANES)
                lr, li = _cmul_add(lr_ref[pl.ds(r0, SUBLANES), sl], li_ref[pl.ds(r0, SUBLANES), sl], pr, pi, c_r, c_i)
                lr_ref[pl.ds(r0, SUBLANES), sl] = lr
                li_ref[pl.ds(r0, SUBLANES), sl] = li
                p0 = pl.multiple_of(jnp.maximum(r - 1, 0) * SUBLANES, SUBLANES)
                hpr = jnp.where(r == 0, hp0_r, hr_ref[pl.ds(p0, SUBLANES), sl])
                hpi = jnp.where(r == 0, hp0_i, hi_ref[pl.ds(p0, SUBLANES), sl])
                return (pr * a_r - pi * a_i, pr * a_i + pi * a_r, acc_r + (lr * hpr + li * hpi), acc_i + (li * hpr - lr * hpi))

            _, _, acc_r, acc_i = lax.fori_loop(0, seg, fix, (a_r, a_i, zero, zero), unroll=4)
            dar_ref[:, sl] += jnp.sum(acc_r, axis=0, keepdims=True)
            dai_ref[:, sl] += jnp.sum(acc_i, axis=0, keepdims=True)
        lrb, lib = lr_ref[...].astype(BF16), li_ref[...].astype(BF16)
        du = []
        for k in range(nblk):
            ublk = ub[:, k * cb:(k + 1) * cb]
            lrk, lik = lrb[:, k * nb:(k + 1) * nb], lib[:, k * nb:(k + 1) * nb]
            dbr_ref[k] += lax.dot_general(ublk, lrk, tn_dims, preferred_element_type=F32)
            dbi_ref[k] += lax.dot_general(ublk, lik, tn_dims, preferred_element_type=F32)
            du.append(lax.dot_general(lrk, bpr_ref[k], nt_dims, preferred_element_type=F32)
                      + lax.dot_general(lik, bpi_ref[k], nt_dims, preferred_element_type=F32))
        du_ref[...] = (d_ref[...] * dyv + _exact_rows(pt_ref[...], jnp.concatenate(du, axis=1))).astype(BF16)

    full = lambda shape: pl.BlockSpec(shape, lambda i: (0,) * len(shape))
    rev = lambda i: nt - 1 - i
    halo_idx = lambda i: jnp.maximum(rev(i) * hb - 1, 0)
    rc = pl.BlockSpec((ts, c), lambda i: (rev(i), 0))
    rn = pl.BlockSpec((ts, n), lambda i: (rev(i), 0))
    hn = pl.BlockSpec((SUBLANES, n), lambda i: (halo_idx(i), 0))
    sd = jax.ShapeDtypeStruct
    vec_n = (1, n)
    return _call_with_side(
        body, side, lambda: pl.program_id(0) == 0, lambda: pl.program_id(0) == nt - 1,
        name="s5_bwd", grid=(nt,),
        in_specs=[rc, rc, rc, rn, rn, hn, hn, full(perm.shape), full(perm_t.shape), full(rtab_r.shape), full(rtab_i.shape),
                  full(bp_r.shape), full(bp_i.shape), full(cp_r.shape), full(cp_i.shape), full(dvec.shape)],
        out_specs=[rc, full(vec_n), full(vec_n), full(bp_r.shape), full(bp_i.shape), full(cp_r.shape), full(cp_i.shape),
                   full(dvec.shape)],
        out_shape=[sd((s, c), BF16), sd(vec_n, F32), sd(vec_n, F32), sd(bp_r.shape, F32), sd(bp_i.shape, F32),
                   sd(cp_r.shape, F32), sd(cp_i.shape, F32), sd(dvec.shape, F32)],
        scratch_shapes=[pltpu.VMEM((ts, n), F32), pltpu.VMEM((ts, n), F32), pltpu.VMEM((1, n), F32), pltpu.VMEM((1, n), F32)],
        args=(dgy, ypre, u, hr, hi, hr, hi, perm, perm_t, rtab_r, rtab_i, bp_r, bp_i, cp_r, cp_i, dvec))


def _glu_mm(gy, w_glu4, ts=1024, tn=512):
    s, d = gy.shape
    p, _, wc = w_glu4.shape
    c = p * wc // 2
    ts, tn = min(ts, s), min(tn, wc)
    per = wc // tn

    def body(x_ref, w1_ref, w2_ref, gl_ref, o_ref):
        x = x_ref[...]
        val = jnp.dot(x, w1_ref[...], preferred_element_type=F32)
        gate = jnp.dot(x, w2_ref[...], preferred_element_type=F32)
        gl_ref[0] = val
        gl_ref[1] = gate
        o_ref[...] = (val * _sigmoid(gate)).astype(BF16)

    return pl.pallas_call(
        body, name="s5_glu", grid=(c // tn, s // ts),
        in_specs=[pl.BlockSpec((ts, d), lambda j, i: (i, 0)),
                  pl.BlockSpec((None, d, tn), lambda j, i: (j // per, 0, j % per)),
                  pl.BlockSpec((None, d, tn), lambda j, i: (p // 2 + j // per, 0, j % per))],
        out_specs=[pl.BlockSpec((2, ts, tn), lambda j, i: (0, i, j)), pl.BlockSpec((ts, tn), lambda j, i: (i, j))],
        out_shape=[jax.ShapeDtypeStruct((2, s, c), F32), jax.ShapeDtypeStruct((s, c), BF16)], compiler_params=_cparams(),
    )(gy, w_glu4, w_glu4)


def _glu_bwd_mm(dh, w_out, gl2, ts=512):
    s, d = dh.shape
    c = w_out.shape[0]
    ts = min(ts, s)
    nt_dims = (((1,), (1,)), ((), ()))

    def body(dh_ref, w_ref, g_ref, o_ref):
        dov = lax.dot_general(dh_ref[...].astype(BF16), w_ref[...], nt_dims, preferred_element_type=F32)
        sg = _sigmoid(g_ref[1])
        o_ref[0] = (dov * sg).astype(BF16)
        o_ref[1] = (dov * g_ref[0] * sg * (1.0 - sg)).astype(BF16)

    blk = pl.BlockSpec((2, ts, c), lambda i: (0, i, 0))
    return pl.pallas_call(
        body, name="s5_out_dx", grid=(s // ts,),
        in_specs=[pl.BlockSpec((ts, d), lambda i: (i, 0)), pl.BlockSpec((c, d), lambda i: (0, 0)), blk],
        out_specs=blk, out_shape=jax.ShapeDtypeStruct((2, s, c), BF16), compiler_params=_cparams(),
    )(dh, w_out, gl2)


PACK_ROW_MULTIPLE = 1024
ELEMENTWISE_BLOCK_ELEMS = 256 * 1024


def _row_tile(rows, cols):
    pref = max(SUBLANES, 1 << int(math.log2(max(1, ELEMENTWISE_BLOCK_ELEMS // cols))))
    if rows <= pref:
        return rows
    t = pref
    while rows % t:
        t //= 2
    assert t >= SUBLANES, rows
    return t


def _sum_parts(rs, side=None):
    nl = len(rs)
    p, rows, cols = rs[0].shape
    tr = _row_tile(rows, cols)
    nt = rows // tr

    def body(*refs):
        o_ref = refs[nl]
        for l in range(nl):
            acc = refs[l][0].astype(F32)
            for k in range(1, p):
                acc = acc + refs[l][k].astype(F32)
            o_ref[l] = acc

    outs, got = _call_with_side(
        body, side, lambda: pl.program_id(0) == 0, lambda: pl.program_id(0) == nt - 1,
        name="sum_parts", grid=(nt,), in_specs=[pl.BlockSpec((p, tr, cols), lambda i: (0, i, 0))] * nl,
        out_specs=[pl.BlockSpec((nl, tr, cols), lambda i: (0, i, 0))], out_shape=[jax.ShapeDtypeStruct((nl, rows, cols), F32)],
        scratch_shapes=[], args=tuple(rs))
    return outs[0], got


def _adamw(w, g_parts, m, v, side=None):
    rows, cols = w.shape
    tr = _row_tile(rows, max(cols, LANES))
    ng = len(g_parts)
    emit_grad = ng > 1
    c1 = 1.0 / (1.0 - ADAM_B1 ** ADAM_STEP)
    c2 = 1.0 / (1.0 - ADAM_B2 ** ADAM_STEP)

    def body(*refs):
        w_ref, m_ref, v_ref = refs[0], refs[1 + ng], refs[2 + ng]
        dl_ref, nm_ref, nv_ref = refs[3 + ng:6 + ng]
        g = refs[1][...]
        for k in range(1, ng):
            g = g + refs[1 + k][...]
        mn = ADAM_B1 * m_ref[...] + (1.0 - ADAM_B1) * g
        vn = ADAM_B2 * v_ref[...] + (1.0 - ADAM_B2) * (g * g)
        if emit_grad:
            refs[6 + ng][...] = g
        nm_ref[...] = mn
        nv_ref[...] = vn
        dl_ref[...] = -ADAM_LR * ((mn * c1) / (jnp.sqrt(vn * c2) + ADAM_EPS) + ADAM_WD * w_ref[...])

    blk = pl.BlockSpec((tr, cols), lambda i: (i, 0))
    sd = jax.ShapeDtypeStruct((rows, cols), F32)
    nout = 4 if emit_grad else 3
    nt = rows // tr
    return _call_with_side(
        body, side, lambda: pl.program_id(0) == 0, lambda: pl.program_id(0) == nt - 1,
        name="adamw", grid=(nt,), in_specs=[blk] * (3 + ng), out_specs=[blk] * nout, out_shape=[sd] * nout,
        scratch_shapes=[], args=(w, *g_parts, m, v))


def _place():
    x, y, c = lax.axis_index("x"), lax.axis_index("y"), lax.axis_index("c")
    chips = [(1 - x, y), (x, 1 - y), (1 - x, 1 - y)]
    return x, y, c, chips


class Side:
    def __init__(self, ins, outs, kind):
        self.ins, self.outs, self.kind = list(ins), list(outs), kind
        n = len(self.ins)
        self.sems = [pltpu.SemaphoreType.DMA((3 * n,)), pltpu.SemaphoreType.DMA((3 * n,)), pltpu.SemaphoreType.DMA((n,))]

    def _copies(self, ins, outs, send, recv, lsem):
        x, y, c, chips = _place()
        me = 2 * x + y
        local, out_going, in_coming = [], [], []
        for t in range(len(ins)):
            if self.kind == 'sibling':
                cp = pltpu.make_async_remote_copy(src_ref=ins[t], dst_ref=outs[t], send_sem=send.at[t], recv_sem=recv.at[t],
                                                  device_id=(x, y, 1 - c), device_id_type=MESH)
                out_going.append(cp)
                in_coming.append(cp)
                continue
            if self.kind == 'gather':
                src_local, srcs, dst_mine = ins[t], [ins[t]] * 3, outs[t].at[me]
            else:
                src_local, srcs, dst_mine = ins[t].at[me], [ins[t].at[2 * px + py] for px, py in chips], outs[t].at[me]
            local.append(pltpu.make_async_copy(src_local, dst_mine, lsem.at[t]))
            for r, (px, py) in enumerate(chips):
                out_going.append(pltpu.make_async_remote_copy(
                    src_ref=srcs[r], dst_ref=dst_mine, send_sem=send.at[3 * t + r], recv_sem=recv.at[3 * t + r],
                    device_id=(px, py, c), device_id_type=MESH))
                in_coming.append(pltpu.make_async_remote_copy(
                    src_ref=srcs[r], dst_ref=outs[t].at[2 * px + py], send_sem=send.at[3 * t + r], recv_sem=recv.at[3 * t + r],
                    device_id=(px, py, c), device_id_type=MESH))
        return local, out_going, in_coming

    def start(self, ins, outs, send, recv, lsem):
        local, out_going, _ = self._copies(ins, outs, send, recv, lsem)
        for cp in local + out_going:
            cp.start()

    def wait(self, ins, outs, send, recv, lsem):
        local, out_going, in_coming = self._copies(ins, outs, send, recv, lsem)
        for cp in in_coming:
            cp.wait_recv()
        for cp in out_going:
            cp.wait_send()
        for cp in local:
            cp.wait()


def _gather_side(shards):
    return Side(shards, [jax.ShapeDtypeStruct((N_CHIPS,) + s.shape, s.dtype) for s in shards], 'gather')


def _scatter_side(grads):
    return Side(grads, [jax.ShapeDtypeStruct(g.shape, g.dtype) for g in grads], 'scatter')


def _sibling_side(arrs):
    return Side(arrs, [jax.ShapeDtypeStruct(a.shape, a.dtype) for a in arrs], 'sibling')


def _call_with_side(body, side, first, last, *, name, grid, in_specs, out_specs, out_shape, scratch_shapes, args):
    if side is None:
        outs = pl.pallas_call(body, name=name, grid=grid, in_specs=in_specs, out_specs=out_specs, out_shape=out_shape,
                              scratch_shapes=scratch_shapes, compiler_params=_cparams())(*args)
        return outs, []
    n_in, n_out, n_sc = len(in_specs), len(out_specs), len(scratch_shapes)
    ns_in, ns_out = len(side.ins), len(side.outs)

    def wrapped(*refs):
        base_in, s_in = refs[:n_in], refs[n_in:n_in + ns_in]
        o0 = n_in + ns_in
        base_out, s_out = refs[o0:o0 + n_out], refs[o0 + n_out:o0 + n_out + ns_out]
        sc0 = o0 + n_out + ns_out
        base_sc, sems = refs[sc0:sc0 + n_sc], refs[sc0 + n_sc:]

        @pl.when(first())
        def _():
            side.start(s_in, s_out, *sems)

        body(*base_in, *base_out, *base_sc)

        @pl.when(last())
        def _():
            side.wait(s_in, s_out, *sems)

    any_spec = pl.BlockSpec(memory_space=pl.ANY)
    outs = pl.pallas_call(
        wrapped, name=name, grid=grid, in_specs=list(in_specs) + [any_spec] * ns_in, out_specs=list(out_specs) + [any_spec] * ns_out,
        out_shape=list(out_shape) + side.outs, scratch_shapes=list(scratch_shapes) + side.sems, compiler_params=_cparams(),
    )(*args, *side.ins)
    return outs[:n_out], outs[n_out:]


def _run_side(name, side):
    def body(*refs):
        n = len(side.ins)
        side.start(refs[:n], refs[n:2 * n], *refs[2 * n:])
        side.wait(refs[:n], refs[n:2 * n], *refs[2 * n:])

    any_spec = pl.BlockSpec(memory_space=pl.ANY)
    return pl.pallas_call(body, name=name, in_specs=[any_spec] * len(side.ins), out_specs=[any_spec] * len(side.outs),
                          out_shape=side.outs, scratch_shapes=side.sems)(*side.ins)


def _allreduce_small(v):
    rows, cols = v.shape
    r8 = rows // (2 * N_CHIPS)
    assert r8 * 2 * N_CHIPS == rows and r8 % SUBLANES == 0, rows

    def body(v_ref, o_ref, sib_ref, cs_ref, slot_ref, send, recv):
        x, y, c, chips = _place()
        me = 2 * x + y
        sibling = (x, y, 1 - c)

        def eighth(ref, chip, core):
            return ref.at[pl.ds(pl.multiple_of((2 * chip + core) * r8, SUBLANES), r8)]

        def copy(src, dst, k, to):
            return pltpu.make_async_remote_copy(src_ref=src, dst_ref=dst, send_sem=send.at[k], recv_sem=recv.at[k],
                                                device_id=to, device_id_type=MESH)

        d2d = copy(v_ref, sib_ref, 0, sibling)
        d2d.start()
        d2d.wait_recv()
        cs_ref[...] = v_ref[...] + sib_ref[...]
        reduce_out = [copy(eighth(cs_ref, 2 * px + py, c), slot_ref.at[me], 1 + r, (px, py, c)) for r, (px, py) in enumerate(chips)]
        for cp in reduce_out:
            cp.start()
        slot_ref[me] = cs_ref[pl.ds(pl.multiple_of((2 * me + c) * r8, SUBLANES), r8), :]
        for r, (px, py) in enumerate(chips):
            copy(eighth(cs_ref, me, c), slot_ref.at[2 * px + py], 1 + r, (px, py, c)).wait_recv()
        o_ref[pl.ds(pl.multiple_of((2 * me + c) * r8, SUBLANES), r8), :] = (slot_ref[0] + slot_ref[1]) + (slot_ref[2] + slot_ref[3])
        mine = eighth(o_ref, me, c)
        hand_out = [copy(mine, mine, 4, sibling)] + [copy(mine, mine, 5 + r, (px, py, c)) for r, (px, py) in enumerate(chips)]
        for cp in hand_out:
            cp.start()
        passed_on = []
        for r, (px, py) in enumerate(chips):
            theirs = eighth(o_ref, 2 * px + py, c)
            copy(theirs, theirs, 5 + r, (px, py, c)).wait_recv()
            fw = copy(theirs, theirs, 8 + r, sibling)
            fw.start()
            passed_on.append(fw)
        sib_own = eighth(o_ref, me, 1 - c)
        copy(sib_own, sib_own, 4, sibling).wait_recv()
        for r, (px, py) in enumerate(chips):
            got = eighth(o_ref, 2 * px + py, 1 - c)
            copy(got, got, 8 + r, sibling).wait_recv()
        for cp in [d2d] + reduce_out + hand_out + passed_on:
            cp.wait_send()

    vm = pl.BlockSpec(memory_space=pltpu.VMEM)
    return pl.pallas_call(
        body, name="allreduce_small", in_specs=[vm], out_specs=vm, out_shape=jax.ShapeDtypeStruct((rows, cols), F32),
        scratch_shapes=[pltpu.VMEM((rows, cols), F32), pltpu.VMEM((rows, cols), F32), pltpu.VMEM((N_CHIPS, r8, cols), F32),
                        pltpu.SemaphoreType.DMA((11,)), pltpu.SemaphoreType.DMA((11,))],
        compiler_params=_cparams(),
    )(v)


def _pack(tensors):
    pieces = []
    for t in tensors:
        flat = t.reshape(-1)
        pad = (-flat.shape[0]) % (SUBLANES * LANES)
        pieces.append(jnp.pad(flat, (0, pad)).reshape(-1, LANES))
    rows = sum(p.shape[0] for p in pieces)
    pieces.append(jnp.zeros(((-rows) % PACK_ROW_MULTIPLE, LANES), tensors[0].dtype))
    return jnp.concatenate(pieces, axis=0)


def _unpack(buf, like):
    out, off = [], 0
    for t in like:
        size = math.prod(t.shape)
        rows = -(-size // (SUBLANES * LANES)) * SUBLANES
        out.append(buf[off:off + rows].reshape(-1)[:size].reshape(t.shape))
        off += rows
    return out


def _s5_pack_b(bb):
    gc, g, p = bb.shape
    q = S5_GROUPS_PER_BLOCK
    t = bb.reshape(gc, g // q, q, p).transpose(1, 2, 0, 3)
    eye = jnp.eye(q, dtype=bb.dtype)
    return (t[:, :, :, None, :] * eye[None, :, None, :, None]).reshape(g // q, q * gc, q * p)


def _s5_unpack_b(dbp, gc, p):
    nb = dbp.shape[0]
    q = S5_GROUPS_PER_BLOCK
    eye = jnp.eye(q, dtype=dbp.dtype)
    t = (dbp.reshape(nb, q, gc, q, p) * eye[None, :, None, :, None]).sum(axis=3)
    return t.transpose(2, 0, 1, 3).reshape(gc, nb * q, p)


def _s5_pack_c(cc):
    g, gc, p = cc.shape
    q = S5_GROUPS_PER_BLOCK
    t = cc.reshape(g // q, q, gc, p).transpose(0, 1, 3, 2)
    eye = jnp.eye(q, dtype=cc.dtype)
    return (t[:, :, :, None, :] * eye[None, :, None, :, None]).reshape(g // q, q * p, q * gc)


def _s5_unpack_c(dcp, gc, p):
    nb = dcp.shape[0]
    q = S5_GROUPS_PER_BLOCK
    eye = jnp.eye(q, dtype=dcp.dtype)
    t = (dcp.reshape(nb, q, p, q, gc) * eye[None, :, None, :, None]).sum(axis=3)
    return t.transpose(0, 1, 3, 2).reshape(nb * q, gc, p)


def _split2(m):
    return m.arr[:, 0]


def kernel(x, norm_mix_g, norm_ffn_g, norm_final_g, rg_w_in, rg_conv_w, rg_conv_b, rg_w_a, rg_b_a, rg_w_x, rg_b_x, rg_lambda, rg_w_out, s5_w_in, s5_a_re, s5_a_im, s5_log_dt, s5_b_re, s5_b_im, s5_c_re, s5_c_im, s5_d, s5_w_glu, s5_w_out, ffn_w_up, ffn_conv_w, ffn_conv_b, ffn_w_down, loss_target, m_norm_mix_g, m_norm_ffn_g, m_norm_final_g, m_rg_w_in, m_rg_conv_w, m_rg_conv_b, m_rg_w_a, m_rg_b_a, m_rg_w_x, m_rg_b_x, m_rg_lambda, m_rg_w_out, m_s5_w_in, m_s5_a_re, m_s5_a_im, m_s5_log_dt, m_s5_b_re, m_s5_b_im, m_s5_c_re, m_s5_c_im, m_s5_d, m_s5_w_glu, m_s5_w_out, m_ffn_w_up, m_ffn_conv_w, m_ffn_conv_b, m_ffn_w_down, v_norm_mix_g, v_norm_ffn_g, v_norm_final_g, v_rg_w_in, v_rg_conv_w, v_rg_conv_b, v_rg_w_a, v_rg_b_a, v_rg_w_x, v_rg_b_x, v_rg_lambda, v_rg_w_out, v_s5_w_in, v_s5_a_re, v_s5_a_im, v_s5_log_dt, v_s5_b_re, v_s5_b_im, v_s5_c_re, v_s5_c_im, v_s5_d, v_s5_w_glu, v_s5_w_out, v_ffn_w_up, v_ffn_conv_w, v_ffn_conv_b, v_ffn_w_down):
    w = dict(zip(PARAM_NAMES, (norm_mix_g, norm_ffn_g, norm_final_g, rg_w_in, rg_conv_w, rg_conv_b, rg_w_a, rg_b_a, rg_w_x, rg_b_x,
                               rg_lambda, rg_w_out, s5_w_in, s5_a_re, s5_a_im, s5_log_dt, s5_b_re, s5_b_im, s5_c_re, s5_c_im, s5_d,
                               s5_w_glu, s5_w_out, ffn_w_up, ffn_conv_w, ffn_conv_b, ffn_w_down)))
    mom = dict(zip(PARAM_NAMES, (m_norm_mix_g, m_norm_ffn_g, m_norm_final_g, m_rg_w_in, m_rg_conv_w, m_rg_conv_b, m_rg_w_a, m_rg_b_a,
                                 m_rg_w_x, m_rg_b_x, m_rg_lambda, m_rg_w_out, m_s5_w_in, m_s5_a_re, m_s5_a_im, m_s5_log_dt, m_s5_b_re,
                                 m_s5_b_im, m_s5_c_re, m_s5_c_im, m_s5_d, m_s5_w_glu, m_s5_w_out, m_ffn_w_up, m_ffn_conv_w,
                                 m_ffn_conv_b, m_ffn_w_down)))
    vel = dict(zip(PARAM_NAMES, (v_norm_mix_g, v_norm_ffn_g, v_norm_final_g, v_rg_w_in, v_rg_conv_w, v_rg_conv_b, v_rg_w_a, v_rg_b_a,
                                 v_rg_w_x, v_rg_b_x, v_rg_lambda, v_rg_w_out, v_s5_w_in, v_s5_a_re, v_s5_a_im, v_s5_log_dt, v_s5_b_re,
                                 v_s5_b_im, v_s5_c_re, v_s5_c_im, v_s5_d, v_s5_w_glu, v_s5_w_out, v_ffn_w_up, v_ffn_conv_w,
                                 v_ffn_conv_b, v_ffn_w_down)))
    _, s, d = x.shape
    depth = norm_mix_g.shape[0]
    n_grp, n_state = s5_a_re.shape[1], s5_a_re.shape[2]
    gc = s5_b_re.shape[3]
    d_ff = ffn_w_down.shape[1] * N_CHIPS
    s5_ts = min(256, s)
    s5_perm = _segment_perm(s5_ts)

    wb = {n: (w[n].astype(BF16) if n in BIG else w[n]) for n in SHARDED}
    gath = {}

    def mixer_keys(i):
        return [(n, i // 2) for n in MIXER_SHARDED[i % 2]] if i < depth else []

    def gather_side(keys):
        return _gather_side([wb[n][l] for n, l in keys])

    def put(keys, arrs):
        for k, a in zip(keys, arrs):
            gath[k] = a

    def wcol(n, l):
        return Mat(gath[(n, l)][:, None], 0, 'c')

    def wrow(n, l):
        g = gath[(n, l)]
        return Mat(g.reshape(1, 1, N_CHIPS * g.shape[1], g.shape[2]), 0, 'c')

    def rg_cw(l):
        return gath[('rg_conv_w', l)].transpose(1, 0, 2).reshape(RG_CONV_W, d)

    def s5_dv(l):
        return gath[('s5_d', l)].reshape(1, d)

    def f_cw(l):
        return gath[('ffn_conv_w', l)].transpose(1, 0, 2).reshape(FFN_CONV_W, 2, d_ff).transpose(1, 0, 2)

    tm = min(1024, s)
    tkw = min(2048, s)
    d_up = 2 * d_ff // N_CHIPS
    f_cb = ffn_conv_b.reshape(depth, 2, 1, d_ff)

    h = x.reshape(s, d)
    saved = []
    for i in range(depth):
        j = i // 2
        sv = {'h_in': h}
        hn, got = _rms_fwd(h, norm_mix_g[i:i + 1], side=gather_side(mixer_keys(0)) if i == 0 else None)
        if i == 0:
            put(mixer_keys(0), got)
        sv['hn'] = hn
        up_keys = [('ffn_w_up', i), ('ffn_conv_w', i)]
        if i % 2 == 0:
            xg = _mm("rg_in", 'nn', act(hn), wcol('rg_w_in', j), out_parts=2, tm=tm, tn=512, tk=d)
            xg2 = _split2(xg)
            wa, wx = rg_w_a[j].astype(BF16), rg_w_x[j].astype(BF16)
            ba, bx = rg_b_a[j].reshape(1, d), rg_b_x[j].reshape(1, d)
            (xr, hs, y), got = _rg_fwd(xg2, rg_cw(j), rg_conv_b[j:j + 1], wa, ba, wx, bx, rg_lambda[j:j + 1],
                                       side=gather_side(up_keys))
            put(up_keys, got)
            sv.update(xg2=xg2, xr=xr, hs=hs, y=y, wa=wa, wx=wx, ba=ba, bx=bx)
            h = _mm("rg_out", 'nn', act(y), wrow('rg_w_out', j), res=act(h), tm=tm, tn=d, tk=d).arr[0, 0]
        else:
            u = _mm("s5_in", 'nn', act(hn), wrow('s5_w_in', j), tm=tm, tn=d, tk=d).arr[0, 0]
            bt_re, bt_im = s5_b_re[j].transpose(2, 0, 1), s5_b_im[j].transpose(2, 0, 1)
            ldt = s5_log_dt[j].reshape(n_grp, 1)
            tab_r, tab_i, rtab_r, rtab_i, bbr, bbi = _s5_tables3(s5_a_re[j], s5_a_im[j], ldt, bt_re, bt_im, seg=s5_ts // SUBLANES)
            nn_ = n_grp * n_state
            tab_r, tab_i, rtab_r, rtab_i = (t.reshape(5, SUBLANES, nn_) for t in (tab_r, tab_i, rtab_r, rtab_i))
            prm = dict(bp_r=_s5_pack_b(bbr).astype(BF16), bp_i=_s5_pack_b(bbi).astype(BF16),
                       cp_r=_s5_pack_c(s5_c_re[j]).astype(BF16), cp_i=_s5_pack_c(s5_c_im[j]).astype(BF16), dvec=s5_dv(j))
            (hr, hi, ypre, gy), got = _s5_fwd3(u, s5_perm, s5_perm.T, tab_r, tab_i, ts=s5_ts, side=gather_side(up_keys), **prm)
            sv.update(rtab_r=rtab_r, rtab_i=rtab_i)
            put(up_keys, got)
            gl2, o = _glu_mm(gy, gath[('s5_w_glu', j)])
            sv.update(u=u, prm=prm, hr=hr, hi=hi, ypre=ypre, gy=gy, gl2=gl2, o=o, bt_re=bt_re, bt_im=bt_im, ldt=ldt)
            h = _mm("s5_out", 'nn', act(o), wrow('s5_w_out', j), res=act(h), tm=tm, tn=d, tk=d).arr[0, 0]
        sv['h_mid'] = h
        hn2, _ = _rms_fwd(h, norm_ffn_g[i:i + 1])
        next_keys = [('ffn_w_down', i)] + mixer_keys(i + 1)
        (up2, c2, a_ffn), got = _ffn_up_act(hn2, gath[('ffn_w_up', i)], f_cw(i), f_cb[i], side=gather_side(next_keys))
        put(next_keys, got)
        sv.update(hn2=hn2, up2=up2, c2=c2, act=a_ffn)
        h = _mm("ffn_down", 'nn', act(a_ffn), wrow('ffn_w_down', i), res=act(h), tm=tm, tn=d, tk=d_ff // 2).arr[0, 0]
        saved.append(sv)

    loss_row, dh, dg_final = _loss_and_grad(h, norm_final_g.reshape(1, d), loss_target.reshape(s, d))
    loss = lax.psum(loss_row[0, 0], ("x", "y", "c"))

    gl_ = {n: [None] * w[n].shape[0] for n in PARAM_NAMES if n != 'norm_final_g'}
    recvd = {}

    def scatter_side(keys):
        return _scatter_side([gl_[n][l].reshape((N_CHIPS,) + w[n].shape[1:]) for n, l in keys])

    def record(keys, arrs):
        for k, a in zip(keys, arrs):
            recvd[k] = a

    pending = None
    for i in reversed(range(depth)):
        j = i // 2
        sv = saved[i]
        gl_['ffn_w_down'][i] = _mm("ffn_down_dw", 'tn', act(sv['act']), act(dh), out_dtype=BF16, tm=d_ff // N_CHIPS, tn=d, tk=tkw).arr
        (dup2, dcw2, dcb2), got = _ffn_bwd_fused(dh, gath[('ffn_w_down', i)].reshape(d_ff, d), sv['up2'], sv['c2'], f_cw(i),
                                                 side=scatter_side(pending) if pending else None)
        if pending:
            record(pending, got)
        gl_['ffn_conv_w'][i] = dcw2.transpose(1, 0, 2).reshape(FFN_CONV_W, 2 * d_ff)
        gl_['ffn_conv_b'][i] = dcb2.reshape(2 * d_ff)
        dup = Mat(dup2[:, None], 0, 'c')
        gl_['ffn_w_up'][i] = _mm("ffn_up_dw", 'tn', act(sv['hn2']), dup, out_parts=N_CHIPS, out_dtype=BF16, tm=d, tn=d_up, tk=tkw).arr
        (dh, dg), _ = _mm_rms_bwd("ffn_up_dx", dup, wcol('ffn_w_up', i), sv['h_mid'], norm_ffn_g[i:i + 1], dh, tm=tm, tk=d_up)
        gl_['norm_ffn_g'][i] = dg[0]
        ffn_keys = [('ffn_w_up', i), ('ffn_w_down', i)]
        if i % 2 == 0:
            dy = _mm("rg_out_dx", 'nt', act(dh), wrow('rg_w_out', j), tm=tm, tn=d, tk=d).arr[0, 0]
            gl_['rg_w_out'][j] = _mm("rg_out_dw", 'tn', act(sv['y']), act(dh), out_dtype=BF16, tm=d, tn=d, tk=tkw).arr
            (dxg2, dcw, dcb, dwa, dba, dwx, dbx, dlam), got = _rg_bwd(
                dy, sv['xg2'], sv['xr'], sv['hs'], rg_cw(j), sv['wa'], sv['ba'], sv['wx'], sv['bx'], rg_lambda[j:j + 1],
                side=scatter_side(ffn_keys))
            record(ffn_keys, got)
            gl_['rg_conv_w'][j] = dcw
            gl_['rg_conv_b'][j] = dcb[0]
            gl_['rg_w_a'][j], gl_['rg_w_x'][j] = dwa, dwx
            gl_['rg_b_a'][j], gl_['rg_b_x'][j] = dba.reshape(rg_b_a.shape[1:]), dbx.reshape(rg_b_x.shape[1:])
            gl_['rg_lambda'][j] = dlam[0]
            dxg = Mat(dxg2[:, None], 0, 'c')
            gl_['rg_w_in'][j] = _mm("rg_in_dw", 'tn', act(sv['hn']), dxg, out_parts=N_CHIPS, out_dtype=BF16, tm=d, tn=512, tk=tkw).arr
            mix_dx = ("rg_in_dx", dxg, wcol('rg_w_in', j), 512)
            pending = [('rg_w_in', j), ('rg_w_out', j)]
        else:
            gl_['s5_w_out'][j] = _mm("s5_out_dw", 'tn', act(sv['o']), act(dh), out_dtype=BF16, tm=d, tn=d, tk=tkw).arr
            dgl2 = _glu_bwd_mm(dh, gath[('s5_w_out', j)].reshape(d, d), sv['gl2'])
            dgl = Mat(dgl2[:, None], 0, 'c')
            gl_['s5_w_glu'][j] = _mm("s5_glu_dw", 'tn', act(sv['gy']), dgl, out_parts=N_CHIPS, out_dtype=BF16, tm=d, tn=512, tk=tkw).arr
            dgy = _mm("s5_glu_dx", 'nt', dgl, wcol('s5_w_glu', j), tm=tm, tn=d, tk=512).arr[0, 0]
            (du, dar, dai, dbpr, dbpi, dcpr, dcpi, dd), got = _s5_bwd3(
                dgy, sv['ypre'], sv['u'], sv['hr'], sv['hi'], s5_perm, s5_perm.T, sv['rtab_r'], sv['rtab_i'], ts=s5_ts,
                side=scatter_side(ffn_keys), **sv['prm'])
            record(ffn_keys, got)
            gl_['s5_d'][j] = dd[0]
            gl_['s5_c_re'][j] = _s5_unpack_c(dcpr, gc, n_state)
            gl_['s5_c_im'][j] = -_s5_unpack_c(dcpi, gc, n_state)
            d_are, d_aim, d_ldt, d_btr, d_bti = _s5_params_bwd(
                s5_a_re[j], s5_a_im[j], sv['ldt'], sv['bt_re'], sv['bt_im'], dar.reshape(n_grp, n_state), dai.reshape(n_grp, n_state),
                _s5_unpack_b(dbpr, gc, n_state), _s5_unpack_b(dbpi, gc, n_state))
            gl_['s5_a_re'][j], gl_['s5_a_im'][j], gl_['s5_log_dt'][j] = d_are, d_aim, d_ldt[:, 0]
            gl_['s5_b_re'][j], gl_['s5_b_im'][j] = d_btr.transpose(1, 2, 0), d_bti.transpose(1, 2, 0)
            dum = act(du)
            gl_['s5_w_in'][j] = _mm("s5_in_dw", 'tn', act(sv['hn']), dum, out_dtype=BF16, tm=d, tn=d, tk=tkw).arr
            mix_dx = ("s5_in_dx", dum, wrow('s5_w_in', j), d)
            pending = [('s5_w_in', j), ('s5_w_glu', j), ('s5_w_out', j)]
        (dh, dg), got = _mm_rms_bwd(mix_dx[0], mix_dx[1], mix_dx[2], sv['h_in'], norm_mix_g[i:i + 1], dh, tm=tm, tk=mix_dx[3],
                                    side=scatter_side(pending) if i == 0 else None)
        if i == 0:
            record(pending, got)
        gl_['norm_mix_g'][i] = dg[0]
    grad_x = dh.reshape(x.shape)

    order = sorted(BIG, key=lambda n: -math.prod(w[n].shape))
    chip_sums, theirs, prev = {}, {}, None
    for n in order:
        cols = w[n].shape[-1]
        cs, got = _sum_parts([recvd[(n, l)].reshape(N_CHIPS, -1, cols) for l in range(w[n].shape[0])],
                             side=_sibling_side([chip_sums[prev]]) if prev else None)
        chip_sums[n] = cs.reshape(-1, cols)
        if prev:
            theirs[prev] = got[0]
        prev = n
    theirs[prev] = _run_side("swap_last", _sibling_side([chip_sums[prev]]))[0]
    results = {}
    for n in BIG:
        cols = w[n].shape[-1]
        (delta, new_m, new_v, grad), _ = _adamw(w[n].reshape(-1, cols), [chip_sums[n], theirs[n]], mom[n].reshape(-1, cols),
                                                vel[n].reshape(-1, cols))
        results[n] = [o.reshape(w[n].shape) for o in (grad, delta, new_m, new_v)]

    small = REPLICATED + SMALL_SHARDED
    local = [dg_final.reshape(d) if n == 'norm_final_g' else jnp.stack(gl_[n]) for n in small]
    summed = _unpack(_allreduce_small(_pack(local)), local)
    me = 2 * lax.axis_index("x") + lax.axis_index("y")
    for n, g in zip(small, summed):
        if n in SMALL_SHARDED:
            g = lax.dynamic_slice_in_dim(g, me * w[n].shape[-1], w[n].shape[-1], axis=g.ndim - 1)
        view = (-1, w[n].shape[-1])
        (delta, new_m, new_v), _ = _adamw(w[n].reshape(view), [g.reshape(view)], mom[n].reshape(view), vel[n].reshape(view))
        results[n] = [g] + [o.reshape(w[n].shape) for o in (delta, new_m, new_v)]

    return (loss, grad_x, *[results[n][0] for n in PARAM_NAMES], *[results[n][1] for n in PARAM_NAMES],
            *[results[n][2] for n in PARAM_NAMES], *[results[n][3] for n in PARAM_NAMES])
```

```python
import math

import jax
import jax.numpy as jnp
from jax import lax
from jax.experimental import pallas as pl
from jax.experimental.pallas import tpu as pltpu

F32 = jnp.float32
BF16 = jnp.bfloat16
MESH = pl.DeviceIdType.MESH

NORM_EPS = 1e-6
RG_HEADS = 8
RG_CONV_W = 4
RG_C = 8.0
S5_GC = 16
S5_P = 64
S5_GROUPS_PER_BLOCK = 8
FFN_CONV_W = 3
N_CHIPS = 4
ADAM_LR, ADAM_B1, ADAM_B2, ADAM_EPS, ADAM_WD, ADAM_STEP = 0.001, 0.9, 0.999, 1e-08, 0.01, 10
VMEM_LIMIT_BYTES = 56 * 1024 * 1024
SUBLANES = 8
LANES = 128

PARAM_NAMES = ['norm_mix_g', 'norm_ffn_g', 'norm_final_g', 'rg_w_in', 'rg_conv_w', 'rg_conv_b', 'rg_w_a', 'rg_b_a', 'rg_w_x',
               'rg_b_x', 'rg_lambda', 'rg_w_out', 's5_w_in', 's5_a_re', 's5_a_im', 's5_log_dt', 's5_b_re', 's5_b_im', 's5_c_re',
               's5_c_im', 's5_d', 's5_w_glu', 's5_w_out', 'ffn_w_up', 'ffn_conv_w', 'ffn_conv_b', 'ffn_w_down']
SHARDED = ['rg_w_in', 'rg_conv_w', 'rg_w_out', 's5_w_in', 's5_d', 's5_w_glu', 's5_w_out', 'ffn_w_up', 'ffn_conv_w', 'ffn_w_down']
BIG = ['rg_w_in', 'rg_w_out', 's5_w_in', 's5_w_glu', 's5_w_out', 'ffn_w_up', 'ffn_w_down']
SMALL_SHARDED = ['rg_conv_w', 's5_d', 'ffn_conv_w']
MIXER_SHARDED = [['rg_w_in', 'rg_conv_w', 'rg_w_out'], ['s5_w_in', 's5_d', 's5_w_glu', 's5_w_out']]
REPLICATED = [n for n in PARAM_NAMES if n not in SHARDED]


def _cparams():
    return pltpu.CompilerParams(vmem_limit_bytes=VMEM_LIMIT_BYTES)


_GELU_C = math.sqrt(2.0 / math.pi)
_GELU_K = 0.044715


def _gelu(x):
    return 0.5 * x * (1.0 + jnp.tanh(_GELU_C * (x + _GELU_K * x * x * x)))


def _gelu_and_grad(x):
    t = jnp.tanh(_GELU_C * (x + _GELU_K * x * x * x))
    g = 0.5 * x * (1.0 + t)
    dg = 0.5 * (1.0 + t) + 0.5 * x * (1.0 - t * t) * (_GELU_C * (1.0 + 3.0 * _GELU_K * x * x))
    return g, dg


def _sigmoid(x):
    return jax.nn.sigmoid(x)


def _neg_expm1(x):
    series = -(x * (1.0 + x * (0.5 + x * (1.0 / 6 + x * (1.0 / 24 + x * (1.0 / 120 + x * (1.0 / 720)))))))
    return jnp.where(x > -0.25, series, 1.0 - jnp.exp(x))


def _softplus(z):
    return jnp.maximum(z, 0.0) + jnp.log1p(jnp.exp(-jnp.abs(z)))


def _rows(shape):
    return lax.broadcasted_iota(jnp.int32, shape, 0)


def _shift_down(x, halo, k):
    ext = jnp.concatenate([halo, x], axis=0)
    return pltpu.roll(ext, k, 0)[SUBLANES:]


def _shift_up(x, halo, k):
    ext = jnp.concatenate([x, halo], axis=0)
    n = ext.shape[0]
    return pltpu.roll(ext, n - k, 0)[:x.shape[0]]


RG_LANE_CHUNK = 512


def _real_slab_scan(a_ref, b_ref, out_ref, carry_ref, reverse):
    t, c = a_ref.shape
    nsl = t // SUBLANES
    lc = min(RG_LANE_CHUNK, c)
    row8 = _rows((SUBLANES, lc))
    for q in range(c // lc):
        sl = slice(q * lc, (q + 1) * lc)

        def slab(jj, carry, sl=sl):
            j = nsl - 1 - jj if reverse else jj
            r0 = pl.multiple_of(j * SUBLANES, SUBLANES)
            a, b = a_ref[pl.ds(r0, SUBLANES), sl], b_ref[pl.ds(r0, SUBLANES), sl]
            for k in range(3):
                sh = 1 << k
                keep = row8 < SUBLANES - sh if reverse else row8 >= sh
                amount = SUBLANES - sh if reverse else sh
                b = a * jnp.where(keep, pltpu.roll(b, amount, 0), 0.0) + b
                a = a * jnp.where(keep, pltpu.roll(a, amount, 0), 1.0)
            x = b + a * jnp.broadcast_to(carry, b.shape)
            out_ref[pl.ds(r0, SUBLANES), sl] = x
            return x[:1, :] if reverse else x[SUBLANES - 1:, :]

        carry_ref[:, sl] = lax.fori_loop(0, nsl, slab, carry_ref[:, sl], unroll=2)


class Mat:
    def __init__(self, arr, l=0, split='c'):
        assert arr.ndim == 4
        self.arr, self.l, self.split = arr, l, split
        p, _, r, c = arr.shape
        self.shape = (r, c * p) if split == 'c' else (r * p, c)

    def spec(self, tr, tc, rc):
        p, _, r, c = self.arr.shape
        l = self.l
        assert r % tr == 0 and c % tc == 0, (self.arr.shape, tr, tc)
        if self.split == 'c':
            per = c // tc
            return pl.BlockSpec((None, None, tr, tc), lambda i, j, k: (rc(i, j, k)[1] // per, l, rc(i, j, k)[0], rc(i, j, k)[1] % per))
        per = r // tr
        return pl.BlockSpec((None, None, tr, tc), lambda i, j, k: (rc(i, j, k)[0] // per, l, rc(i, j, k)[0] % per, rc(i, j, k)[1]))


def act(x, parts=1):
    s, c = x.shape
    return Mat(x.reshape(s, parts, c // parts).transpose(1, 0, 2)[:, None] if parts > 1 else x[None, None])


def _mm(name, mode, a, b, *, out_parts=1, out_split='c', out_dtype=F32, res=None, norm_g=None, tm=512, tn=512, tk=512):
    if mode == 'nn':
        (m, kk), (kb, n) = a.shape, b.shape
    elif mode == 'nt':
        (m, kk), (n, kb) = a.shape, b.shape
    else:
        (kk, m), (kb, n) = a.shape, b.shape
    assert kk == kb, (name, a.shape, b.shape)
    tm, tn, tk = min(tm, m), min(tn, n), min(tk, kk)
    assert m % tm == 0 and n % tn == 0 and kk % tk == 0, (name, m, n, kk, tm, tn, tk)
    nk = kk // tk
    if mode == 'nn':
        a_spec = a.spec(tm, tk, lambda i, j, k: (i, k))
        b_spec = b.spec(tk, tn, lambda i, j, k: (k, j))
        dims = (((1,), (0,)), ((), ()))
    elif mode == 'nt':
        a_spec = a.spec(tm, tk, lambda i, j, k: (i, k))
        b_spec = b.spec(tn, tk, lambda i, j, k: (j, k))
        dims = (((1,), (1,)), ((), ()))
    else:
        a_spec = a.spec(tk, tm, lambda i, j, k: (k, i))
        b_spec = b.spec(tk, tn, lambda i, j, k: (k, j))
        dims = (((0,), (0,)), ((), ()))
    if out_split == 'c':
        out_arr = jax.ShapeDtypeStruct((out_parts, 1, m, n // out_parts), out_dtype)
    else:
        out_arr = jax.ShapeDtypeStruct((out_parts, 1, m // out_parts, n), out_dtype)
    out_mat = Mat(out_arr, 0, out_split)
    o_spec = out_mat.spec(tm, tn, lambda i, j, k: (i, j))
    has_res = res is not None
    has_norm = norm_g is not None
    assert not has_norm or tn == n, (name, tn, n)

    def body(*refs):
        a_ref, b_ref = refs[:2]
        extra = list(refs[2:2 + has_res + has_norm])
        r_ref = extra.pop(0) if has_res else None
        g_ref = extra.pop(0) if has_norm else None
        o_ref = refs[2 + has_res + has_norm]
        prod = lax.dot_general(a_ref[...].astype(BF16), b_ref[...].astype(BF16), dims, preferred_element_type=F32)

        def finish(acc):
            if has_res:
                acc = acc + r_ref[...]
            o_ref[...] = acc.astype(out_dtype)
            if has_norm:
                var = jnp.mean(acc * acc, axis=-1, keepdims=True)
                refs[3 + has_res + has_norm][...] = (acc * lax.rsqrt(var + NORM_EPS) * g_ref[...]).astype(BF16)

        if nk == 1:
            finish(prod)
        else:
            acc_ref = refs[-1]
            k = pl.program_id(2)

            @pl.when(k == 0)
            def _():
                acc_ref[...] = prod

            @pl.when(k > 0)
            def _():
                acc_ref[...] += prod

            @pl.when(k == nk - 1)
            def _():
                finish(acc_ref[...])

    in_specs = [a_spec, b_spec]
    args = [a.arr, b.arr]
    if has_res:
        in_specs.append(res.spec(tm, tn, lambda i, j, k: (i, j)))
        args.append(res.arr)
    out_specs, out_shape = [o_spec], [out_arr]
    if has_norm:
        in_specs.append(pl.BlockSpec((1, n), lambda i, j, k: (0, 0)))
        args.append(norm_g)
        out_specs.append(pl.BlockSpec((tm, n), lambda i, j, k: (i, 0)))
        out_shape.append(jax.ShapeDtypeStruct((m, n), BF16))
    outs = pl.pallas_call(
        body, name=name, grid=(m // tm, n // tn, nk), in_specs=in_specs, out_specs=out_specs, out_shape=out_shape,
        scratch_shapes=[pltpu.VMEM((tm, tn), F32)] if nk > 1 else [], compiler_params=_cparams(),
    )(*args)
    return (Mat(outs[0], 0, out_split), outs[1]) if has_norm else Mat(outs[0], 0, out_split)


def _rms_fwd(h, g, ts=512, side=None):
    s, d = h.shape
    ts = min(ts, s)
    nt = s // ts

    def body(h_ref, g_ref, o_ref):
        x = h_ref[...]
        var = jnp.mean(x * x, axis=-1, keepdims=True)
        o_ref[...] = (x * lax.rsqrt(var + NORM_EPS) * g_ref[...]).astype(BF16)

    outs, got = _call_with_side(
        body, side, lambda: pl.program_id(0) == 0, lambda: pl.program_id(0) == nt - 1,
        name="rms_fwd", grid=(nt,),
        in_specs=[pl.BlockSpec((ts, d), lambda i: (i, 0)), pl.BlockSpec((1, d), lambda i: (0, 0))],
        out_specs=[pl.BlockSpec((ts, d), lambda i: (i, 0))], out_shape=[jax.ShapeDtypeStruct((s, d), BF16)],
        scratch_shapes=[], args=(h, g))
    return outs[0], got


def _loss_and_grad(h, g, tgt, ts=512):
    s, d = h.shape
    ts = min(ts, s)

    def body(h_ref, g_ref, t_ref, loss_ref, dh_ref, dg_ref):
        i = pl.program_id(0)
        x = h_ref[...]
        gv = g_ref[...]
        rstd = lax.rsqrt(jnp.mean(x * x, axis=-1, keepdims=True) + NORM_EPS)
        xhat = x * rstd
        err = xhat * gv - t_ref[...]
        dy = err * (1.0 / d)
        dxh = dy * gv
        dh_ref[...] = rstd * (dxh - xhat * jnp.mean(dxh * xhat, axis=-1, keepdims=True))
        part = jnp.sum(dy * xhat, axis=0, keepdims=True)
        lpart = jnp.broadcast_to(jnp.sum(jnp.sum(err * err, axis=0, keepdims=True), axis=1, keepdims=True) * (0.5 / d), (1, LANES))

        @pl.when(i == 0)
        def _():
            dg_ref[...] = part
            loss_ref[...] = lpart

        @pl.when(i > 0)
        def _():
            dg_ref[...] += part
            loss_ref[...] += lpart

    row = pl.BlockSpec((ts, d), lambda i: (i, 0))
    vec = pl.BlockSpec((1, d), lambda i: (0, 0))
    return pl.pallas_call(
        body, name="loss_and_grad", grid=(s // ts,), in_specs=[row, vec, row],
        out_specs=[pl.BlockSpec((1, LANES), lambda i: (0, 0)), row, vec],
        out_shape=[jax.ShapeDtypeStruct((1, LANES), F32), jax.ShapeDtypeStruct((s, d), F32), jax.ShapeDtypeStruct((1, d), F32)],
        compiler_params=_cparams(),
    )(h, g, tgt)


def _mm_rms_bwd(name, a, b, h, g, dh_in, *, tm, tk, side=None):
    (m, kk), (n, kb) = a.shape, b.shape
    assert kk == kb and h.shape == (m, n), (name, a.shape, b.shape, h.shape)
    tm, tk = min(tm, m), min(tk, kk)
    nk = kk // tk
    dims = (((1,), (1,)), ((), ()))

    def body(a_ref, b_ref, h_ref, g_ref, dhin_ref, dh_ref, dg_ref, *acc):
        i, k = pl.program_id(0), pl.program_id(2)
        prod = lax.dot_general(a_ref[...].astype(BF16), b_ref[...].astype(BF16), dims, preferred_element_type=F32)

        def finish(dhn):
            x = h_ref[...]
            rstd = lax.rsqrt(jnp.mean(x * x, axis=-1, keepdims=True) + NORM_EPS)
            xhat = x * rstd
            dxh = dhn * g_ref[...]
            dh_ref[...] = dhin_ref[...] + rstd * (dxh - xhat * jnp.mean(dxh * xhat, axis=-1, keepdims=True))
            part = jnp.sum(dhn * xhat, axis=0, keepdims=True)

            @pl.when(i == 0)
            def _():
                dg_ref[...] = part

            @pl.when(i > 0)
            def _():
                dg_ref[...] += part

        if nk == 1:
            finish(prod)
        else:
            acc_ref = acc[0]

            @pl.when(k == 0)
            def _():
                acc_ref[...] = prod

            @pl.when(k > 0)
            def _():
                acc_ref[...] += prod

            @pl.when(k == nk - 1)
            def _():
                finish(acc_ref[...])

    row = pl.BlockSpec((tm, n), lambda i, j, k: (i, 0))
    vec = pl.BlockSpec((1, n), lambda i, j, k: (0, 0))
    ni = m // tm
    return _call_with_side(
        body, side, lambda: (pl.program_id(0) == 0) & (pl.program_id(2) == 0),
        lambda: (pl.program_id(0) == ni - 1) & (pl.program_id(2) == nk - 1),
        name=name, grid=(ni, 1, nk),
        in_specs=[a.spec(tm, tk, lambda i, j, k: (i, k)), b.spec(n, tk, lambda i, j, k: (0, k)), row, vec, row],
        out_specs=[row, vec], out_shape=[jax.ShapeDtypeStruct((m, n), F32), jax.ShapeDtypeStruct((1, n), F32)],
        scratch_shapes=[pltpu.VMEM((tm, n), F32)] if nk > 1 else [], args=(a.arr, b.arr, h, g, dh_in))


def _ffn_up_act(hn2, w_up4, conv_w2, conv_b2, ts=1024, tn=512, sub=1024, side=None):
    s, d = hn2.shape
    p, _, wc = w_up4.shape
    f = p * wc // 2
    ts, tn = min(ts, s), min(tn, wc)
    sub = min(sub, ts)
    per = wc // tn
    kw = FFN_CONV_W
    g0, g1 = f // tn, s // ts

    def body(hn_ref, w1_ref, w2_ref, cw_ref, cb_ref, up_ref, c_ref, act_ref, carry_ref):
        @pl.when(pl.program_id(1) == 0)
        def _():
            carry_ref[...] = jnp.zeros_like(carry_ref)

        for q in range(ts // sub):
            rows = slice(q * sub, (q + 1) * sub)
            hn = hn_ref[rows, :]
            cs = []
            for h, w_ref in enumerate((w1_ref, w2_ref)):
                x = jnp.dot(hn, w_ref[...], preferred_element_type=F32)
                up_ref[h, rows, :] = x
                halo = carry_ref[h]
                c = cb_ref[h] + cw_ref[h, kw - 1:kw, :] * x
                for sft in range(1, kw):
                    c = c + cw_ref[h, kw - 1 - sft:kw - sft, :] * _shift_down(x, halo, sft)
                carry_ref[h] = x[sub - SUBLANES:, :]
                c_ref[h, rows, :] = c
                cs.append(c)
            act_ref[rows, :] = (_gelu(cs[0]) * cs[1]).astype(BF16)

    outs, side_outs = _call_with_side(
        body, side, lambda: (pl.program_id(0) == 0) & (pl.program_id(1) == 0),
        lambda: (pl.program_id(0) == g0 - 1) & (pl.program_id(1) == g1 - 1),
        name="ffn_up_act", grid=(g0, g1),
        in_specs=[pl.BlockSpec((ts, d), lambda j, i: (i, 0)),
                  pl.BlockSpec((None, d, tn), lambda j, i: (j // per, 0, j % per)),
                  pl.BlockSpec((None, d, tn), lambda j, i: (p // 2 + j // per, 0, j % per)),
                  pl.BlockSpec((2, kw, tn), lambda j, i: (0, 0, j)),
                  pl.BlockSpec((2, 1, tn), lambda j, i: (0, 0, j))],
        out_specs=[pl.BlockSpec((2, ts, tn), lambda j, i: (0, i, j)), pl.BlockSpec((2, ts, tn), lambda j, i: (0, i, j)),
                   pl.BlockSpec((ts, tn), lambda j, i: (i, j))],
        out_shape=[jax.ShapeDtypeStruct((2, s, f), F32), jax.ShapeDtypeStruct((2, s, f), F32), jax.ShapeDtypeStruct((s, f), BF16)],
        scratch_shapes=[pltpu.VMEM((2, SUBLANES, tn), F32)], args=(hn2, w_up4, w_up4, conv_w2, conv_b2))
    return outs, side_outs


def _ffn_bwd_fused(dh, w_down, up2, c2, conv_w2, ts=1024, tn=512, side=None):
    s, d = dh.shape
    _, _, f = up2.shape
    ts, tn = min(ts, s), min(tn, f)
    kw = FFN_CONV_W
    nt = s // ts
    hb = ts // SUBLANES
    g0 = f // tn
    nt_dims = (((1,), (1,)), ((), ()))

    def body(dh_ref, wd_ref, up_ref, c_ref, w_ref, dup_ref, dw_ref, db_ref, carry_ref):
        i = pl.program_id(1)
        first_step = i == 0

        @pl.when(first_step)
        def _():
            carry_ref[...] = jnp.zeros_like(carry_ref)

        da = lax.dot_general(dh_ref[...].astype(BF16), wd_ref[...], nt_dims, preferred_element_type=F32)
        g1, dg1 = _gelu_and_grad(c_ref[0])
        dcs = [da * c_ref[1] * dg1, da * g1]
        for h in range(2):
            dc = dcs[h]
            after = carry_ref[h]
            ups = [dc] + [_shift_up(dc, after, sft) for sft in range(1, kw)]
            dup = w_ref[h, kw - 1:kw, :] * dc
            for sft in range(1, kw):
                dup = dup + w_ref[h, kw - 1 - sft:kw - sft, :] * ups[sft]
            carry_ref[h] = dc[:SUBLANES]
            dup_ref[h] = dup.astype(BF16)
            dbp = jnp.sum(dc, axis=0, keepdims=True)
            x = up_ref[h]
            dwp = [jnp.sum(ups[kw - 1 - k] * x, axis=0, keepdims=True) for k in range(kw)]

            @pl.when(first_step)
            def _():
                db_ref[h] = dbp
                for k in range(kw):
                    dw_ref[h, k:k + 1, :] = dwp[k]

            @pl.when(i > 0)
            def _():
                db_ref[h] += dbp
                for k in range(kw):
                    dw_ref[h, k:k + 1, :] += dwp[k]

    rev = lambda i: nt - 1 - i
    return _call_with_side(
        body, side, lambda: (pl.program_id(0) == 0) & (pl.program_id(1) == 0),
        lambda: (pl.program_id(0) == g0 - 1) & (pl.program_id(1) == nt - 1),
        name="ffn_bwd", grid=(g0, nt),
        in_specs=[pl.BlockSpec((ts, d), lambda j, i: (rev(i), 0)),
                  pl.BlockSpec((tn, d), lambda j, i: (j, 0)),
                  pl.BlockSpec((2, ts, tn), lambda j, i: (0, rev(i), j)),
                  pl.BlockSpec((2, ts, tn), lambda j, i: (0, rev(i), j)),
                  pl.BlockSpec((2, kw, tn), lambda j, i: (0, 0, j))],
        out_specs=[pl.BlockSpec((2, ts, tn), lambda j, i: (0, rev(i), j)),
                   pl.BlockSpec((2, kw, tn), lambda j, i: (0, 0, j)),
                   pl.BlockSpec((2, 1, tn), lambda j, i: (0, 0, j))],
        out_shape=[jax.ShapeDtypeStruct((2, s, f), BF16), jax.ShapeDtypeStruct((2, kw, f), F32),
                   jax.ShapeDtypeStruct((2, 1, f), F32)],
        scratch_shapes=[pltpu.VMEM((2, SUBLANES, tn), F32)], args=(dh, w_down, up2, c2, conv_w2))


def _rg_gates(xr, wa_ref, ba_ref, wx_ref, bx_ref, lam_ref):
    bw = wa_ref.shape[-1]
    xb = xr.astype(BF16)
    za = jnp.concatenate([jnp.dot(xb[:, h * bw:(h + 1) * bw], wa_ref[h], preferred_element_type=F32)
                          for h in range(RG_HEADS)], axis=1) + ba_ref[...]
    zx = jnp.concatenate([jnp.dot(xb[:, h * bw:(h + 1) * bw], wx_ref[h], preferred_element_type=F32)
                          for h in range(RG_HEADS)], axis=1) + bx_ref[...]
    r, ig = _sigmoid(za), _sigmoid(zx)
    sp = _softplus(-lam_ref[...])
    la = -RG_C * r * sp
    a = jnp.exp(la)
    mult = jnp.sqrt(_neg_expm1(2.0 * la))
    return xb, r, ig, sp, a, mult


def _rg_fwd(xg2, conv_w, conv_b, w_a, b_a, w_x, b_x, lam, ts=256, side=None):
    _, s, c = xg2.shape
    ts = min(ts, s)
    kw = RG_CONV_W
    hb = ts // SUBLANES

    def body(xg_ref, halo_ref, cw_ref, cb_ref, wa_ref, ba_ref, wx_ref, bx_ref, lam_ref, xr_ref, hs_ref, y_ref, carry_ref,
             a_scr, b_scr):
        i = pl.program_id(0)

        @pl.when(i == 0)
        def _():
            carry_ref[...] = jnp.zeros_like(carry_ref)

        xp = xg_ref[0]
        halo = jnp.where(i == 0, 0.0, halo_ref[...])
        xr = cb_ref[...] + cw_ref[kw - 1:kw, :] * xp
        for sft in range(1, kw):
            xr = xr + cw_ref[kw - 1 - sft:kw - sft, :] * _shift_down(xp, halo, sft)
        _, r, ig, sp, a, mult = _rg_gates(xr, wa_ref, ba_ref, wx_ref, bx_ref, lam_ref)
        a_scr[...] = a
        b_scr[...] = mult * (ig * xr)
        _real_slab_scan(a_scr, b_scr, hs_ref, carry_ref, reverse=False)
        xr_ref[...] = xr
        y_ref[...] = (hs_ref[...] * _gelu(xg_ref[1])).astype(BF16)

    full = lambda shape: pl.BlockSpec(shape, lambda i: (0,) * len(shape))
    row_spec = pl.BlockSpec((ts, c), lambda i: (i, 0))
    nt = s // ts
    return _call_with_side(
        body, side, lambda: pl.program_id(0) == 0, lambda: pl.program_id(0) == nt - 1,
        name="rg_fwd", grid=(nt,),
        in_specs=[pl.BlockSpec((2, ts, c), lambda i: (0, i, 0)),
                  pl.BlockSpec((None, SUBLANES, c), lambda i: (0, jnp.maximum(i * hb - 1, 0), 0)),
                  full(conv_w.shape), full(conv_b.shape), full(w_a.shape), full(b_a.shape), full(w_x.shape), full(b_x.shape),
                  full(lam.shape)],
        out_specs=[row_spec, row_spec, row_spec],
        out_shape=[jax.ShapeDtypeStruct((s, c), F32), jax.ShapeDtypeStruct((s, c), F32), jax.ShapeDtypeStruct((s, c), BF16)],
        scratch_shapes=[pltpu.VMEM((1, c), F32), pltpu.VMEM((ts, c), F32), pltpu.VMEM((ts, c), F32)],
        args=(xg2, xg2, conv_w, conv_b, w_a, b_a, w_x, b_x, lam))


def _rg_bwd(dy, xg2, xr, hs, conv_w, w_a, b_a, w_x, b_x, lam, ts=256, side=None):
    _, s, c = xg2.shape
    ts = min(ts, s)
    nt = s // ts
    kw = RG_CONV_W
    hb = ts // SUBLANES
    bw = c // RG_HEADS
    tn_dims = (((0,), (0,)), ((), ()))
    nt_dims = (((1,), (1,)), ((), ()))

    def body(dy_ref, xg_ref, xph_ref, xr_ref, hs_ref, hsh_ref, cw_ref, wa_ref, ba_ref, wx_ref, bx_ref, lam_ref,
             dxg_ref, dcw_ref, dcb_ref, dwa_ref, dba_ref, dwx_ref, dbx_ref, dlam_ref,
             lam_carry, a_carry, dxr_carry, dsp_acc, a_scr, b_scr):
        i = pl.program_id(0)
        first_step = i == 0
        time_first = i == nt - 1

        @pl.when(first_step)
        def _():
            lam_carry[...] = jnp.zeros_like(lam_carry)
            a_carry[...] = jnp.ones_like(a_carry)
            dxr_carry[...] = jnp.zeros_like(dxr_carry)
            dsp_acc[...] = jnp.zeros_like(dsp_acc)
            for ref in (dcw_ref, dcb_ref, dwa_ref, dba_ref, dwx_ref, dbx_ref):
                ref[...] = jnp.zeros_like(ref)

        xr = xr_ref[...]
        hs = hs_ref[...]
        gate = xg_ref[1]
        xb, r, ig, sp, a, mult = _rg_gates(xr, wa_ref, ba_ref, wx_ref, bx_ref, lam_ref)
        dyv = dy_ref[...]
        gg, dgg = _gelu_and_grad(gate)
        dhs = dyv * gg
        dxg_ref[1] = (dyv * hs * dgg).astype(BF16)
        row = _rows(xr.shape)
        a_scr[...] = jnp.where(row == ts - 1, a_carry[0:1, :], pltpu.roll(a, ts - 1, 0))
        b_scr[...] = dhs
        _real_slab_scan(a_scr, b_scr, b_scr, lam_carry, reverse=True)
        lmb = b_scr[...]
        a_carry[...] = a[:SUBLANES]
        hs_prev = _shift_down(hs, jnp.where(time_first, 0.0, hsh_ref[...]), 1)
        d_a = lmb * hs_prev
        d_m = lmb * (ig * xr)
        d_ig = lmb * mult * xr
        d_xr = lmb * mult * ig
        d_la = a * d_a - (a * a / mult) * d_m
        dsp_acc[...] += jnp.sum(-RG_C * r * d_la, axis=0, keepdims=True)
        d_za = (-RG_C * sp) * d_la * r * (1.0 - r)
        d_zx = d_ig * ig * (1.0 - ig)
        dba_ref[...] += jnp.sum(d_za, axis=0, keepdims=True)
        dbx_ref[...] += jnp.sum(d_zx, axis=0, keepdims=True)
        dzab, dzxb = d_za.astype(BF16), d_zx.astype(BF16)
        back = []
        for h in range(RG_HEADS):
            sl = slice(h * bw, (h + 1) * bw)
            dwa_ref[h] += lax.dot_general(xb[:, sl], dzab[:, sl], tn_dims, preferred_element_type=F32)
            dwx_ref[h] += lax.dot_general(xb[:, sl], dzxb[:, sl], tn_dims, preferred_element_type=F32)
            back.append(lax.dot_general(dzab[:, sl], wa_ref[h], nt_dims, preferred_element_type=F32)
                        + lax.dot_general(dzxb[:, sl], wx_ref[h], nt_dims, preferred_element_type=F32))
        d_xr = d_xr + jnp.concatenate(back, axis=1)
        d_xp = cw_ref[kw - 1:kw, :] * d_xr
        after = dxr_carry[...]
        for sft in range(1, kw):
            d_xp = d_xp + cw_ref[kw - 1 - sft:kw - sft, :] * _shift_up(d_xr, after, sft)
        dxr_carry[...] = d_xr[:SUBLANES]
        dxg_ref[0] = d_xp.astype(BF16)
        xp = xg_ref[0]
        before = jnp.where(time_first, 0.0, xph_ref[...])
        dcb_ref[...] += jnp.sum(d_xr, axis=0, keepdims=True)
        dcw_ref[kw - 1:kw, :] += jnp.sum(d_xr * xp, axis=0, keepdims=True)
        for sft in range(1, kw):
            dcw_ref[kw - 1 - sft:kw - sft, :] += jnp.sum(d_xr * _shift_down(xp, before, sft), axis=0, keepdims=True)
        dlam_ref[...] = dsp_acc[...] * (-_sigmoid(-lam_ref[...]))

    full = lambda shape: pl.BlockSpec(shape, lambda i: (0,) * len(shape))
    rev = lambda i: nt - 1 - i
    row_spec = pl.BlockSpec((ts, c), lambda i: (rev(i), 0))
    halo_idx = lambda i: jnp.maximum(rev(i) * hb - 1, 0)
    vec = (1, c)
    return _call_with_side(
        body, side, lambda: pl.program_id(0) == 0, lambda: pl.program_id(0) == nt - 1,
        name="rg_bwd", grid=(nt,),
        in_specs=[row_spec,
                  pl.BlockSpec((2, ts, c), lambda i: (0, rev(i), 0)),
                  pl.BlockSpec((None, SUBLANES, c), lambda i: (0, halo_idx(i), 0)),
                  row_spec, row_spec,
                  pl.BlockSpec((SUBLANES, c), lambda i: (halo_idx(i), 0)),
                  full(conv_w.shape), full(w_a.shape), full(b_a.shape), full(w_x.shape), full(b_x.shape), full(lam.shape)],
        out_specs=[pl.BlockSpec((2, ts, c), lambda i: (0, rev(i), 0)), full(conv_w.shape), full(vec), full(w_a.shape), full(vec),
                   full(w_x.shape), full(vec), full(vec)],
        out_shape=[jax.ShapeDtypeStruct((2, s, c), BF16), jax.ShapeDtypeStruct(conv_w.shape, F32), jax.ShapeDtypeStruct(vec, F32),
                   jax.ShapeDtypeStruct(w_a.shape, F32), jax.ShapeDtypeStruct(vec, F32), jax.ShapeDtypeStruct(w_x.shape, F32),
                   jax.ShapeDtypeStruct(vec, F32), jax.ShapeDtypeStruct(vec, F32)],
        scratch_shapes=[pltpu.VMEM(vec, F32), pltpu.VMEM((SUBLANES, c), F32), pltpu.VMEM((SUBLANES, c), F32),
                        pltpu.VMEM(vec, F32), pltpu.VMEM((ts, c), F32), pltpu.VMEM((ts, c), F32)],
        args=(dy, xg2, xg2, xr, hs, hs, conv_w, w_a, b_a, w_x, b_x, lam))


def _s5_param_fn(a_re, a_im, log_dt, bt_re, bt_im):
    dt = jnp.exp(log_dt)
    mag = jnp.exp(a_re * dt)
    abr = mag * jnp.cos(a_im * dt)
    abi = mag * jnp.sin(a_im * dt)
    ur, ui = abr - 1.0, abi
    den = a_re * a_re + a_im * a_im
    wr = (ur * a_re + ui * a_im) / den
    wi = (ui * a_re - ur * a_im) / den
    bbr = wr[None] * bt_re - wi[None] * bt_im
    bbi = wr[None] * bt_im + wi[None] * bt_re
    return abr, abi, bbr, bbi


def _s5_params_bwd(a_re, a_im, log_dt, bt_re, bt_im, d_abr, d_abi, d_bbr, d_bbi):
    def body(ar_ref, ai_ref, dt_ref, br_ref, bi_ref, g0, g1, g2, g3, o0, o1, o2, o3, o4):
        _, vjp = jax.vjp(_s5_param_fn, ar_ref[...], ai_ref[...], dt_ref[...], br_ref[...], bi_ref[...])
        outs = vjp((g0[...], g1[...], g2[...], g3[...]))
        for o, v in zip((o0, o1, o2, o3, o4), outs):
            o[...] = v

    sd = jax.ShapeDtypeStruct
    return pl.pallas_call(
        body, name="s5_params_bwd",
        out_shape=[sd(a_re.shape, F32), sd(a_im.shape, F32), sd(log_dt.shape, F32), sd(bt_re.shape, F32), sd(bt_im.shape, F32)],
    )(a_re, a_im, log_dt, bt_re, bt_im, d_abr, d_abi, d_bbr, d_bbi)


S5_LANE_CHUNK = 512


def _cmul_add(br, bi, tr, ti, sr, si):
    return br + tr * sr - ti * si, bi + tr * si + ti * sr


def _s5_tables3(a_re, a_im, log_dt, bt_re, bt_im, seg):
    g, p = a_re.shape
    gc = bt_re.shape[0]
    nsq = int(math.log2(seg))
    assert 1 << nsq == seg

    def body(ar_ref, ai_ref, dt_ref, br_ref, bi_ref, tr_ref, ti_ref, rtr_ref, rti_ref, bbr_ref, bbi_ref):
        abr, abi, bbr, bbi = _s5_param_fn(ar_ref[...], ai_ref[...], dt_ref[...], br_ref[...], bi_ref[...])
        bbr_ref[...] = bbr
        bbi_ref[...] = bbi
        qr, qi = abr, abi
        for _ in range(nsq):
            qr, qi = qr * qr - qi * qi, 2.0 * qr * qi
        pows = [(qr, qi)]
        for _ in range(1, SUBLANES):
            cr, ci = pows[-1]
            pows.append((cr * qr - ci * qi, cr * qi + ci * qr))
        zero = jnp.zeros_like(abr)
        for r in range(SUBLANES):
            rows = [(pows[(1 << k) - 1] if r >= (1 << k) else (zero, zero)) for k in range(3)] + [pows[r], (abr, abi)]
            for k, (vr, vi) in enumerate(rows):
                tr_ref[k, r] = vr
                ti_ref[k, r] = vi
                rtr_ref[k, SUBLANES - 1 - r] = vr
                rti_ref[k, SUBLANES - 1 - r] = -vi

    sd = jax.ShapeDtypeStruct
    tab = sd((5, SUBLANES, g, p), F32)
    return pl.pallas_call(
        body, name="s5_tables", out_shape=[tab, tab, tab, tab, sd((gc, g, p), F32), sd((gc, g, p), F32)],
    )(a_re, a_im, log_dt, bt_re, bt_im)


def _segment_perm(ts):
    seg = ts // SUBLANES
    rho = jnp.arange(ts)
    src = (rho % SUBLANES) * seg + rho // SUBLANES
    return (src[:, None] == jnp.arange(ts)[None, :]).astype(BF16)


def _exact_rows(perm_t, x):
    hi = x.astype(BF16)
    r1 = x - hi.astype(F32)
    mid = r1.astype(BF16)
    lo = (r1 - mid.astype(F32)).astype(BF16)
    dot = lambda v: jnp.dot(perm_t, v, preferred_element_type=F32)
    return (dot(hi) + dot(mid)) + dot(lo)


def _s5_fwd3(u, perm, perm_t, tab_r, tab_i, bp_r, bp_i, cp_r, cp_i, dvec, ts=256, side=None):
    s, c = u.shape
    n = tab_r.shape[2]
    nblk, cb, nb = bp_r.shape
    ts = min(ts, s)
    seg = ts // SUBLANES
    lc = min(S5_LANE_CHUNK, n)

    def body(u_ref, p_ref, pt_ref, tr_ref, ti_ref, bpr_ref, bpi_ref, cpr_ref, cpi_ref, d_ref, hr_ref, hi_ref, yp_ref, gy_ref,
             bur_ref, bui_ref, car_r, car_i):
        i = pl.program_id(0)

        @pl.when(i == 0)
        def _():
            car_r[...] = jnp.zeros_like(car_r)
            car_i[...] = jnp.zeros_like(car_i)

        uv = u_ref[...]
        ubp = jnp.dot(p_ref[...], uv.astype(BF16), preferred_element_type=F32).astype(BF16)
        for k in range(nblk):
            bur_ref[:, k * nb:(k + 1) * nb] = jnp.dot(ubp[:, k * cb:(k + 1) * cb], bpr_ref[k], preferred_element_type=F32)
            bui_ref[:, k * nb:(k + 1) * nb] = jnp.dot(ubp[:, k * cb:(k + 1) * cb], bpi_ref[k], preferred_element_type=F32)
        row8 = _rows((SUBLANES, lc))
        for q in range(n // lc):
            sl = slice(q * lc, (q + 1) * lc)
            tabs = [(tr_ref[k, :, sl], ti_ref[k, :, sl]) for k in range(5)]
            a_r, a_i = tabs[4]

            def local(r, carry, sl=sl, a_r=a_r, a_i=a_i):
                r0 = pl.multiple_of(r * SUBLANES, SUBLANES)
                hr, hi = _cmul_add(bur_ref[pl.ds(r0, SUBLANES), sl], bui_ref[pl.ds(r0, SUBLANES), sl], a_r, a_i, carry[0], carry[1])
                hr_ref[pl.ds(r0, SUBLANES), sl] = hr
                hi_ref[pl.ds(r0, SUBLANES), sl] = hi
                return hr, hi

            zero = jnp.zeros((SUBLANES, lc), F32)
            er, ei = lax.fori_loop(0, seg, local, (zero, zero), unroll=4)
            for k in range(3):
                sh = 1 << k
                er, ei = _cmul_add(er, ei, tabs[k][0], tabs[k][1], pltpu.roll(er, sh, 0), pltpu.roll(ei, sh, 0))
            cin_r, cin_i = jnp.broadcast_to(car_r[:, sl], er.shape), jnp.broadcast_to(car_i[:, sl], ei.shape)
            er, ei = _cmul_add(er, ei, tabs[3][0], tabs[3][1], cin_r, cin_i)
            car_r[:, sl] = er[SUBLANES - 1:, :]
            car_i[:, sl] = ei[SUBLANES - 1:, :]
            c_r = jnp.where(row8 == 0, cin_r, pltpu.roll(er, 1, 0))
            c_i = jnp.where(row8 == 0, cin_i, pltpu.roll(ei, 1, 0))

            def fix(r, carry, sl=sl, a_r=a_r, a_i=a_i, c_r=c_r, c_i=c_i):
                pr, pi = carry
                r0 = pl.multiple_of(r * SUBLANES, SUBLANES)
                hr, hi = _cmul_add(hr_ref[pl.ds(r0, SUBLANES), sl], hi_ref[pl.ds(r0, SUBLANES), sl], pr, pi, c_r, c_i)
                hr_ref[pl.ds(r0, SUBLANES), sl] = hr
                hi_ref[pl.ds(r0, SUBLANES), sl] = hi
                return pr * a_r - pi * a_i, pr * a_i + pi * a_r

            lax.fori_loop(0, seg, fix, (a_r, a_i), unroll=4)
        hrb, hib = hr_ref[...].astype(BF16), hi_ref[...].astype(BF16)
        y = jnp.concatenate([jnp.dot(hrb[:, k * nb:(k + 1) * nb], cpr_ref[k], preferred_element_type=F32)
                             - jnp.dot(hib[:, k * nb:(k + 1) * nb], cpi_ref[k], preferred_element_type=F32) for k in range(nblk)], axis=1)
        yp = _exact_rows(pt_ref[...], y) + d_ref[...] * uv
        yp_ref[...] = yp
        gy_ref[...] = _gelu(yp).astype(BF16)

    full = lambda shape: pl.BlockSpec(shape, lambda i: (0,) * len(shape))
    rc = pl.BlockSpec((ts, c), lambda i: (i, 0))
    rn = pl.BlockSpec((ts, n), lambda i: (i, 0))
    sd = jax.ShapeDtypeStruct
    nt = s // ts
    return _call_with_side(
        body, side, lambda: pl.program_id(0) == 0, lambda: pl.program_id(0) == nt - 1,
        name="s5_fwd", grid=(nt,),
        in_specs=[rc, full(perm.shape), full(perm_t.shape), full(tab_r.shape), full(tab_i.shape), full(bp_r.shape), full(bp_i.shape),
                  full(cp_r.shape), full(cp_i.shape), full(dvec.shape)],
        out_specs=[rn, rn, rc, rc],
        out_shape=[sd((s, n), F32), sd((s, n), F32), sd((s, c), F32), sd((s, c), BF16)],
        scratch_shapes=[pltpu.VMEM((ts, n), F32), pltpu.VMEM((ts, n), F32), pltpu.VMEM((1, n), F32), pltpu.VMEM((1, n), F32)],
        args=(u, perm, perm_t, tab_r, tab_i, bp_r, bp_i, cp_r, cp_i, dvec))


def _s5_bwd3(dgy, ypre, u, hr, hi, perm, perm_t, rtab_r, rtab_i, bp_r, bp_i, cp_r, cp_i, dvec, ts=256, side=None):
    s, c = u.shape
    n = rtab_r.shape[2]
    nblk, cb, nb = bp_r.shape
    ts = min(ts, s)
    nt = s // ts
    hb = ts // SUBLANES
    seg = ts // SUBLANES
    lc = min(S5_LANE_CHUNK, n)
    tn_dims = (((0,), (0,)), ((), ()))
    nt_dims = (((1,), (1,)), ((), ()))

    def body(dgy_ref, yp_ref, u_ref, hr_ref, hi_ref, hrh_ref, hih_ref, p_ref, pt_ref, tr_ref, ti_ref, bpr_ref, bpi_ref,
             cpr_ref, cpi_ref, d_ref, du_ref, dar_ref, dai_ref, dbr_ref, dbi_ref, dcr_ref, dci_ref, dd_ref, lr_ref, li_ref,
             car_r, car_i):
        i = pl.program_id(0)
        time_first = i == nt - 1

        @pl.when(i == 0)
        def _():
            car_r[...] = jnp.zeros_like(car_r)
            car_i[...] = jnp.zeros_like(car_i)
            for ref in (dar_ref, dai_ref, dbr_ref, dbi_ref, dcr_ref, dci_ref, dd_ref):
                ref[...] = jnp.zeros_like(ref)

        uv = u_ref[...]
        _, dgel = _gelu_and_grad(yp_ref[...])
        dyv = dgy_ref[...] * dgel
        dd_ref[...] += jnp.sum(dyv * uv, axis=0, keepdims=True)
        perm_m = p_ref[...]
        dyb = jnp.dot(perm_m, dyv.astype(BF16), preferred_element_type=F32).astype(BF16)
        ub = jnp.dot(perm_m, uv.astype(BF16), preferred_element_type=F32).astype(BF16)
        hrb, hib = hr_ref[...].astype(BF16), hi_ref[...].astype(BF16)
        for k in range(nblk):
            dblk = dyb[:, k * cb:(k + 1) * cb]
            lr_ref[:, k * nb:(k + 1) * nb] = lax.dot_general(dblk, cpr_ref[k], nt_dims, preferred_element_type=F32)
            li_ref[:, k * nb:(k + 1) * nb] = -lax.dot_general(dblk, cpi_ref[k], nt_dims, preferred_element_type=F32)
            dcr_ref[k] += lax.dot_general(hrb[:, k * nb:(k + 1) * nb], dblk, tn_dims, preferred_element_type=F32)
            dci_ref[k] += lax.dot_general(hib[:, k * nb:(k + 1) * nb], dblk, tn_dims, preferred_element_type=F32)
        row8 = _rows((SUBLANES, lc))
        last0 = (seg - 1) * SUBLANES
        for q in range(n // lc):
            sl = slice(q * lc, (q + 1) * lc)
            tabs = [(tr_ref[k, :, sl], ti_ref[k, :, sl]) for k in range(5)]
            a_r, a_i = tabs[4]

            def local(rr, carry, sl=sl, a_r=a_r, a_i=a_i):
                r0 = pl.multiple_of((seg - 1 - rr) * SUBLANES, SUBLANES)
                lr, li = _cmul_add(lr_ref[pl.ds(r0, SUBLANES), sl], li_ref[pl.ds(r0, SUBLANES), sl], a_r, a_i, carry[0], carry[1])
                lr_ref[pl.ds(r0, SUBLANES), sl] = lr
                li_ref[pl.ds(r0, SUBLANES), sl] = li
                return lr, li

            zero = jnp.zeros((SUBLANES, lc), F32)
            er, ei = lax.fori_loop(0, seg, local, (zero, zero), unroll=4)
            for k in range(3):
                sh = 1 << k
                er, ei = _cmul_add(er, ei, tabs[k][0], tabs[k][1], pltpu.roll(er, SUBLANES - sh, 0), pltpu.roll(ei, SUBLANES - sh, 0))
            cin_r, cin_i = jnp.broadcast_to(car_r[:, sl], er.shape), jnp.broadcast_to(car_i[:, sl], ei.shape)
            er, ei = _cmul_add(er, ei, tabs[3][0], tabs[3][1], cin_r, cin_i)
            car_r[:, sl] = er[:1, :]
            car_i[:, sl] = ei[:1, :]
            c_r = jnp.where(row8 == SUBLANES - 1, cin_r, pltpu.roll(er, SUBLANES - 1, 0))
            c_i = jnp.where(row8 == SUBLANES - 1, cin_i, pltpu.roll(ei, SUBLANES - 1, 0))
            halo_r = jnp.where(time_first, 0.0, hrh_ref[SUBLANES - 1:, sl])
            halo_i = jnp.where(time_first, 0.0, hih_ref[SUBLANES - 1:, sl])
            hp0_r = jnp.where(row8 == 0, jnp.broadcast_to(halo_r, zero.shape), pltpu.roll(hr_ref[pl.ds(last0, SUBLANES), sl], 1, 0))
            hp0_i = jnp.where(row8 == 0, jnp.broadcast_to(halo_i, zero.shape), pltpu.roll(hi_ref[pl.ds(last0, SUBLANES), sl], 1, 0))

            def fix(rr, carry, sl=sl, a_r=a_r, a_i=a_i, c_r=c_r, c_i=c_i, hp0_r=hp0_r, hp0_i=hp0_i):
                pr, pi, acc_r, acc_i = carry
                r = seg - 1 - rr
                r0 = pl.multiple_of(r * SUBLANES, SUBLANES)
                lr, li = _cmul_add(lr_ref[pl.ds(r0, SUBLANES), sl], li_ref[pl.ds(r0, SUBLANES), sl], pr, pi, c_r, c_i)
                lr_ref[pl.ds(r0, SUBLANES), sl] = lr
                li_ref[pl.ds(r0, SUBLANES), sl] = li
                p0 = pl.multiple_of(jnp.maximum(r - 1, 0) * SUBLANES, SUBLANES)
                hpr = jnp.where(r == 0, hp0_r, hr_ref[pl.ds(p0, SUBLANES), sl])
                hpi = jnp.where(r == 0, hp0_i, hi_ref[pl.ds(p0, SUBLANES), sl])
                return (pr * a_r - pi * a_i, pr * a_i + pi * a_r, acc_r + (lr * hpr + li * hpi), acc_i + (li * hpr - lr * hpi))

            _, _, acc_r, acc_i = lax.fori_loop(0, seg, fix, (a_r, a_i, zero, zero), unroll=4)
            dar_ref[:, sl] += jnp.sum(acc_r, axis=0, keepdims=True)
            dai_ref[:, sl] += jnp.sum(acc_i, axis=0, keepdims=True)
        lrb, lib = lr_ref[...].astype(BF16), li_ref[...].astype(BF16)
        du = []
        for k in range(nblk):
            ublk = ub[:, k * cb:(k + 1) * cb]
            lrk, lik = lrb[:, k * nb:(k + 1) * nb], lib[:, k * nb:(k + 1) * nb]
            dbr_ref[k] += lax.dot_general(ublk, lrk, tn_dims, preferred_element_type=F32)
            dbi_ref[k] += lax.dot_general(ublk, lik, tn_dims, preferred_element_type=F32)
            du.append(lax.dot_general(lrk, bpr_ref[k], nt_dims, preferred_element_type=F32)
                      + lax.dot_general(lik, bpi_ref[k], nt_dims, preferred_element_type=F32))
        du_ref[...] = (d_ref[...] * dyv + _exact_rows(pt_ref[...], jnp.concatenate(du, axis=1))).astype(BF16)

    full = lambda shape: pl.BlockSpec(shape, lambda i: (0,) * len(shape))
    rev = lambda i: nt - 1 - i
    halo_idx = lambda i: jnp.maximum(rev(i) * hb - 1, 0)
    rc = pl.BlockSpec((ts, c), lambda i: (rev(i), 0))
    rn = pl.BlockSpec((ts, n), lambda i: (rev(i), 0))
    hn = pl.BlockSpec((SUBLANES, n), lambda i: (halo_idx(i), 0))
    sd = jax.ShapeDtypeStruct
    vec_n = (1, n)
    return _call_with_side(
        body, side, lambda: pl.program_id(0) == 0, lambda: pl.program_id(0) == nt - 1,
        name="s5_bwd", grid=(nt,),
        in_specs=[rc, rc, rc, rn, rn, hn, hn, full(perm.shape), full(perm_t.shape), full(rtab_r.shape), full(rtab_i.shape),
                  full(bp_r.shape), full(bp_i.shape), full(cp_r.shape), full(cp_i.shape), full(dvec.shape)],
        out_specs=[rc, full(vec_n), full(vec_n), full(bp_r.shape), full(bp_i.shape), full(cp_r.shape), full(cp_i.shape),
                   full(dvec.shape)],
        out_shape=[sd((s, c), BF16), sd(vec_n, F32), sd(vec_n, F32), sd(bp_r.shape, F32), sd(bp_i.shape, F32),
                   sd(cp_r.shape, F32), sd(cp_i.shape, F32), sd(dvec.shape, F32)],
        scratch_shapes=[pltpu.VMEM((ts, n), F32), pltpu.VMEM((ts, n), F32), pltpu.VMEM((1, n), F32), pltpu.VMEM((1, n), F32)],
        args=(dgy, ypre, u, hr, hi, hr, hi, perm, perm_t, rtab_r, rtab_i, bp_r, bp_i, cp_r, cp_i, dvec))


def _glu_mm(gy, w_glu4, ts=1024, tn=512):
    s, d = gy.shape
    p, _, wc = w_glu4.shape
    c = p * wc // 2
    ts, tn = min(ts, s), min(tn, wc)
    per = wc // tn

    def body(x_ref, w1_ref, w2_ref, gl_ref, o_ref):
        x = x_ref[...]
        val = jnp.dot(x, w1_ref[...], preferred_element_type=F32)
        gate = jnp.dot(x, w2_ref[...], preferred_element_type=F32)
        gl_ref[0] = val
        gl_ref[1] = gate
        o_ref[...] = (val * _sigmoid(gate)).astype(BF16)

    return pl.pallas_call(
        body, name="s5_glu", grid=(c // tn, s // ts),
        in_specs=[pl.BlockSpec((ts, d), lambda j, i: (i, 0)),
                  pl.BlockSpec((None, d, tn), lambda j, i: (j // per, 0, j % per)),
                  pl.BlockSpec((None, d, tn), lambda j, i: (p // 2 + j // per, 0, j % per))],
        out_specs=[pl.BlockSpec((2, ts, tn), lambda j, i: (0, i, j)), pl.BlockSpec((ts, tn), lambda j, i: (i, j))],
        out_shape=[jax.ShapeDtypeStruct((2, s, c), F32), jax.ShapeDtypeStruct((s, c), BF16)], compiler_params=_cparams(),
    )(gy, w_glu4, w_glu4)


def _glu_bwd_mm(dh, w_out, gl2, ts=512):
    s, d = dh.shape
    c = w_out.shape[0]
    ts = min(ts, s)
    nt_dims = (((1,), (1,)), ((), ()))

    def body(dh_ref, w_ref, g_ref, o_ref):
        dov = lax.dot_general(dh_ref[...].astype(BF16), w_ref[...], nt_dims, preferred_element_type=F32)
        sg = _sigmoid(g_ref[1])
        o_ref[0] = (dov * sg).astype(BF16)
        o_ref[1] = (dov * g_ref[0] * sg * (1.0 - sg)).astype(BF16)

    blk = pl.BlockSpec((2, ts, c), lambda i: (0, i, 0))
    return pl.pallas_call(
        body, name="s5_out_dx", grid=(s // ts,),
        in_specs=[pl.BlockSpec((ts, d), lambda i: (i, 0)), pl.BlockSpec((c, d), lambda i: (0, 0)), blk],
        out_specs=blk, out_shape=jax.ShapeDtypeStruct((2, s, c), BF16), compiler_params=_cparams(),
    )(dh, w_out, gl2)


PACK_ROW_MULTIPLE = 1024
ELEMENTWISE_BLOCK_ELEMS = 256 * 1024


def _row_tile(rows, cols):
    pref = max(SUBLANES, 1 << int(math.log2(max(1, ELEMENTWISE_BLOCK_ELEMS // cols))))
    if rows <= pref:
        return rows
    t = pref
    while rows % t:
        t //= 2
    assert t >= SUBLANES, rows
    return t


def _sum_parts(rs, side=None):
    nl = len(rs)
    p, rows, cols = rs[0].shape
    tr = _row_tile(rows, cols)
    nt = rows // tr

    def body(*refs):
        o_ref = refs[nl]
        for l in range(nl):
            acc = refs[l][0].astype(F32)
            for k in range(1, p):
                acc = acc + refs[l][k].astype(F32)
            o_ref[l] = acc

    outs, got = _call_with_side(
        body, side, lambda: pl.program_id(0) == 0, lambda: pl.program_id(0) == nt - 1,
        name="sum_parts", grid=(nt,), in_specs=[pl.BlockSpec((p, tr, cols), lambda i: (0, i, 0))] * nl,
        out_specs=[pl.BlockSpec((nl, tr, cols), lambda i: (0, i, 0))], out_shape=[jax.ShapeDtypeStruct((nl, rows, cols), F32)],
        scratch_shapes=[], args=tuple(rs))
    return outs[0], got


def _adamw(w, g_parts, m, v, side=None):
    rows, cols = w.shape
    tr = _row_tile(rows, max(cols, LANES))
    ng = len(g_parts)
    emit_grad = ng > 1
    c1 = 1.0 / (1.0 - ADAM_B1 ** ADAM_STEP)
    c2 = 1.0 / (1.0 - ADAM_B2 ** ADAM_STEP)

    def body(*refs):
        w_ref, m_ref, v_ref = refs[0], refs[1 + ng], refs[2 + ng]
        dl_ref, nm_ref, nv_ref = refs[3 + ng:6 + ng]
        g = refs[1][...]
        for k in range(1, ng):
            g = g + refs[1 + k][...]
        mn = ADAM_B1 * m_ref[...] + (1.0 - ADAM_B1) * g
        vn = ADAM_B2 * v_ref[...] + (1.0 - ADAM_B2) * (g * g)
        if emit_grad:
            refs[6 + ng][...] = g
        nm_ref[...] = mn
        nv_ref[...] = vn
        dl_ref[...] = -ADAM_LR * ((mn * c1) / (jnp.sqrt(vn * c2) + ADAM_EPS) + ADAM_WD * w_ref[...])

    blk = pl.BlockSpec((tr, cols), lambda i: (i, 0))
    sd = jax.ShapeDtypeStruct((rows, cols), F32)
    nout = 4 if emit_grad else 3
    nt = rows // tr
    return _call_with_side(
        body, side, lambda: pl.program_id(0) == 0, lambda: pl.program_id(0) == nt - 1,
        name="adamw", grid=(nt,), in_specs=[blk] * (3 + ng), out_specs=[blk] * nout, out_shape=[sd] * nout,
        scratch_shapes=[], args=(w, *g_parts, m, v))


def _place():
    x, y, c = lax.axis_index("x"), lax.axis_index("y"), lax.axis_index("c")
    chips = [(1 - x, y), (x, 1 - y), (1 - x, 1 - y)]
    return x, y, c, chips


class Side:
    def __init__(self, ins, outs, kind):
        self.ins, self.outs, self.kind = list(ins), list(outs), kind
        n = len(self.ins)
        self.sems = [pltpu.SemaphoreType.DMA((3 * n,)), pltpu.SemaphoreType.DMA((3 * n,)), pltpu.SemaphoreType.DMA((n,))]

    def _copies(self, ins, outs, send, recv, lsem):
        x, y, c, chips = _place()
        me = 2 * x + y
        local, out_going, in_coming = [], [], []
        for t in range(len(ins)):
            if self.kind == 'sibling':
                cp = pltpu.make_async_remote_copy(src_ref=ins[t], dst_ref=outs[t], send_sem=send.at[t], recv_sem=recv.at[t],
                                                  device_id=(x, y, 1 - c), device_id_type=MESH)
                out_going.append(cp)
                in_coming.append(cp)
                continue
            if self.kind == 'gather':
                src_local, srcs, dst_mine = ins[t], [ins[t]] * 3, outs[t].at[me]
            else:
                src_local, srcs, dst_mine = ins[t].at[me], [ins[t].at[2 * px + py] for px, py in chips], outs[t].at[me]
            local.append(pltpu.make_async_copy(src_local, dst_mine, lsem.at[t]))
            for r, (px, py) in enumerate(chips):
                out_going.append(pltpu.make_async_remote_copy(
                    src_ref=srcs[r], dst_ref=dst_mine, send_sem=send.at[3 * t + r], recv_sem=recv.at[3 * t + r],
                    device_id=(px, py, c), device_id_type=MESH))
                in_coming.append(pltpu.make_async_remote_copy(
                    src_ref=srcs[r], dst_ref=outs[t].at[2 * px + py], send_sem=send.at[3 * t + r], recv_sem=recv.at[3 * t + r],
                    device_id=(px, py, c), device_id_type=MESH))
        return local, out_going, in_coming

    def start(self, ins, outs, send, recv, lsem):
        local, out_going, _ = self._copies(ins, outs, send, recv, lsem)
        for cp in local + out_going:
            cp.start()

    def wait(self, ins, outs, send, recv, lsem):
        local, out_going, in_coming = self._copies(ins, outs, send, recv, lsem)
        for cp in in_coming:
            cp.wait_recv()
        for cp in out_going:
            cp.wait_send()
        for cp in local:
            cp.wait()


def _gather_side(shards):
    return Side(shards, [jax.ShapeDtypeStruct((N_CHIPS,) + s.shape, s.dtype) for s in shards], 'gather')


def _scatter_side(grads):
    return Side(grads, [jax.ShapeDtypeStruct(g.shape, g.dtype) for g in grads], 'scatter')


def _sibling_side(arrs):
    return Side(arrs, [jax.ShapeDtypeStruct(a.shape, a.dtype) for a in arrs], 'sibling')


def _call_with_side(body, side, first, last, *, name, grid, in_specs, out_specs, out_shape, scratch_shapes, args):
    if side is None:
        outs = pl.pallas_call(body, name=name, grid=grid, in_specs=in_specs, out_specs=out_specs, out_shape=out_shape,
                              scratch_shapes=scratch_shapes, compiler_params=_cparams())(*args)
        return outs, []
    n_in, n_out, n_sc = len(in_specs), len(out_specs), len(scratch_shapes)
    ns_in, ns_out = len(side.ins), len(side.outs)

    def wrapped(*refs):
        base_in, s_in = refs[:n_in], refs[n_in:n_in + ns_in]
        o0 = n_in + ns_in
        base_out, s_out = refs[o0:o0 + n_out], refs[o0 + n_out:o0 + n_out + ns_out]
        sc0 = o0 + n_out + ns_out
        base_sc, sems = refs[sc0:sc0 + n_sc], refs[sc0 + n_sc:]

        @pl.when(first())
        def _():
            side.start(s_in, s_out, *sems)

        body(*base_in, *base_out, *base_sc)

        @pl.when(last())
        def _():
            side.wait(s_in, s_out, *sems)

    any_spec = pl.BlockSpec(memory_space=pl.ANY)
    outs = pl.pallas_call(
        wrapped, name=name, grid=grid, in_specs=list(in_specs) + [any_spec] * ns_in, out_specs=list(out_specs) + [any_spec] * ns_out,
        out_shape=list(out_shape) + side.outs, scratch_shapes=list(scratch_shapes) + side.sems, compiler_params=_cparams(),
    )(*args, *side.ins)
    return outs[:n_out], outs[n_out:]


def _run_side(name, side):
    def body(*refs):
        n = len(side.ins)
        side.start(refs[:n], refs[n:2 * n], *refs[2 * n:])
        side.wait(refs[:n], refs[n:2 * n], *refs[2 * n:])

    any_spec = pl.BlockSpec(memory_space=pl.ANY)
    return pl.pallas_call(body, name=name, in_specs=[any_spec] * len(side.ins), out_specs=[any_spec] * len(side.outs),
                          out_shape=side.outs, scratch_shapes=side.sems)(*side.ins)


def _allreduce_small(v):
    rows, cols = v.shape
    r8 = rows // (2 * N_CHIPS)
    assert r8 * 2 * N_CHIPS == rows and r8 % SUBLANES == 0, rows

    def body(v_ref, o_ref, sib_ref, cs_ref, slot_ref, send, recv):
        x, y, c, chips = _place()
        me = 2 * x + y
        sibling = (x, y, 1 - c)

        def eighth(ref, chip, core):
            return ref.at[pl.ds(pl.multiple_of((2 * chip + core) * r8, SUBLANES), r8)]

        def copy(src, dst, k, to):
            return pltpu.make_async_remote_copy(src_ref=src, dst_ref=dst, send_sem=send.at[k], recv_sem=recv.at[k],
                                                device_id=to, device_id_type=MESH)

        d2d = copy(v_ref, sib_ref, 0, sibling)
        d2d.start()
        d2d.wait_recv()
        cs_ref[...] = v_ref[...] + sib_ref[...]
        reduce_out = [copy(eighth(cs_ref, 2 * px + py, c), slot_ref.at[me], 1 + r, (px, py, c)) for r, (px, py) in enumerate(chips)]
        for cp in reduce_out:
            cp.start()
        slot_ref[me] = cs_ref[pl.ds(pl.multiple_of((2 * me + c) * r8, SUBLANES), r8), :]
        for r, (px, py) in enumerate(chips):
            copy(eighth(cs_ref, me, c), slot_ref.at[2 * px + py], 1 + r, (px, py, c)).wait_recv()
        o_ref[pl.ds(pl.multiple_of((2 * me + c) * r8, SUBLANES), r8), :] = (slot_ref[0] + slot_ref[1]) + (slot_ref[2] + slot_ref[3])
        mine = eighth(o_ref, me, c)
        hand_out = [copy(mine, mine, 4, sibling)] + [copy(mine, mine, 5 + r, (px, py, c)) for r, (px, py) in enumerate(chips)]
        for cp in hand_out:
            cp.start()
        passed_on = []
        for r, (px, py) in enumerate(chips):
            theirs = eighth(o_ref, 2 * px + py, c)
            copy(theirs, theirs, 5 + r, (px, py, c)).wait_recv()
            fw = copy(theirs, theirs, 8 + r, sibling)
            fw.start()
            passed_on.append(fw)
        sib_own = eighth(o_ref, me, 1 - c)
        copy(sib_own, sib_own, 4, sibling).wait_recv()
        for r, (px, py) in enumerate(chips):
            got = eighth(o_ref, 2 * px + py, 1 - c)
            copy(got, got, 8 + r, sibling).wait_recv()
        for cp in [d2d] + reduce_out + hand_out + passed_on:
            cp.wait_send()

    vm = pl.BlockSpec(memory_space=pltpu.VMEM)
    return pl.pallas_call(
        body, name="allreduce_small", in_specs=[vm], out_specs=vm, out_shape=jax.ShapeDtypeStruct((rows, cols), F32),
        scratch_shapes=[pltpu.VMEM((rows, cols), F32), pltpu.VMEM((rows, cols), F32), pltpu.VMEM((N_CHIPS, r8, cols), F32),
                        pltpu.SemaphoreType.DMA((11,)), pltpu.SemaphoreType.DMA((11,))],
        compiler_params=_cparams(),
    )(v)


def _pack(tensors):
    pieces = []
    for t in tensors:
        flat = t.reshape(-1)
        pad = (-flat.shape[0]) % (SUBLANES * LANES)
        pieces.append(jnp.pad(flat, (0, pad)).reshape(-1, LANES))
    rows = sum(p.shape[0] for p in pieces)
    pieces.append(jnp.zeros(((-rows) % PACK_ROW_MULTIPLE, LANES), tensors[0].dtype))
    return jnp.concatenate(pieces, axis=0)


def _unpack(buf, like):
    out, off = [], 0
    for t in like:
        size = math.prod(t.shape)
        rows = -(-size // (SUBLANES * LANES)) * SUBLANES
        out.append(buf[off:off + rows].reshape(-1)[:size].reshape(t.shape))
        off += rows
    return out


def _s5_pack_b(bb):
    gc, g, p = bb.shape
    q = S5_GROUPS_PER_BLOCK
    t = bb.reshape(gc, g // q, q, p).transpose(1, 2, 0, 3)
    eye = jnp.eye(q, dtype=bb.dtype)
    return (t[:, :, :, None, :] * eye[None, :, None, :, None]).reshape(g // q, q * gc, q * p)


def _s5_unpack_b(dbp, gc, p):
    nb = dbp.shape[0]
    q = S5_GROUPS_PER_BLOCK
    eye = jnp.eye(q, dtype=dbp.dtype)
    t = (dbp.reshape(nb, q, gc, q, p) * eye[None, :, None, :, None]).sum(axis=3)
    return t.transpose(2, 0, 1, 3).reshape(gc, nb * q, p)


def _s5_pack_c(cc):
    g, gc, p = cc.shape
    q = S5_GROUPS_PER_BLOCK
    t = cc.reshape(g // q, q, gc, p).transpose(0, 1, 3, 2)
    eye = jnp.eye(q, dtype=cc.dtype)
    return (t[:, :, :, None, :] * eye[None, :, None, :, None]).reshape(g // q, q * p, q * gc)


def _s5_unpack_c(dcp, gc, p):
    nb = dcp.shape[0]
    q = S5_GROUPS_PER_BLOCK
    eye = jnp.eye(q, dtype=dcp.dtype)
    t = (dcp.reshape(nb, q, p, q, gc) * eye[None, :, None, :, None]).sum(axis=3)
    return t.transpose(0, 1, 3, 2).reshape(nb * q, gc, p)


def _split2(m):
    return m.arr[:, 0]


def kernel(x, norm_mix_g, norm_ffn_g, norm_final_g, rg_w_in, rg_conv_w, rg_conv_b, rg_w_a, rg_b_a, rg_w_x, rg_b_x, rg_lambda, rg_w_out, s5_w_in, s5_a_re, s5_a_im, s5_log_dt, s5_b_re, s5_b_im, s5_c_re, s5_c_im, s5_d, s5_w_glu, s5_w_out, ffn_w_up, ffn_conv_w, ffn_conv_b, ffn_w_down, loss_target, m_norm_mix_g, m_norm_ffn_g, m_norm_final_g, m_rg_w_in, m_rg_conv_w, m_rg_conv_b, m_rg_w_a, m_rg_b_a, m_rg_w_x, m_rg_b_x, m_rg_lambda, m_rg_w_out, m_s5_w_in, m_s5_a_re, m_s5_a_im, m_s5_log_dt, m_s5_b_re, m_s5_b_im, m_s5_c_re, m_s5_c_im, m_s5_d, m_s5_w_glu, m_s5_w_out, m_ffn_w_up, m_ffn_conv_w, m_ffn_conv_b, m_ffn_w_down, v_norm_mix_g, v_norm_ffn_g, v_norm_final_g, v_rg_w_in, v_rg_conv_w, v_rg_conv_b, v_rg_w_a, v_rg_b_a, v_rg_w_x, v_rg_b_x, v_rg_lambda, v_rg_w_out, v_s5_w_in, v_s5_a_re, v_s5_a_im, v_s5_log_dt, v_s5_b_re, v_s5_b_im, v_s5_c_re, v_s5_c_im, v_s5_d, v_s5_w_glu, v_s5_w_out, v_ffn_w_up, v_ffn_conv_w, v_ffn_conv_b, v_ffn_w_down):
    w = dict(zip(PARAM_NAMES, (norm_mix_g, norm_ffn_g, norm_final_g, rg_w_in, rg_conv_w, rg_conv_b, rg_w_a, rg_b_a, rg_w_x, rg_b_x,
                               rg_lambda, rg_w_out, s5_w_in, s5_a_re, s5_a_im, s5_log_dt, s5_b_re, s5_b_im, s5_c_re, s5_c_im, s5_d,
                               s5_w_glu, s5_w_out, ffn_w_up, ffn_conv_w, ffn_conv_b, ffn_w_down)))
    mom = dict(zip(PARAM_NAMES, (m_norm_mix_g, m_norm_ffn_g, m_norm_final_g, m_rg_w_in, m_rg_conv_w, m_rg_conv_b, m_rg_w_a, m_rg_b_a,
                                 m_rg_w_x, m_rg_b_x, m_rg_lambda, m_rg_w_out, m_s5_w_in, m_s5_a_re, m_s5_a_im, m_s5_log_dt, m_s5_b_re,
                                 m_s5_b_im, m_s5_c_re, m_s5_c_im, m_s5_d, m_s5_w_glu, m_s5_w_out, m_ffn_w_up, m_ffn_conv_w,
                                 m_ffn_conv_b, m_ffn_w_down)))
    vel = dict(zip(PARAM_NAMES, (v_norm_mix_g, v_norm_ffn_g, v_norm_final_g, v_rg_w_in, v_rg_conv_w, v_rg_conv_b, v_rg_w_a, v_rg_b_a,
                                 v_rg_w_x, v_rg_b_x, v_rg_lambda, v_rg_w_out, v_s5_w_in, v_s5_a_re, v_s5_a_im, v_s5_log_dt, v_s5_b_re,
                                 v_s5_b_im, v_s5_c_re, v_s5_c_im, v_s5_d, v_s5_w_glu, v_s5_w_out, v_ffn_w_up, v_ffn_conv_w,
                                 v_ffn_conv_b, v_ffn_w_down)))
    _, s, d = x.shape
    depth = norm_mix_g.shape[0]
    n_grp, n_state = s5_a_re.shape[1], s5_a_re.shape[2]
    gc = s5_b_re.shape[3]
    d_ff = ffn_w_down.shape[1] * N_CHIPS
    s5_ts = min(256, s)
    s5_perm = _segment_perm(s5_ts)

    wb = {n: (w[n].astype(BF16) if n in BIG else w[n]) for n in SHARDED}
    gath = {}

    def mixer_keys(i):
        return [(n, i // 2) for n in MIXER_SHARDED[i % 2]] if i < depth else []

    def gather_side(keys):
        return _gather_side([wb[n][l] for n, l in keys])

    def put(keys, arrs):
        for k, a in zip(keys, arrs):
            gath[k] = a

    def wcol(n, l):
        return Mat(gath[(n, l)][:, None], 0, 'c')

    def wrow(n, l):
        g = gath[(n, l)]
        return Mat(g.reshape(1, 1, N_CHIPS * g.shape[1], g.shape[2]), 0, 'c')

    def rg_cw(l):
        return gath[('rg_conv_w', l)].transpose(1, 0, 2).reshape(RG_CONV_W, d)

    def s5_dv(l):
        return gath[('s5_d', l)].reshape(1, d)

    def f_cw(l):
        return gath[('ffn_conv_w', l)].transpose(1, 0, 2).reshape(FFN_CONV_W, 2, d_ff).transpose(1, 0, 2)

    tm = min(1024, s)
    tkw = min(2048, s)
    d_up = 2 * d_ff // N_CHIPS
    f_cb = ffn_conv_b.reshape(depth, 2, 1, d_ff)

    h = x.reshape(s, d)
    saved = []
    for i in range(depth):
        j = i // 2
        sv = {'h_in': h}
        if i == 0:
            hn, got = _rms_fwd(h, norm_mix_g[:1], side=gather_side(mixer_keys(0)))
            put(mixer_keys(0), got)
        sv['hn'] = hn
        up_keys = [('ffn_w_up', i), ('ffn_conv_w', i)]
        if i % 2 == 0:
            xg = _mm("rg_in", 'nn', act(hn), wcol('rg_w_in', j), out_parts=2, tm=tm, tn=512, tk=d)
            xg2 = _split2(xg)
            wa, wx = rg_w_a[j].astype(BF16), rg_w_x[j].astype(BF16)
            ba, bx = rg_b_a[j].reshape(1, d), rg_b_x[j].reshape(1, d)
            (xr, hs, y), got = _rg_fwd(xg2, rg_cw(j), rg_conv_b[j:j + 1], wa, ba, wx, bx, rg_lambda[j:j + 1],
                                       side=gather_side(up_keys))
            put(up_keys, got)
            sv.update(xg2=xg2, xr=xr, hs=hs, y=y, wa=wa, wx=wx, ba=ba, bx=bx)
            h, hn2 = _mm("rg_out", 'nn', act(y), wrow('rg_w_out', j), res=act(h), norm_g=norm_ffn_g[i:i + 1], tm=tm, tn=d, tk=d)
        else:
            u = _mm("s5_in", 'nn', act(hn), wrow('s5_w_in', j), tm=tm, tn=d, tk=d).arr[0, 0]
            bt_re, bt_im = s5_b_re[j].transpose(2, 0, 1), s5_b_im[j].transpose(2, 0, 1)
            ldt = s5_log_dt[j].reshape(n_grp, 1)
            tab_r, tab_i, rtab_r, rtab_i, bbr, bbi = _s5_tables3(s5_a_re[j], s5_a_im[j], ldt, bt_re, bt_im, seg=s5_ts // SUBLANES)
            nn_ = n_grp * n_state
            tab_r, tab_i, rtab_r, rtab_i = (t.reshape(5, SUBLANES, nn_) for t in (tab_r, tab_i, rtab_r, rtab_i))
            prm = dict(bp_r=_s5_pack_b(bbr).astype(BF16), bp_i=_s5_pack_b(bbi).astype(BF16),
                       cp_r=_s5_pack_c(s5_c_re[j]).astype(BF16), cp_i=_s5_pack_c(s5_c_im[j]).astype(BF16), dvec=s5_dv(j))
            (hr, hi, ypre, gy), got = _s5_fwd3(u, s5_perm, s5_perm.T, tab_r, tab_i, ts=s5_ts, side=gather_side(up_keys), **prm)
            sv.update(rtab_r=rtab_r, rtab_i=rtab_i)
            put(up_keys, got)
            gl2, o = _glu_mm(gy, gath[('s5_w_glu', j)])
            sv.update(u=u, prm=prm, hr=hr, hi=hi, ypre=ypre, gy=gy, gl2=gl2, o=o, bt_re=bt_re, bt_im=bt_im, ldt=ldt)
            h, hn2 = _mm("s5_out", 'nn', act(o), wrow('s5_w_out', j), res=act(h), norm_g=norm_ffn_g[i:i + 1], tm=tm, tn=d, tk=d)
        h = h.arr[0, 0]
        sv['h_mid'] = h
        next_keys = [('ffn_w_down', i)] + mixer_keys(i + 1)
        (up2, c2, a_ffn), got = _ffn_up_act(hn2, gath[('ffn_w_up', i)], f_cw(i), f_cb[i], side=gather_side(next_keys))
        put(next_keys, got)
        sv.update(hn2=hn2, up2=up2, c2=c2, act=a_ffn)
        if i + 1 < depth:
            h, hn = _mm("ffn_down", 'nn', act(a_ffn), wrow('ffn_w_down', i), res=act(h), norm_g=norm_mix_g[i + 1:i + 2], tm=tm, tn=d,
                        tk=d_ff // 2)
        else:
            h = _mm("ffn_down", 'nn', act(a_ffn), wrow('ffn_w_down', i), res=act(h), tm=tm, tn=d, tk=d_ff // 2)
        h = h.arr[0, 0]
        saved.append(sv)

    loss_row, dh, dg_final = _loss_and_grad(h, norm_final_g.reshape(1, d), loss_target.reshape(s, d))
    loss = lax.psum(loss_row[0, 0], ("x", "y", "c"))

    gl_ = {n: [None] * w[n].shape[0] for n in PARAM_NAMES if n != 'norm_final_g'}
    recvd = {}

    def scatter_side(keys):
        return _scatter_side([gl_[n][l].reshape((N_CHIPS,) + w[n].shape[1:]) for n, l in keys])

    def record(keys, arrs):
        for k, a in zip(keys, arrs):
            recvd[k] = a

    pending = None
    for i in reversed(range(depth)):
        j = i // 2
        sv = saved[i]
        gl_['ffn_w_down'][i] = _mm("ffn_down_dw", 'tn', act(sv['act']), act(dh), out_dtype=BF16, tm=d_ff // N_CHIPS, tn=d, tk=tkw).arr
        (dup2, dcw2, dcb2), got = _ffn_bwd_fused(dh, gath[('ffn_w_down', i)].reshape(d_ff, d), sv['up2'], sv['c2'], f_cw(i),
                                                 side=scatter_side(pending) if pending else None)
        if pending:
            record(pending, got)
        gl_['ffn_conv_w'][i] = dcw2.transpose(1, 0, 2).reshape(FFN_CONV_W, 2 * d_ff)
        gl_['ffn_conv_b'][i] = dcb2.reshape(2 * d_ff)
        dup = Mat(dup2[:, None], 0, 'c')
        gl_['ffn_w_up'][i] = _mm("ffn_up_dw", 'tn', act(sv['hn2']), dup, out_parts=N_CHIPS, out_dtype=BF16, tm=d, tn=d_up, tk=tkw).arr
        (dh, dg), _ = _mm_rms_bwd("ffn_up_dx", dup, wcol('ffn_w_up', i), sv['h_mid'], norm_ffn_g[i:i + 1], dh, tm=tm, tk=d_up)
        gl_['norm_ffn_g'][i] = dg[0]
        ffn_keys = [('ffn_w_up', i), ('ffn_w_down', i)]
        if i % 2 == 0:
            dy = _mm("rg_out_dx", 'nt', act(dh), wrow('rg_w_out', j), tm=tm, tn=d, tk=d).arr[0, 0]
            gl_['rg_w_out'][j] = _mm("rg_out_dw", 'tn', act(sv['y']), act(dh), out_dtype=BF16, tm=d, tn=d, tk=tkw).arr
            (dxg2, dcw, dcb, dwa, dba, dwx, dbx, dlam), got = _rg_bwd(
                dy, sv['xg2'], sv['xr'], sv['hs'], rg_cw(j), sv['wa'], sv['ba'], sv['wx'], sv['bx'], rg_lambda[j:j + 1],
                side=scatter_side(ffn_keys))
            record(ffn_keys, got)
            gl_['rg_conv_w'][j] = dcw
            gl_['rg_conv_b'][j] = dcb[0]
            gl_['rg_w_a'][j], gl_['rg_w_x'][j] = dwa, dwx
            gl_['rg_b_a'][j], gl_['rg_b_x'][j] = dba.reshape(rg_b_a.shape[1:]), dbx.reshape(rg_b_x.shape[1:])
            gl_['rg_lambda'][j] = dlam[0]
            dxg = Mat(dxg2[:, None], 0, 'c')
            gl_['rg_w_in'][j] = _mm("rg_in_dw", 'tn', act(sv['hn']), dxg, out_parts=N_CHIPS, out_dtype=BF16, tm=d, tn=512, tk=tkw).arr
            mix_dx = ("rg_in_dx", dxg, wcol('rg_w_in', j), 512)
            pending = [('rg_w_in', j), ('rg_w_out', j)]
        else:
            gl_['s5_w_out'][j] = _mm("s5_out_dw", 'tn', act(sv['o']), act(dh), out_dtype=BF16, tm=d, tn=d, tk=tkw).arr
            dgl2 = _glu_bwd_mm(dh, gath[('s5_w_out', j)].reshape(d, d), sv['gl2'])
            dgl = Mat(dgl2[:, None], 0, 'c')
            gl_['s5_w_glu'][j] = _mm("s5_glu_dw", 'tn', act(sv['gy']), dgl, out_parts=N_CHIPS, out_dtype=BF16, tm=d, tn=512, tk=tkw).arr
            dgy = _mm("s5_glu_dx", 'nt', dgl, wcol('s5_w_glu', j), tm=tm, tn=d, tk=512).arr[0, 0]
            (du, dar, dai, dbpr, dbpi, dcpr, dcpi, dd), got = _s5_bwd3(
                dgy, sv['ypre'], sv['u'], sv['hr'], sv['hi'], s5_perm, s5_perm.T, sv['rtab_r'], sv['rtab_i'], ts=s5_ts,
                side=scatter_side(ffn_keys), **sv['prm'])
            record(ffn_keys, got)
            gl_['s5_d'][j] = dd[0]
            gl_['s5_c_re'][j] = _s5_unpack_c(dcpr, gc, n_state)
            gl_['s5_c_im'][j] = -_s5_unpack_c(dcpi, gc, n_state)
            d_are, d_aim, d_ldt, d_btr, d_bti = _s5_params_bwd(
                s5_a_re[j], s5_a_im[j], sv['ldt'], sv['bt_re'], sv['bt_im'], dar.reshape(n_grp, n_state), dai.reshape(n_grp, n_state),
                _s5_unpack_b(dbpr, gc, n_state), _s5_unpack_b(dbpi, gc, n_state))
            gl_['s5_a_re'][j], gl_['s5_a_im'][j], gl_['s5_log_dt'][j] = d_are, d_aim, d_ldt[:, 0]
            gl_['s5_b_re'][j], gl_['s5_b_im'][j] = d_btr.transpose(1, 2, 0), d_bti.transpose(1, 2, 0)
            dum = act(du)
            gl_['s5_w_in'][j] = _mm("s5_in_dw", 'tn', act(sv['hn']), dum, out_dtype=BF16, tm=d, tn=d, tk=tkw).arr
            mix_dx = ("s5_in_dx", dum, wrow('s5_w_in', j), d)
            pending = [('s5_w_in', j), ('s5_w_glu', j), ('s5_w_out', j)]
        (dh, dg), got = _mm_rms_bwd(mix_dx[0], mix_dx[1], mix_dx[2], sv['h_in'], norm_mix_g[i:i + 1], dh, tm=tm, tk=mix_dx[3],
                                    side=scatter_side(pending) if i == 0 else None)
        if i == 0:
            record(pending, got)
        gl_['norm_mix_g'][i] = dg[0]
    grad_x = dh.reshape(x.shape)

    order = sorted(BIG, key=lambda n: -math.prod(w[n].shape))
    chip_sums, theirs, prev = {}, {}, None
    for n in order:
        cols = w[n].shape[-1]
        cs, got = _sum_parts([recvd[(n, l)].reshape(N_CHIPS, -1, cols) for l in range(w[n].shape[0])],
                             side=_sibling_side([chip_sums[prev]]) if prev else None)
        chip_sums[n] = cs.reshape(-1, cols)
        if prev:
            theirs[prev] = got[0]
        prev = n
    theirs[prev] = _run_side("swap_last", _sibling_side([chip_sums[prev]]))[0]
    results = {}
    for n in BIG:
        cols = w[n].shape[-1]
        (delta, new_m, new_v, grad), _ = _adamw(w[n].reshape(-1, cols), [chip_sums[n], theirs[n]], mom[n].reshape(-1, cols),
                                                vel[n].reshape(-1, cols))
        results[n] = [o.reshape(w[n].shape) for o in (grad, delta, new_m, new_v)]

    small = REPLICATED + SMALL_SHARDED
    local = [dg_final.reshape(d) if n == 'norm_final_g' else jnp.stack(gl_[n]) for n in small]
    summed = _unpack(_allreduce_small(_pack(local)), local)
    me = 2 * lax.axis_index("x") + lax.axis_index("y")
    for n, g in zip(small, summed):
        if n in SMALL_SHARDED:
            g = lax.dynamic_slice_in_dim(g, me * w[n].shape[-1], w[n].shape[-1], axis=g.ndim - 1)
        view = (-1, w[n].shape[-1])
        (delta, new_m, new_v), _ = _adamw(w[n].reshape(view), [g.reshape(view)], mom[n].reshape(view), vel[n].reshape(view))
        results[n] = [g] + [o.reshape(w[n].shape) for o in (delta, new_m, new_v)]

    return (loss, grad_x, *[results[n][0] for n in PARAM_NAMES], *[results[n][1] for n in PARAM_NAMES],
            *[results[n][2] for n in PARAM_NAMES], *[results[n][3] for n in PARAM_NAMES])
```

```python
import math

import jax
import jax.numpy as jnp
from jax import lax
from jax.experimental import pallas as pl
from jax.experimental.pallas import tpu as pltpu

F32 = jnp.float32
BF16 = jnp.bfloat16
MESH = pl.DeviceIdType.MESH

NORM_EPS = 1e-6
RG_HEADS = 8
RG_CONV_W = 4
RG_C = 8.0
S5_GC = 16
S5_P = 64
S5_GROUPS_PER_BLOCK = 8
FFN_CONV_W = 3
N_CHIPS = 4
ADAM_LR, ADAM_B1, ADAM_B2, ADAM_EPS, ADAM_WD, ADAM_STEP = 0.001, 0.9, 0.999, 1e-08, 0.01, 10
VMEM_LIMIT_BYTES = 56 * 1024 * 1024
SUBLANES = 8
LANES = 128

PARAM_NAMES = ['norm_mix_g', 'norm_ffn_g', 'norm_final_g', 'rg_w_in', 'rg_conv_w', 'rg_conv_b', 'rg_w_a', 'rg_b_a', 'rg_w_x',
               'rg_b_x', 'rg_lambda', 'rg_w_out', 's5_w_in', 's5_a_re', 's5_a_im', 's5_log_dt', 's5_b_re', 's5_b_im', 's5_c_re',
               's5_c_im', 's5_d', 's5_w_glu', 's5_w_out', 'ffn_w_up', 'ffn_conv_w', 'ffn_conv_b', 'ffn_w_down']
SHARDED = ['rg_w_in', 'rg_conv_w', 'rg_w_out', 's5_w_in', 's5_d', 's5_w_glu', 's5_w_out', 'ffn_w_up', 'ffn_conv_w', 'ffn_w_down']
BIG = ['rg_w_in', 'rg_w_out', 's5_w_in', 's5_w_glu', 's5_w_out', 'ffn_w_up', 'ffn_w_down']
SMALL_SHARDED = ['rg_conv_w', 's5_d', 'ffn_conv_w']
MIXER_SHARDED = [['rg_w_in', 'rg_conv_w', 'rg_w_out'], ['s5_w_in', 's5_d', 's5_w_glu', 's5_w_out']]
REPLICATED = [n for n in PARAM_NAMES if n not in SHARDED]


def _cparams():
    return pltpu.CompilerParams(vmem_limit_bytes=VMEM_LIMIT_BYTES)


_GELU_C = math.sqrt(2.0 / math.pi)
_GELU_K = 0.044715


def _gelu(x):
    return 0.5 * x * (1.0 + jnp.tanh(_GELU_C * (x + _GELU_K * x * x * x)))


def _gelu_and_grad(x):
    t = jnp.tanh(_GELU_C * (x + _GELU_K * x * x * x))
    g = 0.5 * x * (1.0 + t)
    dg = 0.5 * (1.0 + t) + 0.5 * x * (1.0 - t * t) * (_GELU_C * (1.0 + 3.0 * _GELU_K * x * x))
    return g, dg


def _sigmoid(x):
    return jax.nn.sigmoid(x)


def _neg_expm1(x):
    series = -(x * (1.0 + x * (0.5 + x * (1.0 / 6 + x * (1.0 / 24 + x * (1.0 / 120 + x * (1.0 / 720)))))))
    return jnp.where(x > -0.25, series, 1.0 - jnp.exp(x))


def _softplus(z):
    return jnp.maximum(z, 0.0) + jnp.log1p(jnp.exp(-jnp.abs(z)))


def _rows(shape):
    return lax.broadcasted_iota(jnp.int32, shape, 0)


def _shift_down(x, halo, k):
    ext = jnp.concatenate([halo, x], axis=0)
    return pltpu.roll(ext, k, 0)[SUBLANES:]


def _shift_up(x, halo, k):
    ext = jnp.concatenate([x, halo], axis=0)
    n = ext.shape[0]
    return pltpu.roll(ext, n - k, 0)[:x.shape[0]]


RG_LANE_CHUNK = 512


def _real_slab_scan(a_ref, b_ref, out_ref, carry_ref, reverse):
    t, c = a_ref.shape
    nsl = t // SUBLANES
    lc = min(RG_LANE_CHUNK, c)
    row8 = _rows((SUBLANES, lc))
    for q in range(c // lc):
        sl = slice(q * lc, (q + 1) * lc)

        def slab(jj, carry, sl=sl):
            j = nsl - 1 - jj if reverse else jj
            r0 = pl.multiple_of(j * SUBLANES, SUBLANES)
            a, b = a_ref[pl.ds(r0, SUBLANES), sl], b_ref[pl.ds(r0, SUBLANES), sl]
            for k in range(3):
                sh = 1 << k
                keep = row8 < SUBLANES - sh if reverse else row8 >= sh
                amount = SUBLANES - sh if reverse else sh
                b = a * jnp.where(keep, pltpu.roll(b, amount, 0), 0.0) + b
                a = a * jnp.where(keep, pltpu.roll(a, amount, 0), 1.0)
            x = b + a * jnp.broadcast_to(carry, b.shape)
            out_ref[pl.ds(r0, SUBLANES), sl] = x
            return x[:1, :] if reverse else x[SUBLANES - 1:, :]

        carry_ref[:, sl] = lax.fori_loop(0, nsl, slab, carry_ref[:, sl], unroll=2)


class Mat:
    def __init__(self, arr, l=0, split='c'):
        assert arr.ndim == 4
        self.arr, self.l, self.split = arr, l, split
        p, _, r, c = arr.shape
        self.shape = (r, c * p) if split == 'c' else (r * p, c)

    def spec(self, tr, tc, rc):
        p, _, r, c = self.arr.shape
        l = self.l
        assert r % tr == 0 and c % tc == 0, (self.arr.shape, tr, tc)
        if self.split == 'c':
            per = c // tc
            return pl.BlockSpec((None, None, tr, tc), lambda i, j, k: (rc(i, j, k)[1] // per, l, rc(i, j, k)[0], rc(i, j, k)[1] % per))
        per = r // tr
        return pl.BlockSpec((None, None, tr, tc), lambda i, j, k: (rc(i, j, k)[0] // per, l, rc(i, j, k)[0] % per, rc(i, j, k)[1]))


def act(x, parts=1):
    s, c = x.shape
    return Mat(x.reshape(s, parts, c // parts).transpose(1, 0, 2)[:, None] if parts > 1 else x[None, None])


def _mm(name, mode, a, b, *, out_parts=1, out_split='c', out_dtype=F32, res=None, norm_g=None, tm=512, tn=512, tk=512,
        side=None):
    if mode == 'nn':
        (m, kk), (kb, n) = a.shape, b.shape
    elif mode == 'nt':
        (m, kk), (n, kb) = a.shape, b.shape
    else:
        (kk, m), (kb, n) = a.shape, b.shape
    assert kk == kb, (name, a.shape, b.shape)
    tm, tn, tk = min(tm, m), min(tn, n), min(tk, kk)
    assert m % tm == 0 and n % tn == 0 and kk % tk == 0, (name, m, n, kk, tm, tn, tk)
    nk = kk // tk
    if mode == 'nn':
        a_spec = a.spec(tm, tk, lambda i, j, k: (i, k))
        b_spec = b.spec(tk, tn, lambda i, j, k: (k, j))
        dims = (((1,), (0,)), ((), ()))
    elif mode == 'nt':
        a_spec = a.spec(tm, tk, lambda i, j, k: (i, k))
        b_spec = b.spec(tn, tk, lambda i, j, k: (j, k))
        dims = (((1,), (1,)), ((), ()))
    else:
        a_spec = a.spec(tk, tm, lambda i, j, k: (k, i))
        b_spec = b.spec(tk, tn, lambda i, j, k: (k, j))
        dims = (((0,), (0,)), ((), ()))
    if out_split == 'c':
        out_arr = jax.ShapeDtypeStruct((out_parts, 1, m, n // out_parts), out_dtype)
    else:
        out_arr = jax.ShapeDtypeStruct((out_parts, 1, m // out_parts, n), out_dtype)
    out_mat = Mat(out_arr, 0, out_split)
    o_spec = out_mat.spec(tm, tn, lambda i, j, k: (i, j))
    has_res = res is not None
    has_norm = norm_g is not None
    assert not has_norm or tn == n, (name, tn, n)

    def body(*refs):
        a_ref, b_ref = refs[:2]
        extra = list(refs[2:2 + has_res + has_norm])
        r_ref = extra.pop(0) if has_res else None
        g_ref = extra.pop(0) if has_norm else None
        o_ref = refs[2 + has_res + has_norm]
        prod = lax.dot_general(a_ref[...].astype(BF16), b_ref[...].astype(BF16), dims, preferred_element_type=F32)

        def finish(acc):
            if has_res:
                acc = acc + r_ref[...]
            o_ref[...] = acc.astype(out_dtype)
            if has_norm:
                var = jnp.mean(acc * acc, axis=-1, keepdims=True)
                refs[3 + has_res + has_norm][...] = (acc * lax.rsqrt(var + NORM_EPS) * g_ref[...]).astype(BF16)

        if nk == 1:
            finish(prod)
        else:
            acc_ref = refs[-1]
            k = pl.program_id(2)

            @pl.when(k == 0)
            def _():
                acc_ref[...] = prod

            @pl.when(k > 0)
            def _():
                acc_ref[...] += prod

            @pl.when(k == nk - 1)
            def _():
                finish(acc_ref[...])

    in_specs = [a_spec, b_spec]
    args = [a.arr, b.arr]
    if has_res:
        in_specs.append(res.spec(tm, tn, lambda i, j, k: (i, j)))
        args.append(res.arr)
    out_specs, out_shape = [o_spec], [out_arr]
    if has_norm:
        in_specs.append(pl.BlockSpec((1, n), lambda i, j, k: (0, 0)))
        args.append(norm_g)
        out_specs.append(pl.BlockSpec((tm, n), lambda i, j, k: (i, 0)))
        out_shape.append(jax.ShapeDtypeStruct((m, n), BF16))
    gi, gj = m // tm, n // tn
    outs, got = _call_with_side(
        body, side, lambda: (pl.program_id(0) == 0) & (pl.program_id(1) == 0) & (pl.program_id(2) == 0),
        lambda: (pl.program_id(0) == gi - 1) & (pl.program_id(1) == gj - 1) & (pl.program_id(2) == nk - 1),
        name=name, grid=(gi, gj, nk), in_specs=in_specs, out_specs=out_specs, out_shape=out_shape,
        scratch_shapes=[pltpu.VMEM((tm, tn), F32)] if nk > 1 else [], args=tuple(args))
    result = (Mat(outs[0], 0, out_split), outs[1]) if has_norm else Mat(outs[0], 0, out_split)
    return result if side is None else (result, got)


def _rms_fwd(h, g, ts=512, side=None):
    s, d = h.shape
    ts = min(ts, s)
    nt = s // ts

    def body(h_ref, g_ref, o_ref):
        x = h_ref[...]
        var = jnp.mean(x * x, axis=-1, keepdims=True)
        o_ref[...] = (x * lax.rsqrt(var + NORM_EPS) * g_ref[...]).astype(BF16)

    outs, got = _call_with_side(
        body, side, lambda: pl.program_id(0) == 0, lambda: pl.program_id(0) == nt - 1,
        name="rms_fwd", grid=(nt,),
        in_specs=[pl.BlockSpec((ts, d), lambda i: (i, 0)), pl.BlockSpec((1, d), lambda i: (0, 0))],
        out_specs=[pl.BlockSpec((ts, d), lambda i: (i, 0))], out_shape=[jax.ShapeDtypeStruct((s, d), BF16)],
        scratch_shapes=[], args=(h, g))
    return outs[0], got


def _loss_and_grad(h, g, tgt, ts=512):
    s, d = h.shape
    ts = min(ts, s)

    def body(h_ref, g_ref, t_ref, loss_ref, dh_ref, dg_ref):
        i = pl.program_id(0)
        x = h_ref[...]
        gv = g_ref[...]
        rstd = lax.rsqrt(jnp.mean(x * x, axis=-1, keepdims=True) + NORM_EPS)
        xhat = x * rstd
        err = xhat * gv - t_ref[...]
        dy = err * (1.0 / d)
        dxh = dy * gv
        dh_ref[...] = rstd * (dxh - xhat * jnp.mean(dxh * xhat, axis=-1, keepdims=True))
        part = jnp.sum(dy * xhat, axis=0, keepdims=True)
        lpart = jnp.broadcast_to(jnp.sum(jnp.sum(err * err, axis=0, keepdims=True), axis=1, keepdims=True) * (0.5 / d), (1, LANES))

        @pl.when(i == 0)
        def _():
            dg_ref[...] = part
            loss_ref[...] = lpart

        @pl.when(i > 0)
        def _():
            dg_ref[...] += part
            loss_ref[...] += lpart

    row = pl.BlockSpec((ts, d), lambda i: (i, 0))
    vec = pl.BlockSpec((1, d), lambda i: (0, 0))
    return pl.pallas_call(
        body, name="loss_and_grad", grid=(s // ts,), in_specs=[row, vec, row],
        out_specs=[pl.BlockSpec((1, LANES), lambda i: (0, 0)), row, vec],
        out_shape=[jax.ShapeDtypeStruct((1, LANES), F32), jax.ShapeDtypeStruct((s, d), F32), jax.ShapeDtypeStruct((1, d), F32)],
        compiler_params=_cparams(),
    )(h, g, tgt)


def _mm_rms_bwd(name, a, b, h, g, dh_in, *, tm, tk, side=None):
    (m, kk), (n, kb) = a.shape, b.shape
    assert kk == kb and h.shape == (m, n), (name, a.shape, b.shape, h.shape)
    tm, tk = min(tm, m), min(tk, kk)
    nk = kk // tk
    dims = (((1,), (1,)), ((), ()))

    def body(a_ref, b_ref, h_ref, g_ref, dhin_ref, dh_ref, dg_ref, *acc):
        i, k = pl.program_id(0), pl.program_id(2)
        prod = lax.dot_general(a_ref[...].astype(BF16), b_ref[...].astype(BF16), dims, preferred_element_type=F32)

        def finish(dhn):
            x = h_ref[...]
            rstd = lax.rsqrt(jnp.mean(x * x, axis=-1, keepdims=True) + NORM_EPS)
            xhat = x * rstd
            dxh = dhn * g_ref[...]
            dh_ref[...] = dhin_ref[...] + rstd * (dxh - xhat * jnp.mean(dxh * xhat, axis=-1, keepdims=True))
            part = jnp.sum(dhn * xhat, axis=0, keepdims=True)

            @pl.when(i == 0)
            def _():
                dg_ref[...] = part

            @pl.when(i > 0)
            def _():
                dg_ref[...] += part

        if nk == 1:
            finish(prod)
        else:
            acc_ref = acc[0]

            @pl.when(k == 0)
            def _():
                acc_ref[...] = prod

            @pl.when(k > 0)
            def _():
                acc_ref[...] += prod

            @pl.when(k == nk - 1)
            def _():
                finish(acc_ref[...])

    row = pl.BlockSpec((tm, n), lambda i, j, k: (i, 0))
    vec = pl.BlockSpec((1, n), lambda i, j, k: (0, 0))
    ni = m // tm
    return _call_with_side(
        body, side, lambda: (pl.program_id(0) == 0) & (pl.program_id(2) == 0),
        lambda: (pl.program_id(0) == ni - 1) & (pl.program_id(2) == nk - 1),
        name=name, grid=(ni, 1, nk),
        in_specs=[a.spec(tm, tk, lambda i, j, k: (i, k)), b.spec(n, tk, lambda i, j, k: (0, k)), row, vec, row],
        out_specs=[row, vec], out_shape=[jax.ShapeDtypeStruct((m, n), F32), jax.ShapeDtypeStruct((1, n), F32)],
        scratch_shapes=[pltpu.VMEM((tm, n), F32)] if nk > 1 else [], args=(a.arr, b.arr, h, g, dh_in))


def _ffn_up_act(hn2, w_up4, conv_w2, conv_b2, ts=1024, tn=512, sub=1024, side=None):
    s, d = hn2.shape
    p, _, wc = w_up4.shape
    f = p * wc // 2
    ts, tn = min(ts, s), min(tn, wc)
    sub = min(sub, ts)
    per = wc // tn
    kw = FFN_CONV_W
    g0, g1 = f // tn, s // ts

    def body(hn_ref, w1_ref, w2_ref, cw_ref, cb_ref, up_ref, c_ref, act_ref, carry_ref):
        @pl.when(pl.program_id(1) == 0)
        def _():
            carry_ref[...] = jnp.zeros_like(carry_ref)

        for q in range(ts // sub):
            rows = slice(q * sub, (q + 1) * sub)
            hn = hn_ref[rows, :]
            cs = []
            for h, w_ref in enumerate((w1_ref, w2_ref)):
                x = jnp.dot(hn, w_ref[...], preferred_element_type=F32)
                up_ref[h, rows, :] = x
                halo = carry_ref[h]
                c = cb_ref[h] + cw_ref[h, kw - 1:kw, :] * x
                for sft in range(1, kw):
                    c = c + cw_ref[h, kw - 1 - sft:kw - sft, :] * _shift_down(x, halo, sft)
                carry_ref[h] = x[sub - SUBLANES:, :]
                c_ref[h, rows, :] = c
                cs.append(c)
            act_ref[rows, :] = (_gelu(cs[0]) * cs[1]).astype(BF16)

    outs, side_outs = _call_with_side(
        body, side, lambda: (pl.program_id(0) == 0) & (pl.program_id(1) == 0),
        lambda: (pl.program_id(0) == g0 - 1) & (pl.program_id(1) == g1 - 1),
        name="ffn_up_act", grid=(g0, g1),
        in_specs=[pl.BlockSpec((ts, d), lambda j, i: (i, 0)),
                  pl.BlockSpec((None, d, tn), lambda j, i: (j // per, 0, j % per)),
                  pl.BlockSpec((None, d, tn), lambda j, i: (p // 2 + j // per, 0, j % per)),
                  pl.BlockSpec((2, kw, tn), lambda j, i: (0, 0, j)),
                  pl.BlockSpec((2, 1, tn), lambda j, i: (0, 0, j))],
        out_specs=[pl.BlockSpec((2, ts, tn), lambda j, i: (0, i, j)), pl.BlockSpec((2, ts, tn), lambda j, i: (0, i, j)),
                   pl.BlockSpec((ts, tn), lambda j, i: (i, j))],
        out_shape=[jax.ShapeDtypeStruct((2, s, f), F32), jax.ShapeDtypeStruct((2, s, f), F32), jax.ShapeDtypeStruct((s, f), BF16)],
        scratch_shapes=[pltpu.VMEM((2, SUBLANES, tn), F32)], args=(hn2, w_up4, w_up4, conv_w2, conv_b2))
    return outs, side_outs


def _ffn_bwd_fused(dh, w_down, up2, c2, conv_w2, ts=1024, tn=512, side=None):
    s, d = dh.shape
    _, _, f = up2.shape
    ts, tn = min(ts, s), min(tn, f)
    kw = FFN_CONV_W
    nt = s // ts
    hb = ts // SUBLANES
    g0 = f // tn
    nt_dims = (((1,), (1,)), ((), ()))

    def body(dh_ref, wd_ref, up_ref, c_ref, w_ref, dup_ref, dw_ref, db_ref, carry_ref):
        i = pl.program_id(1)
        first_step = i == 0

        @pl.when(first_step)
        def _():
            carry_ref[...] = jnp.zeros_like(carry_ref)

        da = lax.dot_general(dh_ref[...].astype(BF16), wd_ref[...], nt_dims, preferred_element_type=F32)
        g1, dg1 = _gelu_and_grad(c_ref[0])
        dcs = [da * c_ref[1] * dg1, da * g1]
        for h in range(2):
            dc = dcs[h]
            after = carry_ref[h]
            ups = [dc] + [_shift_up(dc, after, sft) for sft in range(1, kw)]
            dup = w_ref[h, kw - 1:kw, :] * dc
            for sft in range(1, kw):
                dup = dup + w_ref[h, kw - 1 - sft:kw - sft, :] * ups[sft]
            carry_ref[h] = dc[:SUBLANES]
            dup_ref[h] = dup.astype(BF16)
            dbp = jnp.sum(dc, axis=0, keepdims=True)
            x = up_ref[h]
            dwp = [jnp.sum(ups[kw - 1 - k] * x, axis=0, keepdims=True) for k in range(kw)]

            @pl.when(first_step)
            def _():
                db_ref[h] = dbp
                for k in range(kw):
                    dw_ref[h, k:k + 1, :] = dwp[k]

            @pl.when(i > 0)
            def _():
                db_ref[h] += dbp
                for k in range(kw):
                    dw_ref[h, k:k + 1, :] += dwp[k]

    rev = lambda i: nt - 1 - i
    return _call_with_side(
        body, side, lambda: (pl.program_id(0) == 0) & (pl.program_id(1) == 0),
        lambda: (pl.program_id(0) == g0 - 1) & (pl.program_id(1) == nt - 1),
        name="ffn_bwd", grid=(g0, nt),
        in_specs=[pl.BlockSpec((ts, d), lambda j, i: (rev(i), 0)),
                  pl.BlockSpec((tn, d), lambda j, i: (j, 0)),
                  pl.BlockSpec((2, ts, tn), lambda j, i: (0, rev(i), j)),
                  pl.BlockSpec((2, ts, tn), lambda j, i: (0, rev(i), j)),
                  pl.BlockSpec((2, kw, tn), lambda j, i: (0, 0, j))],
        out_specs=[pl.BlockSpec((2, ts, tn), lambda j, i: (0, rev(i), j)),
                   pl.BlockSpec((2, kw, tn), lambda j, i: (0, 0, j)),
                   pl.BlockSpec((2, 1, tn), lambda j, i: (0, 0, j))],
        out_shape=[jax.ShapeDtypeStruct((2, s, f), BF16), jax.ShapeDtypeStruct((2, kw, f), F32),
                   jax.ShapeDtypeStruct((2, 1, f), F32)],
        scratch_shapes=[pltpu.VMEM((2, SUBLANES, tn), F32)], args=(dh, w_down, up2, c2, conv_w2))


def _rg_gates(xr, wa_ref, ba_ref, wx_ref, bx_ref, lam_ref):
    bw = wa_ref.shape[-1]
    xb = xr.astype(BF16)
    za = jnp.concatenate([jnp.dot(xb[:, h * bw:(h + 1) * bw], wa_ref[h], preferred_element_type=F32)
                          for h in range(RG_HEADS)], axis=1) + ba_ref[...]
    zx = jnp.concatenate([jnp.dot(xb[:, h * bw:(h + 1) * bw], wx_ref[h], preferred_element_type=F32)
                          for h in range(RG_HEADS)], axis=1) + bx_ref[...]
    r, ig = _sigmoid(za), _sigmoid(zx)
    sp = _softplus(-lam_ref[...])
    la = -RG_C * r * sp
    a = jnp.exp(la)
    mult = jnp.sqrt(_neg_expm1(2.0 * la))
    return xb, r, ig, sp, a, mult


def _rg_fwd(xg2, conv_w, conv_b, w_a, b_a, w_x, b_x, lam, ts=256, side=None):
    _, s, c = xg2.shape
    ts = min(ts, s)
    kw = RG_CONV_W
    hb = ts // SUBLANES

    def body(xg_ref, halo_ref, cw_ref, cb_ref, wa_ref, ba_ref, wx_ref, bx_ref, lam_ref, xr_ref, hs_ref, y_ref, carry_ref,
             a_scr, b_scr):
        i = pl.program_id(0)

        @pl.when(i == 0)
        def _():
            carry_ref[...] = jnp.zeros_like(carry_ref)

        xp = xg_ref[0]
        halo = jnp.where(i == 0, 0.0, halo_ref[...])
        xr = cb_ref[...] + cw_ref[kw - 1:kw, :] * xp
        for sft in range(1, kw):
            xr = xr + cw_ref[kw - 1 - sft:kw - sft, :] * _shift_down(xp, halo, sft)
        _, r, ig, sp, a, mult = _rg_gates(xr, wa_ref, ba_ref, wx_ref, bx_ref, lam_ref)
        a_scr[...] = a
        b_scr[...] = mult * (ig * xr)
        _real_slab_scan(a_scr, b_scr, hs_ref, carry_ref, reverse=False)
        xr_ref[...] = xr
        y_ref[...] = (hs_ref[...] * _gelu(xg_ref[1])).astype(BF16)

    full = lambda shape: pl.BlockSpec(shape, lambda i: (0,) * len(shape))
    row_spec = pl.BlockSpec((ts, c), lambda i: (i, 0))
    nt = s // ts
    return _call_with_side(
        body, side, lambda: pl.program_id(0) == 0, lambda: pl.program_id(0) == nt - 1,
        name="rg_fwd", grid=(nt,),
        in_specs=[pl.BlockSpec((2, ts, c), lambda i: (0, i, 0)),
                  pl.BlockSpec((None, SUBLANES, c), lambda i: (0, jnp.maximum(i * hb - 1, 0), 0)),
                  full(conv_w.shape), full(conv_b.shape), full(w_a.shape), full(b_a.shape), full(w_x.shape), full(b_x.shape),
                  full(lam.shape)],
        out_specs=[row_spec, row_spec, row_spec],
        out_shape=[jax.ShapeDtypeStruct((s, c), F32), jax.ShapeDtypeStruct((s, c), F32), jax.ShapeDtypeStruct((s, c), BF16)],
        scratch_shapes=[pltpu.VMEM((1, c), F32), pltpu.VMEM((ts, c), F32), pltpu.VMEM((ts, c), F32)],
        args=(xg2, xg2, conv_w, conv_b, w_a, b_a, w_x, b_x, lam))


def _rg_bwd(dy, xg2, xr, hs, conv_w, w_a, b_a, w_x, b_x, lam, ts=256, side=None):
    _, s, c = xg2.shape
    ts = min(ts, s)
    nt = s // ts
    kw = RG_CONV_W
    hb = ts // SUBLANES
    bw = c // RG_HEADS
    tn_dims = (((0,), (0,)), ((), ()))
    nt_dims = (((1,), (1,)), ((), ()))

    def body(dy_ref, xg_ref, xph_ref, xr_ref, hs_ref, hsh_ref, cw_ref, wa_ref, ba_ref, wx_ref, bx_ref, lam_ref,
             dxg_ref, dcw_ref, dcb_ref, dwa_ref, dba_ref, dwx_ref, dbx_ref, dlam_ref,
             lam_carry, a_carry, dxr_carry, dsp_acc, a_scr, b_scr):
        i = pl.program_id(0)
        first_step = i == 0
        time_first = i == nt - 1

        @pl.when(first_step)
        def _():
            lam_carry[...] = jnp.zeros_like(lam_carry)
            a_carry[...] = jnp.ones_like(a_carry)
            dxr_carry[...] = jnp.zeros_like(dxr_carry)
            dsp_acc[...] = jnp.zeros_like(dsp_acc)
            for ref in (dcw_ref, dcb_ref, dwa_ref, dba_ref, dwx_ref, dbx_ref):
                ref[...] = jnp.zeros_like(ref)

        xr = xr_ref[...]
        hs = hs_ref[...]
        gate = xg_ref[1]
        xb, r, ig, sp, a, mult = _rg_gates(xr, wa_ref, ba_ref, wx_ref, bx_ref, lam_ref)
        dyv = dy_ref[...]
        gg, dgg = _gelu_and_grad(gate)
        dhs = dyv * gg
        dxg_ref[1] = (dyv * hs * dgg).astype(BF16)
        row = _rows(xr.shape)
        a_scr[...] = jnp.where(row == ts - 1, a_carry[0:1, :], pltpu.roll(a, ts - 1, 0))
        b_scr[...] = dhs
        _real_slab_scan(a_scr, b_scr, b_scr, lam_carry, reverse=True)
        lmb = b_scr[...]
        a_carry[...] = a[:SUBLANES]
        hs_prev = _shift_down(hs, jnp.where(time_first, 0.0, hsh_ref[...]), 1)
        d_a = lmb * hs_prev
        d_m = lmb * (ig * xr)
        d_ig = lmb * mult * xr
        d_xr = lmb * mult * ig
        d_la = a * d_a - (a * a / mult) * d_m
        dsp_acc[...] += jnp.sum(-RG_C * r * d_la, axis=0, keepdims=True)
        d_za = (-RG_C * sp) * d_la * r * (1.0 - r)
        d_zx = d_ig * ig * (1.0 - ig)
        dba_ref[...] += jnp.sum(d_za, axis=0, keepdims=True)
        dbx_ref[...] += jnp.sum(d_zx, axis=0, keepdims=True)
        dzab, dzxb = d_za.astype(BF16), d_zx.astype(BF16)
        back = []
        for h in range(RG_HEADS):
            sl = slice(h * bw, (h + 1) * bw)
            dwa_ref[h] += lax.dot_general(xb[:, sl], dzab[:, sl], tn_dims, preferred_element_type=F32)
            dwx_ref[h] += lax.dot_general(xb[:, sl], dzxb[:, sl], tn_dims, preferred_element_type=F32)
            back.append(lax.dot_general(dzab[:, sl], wa_ref[h], nt_dims, preferred_element_type=F32)
                        + lax.dot_general(dzxb[:, sl], wx_ref[h], nt_dims, preferred_element_type=F32))
        d_xr = d_xr + jnp.concatenate(back, axis=1)
        d_xp = cw_ref[kw - 1:kw, :] * d_xr
        after = dxr_carry[...]
        for sft in range(1, kw):
            d_xp = d_xp + cw_ref[kw - 1 - sft:kw - sft, :] * _shift_up(d_xr, after, sft)
        dxr_carry[...] = d_xr[:SUBLANES]
        dxg_ref[0] = d_xp.astype(BF16)
        xp = xg_ref[0]
        before = jnp.where(time_first, 0.0, xph_ref[...])
        dcb_ref[...] += jnp.sum(d_xr, axis=0, keepdims=True)
        dcw_ref[kw - 1:kw, :] += jnp.sum(d_xr * xp, axis=0, keepdims=True)
        for sft in range(1, kw):
            dcw_ref[kw - 1 - sft:kw - sft, :] += jnp.sum(d_xr * _shift_down(xp, before, sft), axis=0, keepdims=True)
        dlam_ref[...] = dsp_acc[...] * (-_sigmoid(-lam_ref[...]))

    full = lambda shape: pl.BlockSpec(shape, lambda i: (0,) * len(shape))
    rev = lambda i: nt - 1 - i
    row_spec = pl.BlockSpec((ts, c), lambda i: (rev(i), 0))
    halo_idx = lambda i: jnp.maximum(rev(i) * hb - 1, 0)
    vec = (1, c)
    return _call_with_side(
        body, side, lambda: pl.program_id(0) == 0, lambda: pl.program_id(0) == nt - 1,
        name="rg_bwd", grid=(nt,),
        in_specs=[row_spec,
                  pl.BlockSpec((2, ts, c), lambda i: (0, rev(i), 0)),
                  pl.BlockSpec((None, SUBLANES, c), lambda i: (0, halo_idx(i), 0)),
                  row_spec, row_spec,
                  pl.BlockSpec((SUBLANES, c), lambda i: (halo_idx(i), 0)),
                  full(conv_w.shape), full(w_a.shape), full(b_a.shape), full(w_x.shape), full(b_x.shape), full(lam.shape)],
        out_specs=[pl.BlockSpec((2, ts, c), lambda i: (0, rev(i), 0)), full(conv_w.shape), full(vec), full(w_a.shape), full(vec),
                   full(w_x.shape), full(vec), full(vec)],
        out_shape=[jax.ShapeDtypeStruct((2, s, c), BF16), jax.ShapeDtypeStruct(conv_w.shape, F32), jax.ShapeDtypeStruct(vec, F32),
                   jax.ShapeDtypeStruct(w_a.shape, F32), jax.ShapeDtypeStruct(vec, F32), jax.ShapeDtypeStruct(w_x.shape, F32),
                   jax.ShapeDtypeStruct(vec, F32), jax.ShapeDtypeStruct(vec, F32)],
        scratch_shapes=[pltpu.VMEM(vec, F32), pltpu.VMEM((SUBLANES, c), F32), pltpu.VMEM((SUBLANES, c), F32),
                        pltpu.VMEM(vec, F32), pltpu.VMEM((ts, c), F32), pltpu.VMEM((ts, c), F32)],
        args=(dy, xg2, xg2, xr, hs, hs, conv_w, w_a, b_a, w_x, b_x, lam))


def _s5_param_fn(a_re, a_im, log_dt, bt_re, bt_im):
    dt = jnp.exp(log_dt)
    mag = jnp.exp(a_re * dt)
    abr = mag * jnp.cos(a_im * dt)
    abi = mag * jnp.sin(a_im * dt)
    ur, ui = abr - 1.0, abi
    den = a_re * a_re + a_im * a_im
    wr = (ur * a_re + ui * a_im) / den
    wi = (ui * a_re - ur * a_im) / den
    bbr = wr[None] * bt_re - wi[None] * bt_im
    bbi = wr[None] * bt_im + wi[None] * bt_re
    return abr, abi, bbr, bbi


def _s5_params_bwd(a_re, a_im, log_dt, bt_re, bt_im, d_abr, d_abi, d_bbr, d_bbi):
    def body(ar_ref, ai_ref, dt_ref, br_ref, bi_ref, g0, g1, g2, g3, o0, o1, o2, o3, o4):
        _, vjp = jax.vjp(_s5_param_fn, ar_ref[...], ai_ref[...], dt_ref[...], br_ref[...], bi_ref[...])
        outs = vjp((g0[...], g1[...], g2[...], g3[...]))
        for o, v in zip((o0, o1, o2, o3, o4), outs):
            o[...] = v

    sd = jax.ShapeDtypeStruct
    return pl.pallas_call(
        body, name="s5_params_bwd",
        out_shape=[sd(a_re.shape, F32), sd(a_im.shape, F32), sd(log_dt.shape, F32), sd(bt_re.shape, F32), sd(bt_im.shape, F32)],
    )(a_re, a_im, log_dt, bt_re, bt_im, d_abr, d_abi, d_bbr, d_bbi)


S5_LANE_CHUNK = 512


def _cmul_add(br, bi, tr, ti, sr, si):
    return br + tr * sr - ti * si, bi + tr * si + ti * sr


def _s5_tables3(a_re, a_im, log_dt, bt_re, bt_im, seg):
    g, p = a_re.shape
    gc = bt_re.shape[0]
    nsq = int(math.log2(seg))
    assert 1 << nsq == seg

    def body(ar_ref, ai_ref, dt_ref, br_ref, bi_ref, tr_ref, ti_ref, rtr_ref, rti_ref, bbr_ref, bbi_ref):
        abr, abi, bbr, bbi = _s5_param_fn(ar_ref[...], ai_ref[...], dt_ref[...], br_ref[...], bi_ref[...])
        bbr_ref[...] = bbr
        bbi_ref[...] = bbi
        qr, qi = abr, abi
        for _ in range(nsq):
            qr, qi = qr * qr - qi * qi, 2.0 * qr * qi
        pows = [(qr, qi)]
        for _ in range(1, SUBLANES):
            cr, ci = pows[-1]
            pows.append((cr * qr - ci * qi, cr * qi + ci * qr))
        zero = jnp.zeros_like(abr)
        for r in range(SUBLANES):
            rows = [(pows[(1 << k) - 1] if r >= (1 << k) else (zero, zero)) for k in range(3)] + [pows[r], (abr, abi)]
            for k, (vr, vi) in enumerate(rows):
                tr_ref[k, r] = vr
                ti_ref[k, r] = vi
                rtr_ref[k, SUBLANES - 1 - r] = vr
                rti_ref[k, SUBLANES - 1 - r] = -vi

    sd = jax.ShapeDtypeStruct
    tab = sd((5, SUBLANES, g, p), F32)
    return pl.pallas_call(
        body, name="s5_tables", out_shape=[tab, tab, tab, tab, sd((gc, g, p), F32), sd((gc, g, p), F32)],
    )(a_re, a_im, log_dt, bt_re, bt_im)


def _segment_perm(ts):
    seg = ts // SUBLANES
    rho = jnp.arange(ts)
    src = (rho % SUBLANES) * seg + rho // SUBLANES
    return (src[:, None] == jnp.arange(ts)[None, :]).astype(BF16)


def _exact_rows(perm_t, x):
    hi = x.astype(BF16)
    r1 = x - hi.astype(F32)
    mid = r1.astype(BF16)
    lo = (r1 - mid.astype(F32)).astype(BF16)
    dot = lambda v: jnp.dot(perm_t, v, preferred_element_type=F32)
    return (dot(hi) + dot(mid)) + dot(lo)


def _s5_fwd3(u, perm, perm_t, tab_r, tab_i, bp_r, bp_i, cp_r, cp_i, dvec, ts=256, side=None):
    s, c = u.shape
    n = tab_r.shape[2]
    nblk, cb, nb = bp_r.shape
    ts = min(ts, s)
    seg = ts // SUBLANES
    lc = min(S5_LANE_CHUNK, n)

    def body(u_ref, p_ref, pt_ref, tr_ref, ti_ref, bpr_ref, bpi_ref, cpr_ref, cpi_ref, d_ref, hr_ref, hi_ref, yp_ref, gy_ref,
             bur_ref, bui_ref, car_r, car_i):
        i = pl.program_id(0)

        @pl.when(i == 0)
        def _():
            car_r[...] = jnp.zeros_like(car_r)
            car_i[...] = jnp.zeros_like(car_i)

        uv = u_ref[...]
        ubp = jnp.dot(p_ref[...], uv.astype(BF16), preferred_element_type=F32).astype(BF16)
        for k in range(nblk):
            bur_ref[:, k * nb:(k + 1) * nb] = jnp.dot(ubp[:, k * cb:(k + 1) * cb], bpr_ref[k], preferred_element_type=F32)
            bui_ref[:, k * nb:(k + 1) * nb] = jnp.dot(ubp[:, k * cb:(k + 1) * cb], bpi_ref[k], preferred_element_type=F32)
        row8 = _rows((SUBLANES, lc))
        for q in range(n // lc):
            sl = slice(q * lc, (q + 1) * lc)
            tabs = [(tr_ref[k, :, sl], ti_ref[k, :, sl]) for k in range(5)]
            a_r, a_i = tabs[4]

            def local(r, carry, sl=sl, a_r=a_r, a_i=a_i):
                r0 = pl.multiple_of(r * SUBLANES, SUBLANES)
                hr, hi = _cmul_add(bur_ref[pl.ds(r0, SUBLANES), sl], bui_ref[pl.ds(r0, SUBLANES), sl], a_r, a_i, carry[0], carry[1])
                hr_ref[pl.ds(r0, SUBLANES), sl] = hr
                hi_ref[pl.ds(r0, SUBLANES), sl] = hi
                return hr, hi

            zero = jnp.zeros((SUBLANES, lc), F32)
            er, ei = lax.fori_loop(0, seg, local, (zero, zero), unroll=4)
            for k in range(3):
                sh = 1 << k
                er, ei = _cmul_add(er, ei, tabs[k][0], tabs[k][1], pltpu.roll(er, sh, 0), pltpu.roll(ei, sh, 0))
            cin_r, cin_i = jnp.broadcast_to(car_r[:, sl], er.shape), jnp.broadcast_to(car_i[:, sl], ei.shape)
            er, ei = _cmul_add(er, ei, tabs[3][0], tabs[3][1], cin_r, cin_i)
            car_r[:, sl] = er[SUBLANES - 1:, :]
            car_i[:, sl] = ei[SUBLANES - 1:, :]
            c_r = jnp.where(row8 == 0, cin_r, pltpu.roll(er, 1, 0))
            c_i = jnp.where(row8 == 0, cin_i, pltpu.roll(ei, 1, 0))

            def fix(r, carry, sl=sl, a_r=a_r, a_i=a_i, c_r=c_r, c_i=c_i):
                pr, pi = carry
                r0 = pl.multiple_of(r * SUBLANES, SUBLANES)
                hr, hi = _cmul_add(hr_ref[pl.ds(r0, SUBLANES), sl], hi_ref[pl.ds(r0, SUBLANES), sl], pr, pi, c_r, c_i)
                hr_ref[pl.ds(r0, SUBLANES), sl] = hr
                hi_ref[pl.ds(r0, SUBLANES), sl] = hi
                return pr * a_r - pi * a_i, pr * a_i + pi * a_r

            lax.fori_loop(0, seg, fix, (a_r, a_i), unroll=4)
        hrb, hib = hr_ref[...].astype(BF16), hi_ref[...].astype(BF16)
        y = jnp.concatenate([jnp.dot(hrb[:, k * nb:(k + 1) * nb], cpr_ref[k], preferred_element_type=F32)
                             - jnp.dot(hib[:, k * nb:(k + 1) * nb], cpi_ref[k], preferred_element_type=F32) for k in range(nblk)], axis=1)
        yp = _exact_rows(pt_ref[...], y) + d_ref[...] * uv
        yp_ref[...] = yp
        gy_ref[...] = _gelu(yp).astype(BF16)

    full = lambda shape: pl.BlockSpec(shape, lambda i: (0,) * len(shape))
    rc = pl.BlockSpec((ts, c), lambda i: (i, 0))
    rn = pl.BlockSpec((ts, n), lambda i: (i, 0))
    sd = jax.ShapeDtypeStruct
    nt = s // ts
    return _call_with_side(
        body, side, lambda: pl.program_id(0) == 0, lambda: pl.program_id(0) == nt - 1,
        name="s5_fwd", grid=(nt,),
        in_specs=[rc, full(perm.shape), full(perm_t.shape), full(tab_r.shape), full(tab_i.shape), full(bp_r.shape), full(bp_i.shape),
                  full(cp_r.shape), full(cp_i.shape), full(dvec.shape)],
        out_specs=[rn, rn, rc, rc],
        out_shape=[sd((s, n), F32), sd((s, n), F32), sd((s, c), F32), sd((s, c), BF16)],
        scratch_shapes=[pltpu.VMEM((ts, n), F32), pltpu.VMEM((ts, n), F32), pltpu.VMEM((1, n), F32), pltpu.VMEM((1, n), F32)],
        args=(u, perm, perm_t, tab_r, tab_i, bp_r, bp_i, cp_r, cp_i, dvec))


def _s5_bwd3(dgy, ypre, u, hr, hi, perm, perm_t, rtab_r, rtab_i, bp_r, bp_i, cp_r, cp_i, dvec, ts=256, side=None):
    s, c = u.shape
    n = rtab_r.shape[2]
    nblk, cb, nb = bp_r.shape
    ts = min(ts, s)
    nt = s // ts
    hb = ts // SUBLANES
    seg = ts // SUBLANES
    lc = min(S5_LANE_CHUNK, n)
    tn_dims = (((0,), (0,)), ((), ()))
    nt_dims = (((1,), (1,)), ((), ()))

    def body(dgy_ref, yp_ref, u_ref, hr_ref, hi_ref, hrh_ref, hih_ref, p_ref, pt_ref, tr_ref, ti_ref, bpr_ref, bpi_ref,
             cpr_ref, cpi_ref, d_ref, du_ref, dar_ref, dai_ref, dbr_ref, dbi_ref, dcr_ref, dci_ref, dd_ref, lr_ref, li_ref,
             car_r, car_i):
        i = pl.program_id(0)
        time_first = i == nt - 1

        @pl.when(i == 0)
        def _():
            car_r[...] = jnp.zeros_like(car_r)
            car_i[...] = jnp.zeros_like(car_i)
            for ref in (dar_ref, dai_ref, dbr_ref, dbi_ref, dcr_ref, dci_ref, dd_ref):
                ref[...] = jnp.zeros_like(ref)

        uv = u_ref[...]
        _, dgel = _gelu_and_grad(yp_ref[...])
        dyv = dgy_ref[...] * dgel
        dd_ref[...] += jnp.sum(dyv * uv, axis=0, keepdims=True)
        perm_m = p_ref[...]
        dyb = jnp.dot(perm_m, dyv.astype(BF16), preferred_element_type=F32).astype(BF16)
        ub = jnp.dot(perm_m, uv.astype(BF16), preferred_element_type=F32).astype(BF16)
        hrb, hib = hr_ref[...].astype(BF16), hi_ref[...].astype(BF16)
        for k in range(nblk):
            dblk = dyb[:, k * cb:(k + 1) * cb]
            lr_ref[:, k * nb:(k + 1) * nb] = lax.dot_general(dblk, cpr_ref[k], nt_dims, preferred_element_type=F32)
            li_ref[:, k * nb:(k + 1) * nb] = -lax.dot_general(dblk, cpi_ref[k], nt_dims, preferred_element_type=F32)
            dcr_ref[k] += lax.dot_general(hrb[:, k * nb:(k + 1) * nb], dblk, tn_dims, preferred_element_type=F32)
            dci_ref[k] += lax.dot_general(hib[:, k * nb:(k + 1) * nb], dblk, tn_dims, preferred_element_type=F32)
        row8 = _rows((SUBLANES, lc))
        last0 = (seg - 1) * SUBLANES
        for q in range(n // lc):
            sl = slice(q * lc, (q + 1) * lc)
            tabs = [(tr_ref[k, :, sl], ti_ref[k, :, sl]) for k in range(5)]
            a_r, a_i = tabs[4]

            def local(rr, carry, sl=sl, a_r=a_r, a_i=a_i):
                r0 = pl.multiple_of((seg - 1 - rr) * SUBLANES, SUBLANES)
                lr, li = _cmul_add(lr_ref[pl.ds(r0, SUBLANES), sl], li_ref[pl.ds(r0, SUBLANES), sl], a_r, a_i, carry[0], carry[1])
                lr_ref[pl.ds(r0, SUBLANES), sl] = lr
                li_ref[pl.ds(r0, SUBLANES), sl] = li
                return lr, li

            zero = jnp.zeros((SUBLANES, lc), F32)
            er, ei = lax.fori_loop(0, seg, local, (zero, zero), unroll=4)
            for k in range(3):
                sh = 1 << k
                er, ei = _cmul_add(er, ei, tabs[k][0], tabs[k][1], pltpu.roll(er, SUBLANES - sh, 0), pltpu.roll(ei, SUBLANES - sh, 0))
            cin_r, cin_i = jnp.broadcast_to(car_r[:, sl], er.shape), jnp.broadcast_to(car_i[:, sl], ei.shape)
            er, ei = _cmul_add(er, ei, tabs[3][0], tabs[3][1], cin_r, cin_i)
            car_r[:, sl] = er[:1, :]
            car_i[:, sl] = ei[:1, :]
            c_r = jnp.where(row8 == SUBLANES - 1, cin_r, pltpu.roll(er, SUBLANES - 1, 0))
            c_i = jnp.where(row8 == SUBLANES - 1, cin_i, pltpu.roll(ei, SUBLANES - 1, 0))
            halo_r = jnp.where(time_first, 0.0, hrh_ref[SUBLANES - 1:, sl])
            halo_i = jnp.where(time_first, 0.0, hih_ref[SUBLANES - 1:, sl])
            hp0_r = jnp.where(row8 == 0, jnp.broadcast_to(halo_r, zero.shape), pltpu.roll(hr_ref[pl.ds(last0, SUBLANES), sl], 1, 0))
            hp0_i = jnp.where(row8 == 0, jnp.broadcast_to(halo_i, zero.shape), pltpu.roll(hi_ref[pl.ds(last0, SUBLANES), sl], 1, 0))

            def fix(rr, carry, sl=sl, a_r=a_r, a_i=a_i, c_r=c_r, c_i=c_i, hp0_r=hp0_r, hp0_i=hp0_i):
                pr, pi, acc_r, acc_i = carry
                r = seg - 1 - rr
                r0 = pl.multiple_of(r * SUBLANES, SUBLANES)
                lr, li = _cmul_add(lr_ref[pl.ds(r0, SUBLANES), sl], li_ref[pl.ds(r0, SUBLANES), sl], pr, pi, c_r, c_i)
                lr_ref[pl.ds(r0, SUBLANES), sl] = lr
                li_ref[pl.ds(r0, SUBLANES), sl] = li
                p0 = pl.multiple_of(jnp.maximum(r - 1, 0) * SUBLANES, SUBLANES)
                hpr = jnp.where(r == 0, hp0_r, hr_ref[pl.ds(p0, SUBLANES), sl])
                hpi = jnp.where(r == 0, hp0_i, hi_ref[pl.ds(p0, SUBLANES), sl])
                return (pr * a_r - pi * a_i, pr * a_i + pi * a_r, acc_r + (lr * hpr + li * hpi), acc_i + (li * hpr - lr * hpi))

            _, _, acc_r, acc_i = lax.fori_loop(0, seg, fix, (a_r, a_i, zero, zero), unroll=4)
            dar_ref[:, sl] += jnp.sum(acc_r, axis=0, keepdims=True)
            dai_ref[:, sl] += jnp.sum(acc_i, axis=0, keepdims=True)
        lrb, lib = lr_ref[...].astype(BF16), li_ref[...].astype(BF16)
        du = []
        for k in range(nblk):
            ublk = ub[:, k * cb:(k + 1) * cb]
            lrk, lik = lrb[:, k * nb:(k + 1) * nb], lib[:, k * nb:(k + 1) * nb]
            dbr_ref[k] += lax.dot_general(ublk, lrk, tn_dims, preferred_element_type=F32)
            dbi_ref[k] += lax.dot_general(ublk, lik, tn_dims, preferred_element_type=F32)
            du.append(lax.dot_general(lrk, bpr_ref[k], nt_dims, preferred_element_type=F32)
                      + lax.dot_general(lik, bpi_ref[k], nt_dims, preferred_element_type=F32))
        du_ref[...] = (d_ref[...] * dyv + _exact_rows(pt_ref[...], jnp.concatenate(du, axis=1))).astype(BF16)

    full = lambda shape: pl.BlockSpec(shape, lambda i: (0,) * len(shape))
    rev = lambda i: nt - 1 - i
    halo_idx = lambda i: jnp.maximum(rev(i) * hb - 1, 0)
    rc = pl.BlockSpec((ts, c), lambda i: (rev(i), 0))
    rn = pl.BlockSpec((ts, n), lambda i: (rev(i), 0))
    hn = pl.BlockSpec((SUBLANES, n), lambda i: (halo_idx(i), 0))
    sd = jax.ShapeDtypeStruct
    vec_n = (1, n)
    return _call_with_side(
        body, side, lambda: pl.program_id(0) == 0, lambda: pl.program_id(0) == nt - 1,
        name="s5_bwd", grid=(nt,),
        in_specs=[rc, rc, rc, rn, rn, hn, hn, full(perm.shape), full(perm_t.shape), full(rtab_r.shape), full(rtab_i.shape),
                  full(bp_r.shape), full(bp_i.shape), full(cp_r.shape), full(cp_i.shape), full(dvec.shape)],
        out_specs=[rc, full(vec_n), full(vec_n), full(bp_r.shape), full(bp_i.shape), full(cp_r.shape), full(cp_i.shape),
                   full(dvec.shape)],
        out_shape=[sd((s, c), BF16), sd(vec_n, F32), sd(vec_n, F32), sd(bp_r.shape, F32), sd(bp_i.shape, F32),
                   sd(cp_r.shape, F32), sd(cp_i.shape, F32), sd(dvec.shape, F32)],
        scratch_shapes=[pltpu.VMEM((ts, n), F32), pltpu.VMEM((ts, n), F32), pltpu.VMEM((1, n), F32), pltpu.VMEM((1, n), F32)],
        args=(dgy, ypre, u, hr, hi, hr, hi, perm, perm_t, rtab_r, rtab_i, bp_r, bp_i, cp_r, cp_i, dvec))


def _glu_mm(gy, w_glu4, ts=1024, tn=512):
    s, d = gy.shape
    p, _, wc = w_glu4.shape
    c = p * wc // 2
    ts, tn = min(ts, s), min(tn, wc)
    per = wc // tn

    def body(x_ref, w1_ref, w2_ref, gl_ref, o_ref):
        x = x_ref[...]
        val = jnp.dot(x, w1_ref[...], preferred_element_type=F32)
        gate = jnp.dot(x, w2_ref[...], preferred_element_type=F32)
        gl_ref[0] = val
        gl_ref[1] = gate
        o_ref[...] = (val * _sigmoid(gate)).astype(BF16)

    return pl.pallas_call(
        body, name="s5_glu", grid=(c // tn, s // ts),
        in_specs=[pl.BlockSpec((ts, d), lambda j, i: (i, 0)),
                  pl.BlockSpec((None, d, tn), lambda j, i: (j // per, 0, j % per)),
                  pl.BlockSpec((None, d, tn), lambda j, i: (p // 2 + j // per, 0, j % per))],
        out_specs=[pl.BlockSpec((2, ts, tn), lambda j, i: (0, i, j)), pl.BlockSpec((ts, tn), lambda j, i: (i, j))],
        out_shape=[jax.ShapeDtypeStruct((2, s, c), F32), jax.ShapeDtypeStruct((s, c), BF16)], compiler_params=_cparams(),
    )(gy, w_glu4, w_glu4)


def _glu_bwd_mm(dh, w_out, gl2, ts=512):
    s, d = dh.shape
    c = w_out.shape[0]
    ts = min(ts, s)
    nt_dims = (((1,), (1,)), ((), ()))

    def body(dh_ref, w_ref, g_ref, o_ref):
        dov = lax.dot_general(dh_ref[...].astype(BF16), w_ref[...], nt_dims, preferred_element_type=F32)
        sg = _sigmoid(g_ref[1])
        o_ref[0] = (dov * sg).astype(BF16)
        o_ref[1] = (dov * g_ref[0] * sg * (1.0 - sg)).astype(BF16)

    blk = pl.BlockSpec((2, ts, c), lambda i: (0, i, 0))
    return pl.pallas_call(
        body, name="s5_out_dx", grid=(s // ts,),
        in_specs=[pl.BlockSpec((ts, d), lambda i: (i, 0)), pl.BlockSpec((c, d), lambda i: (0, 0)), blk],
        out_specs=blk, out_shape=jax.ShapeDtypeStruct((2, s, c), BF16), compiler_params=_cparams(),
    )(dh, w_out, gl2)


PACK_ROW_MULTIPLE = 1024
ELEMENTWISE_BLOCK_ELEMS = 256 * 1024


def _row_tile(rows, cols):
    pref = max(SUBLANES, 1 << int(math.log2(max(1, ELEMENTWISE_BLOCK_ELEMS // cols))))
    if rows <= pref:
        return rows
    t = pref
    while rows % t:
        t //= 2
    assert t >= SUBLANES, rows
    return t


def _sum_parts(rs, side=None):
    nl = len(rs)
    p, rows, cols = rs[0].shape
    tr = _row_tile(rows, cols)
    nt = rows // tr

    def body(*refs):
        o_ref = refs[nl]
        for l in range(nl):
            acc = refs[l][0].astype(F32)
            for k in range(1, p):
                acc = acc + refs[l][k].astype(F32)
            o_ref[l] = acc

    outs, got = _call_with_side(
        body, side, lambda: pl.program_id(0) == 0, lambda: pl.program_id(0) == nt - 1,
        name="sum_parts", grid=(nt,), in_specs=[pl.BlockSpec((p, tr, cols), lambda i: (0, i, 0))] * nl,
        out_specs=[pl.BlockSpec((nl, tr, cols), lambda i: (0, i, 0))], out_shape=[jax.ShapeDtypeStruct((nl, rows, cols), F32)],
        scratch_shapes=[], args=tuple(rs))
    return outs[0], got


def _adamw(w, g_parts, m, v, side=None):
    rows, cols = w.shape
    tr = _row_tile(rows, max(cols, LANES))
    ng = len(g_parts)
    emit_grad = ng > 1
    c1 = 1.0 / (1.0 - ADAM_B1 ** ADAM_STEP)
    c2 = 1.0 / (1.0 - ADAM_B2 ** ADAM_STEP)

    def body(*refs):
        w_ref, m_ref, v_ref = refs[0], refs[1 + ng], refs[2 + ng]
        dl_ref, nm_ref, nv_ref = refs[3 + ng:6 + ng]
        g = refs[1][...]
        for k in range(1, ng):
            g = g + refs[1 + k][...]
        mn = ADAM_B1 * m_ref[...] + (1.0 - ADAM_B1) * g
        vn = ADAM_B2 * v_ref[...] + (1.0 - ADAM_B2) * (g * g)
        if emit_grad:
            refs[6 + ng][...] = g
        nm_ref[...] = mn
        nv_ref[...] = vn
        dl_ref[...] = -ADAM_LR * ((mn * c1) / (jnp.sqrt(vn * c2) + ADAM_EPS) + ADAM_WD * w_ref[...])

    blk = pl.BlockSpec((tr, cols), lambda i: (i, 0))
    sd = jax.ShapeDtypeStruct((rows, cols), F32)
    nout = 4 if emit_grad else 3
    nt = rows // tr
    return _call_with_side(
        body, side, lambda: pl.program_id(0) == 0, lambda: pl.program_id(0) == nt - 1,
        name="adamw", grid=(nt,), in_specs=[blk] * (3 + ng), out_specs=[blk] * nout, out_shape=[sd] * nout,
        scratch_shapes=[], args=(w, *g_parts, m, v))


def _place():
    x, y, c = lax.axis_index("x"), lax.axis_index("y"), lax.axis_index("c")
    chips = [(1 - x, y), (x, 1 - y), (1 - x, 1 - y)]
    return x, y, c, chips


class Side:
    def __init__(self, ins, outs, kind):
        self.ins, self.outs, self.kind = list(ins), list(outs), kind
        n = len(self.ins)
        self.sems = [pltpu.SemaphoreType.DMA((3 * n,)), pltpu.SemaphoreType.DMA((3 * n,)), pltpu.SemaphoreType.DMA((n,))]

    def _copies(self, ins, outs, send, recv, lsem):
        x, y, c, chips = _place()
        me = 2 * x + y
        local, out_going, in_coming = [], [], []
        for t in range(len(ins)):
            if self.kind == 'sibling':
                cp = pltpu.make_async_remote_copy(src_ref=ins[t], dst_ref=outs[t], send_sem=send.at[t], recv_sem=recv.at[t],
                                                  device_id=(x, y, 1 - c), device_id_type=MESH)
                out_going.append(cp)
                in_coming.append(cp)
                continue
            if self.kind == 'gather':
                src_local, srcs, dst_mine = ins[t], [ins[t]] * 3, outs[t].at[me]
            else:
                src_local, srcs, dst_mine = ins[t].at[me], [ins[t].at[2 * px + py] for px, py in chips], outs[t].at[me]
            local.append(pltpu.make_async_copy(src_local, dst_mine, lsem.at[t]))
            for r, (px, py) in enumerate(chips):
                out_going.append(pltpu.make_async_remote_copy(
                    src_ref=srcs[r], dst_ref=dst_mine, send_sem=send.at[3 * t + r], recv_sem=recv.at[3 * t + r],
                    device_id=(px, py, c), device_id_type=MESH))
                in_coming.append(pltpu.make_async_remote_copy(
                    src_ref=srcs[r], dst_ref=outs[t].at[2 * px + py], send_sem=send.at[3 * t + r], recv_sem=recv.at[3 * t + r],
                    device_id=(px, py, c), device_id_type=MESH))
        return local, out_going, in_coming

    def start(self, ins, outs, send, recv, lsem):
        local, out_going, _ = self._copies(ins, outs, send, recv, lsem)
        for cp in local + out_going:
            cp.start()

    def wait(self, ins, outs, send, recv, lsem):
        local, out_going, in_coming = self._copies(ins, outs, send, recv, lsem)
        for cp in in_coming:
            cp.wait_recv()
        for cp in out_going:
            cp.wait_send()
        for cp in local:
            cp.wait()


def _gather_side(shards):
    return Side(shards, [jax.ShapeDtypeStruct((N_CHIPS,) + s.shape, s.dtype) for s in shards], 'gather')


def _scatter_side(grads):
    return Side(grads, [jax.ShapeDtypeStruct(g.shape, g.dtype) for g in grads], 'scatter')


def _sibling_side(arrs):
    return Side(arrs, [jax.ShapeDtypeStruct(a.shape, a.dtype) for a in arrs], 'sibling')


def _call_with_side(body, side, first, last, *, name, grid, in_specs, out_specs, out_shape, scratch_shapes, args):
    if side is None:
        outs = pl.pallas_call(body, name=name, grid=grid, in_specs=in_specs, out_specs=out_specs, out_shape=out_shape,
                              scratch_shapes=scratch_shapes, compiler_params=_cparams())(*args)
        return outs, []
    n_in, n_out, n_sc = len(in_specs), len(out_specs), len(scratch_shapes)
    ns_in, ns_out = len(side.ins), len(side.outs)

    def wrapped(*refs):
        base_in, s_in = refs[:n_in], refs[n_in:n_in + ns_in]
        o0 = n_in + ns_in
        base_out, s_out = refs[o0:o0 + n_out], refs[o0 + n_out:o0 + n_out + ns_out]
        sc0 = o0 + n_out + ns_out
        base_sc, sems = refs[sc0:sc0 + n_sc], refs[sc0 + n_sc:]

        @pl.when(first())
        def _():
            side.start(s_in, s_out, *sems)

        body(*base_in, *base_out, *base_sc)

        @pl.when(last())
        def _():
            side.wait(s_in, s_out, *sems)

    any_spec = pl.BlockSpec(memory_space=pl.ANY)
    outs = pl.pallas_call(
        wrapped, name=name, grid=grid, in_specs=list(in_specs) + [any_spec] * ns_in, out_specs=list(out_specs) + [any_spec] * ns_out,
        out_shape=list(out_shape) + side.outs, scratch_shapes=list(scratch_shapes) + side.sems, compiler_params=_cparams(),
    )(*args, *side.ins)
    return outs[:n_out], outs[n_out:]


def _run_side(name, side):
    def body(*refs):
        n = len(side.ins)
        side.start(refs[:n], refs[n:2 * n], *refs[2 * n:])
        side.wait(refs[:n], refs[n:2 * n], *refs[2 * n:])

    any_spec = pl.BlockSpec(memory_space=pl.ANY)
    return pl.pallas_call(body, name=name, in_specs=[any_spec] * len(side.ins), out_specs=[any_spec] * len(side.outs),
                          out_shape=side.outs, scratch_shapes=side.sems)(*side.ins)


def _allreduce_small(v):
    rows, cols = v.shape
    r8 = rows // (2 * N_CHIPS)
    assert r8 * 2 * N_CHIPS == rows and r8 % SUBLANES == 0, rows

    def body(v_ref, o_ref, sib_ref, cs_ref, slot_ref, send, recv):
        x, y, c, chips = _place()
        me = 2 * x + y
        sibling = (x, y, 1 - c)

        def eighth(ref, chip, core):
            return ref.at[pl.ds(pl.multiple_of((2 * chip + core) * r8, SUBLANES), r8)]

        def copy(src, dst, k, to):
            return pltpu.make_async_remote_copy(src_ref=src, dst_ref=dst, send_sem=send.at[k], recv_sem=recv.at[k],
                                                device_id=to, device_id_type=MESH)

        d2d = copy(v_ref, sib_ref, 0, sibling)
        d2d.start()
        d2d.wait_recv()
        cs_ref[...] = v_ref[...] + sib_ref[...]
        reduce_out = [copy(eighth(cs_ref, 2 * px + py, c), slot_ref.at[me], 1 + r, (px, py, c)) for r, (px, py) in enumerate(chips)]
        for cp in reduce_out:
            cp.start()
        slot_ref[me] = cs_ref[pl.ds(pl.multiple_of((2 * me + c) * r8, SUBLANES), r8), :]
        for r, (px, py) in enumerate(chips):
            copy(eighth(cs_ref, me, c), slot_ref.at[2 * px + py], 1 + r, (px, py, c)).wait_recv()
        o_ref[pl.ds(pl.multiple_of((2 * me + c) * r8, SUBLANES), r8), :] = (slot_ref[0] + slot_ref[1]) + (slot_ref[2] + slot_ref[3])
        mine = eighth(o_ref, me, c)
        hand_out = [copy(mine, mine, 4, sibling)] + [copy(mine, mine, 5 + r, (px, py, c)) for r, (px, py) in enumerate(chips)]
        for cp in hand_out:
            cp.start()
        passed_on = []
        for r, (px, py) in enumerate(chips):
            theirs = eighth(o_ref, 2 * px + py, c)
            copy(theirs, theirs, 5 + r, (px, py, c)).wait_recv()
            fw = copy(theirs, theirs, 8 + r, sibling)
            fw.start()
            passed_on.append(fw)
        sib_own = eighth(o_ref, me, 1 - c)
        copy(sib_own, sib_own, 4, sibling).wait_recv()
        for r, (px, py) in enumerate(chips):
            got = eighth(o_ref, 2 * px + py, 1 - c)
            copy(got, got, 8 + r, sibling).wait_recv()
        for cp in [d2d] + reduce_out + hand_out + passed_on:
            cp.wait_send()

    vm = pl.BlockSpec(memory_space=pltpu.VMEM)
    return pl.pallas_call(
        body, name="allreduce_small", in_specs=[vm], out_specs=vm, out_shape=jax.ShapeDtypeStruct((rows, cols), F32),
        scratch_shapes=[pltpu.VMEM((rows, cols), F32), pltpu.VMEM((rows, cols), F32), pltpu.VMEM((N_CHIPS, r8, cols), F32),
                        pltpu.SemaphoreType.DMA((11,)), pltpu.SemaphoreType.DMA((11,))],
        compiler_params=_cparams(),
    )(v)


def _pack(tensors):
    pieces = []
    for t in tensors:
        flat = t.reshape(-1)
        pad = (-flat.shape[0]) % (SUBLANES * LANES)
        pieces.append(jnp.pad(flat, (0, pad)).reshape(-1, LANES))
    rows = sum(p.shape[0] for p in pieces)
    pieces.append(jnp.zeros(((-rows) % PACK_ROW_MULTIPLE, LANES), tensors[0].dtype))
    return jnp.concatenate(pieces, axis=0)


def _unpack(buf, like):
    out, off = [], 0
    for t in like:
        size = math.prod(t.shape)
        rows = -(-size // (SUBLANES * LANES)) * SUBLANES
        out.append(buf[off:off + rows].reshape(-1)[:size].reshape(t.shape))
        off += rows
    return out


def _s5_pack_b(bb):
    gc, g, p = bb.shape
    q = S5_GROUPS_PER_BLOCK
    t = bb.reshape(gc, g // q, q, p).transpose(1, 2, 0, 3)
    eye = jnp.eye(q, dtype=bb.dtype)
    return (t[:, :, :, None, :] * eye[None, :, None, :, None]).reshape(g // q, q * gc, q * p)


def _s5_unpack_b(dbp, gc, p):
    nb = dbp.shape[0]
    q = S5_GROUPS_PER_BLOCK
    eye = jnp.eye(q, dtype=dbp.dtype)
    t = (dbp.reshape(nb, q, gc, q, p) * eye[None, :, None, :, None]).sum(axis=3)
    return t.transpose(2, 0, 1, 3).reshape(gc, nb * q, p)


def _s5_pack_c(cc):
    g, gc, p = cc.shape
    q = S5_GROUPS_PER_BLOCK
    t = cc.reshape(g // q, q, gc, p).transpose(0, 1, 3, 2)
    eye = jnp.eye(q, dtype=cc.dtype)
    return (t[:, :, :, None, :] * eye[None, :, None, :, None]).reshape(g // q, q * p, q * gc)


def _s5_unpack_c(dcp, gc, p):
    nb = dcp.shape[0]
    q = S5_GROUPS_PER_BLOCK
    eye = jnp.eye(q, dtype=dcp.dtype)
    t = (dcp.reshape(nb, q, p, q, gc) * eye[None, :, None, :, None]).sum(axis=3)
    return t.transpose(0, 1, 3, 2).reshape(nb * q, gc, p)


def _split2(m):
    return m.arr[:, 0]


def kernel(x, norm_mix_g, norm_ffn_g, norm_final_g, rg_w_in, rg_conv_w, rg_conv_b, rg_w_a, rg_b_a, rg_w_x, rg_b_x, rg_lambda, rg_w_out, s5_w_in, s5_a_re, s5_a_im, s5_log_dt, s5_b_re, s5_b_im, s5_c_re, s5_c_im, s5_d, s5_w_glu, s5_w_out, ffn_w_up, ffn_conv_w, ffn_conv_b, ffn_w_down, loss_target, m_norm_mix_g, m_norm_ffn_g, m_norm_final_g, m_rg_w_in, m_rg_conv_w, m_rg_conv_b, m_rg_w_a, m_rg_b_a, m_rg_w_x, m_rg_b_x, m_rg_lambda, m_rg_w_out, m_s5_w_in, m_s5_a_re, m_s5_a_im, m_s5_log_dt, m_s5_b_re, m_s5_b_im, m_s5_c_re, m_s5_c_im, m_s5_d, m_s5_w_glu, m_s5_w_out, m_ffn_w_up, m_ffn_conv_w, m_ffn_conv_b, m_ffn_w_down, v_norm_mix_g, v_norm_ffn_g, v_norm_final_g, v_rg_w_in, v_rg_conv_w, v_rg_conv_b, v_rg_w_a, v_rg_b_a, v_rg_w_x, v_rg_b_x, v_rg_lambda, v_rg_w_out, v_s5_w_in, v_s5_a_re, v_s5_a_im, v_s5_log_dt, v_s5_b_re, v_s5_b_im, v_s5_c_re, v_s5_c_im, v_s5_d, v_s5_w_glu, v_s5_w_out, v_ffn_w_up, v_ffn_conv_w, v_ffn_conv_b, v_ffn_w_down):
    w = dict(zip(PARAM_NAMES, (norm_mix_g, norm_ffn_g, norm_final_g, rg_w_in, rg_conv_w, rg_conv_b, rg_w_a, rg_b_a, rg_w_x, rg_b_x,
                               rg_lambda, rg_w_out, s5_w_in, s5_a_re, s5_a_im, s5_log_dt, s5_b_re, s5_b_im, s5_c_re, s5_c_im, s5_d,
                               s5_w_glu, s5_w_out, ffn_w_up, ffn_conv_w, ffn_conv_b, ffn_w_down)))
    mom = dict(zip(PARAM_NAMES, (m_norm_mix_g, m_norm_ffn_g, m_norm_final_g, m_rg_w_in, m_rg_conv_w, m_rg_conv_b, m_rg_w_a, m_rg_b_a,
                                 m_rg_w_x, m_rg_b_x, m_rg_lambda, m_rg_w_out, m_s5_w_in, m_s5_a_re, m_s5_a_im, m_s5_log_dt, m_s5_b_re,
                                 m_s5_b_im, m_s5_c_re, m_s5_c_im, m_s5_d, m_s5_w_glu, m_s5_w_out, m_ffn_w_up, m_ffn_conv_w,
                                 m_ffn_conv_b, m_ffn_w_down)))
    vel = dict(zip(PARAM_NAMES, (v_norm_mix_g, v_norm_ffn_g, v_norm_final_g, v_rg_w_in, v_rg_conv_w, v_rg_conv_b, v_rg_w_a, v_rg_b_a,
                                 v_rg_w_x, v_rg_b_x, v_rg_lambda, v_rg_w_out, v_s5_w_in, v_s5_a_re, v_s5_a_im, v_s5_log_dt, v_s5_b_re,
                                 v_s5_b_im, v_s5_c_re, v_s5_c_im, v_s5_d, v_s5_w_glu, v_s5_w_out, v_ffn_w_up, v_ffn_conv_w,
                                 v_ffn_conv_b, v_ffn_w_down)))
    _, s, d = x.shape
    depth = norm_mix_g.shape[0]
    n_grp, n_state = s5_a_re.shape[1], s5_a_re.shape[2]
    gc = s5_b_re.shape[3]
    d_ff = ffn_w_down.shape[1] * N_CHIPS
    s5_ts = min(256, s)
    s5_perm = _segment_perm(s5_ts)

    wb = {n: (w[n].astype(BF16) if n in BIG else w[n]) for n in SHARDED}
    gath = {}

    def mixer_keys(i):
        return [(n, i // 2) for n in MIXER_SHARDED[i % 2]] if i < depth else []

    def gather_side(keys):
        return _gather_side([wb[n][l] for n, l in keys])

    def put(keys, arrs):
        for k, a in zip(keys, arrs):
            gath[k] = a

    def wcol(n, l):
        return Mat(gath[(n, l)][:, None], 0, 'c')

    def wrow(n, l):
        g = gath[(n, l)]
        return Mat(g.reshape(1, 1, N_CHIPS * g.shape[1], g.shape[2]), 0, 'c')

    def rg_cw(l):
        return gath[('rg_conv_w', l)].transpose(1, 0, 2).reshape(RG_CONV_W, d)

    def s5_dv(l):
        return gath[('s5_d', l)].reshape(1, d)

    def f_cw(l):
        return gath[('ffn_conv_w', l)].transpose(1, 0, 2).reshape(FFN_CONV_W, 2, d_ff).transpose(1, 0, 2)

    tm = min(1024, s)
    tkw = min(2048, s)
    d_up = 2 * d_ff // N_CHIPS
    f_cb = ffn_conv_b.reshape(depth, 2, 1, d_ff)

    h = x.reshape(s, d)
    saved = []
    for i in range(depth):
        j = i // 2
        sv = {'h_in': h}
        first_late = mixer_keys(0)[-1:] if i == 0 else []
        if i == 0:
            hn, got = _rms_fwd(h, norm_mix_g[:1], side=gather_side(mixer_keys(0)[:-1]))
            put(mixer_keys(0)[:-1], got)
        sv['hn'] = hn
        up_keys = [('ffn_w_up', i), ('ffn_conv_w', i)]
        if i % 2 == 0:
            xg = _mm("rg_in", 'nn', act(hn), wcol('rg_w_in', j), out_parts=2, tm=tm, tn=512, tk=d,
                     side=gather_side(first_late) if first_late else None)
            if first_late:
                xg, got = xg
                put(first_late, got)
            xg2 = _split2(xg)
            wa, wx = rg_w_a[j].astype(BF16), rg_w_x[j].astype(BF16)
            ba, bx = rg_b_a[j].reshape(1, d), rg_b_x[j].reshape(1, d)
            (xr, hs, y), got = _rg_fwd(xg2, rg_cw(j), rg_conv_b[j:j + 1], wa, ba, wx, bx, rg_lambda[j:j + 1],
                                       side=gather_side(up_keys))
            put(up_keys, got)
            sv.update(xg2=xg2, xr=xr, hs=hs, y=y, wa=wa, wx=wx, ba=ba, bx=bx)
            h, hn2 = _mm("rg_out", 'nn', act(y), wrow('rg_w_out', j), res=act(h), norm_g=norm_ffn_g[i:i + 1], tm=tm, tn=d, tk=d)
        else:
            u = _mm("s5_in", 'nn', act(hn), wrow('s5_w_in', j), tm=tm, tn=d, tk=d).arr[0, 0]
            bt_re, bt_im = s5_b_re[j].transpose(2, 0, 1), s5_b_im[j].transpose(2, 0, 1)
            ldt = s5_log_dt[j].reshape(n_grp, 1)
            tab_r, tab_i, rtab_r, rtab_i, bbr, bbi = _s5_tables3(s5_a_re[j], s5_a_im[j], ldt, bt_re, bt_im, seg=s5_ts // SUBLANES)
            nn_ = n_grp * n_state
            tab_r, tab_i, rtab_r, rtab_i = (t.reshape(5, SUBLANES, nn_) for t in (tab_r, tab_i, rtab_r, rtab_i))
            prm = dict(bp_r=_s5_pack_b(bbr).astype(BF16), bp_i=_s5_pack_b(bbi).astype(BF16),
                       cp_r=_s5_pack_c(s5_c_re[j]).astype(BF16), cp_i=_s5_pack_c(s5_c_im[j]).astype(BF16), dvec=s5_dv(j))
            (hr, hi, ypre, gy), got = _s5_fwd3(u, s5_perm, s5_perm.T, tab_r, tab_i, ts=s5_ts, side=gather_side(up_keys), **prm)
            sv.update(rtab_r=rtab_r, rtab_i=rtab_i)
            put(up_keys, got)
            gl2, o = _glu_mm(gy, gath[('s5_w_glu', j)])
            sv.update(u=u, prm=prm, hr=hr, hi=hi, ypre=ypre, gy=gy, gl2=gl2, o=o, bt_re=bt_re, bt_im=bt_im, ldt=ldt)
            h, hn2 = _mm("s5_out", 'nn', act(o), wrow('s5_w_out', j), res=act(h), norm_g=norm_ffn_g[i:i + 1], tm=tm, tn=d, tk=d)
        h = h.arr[0, 0]
        sv['h_mid'] = h
        next_keys = [('ffn_w_down', i)] + mixer_keys(i + 1)
        (up2, c2, a_ffn), got = _ffn_up_act(hn2, gath[('ffn_w_up', i)], f_cw(i), f_cb[i], side=gather_side(next_keys))
        put(next_keys, got)
        sv.update(hn2=hn2, up2=up2, c2=c2, act=a_ffn)
        if i + 1 < depth:
            h, hn = _mm("ffn_down", 'nn', act(a_ffn), wrow('ffn_w_down', i), res=act(h), norm_g=norm_mix_g[i + 1:i + 2], tm=tm, tn=d,
                        tk=d_ff // 2)
        else:
            h = _mm("ffn_down", 'nn', act(a_ffn), wrow('ffn_w_down', i), res=act(h), tm=tm, tn=d, tk=d_ff // 2)
        h = h.arr[0, 0]
        saved.append(sv)

    loss_row, dh, dg_final = _loss_and_grad(h, norm_final_g.reshape(1, d), loss_target.reshape(s, d))
    loss = lax.psum(loss_row[0, 0], ("x", "y", "c"))

    gl_ = {n: [None] * w[n].shape[0] for n in PARAM_NAMES if n != 'norm_final_g'}
    recvd = {}

    def scatter_side(keys):
        return _scatter_side([gl_[n][l].reshape((N_CHIPS,) + w[n].shape[1:]) for n, l in keys])

    def record(keys, arrs):
        for k, a in zip(keys, arrs):
            recvd[k] = a

    pending = None
    for i in reversed(range(depth)):
        j = i // 2
        sv = saved[i]
        gl_['ffn_w_down'][i] = _mm("ffn_down_dw", 'tn', act(sv['act']), act(dh), out_dtype=BF16, tm=d_ff // N_CHIPS, tn=d, tk=tkw).arr
        (dup2, dcw2, dcb2), got = _ffn_bwd_fused(dh, gath[('ffn_w_down', i)].reshape(d_ff, d), sv['up2'], sv['c2'], f_cw(i),
                                                 side=scatter_side(pending) if pending else None)
        if pending:
            record(pending, got)
        gl_['ffn_conv_w'][i] = dcw2.transpose(1, 0, 2).reshape(FFN_CONV_W, 2 * d_ff)
        gl_['ffn_conv_b'][i] = dcb2.reshape(2 * d_ff)
        dup = Mat(dup2[:, None], 0, 'c')
        gl_['ffn_w_up'][i] = _mm("ffn_up_dw", 'tn', act(sv['hn2']), dup, out_parts=N_CHIPS, out_dtype=BF16, tm=d, tn=d_up, tk=tkw).arr
        (dh, dg), _ = _mm_rms_bwd("ffn_up_dx", dup, wcol('ffn_w_up', i), sv['h_mid'], norm_ffn_g[i:i + 1], dh, tm=tm, tk=d_up)
        gl_['norm_ffn_g'][i] = dg[0]
        ffn_keys = [('ffn_w_up', i), ('ffn_w_down', i)]
        if i % 2 == 0:
            dy = _mm("rg_out_dx", 'nt', act(dh), wrow('rg_w_out', j), tm=tm, tn=d, tk=d).arr[0, 0]
            gl_['rg_w_out'][j] = _mm("rg_out_dw", 'tn', act(sv['y']), act(dh), out_dtype=BF16, tm=d, tn=d, tk=tkw).arr
            (dxg2, dcw, dcb, dwa, dba, dwx, dbx, dlam), got = _rg_bwd(
                dy, sv['xg2'], sv['xr'], sv['hs'], rg_cw(j), sv['wa'], sv['ba'], sv['wx'], sv['bx'], rg_lambda[j:j + 1],
                side=scatter_side(ffn_keys))
            record(ffn_keys, got)
            gl_['rg_conv_w'][j] = dcw
            gl_['rg_conv_b'][j] = dcb[0]
            gl_['rg_w_a'][j], gl_['rg_w_x'][j] = dwa, dwx
            gl_['rg_b_a'][j], gl_['rg_b_x'][j] = dba.reshape(rg_b_a.shape[1:]), dbx.reshape(rg_b_x.shape[1:])
            gl_['rg_lambda'][j] = dlam[0]
            dxg = Mat(dxg2[:, None], 0, 'c')
            gl_['rg_w_in'][j] = _mm("rg_in_dw", 'tn', act(sv['hn']), dxg, out_parts=N_CHIPS, out_dtype=BF16, tm=d, tn=512, tk=tkw).arr
            mix_dx = ("rg_in_dx", dxg, wcol('rg_w_in', j), 512)
            pending = [('rg_w_in', j), ('rg_w_out', j)]
        else:
            gl_['s5_w_out'][j] = _mm("s5_out_dw", 'tn', act(sv['o']), act(dh), out_dtype=BF16, tm=d, tn=d, tk=tkw).arr
            dgl2 = _glu_bwd_mm(dh, gath[('s5_w_out', j)].reshape(d, d), sv['gl2'])
            dgl = Mat(dgl2[:, None], 0, 'c')
            gl_['s5_w_glu'][j] = _mm("s5_glu_dw", 'tn', act(sv['gy']), dgl, out_parts=N_CHIPS, out_dtype=BF16, tm=d, tn=512, tk=tkw).arr
            dgy = _mm("s5_glu_dx", 'nt', dgl, wcol('s5_w_glu', j), tm=tm, tn=d, tk=512).arr[0, 0]
            (du, dar, dai, dbpr, dbpi, dcpr, dcpi, dd), got = _s5_bwd3(
                dgy, sv['ypre'], sv['u'], sv['hr'], sv['hi'], s5_perm, s5_perm.T, sv['rtab_r'], sv['rtab_i'], ts=s5_ts,
                side=scatter_side(ffn_keys), **sv['prm'])
            record(ffn_keys, got)
            gl_['s5_d'][j] = dd[0]
            gl_['s5_c_re'][j] = _s5_unpack_c(dcpr, gc, n_state)
            gl_['s5_c_im'][j] = -_s5_unpack_c(dcpi, gc, n_state)
            d_are, d_aim, d_ldt, d_btr, d_bti = _s5_params_bwd(
                s5_a_re[j], s5_a_im[j], sv['ldt'], sv['bt_re'], sv['bt_im'], dar.reshape(n_grp, n_state), dai.reshape(n_grp, n_state),
                _s5_unpack_b(dbpr, gc, n_state), _s5_unpack_b(dbpi, gc, n_state))
            gl_['s5_a_re'][j], gl_['s5_a_im'][j], gl_['s5_log_dt'][j] = d_are, d_aim, d_ldt[:, 0]
            gl_['s5_b_re'][j], gl_['s5_b_im'][j] = d_btr.transpose(1, 2, 0), d_bti.transpose(1, 2, 0)
            dum = act(du)
            gl_['s5_w_in'][j] = _mm("s5_in_dw", 'tn', act(sv['hn']), dum, out_dtype=BF16, tm=d, tn=d, tk=tkw).arr
            mix_dx = ("s5_in_dx", dum, wrow('s5_w_in', j), d)
            pending = [('s5_w_in', j), ('s5_w_glu', j), ('s5_w_out', j)]
        (dh, dg), got = _mm_rms_bwd(mix_dx[0], mix_dx[1], mix_dx[2], sv['h_in'], norm_mix_g[i:i + 1], dh, tm=tm, tk=mix_dx[3],
                                    side=scatter_side(pending) if i == 0 else None)
        if i == 0:
            record(pending, got)
        gl_['norm_mix_g'][i] = dg[0]
    grad_x = dh.reshape(x.shape)

    order = sorted(BIG, key=lambda n: -math.prod(w[n].shape))
    chip_sums, theirs, prev = {}, {}, None
    for n in order:
        cols = w[n].shape[-1]
        cs, got = _sum_parts([recvd[(n, l)].reshape(N_CHIPS, -1, cols) for l in range(w[n].shape[0])],
                             side=_sibling_side([chip_sums[prev]]) if prev else None)
        chip_sums[n] = cs.reshape(-1, cols)
        if prev:
            theirs[prev] = got[0]
        prev = n
    theirs[prev] = _run_side("swap_last", _sibling_side([chip_sums[prev]]))[0]
    results = {}
    for n in BIG:
        cols = w[n].shape[-1]
        (delta, new_m, new_v, grad), _ = _adamw(w[n].reshape(-1, cols), [chip_sums[n], theirs[n]], mom[n].reshape(-1, cols),
                                                vel[n].reshape(-1, cols))
        results[n] = [o.reshape(w[n].shape) for o in (grad, delta, new_m, new_v)]

    small = REPLICATED + SMALL_SHARDED
    local = [dg_final.reshape(d) if n == 'norm_final_g' else jnp.stack(gl_[n]) for n in small]
    summed = _unpack(_allreduce_small(_pack(local)), local)
    me = 2 * lax.axis_index("x") + lax.axis_index("y")
    for n, g in zip(small, summed):
        if n in SMALL_SHARDED:
            g = lax.dynamic_slice_in_dim(g, me * w[n].shape[-1], w[n].shape[-1], axis=g.ndim - 1)
        view = (-1, w[n].shape[-1])
        (delta, new_m, new_v), _ = _adamw(w[n].reshape(view), [g.reshape(view)], mom[n].reshape(view), vel[n].reshape(view))
        results[n] = [g] + [o.reshape(w[n].shape) for o in (delta, new_m, new_v)]

    return (loss, grad_x, *[results[n][0] for n in PARAM_NAMES], *[results[n][1] for n in PARAM_NAMES],
            *[results[n][2] for n in PARAM_NAMES], *[results[n][3] for n in PARAM_NAMES])
```

```python
import math

import jax
import jax.numpy as jnp
from jax import lax
from jax.experimental import pallas as pl
from jax.experimental.pallas import tpu as pltpu

F32 = jnp.float32
BF16 = jnp.bfloat16
MESH = pl.DeviceIdType.MESH

NORM_EPS = 1e-6
RG_HEADS = 8
RG_CONV_W = 4
RG_C = 8.0
S5_GC = 16
S5_P = 64
S5_GROUPS_PER_BLOCK = 8
FFN_CONV_W = 3
N_CHIPS = 4
ADAM_LR, ADAM_B1, ADAM_B2, ADAM_EPS, ADAM_WD, ADAM_STEP = 0.001, 0.9, 0.999, 1e-08, 0.01, 10
VMEM_LIMIT_BYTES = 56 * 1024 * 1024
SUBLANES = 8
LANES = 128

PARAM_NAMES = ['norm_mix_g', 'norm_ffn_g', 'norm_final_g', 'rg_w_in', 'rg_conv_w', 'rg_conv_b', 'rg_w_a', 'rg_b_a', 'rg_w_x',
               'rg_b_x', 'rg_lambda', 'rg_w_out', 's5_w_in', 's5_a_re', 's5_a_im', 's5_log_dt', 's5_b_re', 's5_b_im', 's5_c_re',
               's5_c_im', 's5_d', 's5_w_glu', 's5_w_out', 'ffn_w_up', 'ffn_conv_w', 'ffn_conv_b', 'ffn_w_down']
SHARDED = ['rg_w_in', 'rg_conv_w', 'rg_w_out', 's5_w_in', 's5_d', 's5_w_glu', 's5_w_out', 'ffn_w_up', 'ffn_conv_w', 'ffn_w_down']
BIG = ['rg_w_in', 'rg_w_out', 's5_w_in', 's5_w_glu', 's5_w_out', 'ffn_w_up', 'ffn_w_down']
SMALL_SHARDED = ['rg_conv_w', 's5_d', 'ffn_conv_w']
MIXER_SHARDED = [['rg_w_in', 'rg_conv_w', 'rg_w_out'], ['s5_w_in', 's5_d', 's5_w_glu', 's5_w_out']]
REPLICATED = [n for n in PARAM_NAMES if n not in SHARDED]


def _cparams():
    return pltpu.CompilerParams(vmem_limit_bytes=VMEM_LIMIT_BYTES)


_GELU_C = math.sqrt(2.0 / math.pi)
_GELU_K = 0.044715


def _gelu(x):
    return 0.5 * x * (1.0 + jnp.tanh(_GELU_C * (x + _GELU_K * x * x * x)))


def _gelu_and_grad(x):
    t = jnp.tanh(_GELU_C * (x + _GELU_K * x * x * x))
    g = 0.5 * x * (1.0 + t)
    dg = 0.5 * (1.0 + t) + 0.5 * x * (1.0 - t * t) * (_GELU_C * (1.0 + 3.0 * _GELU_K * x * x))
    return g, dg


def _sigmoid(x):
    return jax.nn.sigmoid(x)


def _neg_expm1(x):
    series = -(x * (1.0 + x * (0.5 + x * (1.0 / 6 + x * (1.0 / 24 + x * (1.0 / 120 + x * (1.0 / 720)))))))
    return jnp.where(x > -0.25, series, 1.0 - jnp.exp(x))


def _softplus(z):
    return jnp.maximum(z, 0.0) + jnp.log1p(jnp.exp(-jnp.abs(z)))


def _rows(shape):
    return lax.broadcasted_iota(jnp.int32, shape, 0)


def _shift_down(x, halo, k):
    ext = jnp.concatenate([halo, x], axis=0)
    return pltpu.roll(ext, k, 0)[SUBLANES:]


def _shift_up(x, halo, k):
    ext = jnp.concatenate([x, halo], axis=0)
    n = ext.shape[0]
    return pltpu.roll(ext, n - k, 0)[:x.shape[0]]


RG_LANE_CHUNK = 512


def _real_slab_scan(a_ref, b_ref, out_ref, carry_ref, reverse):
    t, c = a_ref.shape
    nsl = t // SUBLANES
    lc = min(RG_LANE_CHUNK, c)
    row8 = _rows((SUBLANES, lc))
    for q in range(c // lc):
        sl = slice(q * lc, (q + 1) * lc)

        def slab(jj, carry, sl=sl):
            j = nsl - 1 - jj if reverse else jj
            r0 = pl.multiple_of(j * SUBLANES, SUBLANES)
            a, b = a_ref[pl.ds(r0, SUBLANES), sl], b_ref[pl.ds(r0, SUBLANES), sl]
            for k in range(3):
                sh = 1 << k
                keep = row8 < SUBLANES - sh if reverse else row8 >= sh
                amount = SUBLANES - sh if reverse else sh
                b = a * jnp.where(keep, pltpu.roll(b, amount, 0), 0.0) + b
                a = a * jnp.where(keep, pltpu.roll(a, amount, 0), 1.0)
            x = b + a * jnp.broadcast_to(carry, b.shape)
            out_ref[pl.ds(r0, SUBLANES), sl] = x
            return x[:1, :] if reverse else x[SUBLANES - 1:, :]

        carry_ref[:, sl] = lax.fori_loop(0, nsl, slab, carry_ref[:, sl], unroll=2)


class Mat:
    def __init__(self, arr, l=0, split='c'):
        assert arr.ndim == 4
        self.arr, self.l, self.split = arr, l, split
        p, _, r, c = arr.shape
        self.shape = (r, c * p) if split == 'c' else (r * p, c)

    def spec(self, tr, tc, rc):
        p, _, r, c = self.arr.shape
        l = self.l
        assert r % tr == 0 and c % tc == 0, (self.arr.shape, tr, tc)
        if self.split == 'c':
            per = c // tc
            return pl.BlockSpec((None, None, tr, tc), lambda i, j, k: (rc(i, j, k)[1] // per, l, rc(i, j, k)[0], rc(i, j, k)[1] % per))
        per = r // tr
        return pl.BlockSpec((None, None, tr, tc), lambda i, j, k: (rc(i, j, k)[0] // per, l, rc(i, j, k)[0] % per, rc(i, j, k)[1]))


def act(x, parts=1):
    s, c = x.shape
    return Mat(x.reshape(s, parts, c // parts).transpose(1, 0, 2)[:, None] if parts > 1 else x[None, None])


def _mm(name, mode, a, b, *, out_parts=1, out_split='c', out_dtype=F32, res=None, norm_g=None, tm=512, tn=512, tk=512,
        side=None):
    if mode == 'nn':
        (m, kk), (kb, n) = a.shape, b.shape
    elif mode == 'nt':
        (m, kk), (n, kb) = a.shape, b.shape
    else:
        (kk, m), (kb, n) = a.shape, b.shape
    assert kk == kb, (name, a.shape, b.shape)
    tm, tn, tk = min(tm, m), min(tn, n), min(tk, kk)
    assert m % tm == 0 and n % tn == 0 and kk % tk == 0, (name, m, n, kk, tm, tn, tk)
    nk = kk // tk
    if mode == 'nn':
        a_spec = a.spec(tm, tk, lambda i, j, k: (i, k))
        b_spec = b.spec(tk, tn, lambda i, j, k: (k, j))
        dims = (((1,), (0,)), ((), ()))
    elif mode == 'nt':
        a_spec = a.spec(tm, tk, lambda i, j, k: (i, k))
        b_spec = b.spec(tn, tk, lambda i, j, k: (j, k))
        dims = (((1,), (1,)), ((), ()))
    else:
        a_spec = a.spec(tk, tm, lambda i, j, k: (k, i))
        b_spec = b.spec(tk, tn, lambda i, j, k: (k, j))
        dims = (((0,), (0,)), ((), ()))
    if out_split == 'c':
        out_arr = jax.ShapeDtypeStruct((out_parts, 1, m, n // out_parts), out_dtype)
    else:
        out_arr = jax.ShapeDtypeStruct((out_parts, 1, m // out_parts, n), out_dtype)
    out_mat = Mat(out_arr, 0, out_split)
    o_spec = out_mat.spec(tm, tn, lambda i, j, k: (i, j))
    has_res = res is not None
    has_norm = norm_g is not None
    assert not has_norm or tn == n, (name, tn, n)

    def body(*refs):
        a_ref, b_ref = refs[:2]
        extra = list(refs[2:2 + has_res + has_norm])
        r_ref = extra.pop(0) if has_res else None
        g_ref = extra.pop(0) if has_norm else None
        o_ref = refs[2 + has_res + has_norm]
        prod = lax.dot_general(a_ref[...].astype(BF16), b_ref[...].astype(BF16), dims, preferred_element_type=F32)

        def finish(acc):
            if has_res:
                acc = acc + r_ref[...]
            o_ref[...] = acc.astype(out_dtype)
            if has_norm:
                var = jnp.mean(acc * acc, axis=-1, keepdims=True)
                refs[3 + has_res + has_norm][...] = (acc * lax.rsqrt(var + NORM_EPS) * g_ref[...]).astype(BF16)

        if nk == 1:
            finish(prod)
        else:
            acc_ref = refs[-1]
            k = pl.program_id(2)

            @pl.when(k == 0)
            def _():
                acc_ref[...] = prod

            @pl.when(k > 0)
            def _():
                acc_ref[...] += prod

            @pl.when(k == nk - 1)
            def _():
                finish(acc_ref[...])

    in_specs = [a_spec, b_spec]
    args = [a.arr, b.arr]
    if has_res:
        in_specs.append(res.spec(tm, tn, lambda i, j, k: (i, j)))
        args.append(res.arr)
    out_specs, out_shape = [o_spec], [out_arr]
    if has_norm:
        in_specs.append(pl.BlockSpec((1, n), lambda i, j, k: (0, 0)))
        args.append(norm_g)
        out_specs.append(pl.BlockSpec((tm, n), lambda i, j, k: (i, 0)))
        out_shape.append(jax.ShapeDtypeStruct((m, n), BF16))
    gi, gj = m // tm, n // tn
    outs, got = _call_with_side(
        body, side, lambda: (pl.program_id(0) == 0) & (pl.program_id(1) == 0) & (pl.program_id(2) == 0),
        lambda: (pl.program_id(0) == gi - 1) & (pl.program_id(1) == gj - 1) & (pl.program_id(2) == nk - 1),
        name=name, grid=(gi, gj, nk), in_specs=in_specs, out_specs=out_specs, out_shape=out_shape,
        scratch_shapes=[pltpu.VMEM((tm, tn), F32)] if nk > 1 else [], args=tuple(args))
    result = (Mat(outs[0], 0, out_split), outs[1]) if has_norm else Mat(outs[0], 0, out_split)
    return result if side is None else (result, got)


def _rms_fwd(h, g, ts=512, side=None):
    s, d = h.shape
    ts = min(ts, s)
    nt = s // ts

    def body(h_ref, g_ref, o_ref):
        x = h_ref[...]
        var = jnp.mean(x * x, axis=-1, keepdims=True)
        o_ref[...] = (x * lax.rsqrt(var + NORM_EPS) * g_ref[...]).astype(BF16)

    outs, got = _call_with_side(
        body, side, lambda: pl.program_id(0) == 0, lambda: pl.program_id(0) == nt - 1,
        name="rms_fwd", grid=(nt,),
        in_specs=[pl.BlockSpec((ts, d), lambda i: (i, 0)), pl.BlockSpec((1, d), lambda i: (0, 0))],
        out_specs=[pl.BlockSpec((ts, d), lambda i: (i, 0))], out_shape=[jax.ShapeDtypeStruct((s, d), BF16)],
        scratch_shapes=[], args=(h, g))
    return outs[0], got


def _loss_and_grad(h, g, tgt, ts=512):
    s, d = h.shape
    ts = min(ts, s)

    def body(h_ref, g_ref, t_ref, loss_ref, dh_ref, dg_ref):
        i = pl.program_id(0)
        x = h_ref[...]
        gv = g_ref[...]
        rstd = lax.rsqrt(jnp.mean(x * x, axis=-1, keepdims=True) + NORM_EPS)
        xhat = x * rstd
        err = xhat * gv - t_ref[...]
        dy = err * (1.0 / d)
        dxh = dy * gv
        dh_ref[...] = rstd * (dxh - xhat * jnp.mean(dxh * xhat, axis=-1, keepdims=True))
        part = jnp.sum(dy * xhat, axis=0, keepdims=True)
        lpart = jnp.broadcast_to(jnp.sum(jnp.sum(err * err, axis=0, keepdims=True), axis=1, keepdims=True) * (0.5 / d), (1, LANES))

        @pl.when(i == 0)
        def _():
            dg_ref[...] = part
            loss_ref[...] = lpart

        @pl.when(i > 0)
        def _():
            dg_ref[...] += part
            loss_ref[...] += lpart

    row = pl.BlockSpec((ts, d), lambda i: (i, 0))
    vec = pl.BlockSpec((1, d), lambda i: (0, 0))
    return pl.pallas_call(
        body, name="loss_and_grad", grid=(s // ts,), in_specs=[row, vec, row],
        out_specs=[pl.BlockSpec((1, LANES), lambda i: (0, 0)), row, vec],
        out_shape=[jax.ShapeDtypeStruct((1, LANES), F32), jax.ShapeDtypeStruct((s, d), F32), jax.ShapeDtypeStruct((1, d), F32)],
        compiler_params=_cparams(),
    )(h, g, tgt)


def _mm_rms_bwd(name, a, b, h, g, dh_in, *, tm, tk, side=None):
    (m, kk), (n, kb) = a.shape, b.shape
    assert kk == kb and h.shape == (m, n), (name, a.shape, b.shape, h.shape)
    tm, tk = min(tm, m), min(tk, kk)
    nk = kk // tk
    dims = (((1,), (1,)), ((), ()))

    def body(a_ref, b_ref, h_ref, g_ref, dhin_ref, dh_ref, dg_ref, *acc):
        i, k = pl.program_id(0), pl.program_id(2)
        prod = lax.dot_general(a_ref[...].astype(BF16), b_ref[...].astype(BF16), dims, preferred_element_type=F32)

        def finish(dhn):
            x = h_ref[...]
            rstd = lax.rsqrt(jnp.mean(x * x, axis=-1, keepdims=True) + NORM_EPS)
            xhat = x * rstd
            dxh = dhn * g_ref[...]
            dh_ref[...] = dhin_ref[...] + rstd * (dxh - xhat * jnp.mean(dxh * xhat, axis=-1, keepdims=True))
            part = jnp.sum(dhn * xhat, axis=0, keepdims=True)

            @pl.when(i == 0)
            def _():
                dg_ref[...] = part

            @pl.when(i > 0)
            def _():
                dg_ref[...] += part

        if nk == 1:
            finish(prod)
        else:
            acc_ref = acc[0]

            @pl.when(k == 0)
            def _():
                acc_ref[...] = prod

            @pl.when(k > 0)
            def _():
                acc_ref[...] += prod

            @pl.when(k == nk - 1)
            def _():
                finish(acc_ref[...])

    row = pl.BlockSpec((tm, n), lambda i, j, k: (i, 0))
    vec = pl.BlockSpec((1, n), lambda i, j, k: (0, 0))
    ni = m // tm
    return _call_with_side(
        body, side, lambda: (pl.program_id(0) == 0) & (pl.program_id(2) == 0),
        lambda: (pl.program_id(0) == ni - 1) & (pl.program_id(2) == nk - 1),
        name=name, grid=(ni, 1, nk),
        in_specs=[a.spec(tm, tk, lambda i, j, k: (i, k)), b.spec(n, tk, lambda i, j, k: (0, k)), row, vec, row],
        out_specs=[row, vec], out_shape=[jax.ShapeDtypeStruct((m, n), F32), jax.ShapeDtypeStruct((1, n), F32)],
        scratch_shapes=[pltpu.VMEM((tm, n), F32)] if nk > 1 else [], args=(a.arr, b.arr, h, g, dh_in))


def _ffn_up_act(hn2, w_up4, conv_w2, conv_b2, ts=1024, tn=512, sub=1024, side=None):
    s, d = hn2.shape
    p, _, wc = w_up4.shape
    f = p * wc // 2
    ts, tn = min(ts, s), min(tn, wc)
    sub = min(sub, ts)
    per = wc // tn
    kw = FFN_CONV_W
    g0, g1 = f // tn, s // ts

    def body(hn_ref, w1_ref, w2_ref, cw_ref, cb_ref, up_ref, c_ref, act_ref, carry_ref):
        @pl.when(pl.program_id(1) == 0)
        def _():
            carry_ref[...] = jnp.zeros_like(carry_ref)

        for q in range(ts // sub):
            rows = slice(q * sub, (q + 1) * sub)
            hn = hn_ref[rows, :]
            cs = []
            for h, w_ref in enumerate((w1_ref, w2_ref)):
                x = jnp.dot(hn, w_ref[...], preferred_element_type=F32)
                up_ref[h, rows, :] = x
                halo = carry_ref[h]
                c = cb_ref[h] + cw_ref[h, kw - 1:kw, :] * x
                for sft in range(1, kw):
                    c = c + cw_ref[h, kw - 1 - sft:kw - sft, :] * _shift_down(x, halo, sft)
                carry_ref[h] = x[sub - SUBLANES:, :]
                c_ref[h, rows, :] = c
                cs.append(c)
            act_ref[rows, :] = (_gelu(cs[0]) * cs[1]).astype(BF16)

    outs, side_outs = _call_with_side(
        body, side, lambda: (pl.program_id(0) == 0) & (pl.program_id(1) == 0),
        lambda: (pl.program_id(0) == g0 - 1) & (pl.program_id(1) == g1 - 1),
        name="ffn_up_act", grid=(g0, g1),
        in_specs=[pl.BlockSpec((ts, d), lambda j, i: (i, 0)),
                  pl.BlockSpec((None, d, tn), lambda j, i: (j // per, 0, j % per)),
                  pl.BlockSpec((None, d, tn), lambda j, i: (p // 2 + j // per, 0, j % per)),
                  pl.BlockSpec((2, kw, tn), lambda j, i: (0, 0, j)),
                  pl.BlockSpec((2, 1, tn), lambda j, i: (0, 0, j))],
        out_specs=[pl.BlockSpec((2, ts, tn), lambda j, i: (0, i, j)), pl.BlockSpec((2, ts, tn), lambda j, i: (0, i, j)),
                   pl.BlockSpec((ts, tn), lambda j, i: (i, j))],
        out_shape=[jax.ShapeDtypeStruct((2, s, f), F32), jax.ShapeDtypeStruct((2, s, f), F32), jax.ShapeDtypeStruct((s, f), BF16)],
        scratch_shapes=[pltpu.VMEM((2, SUBLANES, tn), F32)], args=(hn2, w_up4, w_up4, conv_w2, conv_b2))
    return outs, side_outs


def _ffn_bwd_fused(dh, w_down, up2, c2, conv_w2, ts=1024, tn=512, side=None):
    s, d = dh.shape
    _, _, f = up2.shape
    ts, tn = min(ts, s), min(tn, f)
    kw = FFN_CONV_W
    nt = s // ts
    hb = ts // SUBLANES
    g0 = f // tn
    nt_dims = (((1,), (1,)), ((), ()))

    def body(dh_ref, wd_ref, up_ref, c_ref, w_ref, dup_ref, dw_ref, db_ref, carry_ref):
        i = pl.program_id(1)
        first_step = i == 0

        @pl.when(first_step)
        def _():
            carry_ref[...] = jnp.zeros_like(carry_ref)

        da = lax.dot_general(dh_ref[...].astype(BF16), wd_ref[...], nt_dims, preferred_element_type=F32)
        g1, dg1 = _gelu_and_grad(c_ref[0])
        dcs = [da * c_ref[1] * dg1, da * g1]
        for h in range(2):
            dc = dcs[h]
            after = carry_ref[h]
            ups = [dc] + [_shift_up(dc, after, sft) for sft in range(1, kw)]
            dup = w_ref[h, kw - 1:kw, :] * dc
            for sft in range(1, kw):
                dup = dup + w_ref[h, kw - 1 - sft:kw - sft, :] * ups[sft]
            carry_ref[h] = dc[:SUBLANES]
            dup_ref[h] = dup.astype(BF16)
            dbp = jnp.sum(dc, axis=0, keepdims=True)
            x = up_ref[h]
            dwp = [jnp.sum(ups[kw - 1 - k] * x, axis=0, keepdims=True) for k in range(kw)]

            @pl.when(first_step)
            def _():
                db_ref[h] = dbp
                for k in range(kw):
                    dw_ref[h, k:k + 1, :] = dwp[k]

            @pl.when(i > 0)
            def _():
                db_ref[h] += dbp
                for k in range(kw):
                    dw_ref[h, k:k + 1, :] += dwp[k]

    rev = lambda i: nt - 1 - i
    return _call_with_side(
        body, side, lambda: (pl.program_id(0) == 0) & (pl.program_id(1) == 0),
        lambda: (pl.program_id(0) == g0 - 1) & (pl.program_id(1) == nt - 1),
        name="ffn_bwd", grid=(g0, nt),
        in_specs=[pl.BlockSpec((ts, d), lambda j, i: (rev(i), 0)),
                  pl.BlockSpec((tn, d), lambda j, i: (j, 0)),
                  pl.BlockSpec((2, ts, tn), lambda j, i: (0, rev(i), j)),
                  pl.BlockSpec((2, ts, tn), lambda j, i: (0, rev(i), j)),
                  pl.BlockSpec((2, kw, tn), lambda j, i: (0, 0, j))],
        out_specs=[pl.BlockSpec((2, ts, tn), lambda j, i: (0, rev(i), j)),
                   pl.BlockSpec((2, kw, tn), lambda j, i: (0, 0, j)),
                   pl.BlockSpec((2, 1, tn), lambda j, i: (0, 0, j))],
        out_shape=[jax.ShapeDtypeStruct((2, s, f), BF16), jax.ShapeDtypeStruct((2, kw, f), F32),
                   jax.ShapeDtypeStruct((2, 1, f), F32)],
        scratch_shapes=[pltpu.VMEM((2, SUBLANES, tn), F32)], args=(dh, w_down, up2, c2, conv_w2))


def _rg_gates(xr, wa_ref, ba_ref, wx_ref, bx_ref, lam_ref):
    bw = wa_ref.shape[-1]
    xb = xr.astype(BF16)
    za = jnp.concatenate([jnp.dot(xb[:, h * bw:(h + 1) * bw], wa_ref[h], preferred_element_type=F32)
                          for h in range(RG_HEADS)], axis=1) + ba_ref[...]
    zx = jnp.concatenate([jnp.dot(xb[:, h * bw:(h + 1) * bw], wx_ref[h], preferred_element_type=F32)
                          for h in range(RG_HEADS)], axis=1) + bx_ref[...]
    r, ig = _sigmoid(za), _sigmoid(zx)
    sp = _softplus(-lam_ref[...])
    la = -RG_C * r * sp
    a = jnp.exp(la)
    mult = jnp.sqrt(_neg_expm1(2.0 * la))
    return xb, r, ig, sp, a, mult


def _rg_fwd(xg2, conv_w, conv_b, w_a, b_a, w_x, b_x, lam, ts=256, side=None):
    _, s, c = xg2.shape
    ts = min(ts, s)
    kw = RG_CONV_W
    hb = ts // SUBLANES

    def body(xg_ref, halo_ref, cw_ref, cb_ref, wa_ref, ba_ref, wx_ref, bx_ref, lam_ref, xr_ref, hs_ref, y_ref, gt_ref, carry_ref,
             a_scr, b_scr):
        i = pl.program_id(0)

        @pl.when(i == 0)
        def _():
            carry_ref[...] = jnp.zeros_like(carry_ref)

        xp = xg_ref[0]
        halo = jnp.where(i == 0, 0.0, halo_ref[...])
        xr = cb_ref[...] + cw_ref[kw - 1:kw, :] * xp
        for sft in range(1, kw):
            xr = xr + cw_ref[kw - 1 - sft:kw - sft, :] * _shift_down(xp, halo, sft)
        _, r, ig, sp, a, mult = _rg_gates(xr, wa_ref, ba_ref, wx_ref, bx_ref, lam_ref)
        for k, val in enumerate((r, ig, a, mult)):
            gt_ref[k] = val
        a_scr[...] = a
        b_scr[...] = mult * (ig * xr)
        _real_slab_scan(a_scr, b_scr, hs_ref, carry_ref, reverse=False)
        xr_ref[...] = xr
        y_ref[...] = (hs_ref[...] * _gelu(xg_ref[1])).astype(BF16)

    full = lambda shape: pl.BlockSpec(shape, lambda i: (0,) * len(shape))
    row_spec = pl.BlockSpec((ts, c), lambda i: (i, 0))
    nt = s // ts
    return _call_with_side(
        body, side, lambda: pl.program_id(0) == 0, lambda: pl.program_id(0) == nt - 1,
        name="rg_fwd", grid=(nt,),
        in_specs=[pl.BlockSpec((2, ts, c), lambda i: (0, i, 0)),
                  pl.BlockSpec((None, SUBLANES, c), lambda i: (0, jnp.maximum(i * hb - 1, 0), 0)),
                  full(conv_w.shape), full(conv_b.shape), full(w_a.shape), full(b_a.shape), full(w_x.shape), full(b_x.shape),
                  full(lam.shape)],
        out_specs=[row_spec, row_spec, row_spec, pl.BlockSpec((4, ts, c), lambda i: (0, i, 0))],
        out_shape=[jax.ShapeDtypeStruct((s, c), F32), jax.ShapeDtypeStruct((s, c), F32), jax.ShapeDtypeStruct((s, c), BF16),
                   jax.ShapeDtypeStruct((4, s, c), F32)],
        scratch_shapes=[pltpu.VMEM((1, c), F32), pltpu.VMEM((ts, c), F32), pltpu.VMEM((ts, c), F32)],
        args=(xg2, xg2, conv_w, conv_b, w_a, b_a, w_x, b_x, lam))


def _rg_bwd(dy, xg2, xr, hs, gates, conv_w, w_a, b_a, w_x, b_x, lam, ts=256, side=None):
    _, s, c = xg2.shape
    ts = min(ts, s)
    nt = s // ts
    kw = RG_CONV_W
    hb = ts // SUBLANES
    bw = c // RG_HEADS
    tn_dims = (((0,), (0,)), ((), ()))
    nt_dims = (((1,), (1,)), ((), ()))

    def body(dy_ref, xg_ref, xph_ref, xr_ref, hs_ref, hsh_ref, gt_ref, cw_ref, wa_ref, ba_ref, wx_ref, bx_ref, lam_ref,
             dxg_ref, dcw_ref, dcb_ref, dwa_ref, dba_ref, dwx_ref, dbx_ref, dlam_ref,
             lam_carry, a_carry, dxr_carry, dsp_acc, a_scr, b_scr):
        i = pl.program_id(0)
        first_step = i == 0
        time_first = i == nt - 1

        @pl.when(first_step)
        def _():
            lam_carry[...] = jnp.zeros_like(lam_carry)
            a_carry[...] = jnp.ones_like(a_carry)
            dxr_carry[...] = jnp.zeros_like(dxr_carry)
            dsp_acc[...] = jnp.zeros_like(dsp_acc)
            for ref in (dcw_ref, dcb_ref, dwa_ref, dba_ref, dwx_ref, dbx_ref):
                ref[...] = jnp.zeros_like(ref)

        xr = xr_ref[...]
        hs = hs_ref[...]
        gate = xg_ref[1]
        xb = xr.astype(BF16)
        r, ig, a, mult = gt_ref[0], gt_ref[1], gt_ref[2], gt_ref[3]
        sp = _softplus(-lam_ref[...])
        dyv = dy_ref[...]
        gg, dgg = _gelu_and_grad(gate)
        dhs = dyv * gg
        dxg_ref[1] = (dyv * hs * dgg).astype(BF16)
        row = _rows(xr.shape)
        a_scr[...] = jnp.where(row == ts - 1, a_carry[0:1, :], pltpu.roll(a, ts - 1, 0))
        b_scr[...] = dhs
        _real_slab_scan(a_scr, b_scr, b_scr, lam_carry, reverse=True)
        lmb = b_scr[...]
        a_carry[...] = a[:SUBLANES]
        hs_prev = _shift_down(hs, jnp.where(time_first, 0.0, hsh_ref[...]), 1)
        d_a = lmb * hs_prev
        d_m = lmb * (ig * xr)
        d_ig = lmb * mult * xr
        d_xr = lmb * mult * ig
        d_la = a * d_a - (a * a / mult) * d_m
        dsp_acc[...] += jnp.sum(-RG_C * r * d_la, axis=0, keepdims=True)
        d_za = (-RG_C * sp) * d_la * r * (1.0 - r)
        d_zx = d_ig * ig * (1.0 - ig)
        dba_ref[...] += jnp.sum(d_za, axis=0, keepdims=True)
        dbx_ref[...] += jnp.sum(d_zx, axis=0, keepdims=True)
        dzab, dzxb = d_za.astype(BF16), d_zx.astype(BF16)
        back = []
        for h in range(RG_HEADS):
            sl = slice(h * bw, (h + 1) * bw)
            dwa_ref[h] += lax.dot_general(xb[:, sl], dzab[:, sl], tn_dims, preferred_element_type=F32)
            dwx_ref[h] += lax.dot_general(xb[:, sl], dzxb[:, sl], tn_dims, preferred_element_type=F32)
            back.append(lax.dot_general(dzab[:, sl], wa_ref[h], nt_dims, preferred_element_type=F32)
                        + lax.dot_general(dzxb[:, sl], wx_ref[h], nt_dims, preferred_element_type=F32))
        d_xr = d_xr + jnp.concatenate(back, axis=1)
        d_xp = cw_ref[kw - 1:kw, :] * d_xr
        after = dxr_carry[...]
        for sft in range(1, kw):
            d_xp = d_xp + cw_ref[kw - 1 - sft:kw - sft, :] * _shift_up(d_xr, after, sft)
        dxr_carry[...] = d_xr[:SUBLANES]
        dxg_ref[0] = d_xp.astype(BF16)
        xp = xg_ref[0]
        before = jnp.where(time_first, 0.0, xph_ref[...])
        dcb_ref[...] += jnp.sum(d_xr, axis=0, keepdims=True)
        dcw_ref[kw - 1:kw, :] += jnp.sum(d_xr * xp, axis=0, keepdims=True)
        for sft in range(1, kw):
            dcw_ref[kw - 1 - sft:kw - sft, :] += jnp.sum(d_xr * _shift_down(xp, before, sft), axis=0, keepdims=True)
        dlam_ref[...] = dsp_acc[...] * (-_sigmoid(-lam_ref[...]))

    full = lambda shape: pl.BlockSpec(shape, lambda i: (0,) * len(shape))
    rev = lambda i: nt - 1 - i
    row_spec = pl.BlockSpec((ts, c), lambda i: (rev(i), 0))
    halo_idx = lambda i: jnp.maximum(rev(i) * hb - 1, 0)
    vec = (1, c)
    return _call_with_side(
        body, side, lambda: pl.program_id(0) == 0, lambda: pl.program_id(0) == nt - 1,
        name="rg_bwd", grid=(nt,),
        in_specs=[row_spec,
                  pl.BlockSpec((2, ts, c), lambda i: (0, rev(i), 0)),
                  pl.BlockSpec((None, SUBLANES, c), lambda i: (0, halo_idx(i), 0)),
                  row_spec, row_spec,
                  pl.BlockSpec((SUBLANES, c), lambda i: (halo_idx(i), 0)),
                  pl.BlockSpec((4, ts, c), lambda i: (0, rev(i), 0)),
                  full(conv_w.shape), full(w_a.shape), full(b_a.shape), full(w_x.shape), full(b_x.shape), full(lam.shape)],
        out_specs=[pl.BlockSpec((2, ts, c), lambda i: (0, rev(i), 0)), full(conv_w.shape), full(vec), full(w_a.shape), full(vec),
                   full(w_x.shape), full(vec), full(vec)],
        out_shape=[jax.ShapeDtypeStruct((2, s, c), BF16), jax.ShapeDtypeStruct(conv_w.shape, F32), jax.ShapeDtypeStruct(vec, F32),
                   jax.ShapeDtypeStruct(w_a.shape, F32), jax.ShapeDtypeStruct(vec, F32), jax.ShapeDtypeStruct(w_x.shape, F32),
                   jax.ShapeDtypeStruct(vec, F32), jax.ShapeDtypeStruct(vec, F32)],
        scratch_shapes=[pltpu.VMEM(vec, F32), pltpu.VMEM((SUBLANES, c), F32), pltpu.VMEM((SUBLANES, c), F32),
                        pltpu.VMEM(vec, F32), pltpu.VMEM((ts, c), F32), pltpu.VMEM((ts, c), F32)],
        args=(dy, xg2, xg2, xr, hs, hs, gates, conv_w, w_a, b_a, w_x, b_x, lam))


def _s5_param_fn(a_re, a_im, log_dt, bt_re, bt_im):
    dt = jnp.exp(log_dt)
    mag = jnp.exp(a_re * dt)
    abr = mag * jnp.cos(a_im * dt)
    abi = mag * jnp.sin(a_im * dt)
    ur, ui = abr - 1.0, abi
    den = a_re * a_re + a_im * a_im
    wr = (ur * a_re + ui * a_im) / den
    wi = (ui * a_re - ur * a_im) / den
    bbr = wr[None] * bt_re - wi[None] * bt_im
    bbi = wr[None] * bt_im + wi[None] * bt_re
    return abr, abi, bbr, bbi


def _s5_params_bwd(a_re, a_im, log_dt, bt_re, bt_im, d_abr, d_abi, d_bbr, d_bbi):
    def body(ar_ref, ai_ref, dt_ref, br_ref, bi_ref, g0, g1, g2, g3, o0, o1, o2, o3, o4):
        _, vjp = jax.vjp(_s5_param_fn, ar_ref[...], ai_ref[...], dt_ref[...], br_ref[...], bi_ref[...])
        outs = vjp((g0[...], g1[...], g2[...], g3[...]))
        for o, v in zip((o0, o1, o2, o3, o4), outs):
            o[...] = v

    sd = jax.ShapeDtypeStruct
    return pl.pallas_call(
        body, name="s5_params_bwd",
        out_shape=[sd(a_re.shape, F32), sd(a_im.shape, F32), sd(log_dt.shape, F32), sd(bt_re.shape, F32), sd(bt_im.shape, F32)],
    )(a_re, a_im, log_dt, bt_re, bt_im, d_abr, d_abi, d_bbr, d_bbi)


S5_LANE_CHUNK = 512


def _cmul_add(br, bi, tr, ti, sr, si):
    return br + tr * sr - ti * si, bi + tr * si + ti * sr


def _s5_tables3(a_re, a_im, log_dt, bt_re, bt_im, seg):
    g, p = a_re.shape
    gc = bt_re.shape[0]
    nsq = int(math.log2(seg))
    assert 1 << nsq == seg

    def body(ar_ref, ai_ref, dt_ref, br_ref, bi_ref, tr_ref, ti_ref, rtr_ref, rti_ref, bbr_ref, bbi_ref):
        abr, abi, bbr, bbi = _s5_param_fn(ar_ref[...], ai_ref[...], dt_ref[...], br_ref[...], bi_ref[...])
        bbr_ref[...] = bbr
        bbi_ref[...] = bbi
        qr, qi = abr, abi
        for _ in range(nsq):
            qr, qi = qr * qr - qi * qi, 2.0 * qr * qi
        pows = [(qr, qi)]
        for _ in range(1, SUBLANES):
            cr, ci = pows[-1]
            pows.append((cr * qr - ci * qi, cr * qi + ci * qr))
        zero = jnp.zeros_like(abr)
        for r in range(SUBLANES):
            rows = [(pows[(1 << k) - 1] if r >= (1 << k) else (zero, zero)) for k in range(3)] + [pows[r], (abr, abi)]
            for k, (vr, vi) in enumerate(rows):
                tr_ref[k, r] = vr
                ti_ref[k, r] = vi
                rtr_ref[k, SUBLANES - 1 - r] = vr
                rti_ref[k, SUBLANES - 1 - r] = -vi

    sd = jax.ShapeDtypeStruct
    tab = sd((5, SUBLANES, g, p), F32)
    return pl.pallas_call(
        body, name="s5_tables", out_shape=[tab, tab, tab, tab, sd((gc, g, p), F32), sd((gc, g, p), F32)],
    )(a_re, a_im, log_dt, bt_re, bt_im)


def _segment_perm(ts):
    seg = ts // SUBLANES
    rho = jnp.arange(ts)
    src = (rho % SUBLANES) * seg + rho // SUBLANES
    return (src[:, None] == jnp.arange(ts)[None, :]).astype(BF16)


def _exact_rows(perm_t, x):
    hi = x.astype(BF16)
    r1 = x - hi.astype(F32)
    mid = r1.astype(BF16)
    lo = (r1 - mid.astype(F32)).astype(BF16)
    dot = lambda v: jnp.dot(perm_t, v, preferred_element_type=F32)
    return (dot(hi) + dot(mid)) + dot(lo)


def _s5_fwd3(u, perm, perm_t, tab_r, tab_i, bp_r, bp_i, cp_r, cp_i, dvec, ts=256, side=None):
    s, c = u.shape
    n = tab_r.shape[2]
    nblk, cb, nb = bp_r.shape
    ts = min(ts, s)
    seg = ts // SUBLANES
    lc = min(S5_LANE_CHUNK, n)

    def body(u_ref, p_ref, pt_ref, tr_ref, ti_ref, bpr_ref, bpi_ref, cpr_ref, cpi_ref, d_ref, hr_ref, hi_ref, yp_ref, gy_ref,
             bur_ref, bui_ref, car_r, car_i):
        i = pl.program_id(0)

        @pl.when(i == 0)
        def _():
            car_r[...] = jnp.zeros_like(car_r)
            car_i[...] = jnp.zeros_like(car_i)

        uv = u_ref[...]
        ubp = jnp.dot(p_ref[...], uv.astype(BF16), preferred_element_type=F32).astype(BF16)
        for k in range(nblk):
            bur_ref[:, k * nb:(k + 1) * nb] = jnp.dot(ubp[:, k * cb:(k + 1) * cb], bpr_ref[k], preferred_element_type=F32)
            bui_ref[:, k * nb:(k + 1) * nb] = jnp.dot(ubp[:, k * cb:(k + 1) * cb], bpi_ref[k], preferred_element_type=F32)
        row8 = _rows((SUBLANES, lc))
        for q in range(n // lc):
            sl = slice(q * lc, (q + 1) * lc)
            tabs = [(tr_ref[k, :, sl], ti_ref[k, :, sl]) for k in range(5)]
            a_r, a_i = tabs[4]

            def local(r, carry, sl=sl, a_r=a_r, a_i=a_i):
                r0 = pl.multiple_of(r * SUBLANES, SUBLANES)
                hr, hi = _cmul_add(bur_ref[pl.ds(r0, SUBLANES), sl], bui_ref[pl.ds(r0, SUBLANES), sl], a_r, a_i, carry[0], carry[1])
                hr_ref[pl.ds(r0, SUBLANES), sl] = hr
                hi_ref[pl.ds(r0, SUBLANES), sl] = hi
                return hr, hi

            zero = jnp.zeros((SUBLANES, lc), F32)
            er, ei = lax.fori_loop(0, seg, local, (zero, zero), unroll=4)
            for k in range(3):
                sh = 1 << k
                er, ei = _cmul_add(er, ei, tabs[k][0], tabs[k][1], pltpu.roll(er, sh, 0), pltpu.roll(ei, sh, 0))
            cin_r, cin_i = jnp.broadcast_to(car_r[:, sl], er.shape), jnp.broadcast_to(car_i[:, sl], ei.shape)
            er, ei = _cmul_add(er, ei, tabs[3][0], tabs[3][1], cin_r, cin_i)
            car_r[:, sl] = er[SUBLANES - 1:, :]
            car_i[:, sl] = ei[SUBLANES - 1:, :]
            c_r = jnp.where(row8 == 0, cin_r, pltpu.roll(er, 1, 0))
            c_i = jnp.where(row8 == 0, cin_i, pltpu.roll(ei, 1, 0))

            def fix(r, carry, sl=sl, a_r=a_r, a_i=a_i, c_r=c_r, c_i=c_i):
                pr, pi = carry
                r0 = pl.multiple_of(r * SUBLANES, SUBLANES)
                hr, hi = _cmul_add(hr_ref[pl.ds(r0, SUBLANES), sl], hi_ref[pl.ds(r0, SUBLANES), sl], pr, pi, c_r, c_i)
                hr_ref[pl.ds(r0, SUBLANES), sl] = hr
                hi_ref[pl.ds(r0, SUBLANES), sl] = hi
                return pr * a_r - pi * a_i, pr * a_i + pi * a_r

            lax.fori_loop(0, seg, fix, (a_r, a_i), unroll=4)
        hrb, hib = hr_ref[...].astype(BF16), hi_ref[...].astype(BF16)
        y = jnp.concatenate([jnp.dot(hrb[:, k * nb:(k + 1) * nb], cpr_ref[k], preferred_element_type=F32)
                             - jnp.dot(hib[:, k * nb:(k + 1) * nb], cpi_ref[k], preferred_element_type=F32) for k in range(nblk)], axis=1)
        yp = _exact_rows(pt_ref[...], y) + d_ref[...] * uv
        yp_ref[...] = yp
        gy_ref[...] = _gelu(yp).astype(BF16)

    full = lambda shape: pl.BlockSpec(shape, lambda i: (0,) * len(shape))
    rc = pl.BlockSpec((ts, c), lambda i: (i, 0))
    rn = pl.BlockSpec((ts, n), lambda i: (i, 0))
    sd = jax.ShapeDtypeStruct
    nt = s // ts
    return _call_with_side(
        body, side, lambda: pl.program_id(0) == 0, lambda: pl.program_id(0) == nt - 1,
        name="s5_fwd", grid=(nt,),
        in_specs=[rc, full(perm.shape), full(perm_t.shape), full(tab_r.shape), full(tab_i.shape), full(bp_r.shape), full(bp_i.shape),
                  full(cp_r.shape), full(cp_i.shape), full(dvec.shape)],
        out_specs=[rn, rn, rc, rc],
        out_shape=[sd((s, n), F32), sd((s, n), F32), sd((s, c), F32), sd((s, c), BF16)],
        scratch_shapes=[pltpu.VMEM((ts, n), F32), pltpu.VMEM((ts, n), F32), pltpu.VMEM((1, n), F32), pltpu.VMEM((1, n), F32)],
        args=(u, perm, perm_t, tab_r, tab_i, bp_r, bp_i, cp_r, cp_i, dvec))


def _s5_bwd3(dgy, ypre, u, hr, hi, perm, perm_t, rtab_r, rtab_i, bp_r, bp_i, cp_r, cp_i, dvec, ts=256, side=None):
    s, c = u.shape
    n = rtab_r.shape[2]
    nblk, cb, nb = bp_r.shape
    ts = min(ts, s)
    nt = s // ts
    hb = ts // SUBLANES
    seg = ts // SUBLANES
    lc = min(S5_LANE_CHUNK, n)
    tn_dims = (((0,), (0,)), ((), ()))
    nt_dims = (((1,), (1,)), ((), ()))

    def body(dgy_ref, yp_ref, u_ref, hr_ref, hi_ref, hrh_ref, hih_ref, p_ref, pt_ref, tr_ref, ti_ref, bpr_ref, bpi_ref,
             cpr_ref, cpi_ref, d_ref, du_ref, dar_ref, dai_ref, dbr_ref, dbi_ref, dcr_ref, dci_ref, dd_ref, lr_ref, li_ref,
             car_r, car_i):
        i = pl.program_id(0)
        time_first = i == nt - 1

        @pl.when(i == 0)
        def _():
            car_r[...] = jnp.zeros_like(car_r)
            car_i[...] = jnp.zeros_like(car_i)
            for ref in (dar_ref, dai_ref, dbr_ref, dbi_ref, dcr_ref, dci_ref, dd_ref):
                ref[...] = jnp.zeros_like(ref)

        uv = u_ref[...]
        _, dgel = _gelu_and_grad(yp_ref[...])
        dyv = dgy_ref[...] * dgel
        dd_ref[...] += jnp.sum(dyv * uv, axis=0, keepdims=True)
        perm_m = p_ref[...]
        dyb = jnp.dot(perm_m, dyv.astype(BF16), preferred_element_type=F32).astype(BF16)
        ub = jnp.dot(perm_m, uv.astype(BF16), preferred_element_type=F32).astype(BF16)
        hrb, hib = hr_ref[...].astype(BF16), hi_ref[...].astype(BF16)
        for k in range(nblk):
            dblk = dyb[:, k * cb:(k + 1) * cb]
            lr_ref[:, k * nb:(k + 1) * nb] = lax.dot_general(dblk, cpr_ref[k], nt_dims, preferred_element_type=F32)
            li_ref[:, k * nb:(k + 1) * nb] = -lax.dot_general(dblk, cpi_ref[k], nt_dims, preferred_element_type=F32)
            dcr_ref[k] += lax.dot_general(hrb[:, k * nb:(k + 1) * nb], dblk, tn_dims, preferred_element_type=F32)
            dci_ref[k] += lax.dot_general(hib[:, k * nb:(k + 1) * nb], dblk, tn_dims, preferred_element_type=F32)
        row8 = _rows((SUBLANES, lc))
        last0 = (seg - 1) * SUBLANES
        for q in range(n // lc):
            sl = slice(q * lc, (q + 1) * lc)
            tabs = [(tr_ref[k, :, sl], ti_ref[k, :, sl]) for k in range(5)]
            a_r, a_i = tabs[4]

            def local(rr, carry, sl=sl, a_r=a_r, a_i=a_i):
                r0 = pl.multiple_of((seg - 1 - rr) * SUBLANES, SUBLANES)
                lr, li = _cmul_add(lr_ref[pl.ds(r0, SUBLANES), sl], li_ref[pl.ds(r0, SUBLANES), sl], a_r, a_i, carry[0], carry[1])
                lr_ref[pl.ds(r0, SUBLANES), sl] = lr
                li_ref[pl.ds(r0, SUBLANES), sl] = li
                return lr, li

            zero = jnp.zeros((SUBLANES, lc), F32)
            er, ei = lax.fori_loop(0, seg, local, (zero, zero), unroll=4)
            for k in range(3):
                sh = 1 << k
                er, ei = _cmul_add(er, ei, tabs[k][0], tabs[k][1], pltpu.roll(er, SUBLANES - sh, 0), pltpu.roll(ei, SUBLANES - sh, 0))
            cin_r, cin_i = jnp.broadcast_to(car_r[:, sl], er.shape), jnp.broadcast_to(car_i[:, sl], ei.shape)
            er, ei = _cmul_add(er, ei, tabs[3][0], tabs[3][1], cin_r, cin_i)
            car_r[:, sl] = er[:1, :]
            car_i[:, sl] = ei[:1, :]
            c_r = jnp.where(row8 == SUBLANES - 1, cin_r, pltpu.roll(er, SUBLANES - 1, 0))
            c_i = jnp.where(row8 == SUBLANES - 1, cin_i, pltpu.roll(ei, SUBLANES - 1, 0))
            halo_r = jnp.where(time_first, 0.0, hrh_ref[SUBLANES - 1:, sl])
            halo_i = jnp.where(time_first, 0.0, hih_ref[SUBLANES - 1:, sl])
            hp0_r = jnp.where(row8 == 0, jnp.broadcast_to(halo_r, zero.shape), pltpu.roll(hr_ref[pl.ds(last0, SUBLANES), sl], 1, 0))
            hp0_i = jnp.where(row8 == 0, jnp.broadcast_to(halo_i, zero.shape), pltpu.roll(hi_ref[pl.ds(last0, SUBLANES), sl], 1, 0))

            def fix(rr, carry, sl=sl, a_r=a_r, a_i=a_i, c_r=c_r, c_i=c_i, hp0_r=hp0_r, hp0_i=hp0_i):
                pr, pi, acc_r, acc_i = carry
                r = seg - 1 - rr
                r0 = pl.multiple_of(r * SUBLANES, SUBLANES)
                lr, li = _cmul_add(lr_ref[pl.ds(r0, SUBLANES), sl], li_ref[pl.ds(r0, SUBLANES), sl], pr, pi, c_r, c_i)
                lr_ref[pl.ds(r0, SUBLANES), sl] = lr
                li_ref[pl.ds(r0, SUBLANES), sl] = li
                p0 = pl.multiple_of(jnp.maximum(r - 1, 0) * SUBLANES, SUBLANES)
                hpr = jnp.where(r == 0, hp0_r, hr_ref[pl.ds(p0, SUBLANES), sl])
                hpi = jnp.where(r == 0, hp0_i, hi_ref[pl.ds(p0, SUBLANES), sl])
                return (pr * a_r - pi * a_i, pr * a_i + pi * a_r, acc_r + (lr * hpr + li * hpi), acc_i + (li * hpr - lr * hpi))

            _, _, acc_r, acc_i = lax.fori_loop(0, seg, fix, (a_r, a_i, zero, zero), unroll=4)
            dar_ref[:, sl] += jnp.sum(acc_r, axis=0, keepdims=True)
            dai_ref[:, sl] += jnp.sum(acc_i, axis=0, keepdims=True)
        lrb, lib = lr_ref[...].astype(BF16), li_ref[...].astype(BF16)
        du = []
        for k in range(nblk):
            ublk = ub[:, k * cb:(k + 1) * cb]
            lrk, lik = lrb[:, k * nb:(k + 1) * nb], lib[:, k * nb:(k + 1) * nb]
            dbr_ref[k] += lax.dot_general(ublk, lrk, tn_dims, preferred_element_type=F32)
            dbi_ref[k] += lax.dot_general(ublk, lik, tn_dims, preferred_element_type=F32)
            du.append(lax.dot_general(lrk, bpr_ref[k], nt_dims, preferred_element_type=F32)
                      + lax.dot_general(lik, bpi_ref[k], nt_dims, preferred_element_type=F32))
        du_ref[...] = (d_ref[...] * dyv + _exact_rows(pt_ref[...], jnp.concatenate(du, axis=1))).astype(BF16)

    full = lambda shape: pl.BlockSpec(shape, lambda i: (0,) * len(shape))
    rev = lambda i: nt - 1 - i
    halo_idx = lambda i: jnp.maximum(rev(i) * hb - 1, 0)
    rc = pl.BlockSpec((ts, c), lambda i: (rev(i), 0))
    rn = pl.BlockSpec((ts, n), lambda i: (rev(i), 0))
    hn = pl.BlockSpec((SUBLANES, n), lambda i: (halo_idx(i), 0))
    sd = jax.ShapeDtypeStruct
    vec_n = (1, n)
    return _call_with_side(
        body, side, lambda: pl.program_id(0) == 0, lambda: pl.program_id(0) == nt - 1,
        name="s5_bwd", grid=(nt,),
        in_specs=[rc, rc, rc, rn, rn, hn, hn, full(perm.shape), full(perm_t.shape), full(rtab_r.shape), full(rtab_i.shape),
                  full(bp_r.shape), full(bp_i.shape), full(cp_r.shape), full(cp_i.shape), full(dvec.shape)],
        out_specs=[rc, full(vec_n), full(vec_n), full(bp_r.shape), full(bp_i.shape), full(cp_r.shape), full(cp_i.shape),
                   full(dvec.shape)],
        out_shape=[sd((s, c), BF16), sd(vec_n, F32), sd(vec_n, F32), sd(bp_r.shape, F32), sd(bp_i.shape, F32),
                   sd(cp_r.shape, F32), sd(cp_i.shape, F32), sd(dvec.shape, F32)],
        scratch_shapes=[pltpu.VMEM((ts, n), F32), pltpu.VMEM((ts, n), F32), pltpu.VMEM((1, n), F32), pltpu.VMEM((1, n), F32)],
        args=(dgy, ypre, u, hr, hi, hr, hi, perm, perm_t, rtab_r, rtab_i, bp_r, bp_i, cp_r, cp_i, dvec))


def _glu_mm(gy, w_glu4, ts=1024, tn=512):
    s, d = gy.shape
    p, _, wc = w_glu4.shape
    c = p * wc // 2
    ts, tn = min(ts, s), min(tn, wc)
    per = wc // tn

    def body(x_ref, w1_ref, w2_ref, gl_ref, o_ref):
        x = x_ref[...]
        val = jnp.dot(x, w1_ref[...], preferred_element_type=F32)
        gate = jnp.dot(x, w2_ref[...], preferred_element_type=F32)
        gl_ref[0] = val
        gl_ref[1] = gate
        o_ref[...] = (val * _sigmoid(gate)).astype(BF16)

    return pl.pallas_call(
        body, name="s5_glu", grid=(c // tn, s // ts),
        in_specs=[pl.BlockSpec((ts, d), lambda j, i: (i, 0)),
                  pl.BlockSpec((None, d, tn), lambda j, i: (j // per, 0, j % per)),
                  pl.BlockSpec((None, d, tn), lambda j, i: (p // 2 + j // per, 0, j % per))],
        out_specs=[pl.BlockSpec((2, ts, tn), lambda j, i: (0, i, j)), pl.BlockSpec((ts, tn), lambda j, i: (i, j))],
        out_shape=[jax.ShapeDtypeStruct((2, s, c), F32), jax.ShapeDtypeStruct((s, c), BF16)], compiler_params=_cparams(),
    )(gy, w_glu4, w_glu4)


def _glu_bwd_mm(dh, w_out, gl2, ts=512):
    s, d = dh.shape
    c = w_out.shape[0]
    ts = min(ts, s)
    nt_dims = (((1,), (1,)), ((), ()))

    def body(dh_ref, w_ref, g_ref, o_ref):
        dov = lax.dot_general(dh_ref[...].astype(BF16), w_ref[...], nt_dims, preferred_element_type=F32)
        sg = _sigmoid(g_ref[1])
        o_ref[0] = (dov * sg).astype(BF16)
        o_ref[1] = (dov * g_ref[0] * sg * (1.0 - sg)).astype(BF16)

    blk = pl.BlockSpec((2, ts, c), lambda i: (0, i, 0))
    return pl.pallas_call(
        body, name="s5_out_dx", grid=(s // ts,),
        in_specs=[pl.BlockSpec((ts, d), lambda i: (i, 0)), pl.BlockSpec((c, d), lambda i: (0, 0)), blk],
        out_specs=blk, out_shape=jax.ShapeDtypeStruct((2, s, c), BF16), compiler_params=_cparams(),
    )(dh, w_out, gl2)


PACK_ROW_MULTIPLE = 1024
ELEMENTWISE_BLOCK_ELEMS = 256 * 1024


def _row_tile(rows, cols):
    pref = max(SUBLANES, 1 << int(math.log2(max(1, ELEMENTWISE_BLOCK_ELEMS // cols))))
    if rows <= pref:
        return rows
    t = pref
    while rows % t:
        t //= 2
    assert t >= SUBLANES, rows
    return t


def _sum_parts(rs, side=None):
    nl = len(rs)
    p, rows, cols = rs[0].shape
    tr = _row_tile(rows, cols)
    nt = rows // tr

    def body(*refs):
        o_ref = refs[nl]
        for l in range(nl):
            acc = refs[l][0].astype(F32)
            for k in range(1, p):
                acc = acc + refs[l][k].astype(F32)
            o_ref[l] = acc

    outs, got = _call_with_side(
        body, side, lambda: pl.program_id(0) == 0, lambda: pl.program_id(0) == nt - 1,
        name="sum_parts", grid=(nt,), in_specs=[pl.BlockSpec((p, tr, cols), lambda i: (0, i, 0))] * nl,
        out_specs=[pl.BlockSpec((nl, tr, cols), lambda i: (0, i, 0))], out_shape=[jax.ShapeDtypeStruct((nl, rows, cols), F32)],
        scratch_shapes=[], args=tuple(rs))
    return outs[0], got


def _adamw(w, g_parts, m, v, side=None):
    rows, cols = w.shape
    tr = _row_tile(rows, max(cols, LANES))
    ng = len(g_parts)
    emit_grad = ng > 1
    c1 = 1.0 / (1.0 - ADAM_B1 ** ADAM_STEP)
    c2 = 1.0 / (1.0 - ADAM_B2 ** ADAM_STEP)

    def body(*refs):
        w_ref, m_ref, v_ref = refs[0], refs[1 + ng], refs[2 + ng]
        dl_ref, nm_ref, nv_ref = refs[3 + ng:6 + ng]
        g = refs[1][...]
        for k in range(1, ng):
            g = g + refs[1 + k][...]
        mn = ADAM_B1 * m_ref[...] + (1.0 - ADAM_B1) * g
        vn = ADAM_B2 * v_ref[...] + (1.0 - ADAM_B2) * (g * g)
        if emit_grad:
            refs[6 + ng][...] = g
        nm_ref[...] = mn
        nv_ref[...] = vn
        dl_ref[...] = -ADAM_LR * ((mn * c1) / (jnp.sqrt(vn * c2) + ADAM_EPS) + ADAM_WD * w_ref[...])

    blk = pl.BlockSpec((tr, cols), lambda i: (i, 0))
    sd = jax.ShapeDtypeStruct((rows, cols), F32)
    nout = 4 if emit_grad else 3
    nt = rows // tr
    return _call_with_side(
        body, side, lambda: pl.program_id(0) == 0, lambda: pl.program_id(0) == nt - 1,
        name="adamw", grid=(nt,), in_specs=[blk] * (3 + ng), out_specs=[blk] * nout, out_shape=[sd] * nout,
        scratch_shapes=[], args=(w, *g_parts, m, v))


def _place():
    x, y, c = lax.axis_index("x"), lax.axis_index("y"), lax.axis_index("c")
    chips = [(1 - x, y), (x, 1 - y), (1 - x, 1 - y)]
    return x, y, c, chips


class Side:
    def __init__(self, ins, outs, kind):
        self.ins, self.outs, self.kind = list(ins), list(outs), kind
        n = len(self.ins)
        self.sems = [pltpu.SemaphoreType.DMA((3 * n,)), pltpu.SemaphoreType.DMA((3 * n,)), pltpu.SemaphoreType.DMA((n,))]

    def _copies(self, ins, outs, send, recv, lsem):
        x, y, c, chips = _place()
        me = 2 * x + y
        local, out_going, in_coming = [], [], []
        for t in range(len(ins)):
            if self.kind == 'sibling':
                cp = pltpu.make_async_remote_copy(src_ref=ins[t], dst_ref=outs[t], send_sem=send.at[t], recv_sem=recv.at[t],
                                                  device_id=(x, y, 1 - c), device_id_type=MESH)
                out_going.append(cp)
                in_coming.append(cp)
                continue
            if self.kind == 'gather':
                src_local, srcs, dst_mine = ins[t], [ins[t]] * 3, outs[t].at[me]
            else:
                src_local, srcs, dst_mine = ins[t].at[me], [ins[t].at[2 * px + py] for px, py in chips], outs[t].at[me]
            local.append(pltpu.make_async_copy(src_local, dst_mine, lsem.at[t]))
            for r, (px, py) in enumerate(chips):
                out_going.append(pltpu.make_async_remote_copy(
                    src_ref=srcs[r], dst_ref=dst_mine, send_sem=send.at[3 * t + r], recv_sem=recv.at[3 * t + r],
                    device_id=(px, py, c), device_id_type=MESH))
                in_coming.append(pltpu.make_async_remote_copy(
                    src_ref=srcs[r], dst_ref=outs[t].at[2 * px + py], send_sem=send.at[3 * t + r], recv_sem=recv.at[3 * t + r],
                    device_id=(px, py, c), device_id_type=MESH))
        return local, out_going, in_coming

    def start(self, ins, outs, send, recv, lsem):
        local, out_going, _ = self._copies(ins, outs, send, recv, lsem)
        for cp in local + out_going:
            cp.start()

    def wait(self, ins, outs, send, recv, lsem):
        local, out_going, in_coming = self._copies(ins, outs, send, recv, lsem)
        for cp in in_coming:
            cp.wait_recv()
        for cp in out_going:
            cp.wait_send()
        for cp in local:
            cp.wait()


def _gather_side(shards):
    return Side(shards, [jax.ShapeDtypeStruct((N_CHIPS,) + s.shape, s.dtype) for s in shards], 'gather')


def _scatter_side(grads):
    return Side(grads, [jax.ShapeDtypeStruct(g.shape, g.dtype) for g in grads], 'scatter')


def _sibling_side(arrs):
    return Side(arrs, [jax.ShapeDtypeStruct(a.shape, a.dtype) for a in arrs], 'sibling')


def _call_with_side(body, side, first, last, *, name, grid, in_specs, out_specs, out_shape, scratch_shapes, args):
    if side is None:
        outs = pl.pallas_call(body, name=name, grid=grid, in_specs=in_specs, out_specs=out_specs, out_shape=out_shape,
                              scratch_shapes=scratch_shapes, compiler_params=_cparams())(*args)
        return outs, []
    n_in, n_out, n_sc = len(in_specs), len(out_specs), len(scratch_shapes)
    ns_in, ns_out = len(side.ins), len(side.outs)

    def wrapped(*refs):
        base_in, s_in = refs[:n_in], refs[n_in:n_in + ns_in]
        o0 = n_in + ns_in
        base_out, s_out = refs[o0:o0 + n_out], refs[o0 + n_out:o0 + n_out + ns_out]
        sc0 = o0 + n_out + ns_out
        base_sc, sems = refs[sc0:sc0 + n_sc], refs[sc0 + n_sc:]

        @pl.when(first())
        def _():
            side.start(s_in, s_out, *sems)

        body(*base_in, *base_out, *base_sc)

        @pl.when(last())
        def _():
            side.wait(s_in, s_out, *sems)

    any_spec = pl.BlockSpec(memory_space=pl.ANY)
    outs = pl.pallas_call(
        wrapped, name=name, grid=grid, in_specs=list(in_specs) + [any_spec] * ns_in, out_specs=list(out_specs) + [any_spec] * ns_out,
        out_shape=list(out_shape) + side.outs, scratch_shapes=list(scratch_shapes) + side.sems, compiler_params=_cparams(),
    )(*args, *side.ins)
    return outs[:n_out], outs[n_out:]


def _run_side(name, side):
    def body(*refs):
        n = len(side.ins)
        side.start(refs[:n], refs[n:2 * n], *refs[2 * n:])
        side.wait(refs[:n], refs[n:2 * n], *refs[2 * n:])

    any_spec = pl.BlockSpec(memory_space=pl.ANY)
    return pl.pallas_call(body, name=name, in_specs=[any_spec] * len(side.ins), out_specs=[any_spec] * len(side.outs),
                          out_shape=side.outs, scratch_shapes=side.sems)(*side.ins)


def _allreduce_small(v):
    rows, cols = v.shape
    r8 = rows // (2 * N_CHIPS)
    assert r8 * 2 * N_CHIPS == rows and r8 % SUBLANES == 0, rows

    def body(v_ref, o_ref, sib_ref, cs_ref, slot_ref, send, recv):
        x, y, c, chips = _place()
        me = 2 * x + y
        sibling = (x, y, 1 - c)

        def eighth(ref, chip, core):
            return ref.at[pl.ds(pl.multiple_of((2 * chip + core) * r8, SUBLANES), r8)]

        def copy(src, dst, k, to):
            return pltpu.make_async_remote_copy(src_ref=src, dst_ref=dst, send_sem=send.at[k], recv_sem=recv.at[k],
                                                device_id=to, device_id_type=MESH)

        d2d = copy(v_ref, sib_ref, 0, sibling)
        d2d.start()
        d2d.wait_recv()
        cs_ref[...] = v_ref[...] + sib_ref[...]
        reduce_out = [copy(eighth(cs_ref, 2 * px + py, c), slot_ref.at[me], 1 + r, (px, py, c)) for r, (px, py) in enumerate(chips)]
        for cp in reduce_out:
            cp.start()
        slot_ref[me] = cs_ref[pl.ds(pl.multiple_of((2 * me + c) * r8, SUBLANES), r8), :]
        for r, (px, py) in enumerate(chips):
            copy(eighth(cs_ref, me, c), slot_ref.at[2 * px + py], 1 + r, (px, py, c)).wait_recv()
        o_ref[pl.ds(pl.multiple_of((2 * me + c) * r8, SUBLANES), r8), :] = (slot_ref[0] + slot_ref[1]) + (slot_ref[2] + slot_ref[3])
        mine = eighth(o_ref, me, c)
        hand_out = [copy(mine, mine, 4, sibling)] + [copy(mine, mine, 5 + r, (px, py, c)) for r, (px, py) in enumerate(chips)]
        for cp in hand_out:
            cp.start()
        passed_on = []
        for r, (px, py) in enumerate(chips):
            theirs = eighth(o_ref, 2 * px + py, c)
            copy(theirs, theirs, 5 + r, (px, py, c)).wait_recv()
            fw = copy(theirs, theirs, 8 + r, sibling)
            fw.start()
            passed_on.append(fw)
        sib_own = eighth(o_ref, me, 1 - c)
        copy(sib_own, sib_own, 4, sibling).wait_recv()
        for r, (px, py) in enumerate(chips):
            got = eighth(o_ref, 2 * px + py, 1 - c)
            copy(got, got, 8 + r, sibling).wait_recv()
        for cp in [d2d] + reduce_out + hand_out + passed_on:
            cp.wait_send()

    vm = pl.BlockSpec(memory_space=pltpu.VMEM)
    return pl.pallas_call(
        body, name="allreduce_small", in_specs=[vm], out_specs=vm, out_shape=jax.ShapeDtypeStruct((rows, cols), F32),
        scratch_shapes=[pltpu.VMEM((rows, cols), F32), pltpu.VMEM((rows, cols), F32), pltpu.VMEM((N_CHIPS, r8, cols), F32),
                        pltpu.SemaphoreType.DMA((11,)), pltpu.SemaphoreType.DMA((11,))],
        compiler_params=_cparams(),
    )(v)


def _pack(tensors):
    pieces = []
    for t in tensors:
        flat = t.reshape(-1)
        pad = (-flat.shape[0]) % (SUBLANES * LANES)
        pieces.append(jnp.pad(flat, (0, pad)).reshape(-1, LANES))
    rows = sum(p.shape[0] for p in pieces)
    pieces.append(jnp.zeros(((-rows) % PACK_ROW_MULTIPLE, LANES), tensors[0].dtype))
    return jnp.concatenate(pieces, axis=0)


def _unpack(buf, like):
    out, off = [], 0
    for t in like:
        size = math.prod(t.shape)
        rows = -(-size // (SUBLANES * LANES)) * SUBLANES
        out.append(buf[off:off + rows].reshape(-1)[:size].reshape(t.shape))
        off += rows
    return out


def _s5_pack_b(bb):
    gc, g, p = bb.shape
    q = S5_GROUPS_PER_BLOCK
    t = bb.reshape(gc, g // q, q, p).transpose(1, 2, 0, 3)
    eye = jnp.eye(q, dtype=bb.dtype)
    return (t[:, :, :, None, :] * eye[None, :, None, :, None]).reshape(g // q, q * gc, q * p)


def _s5_unpack_b(dbp, gc, p):
    nb = dbp.shape[0]
    q = S5_GROUPS_PER_BLOCK
    eye = jnp.eye(q, dtype=dbp.dtype)
    t = (dbp.reshape(nb, q, gc, q, p) * eye[None, :, None, :, None]).sum(axis=3)
    return t.transpose(2, 0, 1, 3).reshape(gc, nb * q, p)


def _s5_pack_c(cc):
    g, gc, p = cc.shape
    q = S5_GROUPS_PER_BLOCK
    t = cc.reshape(g // q, q, gc, p).transpose(0, 1, 3, 2)
    eye = jnp.eye(q, dtype=cc.dtype)
    return (t[:, :, :, None, :] * eye[None, :, None, :, None]).reshape(g // q, q * p, q * gc)


def _s5_unpack_c(dcp, gc, p):
    nb = dcp.shape[0]
    q = S5_GROUPS_PER_BLOCK
    eye = jnp.eye(q, dtype=dcp.dtype)
    t = (dcp.reshape(nb, q, p, q, gc) * eye[None, :, None, :, None]).sum(axis=3)
    return t.transpose(0, 1, 3, 2).reshape(nb * q, gc, p)


def _split2(m):
    return m.arr[:, 0]


def kernel(x, norm_mix_g, norm_ffn_g, norm_final_g, rg_w_in, rg_conv_w, rg_conv_b, rg_w_a, rg_b_a, rg_w_x, rg_b_x, rg_lambda, rg_w_out, s5_w_in, s5_a_re, s5_a_im, s5_log_dt, s5_b_re, s5_b_im, s5_c_re, s5_c_im, s5_d, s5_w_glu, s5_w_out, ffn_w_up, ffn_conv_w, ffn_conv_b, ffn_w_down, loss_target, m_norm_mix_g, m_norm_ffn_g, m_norm_final_g, m_rg_w_in, m_rg_conv_w, m_rg_conv_b, m_rg_w_a, m_rg_b_a, m_rg_w_x, m_rg_b_x, m_rg_lambda, m_rg_w_out, m_s5_w_in, m_s5_a_re, m_s5_a_im, m_s5_log_dt, m_s5_b_re, m_s5_b_im, m_s5_c_re, m_s5_c_im, m_s5_d, m_s5_w_glu, m_s5_w_out, m_ffn_w_up, m_ffn_conv_w, m_ffn_conv_b, m_ffn_w_down, v_norm_mix_g, v_norm_ffn_g, v_norm_final_g, v_rg_w_in, v_rg_conv_w, v_rg_conv_b, v_rg_w_a, v_rg_b_a, v_rg_w_x, v_rg_b_x, v_rg_lambda, v_rg_w_out, v_s5_w_in, v_s5_a_re, v_s5_a_im, v_s5_log_dt, v_s5_b_re, v_s5_b_im, v_s5_c_re, v_s5_c_im, v_s5_d, v_s5_w_glu, v_s5_w_out, v_ffn_w_up, v_ffn_conv_w, v_ffn_conv_b, v_ffn_w_down):
    w = dict(zip(PARAM_NAMES, (norm_mix_g, norm_ffn_g, norm_final_g, rg_w_in, rg_conv_w, rg_conv_b, rg_w_a, rg_b_a, rg_w_x, rg_b_x,
                               rg_lambda, rg_w_out, s5_w_in, s5_a_re, s5_a_im, s5_log_dt, s5_b_re, s5_b_im, s5_c_re, s5_c_im, s5_d,
                               s5_w_glu, s5_w_out, ffn_w_up, ffn_conv_w, ffn_conv_b, ffn_w_down)))
    mom = dict(zip(PARAM_NAMES, (m_norm_mix_g, m_norm_ffn_g, m_norm_final_g, m_rg_w_in, m_rg_conv_w, m_rg_conv_b, m_rg_w_a, m_rg_b_a,
                                 m_rg_w_x, m_rg_b_x, m_rg_lambda, m_rg_w_out, m_s5_w_in, m_s5_a_re, m_s5_a_im, m_s5_log_dt, m_s5_b_re,
                                 m_s5_b_im, m_s5_c_re, m_s5_c_im, m_s5_d, m_s5_w_glu, m_s5_w_out, m_ffn_w_up, m_ffn_conv_w,
                                 m_ffn_conv_b, m_ffn_w_down)))
    vel = dict(zip(PARAM_NAMES, (v_norm_mix_g, v_norm_ffn_g, v_norm_final_g, v_rg_w_in, v_rg_conv_w, v_rg_conv_b, v_rg_w_a, v_rg_b_a,
                                 v_rg_w_x, v_rg_b_x, v_rg_lambda, v_rg_w_out, v_s5_w_in, v_s5_a_re, v_s5_a_im, v_s5_log_dt, v_s5_b_re,
                                 v_s5_b_im, v_s5_c_re, v_s5_c_im, v_s5_d, v_s5_w_glu, v_s5_w_out, v_ffn_w_up, v_ffn_conv_w,
                                 v_ffn_conv_b, v_ffn_w_down)))
    _, s, d = x.shape
    depth = norm_mix_g.shape[0]
    n_grp, n_state = s5_a_re.shape[1], s5_a_re.shape[2]
    gc = s5_b_re.shape[3]
    d_ff = ffn_w_down.shape[1] * N_CHIPS
    s5_ts = min(256, s)
    s5_perm = _segment_perm(s5_ts)

    wb = {n: (w[n].astype(BF16) if n in BIG else w[n]) for n in SHARDED}
    gath = {}

    def mixer_keys(i):
        return [(n, i // 2) for n in MIXER_SHARDED[i % 2]] if i < depth else []

    def gather_side(keys):
        return _gather_side([wb[n][l] for n, l in keys])

    def put(keys, arrs):
        for k, a in zip(keys, arrs):
            gath[k] = a

    def wcol(n, l):
        return Mat(gath[(n, l)][:, None], 0, 'c')

    def wrow(n, l):
        g = gath[(n, l)]
        return Mat(g.reshape(1, 1, N_CHIPS * g.shape[1], g.shape[2]), 0, 'c')

    def rg_cw(l):
        return gath[('rg_conv_w', l)].transpose(1, 0, 2).reshape(RG_CONV_W, d)

    def s5_dv(l):
        return gath[('s5_d', l)].reshape(1, d)

    def f_cw(l):
        return gath[('ffn_conv_w', l)].transpose(1, 0, 2).reshape(FFN_CONV_W, 2, d_ff).transpose(1, 0, 2)

    tm = min(1024, s)
    tkw = min(2048, s)
    d_up = 2 * d_ff // N_CHIPS
    f_cb = ffn_conv_b.reshape(depth, 2, 1, d_ff)

    h = x.reshape(s, d)
    saved = []
    for i in range(depth):
        j = i // 2
        sv = {'h_in': h}
        first_late = mixer_keys(0)[-1:] if i == 0 else []
        if i == 0:
            hn, got = _rms_fwd(h, norm_mix_g[:1], side=gather_side(mixer_keys(0)[:-1]))
            put(mixer_keys(0)[:-1], got)
        sv['hn'] = hn
        up_keys = [('ffn_w_up', i), ('ffn_conv_w', i)]
        if i % 2 == 0:
            xg = _mm("rg_in", 'nn', act(hn), wcol('rg_w_in', j), out_parts=2, tm=tm, tn=512, tk=d,
                     side=gather_side(first_late) if first_late else None)
            if first_late:
                xg, got = xg
                put(first_late, got)
            xg2 = _split2(xg)
            wa, wx = rg_w_a[j].astype(BF16), rg_w_x[j].astype(BF16)
            ba, bx = rg_b_a[j].reshape(1, d), rg_b_x[j].reshape(1, d)
            (xr, hs, y, gates), got = _rg_fwd(xg2, rg_cw(j), rg_conv_b[j:j + 1], wa, ba, wx, bx, rg_lambda[j:j + 1],
                                              side=gather_side(up_keys))
            put(up_keys, got)
            sv.update(xg2=xg2, xr=xr, hs=hs, y=y, gates=gates, wa=wa, wx=wx, ba=ba, bx=bx)
            h, hn2 = _mm("rg_out", 'nn', act(y), wrow('rg_w_out', j), res=act(h), norm_g=norm_ffn_g[i:i + 1], tm=tm, tn=d, tk=d)
        else:
            u = _mm("s5_in", 'nn', act(hn), wrow('s5_w_in', j), tm=tm, tn=d, tk=d).arr[0, 0]
            bt_re, bt_im = s5_b_re[j].transpose(2, 0, 1), s5_b_im[j].transpose(2, 0, 1)
            ldt = s5_log_dt[j].reshape(n_grp, 1)
            tab_r, tab_i, rtab_r, rtab_i, bbr, bbi = _s5_tables3(s5_a_re[j], s5_a_im[j], ldt, bt_re, bt_im, seg=s5_ts // SUBLANES)
            nn_ = n_grp * n_state
            tab_r, tab_i, rtab_r, rtab_i = (t.reshape(5, SUBLANES, nn_) for t in (tab_r, tab_i, rtab_r, rtab_i))
            prm = dict(bp_r=_s5_pack_b(bbr).astype(BF16), bp_i=_s5_pack_b(bbi).astype(BF16),
                       cp_r=_s5_pack_c(s5_c_re[j]).astype(BF16), cp_i=_s5_pack_c(s5_c_im[j]).astype(BF16), dvec=s5_dv(j))
            (hr, hi, ypre, gy), got = _s5_fwd3(u, s5_perm, s5_perm.T, tab_r, tab_i, ts=s5_ts, side=gather_side(up_keys), **prm)
            sv.update(rtab_r=rtab_r, rtab_i=rtab_i)
            put(up_keys, got)
            gl2, o = _glu_mm(gy, gath[('s5_w_glu', j)])
            sv.update(u=u, prm=prm, hr=hr, hi=hi, ypre=ypre, gy=gy, gl2=gl2, o=o, bt_re=bt_re, bt_im=bt_im, ldt=ldt)
            h, hn2 = _mm("s5_out", 'nn', act(o), wrow('s5_w_out', j), res=act(h), norm_g=norm_ffn_g[i:i + 1], tm=tm, tn=d, tk=d)
        h = h.arr[0, 0]
        sv['h_mid'] = h
        next_keys = [('ffn_w_down', i)] + mixer_keys(i + 1)
        (up2, c2, a_ffn), got = _ffn_up_act(hn2, gath[('ffn_w_up', i)], f_cw(i), f_cb[i], side=gather_side(next_keys))
        put(next_keys, got)
        sv.update(hn2=hn2, up2=up2, c2=c2, act=a_ffn)
        if i + 1 < depth:
            h, hn = _mm("ffn_down", 'nn', act(a_ffn), wrow('ffn_w_down', i), res=act(h), norm_g=norm_mix_g[i + 1:i + 2], tm=tm, tn=d,
                        tk=d_ff // 2)
        else:
            h = _mm("ffn_down", 'nn', act(a_ffn), wrow('ffn_w_down', i), res=act(h), tm=tm, tn=d, tk=d_ff // 2)
        h = h.arr[0, 0]
        saved.append(sv)

    loss_row, dh, dg_final = _loss_and_grad(h, norm_final_g.reshape(1, d), loss_target.reshape(s, d))
    loss = lax.psum(loss_row[0, 0], ("x", "y", "c"))

    gl_ = {n: [None] * w[n].shape[0] for n in PARAM_NAMES if n != 'norm_final_g'}
    recvd = {}

    def scatter_side(keys):
        return _scatter_side([gl_[n][l].reshape((N_CHIPS,) + w[n].shape[1:]) for n, l in keys])

    def record(keys, arrs):
        for k, a in zip(keys, arrs):
            recvd[k] = a

    pending = None
    for i in reversed(range(depth)):
        j = i // 2
        sv = saved[i]
        gl_['ffn_w_down'][i] = _mm("ffn_down_dw", 'tn', act(sv['act']), act(dh), out_dtype=BF16, tm=d_ff // N_CHIPS, tn=d, tk=tkw).arr
        (dup2, dcw2, dcb2), got = _ffn_bwd_fused(dh, gath[('ffn_w_down', i)].reshape(d_ff, d), sv['up2'], sv['c2'], f_cw(i),
                                                 side=scatter_side(pending) if pending else None)
        if pending:
            record(pending, got)
        gl_['ffn_conv_w'][i] = dcw2.transpose(1, 0, 2).reshape(FFN_CONV_W, 2 * d_ff)
        gl_['ffn_conv_b'][i] = dcb2.reshape(2 * d_ff)
        dup = Mat(dup2[:, None], 0, 'c')
        gl_['ffn_w_up'][i] = _mm("ffn_up_dw", 'tn', act(sv['hn2']), dup, out_parts=N_CHIPS, out_dtype=BF16, tm=d, tn=d_up, tk=tkw).arr
        (dh, dg), _ = _mm_rms_bwd("ffn_up_dx", dup, wcol('ffn_w_up', i), sv['h_mid'], norm_ffn_g[i:i + 1], dh, tm=tm, tk=d_up)
        gl_['norm_ffn_g'][i] = dg[0]
        ffn_keys = [('ffn_w_up', i), ('ffn_w_down', i)]
        if i % 2 == 0:
            dy = _mm("rg_out_dx", 'nt', act(dh), wrow('rg_w_out', j), tm=tm, tn=d, tk=d).arr[0, 0]
            gl_['rg_w_out'][j] = _mm("rg_out_dw", 'tn', act(sv['y']), act(dh), out_dtype=BF16, tm=d, tn=d, tk=tkw).arr
            (dxg2, dcw, dcb, dwa, dba, dwx, dbx, dlam), got = _rg_bwd(
                dy, sv['xg2'], sv['xr'], sv['hs'], sv['gates'], rg_cw(j), sv['wa'], sv['ba'], sv['wx'], sv['bx'], rg_lambda[j:j + 1],
                side=scatter_side(ffn_keys))
            record(ffn_keys, got)
            gl_['rg_conv_w'][j] = dcw
            gl_['rg_conv_b'][j] = dcb[0]
            gl_['rg_w_a'][j], gl_['rg_w_x'][j] = dwa, dwx
            gl_['rg_b_a'][j], gl_['rg_b_x'][j] = dba.reshape(rg_b_a.shape[1:]), dbx.reshape(rg_b_x.shape[1:])
            gl_['rg_lambda'][j] = dlam[0]
            dxg = Mat(dxg2[:, None], 0, 'c')
            gl_['rg_w_in'][j] = _mm("rg_in_dw", 'tn', act(sv['hn']), dxg, out_parts=N_CHIPS, out_dtype=BF16, tm=d, tn=512, tk=tkw).arr
            mix_dx = ("rg_in_dx", dxg, wcol('rg_w_in', j), 512)
            pending = [('rg_w_in', j), ('rg_w_out', j)]
        else:
            gl_['s5_w_out'][j] = _mm("s5_out_dw", 'tn', act(sv['o']), act(dh), out_dtype=BF16, tm=d, tn=d, tk=tkw).arr
            dgl2 = _glu_bwd_mm(dh, gath[('s5_w_out', j)].reshape(d, d), sv['gl2'])
            dgl = Mat(dgl2[:, None], 0, 'c')
            gl_['s5_w_glu'][j] = _mm("s5_glu_dw", 'tn', act(sv['gy']), dgl, out_parts=N_CHIPS, out_dtype=BF16, tm=d, tn=512, tk=tkw).arr
            dgy = _mm("s5_glu_dx", 'nt', dgl, wcol('s5_w_glu', j), tm=tm, tn=d, tk=512).arr[0, 0]
            (du, dar, dai, dbpr, dbpi, dcpr, dcpi, dd), got = _s5_bwd3(
                dgy, sv['ypre'], sv['u'], sv['hr'], sv['hi'], s5_perm, s5_perm.T, sv['rtab_r'], sv['rtab_i'], ts=s5_ts,
                side=scatter_side(ffn_keys), **sv['prm'])
            record(ffn_keys, got)
            gl_['s5_d'][j] = dd[0]
            gl_['s5_c_re'][j] = _s5_unpack_c(dcpr, gc, n_state)
            gl_['s5_c_im'][j] = -_s5_unpack_c(dcpi, gc, n_state)
            d_are, d_aim, d_ldt, d_btr, d_bti = _s5_params_bwd(
                s5_a_re[j], s5_a_im[j], sv['ldt'], sv['bt_re'], sv['bt_im'], dar.reshape(n_grp, n_state), dai.reshape(n_grp, n_state),
                _s5_unpack_b(dbpr, gc, n_state), _s5_unpack_b(dbpi, gc, n_state))
            gl_['s5_a_re'][j], gl_['s5_a_im'][j], gl_['s5_log_dt'][j] = d_are, d_aim, d_ldt[:, 0]
            gl_['s5_b_re'][j], gl_['s5_b_im'][j] = d_btr.transpose(1, 2, 0), d_bti.transpose(1, 2, 0)
            dum = act(du)
            gl_['s5_w_in'][j] = _mm("s5_in_dw", 'tn', act(sv['hn']), dum, out_dtype=BF16, tm=d, tn=d, tk=tkw).arr
            mix_dx = ("s5_in_dx", dum, wrow('s5_w_in', j), d)
            pending = [('s5_w_in', j), ('s5_w_glu', j), ('s5_w_out', j)]
        (dh, dg), got = _mm_rms_bwd(mix_dx[0], mix_dx[1], mix_dx[2], sv['h_in'], norm_mix_g[i:i + 1], dh, tm=tm, tk=mix_dx[3],
                                    side=scatter_side(pending) if i == 0 else None)
        if i == 0:
            record(pending, got)
        gl_['norm_mix_g'][i] = dg[0]
    grad_x = dh.reshape(x.shape)

    order = sorted(BIG, key=lambda n: -math.prod(w[n].shape))
    chip_sums, theirs, prev = {}, {}, None
    for n in order:
        cols = w[n].shape[-1]
        cs, got = _sum_parts([recvd[(n, l)].reshape(N_CHIPS, -1, cols) for l in range(w[n].shape[0])],
                             side=_sibling_side([chip_sums[prev]]) if prev else None)
        chip_sums[n] = cs.reshape(-1, cols)
        if prev:
            theirs[prev] = got[0]
        prev = n
    theirs[prev] = _run_side("swap_last", _sibling_side([chip_sums[prev]]))[0]
    results = {}
    for n in BIG:
        cols = w[n].shape[-1]
        (delta, new_m, new_v, grad), _ = _adamw(w[n].reshape(-1, cols), [chip_sums[n], theirs[n]], mom[n].reshape(-1, cols),
                                                vel[n].reshape(-1, cols))
        results[n] = [o.reshape(w[n].shape) for o in (grad, delta, new_m, new_v)]

    small = REPLICATED + SMALL_SHARDED
    local = [dg_final.reshape(d) if n == 'norm_final_g' else jnp.stack(gl_[n]) for n in small]
    summed = _unpack(_allreduce_small(_pack(local)), local)
    me = 2 * lax.axis_index("x") + lax.axis_index("y")
    for n, g in zip(small, summed):
        if n in SMALL_SHARDED:
            g = lax.dynamic_slice_in_dim(g, me * w[n].shape[-1], w[n].shape[-1], axis=g.ndim - 1)
        view = (-1, w[n].shape[-1])
        (delta, new_m, new_v), _ = _adamw(w[n].reshape(view), [g.reshape(view)], mom[n].reshape(view), vel[n].reshape(view))
        results[n] = [g] + [o.reshape(w[n].shape) for o in (delta, new_m, new_v)]

    return (loss, grad_x, *[results[n][0] for n in PARAM_NAMES], *[results[n][1] for n in PARAM_NAMES],
            *[results[n][2] for n in PARAM_NAMES], *[results[n][3] for n in PARAM_NAMES])
```

```python
import math

import jax
import jax.numpy as jnp
from jax import lax
from jax.experimental import pallas as pl
from jax.experimental.pallas import tpu as pltpu

F32 = jnp.float32
BF16 = jnp.bfloat16
MESH = pl.DeviceIdType.MESH

NORM_EPS = 1e-6
RG_HEADS = 8
RG_CONV_W = 4
RG_C = 8.0
S5_GC = 16
S5_P = 64
S5_GROUPS_PER_BLOCK = 8
FFN_CONV_W = 3
N_CHIPS = 4
ADAM_LR, ADAM_B1, ADAM_B2, ADAM_EPS, ADAM_WD, ADAM_STEP = 0.001, 0.9, 0.999, 1e-08, 0.01, 10
VMEM_LIMIT_BYTES = 56 * 1024 * 1024
SUBLANES = 8
LANES = 128

PARAM_NAMES = ['norm_mix_g', 'norm_ffn_g', 'norm_final_g', 'rg_w_in', 'rg_conv_w', 'rg_conv_b', 'rg_w_a', 'rg_b_a', 'rg_w_x',
               'rg_b_x', 'rg_lambda', 'rg_w_out', 's5_w_in', 's5_a_re', 's5_a_im', 's5_log_dt', 's5_b_re', 's5_b_im', 's5_c_re',
               's5_c_im', 's5_d', 's5_w_glu', 's5_w_out', 'ffn_w_up', 'ffn_conv_w', 'ffn_conv_b', 'ffn_w_down']
SHARDED = ['rg_w_in', 'rg_conv_w', 'rg_w_out', 's5_w_in', 's5_d', 's5_w_glu', 's5_w_out', 'ffn_w_up', 'ffn_conv_w', 'ffn_w_down']
BIG = ['rg_w_in', 'rg_w_out', 's5_w_in', 's5_w_glu', 's5_w_out', 'ffn_w_up', 'ffn_w_down']
SMALL_SHARDED = ['rg_conv_w', 's5_d', 'ffn_conv_w']
MIXER_SHARDED = [['rg_w_in', 'rg_conv_w', 'rg_w_out'], ['s5_w_in', 's5_d', 's5_w_glu', 's5_w_out']]
REPLICATED = [n for n in PARAM_NAMES if n not in SHARDED]


def _cparams():
    return pltpu.CompilerParams(vmem_limit_bytes=VMEM_LIMIT_BYTES)


_GELU_C = math.sqrt(2.0 / math.pi)
_GELU_K = 0.044715


def _gelu(x):
    return 0.5 * x * (1.0 + jnp.tanh(_GELU_C * (x + _GELU_K * x * x * x)))


def _gelu_and_grad(x):
    t = jnp.tanh(_GELU_C * (x + _GELU_K * x * x * x))
    g = 0.5 * x * (1.0 + t)
    dg = 0.5 * (1.0 + t) + 0.5 * x * (1.0 - t * t) * (_GELU_C * (1.0 + 3.0 * _GELU_K * x * x))
    return g, dg


def _sigmoid(x):
    return jax.nn.sigmoid(x)


def _neg_expm1(x):
    series = -(x * (1.0 + x * (0.5 + x * (1.0 / 6 + x * (1.0 / 24 + x * (1.0 / 120 + x * (1.0 / 720)))))))
    return jnp.where(x > -0.25, series, 1.0 - jnp.exp(x))


def _softplus(z):
    return jnp.maximum(z, 0.0) + jnp.log1p(jnp.exp(-jnp.abs(z)))


def _rows(shape):
    return lax.broadcasted_iota(jnp.int32, shape, 0)


def _shift_down(x, halo, k):
    ext = jnp.concatenate([halo, x], axis=0)
    return pltpu.roll(ext, k, 0)[SUBLANES:]


def _shift_up(x, halo, k):
    ext = jnp.concatenate([x, halo], axis=0)
    n = ext.shape[0]
    return pltpu.roll(ext, n - k, 0)[:x.shape[0]]


RG_LANE_CHUNK = 512


def _real_slab_scan(a_ref, b_ref, out_ref, carry_ref, reverse):
    t, c = a_ref.shape
    nsl = t // SUBLANES
    lc = min(RG_LANE_CHUNK, c)
    row8 = _rows((SUBLANES, lc))
    for q in range(c // lc):
        sl = slice(q * lc, (q + 1) * lc)

        def slab(jj, carry, sl=sl):
            j = nsl - 1 - jj if reverse else jj
            r0 = pl.multiple_of(j * SUBLANES, SUBLANES)
            a, b = a_ref[pl.ds(r0, SUBLANES), sl], b_ref[pl.ds(r0, SUBLANES), sl]
            for k in range(3):
                sh = 1 << k
                keep = row8 < SUBLANES - sh if reverse else row8 >= sh
                amount = SUBLANES - sh if reverse else sh
                b = a * jnp.where(keep, pltpu.roll(b, amount, 0), 0.0) + b
                a = a * jnp.where(keep, pltpu.roll(a, amount, 0), 1.0)
            x = b + a * jnp.broadcast_to(carry, b.shape)
            out_ref[pl.ds(r0, SUBLANES), sl] = x
            return x[:1, :] if reverse else x[SUBLANES - 1:, :]

        carry_ref[:, sl] = lax.fori_loop(0, nsl, slab, carry_ref[:, sl], unroll=2)


class Mat:
    def __init__(self, arr, l=0, split='c'):
        assert arr.ndim == 4
        self.arr, self.l, self.split = arr, l, split
        p, _, r, c = arr.shape
        self.shape = (r, c * p) if split == 'c' else (r * p, c)

    def spec(self, tr, tc, rc):
        p, _, r, c = self.arr.shape
        l = self.l
        assert r % tr == 0 and c % tc == 0, (self.arr.shape, tr, tc)
        if self.split == 'c':
            per = c // tc
            return pl.BlockSpec((None, None, tr, tc), lambda i, j, k: (rc(i, j, k)[1] // per, l, rc(i, j, k)[0], rc(i, j, k)[1] % per))
        per = r // tr
        return pl.BlockSpec((None, None, tr, tc), lambda i, j, k: (rc(i, j, k)[0] // per, l, rc(i, j, k)[0] % per, rc(i, j, k)[1]))


def act(x, parts=1):
    s, c = x.shape
    return Mat(x.reshape(s, parts, c // parts).transpose(1, 0, 2)[:, None] if parts > 1 else x[None, None])


def _mm(name, mode, a, b, *, out_parts=1, out_split='c', out_dtype=F32, res=None, norm_g=None, tm=512, tn=512, tk=512,
        side=None):
    if mode == 'nn':
        (m, kk), (kb, n) = a.shape, b.shape
    elif mode == 'nt':
        (m, kk), (n, kb) = a.shape, b.shape
    else:
        (kk, m), (kb, n) = a.shape, b.shape
    assert kk == kb, (name, a.shape, b.shape)
    tm, tn, tk = min(tm, m), min(tn, n), min(tk, kk)
    assert m % tm == 0 and n % tn == 0 and kk % tk == 0, (name, m, n, kk, tm, tn, tk)
    nk = kk // tk
    if mode == 'nn':
        a_spec = a.spec(tm, tk, lambda i, j, k: (i, k))
        b_spec = b.spec(tk, tn, lambda i, j, k: (k, j))
        dims = (((1,), (0,)), ((), ()))
    elif mode == 'nt':
        a_spec = a.spec(tm, tk, lambda i, j, k: (i, k))
        b_spec = b.spec(tn, tk, lambda i, j, k: (j, k))
        dims = (((1,), (1,)), ((), ()))
    else:
        a_spec = a.spec(tk, tm, lambda i, j, k: (k, i))
        b_spec = b.spec(tk, tn, lambda i, j, k: (k, j))
        dims = (((0,), (0,)), ((), ()))
    if out_split == 'c':
        out_arr = jax.ShapeDtypeStruct((out_parts, 1, m, n // out_parts), out_dtype)
    else:
        out_arr = jax.ShapeDtypeStruct((out_parts, 1, m // out_parts, n), out_dtype)
    out_mat = Mat(out_arr, 0, out_split)
    o_spec = out_mat.spec(tm, tn, lambda i, j, k: (i, j))
    has_res = res is not None
    has_norm = norm_g is not None
    assert not has_norm or tn == n, (name, tn, n)

    def body(*refs):
        a_ref, b_ref = refs[:2]
        extra = list(refs[2:2 + has_res + has_norm])
        r_ref = extra.pop(0) if has_res else None
        g_ref = extra.pop(0) if has_norm else None
        o_ref = refs[2 + has_res + has_norm]
        prod = lax.dot_general(a_ref[...].astype(BF16), b_ref[...].astype(BF16), dims, preferred_element_type=F32)

        def finish(acc):
            if has_res:
                acc = acc + r_ref[...]
            o_ref[...] = acc.astype(out_dtype)
            if has_norm:
                var = jnp.mean(acc * acc, axis=-1, keepdims=True)
                refs[3 + has_res + has_norm][...] = (acc * lax.rsqrt(var + NORM_EPS) * g_ref[...]).astype(BF16)

        if nk == 1:
            finish(prod)
        else:
            acc_ref = refs[-1]
            k = pl.program_id(2)

            @pl.when(k == 0)
            def _():
                acc_ref[...] = prod

            @pl.when(k > 0)
            def _():
                acc_ref[...] += prod

            @pl.when(k == nk - 1)
            def _():
                finish(acc_ref[...])

    in_specs = [a_spec, b_spec]
    args = [a.arr, b.arr]
    if has_res:
        in_specs.append(res.spec(tm, tn, lambda i, j, k: (i, j)))
        args.append(res.arr)
    out_specs, out_shape = [o_spec], [out_arr]
    if has_norm:
        in_specs.append(pl.BlockSpec((1, n), lambda i, j, k: (0, 0)))
        args.append(norm_g)
        out_specs.append(pl.BlockSpec((tm, n), lambda i, j, k: (i, 0)))
        out_shape.append(jax.ShapeDtypeStruct((m, n), BF16))
    gi, gj = m // tm, n // tn
    outs, got = _call_with_side(
        body, side, lambda: (pl.program_id(0) == 0) & (pl.program_id(1) == 0) & (pl.program_id(2) == 0),
        lambda: (pl.program_id(0) == gi - 1) & (pl.program_id(1) == gj - 1) & (pl.program_id(2) == nk - 1),
        name=name, grid=(gi, gj, nk), in_specs=in_specs, out_specs=out_specs, out_shape=out_shape,
        scratch_shapes=[pltpu.VMEM((tm, tn), F32)] if nk > 1 else [], args=tuple(args))
    result = (Mat(outs[0], 0, out_split), outs[1]) if has_norm else Mat(outs[0], 0, out_split)
    return result if side is None else (result, got)


def _rms_fwd(h, g, ts=512, side=None):
    s, d = h.shape
    ts = min(ts, s)
    nt = s // ts

    def body(h_ref, g_ref, o_ref):
        x = h_ref[...]
        var = jnp.mean(x * x, axis=-1, keepdims=True)
        o_ref[...] = (x * lax.rsqrt(var + NORM_EPS) * g_ref[...]).astype(BF16)

    outs, got = _call_with_side(
        body, side, lambda: pl.program_id(0) == 0, lambda: pl.program_id(0) == nt - 1,
        name="rms_fwd", grid=(nt,),
        in_specs=[pl.BlockSpec((ts, d), lambda i: (i, 0)), pl.BlockSpec((1, d), lambda i: (0, 0))],
        out_specs=[pl.BlockSpec((ts, d), lambda i: (i, 0))], out_shape=[jax.ShapeDtypeStruct((s, d), BF16)],
        scratch_shapes=[], args=(h, g))
    return outs[0], got


def _loss_and_grad(h, g, tgt, ts=512):
    s, d = h.shape
    ts = min(ts, s)

    def body(h_ref, g_ref, t_ref, loss_ref, dh_ref, dg_ref):
        i = pl.program_id(0)
        x = h_ref[...]
        gv = g_ref[...]
        rstd = lax.rsqrt(jnp.mean(x * x, axis=-1, keepdims=True) + NORM_EPS)
        xhat = x * rstd
        err = xhat * gv - t_ref[...]
        dy = err * (1.0 / d)
        dxh = dy * gv
        dh_ref[...] = rstd * (dxh - xhat * jnp.mean(dxh * xhat, axis=-1, keepdims=True))
        part = jnp.sum(dy * xhat, axis=0, keepdims=True)
        lpart = jnp.broadcast_to(jnp.sum(jnp.sum(err * err, axis=0, keepdims=True), axis=1, keepdims=True) * (0.5 / d), (1, LANES))

        @pl.when(i == 0)
        def _():
            dg_ref[...] = part
            loss_ref[...] = lpart

        @pl.when(i > 0)
        def _():
            dg_ref[...] += part
            loss_ref[...] += lpart

    row = pl.BlockSpec((ts, d), lambda i: (i, 0))
    vec = pl.BlockSpec((1, d), lambda i: (0, 0))
    return pl.pallas_call(
        body, name="loss_and_grad", grid=(s // ts,), in_specs=[row, vec, row],
        out_specs=[pl.BlockSpec((1, LANES), lambda i: (0, 0)), row, vec],
        out_shape=[jax.ShapeDtypeStruct((1, LANES), F32), jax.ShapeDtypeStruct((s, d), F32), jax.ShapeDtypeStruct((1, d), F32)],
        compiler_params=_cparams(),
    )(h, g, tgt)


def _mm_rms_bwd(name, a, b, h, g, dh_in, *, tm, tk, side=None):
    (m, kk), (n, kb) = a.shape, b.shape
    assert kk == kb and h.shape == (m, n), (name, a.shape, b.shape, h.shape)
    tm, tk = min(tm, m), min(tk, kk)
    nk = kk // tk
    dims = (((1,), (1,)), ((), ()))

    def body(a_ref, b_ref, h_ref, g_ref, dhin_ref, dh_ref, dg_ref, *acc):
        i, k = pl.program_id(0), pl.program_id(2)
        prod = lax.dot_general(a_ref[...].astype(BF16), b_ref[...].astype(BF16), dims, preferred_element_type=F32)

        def finish(dhn):
            x = h_ref[...]
            rstd = lax.rsqrt(jnp.mean(x * x, axis=-1, keepdims=True) + NORM_EPS)
            xhat = x * rstd
            dxh = dhn * g_ref[...]
            dh_ref[...] = dhin_ref[...] + rstd * (dxh - xhat * jnp.mean(dxh * xhat, axis=-1, keepdims=True))
            part = jnp.sum(dhn * xhat, axis=0, keepdims=True)

            @pl.when(i == 0)
            def _():
                dg_ref[...] = part

            @pl.when(i > 0)
            def _():
                dg_ref[...] += part

        if nk == 1:
            finish(prod)
        else:
            acc_ref = acc[0]

            @pl.when(k == 0)
            def _():
                acc_ref[...] = prod

            @pl.when(k > 0)
            def _():
                acc_ref[...] += prod

            @pl.when(k == nk - 1)
            def _():
                finish(acc_ref[...])

    row = pl.BlockSpec((tm, n), lambda i, j, k: (i, 0))
    vec = pl.BlockSpec((1, n), lambda i, j, k: (0, 0))
    ni = m // tm
    return _call_with_side(
        body, side, lambda: (pl.program_id(0) == 0) & (pl.program_id(2) == 0),
        lambda: (pl.program_id(0) == ni - 1) & (pl.program_id(2) == nk - 1),
        name=name, grid=(ni, 1, nk),
        in_specs=[a.spec(tm, tk, lambda i, j, k: (i, k)), b.spec(n, tk, lambda i, j, k: (0, k)), row, vec, row],
        out_specs=[row, vec], out_shape=[jax.ShapeDtypeStruct((m, n), F32), jax.ShapeDtypeStruct((1, n), F32)],
        scratch_shapes=[pltpu.VMEM((tm, n), F32)] if nk > 1 else [], args=(a.arr, b.arr, h, g, dh_in))


def _ffn_up_act(hn2, w_up4, conv_w2, conv_b2, ts=1024, tn=512, sub=1024, side=None):
    s, d = hn2.shape
    p, _, wc = w_up4.shape
    f = p * wc // 2
    ts, tn = min(ts, s), min(tn, wc)
    sub = min(sub, ts)
    per = wc // tn
    kw = FFN_CONV_W
    g0, g1 = f // tn, s // ts

    def body(hn_ref, w1_ref, w2_ref, cw_ref, cb_ref, up_ref, c_ref, act_ref, carry_ref):
        @pl.when(pl.program_id(1) == 0)
        def _():
            carry_ref[...] = jnp.zeros_like(carry_ref)

        for q in range(ts // sub):
            rows = slice(q * sub, (q + 1) * sub)
            hn = hn_ref[rows, :]
            cs = []
            for h, w_ref in enumerate((w1_ref, w2_ref)):
                x = jnp.dot(hn, w_ref[...], preferred_element_type=F32)
                up_ref[h, rows, :] = x
                halo = carry_ref[h]
                c = cb_ref[h] + cw_ref[h, kw - 1:kw, :] * x
                for sft in range(1, kw):
                    c = c + cw_ref[h, kw - 1 - sft:kw - sft, :] * _shift_down(x, halo, sft)
                carry_ref[h] = x[sub - SUBLANES:, :]
                cs.append(c)
            g1, dg1 = _gelu_and_grad(cs[0])
            c_ref[0, rows, :] = cs[1]
            c_ref[1, rows, :] = g1
            c_ref[2, rows, :] = dg1
            act_ref[rows, :] = (g1 * cs[1]).astype(BF16)

    outs, side_outs = _call_with_side(
        body, side, lambda: (pl.program_id(0) == 0) & (pl.program_id(1) == 0),
        lambda: (pl.program_id(0) == g0 - 1) & (pl.program_id(1) == g1 - 1),
        name="ffn_up_act", grid=(g0, g1),
        in_specs=[pl.BlockSpec((ts, d), lambda j, i: (i, 0)),
                  pl.BlockSpec((None, d, tn), lambda j, i: (j // per, 0, j % per)),
                  pl.BlockSpec((None, d, tn), lambda j, i: (p // 2 + j // per, 0, j % per)),
                  pl.BlockSpec((2, kw, tn), lambda j, i: (0, 0, j)),
                  pl.BlockSpec((2, 1, tn), lambda j, i: (0, 0, j))],
        out_specs=[pl.BlockSpec((2, ts, tn), lambda j, i: (0, i, j)), pl.BlockSpec((3, ts, tn), lambda j, i: (0, i, j)),
                   pl.BlockSpec((ts, tn), lambda j, i: (i, j))],
        out_shape=[jax.ShapeDtypeStruct((2, s, f), F32), jax.ShapeDtypeStruct((3, s, f), F32), jax.ShapeDtypeStruct((s, f), BF16)],
        scratch_shapes=[pltpu.VMEM((2, SUBLANES, tn), F32)], args=(hn2, w_up4, w_up4, conv_w2, conv_b2))
    return outs, side_outs


def _ffn_bwd_fused(dh, w_down, up2, c2, conv_w2, ts=1024, tn=512, side=None):
    s, d = dh.shape
    _, _, f = up2.shape
    ts, tn = min(ts, s), min(tn, f)
    kw = FFN_CONV_W
    nt = s // ts
    hb = ts // SUBLANES
    g0 = f // tn
    nt_dims = (((1,), (1,)), ((), ()))

    def body(dh_ref, wd_ref, up_ref, c_ref, w_ref, dup_ref, dw_ref, db_ref, carry_ref):
        i = pl.program_id(1)
        first_step = i == 0

        @pl.when(first_step)
        def _():
            carry_ref[...] = jnp.zeros_like(carry_ref)

        da = lax.dot_general(dh_ref[...].astype(BF16), wd_ref[...], nt_dims, preferred_element_type=F32)
        dcs = [da * c_ref[0] * c_ref[2], da * c_ref[1]]
        for h in range(2):
            dc = dcs[h]
            after = carry_ref[h]
            ups = [dc] + [_shift_up(dc, after, sft) for sft in range(1, kw)]
            dup = w_ref[h, kw - 1:kw, :] * dc
            for sft in range(1, kw):
                dup = dup + w_ref[h, kw - 1 - sft:kw - sft, :] * ups[sft]
            carry_ref[h] = dc[:SUBLANES]
            dup_ref[h] = dup.astype(BF16)
            dbp = jnp.sum(dc, axis=0, keepdims=True)
            x = up_ref[h]
            dwp = [jnp.sum(ups[kw - 1 - k] * x, axis=0, keepdims=True) for k in range(kw)]

            @pl.when(first_step)
            def _():
                db_ref[h] = dbp
                for k in range(kw):
                    dw_ref[h, k:k + 1, :] = dwp[k]

            @pl.when(i > 0)
            def _():
                db_ref[h] += dbp
                for k in range(kw):
                    dw_ref[h, k:k + 1, :] += dwp[k]

    rev = lambda i: nt - 1 - i
    return _call_with_side(
        body, side, lambda: (pl.program_id(0) == 0) & (pl.program_id(1) == 0),
        lambda: (pl.program_id(0) == g0 - 1) & (pl.program_id(1) == nt - 1),
        name="ffn_bwd", grid=(g0, nt),
        in_specs=[pl.BlockSpec((ts, d), lambda j, i: (rev(i), 0)),
                  pl.BlockSpec((tn, d), lambda j, i: (j, 0)),
                  pl.BlockSpec((2, ts, tn), lambda j, i: (0, rev(i), j)),
                  pl.BlockSpec((3, ts, tn), lambda j, i: (0, rev(i), j)),
                  pl.BlockSpec((2, kw, tn), lambda j, i: (0, 0, j))],
        out_specs=[pl.BlockSpec((2, ts, tn), lambda j, i: (0, rev(i), j)),
                   pl.BlockSpec((2, kw, tn), lambda j, i: (0, 0, j)),
                   pl.BlockSpec((2, 1, tn), lambda j, i: (0, 0, j))],
        out_shape=[jax.ShapeDtypeStruct((2, s, f), BF16), jax.ShapeDtypeStruct((2, kw, f), F32),
                   jax.ShapeDtypeStruct((2, 1, f), F32)],
        scratch_shapes=[pltpu.VMEM((2, SUBLANES, tn), F32)], args=(dh, w_down, up2, c2, conv_w2))


def _rg_gates(xr, wa_ref, ba_ref, wx_ref, bx_ref, lam_ref):
    bw = wa_ref.shape[-1]
    xb = xr.astype(BF16)
    za = jnp.concatenate([jnp.dot(xb[:, h * bw:(h + 1) * bw], wa_ref[h], preferred_element_type=F32)
                          for h in range(RG_HEADS)], axis=1) + ba_ref[...]
    zx = jnp.concatenate([jnp.dot(xb[:, h * bw:(h + 1) * bw], wx_ref[h], preferred_element_type=F32)
                          for h in range(RG_HEADS)], axis=1) + bx_ref[...]
    r, ig = _sigmoid(za), _sigmoid(zx)
    sp = _softplus(-lam_ref[...])
    la = -RG_C * r * sp
    a = jnp.exp(la)
    mult = jnp.sqrt(_neg_expm1(2.0 * la))
    return xb, r, ig, sp, a, mult


def _rg_fwd(xg2, conv_w, conv_b, w_a, b_a, w_x, b_x, lam, ts=256, side=None):
    _, s, c = xg2.shape
    ts = min(ts, s)
    kw = RG_CONV_W
    hb = ts // SUBLANES

    def body(xg_ref, halo_ref, cw_ref, cb_ref, wa_ref, ba_ref, wx_ref, bx_ref, lam_ref, xr_ref, hs_ref, y_ref, gt_ref, carry_ref,
             a_scr, b_scr):
        i = pl.program_id(0)

        @pl.when(i == 0)
        def _():
            carry_ref[...] = jnp.zeros_like(carry_ref)

        xp = xg_ref[0]
        halo = jnp.where(i == 0, 0.0, halo_ref[...])
        xr = cb_ref[...] + cw_ref[kw - 1:kw, :] * xp
        for sft in range(1, kw):
            xr = xr + cw_ref[kw - 1 - sft:kw - sft, :] * _shift_down(xp, halo, sft)
        _, r, ig, sp, a, mult = _rg_gates(xr, wa_ref, ba_ref, wx_ref, bx_ref, lam_ref)
        for k, val in enumerate((r, ig, a, mult)):
            gt_ref[k] = val
        a_scr[...] = a
        b_scr[...] = mult * (ig * xr)
        _real_slab_scan(a_scr, b_scr, hs_ref, carry_ref, reverse=False)
        xr_ref[...] = xr
        y_ref[...] = (hs_ref[...] * _gelu(xg_ref[1])).astype(BF16)

    full = lambda shape: pl.BlockSpec(shape, lambda i: (0,) * len(shape))
    row_spec = pl.BlockSpec((ts, c), lambda i: (i, 0))
    nt = s // ts
    return _call_with_side(
        body, side, lambda: pl.program_id(0) == 0, lambda: pl.program_id(0) == nt - 1,
        name="rg_fwd", grid=(nt,),
        in_specs=[pl.BlockSpec((2, ts, c), lambda i: (0, i, 0)),
                  pl.BlockSpec((None, SUBLANES, c), lambda i: (0, jnp.maximum(i * hb - 1, 0), 0)),
                  full(conv_w.shape), full(conv_b.shape), full(w_a.shape), full(b_a.shape), full(w_x.shape), full(b_x.shape),
                  full(lam.shape)],
        out_specs=[row_spec, row_spec, row_spec, pl.BlockSpec((4, ts, c), lambda i: (0, i, 0))],
        out_shape=[jax.ShapeDtypeStruct((s, c), F32), jax.ShapeDtypeStruct((s, c), F32), jax.ShapeDtypeStruct((s, c), BF16),
                   jax.ShapeDtypeStruct((4, s, c), F32)],
        scratch_shapes=[pltpu.VMEM((1, c), F32), pltpu.VMEM((ts, c), F32), pltpu.VMEM((ts, c), F32)],
        args=(xg2, xg2, conv_w, conv_b, w_a, b_a, w_x, b_x, lam))


def _rg_bwd(dy, xg2, xr, hs, gates, conv_w, w_a, b_a, w_x, b_x, lam, ts=256, side=None):
    _, s, c = xg2.shape
    ts = min(ts, s)
    nt = s // ts
    kw = RG_CONV_W
    hb = ts // SUBLANES
    bw = c // RG_HEADS
    tn_dims = (((0,), (0,)), ((), ()))
    nt_dims = (((1,), (1,)), ((), ()))

    def body(dy_ref, xg_ref, xph_ref, xr_ref, hs_ref, hsh_ref, gt_ref, cw_ref, wa_ref, ba_ref, wx_ref, bx_ref, lam_ref,
             dxg_ref, dcw_ref, dcb_ref, dwa_ref, dba_ref, dwx_ref, dbx_ref, dlam_ref,
             lam_carry, a_carry, dxr_carry, dsp_acc, a_scr, b_scr):
        i = pl.program_id(0)
        first_step = i == 0
        time_first = i == nt - 1

        @pl.when(first_step)
        def _():
            lam_carry[...] = jnp.zeros_like(lam_carry)
            a_carry[...] = jnp.ones_like(a_carry)
            dxr_carry[...] = jnp.zeros_like(dxr_carry)
            dsp_acc[...] = jnp.zeros_like(dsp_acc)
            for ref in (dcw_ref, dcb_ref, dwa_ref, dba_ref, dwx_ref, dbx_ref):
                ref[...] = jnp.zeros_like(ref)

        xr = xr_ref[...]
        hs = hs_ref[...]
        gate = xg_ref[1]
        xb = xr.astype(BF16)
        r, ig, a, mult = gt_ref[0], gt_ref[1], gt_ref[2], gt_ref[3]
        sp = _softplus(-lam_ref[...])
        dyv = dy_ref[...]
        gg, dgg = _gelu_and_grad(gate)
        dhs = dyv * gg
        dxg_ref[1] = (dyv * hs * dgg).astype(BF16)
        row = _rows(xr.shape)
        a_scr[...] = jnp.where(row == ts - 1, a_carry[0:1, :], pltpu.roll(a, ts - 1, 0))
        b_scr[...] = dhs
        _real_slab_scan(a_scr, b_scr, b_scr, lam_carry, reverse=True)
        lmb = b_scr[...]
        a_carry[...] = a[:SUBLANES]
        hs_prev = _shift_down(hs, jnp.where(time_first, 0.0, hsh_ref[...]), 1)
        d_a = lmb * hs_prev
        d_m = lmb * (ig * xr)
        d_ig = lmb * mult * xr
        d_xr = lmb * mult * ig
        d_la = a * d_a - (a * a / mult) * d_m
        dsp_acc[...] += jnp.sum(-RG_C * r * d_la, axis=0, keepdims=True)
        d_za = (-RG_C * sp) * d_la * r * (1.0 - r)
        d_zx = d_ig * ig * (1.0 - ig)
        dba_ref[...] += jnp.sum(d_za, axis=0, keepdims=True)
        dbx_ref[...] += jnp.sum(d_zx, axis=0, keepdims=True)
        dzab, dzxb = d_za.astype(BF16), d_zx.astype(BF16)
        back = []
        for h in range(RG_HEADS):
            sl = slice(h * bw, (h + 1) * bw)
            dwa_ref[h] += lax.dot_general(xb[:, sl], dzab[:, sl], tn_dims, preferred_element_type=F32)
            dwx_ref[h] += lax.dot_general(xb[:, sl], dzxb[:, sl], tn_dims, preferred_element_type=F32)
            back.append(lax.dot_general(dzab[:, sl], wa_ref[h], nt_dims, preferred_element_type=F32)
                        + lax.dot_general(dzxb[:, sl], wx_ref[h], nt_dims, preferred_element_type=F32))
        d_xr = d_xr + jnp.concatenate(back, axis=1)
        d_xp = cw_ref[kw - 1:kw, :] * d_xr
        after = dxr_carry[...]
        for sft in range(1, kw):
            d_xp = d_xp + cw_ref[kw - 1 - sft:kw - sft, :] * _shift_up(d_xr, after, sft)
        dxr_carry[...] = d_xr[:SUBLANES]
        dxg_ref[0] = d_xp.astype(BF16)
        xp = xg_ref[0]
        before = jnp.where(time_first, 0.0, xph_ref[...])
        dcb_ref[...] += jnp.sum(d_xr, axis=0, keepdims=True)
        dcw_ref[kw - 1:kw, :] += jnp.sum(d_xr * xp, axis=0, keepdims=True)
        for sft in range(1, kw):
            dcw_ref[kw - 1 - sft:kw - sft, :] += jnp.sum(d_xr * _shift_down(xp, before, sft), axis=0, keepdims=True)
        dlam_ref[...] = dsp_acc[...] * (-_sigmoid(-lam_ref[...]))

    full = lambda shape: pl.BlockSpec(shape, lambda i: (0,) * len(shape))
    rev = lambda i: nt - 1 - i
    row_spec = pl.BlockSpec((ts, c), lambda i: (rev(i), 0))
    halo_idx = lambda i: jnp.maximum(rev(i) * hb - 1, 0)
    vec = (1, c)
    return _call_with_side(
        body, side, lambda: pl.program_id(0) == 0, lambda: pl.program_id(0) == nt - 1,
        name="rg_bwd", grid=(nt,),
        in_specs=[row_spec,
                  pl.BlockSpec((2, ts, c), lambda i: (0, rev(i), 0)),
                  pl.BlockSpec((None, SUBLANES, c), lambda i: (0, halo_idx(i), 0)),
                  row_spec, row_spec,
                  pl.BlockSpec((SUBLANES, c), lambda i: (halo_idx(i), 0)),
                  pl.BlockSpec((4, ts, c), lambda i: (0, rev(i), 0)),
                  full(conv_w.shape), full(w_a.shape), full(b_a.shape), full(w_x.shape), full(b_x.shape), full(lam.shape)],
        out_specs=[pl.BlockSpec((2, ts, c), lambda i: (0, rev(i), 0)), full(conv_w.shape), full(vec), full(w_a.shape), full(vec),
                   full(w_x.shape), full(vec), full(vec)],
        out_shape=[jax.ShapeDtypeStruct((2, s, c), BF16), jax.ShapeDtypeStruct(conv_w.shape, F32), jax.ShapeDtypeStruct(vec, F32),
                   jax.ShapeDtypeStruct(w_a.shape, F32), jax.ShapeDtypeStruct(vec, F32), jax.ShapeDtypeStruct(w_x.shape, F32),
                   jax.ShapeDtypeStruct(vec, F32), jax.ShapeDtypeStruct(vec, F32)],
        scratch_shapes=[pltpu.VMEM(vec, F32), pltpu.VMEM((SUBLANES, c), F32), pltpu.VMEM((SUBLANES, c), F32),
                        pltpu.VMEM(vec, F32), pltpu.VMEM((ts, c), F32), pltpu.VMEM((ts, c), F32)],
        args=(dy, xg2, xg2, xr, hs, hs, gates, conv_w, w_a, b_a, w_x, b_x, lam))


def _s5_param_fn(a_re, a_im, log_dt, bt_re, bt_im):
    dt = jnp.exp(log_dt)
    mag = jnp.exp(a_re * dt)
    abr = mag * jnp.cos(a_im * dt)
    abi = mag * jnp.sin(a_im * dt)
    ur, ui = abr - 1.0, abi
    den = a_re * a_re + a_im * a_im
    wr = (ur * a_re + ui * a_im) / den
    wi = (ui * a_re - ur * a_im) / den
    bbr = wr[None] * bt_re - wi[None] * bt_im
    bbi = wr[None] * bt_im + wi[None] * bt_re
    return abr, abi, bbr, bbi


def _s5_params_bwd(a_re, a_im, log_dt, bt_re, bt_im, d_abr, d_abi, d_bbr, d_bbi):
    def body(ar_ref, ai_ref, dt_ref, br_ref, bi_ref, g0, g1, g2, g3, o0, o1, o2, o3, o4):
        _, vjp = jax.vjp(_s5_param_fn, ar_ref[...], ai_ref[...], dt_ref[...], br_ref[...], bi_ref[...])
        outs = vjp((g0[...], g1[...], g2[...], g3[...]))
        for o, v in zip((o0, o1, o2, o3, o4), outs):
            o[...] = v

    sd = jax.ShapeDtypeStruct
    return pl.pallas_call(
        body, name="s5_params_bwd",
        out_shape=[sd(a_re.shape, F32), sd(a_im.shape, F32), sd(log_dt.shape, F32), sd(bt_re.shape, F32), sd(bt_im.shape, F32)],
    )(a_re, a_im, log_dt, bt_re, bt_im, d_abr, d_abi, d_bbr, d_bbi)


S5_LANE_CHUNK = 512


def _cmul_add(br, bi, tr, ti, sr, si):
    return br + tr * sr - ti * si, bi + tr * si + ti * sr


def _s5_tables3(a_re, a_im, log_dt, bt_re, bt_im, seg):
    g, p = a_re.shape
    gc = bt_re.shape[0]
    nsq = int(math.log2(seg))
    assert 1 << nsq == seg

    def body(ar_ref, ai_ref, dt_ref, br_ref, bi_ref, tr_ref, ti_ref, rtr_ref, rti_ref, bbr_ref, bbi_ref):
        abr, abi, bbr, bbi = _s5_param_fn(ar_ref[...], ai_ref[...], dt_ref[...], br_ref[...], bi_ref[...])
        bbr_ref[...] = bbr
        bbi_ref[...] = bbi
        qr, qi = abr, abi
        for _ in range(nsq):
            qr, qi = qr * qr - qi * qi, 2.0 * qr * qi
        pows = [(qr, qi)]
        for _ in range(1, SUBLANES):
            cr, ci = pows[-1]
            pows.append((cr * qr - ci * qi, cr * qi + ci * qr))
        zero = jnp.zeros_like(abr)
        for r in range(SUBLANES):
            rows = [(pows[(1 << k) - 1] if r >= (1 << k) else (zero, zero)) for k in range(3)] + [pows[r], (abr, abi)]
            for k, (vr, vi) in enumerate(rows):
                tr_ref[k, r] = vr
                ti_ref[k, r] = vi
                rtr_ref[k, SUBLANES - 1 - r] = vr
                rti_ref[k, SUBLANES - 1 - r] = -vi

    sd = jax.ShapeDtypeStruct
    tab = sd((5, SUBLANES, g, p), F32)
    return pl.pallas_call(
        body, name="s5_tables", out_shape=[tab, tab, tab, tab, sd((gc, g, p), F32), sd((gc, g, p), F32)],
    )(a_re, a_im, log_dt, bt_re, bt_im)


def _segment_perm(ts):
    seg = ts // SUBLANES
    rho = jnp.arange(ts)
    src = (rho % SUBLANES) * seg + rho // SUBLANES
    return (src[:, None] == jnp.arange(ts)[None, :]).astype(BF16)


def _exact_rows(perm_t, x):
    hi = x.astype(BF16)
    r1 = x - hi.astype(F32)
    mid = r1.astype(BF16)
    lo = (r1 - mid.astype(F32)).astype(BF16)
    dot = lambda v: jnp.dot(perm_t, v, preferred_element_type=F32)
    return (dot(hi) + dot(mid)) + dot(lo)


def _s5_fwd3(u, perm, perm_t, tab_r, tab_i, bp_r, bp_i, cp_r, cp_i, dvec, ts=256, side=None):
    s, c = u.shape
    n = tab_r.shape[2]
    nblk, cb, nb = bp_r.shape
    ts = min(ts, s)
    seg = ts // SUBLANES
    lc = min(S5_LANE_CHUNK, n)

    def body(u_ref, p_ref, pt_ref, tr_ref, ti_ref, bpr_ref, bpi_ref, cpr_ref, cpi_ref, d_ref, hr_ref, hi_ref, yp_ref, gy_ref,
             bur_ref, bui_ref, car_r, car_i):
        i = pl.program_id(0)

        @pl.when(i == 0)
        def _():
            car_r[...] = jnp.zeros_like(car_r)
            car_i[...] = jnp.zeros_like(car_i)

        uv = u_ref[...]
        ubp = jnp.dot(p_ref[...], uv.astype(BF16), preferred_element_type=F32).astype(BF16)
        for k in range(nblk):
            bur_ref[:, k * nb:(k + 1) * nb] = jnp.dot(ubp[:, k * cb:(k + 1) * cb], bpr_ref[k], preferred_element_type=F32)
            bui_ref[:, k * nb:(k + 1) * nb] = jnp.dot(ubp[:, k * cb:(k + 1) * cb], bpi_ref[k], preferred_element_type=F32)
        row8 = _rows((SUBLANES, lc))
        for q in range(n // lc):
            sl = slice(q * lc, (q + 1) * lc)
            tabs = [(tr_ref[k, :, sl], ti_ref[k, :, sl]) for k in range(5)]
            a_r, a_i = tabs[4]

            def local(r, carry, sl=sl, a_r=a_r, a_i=a_i):
                r0 = pl.multiple_of(r * SUBLANES, SUBLANES)
                hr, hi = _cmul_add(bur_ref[pl.ds(r0, SUBLANES), sl], bui_ref[pl.ds(r0, SUBLANES), sl], a_r, a_i, carry[0], carry[1])
                hr_ref[pl.ds(r0, SUBLANES), sl] = hr
                hi_ref[pl.ds(r0, SUBLANES), sl] = hi
                return hr, hi

            zero = jnp.zeros((SUBLANES, lc), F32)
            er, ei = lax.fori_loop(0, seg, local, (zero, zero), unroll=4)
            for k in range(3):
                sh = 1 << k
                er, ei = _cmul_add(er, ei, tabs[k][0], tabs[k][1], pltpu.roll(er, sh, 0), pltpu.roll(ei, sh, 0))
            cin_r, cin_i = jnp.broadcast_to(car_r[:, sl], er.shape), jnp.broadcast_to(car_i[:, sl], ei.shape)
            er, ei = _cmul_add(er, ei, tabs[3][0], tabs[3][1], cin_r, cin_i)
            car_r[:, sl] = er[SUBLANES - 1:, :]
            car_i[:, sl] = ei[SUBLANES - 1:, :]
            c_r = jnp.where(row8 == 0, cin_r, pltpu.roll(er, 1, 0))
            c_i = jnp.where(row8 == 0, cin_i, pltpu.roll(ei, 1, 0))

            def fix(r, carry, sl=sl, a_r=a_r, a_i=a_i, c_r=c_r, c_i=c_i):
                pr, pi = carry
                r0 = pl.multiple_of(r * SUBLANES, SUBLANES)
                hr, hi = _cmul_add(hr_ref[pl.ds(r0, SUBLANES), sl], hi_ref[pl.ds(r0, SUBLANES), sl], pr, pi, c_r, c_i)
                hr_ref[pl.ds(r0, SUBLANES), sl] = hr
                hi_ref[pl.ds(r0, SUBLANES), sl] = hi
                return pr * a_r - pi * a_i, pr * a_i + pi * a_r

            lax.fori_loop(0, seg, fix, (a_r, a_i), unroll=4)
        hrb, hib = hr_ref[...].astype(BF16), hi_ref[...].astype(BF16)
        y = jnp.concatenate([jnp.dot(hrb[:, k * nb:(k + 1) * nb], cpr_ref[k], preferred_element_type=F32)
                             - jnp.dot(hib[:, k * nb:(k + 1) * nb], cpi_ref[k], preferred_element_type=F32) for k in range(nblk)], axis=1)
        yp = _exact_rows(pt_ref[...], y) + d_ref[...] * uv
        yp_ref[...] = yp
        gy_ref[...] = _gelu(yp).astype(BF16)

    full = lambda shape: pl.BlockSpec(shape, lambda i: (0,) * len(shape))
    rc = pl.BlockSpec((ts, c), lambda i: (i, 0))
    rn = pl.BlockSpec((ts, n), lambda i: (i, 0))
    sd = jax.ShapeDtypeStruct
    nt = s // ts
    return _call_with_side(
        body, side, lambda: pl.program_id(0) == 0, lambda: pl.program_id(0) == nt - 1,
        name="s5_fwd", grid=(nt,),
        in_specs=[rc, full(perm.shape), full(perm_t.shape), full(tab_r.shape), full(tab_i.shape), full(bp_r.shape), full(bp_i.shape),
                  full(cp_r.shape), full(cp_i.shape), full(dvec.shape)],
        out_specs=[rn, rn, rc, rc],
        out_shape=[sd((s, n), F32), sd((s, n), F32), sd((s, c), F32), sd((s, c), BF16)],
        scratch_shapes=[pltpu.VMEM((ts, n), F32), pltpu.VMEM((ts, n), F32), pltpu.VMEM((1, n), F32), pltpu.VMEM((1, n), F32)],
        args=(u, perm, perm_t, tab_r, tab_i, bp_r, bp_i, cp_r, cp_i, dvec))


def _s5_bwd3(dgy, ypre, u, hr, hi, perm, perm_t, rtab_r, rtab_i, bp_r, bp_i, cp_r, cp_i, dvec, ts=256, side=None):
    s, c = u.shape
    n = rtab_r.shape[2]
    nblk, cb, nb = bp_r.shape
    ts = min(ts, s)
    nt = s // ts
    hb = ts // SUBLANES
    seg = ts // SUBLANES
    lc = min(S5_LANE_CHUNK, n)
    tn_dims = (((0,), (0,)), ((), ()))
    nt_dims = (((1,), (1,)), ((), ()))

    def body(dgy_ref, yp_ref, u_ref, hr_ref, hi_ref, hrh_ref, hih_ref, p_ref, pt_ref, tr_ref, ti_ref, bpr_ref, bpi_ref,
             cpr_ref, cpi_ref, d_ref, du_ref, dar_ref, dai_ref, dbr_ref, dbi_ref, dcr_ref, dci_ref, dd_ref, lr_ref, li_ref,
             car_r, car_i):
        i = pl.program_id(0)
        time_first = i == nt - 1

        @pl.when(i == 0)
        def _():
            car_r[...] = jnp.zeros_like(car_r)
            car_i[...] = jnp.zeros_like(car_i)
            for ref in (dar_ref, dai_ref, dbr_ref, dbi_ref, dcr_ref, dci_ref, dd_ref):
                ref[...] = jnp.zeros_like(ref)

        uv = u_ref[...]
        _, dgel = _gelu_and_grad(yp_ref[...])
        dyv = dgy_ref[...] * dgel
        dd_ref[...] += jnp.sum(dyv * uv, axis=0, keepdims=True)
        perm_m = p_ref[...]
        dyb = jnp.dot(perm_m, dyv.astype(BF16), preferred_element_type=F32).astype(BF16)
        ub = jnp.dot(perm_m, uv.astype(BF16), preferred_element_type=F32).astype(BF16)
        hrb, hib = hr_ref[...].astype(BF16), hi_ref[...].astype(BF16)
        for k in range(nblk):
            dblk = dyb[:, k * cb:(k + 1) * cb]
            lr_ref[:, k * nb:(k + 1) * nb] = lax.dot_general(dblk, cpr_ref[k], nt_dims, preferred_element_type=F32)
            li_ref[:, k * nb:(k + 1) * nb] = -lax.dot_general(dblk, cpi_ref[k], nt_dims, preferred_element_type=F32)
            dcr_ref[k] += lax.dot_general(hrb[:, k * nb:(k + 1) * nb], dblk, tn_dims, preferred_element_type=F32)
            dci_ref[k] += lax.dot_general(hib[:, k * nb:(k + 1) * nb], dblk, tn_dims, preferred_element_type=F32)
        row8 = _rows((SUBLANES, lc))
        last0 = (seg - 1) * SUBLANES
        for q in range(n // lc):
            sl = slice(q * lc, (q + 1) * lc)
            tabs = [(tr_ref[k, :, sl], ti_ref[k, :, sl]) for k in range(5)]
            a_r, a_i = tabs[4]

            def local(rr, carry, sl=sl, a_r=a_r, a_i=a_i):
                r0 = pl.multiple_of((seg - 1 - rr) * SUBLANES, SUBLANES)
                lr, li = _cmul_add(lr_ref[pl.ds(r0, SUBLANES), sl], li_ref[pl.ds(r0, SUBLANES), sl], a_r, a_i, carry[0], carry[1])
                lr_ref[pl.ds(r0, SUBLANES), sl] = lr
                li_ref[pl.ds(r0, SUBLANES), sl] = li
                return lr, li

            zero = jnp.zeros((SUBLANES, lc), F32)
            er, ei = lax.fori_loop(0, seg, local, (zero, zero), unroll=4)
            for k in range(3):
                sh = 1 << k
                er, ei = _cmul_add(er, ei, tabs[k][0], tabs[k][1], pltpu.roll(er, SUBLANES - sh, 0), pltpu.roll(ei, SUBLANES - sh, 0))
            cin_r, cin_i = jnp.broadcast_to(car_r[:, sl], er.shape), jnp.broadcast_to(car_i[:, sl], ei.shape)
            er, ei = _cmul_add(er, ei, tabs[3][0], tabs[3][1], cin_r, cin_i)
            car_r[:, sl] = er[:1, :]
            car_i[:, sl] = ei[:1, :]
            c_r = jnp.where(row8 == SUBLANES - 1, cin_r, pltpu.roll(er, SUBLANES - 1, 0))
            c_i = jnp.where(row8 == SUBLANES - 1, cin_i, pltpu.roll(ei, SUBLANES - 1, 0))
            halo_r = jnp.where(time_first, 0.0, hrh_ref[SUBLANES - 1:, sl])
            halo_i = jnp.where(time_first, 0.0, hih_ref[SUBLANES - 1:, sl])
            hp0_r = jnp.where(row8 == 0, jnp.broadcast_to(halo_r, zero.shape), pltpu.roll(hr_ref[pl.ds(last0, SUBLANES), sl], 1, 0))
            hp0_i = jnp.where(row8 == 0, jnp.broadcast_to(halo_i, zero.shape), pltpu.roll(hi_ref[pl.ds(last0, SUBLANES), sl], 1, 0))

            def fix(rr, carry, sl=sl, a_r=a_r, a_i=a_i, c_r=c_r, c_i=c_i, hp0_r=hp0_r, hp0_i=hp0_i):
                pr, pi, acc_r, acc_i = carry
                r = seg - 1 - rr
                r0 = pl.multiple_of(r * SUBLANES, SUBLANES)
                lr, li = _cmul_add(lr_ref[pl.ds(r0, SUBLANES), sl], li_ref[pl.ds(r0, SUBLANES), sl], pr, pi, c_r, c_i)
                lr_ref[pl.ds(r0, SUBLANES), sl] = lr
                li_ref[pl.ds(r0, SUBLANES), sl] = li
                p0 = pl.multiple_of(jnp.maximum(r - 1, 0) * SUBLANES, SUBLANES)
                hpr = jnp.where(r == 0, hp0_r, hr_ref[pl.ds(p0, SUBLANES), sl])
                hpi = jnp.where(r == 0, hp0_i, hi_ref[pl.ds(p0, SUBLANES), sl])
                return (pr * a_r - pi * a_i, pr * a_i + pi * a_r, acc_r + (lr * hpr + li * hpi), acc_i + (li * hpr - lr * hpi))

            _, _, acc_r, acc_i = lax.fori_loop(0, seg, fix, (a_r, a_i, zero, zero), unroll=4)
            dar_ref[:, sl] += jnp.sum(acc_r, axis=0, keepdims=True)
            dai_ref[:, sl] += jnp.sum(acc_i, axis=0, keepdims=True)
        lrb, lib = lr_ref[...].astype(BF16), li_ref[...].astype(BF16)
        du = []
        for k in range(nblk):
            ublk = ub[:, k * cb:(k + 1) * cb]
            lrk, lik = lrb[:, k * nb:(k + 1) * nb], lib[:, k * nb:(k + 1) * nb]
            dbr_ref[k] += lax.dot_general(ublk, lrk, tn_dims, preferred_element_type=F32)
            dbi_ref[k] += lax.dot_general(ublk, lik, tn_dims, preferred_element_type=F32)
            du.append(lax.dot_general(lrk, bpr_ref[k], nt_dims, preferred_element_type=F32)
                      + lax.dot_general(lik, bpi_ref[k], nt_dims, preferred_element_type=F32))
        du_ref[...] = (d_ref[...] * dyv + _exact_rows(pt_ref[...], jnp.concatenate(du, axis=1))).astype(BF16)

    full = lambda shape: pl.BlockSpec(shape, lambda i: (0,) * len(shape))
    rev = lambda i: nt - 1 - i
    halo_idx = lambda i: jnp.maximum(rev(i) * hb - 1, 0)
    rc = pl.BlockSpec((ts, c), lambda i: (rev(i), 0))
    rn = pl.BlockSpec((ts, n), lambda i: (rev(i), 0))
    hn = pl.BlockSpec((SUBLANES, n), lambda i: (halo_idx(i), 0))
    sd = jax.ShapeDtypeStruct
    vec_n = (1, n)
    return _call_with_side(
        body, side, lambda: pl.program_id(0) == 0, lambda: pl.program_id(0) == nt - 1,
        name="s5_bwd", grid=(nt,),
        in_specs=[rc, rc, rc, rn, rn, hn, hn, full(perm.shape), full(perm_t.shape), full(rtab_r.shape), full(rtab_i.shape),
                  full(bp_r.shape), full(bp_i.shape), full(cp_r.shape), full(cp_i.shape), full(dvec.shape)],
        out_specs=[rc, full(vec_n), full(vec_n), full(bp_r.shape), full(bp_i.shape), full(cp_r.shape), full(cp_i.shape),
                   full(dvec.shape)],
        out_shape=[sd((s, c), BF16), sd(vec_n, F32), sd(vec_n, F32), sd(bp_r.shape, F32), sd(bp_i.shape, F32),
                   sd(cp_r.shape, F32), sd(cp_i.shape, F32), sd(dvec.shape, F32)],
        scratch_shapes=[pltpu.VMEM((ts, n), F32), pltpu.VMEM((ts, n), F32), pltpu.VMEM((1, n), F32), pltpu.VMEM((1, n), F32)],
        args=(dgy, ypre, u, hr, hi, hr, hi, perm, perm_t, rtab_r, rtab_i, bp_r, bp_i, cp_r, cp_i, dvec))


def _glu_mm(gy, w_glu4, ts=1024, tn=512):
    s, d = gy.shape
    p, _, wc = w_glu4.shape
    c = p * wc // 2
    ts, tn = min(ts, s), min(tn, wc)
    per = wc // tn

    def body(x_ref, w1_ref, w2_ref, gl_ref, o_ref):
        x = x_ref[...]
        val = jnp.dot(x, w1_ref[...], preferred_element_type=F32)
        gate = jnp.dot(x, w2_ref[...], preferred_element_type=F32)
        gl_ref[0] = val
        gl_ref[1] = gate
        o_ref[...] = (val * _sigmoid(gate)).astype(BF16)

    return pl.pallas_call(
        body, name="s5_glu", grid=(c // tn, s // ts),
        in_specs=[pl.BlockSpec((ts, d), lambda j, i: (i, 0)),
                  pl.BlockSpec((None, d, tn), lambda j, i: (j // per, 0, j % per)),
                  pl.BlockSpec((None, d, tn), lambda j, i: (p // 2 + j // per, 0, j % per))],
        out_specs=[pl.BlockSpec((2, ts, tn), lambda j, i: (0, i, j)), pl.BlockSpec((ts, tn), lambda j, i: (i, j))],
        out_shape=[jax.ShapeDtypeStruct((2, s, c), F32), jax.ShapeDtypeStruct((s, c), BF16)], compiler_params=_cparams(),
    )(gy, w_glu4, w_glu4)


def _glu_bwd_mm(dh, w_out, gl2, ts=512):
    s, d = dh.shape
    c = w_out.shape[0]
    ts = min(ts, s)
    nt_dims = (((1,), (1,)), ((), ()))

    def body(dh_ref, w_ref, g_ref, o_ref):
        dov = lax.dot_general(dh_ref[...].astype(BF16), w_ref[...], nt_dims, preferred_element_type=F32)
        sg = _sigmoid(g_ref[1])
        o_ref[0] = (dov * sg).astype(BF16)
        o_ref[1] = (dov * g_ref[0] * sg * (1.0 - sg)).astype(BF16)

    blk = pl.BlockSpec((2, ts, c), lambda i: (0, i, 0))
    return pl.pallas_call(
        body, name="s5_out_dx", grid=(s // ts,),
        in_specs=[pl.BlockSpec((ts, d), lambda i: (i, 0)), pl.BlockSpec((c, d), lambda i: (0, 0)), blk],
        out_specs=blk, out_shape=jax.ShapeDtypeStruct((2, s, c), BF16), compiler_params=_cparams(),
    )(dh, w_out, gl2)


PACK_ROW_MULTIPLE = 1024
ELEMENTWISE_BLOCK_ELEMS = 256 * 1024


def _row_tile(rows, cols):
    pref = max(SUBLANES, 1 << int(math.log2(max(1, ELEMENTWISE_BLOCK_ELEMS // cols))))
    if rows <= pref:
        return rows
    t = pref
    while rows % t:
        t //= 2
    assert t >= SUBLANES, rows
    return t


def _sum_parts(rs, side=None):
    nl = len(rs)
    p, rows, cols = rs[0].shape
    tr = _row_tile(rows, cols)
    nt = rows // tr

    def body(*refs):
        o_ref = refs[nl]
        for l in range(nl):
            acc = refs[l][0].astype(F32)
            for k in range(1, p):
                acc = acc + refs[l][k].astype(F32)
            o_ref[l] = acc

    outs, got = _call_with_side(
        body, side, lambda: pl.program_id(0) == 0, lambda: pl.program_id(0) == nt - 1,
        name="sum_parts", grid=(nt,), in_specs=[pl.BlockSpec((p, tr, cols), lambda i: (0, i, 0))] * nl,
        out_specs=[pl.BlockSpec((nl, tr, cols), lambda i: (0, i, 0))], out_shape=[jax.ShapeDtypeStruct((nl, rows, cols), F32)],
        scratch_shapes=[], args=tuple(rs))
    return outs[0], got


def _adamw(w, g_parts, m, v, side=None):
    rows, cols = w.shape
    tr = _row_tile(rows, max(cols, LANES))
    ng = len(g_parts)
    emit_grad = ng > 1
    c1 = 1.0 / (1.0 - ADAM_B1 ** ADAM_STEP)
    c2 = 1.0 / (1.0 - ADAM_B2 ** ADAM_STEP)

    def body(*refs):
        w_ref, m_ref, v_ref = refs[0], refs[1 + ng], refs[2 + ng]
        dl_ref, nm_ref, nv_ref = refs[3 + ng:6 + ng]
        g = refs[1][...]
        for k in range(1, ng):
            g = g + refs[1 + k][...]
        mn = ADAM_B1 * m_ref[...] + (1.0 - ADAM_B1) * g
        vn = ADAM_B2 * v_ref[...] + (1.0 - ADAM_B2) * (g * g)
        if emit_grad:
            refs[6 + ng][...] = g
        nm_ref[...] = mn
        nv_ref[...] = vn
        dl_ref[...] = -ADAM_LR * ((mn * c1) / (jnp.sqrt(vn * c2) + ADAM_EPS) + ADAM_WD * w_ref[...])

    blk = pl.BlockSpec((tr, cols), lambda i: (i, 0))
    sd = jax.ShapeDtypeStruct((rows, cols), F32)
    nout = 4 if emit_grad else 3
    nt = rows // tr
    return _call_with_side(
        body, side, lambda: pl.program_id(0) == 0, lambda: pl.program_id(0) == nt - 1,
        name="adamw", grid=(nt,), in_specs=[blk] * (3 + ng), out_specs=[blk] * nout, out_shape=[sd] * nout,
        scratch_shapes=[], args=(w, *g_parts, m, v))


def _place():
    x, y, c = lax.axis_index("x"), lax.axis_index("y"), lax.axis_index("c")
    chips = [(1 - x, y), (x, 1 - y), (1 - x, 1 - y)]
    return x, y, c, chips


class Side:
    def __init__(self, ins, outs, kind):
        self.ins, self.outs, self.kind = list(ins), list(outs), kind
        n = len(self.ins)
        self.sems = [pltpu.SemaphoreType.DMA((3 * n,)), pltpu.SemaphoreType.DMA((3 * n,)), pltpu.SemaphoreType.DMA((n,))]

    def _copies(self, ins, outs, send, recv, lsem):
        x, y, c, chips = _place()
        me = 2 * x + y
        local, out_going, in_coming = [], [], []
        for t in range(len(ins)):
            if self.kind == 'sibling':
                cp = pltpu.make_async_remote_copy(src_ref=ins[t], dst_ref=outs[t], send_sem=send.at[t], recv_sem=recv.at[t],
                                                  device_id=(x, y, 1 - c), device_id_type=MESH)
                out_going.append(cp)
                in_coming.append(cp)
                continue
            if self.kind == 'gather':
                src_local, srcs, dst_mine = ins[t], [ins[t]] * 3, outs[t].at[me]
            else:
                src_local, srcs, dst_mine = ins[t].at[me], [ins[t].at[2 * px + py] for px, py in chips], outs[t].at[me]
            local.append(pltpu.make_async_copy(src_local, dst_mine, lsem.at[t]))
            for r, (px, py) in enumerate(chips):
                out_going.append(pltpu.make_async_remote_copy(
                    src_ref=srcs[r], dst_ref=dst_mine, send_sem=send.at[3 * t + r], recv_sem=recv.at[3 * t + r],
                    device_id=(px, py, c), device_id_type=MESH))
                in_coming.append(pltpu.make_async_remote_copy(
                    src_ref=srcs[r], dst_ref=outs[t].at[2 * px + py], send_sem=send.at[3 * t + r], recv_sem=recv.at[3 * t + r],
                    device_id=(px, py, c), device_id_type=MESH))
        return local, out_going, in_coming

    def start(self, ins, outs, send, recv, lsem):
        local, out_going, _ = self._copies(ins, outs, send, recv, lsem)
        for cp in local + out_going:
            cp.start()

    def wait(self, ins, outs, send, recv, lsem):
        local, out_going, in_coming = self._copies(ins, outs, send, recv, lsem)
        for cp in in_coming:
            cp.wait_recv()
        for cp in out_going:
            cp.wait_send()
        for cp in local:
            cp.wait()


def _gather_side(shards):
    return Side(shards, [jax.ShapeDtypeStruct((N_CHIPS,) + s.shape, s.dtype) for s in shards], 'gather')


def _scatter_side(grads):
    return Side(grads, [jax.ShapeDtypeStruct(g.shape, g.dtype) for g in grads], 'scatter')


def _sibling_side(arrs):
    return Side(arrs, [jax.ShapeDtypeStruct(a.shape, a.dtype) for a in arrs], 'sibling')


def _call_with_side(body, side, first, last, *, name, grid, in_specs, out_specs, out_shape, scratch_shapes, args):
    if side is None:
        outs = pl.pallas_call(body, name=name, grid=grid, in_specs=in_specs, out_specs=out_specs, out_shape=out_shape,
                              scratch_shapes=scratch_shapes, compiler_params=_cparams())(*args)
        return outs, []
    n_in, n_out, n_sc = len(in_specs), len(out_specs), len(scratch_shapes)
    ns_in, ns_out = len(side.ins), len(side.outs)

    def wrapped(*refs):
        base_in, s_in = refs[:n_in], refs[n_in:n_in + ns_in]
        o0 = n_in + ns_in
        base_out, s_out = refs[o0:o0 + n_out], refs[o0 + n_out:o0 + n_out + ns_out]
        sc0 = o0 + n_out + ns_out
        base_sc, sems = refs[sc0:sc0 + n_sc], refs[sc0 + n_sc:]

        @pl.when(first())
        def _():
            side.start(s_in, s_out, *sems)

        body(*base_in, *base_out, *base_sc)

        @pl.when(last())
        def _():
            side.wait(s_in, s_out, *sems)

    any_spec = pl.BlockSpec(memory_space=pl.ANY)
    outs = pl.pallas_call(
        wrapped, name=name, grid=grid, in_specs=list(in_specs) + [any_spec] * ns_in, out_specs=list(out_specs) + [any_spec] * ns_out,
        out_shape=list(out_shape) + side.outs, scratch_shapes=list(scratch_shapes) + side.sems, compiler_params=_cparams(),
    )(*args, *side.ins)
    return outs[:n_out], outs[n_out:]


def _run_side(name, side):
    def body(*refs):
        n = len(side.ins)
        side.start(refs[:n], refs[n:2 * n], *refs[2 * n:])
        side.wait(refs[:n], refs[n:2 * n], *refs[2 * n:])

    any_spec = pl.BlockSpec(memory_space=pl.ANY)
    return pl.pallas_call(body, name=name, in_specs=[any_spec] * len(side.ins), out_specs=[any_spec] * len(side.outs),
                          out_shape=side.outs, scratch_shapes=side.sems)(*side.ins)


def _allreduce_small(v):
    rows, cols = v.shape
    r8 = rows // (2 * N_CHIPS)
    assert r8 * 2 * N_CHIPS == rows and r8 % SUBLANES == 0, rows

    def body(v_ref, o_ref, sib_ref, cs_ref, slot_ref, send, recv):
        x, y, c, chips = _place()
        me = 2 * x + y
        sibling = (x, y, 1 - c)

        def eighth(ref, chip, core):
            return ref.at[pl.ds(pl.multiple_of((2 * chip + core) * r8, SUBLANES), r8)]

        def copy(src, dst, k, to):
            return pltpu.make_async_remote_copy(src_ref=src, dst_ref=dst, send_sem=send.at[k], recv_sem=recv.at[k],
                                                device_id=to, device_id_type=MESH)

        d2d = copy(v_ref, sib_ref, 0, sibling)
        d2d.start()
        d2d.wait_recv()
        cs_ref[...] = v_ref[...] + sib_ref[...]
        reduce_out = [copy(eighth(cs_ref, 2 * px + py, c), slot_ref.at[me], 1 + r, (px, py, c)) for r, (px, py) in enumerate(chips)]
        for cp in reduce_out:
            cp.start()
        slot_ref[me] = cs_ref[pl.ds(pl.multiple_of((2 * me + c) * r8, SUBLANES), r8), :]
        for r, (px, py) in enumerate(chips):
            copy(eighth(cs_ref, me, c), slot_ref.at[2 * px + py], 1 + r, (px, py, c)).wait_recv()
        o_ref[pl.ds(pl.multiple_of((2 * me + c) * r8, SUBLANES), r8), :] = (slot_ref[0] + slot_ref[1]) + (slot_ref[2] + slot_ref[3])
        mine = eighth(o_ref, me, c)
        hand_out = [copy(mine, mine, 4, sibling)] + [copy(mine, mine, 5 + r, (px, py, c)) for r, (px, py) in enumerate(chips)]
        for cp in hand_out:
            cp.start()
        passed_on = []
        for r, (px, py) in enumerate(chips):
            theirs = eighth(o_ref, 2 * px + py, c)
            copy(theirs, theirs, 5 + r, (px, py, c)).wait_recv()
            fw = copy(theirs, theirs, 8 + r, sibling)
            fw.start()
            passed_on.append(fw)
        sib_own = eighth(o_ref, me, 1 - c)
        copy(sib_own, sib_own, 4, sibling).wait_recv()
        for r, (px, py) in enumerate(chips):
            got = eighth(o_ref, 2 * px + py, 1 - c)
            copy(got, got, 8 + r, sibling).wait_recv()
        for cp in [d2d] + reduce_out + hand_out + passed_on:
            cp.wait_send()

    vm = pl.BlockSpec(memory_space=pltpu.VMEM)
    return pl.pallas_call(
        body, name="allreduce_small", in_specs=[vm], out_specs=vm, out_shape=jax.ShapeDtypeStruct((rows, cols), F32),
        scratch_shapes=[pltpu.VMEM((rows, cols), F32), pltpu.VMEM((rows, cols), F32), pltpu.VMEM((N_CHIPS, r8, cols), F32),
                        pltpu.SemaphoreType.DMA((11,)), pltpu.SemaphoreType.DMA((11,))],
        compiler_params=_cparams(),
    )(v)


def _pack(tensors):
    pieces = []
    for t in tensors:
        flat = t.reshape(-1)
        pad = (-flat.shape[0]) % (SUBLANES * LANES)
        pieces.append(jnp.pad(flat, (0, pad)).reshape(-1, LANES))
    rows = sum(p.shape[0] for p in pieces)
    pieces.append(jnp.zeros(((-rows) % PACK_ROW_MULTIPLE, LANES), tensors[0].dtype))
    return jnp.concatenate(pieces, axis=0)


def _unpack(buf, like):
    out, off = [], 0
    for t in like:
        size = math.prod(t.shape)
        rows = -(-size // (SUBLANES * LANES)) * SUBLANES
        out.append(buf[off:off + rows].reshape(-1)[:size].reshape(t.shape))
        off += rows
    return out


def _s5_pack_b(bb):
    gc, g, p = bb.shape
    q = S5_GROUPS_PER_BLOCK
    t = bb.reshape(gc, g // q, q, p).transpose(1, 2, 0, 3)
    eye = jnp.eye(q, dtype=bb.dtype)
    return (t[:, :, :, None, :] * eye[None, :, None, :, None]).reshape(g // q, q * gc, q * p)


def _s5_unpack_b(dbp, gc, p):
    nb = dbp.shape[0]
    q = S5_GROUPS_PER_BLOCK
    eye = jnp.eye(q, dtype=dbp.dtype)
    t = (dbp.reshape(nb, q, gc, q, p) * eye[None, :, None, :, None]).sum(axis=3)
    return t.transpose(2, 0, 1, 3).reshape(gc, nb * q, p)


def _s5_pack_c(cc):
    g, gc, p = cc.shape
    q = S5_GROUPS_PER_BLOCK
    t = cc.reshape(g // q, q, gc, p).transpose(0, 1, 3, 2)
    eye = jnp.eye(q, dtype=cc.dtype)
    return (t[:, :, :, None, :] * eye[None, :, None, :, None]).reshape(g // q, q * p, q * gc)


def _s5_unpack_c(dcp, gc, p):
    nb = dcp.shape[0]
    q = S5_GROUPS_PER_BLOCK
    eye = jnp.eye(q, dtype=dcp.dtype)
    t = (dcp.reshape(nb, q, p, q, gc) * eye[None, :, None, :, None]).sum(axis=3)
    return t.transpose(0, 1, 3, 2).reshape(nb * q, gc, p)


def _split2(m):
    return m.arr[:, 0]


def kernel(x, norm_mix_g, norm_ffn_g, norm_final_g, rg_w_in, rg_conv_w, rg_conv_b, rg_w_a, rg_b_a, rg_w_x, rg_b_x, rg_lambda, rg_w_out, s5_w_in, s5_a_re, s5_a_im, s5_log_dt, s5_b_re, s5_b_im, s5_c_re, s5_c_im, s5_d, s5_w_glu, s5_w_out, ffn_w_up, ffn_conv_w, ffn_conv_b, ffn_w_down, loss_target, m_norm_mix_g, m_norm_ffn_g, m_norm_final_g, m_rg_w_in, m_rg_conv_w, m_rg_conv_b, m_rg_w_a, m_rg_b_a, m_rg_w_x, m_rg_b_x, m_rg_lambda, m_rg_w_out, m_s5_w_in, m_s5_a_re, m_s5_a_im, m_s5_log_dt, m_s5_b_re, m_s5_b_im, m_s5_c_re, m_s5_c_im, m_s5_d, m_s5_w_glu, m_s5_w_out, m_ffn_w_up, m_ffn_conv_w, m_ffn_conv_b, m_ffn_w_down, v_norm_mix_g, v_norm_ffn_g, v_norm_final_g, v_rg_w_in, v_rg_conv_w, v_rg_conv_b, v_rg_w_a, v_rg_b_a, v_rg_w_x, v_rg_b_x, v_rg_lambda, v_rg_w_out, v_s5_w_in, v_s5_a_re, v_s5_a_im, v_s5_log_dt, v_s5_b_re, v_s5_b_im, v_s5_c_re, v_s5_c_im, v_s5_d, v_s5_w_glu, v_s5_w_out, v_ffn_w_up, v_ffn_conv_w, v_ffn_conv_b, v_ffn_w_down):
    w = dict(zip(PARAM_NAMES, (norm_mix_g, norm_ffn_g, norm_final_g, rg_w_in, rg_conv_w, rg_conv_b, rg_w_a, rg_b_a, rg_w_x, rg_b_x,
                               rg_lambda, rg_w_out, s5_w_in, s5_a_re, s5_a_im, s5_log_dt, s5_b_re, s5_b_im, s5_c_re, s5_c_im, s5_d,
                               s5_w_glu, s5_w_out, ffn_w_up, ffn_conv_w, ffn_conv_b, ffn_w_down)))
    mom = dict(zip(PARAM_NAMES, (m_norm_mix_g, m_norm_ffn_g, m_norm_final_g, m_rg_w_in, m_rg_conv_w, m_rg_conv_b, m_rg_w_a, m_rg_b_a,
                                 m_rg_w_x, m_rg_b_x, m_rg_lambda, m_rg_w_out, m_s5_w_in, m_s5_a_re, m_s5_a_im, m_s5_log_dt, m_s5_b_re,
                                 m_s5_b_im, m_s5_c_re, m_s5_c_im, m_s5_d, m_s5_w_glu, m_s5_w_out, m_ffn_w_up, m_ffn_conv_w,
                                 m_ffn_conv_b, m_ffn_w_down)))
    vel = dict(zip(PARAM_NAMES, (v_norm_mix_g, v_norm_ffn_g, v_norm_final_g, v_rg_w_in, v_rg_conv_w, v_rg_conv_b, v_rg_w_a, v_rg_b_a,
                                 v_rg_w_x, v_rg_b_x, v_rg_lambda, v_rg_w_out, v_s5_w_in, v_s5_a_re, v_s5_a_im, v_s5_log_dt, v_s5_b_re,
                                 v_s5_b_im, v_s5_c_re, v_s5_c_im, v_s5_d, v_s5_w_glu, v_s5_w_out, v_ffn_w_up, v_ffn_conv_w,
                                 v_ffn_conv_b, v_ffn_w_down)))
    _, s, d = x.shape
    depth = norm_mix_g.shape[0]
    n_grp, n_state = s5_a_re.shape[1], s5_a_re.shape[2]
    gc = s5_b_re.shape[3]
    d_ff = ffn_w_down.shape[1] * N_CHIPS
    s5_ts = min(256, s)
    s5_perm = _segment_perm(s5_ts)

    wb = {n: (w[n].astype(BF16) if n in BIG else w[n]) for n in SHARDED}
    gath = {}

    def mixer_keys(i):
        return [(n, i // 2) for n in MIXER_SHARDED[i % 2]] if i < depth else []

    def gather_side(keys):
        return _gather_side([wb[n][l] for n, l in keys])

    def put(keys, arrs):
        for k, a in zip(keys, arrs):
            gath[k] = a

    def wcol(n, l):
        return Mat(gath[(n, l)][:, None], 0, 'c')

    def wrow(n, l):
        g = gath[(n, l)]
        return Mat(g.reshape(1, 1, N_CHIPS * g.shape[1], g.shape[2]), 0, 'c')

    def rg_cw(l):
        return gath[('rg_conv_w', l)].transpose(1, 0, 2).reshape(RG_CONV_W, d)

    def s5_dv(l):
        return gath[('s5_d', l)].reshape(1, d)

    def f_cw(l):
        return gath[('ffn_conv_w', l)].transpose(1, 0, 2).reshape(FFN_CONV_W, 2, d_ff).transpose(1, 0, 2)

    tm = min(1024, s)
    tkw = min(2048, s)
    d_up = 2 * d_ff // N_CHIPS
    f_cb = ffn_conv_b.reshape(depth, 2, 1, d_ff)

    h = x.reshape(s, d)
    saved = []
    for i in range(depth):
        j = i // 2
        sv = {'h_in': h}
        first_late = mixer_keys(0)[-1:] if i == 0 else []
        if i == 0:
            hn, got = _rms_fwd(h, norm_mix_g[:1], side=gather_side(mixer_keys(0)[:-1]))
            put(mixer_keys(0)[:-1], got)
        sv['hn'] = hn
        up_keys = [('ffn_w_up', i), ('ffn_conv_w', i)]
        if i % 2 == 0:
            xg = _mm("rg_in", 'nn', act(hn), wcol('rg_w_in', j), out_parts=2, tm=tm, tn=512, tk=d,
                     side=gather_side(first_late) if first_late else None)
            if first_late:
                xg, got = xg
                put(first_late, got)
            xg2 = _split2(xg)
            wa, wx = rg_w_a[j].astype(BF16), rg_w_x[j].astype(BF16)
            ba, bx = rg_b_a[j].reshape(1, d), rg_b_x[j].reshape(1, d)
            (xr, hs, y, gates), got = _rg_fwd(xg2, rg_cw(j), rg_conv_b[j:j + 1], wa, ba, wx, bx, rg_lambda[j:j + 1],
                                              side=gather_side(up_keys))
            put(up_keys, got)
            sv.update(xg2=xg2, xr=xr, hs=hs, y=y, gates=gates, wa=wa, wx=wx, ba=ba, bx=bx)
            h, hn2 = _mm("rg_out", 'nn', act(y), wrow('rg_w_out', j), res=act(h), norm_g=norm_ffn_g[i:i + 1], tm=tm, tn=d, tk=d)
        else:
            u = _mm("s5_in", 'nn', act(hn), wrow('s5_w_in', j), tm=tm, tn=d, tk=d).arr[0, 0]
            bt_re, bt_im = s5_b_re[j].transpose(2, 0, 1), s5_b_im[j].transpose(2, 0, 1)
            ldt = s5_log_dt[j].reshape(n_grp, 1)
            tab_r, tab_i, rtab_r, rtab_i, bbr, bbi = _s5_tables3(s5_a_re[j], s5_a_im[j], ldt, bt_re, bt_im, seg=s5_ts // SUBLANES)
            nn_ = n_grp * n_state
            tab_r, tab_i, rtab_r, rtab_i = (t.reshape(5, SUBLANES, nn_) for t in (tab_r, tab_i, rtab_r, rtab_i))
            prm = dict(bp_r=_s5_pack_b(bbr).astype(BF16), bp_i=_s5_pack_b(bbi).astype(BF16),
                       cp_r=_s5_pack_c(s5_c_re[j]).astype(BF16), cp_i=_s5_pack_c(s5_c_im[j]).astype(BF16), dvec=s5_dv(j))
            (hr, hi, ypre, gy), got = _s5_fwd3(u, s5_perm, s5_perm.T, tab_r, tab_i, ts=s5_ts, side=gather_side(up_keys), **prm)
            sv.update(rtab_r=rtab_r, rtab_i=rtab_i)
            put(up_keys, got)
            gl2, o = _glu_mm(gy, gath[('s5_w_glu', j)])
            sv.update(u=u, prm=prm, hr=hr, hi=hi, ypre=ypre, gy=gy, gl2=gl2, o=o, bt_re=bt_re, bt_im=bt_im, ldt=ldt)
            h, hn2 = _mm("s5_out", 'nn', act(o), wrow('s5_w_out', j), res=act(h), norm_g=norm_ffn_g[i:i + 1], tm=tm, tn=d, tk=d)
        h = h.arr[0, 0]
        sv['h_mid'] = h
        next_keys = [('ffn_w_down', i)] + mixer_keys(i + 1)
        (up2, c2, a_ffn), got = _ffn_up_act(hn2, gath[('ffn_w_up', i)], f_cw(i), f_cb[i], side=gather_side(next_keys))
        put(next_keys, got)
        sv.update(hn2=hn2, up2=up2, c2=c2, act=a_ffn)
        if i + 1 < depth:
            h, hn = _mm("ffn_down", 'nn', act(a_ffn), wrow('ffn_w_down', i), res=act(h), norm_g=norm_mix_g[i + 1:i + 2], tm=tm, tn=d,
                        tk=d_ff // 2)
        else:
            h = _mm("ffn_down", 'nn', act(a_ffn), wrow('ffn_w_down', i), res=act(h), tm=tm, tn=d, tk=d_ff // 2)
        h = h.arr[0, 0]
        saved.append(sv)

    loss_row, dh, dg_final = _loss_and_grad(h, norm_final_g.reshape(1, d), loss_target.reshape(s, d))
    loss = lax.psum(loss_row[0, 0], ("x", "y", "c"))

    gl_ = {n: [None] * w[n].shape[0] for n in PARAM_NAMES if n != 'norm_final_g'}
    recvd = {}

    def scatter_side(keys):
        return _scatter_side([gl_[n][l].reshape((N_CHIPS,) + w[n].shape[1:]) for n, l in keys])

    def record(keys, arrs):
        for k, a in zip(keys, arrs):
            recvd[k] = a

    pending = None
    for i in reversed(range(depth)):
        j = i // 2
        sv = saved[i]
        gl_['ffn_w_down'][i] = _mm("ffn_down_dw", 'tn', act(sv['act']), act(dh), out_dtype=BF16, tm=d_ff // N_CHIPS, tn=d, tk=tkw).arr
        (dup2, dcw2, dcb2), got = _ffn_bwd_fused(dh, gath[('ffn_w_down', i)].reshape(d_ff, d), sv['up2'], sv['c2'], f_cw(i),
                                                 side=scatter_side(pending) if pending else None)
        if pending:
            record(pending, got)
        gl_['ffn_conv_w'][i] = dcw2.transpose(1, 0, 2).reshape(FFN_CONV_W, 2 * d_ff)
        gl_['ffn_conv_b'][i] = dcb2.reshape(2 * d_ff)
        dup = Mat(dup2[:, None], 0, 'c')
        gl_['ffn_w_up'][i] = _mm("ffn_up_dw", 'tn', act(sv['hn2']), dup, out_parts=N_CHIPS, out_dtype=BF16, tm=d, tn=d_up, tk=tkw).arr
        (dh, dg), _ = _mm_rms_bwd("ffn_up_dx", dup, wcol('ffn_w_up', i), sv['h_mid'], norm_ffn_g[i:i + 1], dh, tm=tm, tk=d_up)
        gl_['norm_ffn_g'][i] = dg[0]
        ffn_keys = [('ffn_w_up', i), ('ffn_w_down', i)]
        if i % 2 == 0:
            dy = _mm("rg_out_dx", 'nt', act(dh), wrow('rg_w_out', j), tm=tm, tn=d, tk=d).arr[0, 0]
            gl_['rg_w_out'][j] = _mm("rg_out_dw", 'tn', act(sv['y']), act(dh), out_dtype=BF16, tm=d, tn=d, tk=tkw).arr
            (dxg2, dcw, dcb, dwa, dba, dwx, dbx, dlam), got = _rg_bwd(
                dy, sv['xg2'], sv['xr'], sv['hs'], sv['gates'], rg_cw(j), sv['wa'], sv['ba'], sv['wx'], sv['bx'], rg_lambda[j:j + 1],
                side=scatter_side(ffn_keys))
            record(ffn_keys, got)
            gl_['rg_conv_w'][j] = dcw
            gl_['rg_conv_b'][j] = dcb[0]
            gl_['rg_w_a'][j], gl_['rg_w_x'][j] = dwa, dwx
            gl_['rg_b_a'][j], gl_['rg_b_x'][j] = dba.reshape(rg_b_a.shape[1:]), dbx.reshape(rg_b_x.shape[1:])
            gl_['rg_lambda'][j] = dlam[0]
            dxg = Mat(dxg2[:, None], 0, 'c')
            gl_['rg_w_in'][j] = _mm("rg_in_dw", 'tn', act(sv['hn']), dxg, out_parts=N_CHIPS, out_dtype=BF16, tm=d, tn=512, tk=tkw).arr
            mix_dx = ("rg_in_dx", dxg, wcol('rg_w_in', j), 512)
            pending = [('rg_w_in', j), ('rg_w_out', j)]
        else:
            gl_['s5_w_out'][j] = _mm("s5_out_dw", 'tn', act(sv['o']), act(dh), out_dtype=BF16, tm=d, tn=d, tk=tkw).arr
            dgl2 = _glu_bwd_mm(dh, gath[('s5_w_out', j)].reshape(d, d), sv['gl2'])
            dgl = Mat(dgl2[:, None], 0, 'c')
            gl_['s5_w_glu'][j] = _mm("s5_glu_dw", 'tn', act(sv['gy']), dgl, out_parts=N_CHIPS, out_dtype=BF16, tm=d, tn=512, tk=tkw).arr
            dgy = _mm("s5_glu_dx", 'nt', dgl, wcol('s5_w_glu', j), tm=tm, tn=d, tk=512).arr[0, 0]
            (du, dar, dai, dbpr, dbpi, dcpr, dcpi, dd), got = _s5_bwd3(
                dgy, sv['ypre'], sv['u'], sv['hr'], sv['hi'], s5_perm, s5_perm.T, sv['rtab_r'], sv['rtab_i'], ts=s5_ts,
                side=scatter_side(ffn_keys), **sv['prm'])
            record(ffn_keys, got)
            gl_['s5_d'][j] = dd[0]
            gl_['s5_c_re'][j] = _s5_unpack_c(dcpr, gc, n_state)
            gl_['s5_c_im'][j] = -_s5_unpack_c(dcpi, gc, n_state)
            d_are, d_aim, d_ldt, d_btr, d_bti = _s5_params_bwd(
                s5_a_re[j], s5_a_im[j], sv['ldt'], sv['bt_re'], sv['bt_im'], dar.reshape(n_grp, n_state), dai.reshape(n_grp, n_state),
                _s5_unpack_b(dbpr, gc, n_state), _s5_unpack_b(dbpi, gc, n_state))
            gl_['s5_a_re'][j], gl_['s5_a_im'][j], gl_['s5_log_dt'][j] = d_are, d_aim, d_ldt[:, 0]
            gl_['s5_b_re'][j], gl_['s5_b_im'][j] = d_btr.transpose(1, 2, 0), d_bti.transpose(1, 2, 0)
            dum = act(du)
            gl_['s5_w_in'][j] = _mm("s5_in_dw", 'tn', act(sv['hn']), dum, out_dtype=BF16, tm=d, tn=d, tk=tkw).arr
            mix_dx = ("s5_in_dx", dum, wrow('s5_w_in', j), d)
            pending = [('s5_w_in', j), ('s5_w_glu', j), ('s5_w_out', j)]
        (dh, dg), got = _mm_rms_bwd(mix_dx[0], mix_dx[1], mix_dx[2], sv['h_in'], norm_mix_g[i:i + 1], dh, tm=tm, tk=mix_dx[3],
                                    side=scatter_side(pending) if i == 0 else None)
        if i == 0:
            record(pending, got)
        gl_['norm_mix_g'][i] = dg[0]
    grad_x = dh.reshape(x.shape)

    order = sorted(BIG, key=lambda n: -math.prod(w[n].shape))
    chip_sums, theirs, prev = {}, {}, None
    for n in order:
        cols = w[n].shape[-1]
        cs, got = _sum_parts([recvd[(n, l)].reshape(N_CHIPS, -1, cols) for l in range(w[n].shape[0])],
                             side=_sibling_side([chip_sums[prev]]) if prev else None)
        chip_sums[n] = cs.reshape(-1, cols)
        if prev:
            theirs[prev] = got[0]
        prev = n
    theirs[prev] = _run_side("swap_last", _sibling_side([chip_sums[prev]]))[0]
    results = {}
    for n in BIG:
        cols = w[n].shape[-1]
        (delta, new_m, new_v, grad), _ = _adamw(w[n].reshape(-1, cols), [chip_sums[n], theirs[n]], mom[n].reshape(-1, cols),
                                                vel[n].reshape(-1, cols))
        results[n] = [o.reshape(w[n].shape) for o in (grad, delta, new_m, new_v)]

    small = REPLICATED + SMALL_SHARDED
    local = [dg_final.reshape(d) if n == 'norm_final_g' else jnp.stack(gl_[n]) for n in small]
    summed = _unpack(_allreduce_small(_pack(local)), local)
    me = 2 * lax.axis_index("x") + lax.axis_index("y")
    for n, g in zip(small, summed):
        if n in SMALL_SHARDED:
            g = lax.dynamic_slice_in_dim(g, me * w[n].shape[-1], w[n].shape[-1], axis=g.ndim - 1)
        view = (-1, w[n].shape[-1])
        (delta, new_m, new_v), _ = _adamw(w[n].reshape(view), [g.reshape(view)], mom[n].reshape(view), vel[n].reshape(view))
        results[n] = [g] + [o.reshape(w[n].shape) for o in (delta, new_m, new_v)]

    return (loss, grad_x, *[results[n][0] for n in PARAM_NAMES], *[results[n][1] for n in PARAM_NAMES],
            *[results[n][2] for n in PARAM_NAMES], *[results[n][3] for n in PARAM_NAMES])
```
